```python
import jax, jax.numpy as jnp
from jax import lax
import numpy as np

D_MODEL = 1024
BATCH = 16
SEQ = 2048
DEPTH = 2

N_MIXERS = 2
MEM_LEN = 256
EPS = 1e-6
HG_HEADS = 8
HG_DIM = D_MODEL // HG_HEADS
HG_WIDTH = HG_HEADS * HG_DIM
HG_CHUNK = 64
GM_CHUNK = 128
GM_GROUPS = 8
GM_WIDTH = 2 * D_MODEL
GM_GROUP_DIM = GM_WIDTH // GM_GROUPS
XA_HEADS = 4
XA_DIM = D_MODEL // XA_HEADS
XA_WIDTH = XA_HEADS * XA_DIM
D_FF = 2816
N_HGRN = (DEPTH + 1) // 2
N_GMLP = DEPTH // 2
HG_IN = 4 * HG_WIDTH + XA_WIDTH
GM_IN = 2 * GM_WIDTH + XA_WIDTH

kernel_name = "hybrid_hgrn2_gmlp_memory_macaron"


def rmsnorm(x, g):
    xf = x.astype(jnp.float32)
    y = xf * lax.rsqrt(jnp.mean(xf * xf, axis=-1, keepdims=True) + EPS)
    return (y * g.astype(jnp.float32)).astype(x.dtype)


def layernorm(x, g, b):
    xf = x.astype(jnp.float32)
    mu = jnp.mean(xf, axis=-1, keepdims=True)
    xc = xf - mu
    y = xc * lax.rsqrt(jnp.mean(xc * xc, axis=-1, keepdims=True) + EPS)
    return (y * g.astype(jnp.float32) + b.astype(jnp.float32)).astype(x.dtype)


def swiglu_ffn(h, w_in, w_out):
    gate, up = jnp.split(h @ w_in, 2, axis=-1)
    return (jax.nn.silu(gate) * up) @ w_out


def memory_attention(zq, mem_k, mem_v):
    B, T, _ = zq.shape
    q = zq.reshape(B, T, XA_HEADS, XA_DIM)
    s = jnp.einsum('bthd,bmhd->bhtm', q, mem_k).astype(jnp.float32) * (XA_DIM ** -0.5)
    p = jax.nn.softmax(s, axis=-1).astype(mem_v.dtype)
    o = jnp.einsum('bhtm,bmhd->bthd', p, mem_v)
    return o.reshape(B, T, XA_WIDTH)


def hgrn2_recurrence(q, k, v, log_f):
    B, T, H, Dk = q.shape
    Dv = v.shape[-1]
    n = T // HG_CHUNK

    def to_chunks(a):
        return a.reshape(B, n, HG_CHUNK, H, a.shape[-1]).transpose(1, 0, 3, 2, 4)

    qc, kc, vc, lc = to_chunks(q), to_chunks(k), to_chunks(v), to_chunks(log_f)
    causal = jnp.tril(jnp.ones((HG_CHUNK, HG_CHUNK), dtype=bool))

    def step(S, inp):
        qn, kn, vn, ln = inp
        b = jnp.cumsum(ln, axis=2)
        b_last = b[:, :, -1:, :]
        q_dec = qn * jnp.exp(b)
        k_inv = kn * jnp.exp(-b)
        a = jnp.einsum('bhtk,bhsk->bhts', q_dec, k_inv)
        a = jnp.where(causal, a, 0.0)
        o = jnp.einsum('bhts,bhsv->bhtv', a, vn) + jnp.einsum('bhtk,bhkv->bhtv', q_dec, S)
        S_new = jnp.exp(b_last[:, :, 0, :])[..., None] * S + jnp.einsum(
            'bhsk,bhsv->bhkv', kn * jnp.exp(b_last - b), vn)
        return S_new, o

    S0 = jnp.zeros((B, H, Dk, Dv), jnp.float32)
    _, o = lax.scan(step, S0, (qc, kc, vc, lc))
    return o.transpose(1, 0, 3, 2, 4).reshape(B, T, H, Dv)


def hgrn2_mixer(zq, zf, zi, zg, lb, gnorm_g):
    B, T, _ = zq.shape
    shp = (B, T, HG_HEADS, HG_DIM)
    q = jax.nn.silu(zq.astype(jnp.float32)).reshape(shp)
    lbh = lb.astype(jnp.float32).reshape(HG_HEADS, HG_DIM)
    f = lbh + (1.0 - lbh) * jax.nn.sigmoid(zf.astype(jnp.float32).reshape(shp))
    k = 1.0 - f
    v = zi.astype(jnp.float32).reshape(shp)
    o = hgrn2_recurrence(q, k, v, jnp.log(f))
    o = rmsnorm(o, gnorm_g) * jax.nn.silu(zg.astype(jnp.float32).reshape(shp))
    return o.reshape(B, T, HG_WIDTH).astype(zq.dtype)


def chunked_spatial_gating(zu, zv, ln_g, ln_b, w_s, b_s):
    B, T, _ = zu.shape
    n = T // GM_CHUNK
    u = jax.nn.gelu(zu, approximate=False)
    v = layernorm(jax.nn.gelu(zv, approximate=False), ln_g, ln_b)
    vc = v.reshape(B, n, GM_CHUNK, GM_GROUPS, GM_GROUP_DIM)
    causal = jnp.tril(jnp.ones((GM_CHUNK, GM_CHUNK), dtype=bool))
    w = jnp.where(causal[None], w_s, 0.0).astype(v.dtype)
    mixed = jnp.einsum('gts,bnsgc->bntgc', w, vc) + b_s.T.astype(v.dtype)[None, None, :, :, None]
    return u * mixed.reshape(B, T, GM_WIDTH)


def _fwd_setup_inputs(seed: int = 0) -> dict:
    key = jax.random.key(seed)
    ks = iter(jax.random.split(key, 32))

    def nrm(shape, scale):
        return jax.random.normal(next(ks), shape, jnp.float32) * scale

    def gain(shape):
        return 1.0 + nrm(shape, 0.05)

    return {
        "x": nrm((BATCH, SEQ, D_MODEL), 1.0),
        "mem": nrm((BATCH, MEM_LEN, D_MODEL), 1.0),
        "mem_norm": gain((D_MODEL,)),
        "lb_logits": nrm((DEPTH + 1, HG_WIDTH), 0.1),
        "ffn1_norm": gain((DEPTH, D_MODEL)),
        "ffn1_w_in": nrm((DEPTH, D_MODEL, 2 * D_FF), D_MODEL ** -0.5),
        "ffn1_w_out": nrm((DEPTH, D_FF, D_MODEL), D_FF ** -0.5),
        "mix_norm": gain((DEPTH, D_MODEL)),
        "mem_w_kv": nrm((DEPTH, D_MODEL, 2 * XA_WIDTH), D_MODEL ** -0.5),
        "hgrn_w_in": nrm((N_HGRN, D_MODEL, HG_IN), D_MODEL ** -0.5),
        "hgrn_gnorm": gain((N_HGRN, HG_DIM)),
        "hgrn_w_out": nrm((N_HGRN, HG_WIDTH + XA_WIDTH, D_MODEL), (HG_WIDTH + XA_WIDTH) ** -0.5),
        "gmlp_w_in": nrm((N_GMLP, D_MODEL, GM_IN), D_MODEL ** -0.5),
        "gmlp_ln_g": gain((N_GMLP, GM_WIDTH)),
        "gmlp_ln_b": nrm((N_GMLP, GM_WIDTH), 0.02),
        "gmlp_w_s": nrm((N_GMLP, GM_GROUPS, GM_CHUNK, GM_CHUNK), GM_CHUNK ** -0.5),
        "gmlp_b_s": 1.0 + nrm((N_GMLP, GM_GROUPS, GM_CHUNK), 0.1),
        "gmlp_w_out": nrm((N_GMLP, GM_WIDTH + XA_WIDTH, D_MODEL), (GM_WIDTH + XA_WIDTH) ** -0.5),
        "ffn2_norm": gain((DEPTH, D_MODEL)),
        "ffn2_w_in": nrm((DEPTH, D_MODEL, 2 * D_FF), D_MODEL ** -0.5),
        "ffn2_w_out": nrm((DEPTH, D_FF, D_MODEL), D_FF ** -0.5),
        "final_norm": gain((D_MODEL,)),
    }


def _fwd_reference(x, mem, mem_norm, lb_logits, ffn1_norm, ffn1_w_in, ffn1_w_out, mix_norm, mem_w_kv,
              hgrn_w_in, hgrn_gnorm, hgrn_w_out, gmlp_w_in, gmlp_ln_g, gmlp_ln_b, gmlp_w_s, gmlp_b_s,
              gmlp_w_out, ffn2_norm, ffn2_w_in, ffn2_w_out, final_norm):
    B, T, _ = x.shape
    M = mem.shape[1]
    mem_n = rmsnorm(mem, mem_norm)
    lower_bounds = jnp.cumsum(jax.nn.softmax(lb_logits.astype(jnp.float32), axis=0), axis=0)

    for i in range(DEPTH):
        x = x + 0.5 * swiglu_ffn(rmsnorm(x, ffn1_norm[i]), ffn1_w_in[i], ffn1_w_out[i])

        h = rmsnorm(x, mix_norm[i])
        mk, mv = jnp.split(mem_n @ mem_w_kv[i], 2, axis=-1)
        mk = mk.reshape(B, M, XA_HEADS, XA_DIM)
        mv = mv.reshape(B, M, XA_HEADS, XA_DIM)
        j = i // N_MIXERS
        if i % N_MIXERS == 0:
            z = h @ hgrn_w_in[j]
            zq, zf, zi, zg, zx = jnp.split(z, [HG_WIDTH, 2 * HG_WIDTH, 3 * HG_WIDTH, 4 * HG_WIDTH], axis=-1)
            o_mix = hgrn2_mixer(zq, zf, zi, zg, lower_bounds[i], hgrn_gnorm[j])
            w_out = hgrn_w_out[j]
        else:
            z = h @ gmlp_w_in[j]
            zu, zv, zx = jnp.split(z, [GM_WIDTH, 2 * GM_WIDTH], axis=-1)
            o_mix = chunked_spatial_gating(zu, zv, gmlp_ln_g[j], gmlp_ln_b[j], gmlp_w_s[j], gmlp_b_s[j])
            w_out = gmlp_w_out[j]
        o_mem = memory_attention(zx, mk, mv)
        x = x + jnp.concatenate([o_mix, o_mem], axis=-1) @ w_out

        x = x + 0.5 * swiglu_ffn(rmsnorm(x, ffn2_norm[i]), ffn2_w_in[i], ffn2_w_out[i])

    return rmsnorm(x, final_norm)


import jax as _jax
import jax.numpy as _jnp

TWIN_FORMAT = 'train_step'
FWD_PARAMS = ['x', 'mem', 'mem_norm', 'lb_logits', 'ffn1_norm', 'ffn1_w_in', 'ffn1_w_out', 'mix_norm', 'mem_w_kv', 'hgrn_w_in', 'hgrn_gnorm', 'hgrn_w_out', 'gmlp_w_in', 'gmlp_ln_g', 'gmlp_ln_b', 'gmlp_w_s', 'gmlp_b_s', 'gmlp_w_out', 'ffn2_norm', 'ffn2_w_in', 'ffn2_w_out', 'final_norm']
TWIN_WEIGHTS = ['mem_norm', 'lb_logits', 'ffn1_norm', 'ffn1_w_in', 'ffn1_w_out', 'mix_norm', 'mem_w_kv', 'hgrn_w_in', 'hgrn_gnorm', 'hgrn_w_out', 'gmlp_w_in', 'gmlp_ln_g', 'gmlp_ln_b', 'gmlp_w_s', 'gmlp_b_s', 'gmlp_w_out', 'ffn2_norm', 'ffn2_w_in', 'ffn2_w_out', 'final_norm']
TWIN_DIFF_INPUT = 'x'
TWIN_INPUTS = ['x', 'mem', 'mem_norm', 'lb_logits', 'ffn1_norm', 'ffn1_w_in', 'ffn1_w_out', 'mix_norm', 'mem_w_kv', 'hgrn_w_in', 'hgrn_gnorm', 'hgrn_w_out', 'gmlp_w_in', 'gmlp_ln_g', 'gmlp_ln_b', 'gmlp_w_s', 'gmlp_b_s', 'gmlp_w_out', 'ffn2_norm', 'ffn2_w_in', 'ffn2_w_out', 'final_norm', 'loss_target', 'm_mem_norm', 'm_lb_logits', 'm_ffn1_norm', 'm_ffn1_w_in', 'm_ffn1_w_out', 'm_mix_norm', 'm_mem_w_kv', 'm_hgrn_w_in', 'm_hgrn_gnorm', 'm_hgrn_w_out', 'm_gmlp_w_in', 'm_gmlp_ln_g', 'm_gmlp_ln_b', 'm_gmlp_w_s', 'm_gmlp_b_s', 'm_gmlp_w_out', 'm_ffn2_norm', 'm_ffn2_w_in', 'm_ffn2_w_out', 'm_final_norm', 'v_mem_norm', 'v_lb_logits', 'v_ffn1_norm', 'v_ffn1_w_in', 'v_ffn1_w_out', 'v_mix_norm', 'v_mem_w_kv', 'v_hgrn_w_in', 'v_hgrn_gnorm', 'v_hgrn_w_out', 'v_gmlp_w_in', 'v_gmlp_ln_g', 'v_gmlp_ln_b', 'v_gmlp_w_s', 'v_gmlp_b_s', 'v_gmlp_w_out', 'v_ffn2_norm', 'v_ffn2_w_in', 'v_ffn2_w_out', 'v_final_norm']
TWIN_OUTPUTS = ['loss', 'grad_x', 'grad_mem_norm', 'grad_lb_logits', 'grad_ffn1_norm', 'grad_ffn1_w_in', 'grad_ffn1_w_out', 'grad_mix_norm', 'grad_mem_w_kv', 'grad_hgrn_w_in', 'grad_hgrn_gnorm', 'grad_hgrn_w_out', 'grad_gmlp_w_in', 'grad_gmlp_ln_g', 'grad_gmlp_ln_b', 'grad_gmlp_w_s', 'grad_gmlp_b_s', 'grad_gmlp_w_out', 'grad_ffn2_norm', 'grad_ffn2_w_in', 'grad_ffn2_w_out', 'grad_final_norm', 'delta_mem_norm', 'delta_lb_logits', 'delta_ffn1_norm', 'delta_ffn1_w_in', 'delta_ffn1_w_out', 'delta_mix_norm', 'delta_mem_w_kv', 'delta_hgrn_w_in', 'delta_hgrn_gnorm', 'delta_hgrn_w_out', 'delta_gmlp_w_in', 'delta_gmlp_ln_g', 'delta_gmlp_ln_b', 'delta_gmlp_w_s', 'delta_gmlp_b_s', 'delta_gmlp_w_out', 'delta_ffn2_norm', 'delta_ffn2_w_in', 'delta_ffn2_w_out', 'delta_final_norm', 'new_m_mem_norm', 'new_m_lb_logits', 'new_m_ffn1_norm', 'new_m_ffn1_w_in', 'new_m_ffn1_w_out', 'new_m_mix_norm', 'new_m_mem_w_kv', 'new_m_hgrn_w_in', 'new_m_hgrn_gnorm', 'new_m_hgrn_w_out', 'new_m_gmlp_w_in', 'new_m_gmlp_ln_g', 'new_m_gmlp_ln_b', 'new_m_gmlp_w_s', 'new_m_gmlp_b_s', 'new_m_gmlp_w_out', 'new_m_ffn2_norm', 'new_m_ffn2_w_in', 'new_m_ffn2_w_out', 'new_m_final_norm', 'new_v_mem_norm', 'new_v_lb_logits', 'new_v_ffn1_norm', 'new_v_ffn1_w_in', 'new_v_ffn1_w_out', 'new_v_mix_norm', 'new_v_mem_w_kv', 'new_v_hgrn_w_in', 'new_v_hgrn_gnorm', 'new_v_hgrn_w_out', 'new_v_gmlp_w_in', 'new_v_gmlp_ln_g', 'new_v_gmlp_ln_b', 'new_v_gmlp_w_s', 'new_v_gmlp_b_s', 'new_v_gmlp_w_out', 'new_v_ffn2_norm', 'new_v_ffn2_w_in', 'new_v_ffn2_w_out', 'new_v_final_norm']
TWIN_LEAF_KINDS = {'loss': 'loss', 'grad_x': 'grad_x', 'grad_mem_norm': 'grad_w', 'grad_lb_logits': 'grad_w', 'grad_ffn1_norm': 'grad_w', 'grad_ffn1_w_in': 'grad_w', 'grad_ffn1_w_out': 'grad_w', 'grad_mix_norm': 'grad_w', 'grad_mem_w_kv': 'grad_w', 'grad_hgrn_w_in': 'grad_w', 'grad_hgrn_gnorm': 'grad_w', 'grad_hgrn_w_out': 'grad_w', 'grad_gmlp_w_in': 'grad_w', 'grad_gmlp_ln_g': 'grad_w', 'grad_gmlp_ln_b': 'grad_w', 'grad_gmlp_w_s': 'grad_w', 'grad_gmlp_b_s': 'grad_w', 'grad_gmlp_w_out': 'grad_w', 'grad_ffn2_norm': 'grad_w', 'grad_ffn2_w_in': 'grad_w', 'grad_ffn2_w_out': 'grad_w', 'grad_final_norm': 'grad_w', 'delta_mem_norm': 'delta_w', 'delta_lb_logits': 'delta_w', 'delta_ffn1_norm': 'delta_w', 'delta_ffn1_w_in': 'delta_w', 'delta_ffn1_w_out': 'delta_w', 'delta_mix_norm': 'delta_w', 'delta_mem_w_kv': 'delta_w', 'delta_hgrn_w_in': 'delta_w', 'delta_hgrn_gnorm': 'delta_w', 'delta_hgrn_w_out': 'delta_w', 'delta_gmlp_w_in': 'delta_w', 'delta_gmlp_ln_g': 'delta_w', 'delta_gmlp_ln_b': 'delta_w', 'delta_gmlp_w_s': 'delta_w', 'delta_gmlp_b_s': 'delta_w', 'delta_gmlp_w_out': 'delta_w', 'delta_ffn2_norm': 'delta_w', 'delta_ffn2_w_in': 'delta_w', 'delta_ffn2_w_out': 'delta_w', 'delta_final_norm': 'delta_w', 'new_m_mem_norm': 'new_m', 'new_m_lb_logits': 'new_m', 'new_m_ffn1_norm': 'new_m', 'new_m_ffn1_w_in': 'new_m', 'new_m_ffn1_w_out': 'new_m', 'new_m_mix_norm': 'new_m', 'new_m_mem_w_kv': 'new_m', 'new_m_hgrn_w_in': 'new_m', 'new_m_hgrn_gnorm': 'new_m', 'new_m_hgrn_w_out': 'new_m', 'new_m_gmlp_w_in': 'new_m', 'new_m_gmlp_ln_g': 'new_m', 'new_m_gmlp_ln_b': 'new_m', 'new_m_gmlp_w_s': 'new_m', 'new_m_gmlp_b_s': 'new_m', 'new_m_gmlp_w_out': 'new_m', 'new_m_ffn2_norm': 'new_m', 'new_m_ffn2_w_in': 'new_m', 'new_m_ffn2_w_out': 'new_m', 'new_m_final_norm': 'new_m', 'new_v_mem_norm': 'new_v', 'new_v_lb_logits': 'new_v', 'new_v_ffn1_norm': 'new_v', 'new_v_ffn1_w_in': 'new_v', 'new_v_ffn1_w_out': 'new_v', 'new_v_mix_norm': 'new_v', 'new_v_mem_w_kv': 'new_v', 'new_v_hgrn_w_in': 'new_v', 'new_v_hgrn_gnorm': 'new_v', 'new_v_hgrn_w_out': 'new_v', 'new_v_gmlp_w_in': 'new_v', 'new_v_gmlp_ln_g': 'new_v', 'new_v_gmlp_ln_b': 'new_v', 'new_v_gmlp_w_s': 'new_v', 'new_v_gmlp_b_s': 'new_v', 'new_v_gmlp_w_out': 'new_v', 'new_v_ffn2_norm': 'new_v', 'new_v_ffn2_w_in': 'new_v', 'new_v_ffn2_w_out': 'new_v', 'new_v_final_norm': 'new_v'}


def _forward(args):
    return _fwd_reference(*[args[k] for k in FWD_PARAMS])


def _output_shape():
    out = _jax.eval_shape(lambda: _forward(_fwd_setup_inputs(0)))
    return out.shape, out.dtype

N_MICROBATCH = 1
ADAM_LR = 0.001
ADAM_B1 = 0.9
ADAM_B2 = 0.999
ADAM_EPS = 1e-08
ADAM_WD = 0.01
ADAM_STEP = 10
PER_EXAMPLE_BATCH_AXIS = {'x': 0, 'mem': 0, 'loss_target': 0}
SHARED_INPUTS = []
_WEIGHT_DTYPES = {'mem_norm': _jnp.float32, 'lb_logits': _jnp.float32, 'ffn1_norm': _jnp.float32, 'ffn1_w_in': _jnp.float32, 'ffn1_w_out': _jnp.float32, 'mix_norm': _jnp.float32, 'mem_w_kv': _jnp.float32, 'hgrn_w_in': _jnp.float32, 'hgrn_gnorm': _jnp.float32, 'hgrn_w_out': _jnp.float32, 'gmlp_w_in': _jnp.float32, 'gmlp_ln_g': _jnp.float32, 'gmlp_ln_b': _jnp.float32, 'gmlp_w_s': _jnp.float32, 'gmlp_b_s': _jnp.float32, 'gmlp_w_out': _jnp.float32, 'ffn2_norm': _jnp.float32, 'ffn2_w_in': _jnp.float32, 'ffn2_w_out': _jnp.float32, 'final_norm': _jnp.float32}
MOMENT_SCALE = {'mem_norm': 2.439683e-02, 'lb_logits': 3.809870e-03, 'ffn1_norm': 8.442147e-02, 'ffn1_w_in': 3.444821e-02, 'ffn1_w_out': 5.643642e-02, 'mix_norm': 1.152439e-01, 'mem_w_kv': 1.149605e-02, 'hgrn_w_in': 5.031043e-02, 'hgrn_gnorm': 2.217639e-01, 'hgrn_w_out': 7.801778e-02, 'gmlp_w_in': 4.838312e-02, 'gmlp_ln_g': 3.523843e-02, 'gmlp_ln_b': 3.457385e-02, 'gmlp_w_s': 5.084247e-02, 'gmlp_b_s': 7.031734e-02, 'gmlp_w_out': 9.964145e-02, 'ffn2_norm': 6.681917e-02, 'ffn2_w_in': 2.780073e-02, 'ffn2_w_out': 4.535595e-02, 'final_norm': 3.211521e+01}


def _to_microbatches(a, axis):
    t = _jnp.moveaxis(a, axis, 0)
    t = t.reshape((N_MICROBATCH, t.shape[0] // N_MICROBATCH) + t.shape[1:])
    return _jnp.moveaxis(t, 1, axis + 1)


def setup_inputs(seed: int = 0) -> dict:
    inp = _fwd_setup_inputs(seed)
    key = _jax.random.fold_in(_jax.random.key(seed), 7919)
    shape, _ = _output_shape()
    out = dict(inp)
    out["loss_target"] = _jax.random.normal(_jax.random.fold_in(key, 0), shape, _jnp.float32)
    for i, name in enumerate(TWIN_WEIGHTS):
        w = inp[name].astype(_jnp.float32)
        if MOMENT_SCALE is None:
            s = _jnp.sqrt(_jnp.mean(_jnp.square(w)) + 1e-30)
        else:
            s = MOMENT_SCALE[name]
        km, kv = _jax.random.split(_jax.random.fold_in(key, i + 1))
        out[name] = w
        out["m_" + name] = s * _jax.random.normal(km, w.shape, _jnp.float32)
        out["v_" + name] = (s * s) * _jax.random.uniform(kv, w.shape, _jnp.float32, 0.5, 1.5)
    if N_MICROBATCH > 1:
        for name, axis in PER_EXAMPLE_BATCH_AXIS.items():
            out[name] = _to_microbatches(out[name], axis)
    return {'x': out['x'], 'mem': out['mem'], 'mem_norm': out['mem_norm'], 'lb_logits': out['lb_logits'], 'ffn1_norm': out['ffn1_norm'], 'ffn1_w_in': out['ffn1_w_in'], 'ffn1_w_out': out['ffn1_w_out'], 'mix_norm': out['mix_norm'], 'mem_w_kv': out['mem_w_kv'], 'hgrn_w_in': out['hgrn_w_in'], 'hgrn_gnorm': out['hgrn_gnorm'], 'hgrn_w_out': out['hgrn_w_out'], 'gmlp_w_in': out['gmlp_w_in'], 'gmlp_ln_g': out['gmlp_ln_g'], 'gmlp_ln_b': out['gmlp_ln_b'], 'gmlp_w_s': out['gmlp_w_s'], 'gmlp_b_s': out['gmlp_b_s'], 'gmlp_w_out': out['gmlp_w_out'], 'ffn2_norm': out['ffn2_norm'], 'ffn2_w_in': out['ffn2_w_in'], 'ffn2_w_out': out['ffn2_w_out'], 'final_norm': out['final_norm'], 'loss_target': out['loss_target'], 'm_mem_norm': out['m_mem_norm'], 'm_lb_logits': out['m_lb_logits'], 'm_ffn1_norm': out['m_ffn1_norm'], 'm_ffn1_w_in': out['m_ffn1_w_in'], 'm_ffn1_w_out': out['m_ffn1_w_out'], 'm_mix_norm': out['m_mix_norm'], 'm_mem_w_kv': out['m_mem_w_kv'], 'm_hgrn_w_in': out['m_hgrn_w_in'], 'm_hgrn_gnorm': out['m_hgrn_gnorm'], 'm_hgrn_w_out': out['m_hgrn_w_out'], 'm_gmlp_w_in': out['m_gmlp_w_in'], 'm_gmlp_ln_g': out['m_gmlp_ln_g'], 'm_gmlp_ln_b': out['m_gmlp_ln_b'], 'm_gmlp_w_s': out['m_gmlp_w_s'], 'm_gmlp_b_s': out['m_gmlp_b_s'], 'm_gmlp_w_out': out['m_gmlp_w_out'], 'm_ffn2_norm': out['m_ffn2_norm'], 'm_ffn2_w_in': out['m_ffn2_w_in'], 'm_ffn2_w_out': out['m_ffn2_w_out'], 'm_final_norm': out['m_final_norm'], 'v_mem_norm': out['v_mem_norm'], 'v_lb_logits': out['v_lb_logits'], 'v_ffn1_norm': out['v_ffn1_norm'], 'v_ffn1_w_in': out['v_ffn1_w_in'], 'v_ffn1_w_out': out['v_ffn1_w_out'], 'v_mix_norm': out['v_mix_norm'], 'v_mem_w_kv': out['v_mem_w_kv'], 'v_hgrn_w_in': out['v_hgrn_w_in'], 'v_hgrn_gnorm': out['v_hgrn_gnorm'], 'v_hgrn_w_out': out['v_hgrn_w_out'], 'v_gmlp_w_in': out['v_gmlp_w_in'], 'v_gmlp_ln_g': out['v_gmlp_ln_g'], 'v_gmlp_ln_b': out['v_gmlp_ln_b'], 'v_gmlp_w_s': out['v_gmlp_w_s'], 'v_gmlp_b_s': out['v_gmlp_b_s'], 'v_gmlp_w_out': out['v_gmlp_w_out'], 'v_ffn2_norm': out['v_ffn2_norm'], 'v_ffn2_w_in': out['v_ffn2_w_in'], 'v_ffn2_w_out': out['v_ffn2_w_out'], 'v_final_norm': out['v_final_norm']}


def _loss(weights, diff, rest, loss_target):
    with _jax.named_scope("forward"):
        args = {**rest, TWIN_DIFF_INPUT: diff, **{k: w.astype(_WEIGHT_DTYPES[k]) for k, w in weights.items()}}
        y = _forward(args)
    with _jax.named_scope("loss_head"):
        err = _jnp.square(y.astype(_jnp.float32) - loss_target)
        return 0.5 * _jnp.sum(_jnp.mean(err, axis=-1)) if err.ndim else 0.5 * err


def _adamw(w, g, m, v):
    m = ADAM_B1 * m + (1.0 - ADAM_B1) * g
    v = ADAM_B2 * v + (1.0 - ADAM_B2) * _jnp.square(g)
    m_hat = m / (1.0 - ADAM_B1 ** ADAM_STEP)
    v_hat = v / (1.0 - ADAM_B2 ** ADAM_STEP)
    delta = -ADAM_LR * (m_hat / (_jnp.sqrt(v_hat) + ADAM_EPS) + ADAM_WD * w)
    return delta, m, v


def reference(x, mem, mem_norm, lb_logits, ffn1_norm, ffn1_w_in, ffn1_w_out, mix_norm, mem_w_kv, hgrn_w_in, hgrn_gnorm, hgrn_w_out, gmlp_w_in, gmlp_ln_g, gmlp_ln_b, gmlp_w_s, gmlp_b_s, gmlp_w_out, ffn2_norm, ffn2_w_in, ffn2_w_out, final_norm, loss_target, m_mem_norm, m_lb_logits, m_ffn1_norm, m_ffn1_w_in, m_ffn1_w_out, m_mix_norm, m_mem_w_kv, m_hgrn_w_in, m_hgrn_gnorm, m_hgrn_w_out, m_gmlp_w_in, m_gmlp_ln_g, m_gmlp_ln_b, m_gmlp_w_s, m_gmlp_b_s, m_gmlp_w_out, m_ffn2_norm, m_ffn2_w_in, m_ffn2_w_out, m_final_norm, v_mem_norm, v_lb_logits, v_ffn1_norm, v_ffn1_w_in, v_ffn1_w_out, v_mix_norm, v_mem_w_kv, v_hgrn_w_in, v_hgrn_gnorm, v_hgrn_w_out, v_gmlp_w_in, v_gmlp_ln_g, v_gmlp_ln_b, v_gmlp_w_s, v_gmlp_b_s, v_gmlp_w_out, v_ffn2_norm, v_ffn2_w_in, v_ffn2_w_out, v_final_norm):
    given = dict(x=x, mem=mem, mem_norm=mem_norm, lb_logits=lb_logits, ffn1_norm=ffn1_norm, ffn1_w_in=ffn1_w_in, ffn1_w_out=ffn1_w_out, mix_norm=mix_norm, mem_w_kv=mem_w_kv, hgrn_w_in=hgrn_w_in, hgrn_gnorm=hgrn_gnorm, hgrn_w_out=hgrn_w_out, gmlp_w_in=gmlp_w_in, gmlp_ln_g=gmlp_ln_g, gmlp_ln_b=gmlp_ln_b, gmlp_w_s=gmlp_w_s, gmlp_b_s=gmlp_b_s, gmlp_w_out=gmlp_w_out, ffn2_norm=ffn2_norm, ffn2_w_in=ffn2_w_in, ffn2_w_out=ffn2_w_out, final_norm=final_norm, loss_target=loss_target, m_mem_norm=m_mem_norm, m_lb_logits=m_lb_logits, m_ffn1_norm=m_ffn1_norm, m_ffn1_w_in=m_ffn1_w_in, m_ffn1_w_out=m_ffn1_w_out, m_mix_norm=m_mix_norm, m_mem_w_kv=m_mem_w_kv, m_hgrn_w_in=m_hgrn_w_in, m_hgrn_gnorm=m_hgrn_gnorm, m_hgrn_w_out=m_hgrn_w_out, m_gmlp_w_in=m_gmlp_w_in, m_gmlp_ln_g=m_gmlp_ln_g, m_gmlp_ln_b=m_gmlp_ln_b, m_gmlp_w_s=m_gmlp_w_s, m_gmlp_b_s=m_gmlp_b_s, m_gmlp_w_out=m_gmlp_w_out, m_ffn2_norm=m_ffn2_norm, m_ffn2_w_in=m_ffn2_w_in, m_ffn2_w_out=m_ffn2_w_out, m_final_norm=m_final_norm, v_mem_norm=v_mem_norm, v_lb_logits=v_lb_logits, v_ffn1_norm=v_ffn1_norm, v_ffn1_w_in=v_ffn1_w_in, v_ffn1_w_out=v_ffn1_w_out, v_mix_norm=v_mix_norm, v_mem_w_kv=v_mem_w_kv, v_hgrn_w_in=v_hgrn_w_in, v_hgrn_gnorm=v_hgrn_gnorm, v_hgrn_w_out=v_hgrn_w_out, v_gmlp_w_in=v_gmlp_w_in, v_gmlp_ln_g=v_gmlp_ln_g, v_gmlp_ln_b=v_gmlp_ln_b, v_gmlp_w_s=v_gmlp_w_s, v_gmlp_b_s=v_gmlp_b_s, v_gmlp_w_out=v_gmlp_w_out, v_ffn2_norm=v_ffn2_norm, v_ffn2_w_in=v_ffn2_w_in, v_ffn2_w_out=v_ffn2_w_out, v_final_norm=v_final_norm)
    weights = {n: given[n] for n in TWIN_WEIGHTS}
    shared = {n: given[n] for n in SHARED_INPUTS}
    per_example = {n: given[n] for n in ['x', 'mem']}
    grad_fn = _jax.value_and_grad(_loss, argnums=(0, 1))

    def one_microbatch(ex, loss_target):
        ex = dict(ex)
        diff = ex.pop(TWIN_DIFF_INPUT)
        return grad_fn(weights, diff, {**shared, **ex}, loss_target)

    if N_MICROBATCH == 1:
        loss, (grad_w, grad_x) = one_microbatch(per_example, given["loss_target"])
    else:
        def body(carry, xs):
            loss_sum, grad_sum = carry
            l_k, (gw_k, gx_k) = one_microbatch(xs[0], xs[1])
            with _jax.named_scope("update"):
                return (loss_sum + l_k, _jax.tree.map(_jnp.add, grad_sum, gw_k)), gx_k

        init = (_jnp.zeros((), _jnp.float32), _jax.tree.map(_jnp.zeros_like, weights))
        (loss, grad_w), grad_x = _jax.lax.scan(body, init, (per_example, given["loss_target"]))
    with _jax.named_scope("update"):
        delta_w, new_m, new_v = {}, {}, {}
        for n in TWIN_WEIGHTS:
            delta_w[n], new_m[n], new_v[n] = _adamw(weights[n], grad_w[n], given["m_" + n], given["v_" + n])
    return (loss, grad_x, *[grad_w[n] for n in TWIN_WEIGHTS], *[delta_w[n] for n in TWIN_WEIGHTS],
            *[new_m[n] for n in TWIN_WEIGHTS], *[new_v[n] for n in TWIN_WEIGHTS])
```

```python
import functools

import jax
import jax.numpy as jnp
from jax import lax
from jax.experimental import pallas as pl
from jax.experimental.pallas import tpu as pltpu

BF = jnp.bfloat16
F32 = jnp.float32
MESH = pl.DeviceIdType.MESH

EPS = 1e-6
D_MODEL = 1024
HG_HEADS = 8
HG_DIM = 128
HG_CHUNK = 64
GM_CHUNK = 128
GM_GROUPS = 8
GM_GROUP_DIM = 256
XA_HEADS = 4
XA_DIM = 256
ADAM_LR = 0.001
ADAM_B1 = 0.9
ADAM_B2 = 0.999
ADAM_EPS = 1e-08
ADAM_WD = 0.01
ADAM_STEP = 10

VMEM_CAP_BYTES = 60 * 1024 * 1024
LANES = 1024


def _pick(n, cap, mult=16):
    if n <= cap:
        return n
    for d in range(cap - cap % mult, 0, -mult):
        if n % d == 0:
            return d
    raise ValueError((n, cap, mult))


def _dg(a, b, ca, cb):
    return lax.dot_general(a.astype(BF), b.astype(BF), (((ca,), (cb,)), ((), ())), preferred_element_type=F32)


@jax.custom_vjp
def dot_nn(a, b):
    return _dg(a, b, 1, 0)


def _nn_fwd(a, b):
    return _dg(a, b, 1, 0), (a, b)


def _nn_bwd(r, g):
    a, b = r
    return _dg(g, b, 1, 1), _dg(a, g, 0, 0)


dot_nn.defvjp(_nn_fwd, _nn_bwd)


@jax.custom_vjp
def dot_nt(a, b):
    return _dg(a, b, 1, 1)


def _nt_fwd(a, b):
    return _dg(a, b, 1, 1), (a, b)


def _nt_bwd(r, g):
    a, b = r
    return _dg(g, b, 1, 0), _dg(g, a, 0, 0)


dot_nt.defvjp(_nt_fwd, _nt_bwd)


@jax.custom_vjp
def dot_tn(a, b):
    return _dg(a, b, 0, 0)


def _tn_fwd(a, b):
    return _dg(a, b, 0, 0), (a, b)


def _tn_bwd(r, g):
    a, b = r
    return _dg(b, g, 1, 1), _dg(a, g, 1, 0)


dot_tn.defvjp(_tn_fwd, _tn_bwd)


def _rmsnorm(x, g):
    return x * lax.rsqrt(jnp.mean(x * x, axis=-1, keepdims=True) + EPS) * g


def _silu(x):
    return x * jax.nn.sigmoid(x)


def _gelu(x):
    return 0.5 * x * (1.0 + lax.erf(x * (0.5 ** 0.5)))


def _softmax_last(s):
    m = lax.stop_gradient(jnp.max(s, axis=-1, keepdims=True))
    e = jnp.exp(s - m)
    return e / jnp.sum(e, axis=-1, keepdims=True)


def _tril(n):
    r = lax.broadcasted_iota(jnp.int32, (n, n), 0)
    c = lax.broadcasted_iota(jnp.int32, (n, n), 1)
    return r >= c


def _cumsum_rows(l):
    n = l.shape[0]
    return lax.dot_general(_tril(n).astype(F32), l, (((1,), (0,)), ((), ())),
                           precision=lax.Precision.HIGHEST, preferred_element_type=F32)


def _attention(zx, mk, mv):
    s = dot_nt(zx, mk) * (XA_DIM ** -0.5)
    return dot_nn(_softmax_last(s), mv)


def _hgrn_head(zq, zf, zi, zg, l0, l1, l2, gn, S):
    m = lax.stop_gradient(jnp.maximum(jnp.maximum(l0, l1), l2))
    e0 = jnp.exp(l0 - m)
    lb = e0 / (e0 + jnp.exp(l1 - m) + jnp.exp(l2 - m))
    q = _silu(zq)
    f = lb + (1.0 - lb) * jax.nn.sigmoid(zf)
    k = 1.0 - f
    b = _cumsum_rows(jnp.log(f))
    b_last = b[HG_CHUNK - 1:HG_CHUNK, :]
    q_dec = q * jnp.exp(b)
    k_inv = k * jnp.exp(-b)
    a = jnp.where(_tril(HG_CHUNK), dot_nt(q_dec, k_inv), 0.0)
    o = dot_nn(a, zi) + dot_nn(q_dec, S)
    S_new = jnp.exp(b_last).reshape(HG_DIM, 1) * S + dot_tn(k * jnp.exp(b_last - b), zi)
    o = _rmsnorm(o, gn) * _silu(zg)
    return o, S_new


def _hgrn_block(zq, zf, zi, zg, zx, l0, l1, l2, gn, mk, mv, S):
    outs, s_new = [], []
    for h in range(HG_HEADS):
        o, sn = _hgrn_head(zq[h], zf[h], zi[h], zg[h], l0[h], l1[h], l2[h], gn, S[h])
        outs.append(o)
        s_new.append(sn)
    for a in range(XA_HEADS):
        outs.append(_attention(zx[a], mk[a], mv[a]))
    return outs, s_new


def _gmlp_block(zu, zv, zx, lng, lnb, ws, bs, mk, mv):
    gv = [_gelu(v) for v in zv]
    width = GM_GROUPS * GM_GROUP_DIM
    mu = sum(jnp.sum(g, axis=-1, keepdims=True) for g in gv) / width
    xc = [g - mu for g in gv]
    var = sum(jnp.sum(c * c, axis=-1, keepdims=True) for c in xc) / width
    r = lax.rsqrt(var + EPS)
    outs = []
    for g in range(GM_GROUPS):
        v = xc[g] * r * lng[g] + lnb[g]
        w = jnp.where(_tril(GM_CHUNK), ws[g], 0.0)
        mixed = dot_nn(w, v) + bs[g].reshape(GM_CHUNK, 1)
        outs.append(_gelu(zu[g]) * mixed)
    for a in range(XA_HEADS):
        outs.append(_attention(zx[a], mk[a], mv[a]))
    return outs


def _rowcall(name, fn, rows, consts, row_outs, acc_outs, tr):
    nrows = rows[0][0].shape[0]
    tr = _pick(nrows, tr)
    n_r, n_c, n_ro, n_ao = len(rows), len(consts), len(row_outs), len(acc_outs)

    def kern(*refs):
        rv = [r[...] for r in refs[:n_r]]
        cv = [r[...] for r in refs[n_r:n_r + n_c]]
        ro_refs = refs[n_r + n_c:n_r + n_c + n_ro]
        ao_refs = refs[n_r + n_c + n_ro:]
        ro, ao = fn(rv, cv)
        for ref, v in zip(ro_refs, ro):
            ref[...] = v.astype(ref.dtype)
        if n_ao:
            @pl.when(pl.program_id(0) == 0)
            def _():
                for ref in ao_refs:
                    ref[...] = jnp.zeros(ref.shape, ref.dtype)

            for ref, v in zip(ao_refs, ao):
                ref[...] += v.astype(ref.dtype)

    in_specs = [pl.BlockSpec((tr, w), functools.partial(lambda i, cb: (i, cb), cb=cb)) for (_, cb, w) in rows]
    in_specs += [pl.BlockSpec(c.shape, lambda i: (0, 0)) for c in consts]
    out_specs = [pl.BlockSpec((tr, w), lambda i: (i, 0)) for (w, _) in row_outs]
    out_specs += [pl.BlockSpec(s, lambda i: (0, 0)) for (s, _) in acc_outs]
    out_shape = [jax.ShapeDtypeStruct((nrows, w), dt) for (w, dt) in row_outs]
    out_shape += [jax.ShapeDtypeStruct(s, dt) for (s, dt) in acc_outs]
    est = sum(tr * w * a.dtype.itemsize for (a, _, w) in rows) + sum(tr * w * jnp.dtype(dt).itemsize for (w, dt) in row_outs)
    est += sum(c.size * c.dtype.itemsize for c in consts)
    outs = pl.pallas_call(
        kern, grid=(nrows // tr,), in_specs=in_specs, out_specs=out_specs, out_shape=out_shape, name=name,
        compiler_params=pltpu.CompilerParams(dimension_semantics=("arbitrary",),
                                             vmem_limit_bytes=int(min(VMEM_CAP_BYTES, 6 * est + (16 << 20)))),
    )(*[a for (a, _, _) in rows], *consts)
    return outs


def _mm(name, a, b, mode, out_dtype, tm, tn, tk, scale=1.0, res=None, a_lead=None, b_lead=None):
    ash = a.shape[-2:]
    bsh = b.shape[-2:]
    if mode == "nn":
        (M, K), (K2, N) = ash, bsh
    elif mode == "nt":
        (M, K), (N, K2) = ash, bsh
    else:
        (K, M), (K2, N) = ash, bsh
    assert K == K2, (name, a.shape, b.shape)
    tm, tn, tk = min(tm, M), min(tn, N), min(tk, K)
    assert M % tm == 0 and N % tn == 0 and K % tk == 0, (name, M, N, K, tm, tn, tk)
    nk = K // tk
    dims = {"nn": (1, 0), "nt": (1, 1), "tn": (0, 0)}[mode]

    def lead(spec_shape, index_fn, lead_idx):
        if lead_idx is None:
            return pl.BlockSpec(spec_shape, index_fn)
        return pl.BlockSpec((None,) + spec_shape, lambda i, j, k: (lead_idx,) + index_fn(i, j, k))

    if mode == "tn":
        a_spec = lead((tk, tm), lambda i, j, k: (k, i), a_lead)
    else:
        a_spec = lead((tm, tk), lambda i, j, k: (i, k), a_lead)
    if mode == "nt":
        b_spec = lead((tn, tk), lambda i, j, k: (j, k), b_lead)
    else:
        b_spec = lead((tk, tn), lambda i, j, k: (k, j), b_lead)
    o_spec = pl.BlockSpec((tm, tn), lambda i, j, k: (i, j))
    has_res = res is not None

    def kern(*refs):
        a_ref, b_ref = refs[0], refs[1]
        res_ref = refs[2] if has_res else None
        o_ref = refs[3] if has_res else refs[2]
        acc_ref = refs[-1] if nk > 1 else None
        p = lax.dot_general(a_ref[...].astype(BF), b_ref[...].astype(BF), (((dims[0],), (dims[1],)), ((), ())),
                            preferred_element_type=F32)

        def finish(v):
            if scale != 1.0:
                v = v * scale
            if has_res:
                v = res_ref[...] + v
            o_ref[...] = v.astype(o_ref.dtype)

        if nk == 1:
            finish(p)
        else:
            k = pl.program_id(2)

            @pl.when(k == 0)
            def _():
                acc_ref[...] = p

            @pl.when(k > 0)
            def _():
                acc_ref[...] += p

            @pl.when(k == nk - 1)
            def _():
                finish(acc_ref[...])

    ins = [a, b] + ([res] if has_res else [])
    in_specs = [a_spec, b_spec] + ([o_spec] if has_res else [])
    est = tm * tk * a.dtype.itemsize + tk * tn * b.dtype.itemsize + tm * tn * (jnp.dtype(out_dtype).itemsize + 8)
    return pl.pallas_call(
        kern, grid=(M // tm, N // tn, nk), in_specs=in_specs, out_specs=o_spec,
        out_shape=jax.ShapeDtypeStruct((M, N), out_dtype),
        scratch_shapes=[pltpu.VMEM((tm, tn), F32)] if nk > 1 else [],
        name=name,
        compiler_params=pltpu.CompilerParams(dimension_semantics=("parallel", "parallel", "arbitrary"),
                                             vmem_limit_bytes=int(min(VMEM_CAP_BYTES, 3 * est + (16 << 20)))),
    )(*ins)


def _hgrn_pieces(z_ref):
    W = HG_HEADS * HG_DIM
    zq = [z_ref[:, h * HG_DIM:(h + 1) * HG_DIM] for h in range(HG_HEADS)]
    zf = [z_ref[:, W + h * HG_DIM:W + (h + 1) * HG_DIM] for h in range(HG_HEADS)]
    zi = [z_ref[:, 2 * W + h * HG_DIM:2 * W + (h + 1) * HG_DIM] for h in range(HG_HEADS)]
    zg = [z_ref[:, 3 * W + h * HG_DIM:3 * W + (h + 1) * HG_DIM] for h in range(HG_HEADS)]
    zx = [z_ref[:, 4 * W + a * XA_DIM:4 * W + (a + 1) * XA_DIM] for a in range(XA_HEADS)]
    return zq, zf, zi, zg, zx


def _kv_pieces(kv_ref):
    W = XA_HEADS * XA_DIM
    mk = [kv_ref[:, a * XA_DIM:(a + 1) * XA_DIM] for a in range(XA_HEADS)]
    mv = [kv_ref[:, W + a * XA_DIM:W + (a + 1) * XA_DIM] for a in range(XA_HEADS)]
    return mk, mv


def _lb_pieces(lb_ref):
    return [[lb_ref[r:r + 1, h * HG_DIM:(h + 1) * HG_DIM] for h in range(HG_HEADS)] for r in range(3)]


def _hgrn_fwd(z, lb_logits, gnorm, kv, bl, nc):
    T, zw = z.shape
    mem_len = kv.shape[0] // bl
    cat_w = HG_HEADS * HG_DIM + XA_HEADS * XA_DIM

    def kern(z_ref, lb_ref, gn_ref, kv_ref, cat_ref, st_ref, s_scr):
        @pl.when(pl.program_id(1) == 0)
        def _():
            s_scr[...] = jnp.zeros(s_scr.shape, F32)

        st_ref[...] = s_scr[...]
        zq, zf, zi, zg, zx = _hgrn_pieces(z_ref)
        mk, mv = _kv_pieces(kv_ref)
        l0, l1, l2 = _lb_pieces(lb_ref)
        S = [s_scr[h] for h in range(HG_HEADS)]
        outs, s_new = _hgrn_block(zq, zf, zi, zg, zx, l0, l1, l2, gn_ref[...], mk, mv, S)
        for h in range(HG_HEADS):
            cat_ref[:, h * HG_DIM:(h + 1) * HG_DIM] = outs[h].astype(cat_ref.dtype)
            s_scr[h] = s_new[h]
        base = HG_HEADS * HG_DIM
        for a in range(XA_HEADS):
            cat_ref[:, base + a * XA_DIM:base + (a + 1) * XA_DIM] = outs[HG_HEADS + a].astype(cat_ref.dtype)

    return pl.pallas_call(
        kern, grid=(bl, nc),
        in_specs=[pl.BlockSpec((HG_CHUNK, zw), lambda b, n: (b * nc + n, 0)),
                  pl.BlockSpec(lb_logits.shape, lambda b, n: (0, 0)),
                  pl.BlockSpec(gnorm.shape, lambda b, n: (0, 0)),
                  pl.BlockSpec((mem_len, kv.shape[1]), lambda b, n: (b, 0))],
        out_specs=[pl.BlockSpec((HG_CHUNK, cat_w), lambda b, n: (b * nc + n, 0)),
                   pl.BlockSpec((None, HG_HEADS, HG_DIM, HG_DIM), lambda b, n: (b * nc + n, 0, 0, 0))],
        out_shape=[jax.ShapeDtypeStruct((T, cat_w), BF),
                   jax.ShapeDtypeStruct((bl * nc, HG_HEADS, HG_DIM, HG_DIM), F32)],
        scratch_shapes=[pltpu.VMEM((HG_HEADS, HG_DIM, HG_DIM), F32)],
        name="hgrn_fwd",
        compiler_params=pltpu.CompilerParams(dimension_semantics=("arbitrary", "arbitrary"), vmem_limit_bytes=48 << 20),
    )(z, lb_logits, gnorm, kv)


def _hgrn_bwd(z, dcat, stash, lb_logits, gnorm, kv, bl, nc):
    T, zw = z.shape
    mem_len = kv.shape[0] // bl
    cat_w = dcat.shape[1]

    def kern(z_ref, dc_ref, st_ref, lb_ref, gn_ref, kv_ref, dz_ref, dkv_ref, dlb_ref, dgn_ref, ds_scr):
        first = jnp.logical_and(pl.program_id(0) == 0, pl.program_id(1) == 0)

        @pl.when(pl.program_id(1) == 0)
        def _():
            ds_scr[...] = jnp.zeros(ds_scr.shape, F32)
            dkv_ref[...] = jnp.zeros(dkv_ref.shape, F32)

        @pl.when(first)
        def _():
            dlb_ref[...] = jnp.zeros(dlb_ref.shape, F32)
            dgn_ref[...] = jnp.zeros(dgn_ref.shape, F32)

        zq, zf, zi, zg, zx = _hgrn_pieces(z_ref)
        mk, mv = _kv_pieces(kv_ref)
        l0, l1, l2 = _lb_pieces(lb_ref)
        S = [st_ref[h] for h in range(HG_HEADS)]
        _, vjp = jax.vjp(_hgrn_block, zq, zf, zi, zg, zx, l0, l1, l2, gn_ref[...], mk, mv, S)
        d_outs = [dc_ref[:, h * HG_DIM:(h + 1) * HG_DIM] for h in range(HG_HEADS)]
        base = HG_HEADS * HG_DIM
        d_outs += [dc_ref[:, base + a * XA_DIM:base + (a + 1) * XA_DIM] for a in range(XA_HEADS)]
        d_s = [ds_scr[h] for h in range(HG_HEADS)]
        dzq, dzf, dzi, dzg, dzx, dl0, dl1, dl2, dgn, dmk, dmv, dS = vjp((d_outs, d_s))
        W = HG_HEADS * HG_DIM
        for h in range(HG_HEADS):
            sl = slice(h * HG_DIM, (h + 1) * HG_DIM)
            dz_ref[:, sl] = dzq[h].astype(dz_ref.dtype)
            dz_ref[:, W + h * HG_DIM:W + (h + 1) * HG_DIM] = dzf[h].astype(dz_ref.dtype)
            dz_ref[:, 2 * W + h * HG_DIM:2 * W + (h + 1) * HG_DIM] = dzi[h].astype(dz_ref.dtype)
            dz_ref[:, 3 * W + h * HG_DIM:3 * W + (h + 1) * HG_DIM] = dzg[h].astype(dz_ref.dtype)
            ds_scr[h] = dS[h]
            dlb_ref[0:1, sl] += dl0[h]
            dlb_ref[1:2, sl] += dl1[h]
            dlb_ref[2:3, sl] += dl2[h]
        dgn_ref[...] += dgn
        KW = XA_HEADS * XA_DIM
        for a in range(XA_HEADS):
            dz_ref[:, 4 * W + a * XA_DIM:4 * W + (a + 1) * XA_DIM] = dzx[a].astype(dz_ref.dtype)
            dkv_ref[:, a * XA_DIM:(a + 1) * XA_DIM] += dmk[a]
            dkv_ref[:, KW + a * XA_DIM:KW + (a + 1) * XA_DIM] += dmv[a]

    rev = lambda b, n: (b * nc + (nc - 1 - n), 0)
    return pl.pallas_call(
        kern, grid=(bl, nc),
        in_specs=[pl.BlockSpec((HG_CHUNK, zw), rev),
                  pl.BlockSpec((HG_CHUNK, cat_w), rev),
                  pl.BlockSpec((None, HG_HEADS, HG_DIM, HG_DIM), lambda b, n: (b * nc + (nc - 1 - n), 0, 0, 0)),
                  pl.BlockSpec(lb_logits.shape, lambda b, n: (0, 0)),
                  pl.BlockSpec(gnorm.shape, lambda b, n: (0, 0)),
                  pl.BlockSpec((mem_len, kv.shape[1]), lambda b, n: (b, 0))],
        out_specs=[pl.BlockSpec((HG_CHUNK, zw), rev),
                   pl.BlockSpec((mem_len, kv.shape[1]), lambda b, n: (b, 0)),
                   pl.BlockSpec(lb_logits.shape, lambda b, n: (0, 0)),
                   pl.BlockSpec(gnorm.shape, lambda b, n: (0, 0))],
        out_shape=[jax.ShapeDtypeStruct((T, zw), BF), jax.ShapeDtypeStruct(kv.shape, F32),
                   jax.ShapeDtypeStruct(lb_logits.shape, F32), jax.ShapeDtypeStruct(gnorm.shape, F32)],
        scratch_shapes=[pltpu.VMEM((HG_HEADS, HG_DIM, HG_DIM), F32)],
        name="hgrn_bwd",
        compiler_params=pltpu.CompilerParams(dimension_semantics=("arbitrary", "arbitrary"), vmem_limit_bytes=56 << 20),
    )(z, dcat, stash, lb_logits, gnorm, kv)


def _gmlp_pieces(z_ref):
    W = GM_GROUPS * GM_GROUP_DIM
    zu = [z_ref[:, g * GM_GROUP_DIM:(g + 1) * GM_GROUP_DIM] for g in range(GM_GROUPS)]
    zv = [z_ref[:, W + g * GM_GROUP_DIM:W + (g + 1) * GM_GROUP_DIM] for g in range(GM_GROUPS)]
    zx = [z_ref[:, 2 * W + a * XA_DIM:2 * W + (a + 1) * XA_DIM] for a in range(XA_HEADS)]
    return zu, zv, zx


def _gmlp_params(lng_ref, lnb_ref, ws_ref, bs_ref):
    lng = [lng_ref[:, g * GM_GROUP_DIM:(g + 1) * GM_GROUP_DIM] for g in range(GM_GROUPS)]
    lnb = [lnb_ref[:, g * GM_GROUP_DIM:(g + 1) * GM_GROUP_DIM] for g in range(GM_GROUPS)]
    ws = [ws_ref[g] for g in range(GM_GROUPS)]
    bs = [bs_ref[g:g + 1, :] for g in range(GM_GROUPS)]
    return lng, lnb, ws, bs


def _gmlp_fwd(z, ln_g, ln_b, w_s, b_s, kv, bl, nc):
    T, zw = z.shape
    mem_len = kv.shape[0] // bl
    cat_w = GM_GROUPS * GM_GROUP_DIM + XA_HEADS * XA_DIM

    def kern(z_ref, lng_ref, lnb_ref, ws_ref, bs_ref, kv_ref, cat_ref):
        zu, zv, zx = _gmlp_pieces(z_ref)
        lng, lnb, ws, bs = _gmlp_params(lng_ref, lnb_ref, ws_ref, bs_ref)
        mk, mv = _kv_pieces(kv_ref)
        outs = _gmlp_block(zu, zv, zx, lng, lnb, ws, bs, mk, mv)
        for g in range(GM_GROUPS):
            cat_ref[:, g * GM_GROUP_DIM:(g + 1) * GM_GROUP_DIM] = outs[g].astype(cat_ref.dtype)
        base = GM_GROUPS * GM_GROUP_DIM
        for a in range(XA_HEADS):
            cat_ref[:, base + a * XA_DIM:base + (a + 1) * XA_DIM] = outs[GM_GROUPS + a].astype(cat_ref.dtype)

    full2 = lambda b, n: (0, 0)
    return pl.pallas_call(
        kern, grid=(bl, nc),
        in_specs=[pl.BlockSpec((GM_CHUNK, zw), lambda b, n: (b * nc + n, 0)),
                  pl.BlockSpec(ln_g.shape, full2), pl.BlockSpec(ln_b.shape, full2),
                  pl.BlockSpec(w_s.shape, lambda b, n: (0, 0, 0)), pl.BlockSpec(b_s.shape, full2),
                  pl.BlockSpec((mem_len, kv.shape[1]), lambda b, n: (b, 0))],
        out_specs=pl.BlockSpec((GM_CHUNK, cat_w), lambda b, n: (b * nc + n, 0)),
        out_shape=jax.ShapeDtypeStruct((T, cat_w), BF),
        name="gmlp_fwd",
        compiler_params=pltpu.CompilerParams(dimension_semantics=("arbitrary", "arbitrary"), vmem_limit_bytes=48 << 20),
    )(z, ln_g, ln_b, w_s, b_s, kv)


def _gmlp_bwd(z, dcat, ln_g, ln_b, w_s, b_s, kv, bl, nc):
    T, zw = z.shape
    mem_len = kv.shape[0] // bl
    cat_w = dcat.shape[1]

    def kern(z_ref, dc_ref, lng_ref, lnb_ref, ws_ref, bs_ref, kv_ref,
             dz_ref, dkv_ref, dlng_ref, dlnb_ref, dws_ref, dbs_ref):
        first = jnp.logical_and(pl.program_id(0) == 0, pl.program_id(1) == 0)

        @pl.when(pl.program_id(1) == 0)
        def _():
            dkv_ref[...] = jnp.zeros(dkv_ref.shape, F32)

        @pl.when(first)
        def _():
            dlng_ref[...] = jnp.zeros(dlng_ref.shape, F32)
            dlnb_ref[...] = jnp.zeros(dlnb_ref.shape, F32)
            dws_ref[...] = jnp.zeros(dws_ref.shape, F32)
            dbs_ref[...] = jnp.zeros(dbs_ref.shape, F32)

        zu, zv, zx = _gmlp_pieces(z_ref)
        lng, lnb, ws, bs = _gmlp_params(lng_ref, lnb_ref, ws_ref, bs_ref)
        mk, mv = _kv_pieces(kv_ref)
        _, vjp = jax.vjp(_gmlp_block, zu, zv, zx, lng, lnb, ws, bs, mk, mv)
        d_outs = [dc_ref[:, g * GM_GROUP_DIM:(g + 1) * GM_GROUP_DIM] for g in range(GM_GROUPS)]
        base = GM_GROUPS * GM_GROUP_DIM
        d_outs += [dc_ref[:, base + a * XA_DIM:base + (a + 1) * XA_DIM] for a in range(XA_HEADS)]
        dzu, dzv, dzx, dlng, dlnb, dws, dbs, dmk, dmv = vjp(d_outs)
        W = GM_GROUPS * GM_GROUP_DIM
        for g in range(GM_GROUPS):
            sl = slice(g * GM_GROUP_DIM, (g + 1) * GM_GROUP_DIM)
            dz_ref[:, sl] = dzu[g].astype(dz_ref.dtype)
            dz_ref[:, W + g * GM_GROUP_DIM:W + (g + 1) * GM_GROUP_DIM] = dzv[g].astype(dz_ref.dtype)
            dlng_ref[:, sl] += dlng[g]
            dlnb_ref[:, sl] += dlnb[g]
            dws_ref[g] += dws[g]
            dbs_ref[g:g + 1, :] += dbs[g]
        KW = XA_HEADS * XA_DIM
        for a in range(XA_HEADS):
            dz_ref[:, 2 * W + a * XA_DIM:2 * W + (a + 1) * XA_DIM] = dzx[a].astype(dz_ref.dtype)
            dkv_ref[:, a * XA_DIM:(a + 1) * XA_DIM] += dmk[a]
            dkv_ref[:, KW + a * XA_DIM:KW + (a + 1) * XA_DIM] += dmv[a]

    full2 = lambda b, n: (0, 0)
    full3 = lambda b, n: (0, 0, 0)
    blk = lambda b, n: (b * nc + n, 0)
    return pl.pallas_call(
        kern, grid=(bl, nc),
        in_specs=[pl.BlockSpec((GM_CHUNK, zw), blk), pl.BlockSpec((GM_CHUNK, cat_w), blk),
                  pl.BlockSpec(ln_g.shape, full2), pl.BlockSpec(ln_b.shape, full2),
                  pl.BlockSpec(w_s.shape, full3), pl.BlockSpec(b_s.shape, full2),
                  pl.BlockSpec((mem_len, kv.shape[1]), lambda b, n: (b, 0))],
        out_specs=[pl.BlockSpec((GM_CHUNK, zw), blk),
                   pl.BlockSpec((mem_len, kv.shape[1]), lambda b, n: (b, 0)),
                   pl.BlockSpec(ln_g.shape, full2), pl.BlockSpec(ln_b.shape, full2),
                   pl.BlockSpec(w_s.shape, full3), pl.BlockSpec(b_s.shape, full2)],
        out_shape=[jax.ShapeDtypeStruct((T, zw), BF), jax.ShapeDtypeStruct(kv.shape, F32),
                   jax.ShapeDtypeStruct(ln_g.shape, F32), jax.ShapeDtypeStruct(ln_b.shape, F32),
                   jax.ShapeDtypeStruct(w_s.shape, F32), jax.ShapeDtypeStruct(b_s.shape, F32)],
        name="gmlp_bwd",
        compiler_params=pltpu.CompilerParams(dimension_semantics=("arbitrary", "arbitrary"), vmem_limit_bytes=56 << 20),
    )(z, dcat, ln_g, ln_b, w_s, b_s, kv)


def _place():
    x, y, c = lax.axis_index("x"), lax.axis_index("y"), lax.axis_index("c")
    chips = [(1 - x, y), (x, 1 - y), (1 - x, 1 - y)]
    return x, y, c, chips


def _half(ref, kind, e):
    if kind == "col":
        n = ref.shape[1] // 2
        return ref.at[:, pl.ds(pl.multiple_of(e * n, n), n), :]
    n = ref.shape[2] // 2
    return ref.at[:, :, pl.ds(pl.multiple_of(e * n, n), n)]


def _slot(ref, kind, j, n):
    if kind == "col":
        return ref.at[:, :, pl.ds(pl.multiple_of(j * n, n), n)]
    return ref.at[:, pl.ds(pl.multiple_of(j * n, n), n), :]


def _allgather(shards, kinds):
    nt = len(shards)
    out_shape = []
    for s, k in zip(shards, kinds):
        L, r, c = s.shape
        out_shape.append(jax.ShapeDtypeStruct((L, r, 4 * c) if k == "col" else (L, 4 * r, c), s.dtype))

    def body(*refs):
        sh, full = refs[:nt], refs[nt:2 * nt]
        loc, s_ici, r_ici, s_d2d, r_d2d = refs[2 * nt:]
        x, y, c, chips = _place()
        own = 2 * x + y
        started = []
        for t in range(nt):
            k = kinds[t]
            n = sh[t].shape[2] if k == "col" else sh[t].shape[1]
            mine = pltpu.make_async_copy(sh[t], _slot(full[t], k, own, n), loc.at[t])
            mine.start()
            started.append(mine)
        ici = []
        for t in range(nt):
            k = kinds[t]
            n = sh[t].shape[2] if k == "col" else sh[t].shape[1]
            for p, (px, py) in enumerate(chips):
                cp = pltpu.make_async_remote_copy(
                    src_ref=_half(sh[t], k, c), dst_ref=_half(_slot(full[t], k, own, n), k, c),
                    send_sem=s_ici.at[t, p], recv_sem=r_ici.at[t, p], device_id=(px, py, c), device_id_type=MESH)
                cp.start()
                ici.append(cp)
        d2d = []
        for t in range(nt):
            k = kinds[t]
            n = sh[t].shape[2] if k == "col" else sh[t].shape[1]
            for p, (px, py) in enumerate(chips):
                landed = _half(_slot(full[t], k, 2 * px + py, n), k, c)
                pltpu.make_async_remote_copy(
                    src_ref=landed, dst_ref=landed, send_sem=s_ici.at[t, p], recv_sem=r_ici.at[t, p],
                    device_id=(px, py, c), device_id_type=MESH).wait_recv()
                fw = pltpu.make_async_remote_copy(
                    src_ref=landed, dst_ref=landed, send_sem=s_d2d.at[t, p], recv_sem=r_d2d.at[t, p],
                    device_id=(x, y, 1 - c), device_id_type=MESH)
                fw.start()
                d2d.append(fw)
        for t in range(nt):
            k = kinds[t]
            n = sh[t].shape[2] if k == "col" else sh[t].shape[1]
            for p, (px, py) in enumerate(chips):
                other = _half(_slot(full[t], k, 2 * px + py, n), k, 1 - c)
                pltpu.make_async_remote_copy(
                    src_ref=other, dst_ref=other, send_sem=s_d2d.at[t, p], recv_sem=r_d2d.at[t, p],
                    device_id=(x, y, 1 - c), device_id_type=MESH).wait_recv()
        for cp in ici + d2d:
            cp.wait_send()
        for cp in started:
            cp.wait()

    hbm = pl.BlockSpec(memory_space=pl.ANY)
    return pl.pallas_call(
        body, out_shape=out_shape, in_specs=[hbm] * nt, out_specs=[hbm] * nt,
        scratch_shapes=[pltpu.SemaphoreType.DMA((nt,)), pltpu.SemaphoreType.DMA((nt, 3)), pltpu.SemaphoreType.DMA((nt, 3)),
                        pltpu.SemaphoreType.DMA((nt, 3)), pltpu.SemaphoreType.DMA((nt, 3))],
        name="allgather_weights",
    )(*shards)


def _rs_sibling(grads, kinds):
    nt = len(grads)
    out_shape = []
    for g, k in zip(grads, kinds):
        _, r, c = g.shape
        hs = (1, r // 2, c) if k == "col" else (1, r, c // 2)
        out_shape += [jax.ShapeDtypeStruct(hs, g.dtype)] * 2

    def body(*refs):
        g = refs[:nt]
        outs = refs[nt:3 * nt]
        loc, ssem, rsem = refs[3 * nt:]
        x, y, c, _ = _place()
        cps = []
        for t in range(nt):
            own_o, got_o = outs[2 * t], outs[2 * t + 1]
            lc = pltpu.make_async_copy(_half(g[t], kinds[t], c), own_o, loc.at[t])
            lc.start()
            cp = pltpu.make_async_remote_copy(
                src_ref=_half(g[t], kinds[t], 1 - c), dst_ref=got_o, send_sem=ssem.at[t], recv_sem=rsem.at[t],
                device_id=(x, y, 1 - c), device_id_type=MESH)
            cp.start()
            cps.append((lc, cp))
        for lc, cp in cps:
            cp.wait()
            lc.wait()

    hbm = pl.BlockSpec(memory_space=pl.ANY)
    return pl.pallas_call(
        body, out_shape=out_shape, in_specs=[hbm] * nt, out_specs=[hbm] * (2 * nt),
        scratch_shapes=[pltpu.SemaphoreType.DMA((nt,)), pltpu.SemaphoreType.DMA((nt,)), pltpu.SemaphoreType.DMA((nt,))],
        name="reduce_sibling",
    )(*grads)


def _rs_chips(parts, kinds):
    nt = len(parts)
    out_shape = []
    for g, k in zip(parts, kinds):
        _, r, c = g.shape
        ps = (1, r, c // 4) if k == "col" else (1, r // 4, c)
        out_shape += [jax.ShapeDtypeStruct(ps, g.dtype), jax.ShapeDtypeStruct((3,) + ps[1:], g.dtype)]

    def body(*refs):
        g = refs[:nt]
        outs = refs[nt:3 * nt]
        loc, ssem, rsem = refs[3 * nt:]
        x, y, c, chips = _place()
        own = 2 * x + y
        cps = []
        for t in range(nt):
            k = kinds[t]
            own_o, got_o = outs[2 * t], outs[2 * t + 1]
            n = g[t].shape[2] // 4 if k == "col" else g[t].shape[1] // 4
            lc = pltpu.make_async_copy(_slot(g[t], k, own, n), own_o, loc.at[t])
            lc.start()
            cps.append(lc)
            for p, (px, py) in enumerate(chips):
                cp = pltpu.make_async_remote_copy(
                    src_ref=_slot(g[t], k, 2 * px + py, n), dst_ref=got_o.at[pl.ds(p, 1)],
                    send_sem=ssem.at[t, p], recv_sem=rsem.at[t, p], device_id=(px, py, c), device_id_type=MESH)
                cp.start()
                cps.append(cp)
        for cp in cps:
            cp.wait()

    hbm = pl.BlockSpec(memory_space=pl.ANY)
    return pl.pallas_call(
        body, out_shape=out_shape, in_specs=[hbm] * nt, out_specs=[hbm] * (2 * nt),
        scratch_shapes=[pltpu.SemaphoreType.DMA((nt,)), pltpu.SemaphoreType.DMA((nt, 3)), pltpu.SemaphoreType.DMA((nt, 3))],
        name="reduce_chips",
    )(*parts)


def _share_halves(halves, kinds, layers):
    nm = len(halves)
    nt = len(layers)
    out_shape = []
    m = 0
    for t in range(nt):
        _, r, c = halves[m].shape
        k = kinds[t]
        out_shape.append(jax.ShapeDtypeStruct((layers[t], 2 * r, c) if k == "col" else (layers[t], r, 2 * c), halves[m].dtype))
        m += layers[t]

    def body(*refs):
        h = refs[:nm]
        full = refs[nm:nm + nt]
        loc, ssem, rsem = refs[nm + nt:]
        x, y, c, _ = _place()
        cps = []
        m = 0
        for t in range(nt):
            for l in range(layers[t]):
                dst = _half(full[t].at[pl.ds(l, 1)], kinds[t], c)
                lc = pltpu.make_async_copy(h[m], dst, loc.at[m])
                lc.start()
                cp = pltpu.make_async_remote_copy(src_ref=h[m], dst_ref=dst, send_sem=ssem.at[m], recv_sem=rsem.at[m],
                                                  device_id=(x, y, 1 - c), device_id_type=MESH)
                cp.start()
                cps.append((lc, cp, _half(full[t].at[pl.ds(l, 1)], kinds[t], 1 - c), m))
                m += 1
        for lc, cp, other, m in cps:
            cp.wait_send()
            pltpu.make_async_remote_copy(src_ref=other, dst_ref=other, send_sem=ssem.at[m], recv_sem=rsem.at[m],
                                         device_id=(x, y, 1 - c), device_id_type=MESH).wait_recv()
            lc.wait()

    hbm = pl.BlockSpec(memory_space=pl.ANY)
    return pl.pallas_call(
        body, out_shape=out_shape, in_specs=[hbm] * nm, out_specs=[hbm] * nt,
        scratch_shapes=[pltpu.SemaphoreType.DMA((nm,)), pltpu.SemaphoreType.DMA((nm,)), pltpu.SemaphoreType.DMA((nm,))],
        name="share_halves",
    )(*halves)


def _small_allreduce(buf, name):
    R = buf.shape[0]

    def body(x_ref, o_ref, slots, ssem, rsem):
        x, y, c, _ = _place()
        me = 4 * x + 2 * y + c
        slots[0] = x_ref[...]
        cps = []
        for k in range(1, 8):
            bx, by, bc = (k >> 2) & 1, (k >> 1) & 1, k & 1
            peer = (1 - x if bx else x, 1 - y if by else y, 1 - c if bc else c)
            cp = pltpu.make_async_remote_copy(src_ref=x_ref, dst_ref=slots.at[k], send_sem=ssem.at[k - 1],
                                              recv_sem=rsem.at[k - 1], device_id=peer, device_id_type=MESH)
            cp.start()
            cps.append(cp)
        for cp in cps:
            cp.wait()
        acc = slots[jnp.bitwise_xor(me, 0)]
        for d in range(1, 8):
            acc = acc + slots[jnp.bitwise_xor(me, d)]
        o_ref[...] = acc

    vm = pl.BlockSpec(memory_space=pltpu.VMEM)
    return pl.pallas_call(
        body, out_shape=jax.ShapeDtypeStruct(buf.shape, F32), in_specs=[vm], out_specs=vm,
        scratch_shapes=[pltpu.VMEM((8, R, LANES), F32), pltpu.SemaphoreType.DMA((7,)), pltpu.SemaphoreType.DMA((7,))],
        name=name,
        compiler_params=pltpu.CompilerParams(vmem_limit_bytes=int(min(VMEM_CAP_BYTES, 12 * R * LANES * 4 + (8 << 20)))),
    )(buf)


def _pack(arrs, rows_total):
    rows = []
    for a in arrs:
        f = a.reshape(-1).astype(F32)
        n = -(-f.shape[0] // LANES) * LANES
        rows.append(jnp.pad(f, (0, n - f.shape[0])).reshape(-1, LANES))
    buf = jnp.concatenate(rows, axis=0)
    return jnp.pad(buf, ((0, rows_total - buf.shape[0]), (0, 0)))


def _unpack(buf, shapes):
    out, r = [], 0
    for s in shapes:
        n = 1
        for d in s:
            n *= d
        nr = -(-n // LANES)
        out.append(buf[r:r + nr].reshape(-1)[:n].reshape(s))
        r += nr
    return out


def _rows_needed(shapes):
    tot = 0
    for s in shapes:
        n = 1
        for d in s:
            n *= d
        tot += -(-n // LANES)
    return -(-tot // 8) * 8


def _adam(w, g, m, v):
    m = ADAM_B1 * m + (1.0 - ADAM_B1) * g
    v = ADAM_B2 * v + (1.0 - ADAM_B2) * jnp.square(g)
    m_hat = m / (1.0 - ADAM_B1 ** ADAM_STEP)
    v_hat = v / (1.0 - ADAM_B2 ** ADAM_STEP)
    delta = -ADAM_LR * (m_hat / (jnp.sqrt(v_hat) + ADAM_EPS) + ADAM_WD * w)
    return delta, m, v


def _adam_call(name, w2, g2, m2, v2, tr):
    def fn(rv, cv):
        return list(_adam(*rv)), []

    width = w2.shape[1]
    return _rowcall(name, fn, [(w2, 0, width), (g2, 0, width), (m2, 0, width), (v2, 0, width)], [],
                    [(width, F32)] * 3, [], tr)


def kernel(x, mem, mem_norm, lb_logits, ffn1_norm, ffn1_w_in, ffn1_w_out, mix_norm, mem_w_kv, hgrn_w_in, hgrn_gnorm, hgrn_w_out, gmlp_w_in, gmlp_ln_g, gmlp_ln_b, gmlp_w_s, gmlp_b_s, gmlp_w_out, ffn2_norm, ffn2_w_in, ffn2_w_out, final_norm, loss_target, m_mem_norm, m_lb_logits, m_ffn1_norm, m_ffn1_w_in, m_ffn1_w_out, m_mix_norm, m_mem_w_kv, m_hgrn_w_in, m_hgrn_gnorm, m_hgrn_w_out, m_gmlp_w_in, m_gmlp_ln_g, m_gmlp_ln_b, m_gmlp_w_s, m_gmlp_b_s, m_gmlp_w_out, m_ffn2_norm, m_ffn2_w_in, m_ffn2_w_out, m_final_norm, v_mem_norm, v_lb_logits, v_ffn1_norm, v_ffn1_w_in, v_ffn1_w_out, v_mix_norm, v_mem_w_kv, v_hgrn_w_in, v_hgrn_gnorm, v_hgrn_w_out, v_gmlp_w_in, v_gmlp_ln_g, v_gmlp_ln_b, v_gmlp_w_s, v_gmlp_b_s, v_gmlp_w_out, v_ffn2_norm, v_ffn2_w_in, v_ffn2_w_out, v_final_norm):
    bl, seq, D = x.shape
    T = bl * seq
    mem_len = mem.shape[1]
    chip = 2 * lax.axis_index("x") + lax.axis_index("y")
    TR = 256

    big = [("ffn1_w_in", ffn1_w_in, "col"), ("ffn1_w_out", ffn1_w_out, "row"), ("mem_w_kv", mem_w_kv, "col"),
           ("hgrn_w_in", hgrn_w_in, "col"), ("hgrn_w_out", hgrn_w_out, "row"), ("gmlp_w_in", gmlp_w_in, "col"),
           ("gmlp_w_out", gmlp_w_out, "row"), ("ffn2_w_in", ffn2_w_in, "col"), ("ffn2_w_out", ffn2_w_out, "row")]
    kinds = [k for (_, _, k) in big]
    shards_bf = []
    for nm, w, _ in big:
        L, r, c = w.shape
        (wb,) = _rowcall("cast_" + nm, lambda rv, cv: ([rv[0]], []), [(w.reshape(L * r, c), 0, c)], [], [(c, BF)], [], 512)
        shards_bf.append(wb.reshape(L, r, c))
    gathered = dict(zip([nm for (nm, _, _) in big], _allgather(shards_bf, kinds)))

    ln_w = GM_GROUPS * GM_GROUP_DIM
    placed = lax.dynamic_update_slice(jnp.zeros((8, ln_w), F32), jnp.concatenate([0.5 * gmlp_ln_g, 0.5 * gmlp_ln_b], axis=0),
                                      (0, chip * gmlp_ln_g.shape[1]))
    ln_full = _small_allreduce(placed.reshape(16, LANES), "gather_ln").reshape(8, ln_w)
    ln_g_full, ln_b_full = ln_full[0:1], ln_full[1:2]

    def rms_fwd(name, xin, g):
        (h,) = _rowcall(name, lambda rv, cv: ([_rmsnorm(rv[0], cv[0])], []), [(xin, 0, D)], [g.reshape(1, D)], [(D, BF)], [], TR)
        return h

    def ffn_fwd(tag, xin, g, w_in, w_out, layer):
        dff = w_out.shape[1]
        h = rms_fwd("rms_" + tag, xin, g)
        z = _mm("ffn_in_" + tag, h, w_in, "nn", BF, 512, 512, D, b_lead=layer)
        (a,) = _rowcall("swiglu_" + tag, lambda rv, cv: ([_silu(rv[0].astype(F32)) * rv[1].astype(F32)], []),
                        [(z, 0, dff), (z, 1, dff)], [], [(dff, BF)], [], TR)
        xo = _mm("ffn_out_" + tag, a, w_out, "nn", F32, 512, 1024, dff, scale=0.5, res=xin, b_lead=layer)
        return xo, (xin, h, z, a)

    def ffn_bwd(tag, dxo, saved, g, w_in, w_out, layer):
        xin, h, z, a = saved
        dff = w_out.shape[1]
        da = _mm("ffn_da_" + tag, dxo, w_out, "nt", BF, 512, dff // 2, D, scale=0.5, b_lead=layer)
        dw_out = _mm("ffn_dwo_" + tag, a, dxo, "tn", BF, dff // 2, D, 512, scale=0.5)

        def sw_bwd(rv, cv):
            gt, up, d = rv[0].astype(F32), rv[1].astype(F32), rv[2].astype(F32)
            _, vjp = jax.vjp(lambda p, q: _silu(p) * q, gt, up)
            dg, du = vjp(d)
            return [jnp.concatenate([dg, du], axis=1)], []

        (dz,) = _rowcall("swiglu_bwd_" + tag, sw_bwd, [(z, 0, dff), (z, 1, dff), (da, 0, dff)], [], [(2 * dff, BF)], [], TR)
        dh = _mm("ffn_dh_" + tag, dz, w_in, "nt", F32, 512, 512, dff, b_lead=layer)
        dw_in = _mm("ffn_dwi_" + tag, h, dz, "tn", BF, 512, dff, 512)
        dx, dg = rms_bwd("rms_bwd_" + tag, xin, g, dh, dxo)
        return dx, dg, dw_in, dw_out

    def rms_bwd(name, xin, g, dh, dres):
        def fn(rv, cv):
            _, vjp = jax.vjp(_rmsnorm, rv[0], cv[0])
            dx, dg = vjp(rv[1])
            if dres is not None:
                dx = dx + rv[2]
            return [dx], [dg]

        rows = [(xin, 0, D), (dh, 0, D)] + ([(dres, 0, D)] if dres is not None else [])
        dx, dg = _rowcall(name, fn, rows, [g.reshape(1, D)], [(D, F32)], [((1, D), F32)], TR)
        return dx, dg

    x0 = x.reshape(T, D)
    tgt = loss_target.reshape(T, D)
    mem2 = mem.reshape(bl * mem_len, D)
    memn = rms_fwd("rms_mem", mem2, mem_norm)
    kv = [_mm("kv_%d" % i, memn, gathered["mem_w_kv"], "nn", F32, 512, 512, D, b_lead=i) for i in range(2)]

    x1, sv_f10 = ffn_fwd("f1l0", x0, ffn1_norm[0], gathered["ffn1_w_in"], gathered["ffn1_w_out"], 0)
    h_m0 = rms_fwd("rms_mix0", x1, mix_norm[0])
    z_m0 = _mm("mix_in_0", h_m0, gathered["hgrn_w_in"], "nn", F32, 512, 512, D, b_lead=0)
    nc0 = seq // HG_CHUNK
    cat0, stash0 = _hgrn_fwd(z_m0, lb_logits, hgrn_gnorm, kv[0], bl, nc0)
    x2 = _mm("mix_out_0", cat0, gathered["hgrn_w_out"], "nn", F32, 512, 1024, cat0.shape[1], res=x1, b_lead=0)
    x3, sv_f20 = ffn_fwd("f2l0", x2, ffn2_norm[0], gathered["ffn2_w_in"], gathered["ffn2_w_out"], 0)
    x4, sv_f11 = ffn_fwd("f1l1", x3, ffn1_norm[1], gathered["ffn1_w_in"], gathered["ffn1_w_out"], 1)
    h_m1 = rms_fwd("rms_mix1", x4, mix_norm[1])
    z_m1 = _mm("mix_in_1", h_m1, gathered["gmlp_w_in"], "nn", F32, 512, 512, D, b_lead=0)
    nc1 = seq // GM_CHUNK
    w_s, b_s = gmlp_w_s[0], gmlp_b_s[0]
    cat1 = _gmlp_fwd(z_m1, ln_g_full, ln_b_full, w_s, b_s, kv[1], bl, nc1)
    x5 = _mm("mix_out_1", cat1, gathered["gmlp_w_out"], "nn", F32, 512, 1024, cat1.shape[1], res=x4, b_lead=0)
    x6, sv_f21 = ffn_fwd("f2l1", x5, ffn2_norm[1], gathered["ffn2_w_in"], gathered["ffn2_w_out"], 1)

    def head(rv, cv):
        def f(xx, gg):
            err = _rmsnorm(xx, gg) - rv[1]
            return 0.5 * jnp.sum(jnp.mean(err * err, axis=-1, keepdims=True), axis=0, keepdims=True)

        ls, vjp = jax.vjp(f, rv[0], cv[0])
        dx, dg = vjp(jnp.ones((1, 1), F32))
        return [dx], [dg, jnp.broadcast_to(ls, (1, 128))]

    dx6, d_final, loss_part = _rowcall("loss_head", head, [(x6, 0, D), (tgt, 0, D)], [final_norm.reshape(1, D)],
                                       [(D, F32)], [((1, D), F32), ((1, 128), F32)], TR)

    dx5, dg_f21, dwi_f21, dwo_f21 = ffn_bwd("f2l1", dx6, sv_f21, ffn2_norm[1], gathered["ffn2_w_in"], gathered["ffn2_w_out"], 1)
    dcat1 = _mm("mix_dcat_1", dx5, gathered["gmlp_w_out"], "nt", F32, 512, 512, D, b_lead=0)
    dwo_m1 = _mm("mix_dwo_1", cat1, dx5, "tn", BF, 512, D, 512)
    dz_m1, dkv1, d_lng, d_lnb, d_ws, d_bs = _gmlp_bwd(z_m1, dcat1, ln_g_full, ln_b_full, w_s, b_s, kv[1], bl, nc1)
    dh_m1 = _mm("mix_dh_1", dz_m1, gathered["gmlp_w_in"], "nt", F32, 512, 512, 1024, b_lead=0)
    dwi_m1 = _mm("mix_dwi_1", h_m1, dz_m1, "tn", BF, 512, 1024, 512)
    dx4, dg_m1 = rms_bwd("rms_bwd_mix1", x4, mix_norm[1], dh_m1, dx5)
    dx3, dg_f11, dwi_f11, dwo_f11 = ffn_bwd("f1l1", dx4, sv_f11, ffn1_norm[1], gathered["ffn1_w_in"], gathered["ffn1_w_out"], 1)

    dx2, dg_f20, dwi_f20, dwo_f20 = ffn_bwd("f2l0", dx3, sv_f20, ffn2_norm[0], gathered["ffn2_w_in"], gathered["ffn2_w_out"], 0)
    dcat0 = _mm("mix_dcat_0", dx2, gathered["hgrn_w_out"], "nt", F32, 512, 512, D, b_lead=0)
    dwo_m0 = _mm("mix_dwo_0", cat0, dx2, "tn", BF, 512, D, 512)
    dz_m0, dkv0, d_lb, d_gn = _hgrn_bwd(z_m0, dcat0, stash0, lb_logits, hgrn_gnorm, kv[0], bl, nc0)
    dh_m0 = _mm("mix_dh_0", dz_m0, gathered["hgrn_w_in"], "nt", F32, 512, 512, 1024, b_lead=0)
    dwi_m0 = _mm("mix_dwi_0", h_m0, dz_m0, "tn", BF, 512, 1024, 512)
    dx1, dg_m0 = rms_bwd("rms_bwd_mix0", x1, mix_norm[0], dh_m0, dx2)
    dx0, dg_f10, dwi_f10, dwo_f10 = ffn_bwd("f1l0", dx1, sv_f10, ffn1_norm[0], gathered["ffn1_w_in"], gathered["ffn1_w_out"], 0)

    dwkv = [_mm("kv_dw_%d" % i, memn, dkv, "tn", BF, 512, 1024, 512) for i, dkv in enumerate([dkv0, dkv1])]
    dmemn = _mm("kv_dx_0", dkv0, gathered["mem_w_kv"], "nt", F32, 512, 512, 1024, b_lead=0)
    dmemn = _mm("kv_dx_1", dkv1, gathered["mem_w_kv"], "nt", F32, 512, 512, 1024, res=dmemn, b_lead=1)
    _, d_memnorm = rms_bwd("rms_bwd_mem", mem2, mem_norm, dmemn, None)

    mats = [("ffn1_w_in", [dwi_f10, dwi_f11]), ("ffn1_w_out", [dwo_f10, dwo_f11]), ("mem_w_kv", dwkv),
            ("hgrn_w_in", [dwi_m0]), ("hgrn_w_out", [dwo_m0]), ("gmlp_w_in", [dwi_m1]), ("gmlp_w_out", [dwo_m1]),
            ("ffn2_w_in", [dwi_f20, dwi_f21]), ("ffn2_w_out", [dwo_f20, dwo_f21])]
    flat, mkinds, layers = [], [], []
    for (nm, gl), k in zip(mats, kinds):
        layers.append(len(gl))
        for g in gl:
            flat.append(g[None])
            mkinds.append(k)
    sib = _rs_sibling(flat, mkinds)
    parts = []
    for i in range(len(flat)):
        own, got = sib[2 * i], sib[2 * i + 1]
        _, r, c = own.shape
        (p,) = _rowcall("chip_sum_%d" % i, lambda rv, cv: ([rv[0].astype(F32) + rv[1].astype(F32)], []),
                        [(own.reshape(r, c), 0, c), (got.reshape(r, c), 0, c)], [], [(c, BF)], [], 256)
        parts.append(p[None])
    chp = _rs_chips(parts, mkinds)
    halves = []
    for i in range(len(flat)):
        own, got = chp[2 * i], chp[2 * i + 1]
        _, r, c = own.shape
        hsum = _sum4("final_sum_%d" % i, own.reshape(r, c), got.reshape(3 * r, c), _pick(r, 256))
        halves.append(hsum[None])
    shard_grads = _share_halves(halves, kinds, layers)

    big_w = [w for (_, w, _) in big]
    big_m = [m_ffn1_w_in, m_ffn1_w_out, m_mem_w_kv, m_hgrn_w_in, m_hgrn_w_out, m_gmlp_w_in, m_gmlp_w_out, m_ffn2_w_in, m_ffn2_w_out]
    big_v = [v_ffn1_w_in, v_ffn1_w_out, v_mem_w_kv, v_hgrn_w_in, v_hgrn_w_out, v_gmlp_w_in, v_gmlp_w_out, v_ffn2_w_in, v_ffn2_w_out]
    big_out = {}
    for (nm, w, _), g, m, v in zip(big, shard_grads, big_m, big_v):
        L, r, c = w.shape
        d2, m2, v2 = _adam_call("adam_" + nm, w.reshape(L * r, c), g.reshape(L * r, c), m.reshape(L * r, c),
                                v.reshape(L * r, c), 256)
        big_out[nm] = (g, d2.reshape(w.shape), m2.reshape(w.shape), v2.reshape(w.shape))

    d_ffn1n = jnp.concatenate([dg_f10, dg_f11], axis=0)
    d_mixn = jnp.concatenate([dg_m0, dg_m1], axis=0)
    d_ffn2n = jnp.concatenate([dg_f20, dg_f21], axis=0)
    small_parts = [loss_part[:, :1], d_memnorm, d_lb, d_ffn1n, d_mixn, d_gn, d_lng, d_lnb, d_ws, d_bs, d_ffn2n, d_final]
    red_shapes = [(1,), mem_norm.shape, lb_logits.shape, ffn1_norm.shape, mix_norm.shape, hgrn_gnorm.shape, (1, ln_w), (1, ln_w),
                  gmlp_w_s.shape, gmlp_b_s.shape, ffn2_norm.shape, final_norm.shape]
    red = _small_allreduce(_pack(small_parts, _rows_needed(red_shapes)), "reduce_small")
    (loss_v, g_memn, g_lb, g_f1n, g_mixn, g_gn, g_lng_full, g_lnb_full, g_ws, g_bs, g_f2n, g_fin) = _unpack(red, red_shapes)
    lsh = gmlp_ln_g.shape[1]
    g_lng = lax.dynamic_slice(g_lng_full, (0, chip * lsh), (1, lsh))
    g_lnb = lax.dynamic_slice(g_lnb_full, (0, chip * lsh), (1, lsh))
    small_w = [mem_norm, lb_logits, ffn1_norm, mix_norm, hgrn_gnorm, gmlp_ln_g, gmlp_ln_b, gmlp_w_s, gmlp_b_s, ffn2_norm, final_norm]
    small_g = [g_memn, g_lb, g_f1n, g_mixn, g_gn, g_lng, g_lnb, g_ws, g_bs, g_f2n, g_fin]
    small_m = [m_mem_norm, m_lb_logits, m_ffn1_norm, m_mix_norm, m_hgrn_gnorm, m_gmlp_ln_g, m_gmlp_ln_b, m_gmlp_w_s, m_gmlp_b_s, m_ffn2_norm, m_final_norm]
    small_v = [v_mem_norm, v_lb_logits, v_ffn1_norm, v_mix_norm, v_hgrn_gnorm, v_gmlp_ln_g, v_gmlp_ln_b, v_gmlp_w_s, v_gmlp_b_s, v_ffn2_norm, v_final_norm]
    sshapes = [w.shape for w in small_w]
    nrow = _rows_needed(sshapes)
    d_p, m_p, v_p = _adam_call("adam_small", _pack(small_w, nrow), _pack(small_g, nrow), _pack(small_m, nrow), _pack(small_v, nrow), nrow)
    s_delta, s_m, s_v = _unpack(d_p, sshapes), _unpack(m_p, sshapes), _unpack(v_p, sshapes)
    small_names = ["mem_norm", "lb_logits", "ffn1_norm", "mix_norm", "hgrn_gnorm", "gmlp_ln_g", "gmlp_ln_b", "gmlp_w_s", "gmlp_b_s", "ffn2_norm", "final_norm"]
    small_out = {nm: (g.reshape(w.shape), d, m, v) for nm, w, g, d, m, v in zip(small_names, small_w, small_g, s_delta, s_m, s_v)}

    order = ["mem_norm", "lb_logits", "ffn1_norm", "ffn1_w_in", "ffn1_w_out", "mix_norm", "mem_w_kv", "hgrn_w_in", "hgrn_gnorm",
             "hgrn_w_out", "gmlp_w_in", "gmlp_ln_g", "gmlp_ln_b", "gmlp_w_s", "gmlp_b_s", "gmlp_w_out", "ffn2_norm", "ffn2_w_in",
             "ffn2_w_out", "final_norm"]
    allo = {**big_out, **small_out}
    grad_x = dx0.reshape(x.shape)
    return (loss_v.reshape(()), grad_x, *[allo[n][0] for n in order], *[allo[n][1] for n in order],
            *[allo[n][2] for n in order], *[allo[n][3] for n in order])


def _sum4(name, own, got3, tr):
    r, c = own.shape
    nblk = r // tr

    def kern(o_ref, a_ref, b_ref, c_ref, out_ref):
        out_ref[...] = ((o_ref[...].astype(F32) + a_ref[...].astype(F32)) + b_ref[...].astype(F32)) + c_ref[...].astype(F32)

    specs = [pl.BlockSpec((tr, c), lambda i: (i, 0))]
    specs += [pl.BlockSpec((tr, c), functools.partial(lambda i, p: (p * nblk + i, 0), p=p)) for p in range(3)]
    return pl.pallas_call(
        kern, grid=(nblk,), in_specs=specs, out_specs=pl.BlockSpec((tr, c), lambda i: (i, 0)),
        out_shape=jax.ShapeDtypeStruct((r, c), F32), name=name,
        compiler_params=pltpu.CompilerParams(dimension_semantics=("arbitrary",), vmem_limit_bytes=32 << 20),
    )(own, got3, got3, got3)
```

```python
import functools

import jax
import jax.numpy as jnp
from jax import lax
from jax.experimental import pallas as pl
from jax.experimental.pallas import tpu as pltpu

BF = jnp.bfloat16
F32 = jnp.float32
MESH = pl.DeviceIdType.MESH

EPS = 1e-6
D_MODEL = 1024
HG_HEADS = 8
HG_DIM = 128
HG_CHUNK = 64
GM_CHUNK = 128
GM_GROUPS = 8
GM_GROUP_DIM = 256
XA_HEADS = 4
XA_DIM = 256
ADAM_LR = 0.001
ADAM_B1 = 0.9
ADAM_B2 = 0.999
ADAM_EPS = 1e-08
ADAM_WD = 0.01
ADAM_STEP = 10

VMEM_CAP_BYTES = 60 * 1024 * 1024
LANES = 1024


def _pick(n, cap, mult=16):
    if n <= cap:
        return n
    for d in range(cap - cap % mult, 0, -mult):
        if n % d == 0:
            return d
    raise ValueError((n, cap, mult))


def _dg(a, b, ca, cb):
    return lax.dot_general(a.astype(BF), b.astype(BF), (((ca,), (cb,)), ((), ())), preferred_element_type=F32)


@jax.custom_vjp
def dot_nn(a, b):
    return _dg(a, b, 1, 0)


def _nn_fwd(a, b):
    return _dg(a, b, 1, 0), (a, b)


def _nn_bwd(r, g):
    a, b = r
    return _dg(g, b, 1, 1), _dg(a, g, 0, 0)


dot_nn.defvjp(_nn_fwd, _nn_bwd)


@jax.custom_vjp
def dot_nt(a, b):
    return _dg(a, b, 1, 1)


def _nt_fwd(a, b):
    return _dg(a, b, 1, 1), (a, b)


def _nt_bwd(r, g):
    a, b = r
    return _dg(g, b, 1, 0), _dg(g, a, 0, 0)


dot_nt.defvjp(_nt_fwd, _nt_bwd)


@jax.custom_vjp
def dot_tn(a, b):
    return _dg(a, b, 0, 0)


def _tn_fwd(a, b):
    return _dg(a, b, 0, 0), (a, b)


def _tn_bwd(r, g):
    a, b = r
    return _dg(b, g, 1, 1), _dg(a, g, 1, 0)


dot_tn.defvjp(_tn_fwd, _tn_bwd)


def _rmsnorm(x, g):
    return x * lax.rsqrt(jnp.mean(x * x, axis=-1, keepdims=True) + EPS) * g


def _silu(x):
    return x * jax.nn.sigmoid(x)


def _gelu(x):
    return 0.5 * x * (1.0 + lax.erf(x * (0.5 ** 0.5)))


def _softmax_last(s):
    m = lax.stop_gradient(jnp.max(s, axis=-1, keepdims=True))
    e = jnp.exp(s - m)
    return e / jnp.sum(e, axis=-1, keepdims=True)


def _tril(n):
    r = lax.broadcasted_iota(jnp.int32, (n, n), 0)
    c = lax.broadcasted_iota(jnp.int32, (n, n), 1)
    return r >= c


def _cumsum_rows(l):
    n = l.shape[0]
    return lax.dot_general(_tril(n).astype(F32), l, (((1,), (0,)), ((), ())),
                           precision=lax.Precision.HIGHEST, preferred_element_type=F32)


def _attention(zx, mk, mv):
    s = dot_nt(zx, mk) * (XA_DIM ** -0.5)
    return dot_nn(_softmax_last(s), mv)


def _hgrn_head(zq, zf, zi, zg, l0, l1, l2, gn, S):
    m = lax.stop_gradient(jnp.maximum(jnp.maximum(l0, l1), l2))
    e0 = jnp.exp(l0 - m)
    lb = e0 / (e0 + jnp.exp(l1 - m) + jnp.exp(l2 - m))
    q = _silu(zq)
    f = lb + (1.0 - lb) * jax.nn.sigmoid(zf)
    k = 1.0 - f
    b = _cumsum_rows(jnp.log(f))
    b_last = b[HG_CHUNK - 1:HG_CHUNK, :]
    q_dec = q * jnp.exp(b)
    k_inv = k * jnp.exp(-b)
    a = jnp.where(_tril(HG_CHUNK), dot_nt(q_dec, k_inv), 0.0)
    o = dot_nn(a, zi) + dot_nn(q_dec, S)
    S_new = jnp.exp(b_last).reshape(HG_DIM, 1) * S + dot_tn(k * jnp.exp(b_last - b), zi)
    o = _rmsnorm(o, gn) * _silu(zg)
    return o, S_new


def _hgrn_block(zq, zf, zi, zg, zx, l0, l1, l2, gn, mk, mv, S):
    outs, s_new = [], []
    for h in range(HG_HEADS):
        o, sn = _hgrn_head(zq[h], zf[h], zi[h], zg[h], l0[h], l1[h], l2[h], gn, S[h])
        outs.append(o)
        s_new.append(sn)
    for a in range(XA_HEADS):
        outs.append(_attention(zx[a], mk[a], mv[a]))
    return outs, s_new


def _gmlp_block(zu, zv, zx, lng, lnb, ws, bs, mk, mv):
    gv = [_gelu(v) for v in zv]
    width = GM_GROUPS * GM_GROUP_DIM
    mu = sum(jnp.sum(g, axis=-1, keepdims=True) for g in gv) / width
    xc = [g - mu for g in gv]
    var = sum(jnp.sum(c * c, axis=-1, keepdims=True) for c in xc) / width
    r = lax.rsqrt(var + EPS)
    outs = []
    for g in range(GM_GROUPS):
        v = xc[g] * r * lng[g] + lnb[g]
        w = jnp.where(_tril(GM_CHUNK), ws[g], 0.0)
        mixed = dot_nn(w, v) + bs[g].reshape(GM_CHUNK, 1)
        outs.append(_gelu(zu[g]) * mixed)
    for a in range(XA_HEADS):
        outs.append(_attention(zx[a], mk[a], mv[a]))
    return outs


def _rowcall(name, fn, rows, consts, row_outs, acc_outs, tr):
    nrows = rows[0][0].shape[0]
    tr = _pick(nrows, tr)
    n_r, n_c, n_ro, n_ao = len(rows), len(consts), len(row_outs), len(acc_outs)

    def kern(*refs):
        rv = [r[...] for r in refs[:n_r]]
        cv = [r[...] for r in refs[n_r:n_r + n_c]]
        ro_refs = refs[n_r + n_c:n_r + n_c + n_ro]
        ao_refs = refs[n_r + n_c + n_ro:]
        ro, ao = fn(rv, cv)
        for ref, v in zip(ro_refs, ro):
            ref[...] = v.astype(ref.dtype)
        if n_ao:
            @pl.when(pl.program_id(0) == 0)
            def _():
                for ref in ao_refs:
                    ref[...] = jnp.zeros(ref.shape, ref.dtype)

            for ref, v in zip(ao_refs, ao):
                ref[...] += v.astype(ref.dtype)

    in_specs = [pl.BlockSpec((tr, w), functools.partial(lambda i, cb: (i, cb), cb=cb)) for (_, cb, w) in rows]
    in_specs += [pl.BlockSpec(c.shape, lambda i: (0, 0)) for c in consts]
    out_specs = [pl.BlockSpec((tr, w), lambda i: (i, 0)) for (w, _) in row_outs]
    out_specs += [pl.BlockSpec(s, lambda i: (0, 0)) for (s, _) in acc_outs]
    out_shape = [jax.ShapeDtypeStruct((nrows, w), dt) for (w, dt) in row_outs]
    out_shape += [jax.ShapeDtypeStruct(s, dt) for (s, dt) in acc_outs]
    est = sum(tr * w * a.dtype.itemsize for (a, _, w) in rows) + sum(tr * w * jnp.dtype(dt).itemsize for (w, dt) in row_outs)
    est += sum(c.size * c.dtype.itemsize for c in consts)
    outs = pl.pallas_call(
        kern, grid=(nrows // tr,), in_specs=in_specs, out_specs=out_specs, out_shape=out_shape, name=name,
        compiler_params=pltpu.CompilerParams(dimension_semantics=("arbitrary",),
                                             vmem_limit_bytes=int(min(VMEM_CAP_BYTES, 6 * est + (16 << 20)))),
    )(*[a for (a, _, _) in rows], *consts)
    return outs


def _mm(name, a, b, mode, out_dtype, tm, tn, tk, scale=1.0, res=None, a_lead=None, b_lead=None):
    ash = a.shape[-2:]
    bsh = b.shape[-2:]
    if mode == "nn":
        (M, K), (K2, N) = ash, bsh
    elif mode == "nt":
        (M, K), (N, K2) = ash, bsh
    else:
        (K, M), (K2, N) = ash, bsh
    assert K == K2, (name, a.shape, b.shape)
    tm, tn, tk = min(tm, M), min(tn, N), min(tk, K)
    assert M % tm == 0 and N % tn == 0 and K % tk == 0, (name, M, N, K, tm, tn, tk)
    nk = K // tk
    dims = {"nn": (1, 0), "nt": (1, 1), "tn": (0, 0)}[mode]

    def lead(spec_shape, index_fn, lead_idx):
        if lead_idx is None:
            return pl.BlockSpec(spec_shape, index_fn)
        return pl.BlockSpec((None,) + spec_shape, lambda i, j, k: (lead_idx,) + index_fn(i, j, k))

    if mode == "tn":
        a_spec = lead((tk, tm), lambda i, j, k: (k, i), a_lead)
    else:
        a_spec = lead((tm, tk), lambda i, j, k: (i, k), a_lead)
    if mode == "nt":
        b_spec = lead((tn, tk), lambda i, j, k: (j, k), b_lead)
    else:
        b_spec = lead((tk, tn), lambda i, j, k: (k, j), b_lead)
    o_spec = pl.BlockSpec((tm, tn), lambda i, j, k: (i, j))
    has_res = res is not None

    def kern(*refs):
        a_ref, b_ref = refs[0], refs[1]
        res_ref = refs[2] if has_res else None
        o_ref = refs[3] if has_res else refs[2]
        acc_ref = refs[-1] if nk > 1 else None
        p = lax.dot_general(a_ref[...].astype(BF), b_ref[...].astype(BF), (((dims[0],), (dims[1],)), ((), ())),
                            preferred_element_type=F32)

        def finish(v):
            if scale != 1.0:
                v = v * scale
            if has_res:
                v = res_ref[...] + v
            o_ref[...] = v.astype(o_ref.dtype)

        if nk == 1:
            finish(p)
        else:
            k = pl.program_id(2)

            @pl.when(k == 0)
            def _():
                acc_ref[...] = p

            @pl.when(k > 0)
            def _():
                acc_ref[...] += p

            @pl.when(k == nk - 1)
            def _():
                finish(acc_ref[...])

    ins = [a, b] + ([res] if has_res else [])
    in_specs = [a_spec, b_spec] + ([o_spec] if has_res else [])
    est = tm * tk * a.dtype.itemsize + tk * tn * b.dtype.itemsize + tm * tn * (jnp.dtype(out_dtype).itemsize + 8)
    return pl.pallas_call(
        kern, grid=(M // tm, N // tn, nk), in_specs=in_specs, out_specs=o_spec,
        out_shape=jax.ShapeDtypeStruct((M, N), out_dtype),
        scratch_shapes=[pltpu.VMEM((tm, tn), F32)] if nk > 1 else [],
        name=name,
        compiler_params=pltpu.CompilerParams(dimension_semantics=("parallel", "parallel", "arbitrary"),
                                             vmem_limit_bytes=int(min(VMEM_CAP_BYTES, 3 * est + (16 << 20)))),
    )(*ins)


def _mm_tn_pair(name, a, b, kind, c_arr, tq, tk, scale=1.0):
    T, M = a.shape
    _, N = b.shape
    tk = min(tk, T)
    assert T % tk == 0
    nk = T // tk
    if kind == "col":
        hm = M // 2
        assert N % tq == 0
        nq = N // tq
        tile = (hm, tq)
        a_spec = pl.BlockSpec((tk, hm), lambda h, q, k, c: (k, jnp.bitwise_xor(h, 1 - c[0])))
        b_spec = pl.BlockSpec((tk, tq), lambda h, q, k, c: (k, q))
        o_spec = pl.BlockSpec(tile, lambda h, q, k, c: (0, q * h))
        out_sd = (hm, N)
    else:
        hn = N // 2
        assert M % tq == 0
        nq = M // tq
        tile = (tq, hn)
        a_spec = pl.BlockSpec((tk, tq), lambda h, q, k, c: (k, q))
        b_spec = pl.BlockSpec((tk, hn), lambda h, q, k, c: (k, jnp.bitwise_xor(h, 1 - c[0])))
        o_spec = pl.BlockSpec(tile, lambda h, q, k, c: (q * h, 0))
        out_sd = (M, hn)

    def kern(c_ref, a_ref, b_ref, o_ref, acc, stage, recv, ssem, rsem):
        h, q, k = pl.program_id(0), pl.program_id(1), pl.program_id(2)
        x, y, c, _ = _place()
        p = lax.dot_general(a_ref[...].astype(BF), b_ref[...].astype(BF), (((0,), (0,)), ((), ())), preferred_element_type=F32)

        @pl.when(k == 0)
        def _():
            acc[...] = p

        @pl.when(k > 0)
        def _():
            acc[...] += p

        def send(slot, qq):
            return pltpu.make_async_remote_copy(src_ref=stage.at[slot], dst_ref=recv.at[qq], send_sem=ssem.at[slot],
                                                recv_sem=rsem.at[qq], device_id=(x, y, 1 - c), device_id_type=MESH)

        last = k == nk - 1

        @pl.when(jnp.logical_and(last, h == 0))
        def _():
            slot = q % 2

            @pl.when(q >= 2)
            def _():
                send(slot, q).wait_send()

            stage[slot] = (acc[...] * scale).astype(BF)
            send(slot, q).start()

        @pl.when(jnp.logical_and(last, h == 1))
        def _():
            @pl.when(q == 0)
            def _():
                for s in range(min(nq, 2)):
                    send(s, 0).wait_send()

            send(0, q).wait_recv()
            o_ref[...] = (acc[...] * scale + recv[q].astype(F32)).astype(o_ref.dtype)

    tb = tile[0] * tile[1]
    est = tb * (4 + 2 * 2 + nq * 2 + 2 * 2) + 2 * tk * (a_spec.block_shape[1] + b_spec.block_shape[1]) * 2 * 2
    return pl.pallas_call(
        kern,
        grid_spec=pltpu.PrefetchScalarGridSpec(
            num_scalar_prefetch=1, grid=(2, nq, nk), in_specs=[a_spec, b_spec], out_specs=o_spec,
            scratch_shapes=[pltpu.VMEM(tile, F32), pltpu.VMEM((2,) + tile, BF), pltpu.VMEM((nq,) + tile, BF),
                            pltpu.SemaphoreType.DMA((2,)), pltpu.SemaphoreType.DMA((nq,))]),
        out_shape=jax.ShapeDtypeStruct(out_sd, BF), name=name,
        compiler_params=pltpu.CompilerParams(dimension_semantics=("arbitrary", "arbitrary", "arbitrary"),
                                             vmem_limit_bytes=int(min(VMEM_CAP_BYTES, est + (12 << 20)))),
    )(c_arr, a, b)


def _hgrn_pieces(z_ref):
    W = HG_HEADS * HG_DIM
    zq = [z_ref[:, h * HG_DIM:(h + 1) * HG_DIM] for h in range(HG_HEADS)]
    zf = [z_ref[:, W + h * HG_DIM:W + (h + 1) * HG_DIM] for h in range(HG_HEADS)]
    zi = [z_ref[:, 2 * W + h * HG_DIM:2 * W + (h + 1) * HG_DIM] for h in range(HG_HEADS)]
    zg = [z_ref[:, 3 * W + h * HG_DIM:3 * W + (h + 1) * HG_DIM] for h in range(HG_HEADS)]
    zx = [z_ref[:, 4 * W + a * XA_DIM:4 * W + (a + 1) * XA_DIM] for a in range(XA_HEADS)]
    return zq, zf, zi, zg, zx


def _kv_pieces(kv_ref):
    W = XA_HEADS * XA_DIM
    mk = [kv_ref[:, a * XA_DIM:(a + 1) * XA_DIM] for a in range(XA_HEADS)]
    mv = [kv_ref[:, W + a * XA_DIM:W + (a + 1) * XA_DIM] for a in range(XA_HEADS)]
    return mk, mv


def _lb_pieces(lb_ref):
    return [[lb_ref[r:r + 1, h * HG_DIM:(h + 1) * HG_DIM] for h in range(HG_HEADS)] for r in range(3)]


def _hgrn_fwd(z, lb_logits, gnorm, kv, bl, nc):
    T, zw = z.shape
    mem_len = kv.shape[0] // bl
    cat_w = HG_HEADS * HG_DIM + XA_HEADS * XA_DIM

    def kern(z_ref, lb_ref, gn_ref, kv_ref, cat_ref, st_ref, s_scr):
        @pl.when(pl.program_id(1) == 0)
        def _():
            s_scr[...] = jnp.zeros(s_scr.shape, F32)

        st_ref[...] = s_scr[...]
        zq, zf, zi, zg, zx = _hgrn_pieces(z_ref)
        mk, mv = _kv_pieces(kv_ref)
        l0, l1, l2 = _lb_pieces(lb_ref)
        S = [s_scr[h] for h in range(HG_HEADS)]
        outs, s_new = _hgrn_block(zq, zf, zi, zg, zx, l0, l1, l2, gn_ref[...], mk, mv, S)
        for h in range(HG_HEADS):
            cat_ref[:, h * HG_DIM:(h + 1) * HG_DIM] = outs[h].astype(cat_ref.dtype)
            s_scr[h] = s_new[h]
        base = HG_HEADS * HG_DIM
        for a in range(XA_HEADS):
            cat_ref[:, base + a * XA_DIM:base + (a + 1) * XA_DIM] = outs[HG_HEADS + a].astype(cat_ref.dtype)

    return pl.pallas_call(
        kern, grid=(bl, nc),
        in_specs=[pl.BlockSpec((HG_CHUNK, zw), lambda b, n: (b * nc + n, 0)),
                  pl.BlockSpec(lb_logits.shape, lambda b, n: (0, 0)),
                  pl.BlockSpec(gnorm.shape, lambda b, n: (0, 0)),
                  pl.BlockSpec((mem_len, kv.shape[1]), lambda b, n: (b, 0))],
        out_specs=[pl.BlockSpec((HG_CHUNK, cat_w), lambda b, n: (b * nc + n, 0)),
                   pl.BlockSpec((None, HG_HEADS, HG_DIM, HG_DIM), lambda b, n: (b * nc + n, 0, 0, 0))],
        out_shape=[jax.ShapeDtypeStruct((T, cat_w), BF),
                   jax.ShapeDtypeStruct((bl * nc, HG_HEADS, HG_DIM, HG_DIM), F32)],
        scratch_shapes=[pltpu.VMEM((HG_HEADS, HG_DIM, HG_DIM), F32)],
        name="hgrn_fwd",
        compiler_params=pltpu.CompilerParams(dimension_semantics=("arbitrary", "arbitrary"), vmem_limit_bytes=48 << 20),
    )(z, lb_logits, gnorm, kv)


def _hgrn_bwd(z, dcat, stash, lb_logits, gnorm, kv, bl, nc):
    T, zw = z.shape
    mem_len = kv.shape[0] // bl
    cat_w = dcat.shape[1]

    def kern(z_ref, dc_ref, st_ref, lb_ref, gn_ref, kv_ref, dz_ref, dkv_ref, dlb_ref, dgn_ref, ds_scr):
        first = jnp.logical_and(pl.program_id(0) == 0, pl.program_id(1) == 0)

        @pl.when(pl.program_id(1) == 0)
        def _():
            ds_scr[...] = jnp.zeros(ds_scr.shape, F32)
            dkv_ref[...] = jnp.zeros(dkv_ref.shape, F32)

        @pl.when(first)
        def _():
            dlb_ref[...] = jnp.zeros(dlb_ref.shape, F32)
            dgn_ref[...] = jnp.zeros(dgn_ref.shape, F32)

        zq, zf, zi, zg, zx = _hgrn_pieces(z_ref)
        mk, mv = _kv_pieces(kv_ref)
        l0, l1, l2 = _lb_pieces(lb_ref)
        S = [st_ref[h] for h in range(HG_HEADS)]
        _, vjp = jax.vjp(_hgrn_block, zq, zf, zi, zg, zx, l0, l1, l2, gn_ref[...], mk, mv, S)
        d_outs = [dc_ref[:, h * HG_DIM:(h + 1) * HG_DIM] for h in range(HG_HEADS)]
        base = HG_HEADS * HG_DIM
        d_outs += [dc_ref[:, base + a * XA_DIM:base + (a + 1) * XA_DIM] for a in range(XA_HEADS)]
        d_s = [ds_scr[h] for h in range(HG_HEADS)]
        dzq, dzf, dzi, dzg, dzx, dl0, dl1, dl2, dgn, dmk, dmv, dS = vjp((d_outs, d_s))
        W = HG_HEADS * HG_DIM
        for h in range(HG_HEADS):
            sl = slice(h * HG_DIM, (h + 1) * HG_DIM)
            dz_ref[:, sl] = dzq[h].astype(dz_ref.dtype)
            dz_ref[:, W + h * HG_DIM:W + (h + 1) * HG_DIM] = dzf[h].astype(dz_ref.dtype)
            dz_ref[:, 2 * W + h * HG_DIM:2 * W + (h + 1) * HG_DIM] = dzi[h].astype(dz_ref.dtype)
            dz_ref[:, 3 * W + h * HG_DIM:3 * W + (h + 1) * HG_DIM] = dzg[h].astype(dz_ref.dtype)
            ds_scr[h] = dS[h]
            dlb_ref[0:1, sl] += dl0[h]
            dlb_ref[1:2, sl] += dl1[h]
            dlb_ref[2:3, sl] += dl2[h]
        dgn_ref[...] += dgn
        KW = XA_HEADS * XA_DIM
        for a in range(XA_HEADS):
            dz_ref[:, 4 * W + a * XA_DIM:4 * W + (a + 1) * XA_DIM] = dzx[a].astype(dz_ref.dtype)
            dkv_ref[:, a * XA_DIM:(a + 1) * XA_DIM] += dmk[a]
            dkv_ref[:, KW + a * XA_DIM:KW + (a + 1) * XA_DIM] += dmv[a]

    rev = lambda b, n: (b * nc + (nc - 1 - n), 0)
    return pl.pallas_call(
        kern, grid=(bl, nc),
        in_specs=[pl.BlockSpec((HG_CHUNK, zw), rev),
                  pl.BlockSpec((HG_CHUNK, cat_w), rev),
                  pl.BlockSpec((None, HG_HEADS, HG_DIM, HG_DIM), lambda b, n: (b * nc + (nc - 1 - n), 0, 0, 0)),
                  pl.BlockSpec(lb_logits.shape, lambda b, n: (0, 0)),
                  pl.BlockSpec(gnorm.shape, lambda b, n: (0, 0)),
                  pl.BlockSpec((mem_len, kv.shape[1]), lambda b, n: (b, 0))],
        out_specs=[pl.BlockSpec((HG_CHUNK, zw), rev),
                   pl.BlockSpec((mem_len, kv.shape[1]), lambda b, n: (b, 0)),
                   pl.BlockSpec(lb_logits.shape, lambda b, n: (0, 0)),
                   pl.BlockSpec(gnorm.shape, lambda b, n: (0, 0))],
        out_shape=[jax.ShapeDtypeStruct((T, zw), BF), jax.ShapeDtypeStruct(kv.shape, F32),
                   jax.ShapeDtypeStruct(lb_logits.shape, F32), jax.ShapeDtypeStruct(gnorm.shape, F32)],
        scratch_shapes=[pltpu.VMEM((HG_HEADS, HG_DIM, HG_DIM), F32)],
        name="hgrn_bwd",
        compiler_params=pltpu.CompilerParams(dimension_semantics=("arbitrary", "arbitrary"), vmem_limit_bytes=56 << 20),
    )(z, dcat, stash, lb_logits, gnorm, kv)


def _gmlp_pieces(z_ref):
    W = GM_GROUPS * GM_GROUP_DIM
    zu = [z_ref[:, g * GM_GROUP_DIM:(g + 1) * GM_GROUP_DIM] for g in range(GM_GROUPS)]
    zv = [z_ref[:, W + g * GM_GROUP_DIM:W + (g + 1) * GM_GROUP_DIM] for g in range(GM_GROUPS)]
    zx = [z_ref[:, 2 * W + a * XA_DIM:2 * W + (a + 1) * XA_DIM] for a in range(XA_HEADS)]
    return zu, zv, zx


def _gmlp_params(lng_ref, lnb_ref, ws_ref, bs_ref):
    lng = [lng_ref[:, g * GM_GROUP_DIM:(g + 1) * GM_GROUP_DIM] for g in range(GM_GROUPS)]
    lnb = [lnb_ref[:, g * GM_GROUP_DIM:(g + 1) * GM_GROUP_DIM] for g in range(GM_GROUPS)]
    ws = [ws_ref[g] for g in range(GM_GROUPS)]
    bs = [bs_ref[g:g + 1, :] for g in range(GM_GROUPS)]
    return lng, lnb, ws, bs


def _gmlp_fwd(z, ln_g, ln_b, w_s, b_s, kv, bl, nc):
    T, zw = z.shape
    mem_len = kv.shape[0] // bl
    cat_w = GM_GROUPS * GM_GROUP_DIM + XA_HEADS * XA_DIM

    def kern(z_ref, lng_ref, lnb_ref, ws_ref, bs_ref, kv_ref, cat_ref):
        zu, zv, zx = _gmlp_pieces(z_ref)
        lng, lnb, ws, bs = _gmlp_params(lng_ref, lnb_ref, ws_ref, bs_ref)
        mk, mv = _kv_pieces(kv_ref)
        outs = _gmlp_block(zu, zv, zx, lng, lnb, ws, bs, mk, mv)
        for g in range(GM_GROUPS):
            cat_ref[:, g * GM_GROUP_DIM:(g + 1) * GM_GROUP_DIM] = outs[g].astype(cat_ref.dtype)
        base = GM_GROUPS * GM_GROUP_DIM
        for a in range(XA_HEADS):
            cat_ref[:, base + a * XA_DIM:base + (a + 1) * XA_DIM] = outs[GM_GROUPS + a].astype(cat_ref.dtype)

    full2 = lambda b, n: (0, 0)
    return pl.pallas_call(
        kern, grid=(bl, nc),
        in_specs=[pl.BlockSpec((GM_CHUNK, zw), lambda b, n: (b * nc + n, 0)),
                  pl.BlockSpec(ln_g.shape, full2), pl.BlockSpec(ln_b.shape, full2),
                  pl.BlockSpec(w_s.shape, lambda b, n: (0, 0, 0)), pl.BlockSpec(b_s.shape, full2),
                  pl.BlockSpec((mem_len, kv.shape[1]), lambda b, n: (b, 0))],
        out_specs=pl.BlockSpec((GM_CHUNK, cat_w), lambda b, n: (b * nc + n, 0)),
        out_shape=jax.ShapeDtypeStruct((T, cat_w), BF),
        name="gmlp_fwd",
        compiler_params=pltpu.CompilerParams(dimension_semantics=("arbitrary", "arbitrary"), vmem_limit_bytes=48 << 20),
    )(z, ln_g, ln_b, w_s, b_s, kv)


def _gmlp_bwd(z, dcat, ln_g, ln_b, w_s, b_s, kv, bl, nc):
    T, zw = z.shape
    mem_len = kv.shape[0] // bl
    cat_w = dcat.shape[1]

    def kern(z_ref, dc_ref, lng_ref, lnb_ref, ws_ref, bs_ref, kv_ref,
             dz_ref, dkv_ref, dlng_ref, dlnb_ref, dws_ref, dbs_ref):
        first = jnp.logical_and(pl.program_id(0) == 0, pl.program_id(1) == 0)

        @pl.when(pl.program_id(1) == 0)
        def _():
            dkv_ref[...] = jnp.zeros(dkv_ref.shape, F32)

        @pl.when(first)
        def _():
            dlng_ref[...] = jnp.zeros(dlng_ref.shape, F32)
            dlnb_ref[...] = jnp.zeros(dlnb_ref.shape, F32)
            dws_ref[...] = jnp.zeros(dws_ref.shape, F32)
            dbs_ref[...] = jnp.zeros(dbs_ref.shape, F32)

        zu, zv, zx = _gmlp_pieces(z_ref)
        lng, lnb, ws, bs = _gmlp_params(lng_ref, lnb_ref, ws_ref, bs_ref)
        mk, mv = _kv_pieces(kv_ref)
        _, vjp = jax.vjp(_gmlp_block, zu, zv, zx, lng, lnb, ws, bs, mk, mv)
        d_outs = [dc_ref[:, g * GM_GROUP_DIM:(g + 1) * GM_GROUP_DIM] for g in range(GM_GROUPS)]
        base = GM_GROUPS * GM_GROUP_DIM
        d_outs += [dc_ref[:, base + a * XA_DIM:base + (a + 1) * XA_DIM] for a in range(XA_HEADS)]
        dzu, dzv, dzx, dlng, dlnb, dws, dbs, dmk, dmv = vjp(d_outs)
        W = GM_GROUPS * GM_GROUP_DIM
        for g in range(GM_GROUPS):
            sl = slice(g * GM_GROUP_DIM, (g + 1) * GM_GROUP_DIM)
            dz_ref[:, sl] = dzu[g].astype(dz_ref.dtype)
            dz_ref[:, W + g * GM_GROUP_DIM:W + (g + 1) * GM_GROUP_DIM] = dzv[g].astype(dz_ref.dtype)
            dlng_ref[:, sl] += dlng[g]
            dlnb_ref[:, sl] += dlnb[g]
            dws_ref[g] += dws[g]
            dbs_ref[g:g + 1, :] += dbs[g]
        KW = XA_HEADS * XA_DIM
        for a in range(XA_HEADS):
            dz_ref[:, 2 * W + a * XA_DIM:2 * W + (a + 1) * XA_DIM] = dzx[a].astype(dz_ref.dtype)
            dkv_ref[:, a * XA_DIM:(a + 1) * XA_DIM] += dmk[a]
            dkv_ref[:, KW + a * XA_DIM:KW + (a + 1) * XA_DIM] += dmv[a]

    full2 = lambda b, n: (0, 0)
    full3 = lambda b, n: (0, 0, 0)
    blk = lambda b, n: (b * nc + n, 0)
    return pl.pallas_call(
        kern, grid=(bl, nc),
        in_specs=[pl.BlockSpec((GM_CHUNK, zw), blk), pl.BlockSpec((GM_CHUNK, cat_w), blk),
                  pl.BlockSpec(ln_g.shape, full2), pl.BlockSpec(ln_b.shape, full2),
                  pl.BlockSpec(w_s.shape, full3), pl.BlockSpec(b_s.shape, full2),
                  pl.BlockSpec((mem_len, kv.shape[1]), lambda b, n: (b, 0))],
        out_specs=[pl.BlockSpec((GM_CHUNK, zw), blk),
                   pl.BlockSpec((mem_len, kv.shape[1]), lambda b, n: (b, 0)),
                   pl.BlockSpec(ln_g.shape, full2), pl.BlockSpec(ln_b.shape, full2),
                   pl.BlockSpec(w_s.shape, full3), pl.BlockSpec(b_s.shape, full2)],
        out_shape=[jax.ShapeDtypeStruct((T, zw), BF), jax.ShapeDtypeStruct(kv.shape, F32),
                   jax.ShapeDtypeStruct(ln_g.shape, F32), jax.ShapeDtypeStruct(ln_b.shape, F32),
                   jax.ShapeDtypeStruct(w_s.shape, F32), jax.ShapeDtypeStruct(b_s.shape, F32)],
        name="gmlp_bwd",
        compiler_params=pltpu.CompilerParams(dimension_semantics=("arbitrary", "arbitrary"), vmem_limit_bytes=56 << 20),
    )(z, dcat, ln_g, ln_b, w_s, b_s, kv)


def _place():
    x, y, c = lax.axis_index("x"), lax.axis_index("y"), lax.axis_index("c")
    chips = [(1 - x, y), (x, 1 - y), (1 - x, 1 - y)]
    return x, y, c, chips


def _half(ref, kind, e):
    if kind == "col":
        n = ref.shape[1] // 2
        return ref.at[:, pl.ds(pl.multiple_of(e * n, n), n), :]
    n = ref.shape[2] // 2
    return ref.at[:, :, pl.ds(pl.multiple_of(e * n, n), n)]


def _slot(ref, kind, j, n):
    if kind == "col":
        return ref.at[:, :, pl.ds(pl.multiple_of(j * n, n), n)]
    return ref.at[:, pl.ds(pl.multiple_of(j * n, n), n), :]


def _allgather(shards, kinds):
    nt = len(shards)
    out_shape = []
    for s, k in zip(shards, kinds):
        L, r, c = s.shape
        out_shape.append(jax.ShapeDtypeStruct((L, r, 4 * c) if k == "col" else (L, 4 * r, c), s.dtype))

    def body(*refs):
        sh, full = refs[:nt], refs[nt:2 * nt]
        loc, s_ici, r_ici, s_d2d, r_d2d = refs[2 * nt:]
        x, y, c, chips = _place()
        own = 2 * x + y
        started = []
        for t in range(nt):
            k = kinds[t]
            n = sh[t].shape[2] if k == "col" else sh[t].shape[1]
            mine = pltpu.make_async_copy(sh[t], _slot(full[t], k, own, n), loc.at[t])
            mine.start()
            started.append(mine)
        ici = []
        for t in range(nt):
            k = kinds[t]
            n = sh[t].shape[2] if k == "col" else sh[t].shape[1]
            for p, (px, py) in enumerate(chips):
                cp = pltpu.make_async_remote_copy(
                    src_ref=_half(sh[t], k, c), dst_ref=_half(_slot(full[t], k, own, n), k, c),
                    send_sem=s_ici.at[t, p], recv_sem=r_ici.at[t, p], device_id=(px, py, c), device_id_type=MESH)
                cp.start()
                ici.append(cp)
        d2d = []
        for t in range(nt):
            k = kinds[t]
            n = sh[t].shape[2] if k == "col" else sh[t].shape[1]
            for p, (px, py) in enumerate(chips):
                landed = _half(_slot(full[t], k, 2 * px + py, n), k, c)
                pltpu.make_async_remote_copy(
                    src_ref=landed, dst_ref=landed, send_sem=s_ici.at[t, p], recv_sem=r_ici.at[t, p],
                    device_id=(px, py, c), device_id_type=MESH).wait_recv()
                fw = pltpu.make_async_remote_copy(
                    src_ref=landed, dst_ref=landed, send_sem=s_d2d.at[t, p], recv_sem=r_d2d.at[t, p],
                    device_id=(x, y, 1 - c), device_id_type=MESH)
                fw.start()
                d2d.append(fw)
        for t in range(nt):
            k = kinds[t]
            n = sh[t].shape[2] if k == "col" else sh[t].shape[1]
            for p, (px, py) in enumerate(chips):
                other = _half(_slot(full[t], k, 2 * px + py, n), k, 1 - c)
                pltpu.make_async_remote_copy(
                    src_ref=other, dst_ref=other, send_sem=s_d2d.at[t, p], recv_sem=r_d2d.at[t, p],
                    device_id=(x, y, 1 - c), device_id_type=MESH).wait_recv()
        for cp in ici + d2d:
            cp.wait_send()
        for cp in started:
            cp.wait()

    hbm = pl.BlockSpec(memory_space=pl.ANY)
    return pl.pallas_call(
        body, out_shape=out_shape, in_specs=[hbm] * nt, out_specs=[hbm] * nt,
        scratch_shapes=[pltpu.SemaphoreType.DMA((nt,)), pltpu.SemaphoreType.DMA((nt, 3)), pltpu.SemaphoreType.DMA((nt, 3)),
                        pltpu.SemaphoreType.DMA((nt, 3)), pltpu.SemaphoreType.DMA((nt, 3))],
        name="allgather_weights",
    )(*shards)


def _slot2(ref, kind, j, n):
    if kind == "col":
        return ref.at[:, pl.ds(pl.multiple_of(j * n, n), n)]
    return ref.at[pl.ds(pl.multiple_of(j * n, n), n), :]


def _rs_chips(parts, kinds):
    nt = len(parts)
    flat = [p for per_layer in parts for p in per_layer]
    nm = len(flat)
    out_shape = []
    for per_layer, k in zip(parts, kinds):
        r, c = per_layer[0].shape
        ps = (r, c // 4) if k == "col" else (r // 4, c)
        L = len(per_layer)
        out_shape += [jax.ShapeDtypeStruct((L,) + ps, BF), jax.ShapeDtypeStruct((3, L) + ps, BF)]

    def body(*refs):
        g = refs[:nm]
        outs = refs[nm:nm + 2 * nt]
        loc, ssem, rsem = refs[nm + 2 * nt:]
        x, y, c, chips = _place()
        own = 2 * x + y
        cps = []
        m = 0
        for t in range(nt):
            k = kinds[t]
            own_o, got_o = outs[2 * t], outs[2 * t + 1]
            for l in range(len(parts[t])):
                n = g[m].shape[1] // 4 if k == "col" else g[m].shape[0] // 4
                lc = pltpu.make_async_copy(_slot2(g[m], k, own, n), own_o.at[l], loc.at[m])
                lc.start()
                cps.append(lc)
                for p, (px, py) in enumerate(chips):
                    cp = pltpu.make_async_remote_copy(
                        src_ref=_slot2(g[m], k, 2 * px + py, n), dst_ref=got_o.at[p, l],
                        send_sem=ssem.at[m, p], recv_sem=rsem.at[m, p], device_id=(px, py, c), device_id_type=MESH)
                    cp.start()
                    cps.append(cp)
                m += 1
        for cp in cps:
            cp.wait()

    hbm = pl.BlockSpec(memory_space=pl.ANY)
    return pl.pallas_call(
        body, out_shape=out_shape, in_specs=[hbm] * nm, out_specs=[hbm] * (2 * nt),
        scratch_shapes=[pltpu.SemaphoreType.DMA((nm,)), pltpu.SemaphoreType.DMA((nm, 3)), pltpu.SemaphoreType.DMA((nm, 3))],
        name="reduce_chips",
    )(*flat)


def _finish_share(name, own, got, kind, c_arr):
    L, r, c = own.shape
    tr = _pick(r, 128 if kind == "col" else 256)
    nb = r // tr
    nq = L * nb
    own2 = own.reshape(L * r, c)
    got2 = got.reshape(3 * L * r, c)
    pick = lambda h, q: q * (1 - h) + (nq - 1) * h
    in_specs = [pl.BlockSpec((tr, c), lambda h, q, cc: (pick(h, q), 0))]
    in_specs += [pl.BlockSpec((tr, c), functools.partial(lambda h, q, cc, p: (p * nq + pick(h, q), 0), p=p)) for p in range(3)]
    if kind == "col":
        out_sd = (L, 2, r, c)
        o_spec = pl.BlockSpec((None, 2, tr, c), lambda h, q, cc: ((q * h) // nb, 0, (q * h) % nb, 0))
    else:
        out_sd = (L * r, 2 * c)
        o_spec = pl.BlockSpec((tr, 2 * c), lambda h, q, cc: (q * h, 0))

    def kern(c_ref, o_ref, g0, g1, g2, out_ref, mine, recv, ssem, rsem):
        h, q = pl.program_id(0), pl.program_id(1)
        x, y, cc, _ = _place()

        def swap(qq):
            return pltpu.make_async_remote_copy(src_ref=mine.at[qq], dst_ref=recv.at[qq], send_sem=ssem.at[qq],
                                                recv_sem=rsem.at[qq], device_id=(x, y, 1 - cc), device_id_type=MESH)

        @pl.when(h == 0)
        def _():
            mine[q] = ((o_ref[...].astype(F32) + g0[...].astype(F32)) + g1[...].astype(F32)) + g2[...].astype(F32)
            swap(q).start()

        @pl.when(h == 1)
        def _():
            swap(q).wait()
            a, b = mine[q], recv[q]
            first = c_ref[0] == 0
            lo, hi = jnp.where(first, a, b), jnp.where(first, b, a)
            if kind == "col":
                out_ref[0] = lo
                out_ref[1] = hi
            else:
                out_ref[:, :c] = lo
                out_ref[:, c:] = hi

    est = 2 * nq * tr * c * 4 + 6 * tr * c * 4 + 8 * tr * c * 2
    full = pl.pallas_call(
        kern,
        grid_spec=pltpu.PrefetchScalarGridSpec(
            num_scalar_prefetch=1, grid=(2, nq), in_specs=in_specs, out_specs=o_spec,
            scratch_shapes=[pltpu.VMEM((nq, tr, c), F32), pltpu.VMEM((nq, tr, c), F32),
                            pltpu.SemaphoreType.DMA((nq,)), pltpu.SemaphoreType.DMA((nq,))]),
        out_shape=jax.ShapeDtypeStruct(out_sd, F32), name=name,
        compiler_params=pltpu.CompilerParams(dimension_semantics=("arbitrary", "arbitrary"),
                                             vmem_limit_bytes=int(min(VMEM_CAP_BYTES, est + (12 << 20)))),
    )(c_arr, own2, got2, got2, got2)
    return full.reshape(L, 2 * r, c) if kind == "col" else full.reshape(L, r, 2 * c)


def _small_allreduce(buf, name):
    R = buf.shape[0]

    def body(x_ref, o_ref, slots, ssem, rsem):
        x, y, c, _ = _place()
        me = 4 * x + 2 * y + c
        slots[0] = x_ref[...]
        cps = []
        for k in range(1, 8):
            bx, by, bc = (k >> 2) & 1, (k >> 1) & 1, k & 1
            peer = (1 - x if bx else x, 1 - y if by else y, 1 - c if bc else c)
            cp = pltpu.make_async_remote_copy(src_ref=x_ref, dst_ref=slots.at[k], send_sem=ssem.at[k - 1],
                                              recv_sem=rsem.at[k - 1], device_id=peer, device_id_type=MESH)
            cp.start()
            cps.append(cp)
        for cp in cps:
            cp.wait()
        acc = slots[jnp.bitwise_xor(me, 0)]
        for d in range(1, 8):
            acc = acc + slots[jnp.bitwise_xor(me, d)]
        o_ref[...] = acc

    vm = pl.BlockSpec(memory_space=pltpu.VMEM)
    return pl.pallas_call(
        body, out_shape=jax.ShapeDtypeStruct(buf.shape, F32), in_specs=[vm], out_specs=vm,
        scratch_shapes=[pltpu.VMEM((8, R, LANES), F32), pltpu.SemaphoreType.DMA((7,)), pltpu.SemaphoreType.DMA((7,))],
        name=name,
        compiler_params=pltpu.CompilerParams(vmem_limit_bytes=int(min(VMEM_CAP_BYTES, 12 * R * LANES * 4 + (8 << 20)))),
    )(buf)


def _pack(arrs, rows_total):
    rows = []
    for a in arrs:
        f = a.reshape(-1).astype(F32)
        n = -(-f.shape[0] // LANES) * LANES
        rows.append(jnp.pad(f, (0, n - f.shape[0])).reshape(-1, LANES))
    buf = jnp.concatenate(rows, axis=0)
    return jnp.pad(buf, ((0, rows_total - buf.shape[0]), (0, 0)))


def _unpack(buf, shapes):
    out, r = [], 0
    for s in shapes:
        n = 1
        for d in s:
            n *= d
        nr = -(-n // LANES)
        out.append(buf[r:r + nr].reshape(-1)[:n].reshape(s))
        r += nr
    return out


def _rows_needed(shapes):
    tot = 0
    for s in shapes:
        n = 1
        for d in s:
            n *= d
        tot += -(-n // LANES)
    return -(-tot // 8) * 8


def _adam(w, g, m, v):
    m = ADAM_B1 * m + (1.0 - ADAM_B1) * g
    v = ADAM_B2 * v + (1.0 - ADAM_B2) * jnp.square(g)
    m_hat = m / (1.0 - ADAM_B1 ** ADAM_STEP)
    v_hat = v / (1.0 - ADAM_B2 ** ADAM_STEP)
    delta = -ADAM_LR * (m_hat / (jnp.sqrt(v_hat) + ADAM_EPS) + ADAM_WD * w)
    return delta, m, v


def _adam_call(name, w2, g2, m2, v2, tr):
    def fn(rv, cv):
        return list(_adam(*rv)), []

    width = w2.shape[1]
    return _rowcall(name, fn, [(w2, 0, width), (g2, 0, width), (m2, 0, width), (v2, 0, width)], [],
                    [(width, F32)] * 3, [], tr)


def kernel(x, mem, mem_norm, lb_logits, ffn1_norm, ffn1_w_in, ffn1_w_out, mix_norm, mem_w_kv, hgrn_w_in, hgrn_gnorm, hgrn_w_out, gmlp_w_in, gmlp_ln_g, gmlp_ln_b, gmlp_w_s, gmlp_b_s, gmlp_w_out, ffn2_norm, ffn2_w_in, ffn2_w_out, final_norm, loss_target, m_mem_norm, m_lb_logits, m_ffn1_norm, m_ffn1_w_in, m_ffn1_w_out, m_mix_norm, m_mem_w_kv, m_hgrn_w_in, m_hgrn_gnorm, m_hgrn_w_out, m_gmlp_w_in, m_gmlp_ln_g, m_gmlp_ln_b, m_gmlp_w_s, m_gmlp_b_s, m_gmlp_w_out, m_ffn2_norm, m_ffn2_w_in, m_ffn2_w_out, m_final_norm, v_mem_norm, v_lb_logits, v_ffn1_norm, v_ffn1_w_in, v_ffn1_w_out, v_mix_norm, v_mem_w_kv, v_hgrn_w_in, v_hgrn_gnorm, v_hgrn_w_out, v_gmlp_w_in, v_gmlp_ln_g, v_gmlp_ln_b, v_gmlp_w_s, v_gmlp_b_s, v_gmlp_w_out, v_ffn2_norm, v_ffn2_w_in, v_ffn2_w_out, v_final_norm):
    bl, seq, D = x.shape
    T = bl * seq
    mem_len = mem.shape[1]
    chip = 2 * lax.axis_index("x") + lax.axis_index("y")
    c_arr = lax.axis_index("c").astype(jnp.int32).reshape(1)
    TR = 256

    big = [("ffn1_w_in", ffn1_w_in, "col"), ("ffn1_w_out", ffn1_w_out, "row"), ("mem_w_kv", mem_w_kv, "col"),
           ("hgrn_w_in", hgrn_w_in, "col"), ("hgrn_w_out", hgrn_w_out, "row"), ("gmlp_w_in", gmlp_w_in, "col"),
           ("gmlp_w_out", gmlp_w_out, "row"), ("ffn2_w_in", ffn2_w_in, "col"), ("ffn2_w_out", ffn2_w_out, "row")]
    kinds = [k for (_, _, k) in big]
    shards_bf = []
    for nm, w, _ in big:
        L, r, c = w.shape
        (wb,) = _rowcall("cast_" + nm, lambda rv, cv: ([rv[0]], []), [(w.reshape(L * r, c), 0, c)], [], [(c, BF)], [], 512)
        shards_bf.append(wb.reshape(L, r, c))
    gathered = dict(zip([nm for (nm, _, _) in big], _allgather(shards_bf, kinds)))

    ln_w = GM_GROUPS * GM_GROUP_DIM
    placed = lax.dynamic_update_slice(jnp.zeros((8, ln_w), F32), jnp.concatenate([0.5 * gmlp_ln_g, 0.5 * gmlp_ln_b], axis=0),
                                      (0, chip * gmlp_ln_g.shape[1]))
    ln_full = _small_allreduce(placed.reshape(16, LANES), "gather_ln").reshape(8, ln_w)
    ln_g_full, ln_b_full = ln_full[0:1], ln_full[1:2]

    def rms_fwd(name, xin, g):
        (h,) = _rowcall(name, lambda rv, cv: ([_rmsnorm(rv[0], cv[0])], []), [(xin, 0, D)], [g.reshape(1, D)], [(D, BF)], [], TR)
        return h

    def ffn_fwd(tag, xin, g, w_in, w_out, layer):
        dff = w_out.shape[1]
        h = rms_fwd("rms_" + tag, xin, g)
        z = _mm("ffn_in_" + tag, h, w_in, "nn", BF, 512, 512, D, b_lead=layer)
        (a,) = _rowcall("swiglu_" + tag, lambda rv, cv: ([_silu(rv[0].astype(F32)) * rv[1].astype(F32)], []),
                        [(z, 0, dff), (z, 1, dff)], [], [(dff, BF)], [], TR)
        xo = _mm("ffn_out_" + tag, a, w_out, "nn", F32, 512, 1024, dff, scale=0.5, res=xin, b_lead=layer)
        return xo, (xin, h, z, a)

    def ffn_bwd(tag, dxo, saved, g, w_in, w_out, layer):
        xin, h, z, a = saved
        dff = w_out.shape[1]
        da = _mm("ffn_da_" + tag, dxo, w_out, "nt", BF, 512, dff // 2, D, scale=0.5, b_lead=layer)
        dw_out = _mm_tn_pair("ffn_dwo_" + tag, a, dxo, "row", c_arr, dff // 2, 512, scale=0.5)

        def sw_bwd(rv, cv):
            gt, up, d = rv[0].astype(F32), rv[1].astype(F32), rv[2].astype(F32)
            _, vjp = jax.vjp(lambda p, q: _silu(p) * q, gt, up)
            dg, du = vjp(d)
            return [jnp.concatenate([dg, du], axis=1)], []

        (dz,) = _rowcall("swiglu_bwd_" + tag, sw_bwd, [(z, 0, dff), (z, 1, dff), (da, 0, dff)], [], [(2 * dff, BF)], [], TR)
        dh = _mm("ffn_dh_" + tag, dz, w_in, "nt", F32, 512, 512, dff, b_lead=layer)
        dw_in = _mm_tn_pair("ffn_dwi_" + tag, h, dz, "col", c_arr, dff, 512)
        dx, dg = rms_bwd("rms_bwd_" + tag, xin, g, dh, dxo)
        return dx, dg, dw_in, dw_out

    def rms_bwd(name, xin, g, dh, dres):
        def fn(rv, cv):
            _, vjp = jax.vjp(_rmsnorm, rv[0], cv[0])
            dx, dg = vjp(rv[1])
            if dres is not None:
                dx = dx + rv[2]
            return [dx], [dg]

        rows = [(xin, 0, D), (dh, 0, D)] + ([(dres, 0, D)] if dres is not None else [])
        dx, dg = _rowcall(name, fn, rows, [g.reshape(1, D)], [(D, F32)], [((1, D), F32)], TR)
        return dx, dg

    x0 = x.reshape(T, D)
    tgt = loss_target.reshape(T, D)
    mem2 = mem.reshape(bl * mem_len, D)
    memn = rms_fwd("rms_mem", mem2, mem_norm)
    kv = [_mm("kv_%d" % i, memn, gathered["mem_w_kv"], "nn", F32, 512, 512, D, b_lead=i) for i in range(2)]

    x1, sv_f10 = ffn_fwd("f1l0", x0, ffn1_norm[0], gathered["ffn1_w_in"], gathered["ffn1_w_out"], 0)
    h_m0 = rms_fwd("rms_mix0", x1, mix_norm[0])
    z_m0 = _mm("mix_in_0", h_m0, gathered["hgrn_w_in"], "nn", F32, 512, 512, D, b_lead=0)
    nc0 = seq // HG_CHUNK
    cat0, stash0 = _hgrn_fwd(z_m0, lb_logits, hgrn_gnorm, kv[0], bl, nc0)
    x2 = _mm("mix_out_0", cat0, gathered["hgrn_w_out"], "nn", F32, 512, 1024, cat0.shape[1], res=x1, b_lead=0)
    x3, sv_f20 = ffn_fwd("f2l0", x2, ffn2_norm[0], gathered["ffn2_w_in"], gathered["ffn2_w_out"], 0)
    x4, sv_f11 = ffn_fwd("f1l1", x3, ffn1_norm[1], gathered["ffn1_w_in"], gathered["ffn1_w_out"], 1)
    h_m1 = rms_fwd("rms_mix1", x4, mix_norm[1])
    z_m1 = _mm("mix_in_1", h_m1, gathered["gmlp_w_in"], "nn", F32, 512, 512, D, b_lead=0)
    nc1 = seq // GM_CHUNK
    w_s, b_s = gmlp_w_s[0], gmlp_b_s[0]
    cat1 = _gmlp_fwd(z_m1, ln_g_full, ln_b_full, w_s, b_s, kv[1], bl, nc1)
    x5 = _mm("mix_out_1", cat1, gathered["gmlp_w_out"], "nn", F32, 512, 1024, cat1.shape[1], res=x4, b_lead=0)
    x6, sv_f21 = ffn_fwd("f2l1", x5, ffn2_norm[1], gathered["ffn2_w_in"], gathered["ffn2_w_out"], 1)

    def head(rv, cv):
        def f(xx, gg):
            err = _rmsnorm(xx, gg) - rv[1]
            return 0.5 * jnp.sum(jnp.mean(err * err, axis=-1, keepdims=True), axis=0, keepdims=True)

        ls, vjp = jax.vjp(f, rv[0], cv[0])
        dx, dg = vjp(jnp.ones((1, 1), F32))
        return [dx], [dg, jnp.broadcast_to(ls, (1, 128))]

    dx6, d_final, loss_part = _rowcall("loss_head", head, [(x6, 0, D), (tgt, 0, D)], [final_norm.reshape(1, D)],
                                       [(D, F32)], [((1, D), F32), ((1, 128), F32)], TR)

    dx5, dg_f21, dwi_f21, dwo_f21 = ffn_bwd("f2l1", dx6, sv_f21, ffn2_norm[1], gathered["ffn2_w_in"], gathered["ffn2_w_out"], 1)
    dcat1 = _mm("mix_dcat_1", dx5, gathered["gmlp_w_out"], "nt", F32, 512, 512, D, b_lead=0)
    dwo_m1 = _mm_tn_pair("mix_dwo_1", cat1, dx5, "row", c_arr, 1024, 512)
    dz_m1, dkv1, d_lng, d_lnb, d_ws, d_bs = _gmlp_bwd(z_m1, dcat1, ln_g_full, ln_b_full, w_s, b_s, kv[1], bl, nc1)
    dh_m1 = _mm("mix_dh_1", dz_m1, gathered["gmlp_w_in"], "nt", F32, 512, 512, 1024, b_lead=0)
    dwi_m1 = _mm_tn_pair("mix_dwi_1", h_m1, dz_m1, "col", c_arr, 2560, 512)
    dx4, dg_m1 = rms_bwd("rms_bwd_mix1", x4, mix_norm[1], dh_m1, dx5)
    dx3, dg_f11, dwi_f11, dwo_f11 = ffn_bwd("f1l1", dx4, sv_f11, ffn1_norm[1], gathered["ffn1_w_in"], gathered["ffn1_w_out"], 1)

    dx2, dg_f20, dwi_f20, dwo_f20 = ffn_bwd("f2l0", dx3, sv_f20, ffn2_norm[0], gathered["ffn2_w_in"], gathered["ffn2_w_out"], 0)
    dcat0 = _mm("mix_dcat_0", dx2, gathered["hgrn_w_out"], "nt", F32, 512, 512, D, b_lead=0)
    dwo_m0 = _mm_tn_pair("mix_dwo_0", cat0, dx2, "row", c_arr, 1024, 512)
    dz_m0, dkv0, d_lb, d_gn = _hgrn_bwd(z_m0, dcat0, stash0, lb_logits, hgrn_gnorm, kv[0], bl, nc0)
    dh_m0 = _mm("mix_dh_0", dz_m0, gathered["hgrn_w_in"], "nt", F32, 512, 512, 1024, b_lead=0)
    dwi_m0 = _mm_tn_pair("mix_dwi_0", h_m0, dz_m0, "col", c_arr, 2560, 512)
    dx1, dg_m0 = rms_bwd("rms_bwd_mix0", x1, mix_norm[0], dh_m0, dx2)
    dx0, dg_f10, dwi_f10, dwo_f10 = ffn_bwd("f1l0", dx1, sv_f10, ffn1_norm[0], gathered["ffn1_w_in"], gathered["ffn1_w_out"], 0)

    dwkv = [_mm_tn_pair("kv_dw_%d" % i, memn, dkv, "col", c_arr, 1024, 512) for i, dkv in enumerate([dkv0, dkv1])]
    dmemn = _mm("kv_dx_0", dkv0, gathered["mem_w_kv"], "nt", F32, 512, 512, 1024, b_lead=0)
    dmemn = _mm("kv_dx_1", dkv1, gathered["mem_w_kv"], "nt", F32, 512, 512, 1024, res=dmemn, b_lead=1)
    _, d_memnorm = rms_bwd("rms_bwd_mem", mem2, mem_norm, dmemn, None)

    mats = [("ffn1_w_in", [dwi_f10, dwi_f11]), ("ffn1_w_out", [dwo_f10, dwo_f11]), ("mem_w_kv", dwkv),
            ("hgrn_w_in", [dwi_m0]), ("hgrn_w_out", [dwo_m0]), ("gmlp_w_in", [dwi_m1]), ("gmlp_w_out", [dwo_m1]),
            ("ffn2_w_in", [dwi_f20, dwi_f21]), ("ffn2_w_out", [dwo_f20, dwo_f21])]
    chp = _rs_chips([gl for (_, gl) in mats], kinds)
    shard_grads = [_finish_share("finish_" + nm, chp[2 * t], chp[2 * t + 1], kinds[t], c_arr)
                   for t, (nm, _) in enumerate(mats)]

    big_w = [w for (_, w, _) in big]
    big_m = [m_ffn1_w_in, m_ffn1_w_out, m_mem_w_kv, m_hgrn_w_in, m_hgrn_w_out, m_gmlp_w_in, m_gmlp_w_out, m_ffn2_w_in, m_ffn2_w_out]
    big_v = [v_ffn1_w_in, v_ffn1_w_out, v_mem_w_kv, v_hgrn_w_in, v_hgrn_w_out, v_gmlp_w_in, v_gmlp_w_out, v_ffn2_w_in, v_ffn2_w_out]
    big_out = {}
    for (nm, w, _), g, m, v in zip(big, shard_grads, big_m, big_v):
        L, r, c = w.shape
        d2, m2, v2 = _adam_call("adam_" + nm, w.reshape(L * r, c), g.reshape(L * r, c), m.reshape(L * r, c),
                                v.reshape(L * r, c), 256)
        big_out[nm] = (g, d2.reshape(w.shape), m2.reshape(w.shape), v2.reshape(w.shape))

    d_ffn1n = jnp.concatenate([dg_f10, dg_f11], axis=0)
    d_mixn = jnp.concatenate([dg_m0, dg_m1], axis=0)
    d_ffn2n = jnp.concatenate([dg_f20, dg_f21], axis=0)
    small_parts = [loss_part[:, :1], d_memnorm, d_lb, d_ffn1n, d_mixn, d_gn, d_lng, d_lnb, d_ws, d_bs, d_ffn2n, d_final]
    red_shapes = [(1,), mem_norm.shape, lb_logits.shape, ffn1_norm.shape, mix_norm.shape, hgrn_gnorm.shape, (1, ln_w), (1, ln_w),
                  gmlp_w_s.shape, gmlp_b_s.shape, ffn2_norm.shape, final_norm.shape]
    red = _small_allreduce(_pack(small_parts, _rows_needed(red_shapes)), "reduce_small")
    (loss_v, g_memn, g_lb, g_f1n, g_mixn, g_gn, g_lng_full, g_lnb_full, g_ws, g_bs, g_f2n, g_fin) = _unpack(red, red_shapes)
    lsh = gmlp_ln_g.shape[1]
    g_lng = lax.dynamic_slice(g_lng_full, (0, chip * lsh), (1, lsh))
    g_lnb = lax.dynamic_slice(g_lnb_full, (0, chip * lsh), (1, lsh))
    small_w = [mem_norm, lb_logits, ffn1_norm, mix_norm, hgrn_gnorm, gmlp_ln_g, gmlp_ln_b, gmlp_w_s, gmlp_b_s, ffn2_norm, final_norm]
    small_g = [g_memn, g_lb, g_f1n, g_mixn, g_gn, g_lng, g_lnb, g_ws, g_bs, g_f2n, g_fin]
    small_m = [m_mem_norm, m_lb_logits, m_ffn1_norm, m_mix_norm, m_hgrn_gnorm, m_gmlp_ln_g, m_gmlp_ln_b, m_gmlp_w_s, m_gmlp_b_s, m_ffn2_norm, m_final_norm]
    small_v = [v_mem_norm, v_lb_logits, v_ffn1_norm, v_mix_norm, v_hgrn_gnorm, v_gmlp_ln_g, v_gmlp_ln_b, v_gmlp_w_s, v_gmlp_b_s, v_ffn2_norm, v_final_norm]
    sshapes = [w.shape for w in small_w]
    nrow = _rows_needed(sshapes)
    d_p, m_p, v_p = _adam_call("adam_small", _pack(small_w, nrow), _pack(small_g, nrow), _pack(small_m, nrow), _pack(small_v, nrow), nrow)
    s_delta, s_m, s_v = _unpack(d_p, sshapes), _unpack(m_p, sshapes), _unpack(v_p, sshapes)
    small_names = ["mem_norm", "lb_logits", "ffn1_norm", "mix_norm", "hgrn_gnorm", "gmlp_ln_g", "gmlp_ln_b", "gmlp_w_s", "gmlp_b_s", "ffn2_norm", "final_norm"]
    small_out = {nm: (g.reshape(w.shape), d, m, v) for nm, w, g, d, m, v in zip(small_names, small_w, small_g, s_delta, s_m, s_v)}

    order = ["mem_norm", "lb_logits", "ffn1_norm", "ffn1_w_in", "ffn1_w_out", "mix_norm", "mem_w_kv", "hgrn_w_in", "hgrn_gnorm",
             "hgrn_w_out", "gmlp_w_in", "gmlp_ln_g", "gmlp_ln_b", "gmlp_w_s", "gmlp_b_s", "gmlp_w_out", "ffn2_norm", "ffn2_w_in",
             "ffn2_w_out", "final_norm"]
    allo = {**big_out, **small_out}
    grad_x = dx0.reshape(x.shape)
    return (loss_v.reshape(()), grad_x, *[allo[n][0] for n in order], *[allo[n][1] for n in order],
            *[allo[n][2] for n in order], *[allo[n][3] for n in order])
```

```python
import functools

import jax
import jax.numpy as jnp
from jax import lax
from jax.experimental import pallas as pl
from jax.experimental.pallas import tpu as pltpu
from jax.experimental.pallas import tpu_sc as plsc

BF = jnp.bfloat16
F32 = jnp.float32
MESH = pl.DeviceIdType.MESH

EPS = 1e-6
D_MODEL = 1024
HG_HEADS = 8
HG_DIM = 128
HG_CHUNK = 64
GM_CHUNK = 128
GM_GROUPS = 8
GM_GROUP_DIM = 256
XA_HEADS = 4
XA_DIM = 256
ADAM_LR = 0.001
ADAM_B1 = 0.9
ADAM_B2 = 0.999
ADAM_EPS = 1e-08
ADAM_WD = 0.01
ADAM_STEP = 10

VMEM_CAP_BYTES = 60 * 1024 * 1024
LANES = 1024


def _pick(n, cap, mult=16):
    if n <= cap:
        return n
    for d in range(cap - cap % mult, 0, -mult):
        if n % d == 0:
            return d
    raise ValueError((n, cap, mult))


def _dg(a, b, ca, cb):
    return lax.dot_general(a.astype(BF), b.astype(BF), (((ca,), (cb,)), ((), ())), preferred_element_type=F32)


@jax.custom_vjp
def dot_nn(a, b):
    return _dg(a, b, 1, 0)


def _nn_fwd(a, b):
    return _dg(a, b, 1, 0), (a, b)


def _nn_bwd(r, g):
    a, b = r
    return _dg(g, b, 1, 1), _dg(a, g, 0, 0)


dot_nn.defvjp(_nn_fwd, _nn_bwd)


@jax.custom_vjp
def dot_nt(a, b):
    return _dg(a, b, 1, 1)


def _nt_fwd(a, b):
    return _dg(a, b, 1, 1), (a, b)


def _nt_bwd(r, g):
    a, b = r
    return _dg(g, b, 1, 0), _dg(g, a, 0, 0)


dot_nt.defvjp(_nt_fwd, _nt_bwd)


@jax.custom_vjp
def dot_tn(a, b):
    return _dg(a, b, 0, 0)


def _tn_fwd(a, b):
    return _dg(a, b, 0, 0), (a, b)


def _tn_bwd(r, g):
    a, b = r
    return _dg(b, g, 1, 1), _dg(a, g, 1, 0)


dot_tn.defvjp(_tn_fwd, _tn_bwd)


def _rmsnorm(x, g):
    return x * lax.rsqrt(jnp.mean(x * x, axis=-1, keepdims=True) + EPS) * g


def _silu(x):
    return x * jax.nn.sigmoid(x)


def _gelu(x):
    return 0.5 * x * (1.0 + lax.erf(x * (0.5 ** 0.5)))


def _softmax_last(s):
    m = lax.stop_gradient(jnp.max(s, axis=-1, keepdims=True))
    e = jnp.exp(s - m)
    return e / jnp.sum(e, axis=-1, keepdims=True)


def _tril(n):
    r = lax.broadcasted_iota(jnp.int32, (n, n), 0)
    c = lax.broadcasted_iota(jnp.int32, (n, n), 1)
    return r >= c


def _cumsum_rows(l):
    n = l.shape[0]
    return lax.dot_general(_tril(n).astype(F32), l, (((1,), (0,)), ((), ())),
                           precision=lax.Precision.HIGHEST, preferred_element_type=F32)


def _attention(zx, mk, mv):
    s = dot_nt(zx, mk) * (XA_DIM ** -0.5)
    return dot_nn(_softmax_last(s), mv)


def _hgrn_head(zq, zf, zi, zg, l0, l1, l2, gn, S):
    m = lax.stop_gradient(jnp.maximum(jnp.maximum(l0, l1), l2))
    e0 = jnp.exp(l0 - m)
    lb = e0 / (e0 + jnp.exp(l1 - m) + jnp.exp(l2 - m))
    q = _silu(zq)
    f = lb + (1.0 - lb) * jax.nn.sigmoid(zf)
    k = 1.0 - f
    b = _cumsum_rows(jnp.log(f))
    b_last = b[HG_CHUNK - 1:HG_CHUNK, :]
    q_dec = q * jnp.exp(b)
    k_inv = k * jnp.exp(-b)
    a = jnp.where(_tril(HG_CHUNK), dot_nt(q_dec, k_inv), 0.0)
    o = dot_nn(a, zi) + dot_nn(q_dec, S)
    S_new = jnp.exp(b_last).reshape(HG_DIM, 1) * S + dot_tn(k * jnp.exp(b_last - b), zi)
    o = _rmsnorm(o, gn) * _silu(zg)
    return o, S_new


def _hgrn_block(zq, zf, zi, zg, zx, l0, l1, l2, gn, mk, mv, S):
    outs, s_new = [], []
    for h in range(HG_HEADS):
        o, sn = _hgrn_head(zq[h], zf[h], zi[h], zg[h], l0[h], l1[h], l2[h], gn, S[h])
        outs.append(o)
        s_new.append(sn)
    for a in range(XA_HEADS):
        outs.append(_attention(zx[a], mk[a], mv[a]))
    return outs, s_new


def _gmlp_block(zu, zv, zx, lng, lnb, ws, bs, mk, mv):
    gv = [_gelu(v) for v in zv]
    width = GM_GROUPS * GM_GROUP_DIM
    mu = sum(jnp.sum(g, axis=-1, keepdims=True) for g in gv) / width
    xc = [g - mu for g in gv]
    var = sum(jnp.sum(c * c, axis=-1, keepdims=True) for c in xc) / width
    r = lax.rsqrt(var + EPS)
    outs = []
    for g in range(GM_GROUPS):
        v = xc[g] * r * lng[g] + lnb[g]
        w = jnp.where(_tril(GM_CHUNK), ws[g], 0.0)
        mixed = dot_nn(w, v) + bs[g].reshape(GM_CHUNK, 1)
        outs.append(_gelu(zu[g]) * mixed)
    for a in range(XA_HEADS):
        outs.append(_attention(zx[a], mk[a], mv[a]))
    return outs


def _rowcall(name, fn, rows, consts, row_outs, acc_outs, tr):
    nrows = rows[0][0].shape[0]
    tr = _pick(nrows, tr)
    n_r, n_c, n_ro, n_ao = len(rows), len(consts), len(row_outs), len(acc_outs)

    def kern(*refs):
        rv = [r[...] for r in refs[:n_r]]
        cv = [r[...] for r in refs[n_r:n_r + n_c]]
        ro_refs = refs[n_r + n_c:n_r + n_c + n_ro]
        ao_refs = refs[n_r + n_c + n_ro:]
        ro, ao = fn(rv, cv)
        for ref, v in zip(ro_refs, ro):
            ref[...] = v.astype(ref.dtype)
        if n_ao:
            @pl.when(pl.program_id(0) == 0)
            def _():
                for ref in ao_refs:
                    ref[...] = jnp.zeros(ref.shape, ref.dtype)

            for ref, v in zip(ao_refs, ao):
                ref[...] += v.astype(ref.dtype)

    in_specs = [pl.BlockSpec((tr, w), functools.partial(lambda i, cb: (i, cb), cb=cb)) for (_, cb, w) in rows]
    in_specs += [pl.BlockSpec(c.shape, lambda i: (0, 0)) for c in consts]
    out_specs = [pl.BlockSpec((tr, w), lambda i: (i, 0)) for (w, _) in row_outs]
    out_specs += [pl.BlockSpec(s, lambda i: (0, 0)) for (s, _) in acc_outs]
    out_shape = [jax.ShapeDtypeStruct((nrows, w), dt) for (w, dt) in row_outs]
    out_shape += [jax.ShapeDtypeStruct(s, dt) for (s, dt) in acc_outs]
    est = sum(tr * w * a.dtype.itemsize for (a, _, w) in rows) + sum(tr * w * jnp.dtype(dt).itemsize for (w, dt) in row_outs)
    est += sum(c.size * c.dtype.itemsize for c in consts)
    outs = pl.pallas_call(
        kern, grid=(nrows // tr,), in_specs=in_specs, out_specs=out_specs, out_shape=out_shape, name=name,
        compiler_params=pltpu.CompilerParams(dimension_semantics=("arbitrary",),
                                             vmem_limit_bytes=int(min(VMEM_CAP_BYTES, 6 * est + (16 << 20)))),
    )(*[a for (a, _, _) in rows], *consts)
    return outs


def _mm(name, a, b, mode, out_dtype, tm, tn, tk, scale=1.0, res=None, a_lead=None, b_lead=None):
    ash = a.shape[-2:]
    bsh = b.shape[-2:]
    if mode == "nn":
        (M, K), (K2, N) = ash, bsh
    elif mode == "nt":
        (M, K), (N, K2) = ash, bsh
    else:
        (K, M), (K2, N) = ash, bsh
    assert K == K2, (name, a.shape, b.shape)
    tm, tn, tk = min(tm, M), min(tn, N), min(tk, K)
    assert M % tm == 0 and N % tn == 0 and K % tk == 0, (name, M, N, K, tm, tn, tk)
    nk = K // tk
    dims = {"nn": (1, 0), "nt": (1, 1), "tn": (0, 0)}[mode]

    def lead(spec_shape, index_fn, lead_idx):
        if lead_idx is None:
            return pl.BlockSpec(spec_shape, index_fn)
        return pl.BlockSpec((None,) + spec_shape, lambda i, j, k: (lead_idx,) + index_fn(i, j, k))

    if mode == "tn":
        a_spec = lead((tk, tm), lambda i, j, k: (k, i), a_lead)
    else:
        a_spec = lead((tm, tk), lambda i, j, k: (i, k), a_lead)
    if mode == "nt":
        b_spec = lead((tn, tk), lambda i, j, k: (j, k), b_lead)
    else:
        b_spec = lead((tk, tn), lambda i, j, k: (k, j), b_lead)
    o_spec = pl.BlockSpec((tm, tn), lambda i, j, k: (i, j))
    has_res = res is not None

    def kern(*refs):
        a_ref, b_ref = refs[0], refs[1]
        res_ref = refs[2] if has_res else None
        o_ref = refs[3] if has_res else refs[2]
        acc_ref = refs[-1] if nk > 1 else None
        p = lax.dot_general(a_ref[...].astype(BF), b_ref[...].astype(BF), (((dims[0],), (dims[1],)), ((), ())),
                            preferred_element_type=F32)

        def finish(v):
            if scale != 1.0:
                v = v * scale
            if has_res:
                v = res_ref[...] + v
            o_ref[...] = v.astype(o_ref.dtype)

        if nk == 1:
            finish(p)
        else:
            k = pl.program_id(2)

            @pl.when(k == 0)
            def _():
                acc_ref[...] = p

            @pl.when(k > 0)
            def _():
                acc_ref[...] += p

            @pl.when(k == nk - 1)
            def _():
                finish(acc_ref[...])

    ins = [a, b] + ([res] if has_res else [])
    in_specs = [a_spec, b_spec] + ([o_spec] if has_res else [])
    est = tm * tk * a.dtype.itemsize + tk * tn * b.dtype.itemsize + tm * tn * (jnp.dtype(out_dtype).itemsize + 8)
    return pl.pallas_call(
        kern, grid=(M // tm, N // tn, nk), in_specs=in_specs, out_specs=o_spec,
        out_shape=jax.ShapeDtypeStruct((M, N), out_dtype),
        scratch_shapes=[pltpu.VMEM((tm, tn), F32)] if nk > 1 else [],
        name=name,
        compiler_params=pltpu.CompilerParams(dimension_semantics=("parallel", "parallel", "arbitrary"),
                                             vmem_limit_bytes=int(min(VMEM_CAP_BYTES, 3 * est + (16 << 20)))),
    )(*ins)


def _mm_tn_pair(name, a, b, kind, c_arr, tq, tk, scale=1.0):
    T, M = a.shape
    _, N = b.shape
    tk = min(tk, T)
    assert T % tk == 0
    nk = T // tk
    if kind == "col":
        hm = M // 2
        assert N % tq == 0
        nq = N // tq
        tile = (hm, tq)
        a_spec = pl.BlockSpec((tk, hm), lambda h, q, k, c: (k, jnp.bitwise_xor(h, 1 - c[0])))
        b_spec = pl.BlockSpec((tk, tq), lambda h, q, k, c: (k, q))
        o_spec = pl.BlockSpec(tile, lambda h, q, k, c: (0, q * h))
        out_sd = (hm, N)
    else:
        hn = N // 2
        assert M % tq == 0
        nq = M // tq
        tile = (tq, hn)
        a_spec = pl.BlockSpec((tk, tq), lambda h, q, k, c: (k, q))
        b_spec = pl.BlockSpec((tk, hn), lambda h, q, k, c: (k, jnp.bitwise_xor(h, 1 - c[0])))
        o_spec = pl.BlockSpec(tile, lambda h, q, k, c: (q * h, 0))
        out_sd = (M, hn)

    def kern(c_ref, a_ref, b_ref, o_ref, acc, stage, recv, ssem, rsem):
        h, q, k = pl.program_id(0), pl.program_id(1), pl.program_id(2)
        x, y, c, _ = _place()
        p = lax.dot_general(a_ref[...].astype(BF), b_ref[...].astype(BF), (((0,), (0,)), ((), ())), preferred_element_type=F32)

        @pl.when(k == 0)
        def _():
            acc[...] = p

        @pl.when(k > 0)
        def _():
            acc[...] += p

        def send(slot, qq):
            return pltpu.make_async_remote_copy(src_ref=stage.at[slot], dst_ref=recv.at[qq], send_sem=ssem.at[slot],
                                                recv_sem=rsem.at[qq], device_id=(x, y, 1 - c), device_id_type=MESH)

        last = k == nk - 1

        @pl.when(jnp.logical_and(last, h == 0))
        def _():
            slot = q % 2

            @pl.when(q >= 2)
            def _():
                send(slot, q).wait_send()

            stage[slot] = (acc[...] * scale).astype(BF)
            send(slot, q).start()

        @pl.when(jnp.logical_and(last, h == 1))
        def _():
            @pl.when(q == 0)
            def _():
                for s in range(min(nq, 2)):
                    send(s, 0).wait_send()

            send(0, q).wait_recv()
            o_ref[...] = (acc[...] * scale + recv[q].astype(F32)).astype(o_ref.dtype)

    tb = tile[0] * tile[1]
    est = tb * (4 + 2 * 2 + nq * 2 + 2 * 2) + 2 * tk * (a_spec.block_shape[1] + b_spec.block_shape[1]) * 2 * 2
    return pl.pallas_call(
        kern,
        grid_spec=pltpu.PrefetchScalarGridSpec(
            num_scalar_prefetch=1, grid=(2, nq, nk), in_specs=[a_spec, b_spec], out_specs=o_spec,
            scratch_shapes=[pltpu.VMEM(tile, F32), pltpu.VMEM((2,) + tile, BF), pltpu.VMEM((nq,) + tile, BF),
                            pltpu.SemaphoreType.DMA((2,)), pltpu.SemaphoreType.DMA((nq,))]),
        out_shape=jax.ShapeDtypeStruct(out_sd, BF), name=name,
        compiler_params=pltpu.CompilerParams(dimension_semantics=("arbitrary", "arbitrary", "arbitrary"),
                                             vmem_limit_bytes=int(min(VMEM_CAP_BYTES, est + (12 << 20)))),
    )(c_arr, a, b)


def _hgrn_pieces(z_ref):
    W = HG_HEADS * HG_DIM
    zq = [z_ref[:, h * HG_DIM:(h + 1) * HG_DIM] for h in range(HG_HEADS)]
    zf = [z_ref[:, W + h * HG_DIM:W + (h + 1) * HG_DIM] for h in range(HG_HEADS)]
    zi = [z_ref[:, 2 * W + h * HG_DIM:2 * W + (h + 1) * HG_DIM] for h in range(HG_HEADS)]
    zg = [z_ref[:, 3 * W + h * HG_DIM:3 * W + (h + 1) * HG_DIM] for h in range(HG_HEADS)]
    zx = [z_ref[:, 4 * W + a * XA_DIM:4 * W + (a + 1) * XA_DIM] for a in range(XA_HEADS)]
    return zq, zf, zi, zg, zx


def _kv_pieces(kv_ref):
    W = XA_HEADS * XA_DIM
    mk = [kv_ref[:, a * XA_DIM:(a + 1) * XA_DIM] for a in range(XA_HEADS)]
    mv = [kv_ref[:, W + a * XA_DIM:W + (a + 1) * XA_DIM] for a in range(XA_HEADS)]
    return mk, mv


def _lb_pieces(lb_ref):
    return [[lb_ref[r:r + 1, h * HG_DIM:(h + 1) * HG_DIM] for h in range(HG_HEADS)] for r in range(3)]


def _hgrn_fwd(z, lb_logits, gnorm, kv, bl, nc):
    T, zw = z.shape
    mem_len = kv.shape[0] // bl
    cat_w = HG_HEADS * HG_DIM + XA_HEADS * XA_DIM

    def kern(z_ref, lb_ref, gn_ref, kv_ref, cat_ref, st_ref, s_scr):
        @pl.when(pl.program_id(1) == 0)
        def _():
            s_scr[...] = jnp.zeros(s_scr.shape, F32)

        st_ref[...] = s_scr[...]
        zq, zf, zi, zg, zx = _hgrn_pieces(z_ref)
        mk, mv = _kv_pieces(kv_ref)
        l0, l1, l2 = _lb_pieces(lb_ref)
        S = [s_scr[h] for h in range(HG_HEADS)]
        outs, s_new = _hgrn_block(zq, zf, zi, zg, zx, l0, l1, l2, gn_ref[...], mk, mv, S)
        for h in range(HG_HEADS):
            cat_ref[:, h * HG_DIM:(h + 1) * HG_DIM] = outs[h].astype(cat_ref.dtype)
            s_scr[h] = s_new[h]
        base = HG_HEADS * HG_DIM
        for a in range(XA_HEADS):
            cat_ref[:, base + a * XA_DIM:base + (a + 1) * XA_DIM] = outs[HG_HEADS + a].astype(cat_ref.dtype)

    return pl.pallas_call(
        kern, grid=(bl, nc),
        in_specs=[pl.BlockSpec((HG_CHUNK, zw), lambda b, n: (b * nc + n, 0)),
                  pl.BlockSpec(lb_logits.shape, lambda b, n: (0, 0)),
                  pl.BlockSpec(gnorm.shape, lambda b, n: (0, 0)),
                  pl.BlockSpec((mem_len, kv.shape[1]), lambda b, n: (b, 0))],
        out_specs=[pl.BlockSpec((HG_CHUNK, cat_w), lambda b, n: (b * nc + n, 0)),
                   pl.BlockSpec((None, HG_HEADS, HG_DIM, HG_DIM), lambda b, n: (b * nc + n, 0, 0, 0))],
        out_shape=[jax.ShapeDtypeStruct((T, cat_w), BF),
                   jax.ShapeDtypeStruct((bl * nc, HG_HEADS, HG_DIM, HG_DIM), F32)],
        scratch_shapes=[pltpu.VMEM((HG_HEADS, HG_DIM, HG_DIM), F32)],
        name="hgrn_fwd",
        compiler_params=pltpu.CompilerParams(dimension_semantics=("arbitrary", "arbitrary"), vmem_limit_bytes=48 << 20),
    )(z, lb_logits, gnorm, kv)


def _hgrn_bwd(z, dcat, stash, lb_logits, gnorm, kv, bl, nc):
    T, zw = z.shape
    mem_len = kv.shape[0] // bl
    cat_w = dcat.shape[1]

    def kern(z_ref, dc_ref, st_ref, lb_ref, gn_ref, kv_ref, dz_ref, dkv_ref, dlb_ref, dgn_ref, ds_scr):
        first = jnp.logical_and(pl.program_id(0) == 0, pl.program_id(1) == 0)

        @pl.when(pl.program_id(1) == 0)
        def _():
            ds_scr[...] = jnp.zeros(ds_scr.shape, F32)
            dkv_ref[...] = jnp.zeros(dkv_ref.shape, F32)

        @pl.when(first)
        def _():
            dlb_ref[...] = jnp.zeros(dlb_ref.shape, F32)
            dgn_ref[...] = jnp.zeros(dgn_ref.shape, F32)

        zq, zf, zi, zg, zx = _hgrn_pieces(z_ref)
        mk, mv = _kv_pieces(kv_ref)
        l0, l1, l2 = _lb_pieces(lb_ref)
        S = [st_ref[h] for h in range(HG_HEADS)]
        _, vjp = jax.vjp(_hgrn_block, zq, zf, zi, zg, zx, l0, l1, l2, gn_ref[...], mk, mv, S)
        d_outs = [dc_ref[:, h * HG_DIM:(h + 1) * HG_DIM] for h in range(HG_HEADS)]
        base = HG_HEADS * HG_DIM
        d_outs += [dc_ref[:, base + a * XA_DIM:base + (a + 1) * XA_DIM] for a in range(XA_HEADS)]
        d_s = [ds_scr[h] for h in range(HG_HEADS)]
        dzq, dzf, dzi, dzg, dzx, dl0, dl1, dl2, dgn, dmk, dmv, dS = vjp((d_outs, d_s))
        W = HG_HEADS * HG_DIM
        for h in range(HG_HEADS):
            sl = slice(h * HG_DIM, (h + 1) * HG_DIM)
            dz_ref[:, sl] = dzq[h].astype(dz_ref.dtype)
            dz_ref[:, W + h * HG_DIM:W + (h + 1) * HG_DIM] = dzf[h].astype(dz_ref.dtype)
            dz_ref[:, 2 * W + h * HG_DIM:2 * W + (h + 1) * HG_DIM] = dzi[h].astype(dz_ref.dtype)
            dz_ref[:, 3 * W + h * HG_DIM:3 * W + (h + 1) * HG_DIM] = dzg[h].astype(dz_ref.dtype)
            ds_scr[h] = dS[h]
            dlb_ref[0:1, sl] += dl0[h]
            dlb_ref[1:2, sl] += dl1[h]
            dlb_ref[2:3, sl] += dl2[h]
        dgn_ref[...] += dgn
        KW = XA_HEADS * XA_DIM
        for a in range(XA_HEADS):
            dz_ref[:, 4 * W + a * XA_DIM:4 * W + (a + 1) * XA_DIM] = dzx[a].astype(dz_ref.dtype)
            dkv_ref[:, a * XA_DIM:(a + 1) * XA_DIM] += dmk[a]
            dkv_ref[:, KW + a * XA_DIM:KW + (a + 1) * XA_DIM] += dmv[a]

    rev = lambda b, n: (b * nc + (nc - 1 - n), 0)
    return pl.pallas_call(
        kern, grid=(bl, nc),
        in_specs=[pl.BlockSpec((HG_CHUNK, zw), rev),
                  pl.BlockSpec((HG_CHUNK, cat_w), rev),
                  pl.BlockSpec((None, HG_HEADS, HG_DIM, HG_DIM), lambda b, n: (b * nc + (nc - 1 - n), 0, 0, 0)),
                  pl.BlockSpec(lb_logits.shape, lambda b, n: (0, 0)),
                  pl.BlockSpec(gnorm.shape, lambda b, n: (0, 0)),
                  pl.BlockSpec((mem_len, kv.shape[1]), lambda b, n: (b, 0))],
        out_specs=[pl.BlockSpec((HG_CHUNK, zw), rev),
                   pl.BlockSpec((mem_len, kv.shape[1]), lambda b, n: (b, 0)),
                   pl.BlockSpec(lb_logits.shape, lambda b, n: (0, 0)),
                   pl.BlockSpec(gnorm.shape, lambda b, n: (0, 0))],
        out_shape=[jax.ShapeDtypeStruct((T, zw), BF), jax.ShapeDtypeStruct(kv.shape, F32),
                   jax.ShapeDtypeStruct(lb_logits.shape, F32), jax.ShapeDtypeStruct(gnorm.shape, F32)],
        scratch_shapes=[pltpu.VMEM((HG_HEADS, HG_DIM, HG_DIM), F32)],
        name="hgrn_bwd",
        compiler_params=pltpu.CompilerParams(dimension_semantics=("arbitrary", "arbitrary"), vmem_limit_bytes=56 << 20),
    )(z, dcat, stash, lb_logits, gnorm, kv)


def _gmlp_pieces(z_ref):
    W = GM_GROUPS * GM_GROUP_DIM
    zu = [z_ref[:, g * GM_GROUP_DIM:(g + 1) * GM_GROUP_DIM] for g in range(GM_GROUPS)]
    zv = [z_ref[:, W + g * GM_GROUP_DIM:W + (g + 1) * GM_GROUP_DIM] for g in range(GM_GROUPS)]
    zx = [z_ref[:, 2 * W + a * XA_DIM:2 * W + (a + 1) * XA_DIM] for a in range(XA_HEADS)]
    return zu, zv, zx


def _gmlp_params(lng_ref, lnb_ref, ws_ref, bs_ref):
    lng = [lng_ref[:, g * GM_GROUP_DIM:(g + 1) * GM_GROUP_DIM] for g in range(GM_GROUPS)]
    lnb = [lnb_ref[:, g * GM_GROUP_DIM:(g + 1) * GM_GROUP_DIM] for g in range(GM_GROUPS)]
    ws = [ws_ref[g] for g in range(GM_GROUPS)]
    bs = [bs_ref[g:g + 1, :] for g in range(GM_GROUPS)]
    return lng, lnb, ws, bs


def _gmlp_fwd(z, ln_g, ln_b, w_s, b_s, kv, bl, nc):
    T, zw = z.shape
    mem_len = kv.shape[0] // bl
    cat_w = GM_GROUPS * GM_GROUP_DIM + XA_HEADS * XA_DIM

    def kern(z_ref, lng_ref, lnb_ref, ws_ref, bs_ref, kv_ref, cat_ref):
        zu, zv, zx = _gmlp_pieces(z_ref)
        lng, lnb, ws, bs = _gmlp_params(lng_ref, lnb_ref, ws_ref, bs_ref)
        mk, mv = _kv_pieces(kv_ref)
        outs = _gmlp_block(zu, zv, zx, lng, lnb, ws, bs, mk, mv)
        for g in range(GM_GROUPS):
            cat_ref[:, g * GM_GROUP_DIM:(g + 1) * GM_GROUP_DIM] = outs[g].astype(cat_ref.dtype)
        base = GM_GROUPS * GM_GROUP_DIM
        for a in range(XA_HEADS):
            cat_ref[:, base + a * XA_DIM:base + (a + 1) * XA_DIM] = outs[GM_GROUPS + a].astype(cat_ref.dtype)

    full2 = lambda b, n: (0, 0)
    return pl.pallas_call(
        kern, grid=(bl, nc),
        in_specs=[pl.BlockSpec((GM_CHUNK, zw), lambda b, n: (b * nc + n, 0)),
                  pl.BlockSpec(ln_g.shape, full2), pl.BlockSpec(ln_b.shape, full2),
                  pl.BlockSpec(w_s.shape, lambda b, n: (0, 0, 0)), pl.BlockSpec(b_s.shape, full2),
                  pl.BlockSpec((mem_len, kv.shape[1]), lambda b, n: (b, 0))],
        out_specs=pl.BlockSpec((GM_CHUNK, cat_w), lambda b, n: (b * nc + n, 0)),
        out_shape=jax.ShapeDtypeStruct((T, cat_w), BF),
        name="gmlp_fwd",
        compiler_params=pltpu.CompilerParams(dimension_semantics=("arbitrary", "arbitrary"), vmem_limit_bytes=48 << 20),
    )(z, ln_g, ln_b, w_s, b_s, kv)


def _gmlp_bwd(z, dcat, ln_g, ln_b, w_s, b_s, kv, bl, nc):
    T, zw = z.shape
    mem_len = kv.shape[0] // bl
    cat_w = dcat.shape[1]

    def kern(z_ref, dc_ref, lng_ref, lnb_ref, ws_ref, bs_ref, kv_ref,
             dz_ref, dkv_ref, dlng_ref, dlnb_ref, dws_ref, dbs_ref):
        first = jnp.logical_and(pl.program_id(0) == 0, pl.program_id(1) == 0)

        @pl.when(pl.program_id(1) == 0)
        def _():
            dkv_ref[...] = jnp.zeros(dkv_ref.shape, F32)

        @pl.when(first)
        def _():
            dlng_ref[...] = jnp.zeros(dlng_ref.shape, F32)
            dlnb_ref[...] = jnp.zeros(dlnb_ref.shape, F32)
            dws_ref[...] = jnp.zeros(dws_ref.shape, F32)
            dbs_ref[...] = jnp.zeros(dbs_ref.shape, F32)

        zu, zv, zx = _gmlp_pieces(z_ref)
        lng, lnb, ws, bs = _gmlp_params(lng_ref, lnb_ref, ws_ref, bs_ref)
        mk, mv = _kv_pieces(kv_ref)
        _, vjp = jax.vjp(_gmlp_block, zu, zv, zx, lng, lnb, ws, bs, mk, mv)
        d_outs = [dc_ref[:, g * GM_GROUP_DIM:(g + 1) * GM_GROUP_DIM] for g in range(GM_GROUPS)]
        base = GM_GROUPS * GM_GROUP_DIM
        d_outs += [dc_ref[:, base + a * XA_DIM:base + (a + 1) * XA_DIM] for a in range(XA_HEADS)]
        dzu, dzv, dzx, dlng, dlnb, dws, dbs, dmk, dmv = vjp(d_outs)
        W = GM_GROUPS * GM_GROUP_DIM
        for g in range(GM_GROUPS):
            sl = slice(g * GM_GROUP_DIM, (g + 1) * GM_GROUP_DIM)
            dz_ref[:, sl] = dzu[g].astype(dz_ref.dtype)
            dz_ref[:, W + g * GM_GROUP_DIM:W + (g + 1) * GM_GROUP_DIM] = dzv[g].astype(dz_ref.dtype)
            dlng_ref[:, sl] += dlng[g]
            dlnb_ref[:, sl] += dlnb[g]
            dws_ref[g] += dws[g]
            dbs_ref[g:g + 1, :] += dbs[g]
        KW = XA_HEADS * XA_DIM
        for a in range(XA_HEADS):
            dz_ref[:, 2 * W + a * XA_DIM:2 * W + (a + 1) * XA_DIM] = dzx[a].astype(dz_ref.dtype)
            dkv_ref[:, a * XA_DIM:(a + 1) * XA_DIM] += dmk[a]
            dkv_ref[:, KW + a * XA_DIM:KW + (a + 1) * XA_DIM] += dmv[a]

    full2 = lambda b, n: (0, 0)
    full3 = lambda b, n: (0, 0, 0)
    blk = lambda b, n: (b * nc + n, 0)
    return pl.pallas_call(
        kern, grid=(bl, nc),
        in_specs=[pl.BlockSpec((GM_CHUNK, zw), blk), pl.BlockSpec((GM_CHUNK, cat_w), blk),
                  pl.BlockSpec(ln_g.shape, full2), pl.BlockSpec(ln_b.shape, full2),
                  pl.BlockSpec(w_s.shape, full3), pl.BlockSpec(b_s.shape, full2),
                  pl.BlockSpec((mem_len, kv.shape[1]), lambda b, n: (b, 0))],
        out_specs=[pl.BlockSpec((GM_CHUNK, zw), blk),
                   pl.BlockSpec((mem_len, kv.shape[1]), lambda b, n: (b, 0)),
                   pl.BlockSpec(ln_g.shape, full2), pl.BlockSpec(ln_b.shape, full2),
                   pl.BlockSpec(w_s.shape, full3), pl.BlockSpec(b_s.shape, full2)],
        out_shape=[jax.ShapeDtypeStruct((T, zw), BF), jax.ShapeDtypeStruct(kv.shape, F32),
                   jax.ShapeDtypeStruct(ln_g.shape, F32), jax.ShapeDtypeStruct(ln_b.shape, F32),
                   jax.ShapeDtypeStruct(w_s.shape, F32), jax.ShapeDtypeStruct(b_s.shape, F32)],
        name="gmlp_bwd",
        compiler_params=pltpu.CompilerParams(dimension_semantics=("arbitrary", "arbitrary"), vmem_limit_bytes=56 << 20),
    )(z, dcat, ln_g, ln_b, w_s, b_s, kv)


def _place():
    x, y, c = lax.axis_index("x"), lax.axis_index("y"), lax.axis_index("c")
    chips = [(1 - x, y), (x, 1 - y), (1 - x, 1 - y)]
    return x, y, c, chips


def _half(ref, kind, e):
    if kind == "col":
        n = ref.shape[1] // 2
        return ref.at[:, pl.ds(pl.multiple_of(e * n, n), n), :]
    n = ref.shape[2] // 2
    return ref.at[:, :, pl.ds(pl.multiple_of(e * n, n), n)]


def _slot(ref, kind, j, n):
    if kind == "col":
        return ref.at[:, :, pl.ds(pl.multiple_of(j * n, n), n)]
    return ref.at[:, pl.ds(pl.multiple_of(j * n, n), n), :]


def _allgather(shards, kinds):
    nt = len(shards)
    out_shape = []
    for s, k in zip(shards, kinds):
        L, r, c = s.shape
        out_shape.append(jax.ShapeDtypeStruct((L, r, 4 * c) if k == "col" else (L, 4 * r, c), s.dtype))

    def body(*refs):
        sh, full = refs[:nt], refs[nt:2 * nt]
        loc, s_ici, r_ici, s_d2d, r_d2d = refs[2 * nt:]
        x, y, c, chips = _place()
        own = 2 * x + y
        started = []
        for t in range(nt):
            k = kinds[t]
            n = sh[t].shape[2] if k == "col" else sh[t].shape[1]
            mine = pltpu.make_async_copy(sh[t], _slot(full[t], k, own, n), loc.at[t])
            mine.start()
            started.append(mine)
        ici = []
        for t in range(nt):
            k = kinds[t]
            n = sh[t].shape[2] if k == "col" else sh[t].shape[1]
            for p, (px, py) in enumerate(chips):
                cp = pltpu.make_async_remote_copy(
                    src_ref=_half(sh[t], k, c), dst_ref=_half(_slot(full[t], k, own, n), k, c),
                    send_sem=s_ici.at[t, p], recv_sem=r_ici.at[t, p], device_id=(px, py, c), device_id_type=MESH)
                cp.start()
                ici.append(cp)
        d2d = []
        for t in range(nt):
            k = kinds[t]
            n = sh[t].shape[2] if k == "col" else sh[t].shape[1]
            for p, (px, py) in enumerate(chips):
                landed = _half(_slot(full[t], k, 2 * px + py, n), k, c)
                pltpu.make_async_remote_copy(
                    src_ref=landed, dst_ref=landed, send_sem=s_ici.at[t, p], recv_sem=r_ici.at[t, p],
                    device_id=(px, py, c), device_id_type=MESH).wait_recv()
                fw = pltpu.make_async_remote_copy(
                    src_ref=landed, dst_ref=landed, send_sem=s_d2d.at[t, p], recv_sem=r_d2d.at[t, p],
                    device_id=(x, y, 1 - c), device_id_type=MESH)
                fw.start()
                d2d.append(fw)
        for t in range(nt):
            k = kinds[t]
            n = sh[t].shape[2] if k == "col" else sh[t].shape[1]
            for p, (px, py) in enumerate(chips):
                other = _half(_slot(full[t], k, 2 * px + py, n), k, 1 - c)
                pltpu.make_async_remote_copy(
                    src_ref=other, dst_ref=other, send_sem=s_d2d.at[t, p], recv_sem=r_d2d.at[t, p],
                    device_id=(x, y, 1 - c), device_id_type=MESH).wait_recv()
        for cp in ici + d2d:
            cp.wait_send()
        for cp in started:
            cp.wait()

    hbm = pl.BlockSpec(memory_space=pl.ANY)
    return pl.pallas_call(
        body, out_shape=out_shape, in_specs=[hbm] * nt, out_specs=[hbm] * nt,
        scratch_shapes=[pltpu.SemaphoreType.DMA((nt,)), pltpu.SemaphoreType.DMA((nt, 3)), pltpu.SemaphoreType.DMA((nt, 3)),
                        pltpu.SemaphoreType.DMA((nt, 3)), pltpu.SemaphoreType.DMA((nt, 3))],
        name="allgather_weights",
    )(*shards)


def _allgather_seq(name, items, cid):
    nt = len(items)
    kinds = [k for (_, k, _) in items]
    out_type = []
    for s, k, l in items:
        L, r, c = s.shape
        lo = L if l is None else 1
        out_type.append(jax.ShapeDtypeStruct((lo, r, 4 * c) if k == "col" else (lo, 4 * r, c), s.dtype))

    def body(*refs):
        sh = [refs[t] if items[t][2] is None else refs[t].at[pl.ds(items[t][2], 1)] for t in range(nt)]
        full = refs[nt:2 * nt]
        loc, s_ici, r_ici, s_d2d, r_d2d = refs[2 * nt:]
        x, y, c, chips = _place()
        own = 2 * x + y
        sibling = (x, y, 1 - c)
        barrier = pltpu.get_barrier_semaphore()
        for peer in [(px, py, c) for (px, py) in chips] + [sibling]:
            pl.semaphore_signal(barrier, inc=1, device_id=peer, device_id_type=MESH)
        pl.semaphore_wait(barrier, 4)
        width = [sh[t].shape[2] if kinds[t] == "col" else sh[t].shape[1] for t in range(nt)]
        started = []
        for t in range(nt):
            mine = pltpu.make_async_copy(sh[t], _slot(full[t], kinds[t], own, width[t]), loc.at[t])
            mine.start()
            started.append(mine)
        sent = []
        for t in range(nt):
            for p, (px, py) in enumerate(chips):
                cp = pltpu.make_async_remote_copy(
                    src_ref=_half(sh[t], kinds[t], c), dst_ref=_half(_slot(full[t], kinds[t], own, width[t]), kinds[t], c),
                    send_sem=s_ici.at[t, p], recv_sem=r_ici.at[t, p], device_id=(px, py, c), device_id_type=MESH)
                cp.start()
                sent.append(cp)
        for t in range(nt):
            for p, (px, py) in enumerate(chips):
                landed = _half(_slot(full[t], kinds[t], 2 * px + py, width[t]), kinds[t], c)
                pltpu.make_async_remote_copy(
                    src_ref=landed, dst_ref=landed, send_sem=s_ici.at[t, p], recv_sem=r_ici.at[t, p],
                    device_id=(px, py, c), device_id_type=MESH).wait_recv()
                fw = pltpu.make_async_remote_copy(
                    src_ref=landed, dst_ref=landed, send_sem=s_d2d.at[t, p], recv_sem=r_d2d.at[t, p],
                    device_id=sibling, device_id_type=MESH)
                fw.start()
                sent.append(fw)
        for t in range(nt):
            for p, (px, py) in enumerate(chips):
                other = _half(_slot(full[t], kinds[t], 2 * px + py, width[t]), kinds[t], 1 - c)
                pltpu.make_async_remote_copy(
                    src_ref=other, dst_ref=other, send_sem=s_d2d.at[t, p], recv_sem=r_d2d.at[t, p],
                    device_id=sibling, device_id_type=MESH).wait_recv()
        for cp in sent:
            cp.wait_send()
        for cp in started:
            cp.wait()

    return pl.kernel(
        body, out_type=out_type, mesh=plsc.ScalarSubcoreMesh(axis_name="seq", num_cores=1),
        scratch_types=[pltpu.SemaphoreType.DMA((nt,)), pltpu.SemaphoreType.DMA((nt, 3)), pltpu.SemaphoreType.DMA((nt, 3)),
                       pltpu.SemaphoreType.DMA((nt, 3)), pltpu.SemaphoreType.DMA((nt, 3))],
        compiler_params=pltpu.CompilerParams(collective_id=cid), name=name,
    )(*[s for (s, _, _) in items])


def _slot2(ref, kind, j, n):
    if kind == "col":
        return ref.at[:, pl.ds(pl.multiple_of(j * n, n), n)]
    return ref.at[pl.ds(pl.multiple_of(j * n, n), n), :]


def _rs_chips(parts, kinds):
    nt = len(parts)
    flat = [p for per_layer in parts for p in per_layer]
    nm = len(flat)
    out_shape = []
    for per_layer, k in zip(parts, kinds):
        r, c = per_layer[0].shape
        ps = (r, c // 4) if k == "col" else (r // 4, c)
        L = len(per_layer)
        out_shape += [jax.ShapeDtypeStruct((L,) + ps, BF), jax.ShapeDtypeStruct((3, L) + ps, BF)]

    def body(*refs):
        g = refs[:nm]
        outs = refs[nm:nm + 2 * nt]
        loc, ssem, rsem = refs[nm + 2 * nt:]
        x, y, c, chips = _place()
        own = 2 * x + y
        cps = []
        m = 0
        for t in range(nt):
            k = kinds[t]
            own_o, got_o = outs[2 * t], outs[2 * t + 1]
            for l in range(len(parts[t])):
                n = g[m].shape[1] // 4 if k == "col" else g[m].shape[0] // 4
                lc = pltpu.make_async_copy(_slot2(g[m], k, own, n), own_o.at[l], loc.at[m])
                lc.start()
                cps.append(lc)
                for p, (px, py) in enumerate(chips):
                    cp = pltpu.make_async_remote_copy(
                        src_ref=_slot2(g[m], k, 2 * px + py, n), dst_ref=got_o.at[p, l],
                        send_sem=ssem.at[m, p], recv_sem=rsem.at[m, p], device_id=(px, py, c), device_id_type=MESH)
                    cp.start()
                    cps.append(cp)
                m += 1
        for cp in cps:
            cp.wait()

    hbm = pl.BlockSpec(memory_space=pl.ANY)
    return pl.pallas_call(
        body, out_shape=out_shape, in_specs=[hbm] * nm, out_specs=[hbm] * (2 * nt),
        scratch_shapes=[pltpu.SemaphoreType.DMA((nm,)), pltpu.SemaphoreType.DMA((nm, 3)), pltpu.SemaphoreType.DMA((nm, 3))],
        name="reduce_chips",
    )(*flat)


def _finish_share(name, own, got, kind, c_arr):
    L, r, c = own.shape
    tr = _pick(r, 128 if kind == "col" else 256)
    nb = r // tr
    nq = L * nb
    own2 = own.reshape(L * r, c)
    got2 = got.reshape(3 * L * r, c)
    pick = lambda h, q: q * (1 - h) + (nq - 1) * h
    in_specs = [pl.BlockSpec((tr, c), lambda h, q, cc: (pick(h, q), 0))]
    in_specs += [pl.BlockSpec((tr, c), functools.partial(lambda h, q, cc, p: (p * nq + pick(h, q), 0), p=p)) for p in range(3)]
    if kind == "col":
        out_sd = (L, 2, r, c)
        o_spec = pl.BlockSpec((None, 2, tr, c), lambda h, q, cc: ((q * h) // nb, 0, (q * h) % nb, 0))
    else:
        out_sd = (L * r, 2 * c)
        o_spec = pl.BlockSpec((tr, 2 * c), lambda h, q, cc: (q * h, 0))

    def kern(c_ref, o_ref, g0, g1, g2, out_ref, mine, recv, ssem, rsem):
        h, q = pl.program_id(0), pl.program_id(1)
        x, y, cc, _ = _place()

        def swap(qq):
            return pltpu.make_async_remote_copy(src_ref=mine.at[qq], dst_ref=recv.at[qq], send_sem=ssem.at[qq],
                                                recv_sem=rsem.at[qq], device_id=(x, y, 1 - cc), device_id_type=MESH)

        @pl.when(h == 0)
        def _():
            mine[q] = ((o_ref[...].astype(F32) + g0[...].astype(F32)) + g1[...].astype(F32)) + g2[...].astype(F32)
            swap(q).start()

        @pl.when(h == 1)
        def _():
            swap(q).wait()
            a, b = mine[q], recv[q]
            first = c_ref[0] == 0
            lo, hi = jnp.where(first, a, b), jnp.where(first, b, a)
            if kind == "col":
                out_ref[0] = lo
                out_ref[1] = hi
            else:
                out_ref[:, :c] = lo
                out_ref[:, c:] = hi

    est = 2 * nq * tr * c * 4 + 6 * tr * c * 4 + 8 * tr * c * 2
    full = pl.pallas_call(
        kern,
        grid_spec=pltpu.PrefetchScalarGridSpec(
            num_scalar_prefetch=1, grid=(2, nq), in_specs=in_specs, out_specs=o_spec,
            scratch_shapes=[pltpu.VMEM((nq, tr, c), F32), pltpu.VMEM((nq, tr, c), F32),
                            pltpu.SemaphoreType.DMA((nq,)), pltpu.SemaphoreType.DMA((nq,))]),
        out_shape=jax.ShapeDtypeStruct(out_sd, F32), name=name,
        compiler_params=pltpu.CompilerParams(dimension_semantics=("arbitrary", "arbitrary"),
                                             vmem_limit_bytes=int(min(VMEM_CAP_BYTES, est + (12 << 20)))),
    )(c_arr, own2, got2, got2, got2)
    return full.reshape(L, 2 * r, c) if kind == "col" else full.reshape(L, r, 2 * c)


def _small_allreduce(buf, name):
    R = buf.shape[0]

    def body(x_ref, o_ref, slots, ssem, rsem):
        x, y, c, _ = _place()
        me = 4 * x + 2 * y + c
        slots[0] = x_ref[...]
        cps = []
        for k in range(1, 8):
            bx, by, bc = (k >> 2) & 1, (k >> 1) & 1, k & 1
            peer = (1 - x if bx else x, 1 - y if by else y, 1 - c if bc else c)
            cp = pltpu.make_async_remote_copy(src_ref=x_ref, dst_ref=slots.at[k], send_sem=ssem.at[k - 1],
                                              recv_sem=rsem.at[k - 1], device_id=peer, device_id_type=MESH)
            cp.start()
            cps.append(cp)
        for cp in cps:
            cp.wait()
        acc = slots[jnp.bitwise_xor(me, 0)]
        for d in range(1, 8):
            acc = acc + slots[jnp.bitwise_xor(me, d)]
        o_ref[...] = acc

    vm = pl.BlockSpec(memory_space=pltpu.VMEM)
    return pl.pallas_call(
        body, out_shape=jax.ShapeDtypeStruct(buf.shape, F32), in_specs=[vm], out_specs=vm,
        scratch_shapes=[pltpu.VMEM((8, R, LANES), F32), pltpu.SemaphoreType.DMA((7,)), pltpu.SemaphoreType.DMA((7,))],
        name=name,
        compiler_params=pltpu.CompilerParams(vmem_limit_bytes=int(min(VMEM_CAP_BYTES, 12 * R * LANES * 4 + (8 << 20)))),
    )(buf)


def _pack(arrs, rows_total):
    rows = []
    for a in arrs:
        f = a.reshape(-1).astype(F32)
        n = -(-f.shape[0] // LANES) * LANES
        rows.append(jnp.pad(f, (0, n - f.shape[0])).reshape(-1, LANES))
    buf = jnp.concatenate(rows, axis=0)
    return jnp.pad(buf, ((0, rows_total - buf.shape[0]), (0, 0)))


def _unpack(buf, shapes):
    out, r = [], 0
    for s in shapes:
        n = 1
        for d in s:
            n *= d
        nr = -(-n // LANES)
        out.append(buf[r:r + nr].reshape(-1)[:n].reshape(s))
        r += nr
    return out


def _rows_needed(shapes):
    tot = 0
    for s in shapes:
        n = 1
        for d in s:
            n *= d
        tot += -(-n // LANES)
    return -(-tot // 8) * 8


def _adam(w, g, m, v):
    m = ADAM_B1 * m + (1.0 - ADAM_B1) * g
    v = ADAM_B2 * v + (1.0 - ADAM_B2) * jnp.square(g)
    m_hat = m / (1.0 - ADAM_B1 ** ADAM_STEP)
    v_hat = v / (1.0 - ADAM_B2 ** ADAM_STEP)
    delta = -ADAM_LR * (m_hat / (jnp.sqrt(v_hat) + ADAM_EPS) + ADAM_WD * w)
    return delta, m, v


def _adam_call(name, w2, g2, m2, v2, tr):
    def fn(rv, cv):
        return list(_adam(*rv)), []

    width = w2.shape[1]
    return _rowcall(name, fn, [(w2, 0, width), (g2, 0, width), (m2, 0, width), (v2, 0, width)], [],
                    [(width, F32)] * 3, [], tr)


def kernel(x, mem, mem_norm, lb_logits, ffn1_norm, ffn1_w_in, ffn1_w_out, mix_norm, mem_w_kv, hgrn_w_in, hgrn_gnorm, hgrn_w_out, gmlp_w_in, gmlp_ln_g, gmlp_ln_b, gmlp_w_s, gmlp_b_s, gmlp_w_out, ffn2_norm, ffn2_w_in, ffn2_w_out, final_norm, loss_target, m_mem_norm, m_lb_logits, m_ffn1_norm, m_ffn1_w_in, m_ffn1_w_out, m_mix_norm, m_mem_w_kv, m_hgrn_w_in, m_hgrn_gnorm, m_hgrn_w_out, m_gmlp_w_in, m_gmlp_ln_g, m_gmlp_ln_b, m_gmlp_w_s, m_gmlp_b_s, m_gmlp_w_out, m_ffn2_norm, m_ffn2_w_in, m_ffn2_w_out, m_final_norm, v_mem_norm, v_lb_logits, v_ffn1_norm, v_ffn1_w_in, v_ffn1_w_out, v_mix_norm, v_mem_w_kv, v_hgrn_w_in, v_hgrn_gnorm, v_hgrn_w_out, v_gmlp_w_in, v_gmlp_ln_g, v_gmlp_ln_b, v_gmlp_w_s, v_gmlp_b_s, v_gmlp_w_out, v_ffn2_norm, v_ffn2_w_in, v_ffn2_w_out, v_final_norm):
    bl, seq, D = x.shape
    T = bl * seq
    mem_len = mem.shape[1]
    chip = 2 * lax.axis_index("x") + lax.axis_index("y")
    c_arr = lax.axis_index("c").astype(jnp.int32).reshape(1)
    TR = 256

    big = [("ffn1_w_in", ffn1_w_in, "col"), ("ffn1_w_out", ffn1_w_out, "row"), ("mem_w_kv", mem_w_kv, "col"),
           ("hgrn_w_in", hgrn_w_in, "col"), ("hgrn_w_out", hgrn_w_out, "row"), ("gmlp_w_in", gmlp_w_in, "col"),
           ("gmlp_w_out", gmlp_w_out, "row"), ("ffn2_w_in", ffn2_w_in, "col"), ("ffn2_w_out", ffn2_w_out, "row")]
    kinds = [k for (_, _, k) in big]
    shards_bf = []
    for nm, w, _ in big:
        L, r, c = w.shape
        (wb,) = _rowcall("cast_" + nm, lambda rv, cv: ([rv[0]], []), [(w.reshape(L * r, c), 0, c)], [], [(c, BF)], [], 512)
        shards_bf.append(wb.reshape(L, r, c))
    sb = dict(zip([nm for (nm, _, _) in big], shards_bf))
    groups = [[("ffn1_w_in", 0), ("ffn1_w_out", 0)],
              [("mem_w_kv", None), ("hgrn_w_in", None), ("hgrn_w_out", None)],
              [("ffn2_w_in", 0), ("ffn2_w_out", 0)],
              [("ffn1_w_in", 1), ("ffn1_w_out", 1)],
              [("gmlp_w_in", None), ("gmlp_w_out", None)],
              [("ffn2_w_in", 1), ("ffn2_w_out", 1)]]
    kind_of = {nm: k for (nm, _, k) in big}
    gathered = {nm: [None, None] for nm in ("ffn1_w_in", "ffn1_w_out", "ffn2_w_in", "ffn2_w_out")}
    for gi, grp in enumerate(groups):
        outs = _allgather_seq("gather_%d" % gi, [(sb[nm], kind_of[nm], l) for (nm, l) in grp], gi)
        for (nm, l), o in zip(grp, outs):
            if l is None:
                gathered[nm] = o
            else:
                gathered[nm][l] = o

    ln_w = GM_GROUPS * GM_GROUP_DIM
    placed = lax.dynamic_update_slice(jnp.zeros((8, ln_w), F32), jnp.concatenate([0.5 * gmlp_ln_g, 0.5 * gmlp_ln_b], axis=0),
                                      (0, chip * gmlp_ln_g.shape[1]))
    ln_full = _small_allreduce(placed.reshape(16, LANES), "gather_ln").reshape(8, ln_w)
    ln_g_full, ln_b_full = ln_full[0:1], ln_full[1:2]

    def rms_fwd(name, xin, g):
        (h,) = _rowcall(name, lambda rv, cv: ([_rmsnorm(rv[0], cv[0])], []), [(xin, 0, D)], [g.reshape(1, D)], [(D, BF)], [], TR)
        return h

    def ffn_fwd(tag, xin, g, w_in, w_out, layer):
        dff = w_out[layer].shape[1]
        h = rms_fwd("rms_" + tag, xin, g)
        z = _mm("ffn_in_" + tag, h, w_in[layer], "nn", BF, 512, 512, D, b_lead=0)
        (a,) = _rowcall("swiglu_" + tag, lambda rv, cv: ([_silu(rv[0].astype(F32)) * rv[1].astype(F32)], []),
                        [(z, 0, dff), (z, 1, dff)], [], [(dff, BF)], [], TR)
        xo = _mm("ffn_out_" + tag, a, w_out[layer], "nn", F32, 512, 1024, dff, scale=0.5, res=xin, b_lead=0)
        return xo, (xin, h, z, a)

    def ffn_bwd(tag, dxo, saved, g, w_in, w_out, layer):
        xin, h, z, a = saved
        dff = w_out[layer].shape[1]
        da = _mm("ffn_da_" + tag, dxo, w_out[layer], "nt", BF, 512, dff // 2, D, scale=0.5, b_lead=0)
        dw_out = _mm_tn_pair("ffn_dwo_" + tag, a, dxo, "row", c_arr, dff // 2, 512, scale=0.5)

        def sw_bwd(rv, cv):
            gt, up, d = rv[0].astype(F32), rv[1].astype(F32), rv[2].astype(F32)
            _, vjp = jax.vjp(lambda p, q: _silu(p) * q, gt, up)
            dg, du = vjp(d)
            return [jnp.concatenate([dg, du], axis=1)], []

        (dz,) = _rowcall("swiglu_bwd_" + tag, sw_bwd, [(z, 0, dff), (z, 1, dff), (da, 0, dff)], [], [(2 * dff, BF)], [], TR)
        dh = _mm("ffn_dh_" + tag, dz, w_in[layer], "nt", F32, 512, 512, dff, b_lead=0)
        dw_in = _mm_tn_pair("ffn_dwi_" + tag, h, dz, "col", c_arr, dff, 512)
        dx, dg = rms_bwd("rms_bwd_" + tag, xin, g, dh, dxo)
        return dx, dg, dw_in, dw_out

    def rms_bwd(name, xin, g, dh, dres):
        def fn(rv, cv):
            _, vjp = jax.vjp(_rmsnorm, rv[0], cv[0])
            dx, dg = vjp(rv[1])
            if dres is not None:
                dx = dx + rv[2]
            return [dx], [dg]

        rows = [(xin, 0, D), (dh, 0, D)] + ([(dres, 0, D)] if dres is not None else [])
        dx, dg = _rowcall(name, fn, rows, [g.reshape(1, D)], [(D, F32)], [((1, D), F32)], TR)
        return dx, dg

    x0 = x.reshape(T, D)
    tgt = loss_target.reshape(T, D)
    mem2 = mem.reshape(bl * mem_len, D)
    memn = rms_fwd("rms_mem", mem2, mem_norm)
    kv = [_mm("kv_%d" % i, memn, gathered["mem_w_kv"], "nn", F32, 512, 512, D, b_lead=i) for i in range(2)]

    x1, sv_f10 = ffn_fwd("f1l0", x0, ffn1_norm[0], gathered["ffn1_w_in"], gathered["ffn1_w_out"], 0)
    h_m0 = rms_fwd("rms_mix0", x1, mix_norm[0])
    z_m0 = _mm("mix_in_0", h_m0, gathered["hgrn_w_in"], "nn", F32, 512, 512, D, b_lead=0)
    nc0 = seq // HG_CHUNK
    cat0, stash0 = _hgrn_fwd(z_m0, lb_logits, hgrn_gnorm, kv[0], bl, nc0)
    x2 = _mm("mix_out_0", cat0, gathered["hgrn_w_out"], "nn", F32, 512, 1024, cat0.shape[1], res=x1, b_lead=0)
    x3, sv_f20 = ffn_fwd("f2l0", x2, ffn2_norm[0], gathered["ffn2_w_in"], gathered["ffn2_w_out"], 0)
    x4, sv_f11 = ffn_fwd("f1l1", x3, ffn1_norm[1], gathered["ffn1_w_in"], gathered["ffn1_w_out"], 1)
    h_m1 = rms_fwd("rms_mix1", x4, mix_norm[1])
    z_m1 = _mm("mix_in_1", h_m1, gathered["gmlp_w_in"], "nn", F32, 512, 512, D, b_lead=0)
    nc1 = seq // GM_CHUNK
    w_s, b_s = gmlp_w_s[0], gmlp_b_s[0]
    cat1 = _gmlp_fwd(z_m1, ln_g_full, ln_b_full, w_s, b_s, kv[1], bl, nc1)
    x5 = _mm("mix_out_1", cat1, gathered["gmlp_w_out"], "nn", F32, 512, 1024, cat1.shape[1], res=x4, b_lead=0)
    x6, sv_f21 = ffn_fwd("f2l1", x5, ffn2_norm[1], gathered["ffn2_w_in"], gathered["ffn2_w_out"], 1)

    def head(rv, cv):
        def f(xx, gg):
            err = _rmsnorm(xx, gg) - rv[1]
            return 0.5 * jnp.sum(jnp.mean(err * err, axis=-1, keepdims=True), axis=0, keepdims=True)

        ls, vjp = jax.vjp(f, rv[0], cv[0])
        dx, dg = vjp(jnp.ones((1, 1), F32))
        return [dx], [dg, jnp.broadcast_to(ls, (1, 128))]

    dx6, d_final, loss_part = _rowcall("loss_head", head, [(x6, 0, D), (tgt, 0, D)], [final_norm.reshape(1, D)],
                                       [(D, F32)], [((1, D), F32), ((1, 128), F32)], TR)

    dx5, dg_f21, dwi_f21, dwo_f21 = ffn_bwd("f2l1", dx6, sv_f21, ffn2_norm[1], gathered["ffn2_w_in"], gathered["ffn2_w_out"], 1)
    dcat1 = _mm("mix_dcat_1", dx5, gathered["gmlp_w_out"], "nt", F32, 512, 512, D, b_lead=0)
    dwo_m1 = _mm_tn_pair("mix_dwo_1", cat1, dx5, "row", c_arr, 1024, 512)
    dz_m1, dkv1, d_lng, d_lnb, d_ws, d_bs = _gmlp_bwd(z_m1, dcat1, ln_g_full, ln_b_full, w_s, b_s, kv[1], bl, nc1)
    dh_m1 = _mm("mix_dh_1", dz_m1, gathered["gmlp_w_in"], "nt", F32, 512, 512, 1024, b_lead=0)
    dwi_m1 = _mm_tn_pair("mix_dwi_1", h_m1, dz_m1, "col", c_arr, 2560, 512)
    dx4, dg_m1 = rms_bwd("rms_bwd_mix1", x4, mix_norm[1], dh_m1, dx5)
    dx3, dg_f11, dwi_f11, dwo_f11 = ffn_bwd("f1l1", dx4, sv_f11, ffn1_norm[1], gathered["ffn1_w_in"], gathered["ffn1_w_out"], 1)

    dx2, dg_f20, dwi_f20, dwo_f20 = ffn_bwd("f2l0", dx3, sv_f20, ffn2_norm[0], gathered["ffn2_w_in"], gathered["ffn2_w_out"], 0)
    dcat0 = _mm("mix_dcat_0", dx2, gathered["hgrn_w_out"], "nt", F32, 512, 512, D, b_lead=0)
    dwo_m0 = _mm_tn_pair("mix_dwo_0", cat0, dx2, "row", c_arr, 1024, 512)
    dz_m0, dkv0, d_lb, d_gn = _hgrn_bwd(z_m0, dcat0, stash0, lb_logits, hgrn_gnorm, kv[0], bl, nc0)
    dh_m0 = _mm("mix_dh_0", dz_m0, gathered["hgrn_w_in"], "nt", F32, 512, 512, 1024, b_lead=0)
    dwi_m0 = _mm_tn_pair("mix_dwi_0", h_m0, dz_m0, "col", c_arr, 2560, 512)
    dx1, dg_m0 = rms_bwd("rms_bwd_mix0", x1, mix_norm[0], dh_m0, dx2)
    dx0, dg_f10, dwi_f10, dwo_f10 = ffn_bwd("f1l0", dx1, sv_f10, ffn1_norm[0], gathered["ffn1_w_in"], gathered["ffn1_w_out"], 0)

    dwkv = [_mm_tn_pair("kv_dw_%d" % i, memn, dkv, "col", c_arr, 1024, 512) for i, dkv in enumerate([dkv0, dkv1])]
    dmemn = _mm("kv_dx_0", dkv0, gathered["mem_w_kv"], "nt", F32, 512, 512, 1024, b_lead=0)
    dmemn = _mm("kv_dx_1", dkv1, gathered["mem_w_kv"], "nt", F32, 512, 512, 1024, res=dmemn, b_lead=1)
    _, d_memnorm = rms_bwd("rms_bwd_mem", mem2, mem_norm, dmemn, None)

    mats = [("ffn1_w_in", [dwi_f10, dwi_f11]), ("ffn1_w_out", [dwo_f10, dwo_f11]), ("mem_w_kv", dwkv),
            ("hgrn_w_in", [dwi_m0]), ("hgrn_w_out", [dwo_m0]), ("gmlp_w_in", [dwi_m1]), ("gmlp_w_out", [dwo_m1]),
            ("ffn2_w_in", [dwi_f20, dwi_f21]), ("ffn2_w_out", [dwo_f20, dwo_f21])]
    chp = _rs_chips([gl for (_, gl) in mats], kinds)
    shard_grads = [_finish_share("finish_" + nm, chp[2 * t], chp[2 * t + 1], kinds[t], c_arr)
                   for t, (nm, _) in enumerate(mats)]

    big_w = [w for (_, w, _) in big]
    big_m = [m_ffn1_w_in, m_ffn1_w_out, m_mem_w_kv, m_hgrn_w_in, m_hgrn_w_out, m_gmlp_w_in, m_gmlp_w_out, m_ffn2_w_in, m_ffn2_w_out]
    big_v = [v_ffn1_w_in, v_ffn1_w_out, v_mem_w_kv, v_hgrn_w_in, v_hgrn_w_out, v_gmlp_w_in, v_gmlp_w_out, v_ffn2_w_in, v_ffn2_w_out]
    big_out = {}
    for (nm, w, _), g, m, v in zip(big, shard_grads, big_m, big_v):
        L, r, c = w.shape
        d2, m2, v2 = _adam_call("adam_" + nm, w.reshape(L * r, c), g.reshape(L * r, c), m.reshape(L * r, c),
                                v.reshape(L * r, c), 256)
        big_out[nm] = (g, d2.reshape(w.shape), m2.reshape(w.shape), v2.reshape(w.shape))

    d_ffn1n = jnp.concatenate([dg_f10, dg_f11], axis=0)
    d_mixn = jnp.concatenate([dg_m0, dg_m1], axis=0)
    d_ffn2n = jnp.concatenate([dg_f20, dg_f21], axis=0)
    small_parts = [loss_part[:, :1], d_memnorm, d_lb, d_ffn1n, d_mixn, d_gn, d_lng, d_lnb, d_ws, d_bs, d_ffn2n, d_final]
    red_shapes = [(1,), mem_norm.shape, lb_logits.shape, ffn1_norm.shape, mix_norm.shape, hgrn_gnorm.shape, (1, ln_w), (1, ln_w),
                  gmlp_w_s.shape, gmlp_b_s.shape, ffn2_norm.shape, final_norm.shape]
    red = _small_allreduce(_pack(small_parts, _rows_needed(red_shapes)), "reduce_small")
    (loss_v, g_memn, g_lb, g_f1n, g_mixn, g_gn, g_lng_full, g_lnb_full, g_ws, g_bs, g_f2n, g_fin) = _unpack(red, red_shapes)
    lsh = gmlp_ln_g.shape[1]
    g_lng = lax.dynamic_slice(g_lng_full, (0, chip * lsh), (1, lsh))
    g_lnb = lax.dynamic_slice(g_lnb_full, (0, chip * lsh), (1, lsh))
    small_w = [mem_norm, lb_logits, ffn1_norm, mix_norm, hgrn_gnorm, gmlp_ln_g, gmlp_ln_b, gmlp_w_s, gmlp_b_s, ffn2_norm, final_norm]
    small_g = [g_memn, g_lb, g_f1n, g_mixn, g_gn, g_lng, g_lnb, g_ws, g_bs, g_f2n, g_fin]
    small_m = [m_mem_norm, m_lb_logits, m_ffn1_norm, m_mix_norm, m_hgrn_gnorm, m_gmlp_ln_g, m_gmlp_ln_b, m_gmlp_w_s, m_gmlp_b_s, m_ffn2_norm, m_final_norm]
    small_v = [v_mem_norm, v_lb_logits, v_ffn1_norm, v_mix_norm, v_hgrn_gnorm, v_gmlp_ln_g, v_gmlp_ln_b, v_gmlp_w_s, v_gmlp_b_s, v_ffn2_norm, v_final_norm]
    sshapes = [w.shape for w in small_w]
    nrow = _rows_needed(sshapes)
    d_p, m_p, v_p = _adam_call("adam_small", _pack(small_w, nrow), _pack(small_g, nrow), _pack(small_m, nrow), _pack(small_v, nrow), nrow)
    s_delta, s_m, s_v = _unpack(d_p, sshapes), _unpack(m_p, sshapes), _unpack(v_p, sshapes)
    small_names = ["mem_norm", "lb_logits", "ffn1_norm", "mix_norm", "hgrn_gnorm", "gmlp_ln_g", "gmlp_ln_b", "gmlp_w_s", "gmlp_b_s", "ffn2_norm", "final_norm"]
    small_out = {nm: (g.reshape(w.shape), d, m, v) for nm, w, g, d, m, v in zip(small_names, small_w, small_g, s_delta, s_m, s_v)}

    order = ["mem_norm", "lb_logits", "ffn1_norm", "ffn1_w_in", "ffn1_w_out", "mix_norm", "mem_w_kv", "hgrn_w_in", "hgrn_gnorm",
             "hgrn_w_out", "gmlp_w_in", "gmlp_ln_g", "gmlp_ln_b", "gmlp_w_s", "gmlp_b_s", "gmlp_w_out", "ffn2_norm", "ffn2_w_in",
             "ffn2_w_out", "final_norm"]
    allo = {**big_out, **small_out}
    grad_x = dx0.reshape(x.shape)
    return (loss_v.reshape(()), grad_x, *[allo[n][0] for n in order], *[allo[n][1] for n in order],
            *[allo[n][2] for n in order], *[allo[n][3] for n in order])
```

```python
import functools

import jax
import jax.numpy as jnp
from jax import lax
from jax.experimental import pallas as pl
from jax.experimental.pallas import tpu as pltpu
from jax.experimental.pallas import tpu_sc as plsc

BF = jnp.bfloat16
F32 = jnp.float32
MESH = pl.DeviceIdType.MESH

EPS = 1e-6
D_MODEL = 1024
HG_HEADS = 8
HG_DIM = 128
HG_CHUNK = 64
GM_CHUNK = 128
GM_GROUPS = 8
GM_GROUP_DIM = 256
XA_HEADS = 4
XA_DIM = 256
ADAM_LR = 0.001
ADAM_B1 = 0.9
ADAM_B2 = 0.999
ADAM_EPS = 1e-08
ADAM_WD = 0.01
ADAM_STEP = 10

VMEM_CAP_BYTES = 60 * 1024 * 1024
LANES = 1024


def _pick(n, cap, mult=16):
    if n <= cap:
        return n
    for d in range(cap - cap % mult, 0, -mult):
        if n % d == 0:
            return d
    raise ValueError((n, cap, mult))


def _dg(a, b, ca, cb):
    return lax.dot_general(a.astype(BF), b.astype(BF), (((ca,), (cb,)), ((), ())), preferred_element_type=F32)


@jax.custom_vjp
def dot_nn(a, b):
    return _dg(a, b, 1, 0)


def _nn_fwd(a, b):
    return _dg(a, b, 1, 0), (a, b)


def _nn_bwd(r, g):
    a, b = r
    return _dg(g, b, 1, 1), _dg(a, g, 0, 0)


dot_nn.defvjp(_nn_fwd, _nn_bwd)


@jax.custom_vjp
def dot_nt(a, b):
    return _dg(a, b, 1, 1)


def _nt_fwd(a, b):
    return _dg(a, b, 1, 1), (a, b)


def _nt_bwd(r, g):
    a, b = r
    return _dg(g, b, 1, 0), _dg(g, a, 0, 0)


dot_nt.defvjp(_nt_fwd, _nt_bwd)


@jax.custom_vjp
def dot_tn(a, b):
    return _dg(a, b, 0, 0)


def _tn_fwd(a, b):
    return _dg(a, b, 0, 0), (a, b)


def _tn_bwd(r, g):
    a, b = r
    return _dg(b, g, 1, 1), _dg(a, g, 1, 0)


dot_tn.defvjp(_tn_fwd, _tn_bwd)


def _rmsnorm(x, g):
    return x * lax.rsqrt(jnp.mean(x * x, axis=-1, keepdims=True) + EPS) * g


def _silu(x):
    return x * jax.nn.sigmoid(x)


def _gelu(x):
    return 0.5 * x * (1.0 + lax.erf(x * (0.5 ** 0.5)))


def _softmax_last(s):
    m = lax.stop_gradient(jnp.max(s, axis=-1, keepdims=True))
    e = jnp.exp(s - m)
    return e / jnp.sum(e, axis=-1, keepdims=True)


def _tril(n):
    r = lax.broadcasted_iota(jnp.int32, (n, n), 0)
    c = lax.broadcasted_iota(jnp.int32, (n, n), 1)
    return r >= c


def _cumsum_rows(l):
    n = l.shape[0]
    return lax.dot_general(_tril(n).astype(F32), l, (((1,), (0,)), ((), ())),
                           precision=lax.Precision.HIGHEST, preferred_element_type=F32)


def _attention(zx, mk, mv):
    s = dot_nt(zx, mk) * (XA_DIM ** -0.5)
    return dot_nn(_softmax_last(s), mv)


def _hgrn_head(zq, zf, zi, zg, l0, l1, l2, gn, S):
    m = lax.stop_gradient(jnp.maximum(jnp.maximum(l0, l1), l2))
    e0 = jnp.exp(l0 - m)
    lb = e0 / (e0 + jnp.exp(l1 - m) + jnp.exp(l2 - m))
    q = _silu(zq)
    f = lb + (1.0 - lb) * jax.nn.sigmoid(zf)
    k = 1.0 - f
    b = _cumsum_rows(jnp.log(f))
    b_last = b[HG_CHUNK - 1:HG_CHUNK, :]
    q_dec = q * jnp.exp(b)
    k_inv = k * jnp.exp(-b)
    a = jnp.where(_tril(HG_CHUNK), dot_nt(q_dec, k_inv), 0.0)
    o = dot_nn(a, zi) + dot_nn(q_dec, S)
    S_new = jnp.exp(b_last).reshape(HG_DIM, 1) * S + dot_tn(k * jnp.exp(b_last - b), zi)
    o = _rmsnorm(o, gn) * _silu(zg)
    return o, S_new


def _hgrn_block(zq, zf, zi, zg, zx, l0, l1, l2, gn, mk, mv, S):
    outs, s_new = [], []
    for h in range(HG_HEADS):
        o, sn = _hgrn_head(zq[h], zf[h], zi[h], zg[h], l0[h], l1[h], l2[h], gn, S[h])
        outs.append(o)
        s_new.append(sn)
    for a in range(XA_HEADS):
        outs.append(_attention(zx[a], mk[a], mv[a]))
    return outs, s_new


def _gmlp_block(zu, zv, zx, lng, lnb, ws, bs, mk, mv):
    gv = [_gelu(v) for v in zv]
    width = GM_GROUPS * GM_GROUP_DIM
    mu = sum(jnp.sum(g, axis=-1, keepdims=True) for g in gv) / width
    xc = [g - mu for g in gv]
    var = sum(jnp.sum(c * c, axis=-1, keepdims=True) for c in xc) / width
    r = lax.rsqrt(var + EPS)
    outs = []
    for g in range(GM_GROUPS):
        v = xc[g] * r * lng[g] + lnb[g]
        w = jnp.where(_tril(GM_CHUNK), ws[g], 0.0)
        mixed = dot_nn(w, v) + bs[g].reshape(GM_CHUNK, 1)
        outs.append(_gelu(zu[g]) * mixed)
    for a in range(XA_HEADS):
        outs.append(_attention(zx[a], mk[a], mv[a]))
    return outs


def _rowcall(name, fn, rows, consts, row_outs, acc_outs, tr):
    nrows = rows[0][0].shape[0]
    tr = _pick(nrows, tr)
    n_r, n_c, n_ro, n_ao = len(rows), len(consts), len(row_outs), len(acc_outs)

    def kern(*refs):
        rv = [r[...] for r in refs[:n_r]]
        cv = [r[...] for r in refs[n_r:n_r + n_c]]
        ro_refs = refs[n_r + n_c:n_r + n_c + n_ro]
        ao_refs = refs[n_r + n_c + n_ro:]
        ro, ao = fn(rv, cv)
        for ref, v in zip(ro_refs, ro):
            ref[...] = v.astype(ref.dtype)
        if n_ao:
            @pl.when(pl.program_id(0) == 0)
            def _():
                for ref in ao_refs:
                    ref[...] = jnp.zeros(ref.shape, ref.dtype)

            for ref, v in zip(ao_refs, ao):
                ref[...] += v.astype(ref.dtype)

    in_specs = [pl.BlockSpec((tr, w), functools.partial(lambda i, cb: (i, cb), cb=cb)) for (_, cb, w) in rows]
    in_specs += [pl.BlockSpec(c.shape, lambda i: (0, 0)) for c in consts]
    out_specs = [pl.BlockSpec((tr, w), lambda i: (i, 0)) for (w, _) in row_outs]
    out_specs += [pl.BlockSpec(s, lambda i: (0, 0)) for (s, _) in acc_outs]
    out_shape = [jax.ShapeDtypeStruct((nrows, w), dt) for (w, dt) in row_outs]
    out_shape += [jax.ShapeDtypeStruct(s, dt) for (s, dt) in acc_outs]
    est = sum(tr * w * a.dtype.itemsize for (a, _, w) in rows) + sum(tr * w * jnp.dtype(dt).itemsize for (w, dt) in row_outs)
    est += sum(c.size * c.dtype.itemsize for c in consts)
    outs = pl.pallas_call(
        kern, grid=(nrows // tr,), in_specs=in_specs, out_specs=out_specs, out_shape=out_shape, name=name,
        compiler_params=pltpu.CompilerParams(dimension_semantics=("arbitrary",),
                                             vmem_limit_bytes=int(min(VMEM_CAP_BYTES, 6 * est + (16 << 20)))),
    )(*[a for (a, _, _) in rows], *consts)
    return outs


def _mm(name, a, b, mode, out_dtype, tm, tn, tk, scale=1.0, res=None, a_lead=None, b_lead=None):
    ash = a.shape[-2:]
    bsh = b.shape[-2:]
    if mode == "nn":
        (M, K), (K2, N) = ash, bsh
    elif mode == "nt":
        (M, K), (N, K2) = ash, bsh
    else:
        (K, M), (K2, N) = ash, bsh
    assert K == K2, (name, a.shape, b.shape)
    tm, tn, tk = min(tm, M), min(tn, N), min(tk, K)
    assert M % tm == 0 and N % tn == 0 and K % tk == 0, (name, M, N, K, tm, tn, tk)
    nk = K // tk
    dims = {"nn": (1, 0), "nt": (1, 1), "tn": (0, 0)}[mode]

    def lead(spec_shape, index_fn, lead_idx):
        if lead_idx is None:
            return pl.BlockSpec(spec_shape, index_fn)
        return pl.BlockSpec((None,) + spec_shape, lambda i, j, k: (lead_idx,) + index_fn(i, j, k))

    if mode == "tn":
        a_spec = lead((tk, tm), lambda i, j, k: (k, i), a_lead)
    else:
        a_spec = lead((tm, tk), lambda i, j, k: (i, k), a_lead)
    if mode == "nt":
        b_spec = lead((tn, tk), lambda i, j, k: (j, k), b_lead)
    else:
        b_spec = lead((tk, tn), lambda i, j, k: (k, j), b_lead)
    o_spec = pl.BlockSpec((tm, tn), lambda i, j, k: (i, j))
    has_res = res is not None

    def kern(*refs):
        a_ref, b_ref = refs[0], refs[1]
        res_ref = refs[2] if has_res else None
        o_ref = refs[3] if has_res else refs[2]
        acc_ref = refs[-1] if nk > 1 else None
        p = lax.dot_general(a_ref[...].astype(BF), b_ref[...].astype(BF), (((dims[0],), (dims[1],)), ((), ())),
                            preferred_element_type=F32)

        def finish(v):
            if scale != 1.0:
                v = v * scale
            if has_res:
                v = res_ref[...] + v
            o_ref[...] = v.astype(o_ref.dtype)

        if nk == 1:
            finish(p)
        else:
            k = pl.program_id(2)

            @pl.when(k == 0)
            def _():
                acc_ref[...] = p

            @pl.when(k > 0)
            def _():
                acc_ref[...] += p

            @pl.when(k == nk - 1)
            def _():
                finish(acc_ref[...])

    ins = [a, b] + ([res] if has_res else [])
    in_specs = [a_spec, b_spec] + ([o_spec] if has_res else [])
    est = tm * tk * a.dtype.itemsize + tk * tn * b.dtype.itemsize + tm * tn * (jnp.dtype(out_dtype).itemsize + 8)
    return pl.pallas_call(
        kern, grid=(M // tm, N // tn, nk), in_specs=in_specs, out_specs=o_spec,
        out_shape=jax.ShapeDtypeStruct((M, N), out_dtype),
        scratch_shapes=[pltpu.VMEM((tm, tn), F32)] if nk > 1 else [],
        name=name,
        compiler_params=pltpu.CompilerParams(dimension_semantics=("parallel", "parallel", "arbitrary"),
                                             vmem_limit_bytes=int(min(VMEM_CAP_BYTES, 3 * est + (16 << 20)))),
    )(*ins)


def _mm_tn_pair(name, a, b, kind, c_arr, tq, tk, scale=1.0):
    T, M = a.shape
    _, N = b.shape
    tk = min(tk, T)
    assert T % tk == 0
    nk = T // tk
    if kind == "col":
        hm = M // 2
        assert N % tq == 0
        nq = N // tq
        tile = (hm, tq)
        a_spec = pl.BlockSpec((tk, hm), lambda h, q, k, c: (k, jnp.bitwise_xor(h, 1 - c[0])))
        b_spec = pl.BlockSpec((tk, tq), lambda h, q, k, c: (k, q))
        o_spec = pl.BlockSpec(tile, lambda h, q, k, c: (0, q * h))
        out_sd = (hm, N)
    else:
        hn = N // 2
        assert M % tq == 0
        nq = M // tq
        tile = (tq, hn)
        a_spec = pl.BlockSpec((tk, tq), lambda h, q, k, c: (k, q))
        b_spec = pl.BlockSpec((tk, hn), lambda h, q, k, c: (k, jnp.bitwise_xor(h, 1 - c[0])))
        o_spec = pl.BlockSpec(tile, lambda h, q, k, c: (q * h, 0))
        out_sd = (M, hn)

    def kern(c_ref, a_ref, b_ref, o_ref, acc, stage, recv, ssem, rsem):
        h, q, k = pl.program_id(0), pl.program_id(1), pl.program_id(2)
        x, y, c, _ = _place()
        p = lax.dot_general(a_ref[...].astype(BF), b_ref[...].astype(BF), (((0,), (0,)), ((), ())), preferred_element_type=F32)

        @pl.when(k == 0)
        def _():
            acc[...] = p

        @pl.when(k > 0)
        def _():
            acc[...] += p

        def send(slot, qq):
            return pltpu.make_async_remote_copy(src_ref=stage.at[slot], dst_ref=recv.at[qq], send_sem=ssem.at[slot],
                                                recv_sem=rsem.at[qq], device_id=(x, y, 1 - c), device_id_type=MESH)

        last = k == nk - 1

        @pl.when(jnp.logical_and(last, h == 0))
        def _():
            slot = q % 2

            @pl.when(q >= 2)
            def _():
                send(slot, q).wait_send()

            stage[slot] = (acc[...] * scale).astype(BF)
            send(slot, q).start()

        @pl.when(jnp.logical_and(last, h == 1))
        def _():
            @pl.when(q == 0)
            def _():
                for s in range(min(nq, 2)):
                    send(s, 0).wait_send()

            send(0, q).wait_recv()
            o_ref[...] = (acc[...] * scale + recv[q].astype(F32)).astype(o_ref.dtype)

    tb = tile[0] * tile[1]
    est = tb * (4 + 2 * 2 + nq * 2 + 2 * 2) + 2 * tk * (a_spec.block_shape[1] + b_spec.block_shape[1]) * 2 * 2
    return pl.pallas_call(
        kern,
        grid_spec=pltpu.PrefetchScalarGridSpec(
            num_scalar_prefetch=1, grid=(2, nq, nk), in_specs=[a_spec, b_spec], out_specs=o_spec,
            scratch_shapes=[pltpu.VMEM(tile, F32), pltpu.VMEM((2,) + tile, BF), pltpu.VMEM((nq,) + tile, BF),
                            pltpu.SemaphoreType.DMA((2,)), pltpu.SemaphoreType.DMA((nq,))]),
        out_shape=jax.ShapeDtypeStruct(out_sd, BF), name=name,
        compiler_params=pltpu.CompilerParams(dimension_semantics=("arbitrary", "arbitrary", "arbitrary"),
                                             vmem_limit_bytes=int(min(VMEM_CAP_BYTES, est + (12 << 20)))),
    )(c_arr, a, b)


def _hgrn_pieces(z_ref):
    W = HG_HEADS * HG_DIM
    zq = [z_ref[:, h * HG_DIM:(h + 1) * HG_DIM] for h in range(HG_HEADS)]
    zf = [z_ref[:, W + h * HG_DIM:W + (h + 1) * HG_DIM] for h in range(HG_HEADS)]
    zi = [z_ref[:, 2 * W + h * HG_DIM:2 * W + (h + 1) * HG_DIM] for h in range(HG_HEADS)]
    zg = [z_ref[:, 3 * W + h * HG_DIM:3 * W + (h + 1) * HG_DIM] for h in range(HG_HEADS)]
    zx = [z_ref[:, 4 * W + a * XA_DIM:4 * W + (a + 1) * XA_DIM] for a in range(XA_HEADS)]
    return zq, zf, zi, zg, zx


def _kv_pieces(kv_ref):
    W = XA_HEADS * XA_DIM
    mk = [kv_ref[:, a * XA_DIM:(a + 1) * XA_DIM] for a in range(XA_HEADS)]
    mv = [kv_ref[:, W + a * XA_DIM:W + (a + 1) * XA_DIM] for a in range(XA_HEADS)]
    return mk, mv


def _lb_pieces(lb_ref):
    return [[lb_ref[r:r + 1, h * HG_DIM:(h + 1) * HG_DIM] for h in range(HG_HEADS)] for r in range(3)]


def _hgrn_fwd(z, lb_logits, gnorm, kv, bl, nc):
    T, zw = z.shape
    mem_len = kv.shape[0] // bl
    cat_w = HG_HEADS * HG_DIM + XA_HEADS * XA_DIM

    def kern(z_ref, lb_ref, gn_ref, kv_ref, cat_ref, st_ref, s_scr):
        @pl.when(pl.program_id(1) == 0)
        def _():
            s_scr[...] = jnp.zeros(s_scr.shape, F32)

        st_ref[...] = s_scr[...]
        zq, zf, zi, zg, zx = _hgrn_pieces(z_ref)
        mk, mv = _kv_pieces(kv_ref)
        l0, l1, l2 = _lb_pieces(lb_ref)
        S = [s_scr[h] for h in range(HG_HEADS)]
        outs, s_new = _hgrn_block(zq, zf, zi, zg, zx, l0, l1, l2, gn_ref[...], mk, mv, S)
        for h in range(HG_HEADS):
            cat_ref[:, h * HG_DIM:(h + 1) * HG_DIM] = outs[h].astype(cat_ref.dtype)
            s_scr[h] = s_new[h]
        base = HG_HEADS * HG_DIM
        for a in range(XA_HEADS):
            cat_ref[:, base + a * XA_DIM:base + (a + 1) * XA_DIM] = outs[HG_HEADS + a].astype(cat_ref.dtype)

    return pl.pallas_call(
        kern, grid=(bl, nc),
        in_specs=[pl.BlockSpec((HG_CHUNK, zw), lambda b, n: (b * nc + n, 0)),
                  pl.BlockSpec(lb_logits.shape, lambda b, n: (0, 0)),
                  pl.BlockSpec(gnorm.shape, lambda b, n: (0, 0)),
                  pl.BlockSpec((mem_len, kv.shape[1]), lambda b, n: (b, 0))],
        out_specs=[pl.BlockSpec((HG_CHUNK, cat_w), lambda b, n: (b * nc + n, 0)),
                   pl.BlockSpec((None, HG_HEADS, HG_DIM, HG_DIM), lambda b, n: (b * nc + n, 0, 0, 0))],
        out_shape=[jax.ShapeDtypeStruct((T, cat_w), BF),
                   jax.ShapeDtypeStruct((bl * nc, HG_HEADS, HG_DIM, HG_DIM), F32)],
        scratch_shapes=[pltpu.VMEM((HG_HEADS, HG_DIM, HG_DIM), F32)],
        name="hgrn_fwd",
        compiler_params=pltpu.CompilerParams(dimension_semantics=("arbitrary", "arbitrary"), vmem_limit_bytes=48 << 20),
    )(z, lb_logits, gnorm, kv)


def _hgrn_bwd(z, dcat, stash, lb_logits, gnorm, kv, bl, nc):
    T, zw = z.shape
    mem_len = kv.shape[0] // bl
    cat_w = dcat.shape[1]

    def kern(z_ref, dc_ref, st_ref, lb_ref, gn_ref, kv_ref, dz_ref, dkv_ref, dlb_ref, dgn_ref, ds_scr):
        first = jnp.logical_and(pl.program_id(0) == 0, pl.program_id(1) == 0)

        @pl.when(pl.program_id(1) == 0)
        def _():
            ds_scr[...] = jnp.zeros(ds_scr.shape, F32)
            dkv_ref[...] = jnp.zeros(dkv_ref.shape, F32)

        @pl.when(first)
        def _():
            dlb_ref[...] = jnp.zeros(dlb_ref.shape, F32)
            dgn_ref[...] = jnp.zeros(dgn_ref.shape, F32)

        zq, zf, zi, zg, zx = _hgrn_pieces(z_ref)
        mk, mv = _kv_pieces(kv_ref)
        l0, l1, l2 = _lb_pieces(lb_ref)
        S = [st_ref[h] for h in range(HG_HEADS)]
        _, vjp = jax.vjp(_hgrn_block, zq, zf, zi, zg, zx, l0, l1, l2, gn_ref[...], mk, mv, S)
        d_outs = [dc_ref[:, h * HG_DIM:(h + 1) * HG_DIM] for h in range(HG_HEADS)]
        base = HG_HEADS * HG_DIM
        d_outs += [dc_ref[:, base + a * XA_DIM:base + (a + 1) * XA_DIM] for a in range(XA_HEADS)]
        d_s = [ds_scr[h] for h in range(HG_HEADS)]
        dzq, dzf, dzi, dzg, dzx, dl0, dl1, dl2, dgn, dmk, dmv, dS = vjp((d_outs, d_s))
        W = HG_HEADS * HG_DIM
        for h in range(HG_HEADS):
            sl = slice(h * HG_DIM, (h + 1) * HG_DIM)
            dz_ref[:, sl] = dzq[h].astype(dz_ref.dtype)
            dz_ref[:, W + h * HG_DIM:W + (h + 1) * HG_DIM] = dzf[h].astype(dz_ref.dtype)
            dz_ref[:, 2 * W + h * HG_DIM:2 * W + (h + 1) * HG_DIM] = dzi[h].astype(dz_ref.dtype)
            dz_ref[:, 3 * W + h * HG_DIM:3 * W + (h + 1) * HG_DIM] = dzg[h].astype(dz_ref.dtype)
            ds_scr[h] = dS[h]
            dlb_ref[0:1, sl] += dl0[h]
            dlb_ref[1:2, sl] += dl1[h]
            dlb_ref[2:3, sl] += dl2[h]
        dgn_ref[...] += dgn
        KW = XA_HEADS * XA_DIM
        for a in range(XA_HEADS):
            dz_ref[:, 4 * W + a * XA_DIM:4 * W + (a + 1) * XA_DIM] = dzx[a].astype(dz_ref.dtype)
            dkv_ref[:, a * XA_DIM:(a + 1) * XA_DIM] += dmk[a]
            dkv_ref[:, KW + a * XA_DIM:KW + (a + 1) * XA_DIM] += dmv[a]

    rev = lambda b, n: (b * nc + (nc - 1 - n), 0)
    return pl.pallas_call(
        kern, grid=(bl, nc),
        in_specs=[pl.BlockSpec((HG_CHUNK, zw), rev),
                  pl.BlockSpec((HG_CHUNK, cat_w), rev),
                  pl.BlockSpec((None, HG_HEADS, HG_DIM, HG_DIM), lambda b, n: (b * nc + (nc - 1 - n), 0, 0, 0)),
                  pl.BlockSpec(lb_logits.shape, lambda b, n: (0, 0)),
                  pl.BlockSpec(gnorm.shape, lambda b, n: (0, 0)),
                  pl.BlockSpec((mem_len, kv.shape[1]), lambda b, n: (b, 0))],
        out_specs=[pl.BlockSpec((HG_CHUNK, zw), rev),
                   pl.BlockSpec((mem_len, kv.shape[1]), lambda b, n: (b, 0)),
                   pl.BlockSpec(lb_logits.shape, lambda b, n: (0, 0)),
                   pl.BlockSpec(gnorm.shape, lambda b, n: (0, 0))],
        out_shape=[jax.ShapeDtypeStruct((T, zw), BF), jax.ShapeDtypeStruct(kv.shape, F32),
                   jax.ShapeDtypeStruct(lb_logits.shape, F32), jax.ShapeDtypeStruct(gnorm.shape, F32)],
        scratch_shapes=[pltpu.VMEM((HG_HEADS, HG_DIM, HG_DIM), F32)],
        name="hgrn_bwd",
        compiler_params=pltpu.CompilerParams(dimension_semantics=("arbitrary", "arbitrary"), vmem_limit_bytes=56 << 20),
    )(z, dcat, stash, lb_logits, gnorm, kv)


def _gmlp_pieces(z_ref):
    W = GM_GROUPS * GM_GROUP_DIM
    zu = [z_ref[:, g * GM_GROUP_DIM:(g + 1) * GM_GROUP_DIM] for g in range(GM_GROUPS)]
    zv = [z_ref[:, W + g * GM_GROUP_DIM:W + (g + 1) * GM_GROUP_DIM] for g in range(GM_GROUPS)]
    zx = [z_ref[:, 2 * W + a * XA_DIM:2 * W + (a + 1) * XA_DIM] for a in range(XA_HEADS)]
    return zu, zv, zx


def _gmlp_params(lng_ref, lnb_ref, ws_ref, bs_ref):
    lng = [lng_ref[:, g * GM_GROUP_DIM:(g + 1) * GM_GROUP_DIM] for g in range(GM_GROUPS)]
    lnb = [lnb_ref[:, g * GM_GROUP_DIM:(g + 1) * GM_GROUP_DIM] for g in range(GM_GROUPS)]
    ws = [ws_ref[g] for g in range(GM_GROUPS)]
    bs = [bs_ref[g:g + 1, :] for g in range(GM_GROUPS)]
    return lng, lnb, ws, bs


def _gmlp_fwd(z, ln_g, ln_b, w_s, b_s, kv, bl, nc):
    T, zw = z.shape
    mem_len = kv.shape[0] // bl
    cat_w = GM_GROUPS * GM_GROUP_DIM + XA_HEADS * XA_DIM

    def kern(z_ref, lng_ref, lnb_ref, ws_ref, bs_ref, kv_ref, cat_ref):
        zu, zv, zx = _gmlp_pieces(z_ref)
        lng, lnb, ws, bs = _gmlp_params(lng_ref, lnb_ref, ws_ref, bs_ref)
        mk, mv = _kv_pieces(kv_ref)
        outs = _gmlp_block(zu, zv, zx, lng, lnb, ws, bs, mk, mv)
        for g in range(GM_GROUPS):
            cat_ref[:, g * GM_GROUP_DIM:(g + 1) * GM_GROUP_DIM] = outs[g].astype(cat_ref.dtype)
        base = GM_GROUPS * GM_GROUP_DIM
        for a in range(XA_HEADS):
            cat_ref[:, base + a * XA_DIM:base + (a + 1) * XA_DIM] = outs[GM_GROUPS + a].astype(cat_ref.dtype)

    full2 = lambda b, n: (0, 0)
    return pl.pallas_call(
        kern, grid=(bl, nc),
        in_specs=[pl.BlockSpec((GM_CHUNK, zw), lambda b, n: (b * nc + n, 0)),
                  pl.BlockSpec(ln_g.shape, full2), pl.BlockSpec(ln_b.shape, full2),
                  pl.BlockSpec(w_s.shape, lambda b, n: (0, 0, 0)), pl.BlockSpec(b_s.shape, full2),
                  pl.BlockSpec((mem_len, kv.shape[1]), lambda b, n: (b, 0))],
        out_specs=pl.BlockSpec((GM_CHUNK, cat_w), lambda b, n: (b * nc + n, 0)),
        out_shape=jax.ShapeDtypeStruct((T, cat_w), BF),
        name="gmlp_fwd",
        compiler_params=pltpu.CompilerParams(dimension_semantics=("arbitrary", "arbitrary"), vmem_limit_bytes=48 << 20),
    )(z, ln_g, ln_b, w_s, b_s, kv)


def _gmlp_bwd(z, dcat, ln_g, ln_b, w_s, b_s, kv, bl, nc):
    T, zw = z.shape
    mem_len = kv.shape[0] // bl
    cat_w = dcat.shape[1]

    def kern(z_ref, dc_ref, lng_ref, lnb_ref, ws_ref, bs_ref, kv_ref,
             dz_ref, dkv_ref, dlng_ref, dlnb_ref, dws_ref, dbs_ref):
        first = jnp.logical_and(pl.program_id(0) == 0, pl.program_id(1) == 0)

        @pl.when(pl.program_id(1) == 0)
        def _():
            dkv_ref[...] = jnp.zeros(dkv_ref.shape, F32)

        @pl.when(first)
        def _():
            dlng_ref[...] = jnp.zeros(dlng_ref.shape, F32)
            dlnb_ref[...] = jnp.zeros(dlnb_ref.shape, F32)
            dws_ref[...] = jnp.zeros(dws_ref.shape, F32)
            dbs_ref[...] = jnp.zeros(dbs_ref.shape, F32)

        zu, zv, zx = _gmlp_pieces(z_ref)
        lng, lnb, ws, bs = _gmlp_params(lng_ref, lnb_ref, ws_ref, bs_ref)
        mk, mv = _kv_pieces(kv_ref)
        _, vjp = jax.vjp(_gmlp_block, zu, zv, zx, lng, lnb, ws, bs, mk, mv)
        d_outs = [dc_ref[:, g * GM_GROUP_DIM:(g + 1) * GM_GROUP_DIM] for g in range(GM_GROUPS)]
        base = GM_GROUPS * GM_GROUP_DIM
        d_outs += [dc_ref[:, base + a * XA_DIM:base + (a + 1) * XA_DIM] for a in range(XA_HEADS)]
        dzu, dzv, dzx, dlng, dlnb, dws, dbs, dmk, dmv = vjp(d_outs)
        W = GM_GROUPS * GM_GROUP_DIM
        for g in range(GM_GROUPS):
            sl = slice(g * GM_GROUP_DIM, (g + 1) * GM_GROUP_DIM)
            dz_ref[:, sl] = dzu[g].astype(dz_ref.dtype)
            dz_ref[:, W + g * GM_GROUP_DIM:W + (g + 1) * GM_GROUP_DIM] = dzv[g].astype(dz_ref.dtype)
            dlng_ref[:, sl] += dlng[g]
            dlnb_ref[:, sl] += dlnb[g]
            dws_ref[g] += dws[g]
            dbs_ref[g:g + 1, :] += dbs[g]
        KW = XA_HEADS * XA_DIM
        for a in range(XA_HEADS):
            dz_ref[:, 2 * W + a * XA_DIM:2 * W + (a + 1) * XA_DIM] = dzx[a].astype(dz_ref.dtype)
            dkv_ref[:, a * XA_DIM:(a + 1) * XA_DIM] += dmk[a]
            dkv_ref[:, KW + a * XA_DIM:KW + (a + 1) * XA_DIM] += dmv[a]

    full2 = lambda b, n: (0, 0)
    full3 = lambda b, n: (0, 0, 0)
    blk = lambda b, n: (b * nc + n, 0)
    return pl.pallas_call(
        kern, grid=(bl, nc),
        in_specs=[pl.BlockSpec((GM_CHUNK, zw), blk), pl.BlockSpec((GM_CHUNK, cat_w), blk),
                  pl.BlockSpec(ln_g.shape, full2), pl.BlockSpec(ln_b.shape, full2),
                  pl.BlockSpec(w_s.shape, full3), pl.BlockSpec(b_s.shape, full2),
                  pl.BlockSpec((mem_len, kv.shape[1]), lambda b, n: (b, 0))],
        out_specs=[pl.BlockSpec((GM_CHUNK, zw), blk),
                   pl.BlockSpec((mem_len, kv.shape[1]), lambda b, n: (b, 0)),
                   pl.BlockSpec(ln_g.shape, full2), pl.BlockSpec(ln_b.shape, full2),
                   pl.BlockSpec(w_s.shape, full3), pl.BlockSpec(b_s.shape, full2)],
        out_shape=[jax.ShapeDtypeStruct((T, zw), BF), jax.ShapeDtypeStruct(kv.shape, F32),
                   jax.ShapeDtypeStruct(ln_g.shape, F32), jax.ShapeDtypeStruct(ln_b.shape, F32),
                   jax.ShapeDtypeStruct(w_s.shape, F32), jax.ShapeDtypeStruct(b_s.shape, F32)],
        name="gmlp_bwd",
        compiler_params=pltpu.CompilerParams(dimension_semantics=("arbitrary", "arbitrary"), vmem_limit_bytes=56 << 20),
    )(z, dcat, ln_g, ln_b, w_s, b_s, kv)


def _place():
    x, y, c = lax.axis_index("x"), lax.axis_index("y"), lax.axis_index("c")
    chips = [(1 - x, y), (x, 1 - y), (1 - x, 1 - y)]
    return x, y, c, chips


def _half(ref, kind, e):
    if kind == "col":
        n = ref.shape[1] // 2
        return ref.at[:, pl.ds(pl.multiple_of(e * n, n), n), :]
    n = ref.shape[2] // 2
    return ref.at[:, :, pl.ds(pl.multiple_of(e * n, n), n)]


def _slot(ref, kind, j, n):
    if kind == "col":
        return ref.at[:, :, pl.ds(pl.multiple_of(j * n, n), n)]
    return ref.at[:, pl.ds(pl.multiple_of(j * n, n), n), :]


def _allgather_seq(name, items, cid):
    nt = len(items)
    kinds = [k for (_, k, _) in items]
    out_type = []
    for s, k, l in items:
        L, r, c = s.shape
        lo = L if l is None else 1
        out_type.append(jax.ShapeDtypeStruct((lo, r, 4 * c) if k == "col" else (lo, 4 * r, c), s.dtype))

    def body(*refs):
        sh = [refs[t] if items[t][2] is None else refs[t].at[pl.ds(items[t][2], 1)] for t in range(nt)]
        full = refs[nt:2 * nt]
        loc, s_ici, r_ici, s_d2d, r_d2d = refs[2 * nt:]
        x, y, c, chips = _place()
        own = 2 * x + y
        sibling = (x, y, 1 - c)
        barrier = pltpu.get_barrier_semaphore()
        for peer in [(px, py, c) for (px, py) in chips] + [sibling]:
            pl.semaphore_signal(barrier, inc=1, device_id=peer, device_id_type=MESH)
        pl.semaphore_wait(barrier, 4)
        width = [sh[t].shape[2] if kinds[t] == "col" else sh[t].shape[1] for t in range(nt)]
        started = []
        for t in range(nt):
            mine = pltpu.make_async_copy(sh[t], _slot(full[t], kinds[t], own, width[t]), loc.at[t])
            mine.start()
            started.append(mine)
        sent = []
        for t in range(nt):
            for p, (px, py) in enumerate(chips):
                cp = pltpu.make_async_remote_copy(
                    src_ref=_half(sh[t], kinds[t], c), dst_ref=_half(_slot(full[t], kinds[t], own, width[t]), kinds[t], c),
                    send_sem=s_ici.at[t, p], recv_sem=r_ici.at[t, p], device_id=(px, py, c), device_id_type=MESH)
                cp.start()
                sent.append(cp)
        for t in range(nt):
            for p, (px, py) in enumerate(chips):
                landed = _half(_slot(full[t], kinds[t], 2 * px + py, width[t]), kinds[t], c)
                pltpu.make_async_remote_copy(
                    src_ref=landed, dst_ref=landed, send_sem=s_ici.at[t, p], recv_sem=r_ici.at[t, p],
                    device_id=(px, py, c), device_id_type=MESH).wait_recv()
                fw = pltpu.make_async_remote_copy(
                    src_ref=landed, dst_ref=landed, send_sem=s_d2d.at[t, p], recv_sem=r_d2d.at[t, p],
                    device_id=sibling, device_id_type=MESH)
                fw.start()
                sent.append(fw)
        for t in range(nt):
            for p, (px, py) in enumerate(chips):
                other = _half(_slot(full[t], kinds[t], 2 * px + py, width[t]), kinds[t], 1 - c)
                pltpu.make_async_remote_copy(
                    src_ref=other, dst_ref=other, send_sem=s_d2d.at[t, p], recv_sem=r_d2d.at[t, p],
                    device_id=sibling, device_id_type=MESH).wait_recv()
        for cp in sent:
            cp.wait_send()
        for cp in started:
            cp.wait()

    return pl.kernel(
        body, out_type=out_type, mesh=plsc.ScalarSubcoreMesh(axis_name="seq", num_cores=1),
        scratch_types=[pltpu.SemaphoreType.DMA((nt,)), pltpu.SemaphoreType.DMA((nt, 3)), pltpu.SemaphoreType.DMA((nt, 3)),
                       pltpu.SemaphoreType.DMA((nt, 3)), pltpu.SemaphoreType.DMA((nt, 3))],
        compiler_params=pltpu.CompilerParams(collective_id=cid), name=name,
    )(*[s for (s, _, _) in items])


def _slot2(ref, kind, j, n):
    if kind == "col":
        return ref.at[:, pl.ds(pl.multiple_of(j * n, n), n)]
    return ref.at[pl.ds(pl.multiple_of(j * n, n), n), :]


def _rs_chips_seq(name, parts, kinds, cid):
    nm = len(parts)
    out_type = []
    for g, k in zip(parts, kinds):
        r, c = g.shape
        ps = (r, c // 4) if k == "col" else (r // 4, c)
        out_type += [jax.ShapeDtypeStruct(ps, BF), jax.ShapeDtypeStruct((3,) + ps, BF)]

    def body(*refs):
        g = refs[:nm]
        outs = refs[nm:3 * nm]
        loc, ssem, rsem = refs[3 * nm:]
        x, y, c, chips = _place()
        own = 2 * x + y
        barrier = pltpu.get_barrier_semaphore()
        for (px, py) in chips:
            pl.semaphore_signal(barrier, inc=1, device_id=(px, py, c), device_id_type=MESH)
        pl.semaphore_wait(barrier, 3)
        cps = []
        for m in range(nm):
            k = kinds[m]
            own_o, got_o = outs[2 * m], outs[2 * m + 1]
            n = g[m].shape[1] // 4 if k == "col" else g[m].shape[0] // 4
            lc = pltpu.make_async_copy(_slot2(g[m], k, own, n), own_o, loc.at[m])
            lc.start()
            cps.append(lc)
            for p, (px, py) in enumerate(chips):
                cp = pltpu.make_async_remote_copy(
                    src_ref=_slot2(g[m], k, 2 * px + py, n), dst_ref=got_o.at[p],
                    send_sem=ssem.at[m, p], recv_sem=rsem.at[m, p], device_id=(px, py, c), device_id_type=MESH)
                cp.start()
                cps.append(cp)
        for cp in cps:
            cp.wait()

    return pl.kernel(
        body, out_type=out_type, mesh=plsc.ScalarSubcoreMesh(axis_name="seq", num_cores=1),
        scratch_types=[pltpu.SemaphoreType.DMA((nm,)), pltpu.SemaphoreType.DMA((nm, 3)), pltpu.SemaphoreType.DMA((nm, 3))],
        compiler_params=pltpu.CompilerParams(collective_id=cid), name=name,
    )(*parts)


def _finish_share(name, own, got, kind, c_arr):
    L, r, c = own.shape
    tr = _pick(r, 128 if kind == "col" else 256)
    nb = r // tr
    nq = L * nb
    own2 = own.reshape(L * r, c)
    got2 = got.reshape(3 * L * r, c)
    pick = lambda h, q: q * (1 - h) + (nq - 1) * h
    in_specs = [pl.BlockSpec((tr, c), lambda h, q, cc: (pick(h, q), 0))]
    in_specs += [pl.BlockSpec((tr, c), functools.partial(lambda h, q, cc, p: (p * nq + pick(h, q), 0), p=p)) for p in range(3)]
    if kind == "col":
        out_sd = (L, 2, r, c)
        o_spec = pl.BlockSpec((None, 2, tr, c), lambda h, q, cc: ((q * h) // nb, 0, (q * h) % nb, 0))
    else:
        out_sd = (L * r, 2 * c)
        o_spec = pl.BlockSpec((tr, 2 * c), lambda h, q, cc: (q * h, 0))

    def kern(c_ref, o_ref, g0, g1, g2, out_ref, mine, recv, ssem, rsem):
        h, q = pl.program_id(0), pl.program_id(1)
        x, y, cc, _ = _place()

        def swap(qq):
            return pltpu.make_async_remote_copy(src_ref=mine.at[qq], dst_ref=recv.at[qq], send_sem=ssem.at[qq],
                                                recv_sem=rsem.at[qq], device_id=(x, y, 1 - cc), device_id_type=MESH)

        @pl.when(h == 0)
        def _():
            mine[q] = ((o_ref[...].astype(F32) + g0[...].astype(F32)) + g1[...].astype(F32)) + g2[...].astype(F32)
            swap(q).start()

        @pl.when(h == 1)
        def _():
            swap(q).wait()
            a, b = mine[q], recv[q]
            first = c_ref[0] == 0
            lo, hi = jnp.where(first, a, b), jnp.where(first, b, a)
            if kind == "col":
                out_ref[0] = lo
                out_ref[1] = hi
            else:
                out_ref[:, :c] = lo
                out_ref[:, c:] = hi

    est = 2 * nq * tr * c * 4 + 6 * tr * c * 4 + 8 * tr * c * 2
    full = pl.pallas_call(
        kern,
        grid_spec=pltpu.PrefetchScalarGridSpec(
            num_scalar_prefetch=1, grid=(2, nq), in_specs=in_specs, out_specs=o_spec,
            scratch_shapes=[pltpu.VMEM((nq, tr, c), F32), pltpu.VMEM((nq, tr, c), F32),
                            pltpu.SemaphoreType.DMA((nq,)), pltpu.SemaphoreType.DMA((nq,))]),
        out_shape=jax.ShapeDtypeStruct(out_sd, F32), name=name,
        compiler_params=pltpu.CompilerParams(dimension_semantics=("arbitrary", "arbitrary"),
                                             vmem_limit_bytes=int(min(VMEM_CAP_BYTES, est + (12 << 20)))),
    )(c_arr, own2, got2, got2, got2)
    return full.reshape(L, 2 * r, c) if kind == "col" else full.reshape(L, r, 2 * c)


def _small_allreduce(buf, name):
    R = buf.shape[0]

    def body(x_ref, o_ref, slots, ssem, rsem):
        x, y, c, _ = _place()
        me = 4 * x + 2 * y + c
        slots[0] = x_ref[...]
        cps = []
        for k in range(1, 8):
            bx, by, bc = (k >> 2) & 1, (k >> 1) & 1, k & 1
            peer = (1 - x if bx else x, 1 - y if by else y, 1 - c if bc else c)
            cp = pltpu.make_async_remote_copy(src_ref=x_ref, dst_ref=slots.at[k], send_sem=ssem.at[k - 1],
                                              recv_sem=rsem.at[k - 1], device_id=peer, device_id_type=MESH)
            cp.start()
            cps.append(cp)
        for cp in cps:
            cp.wait()
        acc = slots[jnp.bitwise_xor(me, 0)]
        for d in range(1, 8):
            acc = acc + slots[jnp.bitwise_xor(me, d)]
        o_ref[...] = acc

    vm = pl.BlockSpec(memory_space=pltpu.VMEM)
    return pl.pallas_call(
        body, out_shape=jax.ShapeDtypeStruct(buf.shape, F32), in_specs=[vm], out_specs=vm,
        scratch_shapes=[pltpu.VMEM((8, R, LANES), F32), pltpu.SemaphoreType.DMA((7,)), pltpu.SemaphoreType.DMA((7,))],
        name=name,
        compiler_params=pltpu.CompilerParams(vmem_limit_bytes=int(min(VMEM_CAP_BYTES, 12 * R * LANES * 4 + (8 << 20)))),
    )(buf)


PACK_TILE_ROWS = 8


def _item_rows(shape):
    n = 1
    for d in shape:
        n *= d
    return -(-n // (PACK_TILE_ROWS * LANES)) * PACK_TILE_ROWS


def _pack(arrs, rows_total):
    buf = jnp.zeros((rows_total, LANES), F32)
    r = 0
    for a in arrs:
        f = a.reshape(-1).astype(F32)
        nr = _item_rows(a.shape)
        block = jnp.pad(f, (0, nr * LANES - f.shape[0])).reshape(nr, LANES)
        buf = lax.dynamic_update_slice(buf, block, (r, 0))
        r += nr
    return buf


def _unpack(buf, shapes):
    out, r = [], 0
    for s in shapes:
        n = 1
        for d in s:
            n *= d
        nr = _item_rows(s)
        out.append(buf[r:r + nr].reshape(-1)[:n].reshape(s))
        r += nr
    return out


def _rows_needed(shapes):
    return sum(_item_rows(s) for s in shapes)


def _two_rows(a, b):
    out = jnp.zeros((2, a.shape[1]), a.dtype)
    return lax.dynamic_update_slice(lax.dynamic_update_slice(out, a, (0, 0)), b, (1, 0))


def _adam(w, g, m, v):
    m = ADAM_B1 * m + (1.0 - ADAM_B1) * g
    v = ADAM_B2 * v + (1.0 - ADAM_B2) * jnp.square(g)
    m_hat = m / (1.0 - ADAM_B1 ** ADAM_STEP)
    v_hat = v / (1.0 - ADAM_B2 ** ADAM_STEP)
    delta = -ADAM_LR * (m_hat / (jnp.sqrt(v_hat) + ADAM_EPS) + ADAM_WD * w)
    return delta, m, v


def _adam_call(name, w2, g2, m2, v2, tr):
    def fn(rv, cv):
        return list(_adam(*rv)), []

    width = w2.shape[1]
    return _rowcall(name, fn, [(w2, 0, width), (g2, 0, width), (m2, 0, width), (v2, 0, width)], [],
                    [(width, F32)] * 3, [], tr)


def kernel(x, mem, mem_norm, lb_logits, ffn1_norm, ffn1_w_in, ffn1_w_out, mix_norm, mem_w_kv, hgrn_w_in, hgrn_gnorm, hgrn_w_out, gmlp_w_in, gmlp_ln_g, gmlp_ln_b, gmlp_w_s, gmlp_b_s, gmlp_w_out, ffn2_norm, ffn2_w_in, ffn2_w_out, final_norm, loss_target, m_mem_norm, m_lb_logits, m_ffn1_norm, m_ffn1_w_in, m_ffn1_w_out, m_mix_norm, m_mem_w_kv, m_hgrn_w_in, m_hgrn_gnorm, m_hgrn_w_out, m_gmlp_w_in, m_gmlp_ln_g, m_gmlp_ln_b, m_gmlp_w_s, m_gmlp_b_s, m_gmlp_w_out, m_ffn2_norm, m_ffn2_w_in, m_ffn2_w_out, m_final_norm, v_mem_norm, v_lb_logits, v_ffn1_norm, v_ffn1_w_in, v_ffn1_w_out, v_mix_norm, v_mem_w_kv, v_hgrn_w_in, v_hgrn_gnorm, v_hgrn_w_out, v_gmlp_w_in, v_gmlp_ln_g, v_gmlp_ln_b, v_gmlp_w_s, v_gmlp_b_s, v_gmlp_w_out, v_ffn2_norm, v_ffn2_w_in, v_ffn2_w_out, v_final_norm):
    bl, seq, D = x.shape
    T = bl * seq
    mem_len = mem.shape[1]
    chip = 2 * lax.axis_index("x") + lax.axis_index("y")
    c_arr = lax.axis_index("c").astype(jnp.int32).reshape(1)
    TR = 256

    big = [("ffn1_w_in", ffn1_w_in, "col"), ("ffn1_w_out", ffn1_w_out, "row"), ("mem_w_kv", mem_w_kv, "col"),
           ("hgrn_w_in", hgrn_w_in, "col"), ("hgrn_w_out", hgrn_w_out, "row"), ("gmlp_w_in", gmlp_w_in, "col"),
           ("gmlp_w_out", gmlp_w_out, "row"), ("ffn2_w_in", ffn2_w_in, "col"), ("ffn2_w_out", ffn2_w_out, "row")]
    kinds = [k for (_, _, k) in big]
    shards_bf = []
    for nm, w, _ in big:
        L, r, c = w.shape
        (wb,) = _rowcall("cast_" + nm, lambda rv, cv: ([rv[0]], []), [(w.reshape(L * r, c), 0, c)], [], [(c, BF)], [], 512)
        shards_bf.append(wb.reshape(L, r, c))
    sb = dict(zip([nm for (nm, _, _) in big], shards_bf))
    groups = [[("ffn1_w_in", 0), ("ffn1_w_out", 0)],
              [("mem_w_kv", None), ("hgrn_w_in", None), ("hgrn_w_out", None)],
              [("ffn2_w_in", 0), ("ffn2_w_out", 0)],
              [("ffn1_w_in", 1), ("ffn1_w_out", 1)],
              [("gmlp_w_in", None), ("gmlp_w_out", None)],
              [("ffn2_w_in", 1), ("ffn2_w_out", 1)]]
    kind_of = {nm: k for (nm, _, k) in big}
    gathered = {nm: [None, None] for nm in ("ffn1_w_in", "ffn1_w_out", "ffn2_w_in", "ffn2_w_out")}
    for gi, grp in enumerate(groups):
        outs = _allgather_seq("gather_%d" % gi, [(sb[nm], kind_of[nm], l) for (nm, l) in grp], gi)
        for (nm, l), o in zip(grp, outs):
            if l is None:
                gathered[nm] = o
            else:
                gathered[nm][l] = o

    ln_w = GM_GROUPS * GM_GROUP_DIM
    placed = lax.dynamic_update_slice(jnp.zeros((8, ln_w), F32), 0.5 * gmlp_ln_g, (0, chip * gmlp_ln_g.shape[1]))
    placed = lax.dynamic_update_slice(placed, 0.5 * gmlp_ln_b, (1, chip * gmlp_ln_g.shape[1]))
    ln_full = _small_allreduce(placed.reshape(16, LANES), "gather_ln").reshape(8, ln_w)
    ln_g_full, ln_b_full = ln_full[0:1], ln_full[1:2]

    def rms_fwd(name, xin, g):
        (h,) = _rowcall(name, lambda rv, cv: ([_rmsnorm(rv[0], cv[0])], []), [(xin, 0, D)], [g.reshape(1, D)], [(D, BF)], [], TR)
        return h

    def ffn_fwd(tag, xin, g, w_in, w_out, layer):
        dff = w_out[layer].shape[1]
        h = rms_fwd("rms_" + tag, xin, g)
        z = _mm("ffn_in_" + tag, h, w_in[layer], "nn", BF, 512, 512, D, b_lead=0)
        (a,) = _rowcall("swiglu_" + tag, lambda rv, cv: ([_silu(rv[0].astype(F32)) * rv[1].astype(F32)], []),
                        [(z, 0, dff), (z, 1, dff)], [], [(dff, BF)], [], TR)
        xo = _mm("ffn_out_" + tag, a, w_out[layer], "nn", F32, 512, 1024, dff, scale=0.5, res=xin, b_lead=0)
        return xo, (xin, h, z, a)

    def ffn_bwd(tag, dxo, saved, g, w_in, w_out, layer):
        xin, h, z, a = saved
        dff = w_out[layer].shape[1]
        da = _mm("ffn_da_" + tag, dxo, w_out[layer], "nt", BF, 512, dff // 2, D, scale=0.5, b_lead=0)
        dw_out = _mm_tn_pair("ffn_dwo_" + tag, a, dxo, "row", c_arr, dff // 2, 512, scale=0.5)

        def sw_bwd(rv, cv):
            gt, up, d = rv[0].astype(F32), rv[1].astype(F32), rv[2].astype(F32)
            _, vjp = jax.vjp(lambda p, q: _silu(p) * q, gt, up)
            dg, du = vjp(d)
            return [jnp.concatenate([dg, du], axis=1)], []

        (dz,) = _rowcall("swiglu_bwd_" + tag, sw_bwd, [(z, 0, dff), (z, 1, dff), (da, 0, dff)], [], [(2 * dff, BF)], [], TR)
        dh = _mm("ffn_dh_" + tag, dz, w_in[layer], "nt", F32, 512, 512, dff, b_lead=0)
        dw_in = _mm_tn_pair("ffn_dwi_" + tag, h, dz, "col", c_arr, dff, 512)
        dx, dg = rms_bwd("rms_bwd_" + tag, xin, g, dh, dxo)
        return dx, dg, dw_in, dw_out

    def rms_bwd(name, xin, g, dh, dres):
        def fn(rv, cv):
            _, vjp = jax.vjp(_rmsnorm, rv[0], cv[0])
            dx, dg = vjp(rv[1])
            if dres is not None:
                dx = dx + rv[2]
            return [dx], [dg]

        rows = [(xin, 0, D), (dh, 0, D)] + ([(dres, 0, D)] if dres is not None else [])
        dx, dg = _rowcall(name, fn, rows, [g.reshape(1, D)], [(D, F32)], [((1, D), F32)], TR)
        return dx, dg

    x0 = x.reshape(T, D)
    tgt = loss_target.reshape(T, D)
    mem2 = mem.reshape(bl * mem_len, D)
    memn = rms_fwd("rms_mem", mem2, mem_norm)
    kv = [_mm("kv_%d" % i, memn, gathered["mem_w_kv"], "nn", F32, 512, 512, D, b_lead=i) for i in range(2)]

    x1, sv_f10 = ffn_fwd("f1l0", x0, ffn1_norm[0], gathered["ffn1_w_in"], gathered["ffn1_w_out"], 0)
    h_m0 = rms_fwd("rms_mix0", x1, mix_norm[0])
    z_m0 = _mm("mix_in_0", h_m0, gathered["hgrn_w_in"], "nn", F32, 512, 512, D, b_lead=0)
    nc0 = seq // HG_CHUNK
    cat0, stash0 = _hgrn_fwd(z_m0, lb_logits, hgrn_gnorm, kv[0], bl, nc0)
    x2 = _mm("mix_out_0", cat0, gathered["hgrn_w_out"], "nn", F32, 512, 1024, cat0.shape[1], res=x1, b_lead=0)
    x3, sv_f20 = ffn_fwd("f2l0", x2, ffn2_norm[0], gathered["ffn2_w_in"], gathered["ffn2_w_out"], 0)
    x4, sv_f11 = ffn_fwd("f1l1", x3, ffn1_norm[1], gathered["ffn1_w_in"], gathered["ffn1_w_out"], 1)
    h_m1 = rms_fwd("rms_mix1", x4, mix_norm[1])
    z_m1 = _mm("mix_in_1", h_m1, gathered["gmlp_w_in"], "nn", F32, 512, 512, D, b_lead=0)
    nc1 = seq // GM_CHUNK
    w_s, b_s = gmlp_w_s[0], gmlp_b_s[0]
    cat1 = _gmlp_fwd(z_m1, ln_g_full, ln_b_full, w_s, b_s, kv[1], bl, nc1)
    x5 = _mm("mix_out_1", cat1, gathered["gmlp_w_out"], "nn", F32, 512, 1024, cat1.shape[1], res=x4, b_lead=0)
    x6, sv_f21 = ffn_fwd("f2l1", x5, ffn2_norm[1], gathered["ffn2_w_in"], gathered["ffn2_w_out"], 1)

    def head(rv, cv):
        def f(xx, gg):
            err = _rmsnorm(xx, gg) - rv[1]
            return 0.5 * jnp.sum(jnp.mean(err * err, axis=-1, keepdims=True), axis=0, keepdims=True)

        ls, vjp = jax.vjp(f, rv[0], cv[0])
        dx, dg = vjp(jnp.ones((1, 1), F32))
        return [dx], [dg, jnp.broadcast_to(ls, (1, 128))]

    dx6, d_final, loss_part = _rowcall("loss_head", head, [(x6, 0, D), (tgt, 0, D)], [final_norm.reshape(1, D)],
                                       [(D, F32)], [((1, D), F32), ((1, 128), F32)], TR)

    rs_out = {}
    n_gather = len(groups)

    def rs(gi, items):
        outs = _rs_chips_seq("reduce_%d" % gi, [p for (_, p, _) in items], [k for (_, _, k) in items], n_gather + gi)
        for i, (key, _, _) in enumerate(items):
            rs_out[key] = (outs[2 * i], outs[2 * i + 1])

    dx5, dg_f21, dwi_f21, dwo_f21 = ffn_bwd("f2l1", dx6, sv_f21, ffn2_norm[1], gathered["ffn2_w_in"], gathered["ffn2_w_out"], 1)
    rs(0, [(("ffn2_w_out", 1), dwo_f21, "row"), (("ffn2_w_in", 1), dwi_f21, "col")])
    dcat1 = _mm("mix_dcat_1", dx5, gathered["gmlp_w_out"], "nt", F32, 512, 512, D, b_lead=0)
    dwo_m1 = _mm_tn_pair("mix_dwo_1", cat1, dx5, "row", c_arr, 1024, 512)
    dz_m1, dkv1, d_lng, d_lnb, d_ws, d_bs = _gmlp_bwd(z_m1, dcat1, ln_g_full, ln_b_full, w_s, b_s, kv[1], bl, nc1)
    dh_m1 = _mm("mix_dh_1", dz_m1, gathered["gmlp_w_in"], "nt", F32, 512, 512, 1024, b_lead=0)
    dwi_m1 = _mm_tn_pair("mix_dwi_1", h_m1, dz_m1, "col", c_arr, 2560, 512)
    rs(1, [(("gmlp_w_out", 0), dwo_m1, "row"), (("gmlp_w_in", 0), dwi_m1, "col")])
    dx4, dg_m1 = rms_bwd("rms_bwd_mix1", x4, mix_norm[1], dh_m1, dx5)
    dx3, dg_f11, dwi_f11, dwo_f11 = ffn_bwd("f1l1", dx4, sv_f11, ffn1_norm[1], gathered["ffn1_w_in"], gathered["ffn1_w_out"], 1)
    rs(2, [(("ffn1_w_out", 1), dwo_f11, "row"), (("ffn1_w_in", 1), dwi_f11, "col")])

    dx2, dg_f20, dwi_f20, dwo_f20 = ffn_bwd("f2l0", dx3, sv_f20, ffn2_norm[0], gathered["ffn2_w_in"], gathered["ffn2_w_out"], 0)
    rs(3, [(("ffn2_w_out", 0), dwo_f20, "row"), (("ffn2_w_in", 0), dwi_f20, "col")])
    dcat0 = _mm("mix_dcat_0", dx2, gathered["hgrn_w_out"], "nt", F32, 512, 512, D, b_lead=0)
    dwo_m0 = _mm_tn_pair("mix_dwo_0", cat0, dx2, "row", c_arr, 1024, 512)
    dz_m0, dkv0, d_lb, d_gn = _hgrn_bwd(z_m0, dcat0, stash0, lb_logits, hgrn_gnorm, kv[0], bl, nc0)
    dh_m0 = _mm("mix_dh_0", dz_m0, gathered["hgrn_w_in"], "nt", F32, 512, 512, 1024, b_lead=0)
    dwi_m0 = _mm_tn_pair("mix_dwi_0", h_m0, dz_m0, "col", c_arr, 2560, 512)
    rs(4, [(("hgrn_w_out", 0), dwo_m0, "row"), (("hgrn_w_in", 0), dwi_m0, "col")])
    dx1, dg_m0 = rms_bwd("rms_bwd_mix0", x1, mix_norm[0], dh_m0, dx2)

    dwkv = [_mm_tn_pair("kv_dw_%d" % i, memn, dkv, "col", c_arr, 1024, 512) for i, dkv in enumerate([dkv0, dkv1])]
    rs(5, [(("mem_w_kv", 0), dwkv[0], "col"), (("mem_w_kv", 1), dwkv[1], "col")])
    dmemn = _mm("kv_dx_0", dkv0, gathered["mem_w_kv"], "nt", F32, 512, 512, 1024, b_lead=0)
    dmemn = _mm("kv_dx_1", dkv1, gathered["mem_w_kv"], "nt", F32, 512, 512, 1024, res=dmemn, b_lead=1)
    _, d_memnorm = rms_bwd("rms_bwd_mem", mem2, mem_norm, dmemn, None)

    dx0, dg_f10, dwi_f10, dwo_f10 = ffn_bwd("f1l0", dx1, sv_f10, ffn1_norm[0], gathered["ffn1_w_in"], gathered["ffn1_w_out"], 0)
    rs(6, [(("ffn1_w_out", 0), dwo_f10, "row")])
    rs(7, [(("ffn1_w_in", 0), dwi_f10, "col")])

    shard_grads = []
    for (nm, w, k) in big:
        per_layer = []
        for l in range(w.shape[0]):
            own, got = rs_out[(nm, l)]
            per_layer.append(_finish_share("finish_%s_%d" % (nm, l), own[None], got[:, None], k, c_arr))
        shard_grads.append(per_layer[0] if len(per_layer) == 1 else jnp.concatenate(per_layer, axis=0))

    big_w = [w for (_, w, _) in big]
    big_m = [m_ffn1_w_in, m_ffn1_w_out, m_mem_w_kv, m_hgrn_w_in, m_hgrn_w_out, m_gmlp_w_in, m_gmlp_w_out, m_ffn2_w_in, m_ffn2_w_out]
    big_v = [v_ffn1_w_in, v_ffn1_w_out, v_mem_w_kv, v_hgrn_w_in, v_hgrn_w_out, v_gmlp_w_in, v_gmlp_w_out, v_ffn2_w_in, v_ffn2_w_out]
    big_out = {}
    for (nm, w, _), g, m, v in zip(big, shard_grads, big_m, big_v):
        L, r, c = w.shape
        d2, m2, v2 = _adam_call("adam_" + nm, w.reshape(L * r, c), g.reshape(L * r, c), m.reshape(L * r, c),
                                v.reshape(L * r, c), 256)
        big_out[nm] = (g, d2.reshape(w.shape), m2.reshape(w.shape), v2.reshape(w.shape))

    d_ffn1n = _two_rows(dg_f10, dg_f11)
    d_mixn = _two_rows(dg_m0, dg_m1)
    d_ffn2n = _two_rows(dg_f20, dg_f21)
    small_parts = [loss_part[:, :1], d_memnorm, d_lb, d_ffn1n, d_mixn, d_gn, d_lng, d_lnb, d_ws, d_bs, d_ffn2n, d_final]
    red_shapes = [(1,), mem_norm.shape, lb_logits.shape, ffn1_norm.shape, mix_norm.shape, hgrn_gnorm.shape, (1, ln_w), (1, ln_w),
                  gmlp_w_s.shape, gmlp_b_s.shape, ffn2_norm.shape, final_norm.shape]
    red = _small_allreduce(_pack(small_parts, _rows_needed(red_shapes)), "reduce_small")
    (loss_v, g_memn, g_lb, g_f1n, g_mixn, g_gn, g_lng_full, g_lnb_full, g_ws, g_bs, g_f2n, g_fin) = _unpack(red, red_shapes)
    lsh = gmlp_ln_g.shape[1]
    g_lng = lax.dynamic_slice(g_lng_full, (0, chip * lsh), (1, lsh))
    g_lnb = lax.dynamic_slice(g_lnb_full, (0, chip * lsh), (1, lsh))
    small_w = [mem_norm, lb_logits, ffn1_norm, mix_norm, hgrn_gnorm, gmlp_ln_g, gmlp_ln_b, gmlp_w_s, gmlp_b_s, ffn2_norm, final_norm]
    small_g = [g_memn, g_lb, g_f1n, g_mixn, g_gn, g_lng, g_lnb, g_ws, g_bs, g_f2n, g_fin]
    small_m = [m_mem_norm, m_lb_logits, m_ffn1_norm, m_mix_norm, m_hgrn_gnorm, m_gmlp_ln_g, m_gmlp_ln_b, m_gmlp_w_s, m_gmlp_b_s, m_ffn2_norm, m_final_norm]
    small_v = [v_mem_norm, v_lb_logits, v_ffn1_norm, v_mix_norm, v_hgrn_gnorm, v_gmlp_ln_g, v_gmlp_ln_b, v_gmlp_w_s, v_gmlp_b_s, v_ffn2_norm, v_final_norm]
    sshapes = [w.shape for w in small_w]
    nrow = _rows_needed(sshapes)
    d_p, m_p, v_p = _adam_call("adam_small", _pack(small_w, nrow), _pack(small_g, nrow), _pack(small_m, nrow), _pack(small_v, nrow), nrow)
    s_delta, s_m, s_v = _unpack(d_p, sshapes), _unpack(m_p, sshapes), _unpack(v_p, sshapes)
    small_names = ["mem_norm", "lb_logits", "ffn1_norm", "mix_norm", "hgrn_gnorm", "gmlp_ln_g", "gmlp_ln_b", "gmlp_w_s", "gmlp_b_s", "ffn2_norm", "final_norm"]
    small_out = {nm: (g.reshape(w.shape), d, m, v) for nm, w, g, d, m, v in zip(small_names, small_w, small_g, s_delta, s_m, s_v)}

    order = ["mem_norm", "lb_logits", "ffn1_norm", "ffn1_w_in", "ffn1_w_out", "mix_norm", "mem_w_kv", "hgrn_w_in", "hgrn_gnorm",
             "hgrn_w_out", "gmlp_w_in", "gmlp_ln_g", "gmlp_ln_b", "gmlp_w_s", "gmlp_b_s", "gmlp_w_out", "ffn2_norm", "ffn2_w_in",
             "ffn2_w_out", "final_norm"]
    allo = {**big_out, **small_out}
    grad_x = dx0.reshape(x.shape)
    return (loss_v.reshape(()), grad_x, *[allo[n][0] for n in order], *[allo[n][1] for n in order],
            *[allo[n][2] for n in order], *[allo[n][3] for n in order])
```

```python
import functools

import jax
import jax.numpy as jnp
from jax import lax
from jax.experimental import pallas as pl
from jax.experimental.pallas import tpu as pltpu
from jax.experimental.pallas import tpu_sc as plsc

BF = jnp.bfloat16
F32 = jnp.float32
MESH = pl.DeviceIdType.MESH

EPS = 1e-6
D_MODEL = 1024
HG_HEADS = 8
HG_DIM = 128
HG_CHUNK = 64
GM_CHUNK = 128
GM_GROUPS = 8
GM_GROUP_DIM = 256
XA_HEADS = 4
XA_DIM = 256
ADAM_LR = 0.001
ADAM_B1 = 0.9
ADAM_B2 = 0.999
ADAM_EPS = 1e-08
ADAM_WD = 0.01
ADAM_STEP = 10

VMEM_CAP_BYTES = 60 * 1024 * 1024
LANES = 1024


def _pick(n, cap, mult=16):
    if n <= cap:
        return n
    for d in range(cap - cap % mult, 0, -mult):
        if n % d == 0:
            return d
    raise ValueError((n, cap, mult))


def _dg(a, b, ca, cb):
    return lax.dot_general(a.astype(BF), b.astype(BF), (((ca,), (cb,)), ((), ())), preferred_element_type=F32)


@jax.custom_vjp
def dot_nn(a, b):
    return _dg(a, b, 1, 0)


def _nn_fwd(a, b):
    return _dg(a, b, 1, 0), (a, b)


def _nn_bwd(r, g):
    a, b = r
    return _dg(g, b, 1, 1), _dg(a, g, 0, 0)


dot_nn.defvjp(_nn_fwd, _nn_bwd)


@jax.custom_vjp
def dot_nt(a, b):
    return _dg(a, b, 1, 1)


def _nt_fwd(a, b):
    return _dg(a, b, 1, 1), (a, b)


def _nt_bwd(r, g):
    a, b = r
    return _dg(g, b, 1, 0), _dg(g, a, 0, 0)


dot_nt.defvjp(_nt_fwd, _nt_bwd)


@jax.custom_vjp
def dot_tn(a, b):
    return _dg(a, b, 0, 0)


def _tn_fwd(a, b):
    return _dg(a, b, 0, 0), (a, b)


def _tn_bwd(r, g):
    a, b = r
    return _dg(b, g, 1, 1), _dg(a, g, 1, 0)


dot_tn.defvjp(_tn_fwd, _tn_bwd)


def _rmsnorm(x, g):
    return x * lax.rsqrt(jnp.mean(x * x, axis=-1, keepdims=True) + EPS) * g


def _silu(x):
    return x * jax.nn.sigmoid(x)


def _gelu(x):
    return 0.5 * x * (1.0 + lax.erf(x * (0.5 ** 0.5)))


def _softmax_last(s):
    m = lax.stop_gradient(jnp.max(s, axis=-1, keepdims=True))
    e = jnp.exp(s - m)
    return e / jnp.sum(e, axis=-1, keepdims=True)


def _tril(n):
    r = lax.broadcasted_iota(jnp.int32, (n, n), 0)
    c = lax.broadcasted_iota(jnp.int32, (n, n), 1)
    return r >= c


def _cumsum_rows(l):
    n = l.shape[0]
    return lax.dot_general(_tril(n).astype(F32), l, (((1,), (0,)), ((), ())),
                           precision=lax.Precision.HIGHEST, preferred_element_type=F32)


def _attention(zx, mk, mv):
    s = dot_nt(zx, mk) * (XA_DIM ** -0.5)
    return dot_nn(_softmax_last(s), mv)


def _hgrn_head(zq, zf, zi, zg, l0, l1, l2, gn, S):
    m = lax.stop_gradient(jnp.maximum(jnp.maximum(l0, l1), l2))
    e0 = jnp.exp(l0 - m)
    lb = e0 / (e0 + jnp.exp(l1 - m) + jnp.exp(l2 - m))
    q = _silu(zq)
    f = lb + (1.0 - lb) * jax.nn.sigmoid(zf)
    k = 1.0 - f
    b = _cumsum_rows(jnp.log(f))
    b_last = b[HG_CHUNK - 1:HG_CHUNK, :]
    q_dec = q * jnp.exp(b)
    k_inv = k * jnp.exp(-b)
    a = jnp.where(_tril(HG_CHUNK), dot_nt(q_dec, k_inv), 0.0)
    o = dot_nn(a, zi) + dot_nn(q_dec, S)
    S_new = jnp.exp(b_last).reshape(HG_DIM, 1) * S + dot_tn(k * jnp.exp(b_last - b), zi)
    o = _rmsnorm(o, gn) * _silu(zg)
    return o, S_new


def _hgrn_block(zq, zf, zi, zg, zx, l0, l1, l2, gn, mk, mv, S):
    outs, s_new = [], []
    for h in range(HG_HEADS):
        o, sn = _hgrn_head(zq[h], zf[h], zi[h], zg[h], l0[h], l1[h], l2[h], gn, S[h])
        outs.append(o)
        s_new.append(sn)
    for a in range(XA_HEADS):
        outs.append(_attention(zx[a], mk[a], mv[a]))
    return outs, s_new


def _gmlp_block(zu, zv, zx, lng, lnb, ws, bs, mk, mv):
    gv = [_gelu(v) for v in zv]
    width = GM_GROUPS * GM_GROUP_DIM
    mu = sum(jnp.sum(g, axis=-1, keepdims=True) for g in gv) / width
    xc = [g - mu for g in gv]
    var = sum(jnp.sum(c * c, axis=-1, keepdims=True) for c in xc) / width
    r = lax.rsqrt(var + EPS)
    outs = []
    for g in range(GM_GROUPS):
        v = xc[g] * r * lng[g] + lnb[g]
        w = jnp.where(_tril(GM_CHUNK), ws[g], 0.0)
        mixed = dot_nn(w, v) + bs[g].reshape(GM_CHUNK, 1)
        outs.append(_gelu(zu[g]) * mixed)
    for a in range(XA_HEADS):
        outs.append(_attention(zx[a], mk[a], mv[a]))
    return outs


def _rowcall(name, fn, rows, consts, row_outs, acc_outs, tr):
    nrows = rows[0][0].shape[0]
    tr = _pick(nrows, tr)
    n_r, n_c, n_ro, n_ao = len(rows), len(consts), len(row_outs), len(acc_outs)

    def kern(*refs):
        rv = [r[...] for r in refs[:n_r]]
        cv = [r[...] for r in refs[n_r:n_r + n_c]]
        ro_refs = refs[n_r + n_c:n_r + n_c + n_ro]
        ao_refs = refs[n_r + n_c + n_ro:]
        ro, ao = fn(rv, cv)
        for ref, v in zip(ro_refs, ro):
            ref[...] = v.astype(ref.dtype)
        if n_ao:
            @pl.when(pl.program_id(0) == 0)
            def _():
                for ref in ao_refs:
                    ref[...] = jnp.zeros(ref.shape, ref.dtype)

            for ref, v in zip(ao_refs, ao):
                ref[...] += v.astype(ref.dtype)

    in_specs = [pl.BlockSpec((tr, w), functools.partial(lambda i, cb: (i, cb), cb=cb)) for (_, cb, w) in rows]
    in_specs += [pl.BlockSpec(c.shape, lambda i: (0, 0)) for c in consts]
    out_specs = [pl.BlockSpec((tr, w), lambda i: (i, 0)) for (w, _) in row_outs]
    out_specs += [pl.BlockSpec(s, lambda i: (0, 0)) for (s, _) in acc_outs]
    out_shape = [jax.ShapeDtypeStruct((nrows, w), dt) for (w, dt) in row_outs]
    out_shape += [jax.ShapeDtypeStruct(s, dt) for (s, dt) in acc_outs]
    est = sum(tr * w * a.dtype.itemsize for (a, _, w) in rows) + sum(tr * w * jnp.dtype(dt).itemsize for (w, dt) in row_outs)
    est += sum(c.size * c.dtype.itemsize for c in consts)
    outs = pl.pallas_call(
        kern, grid=(nrows // tr,), in_specs=in_specs, out_specs=out_specs, out_shape=out_shape, name=name,
        compiler_params=pltpu.CompilerParams(dimension_semantics=("arbitrary",),
                                             vmem_limit_bytes=int(min(VMEM_CAP_BYTES, 6 * est + (16 << 20)))),
    )(*[a for (a, _, _) in rows], *consts)
    return outs


def _mm(name, a, b, mode, out_dtype, tm, tn, tk, scale=1.0, res=None, a_lead=None, b_lead=None):
    ash = a.shape[-2:]
    bsh = b.shape[-2:]
    if mode == "nn":
        (M, K), (K2, N) = ash, bsh
    elif mode == "nt":
        (M, K), (N, K2) = ash, bsh
    else:
        (K, M), (K2, N) = ash, bsh
    assert K == K2, (name, a.shape, b.shape)
    tm, tn, tk = min(tm, M), min(tn, N), min(tk, K)
    assert M % tm == 0 and N % tn == 0 and K % tk == 0, (name, M, N, K, tm, tn, tk)
    nk = K // tk
    dims = {"nn": (1, 0), "nt": (1, 1), "tn": (0, 0)}[mode]

    def lead(spec_shape, index_fn, lead_idx):
        if lead_idx is None:
            return pl.BlockSpec(spec_shape, index_fn)
        return pl.BlockSpec((None,) + spec_shape, lambda i, j, k: (lead_idx,) + index_fn(i, j, k))

    if mode == "tn":
        a_spec = lead((tk, tm), lambda i, j, k: (k, i), a_lead)
    else:
        a_spec = lead((tm, tk), lambda i, j, k: (i, k), a_lead)
    if mode == "nt":
        b_spec = lead((tn, tk), lambda i, j, k: (j, k), b_lead)
    else:
        b_spec = lead((tk, tn), lambda i, j, k: (k, j), b_lead)
    o_spec = pl.BlockSpec((tm, tn), lambda i, j, k: (i, j))
    has_res = res is not None

    def kern(*refs):
        a_ref, b_ref = refs[0], refs[1]
        res_ref = refs[2] if has_res else None
        o_ref = refs[3] if has_res else refs[2]
        acc_ref = refs[-1] if nk > 1 else None
        p = lax.dot_general(a_ref[...].astype(BF), b_ref[...].astype(BF), (((dims[0],), (dims[1],)), ((), ())),
                            preferred_element_type=F32)

        def finish(v):
            if scale != 1.0:
                v = v * scale
            if has_res:
                v = res_ref[...] + v
            o_ref[...] = v.astype(o_ref.dtype)

        if nk == 1:
            finish(p)
        else:
            k = pl.program_id(2)

            @pl.when(k == 0)
            def _():
                acc_ref[...] = p

            @pl.when(k > 0)
            def _():
                acc_ref[...] += p

            @pl.when(k == nk - 1)
            def _():
                finish(acc_ref[...])

    ins = [a, b] + ([res] if has_res else [])
    in_specs = [a_spec, b_spec] + ([o_spec] if has_res else [])
    est = tm * tk * a.dtype.itemsize + tk * tn * b.dtype.itemsize + tm * tn * (jnp.dtype(out_dtype).itemsize + 8)
    return pl.pallas_call(
        kern, grid=(M // tm, N // tn, nk), in_specs=in_specs, out_specs=o_spec,
        out_shape=jax.ShapeDtypeStruct((M, N), out_dtype),
        scratch_shapes=[pltpu.VMEM((tm, tn), F32)] if nk > 1 else [],
        name=name,
        compiler_params=pltpu.CompilerParams(dimension_semantics=("parallel", "parallel", "arbitrary"),
                                             vmem_limit_bytes=int(min(VMEM_CAP_BYTES, 3 * est + (16 << 20)))),
    )(*ins)


def _mm_tn_pair(name, a, b, kind, c_arr, tq, tk, scale=1.0):
    T, M = a.shape
    _, N = b.shape
    tk = min(tk, T)
    assert T % tk == 0
    nk = T // tk
    if kind == "col":
        hm = M // 2
        assert N % tq == 0
        nq = N // tq
        tile = (hm, tq)
        a_spec = pl.BlockSpec((tk, hm), lambda h, q, k, c: (k, jnp.bitwise_xor(h, 1 - c[0])))
        b_spec = pl.BlockSpec((tk, tq), lambda h, q, k, c: (k, q))
        o_spec = pl.BlockSpec(tile, lambda h, q, k, c: (0, q * h))
        out_sd = (hm, N)
    else:
        hn = N // 2
        assert M % tq == 0
        nq = M // tq
        tile = (tq, hn)
        a_spec = pl.BlockSpec((tk, tq), lambda h, q, k, c: (k, q))
        b_spec = pl.BlockSpec((tk, hn), lambda h, q, k, c: (k, jnp.bitwise_xor(h, 1 - c[0])))
        o_spec = pl.BlockSpec(tile, lambda h, q, k, c: (q * h, 0))
        out_sd = (M, hn)

    def kern(c_ref, a_ref, b_ref, o_ref, acc, stage, recv, ssem, rsem):
        h, q, k = pl.program_id(0), pl.program_id(1), pl.program_id(2)
        x, y, c, _ = _place()
        p = lax.dot_general(a_ref[...].astype(BF), b_ref[...].astype(BF), (((0,), (0,)), ((), ())), preferred_element_type=F32)

        @pl.when(k == 0)
        def _():
            acc[...] = p

        @pl.when(k > 0)
        def _():
            acc[...] += p

        def send(slot, qq):
            return pltpu.make_async_remote_copy(src_ref=stage.at[slot], dst_ref=recv.at[qq], send_sem=ssem.at[slot],
                                                recv_sem=rsem.at[qq], device_id=(x, y, 1 - c), device_id_type=MESH)

        last = k == nk - 1

        @pl.when(jnp.logical_and(last, h == 0))
        def _():
            slot = q % 2

            @pl.when(q >= 2)
            def _():
                send(slot, q).wait_send()

            stage[slot] = (acc[...] * scale).astype(BF)
            send(slot, q).start()

        @pl.when(jnp.logical_and(last, h == 1))
        def _():
            @pl.when(q == 0)
            def _():
                for s in range(min(nq, 2)):
                    send(s, 0).wait_send()

            send(0, q).wait_recv()
            o_ref[...] = (acc[...] * scale + recv[q].astype(F32)).astype(o_ref.dtype)

    tb = tile[0] * tile[1]
    est = tb * (4 + 2 * 2 + nq * 2 + 2 * 2) + 2 * tk * (a_spec.block_shape[1] + b_spec.block_shape[1]) * 2 * 2
    return pl.pallas_call(
        kern,
        grid_spec=pltpu.PrefetchScalarGridSpec(
            num_scalar_prefetch=1, grid=(2, nq, nk), in_specs=[a_spec, b_spec], out_specs=o_spec,
            scratch_shapes=[pltpu.VMEM(tile, F32), pltpu.VMEM((2,) + tile, BF), pltpu.VMEM((nq,) + tile, BF),
                            pltpu.SemaphoreType.DMA((2,)), pltpu.SemaphoreType.DMA((nq,))]),
        out_shape=jax.ShapeDtypeStruct(out_sd, BF), name=name,
        compiler_params=pltpu.CompilerParams(dimension_semantics=("arbitrary", "arbitrary", "arbitrary"),
                                             vmem_limit_bytes=int(min(VMEM_CAP_BYTES, est + (12 << 20)))),
    )(c_arr, a, b)


def _hgrn_pieces(z_ref):
    W = HG_HEADS * HG_DIM
    zq = [z_ref[:, h * HG_DIM:(h + 1) * HG_DIM] for h in range(HG_HEADS)]
    zf = [z_ref[:, W + h * HG_DIM:W + (h + 1) * HG_DIM] for h in range(HG_HEADS)]
    zi = [z_ref[:, 2 * W + h * HG_DIM:2 * W + (h + 1) * HG_DIM] for h in range(HG_HEADS)]
    zg = [z_ref[:, 3 * W + h * HG_DIM:3 * W + (h + 1) * HG_DIM] for h in range(HG_HEADS)]
    zx = [z_ref[:, 4 * W + a * XA_DIM:4 * W + (a + 1) * XA_DIM] for a in range(XA_HEADS)]
    return zq, zf, zi, zg, zx


def _kv_pieces(kv_ref):
    W = XA_HEADS * XA_DIM
    mk = [kv_ref[:, a * XA_DIM:(a + 1) * XA_DIM] for a in range(XA_HEADS)]
    mv = [kv_ref[:, W + a * XA_DIM:W + (a + 1) * XA_DIM] for a in range(XA_HEADS)]
    return mk, mv


def _lb_pieces(lb_ref):
    return [[lb_ref[r:r + 1, h * HG_DIM:(h + 1) * HG_DIM] for h in range(HG_HEADS)] for r in range(3)]


def _hgrn_fwd(z, lb_logits, gnorm, kv, bl, nc):
    T, zw = z.shape
    mem_len = kv.shape[0] // bl
    cat_w = HG_HEADS * HG_DIM + XA_HEADS * XA_DIM

    def kern(z_ref, lb_ref, gn_ref, kv_ref, cat_ref, st_ref, s_scr):
        @pl.when(pl.program_id(1) == 0)
        def _():
            s_scr[...] = jnp.zeros(s_scr.shape, F32)

        st_ref[...] = s_scr[...]
        zq, zf, zi, zg, zx = _hgrn_pieces(z_ref)
        mk, mv = _kv_pieces(kv_ref)
        l0, l1, l2 = _lb_pieces(lb_ref)
        S = [s_scr[h] for h in range(HG_HEADS)]
        outs, s_new = _hgrn_block(zq, zf, zi, zg, zx, l0, l1, l2, gn_ref[...], mk, mv, S)
        for h in range(HG_HEADS):
            cat_ref[:, h * HG_DIM:(h + 1) * HG_DIM] = outs[h].astype(cat_ref.dtype)
            s_scr[h] = s_new[h]
        base = HG_HEADS * HG_DIM
        for a in range(XA_HEADS):
            cat_ref[:, base + a * XA_DIM:base + (a + 1) * XA_DIM] = outs[HG_HEADS + a].astype(cat_ref.dtype)

    return pl.pallas_call(
        kern, grid=(bl, nc),
        in_specs=[pl.BlockSpec((HG_CHUNK, zw), lambda b, n: (b * nc + n, 0)),
                  pl.BlockSpec(lb_logits.shape, lambda b, n: (0, 0)),
                  pl.BlockSpec(gnorm.shape, lambda b, n: (0, 0)),
                  pl.BlockSpec((mem_len, kv.shape[1]), lambda b, n: (b, 0))],
        out_specs=[pl.BlockSpec((HG_CHUNK, cat_w), lambda b, n: (b * nc + n, 0)),
                   pl.BlockSpec((None, HG_HEADS, HG_DIM, HG_DIM), lambda b, n: (b * nc + n, 0, 0, 0))],
        out_shape=[jax.ShapeDtypeStruct((T, cat_w), BF),
                   jax.ShapeDtypeStruct((bl * nc, HG_HEADS, HG_DIM, HG_DIM), F32)],
        scratch_shapes=[pltpu.VMEM((HG_HEADS, HG_DIM, HG_DIM), F32)],
        name="hgrn_fwd",
        compiler_params=pltpu.CompilerParams(dimension_semantics=("arbitrary", "arbitrary"), vmem_limit_bytes=48 << 20),
    )(z, lb_logits, gnorm, kv)


def _hgrn_bwd(z, dcat, stash, lb_logits, gnorm, kv, bl, nc):
    T, zw = z.shape
    mem_len = kv.shape[0] // bl
    cat_w = dcat.shape[1]

    def kern(z_ref, dc_ref, st_ref, lb_ref, gn_ref, kv_ref, dz_ref, dkv_ref, dlb_ref, dgn_ref, ds_scr):
        first = jnp.logical_and(pl.program_id(0) == 0, pl.program_id(1) == 0)

        @pl.when(pl.program_id(1) == 0)
        def _():
            ds_scr[...] = jnp.zeros(ds_scr.shape, F32)
            dkv_ref[...] = jnp.zeros(dkv_ref.shape, F32)

        @pl.when(first)
        def _():
            dlb_ref[...] = jnp.zeros(dlb_ref.shape, F32)
            dgn_ref[...] = jnp.zeros(dgn_ref.shape, F32)

        zq, zf, zi, zg, zx = _hgrn_pieces(z_ref)
        mk, mv = _kv_pieces(kv_ref)
        l0, l1, l2 = _lb_pieces(lb_ref)
        S = [st_ref[h] for h in range(HG_HEADS)]
        _, vjp = jax.vjp(_hgrn_block, zq, zf, zi, zg, zx, l0, l1, l2, gn_ref[...], mk, mv, S)
        d_outs = [dc_ref[:, h * HG_DIM:(h + 1) * HG_DIM] for h in range(HG_HEADS)]
        base = HG_HEADS * HG_DIM
        d_outs += [dc_ref[:, base + a * XA_DIM:base + (a + 1) * XA_DIM] for a in range(XA_HEADS)]
        d_s = [ds_scr[h] for h in range(HG_HEADS)]
        dzq, dzf, dzi, dzg, dzx, dl0, dl1, dl2, dgn, dmk, dmv, dS = vjp((d_outs, d_s))
        W = HG_HEADS * HG_DIM
        for h in range(HG_HEADS):
            sl = slice(h * HG_DIM, (h + 1) * HG_DIM)
            dz_ref[:, sl] = dzq[h].astype(dz_ref.dtype)
            dz_ref[:, W + h * HG_DIM:W + (h + 1) * HG_DIM] = dzf[h].astype(dz_ref.dtype)
            dz_ref[:, 2 * W + h * HG_DIM:2 * W + (h + 1) * HG_DIM] = dzi[h].astype(dz_ref.dtype)
            dz_ref[:, 3 * W + h * HG_DIM:3 * W + (h + 1) * HG_DIM] = dzg[h].astype(dz_ref.dtype)
            ds_scr[h] = dS[h]
            dlb_ref[0:1, sl] += dl0[h]
            dlb_ref[1:2, sl] += dl1[h]
            dlb_ref[2:3, sl] += dl2[h]
        dgn_ref[...] += dgn
        KW = XA_HEADS * XA_DIM
        for a in range(XA_HEADS):
            dz_ref[:, 4 * W + a * XA_DIM:4 * W + (a + 1) * XA_DIM] = dzx[a].astype(dz_ref.dtype)
            dkv_ref[:, a * XA_DIM:(a + 1) * XA_DIM] += dmk[a]
            dkv_ref[:, KW + a * XA_DIM:KW + (a + 1) * XA_DIM] += dmv[a]

    rev = lambda b, n: (b * nc + (nc - 1 - n), 0)
    return pl.pallas_call(
        kern, grid=(bl, nc),
        in_specs=[pl.BlockSpec((HG_CHUNK, zw), rev),
                  pl.BlockSpec((HG_CHUNK, cat_w), rev),
                  pl.BlockSpec((None, HG_HEADS, HG_DIM, HG_DIM), lambda b, n: (b * nc + (nc - 1 - n), 0, 0, 0)),
                  pl.BlockSpec(lb_logits.shape, lambda b, n: (0, 0)),
                  pl.BlockSpec(gnorm.shape, lambda b, n: (0, 0)),
                  pl.BlockSpec((mem_len, kv.shape[1]), lambda b, n: (b, 0))],
        out_specs=[pl.BlockSpec((HG_CHUNK, zw), rev),
                   pl.BlockSpec((mem_len, kv.shape[1]), lambda b, n: (b, 0)),
                   pl.BlockSpec(lb_logits.shape, lambda b, n: (0, 0)),
                   pl.BlockSpec(gnorm.shape, lambda b, n: (0, 0))],
        out_shape=[jax.ShapeDtypeStruct((T, zw), BF), jax.ShapeDtypeStruct(kv.shape, F32),
                   jax.ShapeDtypeStruct(lb_logits.shape, F32), jax.ShapeDtypeStruct(gnorm.shape, F32)],
        scratch_shapes=[pltpu.VMEM((HG_HEADS, HG_DIM, HG_DIM), F32)],
        name="hgrn_bwd",
        compiler_params=pltpu.CompilerParams(dimension_semantics=("arbitrary", "arbitrary"), vmem_limit_bytes=56 << 20),
    )(z, dcat, stash, lb_logits, gnorm, kv)


def _gmlp_pieces(z_ref):
    W = GM_GROUPS * GM_GROUP_DIM
    zu = [z_ref[:, g * GM_GROUP_DIM:(g + 1) * GM_GROUP_DIM] for g in range(GM_GROUPS)]
    zv = [z_ref[:, W + g * GM_GROUP_DIM:W + (g + 1) * GM_GROUP_DIM] for g in range(GM_GROUPS)]
    zx = [z_ref[:, 2 * W + a * XA_DIM:2 * W + (a + 1) * XA_DIM] for a in range(XA_HEADS)]
    return zu, zv, zx


def _gmlp_params(lng_ref, lnb_ref, ws_ref, bs_ref):
    lng = [lng_ref[:, g * GM_GROUP_DIM:(g + 1) * GM_GROUP_DIM] for g in range(GM_GROUPS)]
    lnb = [lnb_ref[:, g * GM_GROUP_DIM:(g + 1) * GM_GROUP_DIM] for g in range(GM_GROUPS)]
    ws = [ws_ref[g] for g in range(GM_GROUPS)]
    bs = [bs_ref[g:g + 1, :] for g in range(GM_GROUPS)]
    return lng, lnb, ws, bs


def _gmlp_fwd(z, ln_g, ln_b, w_s, b_s, kv, bl, nc):
    T, zw = z.shape
    mem_len = kv.shape[0] // bl
    cat_w = GM_GROUPS * GM_GROUP_DIM + XA_HEADS * XA_DIM

    def kern(z_ref, lng_ref, lnb_ref, ws_ref, bs_ref, kv_ref, cat_ref):
        zu, zv, zx = _gmlp_pieces(z_ref)
        lng, lnb, ws, bs = _gmlp_params(lng_ref, lnb_ref, ws_ref, bs_ref)
        mk, mv = _kv_pieces(kv_ref)
        outs = _gmlp_block(zu, zv, zx, lng, lnb, ws, bs, mk, mv)
        for g in range(GM_GROUPS):
            cat_ref[:, g * GM_GROUP_DIM:(g + 1) * GM_GROUP_DIM] = outs[g].astype(cat_ref.dtype)
        base = GM_GROUPS * GM_GROUP_DIM
        for a in range(XA_HEADS):
            cat_ref[:, base + a * XA_DIM:base + (a + 1) * XA_DIM] = outs[GM_GROUPS + a].astype(cat_ref.dtype)

    full2 = lambda b, n: (0, 0)
    return pl.pallas_call(
        kern, grid=(bl, nc),
        in_specs=[pl.BlockSpec((GM_CHUNK, zw), lambda b, n: (b * nc + n, 0)),
                  pl.BlockSpec(ln_g.shape, full2), pl.BlockSpec(ln_b.shape, full2),
                  pl.BlockSpec(w_s.shape, lambda b, n: (0, 0, 0)), pl.BlockSpec(b_s.shape, full2),
                  pl.BlockSpec((mem_len, kv.shape[1]), lambda b, n: (b, 0))],
        out_specs=pl.BlockSpec((GM_CHUNK, cat_w), lambda b, n: (b * nc + n, 0)),
        out_shape=jax.ShapeDtypeStruct((T, cat_w), BF),
        name="gmlp_fwd",
        compiler_params=pltpu.CompilerParams(dimension_semantics=("arbitrary", "arbitrary"), vmem_limit_bytes=48 << 20),
    )(z, ln_g, ln_b, w_s, b_s, kv)


def _gmlp_bwd(z, dcat, ln_g, ln_b, w_s, b_s, kv, bl, nc):
    T, zw = z.shape
    mem_len = kv.shape[0] // bl
    cat_w = dcat.shape[1]

    def kern(z_ref, dc_ref, lng_ref, lnb_ref, ws_ref, bs_ref, kv_ref,
             dz_ref, dkv_ref, dlng_ref, dlnb_ref, dws_ref, dbs_ref):
        first = jnp.logical_and(pl.program_id(0) == 0, pl.program_id(1) == 0)

        @pl.when(pl.program_id(1) == 0)
        def _():
            dkv_ref[...] = jnp.zeros(dkv_ref.shape, F32)

        @pl.when(first)
        def _():
            dlng_ref[...] = jnp.zeros(dlng_ref.shape, F32)
            dlnb_ref[...] = jnp.zeros(dlnb_ref.shape, F32)
            dws_ref[...] = jnp.zeros(dws_ref.shape, F32)
            dbs_ref[...] = jnp.zeros(dbs_ref.shape, F32)

        zu, zv, zx = _gmlp_pieces(z_ref)
        lng, lnb, ws, bs = _gmlp_params(lng_ref, lnb_ref, ws_ref, bs_ref)
        mk, mv = _kv_pieces(kv_ref)
        _, vjp = jax.vjp(_gmlp_block, zu, zv, zx, lng, lnb, ws, bs, mk, mv)
        d_outs = [dc_ref[:, g * GM_GROUP_DIM:(g + 1) * GM_GROUP_DIM] for g in range(GM_GROUPS)]
        base = GM_GROUPS * GM_GROUP_DIM
        d_outs += [dc_ref[:, base + a * XA_DIM:base + (a + 1) * XA_DIM] for a in range(XA_HEADS)]
        dzu, dzv, dzx, dlng, dlnb, dws, dbs, dmk, dmv = vjp(d_outs)
        W = GM_GROUPS * GM_GROUP_DIM
        for g in range(GM_GROUPS):
            sl = slice(g * GM_GROUP_DIM, (g + 1) * GM_GROUP_DIM)
            dz_ref[:, sl] = dzu[g].astype(dz_ref.dtype)
            dz_ref[:, W + g * GM_GROUP_DIM:W + (g + 1) * GM_GROUP_DIM] = dzv[g].astype(dz_ref.dtype)
            dlng_ref[:, sl] += dlng[g]
            dlnb_ref[:, sl] += dlnb[g]
            dws_ref[g] += dws[g]
            dbs_ref[g:g + 1, :] += dbs[g]
        KW = XA_HEADS * XA_DIM
        for a in range(XA_HEADS):
            dz_ref[:, 2 * W + a * XA_DIM:2 * W + (a + 1) * XA_DIM] = dzx[a].astype(dz_ref.dtype)
            dkv_ref[:, a * XA_DIM:(a + 1) * XA_DIM] += dmk[a]
            dkv_ref[:, KW + a * XA_DIM:KW + (a + 1) * XA_DIM] += dmv[a]

    full2 = lambda b, n: (0, 0)
    full3 = lambda b, n: (0, 0, 0)
    blk = lambda b, n: (b * nc + n, 0)
    return pl.pallas_call(
        kern, grid=(bl, nc),
        in_specs=[pl.BlockSpec((GM_CHUNK, zw), blk), pl.BlockSpec((GM_CHUNK, cat_w), blk),
                  pl.BlockSpec(ln_g.shape, full2), pl.BlockSpec(ln_b.shape, full2),
                  pl.BlockSpec(w_s.shape, full3), pl.BlockSpec(b_s.shape, full2),
                  pl.BlockSpec((mem_len, kv.shape[1]), lambda b, n: (b, 0))],
        out_specs=[pl.BlockSpec((GM_CHUNK, zw), blk),
                   pl.BlockSpec((mem_len, kv.shape[1]), lambda b, n: (b, 0)),
                   pl.BlockSpec(ln_g.shape, full2), pl.BlockSpec(ln_b.shape, full2),
                   pl.BlockSpec(w_s.shape, full3), pl.BlockSpec(b_s.shape, full2)],
        out_shape=[jax.ShapeDtypeStruct((T, zw), BF), jax.ShapeDtypeStruct(kv.shape, F32),
                   jax.ShapeDtypeStruct(ln_g.shape, F32), jax.ShapeDtypeStruct(ln_b.shape, F32),
                   jax.ShapeDtypeStruct(w_s.shape, F32), jax.ShapeDtypeStruct(b_s.shape, F32)],
        name="gmlp_bwd",
        compiler_params=pltpu.CompilerParams(dimension_semantics=("arbitrary", "arbitrary"), vmem_limit_bytes=56 << 20),
    )(z, dcat, ln_g, ln_b, w_s, b_s, kv)


def _place():
    x, y, c = lax.axis_index("x"), lax.axis_index("y"), lax.axis_index("c")
    chips = [(1 - x, y), (x, 1 - y), (1 - x, 1 - y)]
    return x, y, c, chips


def _half(ref, kind, e):
    if kind == "col":
        n = ref.shape[1] // 2
        return ref.at[:, pl.ds(pl.multiple_of(e * n, n), n), :]
    n = ref.shape[2] // 2
    return ref.at[:, :, pl.ds(pl.multiple_of(e * n, n), n)]


def _slot(ref, kind, j, n):
    if kind == "col":
        return ref.at[:, :, pl.ds(pl.multiple_of(j * n, n), n)]
    return ref.at[:, pl.ds(pl.multiple_of(j * n, n), n), :]


def _allgather_seq(name, items, cid):
    nt = len(items)
    kinds = [k for (_, k, _) in items]
    out_type = []
    for s, k, l in items:
        L, r, c = s.shape
        lo = L if l is None else 1
        out_type.append(jax.ShapeDtypeStruct((lo, r, 4 * c) if k == "col" else (lo, 4 * r, c), s.dtype))

    def body(*refs):
        sh = [refs[t] if items[t][2] is None else refs[t].at[pl.ds(items[t][2], 1)] for t in range(nt)]
        full = refs[nt:2 * nt]
        loc, s_ici, r_ici, s_d2d, r_d2d = refs[2 * nt:]
        x, y, c, chips = _place()
        own = 2 * x + y
        sibling = (x, y, 1 - c)
        barrier = pltpu.get_barrier_semaphore()
        for peer in [(px, py, c) for (px, py) in chips] + [sibling]:
            pl.semaphore_signal(barrier, inc=1, device_id=peer, device_id_type=MESH)
        pl.semaphore_wait(barrier, 4)
        width = [sh[t].shape[2] if kinds[t] == "col" else sh[t].shape[1] for t in range(nt)]
        started = []
        for t in range(nt):
            mine = pltpu.make_async_copy(sh[t], _slot(full[t], kinds[t], own, width[t]), loc.at[t])
            mine.start()
            started.append(mine)
        sent = []
        for t in range(nt):
            for p, (px, py) in enumerate(chips):
                cp = pltpu.make_async_remote_copy(
                    src_ref=_half(sh[t], kinds[t], c), dst_ref=_half(_slot(full[t], kinds[t], own, width[t]), kinds[t], c),
                    send_sem=s_ici.at[t, p], recv_sem=r_ici.at[t, p], device_id=(px, py, c), device_id_type=MESH)
                cp.start()
                sent.append(cp)
        for t in range(nt):
            for p, (px, py) in enumerate(chips):
                landed = _half(_slot(full[t], kinds[t], 2 * px + py, width[t]), kinds[t], c)
                pltpu.make_async_remote_copy(
                    src_ref=landed, dst_ref=landed, send_sem=s_ici.at[t, p], recv_sem=r_ici.at[t, p],
                    device_id=(px, py, c), device_id_type=MESH).wait_recv()
                fw = pltpu.make_async_remote_copy(
                    src_ref=landed, dst_ref=landed, send_sem=s_d2d.at[t, p], recv_sem=r_d2d.at[t, p],
                    device_id=sibling, device_id_type=MESH)
                fw.start()
                sent.append(fw)
        for t in range(nt):
            for p, (px, py) in enumerate(chips):
                other = _half(_slot(full[t], kinds[t], 2 * px + py, width[t]), kinds[t], 1 - c)
                pltpu.make_async_remote_copy(
                    src_ref=other, dst_ref=other, send_sem=s_d2d.at[t, p], recv_sem=r_d2d.at[t, p],
                    device_id=sibling, device_id_type=MESH).wait_recv()
        for cp in sent:
            cp.wait_send()
        for cp in started:
            cp.wait()

    return pl.kernel(
        body, out_type=out_type, mesh=plsc.ScalarSubcoreMesh(axis_name="seq", num_cores=1),
        scratch_types=[pltpu.SemaphoreType.DMA((nt,)), pltpu.SemaphoreType.DMA((nt, 3)), pltpu.SemaphoreType.DMA((nt, 3)),
                       pltpu.SemaphoreType.DMA((nt, 3)), pltpu.SemaphoreType.DMA((nt, 3))],
        compiler_params=pltpu.CompilerParams(collective_id=cid), name=name,
    )(*[s for (s, _, _) in items])


def _slot2(ref, kind, j, n):
    if kind == "col":
        return ref.at[:, pl.ds(pl.multiple_of(j * n, n), n)]
    return ref.at[pl.ds(pl.multiple_of(j * n, n), n), :]


def _rs_chips_seq(name, parts, kinds, cid):
    nm = len(parts)
    out_type = []
    for g, k in zip(parts, kinds):
        r, c = g.shape
        ps = (r, c // 4) if k == "col" else (r // 4, c)
        out_type += [jax.ShapeDtypeStruct(ps, BF), jax.ShapeDtypeStruct((3,) + ps, BF)]

    def body(*refs):
        g = refs[:nm]
        outs = refs[nm:3 * nm]
        loc, ssem, rsem = refs[3 * nm:]
        x, y, c, chips = _place()
        own = 2 * x + y
        barrier = pltpu.get_barrier_semaphore()
        for (px, py) in chips:
            pl.semaphore_signal(barrier, inc=1, device_id=(px, py, c), device_id_type=MESH)
        pl.semaphore_wait(barrier, 3)
        cps = []
        for m in range(nm):
            k = kinds[m]
            own_o, got_o = outs[2 * m], outs[2 * m + 1]
            n = g[m].shape[1] // 4 if k == "col" else g[m].shape[0] // 4
            lc = pltpu.make_async_copy(_slot2(g[m], k, own, n), own_o, loc.at[m])
            lc.start()
            cps.append(lc)
            for p, (px, py) in enumerate(chips):
                cp = pltpu.make_async_remote_copy(
                    src_ref=_slot2(g[m], k, 2 * px + py, n), dst_ref=got_o.at[p],
                    send_sem=ssem.at[m, p], recv_sem=rsem.at[m, p], device_id=(px, py, c), device_id_type=MESH)
                cp.start()
                cps.append(cp)
        for cp in cps:
            cp.wait()

    return pl.kernel(
        body, out_type=out_type, mesh=plsc.ScalarSubcoreMesh(axis_name="seq", num_cores=1),
        scratch_types=[pltpu.SemaphoreType.DMA((nm,)), pltpu.SemaphoreType.DMA((nm, 3)), pltpu.SemaphoreType.DMA((nm, 3))],
        compiler_params=pltpu.CompilerParams(collective_id=cid), name=name,
    )(*parts)


def _finish_share(name, own, got, kind, c_arr):
    L, r, c = own.shape
    tr = _pick(r, 128 if kind == "col" else 256)
    nb = r // tr
    nq = L * nb
    own2 = own.reshape(L * r, c)
    got2 = got.reshape(3 * L * r, c)
    pick = lambda h, q: q * (1 - h) + (nq - 1) * h
    in_specs = [pl.BlockSpec((tr, c), lambda h, q, cc: (pick(h, q), 0))]
    in_specs += [pl.BlockSpec((tr, c), functools.partial(lambda h, q, cc, p: (p * nq + pick(h, q), 0), p=p)) for p in range(3)]
    if kind == "col":
        out_sd = (L, 2, r, c)
        o_spec = pl.BlockSpec((None, 2, tr, c), lambda h, q, cc: ((q * h) // nb, 0, (q * h) % nb, 0))
    else:
        out_sd = (L * r, 2 * c)
        o_spec = pl.BlockSpec((tr, 2 * c), lambda h, q, cc: (q * h, 0))

    def kern(c_ref, o_ref, g0, g1, g2, out_ref, mine, recv, ssem, rsem):
        h, q = pl.program_id(0), pl.program_id(1)
        x, y, cc, _ = _place()

        def swap(qq):
            return pltpu.make_async_remote_copy(src_ref=mine.at[qq], dst_ref=recv.at[qq], send_sem=ssem.at[qq],
                                                recv_sem=rsem.at[qq], device_id=(x, y, 1 - cc), device_id_type=MESH)

        @pl.when(h == 0)
        def _():
            mine[q] = ((o_ref[...].astype(F32) + g0[...].astype(F32)) + g1[...].astype(F32)) + g2[...].astype(F32)
            swap(q).start()

        @pl.when(h == 1)
        def _():
            swap(q).wait()
            a, b = mine[q], recv[q]
            first = c_ref[0] == 0
            lo, hi = jnp.where(first, a, b), jnp.where(first, b, a)
            if kind == "col":
                out_ref[0] = lo
                out_ref[1] = hi
            else:
                out_ref[:, :c] = lo
                out_ref[:, c:] = hi

    est = 2 * nq * tr * c * 4 + 6 * tr * c * 4 + 8 * tr * c * 2
    full = pl.pallas_call(
        kern,
        grid_spec=pltpu.PrefetchScalarGridSpec(
            num_scalar_prefetch=1, grid=(2, nq), in_specs=in_specs, out_specs=o_spec,
            scratch_shapes=[pltpu.VMEM((nq, tr, c), F32), pltpu.VMEM((nq, tr, c), F32),
                            pltpu.SemaphoreType.DMA((nq,)), pltpu.SemaphoreType.DMA((nq,))]),
        out_shape=jax.ShapeDtypeStruct(out_sd, F32), name=name,
        compiler_params=pltpu.CompilerParams(dimension_semantics=("arbitrary", "arbitrary"),
                                             vmem_limit_bytes=int(min(VMEM_CAP_BYTES, est + (12 << 20)))),
    )(c_arr, own2, got2, got2, got2)
    return full.reshape(L, 2 * r, c) if kind == "col" else full.reshape(L, r, 2 * c)


def _small_allreduce(buf, name):
    R = buf.shape[0]

    def body(x_ref, o_ref, slots, ssem, rsem):
        x, y, c, _ = _place()
        me = 4 * x + 2 * y + c
        slots[0] = x_ref[...]
        cps = []
        for k in range(1, 8):
            bx, by, bc = (k >> 2) & 1, (k >> 1) & 1, k & 1
            peer = (1 - x if bx else x, 1 - y if by else y, 1 - c if bc else c)
            cp = pltpu.make_async_remote_copy(src_ref=x_ref, dst_ref=slots.at[k], send_sem=ssem.at[k - 1],
                                              recv_sem=rsem.at[k - 1], device_id=peer, device_id_type=MESH)
            cp.start()
            cps.append(cp)
        for cp in cps:
            cp.wait()
        acc = slots[jnp.bitwise_xor(me, 0)]
        for d in range(1, 8):
            acc = acc + slots[jnp.bitwise_xor(me, d)]
        o_ref[...] = acc

    vm = pl.BlockSpec(memory_space=pltpu.VMEM)
    return pl.pallas_call(
        body, out_shape=jax.ShapeDtypeStruct(buf.shape, F32), in_specs=[vm], out_specs=vm,
        scratch_shapes=[pltpu.VMEM((8, R, LANES), F32), pltpu.SemaphoreType.DMA((7,)), pltpu.SemaphoreType.DMA((7,))],
        name=name,
        compiler_params=pltpu.CompilerParams(vmem_limit_bytes=int(min(VMEM_CAP_BYTES, 12 * R * LANES * 4 + (8 << 20)))),
    )(buf)


PACK_TILE_ROWS = 8


def _item_rows(shape):
    n = 1
    for d in shape:
        n *= d
    return -(-n // (PACK_TILE_ROWS * LANES)) * PACK_TILE_ROWS


def _pack(arrs, rows_total):
    buf = jnp.zeros((rows_total, LANES), F32)
    r = 0
    for a in arrs:
        f = a.reshape(-1).astype(F32)
        nr = _item_rows(a.shape)
        block = jnp.pad(f, (0, nr * LANES - f.shape[0])).reshape(nr, LANES)
        buf = lax.dynamic_update_slice(buf, block, (r, 0))
        r += nr
    return buf


def _unpack(buf, shapes):
    out, r = [], 0
    for s in shapes:
        n = 1
        for d in s:
            n *= d
        nr = _item_rows(s)
        out.append(buf[r:r + nr].reshape(-1)[:n].reshape(s))
        r += nr
    return out


def _rows_needed(shapes):
    return sum(_item_rows(s) for s in shapes)


def _two_rows(a, b):
    out = jnp.zeros((2, a.shape[1]), a.dtype)
    return lax.dynamic_update_slice(lax.dynamic_update_slice(out, a, (0, 0)), b, (1, 0))


def _adam(w, g, m, v):
    m = ADAM_B1 * m + (1.0 - ADAM_B1) * g
    v = ADAM_B2 * v + (1.0 - ADAM_B2) * jnp.square(g)
    m_hat = m / (1.0 - ADAM_B1 ** ADAM_STEP)
    v_hat = v / (1.0 - ADAM_B2 ** ADAM_STEP)
    delta = -ADAM_LR * (m_hat / (jnp.sqrt(v_hat) + ADAM_EPS) + ADAM_WD * w)
    return delta, m, v


def _adam_call(name, w2, g2, m2, v2, tr):
    def fn(rv, cv):
        return list(_adam(*rv)), []

    width = w2.shape[1]
    return _rowcall(name, fn, [(w2, 0, width), (g2, 0, width), (m2, 0, width), (v2, 0, width)], [],
                    [(width, F32)] * 3, [], tr)


def kernel(x, mem, mem_norm, lb_logits, ffn1_norm, ffn1_w_in, ffn1_w_out, mix_norm, mem_w_kv, hgrn_w_in, hgrn_gnorm, hgrn_w_out, gmlp_w_in, gmlp_ln_g, gmlp_ln_b, gmlp_w_s, gmlp_b_s, gmlp_w_out, ffn2_norm, ffn2_w_in, ffn2_w_out, final_norm, loss_target, m_mem_norm, m_lb_logits, m_ffn1_norm, m_ffn1_w_in, m_ffn1_w_out, m_mix_norm, m_mem_w_kv, m_hgrn_w_in, m_hgrn_gnorm, m_hgrn_w_out, m_gmlp_w_in, m_gmlp_ln_g, m_gmlp_ln_b, m_gmlp_w_s, m_gmlp_b_s, m_gmlp_w_out, m_ffn2_norm, m_ffn2_w_in, m_ffn2_w_out, m_final_norm, v_mem_norm, v_lb_logits, v_ffn1_norm, v_ffn1_w_in, v_ffn1_w_out, v_mix_norm, v_mem_w_kv, v_hgrn_w_in, v_hgrn_gnorm, v_hgrn_w_out, v_gmlp_w_in, v_gmlp_ln_g, v_gmlp_ln_b, v_gmlp_w_s, v_gmlp_b_s, v_gmlp_w_out, v_ffn2_norm, v_ffn2_w_in, v_ffn2_w_out, v_final_norm):
    bl, seq, D = x.shape
    T = bl * seq
    mem_len = mem.shape[1]
    chip = 2 * lax.axis_index("x") + lax.axis_index("y")
    c_arr = lax.axis_index("c").astype(jnp.int32).reshape(1)
    TR = 256

    big = [("ffn1_w_in", ffn1_w_in, "col"), ("ffn1_w_out", ffn1_w_out, "row"), ("mem_w_kv", mem_w_kv, "col"),
           ("hgrn_w_in", hgrn_w_in, "col"), ("hgrn_w_out", hgrn_w_out, "row"), ("gmlp_w_in", gmlp_w_in, "col"),
           ("gmlp_w_out", gmlp_w_out, "row"), ("ffn2_w_in", ffn2_w_in, "col"), ("ffn2_w_out", ffn2_w_out, "row")]
    kinds = [k for (_, _, k) in big]
    shards_bf = []
    for nm, w, _ in big:
        L, r, c = w.shape
        (wb,) = _rowcall("cast_" + nm, lambda rv, cv: ([rv[0]], []), [(w.reshape(L * r, c), 0, c)], [], [(c, BF)], [], 512)
        shards_bf.append(wb.reshape(L, r, c))
    sb = dict(zip([nm for (nm, _, _) in big], shards_bf))
    groups = [[("ffn1_w_in", 0), ("ffn1_w_out", 0)],
              [("mem_w_kv", None), ("hgrn_w_in", None), ("hgrn_w_out", None)],
              [("ffn2_w_in", 0), ("ffn2_w_out", 0)],
              [("ffn1_w_in", 1), ("ffn1_w_out", 1)],
              [("gmlp_w_in", None), ("gmlp_w_out", None)],
              [("ffn2_w_in", 1), ("ffn2_w_out", 1)]]
    kind_of = {nm: k for (nm, _, k) in big}
    gathered = {nm: [None, None] for nm in ("ffn1_w_in", "ffn1_w_out", "ffn2_w_in", "ffn2_w_out")}
    for gi, grp in enumerate(groups):
        outs = _allgather_seq("gather_%d" % gi, [(sb[nm], kind_of[nm], l) for (nm, l) in grp], gi)
        for (nm, l), o in zip(grp, outs):
            if l is None:
                gathered[nm] = o
            else:
                gathered[nm][l] = o

    ln_w = GM_GROUPS * GM_GROUP_DIM
    placed = lax.dynamic_update_slice(jnp.zeros((8, ln_w), F32), 0.5 * gmlp_ln_g, (0, chip * gmlp_ln_g.shape[1]))
    placed = lax.dynamic_update_slice(placed, 0.5 * gmlp_ln_b, (1, chip * gmlp_ln_g.shape[1]))
    ln_full = _small_allreduce(placed.reshape(16, LANES), "gather_ln").reshape(8, ln_w)
    ln_g_full, ln_b_full = ln_full[0:1], ln_full[1:2]

    def rms_fwd(name, xin, g):
        (h,) = _rowcall(name, lambda rv, cv: ([_rmsnorm(rv[0], cv[0])], []), [(xin, 0, D)], [g.reshape(1, D)], [(D, BF)], [], TR)
        return h

    def ffn_fwd(tag, xin, g, w_in, w_out, layer):
        dff = w_out[layer].shape[1]
        h = rms_fwd("rms_" + tag, xin, g)
        z = _mm("ffn_in_" + tag, h, w_in[layer], "nn", BF, 2048, 512, D, b_lead=0)
        (a,) = _rowcall("swiglu_" + tag, lambda rv, cv: ([_silu(rv[0].astype(F32)) * rv[1].astype(F32)], []),
                        [(z, 0, dff), (z, 1, dff)], [], [(dff, BF)], [], TR)
        xo = _mm("ffn_out_" + tag, a, w_out[layer], "nn", F32, 1024, 1024, dff, scale=0.5, res=xin, b_lead=0)
        return xo, (xin, h, z, a)

    def ffn_bwd(tag, dxo, saved, g, w_in, w_out, layer):
        xin, h, z, a = saved
        dff = w_out[layer].shape[1]
        da = _mm("ffn_da_" + tag, dxo, w_out[layer], "nt", BF, 2048, dff // 2, D, scale=0.5, b_lead=0)
        dw_out = _mm_tn_pair("ffn_dwo_" + tag, a, dxo, "row", c_arr, dff // 2, T, scale=0.5)

        def sw_bwd(rv, cv):
            gt, up, d = rv[0].astype(F32), rv[1].astype(F32), rv[2].astype(F32)
            _, vjp = jax.vjp(lambda p, q: _silu(p) * q, gt, up)
            dg, du = vjp(d)
            return [jnp.concatenate([dg, du], axis=1)], []

        (dz,) = _rowcall("swiglu_bwd_" + tag, sw_bwd, [(z, 0, dff), (z, 1, dff), (da, 0, dff)], [], [(2 * dff, BF)], [], TR)
        dh = _mm("ffn_dh_" + tag, dz, w_in[layer], "nt", F32, 1024, 512, 2 * dff, b_lead=0)
        dw_in = _mm_tn_pair("ffn_dwi_" + tag, h, dz, "col", c_arr, 512, T)
        dx, dg = rms_bwd("rms_bwd_" + tag, xin, g, dh, dxo)
        return dx, dg, dw_in, dw_out

    def rms_bwd(name, xin, g, dh, dres):
        def fn(rv, cv):
            _, vjp = jax.vjp(_rmsnorm, rv[0], cv[0])
            dx, dg = vjp(rv[1])
            if dres is not None:
                dx = dx + rv[2]
            return [dx], [dg]

        rows = [(xin, 0, D), (dh, 0, D)] + ([(dres, 0, D)] if dres is not None else [])
        dx, dg = _rowcall(name, fn, rows, [g.reshape(1, D)], [(D, F32)], [((1, D), F32)], TR)
        return dx, dg

    x0 = x.reshape(T, D)
    tgt = loss_target.reshape(T, D)
    mem2 = mem.reshape(bl * mem_len, D)
    memn = rms_fwd("rms_mem", mem2, mem_norm)
    kv = [_mm("kv_%d" % i, memn, gathered["mem_w_kv"], "nn", F32, 512, 512, D, b_lead=i) for i in range(2)]

    x1, sv_f10 = ffn_fwd("f1l0", x0, ffn1_norm[0], gathered["ffn1_w_in"], gathered["ffn1_w_out"], 0)
    h_m0 = rms_fwd("rms_mix0", x1, mix_norm[0])
    z_m0 = _mm("mix_in_0", h_m0, gathered["hgrn_w_in"], "nn", F32, 2048, 512, D, b_lead=0)
    nc0 = seq // HG_CHUNK
    cat0, stash0 = _hgrn_fwd(z_m0, lb_logits, hgrn_gnorm, kv[0], bl, nc0)
    x2 = _mm("mix_out_0", cat0, gathered["hgrn_w_out"], "nn", F32, 1024, 1024, cat0.shape[1], res=x1, b_lead=0)
    x3, sv_f20 = ffn_fwd("f2l0", x2, ffn2_norm[0], gathered["ffn2_w_in"], gathered["ffn2_w_out"], 0)
    x4, sv_f11 = ffn_fwd("f1l1", x3, ffn1_norm[1], gathered["ffn1_w_in"], gathered["ffn1_w_out"], 1)
    h_m1 = rms_fwd("rms_mix1", x4, mix_norm[1])
    z_m1 = _mm("mix_in_1", h_m1, gathered["gmlp_w_in"], "nn", F32, 2048, 512, D, b_lead=0)
    nc1 = seq // GM_CHUNK
    w_s, b_s = gmlp_w_s[0], gmlp_b_s[0]
    cat1 = _gmlp_fwd(z_m1, ln_g_full, ln_b_full, w_s, b_s, kv[1], bl, nc1)
    x5 = _mm("mix_out_1", cat1, gathered["gmlp_w_out"], "nn", F32, 1024, 1024, cat1.shape[1], res=x4, b_lead=0)
    x6, sv_f21 = ffn_fwd("f2l1", x5, ffn2_norm[1], gathered["ffn2_w_in"], gathered["ffn2_w_out"], 1)

    def head(rv, cv):
        def f(xx, gg):
            err = _rmsnorm(xx, gg) - rv[1]
            return 0.5 * jnp.sum(jnp.mean(err * err, axis=-1, keepdims=True), axis=0, keepdims=True)

        ls, vjp = jax.vjp(f, rv[0], cv[0])
        dx, dg = vjp(jnp.ones((1, 1), F32))
        return [dx], [dg, jnp.broadcast_to(ls, (1, 128))]

    dx6, d_final, loss_part = _rowcall("loss_head", head, [(x6, 0, D), (tgt, 0, D)], [final_norm.reshape(1, D)],
                                       [(D, F32)], [((1, D), F32), ((1, 128), F32)], TR)

    rs_out = {}
    n_gather = len(groups)

    def rs(gi, items):
        outs = _rs_chips_seq("reduce_%d" % gi, [p for (_, p, _) in items], [k for (_, _, k) in items], n_gather + gi)
        for i, (key, _, _) in enumerate(items):
            rs_out[key] = (outs[2 * i], outs[2 * i + 1])

    dx5, dg_f21, dwi_f21, dwo_f21 = ffn_bwd("f2l1", dx6, sv_f21, ffn2_norm[1], gathered["ffn2_w_in"], gathered["ffn2_w_out"], 1)
    rs(0, [(("ffn2_w_out", 1), dwo_f21, "row"), (("ffn2_w_in", 1), dwi_f21, "col")])
    dcat1 = _mm("mix_dcat_1", dx5, gathered["gmlp_w_out"], "nt", F32, 2048, 1024, D, b_lead=0)
    dwo_m1 = _mm_tn_pair("mix_dwo_1", cat1, dx5, "row", c_arr, 1024, T)
    dz_m1, dkv1, d_lng, d_lnb, d_ws, d_bs = _gmlp_bwd(z_m1, dcat1, ln_g_full, ln_b_full, w_s, b_s, kv[1], bl, nc1)
    dh_m1 = _mm("mix_dh_1", dz_m1, gathered["gmlp_w_in"], "nt", F32, 1024, 512, z_m1.shape[1], b_lead=0)
    dwi_m1 = _mm_tn_pair("mix_dwi_1", h_m1, dz_m1, "col", c_arr, 1024, T)
    rs(1, [(("gmlp_w_out", 0), dwo_m1, "row"), (("gmlp_w_in", 0), dwi_m1, "col")])
    dx4, dg_m1 = rms_bwd("rms_bwd_mix1", x4, mix_norm[1], dh_m1, dx5)
    dx3, dg_f11, dwi_f11, dwo_f11 = ffn_bwd("f1l1", dx4, sv_f11, ffn1_norm[1], gathered["ffn1_w_in"], gathered["ffn1_w_out"], 1)
    rs(2, [(("ffn1_w_out", 1), dwo_f11, "row"), (("ffn1_w_in", 1), dwi_f11, "col")])

    dx2, dg_f20, dwi_f20, dwo_f20 = ffn_bwd("f2l0", dx3, sv_f20, ffn2_norm[0], gathered["ffn2_w_in"], gathered["ffn2_w_out"], 0)
    rs(3, [(("ffn2_w_out", 0), dwo_f20, "row"), (("ffn2_w_in", 0), dwi_f20, "col")])
    dcat0 = _mm("mix_dcat_0", dx2, gathered["hgrn_w_out"], "nt", F32, 2048, 1024, D, b_lead=0)
    dwo_m0 = _mm_tn_pair("mix_dwo_0", cat0, dx2, "row", c_arr, 1024, T)
    dz_m0, dkv0, d_lb, d_gn = _hgrn_bwd(z_m0, dcat0, stash0, lb_logits, hgrn_gnorm, kv[0], bl, nc0)
    dh_m0 = _mm("mix_dh_0", dz_m0, gathered["hgrn_w_in"], "nt", F32, 1024, 512, z_m0.shape[1], b_lead=0)
    dwi_m0 = _mm_tn_pair("mix_dwi_0", h_m0, dz_m0, "col", c_arr, 1024, T)
    rs(4, [(("hgrn_w_out", 0), dwo_m0, "row"), (("hgrn_w_in", 0), dwi_m0, "col")])
    dx1, dg_m0 = rms_bwd("rms_bwd_mix0", x1, mix_norm[0], dh_m0, dx2)

    dwkv = [_mm_tn_pair("kv_dw_%d" % i, memn, dkv, "col", c_arr, 1024, 512) for i, dkv in enumerate([dkv0, dkv1])]
    rs(5, [(("mem_w_kv", 0), dwkv[0], "col"), (("mem_w_kv", 1), dwkv[1], "col")])
    dmemn = _mm("kv_dx_0", dkv0, gathered["mem_w_kv"], "nt", F32, 512, 512, 1024, b_lead=0)
    dmemn = _mm("kv_dx_1", dkv1, gathered["mem_w_kv"], "nt", F32, 512, 512, 1024, res=dmemn, b_lead=1)
    _, d_memnorm = rms_bwd("rms_bwd_mem", mem2, mem_norm, dmemn, None)

    dx0, dg_f10, dwi_f10, dwo_f10 = ffn_bwd("f1l0", dx1, sv_f10, ffn1_norm[0], gathered["ffn1_w_in"], gathered["ffn1_w_out"], 0)
    rs(6, [(("ffn1_w_out", 0), dwo_f10, "row")])
    rs(7, [(("ffn1_w_in", 0), dwi_f10, "col")])

    shard_grads = []
    for (nm, w, k) in big:
        per_layer = []
        for l in range(w.shape[0]):
            own, got = rs_out[(nm, l)]
            per_layer.append(_finish_share("finish_%s_%d" % (nm, l), own[None], got[:, None], k, c_arr))
        shard_grads.append(per_layer[0] if len(per_layer) == 1 else jnp.concatenate(per_layer, axis=0))

    big_w = [w for (_, w, _) in big]
    big_m = [m_ffn1_w_in, m_ffn1_w_out, m_mem_w_kv, m_hgrn_w_in, m_hgrn_w_out, m_gmlp_w_in, m_gmlp_w_out, m_ffn2_w_in, m_ffn2_w_out]
    big_v = [v_ffn1_w_in, v_ffn1_w_out, v_mem_w_kv, v_hgrn_w_in, v_hgrn_w_out, v_gmlp_w_in, v_gmlp_w_out, v_ffn2_w_in, v_ffn2_w_out]
    big_out = {}
    for (nm, w, _), g, m, v in zip(big, shard_grads, big_m, big_v):
        L, r, c = w.shape
        d2, m2, v2 = _adam_call("adam_" + nm, w.reshape(L * r, c), g.reshape(L * r, c), m.reshape(L * r, c),
                                v.reshape(L * r, c), 256)
        big_out[nm] = (g, d2.reshape(w.shape), m2.reshape(w.shape), v2.reshape(w.shape))

    d_ffn1n = _two_rows(dg_f10, dg_f11)
    d_mixn = _two_rows(dg_m0, dg_m1)
    d_ffn2n = _two_rows(dg_f20, dg_f21)
    small_parts = [loss_part[:, :1], d_memnorm, d_lb, d_ffn1n, d_mixn, d_gn, d_lng, d_lnb, d_ws, d_bs, d_ffn2n, d_final]
    red_shapes = [(1,), mem_norm.shape, lb_logits.shape, ffn1_norm.shape, mix_norm.shape, hgrn_gnorm.shape, (1, ln_w), (1, ln_w),
                  gmlp_w_s.shape, gmlp_b_s.shape, ffn2_norm.shape, final_norm.shape]
    red = _small_allreduce(_pack(small_parts, _rows_needed(red_shapes)), "reduce_small")
    (loss_v, g_memn, g_lb, g_f1n, g_mixn, g_gn, g_lng_full, g_lnb_full, g_ws, g_bs, g_f2n, g_fin) = _unpack(red, red_shapes)
    lsh = gmlp_ln_g.shape[1]
    g_lng = lax.dynamic_slice(g_lng_full, (0, chip * lsh), (1, lsh))
    g_lnb = lax.dynamic_slice(g_lnb_full, (0, chip * lsh), (1, lsh))
    small_w = [mem_norm, lb_logits, ffn1_norm, mix_norm, hgrn_gnorm, gmlp_ln_g, gmlp_ln_b, gmlp_w_s, gmlp_b_s, ffn2_norm, final_norm]
    small_g = [g_memn, g_lb, g_f1n, g_mixn, g_gn, g_lng, g_lnb, g_ws, g_bs, g_f2n, g_fin]
    small_m = [m_mem_norm, m_lb_logits, m_ffn1_norm, m_mix_norm, m_hgrn_gnorm, m_gmlp_ln_g, m_gmlp_ln_b, m_gmlp_w_s, m_gmlp_b_s, m_ffn2_norm, m_final_norm]
    small_v = [v_mem_norm, v_lb_logits, v_ffn1_norm, v_mix_norm, v_hgrn_gnorm, v_gmlp_ln_g, v_gmlp_ln_b, v_gmlp_w_s, v_gmlp_b_s, v_ffn2_norm, v_final_norm]
    sshapes = [w.shape for w in small_w]
    nrow = _rows_needed(sshapes)
    d_p, m_p, v_p = _adam_call("adam_small", _pack(small_w, nrow), _pack(small_g, nrow), _pack(small_m, nrow), _pack(small_v, nrow), nrow)
    s_delta, s_m, s_v = _unpack(d_p, sshapes), _unpack(m_p, sshapes), _unpack(v_p, sshapes)
    small_names = ["mem_norm", "lb_logits", "ffn1_norm", "mix_norm", "hgrn_gnorm", "gmlp_ln_g", "gmlp_ln_b", "gmlp_w_s", "gmlp_b_s", "ffn2_norm", "final_norm"]
    small_out = {nm: (g.reshape(w.shape), d, m, v) for nm, w, g, d, m, v in zip(small_names, small_w, small_g, s_delta, s_m, s_v)}

    order = ["mem_norm", "lb_logits", "ffn1_norm", "ffn1_w_in", "ffn1_w_out", "mix_norm", "mem_w_kv", "hgrn_w_in", "hgrn_gnorm",
             "hgrn_w_out", "gmlp_w_in", "gmlp_ln_g", "gmlp_ln_b", "gmlp_w_s", "gmlp_b_s", "gmlp_w_out", "ffn2_norm", "ffn2_w_in",
             "ffn2_w_out", "final_norm"]
    allo = {**big_out, **small_out}
    grad_x = dx0.reshape(x.shape)
    return (loss_v.reshape(()), grad_x, *[allo[n][0] for n in order], *[allo[n][1] for n in order],
            *[allo[n][2] for n in order], *[allo[n][3] for n in order])
```

```python
import functools

import jax
import jax.numpy as jnp
from jax import lax
from jax.experimental import pallas as pl
from jax.experimental.pallas import tpu as pltpu
from jax.experimental.pallas import tpu_sc as plsc

BF = jnp.bfloat16
F32 = jnp.float32
MESH = pl.DeviceIdType.MESH

EPS = 1e-6
D_MODEL = 1024
HG_HEADS = 8
HG_DIM = 128
HG_CHUNK = 64
GM_CHUNK = 128
GM_GROUPS = 8
GM_GROUP_DIM = 256
XA_HEADS = 4
XA_DIM = 256
ADAM_LR = 0.001
ADAM_B1 = 0.9
ADAM_B2 = 0.999
ADAM_EPS = 1e-08
ADAM_WD = 0.01
ADAM_STEP = 10

VMEM_CAP_BYTES = 60 * 1024 * 1024
LANES = 1024


def _pick(n, cap, mult=16):
    if n <= cap:
        return n
    for d in range(cap - cap % mult, 0, -mult):
        if n % d == 0:
            return d
    raise ValueError((n, cap, mult))


def _dg(a, b, ca, cb):
    return lax.dot_general(a.astype(BF), b.astype(BF), (((ca,), (cb,)), ((), ())), preferred_element_type=F32)


@jax.custom_vjp
def dot_nn(a, b):
    return _dg(a, b, 1, 0)


def _nn_fwd(a, b):
    return _dg(a, b, 1, 0), (a, b)


def _nn_bwd(r, g):
    a, b = r
    return _dg(g, b, 1, 1), _dg(a, g, 0, 0)


dot_nn.defvjp(_nn_fwd, _nn_bwd)


@jax.custom_vjp
def dot_nt(a, b):
    return _dg(a, b, 1, 1)


def _nt_fwd(a, b):
    return _dg(a, b, 1, 1), (a, b)


def _nt_bwd(r, g):
    a, b = r
    return _dg(g, b, 1, 0), _dg(g, a, 0, 0)


dot_nt.defvjp(_nt_fwd, _nt_bwd)


@jax.custom_vjp
def dot_tn(a, b):
    return _dg(a, b, 0, 0)


def _tn_fwd(a, b):
    return _dg(a, b, 0, 0), (a, b)


def _tn_bwd(r, g):
    a, b = r
    return _dg(b, g, 1, 1), _dg(a, g, 1, 0)


dot_tn.defvjp(_tn_fwd, _tn_bwd)


def _rmsnorm(x, g):
    return x * lax.rsqrt(jnp.mean(x * x, axis=-1, keepdims=True) + EPS) * g


def _silu(x):
    return x * jax.nn.sigmoid(x)


def _gelu(x):
    return 0.5 * x * (1.0 + lax.erf(x * (0.5 ** 0.5)))


def _softmax_last(s):
    m = lax.stop_gradient(jnp.max(s, axis=-1, keepdims=True))
    e = jnp.exp(s - m)
    return e / jnp.sum(e, axis=-1, keepdims=True)


def _tril(n):
    r = lax.broadcasted_iota(jnp.int32, (n, n), 0)
    c = lax.broadcasted_iota(jnp.int32, (n, n), 1)
    return r >= c


def _cumsum_rows(l):
    n = l.shape[0]
    return lax.dot_general(_tril(n).astype(F32), l, (((1,), (0,)), ((), ())),
                           precision=lax.Precision.HIGHEST, preferred_element_type=F32)


def _attention(zx, mk, mv):
    s = dot_nt(zx, mk) * (XA_DIM ** -0.5)
    return dot_nn(_softmax_last(s), mv)


def _hgrn_head(zq, zf, zi, zg, l0, l1, l2, gn, S):
    m = lax.stop_gradient(jnp.maximum(jnp.maximum(l0, l1), l2))
    e0 = jnp.exp(l0 - m)
    lb = e0 / (e0 + jnp.exp(l1 - m) + jnp.exp(l2 - m))
    q = _silu(zq)
    f = lb + (1.0 - lb) * jax.nn.sigmoid(zf)
    k = 1.0 - f
    b = _cumsum_rows(jnp.log(f))
    b_last = b[HG_CHUNK - 1:HG_CHUNK, :]
    q_dec = q * jnp.exp(b)
    k_inv = k * jnp.exp(-b)
    a = jnp.where(_tril(HG_CHUNK), dot_nt(q_dec, k_inv), 0.0)
    o = dot_nn(a, zi) + dot_nn(q_dec, S)
    S_new = jnp.exp(b_last).reshape(HG_DIM, 1) * S + dot_tn(k * jnp.exp(b_last - b), zi)
    o = _rmsnorm(o, gn) * _silu(zg)
    return o, S_new


def _hgrn_block(zq, zf, zi, zg, zx, l0, l1, l2, gn, mk, mv, S):
    outs, s_new = [], []
    for h in range(HG_HEADS):
        o, sn = _hgrn_head(zq[h], zf[h], zi[h], zg[h], l0[h], l1[h], l2[h], gn, S[h])
        outs.append(o)
        s_new.append(sn)
    for a in range(XA_HEADS):
        outs.append(_attention(zx[a], mk[a], mv[a]))
    return outs, s_new


def _gmlp_block(zu, zv, zx, lng, lnb, ws, bs, mk, mv):
    gv = [_gelu(v) for v in zv]
    width = GM_GROUPS * GM_GROUP_DIM
    mu = sum(jnp.sum(g, axis=-1, keepdims=True) for g in gv) / width
    xc = [g - mu for g in gv]
    var = sum(jnp.sum(c * c, axis=-1, keepdims=True) for c in xc) / width
    r = lax.rsqrt(var + EPS)
    outs = []
    for g in range(GM_GROUPS):
        v = xc[g] * r * lng[g] + lnb[g]
        w = jnp.where(_tril(GM_CHUNK), ws[g], 0.0)
        mixed = dot_nn(w, v) + bs[g].reshape(GM_CHUNK, 1)
        outs.append(_gelu(zu[g]) * mixed)
    for a in range(XA_HEADS):
        outs.append(_attention(zx[a], mk[a], mv[a]))
    return outs


def _rowcall(name, fn, rows, consts, row_outs, acc_outs, tr):
    nrows = rows[0][0].shape[0]
    tr = _pick(nrows, tr)
    n_r, n_c, n_ro, n_ao = len(rows), len(consts), len(row_outs), len(acc_outs)

    def kern(*refs):
        rv = [r[...] for r in refs[:n_r]]
        cv = [r[...] for r in refs[n_r:n_r + n_c]]
        ro_refs = refs[n_r + n_c:n_r + n_c + n_ro]
        ao_refs = refs[n_r + n_c + n_ro:]
        ro, ao = fn(rv, cv)
        for ref, v in zip(ro_refs, ro):
            ref[...] = v.astype(ref.dtype)
        if n_ao:
            @pl.when(pl.program_id(0) == 0)
            def _():
                for ref in ao_refs:
                    ref[...] = jnp.zeros(ref.shape, ref.dtype)

            for ref, v in zip(ao_refs, ao):
                ref[...] += v.astype(ref.dtype)

    in_specs = [pl.BlockSpec((tr, w), functools.partial(lambda i, cb: (i, cb), cb=cb)) for (_, cb, w) in rows]
    in_specs += [pl.BlockSpec(c.shape, lambda i: (0, 0)) for c in consts]
    out_specs = [pl.BlockSpec((tr, w), lambda i: (i, 0)) for (w, _) in row_outs]
    out_specs += [pl.BlockSpec(s, lambda i: (0, 0)) for (s, _) in acc_outs]
    out_shape = [jax.ShapeDtypeStruct((nrows, w), dt) for (w, dt) in row_outs]
    out_shape += [jax.ShapeDtypeStruct(s, dt) for (s, dt) in acc_outs]
    est = sum(tr * w * a.dtype.itemsize for (a, _, w) in rows) + sum(tr * w * jnp.dtype(dt).itemsize for (w, dt) in row_outs)
    est += sum(c.size * c.dtype.itemsize for c in consts)
    outs = pl.pallas_call(
        kern, grid=(nrows // tr,), in_specs=in_specs, out_specs=out_specs, out_shape=out_shape, name=name,
        compiler_params=pltpu.CompilerParams(dimension_semantics=("arbitrary",),
                                             vmem_limit_bytes=int(min(VMEM_CAP_BYTES, 6 * est + (16 << 20)))),
    )(*[a for (a, _, _) in rows], *consts)
    return outs


def _mm(name, a, b, mode, out_dtype, tm, tn, tk, scale=1.0, res=None, a_lead=None, b_lead=None):
    ash = a.shape[-2:]
    bsh = b.shape[-2:]
    if mode == "nn":
        (M, K), (K2, N) = ash, bsh
    elif mode == "nt":
        (M, K), (N, K2) = ash, bsh
    else:
        (K, M), (K2, N) = ash, bsh
    assert K == K2, (name, a.shape, b.shape)
    tm, tn, tk = min(tm, M), min(tn, N), min(tk, K)
    assert M % tm == 0 and N % tn == 0 and K % tk == 0, (name, M, N, K, tm, tn, tk)
    nk = K // tk
    dims = {"nn": (1, 0), "nt": (1, 1), "tn": (0, 0)}[mode]

    def lead(spec_shape, index_fn, lead_idx):
        if lead_idx is None:
            return pl.BlockSpec(spec_shape, index_fn)
        return pl.BlockSpec((None,) + spec_shape, lambda i, j, k: (lead_idx,) + index_fn(i, j, k))

    if mode == "tn":
        a_spec = lead((tk, tm), lambda i, j, k: (k, i), a_lead)
    else:
        a_spec = lead((tm, tk), lambda i, j, k: (i, k), a_lead)
    if mode == "nt":
        b_spec = lead((tn, tk), lambda i, j, k: (j, k), b_lead)
    else:
        b_spec = lead((tk, tn), lambda i, j, k: (k, j), b_lead)
    o_spec = pl.BlockSpec((tm, tn), lambda i, j, k: (i, j))
    has_res = res is not None

    def kern(*refs):
        a_ref, b_ref = refs[0], refs[1]
        res_ref = refs[2] if has_res else None
        o_ref = refs[3] if has_res else refs[2]
        acc_ref = refs[-1] if nk > 1 else None
        p = lax.dot_general(a_ref[...].astype(BF), b_ref[...].astype(BF), (((dims[0],), (dims[1],)), ((), ())),
                            preferred_element_type=F32)

        def finish(v):
            if scale != 1.0:
                v = v * scale
            if has_res:
                v = res_ref[...] + v
            o_ref[...] = v.astype(o_ref.dtype)

        if nk == 1:
            finish(p)
        else:
            k = pl.program_id(2)

            @pl.when(k == 0)
            def _():
                acc_ref[...] = p

            @pl.when(k > 0)
            def _():
                acc_ref[...] += p

            @pl.when(k == nk - 1)
            def _():
                finish(acc_ref[...])

    ins = [a, b] + ([res] if has_res else [])
    in_specs = [a_spec, b_spec] + ([o_spec] if has_res else [])
    est = tm * tk * a.dtype.itemsize + tk * tn * b.dtype.itemsize + tm * tn * (jnp.dtype(out_dtype).itemsize + 8)
    return pl.pallas_call(
        kern, grid=(M // tm, N // tn, nk), in_specs=in_specs, out_specs=o_spec,
        out_shape=jax.ShapeDtypeStruct((M, N), out_dtype),
        scratch_shapes=[pltpu.VMEM((tm, tn), F32)] if nk > 1 else [],
        name=name,
        compiler_params=pltpu.CompilerParams(dimension_semantics=("parallel", "parallel", "arbitrary"),
                                             vmem_limit_bytes=int(min(VMEM_CAP_BYTES, 3 * est + (16 << 20)))),
    )(*ins)


def _ffn_in_swiglu(name, h, w3, tm, tn):
    T, D = h.shape
    dff = w3.shape[2] // 2
    tm = min(tm, T)
    assert T % tm == 0 and dff % tn == 0
    nj = dff // tn

    def kern(h_ref, wg_ref, wu_ref, zg_ref, zu_ref, a_ref):
        hb = h_ref[...]
        g = jnp.dot(hb, wg_ref[...], preferred_element_type=F32).astype(BF)
        u = jnp.dot(hb, wu_ref[...], preferred_element_type=F32).astype(BF)
        zg_ref[...] = g
        zu_ref[...] = u
        a_ref[...] = (_silu(g.astype(F32)) * u.astype(F32)).astype(BF)

    o_spec = pl.BlockSpec((tm, tn), lambda i, j: (i, j))
    return pl.pallas_call(
        kern, grid=(T // tm, nj),
        in_specs=[pl.BlockSpec((tm, D), lambda i, j: (i, 0)),
                  pl.BlockSpec((None, D, tn), lambda i, j: (0, 0, j)),
                  pl.BlockSpec((None, D, tn), lambda i, j: (0, 0, j + nj))],
        out_specs=[o_spec, o_spec, o_spec],
        out_shape=[jax.ShapeDtypeStruct((T, dff), BF)] * 3, name=name,
        compiler_params=pltpu.CompilerParams(dimension_semantics=("parallel", "arbitrary"),
                                             vmem_limit_bytes=int(min(VMEM_CAP_BYTES, 4 * tm * D + 40 * tm * tn + 8 * D * tn + (16 << 20)))),
    )(h, w3, w3)


def _ffn_da_swiglu(name, dxo, w3, zg, zu, tm):
    T, D = dxo.shape
    dff = w3.shape[1]
    tm = min(tm, T)
    assert T % tm == 0 and dff % 2 == 0
    hc = dff // 2

    def kern(d_ref, w_ref, g_ref, u_ref, dz_ref):
        db = d_ref[...].astype(BF)
        for s in range(2):
            cols = slice(s * hc, (s + 1) * hc)
            da = lax.dot_general(db, w_ref[cols, :], (((1,), (1,)), ((), ())), preferred_element_type=F32) * 0.5
            da = da.astype(BF).astype(F32)
            _, vjp = jax.vjp(lambda p, q: _silu(p) * q, g_ref[:, cols].astype(F32), u_ref[:, cols].astype(F32))
            dg, du = vjp(da)
            dz_ref[:, cols] = dg.astype(dz_ref.dtype)
            dz_ref[:, dff + s * hc:dff + (s + 1) * hc] = du.astype(dz_ref.dtype)

    row = lambda w: pl.BlockSpec((tm, w), lambda i: (i, 0))
    return pl.pallas_call(
        kern, grid=(T // tm,),
        in_specs=[row(D), pl.BlockSpec((None, dff, D), lambda i: (0, 0, 0), pipeline_mode=pl.Buffered(1)), row(dff), row(dff)],
        out_specs=row(2 * dff), out_shape=jax.ShapeDtypeStruct((T, 2 * dff), BF), name=name,
        compiler_params=pltpu.CompilerParams(dimension_semantics=("arbitrary",), vmem_limit_bytes=VMEM_CAP_BYTES),
    )(dxo, w3, zg, zu)


def _mm_dh_rms(name, dz, w3, xin, g, dres, tm):
    T, K = dz.shape
    D = w3.shape[1]
    tm = min(tm, T)
    assert T % tm == 0

    def kern(dz_ref, w_ref, x_ref, g_ref, r_ref, dx_ref, dg_ref):
        dh = lax.dot_general(dz_ref[...], w_ref[...], (((1,), (1,)), ((), ())), preferred_element_type=F32)
        _, vjp = jax.vjp(_rmsnorm, x_ref[...], g_ref[...])
        dx, dg = vjp(dh)
        dx_ref[...] = dx + r_ref[...]

        @pl.when(pl.program_id(0) == 0)
        def _():
            dg_ref[...] = jnp.zeros(dg_ref.shape, F32)

        dg_ref[...] += dg

    row = lambda w: pl.BlockSpec((tm, w), lambda i: (i, 0))
    one = pl.BlockSpec((1, D), lambda i: (0, 0))
    return pl.pallas_call(
        kern, grid=(T // tm,),
        in_specs=[row(K), pl.BlockSpec((None, D, K), lambda i: (0, 0, 0), pipeline_mode=pl.Buffered(1)), row(D), one, row(D)],
        out_specs=[row(D), one], out_shape=[jax.ShapeDtypeStruct((T, D), F32), jax.ShapeDtypeStruct((1, D), F32)], name=name,
        compiler_params=pltpu.CompilerParams(dimension_semantics=("arbitrary",), vmem_limit_bytes=VMEM_CAP_BYTES),
    )(dz, w3, xin, g, dres)


def _mm_tn_pair(name, a, b, kind, c_arr, tq, tk, scale=1.0):
    T, M = a.shape
    _, N = b.shape
    tk = min(tk, T)
    assert T % tk == 0
    nk = T // tk
    if kind == "col":
        hm = M // 2
        assert N % tq == 0
        nq = N // tq
        tile = (hm, tq)
        a_spec = pl.BlockSpec((tk, hm), lambda h, q, k, c: (k, jnp.bitwise_xor(h, 1 - c[0])))
        b_spec = pl.BlockSpec((tk, tq), lambda h, q, k, c: (k, q))
        o_spec = pl.BlockSpec(tile, lambda h, q, k, c: (0, q * h))
        out_sd = (hm, N)
    else:
        hn = N // 2
        assert M % tq == 0
        nq = M // tq
        tile = (tq, hn)
        a_spec = pl.BlockSpec((tk, tq), lambda h, q, k, c: (k, q))
        b_spec = pl.BlockSpec((tk, hn), lambda h, q, k, c: (k, jnp.bitwise_xor(h, 1 - c[0])))
        o_spec = pl.BlockSpec(tile, lambda h, q, k, c: (q * h, 0))
        out_sd = (M, hn)

    def kern(c_ref, a_ref, b_ref, o_ref, acc, stage, recv, ssem, rsem):
        h, q, k = pl.program_id(0), pl.program_id(1), pl.program_id(2)
        x, y, c, _ = _place()
        p = lax.dot_general(a_ref[...].astype(BF), b_ref[...].astype(BF), (((0,), (0,)), ((), ())), preferred_element_type=F32)

        @pl.when(k == 0)
        def _():
            acc[...] = p

        @pl.when(k > 0)
        def _():
            acc[...] += p

        def send(slot, qq):
            return pltpu.make_async_remote_copy(src_ref=stage.at[slot], dst_ref=recv.at[qq], send_sem=ssem.at[slot],
                                                recv_sem=rsem.at[qq], device_id=(x, y, 1 - c), device_id_type=MESH)

        last = k == nk - 1

        @pl.when(jnp.logical_and(last, h == 0))
        def _():
            slot = q % 2

            @pl.when(q >= 2)
            def _():
                send(slot, q).wait_send()

            stage[slot] = (acc[...] * scale).astype(BF)
            send(slot, q).start()

        @pl.when(jnp.logical_and(last, h == 1))
        def _():
            @pl.when(q == 0)
            def _():
                for s in range(min(nq, 2)):
                    send(s, 0).wait_send()

            send(0, q).wait_recv()
            o_ref[...] = (acc[...] * scale + recv[q].astype(F32)).astype(o_ref.dtype)

    tb = tile[0] * tile[1]
    est = tb * (4 + 2 * 2 + nq * 2 + 2 * 2) + 2 * tk * (a_spec.block_shape[1] + b_spec.block_shape[1]) * 2 * 2
    return pl.pallas_call(
        kern,
        grid_spec=pltpu.PrefetchScalarGridSpec(
            num_scalar_prefetch=1, grid=(2, nq, nk), in_specs=[a_spec, b_spec], out_specs=o_spec,
            scratch_shapes=[pltpu.VMEM(tile, F32), pltpu.VMEM((2,) + tile, BF), pltpu.VMEM((nq,) + tile, BF),
                            pltpu.SemaphoreType.DMA((2,)), pltpu.SemaphoreType.DMA((nq,))]),
        out_shape=jax.ShapeDtypeStruct(out_sd, BF), name=name,
        compiler_params=pltpu.CompilerParams(dimension_semantics=("arbitrary", "arbitrary", "arbitrary"),
                                             vmem_limit_bytes=int(min(VMEM_CAP_BYTES, est + (12 << 20)))),
    )(c_arr, a, b)


def _hgrn_pieces(z_ref):
    W = HG_HEADS * HG_DIM
    zq = [z_ref[:, h * HG_DIM:(h + 1) * HG_DIM] for h in range(HG_HEADS)]
    zf = [z_ref[:, W + h * HG_DIM:W + (h + 1) * HG_DIM] for h in range(HG_HEADS)]
    zi = [z_ref[:, 2 * W + h * HG_DIM:2 * W + (h + 1) * HG_DIM] for h in range(HG_HEADS)]
    zg = [z_ref[:, 3 * W + h * HG_DIM:3 * W + (h + 1) * HG_DIM] for h in range(HG_HEADS)]
    zx = [z_ref[:, 4 * W + a * XA_DIM:4 * W + (a + 1) * XA_DIM] for a in range(XA_HEADS)]
    return zq, zf, zi, zg, zx


def _kv_pieces(kv_ref):
    W = XA_HEADS * XA_DIM
    mk = [kv_ref[:, a * XA_DIM:(a + 1) * XA_DIM] for a in range(XA_HEADS)]
    mv = [kv_ref[:, W + a * XA_DIM:W + (a + 1) * XA_DIM] for a in range(XA_HEADS)]
    return mk, mv


def _lb_pieces(lb_ref):
    return [[lb_ref[r:r + 1, h * HG_DIM:(h + 1) * HG_DIM] for h in range(HG_HEADS)] for r in range(3)]


def _hgrn_fwd(z, lb_logits, gnorm, kv, bl, nc):
    T, zw = z.shape
    mem_len = kv.shape[0] // bl
    cat_w = HG_HEADS * HG_DIM + XA_HEADS * XA_DIM

    def kern(z_ref, lb_ref, gn_ref, kv_ref, cat_ref, st_ref, s_scr):
        @pl.when(pl.program_id(1) == 0)
        def _():
            s_scr[...] = jnp.zeros(s_scr.shape, F32)

        st_ref[...] = s_scr[...]
        zq, zf, zi, zg, zx = _hgrn_pieces(z_ref)
        mk, mv = _kv_pieces(kv_ref)
        l0, l1, l2 = _lb_pieces(lb_ref)
        S = [s_scr[h] for h in range(HG_HEADS)]
        outs, s_new = _hgrn_block(zq, zf, zi, zg, zx, l0, l1, l2, gn_ref[...], mk, mv, S)
        for h in range(HG_HEADS):
            cat_ref[:, h * HG_DIM:(h + 1) * HG_DIM] = outs[h].astype(cat_ref.dtype)
            s_scr[h] = s_new[h]
        base = HG_HEADS * HG_DIM
        for a in range(XA_HEADS):
            cat_ref[:, base + a * XA_DIM:base + (a + 1) * XA_DIM] = outs[HG_HEADS + a].astype(cat_ref.dtype)

    return pl.pallas_call(
        kern, grid=(bl, nc),
        in_specs=[pl.BlockSpec((HG_CHUNK, zw), lambda b, n: (b * nc + n, 0)),
                  pl.BlockSpec(lb_logits.shape, lambda b, n: (0, 0)),
                  pl.BlockSpec(gnorm.shape, lambda b, n: (0, 0)),
                  pl.BlockSpec((mem_len, kv.shape[1]), lambda b, n: (b, 0))],
        out_specs=[pl.BlockSpec((HG_CHUNK, cat_w), lambda b, n: (b * nc + n, 0)),
                   pl.BlockSpec((None, HG_HEADS, HG_DIM, HG_DIM), lambda b, n: (b * nc + n, 0, 0, 0))],
        out_shape=[jax.ShapeDtypeStruct((T, cat_w), BF),
                   jax.ShapeDtypeStruct((bl * nc, HG_HEADS, HG_DIM, HG_DIM), F32)],
        scratch_shapes=[pltpu.VMEM((HG_HEADS, HG_DIM, HG_DIM), F32)],
        name="hgrn_fwd",
        compiler_params=pltpu.CompilerParams(dimension_semantics=("arbitrary", "arbitrary"), vmem_limit_bytes=48 << 20),
    )(z, lb_logits, gnorm, kv)


def _hgrn_bwd(z, dcat, stash, lb_logits, gnorm, kv, bl, nc):
    T, zw = z.shape
    mem_len = kv.shape[0] // bl
    cat_w = dcat.shape[1]

    def kern(z_ref, dc_ref, st_ref, lb_ref, gn_ref, kv_ref, dz_ref, dkv_ref, dlb_ref, dgn_ref, ds_scr):
        first = jnp.logical_and(pl.program_id(0) == 0, pl.program_id(1) == 0)

        @pl.when(pl.program_id(1) == 0)
        def _():
            ds_scr[...] = jnp.zeros(ds_scr.shape, F32)
            dkv_ref[...] = jnp.zeros(dkv_ref.shape, F32)

        @pl.when(first)
        def _():
            dlb_ref[...] = jnp.zeros(dlb_ref.shape, F32)
            dgn_ref[...] = jnp.zeros(dgn_ref.shape, F32)

        zq, zf, zi, zg, zx = _hgrn_pieces(z_ref)
        mk, mv = _kv_pieces(kv_ref)
        l0, l1, l2 = _lb_pieces(lb_ref)
        S = [st_ref[h] for h in range(HG_HEADS)]
        _, vjp = jax.vjp(_hgrn_block, zq, zf, zi, zg, zx, l0, l1, l2, gn_ref[...], mk, mv, S)
        d_outs = [dc_ref[:, h * HG_DIM:(h + 1) * HG_DIM] for h in range(HG_HEADS)]
        base = HG_HEADS * HG_DIM
        d_outs += [dc_ref[:, base + a * XA_DIM:base + (a + 1) * XA_DIM] for a in range(XA_HEADS)]
        d_s = [ds_scr[h] for h in range(HG_HEADS)]
        dzq, dzf, dzi, dzg, dzx, dl0, dl1, dl2, dgn, dmk, dmv, dS = vjp((d_outs, d_s))
        W = HG_HEADS * HG_DIM
        for h in range(HG_HEADS):
            sl = slice(h * HG_DIM, (h + 1) * HG_DIM)
            dz_ref[:, sl] = dzq[h].astype(dz_ref.dtype)
            dz_ref[:, W + h * HG_DIM:W + (h + 1) * HG_DIM] = dzf[h].astype(dz_ref.dtype)
            dz_ref[:, 2 * W + h * HG_DIM:2 * W + (h + 1) * HG_DIM] = dzi[h].astype(dz_ref.dtype)
            dz_ref[:, 3 * W + h * HG_DIM:3 * W + (h + 1) * HG_DIM] = dzg[h].astype(dz_ref.dtype)
            ds_scr[h] = dS[h]
            dlb_ref[0:1, sl] += dl0[h]
            dlb_ref[1:2, sl] += dl1[h]
            dlb_ref[2:3, sl] += dl2[h]
        dgn_ref[...] += dgn
        KW = XA_HEADS * XA_DIM
        for a in range(XA_HEADS):
            dz_ref[:, 4 * W + a * XA_DIM:4 * W + (a + 1) * XA_DIM] = dzx[a].astype(dz_ref.dtype)
            dkv_ref[:, a * XA_DIM:(a + 1) * XA_DIM] += dmk[a]
            dkv_ref[:, KW + a * XA_DIM:KW + (a + 1) * XA_DIM] += dmv[a]

    rev = lambda b, n: (b * nc + (nc - 1 - n), 0)
    return pl.pallas_call(
        kern, grid=(bl, nc),
        in_specs=[pl.BlockSpec((HG_CHUNK, zw), rev),
                  pl.BlockSpec((HG_CHUNK, cat_w), rev),
                  pl.BlockSpec((None, HG_HEADS, HG_DIM, HG_DIM), lambda b, n: (b * nc + (nc - 1 - n), 0, 0, 0)),
                  pl.BlockSpec(lb_logits.shape, lambda b, n: (0, 0)),
                  pl.BlockSpec(gnorm.shape, lambda b, n: (0, 0)),
                  pl.BlockSpec((mem_len, kv.shape[1]), lambda b, n: (b, 0))],
        out_specs=[pl.BlockSpec((HG_CHUNK, zw), rev),
                   pl.BlockSpec((mem_len, kv.shape[1]), lambda b, n: (b, 0)),
                   pl.BlockSpec(lb_logits.shape, lambda b, n: (0, 0)),
                   pl.BlockSpec(gnorm.shape, lambda b, n: (0, 0))],
        out_shape=[jax.ShapeDtypeStruct((T, zw), BF), jax.ShapeDtypeStruct(kv.shape, F32),
                   jax.ShapeDtypeStruct(lb_logits.shape, F32), jax.ShapeDtypeStruct(gnorm.shape, F32)],
        scratch_shapes=[pltpu.VMEM((HG_HEADS, HG_DIM, HG_DIM), F32)],
        name="hgrn_bwd",
        compiler_params=pltpu.CompilerParams(dimension_semantics=("arbitrary", "arbitrary"), vmem_limit_bytes=56 << 20),
    )(z, dcat, stash, lb_logits, gnorm, kv)


HG_SUB = 4


def _hgrn_rows(z_ref, dtype_cast=None):
    W = HG_HEADS * HG_DIM

    def piece(c, col, w):
        return z_ref[c * HG_CHUNK:(c + 1) * HG_CHUNK, col:col + w]

    zq = [[piece(c, h * HG_DIM, HG_DIM) for h in range(HG_HEADS)] for c in range(HG_SUB)]
    zf = [[piece(c, W + h * HG_DIM, HG_DIM) for h in range(HG_HEADS)] for c in range(HG_SUB)]
    zi = [[piece(c, 2 * W + h * HG_DIM, HG_DIM) for h in range(HG_HEADS)] for c in range(HG_SUB)]
    zg = [[piece(c, 3 * W + h * HG_DIM, HG_DIM) for h in range(HG_HEADS)] for c in range(HG_SUB)]
    zx = [z_ref[:, 4 * W + a * XA_DIM:4 * W + (a + 1) * XA_DIM] for a in range(XA_HEADS)]
    return zq, zf, zi, zg, zx


def _hgrn_steps(zq, zf, zi, zg, zx, l0, l1, l2, gn, mk, mv, S):
    mix = []
    for c in range(HG_SUB):
        row, s_next = [], []
        for h in range(HG_HEADS):
            o, sn = _hgrn_head(zq[c][h], zf[c][h], zi[c][h], zg[c][h], l0[h], l1[h], l2[h], gn, S[h])
            row.append(o)
            s_next.append(sn)
        mix.append(row)
        S = s_next
    att = [_attention(zx[a], mk[a], mv[a]) for a in range(XA_HEADS)]
    return mix, att, S


def _hgrn_fwd2(z, lb_logits, gnorm, kv, bl, seq):
    T, zw = z.shape
    mem_len = kv.shape[0] // bl
    cat_w = HG_HEADS * HG_DIM + XA_HEADS * XA_DIM
    R = HG_SUB * HG_CHUNK
    nb = seq // R

    def kern(z_ref, lb_ref, gn_ref, kv_ref, cat_ref, st_ref, s_scr):
        @pl.when(pl.program_id(1) == 0)
        def _():
            s_scr[...] = jnp.zeros(s_scr.shape, F32)

        st_ref[...] = s_scr[...]
        zq, zf, zi, zg, zx = _hgrn_rows(z_ref)
        mk, mv = _kv_pieces(kv_ref)
        l0, l1, l2 = _lb_pieces(lb_ref)
        S = [s_scr[h] for h in range(HG_HEADS)]
        mix, att, s_new = _hgrn_steps(zq, zf, zi, zg, zx, l0, l1, l2, gn_ref[...], mk, mv, S)
        for c in range(HG_SUB):
            for h in range(HG_HEADS):
                cat_ref[c * HG_CHUNK:(c + 1) * HG_CHUNK, h * HG_DIM:(h + 1) * HG_DIM] = mix[c][h].astype(cat_ref.dtype)
        for h in range(HG_HEADS):
            s_scr[h] = s_new[h]
        base = HG_HEADS * HG_DIM
        for a in range(XA_HEADS):
            cat_ref[:, base + a * XA_DIM:base + (a + 1) * XA_DIM] = att[a].astype(cat_ref.dtype)

    return pl.pallas_call(
        kern, grid=(bl, nb),
        in_specs=[pl.BlockSpec((R, zw), lambda b, n: (b * nb + n, 0)),
                  pl.BlockSpec(lb_logits.shape, lambda b, n: (0, 0)),
                  pl.BlockSpec(gnorm.shape, lambda b, n: (0, 0)),
                  pl.BlockSpec((mem_len, kv.shape[1]), lambda b, n: (b, 0))],
        out_specs=[pl.BlockSpec((R, cat_w), lambda b, n: (b * nb + n, 0)),
                   pl.BlockSpec((None, HG_HEADS, HG_DIM, HG_DIM), lambda b, n: (b * nb + n, 0, 0, 0))],
        out_shape=[jax.ShapeDtypeStruct((T, cat_w), BF),
                   jax.ShapeDtypeStruct((bl * nb, HG_HEADS, HG_DIM, HG_DIM), F32)],
        scratch_shapes=[pltpu.VMEM((HG_HEADS, HG_DIM, HG_DIM), F32)],
        name="hgrn_fwd",
        compiler_params=pltpu.CompilerParams(dimension_semantics=("arbitrary", "arbitrary"), vmem_limit_bytes=56 << 20),
    )(z, lb_logits, gnorm, kv)


def _hgrn_bwd2(z, dcat, stash, lb_logits, gnorm, kv, bl, seq):
    T, zw = z.shape
    mem_len = kv.shape[0] // bl
    cat_w = dcat.shape[1]
    R = HG_SUB * HG_CHUNK
    nb = seq // R

    def kern(z_ref, dc_ref, st_ref, lb_ref, gn_ref, kv_ref, dz_ref, dkv_ref, dlb_ref, dgn_ref, ds_scr):
        first = jnp.logical_and(pl.program_id(0) == 0, pl.program_id(1) == 0)

        @pl.when(pl.program_id(1) == 0)
        def _():
            ds_scr[...] = jnp.zeros(ds_scr.shape, F32)
            dkv_ref[...] = jnp.zeros(dkv_ref.shape, F32)

        @pl.when(first)
        def _():
            dlb_ref[...] = jnp.zeros(dlb_ref.shape, F32)
            dgn_ref[...] = jnp.zeros(dgn_ref.shape, F32)

        zq, zf, zi, zg, zx = _hgrn_rows(z_ref)
        mk, mv = _kv_pieces(kv_ref)
        l0, l1, l2 = _lb_pieces(lb_ref)
        S = [st_ref[h] for h in range(HG_HEADS)]
        _, vjp = jax.vjp(_hgrn_steps, zq, zf, zi, zg, zx, l0, l1, l2, gn_ref[...], mk, mv, S)
        d_mix = [[dc_ref[c * HG_CHUNK:(c + 1) * HG_CHUNK, h * HG_DIM:(h + 1) * HG_DIM] for h in range(HG_HEADS)]
                 for c in range(HG_SUB)]
        base = HG_HEADS * HG_DIM
        d_att = [dc_ref[:, base + a * XA_DIM:base + (a + 1) * XA_DIM] for a in range(XA_HEADS)]
        d_s = [ds_scr[h] for h in range(HG_HEADS)]
        dzq, dzf, dzi, dzg, dzx, dl0, dl1, dl2, dgn, dmk, dmv, dS = vjp((d_mix, d_att, d_s))
        W = HG_HEADS * HG_DIM
        for c in range(HG_SUB):
            rows = slice(c * HG_CHUNK, (c + 1) * HG_CHUNK)
            for h in range(HG_HEADS):
                for k, part in enumerate((dzq, dzf, dzi, dzg)):
                    dz_ref[rows, k * W + h * HG_DIM:k * W + (h + 1) * HG_DIM] = part[c][h].astype(dz_ref.dtype)
        for h in range(HG_HEADS):
            sl = slice(h * HG_DIM, (h + 1) * HG_DIM)
            ds_scr[h] = dS[h]
            dlb_ref[0:1, sl] += dl0[h]
            dlb_ref[1:2, sl] += dl1[h]
            dlb_ref[2:3, sl] += dl2[h]
        dgn_ref[...] += dgn
        KW = XA_HEADS * XA_DIM
        for a in range(XA_HEADS):
            dz_ref[:, 4 * W + a * XA_DIM:4 * W + (a + 1) * XA_DIM] = dzx[a].astype(dz_ref.dtype)
            dkv_ref[:, a * XA_DIM:(a + 1) * XA_DIM] += dmk[a]
            dkv_ref[:, KW + a * XA_DIM:KW + (a + 1) * XA_DIM] += dmv[a]

    rev = lambda b, n: (b * nb + (nb - 1 - n), 0)
    return pl.pallas_call(
        kern, grid=(bl, nb),
        in_specs=[pl.BlockSpec((R, zw), rev),
                  pl.BlockSpec((R, cat_w), rev),
                  pl.BlockSpec((None, HG_HEADS, HG_DIM, HG_DIM), lambda b, n: (b * nb + (nb - 1 - n), 0, 0, 0)),
                  pl.BlockSpec(lb_logits.shape, lambda b, n: (0, 0)),
                  pl.BlockSpec(gnorm.shape, lambda b, n: (0, 0)),
                  pl.BlockSpec((mem_len, kv.shape[1]), lambda b, n: (b, 0))],
        out_specs=[pl.BlockSpec((R, zw), rev),
                   pl.BlockSpec((mem_len, kv.shape[1]), lambda b, n: (b, 0)),
                   pl.BlockSpec(lb_logits.shape, lambda b, n: (0, 0)),
                   pl.BlockSpec(gnorm.shape, lambda b, n: (0, 0))],
        out_shape=[jax.ShapeDtypeStruct((T, zw), BF), jax.ShapeDtypeStruct(kv.shape, F32),
                   jax.ShapeDtypeStruct(lb_logits.shape, F32), jax.ShapeDtypeStruct(gnorm.shape, F32)],
        scratch_shapes=[pltpu.VMEM((HG_HEADS, HG_DIM, HG_DIM), F32)],
        name="hgrn_bwd",
        compiler_params=pltpu.CompilerParams(dimension_semantics=("arbitrary", "arbitrary"), vmem_limit_bytes=56 << 20),
    )(z, dcat, stash, lb_logits, gnorm, kv)


def _gmlp_pieces(z_ref):
    W = GM_GROUPS * GM_GROUP_DIM
    zu = [z_ref[:, g * GM_GROUP_DIM:(g + 1) * GM_GROUP_DIM] for g in range(GM_GROUPS)]
    zv = [z_ref[:, W + g * GM_GROUP_DIM:W + (g + 1) * GM_GROUP_DIM] for g in range(GM_GROUPS)]
    zx = [z_ref[:, 2 * W + a * XA_DIM:2 * W + (a + 1) * XA_DIM] for a in range(XA_HEADS)]
    return zu, zv, zx


def _gmlp_params(lng_ref, lnb_ref, ws_ref, bs_ref):
    lng = [lng_ref[:, g * GM_GROUP_DIM:(g + 1) * GM_GROUP_DIM] for g in range(GM_GROUPS)]
    lnb = [lnb_ref[:, g * GM_GROUP_DIM:(g + 1) * GM_GROUP_DIM] for g in range(GM_GROUPS)]
    ws = [ws_ref[g] for g in range(GM_GROUPS)]
    bs = [bs_ref[g:g + 1, :] for g in range(GM_GROUPS)]
    return lng, lnb, ws, bs


def _gmlp_fwd(z, ln_g, ln_b, w_s, b_s, kv, bl, nc):
    T, zw = z.shape
    mem_len = kv.shape[0] // bl
    cat_w = GM_GROUPS * GM_GROUP_DIM + XA_HEADS * XA_DIM

    def kern(z_ref, lng_ref, lnb_ref, ws_ref, bs_ref, kv_ref, cat_ref):
        zu, zv, zx = _gmlp_pieces(z_ref)
        lng, lnb, ws, bs = _gmlp_params(lng_ref, lnb_ref, ws_ref, bs_ref)
        mk, mv = _kv_pieces(kv_ref)
        outs = _gmlp_block(zu, zv, zx, lng, lnb, ws, bs, mk, mv)
        for g in range(GM_GROUPS):
            cat_ref[:, g * GM_GROUP_DIM:(g + 1) * GM_GROUP_DIM] = outs[g].astype(cat_ref.dtype)
        base = GM_GROUPS * GM_GROUP_DIM
        for a in range(XA_HEADS):
            cat_ref[:, base + a * XA_DIM:base + (a + 1) * XA_DIM] = outs[GM_GROUPS + a].astype(cat_ref.dtype)

    full2 = lambda b, n: (0, 0)
    return pl.pallas_call(
        kern, grid=(bl, nc),
        in_specs=[pl.BlockSpec((GM_CHUNK, zw), lambda b, n: (b * nc + n, 0)),
                  pl.BlockSpec(ln_g.shape, full2), pl.BlockSpec(ln_b.shape, full2),
                  pl.BlockSpec(w_s.shape, lambda b, n: (0, 0, 0)), pl.BlockSpec(b_s.shape, full2),
                  pl.BlockSpec((mem_len, kv.shape[1]), lambda b, n: (b, 0))],
        out_specs=pl.BlockSpec((GM_CHUNK, cat_w), lambda b, n: (b * nc + n, 0)),
        out_shape=jax.ShapeDtypeStruct((T, cat_w), BF),
        name="gmlp_fwd",
        compiler_params=pltpu.CompilerParams(dimension_semantics=("arbitrary", "arbitrary"), vmem_limit_bytes=48 << 20),
    )(z, ln_g, ln_b, w_s, b_s, kv)


def _gmlp_bwd(z, dcat, ln_g, ln_b, w_s, b_s, kv, bl, nc):
    T, zw = z.shape
    mem_len = kv.shape[0] // bl
    cat_w = dcat.shape[1]

    def kern(z_ref, dc_ref, lng_ref, lnb_ref, ws_ref, bs_ref, kv_ref,
             dz_ref, dkv_ref, dlng_ref, dlnb_ref, dws_ref, dbs_ref):
        first = jnp.logical_and(pl.program_id(0) == 0, pl.program_id(1) == 0)

        @pl.when(pl.program_id(1) == 0)
        def _():
            dkv_ref[...] = jnp.zeros(dkv_ref.shape, F32)

        @pl.when(first)
        def _():
            dlng_ref[...] = jnp.zeros(dlng_ref.shape, F32)
            dlnb_ref[...] = jnp.zeros(dlnb_ref.shape, F32)
            dws_ref[...] = jnp.zeros(dws_ref.shape, F32)
            dbs_ref[...] = jnp.zeros(dbs_ref.shape, F32)

        zu, zv, zx = _gmlp_pieces(z_ref)
        lng, lnb, ws, bs = _gmlp_params(lng_ref, lnb_ref, ws_ref, bs_ref)
        mk, mv = _kv_pieces(kv_ref)
        _, vjp = jax.vjp(_gmlp_block, zu, zv, zx, lng, lnb, ws, bs, mk, mv)
        d_outs = [dc_ref[:, g * GM_GROUP_DIM:(g + 1) * GM_GROUP_DIM] for g in range(GM_GROUPS)]
        base = GM_GROUPS * GM_GROUP_DIM
        d_outs += [dc_ref[:, base + a * XA_DIM:base + (a + 1) * XA_DIM] for a in range(XA_HEADS)]
        dzu, dzv, dzx, dlng, dlnb, dws, dbs, dmk, dmv = vjp(d_outs)
        W = GM_GROUPS * GM_GROUP_DIM
        for g in range(GM_GROUPS):
            sl = slice(g * GM_GROUP_DIM, (g + 1) * GM_GROUP_DIM)
            dz_ref[:, sl] = dzu[g].astype(dz_ref.dtype)
            dz_ref[:, W + g * GM_GROUP_DIM:W + (g + 1) * GM_GROUP_DIM] = dzv[g].astype(dz_ref.dtype)
            dlng_ref[:, sl] += dlng[g]
            dlnb_ref[:, sl] += dlnb[g]
            dws_ref[g] += dws[g]
            dbs_ref[g:g + 1, :] += dbs[g]
        KW = XA_HEADS * XA_DIM
        for a in range(XA_HEADS):
            dz_ref[:, 2 * W + a * XA_DIM:2 * W + (a + 1) * XA_DIM] = dzx[a].astype(dz_ref.dtype)
            dkv_ref[:, a * XA_DIM:(a + 1) * XA_DIM] += dmk[a]
            dkv_ref[:, KW + a * XA_DIM:KW + (a + 1) * XA_DIM] += dmv[a]

    full2 = lambda b, n: (0, 0)
    full3 = lambda b, n: (0, 0, 0)
    blk = lambda b, n: (b * nc + n, 0)
    return pl.pallas_call(
        kern, grid=(bl, nc),
        in_specs=[pl.BlockSpec((GM_CHUNK, zw), blk), pl.BlockSpec((GM_CHUNK, cat_w), blk),
                  pl.BlockSpec(ln_g.shape, full2), pl.BlockSpec(ln_b.shape, full2),
                  pl.BlockSpec(w_s.shape, full3), pl.BlockSpec(b_s.shape, full2),
                  pl.BlockSpec((mem_len, kv.shape[1]), lambda b, n: (b, 0))],
        out_specs=[pl.BlockSpec((GM_CHUNK, zw), blk),
                   pl.BlockSpec((mem_len, kv.shape[1]), lambda b, n: (b, 0)),
                   pl.BlockSpec(ln_g.shape, full2), pl.BlockSpec(ln_b.shape, full2),
                   pl.BlockSpec(w_s.shape, full3), pl.BlockSpec(b_s.shape, full2)],
        out_shape=[jax.ShapeDtypeStruct((T, zw), BF), jax.ShapeDtypeStruct(kv.shape, F32),
                   jax.ShapeDtypeStruct(ln_g.shape, F32), jax.ShapeDtypeStruct(ln_b.shape, F32),
                   jax.ShapeDtypeStruct(w_s.shape, F32), jax.ShapeDtypeStruct(b_s.shape, F32)],
        name="gmlp_bwd",
        compiler_params=pltpu.CompilerParams(dimension_semantics=("arbitrary", "arbitrary"), vmem_limit_bytes=56 << 20),
    )(z, dcat, ln_g, ln_b, w_s, b_s, kv)


def _place():
    x, y, c = lax.axis_index("x"), lax.axis_index("y"), lax.axis_index("c")
    chips = [(1 - x, y), (x, 1 - y), (1 - x, 1 - y)]
    return x, y, c, chips


def _half(ref, kind, e):
    if kind == "col":
        n = ref.shape[1] // 2
        return ref.at[:, pl.ds(pl.multiple_of(e * n, n), n), :]
    n = ref.shape[2] // 2
    return ref.at[:, :, pl.ds(pl.multiple_of(e * n, n), n)]


def _slot(ref, kind, j, n):
    if kind == "col":
        return ref.at[:, :, pl.ds(pl.multiple_of(j * n, n), n)]
    return ref.at[:, pl.ds(pl.multiple_of(j * n, n), n), :]


def _allgather_seq(name, items, cid):
    nt = len(items)
    kinds = [k for (_, k, _) in items]
    out_type = []
    for s, k, l in items:
        L, r, c = s.shape
        lo = L if l is None else 1
        out_type.append(jax.ShapeDtypeStruct((lo, r, 4 * c) if k == "col" else (lo, 4 * r, c), s.dtype))

    def body(*refs):
        sh = [refs[t] if items[t][2] is None else refs[t].at[pl.ds(items[t][2], 1)] for t in range(nt)]
        full = refs[nt:2 * nt]
        loc, s_ici, r_ici, s_d2d, r_d2d = refs[2 * nt:]
        x, y, c, chips = _place()
        own = 2 * x + y
        sibling = (x, y, 1 - c)
        barrier = pltpu.get_barrier_semaphore()
        for peer in [(px, py, c) for (px, py) in chips] + [sibling]:
            pl.semaphore_signal(barrier, inc=1, device_id=peer, device_id_type=MESH)
        pl.semaphore_wait(barrier, 4)
        width = [sh[t].shape[2] if kinds[t] == "col" else sh[t].shape[1] for t in range(nt)]
        started = []
        for t in range(nt):
            mine = pltpu.make_async_copy(sh[t], _slot(full[t], kinds[t], own, width[t]), loc.at[t])
            mine.start()
            started.append(mine)
        sent = []
        for t in range(nt):
            for p, (px, py) in enumerate(chips):
                cp = pltpu.make_async_remote_copy(
                    src_ref=_half(sh[t], kinds[t], c), dst_ref=_half(_slot(full[t], kinds[t], own, width[t]), kinds[t], c),
                    send_sem=s_ici.at[t, p], recv_sem=r_ici.at[t, p], device_id=(px, py, c), device_id_type=MESH)
                cp.start()
                sent.append(cp)
        for t in range(nt):
            for p, (px, py) in enumerate(chips):
                landed = _half(_slot(full[t], kinds[t], 2 * px + py, width[t]), kinds[t], c)
                pltpu.make_async_remote_copy(
                    src_ref=landed, dst_ref=landed, send_sem=s_ici.at[t, p], recv_sem=r_ici.at[t, p],
                    device_id=(px, py, c), device_id_type=MESH).wait_recv()
                fw = pltpu.make_async_remote_copy(
                    src_ref=landed, dst_ref=landed, send_sem=s_d2d.at[t, p], recv_sem=r_d2d.at[t, p],
                    device_id=sibling, device_id_type=MESH)
                fw.start()
                sent.append(fw)
        for t in range(nt):
            for p, (px, py) in enumerate(chips):
                other = _half(_slot(full[t], kinds[t], 2 * px + py, width[t]), kinds[t], 1 - c)
                pltpu.make_async_remote_copy(
                    src_ref=other, dst_ref=other, send_sem=s_d2d.at[t, p], recv_sem=r_d2d.at[t, p],
                    device_id=sibling, device_id_type=MESH).wait_recv()
        for cp in sent:
            cp.wait_send()
        for cp in started:
            cp.wait()

    return pl.kernel(
        body, out_type=out_type, mesh=plsc.ScalarSubcoreMesh(axis_name="seq", num_cores=1),
        scratch_types=[pltpu.SemaphoreType.DMA((nt,)), pltpu.SemaphoreType.DMA((nt, 3)), pltpu.SemaphoreType.DMA((nt, 3)),
                       pltpu.SemaphoreType.DMA((nt, 3)), pltpu.SemaphoreType.DMA((nt, 3))],
        compiler_params=pltpu.CompilerParams(collective_id=cid), name=name,
    )(*[s for (s, _, _) in items])


def _slot2(ref, kind, j, n):
    if kind == "col":
        return ref.at[:, pl.ds(pl.multiple_of(j * n, n), n)]
    return ref.at[pl.ds(pl.multiple_of(j * n, n), n), :]


def _rs_chips_seq(name, parts, kinds, cid):
    nm = len(parts)
    out_type = []
    for g, k in zip(parts, kinds):
        r, c = g.shape
        ps = (r, c // 4) if k == "col" else (r // 4, c)
        out_type += [jax.ShapeDtypeStruct(ps, BF), jax.ShapeDtypeStruct((3,) + ps, BF)]

    def body(*refs):
        g = refs[:nm]
        outs = refs[nm:3 * nm]
        loc, ssem, rsem = refs[3 * nm:]
        x, y, c, chips = _place()
        own = 2 * x + y
        barrier = pltpu.get_barrier_semaphore()
        for (px, py) in chips:
            pl.semaphore_signal(barrier, inc=1, device_id=(px, py, c), device_id_type=MESH)
        pl.semaphore_wait(barrier, 3)
        cps = []
        for m in range(nm):
            k = kinds[m]
            own_o, got_o = outs[2 * m], outs[2 * m + 1]
            n = g[m].shape[1] // 4 if k == "col" else g[m].shape[0] // 4
            lc = pltpu.make_async_copy(_slot2(g[m], k, own, n), own_o, loc.at[m])
            lc.start()
            cps.append(lc)
            for p, (px, py) in enumerate(chips):
                cp = pltpu.make_async_remote_copy(
                    src_ref=_slot2(g[m], k, 2 * px + py, n), dst_ref=got_o.at[p],
                    send_sem=ssem.at[m, p], recv_sem=rsem.at[m, p], device_id=(px, py, c), device_id_type=MESH)
                cp.start()
                cps.append(cp)
        for cp in cps:
            cp.wait()

    return pl.kernel(
        body, out_type=out_type, mesh=plsc.ScalarSubcoreMesh(axis_name="seq", num_cores=1),
        scratch_types=[pltpu.SemaphoreType.DMA((nm,)), pltpu.SemaphoreType.DMA((nm, 3)), pltpu.SemaphoreType.DMA((nm, 3))],
        compiler_params=pltpu.CompilerParams(collective_id=cid), name=name,
    )(*parts)


def _finish_share(name, own, got, kind, c_arr):
    L, r, c = own.shape
    tr = _pick(r, 128 if kind == "col" else 256)
    nb = r // tr
    nq = L * nb
    own2 = own.reshape(L * r, c)
    got2 = got.reshape(3 * L * r, c)
    pick = lambda h, q: q * (1 - h) + (nq - 1) * h
    in_specs = [pl.BlockSpec((tr, c), lambda h, q, cc: (pick(h, q), 0))]
    in_specs += [pl.BlockSpec((tr, c), functools.partial(lambda h, q, cc, p: (p * nq + pick(h, q), 0), p=p)) for p in range(3)]
    if kind == "col":
        out_sd = (L, 2, r, c)
        o_spec = pl.BlockSpec((None, 2, tr, c), lambda h, q, cc: ((q * h) // nb, 0, (q * h) % nb, 0))
    else:
        out_sd = (L * r, 2 * c)
        o_spec = pl.BlockSpec((tr, 2 * c), lambda h, q, cc: (q * h, 0))

    def kern(c_ref, o_ref, g0, g1, g2, out_ref, mine, recv, ssem, rsem):
        h, q = pl.program_id(0), pl.program_id(1)
        x, y, cc, _ = _place()

        def swap(qq):
            return pltpu.make_async_remote_copy(src_ref=mine.at[qq], dst_ref=recv.at[qq], send_sem=ssem.at[qq],
                                                recv_sem=rsem.at[qq], device_id=(x, y, 1 - cc), device_id_type=MESH)

        @pl.when(h == 0)
        def _():
            mine[q] = ((o_ref[...].astype(F32) + g0[...].astype(F32)) + g1[...].astype(F32)) + g2[...].astype(F32)
            swap(q).start()

        @pl.when(h == 1)
        def _():
            swap(q).wait()
            a, b = mine[q], recv[q]
            first = c_ref[0] == 0
            lo, hi = jnp.where(first, a, b), jnp.where(first, b, a)
            if kind == "col":
                out_ref[0] = lo
                out_ref[1] = hi
            else:
                out_ref[:, :c] = lo
                out_ref[:, c:] = hi

    est = 2 * nq * tr * c * 4 + 6 * tr * c * 4 + 8 * tr * c * 2
    full = pl.pallas_call(
        kern,
        grid_spec=pltpu.PrefetchScalarGridSpec(
            num_scalar_prefetch=1, grid=(2, nq), in_specs=in_specs, out_specs=o_spec,
            scratch_shapes=[pltpu.VMEM((nq, tr, c), F32), pltpu.VMEM((nq, tr, c), F32),
                            pltpu.SemaphoreType.DMA((nq,)), pltpu.SemaphoreType.DMA((nq,))]),
        out_shape=jax.ShapeDtypeStruct(out_sd, F32), name=name,
        compiler_params=pltpu.CompilerParams(dimension_semantics=("arbitrary", "arbitrary"),
                                             vmem_limit_bytes=int(min(VMEM_CAP_BYTES, est + (12 << 20)))),
    )(c_arr, own2, got2, got2, got2)
    return full.reshape(L, 2 * r, c) if kind == "col" else full.reshape(L, r, 2 * c)


def _small_allreduce(buf, name):
    R = buf.shape[0]

    def body(x_ref, o_ref, slots, ssem, rsem):
        x, y, c, _ = _place()
        me = 4 * x + 2 * y + c
        slots[0] = x_ref[...]
        cps = []
        for k in range(1, 8):
            bx, by, bc = (k >> 2) & 1, (k >> 1) & 1, k & 1
            peer = (1 - x if bx else x, 1 - y if by else y, 1 - c if bc else c)
            cp = pltpu.make_async_remote_copy(src_ref=x_ref, dst_ref=slots.at[k], send_sem=ssem.at[k - 1],
                                              recv_sem=rsem.at[k - 1], device_id=peer, device_id_type=MESH)
            cp.start()
            cps.append(cp)
        for cp in cps:
            cp.wait()
        acc = slots[jnp.bitwise_xor(me, 0)]
        for d in range(1, 8):
            acc = acc + slots[jnp.bitwise_xor(me, d)]
        o_ref[...] = acc

    vm = pl.BlockSpec(memory_space=pltpu.VMEM)
    return pl.pallas_call(
        body, out_shape=jax.ShapeDtypeStruct(buf.shape, F32), in_specs=[vm], out_specs=vm,
        scratch_shapes=[pltpu.VMEM((8, R, LANES), F32), pltpu.SemaphoreType.DMA((7,)), pltpu.SemaphoreType.DMA((7,))],
        name=name,
        compiler_params=pltpu.CompilerParams(vmem_limit_bytes=int(min(VMEM_CAP_BYTES, 12 * R * LANES * 4 + (8 << 20)))),
    )(buf)


PACK_TILE_ROWS = 8


def _item_rows(shape):
    n = 1
    for d in shape:
        n *= d
    return -(-n // (PACK_TILE_ROWS * LANES)) * PACK_TILE_ROWS


def _pack(arrs, rows_total):
    buf = jnp.zeros((rows_total, LANES), F32)
    r = 0
    for a in arrs:
        f = a.reshape(-1).astype(F32)
        nr = _item_rows(a.shape)
        block = jnp.pad(f, (0, nr * LANES - f.shape[0])).reshape(nr, LANES)
        buf = lax.dynamic_update_slice(buf, block, (r, 0))
        r += nr
    return buf


def _unpack(buf, shapes):
    out, r = [], 0
    for s in shapes:
        n = 1
        for d in s:
            n *= d
        nr = _item_rows(s)
        out.append(buf[r:r + nr].reshape(-1)[:n].reshape(s))
        r += nr
    return out


def _rows_needed(shapes):
    return sum(_item_rows(s) for s in shapes)


def _two_rows(a, b):
    out = jnp.zeros((2, a.shape[1]), a.dtype)
    return lax.dynamic_update_slice(lax.dynamic_update_slice(out, a, (0, 0)), b, (1, 0))


def _adam(w, g, m, v):
    m = ADAM_B1 * m + (1.0 - ADAM_B1) * g
    v = ADAM_B2 * v + (1.0 - ADAM_B2) * jnp.square(g)
    m_hat = m / (1.0 - ADAM_B1 ** ADAM_STEP)
    v_hat = v / (1.0 - ADAM_B2 ** ADAM_STEP)
    delta = -ADAM_LR * (m_hat / (jnp.sqrt(v_hat) + ADAM_EPS) + ADAM_WD * w)
    return delta, m, v


def _adam_call(name, w2, g2, m2, v2, tr):
    def fn(rv, cv):
        return list(_adam(*rv)), []

    width = w2.shape[1]
    return _rowcall(name, fn, [(w2, 0, width), (g2, 0, width), (m2, 0, width), (v2, 0, width)], [],
                    [(width, F32)] * 3, [], tr)


def kernel(x, mem, mem_norm, lb_logits, ffn1_norm, ffn1_w_in, ffn1_w_out, mix_norm, mem_w_kv, hgrn_w_in, hgrn_gnorm, hgrn_w_out, gmlp_w_in, gmlp_ln_g, gmlp_ln_b, gmlp_w_s, gmlp_b_s, gmlp_w_out, ffn2_norm, ffn2_w_in, ffn2_w_out, final_norm, loss_target, m_mem_norm, m_lb_logits, m_ffn1_norm, m_ffn1_w_in, m_ffn1_w_out, m_mix_norm, m_mem_w_kv, m_hgrn_w_in, m_hgrn_gnorm, m_hgrn_w_out, m_gmlp_w_in, m_gmlp_ln_g, m_gmlp_ln_b, m_gmlp_w_s, m_gmlp_b_s, m_gmlp_w_out, m_ffn2_norm, m_ffn2_w_in, m_ffn2_w_out, m_final_norm, v_mem_norm, v_lb_logits, v_ffn1_norm, v_ffn1_w_in, v_ffn1_w_out, v_mix_norm, v_mem_w_kv, v_hgrn_w_in, v_hgrn_gnorm, v_hgrn_w_out, v_gmlp_w_in, v_gmlp_ln_g, v_gmlp_ln_b, v_gmlp_w_s, v_gmlp_b_s, v_gmlp_w_out, v_ffn2_norm, v_ffn2_w_in, v_ffn2_w_out, v_final_norm):
    bl, seq, D = x.shape
    T = bl * seq
    mem_len = mem.shape[1]
    chip = 2 * lax.axis_index("x") + lax.axis_index("y")
    c_arr = lax.axis_index("c").astype(jnp.int32).reshape(1)
    TR = 256

    big = [("ffn1_w_in", ffn1_w_in, "col"), ("ffn1_w_out", ffn1_w_out, "row"), ("mem_w_kv", mem_w_kv, "col"),
           ("hgrn_w_in", hgrn_w_in, "col"), ("hgrn_w_out", hgrn_w_out, "row"), ("gmlp_w_in", gmlp_w_in, "col"),
           ("gmlp_w_out", gmlp_w_out, "row"), ("ffn2_w_in", ffn2_w_in, "col"), ("ffn2_w_out", ffn2_w_out, "row")]
    kinds = [k for (_, _, k) in big]
    shards_bf = []
    for nm, w, _ in big:
        L, r, c = w.shape
        (wb,) = _rowcall("cast_" + nm, lambda rv, cv: ([rv[0]], []), [(w.reshape(L * r, c), 0, c)], [], [(c, BF)], [], 512)
        shards_bf.append(wb.reshape(L, r, c))
    sb = dict(zip([nm for (nm, _, _) in big], shards_bf))
    groups = [[("ffn1_w_in", 0), ("ffn1_w_out", 0)],
              [("mem_w_kv", None), ("hgrn_w_in", None), ("hgrn_w_out", None)],
              [("ffn2_w_in", 0), ("ffn2_w_out", 0)],
              [("ffn1_w_in", 1), ("ffn1_w_out", 1)],
              [("gmlp_w_in", None), ("gmlp_w_out", None)],
              [("ffn2_w_in", 1), ("ffn2_w_out", 1)]]
    kind_of = {nm: k for (nm, _, k) in big}
    gathered = {nm: [None, None] for nm in ("ffn1_w_in", "ffn1_w_out", "ffn2_w_in", "ffn2_w_out")}
    for gi, grp in enumerate(groups):
        outs = _allgather_seq("gather_%d" % gi, [(sb[nm], kind_of[nm], l) for (nm, l) in grp], gi)
        for (nm, l), o in zip(grp, outs):
            if l is None:
                gathered[nm] = o
            else:
                gathered[nm][l] = o

    ln_w = GM_GROUPS * GM_GROUP_DIM
    placed = lax.dynamic_update_slice(jnp.zeros((8, ln_w), F32), 0.5 * gmlp_ln_g, (0, chip * gmlp_ln_g.shape[1]))
    placed = lax.dynamic_update_slice(placed, 0.5 * gmlp_ln_b, (1, chip * gmlp_ln_g.shape[1]))
    ln_full = _small_allreduce(placed.reshape(16, LANES), "gather_ln").reshape(8, ln_w)
    ln_g_full, ln_b_full = ln_full[0:1], ln_full[1:2]

    def rms_fwd(name, xin, g):
        (h,) = _rowcall(name, lambda rv, cv: ([_rmsnorm(rv[0], cv[0])], []), [(xin, 0, D)], [g.reshape(1, D)], [(D, BF)], [], TR)
        return h

    def ffn_fwd(tag, xin, g, w_in, w_out, layer):
        dff = w_out[layer].shape[1]
        h = rms_fwd("rms_" + tag, xin, g)
        zg, zu, a = _ffn_in_swiglu("ffn_in_" + tag, h, w_in[layer], 2048, 256)
        xo = _mm("ffn_out_" + tag, a, w_out[layer], "nn", F32, 1024, 1024, dff, scale=0.5, res=xin, b_lead=0)
        return xo, (xin, h, zg, zu, a)

    def ffn_bwd(tag, dxo, saved, g, w_in, w_out, layer):
        xin, h, zg, zu, a = saved
        dff = w_out[layer].shape[1]
        dw_out = _mm_tn_pair("ffn_dwo_" + tag, a, dxo, "row", c_arr, dff // 2, T, scale=0.5)
        dz = _ffn_da_swiglu("ffn_da_" + tag, dxo, w_out[layer], zg, zu, 512)
        dw_in = _mm_tn_pair("ffn_dwi_" + tag, h, dz, "col", c_arr, 512, T)
        dx, dg = _mm_dh_rms("ffn_dh_" + tag, dz, w_in[layer], xin, g.reshape(1, D), dxo, 512)
        return dx, dg, dw_in, dw_out

    def rms_bwd(name, xin, g, dh, dres):
        def fn(rv, cv):
            _, vjp = jax.vjp(_rmsnorm, rv[0], cv[0])
            dx, dg = vjp(rv[1])
            if dres is not None:
                dx = dx + rv[2]
            return [dx], [dg]

        rows = [(xin, 0, D), (dh, 0, D)] + ([(dres, 0, D)] if dres is not None else [])
        dx, dg = _rowcall(name, fn, rows, [g.reshape(1, D)], [(D, F32)], [((1, D), F32)], TR)
        return dx, dg

    x0 = x.reshape(T, D)
    tgt = loss_target.reshape(T, D)
    mem2 = mem.reshape(bl * mem_len, D)
    memn = rms_fwd("rms_mem", mem2, mem_norm)
    kv = [_mm("kv_%d" % i, memn, gathered["mem_w_kv"], "nn", F32, 512, 512, D, b_lead=i) for i in range(2)]

    x1, sv_f10 = ffn_fwd("f1l0", x0, ffn1_norm[0], gathered["ffn1_w_in"], gathered["ffn1_w_out"], 0)
    h_m0 = rms_fwd("rms_mix0", x1, mix_norm[0])
    z_m0 = _mm("mix_in_0", h_m0, gathered["hgrn_w_in"], "nn", F32, 2048, 512, D, b_lead=0)
    cat0, stash0 = _hgrn_fwd2(z_m0, lb_logits, hgrn_gnorm, kv[0], bl, seq)
    x2 = _mm("mix_out_0", cat0, gathered["hgrn_w_out"], "nn", F32, 1024, 1024, cat0.shape[1], res=x1, b_lead=0)
    x3, sv_f20 = ffn_fwd("f2l0", x2, ffn2_norm[0], gathered["ffn2_w_in"], gathered["ffn2_w_out"], 0)
    x4, sv_f11 = ffn_fwd("f1l1", x3, ffn1_norm[1], gathered["ffn1_w_in"], gathered["ffn1_w_out"], 1)
    h_m1 = rms_fwd("rms_mix1", x4, mix_norm[1])
    z_m1 = _mm("mix_in_1", h_m1, gathered["gmlp_w_in"], "nn", F32, 2048, 512, D, b_lead=0)
    nc1 = seq // GM_CHUNK
    w_s, b_s = gmlp_w_s[0], gmlp_b_s[0]
    cat1 = _gmlp_fwd(z_m1, ln_g_full, ln_b_full, w_s, b_s, kv[1], bl, nc1)
    x5 = _mm("mix_out_1", cat1, gathered["gmlp_w_out"], "nn", F32, 1024, 1024, cat1.shape[1], res=x4, b_lead=0)
    x6, sv_f21 = ffn_fwd("f2l1", x5, ffn2_norm[1], gathered["ffn2_w_in"], gathered["ffn2_w_out"], 1)

    def head(rv, cv):
        def f(xx, gg):
            err = _rmsnorm(xx, gg) - rv[1]
            return 0.5 * jnp.sum(jnp.mean(err * err, axis=-1, keepdims=True), axis=0, keepdims=True)

        ls, vjp = jax.vjp(f, rv[0], cv[0])
        dx, dg = vjp(jnp.ones((1, 1), F32))
        return [dx], [dg, jnp.broadcast_to(ls, (1, 128))]

    dx6, d_final, loss_part = _rowcall("loss_head", head, [(x6, 0, D), (tgt, 0, D)], [final_norm.reshape(1, D)],
                                       [(D, F32)], [((1, D), F32), ((1, 128), F32)], TR)

    rs_out = {}
    n_gather = len(groups)

    def rs(gi, items):
        outs = _rs_chips_seq("reduce_%d" % gi, [p for (_, p, _) in items], [k for (_, _, k) in items], n_gather + gi)
        for i, (key, _, _) in enumerate(items):
            rs_out[key] = (outs[2 * i], outs[2 * i + 1])

    dx5, dg_f21, dwi_f21, dwo_f21 = ffn_bwd("f2l1", dx6, sv_f21, ffn2_norm[1], gathered["ffn2_w_in"], gathered["ffn2_w_out"], 1)
    rs(0, [(("ffn2_w_out", 1), dwo_f21, "row"), (("ffn2_w_in", 1), dwi_f21, "col")])
    dcat1 = _mm("mix_dcat_1", dx5, gathered["gmlp_w_out"], "nt", F32, 2048, 1024, D, b_lead=0)
    dwo_m1 = _mm_tn_pair("mix_dwo_1", cat1, dx5, "row", c_arr, 1024, T)
    dz_m1, dkv1, d_lng, d_lnb, d_ws, d_bs = _gmlp_bwd(z_m1, dcat1, ln_g_full, ln_b_full, w_s, b_s, kv[1], bl, nc1)
    dx4, dg_m1 = _mm_dh_rms("mix_dh_1", dz_m1, gathered["gmlp_w_in"], x4, mix_norm[1].reshape(1, D), dx5, 512)
    dwi_m1 = _mm_tn_pair("mix_dwi_1", h_m1, dz_m1, "col", c_arr, 1024, T)
    rs(1, [(("gmlp_w_out", 0), dwo_m1, "row"), (("gmlp_w_in", 0), dwi_m1, "col")])
    dx3, dg_f11, dwi_f11, dwo_f11 = ffn_bwd("f1l1", dx4, sv_f11, ffn1_norm[1], gathered["ffn1_w_in"], gathered["ffn1_w_out"], 1)
    rs(2, [(("ffn1_w_out", 1), dwo_f11, "row"), (("ffn1_w_in", 1), dwi_f11, "col")])

    dx2, dg_f20, dwi_f20, dwo_f20 = ffn_bwd("f2l0", dx3, sv_f20, ffn2_norm[0], gathered["ffn2_w_in"], gathered["ffn2_w_out"], 0)
    rs(3, [(("ffn2_w_out", 0), dwo_f20, "row"), (("ffn2_w_in", 0), dwi_f20, "col")])
    dcat0 = _mm("mix_dcat_0", dx2, gathered["hgrn_w_out"], "nt", F32, 2048, 1024, D, b_lead=0)
    dwo_m0 = _mm_tn_pair("mix_dwo_0", cat0, dx2, "row", c_arr, 1024, T)
    dz_m0, dkv0, d_lb, d_gn = _hgrn_bwd2(z_m0, dcat0, stash0, lb_logits, hgrn_gnorm, kv[0], bl, seq)
    dx1, dg_m0 = _mm_dh_rms("mix_dh_0", dz_m0, gathered["hgrn_w_in"], x1, mix_norm[0].reshape(1, D), dx2, 512)
    dwi_m0 = _mm_tn_pair("mix_dwi_0", h_m0, dz_m0, "col", c_arr, 1024, T)
    rs(4, [(("hgrn_w_out", 0), dwo_m0, "row"), (("hgrn_w_in", 0), dwi_m0, "col")])

    dwkv = [_mm_tn_pair("kv_dw_%d" % i, memn, dkv, "col", c_arr, 1024, 512) for i, dkv in enumerate([dkv0, dkv1])]
    rs(5, [(("mem_w_kv", 0), dwkv[0], "col"), (("mem_w_kv", 1), dwkv[1], "col")])
    dmemn = _mm("kv_dx_0", dkv0, gathered["mem_w_kv"], "nt", F32, 512, 512, 1024, b_lead=0)
    dmemn = _mm("kv_dx_1", dkv1, gathered["mem_w_kv"], "nt", F32, 512, 512, 1024, res=dmemn, b_lead=1)
    _, d_memnorm = rms_bwd("rms_bwd_mem", mem2, mem_norm, dmemn, None)

    dx0, dg_f10, dwi_f10, dwo_f10 = ffn_bwd("f1l0", dx1, sv_f10, ffn1_norm[0], gathered["ffn1_w_in"], gathered["ffn1_w_out"], 0)
    rs(6, [(("ffn1_w_out", 0), dwo_f10, "row")])
    rs(7, [(("ffn1_w_in", 0), dwi_f10, "col")])

    shard_grads = []
    for (nm, w, k) in big:
        per_layer = []
        for l in range(w.shape[0]):
            own, got = rs_out[(nm, l)]
            per_layer.append(_finish_share("finish_%s_%d" % (nm, l), own[None], got[:, None], k, c_arr))
        shard_grads.append(per_layer[0] if len(per_layer) == 1 else jnp.concatenate(per_layer, axis=0))

    big_w = [w for (_, w, _) in big]
    big_m = [m_ffn1_w_in, m_ffn1_w_out, m_mem_w_kv, m_hgrn_w_in, m_hgrn_w_out, m_gmlp_w_in, m_gmlp_w_out, m_ffn2_w_in, m_ffn2_w_out]
    big_v = [v_ffn1_w_in, v_ffn1_w_out, v_mem_w_kv, v_hgrn_w_in, v_hgrn_w_out, v_gmlp_w_in, v_gmlp_w_out, v_ffn2_w_in, v_ffn2_w_out]
    big_out = {}
    for (nm, w, _), g, m, v in zip(big, shard_grads, big_m, big_v):
        L, r, c = w.shape
        d2, m2, v2 = _adam_call("adam_" + nm, w.reshape(L * r, c), g.reshape(L * r, c), m.reshape(L * r, c),
                                v.reshape(L * r, c), 256)
        big_out[nm] = (g, d2.reshape(w.shape), m2.reshape(w.shape), v2.reshape(w.shape))

    d_ffn1n = _two_rows(dg_f10, dg_f11)
    d_mixn = _two_rows(dg_m0, dg_m1)
    d_ffn2n = _two_rows(dg_f20, dg_f21)
    small_parts = [loss_part[:, :1], d_memnorm, d_lb, d_ffn1n, d_mixn, d_gn, d_lng, d_lnb, d_ws, d_bs, d_ffn2n, d_final]
    red_shapes = [(1,), mem_norm.shape, lb_logits.shape, ffn1_norm.shape, mix_norm.shape, hgrn_gnorm.shape, (1, ln_w), (1, ln_w),
                  gmlp_w_s.shape, gmlp_b_s.shape, ffn2_norm.shape, final_norm.shape]
    red = _small_allreduce(_pack(small_parts, _rows_needed(red_shapes)), "reduce_small")
    (loss_v, g_memn, g_lb, g_f1n, g_mixn, g_gn, g_lng_full, g_lnb_full, g_ws, g_bs, g_f2n, g_fin) = _unpack(red, red_shapes)
    lsh = gmlp_ln_g.shape[1]
    g_lng = lax.dynamic_slice(g_lng_full, (0, chip * lsh), (1, lsh))
    g_lnb = lax.dynamic_slice(g_lnb_full, (0, chip * lsh), (1, lsh))
    small_w = [mem_norm, lb_logits, ffn1_norm, mix_norm, hgrn_gnorm, gmlp_ln_g, gmlp_ln_b, gmlp_w_s, gmlp_b_s, ffn2_norm, final_norm]
    small_g = [g_memn, g_lb, g_f1n, g_mixn, g_gn, g_lng, g_lnb, g_ws, g_bs, g_f2n, g_fin]
    small_m = [m_mem_norm, m_lb_logits, m_ffn1_norm, m_mix_norm, m_hgrn_gnorm, m_gmlp_ln_g, m_gmlp_ln_b, m_gmlp_w_s, m_gmlp_b_s, m_ffn2_norm, m_final_norm]
    small_v = [v_mem_norm, v_lb_logits, v_ffn1_norm, v_mix_norm, v_hgrn_gnorm, v_gmlp_ln_g, v_gmlp_ln_b, v_gmlp_w_s, v_gmlp_b_s, v_ffn2_norm, v_final_norm]
    sshapes = [w.shape for w in small_w]
    nrow = _rows_needed(sshapes)
    d_p, m_p, v_p = _adam_call("adam_small", _pack(small_w, nrow), _pack(small_g, nrow), _pack(small_m, nrow), _pack(small_v, nrow), nrow)
    s_delta, s_m, s_v = _unpack(d_p, sshapes), _unpack(m_p, sshapes), _unpack(v_p, sshapes)
    small_names = ["mem_norm", "lb_logits", "ffn1_norm", "mix_norm", "hgrn_gnorm", "gmlp_ln_g", "gmlp_ln_b", "gmlp_w_s", "gmlp_b_s", "ffn2_norm", "final_norm"]
    small_out = {nm: (g.reshape(w.shape), d, m, v) for nm, w, g, d, m, v in zip(small_names, small_w, small_g, s_delta, s_m, s_v)}

    order = ["mem_norm", "lb_logits", "ffn1_norm", "ffn1_w_in", "ffn1_w_out", "mix_norm", "mem_w_kv", "hgrn_w_in", "hgrn_gnorm",
             "hgrn_w_out", "gmlp_w_in", "gmlp_ln_g", "gmlp_ln_b", "gmlp_w_s", "gmlp_b_s", "gmlp_w_out", "ffn2_norm", "ffn2_w_in",
             "ffn2_w_out", "final_norm"]
    allo = {**big_out, **small_out}
    grad_x = dx0.reshape(x.shape)
    return (loss_v.reshape(()), grad_x, *[allo[n][0] for n in order], *[allo[n][1] for n in order],
            *[allo[n][2] for n in order], *[allo[n][3] for n in order])
```

```python
import functools

import jax
import jax.numpy as jnp
from jax import lax
from jax.experimental import pallas as pl
from jax.experimental.pallas import tpu as pltpu
from jax.experimental.pallas import tpu_sc as plsc

BF = jnp.bfloat16
F32 = jnp.float32
MESH = pl.DeviceIdType.MESH

EPS = 1e-6
D_MODEL = 1024
HG_HEADS = 8
HG_DIM = 128
HG_CHUNK = 64
GM_CHUNK = 128
GM_GROUPS = 8
GM_GROUP_DIM = 256
XA_HEADS = 4
XA_DIM = 256
ADAM_LR = 0.001
ADAM_B1 = 0.9
ADAM_B2 = 0.999
ADAM_EPS = 1e-08
ADAM_WD = 0.01
ADAM_STEP = 10

VMEM_CAP_BYTES = 60 * 1024 * 1024
LANES = 1024


def _pick(n, cap, mult=16):
    if n <= cap:
        return n
    for d in range(cap - cap % mult, 0, -mult):
        if n % d == 0:
            return d
    raise ValueError((n, cap, mult))


def _dg(a, b, ca, cb):
    return lax.dot_general(a.astype(BF), b.astype(BF), (((ca,), (cb,)), ((), ())), preferred_element_type=F32)


@jax.custom_vjp
def dot_nn(a, b):
    return _dg(a, b, 1, 0)


def _nn_fwd(a, b):
    return _dg(a, b, 1, 0), (a, b)


def _nn_bwd(r, g):
    a, b = r
    return _dg(g, b, 1, 1), _dg(a, g, 0, 0)


dot_nn.defvjp(_nn_fwd, _nn_bwd)


@jax.custom_vjp
def dot_nt(a, b):
    return _dg(a, b, 1, 1)


def _nt_fwd(a, b):
    return _dg(a, b, 1, 1), (a, b)


def _nt_bwd(r, g):
    a, b = r
    return _dg(g, b, 1, 0), _dg(g, a, 0, 0)


dot_nt.defvjp(_nt_fwd, _nt_bwd)


@jax.custom_vjp
def dot_tn(a, b):
    return _dg(a, b, 0, 0)


def _tn_fwd(a, b):
    return _dg(a, b, 0, 0), (a, b)


def _tn_bwd(r, g):
    a, b = r
    return _dg(b, g, 1, 1), _dg(a, g, 1, 0)


dot_tn.defvjp(_tn_fwd, _tn_bwd)


def _rmsnorm(x, g):
    return x * lax.rsqrt(jnp.mean(x * x, axis=-1, keepdims=True) + EPS) * g


def _silu(x):
    return x * jax.nn.sigmoid(x)


def _gelu(x):
    return 0.5 * x * (1.0 + lax.erf(x * (0.5 ** 0.5)))


def _softmax_last(s):
    m = lax.stop_gradient(jnp.max(s, axis=-1, keepdims=True))
    e = jnp.exp(s - m)
    return e / jnp.sum(e, axis=-1, keepdims=True)


def _tril(n):
    r = lax.broadcasted_iota(jnp.int32, (n, n), 0)
    c = lax.broadcasted_iota(jnp.int32, (n, n), 1)
    return r >= c


def _cumsum_rows(l):
    n = l.shape[0]
    return lax.dot_general(_tril(n).astype(F32), l, (((1,), (0,)), ((), ())),
                           precision=lax.Precision.HIGHEST, preferred_element_type=F32)


def _attention(zx, mk, mv):
    s = dot_nt(zx, mk) * (XA_DIM ** -0.5)
    return dot_nn(_softmax_last(s), mv)


def _hgrn_head(zq, zf, zi, zg, l0, l1, l2, gn, S):
    m = lax.stop_gradient(jnp.maximum(jnp.maximum(l0, l1), l2))
    e0 = jnp.exp(l0 - m)
    lb = e0 / (e0 + jnp.exp(l1 - m) + jnp.exp(l2 - m))
    q = _silu(zq)
    f = lb + (1.0 - lb) * jax.nn.sigmoid(zf)
    k = 1.0 - f
    b = _cumsum_rows(jnp.log(f))
    b_last = b[HG_CHUNK - 1:HG_CHUNK, :]
    q_dec = q * jnp.exp(b)
    k_inv = k * jnp.exp(-b)
    a = jnp.where(_tril(HG_CHUNK), dot_nt(q_dec, k_inv), 0.0)
    o = dot_nn(a, zi) + dot_nn(q_dec, S)
    S_new = jnp.exp(b_last).reshape(HG_DIM, 1) * S + dot_tn(k * jnp.exp(b_last - b), zi)
    o = _rmsnorm(o, gn) * _silu(zg)
    return o, S_new


def _hgrn_block(zq, zf, zi, zg, zx, l0, l1, l2, gn, mk, mv, S):
    outs, s_new = [], []
    for h in range(HG_HEADS):
        o, sn = _hgrn_head(zq[h], zf[h], zi[h], zg[h], l0[h], l1[h], l2[h], gn, S[h])
        outs.append(o)
        s_new.append(sn)
    for a in range(XA_HEADS):
        outs.append(_attention(zx[a], mk[a], mv[a]))
    return outs, s_new


def _gmlp_block(zu, zv, zx, lng, lnb, ws, bs, mk, mv):
    gv = [_gelu(v) for v in zv]
    width = GM_GROUPS * GM_GROUP_DIM
    mu = sum(jnp.sum(g, axis=-1, keepdims=True) for g in gv) / width
    xc = [g - mu for g in gv]
    var = sum(jnp.sum(c * c, axis=-1, keepdims=True) for c in xc) / width
    r = lax.rsqrt(var + EPS)
    outs = []
    for g in range(GM_GROUPS):
        v = xc[g] * r * lng[g] + lnb[g]
        w = jnp.where(_tril(GM_CHUNK), ws[g], 0.0)
        mixed = dot_nn(w, v) + bs[g].reshape(GM_CHUNK, 1)
        outs.append(_gelu(zu[g]) * mixed)
    for a in range(XA_HEADS):
        outs.append(_attention(zx[a], mk[a], mv[a]))
    return outs


def _rowcall(name, fn, rows, consts, row_outs, acc_outs, tr):
    nrows = rows[0][0].shape[0]
    tr = _pick(nrows, tr)
    n_r, n_c, n_ro, n_ao = len(rows), len(consts), len(row_outs), len(acc_outs)

    def kern(*refs):
        rv = [r[...] for r in refs[:n_r]]
        cv = [r[...] for r in refs[n_r:n_r + n_c]]
        ro_refs = refs[n_r + n_c:n_r + n_c + n_ro]
        ao_refs = refs[n_r + n_c + n_ro:]
        ro, ao = fn(rv, cv)
        for ref, v in zip(ro_refs, ro):
            ref[...] = v.astype(ref.dtype)
        if n_ao:
            @pl.when(pl.program_id(0) == 0)
            def _():
                for ref in ao_refs:
                    ref[...] = jnp.zeros(ref.shape, ref.dtype)

            for ref, v in zip(ao_refs, ao):
                ref[...] += v.astype(ref.dtype)

    in_specs = [pl.BlockSpec((tr, w), functools.partial(lambda i, cb: (i, cb), cb=cb)) for (_, cb, w) in rows]
    in_specs += [pl.BlockSpec(c.shape, lambda i: (0, 0)) for c in consts]
    out_specs = [pl.BlockSpec((tr, w), lambda i: (i, 0)) for (w, _) in row_outs]
    out_specs += [pl.BlockSpec(s, lambda i: (0, 0)) for (s, _) in acc_outs]
    out_shape = [jax.ShapeDtypeStruct((nrows, w), dt) for (w, dt) in row_outs]
    out_shape += [jax.ShapeDtypeStruct(s, dt) for (s, dt) in acc_outs]
    est = sum(tr * w * a.dtype.itemsize for (a, _, w) in rows) + sum(tr * w * jnp.dtype(dt).itemsize for (w, dt) in row_outs)
    est += sum(c.size * c.dtype.itemsize for c in consts)
    outs = pl.pallas_call(
        kern, grid=(nrows // tr,), in_specs=in_specs, out_specs=out_specs, out_shape=out_shape, name=name,
        compiler_params=pltpu.CompilerParams(dimension_semantics=("arbitrary",),
                                             vmem_limit_bytes=VMEM_CAP_BYTES),
    )(*[a for (a, _, _) in rows], *consts)
    return outs


def _mm(name, a, b, mode, out_dtype, tm, tn, tk, scale=1.0, res=None, a_lead=None, b_lead=None):
    ash = a.shape[-2:]
    bsh = b.shape[-2:]
    if mode == "nn":
        (M, K), (K2, N) = ash, bsh
    elif mode == "nt":
        (M, K), (N, K2) = ash, bsh
    else:
        (K, M), (K2, N) = ash, bsh
    assert K == K2, (name, a.shape, b.shape)
    tm, tn, tk = min(tm, M), min(tn, N), min(tk, K)
    assert M % tm == 0 and N % tn == 0 and K % tk == 0, (name, M, N, K, tm, tn, tk)
    nk = K // tk
    dims = {"nn": (1, 0), "nt": (1, 1), "tn": (0, 0)}[mode]

    def lead(spec_shape, index_fn, lead_idx):
        if lead_idx is None:
            return pl.BlockSpec(spec_shape, index_fn)
        return pl.BlockSpec((None,) + spec_shape, lambda i, j, k: (lead_idx,) + index_fn(i, j, k))

    if mode == "tn":
        a_spec = lead((tk, tm), lambda i, j, k: (k, i), a_lead)
    else:
        a_spec = lead((tm, tk), lambda i, j, k: (i, k), a_lead)
    if mode == "nt":
        b_spec = lead((tn, tk), lambda i, j, k: (j, k), b_lead)
    else:
        b_spec = lead((tk, tn), lambda i, j, k: (k, j), b_lead)
    o_spec = pl.BlockSpec((tm, tn), lambda i, j, k: (i, j))
    has_res = res is not None

    def kern(*refs):
        a_ref, b_ref = refs[0], refs[1]
        res_ref = refs[2] if has_res else None
        o_ref = refs[3] if has_res else refs[2]
        acc_ref = refs[-1] if nk > 1 else None
        p = lax.dot_general(a_ref[...].astype(BF), b_ref[...].astype(BF), (((dims[0],), (dims[1],)), ((), ())),
                            preferred_element_type=F32)

        def finish(v):
            if scale != 1.0:
                v = v * scale
            if has_res:
                v = res_ref[...] + v
            o_ref[...] = v.astype(o_ref.dtype)

        if nk == 1:
            finish(p)
        else:
            k = pl.program_id(2)

            @pl.when(k == 0)
            def _():
                acc_ref[...] = p

            @pl.when(k > 0)
            def _():
                acc_ref[...] += p

            @pl.when(k == nk - 1)
            def _():
                finish(acc_ref[...])

    ins = [a, b] + ([res] if has_res else [])
    in_specs = [a_spec, b_spec] + ([o_spec] if has_res else [])
    est = tm * tk * a.dtype.itemsize + tk * tn * b.dtype.itemsize + tm * tn * (jnp.dtype(out_dtype).itemsize + 8)
    return pl.pallas_call(
        kern, grid=(M // tm, N // tn, nk), in_specs=in_specs, out_specs=o_spec,
        out_shape=jax.ShapeDtypeStruct((M, N), out_dtype),
        scratch_shapes=[pltpu.VMEM((tm, tn), F32)] if nk > 1 else [],
        name=name,
        compiler_params=pltpu.CompilerParams(dimension_semantics=("parallel", "parallel", "arbitrary"),
                                             vmem_limit_bytes=VMEM_CAP_BYTES),
    )(*ins)


def _ffn_in_swiglu(name, h, w3, tm, tn):
    T, D = h.shape
    dff = w3.shape[2] // 2
    tm = min(tm, T)
    assert T % tm == 0 and dff % tn == 0
    nj = dff // tn

    def kern(h_ref, wg_ref, wu_ref, zg_ref, zu_ref, a_ref):
        hb = h_ref[...]
        g = jnp.dot(hb, wg_ref[...], preferred_element_type=F32).astype(BF)
        u = jnp.dot(hb, wu_ref[...], preferred_element_type=F32).astype(BF)
        zg_ref[...] = g
        zu_ref[...] = u
        a_ref[...] = (_silu(g.astype(F32)) * u.astype(F32)).astype(BF)

    o_spec = pl.BlockSpec((tm, tn), lambda i, j: (i, j))
    return pl.pallas_call(
        kern, grid=(T // tm, nj),
        in_specs=[pl.BlockSpec((tm, D), lambda i, j: (i, 0)),
                  pl.BlockSpec((None, D, tn), lambda i, j: (0, 0, j)),
                  pl.BlockSpec((None, D, tn), lambda i, j: (0, 0, j + nj))],
        out_specs=[o_spec, o_spec, o_spec],
        out_shape=[jax.ShapeDtypeStruct((T, dff), BF)] * 3, name=name,
        compiler_params=pltpu.CompilerParams(dimension_semantics=("parallel", "arbitrary"),
                                             vmem_limit_bytes=VMEM_CAP_BYTES),
    )(h, w3, w3)


def _ffn_da_swiglu(name, dxo, w3, zg, zu, tm):
    T, D = dxo.shape
    dff = w3.shape[1]
    tm = min(tm, T)
    assert T % tm == 0 and dff % 2 == 0
    hc = dff // 2

    def kern(d_ref, w_ref, g_ref, u_ref, dz_ref):
        db = d_ref[...].astype(BF)
        for s in range(2):
            cols = slice(s * hc, (s + 1) * hc)
            da = lax.dot_general(db, w_ref[cols, :], (((1,), (1,)), ((), ())), preferred_element_type=F32) * 0.5
            da = da.astype(BF).astype(F32)
            _, vjp = jax.vjp(lambda p, q: _silu(p) * q, g_ref[:, cols].astype(F32), u_ref[:, cols].astype(F32))
            dg, du = vjp(da)
            dz_ref[:, cols] = dg.astype(dz_ref.dtype)
            dz_ref[:, dff + s * hc:dff + (s + 1) * hc] = du.astype(dz_ref.dtype)

    row = lambda w: pl.BlockSpec((tm, w), lambda i: (i, 0))
    return pl.pallas_call(
        kern, grid=(T // tm,),
        in_specs=[row(D), pl.BlockSpec((None, dff, D), lambda i: (0, 0, 0), pipeline_mode=pl.Buffered(1)), row(dff), row(dff)],
        out_specs=row(2 * dff), out_shape=jax.ShapeDtypeStruct((T, 2 * dff), BF), name=name,
        compiler_params=pltpu.CompilerParams(dimension_semantics=("arbitrary",), vmem_limit_bytes=VMEM_CAP_BYTES),
    )(dxo, w3, zg, zu)


def _mm_dh_rms(name, dz, w3, xin, g, dres, tm):
    T, K = dz.shape
    D = w3.shape[1]
    tm = min(tm, T)
    assert T % tm == 0

    def kern(dz_ref, w_ref, x_ref, g_ref, r_ref, dx_ref, dg_ref):
        dh = lax.dot_general(dz_ref[...], w_ref[...], (((1,), (1,)), ((), ())), preferred_element_type=F32)
        _, vjp = jax.vjp(_rmsnorm, x_ref[...], g_ref[...])
        dx, dg = vjp(dh)
        dx_ref[...] = dx + r_ref[...]

        @pl.when(pl.program_id(0) == 0)
        def _():
            dg_ref[...] = jnp.zeros(dg_ref.shape, F32)

        dg_ref[...] += dg

    row = lambda w: pl.BlockSpec((tm, w), lambda i: (i, 0))
    one = pl.BlockSpec((1, D), lambda i: (0, 0))
    return pl.pallas_call(
        kern, grid=(T // tm,),
        in_specs=[row(K), pl.BlockSpec((None, D, K), lambda i: (0, 0, 0), pipeline_mode=pl.Buffered(1)), row(D), one, row(D)],
        out_specs=[row(D), one], out_shape=[jax.ShapeDtypeStruct((T, D), F32), jax.ShapeDtypeStruct((1, D), F32)], name=name,
        compiler_params=pltpu.CompilerParams(dimension_semantics=("arbitrary",), vmem_limit_bytes=VMEM_CAP_BYTES),
    )(dz, w3, xin, g, dres)


def _mm_tn_pair(name, a, b, kind, c_arr, tq, tk, scale=1.0):
    T, M = a.shape
    _, N = b.shape
    tk = min(tk, T)
    assert T % tk == 0
    nk = T // tk
    if kind == "col":
        hm = M // 2
        assert N % tq == 0
        nq = N // tq
        tile = (hm, tq)
        a_spec = pl.BlockSpec((tk, hm), lambda h, q, k, c: (k, jnp.bitwise_xor(h, 1 - c[0])))
        b_spec = pl.BlockSpec((tk, tq), lambda h, q, k, c: (k, q))
        o_spec = pl.BlockSpec(tile, lambda h, q, k, c: (0, q * h))
        out_sd = (hm, N)
    else:
        hn = N // 2
        assert M % tq == 0
        nq = M // tq
        tile = (tq, hn)
        a_spec = pl.BlockSpec((tk, tq), lambda h, q, k, c: (k, q))
        b_spec = pl.BlockSpec((tk, hn), lambda h, q, k, c: (k, jnp.bitwise_xor(h, 1 - c[0])))
        o_spec = pl.BlockSpec(tile, lambda h, q, k, c: (q * h, 0))
        out_sd = (M, hn)

    def kern(c_ref, a_ref, b_ref, o_ref, acc, stage, recv, ssem, rsem):
        h, q, k = pl.program_id(0), pl.program_id(1), pl.program_id(2)
        x, y, c, _ = _place()
        p = lax.dot_general(a_ref[...].astype(BF), b_ref[...].astype(BF), (((0,), (0,)), ((), ())), preferred_element_type=F32)

        @pl.when(k == 0)
        def _():
            acc[...] = p

        @pl.when(k > 0)
        def _():
            acc[...] += p

        def send(slot, qq):
            return pltpu.make_async_remote_copy(src_ref=stage.at[slot], dst_ref=recv.at[qq], send_sem=ssem.at[slot],
                                                recv_sem=rsem.at[qq], device_id=(x, y, 1 - c), device_id_type=MESH)

        last = k == nk - 1

        @pl.when(jnp.logical_and(last, h == 0))
        def _():
            slot = q % 2

            @pl.when(q >= 2)
            def _():
                send(slot, q).wait_send()

            stage[slot] = (acc[...] * scale).astype(BF)
            send(slot, q).start()

        @pl.when(jnp.logical_and(last, h == 1))
        def _():
            @pl.when(q == 0)
            def _():
                for s in range(min(nq, 2)):
                    send(s, 0).wait_send()

            send(0, q).wait_recv()
            o_ref[...] = (acc[...] * scale + recv[q].astype(F32)).astype(o_ref.dtype)

    tb = tile[0] * tile[1]
    est = tb * (4 + 2 * 2 + nq * 2 + 2 * 2) + 2 * tk * (a_spec.block_shape[1] + b_spec.block_shape[1]) * 2 * 2
    return pl.pallas_call(
        kern,
        grid_spec=pltpu.PrefetchScalarGridSpec(
            num_scalar_prefetch=1, grid=(2, nq, nk), in_specs=[a_spec, b_spec], out_specs=o_spec,
            scratch_shapes=[pltpu.VMEM(tile, F32), pltpu.VMEM((2,) + tile, BF), pltpu.VMEM((nq,) + tile, BF),
                            pltpu.SemaphoreType.DMA((2,)), pltpu.SemaphoreType.DMA((nq,))]),
        out_shape=jax.ShapeDtypeStruct(out_sd, BF), name=name,
        compiler_params=pltpu.CompilerParams(dimension_semantics=("arbitrary", "arbitrary", "arbitrary"),
                                             vmem_limit_bytes=VMEM_CAP_BYTES),
    )(c_arr, a, b)


def _hgrn_pieces(z_ref):
    W = HG_HEADS * HG_DIM
    zq = [z_ref[:, h * HG_DIM:(h + 1) * HG_DIM] for h in range(HG_HEADS)]
    zf = [z_ref[:, W + h * HG_DIM:W + (h + 1) * HG_DIM] for h in range(HG_HEADS)]
    zi = [z_ref[:, 2 * W + h * HG_DIM:2 * W + (h + 1) * HG_DIM] for h in range(HG_HEADS)]
    zg = [z_ref[:, 3 * W + h * HG_DIM:3 * W + (h + 1) * HG_DIM] for h in range(HG_HEADS)]
    zx = [z_ref[:, 4 * W + a * XA_DIM:4 * W + (a + 1) * XA_DIM] for a in range(XA_HEADS)]
    return zq, zf, zi, zg, zx


def _kv_pieces(kv_ref):
    W = XA_HEADS * XA_DIM
    mk = [kv_ref[:, a * XA_DIM:(a + 1) * XA_DIM] for a in range(XA_HEADS)]
    mv = [kv_ref[:, W + a * XA_DIM:W + (a + 1) * XA_DIM] for a in range(XA_HEADS)]
    return mk, mv


def _lb_pieces(lb_ref):
    return [[lb_ref[r:r + 1, h * HG_DIM:(h + 1) * HG_DIM] for h in range(HG_HEADS)] for r in range(3)]


def _hgrn_fwd(z, lb_logits, gnorm, kv, bl, nc):
    T, zw = z.shape
    mem_len = kv.shape[0] // bl
    cat_w = HG_HEADS * HG_DIM + XA_HEADS * XA_DIM

    def kern(z_ref, lb_ref, gn_ref, kv_ref, cat_ref, st_ref, s_scr):
        @pl.when(pl.program_id(1) == 0)
        def _():
            s_scr[...] = jnp.zeros(s_scr.shape, F32)

        st_ref[...] = s_scr[...]
        zq, zf, zi, zg, zx = _hgrn_pieces(z_ref)
        mk, mv = _kv_pieces(kv_ref)
        l0, l1, l2 = _lb_pieces(lb_ref)
        S = [s_scr[h] for h in range(HG_HEADS)]
        outs, s_new = _hgrn_block(zq, zf, zi, zg, zx, l0, l1, l2, gn_ref[...], mk, mv, S)
        for h in range(HG_HEADS):
            cat_ref[:, h * HG_DIM:(h + 1) * HG_DIM] = outs[h].astype(cat_ref.dtype)
            s_scr[h] = s_new[h]
        base = HG_HEADS * HG_DIM
        for a in range(XA_HEADS):
            cat_ref[:, base + a * XA_DIM:base + (a + 1) * XA_DIM] = outs[HG_HEADS + a].astype(cat_ref.dtype)

    return pl.pallas_call(
        kern, grid=(bl, nc),
        in_specs=[pl.BlockSpec((HG_CHUNK, zw), lambda b, n: (b * nc + n, 0)),
                  pl.BlockSpec(lb_logits.shape, lambda b, n: (0, 0)),
                  pl.BlockSpec(gnorm.shape, lambda b, n: (0, 0)),
                  pl.BlockSpec((mem_len, kv.shape[1]), lambda b, n: (b, 0))],
        out_specs=[pl.BlockSpec((HG_CHUNK, cat_w), lambda b, n: (b * nc + n, 0)),
                   pl.BlockSpec((None, HG_HEADS, HG_DIM, HG_DIM), lambda b, n: (b * nc + n, 0, 0, 0))],
        out_shape=[jax.ShapeDtypeStruct((T, cat_w), BF),
                   jax.ShapeDtypeStruct((bl * nc, HG_HEADS, HG_DIM, HG_DIM), F32)],
        scratch_shapes=[pltpu.VMEM((HG_HEADS, HG_DIM, HG_DIM), F32)],
        name="hgrn_fwd",
        compiler_params=pltpu.CompilerParams(dimension_semantics=("arbitrary", "arbitrary"), vmem_limit_bytes=VMEM_CAP_BYTES),
    )(z, lb_logits, gnorm, kv)


def _hgrn_bwd(z, dcat, stash, lb_logits, gnorm, kv, bl, nc):
    T, zw = z.shape
    mem_len = kv.shape[0] // bl
    cat_w = dcat.shape[1]

    def kern(z_ref, dc_ref, st_ref, lb_ref, gn_ref, kv_ref, dz_ref, dkv_ref, dlb_ref, dgn_ref, ds_scr):
        first = jnp.logical_and(pl.program_id(0) == 0, pl.program_id(1) == 0)

        @pl.when(pl.program_id(1) == 0)
        def _():
            ds_scr[...] = jnp.zeros(ds_scr.shape, F32)
            dkv_ref[...] = jnp.zeros(dkv_ref.shape, F32)

        @pl.when(first)
        def _():
            dlb_ref[...] = jnp.zeros(dlb_ref.shape, F32)
            dgn_ref[...] = jnp.zeros(dgn_ref.shape, F32)

        zq, zf, zi, zg, zx = _hgrn_pieces(z_ref)
        mk, mv = _kv_pieces(kv_ref)
        l0, l1, l2 = _lb_pieces(lb_ref)
        S = [st_ref[h] for h in range(HG_HEADS)]
        _, vjp = jax.vjp(_hgrn_block, zq, zf, zi, zg, zx, l0, l1, l2, gn_ref[...], mk, mv, S)
        d_outs = [dc_ref[:, h * HG_DIM:(h + 1) * HG_DIM] for h in range(HG_HEADS)]
        base = HG_HEADS * HG_DIM
        d_outs += [dc_ref[:, base + a * XA_DIM:base + (a + 1) * XA_DIM] for a in range(XA_HEADS)]
        d_s = [ds_scr[h] for h in range(HG_HEADS)]
        dzq, dzf, dzi, dzg, dzx, dl0, dl1, dl2, dgn, dmk, dmv, dS = vjp((d_outs, d_s))
        W = HG_HEADS * HG_DIM
        for h in range(HG_HEADS):
            sl = slice(h * HG_DIM, (h + 1) * HG_DIM)
            dz_ref[:, sl] = dzq[h].astype(dz_ref.dtype)
            dz_ref[:, W + h * HG_DIM:W + (h + 1) * HG_DIM] = dzf[h].astype(dz_ref.dtype)
            dz_ref[:, 2 * W + h * HG_DIM:2 * W + (h + 1) * HG_DIM] = dzi[h].astype(dz_ref.dtype)
            dz_ref[:, 3 * W + h * HG_DIM:3 * W + (h + 1) * HG_DIM] = dzg[h].astype(dz_ref.dtype)
            ds_scr[h] = dS[h]
            dlb_ref[0:1, sl] += dl0[h]
            dlb_ref[1:2, sl] += dl1[h]
            dlb_ref[2:3, sl] += dl2[h]
        dgn_ref[...] += dgn
        KW = XA_HEADS * XA_DIM
        for a in range(XA_HEADS):
            dz_ref[:, 4 * W + a * XA_DIM:4 * W + (a + 1) * XA_DIM] = dzx[a].astype(dz_ref.dtype)
            dkv_ref[:, a * XA_DIM:(a + 1) * XA_DIM] += dmk[a]
            dkv_ref[:, KW + a * XA_DIM:KW + (a + 1) * XA_DIM] += dmv[a]

    rev = lambda b, n: (b * nc + (nc - 1 - n), 0)
    return pl.pallas_call(
        kern, grid=(bl, nc),
        in_specs=[pl.BlockSpec((HG_CHUNK, zw), rev),
                  pl.BlockSpec((HG_CHUNK, cat_w), rev),
                  pl.BlockSpec((None, HG_HEADS, HG_DIM, HG_DIM), lambda b, n: (b * nc + (nc - 1 - n), 0, 0, 0)),
                  pl.BlockSpec(lb_logits.shape, lambda b, n: (0, 0)),
                  pl.BlockSpec(gnorm.shape, lambda b, n: (0, 0)),
                  pl.BlockSpec((mem_len, kv.shape[1]), lambda b, n: (b, 0))],
        out_specs=[pl.BlockSpec((HG_CHUNK, zw), rev),
                   pl.BlockSpec((mem_len, kv.shape[1]), lambda b, n: (b, 0)),
                   pl.BlockSpec(lb_logits.shape, lambda b, n: (0, 0)),
                   pl.BlockSpec(gnorm.shape, lambda b, n: (0, 0))],
        out_shape=[jax.ShapeDtypeStruct((T, zw), BF), jax.ShapeDtypeStruct(kv.shape, F32),
                   jax.ShapeDtypeStruct(lb_logits.shape, F32), jax.ShapeDtypeStruct(gnorm.shape, F32)],
        scratch_shapes=[pltpu.VMEM((HG_HEADS, HG_DIM, HG_DIM), F32)],
        name="hgrn_bwd",
        compiler_params=pltpu.CompilerParams(dimension_semantics=("arbitrary", "arbitrary"), vmem_limit_bytes=VMEM_CAP_BYTES),
    )(z, dcat, stash, lb_logits, gnorm, kv)


HG_SUB = 4


def _hgrn_rows(z_ref, dtype_cast=None):
    W = HG_HEADS * HG_DIM

    def piece(c, col, w):
        return z_ref[c * HG_CHUNK:(c + 1) * HG_CHUNK, col:col + w]

    zq = [[piece(c, h * HG_DIM, HG_DIM) for h in range(HG_HEADS)] for c in range(HG_SUB)]
    zf = [[piece(c, W + h * HG_DIM, HG_DIM) for h in range(HG_HEADS)] for c in range(HG_SUB)]
    zi = [[piece(c, 2 * W + h * HG_DIM, HG_DIM) for h in range(HG_HEADS)] for c in range(HG_SUB)]
    zg = [[piece(c, 3 * W + h * HG_DIM, HG_DIM) for h in range(HG_HEADS)] for c in range(HG_SUB)]
    zx = [z_ref[:, 4 * W + a * XA_DIM:4 * W + (a + 1) * XA_DIM] for a in range(XA_HEADS)]
    return zq, zf, zi, zg, zx


def _hgrn_steps(zq, zf, zi, zg, zx, l0, l1, l2, gn, mk, mv, S):
    mix = []
    for c in range(HG_SUB):
        row, s_next = [], []
        for h in range(HG_HEADS):
            o, sn = _hgrn_head(zq[c][h], zf[c][h], zi[c][h], zg[c][h], l0[h], l1[h], l2[h], gn, S[h])
            row.append(o)
            s_next.append(sn)
        mix.append(row)
        S = s_next
    att = [_attention(zx[a], mk[a], mv[a]) for a in range(XA_HEADS)]
    return mix, att, S


def _hgrn_fwd2(z, lb_logits, gnorm, kv, bl, seq):
    T, zw = z.shape
    mem_len = kv.shape[0] // bl
    cat_w = HG_HEADS * HG_DIM + XA_HEADS * XA_DIM
    R = HG_SUB * HG_CHUNK
    nb = seq // R

    def kern(z_ref, lb_ref, gn_ref, kv_ref, cat_ref, st_ref, s_scr):
        @pl.when(pl.program_id(1) == 0)
        def _():
            s_scr[...] = jnp.zeros(s_scr.shape, F32)

        st_ref[...] = s_scr[...]
        zq, zf, zi, zg, zx = _hgrn_rows(z_ref)
        mk, mv = _kv_pieces(kv_ref)
        l0, l1, l2 = _lb_pieces(lb_ref)
        S = [s_scr[h] for h in range(HG_HEADS)]
        mix, att, s_new = _hgrn_steps(zq, zf, zi, zg, zx, l0, l1, l2, gn_ref[...], mk, mv, S)
        for c in range(HG_SUB):
            for h in range(HG_HEADS):
                cat_ref[c * HG_CHUNK:(c + 1) * HG_CHUNK, h * HG_DIM:(h + 1) * HG_DIM] = mix[c][h].astype(cat_ref.dtype)
        for h in range(HG_HEADS):
            s_scr[h] = s_new[h]
        base = HG_HEADS * HG_DIM
        for a in range(XA_HEADS):
            cat_ref[:, base + a * XA_DIM:base + (a + 1) * XA_DIM] = att[a].astype(cat_ref.dtype)

    return pl.pallas_call(
        kern, grid=(bl, nb),
        in_specs=[pl.BlockSpec((R, zw), lambda b, n: (b * nb + n, 0)),
                  pl.BlockSpec(lb_logits.shape, lambda b, n: (0, 0)),
                  pl.BlockSpec(gnorm.shape, lambda b, n: (0, 0)),
                  pl.BlockSpec((mem_len, kv.shape[1]), lambda b, n: (b, 0))],
        out_specs=[pl.BlockSpec((R, cat_w), lambda b, n: (b * nb + n, 0)),
                   pl.BlockSpec((None, HG_HEADS, HG_DIM, HG_DIM), lambda b, n: (b * nb + n, 0, 0, 0))],
        out_shape=[jax.ShapeDtypeStruct((T, cat_w), BF),
                   jax.ShapeDtypeStruct((bl * nb, HG_HEADS, HG_DIM, HG_DIM), F32)],
        scratch_shapes=[pltpu.VMEM((HG_HEADS, HG_DIM, HG_DIM), F32)],
        name="hgrn_fwd",
        compiler_params=pltpu.CompilerParams(dimension_semantics=("arbitrary", "arbitrary"), vmem_limit_bytes=VMEM_CAP_BYTES),
    )(z, lb_logits, gnorm, kv)


def _hgrn_bwd2(z, dcat, stash, lb_logits, gnorm, kv, bl, seq):
    T, zw = z.shape
    mem_len = kv.shape[0] // bl
    cat_w = dcat.shape[1]
    R = HG_SUB * HG_CHUNK
    nb = seq // R

    def kern(z_ref, dc_ref, st_ref, lb_ref, gn_ref, kv_ref, dz_ref, dkv_ref, dlb_ref, dgn_ref, ds_scr):
        first = jnp.logical_and(pl.program_id(0) == 0, pl.program_id(1) == 0)

        @pl.when(pl.program_id(1) == 0)
        def _():
            ds_scr[...] = jnp.zeros(ds_scr.shape, F32)
            dkv_ref[...] = jnp.zeros(dkv_ref.shape, F32)

        @pl.when(first)
        def _():
            dlb_ref[...] = jnp.zeros(dlb_ref.shape, F32)
            dgn_ref[...] = jnp.zeros(dgn_ref.shape, F32)

        zq, zf, zi, zg, zx = _hgrn_rows(z_ref)
        mk, mv = _kv_pieces(kv_ref)
        l0, l1, l2 = _lb_pieces(lb_ref)
        S = [st_ref[h] for h in range(HG_HEADS)]
        _, vjp = jax.vjp(_hgrn_steps, zq, zf, zi, zg, zx, l0, l1, l2, gn_ref[...], mk, mv, S)
        d_mix = [[dc_ref[c * HG_CHUNK:(c + 1) * HG_CHUNK, h * HG_DIM:(h + 1) * HG_DIM] for h in range(HG_HEADS)]
                 for c in range(HG_SUB)]
        base = HG_HEADS * HG_DIM
        d_att = [dc_ref[:, base + a * XA_DIM:base + (a + 1) * XA_DIM] for a in range(XA_HEADS)]
        d_s = [ds_scr[h] for h in range(HG_HEADS)]
        dzq, dzf, dzi, dzg, dzx, dl0, dl1, dl2, dgn, dmk, dmv, dS = vjp((d_mix, d_att, d_s))
        W = HG_HEADS * HG_DIM
        for c in range(HG_SUB):
            rows = slice(c * HG_CHUNK, (c + 1) * HG_CHUNK)
            for h in range(HG_HEADS):
                for k, part in enumerate((dzq, dzf, dzi, dzg)):
                    dz_ref[rows, k * W + h * HG_DIM:k * W + (h + 1) * HG_DIM] = part[c][h].astype(dz_ref.dtype)
        for h in range(HG_HEADS):
            sl = slice(h * HG_DIM, (h + 1) * HG_DIM)
            ds_scr[h] = dS[h]
            dlb_ref[0:1, sl] += dl0[h]
            dlb_ref[1:2, sl] += dl1[h]
            dlb_ref[2:3, sl] += dl2[h]
        dgn_ref[...] += dgn
        KW = XA_HEADS * XA_DIM
        for a in range(XA_HEADS):
            dz_ref[:, 4 * W + a * XA_DIM:4 * W + (a + 1) * XA_DIM] = dzx[a].astype(dz_ref.dtype)
            dkv_ref[:, a * XA_DIM:(a + 1) * XA_DIM] += dmk[a]
            dkv_ref[:, KW + a * XA_DIM:KW + (a + 1) * XA_DIM] += dmv[a]

    rev = lambda b, n: (b * nb + (nb - 1 - n), 0)
    return pl.pallas_call(
        kern, grid=(bl, nb),
        in_specs=[pl.BlockSpec((R, zw), rev),
                  pl.BlockSpec((R, cat_w), rev),
                  pl.BlockSpec((None, HG_HEADS, HG_DIM, HG_DIM), lambda b, n: (b * nb + (nb - 1 - n), 0, 0, 0)),
                  pl.BlockSpec(lb_logits.shape, lambda b, n: (0, 0)),
                  pl.BlockSpec(gnorm.shape, lambda b, n: (0, 0)),
                  pl.BlockSpec((mem_len, kv.shape[1]), lambda b, n: (b, 0))],
        out_specs=[pl.BlockSpec((R, zw), rev),
                   pl.BlockSpec((mem_len, kv.shape[1]), lambda b, n: (b, 0)),
                   pl.BlockSpec(lb_logits.shape, lambda b, n: (0, 0)),
                   pl.BlockSpec(gnorm.shape, lambda b, n: (0, 0))],
        out_shape=[jax.ShapeDtypeStruct((T, zw), BF), jax.ShapeDtypeStruct(kv.shape, F32),
                   jax.ShapeDtypeStruct(lb_logits.shape, F32), jax.ShapeDtypeStruct(gnorm.shape, F32)],
        scratch_shapes=[pltpu.VMEM((HG_HEADS, HG_DIM, HG_DIM), F32)],
        name="hgrn_bwd",
        compiler_params=pltpu.CompilerParams(dimension_semantics=("arbitrary", "arbitrary"), vmem_limit_bytes=VMEM_CAP_BYTES),
    )(z, dcat, stash, lb_logits, gnorm, kv)


def _gmlp_pieces(z_ref):
    W = GM_GROUPS * GM_GROUP_DIM
    zu = [z_ref[:, g * GM_GROUP_DIM:(g + 1) * GM_GROUP_DIM] for g in range(GM_GROUPS)]
    zv = [z_ref[:, W + g * GM_GROUP_DIM:W + (g + 1) * GM_GROUP_DIM] for g in range(GM_GROUPS)]
    zx = [z_ref[:, 2 * W + a * XA_DIM:2 * W + (a + 1) * XA_DIM] for a in range(XA_HEADS)]
    return zu, zv, zx


def _gmlp_params(lng_ref, lnb_ref, ws_ref, bs_ref):
    lng = [lng_ref[:, g * GM_GROUP_DIM:(g + 1) * GM_GROUP_DIM] for g in range(GM_GROUPS)]
    lnb = [lnb_ref[:, g * GM_GROUP_DIM:(g + 1) * GM_GROUP_DIM] for g in range(GM_GROUPS)]
    ws = [ws_ref[g] for g in range(GM_GROUPS)]
    bs = [bs_ref[g:g + 1, :] for g in range(GM_GROUPS)]
    return lng, lnb, ws, bs


def _gmlp_fwd(z, ln_g, ln_b, w_s, b_s, kv, bl, nc):
    T, zw = z.shape
    mem_len = kv.shape[0] // bl
    cat_w = GM_GROUPS * GM_GROUP_DIM + XA_HEADS * XA_DIM

    def kern(z_ref, lng_ref, lnb_ref, ws_ref, bs_ref, kv_ref, cat_ref):
        zu, zv, zx = _gmlp_pieces(z_ref)
        lng, lnb, ws, bs = _gmlp_params(lng_ref, lnb_ref, ws_ref, bs_ref)
        mk, mv = _kv_pieces(kv_ref)
        outs = _gmlp_block(zu, zv, zx, lng, lnb, ws, bs, mk, mv)
        for g in range(GM_GROUPS):
            cat_ref[:, g * GM_GROUP_DIM:(g + 1) * GM_GROUP_DIM] = outs[g].astype(cat_ref.dtype)
        base = GM_GROUPS * GM_GROUP_DIM
        for a in range(XA_HEADS):
            cat_ref[:, base + a * XA_DIM:base + (a + 1) * XA_DIM] = outs[GM_GROUPS + a].astype(cat_ref.dtype)

    full2 = lambda b, n: (0, 0)
    return pl.pallas_call(
        kern, grid=(bl, nc),
        in_specs=[pl.BlockSpec((GM_CHUNK, zw), lambda b, n: (b * nc + n, 0)),
                  pl.BlockSpec(ln_g.shape, full2), pl.BlockSpec(ln_b.shape, full2),
                  pl.BlockSpec(w_s.shape, lambda b, n: (0, 0, 0)), pl.BlockSpec(b_s.shape, full2),
                  pl.BlockSpec((mem_len, kv.shape[1]), lambda b, n: (b, 0))],
        out_specs=pl.BlockSpec((GM_CHUNK, cat_w), lambda b, n: (b * nc + n, 0)),
        out_shape=jax.ShapeDtypeStruct((T, cat_w), BF),
        name="gmlp_fwd",
        compiler_params=pltpu.CompilerParams(dimension_semantics=("arbitrary", "arbitrary"), vmem_limit_bytes=VMEM_CAP_BYTES),
    )(z, ln_g, ln_b, w_s, b_s, kv)


def _gmlp_bwd(z, dcat, ln_g, ln_b, w_s, b_s, kv, bl, nc):
    T, zw = z.shape
    mem_len = kv.shape[0] // bl
    cat_w = dcat.shape[1]

    def kern(z_ref, dc_ref, lng_ref, lnb_ref, ws_ref, bs_ref, kv_ref,
             dz_ref, dkv_ref, dlng_ref, dlnb_ref, dws_ref, dbs_ref):
        first = jnp.logical_and(pl.program_id(0) == 0, pl.program_id(1) == 0)

        @pl.when(pl.program_id(1) == 0)
        def _():
            dkv_ref[...] = jnp.zeros(dkv_ref.shape, F32)

        @pl.when(first)
        def _():
            dlng_ref[...] = jnp.zeros(dlng_ref.shape, F32)
            dlnb_ref[...] = jnp.zeros(dlnb_ref.shape, F32)
            dws_ref[...] = jnp.zeros(dws_ref.shape, F32)
            dbs_ref[...] = jnp.zeros(dbs_ref.shape, F32)

        zu, zv, zx = _gmlp_pieces(z_ref)
        lng, lnb, ws, bs = _gmlp_params(lng_ref, lnb_ref, ws_ref, bs_ref)
        mk, mv = _kv_pieces(kv_ref)
        _, vjp = jax.vjp(_gmlp_block, zu, zv, zx, lng, lnb, ws, bs, mk, mv)
        d_outs = [dc_ref[:, g * GM_GROUP_DIM:(g + 1) * GM_GROUP_DIM] for g in range(GM_GROUPS)]
        base = GM_GROUPS * GM_GROUP_DIM
        d_outs += [dc_ref[:, base + a * XA_DIM:base + (a + 1) * XA_DIM] for a in range(XA_HEADS)]
        dzu, dzv, dzx, dlng, dlnb, dws, dbs, dmk, dmv = vjp(d_outs)
        W = GM_GROUPS * GM_GROUP_DIM
        for g in range(GM_GROUPS):
            sl = slice(g * GM_GROUP_DIM, (g + 1) * GM_GROUP_DIM)
            dz_ref[:, sl] = dzu[g].astype(dz_ref.dtype)
            dz_ref[:, W + g * GM_GROUP_DIM:W + (g + 1) * GM_GROUP_DIM] = dzv[g].astype(dz_ref.dtype)
            dlng_ref[:, sl] += dlng[g]
            dlnb_ref[:, sl] += dlnb[g]
            dws_ref[g] += dws[g]
            dbs_ref[g:g + 1, :] += dbs[g]
        KW = XA_HEADS * XA_DIM
        for a in range(XA_HEADS):
            dz_ref[:, 2 * W + a * XA_DIM:2 * W + (a + 1) * XA_DIM] = dzx[a].astype(dz_ref.dtype)
            dkv_ref[:, a * XA_DIM:(a + 1) * XA_DIM] += dmk[a]
            dkv_ref[:, KW + a * XA_DIM:KW + (a + 1) * XA_DIM] += dmv[a]

    full2 = lambda b, n: (0, 0)
    full3 = lambda b, n: (0, 0, 0)
    blk = lambda b, n: (b * nc + n, 0)
    return pl.pallas_call(
        kern, grid=(bl, nc),
        in_specs=[pl.BlockSpec((GM_CHUNK, zw), blk), pl.BlockSpec((GM_CHUNK, cat_w), blk),
                  pl.BlockSpec(ln_g.shape, full2), pl.BlockSpec(ln_b.shape, full2),
                  pl.BlockSpec(w_s.shape, full3), pl.BlockSpec(b_s.shape, full2),
                  pl.BlockSpec((mem_len, kv.shape[1]), lambda b, n: (b, 0))],
        out_specs=[pl.BlockSpec((GM_CHUNK, zw), blk),
                   pl.BlockSpec((mem_len, kv.shape[1]), lambda b, n: (b, 0)),
                   pl.BlockSpec(ln_g.shape, full2), pl.BlockSpec(ln_b.shape, full2),
                   pl.BlockSpec(w_s.shape, full3), pl.BlockSpec(b_s.shape, full2)],
        out_shape=[jax.ShapeDtypeStruct((T, zw), BF), jax.ShapeDtypeStruct(kv.shape, F32),
                   jax.ShapeDtypeStruct(ln_g.shape, F32), jax.ShapeDtypeStruct(ln_b.shape, F32),
                   jax.ShapeDtypeStruct(w_s.shape, F32), jax.ShapeDtypeStruct(b_s.shape, F32)],
        name="gmlp_bwd",
        compiler_params=pltpu.CompilerParams(dimension_semantics=("arbitrary", "arbitrary"), vmem_limit_bytes=VMEM_CAP_BYTES),
    )(z, dcat, ln_g, ln_b, w_s, b_s, kv)


def _place():
    x, y, c = lax.axis_index("x"), lax.axis_index("y"), lax.axis_index("c")
    chips = [(1 - x, y), (x, 1 - y), (1 - x, 1 - y)]
    return x, y, c, chips


def _half(ref, kind, e):
    if kind == "col":
        n = ref.shape[1] // 2
        return ref.at[:, pl.ds(pl.multiple_of(e * n, n), n), :]
    n = ref.shape[2] // 2
    return ref.at[:, :, pl.ds(pl.multiple_of(e * n, n), n)]


def _slot(ref, kind, j, n):
    if kind == "col":
        return ref.at[:, :, pl.ds(pl.multiple_of(j * n, n), n)]
    return ref.at[:, pl.ds(pl.multiple_of(j * n, n), n), :]


def _allgather_seq(name, items, cid):
    nt = len(items)
    kinds = [k for (_, k, _) in items]
    out_type = []
    for s, k, l in items:
        L, r, c = s.shape
        lo = L if l is None else 1
        out_type.append(jax.ShapeDtypeStruct((lo, r, 4 * c) if k == "col" else (lo, 4 * r, c), s.dtype))

    def body(*refs):
        sh = [refs[t] if items[t][2] is None else refs[t].at[pl.ds(items[t][2], 1)] for t in range(nt)]
        full = refs[nt:2 * nt]
        loc, s_ici, r_ici, s_d2d, r_d2d = refs[2 * nt:]
        x, y, c, chips = _place()
        own = 2 * x + y
        sibling = (x, y, 1 - c)
        barrier = pltpu.get_barrier_semaphore()
        for peer in [(px, py, c) for (px, py) in chips] + [sibling]:
            pl.semaphore_signal(barrier, inc=1, device_id=peer, device_id_type=MESH)
        pl.semaphore_wait(barrier, 4)
        width = [sh[t].shape[2] if kinds[t] == "col" else sh[t].shape[1] for t in range(nt)]
        started = []
        for t in range(nt):
            mine = pltpu.make_async_copy(sh[t], _slot(full[t], kinds[t], own, width[t]), loc.at[t])
            mine.start()
            started.append(mine)
        sent = []
        for t in range(nt):
            for p, (px, py) in enumerate(chips):
                cp = pltpu.make_async_remote_copy(
                    src_ref=_half(sh[t], kinds[t], c), dst_ref=_half(_slot(full[t], kinds[t], own, width[t]), kinds[t], c),
                    send_sem=s_ici.at[t, p], recv_sem=r_ici.at[t, p], device_id=(px, py, c), device_id_type=MESH)
                cp.start()
                sent.append(cp)
        for t in range(nt):
            for p, (px, py) in enumerate(chips):
                landed = _half(_slot(full[t], kinds[t], 2 * px + py, width[t]), kinds[t], c)
                pltpu.make_async_remote_copy(
                    src_ref=landed, dst_ref=landed, send_sem=s_ici.at[t, p], recv_sem=r_ici.at[t, p],
                    device_id=(px, py, c), device_id_type=MESH).wait_recv()
                fw = pltpu.make_async_remote_copy(
                    src_ref=landed, dst_ref=landed, send_sem=s_d2d.at[t, p], recv_sem=r_d2d.at[t, p],
                    device_id=sibling, device_id_type=MESH)
                fw.start()
                sent.append(fw)
        for t in range(nt):
            for p, (px, py) in enumerate(chips):
                other = _half(_slot(full[t], kinds[t], 2 * px + py, width[t]), kinds[t], 1 - c)
                pltpu.make_async_remote_copy(
                    src_ref=other, dst_ref=other, send_sem=s_d2d.at[t, p], recv_sem=r_d2d.at[t, p],
                    device_id=sibling, device_id_type=MESH).wait_recv()
        for cp in sent:
            cp.wait_send()
        for cp in started:
            cp.wait()

    return pl.kernel(
        body, out_type=out_type, mesh=plsc.ScalarSubcoreMesh(axis_name="seq", num_cores=1),
        scratch_types=[pltpu.SemaphoreType.DMA((nt,)), pltpu.SemaphoreType.DMA((nt, 3)), pltpu.SemaphoreType.DMA((nt, 3)),
                       pltpu.SemaphoreType.DMA((nt, 3)), pltpu.SemaphoreType.DMA((nt, 3))],
        compiler_params=pltpu.CompilerParams(collective_id=cid), name=name,
    )(*[s for (s, _, _) in items])


def _slot2(ref, kind, j, n):
    if kind == "col":
        return ref.at[:, pl.ds(pl.multiple_of(j * n, n), n)]
    return ref.at[pl.ds(pl.multiple_of(j * n, n), n), :]


def _rs_chips_seq(name, parts, kinds, cid):
    nm = len(parts)
    out_type = []
    for g, k in zip(parts, kinds):
        r, c = g.shape
        ps = (r, c // 4) if k == "col" else (r // 4, c)
        out_type += [jax.ShapeDtypeStruct(ps, BF), jax.ShapeDtypeStruct((3,) + ps, BF)]

    def body(*refs):
        g = refs[:nm]
        outs = refs[nm:3 * nm]
        loc, ssem, rsem = refs[3 * nm:]
        x, y, c, chips = _place()
        own = 2 * x + y
        barrier = pltpu.get_barrier_semaphore()
        for (px, py) in chips:
            pl.semaphore_signal(barrier, inc=1, device_id=(px, py, c), device_id_type=MESH)
        pl.semaphore_wait(barrier, 3)
        cps = []
        for m in range(nm):
            k = kinds[m]
            own_o, got_o = outs[2 * m], outs[2 * m + 1]
            n = g[m].shape[1] // 4 if k == "col" else g[m].shape[0] // 4
            lc = pltpu.make_async_copy(_slot2(g[m], k, own, n), own_o, loc.at[m])
            lc.start()
            cps.append(lc)
            for p, (px, py) in enumerate(chips):
                cp = pltpu.make_async_remote_copy(
                    src_ref=_slot2(g[m], k, 2 * px + py, n), dst_ref=got_o.at[p],
                    send_sem=ssem.at[m, p], recv_sem=rsem.at[m, p], device_id=(px, py, c), device_id_type=MESH)
                cp.start()
                cps.append(cp)
        for cp in cps:
            cp.wait()

    return pl.kernel(
        body, out_type=out_type, mesh=plsc.ScalarSubcoreMesh(axis_name="seq", num_cores=1),
        scratch_types=[pltpu.SemaphoreType.DMA((nm,)), pltpu.SemaphoreType.DMA((nm, 3)), pltpu.SemaphoreType.DMA((nm, 3))],
        compiler_params=pltpu.CompilerParams(collective_id=cid), name=name,
    )(*parts)


def _finish_share(name, own, got, kind, c_arr):
    L, r, c = own.shape
    tr = _pick(r, 128 if kind == "col" else 256)
    nb = r // tr
    nq = L * nb
    own2 = own.reshape(L * r, c)
    got2 = got.reshape(3 * L * r, c)
    pick = lambda h, q: q * (1 - h) + (nq - 1) * h
    in_specs = [pl.BlockSpec((tr, c), lambda h, q, cc: (pick(h, q), 0))]
    in_specs += [pl.BlockSpec((tr, c), functools.partial(lambda h, q, cc, p: (p * nq + pick(h, q), 0), p=p)) for p in range(3)]
    if kind == "col":
        out_sd = (L, 2, r, c)
        o_spec = pl.BlockSpec((None, 2, tr, c), lambda h, q, cc: ((q * h) // nb, 0, (q * h) % nb, 0))
    else:
        out_sd = (L * r, 2 * c)
        o_spec = pl.BlockSpec((tr, 2 * c), lambda h, q, cc: (q * h, 0))

    def kern(c_ref, o_ref, g0, g1, g2, out_ref, mine, recv, ssem, rsem):
        h, q = pl.program_id(0), pl.program_id(1)
        x, y, cc, _ = _place()

        def swap(qq):
            return pltpu.make_async_remote_copy(src_ref=mine.at[qq], dst_ref=recv.at[qq], send_sem=ssem.at[qq],
                                                recv_sem=rsem.at[qq], device_id=(x, y, 1 - cc), device_id_type=MESH)

        @pl.when(h == 0)
        def _():
            mine[q] = ((o_ref[...].astype(F32) + g0[...].astype(F32)) + g1[...].astype(F32)) + g2[...].astype(F32)
            swap(q).start()

        @pl.when(h == 1)
        def _():
            swap(q).wait()
            a, b = mine[q], recv[q]
            first = c_ref[0] == 0
            lo, hi = jnp.where(first, a, b), jnp.where(first, b, a)
            if kind == "col":
                out_ref[0] = lo
                out_ref[1] = hi
            else:
                out_ref[:, :c] = lo
                out_ref[:, c:] = hi

    est = 2 * nq * tr * c * 4 + 6 * tr * c * 4 + 8 * tr * c * 2
    full = pl.pallas_call(
        kern,
        grid_spec=pltpu.PrefetchScalarGridSpec(
            num_scalar_prefetch=1, grid=(2, nq), in_specs=in_specs, out_specs=o_spec,
            scratch_shapes=[pltpu.VMEM((nq, tr, c), F32), pltpu.VMEM((nq, tr, c), F32),
                            pltpu.SemaphoreType.DMA((nq,)), pltpu.SemaphoreType.DMA((nq,))]),
        out_shape=jax.ShapeDtypeStruct(out_sd, F32), name=name,
        compiler_params=pltpu.CompilerParams(dimension_semantics=("arbitrary", "arbitrary"),
                                             vmem_limit_bytes=VMEM_CAP_BYTES),
    )(c_arr, own2, got2, got2, got2)
    return full.reshape(L, 2 * r, c) if kind == "col" else full.reshape(L, r, 2 * c)


def _small_allreduce(buf, name):
    R = buf.shape[0]
    assert R % 16 == 0
    h = R // 2

    def body(x_ref, o_ref, sib, csum, got, s_a, r_a, s_b, r_b, s_c, r_c):
        x, y, c, chips = _place()
        sibling = (x, y, 1 - c)
        own = 2 * x + y
        swap = pltpu.make_async_remote_copy(src_ref=x_ref, dst_ref=sib, send_sem=s_a, recv_sem=r_a,
                                            device_id=sibling, device_id_type=MESH)
        swap.start()
        swap.wait()
        a, b = x_ref[...], sib[...]
        south = c == 0
        csum[...] = jnp.where(south, a, b) + jnp.where(south, b, a)
        lo = pl.multiple_of(c * h, 8)
        mine = csum.at[pl.ds(lo, h)]
        got[own] = csum[pl.ds(lo, h)]
        sends = []
        for p, (px, py) in enumerate(chips):
            cp = pltpu.make_async_remote_copy(src_ref=mine, dst_ref=got.at[own], send_sem=s_b.at[p], recv_sem=r_b.at[p],
                                              device_id=(px, py, c), device_id_type=MESH)
            cp.start()
            sends.append(cp)
        for cp in sends:
            cp.wait()
        o_ref[pl.ds(lo, h)] = ((got[0] + got[1]) + got[2]) + got[3]
        done = o_ref.at[pl.ds(lo, h)]
        back = pltpu.make_async_remote_copy(src_ref=done, dst_ref=done, send_sem=s_c, recv_sem=r_c,
                                            device_id=sibling, device_id_type=MESH)
        back.start()
        back.wait_send()
        other = o_ref.at[pl.ds(pl.multiple_of((1 - c) * h, 8), h)]
        pltpu.make_async_remote_copy(src_ref=other, dst_ref=other, send_sem=s_c, recv_sem=r_c,
                                     device_id=sibling, device_id_type=MESH).wait_recv()

    vm = pl.BlockSpec(memory_space=pltpu.VMEM)
    return pl.pallas_call(
        body, out_shape=jax.ShapeDtypeStruct(buf.shape, F32), in_specs=[vm], out_specs=vm,
        scratch_shapes=[pltpu.VMEM((R, LANES), F32), pltpu.VMEM((R, LANES), F32), pltpu.VMEM((4, h, LANES), F32),
                        pltpu.SemaphoreType.DMA, pltpu.SemaphoreType.DMA, pltpu.SemaphoreType.DMA((3,)),
                        pltpu.SemaphoreType.DMA((3,)), pltpu.SemaphoreType.DMA, pltpu.SemaphoreType.DMA],
        name=name,
        compiler_params=pltpu.CompilerParams(vmem_limit_bytes=VMEM_CAP_BYTES),
    )(buf)


PACK_TILE_ROWS = 8


def _item_rows(shape):
    n = 1
    for d in shape:
        n *= d
    return -(-n // (PACK_TILE_ROWS * LANES)) * PACK_TILE_ROWS


def _pack(arrs, rows_total):
    buf = jnp.zeros((rows_total, LANES), F32)
    r = 0
    for a in arrs:
        f = a.reshape(-1).astype(F32)
        nr = _item_rows(a.shape)
        block = jnp.pad(f, (0, nr * LANES - f.shape[0])).reshape(nr, LANES)
        buf = lax.dynamic_update_slice(buf, block, (r, 0))
        r += nr
    return buf


def _unpack(buf, shapes):
    out, r = [], 0
    for s in shapes:
        n = 1
        for d in s:
            n *= d
        nr = _item_rows(s)
        out.append(buf[r:r + nr].reshape(-1)[:n].reshape(s))
        r += nr
    return out


def _rows_needed(shapes):
    return -(-sum(_item_rows(s) for s in shapes) // (2 * PACK_TILE_ROWS)) * (2 * PACK_TILE_ROWS)


def _two_rows(a, b):
    out = jnp.zeros((2, a.shape[1]), a.dtype)
    return lax.dynamic_update_slice(lax.dynamic_update_slice(out, a, (0, 0)), b, (1, 0))


def _adam(w, g, m, v):
    m = ADAM_B1 * m + (1.0 - ADAM_B1) * g
    v = ADAM_B2 * v + (1.0 - ADAM_B2) * jnp.square(g)
    m_hat = m / (1.0 - ADAM_B1 ** ADAM_STEP)
    v_hat = v / (1.0 - ADAM_B2 ** ADAM_STEP)
    delta = -ADAM_LR * (m_hat / (jnp.sqrt(v_hat) + ADAM_EPS) + ADAM_WD * w)
    return delta, m, v


def _adam_call(name, w2, g2, m2, v2, tr):
    def fn(rv, cv):
        return list(_adam(*rv)), []

    width = w2.shape[1]
    return _rowcall(name, fn, [(w2, 0, width), (g2, 0, width), (m2, 0, width), (v2, 0, width)], [],
                    [(width, F32)] * 3, [], tr)


def kernel(x, mem, mem_norm, lb_logits, ffn1_norm, ffn1_w_in, ffn1_w_out, mix_norm, mem_w_kv, hgrn_w_in, hgrn_gnorm, hgrn_w_out, gmlp_w_in, gmlp_ln_g, gmlp_ln_b, gmlp_w_s, gmlp_b_s, gmlp_w_out, ffn2_norm, ffn2_w_in, ffn2_w_out, final_norm, loss_target, m_mem_norm, m_lb_logits, m_ffn1_norm, m_ffn1_w_in, m_ffn1_w_out, m_mix_norm, m_mem_w_kv, m_hgrn_w_in, m_hgrn_gnorm, m_hgrn_w_out, m_gmlp_w_in, m_gmlp_ln_g, m_gmlp_ln_b, m_gmlp_w_s, m_gmlp_b_s, m_gmlp_w_out, m_ffn2_norm, m_ffn2_w_in, m_ffn2_w_out, m_final_norm, v_mem_norm, v_lb_logits, v_ffn1_norm, v_ffn1_w_in, v_ffn1_w_out, v_mix_norm, v_mem_w_kv, v_hgrn_w_in, v_hgrn_gnorm, v_hgrn_w_out, v_gmlp_w_in, v_gmlp_ln_g, v_gmlp_ln_b, v_gmlp_w_s, v_gmlp_b_s, v_gmlp_w_out, v_ffn2_norm, v_ffn2_w_in, v_ffn2_w_out, v_final_norm):
    bl, seq, D = x.shape
    T = bl * seq
    mem_len = mem.shape[1]
    chip = 2 * lax.axis_index("x") + lax.axis_index("y")
    c_arr = lax.axis_index("c").astype(jnp.int32).reshape(1)
    TR = 256

    big = [("ffn1_w_in", ffn1_w_in, "col"), ("ffn1_w_out", ffn1_w_out, "row"), ("mem_w_kv", mem_w_kv, "col"),
           ("hgrn_w_in", hgrn_w_in, "col"), ("hgrn_w_out", hgrn_w_out, "row"), ("gmlp_w_in", gmlp_w_in, "col"),
           ("gmlp_w_out", gmlp_w_out, "row"), ("ffn2_w_in", ffn2_w_in, "col"), ("ffn2_w_out", ffn2_w_out, "row")]
    kinds = [k for (_, _, k) in big]
    shards_bf = []
    for nm, w, _ in big:
        L, r, c = w.shape
        (wb,) = _rowcall("cast_" + nm, lambda rv, cv: ([rv[0]], []), [(w.reshape(L * r, c), 0, c)], [], [(c, BF)], [], 512)
        shards_bf.append(wb.reshape(L, r, c))
    sb = dict(zip([nm for (nm, _, _) in big], shards_bf))
    groups = [[("ffn1_w_in", 0)], [("ffn1_w_out", 0)], [("hgrn_w_in", None)], [("mem_w_kv", None)], [("hgrn_w_out", None)],
              [("ffn2_w_in", 0), ("ffn2_w_out", 0)],
              [("ffn1_w_in", 1), ("ffn1_w_out", 1)],
              [("gmlp_w_in", None), ("gmlp_w_out", None)],
              [("ffn2_w_in", 1), ("ffn2_w_out", 1)]]
    kind_of = {nm: k for (nm, _, k) in big}
    gathered = {nm: [None, None] for nm in ("ffn1_w_in", "ffn1_w_out", "ffn2_w_in", "ffn2_w_out")}
    for gi, grp in enumerate(groups):
        outs = _allgather_seq("gather_%d" % gi, [(sb[nm], kind_of[nm], l) for (nm, l) in grp], gi)
        for (nm, l), o in zip(grp, outs):
            if l is None:
                gathered[nm] = o
            else:
                gathered[nm][l] = o

    ln_w = GM_GROUPS * GM_GROUP_DIM
    placed = lax.dynamic_update_slice(jnp.zeros((8, ln_w), F32), 0.5 * gmlp_ln_g, (0, chip * gmlp_ln_g.shape[1]))
    placed = lax.dynamic_update_slice(placed, 0.5 * gmlp_ln_b, (1, chip * gmlp_ln_g.shape[1]))
    ln_full = _small_allreduce(placed.reshape(16, LANES), "gather_ln").reshape(8, ln_w)
    ln_g_full, ln_b_full = ln_full[0:1], ln_full[1:2]

    def rms_fwd(name, xin, g):
        (h,) = _rowcall(name, lambda rv, cv: ([_rmsnorm(rv[0], cv[0])], []), [(xin, 0, D)], [g.reshape(1, D)], [(D, BF)], [], TR)
        return h

    def ffn_fwd(tag, xin, g, w_in, w_out, layer):
        dff = w_out[layer].shape[1]
        h = rms_fwd("rms_" + tag, xin, g)
        zg, zu, a = _ffn_in_swiglu("ffn_in_" + tag, h, w_in[layer], 2048, 256)
        xo = _mm("ffn_out_" + tag, a, w_out[layer], "nn", F32, 1024, 1024, dff, scale=0.5, res=xin, b_lead=0)
        return xo, (xin, h, zg, zu, a)

    def ffn_bwd(tag, dxo, saved, g, w_in, w_out, layer):
        xin, h, zg, zu, a = saved
        dff = w_out[layer].shape[1]
        dw_out = _mm_tn_pair("ffn_dwo_" + tag, a, dxo, "row", c_arr, dff // 2, T, scale=0.5)
        dz = _ffn_da_swiglu("ffn_da_" + tag, dxo, w_out[layer], zg, zu, 512)
        dw_in = _mm_tn_pair("ffn_dwi_" + tag, h, dz, "col", c_arr, 512, T)
        dx, dg = _mm_dh_rms("ffn_dh_" + tag, dz, w_in[layer], xin, g.reshape(1, D), dxo, 512)
        return dx, dg, dw_in, dw_out

    def rms_bwd(name, xin, g, dh, dres):
        def fn(rv, cv):
            _, vjp = jax.vjp(_rmsnorm, rv[0], cv[0])
            dx, dg = vjp(rv[1])
            if dres is not None:
                dx = dx + rv[2]
            return [dx], [dg]

        rows = [(xin, 0, D), (dh, 0, D)] + ([(dres, 0, D)] if dres is not None else [])
        dx, dg = _rowcall(name, fn, rows, [g.reshape(1, D)], [(D, F32)], [((1, D), F32)], TR)
        return dx, dg

    x0 = x.reshape(T, D)
    tgt = loss_target.reshape(T, D)
    mem2 = mem.reshape(bl * mem_len, D)
    memn = rms_fwd("rms_mem", mem2, mem_norm)

    x1, sv_f10 = ffn_fwd("f1l0", x0, ffn1_norm[0], gathered["ffn1_w_in"], gathered["ffn1_w_out"], 0)
    h_m0 = rms_fwd("rms_mix0", x1, mix_norm[0])
    z_m0 = _mm("mix_in_0", h_m0, gathered["hgrn_w_in"], "nn", F32, 2048, 512, D, b_lead=0)
    kv = [_mm("kv_%d" % i, memn, gathered["mem_w_kv"], "nn", F32, 512, 512, D, b_lead=i) for i in range(2)]
    cat0, stash0 = _hgrn_fwd2(z_m0, lb_logits, hgrn_gnorm, kv[0], bl, seq)
    x2 = _mm("mix_out_0", cat0, gathered["hgrn_w_out"], "nn", F32, 1024, 1024, cat0.shape[1], res=x1, b_lead=0)
    x3, sv_f20 = ffn_fwd("f2l0", x2, ffn2_norm[0], gathered["ffn2_w_in"], gathered["ffn2_w_out"], 0)
    x4, sv_f11 = ffn_fwd("f1l1", x3, ffn1_norm[1], gathered["ffn1_w_in"], gathered["ffn1_w_out"], 1)
    h_m1 = rms_fwd("rms_mix1", x4, mix_norm[1])
    z_m1 = _mm("mix_in_1", h_m1, gathered["gmlp_w_in"], "nn", F32, 2048, 512, D, b_lead=0)
    nc1 = seq // GM_CHUNK
    w_s, b_s = gmlp_w_s[0], gmlp_b_s[0]
    cat1 = _gmlp_fwd(z_m1, ln_g_full, ln_b_full, w_s, b_s, kv[1], bl, nc1)
    x5 = _mm("mix_out_1", cat1, gathered["gmlp_w_out"], "nn", F32, 1024, 1024, cat1.shape[1], res=x4, b_lead=0)
    x6, sv_f21 = ffn_fwd("f2l1", x5, ffn2_norm[1], gathered["ffn2_w_in"], gathered["ffn2_w_out"], 1)

    def head(rv, cv):
        def f(xx, gg):
            err = _rmsnorm(xx, gg) - rv[1]
            return 0.5 * jnp.sum(jnp.mean(err * err, axis=-1, keepdims=True), axis=0, keepdims=True)

        ls, vjp = jax.vjp(f, rv[0], cv[0])
        dx, dg = vjp(jnp.ones((1, 1), F32))
        return [dx], [dg, jnp.broadcast_to(ls, (1, 128))]

    dx6, d_final, loss_part = _rowcall("loss_head", head, [(x6, 0, D), (tgt, 0, D)], [final_norm.reshape(1, D)],
                                       [(D, F32)], [((1, D), F32), ((1, 128), F32)], TR)

    rs_out = {}
    n_gather = len(groups)

    def rs(gi, items):
        outs = _rs_chips_seq("reduce_%d" % gi, [p for (_, p, _) in items], [k for (_, _, k) in items], n_gather + gi)
        for i, (key, _, _) in enumerate(items):
            rs_out[key] = (outs[2 * i], outs[2 * i + 1])

    dx5, dg_f21, dwi_f21, dwo_f21 = ffn_bwd("f2l1", dx6, sv_f21, ffn2_norm[1], gathered["ffn2_w_in"], gathered["ffn2_w_out"], 1)
    rs(0, [(("ffn2_w_out", 1), dwo_f21, "row"), (("ffn2_w_in", 1), dwi_f21, "col")])
    dcat1 = _mm("mix_dcat_1", dx5, gathered["gmlp_w_out"], "nt", F32, 2048, 1024, D, b_lead=0)
    dwo_m1 = _mm_tn_pair("mix_dwo_1", cat1, dx5, "row", c_arr, 1024, T)
    dz_m1, dkv1, d_lng, d_lnb, d_ws, d_bs = _gmlp_bwd(z_m1, dcat1, ln_g_full, ln_b_full, w_s, b_s, kv[1], bl, nc1)
    dx4, dg_m1 = _mm_dh_rms("mix_dh_1", dz_m1, gathered["gmlp_w_in"], x4, mix_norm[1].reshape(1, D), dx5, 512)
    dwi_m1 = _mm_tn_pair("mix_dwi_1", h_m1, dz_m1, "col", c_arr, 1024, T)
    rs(1, [(("gmlp_w_out", 0), dwo_m1, "row"), (("gmlp_w_in", 0), dwi_m1, "col")])
    dx3, dg_f11, dwi_f11, dwo_f11 = ffn_bwd("f1l1", dx4, sv_f11, ffn1_norm[1], gathered["ffn1_w_in"], gathered["ffn1_w_out"], 1)
    rs(2, [(("ffn1_w_out", 1), dwo_f11, "row"), (("ffn1_w_in", 1), dwi_f11, "col")])

    dx2, dg_f20, dwi_f20, dwo_f20 = ffn_bwd("f2l0", dx3, sv_f20, ffn2_norm[0], gathered["ffn2_w_in"], gathered["ffn2_w_out"], 0)
    rs(3, [(("ffn2_w_out", 0), dwo_f20, "row"), (("ffn2_w_in", 0), dwi_f20, "col")])
    dcat0 = _mm("mix_dcat_0", dx2, gathered["hgrn_w_out"], "nt", F32, 2048, 1024, D, b_lead=0)
    dwo_m0 = _mm_tn_pair("mix_dwo_0", cat0, dx2, "row", c_arr, 1024, T)
    dz_m0, dkv0, d_lb, d_gn = _hgrn_bwd2(z_m0, dcat0, stash0, lb_logits, hgrn_gnorm, kv[0], bl, seq)
    dx1, dg_m0 = _mm_dh_rms("mix_dh_0", dz_m0, gathered["hgrn_w_in"], x1, mix_norm[0].reshape(1, D), dx2, 512)
    dwi_m0 = _mm_tn_pair("mix_dwi_0", h_m0, dz_m0, "col", c_arr, 1024, T)
    rs(4, [(("hgrn_w_out", 0), dwo_m0, "row"), (("hgrn_w_in", 0), dwi_m0, "col")])

    dwkv = [_mm_tn_pair("kv_dw_%d" % i, memn, dkv, "col", c_arr, 1024, 512) for i, dkv in enumerate([dkv0, dkv1])]
    rs(5, [(("mem_w_kv", 0), dwkv[0], "col"), (("mem_w_kv", 1), dwkv[1], "col")])
    dmemn = _mm("kv_dx_0", dkv0, gathered["mem_w_kv"], "nt", F32, 512, 512, 1024, b_lead=0)
    dmemn = _mm("kv_dx_1", dkv1, gathered["mem_w_kv"], "nt", F32, 512, 512, 1024, res=dmemn, b_lead=1)
    _, d_memnorm = rms_bwd("rms_bwd_mem", mem2, mem_norm, dmemn, None)

    dx0, dg_f10, dwi_f10, dwo_f10 = ffn_bwd("f1l0", dx1, sv_f10, ffn1_norm[0], gathered["ffn1_w_in"], gathered["ffn1_w_out"], 0)
    rs(6, [(("ffn1_w_out", 0), dwo_f10, "row")])
    rs(7, [(("ffn1_w_in", 0), dwi_f10, "col")])

    shard_grads = []
    for (nm, w, k) in big:
        per_layer = []
        for l in range(w.shape[0]):
            own, got = rs_out[(nm, l)]
            per_layer.append(_finish_share("finish_%s_%d" % (nm, l), own[None], got[:, None], k, c_arr))
        shard_grads.append(per_layer[0] if len(per_layer) == 1 else jnp.concatenate(per_layer, axis=0))

    big_w = [w for (_, w, _) in big]
    big_m = [m_ffn1_w_in, m_ffn1_w_out, m_mem_w_kv, m_hgrn_w_in, m_hgrn_w_out, m_gmlp_w_in, m_gmlp_w_out, m_ffn2_w_in, m_ffn2_w_out]
    big_v = [v_ffn1_w_in, v_ffn1_w_out, v_mem_w_kv, v_hgrn_w_in, v_hgrn_w_out, v_gmlp_w_in, v_gmlp_w_out, v_ffn2_w_in, v_ffn2_w_out]
    big_out = {}
    for (nm, w, _), g, m, v in zip(big, shard_grads, big_m, big_v):
        L, r, c = w.shape
        d2, m2, v2 = _adam_call("adam_" + nm, w.reshape(L * r, c), g.reshape(L * r, c), m.reshape(L * r, c),
                                v.reshape(L * r, c), 256)
        big_out[nm] = (g, d2.reshape(w.shape), m2.reshape(w.shape), v2.reshape(w.shape))

    d_ffn1n = _two_rows(dg_f10, dg_f11)
    d_mixn = _two_rows(dg_m0, dg_m1)
    d_ffn2n = _two_rows(dg_f20, dg_f21)
    small_parts = [loss_part[:, :1], d_memnorm, d_lb, d_ffn1n, d_mixn, d_gn, d_lng, d_lnb, d_ws, d_bs, d_ffn2n, d_final]
    red_shapes = [(1,), mem_norm.shape, lb_logits.shape, ffn1_norm.shape, mix_norm.shape, hgrn_gnorm.shape, (1, ln_w), (1, ln_w),
                  gmlp_w_s.shape, gmlp_b_s.shape, ffn2_norm.shape, final_norm.shape]
    red = _small_allreduce(_pack(small_parts, _rows_needed(red_shapes)), "reduce_small")
    (loss_v, g_memn, g_lb, g_f1n, g_mixn, g_gn, g_lng_full, g_lnb_full, g_ws, g_bs, g_f2n, g_fin) = _unpack(red, red_shapes)
    lsh = gmlp_ln_g.shape[1]
    g_lng = lax.dynamic_slice(g_lng_full, (0, chip * lsh), (1, lsh))
    g_lnb = lax.dynamic_slice(g_lnb_full, (0, chip * lsh), (1, lsh))
    small_w = [mem_norm, lb_logits, ffn1_norm, mix_norm, hgrn_gnorm, gmlp_ln_g, gmlp_ln_b, gmlp_w_s, gmlp_b_s, ffn2_norm, final_norm]
    small_g = [g_memn, g_lb, g_f1n, g_mixn, g_gn, g_lng, g_lnb, g_ws, g_bs, g_f2n, g_fin]
    small_m = [m_mem_norm, m_lb_logits, m_ffn1_norm, m_mix_norm, m_hgrn_gnorm, m_gmlp_ln_g, m_gmlp_ln_b, m_gmlp_w_s, m_gmlp_b_s, m_ffn2_norm, m_final_norm]
    small_v = [v_mem_norm, v_lb_logits, v_ffn1_norm, v_mix_norm, v_hgrn_gnorm, v_gmlp_ln_g, v_gmlp_ln_b, v_gmlp_w_s, v_gmlp_b_s, v_ffn2_norm, v_final_norm]
    sshapes = [w.shape for w in small_w]
    nrow = _rows_needed(sshapes)
    d_p, m_p, v_p = _adam_call("adam_small", _pack(small_w, nrow), _pack(small_g, nrow), _pack(small_m, nrow), _pack(small_v, nrow), nrow)
    s_delta, s_m, s_v = _unpack(d_p, sshapes), _unpack(m_p, sshapes), _unpack(v_p, sshapes)
    small_names = ["mem_norm", "lb_logits", "ffn1_norm", "mix_norm", "hgrn_gnorm", "gmlp_ln_g", "gmlp_ln_b", "gmlp_w_s", "gmlp_b_s", "ffn2_norm", "final_norm"]
    small_out = {nm: (g.reshape(w.shape), d, m, v) for nm, w, g, d, m, v in zip(small_names, small_w, small_g, s_delta, s_m, s_v)}

    order = ["mem_norm", "lb_logits", "ffn1_norm", "ffn1_w_in", "ffn1_w_out", "mix_norm", "mem_w_kv", "hgrn_w_in", "hgrn_gnorm",
             "hgrn_w_out", "gmlp_w_in", "gmlp_ln_g", "gmlp_ln_b", "gmlp_w_s", "gmlp_b_s", "gmlp_w_out", "ffn2_norm", "ffn2_w_in",
             "ffn2_w_out", "final_norm"]
    allo = {**big_out, **small_out}
    grad_x = dx0.reshape(x.shape)
    return (loss_v.reshape(()), grad_x, *[allo[n][0] for n in order], *[allo[n][1] for n in order],
            *[allo[n][2] for n in order], *[allo[n][3] for n in order])
```

```python
import functools

import jax
import jax.numpy as jnp
from jax import lax
from jax.experimental import pallas as pl
from jax.experimental.pallas import tpu as pltpu
from jax.experimental.pallas import tpu_sc as plsc

BF = jnp.bfloat16
F32 = jnp.float32
MESH = pl.DeviceIdType.MESH

EPS = 1e-6
D_MODEL = 1024
HG_HEADS = 8
HG_DIM = 128
HG_CHUNK = 64
GM_CHUNK = 128
GM_GROUPS = 8
GM_GROUP_DIM = 256
XA_HEADS = 4
XA_DIM = 256
ADAM_LR = 0.001
ADAM_B1 = 0.9
ADAM_B2 = 0.999
ADAM_EPS = 1e-08
ADAM_WD = 0.01
ADAM_STEP = 10

VMEM_CAP_BYTES = 60 * 1024 * 1024
LANES = 1024


def _pick(n, cap, mult=16):
    if n <= cap:
        return n
    for d in range(cap - cap % mult, 0, -mult):
        if n % d == 0:
            return d
    raise ValueError((n, cap, mult))


def _dg(a, b, ca, cb):
    return lax.dot_general(a.astype(BF), b.astype(BF), (((ca,), (cb,)), ((), ())), preferred_element_type=F32)


@jax.custom_vjp
def dot_nn(a, b):
    return _dg(a, b, 1, 0)


def _nn_fwd(a, b):
    return _dg(a, b, 1, 0), (a, b)


def _nn_bwd(r, g):
    a, b = r
    return _dg(g, b, 1, 1), _dg(a, g, 0, 0)


dot_nn.defvjp(_nn_fwd, _nn_bwd)


@jax.custom_vjp
def dot_nt(a, b):
    return _dg(a, b, 1, 1)


def _nt_fwd(a, b):
    return _dg(a, b, 1, 1), (a, b)


def _nt_bwd(r, g):
    a, b = r
    return _dg(g, b, 1, 0), _dg(g, a, 0, 0)


dot_nt.defvjp(_nt_fwd, _nt_bwd)


@jax.custom_vjp
def dot_tn(a, b):
    return _dg(a, b, 0, 0)


def _tn_fwd(a, b):
    return _dg(a, b, 0, 0), (a, b)


def _tn_bwd(r, g):
    a, b = r
    return _dg(b, g, 1, 1), _dg(a, g, 1, 0)


dot_tn.defvjp(_tn_fwd, _tn_bwd)


def _rmsnorm(x, g):
    return x * lax.rsqrt(jnp.mean(x * x, axis=-1, keepdims=True) + EPS) * g


def _silu(x):
    return x * jax.nn.sigmoid(x)


def _gelu(x):
    return 0.5 * x * (1.0 + lax.erf(x * (0.5 ** 0.5)))


def _softmax_last(s):
    m = lax.stop_gradient(jnp.max(s, axis=-1, keepdims=True))
    e = jnp.exp(s - m)
    return e / jnp.sum(e, axis=-1, keepdims=True)


def _tril(n):
    r = lax.broadcasted_iota(jnp.int32, (n, n), 0)
    c = lax.broadcasted_iota(jnp.int32, (n, n), 1)
    return r >= c


def _cumsum_rows(l):
    n = l.shape[0]
    return lax.dot_general(_tril(n).astype(F32), l, (((1,), (0,)), ((), ())),
                           precision=lax.Precision.HIGHEST, preferred_element_type=F32)


def _attention(zx, mk, mv):
    s = dot_nt(zx, mk) * (XA_DIM ** -0.5)
    return dot_nn(_softmax_last(s), mv)


def _hgrn_head(zq, zf, zi, zg, l0, l1, l2, gn, S):
    m = lax.stop_gradient(jnp.maximum(jnp.maximum(l0, l1), l2))
    e0 = jnp.exp(l0 - m)
    lb = e0 / (e0 + jnp.exp(l1 - m) + jnp.exp(l2 - m))
    q = _silu(zq)
    f = lb + (1.0 - lb) * jax.nn.sigmoid(zf)
    k = 1.0 - f
    b = _cumsum_rows(jnp.log(f))
    b_last = b[HG_CHUNK - 1:HG_CHUNK, :]
    q_dec = q * jnp.exp(b)
    k_inv = k * jnp.exp(-b)
    a = jnp.where(_tril(HG_CHUNK), dot_nt(q_dec, k_inv), 0.0)
    o = dot_nn(a, zi) + dot_nn(q_dec, S)
    S_new = jnp.exp(b_last).reshape(HG_DIM, 1) * S + dot_tn(k * jnp.exp(b_last - b), zi)
    o = _rmsnorm(o, gn) * _silu(zg)
    return o, S_new


def _hgrn_block(zq, zf, zi, zg, zx, l0, l1, l2, gn, mk, mv, S):
    outs, s_new = [], []
    for h in range(HG_HEADS):
        o, sn = _hgrn_head(zq[h], zf[h], zi[h], zg[h], l0[h], l1[h], l2[h], gn, S[h])
        outs.append(o)
        s_new.append(sn)
    for a in range(XA_HEADS):
        outs.append(_attention(zx[a], mk[a], mv[a]))
    return outs, s_new


def _gmlp_block(zu, zv, zx, lng, lnb, ws, bs, mk, mv):
    gv = [_gelu(v) for v in zv]
    width = GM_GROUPS * GM_GROUP_DIM
    mu = sum(jnp.sum(g, axis=-1, keepdims=True) for g in gv) / width
    xc = [g - mu for g in gv]
    var = sum(jnp.sum(c * c, axis=-1, keepdims=True) for c in xc) / width
    r = lax.rsqrt(var + EPS)
    outs = []
    for g in range(GM_GROUPS):
        v = xc[g] * r * lng[g] + lnb[g]
        w = jnp.where(_tril(GM_CHUNK), ws[g], 0.0)
        mixed = dot_nn(w, v) + bs[g].reshape(GM_CHUNK, 1)
        outs.append(_gelu(zu[g]) * mixed)
    for a in range(XA_HEADS):
        outs.append(_attention(zx[a], mk[a], mv[a]))
    return outs


def _rowcall(name, fn, rows, consts, row_outs, acc_outs, tr):
    nrows = rows[0][0].shape[0]
    tr = _pick(nrows, tr)
    n_r, n_c, n_ro, n_ao = len(rows), len(consts), len(row_outs), len(acc_outs)

    def kern(*refs):
        rv = [r[...] for r in refs[:n_r]]
        cv = [r[...] for r in refs[n_r:n_r + n_c]]
        ro_refs = refs[n_r + n_c:n_r + n_c + n_ro]
        ao_refs = refs[n_r + n_c + n_ro:]
        ro, ao = fn(rv, cv)
        for ref, v in zip(ro_refs, ro):
            ref[...] = v.astype(ref.dtype)
        if n_ao:
            @pl.when(pl.program_id(0) == 0)
            def _():
                for ref in ao_refs:
                    ref[...] = jnp.zeros(ref.shape, ref.dtype)

            for ref, v in zip(ao_refs, ao):
                ref[...] += v.astype(ref.dtype)

    in_specs = [pl.BlockSpec((tr, w), functools.partial(lambda i, cb: (i, cb), cb=cb)) for (_, cb, w) in rows]
    in_specs += [pl.BlockSpec(c.shape, lambda i: (0, 0)) for c in consts]
    out_specs = [pl.BlockSpec((tr, w), lambda i: (i, 0)) for (w, _) in row_outs]
    out_specs += [pl.BlockSpec(s, lambda i: (0, 0)) for (s, _) in acc_outs]
    out_shape = [jax.ShapeDtypeStruct((nrows, w), dt) for (w, dt) in row_outs]
    out_shape += [jax.ShapeDtypeStruct(s, dt) for (s, dt) in acc_outs]
    est = sum(tr * w * a.dtype.itemsize for (a, _, w) in rows) + sum(tr * w * jnp.dtype(dt).itemsize for (w, dt) in row_outs)
    est += sum(c.size * c.dtype.itemsize for c in consts)
    outs = pl.pallas_call(
        kern, grid=(nrows // tr,), in_specs=in_specs, out_specs=out_specs, out_shape=out_shape, name=name,
        compiler_params=pltpu.CompilerParams(dimension_semantics=("arbitrary",),
                                             vmem_limit_bytes=VMEM_CAP_BYTES),
    )(*[a for (a, _, _) in rows], *consts)
    return outs


def _mm(name, a, b, mode, out_dtype, tm, tn, tk, scale=1.0, res=None, a_lead=None, b_lead=None):
    ash = a.shape[-2:]
    bsh = b.shape[-2:]
    if mode == "nn":
        (M, K), (K2, N) = ash, bsh
    elif mode == "nt":
        (M, K), (N, K2) = ash, bsh
    else:
        (K, M), (K2, N) = ash, bsh
    assert K == K2, (name, a.shape, b.shape)
    tm, tn, tk = min(tm, M), min(tn, N), min(tk, K)
    assert M % tm == 0 and N % tn == 0 and K % tk == 0, (name, M, N, K, tm, tn, tk)
    nk = K // tk
    dims = {"nn": (1, 0), "nt": (1, 1), "tn": (0, 0)}[mode]

    def lead(spec_shape, index_fn, lead_idx):
        if lead_idx is None:
            return pl.BlockSpec(spec_shape, index_fn)
        return pl.BlockSpec((None,) + spec_shape, lambda i, j, k: (lead_idx,) + index_fn(i, j, k))

    if mode == "tn":
        a_spec = lead((tk, tm), lambda i, j, k: (k, i), a_lead)
    else:
        a_spec = lead((tm, tk), lambda i, j, k: (i, k), a_lead)
    if mode == "nt":
        b_spec = lead((tn, tk), lambda i, j, k: (j, k), b_lead)
    else:
        b_spec = lead((tk, tn), lambda i, j, k: (k, j), b_lead)
    o_spec = pl.BlockSpec((tm, tn), lambda i, j, k: (i, j))
    has_res = res is not None

    def kern(*refs):
        a_ref, b_ref = refs[0], refs[1]
        res_ref = refs[2] if has_res else None
        o_ref = refs[3] if has_res else refs[2]
        acc_ref = refs[-1] if nk > 1 else None
        p = lax.dot_general(a_ref[...].astype(BF), b_ref[...].astype(BF), (((dims[0],), (dims[1],)), ((), ())),
                            preferred_element_type=F32)

        def finish(v):
            if scale != 1.0:
                v = v * scale
            if has_res:
                v = res_ref[...] + v
            o_ref[...] = v.astype(o_ref.dtype)

        if nk == 1:
            finish(p)
        else:
            k = pl.program_id(2)

            @pl.when(k == 0)
            def _():
                acc_ref[...] = p

            @pl.when(k > 0)
            def _():
                acc_ref[...] += p

            @pl.when(k == nk - 1)
            def _():
                finish(acc_ref[...])

    ins = [a, b] + ([res] if has_res else [])
    in_specs = [a_spec, b_spec] + ([o_spec] if has_res else [])
    est = tm * tk * a.dtype.itemsize + tk * tn * b.dtype.itemsize + tm * tn * (jnp.dtype(out_dtype).itemsize + 8)
    return pl.pallas_call(
        kern, grid=(M // tm, N // tn, nk), in_specs=in_specs, out_specs=o_spec,
        out_shape=jax.ShapeDtypeStruct((M, N), out_dtype),
        scratch_shapes=[pltpu.VMEM((tm, tn), F32)] if nk > 1 else [],
        name=name,
        compiler_params=pltpu.CompilerParams(dimension_semantics=("parallel", "parallel", "arbitrary"),
                                             vmem_limit_bytes=VMEM_CAP_BYTES),
    )(*ins)


def _ffn_in_swiglu(name, h, w3, tm, tn):
    T, D = h.shape
    dff = w3.shape[2] // 2
    tm = min(tm, T)
    assert T % tm == 0 and dff % tn == 0
    nj = dff // tn

    def kern(h_ref, wg_ref, wu_ref, zg_ref, zu_ref, a_ref):
        hb = h_ref[...]
        g = jnp.dot(hb, wg_ref[...], preferred_element_type=F32).astype(BF)
        u = jnp.dot(hb, wu_ref[...], preferred_element_type=F32).astype(BF)
        zg_ref[...] = g
        zu_ref[...] = u
        a_ref[...] = (_silu(g.astype(F32)) * u.astype(F32)).astype(BF)

    o_spec = pl.BlockSpec((tm, tn), lambda i, j: (i, j))
    return pl.pallas_call(
        kern, grid=(T // tm, nj),
        in_specs=[pl.BlockSpec((tm, D), lambda i, j: (i, 0)),
                  pl.BlockSpec((None, D, tn), lambda i, j: (0, 0, j)),
                  pl.BlockSpec((None, D, tn), lambda i, j: (0, 0, j + nj))],
        out_specs=[o_spec, o_spec, o_spec],
        out_shape=[jax.ShapeDtypeStruct((T, dff), BF)] * 3, name=name,
        compiler_params=pltpu.CompilerParams(dimension_semantics=("parallel", "arbitrary"),
                                             vmem_limit_bytes=VMEM_CAP_BYTES),
    )(h, w3, w3)


def _ffn_da_swiglu(name, dxo, w3, zg, zu, tm):
    T, D = dxo.shape
    dff = w3.shape[1]
    tm = min(tm, T)
    assert T % tm == 0 and dff % 2 == 0
    hc = dff // 2

    def kern(d_ref, w_ref, g_ref, u_ref, dz_ref):
        db = (d_ref[...] * 0.5).astype(BF)
        for s in range(2):
            cols = slice(s * hc, (s + 1) * hc)
            da = lax.dot_general(db, w_ref[cols, :], (((1,), (1,)), ((), ())), preferred_element_type=F32)
            g, u = g_ref[:, cols].astype(F32), u_ref[:, cols].astype(F32)
            sg = 1.0 / (1.0 + jnp.exp(-g))
            gs = g * sg
            dz_ref[:, cols] = (da * u * (sg + gs * (1.0 - sg))).astype(dz_ref.dtype)
            dz_ref[:, dff + s * hc:dff + (s + 1) * hc] = (da * gs).astype(dz_ref.dtype)

    row = lambda w: pl.BlockSpec((tm, w), lambda i: (i, 0))
    return pl.pallas_call(
        kern, grid=(T // tm,),
        in_specs=[row(D), pl.BlockSpec((None, dff, D), lambda i: (0, 0, 0), pipeline_mode=pl.Buffered(1)), row(dff), row(dff)],
        out_specs=row(2 * dff), out_shape=jax.ShapeDtypeStruct((T, 2 * dff), BF), name=name,
        compiler_params=pltpu.CompilerParams(dimension_semantics=("arbitrary",), vmem_limit_bytes=VMEM_CAP_BYTES),
    )(dxo, w3, zg, zu)


def _mm_dh_rms(name, dz, w3, xin, g, dres, tm):
    T, K = dz.shape
    D = w3.shape[1]
    tm = min(tm, T)
    assert T % tm == 0

    def kern(dz_ref, w_ref, x_ref, g_ref, r_ref, dx_ref, dg_ref):
        dh = lax.dot_general(dz_ref[...], w_ref[...], (((1,), (1,)), ((), ())), preferred_element_type=F32)
        _, vjp = jax.vjp(_rmsnorm, x_ref[...], g_ref[...])
        dx, dg = vjp(dh)
        dx_ref[...] = dx + r_ref[...]

        @pl.when(pl.program_id(0) == 0)
        def _():
            dg_ref[...] = jnp.zeros(dg_ref.shape, F32)

        dg_ref[...] += dg

    row = lambda w: pl.BlockSpec((tm, w), lambda i: (i, 0))
    one = pl.BlockSpec((1, D), lambda i: (0, 0))
    return pl.pallas_call(
        kern, grid=(T // tm,),
        in_specs=[row(K), pl.BlockSpec((None, D, K), lambda i: (0, 0, 0), pipeline_mode=pl.Buffered(1)), row(D), one, row(D)],
        out_specs=[row(D), one], out_shape=[jax.ShapeDtypeStruct((T, D), F32), jax.ShapeDtypeStruct((1, D), F32)], name=name,
        compiler_params=pltpu.CompilerParams(dimension_semantics=("arbitrary",), vmem_limit_bytes=VMEM_CAP_BYTES),
    )(dz, w3, xin, g, dres)


def _mm_tn_pair(name, a, b, kind, c_arr, tq, tk, scale=1.0):
    T, M = a.shape
    _, N = b.shape
    tk = min(tk, T)
    assert T % tk == 0
    nk = T // tk
    if kind == "col":
        hm = M // 2
        assert N % tq == 0
        nq = N // tq
        tile = (hm, tq)
        a_spec = pl.BlockSpec((tk, hm), lambda h, q, k, c: (k, jnp.bitwise_xor(h, 1 - c[0])))
        b_spec = pl.BlockSpec((tk, tq), lambda h, q, k, c: (k, q))
        o_spec = pl.BlockSpec(tile, lambda h, q, k, c: (0, q * h))
        out_sd = (hm, N)
    else:
        hn = N // 2
        assert M % tq == 0
        nq = M // tq
        tile = (tq, hn)
        a_spec = pl.BlockSpec((tk, tq), lambda h, q, k, c: (k, q))
        b_spec = pl.BlockSpec((tk, hn), lambda h, q, k, c: (k, jnp.bitwise_xor(h, 1 - c[0])))
        o_spec = pl.BlockSpec(tile, lambda h, q, k, c: (q * h, 0))
        out_sd = (M, hn)

    def kern(c_ref, a_ref, b_ref, o_ref, acc, stage, recv, ssem, rsem):
        h, q, k = pl.program_id(0), pl.program_id(1), pl.program_id(2)
        x, y, c, _ = _place()
        p = lax.dot_general(a_ref[...].astype(BF), b_ref[...].astype(BF), (((0,), (0,)), ((), ())), preferred_element_type=F32)

        @pl.when(k == 0)
        def _():
            acc[...] = p

        @pl.when(k > 0)
        def _():
            acc[...] += p

        def send(slot, qq):
            return pltpu.make_async_remote_copy(src_ref=stage.at[slot], dst_ref=recv.at[qq], send_sem=ssem.at[slot],
                                                recv_sem=rsem.at[qq], device_id=(x, y, 1 - c), device_id_type=MESH)

        last = k == nk - 1

        @pl.when(jnp.logical_and(last, h == 0))
        def _():
            slot = q % 2

            @pl.when(q >= 2)
            def _():
                send(slot, q).wait_send()

            stage[slot] = (acc[...] * scale).astype(BF)
            send(slot, q).start()

        @pl.when(jnp.logical_and(last, h == 1))
        def _():
            @pl.when(q == 0)
            def _():
                for s in range(min(nq, 2)):
                    send(s, 0).wait_send()

            send(0, q).wait_recv()
            o_ref[...] = (acc[...] * scale + recv[q].astype(F32)).astype(o_ref.dtype)

    tb = tile[0] * tile[1]
    est = tb * (4 + 2 * 2 + nq * 2 + 2 * 2) + 2 * tk * (a_spec.block_shape[1] + b_spec.block_shape[1]) * 2 * 2
    return pl.pallas_call(
        kern,
        grid_spec=pltpu.PrefetchScalarGridSpec(
            num_scalar_prefetch=1, grid=(2, nq, nk), in_specs=[a_spec, b_spec], out_specs=o_spec,
            scratch_shapes=[pltpu.VMEM(tile, F32), pltpu.VMEM((2,) + tile, BF), pltpu.VMEM((nq,) + tile, BF),
                            pltpu.SemaphoreType.DMA((2,)), pltpu.SemaphoreType.DMA((nq,))]),
        out_shape=jax.ShapeDtypeStruct(out_sd, BF), name=name,
        compiler_params=pltpu.CompilerParams(dimension_semantics=("arbitrary", "arbitrary", "arbitrary"),
                                             vmem_limit_bytes=VMEM_CAP_BYTES),
    )(c_arr, a, b)


def _hgrn_pieces(z_ref):
    W = HG_HEADS * HG_DIM
    zq = [z_ref[:, h * HG_DIM:(h + 1) * HG_DIM] for h in range(HG_HEADS)]
    zf = [z_ref[:, W + h * HG_DIM:W + (h + 1) * HG_DIM] for h in range(HG_HEADS)]
    zi = [z_ref[:, 2 * W + h * HG_DIM:2 * W + (h + 1) * HG_DIM] for h in range(HG_HEADS)]
    zg = [z_ref[:, 3 * W + h * HG_DIM:3 * W + (h + 1) * HG_DIM] for h in range(HG_HEADS)]
    zx = [z_ref[:, 4 * W + a * XA_DIM:4 * W + (a + 1) * XA_DIM] for a in range(XA_HEADS)]
    return zq, zf, zi, zg, zx


def _kv_pieces(kv_ref):
    W = XA_HEADS * XA_DIM
    mk = [kv_ref[:, a * XA_DIM:(a + 1) * XA_DIM] for a in range(XA_HEADS)]
    mv = [kv_ref[:, W + a * XA_DIM:W + (a + 1) * XA_DIM] for a in range(XA_HEADS)]
    return mk, mv


def _lb_pieces(lb_ref):
    return [[lb_ref[r:r + 1, h * HG_DIM:(h + 1) * HG_DIM] for h in range(HG_HEADS)] for r in range(3)]


def _hgrn_fwd(z, lb_logits, gnorm, kv, bl, nc):
    T, zw = z.shape
    mem_len = kv.shape[0] // bl
    cat_w = HG_HEADS * HG_DIM + XA_HEADS * XA_DIM

    def kern(z_ref, lb_ref, gn_ref, kv_ref, cat_ref, st_ref, s_scr):
        @pl.when(pl.program_id(1) == 0)
        def _():
            s_scr[...] = jnp.zeros(s_scr.shape, F32)

        st_ref[...] = s_scr[...]
        zq, zf, zi, zg, zx = _hgrn_pieces(z_ref)
        mk, mv = _kv_pieces(kv_ref)
        l0, l1, l2 = _lb_pieces(lb_ref)
        S = [s_scr[h] for h in range(HG_HEADS)]
        outs, s_new = _hgrn_block(zq, zf, zi, zg, zx, l0, l1, l2, gn_ref[...], mk, mv, S)
        for h in range(HG_HEADS):
            cat_ref[:, h * HG_DIM:(h + 1) * HG_DIM] = outs[h].astype(cat_ref.dtype)
            s_scr[h] = s_new[h]
        base = HG_HEADS * HG_DIM
        for a in range(XA_HEADS):
            cat_ref[:, base + a * XA_DIM:base + (a + 1) * XA_DIM] = outs[HG_HEADS + a].astype(cat_ref.dtype)

    return pl.pallas_call(
        kern, grid=(bl, nc),
        in_specs=[pl.BlockSpec((HG_CHUNK, zw), lambda b, n: (b * nc + n, 0)),
                  pl.BlockSpec(lb_logits.shape, lambda b, n: (0, 0)),
                  pl.BlockSpec(gnorm.shape, lambda b, n: (0, 0)),
                  pl.BlockSpec((mem_len, kv.shape[1]), lambda b, n: (b, 0))],
        out_specs=[pl.BlockSpec((HG_CHUNK, cat_w), lambda b, n: (b * nc + n, 0)),
                   pl.BlockSpec((None, HG_HEADS, HG_DIM, HG_DIM), lambda b, n: (b * nc + n, 0, 0, 0))],
        out_shape=[jax.ShapeDtypeStruct((T, cat_w), BF),
                   jax.ShapeDtypeStruct((bl * nc, HG_HEADS, HG_DIM, HG_DIM), F32)],
        scratch_shapes=[pltpu.VMEM((HG_HEADS, HG_DIM, HG_DIM), F32)],
        name="hgrn_fwd",
        compiler_params=pltpu.CompilerParams(dimension_semantics=("arbitrary", "arbitrary"), vmem_limit_bytes=VMEM_CAP_BYTES),
    )(z, lb_logits, gnorm, kv)


def _hgrn_bwd(z, dcat, stash, lb_logits, gnorm, kv, bl, nc):
    T, zw = z.shape
    mem_len = kv.shape[0] // bl
    cat_w = dcat.shape[1]

    def kern(z_ref, dc_ref, st_ref, lb_ref, gn_ref, kv_ref, dz_ref, dkv_ref, dlb_ref, dgn_ref, ds_scr):
        first = jnp.logical_and(pl.program_id(0) == 0, pl.program_id(1) == 0)

        @pl.when(pl.program_id(1) == 0)
        def _():
            ds_scr[...] = jnp.zeros(ds_scr.shape, F32)
            dkv_ref[...] = jnp.zeros(dkv_ref.shape, F32)

        @pl.when(first)
        def _():
            dlb_ref[...] = jnp.zeros(dlb_ref.shape, F32)
            dgn_ref[...] = jnp.zeros(dgn_ref.shape, F32)

        zq, zf, zi, zg, zx = _hgrn_pieces(z_ref)
        mk, mv = _kv_pieces(kv_ref)
        l0, l1, l2 = _lb_pieces(lb_ref)
        S = [st_ref[h] for h in range(HG_HEADS)]
        _, vjp = jax.vjp(_hgrn_block, zq, zf, zi, zg, zx, l0, l1, l2, gn_ref[...], mk, mv, S)
        d_outs = [dc_ref[:, h * HG_DIM:(h + 1) * HG_DIM] for h in range(HG_HEADS)]
        base = HG_HEADS * HG_DIM
        d_outs += [dc_ref[:, base + a * XA_DIM:base + (a + 1) * XA_DIM] for a in range(XA_HEADS)]
        d_s = [ds_scr[h] for h in range(HG_HEADS)]
        dzq, dzf, dzi, dzg, dzx, dl0, dl1, dl2, dgn, dmk, dmv, dS = vjp((d_outs, d_s))
        W = HG_HEADS * HG_DIM
        for h in range(HG_HEADS):
            sl = slice(h * HG_DIM, (h + 1) * HG_DIM)
            dz_ref[:, sl] = dzq[h].astype(dz_ref.dtype)
            dz_ref[:, W + h * HG_DIM:W + (h + 1) * HG_DIM] = dzf[h].astype(dz_ref.dtype)
            dz_ref[:, 2 * W + h * HG_DIM:2 * W + (h + 1) * HG_DIM] = dzi[h].astype(dz_ref.dtype)
            dz_ref[:, 3 * W + h * HG_DIM:3 * W + (h + 1) * HG_DIM] = dzg[h].astype(dz_ref.dtype)
            ds_scr[h] = dS[h]
            dlb_ref[0:1, sl] += dl0[h]
            dlb_ref[1:2, sl] += dl1[h]
            dlb_ref[2:3, sl] += dl2[h]
        dgn_ref[...] += dgn
        KW = XA_HEADS * XA_DIM
        for a in range(XA_HEADS):
            dz_ref[:, 4 * W + a * XA_DIM:4 * W + (a + 1) * XA_DIM] = dzx[a].astype(dz_ref.dtype)
            dkv_ref[:, a * XA_DIM:(a + 1) * XA_DIM] += dmk[a]
            dkv_ref[:, KW + a * XA_DIM:KW + (a + 1) * XA_DIM] += dmv[a]

    rev = lambda b, n: (b * nc + (nc - 1 - n), 0)
    return pl.pallas_call(
        kern, grid=(bl, nc),
        in_specs=[pl.BlockSpec((HG_CHUNK, zw), rev),
                  pl.BlockSpec((HG_CHUNK, cat_w), rev),
                  pl.BlockSpec((None, HG_HEADS, HG_DIM, HG_DIM), lambda b, n: (b * nc + (nc - 1 - n), 0, 0, 0)),
                  pl.BlockSpec(lb_logits.shape, lambda b, n: (0, 0)),
                  pl.BlockSpec(gnorm.shape, lambda b, n: (0, 0)),
                  pl.BlockSpec((mem_len, kv.shape[1]), lambda b, n: (b, 0))],
        out_specs=[pl.BlockSpec((HG_CHUNK, zw), rev),
                   pl.BlockSpec((mem_len, kv.shape[1]), lambda b, n: (b, 0)),
                   pl.BlockSpec(lb_logits.shape, lambda b, n: (0, 0)),
                   pl.BlockSpec(gnorm.shape, lambda b, n: (0, 0))],
        out_shape=[jax.ShapeDtypeStruct((T, zw), BF), jax.ShapeDtypeStruct(kv.shape, F32),
                   jax.ShapeDtypeStruct(lb_logits.shape, F32), jax.ShapeDtypeStruct(gnorm.shape, F32)],
        scratch_shapes=[pltpu.VMEM((HG_HEADS, HG_DIM, HG_DIM), F32)],
        name="hgrn_bwd",
        compiler_params=pltpu.CompilerParams(dimension_semantics=("arbitrary", "arbitrary"), vmem_limit_bytes=VMEM_CAP_BYTES),
    )(z, dcat, stash, lb_logits, gnorm, kv)


HG_SUB = 4


def _hgrn_rows(z_ref, dtype_cast=None):
    W = HG_HEADS * HG_DIM

    def piece(c, col, w):
        return z_ref[c * HG_CHUNK:(c + 1) * HG_CHUNK, col:col + w]

    zq = [[piece(c, h * HG_DIM, HG_DIM) for h in range(HG_HEADS)] for c in range(HG_SUB)]
    zf = [[piece(c, W + h * HG_DIM, HG_DIM) for h in range(HG_HEADS)] for c in range(HG_SUB)]
    zi = [[piece(c, 2 * W + h * HG_DIM, HG_DIM) for h in range(HG_HEADS)] for c in range(HG_SUB)]
    zg = [[piece(c, 3 * W + h * HG_DIM, HG_DIM) for h in range(HG_HEADS)] for c in range(HG_SUB)]
    zx = [z_ref[:, 4 * W + a * XA_DIM:4 * W + (a + 1) * XA_DIM] for a in range(XA_HEADS)]
    return zq, zf, zi, zg, zx


def _hgrn_steps(zq, zf, zi, zg, zx, l0, l1, l2, gn, mk, mv, S):
    mix = []
    for c in range(HG_SUB):
        row, s_next = [], []
        for h in range(HG_HEADS):
            o, sn = _hgrn_head(zq[c][h], zf[c][h], zi[c][h], zg[c][h], l0[h], l1[h], l2[h], gn, S[h])
            row.append(o)
            s_next.append(sn)
        mix.append(row)
        S = s_next
    att = [_attention(zx[a], mk[a], mv[a]) for a in range(XA_HEADS)]
    return mix, att, S


def _hgrn_fwd2(z, lb_logits, gnorm, kv, bl, seq):
    T, zw = z.shape
    mem_len = kv.shape[0] // bl
    cat_w = HG_HEADS * HG_DIM + XA_HEADS * XA_DIM
    R = HG_SUB * HG_CHUNK
    nb = seq // R

    def kern(z_ref, lb_ref, gn_ref, kv_ref, cat_ref, st_ref, s_scr):
        @pl.when(pl.program_id(1) == 0)
        def _():
            s_scr[...] = jnp.zeros(s_scr.shape, F32)

        st_ref[...] = s_scr[...]
        zq, zf, zi, zg, zx = _hgrn_rows(z_ref)
        mk, mv = _kv_pieces(kv_ref)
        l0, l1, l2 = _lb_pieces(lb_ref)
        S = [s_scr[h] for h in range(HG_HEADS)]
        mix, att, s_new = _hgrn_steps(zq, zf, zi, zg, zx, l0, l1, l2, gn_ref[...], mk, mv, S)
        for c in range(HG_SUB):
            for h in range(HG_HEADS):
                cat_ref[c * HG_CHUNK:(c + 1) * HG_CHUNK, h * HG_DIM:(h + 1) * HG_DIM] = mix[c][h].astype(cat_ref.dtype)
        for h in range(HG_HEADS):
            s_scr[h] = s_new[h]
        base = HG_HEADS * HG_DIM
        for a in range(XA_HEADS):
            cat_ref[:, base + a * XA_DIM:base + (a + 1) * XA_DIM] = att[a].astype(cat_ref.dtype)

    return pl.pallas_call(
        kern, grid=(bl, nb),
        in_specs=[pl.BlockSpec((R, zw), lambda b, n: (b * nb + n, 0)),
                  pl.BlockSpec(lb_logits.shape, lambda b, n: (0, 0)),
                  pl.BlockSpec(gnorm.shape, lambda b, n: (0, 0)),
                  pl.BlockSpec((mem_len, kv.shape[1]), lambda b, n: (b, 0))],
        out_specs=[pl.BlockSpec((R, cat_w), lambda b, n: (b * nb + n, 0)),
                   pl.BlockSpec((None, HG_HEADS, HG_DIM, HG_DIM), lambda b, n: (b * nb + n, 0, 0, 0))],
        out_shape=[jax.ShapeDtypeStruct((T, cat_w), BF),
                   jax.ShapeDtypeStruct((bl * nb, HG_HEADS, HG_DIM, HG_DIM), F32)],
        scratch_shapes=[pltpu.VMEM((HG_HEADS, HG_DIM, HG_DIM), F32)],
        name="hgrn_fwd",
        compiler_params=pltpu.CompilerParams(dimension_semantics=("arbitrary", "arbitrary"), vmem_limit_bytes=VMEM_CAP_BYTES),
    )(z, lb_logits, gnorm, kv)


def _hgrn_bwd2(z, dcat, stash, lb_logits, gnorm, kv, bl, seq):
    T, zw = z.shape
    mem_len = kv.shape[0] // bl
    cat_w = dcat.shape[1]
    R = HG_SUB * HG_CHUNK
    nb = seq // R

    def kern(z_ref, dc_ref, st_ref, lb_ref, gn_ref, kv_ref, dz_ref, dkv_ref, dlb_ref, dgn_ref, ds_scr):
        first = jnp.logical_and(pl.program_id(0) == 0, pl.program_id(1) == 0)

        @pl.when(pl.program_id(1) == 0)
        def _():
            ds_scr[...] = jnp.zeros(ds_scr.shape, F32)
            dkv_ref[...] = jnp.zeros(dkv_ref.shape, F32)

        @pl.when(first)
        def _():
            dlb_ref[...] = jnp.zeros(dlb_ref.shape, F32)
            dgn_ref[...] = jnp.zeros(dgn_ref.shape, F32)

        zq, zf, zi, zg, zx = _hgrn_rows(z_ref)
        mk, mv = _kv_pieces(kv_ref)
        l0, l1, l2 = _lb_pieces(lb_ref)
        S = [st_ref[h] for h in range(HG_HEADS)]
        _, vjp = jax.vjp(_hgrn_steps, zq, zf, zi, zg, zx, l0, l1, l2, gn_ref[...], mk, mv, S)
        d_mix = [[dc_ref[c * HG_CHUNK:(c + 1) * HG_CHUNK, h * HG_DIM:(h + 1) * HG_DIM] for h in range(HG_HEADS)]
                 for c in range(HG_SUB)]
        base = HG_HEADS * HG_DIM
        d_att = [dc_ref[:, base + a * XA_DIM:base + (a + 1) * XA_DIM] for a in range(XA_HEADS)]
        d_s = [ds_scr[h] for h in range(HG_HEADS)]
        dzq, dzf, dzi, dzg, dzx, dl0, dl1, dl2, dgn, dmk, dmv, dS = vjp((d_mix, d_att, d_s))
        W = HG_HEADS * HG_DIM
        for c in range(HG_SUB):
            rows = slice(c * HG_CHUNK, (c + 1) * HG_CHUNK)
            for h in range(HG_HEADS):
                for k, part in enumerate((dzq, dzf, dzi, dzg)):
                    dz_ref[rows, k * W + h * HG_DIM:k * W + (h + 1) * HG_DIM] = part[c][h].astype(dz_ref.dtype)
        for h in range(HG_HEADS):
            sl = slice(h * HG_DIM, (h + 1) * HG_DIM)
            ds_scr[h] = dS[h]
            dlb_ref[0:1, sl] += dl0[h]
            dlb_ref[1:2, sl] += dl1[h]
            dlb_ref[2:3, sl] += dl2[h]
        dgn_ref[...] += dgn
        KW = XA_HEADS * XA_DIM
        for a in range(XA_HEADS):
            dz_ref[:, 4 * W + a * XA_DIM:4 * W + (a + 1) * XA_DIM] = dzx[a].astype(dz_ref.dtype)
            dkv_ref[:, a * XA_DIM:(a + 1) * XA_DIM] += dmk[a]
            dkv_ref[:, KW + a * XA_DIM:KW + (a + 1) * XA_DIM] += dmv[a]

    rev = lambda b, n: (b * nb + (nb - 1 - n), 0)
    return pl.pallas_call(
        kern, grid=(bl, nb),
        in_specs=[pl.BlockSpec((R, zw), rev),
                  pl.BlockSpec((R, cat_w), rev),
                  pl.BlockSpec((None, HG_HEADS, HG_DIM, HG_DIM), lambda b, n: (b * nb + (nb - 1 - n), 0, 0, 0)),
                  pl.BlockSpec(lb_logits.shape, lambda b, n: (0, 0)),
                  pl.BlockSpec(gnorm.shape, lambda b, n: (0, 0)),
                  pl.BlockSpec((mem_len, kv.shape[1]), lambda b, n: (b, 0))],
        out_specs=[pl.BlockSpec((R, zw), rev),
                   pl.BlockSpec((mem_len, kv.shape[1]), lambda b, n: (b, 0)),
                   pl.BlockSpec(lb_logits.shape, lambda b, n: (0, 0)),
                   pl.BlockSpec(gnorm.shape, lambda b, n: (0, 0))],
        out_shape=[jax.ShapeDtypeStruct((T, zw), BF), jax.ShapeDtypeStruct(kv.shape, F32),
                   jax.ShapeDtypeStruct(lb_logits.shape, F32), jax.ShapeDtypeStruct(gnorm.shape, F32)],
        scratch_shapes=[pltpu.VMEM((HG_HEADS, HG_DIM, HG_DIM), F32)],
        name="hgrn_bwd",
        compiler_params=pltpu.CompilerParams(dimension_semantics=("arbitrary", "arbitrary"), vmem_limit_bytes=VMEM_CAP_BYTES),
    )(z, dcat, stash, lb_logits, gnorm, kv)


GM_SUB = 2


def _gmlp_pieces(z_ref):
    W = GM_GROUPS * GM_GROUP_DIM
    zu = [z_ref[:, g * GM_GROUP_DIM:(g + 1) * GM_GROUP_DIM] for g in range(GM_GROUPS)]
    zv = [z_ref[:, W + g * GM_GROUP_DIM:W + (g + 1) * GM_GROUP_DIM] for g in range(GM_GROUPS)]
    zx = [z_ref[:, 2 * W + a * XA_DIM:2 * W + (a + 1) * XA_DIM] for a in range(XA_HEADS)]
    return zu, zv, zx


def _gmlp_params(lng_ref, lnb_ref, ws_ref, bs_ref):
    lng = [lng_ref[:, g * GM_GROUP_DIM:(g + 1) * GM_GROUP_DIM] for g in range(GM_GROUPS)]
    lnb = [lnb_ref[:, g * GM_GROUP_DIM:(g + 1) * GM_GROUP_DIM] for g in range(GM_GROUPS)]
    ws = [ws_ref[g] for g in range(GM_GROUPS)]
    bs = [bs_ref[g:g + 1, :] for g in range(GM_GROUPS)]
    return lng, lnb, ws, bs


def _gmlp_fwd(z, ln_g, ln_b, w_s, b_s, kv, bl, nc):
    T, zw = z.shape
    mem_len = kv.shape[0] // bl
    cat_w = GM_GROUPS * GM_GROUP_DIM + XA_HEADS * XA_DIM

    assert nc % GM_SUB == 0
    nc = nc // GM_SUB
    R = GM_SUB * GM_CHUNK

    def kern(z_ref, lng_ref, lnb_ref, ws_ref, bs_ref, kv_ref, cat_ref):
        lng, lnb, ws, bs = _gmlp_params(lng_ref, lnb_ref, ws_ref, bs_ref)
        mk, mv = _kv_pieces(kv_ref)
        for c in range(GM_SUB):
            rows = pl.ds(c * GM_CHUNK, GM_CHUNK)
            zu, zv, zx = _gmlp_pieces(z_ref.at[rows])
            out = cat_ref.at[rows]
            outs = _gmlp_block(zu, zv, zx, lng, lnb, ws, bs, mk, mv)
            for g in range(GM_GROUPS):
                out[:, g * GM_GROUP_DIM:(g + 1) * GM_GROUP_DIM] = outs[g].astype(cat_ref.dtype)
            base = GM_GROUPS * GM_GROUP_DIM
            for a in range(XA_HEADS):
                out[:, base + a * XA_DIM:base + (a + 1) * XA_DIM] = outs[GM_GROUPS + a].astype(cat_ref.dtype)

    full2 = lambda b, n: (0, 0)
    return pl.pallas_call(
        kern, grid=(bl, nc),
        in_specs=[pl.BlockSpec((R, zw), lambda b, n: (b * nc + n, 0)),
                  pl.BlockSpec(ln_g.shape, full2), pl.BlockSpec(ln_b.shape, full2),
                  pl.BlockSpec(w_s.shape, lambda b, n: (0, 0, 0)), pl.BlockSpec(b_s.shape, full2),
                  pl.BlockSpec((mem_len, kv.shape[1]), lambda b, n: (b, 0))],
        out_specs=pl.BlockSpec((R, cat_w), lambda b, n: (b * nc + n, 0)),
        out_shape=jax.ShapeDtypeStruct((T, cat_w), BF),
        name="gmlp_fwd",
        compiler_params=pltpu.CompilerParams(dimension_semantics=("arbitrary", "arbitrary"), vmem_limit_bytes=VMEM_CAP_BYTES),
    )(z, ln_g, ln_b, w_s, b_s, kv)


def _gmlp_bwd(z, dcat, ln_g, ln_b, w_s, b_s, kv, bl, nc):
    T, zw = z.shape
    mem_len = kv.shape[0] // bl
    cat_w = dcat.shape[1]
    assert nc % GM_SUB == 0
    nc = nc // GM_SUB

    def kern(z_ref, dc_ref, lng_ref, lnb_ref, ws_ref, bs_ref, kv_ref,
             dz_ref, dkv_ref, dlng_ref, dlnb_ref, dws_ref, dbs_ref):
        first = jnp.logical_and(pl.program_id(0) == 0, pl.program_id(1) == 0)

        @pl.when(pl.program_id(1) == 0)
        def _():
            dkv_ref[...] = jnp.zeros(dkv_ref.shape, F32)

        @pl.when(first)
        def _():
            dlng_ref[...] = jnp.zeros(dlng_ref.shape, F32)
            dlnb_ref[...] = jnp.zeros(dlnb_ref.shape, F32)
            dws_ref[...] = jnp.zeros(dws_ref.shape, F32)
            dbs_ref[...] = jnp.zeros(dbs_ref.shape, F32)

        lng, lnb, ws, bs = _gmlp_params(lng_ref, lnb_ref, ws_ref, bs_ref)
        mk, mv = _kv_pieces(kv_ref)
        W = GM_GROUPS * GM_GROUP_DIM
        KW = XA_HEADS * XA_DIM
        for c in range(GM_SUB):
            rows = pl.ds(c * GM_CHUNK, GM_CHUNK)
            zu, zv, zx = _gmlp_pieces(z_ref.at[rows])
            dc, dz = dc_ref.at[rows], dz_ref.at[rows]
            _, vjp = jax.vjp(_gmlp_block, zu, zv, zx, lng, lnb, ws, bs, mk, mv)
            d_outs = [dc[:, g * GM_GROUP_DIM:(g + 1) * GM_GROUP_DIM] for g in range(GM_GROUPS)]
            d_outs += [dc[:, W + a * XA_DIM:W + (a + 1) * XA_DIM] for a in range(XA_HEADS)]
            dzu, dzv, dzx, dlng, dlnb, dws, dbs, dmk, dmv = vjp(d_outs)
            for g in range(GM_GROUPS):
                sl = slice(g * GM_GROUP_DIM, (g + 1) * GM_GROUP_DIM)
                dz[:, sl] = dzu[g].astype(dz_ref.dtype)
                dz[:, W + g * GM_GROUP_DIM:W + (g + 1) * GM_GROUP_DIM] = dzv[g].astype(dz_ref.dtype)
                dlng_ref[:, sl] += dlng[g]
                dlnb_ref[:, sl] += dlnb[g]
                dws_ref[g] += dws[g]
                dbs_ref[g:g + 1, :] += dbs[g]
            for a in range(XA_HEADS):
                dz[:, 2 * W + a * XA_DIM:2 * W + (a + 1) * XA_DIM] = dzx[a].astype(dz_ref.dtype)
                dkv_ref[:, a * XA_DIM:(a + 1) * XA_DIM] += dmk[a]
                dkv_ref[:, KW + a * XA_DIM:KW + (a + 1) * XA_DIM] += dmv[a]

    full2 = lambda b, n: (0, 0)
    full3 = lambda b, n: (0, 0, 0)
    blk = lambda b, n: (b * nc + n, 0)
    return pl.pallas_call(
        kern, grid=(bl, nc),
        in_specs=[pl.BlockSpec((GM_SUB * GM_CHUNK, zw), blk), pl.BlockSpec((GM_SUB * GM_CHUNK, cat_w), blk),
                  pl.BlockSpec(ln_g.shape, full2), pl.BlockSpec(ln_b.shape, full2),
                  pl.BlockSpec(w_s.shape, full3), pl.BlockSpec(b_s.shape, full2),
                  pl.BlockSpec((mem_len, kv.shape[1]), lambda b, n: (b, 0))],
        out_specs=[pl.BlockSpec((GM_SUB * GM_CHUNK, zw), blk),
                   pl.BlockSpec((mem_len, kv.shape[1]), lambda b, n: (b, 0)),
                   pl.BlockSpec(ln_g.shape, full2), pl.BlockSpec(ln_b.shape, full2),
                   pl.BlockSpec(w_s.shape, full3), pl.BlockSpec(b_s.shape, full2)],
        out_shape=[jax.ShapeDtypeStruct((T, zw), BF), jax.ShapeDtypeStruct(kv.shape, F32),
                   jax.ShapeDtypeStruct(ln_g.shape, F32), jax.ShapeDtypeStruct(ln_b.shape, F32),
                   jax.ShapeDtypeStruct(w_s.shape, F32), jax.ShapeDtypeStruct(b_s.shape, F32)],
        name="gmlp_bwd",
        compiler_params=pltpu.CompilerParams(dimension_semantics=("arbitrary", "arbitrary"), vmem_limit_bytes=VMEM_CAP_BYTES),
    )(z, dcat, ln_g, ln_b, w_s, b_s, kv)


def _place():
    x, y, c = lax.axis_index("x"), lax.axis_index("y"), lax.axis_index("c")
    chips = [(1 - x, y), (x, 1 - y), (1 - x, 1 - y)]
    return x, y, c, chips


def _half(ref, kind, e):
    if kind == "col":
        n = ref.shape[1] // 2
        return ref.at[:, pl.ds(pl.multiple_of(e * n, n), n), :]
    n = ref.shape[2] // 2
    return ref.at[:, :, pl.ds(pl.multiple_of(e * n, n), n)]


def _slot(ref, kind, j, n):
    if kind == "col":
        return ref.at[:, :, pl.ds(pl.multiple_of(j * n, n), n)]
    return ref.at[:, pl.ds(pl.multiple_of(j * n, n), n), :]


def _allgather_seq(name, items, cid):
    nt = len(items)
    kinds = [k for (_, k, _) in items]
    out_type = []
    for s, k, l in items:
        L, r, c = s.shape
        lo = L if l is None else 1
        out_type.append(jax.ShapeDtypeStruct((lo, r, 4 * c) if k == "col" else (lo, 4 * r, c), s.dtype))

    def body(*refs):
        sh = [refs[t] if items[t][2] is None else refs[t].at[pl.ds(items[t][2], 1)] for t in range(nt)]
        full = refs[nt:2 * nt]
        loc, s_ici, r_ici, s_d2d, r_d2d = refs[2 * nt:]
        x, y, c, chips = _place()
        own = 2 * x + y
        sibling = (x, y, 1 - c)
        barrier = pltpu.get_barrier_semaphore()
        for peer in [(px, py, c) for (px, py) in chips] + [sibling]:
            pl.semaphore_signal(barrier, inc=1, device_id=peer, device_id_type=MESH)
        pl.semaphore_wait(barrier, 4)
        width = [sh[t].shape[2] if kinds[t] == "col" else sh[t].shape[1] for t in range(nt)]
        started = []
        for t in range(nt):
            mine = pltpu.make_async_copy(sh[t], _slot(full[t], kinds[t], own, width[t]), loc.at[t])
            mine.start()
            started.append(mine)
        sent = []
        for t in range(nt):
            for p, (px, py) in enumerate(chips):
                cp = pltpu.make_async_remote_copy(
                    src_ref=_half(sh[t], kinds[t], c), dst_ref=_half(_slot(full[t], kinds[t], own, width[t]), kinds[t], c),
                    send_sem=s_ici.at[t, p], recv_sem=r_ici.at[t, p], device_id=(px, py, c), device_id_type=MESH)
                cp.start()
                sent.append(cp)
        for t in range(nt):
            for p, (px, py) in enumerate(chips):
                landed = _half(_slot(full[t], kinds[t], 2 * px + py, width[t]), kinds[t], c)
                pltpu.make_async_remote_copy(
                    src_ref=landed, dst_ref=landed, send_sem=s_ici.at[t, p], recv_sem=r_ici.at[t, p],
                    device_id=(px, py, c), device_id_type=MESH).wait_recv()
                fw = pltpu.make_async_remote_copy(
                    src_ref=landed, dst_ref=landed, send_sem=s_d2d.at[t, p], recv_sem=r_d2d.at[t, p],
                    device_id=sibling, device_id_type=MESH)
                fw.start()
                sent.append(fw)
        for t in range(nt):
            for p, (px, py) in enumerate(chips):
                other = _half(_slot(full[t], kinds[t], 2 * px + py, width[t]), kinds[t], 1 - c)
                pltpu.make_async_remote_copy(
                    src_ref=other, dst_ref=other, send_sem=s_d2d.at[t, p], recv_sem=r_d2d.at[t, p],
                    device_id=sibling, device_id_type=MESH).wait_recv()
        for cp in sent:
            cp.wait_send()
        for cp in started:
            cp.wait()

    return pl.kernel(
        body, out_type=out_type, mesh=plsc.ScalarSubcoreMesh(axis_name="seq", num_cores=1),
        scratch_types=[pltpu.SemaphoreType.DMA((nt,)), pltpu.SemaphoreType.DMA((nt, 3)), pltpu.SemaphoreType.DMA((nt, 3)),
                       pltpu.SemaphoreType.DMA((nt, 3)), pltpu.SemaphoreType.DMA((nt, 3))],
        compiler_params=pltpu.CompilerParams(collective_id=cid), name=name,
    )(*[s for (s, _, _) in items])


def _slot2(ref, kind, j, n):
    if kind == "col":
        return ref.at[:, pl.ds(pl.multiple_of(j * n, n), n)]
    return ref.at[pl.ds(pl.multiple_of(j * n, n), n), :]


def _rs_chips_seq(name, parts, kinds, cid):
    nm = len(parts)
    out_type = []
    for g, k in zip(parts, kinds):
        r, c = g.shape
        ps = (r, c // 4) if k == "col" else (r // 4, c)
        out_type += [jax.ShapeDtypeStruct(ps, BF), jax.ShapeDtypeStruct((3,) + ps, BF)]

    def body(*refs):
        g = refs[:nm]
        outs = refs[nm:3 * nm]
        loc, ssem, rsem = refs[3 * nm:]
        x, y, c, chips = _place()
        own = 2 * x + y
        barrier = pltpu.get_barrier_semaphore()
        for (px, py) in chips:
            pl.semaphore_signal(barrier, inc=1, device_id=(px, py, c), device_id_type=MESH)
        pl.semaphore_wait(barrier, 3)
        cps = []
        for m in range(nm):
            k = kinds[m]
            own_o, got_o = outs[2 * m], outs[2 * m + 1]
            n = g[m].shape[1] // 4 if k == "col" else g[m].shape[0] // 4
            lc = pltpu.make_async_copy(_slot2(g[m], k, own, n), own_o, loc.at[m])
            lc.start()
            cps.append(lc)
            for p, (px, py) in enumerate(chips):
                cp = pltpu.make_async_remote_copy(
                    src_ref=_slot2(g[m], k, 2 * px + py, n), dst_ref=got_o.at[p],
                    send_sem=ssem.at[m, p], recv_sem=rsem.at[m, p], device_id=(px, py, c), device_id_type=MESH)
                cp.start()
                cps.append(cp)
        for cp in cps:
            cp.wait()

    return pl.kernel(
        body, out_type=out_type, mesh=plsc.ScalarSubcoreMesh(axis_name="seq", num_cores=1),
        scratch_types=[pltpu.SemaphoreType.DMA((nm,)), pltpu.SemaphoreType.DMA((nm, 3)), pltpu.SemaphoreType.DMA((nm, 3))],
        compiler_params=pltpu.CompilerParams(collective_id=cid), name=name,
    )(*parts)


def _finish_share(name, own, got, kind, c_arr):
    L, r, c = own.shape
    tr = _pick(r, 128 if kind == "col" else 256)
    nb = r // tr
    nq = L * nb
    own2 = own.reshape(L * r, c)
    got2 = got.reshape(3 * L * r, c)
    pick = lambda h, q: q * (1 - h) + (nq - 1) * h
    in_specs = [pl.BlockSpec((tr, c), lambda h, q, cc: (pick(h, q), 0))]
    in_specs += [pl.BlockSpec((tr, c), functools.partial(lambda h, q, cc, p: (p * nq + pick(h, q), 0), p=p)) for p in range(3)]
    if kind == "col":
        out_sd = (L, 2, r, c)
        o_spec = pl.BlockSpec((None, 2, tr, c), lambda h, q, cc: ((q * h) // nb, 0, (q * h) % nb, 0))
    else:
        out_sd = (L * r, 2 * c)
        o_spec = pl.BlockSpec((tr, 2 * c), lambda h, q, cc: (q * h, 0))

    def kern(c_ref, o_ref, g0, g1, g2, out_ref, mine, recv, ssem, rsem):
        h, q = pl.program_id(0), pl.program_id(1)
        x, y, cc, _ = _place()

        def swap(qq):
            return pltpu.make_async_remote_copy(src_ref=mine.at[qq], dst_ref=recv.at[qq], send_sem=ssem.at[qq],
                                                recv_sem=rsem.at[qq], device_id=(x, y, 1 - cc), device_id_type=MESH)

        @pl.when(h == 0)
        def _():
            mine[q] = ((o_ref[...].astype(F32) + g0[...].astype(F32)) + g1[...].astype(F32)) + g2[...].astype(F32)
            swap(q).start()

        @pl.when(h == 1)
        def _():
            swap(q).wait()
            a, b = mine[q], recv[q]
            first = c_ref[0] == 0
            lo, hi = jnp.where(first, a, b), jnp.where(first, b, a)
            if kind == "col":
                out_ref[0] = lo
                out_ref[1] = hi
            else:
                out_ref[:, :c] = lo
                out_ref[:, c:] = hi

    est = 2 * nq * tr * c * 4 + 6 * tr * c * 4 + 8 * tr * c * 2
    full = pl.pallas_call(
        kern,
        grid_spec=pltpu.PrefetchScalarGridSpec(
            num_scalar_prefetch=1, grid=(2, nq), in_specs=in_specs, out_specs=o_spec,
            scratch_shapes=[pltpu.VMEM((nq, tr, c), F32), pltpu.VMEM((nq, tr, c), F32),
                            pltpu.SemaphoreType.DMA((nq,)), pltpu.SemaphoreType.DMA((nq,))]),
        out_shape=jax.ShapeDtypeStruct(out_sd, F32), name=name,
        compiler_params=pltpu.CompilerParams(dimension_semantics=("arbitrary", "arbitrary"),
                                             vmem_limit_bytes=VMEM_CAP_BYTES),
    )(c_arr, own2, got2, got2, got2)
    return full.reshape(L, 2 * r, c) if kind == "col" else full.reshape(L, r, 2 * c)


def _small_allreduce(buf, name):
    R = buf.shape[0]
    assert R % 16 == 0
    h = R // 2

    def body(x_ref, o_ref, sib, csum, got, s_a, r_a, s_b, r_b, s_c, r_c):
        x, y, c, chips = _place()
        sibling = (x, y, 1 - c)
        own = 2 * x + y
        swap = pltpu.make_async_remote_copy(src_ref=x_ref, dst_ref=sib, send_sem=s_a, recv_sem=r_a,
                                            device_id=sibling, device_id_type=MESH)
        swap.start()
        swap.wait()
        a, b = x_ref[...], sib[...]
        south = c == 0
        csum[...] = jnp.where(south, a, b) + jnp.where(south, b, a)
        lo = pl.multiple_of(c * h, 8)
        mine = csum.at[pl.ds(lo, h)]
        got[own] = csum[pl.ds(lo, h)]
        sends = []
        for p, (px, py) in enumerate(chips):
            cp = pltpu.make_async_remote_copy(src_ref=mine, dst_ref=got.at[own], send_sem=s_b.at[p], recv_sem=r_b.at[p],
                                              device_id=(px, py, c), device_id_type=MESH)
            cp.start()
            sends.append(cp)
        for cp in sends:
            cp.wait()
        o_ref[pl.ds(lo, h)] = ((got[0] + got[1]) + got[2]) + got[3]
        done = o_ref.at[pl.ds(lo, h)]
        back = pltpu.make_async_remote_copy(src_ref=done, dst_ref=done, send_sem=s_c, recv_sem=r_c,
                                            device_id=sibling, device_id_type=MESH)
        back.start()
        back.wait_send()
        other = o_ref.at[pl.ds(pl.multiple_of((1 - c) * h, 8), h)]
        pltpu.make_async_remote_copy(src_ref=other, dst_ref=other, send_sem=s_c, recv_sem=r_c,
                                     device_id=sibling, device_id_type=MESH).wait_recv()

    vm = pl.BlockSpec(memory_space=pltpu.VMEM)
    return pl.pallas_call(
        body, out_shape=jax.ShapeDtypeStruct(buf.shape, F32), in_specs=[vm], out_specs=vm,
        scratch_shapes=[pltpu.VMEM((R, LANES), F32), pltpu.VMEM((R, LANES), F32), pltpu.VMEM((4, h, LANES), F32),
                        pltpu.SemaphoreType.DMA, pltpu.SemaphoreType.DMA, pltpu.SemaphoreType.DMA((3,)),
                        pltpu.SemaphoreType.DMA((3,)), pltpu.SemaphoreType.DMA, pltpu.SemaphoreType.DMA],
        name=name,
        compiler_params=pltpu.CompilerParams(vmem_limit_bytes=VMEM_CAP_BYTES),
    )(buf)


PACK_TILE_ROWS = 8


def _item_rows(shape):
    n = 1
    for d in shape:
        n *= d
    return -(-n // (PACK_TILE_ROWS * LANES)) * PACK_TILE_ROWS


def _pack(arrs, rows_total):
    buf = jnp.zeros((rows_total, LANES), F32)
    r = 0
    for a in arrs:
        f = a.reshape(-1).astype(F32)
        nr = _item_rows(a.shape)
        block = jnp.pad(f, (0, nr * LANES - f.shape[0])).reshape(nr, LANES)
        buf = lax.dynamic_update_slice(buf, block, (r, 0))
        r += nr
    return buf


def _unpack(buf, shapes):
    out, r = [], 0
    for s in shapes:
        n = 1
        for d in s:
            n *= d
        nr = _item_rows(s)
        out.append(buf[r:r + nr].reshape(-1)[:n].reshape(s))
        r += nr
    return out


def _rows_needed(shapes):
    return -(-sum(_item_rows(s) for s in shapes) // (2 * PACK_TILE_ROWS)) * (2 * PACK_TILE_ROWS)


def _two_rows(a, b):
    out = jnp.zeros((2, a.shape[1]), a.dtype)
    return lax.dynamic_update_slice(lax.dynamic_update_slice(out, a, (0, 0)), b, (1, 0))


def _adam(w, g, m, v):
    m = ADAM_B1 * m + (1.0 - ADAM_B1) * g
    v = ADAM_B2 * v + (1.0 - ADAM_B2) * jnp.square(g)
    m_hat = m / (1.0 - ADAM_B1 ** ADAM_STEP)
    v_hat = v / (1.0 - ADAM_B2 ** ADAM_STEP)
    delta = -ADAM_LR * (m_hat / (jnp.sqrt(v_hat) + ADAM_EPS) + ADAM_WD * w)
    return delta, m, v


def _adam_call(name, w2, g2, m2, v2, tr):
    def fn(rv, cv):
        return list(_adam(*rv)), []

    width = w2.shape[1]
    return _rowcall(name, fn, [(w2, 0, width), (g2, 0, width), (m2, 0, width), (v2, 0, width)], [],
                    [(width, F32)] * 3, [], tr)


def kernel(x, mem, mem_norm, lb_logits, ffn1_norm, ffn1_w_in, ffn1_w_out, mix_norm, mem_w_kv, hgrn_w_in, hgrn_gnorm, hgrn_w_out, gmlp_w_in, gmlp_ln_g, gmlp_ln_b, gmlp_w_s, gmlp_b_s, gmlp_w_out, ffn2_norm, ffn2_w_in, ffn2_w_out, final_norm, loss_target, m_mem_norm, m_lb_logits, m_ffn1_norm, m_ffn1_w_in, m_ffn1_w_out, m_mix_norm, m_mem_w_kv, m_hgrn_w_in, m_hgrn_gnorm, m_hgrn_w_out, m_gmlp_w_in, m_gmlp_ln_g, m_gmlp_ln_b, m_gmlp_w_s, m_gmlp_b_s, m_gmlp_w_out, m_ffn2_norm, m_ffn2_w_in, m_ffn2_w_out, m_final_norm, v_mem_norm, v_lb_logits, v_ffn1_norm, v_ffn1_w_in, v_ffn1_w_out, v_mix_norm, v_mem_w_kv, v_hgrn_w_in, v_hgrn_gnorm, v_hgrn_w_out, v_gmlp_w_in, v_gmlp_ln_g, v_gmlp_ln_b, v_gmlp_w_s, v_gmlp_b_s, v_gmlp_w_out, v_ffn2_norm, v_ffn2_w_in, v_ffn2_w_out, v_final_norm):
    bl, seq, D = x.shape
    T = bl * seq
    mem_len = mem.shape[1]
    chip = 2 * lax.axis_index("x") + lax.axis_index("y")
    c_arr = lax.axis_index("c").astype(jnp.int32).reshape(1)
    TR = 1024

    big = [("ffn1_w_in", ffn1_w_in, "col"), ("ffn1_w_out", ffn1_w_out, "row"), ("mem_w_kv", mem_w_kv, "col"),
           ("hgrn_w_in", hgrn_w_in, "col"), ("hgrn_w_out", hgrn_w_out, "row"), ("gmlp_w_in", gmlp_w_in, "col"),
           ("gmlp_w_out", gmlp_w_out, "row"), ("ffn2_w_in", ffn2_w_in, "col"), ("ffn2_w_out", ffn2_w_out, "row")]
    kinds = [k for (_, _, k) in big]
    shards_bf = []
    for nm, w, _ in big:
        L, r, c = w.shape
        (wb,) = _rowcall("cast_" + nm, lambda rv, cv: ([rv[0]], []), [(w.reshape(L * r, c), 0, c)], [], [(c, BF)], [], 512)
        shards_bf.append(wb.reshape(L, r, c))
    sb = dict(zip([nm for (nm, _, _) in big], shards_bf))
    groups = [[("ffn1_w_in", 0)], [("ffn1_w_out", 0)], [("hgrn_w_in", None)], [("mem_w_kv", None)], [("hgrn_w_out", None)],
              [("ffn2_w_in", 0), ("ffn2_w_out", 0)],
              [("ffn1_w_in", 1), ("ffn1_w_out", 1)],
              [("gmlp_w_in", None), ("gmlp_w_out", None)],
              [("ffn2_w_in", 1), ("ffn2_w_out", 1)]]
    kind_of = {nm: k for (nm, _, k) in big}
    gathered = {nm: [None, None] for nm in ("ffn1_w_in", "ffn1_w_out", "ffn2_w_in", "ffn2_w_out")}
    for gi, grp in enumerate(groups):
        outs = _allgather_seq("gather_%d" % gi, [(sb[nm], kind_of[nm], l) for (nm, l) in grp], gi)
        for (nm, l), o in zip(grp, outs):
            if l is None:
                gathered[nm] = o
            else:
                gathered[nm][l] = o

    ln_w = GM_GROUPS * GM_GROUP_DIM
    placed = lax.dynamic_update_slice(jnp.zeros((8, ln_w), F32), 0.5 * gmlp_ln_g, (0, chip * gmlp_ln_g.shape[1]))
    placed = lax.dynamic_update_slice(placed, 0.5 * gmlp_ln_b, (1, chip * gmlp_ln_g.shape[1]))
    ln_full = _small_allreduce(placed.reshape(16, LANES), "gather_ln").reshape(8, ln_w)
    ln_g_full, ln_b_full = ln_full[0:1], ln_full[1:2]

    def rms_fwd(name, xin, g):
        (h,) = _rowcall(name, lambda rv, cv: ([_rmsnorm(rv[0], cv[0])], []), [(xin, 0, D)], [g.reshape(1, D)], [(D, BF)], [], TR)
        return h

    def ffn_fwd(tag, xin, g, w_in, w_out, layer):
        dff = w_out[layer].shape[1]
        h = rms_fwd("rms_" + tag, xin, g)
        zg, zu, a = _ffn_in_swiglu("ffn_in_" + tag, h, w_in[layer], 2048, 256)
        xo = _mm("ffn_out_" + tag, a, w_out[layer], "nn", F32, 1024, 1024, dff, scale=0.5, res=xin, b_lead=0)
        return xo, (xin, h, zg, zu, a)

    def ffn_bwd(tag, dxo, saved, g, w_in, w_out, layer):
        xin, h, zg, zu, a = saved
        dff = w_out[layer].shape[1]
        dw_out = _mm_tn_pair("ffn_dwo_" + tag, a, dxo, "row", c_arr, dff // 2, T, scale=0.5)
        dz = _ffn_da_swiglu("ffn_da_" + tag, dxo, w_out[layer], zg, zu, 512)
        dw_in = _mm_tn_pair("ffn_dwi_" + tag, h, dz, "col", c_arr, 512, T)
        dx, dg = _mm_dh_rms("ffn_dh_" + tag, dz, w_in[layer], xin, g.reshape(1, D), dxo, 512)
        return dx, dg, dw_in, dw_out

    def rms_bwd(name, xin, g, dh, dres):
        def fn(rv, cv):
            _, vjp = jax.vjp(_rmsnorm, rv[0], cv[0])
            dx, dg = vjp(rv[1])
            if dres is not None:
                dx = dx + rv[2]
            return [dx], [dg]

        rows = [(xin, 0, D), (dh, 0, D)] + ([(dres, 0, D)] if dres is not None else [])
        dx, dg = _rowcall(name, fn, rows, [g.reshape(1, D)], [(D, F32)], [((1, D), F32)], TR)
        return dx, dg

    x0 = x.reshape(T, D)
    tgt = loss_target.reshape(T, D)
    mem2 = mem.reshape(bl * mem_len, D)
    memn = rms_fwd("rms_mem", mem2, mem_norm)

    x1, sv_f10 = ffn_fwd("f1l0", x0, ffn1_norm[0], gathered["ffn1_w_in"], gathered["ffn1_w_out"], 0)
    h_m0 = rms_fwd("rms_mix0", x1, mix_norm[0])
    z_m0 = _mm("mix_in_0", h_m0, gathered["hgrn_w_in"], "nn", F32, 2048, 512, D, b_lead=0)
    kv = [_mm("kv_%d" % i, memn, gathered["mem_w_kv"], "nn", F32, 512, 512, D, b_lead=i) for i in range(2)]
    cat0, stash0 = _hgrn_fwd2(z_m0, lb_logits, hgrn_gnorm, kv[0], bl, seq)
    x2 = _mm("mix_out_0", cat0, gathered["hgrn_w_out"], "nn", F32, 1024, 1024, cat0.shape[1], res=x1, b_lead=0)
    x3, sv_f20 = ffn_fwd("f2l0", x2, ffn2_norm[0], gathered["ffn2_w_in"], gathered["ffn2_w_out"], 0)
    x4, sv_f11 = ffn_fwd("f1l1", x3, ffn1_norm[1], gathered["ffn1_w_in"], gathered["ffn1_w_out"], 1)
    h_m1 = rms_fwd("rms_mix1", x4, mix_norm[1])
    z_m1 = _mm("mix_in_1", h_m1, gathered["gmlp_w_in"], "nn", F32, 2048, 512, D, b_lead=0)
    nc1 = seq // GM_CHUNK
    w_s, b_s = gmlp_w_s[0], gmlp_b_s[0]
    cat1 = _gmlp_fwd(z_m1, ln_g_full, ln_b_full, w_s, b_s, kv[1], bl, nc1)
    x5 = _mm("mix_out_1", cat1, gathered["gmlp_w_out"], "nn", F32, 1024, 1024, cat1.shape[1], res=x4, b_lead=0)
    x6, sv_f21 = ffn_fwd("f2l1", x5, ffn2_norm[1], gathered["ffn2_w_in"], gathered["ffn2_w_out"], 1)

    def head(rv, cv):
        def f(xx, gg):
            err = _rmsnorm(xx, gg) - rv[1]
            return 0.5 * jnp.sum(jnp.mean(err * err, axis=-1, keepdims=True), axis=0, keepdims=True)

        ls, vjp = jax.vjp(f, rv[0], cv[0])
        dx, dg = vjp(jnp.ones((1, 1), F32))
        return [dx], [dg, jnp.broadcast_to(ls, (1, 128))]

    dx6, d_final, loss_part = _rowcall("loss_head", head, [(x6, 0, D), (tgt, 0, D)], [final_norm.reshape(1, D)],
                                       [(D, F32)], [((1, D), F32), ((1, 128), F32)], TR)

    rs_out = {}
    n_gather = len(groups)

    def rs(gi, items):
        outs = _rs_chips_seq("reduce_%d" % gi, [p for (_, p, _) in items], [k for (_, _, k) in items], n_gather + gi)
        for i, (key, _, _) in enumerate(items):
            rs_out[key] = (outs[2 * i], outs[2 * i + 1])

    dx5, dg_f21, dwi_f21, dwo_f21 = ffn_bwd("f2l1", dx6, sv_f21, ffn2_norm[1], gathered["ffn2_w_in"], gathered["ffn2_w_out"], 1)
    rs(0, [(("ffn2_w_out", 1), dwo_f21, "row"), (("ffn2_w_in", 1), dwi_f21, "col")])
    dcat1 = _mm("mix_dcat_1", dx5, gathered["gmlp_w_out"], "nt", F32, 2048, 1024, D, b_lead=0)
    dwo_m1 = _mm_tn_pair("mix_dwo_1", cat1, dx5, "row", c_arr, 1024, T)
    dz_m1, dkv1, d_lng, d_lnb, d_ws, d_bs = _gmlp_bwd(z_m1, dcat1, ln_g_full, ln_b_full, w_s, b_s, kv[1], bl, nc1)
    dx4, dg_m1 = _mm_dh_rms("mix_dh_1", dz_m1, gathered["gmlp_w_in"], x4, mix_norm[1].reshape(1, D), dx5, 512)
    dwi_m1 = _mm_tn_pair("mix_dwi_1", h_m1, dz_m1, "col", c_arr, 1024, T)
    rs(1, [(("gmlp_w_out", 0), dwo_m1, "row"), (("gmlp_w_in", 0), dwi_m1, "col")])
    dx3, dg_f11, dwi_f11, dwo_f11 = ffn_bwd("f1l1", dx4, sv_f11, ffn1_norm[1], gathered["ffn1_w_in"], gathered["ffn1_w_out"], 1)
    rs(2, [(("ffn1_w_out", 1), dwo_f11, "row"), (("ffn1_w_in", 1), dwi_f11, "col")])

    dx2, dg_f20, dwi_f20, dwo_f20 = ffn_bwd("f2l0", dx3, sv_f20, ffn2_norm[0], gathered["ffn2_w_in"], gathered["ffn2_w_out"], 0)
    rs(3, [(("ffn2_w_out", 0), dwo_f20, "row"), (("ffn2_w_in", 0), dwi_f20, "col")])
    dcat0 = _mm("mix_dcat_0", dx2, gathered["hgrn_w_out"], "nt", F32, 2048, 1024, D, b_lead=0)
    dwo_m0 = _mm_tn_pair("mix_dwo_0", cat0, dx2, "row", c_arr, 1024, T)
    dz_m0, dkv0, d_lb, d_gn = _hgrn_bwd2(z_m0, dcat0, stash0, lb_logits, hgrn_gnorm, kv[0], bl, seq)
    dx1, dg_m0 = _mm_dh_rms("mix_dh_0", dz_m0, gathered["hgrn_w_in"], x1, mix_norm[0].reshape(1, D), dx2, 512)
    dwi_m0 = _mm_tn_pair("mix_dwi_0", h_m0, dz_m0, "col", c_arr, 1024, T)
    rs(4, [(("hgrn_w_out", 0), dwo_m0, "row"), (("hgrn_w_in", 0), dwi_m0, "col")])

    dwkv = [_mm_tn_pair("kv_dw_%d" % i, memn, dkv, "col", c_arr, 1024, 512) for i, dkv in enumerate([dkv0, dkv1])]
    rs(5, [(("mem_w_kv", 0), dwkv[0], "col"), (("mem_w_kv", 1), dwkv[1], "col")])
    dmemn = _mm("kv_dx_0", dkv0, gathered["mem_w_kv"], "nt", F32, 512, 512, 1024, b_lead=0)
    dmemn = _mm("kv_dx_1", dkv1, gathered["mem_w_kv"], "nt", F32, 512, 512, 1024, res=dmemn, b_lead=1)
    _, d_memnorm = rms_bwd("rms_bwd_mem", mem2, mem_norm, dmemn, None)

    dx0, dg_f10, dwi_f10, dwo_f10 = ffn_bwd("f1l0", dx1, sv_f10, ffn1_norm[0], gathered["ffn1_w_in"], gathered["ffn1_w_out"], 0)
    rs(6, [(("ffn1_w_out", 0), dwo_f10, "row")])
    rs(7, [(("ffn1_w_in", 0), dwi_f10, "col")])

    shard_grads = []
    for (nm, w, k) in big:
        per_layer = []
        for l in range(w.shape[0]):
            own, got = rs_out[(nm, l)]
            per_layer.append(_finish_share("finish_%s_%d" % (nm, l), own[None], got[:, None], k, c_arr))
        shard_grads.append(per_layer[0] if len(per_layer) == 1 else jnp.concatenate(per_layer, axis=0))

    big_w = [w for (_, w, _) in big]
    big_m = [m_ffn1_w_in, m_ffn1_w_out, m_mem_w_kv, m_hgrn_w_in, m_hgrn_w_out, m_gmlp_w_in, m_gmlp_w_out, m_ffn2_w_in, m_ffn2_w_out]
    big_v = [v_ffn1_w_in, v_ffn1_w_out, v_mem_w_kv, v_hgrn_w_in, v_hgrn_w_out, v_gmlp_w_in, v_gmlp_w_out, v_ffn2_w_in, v_ffn2_w_out]
    big_out = {}
    for (nm, w, _), g, m, v in zip(big, shard_grads, big_m, big_v):
        L, r, c = w.shape
        d2, m2, v2 = _adam_call("adam_" + nm, w.reshape(L * r, c), g.reshape(L * r, c), m.reshape(L * r, c),
                                v.reshape(L * r, c), 256)
        big_out[nm] = (g, d2.reshape(w.shape), m2.reshape(w.shape), v2.reshape(w.shape))

    d_ffn1n = _two_rows(dg_f10, dg_f11)
    d_mixn = _two_rows(dg_m0, dg_m1)
    d_ffn2n = _two_rows(dg_f20, dg_f21)
    small_parts = [loss_part[:, :1], d_memnorm, d_lb, d_ffn1n, d_mixn, d_gn, d_lng, d_lnb, d_ws, d_bs, d_ffn2n, d_final]
    red_shapes = [(1,), mem_norm.shape, lb_logits.shape, ffn1_norm.shape, mix_norm.shape, hgrn_gnorm.shape, (1, ln_w), (1, ln_w),
                  gmlp_w_s.shape, gmlp_b_s.shape, ffn2_norm.shape, final_norm.shape]
    red = _small_allreduce(_pack(small_parts, _rows_needed(red_shapes)), "reduce_small")
    (loss_v, g_memn, g_lb, g_f1n, g_mixn, g_gn, g_lng_full, g_lnb_full, g_ws, g_bs, g_f2n, g_fin) = _unpack(red, red_shapes)
    lsh = gmlp_ln_g.shape[1]
    g_lng = lax.dynamic_slice(g_lng_full, (0, chip * lsh), (1, lsh))
    g_lnb = lax.dynamic_slice(g_lnb_full, (0, chip * lsh), (1, lsh))
    small_w = [mem_norm, lb_logits, ffn1_norm, mix_norm, hgrn_gnorm, gmlp_ln_g, gmlp_ln_b, gmlp_w_s, gmlp_b_s, ffn2_norm, final_norm]
    small_g = [g_memn, g_lb, g_f1n, g_mixn, g_gn, g_lng, g_lnb, g_ws, g_bs, g_f2n, g_fin]
    small_m = [m_mem_norm, m_lb_logits, m_ffn1_norm, m_mix_norm, m_hgrn_gnorm, m_gmlp_ln_g, m_gmlp_ln_b, m_gmlp_w_s, m_gmlp_b_s, m_ffn2_norm, m_final_norm]
    small_v = [v_mem_norm, v_lb_logits, v_ffn1_norm, v_mix_norm, v_hgrn_gnorm, v_gmlp_ln_g, v_gmlp_ln_b, v_gmlp_w_s, v_gmlp_b_s, v_ffn2_norm, v_final_norm]
    sshapes = [w.shape for w in small_w]
    nrow = _rows_needed(sshapes)
    d_p, m_p, v_p = _adam_call("adam_small", _pack(small_w, nrow), _pack(small_g, nrow), _pack(small_m, nrow), _pack(small_v, nrow), nrow)
    s_delta, s_m, s_v = _unpack(d_p, sshapes), _unpack(m_p, sshapes), _unpack(v_p, sshapes)
    small_names = ["mem_norm", "lb_logits", "ffn1_norm", "mix_norm", "hgrn_gnorm", "gmlp_ln_g", "gmlp_ln_b", "gmlp_w_s", "gmlp_b_s", "ffn2_norm", "final_norm"]
    small_out = {nm: (g.reshape(w.shape), d, m, v) for nm, w, g, d, m, v in zip(small_names, small_w, small_g, s_delta, s_m, s_v)}

    order = ["mem_norm", "lb_logits", "ffn1_norm", "ffn1_w_in", "ffn1_w_out", "mix_norm", "mem_w_kv", "hgrn_w_in", "hgrn_gnorm",
             "hgrn_w_out", "gmlp_w_in", "gmlp_ln_g", "gmlp_ln_b", "gmlp_w_s", "gmlp_b_s", "gmlp_w_out", "ffn2_norm", "ffn2_w_in",
             "ffn2_w_out", "final_norm"]
    allo = {**big_out, **small_out}
    grad_x = dx0.reshape(x.shape)
    return (loss_v.reshape(()), grad_x, *[allo[n][0] for n in order], *[allo[n][1] for n in order],
            *[allo[n][2] for n in order], *[allo[n][3] for n in order])
```

```python
import functools

import jax
import jax.numpy as jnp
from jax import lax
from jax.experimental import pallas as pl
from jax.experimental.pallas import tpu as pltpu
from jax.experimental.pallas import tpu_sc as plsc

BF = jnp.bfloat16
F32 = jnp.float32
MESH = pl.DeviceIdType.MESH

EPS = 1e-6
D_MODEL = 1024
HG_HEADS = 8
HG_DIM = 128
HG_CHUNK = 64
GM_CHUNK = 128
GM_GROUPS = 8
GM_GROUP_DIM = 256
XA_HEADS = 4
XA_DIM = 256
ADAM_LR = 0.001
ADAM_B1 = 0.9
ADAM_B2 = 0.999
ADAM_EPS = 1e-08
ADAM_WD = 0.01
ADAM_STEP = 10

VMEM_CAP_BYTES = 60 * 1024 * 1024
LANES = 1024


def _pick(n, cap, mult=16):
    if n <= cap:
        return n
    for d in range(cap - cap % mult, 0, -mult):
        if n % d == 0:
            return d
    raise ValueError((n, cap, mult))


def _dg(a, b, ca, cb):
    return lax.dot_general(a.astype(BF), b.astype(BF), (((ca,), (cb,)), ((), ())), preferred_element_type=F32)


@jax.custom_vjp
def dot_nn(a, b):
    return _dg(a, b, 1, 0)


def _nn_fwd(a, b):
    return _dg(a, b, 1, 0), (a, b)


def _nn_bwd(r, g):
    a, b = r
    return _dg(g, b, 1, 1), _dg(a, g, 0, 0)


dot_nn.defvjp(_nn_fwd, _nn_bwd)


@jax.custom_vjp
def dot_nt(a, b):
    return _dg(a, b, 1, 1)


def _nt_fwd(a, b):
    return _dg(a, b, 1, 1), (a, b)


def _nt_bwd(r, g):
    a, b = r
    return _dg(g, b, 1, 0), _dg(g, a, 0, 0)


dot_nt.defvjp(_nt_fwd, _nt_bwd)


@jax.custom_vjp
def dot_tn(a, b):
    return _dg(a, b, 0, 0)


def _tn_fwd(a, b):
    return _dg(a, b, 0, 0), (a, b)


def _tn_bwd(r, g):
    a, b = r
    return _dg(b, g, 1, 1), _dg(a, g, 1, 0)


dot_tn.defvjp(_tn_fwd, _tn_bwd)


def _rmsnorm(x, g):
    return x * lax.rsqrt(jnp.mean(x * x, axis=-1, keepdims=True) + EPS) * g


def _silu(x):
    return x * jax.nn.sigmoid(x)


def _gelu(x):
    return 0.5 * x * (1.0 + lax.erf(x * (0.5 ** 0.5)))


def _softmax_last(s):
    m = lax.stop_gradient(jnp.max(s, axis=-1, keepdims=True))
    e = jnp.exp(s - m)
    return e / jnp.sum(e, axis=-1, keepdims=True)


def _tril(n):
    r = lax.broadcasted_iota(jnp.int32, (n, n), 0)
    c = lax.broadcasted_iota(jnp.int32, (n, n), 1)
    return r >= c


def _cumsum_rows(l):
    n = l.shape[0]
    return lax.dot_general(_tril(n).astype(F32), l, (((1,), (0,)), ((), ())),
                           precision=lax.Precision.HIGHEST, preferred_element_type=F32)


def _attention(zx, mk, mv):
    s = dot_nt(zx, mk) * (XA_DIM ** -0.5)
    return dot_nn(_softmax_last(s), mv)


def _hgrn_head(zq, zf, zi, zg, l0, l1, l2, gn, S):
    m = lax.stop_gradient(jnp.maximum(jnp.maximum(l0, l1), l2))
    e0 = jnp.exp(l0 - m)
    lb = e0 / (e0 + jnp.exp(l1 - m) + jnp.exp(l2 - m))
    q = _silu(zq)
    f = lb + (1.0 - lb) * jax.nn.sigmoid(zf)
    k = 1.0 - f
    b = _cumsum_rows(jnp.log(f))
    b_last = b[HG_CHUNK - 1:HG_CHUNK, :]
    q_dec = q * jnp.exp(b)
    k_inv = k * jnp.exp(-b)
    a = jnp.where(_tril(HG_CHUNK), dot_nt(q_dec, k_inv), 0.0)
    o = dot_nn(a, zi) + dot_nn(q_dec, S)
    S_new = jnp.exp(b_last).reshape(HG_DIM, 1) * S + dot_tn(k * jnp.exp(b_last - b), zi)
    o = _rmsnorm(o, gn) * _silu(zg)
    return o, S_new


def _hgrn_block(zq, zf, zi, zg, zx, l0, l1, l2, gn, mk, mv, S):
    outs, s_new = [], []
    for h in range(HG_HEADS):
        o, sn = _hgrn_head(zq[h], zf[h], zi[h], zg[h], l0[h], l1[h], l2[h], gn, S[h])
        outs.append(o)
        s_new.append(sn)
    for a in range(XA_HEADS):
        outs.append(_attention(zx[a], mk[a], mv[a]))
    return outs, s_new


def _gmlp_block(zu, zv, zx, lng, lnb, ws, bs, mk, mv):
    gv = [_gelu(v) for v in zv]
    width = GM_GROUPS * GM_GROUP_DIM
    mu = sum(jnp.sum(g, axis=-1, keepdims=True) for g in gv) / width
    xc = [g - mu for g in gv]
    var = sum(jnp.sum(c * c, axis=-1, keepdims=True) for c in xc) / width
    r = lax.rsqrt(var + EPS)
    outs = []
    for g in range(GM_GROUPS):
        v = xc[g] * r * lng[g] + lnb[g]
        w = jnp.where(_tril(GM_CHUNK), ws[g], 0.0)
        mixed = dot_nn(w, v) + bs[g].reshape(GM_CHUNK, 1)
        outs.append(_gelu(zu[g]) * mixed)
    for a in range(XA_HEADS):
        outs.append(_attention(zx[a], mk[a], mv[a]))
    return outs


def _rowcall(name, fn, rows, consts, row_outs, acc_outs, tr):
    nrows = rows[0][0].shape[0]
    tr = _pick(nrows, tr)
    n_r, n_c, n_ro, n_ao = len(rows), len(consts), len(row_outs), len(acc_outs)

    def kern(*refs):
        rv = [r[...] for r in refs[:n_r]]
        cv = [r[...] for r in refs[n_r:n_r + n_c]]
        ro_refs = refs[n_r + n_c:n_r + n_c + n_ro]
        ao_refs = refs[n_r + n_c + n_ro:]
        ro, ao = fn(rv, cv)
        for ref, v in zip(ro_refs, ro):
            ref[...] = v.astype(ref.dtype)
        if n_ao:
            @pl.when(pl.program_id(0) == 0)
            def _():
                for ref in ao_refs:
                    ref[...] = jnp.zeros(ref.shape, ref.dtype)

            for ref, v in zip(ao_refs, ao):
                ref[...] += v.astype(ref.dtype)

    in_specs = [pl.BlockSpec((tr, w), functools.partial(lambda i, cb: (i, cb), cb=cb)) for (_, cb, w) in rows]
    in_specs += [pl.BlockSpec(c.shape, lambda i: (0, 0)) for c in consts]
    out_specs = [pl.BlockSpec((tr, w), lambda i: (i, 0)) for (w, _) in row_outs]
    out_specs += [pl.BlockSpec(s, lambda i: (0, 0)) for (s, _) in acc_outs]
    out_shape = [jax.ShapeDtypeStruct((nrows, w), dt) for (w, dt) in row_outs]
    out_shape += [jax.ShapeDtypeStruct(s, dt) for (s, dt) in acc_outs]
    est = sum(tr * w * a.dtype.itemsize for (a, _, w) in rows) + sum(tr * w * jnp.dtype(dt).itemsize for (w, dt) in row_outs)
    est += sum(c.size * c.dtype.itemsize for c in consts)
    outs = pl.pallas_call(
        kern, grid=(nrows // tr,), in_specs=in_specs, out_specs=out_specs, out_shape=out_shape, name=name,
        compiler_params=pltpu.CompilerParams(dimension_semantics=("arbitrary",),
                                             vmem_limit_bytes=VMEM_CAP_BYTES),
    )(*[a for (a, _, _) in rows], *consts)
    return outs


def _mm(name, a, b, mode, out_dtype, tm, tn, tk, scale=1.0, res=None, a_lead=None, b_lead=None):
    ash = a.shape[-2:]
    bsh = b.shape[-2:]
    if mode == "nn":
        (M, K), (K2, N) = ash, bsh
    elif mode == "nt":
        (M, K), (N, K2) = ash, bsh
    else:
        (K, M), (K2, N) = ash, bsh
    assert K == K2, (name, a.shape, b.shape)
    tm, tn, tk = min(tm, M), min(tn, N), min(tk, K)
    assert M % tm == 0 and N % tn == 0 and K % tk == 0, (name, M, N, K, tm, tn, tk)
    nk = K // tk
    dims = {"nn": (1, 0), "nt": (1, 1), "tn": (0, 0)}[mode]

    def lead(spec_shape, index_fn, lead_idx):
        if lead_idx is None:
            return pl.BlockSpec(spec_shape, index_fn)
        return pl.BlockSpec((None,) + spec_shape, lambda i, j, k: (lead_idx,) + index_fn(i, j, k))

    if mode == "tn":
        a_spec = lead((tk, tm), lambda i, j, k: (k, i), a_lead)
    else:
        a_spec = lead((tm, tk), lambda i, j, k: (i, k), a_lead)
    if mode == "nt":
        b_spec = lead((tn, tk), lambda i, j, k: (j, k), b_lead)
    else:
        b_spec = lead((tk, tn), lambda i, j, k: (k, j), b_lead)
    o_spec = pl.BlockSpec((tm, tn), lambda i, j, k: (i, j))
    has_res = res is not None

    def kern(*refs):
        a_ref, b_ref = refs[0], refs[1]
        res_ref = refs[2] if has_res else None
        o_ref = refs[3] if has_res else refs[2]
        acc_ref = refs[-1] if nk > 1 else None
        p = lax.dot_general(a_ref[...].astype(BF), b_ref[...].astype(BF), (((dims[0],), (dims[1],)), ((), ())),
                            preferred_element_type=F32)

        def finish(v):
            if scale != 1.0:
                v = v * scale
            if has_res:
                v = res_ref[...] + v
            o_ref[...] = v.astype(o_ref.dtype)

        if nk == 1:
            finish(p)
        else:
            k = pl.program_id(2)

            @pl.when(k == 0)
            def _():
                acc_ref[...] = p

            @pl.when(k > 0)
            def _():
                acc_ref[...] += p

            @pl.when(k == nk - 1)
            def _():
                finish(acc_ref[...])

    ins = [a, b] + ([res] if has_res else [])
    in_specs = [a_spec, b_spec] + ([o_spec] if has_res else [])
    est = tm * tk * a.dtype.itemsize + tk * tn * b.dtype.itemsize + tm * tn * (jnp.dtype(out_dtype).itemsize + 8)
    return pl.pallas_call(
        kern, grid=(M // tm, N // tn, nk), in_specs=in_specs, out_specs=o_spec,
        out_shape=jax.ShapeDtypeStruct((M, N), out_dtype),
        scratch_shapes=[pltpu.VMEM((tm, tn), F32)] if nk > 1 else [],
        name=name,
        compiler_params=pltpu.CompilerParams(dimension_semantics=("parallel", "parallel", "arbitrary"),
                                             vmem_limit_bytes=VMEM_CAP_BYTES),
    )(*ins)


def _ffn_in_swiglu(name, h, w3, tm, tn):
    T, D = h.shape
    dff = w3.shape[2] // 2
    tm = min(tm, T)
    assert T % tm == 0 and dff % tn == 0
    nj = dff // tn

    def kern(h_ref, wg_ref, wu_ref, zg_ref, zu_ref, a_ref):
        hb = h_ref[...]
        g = jnp.dot(hb, wg_ref[...], preferred_element_type=F32).astype(BF)
        u = jnp.dot(hb, wu_ref[...], preferred_element_type=F32).astype(BF)
        zg_ref[...] = g
        zu_ref[...] = u
        a_ref[...] = (_silu(g.astype(F32)) * u.astype(F32)).astype(BF)

    o_spec = pl.BlockSpec((tm, tn), lambda i, j: (i, j))
    return pl.pallas_call(
        kern, grid=(T // tm, nj),
        in_specs=[pl.BlockSpec((tm, D), lambda i, j: (i, 0)),
                  pl.BlockSpec((None, D, tn), lambda i, j: (0, 0, j)),
                  pl.BlockSpec((None, D, tn), lambda i, j: (0, 0, j + nj))],
        out_specs=[o_spec, o_spec, o_spec],
        out_shape=[jax.ShapeDtypeStruct((T, dff), BF)] * 3, name=name,
        compiler_params=pltpu.CompilerParams(dimension_semantics=("parallel", "arbitrary"),
                                             vmem_limit_bytes=VMEM_CAP_BYTES),
    )(h, w3, w3)


def _ffn_da_swiglu(name, dxo, w3, zg, zu, tm):
    T, D = dxo.shape
    dff = w3.shape[1]
    tm = min(tm, T)
    assert T % tm == 0 and dff % 2 == 0
    hc = dff // 2

    def kern(d_ref, w_ref, g_ref, u_ref, dz_ref):
        db = (d_ref[...] * 0.5).astype(BF)
        for s in range(2):
            cols = slice(s * hc, (s + 1) * hc)
            da = lax.dot_general(db, w_ref[cols, :], (((1,), (1,)), ((), ())), preferred_element_type=F32)
            g = g_ref[:, cols].astype(F32)
            sg = 1.0 / (1.0 + jnp.exp(-g))
            gs = g * sg
            dab = da.astype(BF)
            dz_ref[:, cols] = (dab * u_ref[:, cols]) * (sg + gs * (1.0 - sg)).astype(BF)
            dz_ref[:, dff + s * hc:dff + (s + 1) * hc] = dab * gs.astype(BF)

    row = lambda w: pl.BlockSpec((tm, w), lambda i: (i, 0))
    return pl.pallas_call(
        kern, grid=(T // tm,),
        in_specs=[row(D), pl.BlockSpec((None, dff, D), lambda i: (0, 0, 0), pipeline_mode=pl.Buffered(1)), row(dff), row(dff)],
        out_specs=row(2 * dff), out_shape=jax.ShapeDtypeStruct((T, 2 * dff), BF), name=name,
        compiler_params=pltpu.CompilerParams(dimension_semantics=("arbitrary",), vmem_limit_bytes=VMEM_CAP_BYTES),
    )(dxo, w3, zg, zu)


def _mm_dh_rms(name, dz, w3, xin, g, dres, tm):
    T, K = dz.shape
    D = w3.shape[1]
    tm = min(tm, T)
    assert T % tm == 0

    def kern(dz_ref, w_ref, x_ref, g_ref, r_ref, dx_ref, dg_ref):
        dh = lax.dot_general(dz_ref[...], w_ref[...], (((1,), (1,)), ((), ())), preferred_element_type=F32)
        _, vjp = jax.vjp(_rmsnorm, x_ref[...], g_ref[...])
        dx, dg = vjp(dh)
        dx_ref[...] = dx + r_ref[...]

        @pl.when(pl.program_id(0) == 0)
        def _():
            dg_ref[...] = jnp.zeros(dg_ref.shape, F32)

        dg_ref[...] += dg

    row = lambda w: pl.BlockSpec((tm, w), lambda i: (i, 0))
    one = pl.BlockSpec((1, D), lambda i: (0, 0))
    return pl.pallas_call(
        kern, grid=(T // tm,),
        in_specs=[row(K), pl.BlockSpec((None, D, K), lambda i: (0, 0, 0), pipeline_mode=pl.Buffered(1)), row(D), one, row(D)],
        out_specs=[row(D), one], out_shape=[jax.ShapeDtypeStruct((T, D), F32), jax.ShapeDtypeStruct((1, D), F32)], name=name,
        compiler_params=pltpu.CompilerParams(dimension_semantics=("arbitrary",), vmem_limit_bytes=VMEM_CAP_BYTES),
    )(dz, w3, xin, g, dres)


def _mm_tn_pair(name, a, b, kind, c_arr, tq, tk, scale=1.0):
    T, M = a.shape
    _, N = b.shape
    tk = min(tk, T)
    assert T % tk == 0
    nk = T // tk
    if kind == "col":
        hm = M // 2
        assert N % tq == 0
        nq = N // tq
        tile = (hm, tq)
        a_spec = pl.BlockSpec((tk, hm), lambda h, q, k, c: (k, jnp.bitwise_xor(h, 1 - c[0])))
        b_spec = pl.BlockSpec((tk, tq), lambda h, q, k, c: (k, q))
        o_spec = pl.BlockSpec(tile, lambda h, q, k, c: (0, q * h))
        out_sd = (hm, N)
    else:
        hn = N // 2
        assert M % tq == 0
        nq = M // tq
        tile = (tq, hn)
        a_spec = pl.BlockSpec((tk, tq), lambda h, q, k, c: (k, q))
        b_spec = pl.BlockSpec((tk, hn), lambda h, q, k, c: (k, jnp.bitwise_xor(h, 1 - c[0])))
        o_spec = pl.BlockSpec(tile, lambda h, q, k, c: (q * h, 0))
        out_sd = (M, hn)

    def kern(c_ref, a_ref, b_ref, o_ref, acc, stage, recv, ssem, rsem):
        h, q, k = pl.program_id(0), pl.program_id(1), pl.program_id(2)
        x, y, c, _ = _place()
        p = lax.dot_general(a_ref[...].astype(BF), b_ref[...].astype(BF), (((0,), (0,)), ((), ())), preferred_element_type=F32)

        @pl.when(k == 0)
        def _():
            acc[...] = p

        @pl.when(k > 0)
        def _():
            acc[...] += p

        def send(slot, qq):
            return pltpu.make_async_remote_copy(src_ref=stage.at[slot], dst_ref=recv.at[qq], send_sem=ssem.at[slot],
                                                recv_sem=rsem.at[qq], device_id=(x, y, 1 - c), device_id_type=MESH)

        last = k == nk - 1

        @pl.when(jnp.logical_and(last, h == 0))
        def _():
            slot = q % 2

            @pl.when(q >= 2)
            def _():
                send(slot, q).wait_send()

            stage[slot] = (acc[...] * scale).astype(BF)
            send(slot, q).start()

        @pl.when(jnp.logical_and(last, h == 1))
        def _():
            @pl.when(q == 0)
            def _():
                for s in range(min(nq, 2)):
                    send(s, 0).wait_send()

            send(0, q).wait_recv()
            o_ref[...] = (acc[...] * scale + recv[q].astype(F32)).astype(o_ref.dtype)

    tb = tile[0] * tile[1]
    est = tb * (4 + 2 * 2 + nq * 2 + 2 * 2) + 2 * tk * (a_spec.block_shape[1] + b_spec.block_shape[1]) * 2 * 2
    return pl.pallas_call(
        kern,
        grid_spec=pltpu.PrefetchScalarGridSpec(
            num_scalar_prefetch=1, grid=(2, nq, nk), in_specs=[a_spec, b_spec], out_specs=o_spec,
            scratch_shapes=[pltpu.VMEM(tile, F32), pltpu.VMEM((2,) + tile, BF), pltpu.VMEM((nq,) + tile, BF),
                            pltpu.SemaphoreType.DMA((2,)), pltpu.SemaphoreType.DMA((nq,))]),
        out_shape=jax.ShapeDtypeStruct(out_sd, BF), name=name,
        compiler_params=pltpu.CompilerParams(dimension_semantics=("arbitrary", "arbitrary", "arbitrary"),
                                             vmem_limit_bytes=VMEM_CAP_BYTES),
    )(c_arr, a, b)


def _hgrn_pieces(z_ref):
    W = HG_HEADS * HG_DIM
    zq = [z_ref[:, h * HG_DIM:(h + 1) * HG_DIM] for h in range(HG_HEADS)]
    zf = [z_ref[:, W + h * HG_DIM:W + (h + 1) * HG_DIM] for h in range(HG_HEADS)]
    zi = [z_ref[:, 2 * W + h * HG_DIM:2 * W + (h + 1) * HG_DIM] for h in range(HG_HEADS)]
    zg = [z_ref[:, 3 * W + h * HG_DIM:3 * W + (h + 1) * HG_DIM] for h in range(HG_HEADS)]
    zx = [z_ref[:, 4 * W + a * XA_DIM:4 * W + (a + 1) * XA_DIM] for a in range(XA_HEADS)]
    return zq, zf, zi, zg, zx


def _kv_pieces(kv_ref):
    W = XA_HEADS * XA_DIM
    mk = [kv_ref[:, a * XA_DIM:(a + 1) * XA_DIM] for a in range(XA_HEADS)]
    mv = [kv_ref[:, W + a * XA_DIM:W + (a + 1) * XA_DIM] for a in range(XA_HEADS)]
    return mk, mv


def _lb_pieces(lb_ref):
    return [[lb_ref[r:r + 1, h * HG_DIM:(h + 1) * HG_DIM] for h in range(HG_HEADS)] for r in range(3)]


def _hgrn_fwd(z, lb_logits, gnorm, kv, bl, nc):
    T, zw = z.shape
    mem_len = kv.shape[0] // bl
    cat_w = HG_HEADS * HG_DIM + XA_HEADS * XA_DIM

    def kern(z_ref, lb_ref, gn_ref, kv_ref, cat_ref, st_ref, s_scr):
        @pl.when(pl.program_id(1) == 0)
        def _():
            s_scr[...] = jnp.zeros(s_scr.shape, F32)

        st_ref[...] = s_scr[...]
        zq, zf, zi, zg, zx = _hgrn_pieces(z_ref)
        mk, mv = _kv_pieces(kv_ref)
        l0, l1, l2 = _lb_pieces(lb_ref)
        S = [s_scr[h] for h in range(HG_HEADS)]
        outs, s_new = _hgrn_block(zq, zf, zi, zg, zx, l0, l1, l2, gn_ref[...], mk, mv, S)
        for h in range(HG_HEADS):
            cat_ref[:, h * HG_DIM:(h + 1) * HG_DIM] = outs[h].astype(cat_ref.dtype)
            s_scr[h] = s_new[h]
        base = HG_HEADS * HG_DIM
        for a in range(XA_HEADS):
            cat_ref[:, base + a * XA_DIM:base + (a + 1) * XA_DIM] = outs[HG_HEADS + a].astype(cat_ref.dtype)

    return pl.pallas_call(
        kern, grid=(bl, nc),
        in_specs=[pl.BlockSpec((HG_CHUNK, zw), lambda b, n: (b * nc + n, 0)),
                  pl.BlockSpec(lb_logits.shape, lambda b, n: (0, 0)),
                  pl.BlockSpec(gnorm.shape, lambda b, n: (0, 0)),
                  pl.BlockSpec((mem_len, kv.shape[1]), lambda b, n: (b, 0))],
        out_specs=[pl.BlockSpec((HG_CHUNK, cat_w), lambda b, n: (b * nc + n, 0)),
                   pl.BlockSpec((None, HG_HEADS, HG_DIM, HG_DIM), lambda b, n: (b * nc + n, 0, 0, 0))],
        out_shape=[jax.ShapeDtypeStruct((T, cat_w), BF),
                   jax.ShapeDtypeStruct((bl * nc, HG_HEADS, HG_DIM, HG_DIM), F32)],
        scratch_shapes=[pltpu.VMEM((HG_HEADS, HG_DIM, HG_DIM), F32)],
        name="hgrn_fwd",
        compiler_params=pltpu.CompilerParams(dimension_semantics=("arbitrary", "arbitrary"), vmem_limit_bytes=VMEM_CAP_BYTES),
    )(z, lb_logits, gnorm, kv)


def _hgrn_bwd(z, dcat, stash, lb_logits, gnorm, kv, bl, nc):
    T, zw = z.shape
    mem_len = kv.shape[0] // bl
    cat_w = dcat.shape[1]

    def kern(z_ref, dc_ref, st_ref, lb_ref, gn_ref, kv_ref, dz_ref, dkv_ref, dlb_ref, dgn_ref, ds_scr):
        first = jnp.logical_and(pl.program_id(0) == 0, pl.program_id(1) == 0)

        @pl.when(pl.program_id(1) == 0)
        def _():
            ds_scr[...] = jnp.zeros(ds_scr.shape, F32)
            dkv_ref[...] = jnp.zeros(dkv_ref.shape, F32)

        @pl.when(first)
        def _():
            dlb_ref[...] = jnp.zeros(dlb_ref.shape, F32)
            dgn_ref[...] = jnp.zeros(dgn_ref.shape, F32)

        zq, zf, zi, zg, zx = _hgrn_pieces(z_ref)
        mk, mv = _kv_pieces(kv_ref)
        l0, l1, l2 = _lb_pieces(lb_ref)
        S = [st_ref[h] for h in range(HG_HEADS)]
        _, vjp = jax.vjp(_hgrn_block, zq, zf, zi, zg, zx, l0, l1, l2, gn_ref[...], mk, mv, S)
        d_outs = [dc_ref[:, h * HG_DIM:(h + 1) * HG_DIM] for h in range(HG_HEADS)]
        base = HG_HEADS * HG_DIM
        d_outs += [dc_ref[:, base + a * XA_DIM:base + (a + 1) * XA_DIM] for a in range(XA_HEADS)]
        d_s = [ds_scr[h] for h in range(HG_HEADS)]
        dzq, dzf, dzi, dzg, dzx, dl0, dl1, dl2, dgn, dmk, dmv, dS = vjp((d_outs, d_s))
        W = HG_HEADS * HG_DIM
        for h in range(HG_HEADS):
            sl = slice(h * HG_DIM, (h + 1) * HG_DIM)
            dz_ref[:, sl] = dzq[h].astype(dz_ref.dtype)
            dz_ref[:, W + h * HG_DIM:W + (h + 1) * HG_DIM] = dzf[h].astype(dz_ref.dtype)
            dz_ref[:, 2 * W + h * HG_DIM:2 * W + (h + 1) * HG_DIM] = dzi[h].astype(dz_ref.dtype)
            dz_ref[:, 3 * W + h * HG_DIM:3 * W + (h + 1) * HG_DIM] = dzg[h].astype(dz_ref.dtype)
            ds_scr[h] = dS[h]
            dlb_ref[0:1, sl] += dl0[h]
            dlb_ref[1:2, sl] += dl1[h]
            dlb_ref[2:3, sl] += dl2[h]
        dgn_ref[...] += dgn
        KW = XA_HEADS * XA_DIM
        for a in range(XA_HEADS):
            dz_ref[:, 4 * W + a * XA_DIM:4 * W + (a + 1) * XA_DIM] = dzx[a].astype(dz_ref.dtype)
            dkv_ref[:, a * XA_DIM:(a + 1) * XA_DIM] += dmk[a]
            dkv_ref[:, KW + a * XA_DIM:KW + (a + 1) * XA_DIM] += dmv[a]

    rev = lambda b, n: (b * nc + (nc - 1 - n), 0)
    return pl.pallas_call(
        kern, grid=(bl, nc),
        in_specs=[pl.BlockSpec((HG_CHUNK, zw), rev),
                  pl.BlockSpec((HG_CHUNK, cat_w), rev),
                  pl.BlockSpec((None, HG_HEADS, HG_DIM, HG_DIM), lambda b, n: (b * nc + (nc - 1 - n), 0, 0, 0)),
                  pl.BlockSpec(lb_logits.shape, lambda b, n: (0, 0)),
                  pl.BlockSpec(gnorm.shape, lambda b, n: (0, 0)),
                  pl.BlockSpec((mem_len, kv.shape[1]), lambda b, n: (b, 0))],
        out_specs=[pl.BlockSpec((HG_CHUNK, zw), rev),
                   pl.BlockSpec((mem_len, kv.shape[1]), lambda b, n: (b, 0)),
                   pl.BlockSpec(lb_logits.shape, lambda b, n: (0, 0)),
                   pl.BlockSpec(gnorm.shape, lambda b, n: (0, 0))],
        out_shape=[jax.ShapeDtypeStruct((T, zw), BF), jax.ShapeDtypeStruct(kv.shape, F32),
                   jax.ShapeDtypeStruct(lb_logits.shape, F32), jax.ShapeDtypeStruct(gnorm.shape, F32)],
        scratch_shapes=[pltpu.VMEM((HG_HEADS, HG_DIM, HG_DIM), F32)],
        name="hgrn_bwd",
        compiler_params=pltpu.CompilerParams(dimension_semantics=("arbitrary", "arbitrary"), vmem_limit_bytes=VMEM_CAP_BYTES),
    )(z, dcat, stash, lb_logits, gnorm, kv)


HG_SUB = 4


def _hgrn_rows(z_ref, dtype_cast=None):
    W = HG_HEADS * HG_DIM

    def piece(c, col, w):
        return z_ref[c * HG_CHUNK:(c + 1) * HG_CHUNK, col:col + w]

    zq = [[piece(c, h * HG_DIM, HG_DIM) for h in range(HG_HEADS)] for c in range(HG_SUB)]
    zf = [[piece(c, W + h * HG_DIM, HG_DIM) for h in range(HG_HEADS)] for c in range(HG_SUB)]
    zi = [[piece(c, 2 * W + h * HG_DIM, HG_DIM) for h in range(HG_HEADS)] for c in range(HG_SUB)]
    zg = [[piece(c, 3 * W + h * HG_DIM, HG_DIM) for h in range(HG_HEADS)] for c in range(HG_SUB)]
    zx = [z_ref[:, 4 * W + a * XA_DIM:4 * W + (a + 1) * XA_DIM] for a in range(XA_HEADS)]
    return zq, zf, zi, zg, zx


def _hgrn_steps(zq, zf, zi, zg, zx, l0, l1, l2, gn, mk, mv, S):
    mix = []
    for c in range(HG_SUB):
        row, s_next = [], []
        for h in range(HG_HEADS):
            o, sn = _hgrn_head(zq[c][h], zf[c][h], zi[c][h], zg[c][h], l0[h], l1[h], l2[h], gn, S[h])
            row.append(o)
            s_next.append(sn)
        mix.append(row)
        S = s_next
    att = [_attention(zx[a], mk[a], mv[a]) for a in range(XA_HEADS)]
    return mix, att, S


def _hgrn_fwd2(z, lb_logits, gnorm, kv, bl, seq):
    T, zw = z.shape
    mem_len = kv.shape[0] // bl
    cat_w = HG_HEADS * HG_DIM + XA_HEADS * XA_DIM
    R = HG_SUB * HG_CHUNK
    nb = seq // R

    def kern(z_ref, lb_ref, gn_ref, kv_ref, cat_ref, st_ref, s_scr):
        @pl.when(pl.program_id(1) == 0)
        def _():
            s_scr[...] = jnp.zeros(s_scr.shape, F32)

        st_ref[...] = s_scr[...]
        zq, zf, zi, zg, zx = _hgrn_rows(z_ref)
        mk, mv = _kv_pieces(kv_ref)
        l0, l1, l2 = _lb_pieces(lb_ref)
        S = [s_scr[h] for h in range(HG_HEADS)]
        mix, att, s_new = _hgrn_steps(zq, zf, zi, zg, zx, l0, l1, l2, gn_ref[...], mk, mv, S)
        for c in range(HG_SUB):
            for h in range(HG_HEADS):
                cat_ref[c * HG_CHUNK:(c + 1) * HG_CHUNK, h * HG_DIM:(h + 1) * HG_DIM] = mix[c][h].astype(cat_ref.dtype)
        for h in range(HG_HEADS):
            s_scr[h] = s_new[h]
        base = HG_HEADS * HG_DIM
        for a in range(XA_HEADS):
            cat_ref[:, base + a * XA_DIM:base + (a + 1) * XA_DIM] = att[a].astype(cat_ref.dtype)

    return pl.pallas_call(
        kern, grid=(bl, nb),
        in_specs=[pl.BlockSpec((R, zw), lambda b, n: (b * nb + n, 0)),
                  pl.BlockSpec(lb_logits.shape, lambda b, n: (0, 0)),
                  pl.BlockSpec(gnorm.shape, lambda b, n: (0, 0)),
                  pl.BlockSpec((mem_len, kv.shape[1]), lambda b, n: (b, 0))],
        out_specs=[pl.BlockSpec((R, cat_w), lambda b, n: (b * nb + n, 0)),
                   pl.BlockSpec((None, HG_HEADS, HG_DIM, HG_DIM), lambda b, n: (b * nb + n, 0, 0, 0))],
        out_shape=[jax.ShapeDtypeStruct((T, cat_w), BF),
                   jax.ShapeDtypeStruct((bl * nb, HG_HEADS, HG_DIM, HG_DIM), F32)],
        scratch_shapes=[pltpu.VMEM((HG_HEADS, HG_DIM, HG_DIM), F32)],
        name="hgrn_fwd",
        compiler_params=pltpu.CompilerParams(dimension_semantics=("arbitrary", "arbitrary"), vmem_limit_bytes=VMEM_CAP_BYTES),
    )(z, lb_logits, gnorm, kv)


def _hgrn_bwd2(z, dcat, stash, lb_logits, gnorm, kv, bl, seq):
    T, zw = z.shape
    mem_len = kv.shape[0] // bl
    cat_w = dcat.shape[1]
    R = HG_SUB * HG_CHUNK
    nb = seq // R

    def kern(z_ref, dc_ref, st_ref, lb_ref, gn_ref, kv_ref, dz_ref, dkv_ref, dlb_ref, dgn_ref, ds_scr):
        first = jnp.logical_and(pl.program_id(0) == 0, pl.program_id(1) == 0)

        @pl.when(pl.program_id(1) == 0)
        def _():
            ds_scr[...] = jnp.zeros(ds_scr.shape, F32)
            dkv_ref[...] = jnp.zeros(dkv_ref.shape, F32)

        @pl.when(first)
        def _():
            dlb_ref[...] = jnp.zeros(dlb_ref.shape, F32)
            dgn_ref[...] = jnp.zeros(dgn_ref.shape, F32)

        zq, zf, zi, zg, zx = _hgrn_rows(z_ref)
        mk, mv = _kv_pieces(kv_ref)
        l0, l1, l2 = _lb_pieces(lb_ref)
        S = [st_ref[h] for h in range(HG_HEADS)]
        _, vjp = jax.vjp(_hgrn_steps, zq, zf, zi, zg, zx, l0, l1, l2, gn_ref[...], mk, mv, S)
        d_mix = [[dc_ref[c * HG_CHUNK:(c + 1) * HG_CHUNK, h * HG_DIM:(h + 1) * HG_DIM] for h in range(HG_HEADS)]
                 for c in range(HG_SUB)]
        base = HG_HEADS * HG_DIM
        d_att = [dc_ref[:, base + a * XA_DIM:base + (a + 1) * XA_DIM] for a in range(XA_HEADS)]
        d_s = [ds_scr[h] for h in range(HG_HEADS)]
        dzq, dzf, dzi, dzg, dzx, dl0, dl1, dl2, dgn, dmk, dmv, dS = vjp((d_mix, d_att, d_s))
        W = HG_HEADS * HG_DIM
        for c in range(HG_SUB):
            rows = slice(c * HG_CHUNK, (c + 1) * HG_CHUNK)
            for h in range(HG_HEADS):
                for k, part in enumerate((dzq, dzf, dzi, dzg)):
                    dz_ref[rows, k * W + h * HG_DIM:k * W + (h + 1) * HG_DIM] = part[c][h].astype(dz_ref.dtype)
        for h in range(HG_HEADS):
            sl = slice(h * HG_DIM, (h + 1) * HG_DIM)
            ds_scr[h] = dS[h]
            dlb_ref[0:1, sl] += dl0[h]
            dlb_ref[1:2, sl] += dl1[h]
            dlb_ref[2:3, sl] += dl2[h]
        dgn_ref[...] += dgn
        KW = XA_HEADS * XA_DIM
        for a in range(XA_HEADS):
            dz_ref[:, 4 * W + a * XA_DIM:4 * W + (a + 1) * XA_DIM] = dzx[a].astype(dz_ref.dtype)
            dkv_ref[:, a * XA_DIM:(a + 1) * XA_DIM] += dmk[a]
            dkv_ref[:, KW + a * XA_DIM:KW + (a + 1) * XA_DIM] += dmv[a]

    rev = lambda b, n: (b * nb + (nb - 1 - n), 0)
    return pl.pallas_call(
        kern, grid=(bl, nb),
        in_specs=[pl.BlockSpec((R, zw), rev),
                  pl.BlockSpec((R, cat_w), rev),
                  pl.BlockSpec((None, HG_HEADS, HG_DIM, HG_DIM), lambda b, n: (b * nb + (nb - 1 - n), 0, 0, 0)),
                  pl.BlockSpec(lb_logits.shape, lambda b, n: (0, 0)),
                  pl.BlockSpec(gnorm.shape, lambda b, n: (0, 0)),
                  pl.BlockSpec((mem_len, kv.shape[1]), lambda b, n: (b, 0))],
        out_specs=[pl.BlockSpec((R, zw), rev),
                   pl.BlockSpec((mem_len, kv.shape[1]), lambda b, n: (b, 0)),
                   pl.BlockSpec(lb_logits.shape, lambda b, n: (0, 0)),
                   pl.BlockSpec(gnorm.shape, lambda b, n: (0, 0))],
        out_shape=[jax.ShapeDtypeStruct((T, zw), BF), jax.ShapeDtypeStruct(kv.shape, F32),
                   jax.ShapeDtypeStruct(lb_logits.shape, F32), jax.ShapeDtypeStruct(gnorm.shape, F32)],
        scratch_shapes=[pltpu.VMEM((HG_HEADS, HG_DIM, HG_DIM), F32)],
        name="hgrn_bwd",
        compiler_params=pltpu.CompilerParams(dimension_semantics=("arbitrary", "arbitrary"), vmem_limit_bytes=VMEM_CAP_BYTES),
    )(z, dcat, stash, lb_logits, gnorm, kv)


GM_SUB = 2


def _gmlp_pieces(z_ref):
    W = GM_GROUPS * GM_GROUP_DIM
    zu = [z_ref[:, g * GM_GROUP_DIM:(g + 1) * GM_GROUP_DIM] for g in range(GM_GROUPS)]
    zv = [z_ref[:, W + g * GM_GROUP_DIM:W + (g + 1) * GM_GROUP_DIM] for g in range(GM_GROUPS)]
    zx = [z_ref[:, 2 * W + a * XA_DIM:2 * W + (a + 1) * XA_DIM] for a in range(XA_HEADS)]
    return zu, zv, zx


def _gmlp_params(lng_ref, lnb_ref, ws_ref, bs_ref):
    lng = [lng_ref[:, g * GM_GROUP_DIM:(g + 1) * GM_GROUP_DIM] for g in range(GM_GROUPS)]
    lnb = [lnb_ref[:, g * GM_GROUP_DIM:(g + 1) * GM_GROUP_DIM] for g in range(GM_GROUPS)]
    ws = [ws_ref[g] for g in range(GM_GROUPS)]
    bs = [bs_ref[g:g + 1, :] for g in range(GM_GROUPS)]
    return lng, lnb, ws, bs


def _gmlp_fwd(z, ln_g, ln_b, w_s, b_s, kv, bl, nc):
    T, zw = z.shape
    mem_len = kv.shape[0] // bl
    cat_w = GM_GROUPS * GM_GROUP_DIM + XA_HEADS * XA_DIM

    assert nc % GM_SUB == 0
    nc = nc // GM_SUB
    R = GM_SUB * GM_CHUNK

    def kern(z_ref, lng_ref, lnb_ref, ws_ref, bs_ref, kv_ref, cat_ref):
        lng, lnb, ws, bs = _gmlp_params(lng_ref, lnb_ref, ws_ref, bs_ref)
        mk, mv = _kv_pieces(kv_ref)
        for c in range(GM_SUB):
            rows = pl.ds(c * GM_CHUNK, GM_CHUNK)
            zu, zv, zx = _gmlp_pieces(z_ref.at[rows])
            out = cat_ref.at[rows]
            outs = _gmlp_block(zu, zv, zx, lng, lnb, ws, bs, mk, mv)
            for g in range(GM_GROUPS):
                out[:, g * GM_GROUP_DIM:(g + 1) * GM_GROUP_DIM] = outs[g].astype(cat_ref.dtype)
            base = GM_GROUPS * GM_GROUP_DIM
            for a in range(XA_HEADS):
                out[:, base + a * XA_DIM:base + (a + 1) * XA_DIM] = outs[GM_GROUPS + a].astype(cat_ref.dtype)

    full2 = lambda b, n: (0, 0)
    return pl.pallas_call(
        kern, grid=(bl, nc),
        in_specs=[pl.BlockSpec((R, zw), lambda b, n: (b * nc + n, 0)),
                  pl.BlockSpec(ln_g.shape, full2), pl.BlockSpec(ln_b.shape, full2),
                  pl.BlockSpec(w_s.shape, lambda b, n: (0, 0, 0)), pl.BlockSpec(b_s.shape, full2),
                  pl.BlockSpec((mem_len, kv.shape[1]), lambda b, n: (b, 0))],
        out_specs=pl.BlockSpec((R, cat_w), lambda b, n: (b * nc + n, 0)),
        out_shape=jax.ShapeDtypeStruct((T, cat_w), BF),
        name="gmlp_fwd",
        compiler_params=pltpu.CompilerParams(dimension_semantics=("arbitrary", "arbitrary"), vmem_limit_bytes=VMEM_CAP_BYTES),
    )(z, ln_g, ln_b, w_s, b_s, kv)


def _gmlp_bwd(z, dcat, ln_g, ln_b, w_s, b_s, kv, bl, nc):
    T, zw = z.shape
    mem_len = kv.shape[0] // bl
    cat_w = dcat.shape[1]
    assert nc % GM_SUB == 0
    nc = nc // GM_SUB

    def kern(z_ref, dc_ref, lng_ref, lnb_ref, ws_ref, bs_ref, kv_ref,
             dz_ref, dkv_ref, dlng_ref, dlnb_ref, dws_ref, dbs_ref):
        first = jnp.logical_and(pl.program_id(0) == 0, pl.program_id(1) == 0)

        @pl.when(pl.program_id(1) == 0)
        def _():
            dkv_ref[...] = jnp.zeros(dkv_ref.shape, F32)

        @pl.when(first)
        def _():
            dlng_ref[...] = jnp.zeros(dlng_ref.shape, F32)
            dlnb_ref[...] = jnp.zeros(dlnb_ref.shape, F32)
            dws_ref[...] = jnp.zeros(dws_ref.shape, F32)
            dbs_ref[...] = jnp.zeros(dbs_ref.shape, F32)

        lng, lnb, ws, bs = _gmlp_params(lng_ref, lnb_ref, ws_ref, bs_ref)
        mk, mv = _kv_pieces(kv_ref)
        W = GM_GROUPS * GM_GROUP_DIM
        KW = XA_HEADS * XA_DIM
        for c in range(GM_SUB):
            rows = pl.ds(c * GM_CHUNK, GM_CHUNK)
            zu, zv, zx = _gmlp_pieces(z_ref.at[rows])
            dc, dz = dc_ref.at[rows], dz_ref.at[rows]
            _, vjp = jax.vjp(_gmlp_block, zu, zv, zx, lng, lnb, ws, bs, mk, mv)
            d_outs = [dc[:, g * GM_GROUP_DIM:(g + 1) * GM_GROUP_DIM] for g in range(GM_GROUPS)]
            d_outs += [dc[:, W + a * XA_DIM:W + (a + 1) * XA_DIM] for a in range(XA_HEADS)]
            dzu, dzv, dzx, dlng, dlnb, dws, dbs, dmk, dmv = vjp(d_outs)
            for g in range(GM_GROUPS):
                sl = slice(g * GM_GROUP_DIM, (g + 1) * GM_GROUP_DIM)
                dz[:, sl] = dzu[g].astype(dz_ref.dtype)
                dz[:, W + g * GM_GROUP_DIM:W + (g + 1) * GM_GROUP_DIM] = dzv[g].astype(dz_ref.dtype)
                dlng_ref[:, sl] += dlng[g]
                dlnb_ref[:, sl] += dlnb[g]
                dws_ref[g] += dws[g]
                dbs_ref[g:g + 1, :] += dbs[g]
            for a in range(XA_HEADS):
                dz[:, 2 * W + a * XA_DIM:2 * W + (a + 1) * XA_DIM] = dzx[a].astype(dz_ref.dtype)
                dkv_ref[:, a * XA_DIM:(a + 1) * XA_DIM] += dmk[a]
                dkv_ref[:, KW + a * XA_DIM:KW + (a + 1) * XA_DIM] += dmv[a]

    full2 = lambda b, n: (0, 0)
    full3 = lambda b, n: (0, 0, 0)
    blk = lambda b, n: (b * nc + n, 0)
    return pl.pallas_call(
        kern, grid=(bl, nc),
        in_specs=[pl.BlockSpec((GM_SUB * GM_CHUNK, zw), blk), pl.BlockSpec((GM_SUB * GM_CHUNK, cat_w), blk),
                  pl.BlockSpec(ln_g.shape, full2), pl.BlockSpec(ln_b.shape, full2),
                  pl.BlockSpec(w_s.shape, full3), pl.BlockSpec(b_s.shape, full2),
                  pl.BlockSpec((mem_len, kv.shape[1]), lambda b, n: (b, 0))],
        out_specs=[pl.BlockSpec((GM_SUB * GM_CHUNK, zw), blk),
                   pl.BlockSpec((mem_len, kv.shape[1]), lambda b, n: (b, 0)),
                   pl.BlockSpec(ln_g.shape, full2), pl.BlockSpec(ln_b.shape, full2),
                   pl.BlockSpec(w_s.shape, full3), pl.BlockSpec(b_s.shape, full2)],
        out_shape=[jax.ShapeDtypeStruct((T, zw), BF), jax.ShapeDtypeStruct(kv.shape, F32),
                   jax.ShapeDtypeStruct(ln_g.shape, F32), jax.ShapeDtypeStruct(ln_b.shape, F32),
                   jax.ShapeDtypeStruct(w_s.shape, F32), jax.ShapeDtypeStruct(b_s.shape, F32)],
        name="gmlp_bwd",
        compiler_params=pltpu.CompilerParams(dimension_semantics=("arbitrary", "arbitrary"), vmem_limit_bytes=VMEM_CAP_BYTES),
    )(z, dcat, ln_g, ln_b, w_s, b_s, kv)


def _place():
    x, y, c = lax.axis_index("x"), lax.axis_index("y"), lax.axis_index("c")
    chips = [(1 - x, y), (x, 1 - y), (1 - x, 1 - y)]
    return x, y, c, chips


def _half(ref, kind, e):
    if kind == "col":
        n = ref.shape[1] // 2
        return ref.at[:, pl.ds(pl.multiple_of(e * n, n), n), :]
    n = ref.shape[2] // 2
    return ref.at[:, :, pl.ds(pl.multiple_of(e * n, n), n)]


def _slot(ref, kind, j, n):
    if kind == "col":
        return ref.at[:, :, pl.ds(pl.multiple_of(j * n, n), n)]
    return ref.at[:, pl.ds(pl.multiple_of(j * n, n), n), :]


def _allgather_seq(name, items, cid):
    nt = len(items)
    kinds = [k for (_, k, _) in items]
    slot_kind = ["row" if k == "row" else "col" for k in kinds]
    out_type = []
    for s, k, l in items:
        L, r, c = s.shape
        lo = L if l is None else 1
        out_type.append(jax.ShapeDtypeStruct((lo, 4 * r, c) if k == "row" else (lo, r, 4 * c), s.dtype))

    def part(ref, t, e):
        return ref if kinds[t] == "vec" else _half(ref, kinds[t], e)

    def body(*refs):
        sh = [refs[t] if items[t][2] is None else refs[t].at[pl.ds(items[t][2], 1)] for t in range(nt)]
        full = refs[nt:2 * nt]
        loc, s_ici, r_ici, s_d2d, r_d2d = refs[2 * nt:]
        x, y, c, chips = _place()
        own = 2 * x + y
        sibling = (x, y, 1 - c)
        barrier = pltpu.get_barrier_semaphore()
        for peer in [(px, py, c) for (px, py) in chips] + [sibling]:
            pl.semaphore_signal(barrier, inc=1, device_id=peer, device_id_type=MESH)
        pl.semaphore_wait(barrier, 4)
        width = [sh[t].shape[1] if kinds[t] == "row" else sh[t].shape[2] for t in range(nt)]
        started = []
        for t in range(nt):
            mine = pltpu.make_async_copy(sh[t], _slot(full[t], slot_kind[t], own, width[t]), loc.at[t])
            mine.start()
            started.append(mine)
        sent = []
        for t in range(nt):
            for p, (px, py) in enumerate(chips):
                cp = pltpu.make_async_remote_copy(
                    src_ref=part(sh[t], t, c), dst_ref=part(_slot(full[t], slot_kind[t], own, width[t]), t, c),
                    send_sem=s_ici.at[t, p], recv_sem=r_ici.at[t, p], device_id=(px, py, c), device_id_type=MESH)
                cp.start()
                sent.append(cp)
        for t in range(nt):
            for p, (px, py) in enumerate(chips):
                landed = part(_slot(full[t], slot_kind[t], 2 * px + py, width[t]), t, c)
                pltpu.make_async_remote_copy(
                    src_ref=landed, dst_ref=landed, send_sem=s_ici.at[t, p], recv_sem=r_ici.at[t, p],
                    device_id=(px, py, c), device_id_type=MESH).wait_recv()
                if kinds[t] == "vec":
                    continue
                fw = pltpu.make_async_remote_copy(
                    src_ref=landed, dst_ref=landed, send_sem=s_d2d.at[t, p], recv_sem=r_d2d.at[t, p],
                    device_id=sibling, device_id_type=MESH)
                fw.start()
                sent.append(fw)
        for t in range(nt):
            if kinds[t] == "vec":
                continue
            for p, (px, py) in enumerate(chips):
                other = _half(_slot(full[t], kinds[t], 2 * px + py, width[t]), kinds[t], 1 - c)
                pltpu.make_async_remote_copy(
                    src_ref=other, dst_ref=other, send_sem=s_d2d.at[t, p], recv_sem=r_d2d.at[t, p],
                    device_id=sibling, device_id_type=MESH).wait_recv()
        for cp in sent:
            cp.wait_send()
        for cp in started:
            cp.wait()

    return pl.kernel(
        body, out_type=out_type, mesh=plsc.ScalarSubcoreMesh(axis_name="seq", num_cores=1),
        scratch_types=[pltpu.SemaphoreType.DMA((nt,)), pltpu.SemaphoreType.DMA((nt, 3)), pltpu.SemaphoreType.DMA((nt, 3)),
                       pltpu.SemaphoreType.DMA((nt, 3)), pltpu.SemaphoreType.DMA((nt, 3))],
        compiler_params=pltpu.CompilerParams(collective_id=cid), name=name,
    )(*[s for (s, _, _) in items])


def _slot2(ref, kind, j, n):
    if kind == "col":
        return ref.at[:, pl.ds(pl.multiple_of(j * n, n), n)]
    return ref.at[pl.ds(pl.multiple_of(j * n, n), n), :]


def _rs_chips_seq(name, parts, kinds, cid):
    nm = len(parts)
    out_type = []
    for g, k in zip(parts, kinds):
        r, c = g.shape
        ps = (r, c // 4) if k == "col" else (r // 4, c)
        out_type += [jax.ShapeDtypeStruct(ps, BF), jax.ShapeDtypeStruct((3,) + ps, BF)]

    def body(*refs):
        g = refs[:nm]
        outs = refs[nm:3 * nm]
        loc, ssem, rsem = refs[3 * nm:]
        x, y, c, chips = _place()
        own = 2 * x + y
        barrier = pltpu.get_barrier_semaphore()
        for (px, py) in chips:
            pl.semaphore_signal(barrier, inc=1, device_id=(px, py, c), device_id_type=MESH)
        pl.semaphore_wait(barrier, 3)
        cps = []
        for m in range(nm):
            k = kinds[m]
            own_o, got_o = outs[2 * m], outs[2 * m + 1]
            n = g[m].shape[1] // 4 if k == "col" else g[m].shape[0] // 4
            lc = pltpu.make_async_copy(_slot2(g[m], k, own, n), own_o, loc.at[m])
            lc.start()
            cps.append(lc)
            for p, (px, py) in enumerate(chips):
                cp = pltpu.make_async_remote_copy(
                    src_ref=_slot2(g[m], k, 2 * px + py, n), dst_ref=got_o.at[p],
                    send_sem=ssem.at[m, p], recv_sem=rsem.at[m, p], device_id=(px, py, c), device_id_type=MESH)
                cp.start()
                cps.append(cp)
        for cp in cps:
            cp.wait()

    return pl.kernel(
        body, out_type=out_type, mesh=plsc.ScalarSubcoreMesh(axis_name="seq", num_cores=1),
        scratch_types=[pltpu.SemaphoreType.DMA((nm,)), pltpu.SemaphoreType.DMA((nm, 3)), pltpu.SemaphoreType.DMA((nm, 3))],
        compiler_params=pltpu.CompilerParams(collective_id=cid), name=name,
    )(*parts)


def _finish_share(name, own, got, kind, c_arr):
    L, r, c = own.shape
    tr = _pick(r, 128 if kind == "col" else 256)
    nb = r // tr
    nq = L * nb
    own2 = own.reshape(L * r, c)
    got2 = got.reshape(3 * L * r, c)
    pick = lambda h, q: q * (1 - h) + (nq - 1) * h
    in_specs = [pl.BlockSpec((tr, c), lambda h, q, cc: (pick(h, q), 0))]
    in_specs += [pl.BlockSpec((tr, c), functools.partial(lambda h, q, cc, p: (p * nq + pick(h, q), 0), p=p)) for p in range(3)]
    if kind == "col":
        out_sd = (L, 2, r, c)
        o_spec = pl.BlockSpec((None, 2, tr, c), lambda h, q, cc: ((q * h) // nb, 0, (q * h) % nb, 0))
    else:
        out_sd = (L * r, 2 * c)
        o_spec = pl.BlockSpec((tr, 2 * c), lambda h, q, cc: (q * h, 0))

    def kern(c_ref, o_ref, g0, g1, g2, out_ref, mine, recv, ssem, rsem):
        h, q = pl.program_id(0), pl.program_id(1)
        x, y, cc, _ = _place()

        def swap(qq):
            return pltpu.make_async_remote_copy(src_ref=mine.at[qq], dst_ref=recv.at[qq], send_sem=ssem.at[qq],
                                                recv_sem=rsem.at[qq], device_id=(x, y, 1 - cc), device_id_type=MESH)

        @pl.when(h == 0)
        def _():
            mine[q] = ((o_ref[...].astype(F32) + g0[...].astype(F32)) + g1[...].astype(F32)) + g2[...].astype(F32)
            swap(q).start()

        @pl.when(h == 1)
        def _():
            swap(q).wait()
            a, b = mine[q], recv[q]
            first = c_ref[0] == 0
            lo, hi = jnp.where(first, a, b), jnp.where(first, b, a)
            if kind == "col":
                out_ref[0] = lo
                out_ref[1] = hi
            else:
                out_ref[:, :c] = lo
                out_ref[:, c:] = hi

    est = 2 * nq * tr * c * 4 + 6 * tr * c * 4 + 8 * tr * c * 2
    full = pl.pallas_call(
        kern,
        grid_spec=pltpu.PrefetchScalarGridSpec(
            num_scalar_prefetch=1, grid=(2, nq), in_specs=in_specs, out_specs=o_spec,
            scratch_shapes=[pltpu.VMEM((nq, tr, c), F32), pltpu.VMEM((nq, tr, c), F32),
                            pltpu.SemaphoreType.DMA((nq,)), pltpu.SemaphoreType.DMA((nq,))]),
        out_shape=jax.ShapeDtypeStruct(out_sd, F32), name=name,
        compiler_params=pltpu.CompilerParams(dimension_semantics=("arbitrary", "arbitrary"),
                                             vmem_limit_bytes=VMEM_CAP_BYTES),
    )(c_arr, own2, got2, got2, got2)
    return full.reshape(L, 2 * r, c) if kind == "col" else full.reshape(L, r, 2 * c)


def _small_allreduce(buf, name):
    R = buf.shape[0]
    assert R % 16 == 0
    h = R // 2

    def body(x_ref, o_ref, sib, csum, got, s_a, r_a, s_b, r_b, s_c, r_c):
        x, y, c, chips = _place()
        sibling = (x, y, 1 - c)
        own = 2 * x + y
        swap = pltpu.make_async_remote_copy(src_ref=x_ref, dst_ref=sib, send_sem=s_a, recv_sem=r_a,
                                            device_id=sibling, device_id_type=MESH)
        swap.start()
        swap.wait()
        a, b = x_ref[...], sib[...]
        south = c == 0
        csum[...] = jnp.where(south, a, b) + jnp.where(south, b, a)
        lo = pl.multiple_of(c * h, 8)
        mine = csum.at[pl.ds(lo, h)]
        got[own] = csum[pl.ds(lo, h)]
        sends = []
        for p, (px, py) in enumerate(chips):
            cp = pltpu.make_async_remote_copy(src_ref=mine, dst_ref=got.at[own], send_sem=s_b.at[p], recv_sem=r_b.at[p],
                                              device_id=(px, py, c), device_id_type=MESH)
            cp.start()
            sends.append(cp)
        for cp in sends:
            cp.wait()
        o_ref[pl.ds(lo, h)] = ((got[0] + got[1]) + got[2]) + got[3]
        done = o_ref.at[pl.ds(lo, h)]
        back = pltpu.make_async_remote_copy(src_ref=done, dst_ref=done, send_sem=s_c, recv_sem=r_c,
                                            device_id=sibling, device_id_type=MESH)
        back.start()
        back.wait_send()
        other = o_ref.at[pl.ds(pl.multiple_of((1 - c) * h, 8), h)]
        pltpu.make_async_remote_copy(src_ref=other, dst_ref=other, send_sem=s_c, recv_sem=r_c,
                                     device_id=sibling, device_id_type=MESH).wait_recv()

    vm = pl.BlockSpec(memory_space=pltpu.VMEM)
    return pl.pallas_call(
        body, out_shape=jax.ShapeDtypeStruct(buf.shape, F32), in_specs=[vm], out_specs=vm,
        scratch_shapes=[pltpu.VMEM((R, LANES), F32), pltpu.VMEM((R, LANES), F32), pltpu.VMEM((4, h, LANES), F32),
                        pltpu.SemaphoreType.DMA, pltpu.SemaphoreType.DMA, pltpu.SemaphoreType.DMA((3,)),
                        pltpu.SemaphoreType.DMA((3,)), pltpu.SemaphoreType.DMA, pltpu.SemaphoreType.DMA],
        name=name,
        compiler_params=pltpu.CompilerParams(vmem_limit_bytes=VMEM_CAP_BYTES),
    )(buf)


PACK_TILE_ROWS = 8


def _item_rows(shape):
    n = 1
    for d in shape:
        n *= d
    return -(-n // (PACK_TILE_ROWS * LANES)) * PACK_TILE_ROWS


def _pack(arrs, rows_total):
    buf = jnp.zeros((rows_total, LANES), F32)
    r = 0
    for a in arrs:
        f = a.reshape(-1).astype(F32)
        nr = _item_rows(a.shape)
        block = jnp.pad(f, (0, nr * LANES - f.shape[0])).reshape(nr, LANES)
        buf = lax.dynamic_update_slice(buf, block, (r, 0))
        r += nr
    return buf


def _unpack(buf, shapes):
    out, r = [], 0
    for s in shapes:
        n = 1
        for d in s:
            n *= d
        nr = _item_rows(s)
        out.append(buf[r:r + nr].reshape(-1)[:n].reshape(s))
        r += nr
    return out


def _rows_needed(shapes):
    return -(-sum(_item_rows(s) for s in shapes) // (2 * PACK_TILE_ROWS)) * (2 * PACK_TILE_ROWS)


def _two_rows(a, b):
    out = jnp.zeros((2, a.shape[1]), a.dtype)
    return lax.dynamic_update_slice(lax.dynamic_update_slice(out, a, (0, 0)), b, (1, 0))


def _adam(w, g, m, v):
    m = ADAM_B1 * m + (1.0 - ADAM_B1) * g
    v = ADAM_B2 * v + (1.0 - ADAM_B2) * jnp.square(g)
    m_hat = m / (1.0 - ADAM_B1 ** ADAM_STEP)
    v_hat = v / (1.0 - ADAM_B2 ** ADAM_STEP)
    delta = -ADAM_LR * (m_hat / (jnp.sqrt(v_hat) + ADAM_EPS) + ADAM_WD * w)
    return delta, m, v


def _adam_call(name, w2, g2, m2, v2, tr):
    def fn(rv, cv):
        return list(_adam(*rv)), []

    width = w2.shape[1]
    return _rowcall(name, fn, [(w2, 0, width), (g2, 0, width), (m2, 0, width), (v2, 0, width)], [],
                    [(width, F32)] * 3, [], tr)


def kernel(x, mem, mem_norm, lb_logits, ffn1_norm, ffn1_w_in, ffn1_w_out, mix_norm, mem_w_kv, hgrn_w_in, hgrn_gnorm, hgrn_w_out, gmlp_w_in, gmlp_ln_g, gmlp_ln_b, gmlp_w_s, gmlp_b_s, gmlp_w_out, ffn2_norm, ffn2_w_in, ffn2_w_out, final_norm, loss_target, m_mem_norm, m_lb_logits, m_ffn1_norm, m_ffn1_w_in, m_ffn1_w_out, m_mix_norm, m_mem_w_kv, m_hgrn_w_in, m_hgrn_gnorm, m_hgrn_w_out, m_gmlp_w_in, m_gmlp_ln_g, m_gmlp_ln_b, m_gmlp_w_s, m_gmlp_b_s, m_gmlp_w_out, m_ffn2_norm, m_ffn2_w_in, m_ffn2_w_out, m_final_norm, v_mem_norm, v_lb_logits, v_ffn1_norm, v_ffn1_w_in, v_ffn1_w_out, v_mix_norm, v_mem_w_kv, v_hgrn_w_in, v_hgrn_gnorm, v_hgrn_w_out, v_gmlp_w_in, v_gmlp_ln_g, v_gmlp_ln_b, v_gmlp_w_s, v_gmlp_b_s, v_gmlp_w_out, v_ffn2_norm, v_ffn2_w_in, v_ffn2_w_out, v_final_norm):
    bl, seq, D = x.shape
    T = bl * seq
    mem_len = mem.shape[1]
    chip = 2 * lax.axis_index("x") + lax.axis_index("y")
    c_arr = lax.axis_index("c").astype(jnp.int32).reshape(1)
    TR = 1024

    big = [("ffn1_w_in", ffn1_w_in, "col"), ("ffn1_w_out", ffn1_w_out, "row"), ("mem_w_kv", mem_w_kv, "col"),
           ("hgrn_w_in", hgrn_w_in, "col"), ("hgrn_w_out", hgrn_w_out, "row"), ("gmlp_w_in", gmlp_w_in, "col"),
           ("gmlp_w_out", gmlp_w_out, "row"), ("ffn2_w_in", ffn2_w_in, "col"), ("ffn2_w_out", ffn2_w_out, "row")]
    kinds = [k for (_, _, k) in big]
    shards_bf = []
    for nm, w, _ in big:
        L, r, c = w.shape
        (wb,) = _rowcall("cast_" + nm, lambda rv, cv: ([rv[0]], []), [(w.reshape(L * r, c), 0, c)], [], [(c, BF)], [], 512)
        shards_bf.append(wb.reshape(L, r, c))
    sb = dict(zip([nm for (nm, _, _) in big], shards_bf))
    groups = [[("ffn1_w_in", 0)], [("ffn1_w_out", 0)], [("hgrn_w_in", None)], [("mem_w_kv", None)], [("hgrn_w_out", None)],
              [("ffn2_w_in", 0), ("ffn2_w_out", 0), ("gmlp_ln_g", None), ("gmlp_ln_b", None)],
              [("ffn1_w_in", 1), ("ffn1_w_out", 1)],
              [("gmlp_w_in", None), ("gmlp_w_out", None)],
              [("ffn2_w_in", 1), ("ffn2_w_out", 1)]]
    kind_of = {nm: k for (nm, _, k) in big}
    for nm, vec in (("gmlp_ln_g", gmlp_ln_g), ("gmlp_ln_b", gmlp_ln_b)):
        sb[nm] = vec.reshape(1, 1, -1)
        kind_of[nm] = "vec"
    gathered = {nm: [None, None] for nm in ("ffn1_w_in", "ffn1_w_out", "ffn2_w_in", "ffn2_w_out")}
    for gi, grp in enumerate(groups):
        outs = _allgather_seq("gather_%d" % gi, [(sb[nm], kind_of[nm], l) for (nm, l) in grp], gi)
        for (nm, l), o in zip(grp, outs):
            if l is None:
                gathered[nm] = o
            else:
                gathered[nm][l] = o

    ln_w = GM_GROUPS * GM_GROUP_DIM
    ln_g_full, ln_b_full = gathered["gmlp_ln_g"].reshape(1, ln_w), gathered["gmlp_ln_b"].reshape(1, ln_w)

    def rms_fwd(name, xin, g):
        (h,) = _rowcall(name, lambda rv, cv: ([_rmsnorm(rv[0], cv[0])], []), [(xin, 0, D)], [g.reshape(1, D)], [(D, BF)], [], TR)
        return h

    def ffn_fwd(tag, xin, g, w_in, w_out, layer):
        dff = w_out[layer].shape[1]
        h = rms_fwd("rms_" + tag, xin, g)
        zg, zu, a = _ffn_in_swiglu("ffn_in_" + tag, h, w_in[layer], 1024, dff // 2)
        xo = _mm("ffn_out_" + tag, a, w_out[layer], "nn", F32, 1024, 1024, dff, scale=0.5, res=xin, b_lead=0)
        return xo, (xin, h, zg, zu, a)

    def ffn_bwd(tag, dxo, saved, g, w_in, w_out, layer):
        xin, h, zg, zu, a = saved
        dff = w_out[layer].shape[1]
        dw_out = _mm_tn_pair("ffn_dwo_" + tag, a, dxo, "row", c_arr, dff // 2, T, scale=0.5)
        dz = _ffn_da_swiglu("ffn_da_" + tag, dxo, w_out[layer], zg, zu, 512)
        dw_in = _mm_tn_pair("ffn_dwi_" + tag, h, dz, "col", c_arr, 512, T)
        dx, dg = _mm_dh_rms("ffn_dh_" + tag, dz, w_in[layer], xin, g.reshape(1, D), dxo, 512)
        return dx, dg, dw_in, dw_out

    def rms_bwd(name, xin, g, dh, dres):
        def fn(rv, cv):
            _, vjp = jax.vjp(_rmsnorm, rv[0], cv[0])
            dx, dg = vjp(rv[1])
            if dres is not None:
                dx = dx + rv[2]
            return [dx], [dg]

        rows = [(xin, 0, D), (dh, 0, D)] + ([(dres, 0, D)] if dres is not None else [])
        dx, dg = _rowcall(name, fn, rows, [g.reshape(1, D)], [(D, F32)], [((1, D), F32)], TR)
        return dx, dg

    x0 = x.reshape(T, D)
    tgt = loss_target.reshape(T, D)
    mem2 = mem.reshape(bl * mem_len, D)
    memn = rms_fwd("rms_mem", mem2, mem_norm)

    x1, sv_f10 = ffn_fwd("f1l0", x0, ffn1_norm[0], gathered["ffn1_w_in"], gathered["ffn1_w_out"], 0)
    h_m0 = rms_fwd("rms_mix0", x1, mix_norm[0])
    z_m0 = _mm("mix_in_0", h_m0, gathered["hgrn_w_in"], "nn", F32, 2048, 512, D, b_lead=0)
    kv = [_mm("kv_%d" % i, memn, gathered["mem_w_kv"], "nn", F32, 512, 512, D, b_lead=i) for i in range(2)]
    cat0, stash0 = _hgrn_fwd2(z_m0, lb_logits, hgrn_gnorm, kv[0], bl, seq)
    x2 = _mm("mix_out_0", cat0, gathered["hgrn_w_out"], "nn", F32, 1024, 1024, cat0.shape[1], res=x1, b_lead=0)
    x3, sv_f20 = ffn_fwd("f2l0", x2, ffn2_norm[0], gathered["ffn2_w_in"], gathered["ffn2_w_out"], 0)
    x4, sv_f11 = ffn_fwd("f1l1", x3, ffn1_norm[1], gathered["ffn1_w_in"], gathered["ffn1_w_out"], 1)
    h_m1 = rms_fwd("rms_mix1", x4, mix_norm[1])
    z_m1 = _mm("mix_in_1", h_m1, gathered["gmlp_w_in"], "nn", F32, 2048, 512, D, b_lead=0)
    nc1 = seq // GM_CHUNK
    w_s, b_s = gmlp_w_s[0], gmlp_b_s[0]
    cat1 = _gmlp_fwd(z_m1, ln_g_full, ln_b_full, w_s, b_s, kv[1], bl, nc1)
    x5 = _mm("mix_out_1", cat1, gathered["gmlp_w_out"], "nn", F32, 1024, 1024, cat1.shape[1], res=x4, b_lead=0)
    x6, sv_f21 = ffn_fwd("f2l1", x5, ffn2_norm[1], gathered["ffn2_w_in"], gathered["ffn2_w_out"], 1)

    def head(rv, cv):
        def f(xx, gg):
            err = _rmsnorm(xx, gg) - rv[1]
            return 0.5 * jnp.sum(jnp.mean(err * err, axis=-1, keepdims=True), axis=0, keepdims=True)

        ls, vjp = jax.vjp(f, rv[0], cv[0])
        dx, dg = vjp(jnp.ones((1, 1), F32))
        return [dx], [dg, jnp.broadcast_to(ls, (1, 128))]

    dx6, d_final, loss_part = _rowcall("loss_head", head, [(x6, 0, D), (tgt, 0, D)], [final_norm.reshape(1, D)],
                                       [(D, F32)], [((1, D), F32), ((1, 128), F32)], TR)

    rs_out = {}
    n_gather = len(groups)

    def rs(gi, items):
        outs = _rs_chips_seq("reduce_%d" % gi, [p for (_, p, _) in items], [k for (_, _, k) in items], n_gather + gi)
        for i, (key, _, _) in enumerate(items):
            rs_out[key] = (outs[2 * i], outs[2 * i + 1])

    dx5, dg_f21, dwi_f21, dwo_f21 = ffn_bwd("f2l1", dx6, sv_f21, ffn2_norm[1], gathered["ffn2_w_in"], gathered["ffn2_w_out"], 1)
    rs(0, [(("ffn2_w_out", 1), dwo_f21, "row"), (("ffn2_w_in", 1), dwi_f21, "col")])
    dcat1 = _mm("mix_dcat_1", dx5, gathered["gmlp_w_out"], "nt", F32, 2048, 1024, D, b_lead=0)
    dwo_m1 = _mm_tn_pair("mix_dwo_1", cat1, dx5, "row", c_arr, 1024, T)
    dz_m1, dkv1, d_lng, d_lnb, d_ws, d_bs = _gmlp_bwd(z_m1, dcat1, ln_g_full, ln_b_full, w_s, b_s, kv[1], bl, nc1)
    dx4, dg_m1 = _mm_dh_rms("mix_dh_1", dz_m1, gathered["gmlp_w_in"], x4, mix_norm[1].reshape(1, D), dx5, 512)
    dwi_m1 = _mm_tn_pair("mix_dwi_1", h_m1, dz_m1, "col", c_arr, 1024, T)
    rs(1, [(("gmlp_w_out", 0), dwo_m1, "row"), (("gmlp_w_in", 0), dwi_m1, "col")])
    dx3, dg_f11, dwi_f11, dwo_f11 = ffn_bwd("f1l1", dx4, sv_f11, ffn1_norm[1], gathered["ffn1_w_in"], gathered["ffn1_w_out"], 1)
    rs(2, [(("ffn1_w_out", 1), dwo_f11, "row"), (("ffn1_w_in", 1), dwi_f11, "col")])

    dx2, dg_f20, dwi_f20, dwo_f20 = ffn_bwd("f2l0", dx3, sv_f20, ffn2_norm[0], gathered["ffn2_w_in"], gathered["ffn2_w_out"], 0)
    rs(3, [(("ffn2_w_out", 0), dwo_f20, "row"), (("ffn2_w_in", 0), dwi_f20, "col")])
    dcat0 = _mm("mix_dcat_0", dx2, gathered["hgrn_w_out"], "nt", F32, 2048, 1024, D, b_lead=0)
    dwo_m0 = _mm_tn_pair("mix_dwo_0", cat0, dx2, "row", c_arr, 1024, T)
    dz_m0, dkv0, d_lb, d_gn = _hgrn_bwd2(z_m0, dcat0, stash0, lb_logits, hgrn_gnorm, kv[0], bl, seq)
    dx1, dg_m0 = _mm_dh_rms("mix_dh_0", dz_m0, gathered["hgrn_w_in"], x1, mix_norm[0].reshape(1, D), dx2, 512)
    dwi_m0 = _mm_tn_pair("mix_dwi_0", h_m0, dz_m0, "col", c_arr, 1024, T)
    rs(4, [(("hgrn_w_out", 0), dwo_m0, "row"), (("hgrn_w_in", 0), dwi_m0, "col")])

    dwkv = [_mm_tn_pair("kv_dw_%d" % i, memn, dkv, "col", c_arr, 1024, 512) for i, dkv in enumerate([dkv0, dkv1])]
    rs(5, [(("mem_w_kv", 0), dwkv[0], "col"), (("mem_w_kv", 1), dwkv[1], "col")])
    dmemn = _mm("kv_dx_0", dkv0, gathered["mem_w_kv"], "nt", F32, 512, 512, 1024, b_lead=0)
    dmemn = _mm("kv_dx_1", dkv1, gathered["mem_w_kv"], "nt", F32, 512, 512, 1024, res=dmemn, b_lead=1)
    _, d_memnorm = rms_bwd("rms_bwd_mem", mem2, mem_norm, dmemn, None)

    dx0, dg_f10, dwi_f10, dwo_f10 = ffn_bwd("f1l0", dx1, sv_f10, ffn1_norm[0], gathered["ffn1_w_in"], gathered["ffn1_w_out"], 0)
    rs(6, [(("ffn1_w_out", 0), dwo_f10, "row")])
    rs(7, [(("ffn1_w_in", 0), dwi_f10, "col")])

    shard_grads = []
    for (nm, w, k) in big:
        per_layer = []
        for l in range(w.shape[0]):
            own, got = rs_out[(nm, l)]
            per_layer.append(_finish_share("finish_%s_%d" % (nm, l), own[None], got[:, None], k, c_arr))
        shard_grads.append(per_layer[0] if len(per_layer) == 1 else jnp.concatenate(per_layer, axis=0))

    big_w = [w for (_, w, _) in big]
    big_m = [m_ffn1_w_in, m_ffn1_w_out, m_mem_w_kv, m_hgrn_w_in, m_hgrn_w_out, m_gmlp_w_in, m_gmlp_w_out, m_ffn2_w_in, m_ffn2_w_out]
    big_v = [v_ffn1_w_in, v_ffn1_w_out, v_mem_w_kv, v_hgrn_w_in, v_hgrn_w_out, v_gmlp_w_in, v_gmlp_w_out, v_ffn2_w_in, v_ffn2_w_out]
    big_out = {}
    for (nm, w, _), g, m, v in zip(big, shard_grads, big_m, big_v):
        L, r, c = w.shape
        d2, m2, v2 = _adam_call("adam_" + nm, w.reshape(L * r, c), g.reshape(L * r, c), m.reshape(L * r, c),
                                v.reshape(L * r, c), 256)
        big_out[nm] = (g, d2.reshape(w.shape), m2.reshape(w.shape), v2.reshape(w.shape))

    d_ffn1n = _two_rows(dg_f10, dg_f11)
    d_mixn = _two_rows(dg_m0, dg_m1)
    d_ffn2n = _two_rows(dg_f20, dg_f21)
    small_parts = [loss_part[:, :1], d_memnorm, d_lb, d_ffn1n, d_mixn, d_gn, d_lng, d_lnb, d_ws, d_bs, d_ffn2n, d_final]
    red_shapes = [(1,), mem_norm.shape, lb_logits.shape, ffn1_norm.shape, mix_norm.shape, hgrn_gnorm.shape, (1, ln_w), (1, ln_w),
                  gmlp_w_s.shape, gmlp_b_s.shape, ffn2_norm.shape, final_norm.shape]
    red = _small_allreduce(_pack(small_parts, _rows_needed(red_shapes)), "reduce_small")
    (loss_v, g_memn, g_lb, g_f1n, g_mixn, g_gn, g_lng_full, g_lnb_full, g_ws, g_bs, g_f2n, g_fin) = _unpack(red, red_shapes)
    lsh = gmlp_ln_g.shape[1]
    g_lng = lax.dynamic_slice(g_lng_full, (0, chip * lsh), (1, lsh))
    g_lnb = lax.dynamic_slice(g_lnb_full, (0, chip * lsh), (1, lsh))
    small_w = [mem_norm, lb_logits, ffn1_norm, mix_norm, hgrn_gnorm, gmlp_ln_g, gmlp_ln_b, gmlp_w_s, gmlp_b_s, ffn2_norm, final_norm]
    small_g = [g_memn, g_lb, g_f1n, g_mixn, g_gn, g_lng, g_lnb, g_ws, g_bs, g_f2n, g_fin]
    small_m = [m_mem_norm, m_lb_logits, m_ffn1_norm, m_mix_norm, m_hgrn_gnorm, m_gmlp_ln_g, m_gmlp_ln_b, m_gmlp_w_s, m_gmlp_b_s, m_ffn2_norm, m_final_norm]
    small_v = [v_mem_norm, v_lb_logits, v_ffn1_norm, v_mix_norm, v_hgrn_gnorm, v_gmlp_ln_g, v_gmlp_ln_b, v_gmlp_w_s, v_gmlp_b_s, v_ffn2_norm, v_final_norm]
    sshapes = [w.shape for w in small_w]
    nrow = _rows_needed(sshapes)
    d_p, m_p, v_p = _adam_call("adam_small", _pack(small_w, nrow), _pack(small_g, nrow), _pack(small_m, nrow), _pack(small_v, nrow), nrow)
    s_delta, s_m, s_v = _unpack(d_p, sshapes), _unpack(m_p, sshapes), _unpack(v_p, sshapes)
    small_names = ["mem_norm", "lb_logits", "ffn1_norm", "mix_norm", "hgrn_gnorm", "gmlp_ln_g", "gmlp_ln_b", "gmlp_w_s", "gmlp_b_s", "ffn2_norm", "final_norm"]
    small_out = {nm: (g.reshape(w.shape), d, m, v) for nm, w, g, d, m, v in zip(small_names, small_w, small_g, s_delta, s_m, s_v)}

    order = ["mem_norm", "lb_logits", "ffn1_norm", "ffn1_w_in", "ffn1_w_out", "mix_norm", "mem_w_kv", "hgrn_w_in", "hgrn_gnorm",
             "hgrn_w_out", "gmlp_w_in", "gmlp_ln_g", "gmlp_ln_b", "gmlp_w_s", "gmlp_b_s", "gmlp_w_out", "ffn2_norm", "ffn2_w_in",
             "ffn2_w_out", "final_norm"]
    allo = {**big_out, **small_out}
    grad_x = dx0.reshape(x.shape)
    return (loss_v.reshape(()), grad_x, *[allo[n][0] for n in order], *[allo[n][1] for n in order],
            *[allo[n][2] for n in order], *[allo[n][3] for n in order])
```

```python
import functools

import jax
import jax.numpy as jnp
from jax import lax
from jax.experimental import pallas as pl
from jax.experimental.pallas import tpu as pltpu
from jax.experimental.pallas import tpu_sc as plsc

BF = jnp.bfloat16
F32 = jnp.float32
MESH = pl.DeviceIdType.MESH

EPS = 1e-6
D_MODEL = 1024
HG_HEADS = 8
HG_DIM = 128
HG_CHUNK = 64
GM_CHUNK = 128
GM_GROUPS = 8
GM_GROUP_DIM = 256
XA_HEADS = 4
XA_DIM = 256
ADAM_LR = 0.001
ADAM_B1 = 0.9
ADAM_B2 = 0.999
ADAM_EPS = 1e-08
ADAM_WD = 0.01
ADAM_STEP = 10

VMEM_CAP_BYTES = 60 * 1024 * 1024
LANES = 1024


def _pick(n, cap, mult=16):
    if n <= cap:
        return n
    for d in range(cap - cap % mult, 0, -mult):
        if n % d == 0:
            return d
    raise ValueError((n, cap, mult))


def _dg(a, b, ca, cb):
    return lax.dot_general(a.astype(BF), b.astype(BF), (((ca,), (cb,)), ((), ())), preferred_element_type=F32)


@jax.custom_vjp
def dot_nn(a, b):
    return _dg(a, b, 1, 0)


def _nn_fwd(a, b):
    return _dg(a, b, 1, 0), (a, b)


def _nn_bwd(r, g):
    a, b = r
    return _dg(g, b, 1, 1), _dg(a, g, 0, 0)


dot_nn.defvjp(_nn_fwd, _nn_bwd)


@jax.custom_vjp
def dot_nt(a, b):
    return _dg(a, b, 1, 1)


def _nt_fwd(a, b):
    return _dg(a, b, 1, 1), (a, b)


def _nt_bwd(r, g):
    a, b = r
    return _dg(g, b, 1, 0), _dg(g, a, 0, 0)


dot_nt.defvjp(_nt_fwd, _nt_bwd)


@jax.custom_vjp
def dot_tn(a, b):
    return _dg(a, b, 0, 0)


def _tn_fwd(a, b):
    return _dg(a, b, 0, 0), (a, b)


def _tn_bwd(r, g):
    a, b = r
    return _dg(b, g, 1, 1), _dg(a, g, 1, 0)


dot_tn.defvjp(_tn_fwd, _tn_bwd)


def _rmsnorm(x, g):
    return x * lax.rsqrt(jnp.mean(x * x, axis=-1, keepdims=True) + EPS) * g


def _silu(x):
    return x * jax.nn.sigmoid(x)


@jax.custom_vjp
def _gelu(x):
    return 0.5 * x * (1.0 + lax.erf(x * (0.5 ** 0.5)))


def _gelu_fwd(x):
    return _gelu(x), x


def _gelu_bwd(x, g):
    t = x * (0.5 ** 0.5)
    cdf = 0.5 * (1.0 + lax.erf(t))
    return (g * (cdf + x * (jnp.exp(-(t * t)) * (0.5 / 3.141592653589793) ** 0.5)),)


_gelu.defvjp(_gelu_fwd, _gelu_bwd)


def _softmax_last(s):
    m = lax.stop_gradient(jnp.max(s, axis=-1, keepdims=True))
    e = jnp.exp(s - m)
    return e / jnp.sum(e, axis=-1, keepdims=True)


def _tril(n):
    r = lax.broadcasted_iota(jnp.int32, (n, n), 0)
    c = lax.broadcasted_iota(jnp.int32, (n, n), 1)
    return r >= c


def _cumsum_rows(l):
    n = l.shape[0]
    return lax.dot_general(_tril(n).astype(F32), l, (((1,), (0,)), ((), ())),
                           precision=lax.Precision.HIGHEST, preferred_element_type=F32)


def _attention(zx, mk, mv):
    s = dot_nt(zx, mk) * (XA_DIM ** -0.5)
    return dot_nn(_softmax_last(s), mv)


def _hgrn_head(zq, zf, zi, zg, l0, l1, l2, gn, S):
    m = lax.stop_gradient(jnp.maximum(jnp.maximum(l0, l1), l2))
    e0 = jnp.exp(l0 - m)
    lb = e0 / (e0 + jnp.exp(l1 - m) + jnp.exp(l2 - m))
    q = _silu(zq)
    f = lb + (1.0 - lb) * jax.nn.sigmoid(zf)
    k = 1.0 - f
    b = _cumsum_rows(jnp.log(f))
    b_last = b[HG_CHUNK - 1:HG_CHUNK, :]
    q_dec = q * jnp.exp(b)
    k_inv = k * jnp.exp(-b)
    a = jnp.where(_tril(HG_CHUNK), dot_nt(q_dec, k_inv), 0.0)
    o = dot_nn(a, zi) + dot_nn(q_dec, S)
    S_new = jnp.exp(b_last).reshape(HG_DIM, 1) * S + dot_tn(k * jnp.exp(b_last - b), zi)
    o = _rmsnorm(o, gn) * _silu(zg)
    return o, S_new


def _hgrn_block(zq, zf, zi, zg, zx, l0, l1, l2, gn, mk, mv, S):
    outs, s_new = [], []
    for h in range(HG_HEADS):
        o, sn = _hgrn_head(zq[h], zf[h], zi[h], zg[h], l0[h], l1[h], l2[h], gn, S[h])
        outs.append(o)
        s_new.append(sn)
    for a in range(XA_HEADS):
        outs.append(_attention(zx[a], mk[a], mv[a]))
    return outs, s_new


def _gmlp_block(zu, zv, zx, lng, lnb, ws, bs, mk, mv):
    gv = [_gelu(v) for v in zv]
    width = GM_GROUPS * GM_GROUP_DIM
    mu = sum(jnp.sum(g, axis=-1, keepdims=True) for g in gv) / width
    xc = [g - mu for g in gv]
    var = sum(jnp.sum(c * c, axis=-1, keepdims=True) for c in xc) / width
    r = lax.rsqrt(var + EPS)
    outs = []
    for g in range(GM_GROUPS):
        v = xc[g] * r * lng[g] + lnb[g]
        w = jnp.where(_tril(GM_CHUNK), ws[g], 0.0)
        mixed = dot_nn(w, v) + bs[g].reshape(GM_CHUNK, 1)
        outs.append(_gelu(zu[g]) * mixed)
    for a in range(XA_HEADS):
        outs.append(_attention(zx[a], mk[a], mv[a]))
    return outs


def _rowcall(name, fn, rows, consts, row_outs, acc_outs, tr):
    nrows = rows[0][0].shape[0]
    tr = _pick(nrows, tr)
    n_r, n_c, n_ro, n_ao = len(rows), len(consts), len(row_outs), len(acc_outs)

    def kern(*refs):
        rv = [r[...] for r in refs[:n_r]]
        cv = [r[...] for r in refs[n_r:n_r + n_c]]
        ro_refs = refs[n_r + n_c:n_r + n_c + n_ro]
        ao_refs = refs[n_r + n_c + n_ro:]
        ro, ao = fn(rv, cv)
        for ref, v in zip(ro_refs, ro):
            ref[...] = v.astype(ref.dtype)
        if n_ao:
            @pl.when(pl.program_id(0) == 0)
            def _():
                for ref in ao_refs:
                    ref[...] = jnp.zeros(ref.shape, ref.dtype)

            for ref, v in zip(ao_refs, ao):
                ref[...] += v.astype(ref.dtype)

    in_specs = [pl.BlockSpec((tr, w), functools.partial(lambda i, cb: (i, cb), cb=cb)) for (_, cb, w) in rows]
    in_specs += [pl.BlockSpec(c.shape, lambda i: (0, 0)) for c in consts]
    out_specs = [pl.BlockSpec((tr, w), lambda i: (i, 0)) for (w, _) in row_outs]
    out_specs += [pl.BlockSpec(s, lambda i: (0, 0)) for (s, _) in acc_outs]
    out_shape = [jax.ShapeDtypeStruct((nrows, w), dt) for (w, dt) in row_outs]
    out_shape += [jax.ShapeDtypeStruct(s, dt) for (s, dt) in acc_outs]
    est = sum(tr * w * a.dtype.itemsize for (a, _, w) in rows) + sum(tr * w * jnp.dtype(dt).itemsize for (w, dt) in row_outs)
    est += sum(c.size * c.dtype.itemsize for c in consts)
    outs = pl.pallas_call(
        kern, grid=(nrows // tr,), in_specs=in_specs, out_specs=out_specs, out_shape=out_shape, name=name,
        compiler_params=pltpu.CompilerParams(dimension_semantics=("arbitrary",),
                                             vmem_limit_bytes=VMEM_CAP_BYTES),
    )(*[a for (a, _, _) in rows], *consts)
    return outs


def _mm(name, a, b, mode, out_dtype, tm, tn, tk, scale=1.0, res=None, a_lead=None, b_lead=None, norm_gain=None):
    ash = a.shape[-2:]
    bsh = b.shape[-2:]
    if mode == "nn":
        (M, K), (K2, N) = ash, bsh
    elif mode == "nt":
        (M, K), (N, K2) = ash, bsh
    else:
        (K, M), (K2, N) = ash, bsh
    assert K == K2, (name, a.shape, b.shape)
    tm, tn, tk = min(tm, M), min(tn, N), min(tk, K)
    assert M % tm == 0 and N % tn == 0 and K % tk == 0, (name, M, N, K, tm, tn, tk)
    nk = K // tk
    dims = {"nn": (1, 0), "nt": (1, 1), "tn": (0, 0)}[mode]

    def lead(spec_shape, index_fn, lead_idx):
        if lead_idx is None:
            return pl.BlockSpec(spec_shape, index_fn)
        return pl.BlockSpec((None,) + spec_shape, lambda i, j, k: (lead_idx,) + index_fn(i, j, k))

    if mode == "tn":
        a_spec = lead((tk, tm), lambda i, j, k: (k, i), a_lead)
    else:
        a_spec = lead((tm, tk), lambda i, j, k: (i, k), a_lead)
    if mode == "nt":
        b_spec = lead((tn, tk), lambda i, j, k: (j, k), b_lead)
    else:
        b_spec = lead((tk, tn), lambda i, j, k: (k, j), b_lead)
    o_spec = pl.BlockSpec((tm, tn), lambda i, j, k: (i, j))
    has_res = res is not None
    has_norm = norm_gain is not None
    assert not has_norm or tn == N

    def kern(*refs):
        a_ref, b_ref = refs[0], refs[1]
        pos = 2
        res_ref = gain_ref = h_ref = None
        if has_res:
            res_ref, pos = refs[pos], pos + 1
        if has_norm:
            gain_ref, pos = refs[pos], pos + 1
        o_ref, pos = refs[pos], pos + 1
        if has_norm:
            h_ref = refs[pos]
        acc_ref = refs[-1] if nk > 1 else None
        p = lax.dot_general(a_ref[...].astype(BF), b_ref[...].astype(BF), (((dims[0],), (dims[1],)), ((), ())),
                            preferred_element_type=F32)

        def finish(v):
            if scale != 1.0:
                v = v * scale
            if has_res:
                v = res_ref[...] + v
            o_ref[...] = v.astype(o_ref.dtype)
            if has_norm:
                h_ref[...] = _rmsnorm(v, gain_ref[...]).astype(h_ref.dtype)

        if nk == 1:
            finish(p)
        else:
            k = pl.program_id(2)

            @pl.when(k == 0)
            def _():
                acc_ref[...] = p

            @pl.when(k > 0)
            def _():
                acc_ref[...] += p

            @pl.when(k == nk - 1)
            def _():
                finish(acc_ref[...])

    ins = [a, b] + ([res] if has_res else []) + ([norm_gain] if has_norm else [])
    in_specs = [a_spec, b_spec] + ([o_spec] if has_res else [])
    in_specs += [pl.BlockSpec((1, N), lambda i, j, k: (0, 0))] if has_norm else []
    out_sd = jax.ShapeDtypeStruct((M, N), out_dtype)
    return pl.pallas_call(
        kern, grid=(M // tm, N // tn, nk), in_specs=in_specs,
        out_specs=[o_spec, o_spec] if has_norm else o_spec,
        out_shape=[out_sd, jax.ShapeDtypeStruct((M, N), BF)] if has_norm else out_sd,
        scratch_shapes=[pltpu.VMEM((tm, tn), F32)] if nk > 1 else [],
        name=name,
        compiler_params=pltpu.CompilerParams(dimension_semantics=("parallel", "parallel", "arbitrary"),
                                             vmem_limit_bytes=VMEM_CAP_BYTES),
    )(*ins)


def _ffn_in_swiglu(name, h, w3, tm, tn):
    T, D = h.shape
    dff = w3.shape[2] // 2
    tm = min(tm, T)
    assert T % tm == 0 and dff % tn == 0
    nj = dff // tn

    def kern(h_ref, wg_ref, wu_ref, zg_ref, zu_ref, a_ref):
        hb = h_ref[...]
        g = jnp.dot(hb, wg_ref[...], preferred_element_type=F32).astype(BF)
        u = jnp.dot(hb, wu_ref[...], preferred_element_type=F32).astype(BF)
        zg_ref[...] = g
        zu_ref[...] = u
        a_ref[...] = (_silu(g.astype(F32)) * u.astype(F32)).astype(BF)

    o_spec = pl.BlockSpec((tm, tn), lambda i, j: (i, j))
    return pl.pallas_call(
        kern, grid=(T // tm, nj),
        in_specs=[pl.BlockSpec((tm, D), lambda i, j: (i, 0)),
                  pl.BlockSpec((None, D, tn), lambda i, j: (0, 0, j)),
                  pl.BlockSpec((None, D, tn), lambda i, j: (0, 0, j + nj))],
        out_specs=[o_spec, o_spec, o_spec],
        out_shape=[jax.ShapeDtypeStruct((T, dff), BF)] * 3, name=name,
        compiler_params=pltpu.CompilerParams(dimension_semantics=("parallel", "arbitrary"),
                                             vmem_limit_bytes=VMEM_CAP_BYTES),
    )(h, w3, w3)


def _ffn_da_swiglu(name, dxo, w3, zg, zu, tm):
    T, D = dxo.shape
    dff = w3.shape[1]
    tm = min(tm, T)
    assert T % tm == 0 and dff % 2 == 0
    hc = dff // 2

    def kern(d_ref, w_ref, g_ref, u_ref, dz_ref):
        db = (d_ref[...] * 0.5).astype(BF)
        for s in range(2):
            cols = slice(s * hc, (s + 1) * hc)
            da = lax.dot_general(db, w_ref[cols, :], (((1,), (1,)), ((), ())), preferred_element_type=F32)
            g = g_ref[:, cols].astype(F32)
            sg = 1.0 / (1.0 + jnp.exp(-g))
            gs = g * sg
            dab = da.astype(BF)
            dz_ref[:, cols] = (dab * u_ref[:, cols]) * (sg + gs * (1.0 - sg)).astype(BF)
            dz_ref[:, dff + s * hc:dff + (s + 1) * hc] = dab * gs.astype(BF)

    row = lambda w: pl.BlockSpec((tm, w), lambda i: (i, 0))
    return pl.pallas_call(
        kern, grid=(T // tm,),
        in_specs=[row(D), pl.BlockSpec((None, dff, D), lambda i: (0, 0, 0), pipeline_mode=pl.Buffered(1)), row(dff), row(dff)],
        out_specs=row(2 * dff), out_shape=jax.ShapeDtypeStruct((T, 2 * dff), BF), name=name,
        compiler_params=pltpu.CompilerParams(dimension_semantics=("arbitrary",), vmem_limit_bytes=VMEM_CAP_BYTES),
    )(dxo, w3, zg, zu)


def _mm_dh_rms(name, dz, w3, xin, g, dres, tm):
    T, K = dz.shape
    D = w3.shape[1]
    tm = min(tm, T)
    assert T % tm == 0

    def kern(dz_ref, w_ref, x_ref, g_ref, r_ref, dx_ref, dg_ref):
        dh = lax.dot_general(dz_ref[...], w_ref[...], (((1,), (1,)), ((), ())), preferred_element_type=F32)
        _, vjp = jax.vjp(_rmsnorm, x_ref[...], g_ref[...])
        dx, dg = vjp(dh)
        dx_ref[...] = dx + r_ref[...]

        @pl.when(pl.program_id(0) == 0)
        def _():
            dg_ref[...] = jnp.zeros(dg_ref.shape, F32)

        dg_ref[...] += dg

    row = lambda w: pl.BlockSpec((tm, w), lambda i: (i, 0))
    one = pl.BlockSpec((1, D), lambda i: (0, 0))
    return pl.pallas_call(
        kern, grid=(T // tm,),
        in_specs=[row(K), pl.BlockSpec((None, D, K), lambda i: (0, 0, 0), pipeline_mode=pl.Buffered(1)), row(D), one, row(D)],
        out_specs=[row(D), one], out_shape=[jax.ShapeDtypeStruct((T, D), F32), jax.ShapeDtypeStruct((1, D), F32)], name=name,
        compiler_params=pltpu.CompilerParams(dimension_semantics=("arbitrary",), vmem_limit_bytes=VMEM_CAP_BYTES),
    )(dz, w3, xin, g, dres)


def _mm_tn_pair(name, a, b, kind, c_arr, tq, tk, scale=1.0):
    T, M = a.shape
    _, N = b.shape
    tk = min(tk, T)
    assert T % tk == 0
    nk = T // tk
    if kind == "col":
        hm = M // 2
        assert N % tq == 0
        nq = N // tq
        tile = (hm, tq)
        a_spec = pl.BlockSpec((tk, hm), lambda h, q, k, c: (k, jnp.bitwise_xor(h, 1 - c[0])))
        b_spec = pl.BlockSpec((tk, tq), lambda h, q, k, c: (k, q))
        o_spec = pl.BlockSpec(tile, lambda h, q, k, c: (0, q * h))
        out_sd = (hm, N)
    else:
        hn = N // 2
        assert M % tq == 0
        nq = M // tq
        tile = (tq, hn)
        a_spec = pl.BlockSpec((tk, tq), lambda h, q, k, c: (k, q))
        b_spec = pl.BlockSpec((tk, hn), lambda h, q, k, c: (k, jnp.bitwise_xor(h, 1 - c[0])))
        o_spec = pl.BlockSpec(tile, lambda h, q, k, c: (q * h, 0))
        out_sd = (M, hn)

    def kern(c_ref, a_ref, b_ref, o_ref, acc, stage, recv, ssem, rsem):
        h, q, k = pl.program_id(0), pl.program_id(1), pl.program_id(2)
        x, y, c, _ = _place()
        p = lax.dot_general(a_ref[...].astype(BF), b_ref[...].astype(BF), (((0,), (0,)), ((), ())), preferred_element_type=F32)

        @pl.when(k == 0)
        def _():
            acc[...] = p

        @pl.when(k > 0)
        def _():
            acc[...] += p

        def send(slot, qq):
            return pltpu.make_async_remote_copy(src_ref=stage.at[slot], dst_ref=recv.at[qq], send_sem=ssem.at[slot],
                                                recv_sem=rsem.at[qq], device_id=(x, y, 1 - c), device_id_type=MESH)

        last = k == nk - 1

        @pl.when(jnp.logical_and(last, h == 0))
        def _():
            slot = q % 2

            @pl.when(q >= 2)
            def _():
                send(slot, q).wait_send()

            stage[slot] = (acc[...] * scale).astype(BF)
            send(slot, q).start()

        @pl.when(jnp.logical_and(last, h == 1))
        def _():
            @pl.when(q == 0)
            def _():
                for s in range(min(nq, 2)):
                    send(s, 0).wait_send()

            send(0, q).wait_recv()
            o_ref[...] = (acc[...] * scale + recv[q].astype(F32)).astype(o_ref.dtype)

    tb = tile[0] * tile[1]
    est = tb * (4 + 2 * 2 + nq * 2 + 2 * 2) + 2 * tk * (a_spec.block_shape[1] + b_spec.block_shape[1]) * 2 * 2
    return pl.pallas_call(
        kern,
        grid_spec=pltpu.PrefetchScalarGridSpec(
            num_scalar_prefetch=1, grid=(2, nq, nk), in_specs=[a_spec, b_spec], out_specs=o_spec,
            scratch_shapes=[pltpu.VMEM(tile, F32), pltpu.VMEM((2,) + tile, BF), pltpu.VMEM((nq,) + tile, BF),
                            pltpu.SemaphoreType.DMA((2,)), pltpu.SemaphoreType.DMA((nq,))]),
        out_shape=jax.ShapeDtypeStruct(out_sd, BF), name=name,
        compiler_params=pltpu.CompilerParams(dimension_semantics=("arbitrary", "arbitrary", "arbitrary"),
                                             vmem_limit_bytes=VMEM_CAP_BYTES),
    )(c_arr, a, b)


def _hgrn_pieces(z_ref):
    W = HG_HEADS * HG_DIM
    zq = [z_ref[:, h * HG_DIM:(h + 1) * HG_DIM] for h in range(HG_HEADS)]
    zf = [z_ref[:, W + h * HG_DIM:W + (h + 1) * HG_DIM] for h in range(HG_HEADS)]
    zi = [z_ref[:, 2 * W + h * HG_DIM:2 * W + (h + 1) * HG_DIM] for h in range(HG_HEADS)]
    zg = [z_ref[:, 3 * W + h * HG_DIM:3 * W + (h + 1) * HG_DIM] for h in range(HG_HEADS)]
    zx = [z_ref[:, 4 * W + a * XA_DIM:4 * W + (a + 1) * XA_DIM] for a in range(XA_HEADS)]
    return zq, zf, zi, zg, zx


def _kv_pieces(kv_ref):
    W = XA_HEADS * XA_DIM
    mk = [kv_ref[:, a * XA_DIM:(a + 1) * XA_DIM] for a in range(XA_HEADS)]
    mv = [kv_ref[:, W + a * XA_DIM:W + (a + 1) * XA_DIM] for a in range(XA_HEADS)]
    return mk, mv


def _lb_pieces(lb_ref):
    return [[lb_ref[r:r + 1, h * HG_DIM:(h + 1) * HG_DIM] for h in range(HG_HEADS)] for r in range(3)]


def _hgrn_fwd(z, lb_logits, gnorm, kv, bl, nc):
    T, zw = z.shape
    mem_len = kv.shape[0] // bl
    cat_w = HG_HEADS * HG_DIM + XA_HEADS * XA_DIM

    def kern(z_ref, lb_ref, gn_ref, kv_ref, cat_ref, st_ref, s_scr):
        @pl.when(pl.program_id(1) == 0)
        def _():
            s_scr[...] = jnp.zeros(s_scr.shape, F32)

        st_ref[...] = s_scr[...]
        zq, zf, zi, zg, zx = _hgrn_pieces(z_ref)
        mk, mv = _kv_pieces(kv_ref)
        l0, l1, l2 = _lb_pieces(lb_ref)
        S = [s_scr[h] for h in range(HG_HEADS)]
        outs, s_new = _hgrn_block(zq, zf, zi, zg, zx, l0, l1, l2, gn_ref[...], mk, mv, S)
        for h in range(HG_HEADS):
            cat_ref[:, h * HG_DIM:(h + 1) * HG_DIM] = outs[h].astype(cat_ref.dtype)
            s_scr[h] = s_new[h]
        base = HG_HEADS * HG_DIM
        for a in range(XA_HEADS):
            cat_ref[:, base + a * XA_DIM:base + (a + 1) * XA_DIM] = outs[HG_HEADS + a].astype(cat_ref.dtype)

    return pl.pallas_call(
        kern, grid=(bl, nc),
        in_specs=[pl.BlockSpec((HG_CHUNK, zw), lambda b, n: (b * nc + n, 0)),
                  pl.BlockSpec(lb_logits.shape, lambda b, n: (0, 0)),
                  pl.BlockSpec(gnorm.shape, lambda b, n: (0, 0)),
                  pl.BlockSpec((mem_len, kv.shape[1]), lambda b, n: (b, 0))],
        out_specs=[pl.BlockSpec((HG_CHUNK, cat_w), lambda b, n: (b * nc + n, 0)),
                   pl.BlockSpec((None, HG_HEADS, HG_DIM, HG_DIM), lambda b, n: (b * nc + n, 0, 0, 0))],
        out_shape=[jax.ShapeDtypeStruct((T, cat_w), BF),
                   jax.ShapeDtypeStruct((bl * nc, HG_HEADS, HG_DIM, HG_DIM), F32)],
        scratch_shapes=[pltpu.VMEM((HG_HEADS, HG_DIM, HG_DIM), F32)],
        name="hgrn_fwd",
        compiler_params=pltpu.CompilerParams(dimension_semantics=("arbitrary", "arbitrary"), vmem_limit_bytes=VMEM_CAP_BYTES),
    )(z, lb_logits, gnorm, kv)


def _hgrn_bwd(z, dcat, stash, lb_logits, gnorm, kv, bl, nc):
    T, zw = z.shape
    mem_len = kv.shape[0] // bl
    cat_w = dcat.shape[1]

    def kern(z_ref, dc_ref, st_ref, lb_ref, gn_ref, kv_ref, dz_ref, dkv_ref, dlb_ref, dgn_ref, ds_scr):
        first = jnp.logical_and(pl.program_id(0) == 0, pl.program_id(1) == 0)

        @pl.when(pl.program_id(1) == 0)
        def _():
            ds_scr[...] = jnp.zeros(ds_scr.shape, F32)
            dkv_ref[...] = jnp.zeros(dkv_ref.shape, F32)

        @pl.when(first)
        def _():
            dlb_ref[...] = jnp.zeros(dlb_ref.shape, F32)
            dgn_ref[...] = jnp.zeros(dgn_ref.shape, F32)

        zq, zf, zi, zg, zx = _hgrn_pieces(z_ref)
        mk, mv = _kv_pieces(kv_ref)
        l0, l1, l2 = _lb_pieces(lb_ref)
        S = [st_ref[h] for h in range(HG_HEADS)]
        _, vjp = jax.vjp(_hgrn_block, zq, zf, zi, zg, zx, l0, l1, l2, gn_ref[...], mk, mv, S)
        d_outs = [dc_ref[:, h * HG_DIM:(h + 1) * HG_DIM] for h in range(HG_HEADS)]
        base = HG_HEADS * HG_DIM
        d_outs += [dc_ref[:, base + a * XA_DIM:base + (a + 1) * XA_DIM] for a in range(XA_HEADS)]
        d_s = [ds_scr[h] for h in range(HG_HEADS)]
        dzq, dzf, dzi, dzg, dzx, dl0, dl1, dl2, dgn, dmk, dmv, dS = vjp((d_outs, d_s))
        W = HG_HEADS * HG_DIM
        for h in range(HG_HEADS):
            sl = slice(h * HG_DIM, (h + 1) * HG_DIM)
            dz_ref[:, sl] = dzq[h].astype(dz_ref.dtype)
            dz_ref[:, W + h * HG_DIM:W + (h + 1) * HG_DIM] = dzf[h].astype(dz_ref.dtype)
            dz_ref[:, 2 * W + h * HG_DIM:2 * W + (h + 1) * HG_DIM] = dzi[h].astype(dz_ref.dtype)
            dz_ref[:, 3 * W + h * HG_DIM:3 * W + (h + 1) * HG_DIM] = dzg[h].astype(dz_ref.dtype)
            ds_scr[h] = dS[h]
            dlb_ref[0:1, sl] += dl0[h]
            dlb_ref[1:2, sl] += dl1[h]
            dlb_ref[2:3, sl] += dl2[h]
        dgn_ref[...] += dgn
        KW = XA_HEADS * XA_DIM
        for a in range(XA_HEADS):
            dz_ref[:, 4 * W + a * XA_DIM:4 * W + (a + 1) * XA_DIM] = dzx[a].astype(dz_ref.dtype)
            dkv_ref[:, a * XA_DIM:(a + 1) * XA_DIM] += dmk[a]
            dkv_ref[:, KW + a * XA_DIM:KW + (a + 1) * XA_DIM] += dmv[a]

    rev = lambda b, n: (b * nc + (nc - 1 - n), 0)
    return pl.pallas_call(
        kern, grid=(bl, nc),
        in_specs=[pl.BlockSpec((HG_CHUNK, zw), rev),
                  pl.BlockSpec((HG_CHUNK, cat_w), rev),
                  pl.BlockSpec((None, HG_HEADS, HG_DIM, HG_DIM), lambda b, n: (b * nc + (nc - 1 - n), 0, 0, 0)),
                  pl.BlockSpec(lb_logits.shape, lambda b, n: (0, 0)),
                  pl.BlockSpec(gnorm.shape, lambda b, n: (0, 0)),
                  pl.BlockSpec((mem_len, kv.shape[1]), lambda b, n: (b, 0))],
        out_specs=[pl.BlockSpec((HG_CHUNK, zw), rev),
                   pl.BlockSpec((mem_len, kv.shape[1]), lambda b, n: (b, 0)),
                   pl.BlockSpec(lb_logits.shape, lambda b, n: (0, 0)),
                   pl.BlockSpec(gnorm.shape, lambda b, n: (0, 0))],
        out_shape=[jax.ShapeDtypeStruct((T, zw), BF), jax.ShapeDtypeStruct(kv.shape, F32),
                   jax.ShapeDtypeStruct(lb_logits.shape, F32), jax.ShapeDtypeStruct(gnorm.shape, F32)],
        scratch_shapes=[pltpu.VMEM((HG_HEADS, HG_DIM, HG_DIM), F32)],
        name="hgrn_bwd",
        compiler_params=pltpu.CompilerParams(dimension_semantics=("arbitrary", "arbitrary"), vmem_limit_bytes=VMEM_CAP_BYTES),
    )(z, dcat, stash, lb_logits, gnorm, kv)


HG_SUB = 4


def _hgrn_rows(z_ref, dtype_cast=None):
    W = HG_HEADS * HG_DIM

    def piece(c, col, w):
        return z_ref[c * HG_CHUNK:(c + 1) * HG_CHUNK, col:col + w]

    zq = [[piece(c, h * HG_DIM, HG_DIM) for h in range(HG_HEADS)] for c in range(HG_SUB)]
    zf = [[piece(c, W + h * HG_DIM, HG_DIM) for h in range(HG_HEADS)] for c in range(HG_SUB)]
    zi = [[piece(c, 2 * W + h * HG_DIM, HG_DIM) for h in range(HG_HEADS)] for c in range(HG_SUB)]
    zg = [[piece(c, 3 * W + h * HG_DIM, HG_DIM) for h in range(HG_HEADS)] for c in range(HG_SUB)]
    zx = [z_ref[:, 4 * W + a * XA_DIM:4 * W + (a + 1) * XA_DIM] for a in range(XA_HEADS)]
    return zq, zf, zi, zg, zx


def _hgrn_steps(zq, zf, zi, zg, zx, l0, l1, l2, gn, mk, mv, S):
    mix = []
    for c in range(HG_SUB):
        row, s_next = [], []
        for h in range(HG_HEADS):
            o, sn = _hgrn_head(zq[c][h], zf[c][h], zi[c][h], zg[c][h], l0[h], l1[h], l2[h], gn, S[h])
            row.append(o)
            s_next.append(sn)
        mix.append(row)
        S = s_next
    att = [_attention(zx[a], mk[a], mv[a]) for a in range(XA_HEADS)]
    return mix, att, S


def _hgrn_fwd2(z, lb_logits, gnorm, kv, bl, seq):
    T, zw = z.shape
    mem_len = kv.shape[0] // bl
    cat_w = HG_HEADS * HG_DIM + XA_HEADS * XA_DIM
    R = HG_SUB * HG_CHUNK
    nb = seq // R

    def kern(z_ref, lb_ref, gn_ref, kv_ref, cat_ref, st_ref, s_scr):
        @pl.when(pl.program_id(1) == 0)
        def _():
            s_scr[...] = jnp.zeros(s_scr.shape, F32)

        st_ref[...] = s_scr[...]
        zq, zf, zi, zg, zx = _hgrn_rows(z_ref)
        mk, mv = _kv_pieces(kv_ref)
        l0, l1, l2 = _lb_pieces(lb_ref)
        S = [s_scr[h] for h in range(HG_HEADS)]
        mix, att, s_new = _hgrn_steps(zq, zf, zi, zg, zx, l0, l1, l2, gn_ref[...], mk, mv, S)
        for c in range(HG_SUB):
            for h in range(HG_HEADS):
                cat_ref[c * HG_CHUNK:(c + 1) * HG_CHUNK, h * HG_DIM:(h + 1) * HG_DIM] = mix[c][h].astype(cat_ref.dtype)
        for h in range(HG_HEADS):
            s_scr[h] = s_new[h]
        base = HG_HEADS * HG_DIM
        for a in range(XA_HEADS):
            cat_ref[:, base + a * XA_DIM:base + (a + 1) * XA_DIM] = att[a].astype(cat_ref.dtype)

    return pl.pallas_call(
        kern, grid=(bl, nb),
        in_specs=[pl.BlockSpec((R, zw), lambda b, n: (b * nb + n, 0)),
                  pl.BlockSpec(lb_logits.shape, lambda b, n: (0, 0)),
                  pl.BlockSpec(gnorm.shape, lambda b, n: (0, 0)),
                  pl.BlockSpec((mem_len, kv.shape[1]), lambda b, n: (b, 0))],
        out_specs=[pl.BlockSpec((R, cat_w), lambda b, n: (b * nb + n, 0)),
                   pl.BlockSpec((None, HG_HEADS, HG_DIM, HG_DIM), lambda b, n: (b * nb + n, 0, 0, 0))],
        out_shape=[jax.ShapeDtypeStruct((T, cat_w), BF),
                   jax.ShapeDtypeStruct((bl * nb, HG_HEADS, HG_DIM, HG_DIM), F32)],
        scratch_shapes=[pltpu.VMEM((HG_HEADS, HG_DIM, HG_DIM), F32)],
        name="hgrn_fwd",
        compiler_params=pltpu.CompilerParams(dimension_semantics=("arbitrary", "arbitrary"), vmem_limit_bytes=VMEM_CAP_BYTES),
    )(z, lb_logits, gnorm, kv)


def _hgrn_bwd2(z, dcat, stash, lb_logits, gnorm, kv, bl, seq):
    T, zw = z.shape
    mem_len = kv.shape[0] // bl
    cat_w = dcat.shape[1]
    R = HG_SUB * HG_CHUNK
    nb = seq // R

    def kern(z_ref, dc_ref, st_ref, lb_ref, gn_ref, kv_ref, dz_ref, dkv_ref, dlb_ref, dgn_ref, ds_scr):
        first = jnp.logical_and(pl.program_id(0) == 0, pl.program_id(1) == 0)

        @pl.when(pl.program_id(1) == 0)
        def _():
            ds_scr[...] = jnp.zeros(ds_scr.shape, F32)
            dkv_ref[...] = jnp.zeros(dkv_ref.shape, F32)

        @pl.when(first)
        def _():
            dlb_ref[...] = jnp.zeros(dlb_ref.shape, F32)
            dgn_ref[...] = jnp.zeros(dgn_ref.shape, F32)

        zq, zf, zi, zg, zx = _hgrn_rows(z_ref)
        mk, mv = _kv_pieces(kv_ref)
        l0, l1, l2 = _lb_pieces(lb_ref)
        S = [st_ref[h] for h in range(HG_HEADS)]
        _, vjp = jax.vjp(_hgrn_steps, zq, zf, zi, zg, zx, l0, l1, l2, gn_ref[...], mk, mv, S)
        d_mix = [[dc_ref[c * HG_CHUNK:(c + 1) * HG_CHUNK, h * HG_DIM:(h + 1) * HG_DIM] for h in range(HG_HEADS)]
                 for c in range(HG_SUB)]
        base = HG_HEADS * HG_DIM
        d_att = [dc_ref[:, base + a * XA_DIM:base + (a + 1) * XA_DIM] for a in range(XA_HEADS)]
        d_s = [ds_scr[h] for h in range(HG_HEADS)]
        dzq, dzf, dzi, dzg, dzx, dl0, dl1, dl2, dgn, dmk, dmv, dS = vjp((d_mix, d_att, d_s))
        W = HG_HEADS * HG_DIM
        for c in range(HG_SUB):
            rows = slice(c * HG_CHUNK, (c + 1) * HG_CHUNK)
            for h in range(HG_HEADS):
                for k, part in enumerate((dzq, dzf, dzi, dzg)):
                    dz_ref[rows, k * W + h * HG_DIM:k * W + (h + 1) * HG_DIM] = part[c][h].astype(dz_ref.dtype)
        for h in range(HG_HEADS):
            sl = slice(h * HG_DIM, (h + 1) * HG_DIM)
            ds_scr[h] = dS[h]
            dlb_ref[0:1, sl] += dl0[h]
            dlb_ref[1:2, sl] += dl1[h]
            dlb_ref[2:3, sl] += dl2[h]
        dgn_ref[...] += dgn
        KW = XA_HEADS * XA_DIM
        for a in range(XA_HEADS):
            dz_ref[:, 4 * W + a * XA_DIM:4 * W + (a + 1) * XA_DIM] = dzx[a].astype(dz_ref.dtype)
            dkv_ref[:, a * XA_DIM:(a + 1) * XA_DIM] += dmk[a]
            dkv_ref[:, KW + a * XA_DIM:KW + (a + 1) * XA_DIM] += dmv[a]

    rev = lambda b, n: (b * nb + (nb - 1 - n), 0)
    return pl.pallas_call(
        kern, grid=(bl, nb),
        in_specs=[pl.BlockSpec((R, zw), rev),
                  pl.BlockSpec((R, cat_w), rev),
                  pl.BlockSpec((None, HG_HEADS, HG_DIM, HG_DIM), lambda b, n: (b * nb + (nb - 1 - n), 0, 0, 0)),
                  pl.BlockSpec(lb_logits.shape, lambda b, n: (0, 0)),
                  pl.BlockSpec(gnorm.shape, lambda b, n: (0, 0)),
                  pl.BlockSpec((mem_len, kv.shape[1]), lambda b, n: (b, 0))],
        out_specs=[pl.BlockSpec((R, zw), rev),
                   pl.BlockSpec((mem_len, kv.shape[1]), lambda b, n: (b, 0)),
                   pl.BlockSpec(lb_logits.shape, lambda b, n: (0, 0)),
                   pl.BlockSpec(gnorm.shape, lambda b, n: (0, 0))],
        out_shape=[jax.ShapeDtypeStruct((T, zw), BF), jax.ShapeDtypeStruct(kv.shape, F32),
                   jax.ShapeDtypeStruct(lb_logits.shape, F32), jax.ShapeDtypeStruct(gnorm.shape, F32)],
        scratch_shapes=[pltpu.VMEM((HG_HEADS, HG_DIM, HG_DIM), F32)],
        name="hgrn_bwd",
        compiler_params=pltpu.CompilerParams(dimension_semantics=("arbitrary", "arbitrary"), vmem_limit_bytes=VMEM_CAP_BYTES),
    )(z, dcat, stash, lb_logits, gnorm, kv)


GM_SUB = 2


def _gmlp_pieces(z_ref):
    W = GM_GROUPS * GM_GROUP_DIM
    zu = [z_ref[:, g * GM_GROUP_DIM:(g + 1) * GM_GROUP_DIM] for g in range(GM_GROUPS)]
    zv = [z_ref[:, W + g * GM_GROUP_DIM:W + (g + 1) * GM_GROUP_DIM] for g in range(GM_GROUPS)]
    zx = [z_ref[:, 2 * W + a * XA_DIM:2 * W + (a + 1) * XA_DIM] for a in range(XA_HEADS)]
    return zu, zv, zx


def _gmlp_params(lng_ref, lnb_ref, ws_ref, bs_ref):
    lng = [lng_ref[:, g * GM_GROUP_DIM:(g + 1) * GM_GROUP_DIM] for g in range(GM_GROUPS)]
    lnb = [lnb_ref[:, g * GM_GROUP_DIM:(g + 1) * GM_GROUP_DIM] for g in range(GM_GROUPS)]
    ws = [ws_ref[g] for g in range(GM_GROUPS)]
    bs = [bs_ref[g:g + 1, :] for g in range(GM_GROUPS)]
    return lng, lnb, ws, bs


def _gmlp_fwd(z, ln_g, ln_b, w_s, b_s, kv, bl, nc):
    T, zw = z.shape
    mem_len = kv.shape[0] // bl
    cat_w = GM_GROUPS * GM_GROUP_DIM + XA_HEADS * XA_DIM

    assert nc % GM_SUB == 0
    nc = nc // GM_SUB
    R = GM_SUB * GM_CHUNK

    def kern(z_ref, lng_ref, lnb_ref, ws_ref, bs_ref, kv_ref, cat_ref):
        lng, lnb, ws, bs = _gmlp_params(lng_ref, lnb_ref, ws_ref, bs_ref)
        mk, mv = _kv_pieces(kv_ref)
        for c in range(GM_SUB):
            rows = pl.ds(c * GM_CHUNK, GM_CHUNK)
            zu, zv, zx = _gmlp_pieces(z_ref.at[rows])
            out = cat_ref.at[rows]
            outs = _gmlp_block(zu, zv, zx, lng, lnb, ws, bs, mk, mv)
            for g in range(GM_GROUPS):
                out[:, g * GM_GROUP_DIM:(g + 1) * GM_GROUP_DIM] = outs[g].astype(cat_ref.dtype)
            base = GM_GROUPS * GM_GROUP_DIM
            for a in range(XA_HEADS):
                out[:, base + a * XA_DIM:base + (a + 1) * XA_DIM] = outs[GM_GROUPS + a].astype(cat_ref.dtype)

    full2 = lambda b, n: (0, 0)
    return pl.pallas_call(
        kern, grid=(bl, nc),
        in_specs=[pl.BlockSpec((R, zw), lambda b, n: (b * nc + n, 0)),
                  pl.BlockSpec(ln_g.shape, full2), pl.BlockSpec(ln_b.shape, full2),
                  pl.BlockSpec(w_s.shape, lambda b, n: (0, 0, 0)), pl.BlockSpec(b_s.shape, full2),
                  pl.BlockSpec((mem_len, kv.shape[1]), lambda b, n: (b, 0))],
        out_specs=pl.BlockSpec((R, cat_w), lambda b, n: (b * nc + n, 0)),
        out_shape=jax.ShapeDtypeStruct((T, cat_w), BF),
        name="gmlp_fwd",
        compiler_params=pltpu.CompilerParams(dimension_semantics=("arbitrary", "arbitrary"), vmem_limit_bytes=VMEM_CAP_BYTES),
    )(z, ln_g, ln_b, w_s, b_s, kv)


def _gmlp_bwd(z, dcat, ln_g, ln_b, w_s, b_s, kv, bl, nc):
    T, zw = z.shape
    mem_len = kv.shape[0] // bl
    cat_w = dcat.shape[1]
    assert nc % GM_SUB == 0
    nc = nc // GM_SUB

    def kern(z_ref, dc_ref, lng_ref, lnb_ref, ws_ref, bs_ref, kv_ref,
             dz_ref, dkv_ref, dlng_ref, dlnb_ref, dws_ref, dbs_ref):
        first = jnp.logical_and(pl.program_id(0) == 0, pl.program_id(1) == 0)

        @pl.when(pl.program_id(1) == 0)
        def _():
            dkv_ref[...] = jnp.zeros(dkv_ref.shape, F32)

        @pl.when(first)
        def _():
            dlng_ref[...] = jnp.zeros(dlng_ref.shape, F32)
            dlnb_ref[...] = jnp.zeros(dlnb_ref.shape, F32)
            dws_ref[...] = jnp.zeros(dws_ref.shape, F32)
            dbs_ref[...] = jnp.zeros(dbs_ref.shape, F32)

        lng, lnb, ws, bs = _gmlp_params(lng_ref, lnb_ref, ws_ref, bs_ref)
        mk, mv = _kv_pieces(kv_ref)
        W = GM_GROUPS * GM_GROUP_DIM
        KW = XA_HEADS * XA_DIM
        for c in range(GM_SUB):
            rows = pl.ds(c * GM_CHUNK, GM_CHUNK)
            zu, zv, zx = _gmlp_pieces(z_ref.at[rows])
            dc, dz = dc_ref.at[rows], dz_ref.at[rows]
            _, vjp = jax.vjp(_gmlp_block, zu, zv, zx, lng, lnb, ws, bs, mk, mv)
            d_outs = [dc[:, g * GM_GROUP_DIM:(g + 1) * GM_GROUP_DIM] for g in range(GM_GROUPS)]
            d_outs += [dc[:, W + a * XA_DIM:W + (a + 1) * XA_DIM] for a in range(XA_HEADS)]
            dzu, dzv, dzx, dlng, dlnb, dws, dbs, dmk, dmv = vjp(d_outs)
            for g in range(GM_GROUPS):
                sl = slice(g * GM_GROUP_DIM, (g + 1) * GM_GROUP_DIM)
                dz[:, sl] = dzu[g].astype(dz_ref.dtype)
                dz[:, W + g * GM_GROUP_DIM:W + (g + 1) * GM_GROUP_DIM] = dzv[g].astype(dz_ref.dtype)
                dlng_ref[:, sl] += dlng[g]
                dlnb_ref[:, sl] += dlnb[g]
                dws_ref[g] += dws[g]
                dbs_ref[g:g + 1, :] += dbs[g]
            for a in range(XA_HEADS):
                dz[:, 2 * W + a * XA_DIM:2 * W + (a + 1) * XA_DIM] = dzx[a].astype(dz_ref.dtype)
                dkv_ref[:, a * XA_DIM:(a + 1) * XA_DIM] += dmk[a]
                dkv_ref[:, KW + a * XA_DIM:KW + (a + 1) * XA_DIM] += dmv[a]

    full2 = lambda b, n: (0, 0)
    full3 = lambda b, n: (0, 0, 0)
    blk = lambda b, n: (b * nc + n, 0)
    return pl.pallas_call(
        kern, grid=(bl, nc),
        in_specs=[pl.BlockSpec((GM_SUB * GM_CHUNK, zw), blk), pl.BlockSpec((GM_SUB * GM_CHUNK, cat_w), blk),
                  pl.BlockSpec(ln_g.shape, full2), pl.BlockSpec(ln_b.shape, full2),
                  pl.BlockSpec(w_s.shape, full3), pl.BlockSpec(b_s.shape, full2),
                  pl.BlockSpec((mem_len, kv.shape[1]), lambda b, n: (b, 0))],
        out_specs=[pl.BlockSpec((GM_SUB * GM_CHUNK, zw), blk),
                   pl.BlockSpec((mem_len, kv.shape[1]), lambda b, n: (b, 0)),
                   pl.BlockSpec(ln_g.shape, full2), pl.BlockSpec(ln_b.shape, full2),
                   pl.BlockSpec(w_s.shape, full3), pl.BlockSpec(b_s.shape, full2)],
        out_shape=[jax.ShapeDtypeStruct((T, zw), BF), jax.ShapeDtypeStruct(kv.shape, F32),
                   jax.ShapeDtypeStruct(ln_g.shape, F32), jax.ShapeDtypeStruct(ln_b.shape, F32),
                   jax.ShapeDtypeStruct(w_s.shape, F32), jax.ShapeDtypeStruct(b_s.shape, F32)],
        name="gmlp_bwd",
        compiler_params=pltpu.CompilerParams(dimension_semantics=("arbitrary", "arbitrary"), vmem_limit_bytes=VMEM_CAP_BYTES),
    )(z, dcat, ln_g, ln_b, w_s, b_s, kv)


def _place():
    x, y, c = lax.axis_index("x"), lax.axis_index("y"), lax.axis_index("c")
    chips = [(1 - x, y), (x, 1 - y), (1 - x, 1 - y)]
    return x, y, c, chips


def _half(ref, kind, e):
    if kind == "col":
        n = ref.shape[1] // 2
        return ref.at[:, pl.ds(pl.multiple_of(e * n, n), n), :]
    n = ref.shape[2] // 2
    return ref.at[:, :, pl.ds(pl.multiple_of(e * n, n), n)]


def _slot(ref, kind, j, n):
    if kind == "col":
        return ref.at[:, :, pl.ds(pl.multiple_of(j * n, n), n)]
    return ref.at[:, pl.ds(pl.multiple_of(j * n, n), n), :]


def _allgather_seq(name, items, cid):
    nt = len(items)
    kinds = [k for (_, k, _) in items]
    slot_kind = ["row" if k == "row" else "col" for k in kinds]
    out_type = []
    for s, k, l in items:
        L, r, c = s.shape
        lo = L if l is None else 1
        out_type.append(jax.ShapeDtypeStruct((lo, 4 * r, c) if k == "row" else (lo, r, 4 * c), s.dtype))

    def part(ref, t, e):
        return ref if kinds[t] == "vec" else _half(ref, kinds[t], e)

    def body(*refs):
        sh = [refs[t] if items[t][2] is None else refs[t].at[pl.ds(items[t][2], 1)] for t in range(nt)]
        full = refs[nt:2 * nt]
        loc, s_ici, r_ici, s_d2d, r_d2d = refs[2 * nt:]
        x, y, c, chips = _place()
        own = 2 * x + y
        sibling = (x, y, 1 - c)
        barrier = pltpu.get_barrier_semaphore()
        for peer in [(px, py, c) for (px, py) in chips] + [sibling]:
            pl.semaphore_signal(barrier, inc=1, device_id=peer, device_id_type=MESH)
        pl.semaphore_wait(barrier, 4)
        width = [sh[t].shape[1] if kinds[t] == "row" else sh[t].shape[2] for t in range(nt)]
        started = []
        for t in range(nt):
            mine = pltpu.make_async_copy(sh[t], _slot(full[t], slot_kind[t], own, width[t]), loc.at[t])
            mine.start()
            started.append(mine)
        sent = []
        for t in range(nt):
            for p, (px, py) in enumerate(chips):
                cp = pltpu.make_async_remote_copy(
                    src_ref=part(sh[t], t, c), dst_ref=part(_slot(full[t], slot_kind[t], own, width[t]), t, c),
                    send_sem=s_ici.at[t, p], recv_sem=r_ici.at[t, p], device_id=(px, py, c), device_id_type=MESH)
                cp.start()
                sent.append(cp)
        for t in range(nt):
            for p, (px, py) in enumerate(chips):
                landed = part(_slot(full[t], slot_kind[t], 2 * px + py, width[t]), t, c)
                pltpu.make_async_remote_copy(
                    src_ref=landed, dst_ref=landed, send_sem=s_ici.at[t, p], recv_sem=r_ici.at[t, p],
                    device_id=(px, py, c), device_id_type=MESH).wait_recv()
                if kinds[t] == "vec":
                    continue
                fw = pltpu.make_async_remote_copy(
                    src_ref=landed, dst_ref=landed, send_sem=s_d2d.at[t, p], recv_sem=r_d2d.at[t, p],
                    device_id=sibling, device_id_type=MESH)
                fw.start()
                sent.append(fw)
        for t in range(nt):
            if kinds[t] == "vec":
                continue
            for p, (px, py) in enumerate(chips):
                other = _half(_slot(full[t], kinds[t], 2 * px + py, width[t]), kinds[t], 1 - c)
                pltpu.make_async_remote_copy(
                    src_ref=other, dst_ref=other, send_sem=s_d2d.at[t, p], recv_sem=r_d2d.at[t, p],
                    device_id=sibling, device_id_type=MESH).wait_recv()
        for cp in sent:
            cp.wait_send()
        for cp in started:
            cp.wait()

    return pl.kernel(
        body, out_type=out_type, mesh=plsc.ScalarSubcoreMesh(axis_name="seq", num_cores=1),
        scratch_types=[pltpu.SemaphoreType.DMA((nt,)), pltpu.SemaphoreType.DMA((nt, 3)), pltpu.SemaphoreType.DMA((nt, 3)),
                       pltpu.SemaphoreType.DMA((nt, 3)), pltpu.SemaphoreType.DMA((nt, 3))],
        compiler_params=pltpu.CompilerParams(collective_id=cid), name=name,
    )(*[s for (s, _, _) in items])


def _slot2(ref, kind, j, n):
    if kind == "col":
        return ref.at[:, pl.ds(pl.multiple_of(j * n, n), n)]
    return ref.at[pl.ds(pl.multiple_of(j * n, n), n), :]


def _rs_chips_seq(name, parts, kinds, cid):
    nm = len(parts)
    out_type = []
    for g, k in zip(parts, kinds):
        r, c = g.shape
        ps = (r, c // 4) if k == "col" else (r // 4, c)
        out_type += [jax.ShapeDtypeStruct(ps, BF), jax.ShapeDtypeStruct((3,) + ps, BF)]

    def body(*refs):
        g = refs[:nm]
        outs = refs[nm:3 * nm]
        loc, ssem, rsem = refs[3 * nm:]
        x, y, c, chips = _place()
        own = 2 * x + y
        barrier = pltpu.get_barrier_semaphore()
        for (px, py) in chips:
            pl.semaphore_signal(barrier, inc=1, device_id=(px, py, c), device_id_type=MESH)
        pl.semaphore_wait(barrier, 3)
        cps = []
        for m in range(nm):
            k = kinds[m]
            own_o, got_o = outs[2 * m], outs[2 * m + 1]
            n = g[m].shape[1] // 4 if k == "col" else g[m].shape[0] // 4
            lc = pltpu.make_async_copy(_slot2(g[m], k, own, n), own_o, loc.at[m])
            lc.start()
            cps.append(lc)
            for p, (px, py) in enumerate(chips):
                cp = pltpu.make_async_remote_copy(
                    src_ref=_slot2(g[m], k, 2 * px + py, n), dst_ref=got_o.at[p],
                    send_sem=ssem.at[m, p], recv_sem=rsem.at[m, p], device_id=(px, py, c), device_id_type=MESH)
                cp.start()
                cps.append(cp)
        for cp in cps:
            cp.wait()

    return pl.kernel(
        body, out_type=out_type, mesh=plsc.ScalarSubcoreMesh(axis_name="seq", num_cores=1),
        scratch_types=[pltpu.SemaphoreType.DMA((nm,)), pltpu.SemaphoreType.DMA((nm, 3)), pltpu.SemaphoreType.DMA((nm, 3))],
        compiler_params=pltpu.CompilerParams(collective_id=cid), name=name,
    )(*parts)


def _finish_share(name, own, got, kind, c_arr):
    L, r, c = own.shape
    tr = _pick(r, 128 if kind == "col" else 256)
    nb = r // tr
    nq = L * nb
    own2 = own.reshape(L * r, c)
    got2 = got.reshape(3 * L * r, c)
    pick = lambda h, q: q * (1 - h) + (nq - 1) * h
    in_specs = [pl.BlockSpec((tr, c), lambda h, q, cc: (pick(h, q), 0))]
    in_specs += [pl.BlockSpec((tr, c), functools.partial(lambda h, q, cc, p: (p * nq + pick(h, q), 0), p=p)) for p in range(3)]
    if kind == "col":
        out_sd = (L, 2, r, c)
        o_spec = pl.BlockSpec((None, 2, tr, c), lambda h, q, cc: ((q * h) // nb, 0, (q * h) % nb, 0))
    else:
        out_sd = (L * r, 2 * c)
        o_spec = pl.BlockSpec((tr, 2 * c), lambda h, q, cc: (q * h, 0))

    def kern(c_ref, o_ref, g0, g1, g2, out_ref, mine, recv, ssem, rsem):
        h, q = pl.program_id(0), pl.program_id(1)
        x, y, cc, _ = _place()

        def swap(qq):
            return pltpu.make_async_remote_copy(src_ref=mine.at[qq], dst_ref=recv.at[qq], send_sem=ssem.at[qq],
                                                recv_sem=rsem.at[qq], device_id=(x, y, 1 - cc), device_id_type=MESH)

        @pl.when(h == 0)
        def _():
            mine[q] = ((o_ref[...].astype(F32) + g0[...].astype(F32)) + g1[...].astype(F32)) + g2[...].astype(F32)
            swap(q).start()

        @pl.when(h == 1)
        def _():
            swap(q).wait()
            a, b = mine[q], recv[q]
            first = c_ref[0] == 0
            lo, hi = jnp.where(first, a, b), jnp.where(first, b, a)
            if kind == "col":
                out_ref[0] = lo
                out_ref[1] = hi
            else:
                out_ref[:, :c] = lo
                out_ref[:, c:] = hi

    est = 2 * nq * tr * c * 4 + 6 * tr * c * 4 + 8 * tr * c * 2
    full = pl.pallas_call(
        kern,
        grid_spec=pltpu.PrefetchScalarGridSpec(
            num_scalar_prefetch=1, grid=(2, nq), in_specs=in_specs, out_specs=o_spec,
            scratch_shapes=[pltpu.VMEM((nq, tr, c), F32), pltpu.VMEM((nq, tr, c), F32),
                            pltpu.SemaphoreType.DMA((nq,)), pltpu.SemaphoreType.DMA((nq,))]),
        out_shape=jax.ShapeDtypeStruct(out_sd, F32), name=name,
        compiler_params=pltpu.CompilerParams(dimension_semantics=("arbitrary", "arbitrary"),
                                             vmem_limit_bytes=VMEM_CAP_BYTES),
    )(c_arr, own2, got2, got2, got2)
    return full.reshape(L, 2 * r, c) if kind == "col" else full.reshape(L, r, 2 * c)


def _small_allreduce(buf, name):
    R = buf.shape[0]
    assert R % 16 == 0
    h = R // 2

    def body(x_ref, o_ref, sib, csum, got, s_a, r_a, s_b, r_b, s_c, r_c):
        x, y, c, chips = _place()
        sibling = (x, y, 1 - c)
        own = 2 * x + y
        swap = pltpu.make_async_remote_copy(src_ref=x_ref, dst_ref=sib, send_sem=s_a, recv_sem=r_a,
                                            device_id=sibling, device_id_type=MESH)
        swap.start()
        swap.wait()
        a, b = x_ref[...], sib[...]
        south = c == 0
        csum[...] = jnp.where(south, a, b) + jnp.where(south, b, a)
        lo = pl.multiple_of(c * h, 8)
        mine = csum.at[pl.ds(lo, h)]
        got[own] = csum[pl.ds(lo, h)]
        sends = []
        for p, (px, py) in enumerate(chips):
            cp = pltpu.make_async_remote_copy(src_ref=mine, dst_ref=got.at[own], send_sem=s_b.at[p], recv_sem=r_b.at[p],
                                              device_id=(px, py, c), device_id_type=MESH)
            cp.start()
            sends.append(cp)
        for cp in sends:
            cp.wait()
        o_ref[pl.ds(lo, h)] = ((got[0] + got[1]) + got[2]) + got[3]
        done = o_ref.at[pl.ds(lo, h)]
        back = pltpu.make_async_remote_copy(src_ref=done, dst_ref=done, send_sem=s_c, recv_sem=r_c,
                                            device_id=sibling, device_id_type=MESH)
        back.start()
        back.wait_send()
        other = o_ref.at[pl.ds(pl.multiple_of((1 - c) * h, 8), h)]
        pltpu.make_async_remote_copy(src_ref=other, dst_ref=other, send_sem=s_c, recv_sem=r_c,
                                     device_id=sibling, device_id_type=MESH).wait_recv()

    vm = pl.BlockSpec(memory_space=pltpu.VMEM)
    return pl.pallas_call(
        body, out_shape=jax.ShapeDtypeStruct(buf.shape, F32), in_specs=[vm], out_specs=vm,
        scratch_shapes=[pltpu.VMEM((R, LANES), F32), pltpu.VMEM((R, LANES), F32), pltpu.VMEM((4, h, LANES), F32),
                        pltpu.SemaphoreType.DMA, pltpu.SemaphoreType.DMA, pltpu.SemaphoreType.DMA((3,)),
                        pltpu.SemaphoreType.DMA((3,)), pltpu.SemaphoreType.DMA, pltpu.SemaphoreType.DMA],
        name=name,
        compiler_params=pltpu.CompilerParams(vmem_limit_bytes=VMEM_CAP_BYTES),
    )(buf)


PACK_TILE_ROWS = 8


def _item_rows(shape):
    n = 1
    for d in shape:
        n *= d
    return -(-n // (PACK_TILE_ROWS * LANES)) * PACK_TILE_ROWS


def _pack(arrs, rows_total):
    buf = jnp.zeros((rows_total, LANES), F32)
    r = 0
    for a in arrs:
        f = a.reshape(-1).astype(F32)
        nr = _item_rows(a.shape)
        block = jnp.pad(f, (0, nr * LANES - f.shape[0])).reshape(nr, LANES)
        buf = lax.dynamic_update_slice(buf, block, (r, 0))
        r += nr
    return buf


def _unpack(buf, shapes):
    out, r = [], 0
    for s in shapes:
        n = 1
        for d in s:
            n *= d
        nr = _item_rows(s)
        out.append(buf[r:r + nr].reshape(-1)[:n].reshape(s))
        r += nr
    return out


def _rows_needed(shapes):
    return -(-sum(_item_rows(s) for s in shapes) // (2 * PACK_TILE_ROWS)) * (2 * PACK_TILE_ROWS)


def _two_rows(a, b):
    out = jnp.zeros((2, a.shape[1]), a.dtype)
    return lax.dynamic_update_slice(lax.dynamic_update_slice(out, a, (0, 0)), b, (1, 0))


def _adam(w, g, m, v):
    m = ADAM_B1 * m + (1.0 - ADAM_B1) * g
    v = ADAM_B2 * v + (1.0 - ADAM_B2) * jnp.square(g)
    m_hat = m / (1.0 - ADAM_B1 ** ADAM_STEP)
    v_hat = v / (1.0 - ADAM_B2 ** ADAM_STEP)
    delta = -ADAM_LR * (m_hat / (jnp.sqrt(v_hat) + ADAM_EPS) + ADAM_WD * w)
    return delta, m, v


def _adam_call(name, w2, g2, m2, v2, tr):
    def fn(rv, cv):
        return list(_adam(*rv)), []

    width = w2.shape[1]
    return _rowcall(name, fn, [(w2, 0, width), (g2, 0, width), (m2, 0, width), (v2, 0, width)], [],
                    [(width, F32)] * 3, [], tr)


def kernel(x, mem, mem_norm, lb_logits, ffn1_norm, ffn1_w_in, ffn1_w_out, mix_norm, mem_w_kv, hgrn_w_in, hgrn_gnorm, hgrn_w_out, gmlp_w_in, gmlp_ln_g, gmlp_ln_b, gmlp_w_s, gmlp_b_s, gmlp_w_out, ffn2_norm, ffn2_w_in, ffn2_w_out, final_norm, loss_target, m_mem_norm, m_lb_logits, m_ffn1_norm, m_ffn1_w_in, m_ffn1_w_out, m_mix_norm, m_mem_w_kv, m_hgrn_w_in, m_hgrn_gnorm, m_hgrn_w_out, m_gmlp_w_in, m_gmlp_ln_g, m_gmlp_ln_b, m_gmlp_w_s, m_gmlp_b_s, m_gmlp_w_out, m_ffn2_norm, m_ffn2_w_in, m_ffn2_w_out, m_final_norm, v_mem_norm, v_lb_logits, v_ffn1_norm, v_ffn1_w_in, v_ffn1_w_out, v_mix_norm, v_mem_w_kv, v_hgrn_w_in, v_hgrn_gnorm, v_hgrn_w_out, v_gmlp_w_in, v_gmlp_ln_g, v_gmlp_ln_b, v_gmlp_w_s, v_gmlp_b_s, v_gmlp_w_out, v_ffn2_norm, v_ffn2_w_in, v_ffn2_w_out, v_final_norm):
    bl, seq, D = x.shape
    T = bl * seq
    mem_len = mem.shape[1]
    chip = 2 * lax.axis_index("x") + lax.axis_index("y")
    c_arr = lax.axis_index("c").astype(jnp.int32).reshape(1)
    TR = 1024

    big = [("ffn1_w_in", ffn1_w_in, "col"), ("ffn1_w_out", ffn1_w_out, "row"), ("mem_w_kv", mem_w_kv, "col"),
           ("hgrn_w_in", hgrn_w_in, "col"), ("hgrn_w_out", hgrn_w_out, "row"), ("gmlp_w_in", gmlp_w_in, "col"),
           ("gmlp_w_out", gmlp_w_out, "row"), ("ffn2_w_in", ffn2_w_in, "col"), ("ffn2_w_out", ffn2_w_out, "row")]
    kinds = [k for (_, _, k) in big]
    shards_bf = []
    for nm, w, _ in big:
        L, r, c = w.shape
        (wb,) = _rowcall("cast_" + nm, lambda rv, cv: ([rv[0]], []), [(w.reshape(L * r, c), 0, c)], [], [(c, BF)], [], 512)
        shards_bf.append(wb.reshape(L, r, c))
    sb = dict(zip([nm for (nm, _, _) in big], shards_bf))
    groups = [[("ffn1_w_in", 0)], [("ffn1_w_out", 0)], [("hgrn_w_in", None)], [("mem_w_kv", None)], [("hgrn_w_out", None)],
              [("ffn2_w_in", 0), ("ffn2_w_out", 0), ("gmlp_ln_g", None), ("gmlp_ln_b", None)],
              [("ffn1_w_in", 1), ("ffn1_w_out", 1)],
              [("gmlp_w_in", None), ("gmlp_w_out", None)],
              [("ffn2_w_in", 1), ("ffn2_w_out", 1)]]
    kind_of = {nm: k for (nm, _, k) in big}
    for nm, vec in (("gmlp_ln_g", gmlp_ln_g), ("gmlp_ln_b", gmlp_ln_b)):
        sb[nm] = vec.reshape(1, 1, -1)
        kind_of[nm] = "vec"
    gathered = {nm: [None, None] for nm in ("ffn1_w_in", "ffn1_w_out", "ffn2_w_in", "ffn2_w_out")}
    for gi, grp in enumerate(groups):
        outs = _allgather_seq("gather_%d" % gi, [(sb[nm], kind_of[nm], l) for (nm, l) in grp], gi)
        for (nm, l), o in zip(grp, outs):
            if l is None:
                gathered[nm] = o
            else:
                gathered[nm][l] = o

    ln_w = GM_GROUPS * GM_GROUP_DIM
    ln_g_full, ln_b_full = gathered["gmlp_ln_g"].reshape(1, ln_w), gathered["gmlp_ln_b"].reshape(1, ln_w)

    def rms_fwd(name, xin, g):
        (h,) = _rowcall(name, lambda rv, cv: ([_rmsnorm(rv[0], cv[0])], []), [(xin, 0, D)], [g.reshape(1, D)], [(D, BF)], [], TR)
        return h

    def ffn_fwd(tag, xin, h, w_in, w_out, layer, next_gain):
        dff = w_out[layer].shape[1]
        zg, zu, a = _ffn_in_swiglu("ffn_in_" + tag, h, w_in[layer], 1024, dff // 2)
        out = _mm("ffn_out_" + tag, a, w_out[layer], "nn", F32, 1024, 1024, dff, scale=0.5, res=xin, b_lead=0,
                  norm_gain=None if next_gain is None else next_gain.reshape(1, D))
        xo, h_next = (out, None) if next_gain is None else out
        return xo, h_next, (xin, h, zg, zu, a)

    def ffn_bwd(tag, dxo, saved, g, w_in, w_out, layer):
        xin, h, zg, zu, a = saved
        dff = w_out[layer].shape[1]
        dw_out = _mm_tn_pair("ffn_dwo_" + tag, a, dxo, "row", c_arr, dff // 2, T, scale=0.5)
        dz = _ffn_da_swiglu("ffn_da_" + tag, dxo, w_out[layer], zg, zu, 512)
        dw_in = _mm_tn_pair("ffn_dwi_" + tag, h, dz, "col", c_arr, 512, T)
        dx, dg = _mm_dh_rms("ffn_dh_" + tag, dz, w_in[layer], xin, g.reshape(1, D), dxo, 512)
        return dx, dg, dw_in, dw_out

    def rms_bwd(name, xin, g, dh, dres):
        def fn(rv, cv):
            _, vjp = jax.vjp(_rmsnorm, rv[0], cv[0])
            dx, dg = vjp(rv[1])
            if dres is not None:
                dx = dx + rv[2]
            return [dx], [dg]

        rows = [(xin, 0, D), (dh, 0, D)] + ([(dres, 0, D)] if dres is not None else [])
        dx, dg = _rowcall(name, fn, rows, [g.reshape(1, D)], [(D, F32)], [((1, D), F32)], TR)
        return dx, dg

    x0 = x.reshape(T, D)
    tgt = loss_target.reshape(T, D)
    mem2 = mem.reshape(bl * mem_len, D)
    memn = rms_fwd("rms_mem", mem2, mem_norm)

    h_f10 = rms_fwd("rms_f1l0", x0, ffn1_norm[0])
    x1, h_m0, sv_f10 = ffn_fwd("f1l0", x0, h_f10, gathered["ffn1_w_in"], gathered["ffn1_w_out"], 0, mix_norm[0])
    z_m0 = _mm("mix_in_0", h_m0, gathered["hgrn_w_in"], "nn", F32, 2048, 512, D, b_lead=0)
    kv = [_mm("kv_%d" % i, memn, gathered["mem_w_kv"], "nn", F32, 512, 512, D, b_lead=i) for i in range(2)]
    cat0, stash0 = _hgrn_fwd2(z_m0, lb_logits, hgrn_gnorm, kv[0], bl, seq)
    x2, h_f20 = _mm("mix_out_0", cat0, gathered["hgrn_w_out"], "nn", F32, 1024, 1024, cat0.shape[1], res=x1, b_lead=0,
                    norm_gain=ffn2_norm[0].reshape(1, D))
    x3, h_f11, sv_f20 = ffn_fwd("f2l0", x2, h_f20, gathered["ffn2_w_in"], gathered["ffn2_w_out"], 0, ffn1_norm[1])
    x4, h_m1, sv_f11 = ffn_fwd("f1l1", x3, h_f11, gathered["ffn1_w_in"], gathered["ffn1_w_out"], 1, mix_norm[1])
    z_m1 = _mm("mix_in_1", h_m1, gathered["gmlp_w_in"], "nn", F32, 2048, 512, D, b_lead=0)
    nc1 = seq // GM_CHUNK
    w_s, b_s = gmlp_w_s[0], gmlp_b_s[0]
    cat1 = _gmlp_fwd(z_m1, ln_g_full, ln_b_full, w_s, b_s, kv[1], bl, nc1)
    x5, h_f21 = _mm("mix_out_1", cat1, gathered["gmlp_w_out"], "nn", F32, 1024, 1024, cat1.shape[1], res=x4, b_lead=0,
                    norm_gain=ffn2_norm[1].reshape(1, D))
    x6, _, sv_f21 = ffn_fwd("f2l1", x5, h_f21, gathered["ffn2_w_in"], gathered["ffn2_w_out"], 1, None)

    def head(rv, cv):
        def f(xx, gg):
            err = _rmsnorm(xx, gg) - rv[1]
            return 0.5 * jnp.sum(jnp.mean(err * err, axis=-1, keepdims=True), axis=0, keepdims=True)

        ls, vjp = jax.vjp(f, rv[0], cv[0])
        dx, dg = vjp(jnp.ones((1, 1), F32))
        return [dx], [dg, jnp.broadcast_to(ls, (1, 128))]

    dx6, d_final, loss_part = _rowcall("loss_head", head, [(x6, 0, D), (tgt, 0, D)], [final_norm.reshape(1, D)],
                                       [(D, F32)], [((1, D), F32), ((1, 128), F32)], TR)

    rs_out = {}
    n_gather = len(groups)

    def rs(gi, items):
        outs = _rs_chips_seq("reduce_%d" % gi, [p for (_, p, _) in items], [k for (_, _, k) in items], n_gather + gi)
        for i, (key, _, _) in enumerate(items):
            rs_out[key] = (outs[2 * i], outs[2 * i + 1])

    dx5, dg_f21, dwi_f21, dwo_f21 = ffn_bwd("f2l1", dx6, sv_f21, ffn2_norm[1], gathered["ffn2_w_in"], gathered["ffn2_w_out"], 1)
    rs(0, [(("ffn2_w_out", 1), dwo_f21, "row"), (("ffn2_w_in", 1), dwi_f21, "col")])
    dcat1 = _mm("mix_dcat_1", dx5, gathered["gmlp_w_out"], "nt", F32, 2048, 1024, D, b_lead=0)
    dwo_m1 = _mm_tn_pair("mix_dwo_1", cat1, dx5, "row", c_arr, 1024, T)
    dz_m1, dkv1, d_lng, d_lnb, d_ws, d_bs = _gmlp_bwd(z_m1, dcat1, ln_g_full, ln_b_full, w_s, b_s, kv[1], bl, nc1)
    dx4, dg_m1 = _mm_dh_rms("mix_dh_1", dz_m1, gathered["gmlp_w_in"], x4, mix_norm[1].reshape(1, D), dx5, 512)
    dwi_m1 = _mm_tn_pair("mix_dwi_1", h_m1, dz_m1, "col", c_arr, 1024, T)
    rs(1, [(("gmlp_w_out", 0), dwo_m1, "row"), (("gmlp_w_in", 0), dwi_m1, "col")])
    dx3, dg_f11, dwi_f11, dwo_f11 = ffn_bwd("f1l1", dx4, sv_f11, ffn1_norm[1], gathered["ffn1_w_in"], gathered["ffn1_w_out"], 1)
    rs(2, [(("ffn1_w_out", 1), dwo_f11, "row"), (("ffn1_w_in", 1), dwi_f11, "col")])

    dx2, dg_f20, dwi_f20, dwo_f20 = ffn_bwd("f2l0", dx3, sv_f20, ffn2_norm[0], gathered["ffn2_w_in"], gathered["ffn2_w_out"], 0)
    rs(3, [(("ffn2_w_out", 0), dwo_f20, "row"), (("ffn2_w_in", 0), dwi_f20, "col")])
    dcat0 = _mm("mix_dcat_0", dx2, gathered["hgrn_w_out"], "nt", F32, 2048, 1024, D, b_lead=0)
    dwo_m0 = _mm_tn_pair("mix_dwo_0", cat0, dx2, "row", c_arr, 1024, T)
    dz_m0, dkv0, d_lb, d_gn = _hgrn_bwd2(z_m0, dcat0, stash0, lb_logits, hgrn_gnorm, kv[0], bl, seq)
    dx1, dg_m0 = _mm_dh_rms("mix_dh_0", dz_m0, gathered["hgrn_w_in"], x1, mix_norm[0].reshape(1, D), dx2, 512)
    dwi_m0 = _mm_tn_pair("mix_dwi_0", h_m0, dz_m0, "col", c_arr, 1024, T)
    rs(4, [(("hgrn_w_out", 0), dwo_m0, "row"), (("hgrn_w_in", 0), dwi_m0, "col")])

    dwkv = [_mm_tn_pair("kv_dw_%d" % i, memn, dkv, "col", c_arr, 1024, 512) for i, dkv in enumerate([dkv0, dkv1])]
    rs(5, [(("mem_w_kv", 0), dwkv[0], "col"), (("mem_w_kv", 1), dwkv[1], "col")])
    dmemn = _mm("kv_dx_0", dkv0, gathered["mem_w_kv"], "nt", F32, 512, 512, 1024, b_lead=0)
    dmemn = _mm("kv_dx_1", dkv1, gathered["mem_w_kv"], "nt", F32, 512, 512, 1024, res=dmemn, b_lead=1)
    _, d_memnorm = rms_bwd("rms_bwd_mem", mem2, mem_norm, dmemn, None)

    dx0, dg_f10, dwi_f10, dwo_f10 = ffn_bwd("f1l0", dx1, sv_f10, ffn1_norm[0], gathered["ffn1_w_in"], gathered["ffn1_w_out"], 0)
    rs(6, [(("ffn1_w_out", 0), dwo_f10, "row")])
    rs(7, [(("ffn1_w_in", 0), dwi_f10, "col")])

    shard_grads = []
    for (nm, w, k) in big:
        per_layer = []
        for l in range(w.shape[0]):
            own, got = rs_out[(nm, l)]
            per_layer.append(_finish_share("finish_%s_%d" % (nm, l), own[None], got[:, None], k, c_arr))
        shard_grads.append(per_layer[0] if len(per_layer) == 1 else jnp.concatenate(per_layer, axis=0))

    big_w = [w for (_, w, _) in big]
    big_m = [m_ffn1_w_in, m_ffn1_w_out, m_mem_w_kv, m_hgrn_w_in, m_hgrn_w_out, m_gmlp_w_in, m_gmlp_w_out, m_ffn2_w_in, m_ffn2_w_out]
    big_v = [v_ffn1_w_in, v_ffn1_w_out, v_mem_w_kv, v_hgrn_w_in, v_hgrn_w_out, v_gmlp_w_in, v_gmlp_w_out, v_ffn2_w_in, v_ffn2_w_out]
    big_out = {}
    for (nm, w, _), g, m, v in zip(big, shard_grads, big_m, big_v):
        L, r, c = w.shape
        d2, m2, v2 = _adam_call("adam_" + nm, w.reshape(L * r, c), g.reshape(L * r, c), m.reshape(L * r, c),
                                v.reshape(L * r, c), 256)
        big_out[nm] = (g, d2.reshape(w.shape), m2.reshape(w.shape), v2.reshape(w.shape))

    d_ffn1n = _two_rows(dg_f10, dg_f11)
    d_mixn = _two_rows(dg_m0, dg_m1)
    d_ffn2n = _two_rows(dg_f20, dg_f21)
    small_parts = [loss_part[:, :1], d_memnorm, d_lb, d_ffn1n, d_mixn, d_gn, d_lng, d_lnb, d_ws, d_bs, d_ffn2n, d_final]
    red_shapes = [(1,), mem_norm.shape, lb_logits.shape, ffn1_norm.shape, mix_norm.shape, hgrn_gnorm.shape, (1, ln_w), (1, ln_w),
                  gmlp_w_s.shape, gmlp_b_s.shape, ffn2_norm.shape, final_norm.shape]
    red = _small_allreduce(_pack(small_parts, _rows_needed(red_shapes)), "reduce_small")
    (loss_v, g_memn, g_lb, g_f1n, g_mixn, g_gn, g_lng_full, g_lnb_full, g_ws, g_bs, g_f2n, g_fin) = _unpack(red, red_shapes)
    lsh = gmlp_ln_g.shape[1]
    g_lng = lax.dynamic_slice(g_lng_full, (0, chip * lsh), (1, lsh))
    g_lnb = lax.dynamic_slice(g_lnb_full, (0, chip * lsh), (1, lsh))
    small_w = [mem_norm, lb_logits, ffn1_norm, mix_norm, hgrn_gnorm, gmlp_ln_g, gmlp_ln_b, gmlp_w_s, gmlp_b_s, ffn2_norm, final_norm]
    small_g = [g_memn, g_lb, g_f1n, g_mixn, g_gn, g_lng, g_lnb, g_ws, g_bs, g_f2n, g_fin]
    small_m = [m_mem_norm, m_lb_logits, m_ffn1_norm, m_mix_norm, m_hgrn_gnorm, m_gmlp_ln_g, m_gmlp_ln_b, m_gmlp_w_s, m_gmlp_b_s, m_ffn2_norm, m_final_norm]
    small_v = [v_mem_norm, v_lb_logits, v_ffn1_norm, v_mix_norm, v_hgrn_gnorm, v_gmlp_ln_g, v_gmlp_ln_b, v_gmlp_w_s, v_gmlp_b_s, v_ffn2_norm, v_final_norm]
    sshapes = [w.shape for w in small_w]
    nrow = _rows_needed(sshapes)
    d_p, m_p, v_p = _adam_call("adam_small", _pack(small_w, nrow), _pack(small_g, nrow), _pack(small_m, nrow), _pack(small_v, nrow), nrow)
    s_delta, s_m, s_v = _unpack(d_p, sshapes), _unpack(m_p, sshapes), _unpack(v_p, sshapes)
    small_names = ["mem_norm", "lb_logits", "ffn1_norm", "mix_norm", "hgrn_gnorm", "gmlp_ln_g", "gmlp_ln_b", "gmlp_w_s", "gmlp_b_s", "ffn2_norm", "final_norm"]
    small_out = {nm: (g.reshape(w.shape), d, m, v) for nm, w, g, d, m, v in zip(small_names, small_w, small_g, s_delta, s_m, s_v)}

    order = ["mem_norm", "lb_logits", "ffn1_norm", "ffn1_w_in", "ffn1_w_out", "mix_norm", "mem_w_kv", "hgrn_w_in", "hgrn_gnorm",
             "hgrn_w_out", "gmlp_w_in", "gmlp_ln_g", "gmlp_ln_b", "gmlp_w_s", "gmlp_b_s", "gmlp_w_out", "ffn2_norm", "ffn2_w_in",
             "ffn2_w_out", "final_norm"]
    allo = {**big_out, **small_out}
    grad_x = dx0.reshape(x.shape)
    return (loss_v.reshape(()), grad_x, *[allo[n][0] for n in order], *[allo[n][1] for n in order],
            *[allo[n][2] for n in order], *[allo[n][3] for n in order])
```

```python
import functools

import jax
import jax.numpy as jnp
from jax import lax
from jax.experimental import pallas as pl
from jax.experimental.pallas import tpu as pltpu
from jax.experimental.pallas import tpu_sc as plsc

BF = jnp.bfloat16
F32 = jnp.float32
MESH = pl.DeviceIdType.MESH

EPS = 1e-6
D_MODEL = 1024
HG_HEADS = 8
HG_DIM = 128
HG_CHUNK = 64
GM_CHUNK = 128
GM_GROUPS = 8
GM_GROUP_DIM = 256
XA_HEADS = 4
XA_DIM = 256
ADAM_LR = 0.001
ADAM_B1 = 0.9
ADAM_B2 = 0.999
ADAM_EPS = 1e-08
ADAM_WD = 0.01
ADAM_STEP = 10

VMEM_CAP_BYTES = 60 * 1024 * 1024
LANES = 1024


def _pick(n, cap, mult=16):
    if n <= cap:
        return n
    for d in range(cap - cap % mult, 0, -mult):
        if n % d == 0:
            return d
    raise ValueError((n, cap, mult))


def _dg(a, b, ca, cb):
    return lax.dot_general(a.astype(BF), b.astype(BF), (((ca,), (cb,)), ((), ())), preferred_element_type=F32)


@jax.custom_vjp
def dot_nn(a, b):
    return _dg(a, b, 1, 0)


def _nn_fwd(a, b):
    return _dg(a, b, 1, 0), (a, b)


def _nn_bwd(r, g):
    a, b = r
    return _dg(g, b, 1, 1), _dg(a, g, 0, 0)


dot_nn.defvjp(_nn_fwd, _nn_bwd)


@jax.custom_vjp
def dot_nt(a, b):
    return _dg(a, b, 1, 1)


def _nt_fwd(a, b):
    return _dg(a, b, 1, 1), (a, b)


def _nt_bwd(r, g):
    a, b = r
    return _dg(g, b, 1, 0), _dg(g, a, 0, 0)


dot_nt.defvjp(_nt_fwd, _nt_bwd)


@jax.custom_vjp
def dot_tn(a, b):
    return _dg(a, b, 0, 0)


def _tn_fwd(a, b):
    return _dg(a, b, 0, 0), (a, b)


def _tn_bwd(r, g):
    a, b = r
    return _dg(b, g, 1, 1), _dg(a, g, 1, 0)


dot_tn.defvjp(_tn_fwd, _tn_bwd)


def _rmsnorm(x, g):
    return x * lax.rsqrt(jnp.mean(x * x, axis=-1, keepdims=True) + EPS) * g


def _silu(x):
    return x * jax.nn.sigmoid(x)


@jax.custom_vjp
def _gelu(x):
    return 0.5 * x * (1.0 + lax.erf(x * (0.5 ** 0.5)))


def _gelu_fwd(x):
    return _gelu(x), x


def _gelu_bwd(x, g):
    t = x * (0.5 ** 0.5)
    cdf = 0.5 * (1.0 + lax.erf(t))
    return (g * (cdf + x * (jnp.exp(-(t * t)) * (0.5 / 3.141592653589793) ** 0.5)),)


_gelu.defvjp(_gelu_fwd, _gelu_bwd)


def _softmax_last(s):
    m = lax.stop_gradient(jnp.max(s, axis=-1, keepdims=True))
    e = jnp.exp(s - m)
    return e / jnp.sum(e, axis=-1, keepdims=True)


def _tril(n):
    r = lax.broadcasted_iota(jnp.int32, (n, n), 0)
    c = lax.broadcasted_iota(jnp.int32, (n, n), 1)
    return r >= c


def _cumsum_rows(l):
    n = l.shape[0]
    return lax.dot_general(_tril(n).astype(F32), l, (((1,), (0,)), ((), ())),
                           precision=lax.Precision.HIGHEST, preferred_element_type=F32)


def _attention(zx, mk, mv):
    s = dot_nt(zx, mk) * (XA_DIM ** -0.5)
    return dot_nn(_softmax_last(s), mv)


def _hgrn_head(zq, zf, zi, zg, l0, l1, l2, gn, S):
    m = lax.stop_gradient(jnp.maximum(jnp.maximum(l0, l1), l2))
    e0 = jnp.exp(l0 - m)
    lb = e0 / (e0 + jnp.exp(l1 - m) + jnp.exp(l2 - m))
    q = _silu(zq)
    f = lb + (1.0 - lb) * jax.nn.sigmoid(zf)
    k = 1.0 - f
    b = _cumsum_rows(jnp.log(f))
    b_last = b[HG_CHUNK - 1:HG_CHUNK, :]
    q_dec = q * jnp.exp(b)
    k_inv = k * jnp.exp(-b)
    a = jnp.where(_tril(HG_CHUNK), dot_nt(q_dec, k_inv), 0.0)
    o = dot_nn(a, zi) + dot_nn(q_dec, S)
    S_new = jnp.exp(b_last).reshape(HG_DIM, 1) * S + dot_tn(k * jnp.exp(b_last - b), zi)
    o = _rmsnorm(o, gn) * _silu(zg)
    return o, S_new


def _hgrn_block(zq, zf, zi, zg, zx, l0, l1, l2, gn, mk, mv, S):
    outs, s_new = [], []
    for h in range(HG_HEADS):
        o, sn = _hgrn_head(zq[h], zf[h], zi[h], zg[h], l0[h], l1[h], l2[h], gn, S[h])
        outs.append(o)
        s_new.append(sn)
    for a in range(XA_HEADS):
        outs.append(_attention(zx[a], mk[a], mv[a]))
    return outs, s_new


def _gmlp_block(zu, zv, zx, lng, lnb, ws, bs, mk, mv):
    gv = [_gelu(v) for v in zv]
    width = GM_GROUPS * GM_GROUP_DIM
    mu = sum(jnp.sum(g, axis=-1, keepdims=True) for g in gv) / width
    xc = [g - mu for g in gv]
    var = sum(jnp.sum(c * c, axis=-1, keepdims=True) for c in xc) / width
    r = lax.rsqrt(var + EPS)
    outs = []
    for g in range(GM_GROUPS):
        v = xc[g] * r * lng[g] + lnb[g]
        w = jnp.where(_tril(GM_CHUNK), ws[g], 0.0)
        mixed = dot_nn(w, v) + bs[g].reshape(GM_CHUNK, 1)
        outs.append(_gelu(zu[g]) * mixed)
    for a in range(XA_HEADS):
        outs.append(_attention(zx[a], mk[a], mv[a]))
    return outs


def _rowcall(name, fn, rows, consts, row_outs, acc_outs, tr):
    nrows = rows[0][0].shape[0]
    tr = _pick(nrows, tr)
    n_r, n_c, n_ro, n_ao = len(rows), len(consts), len(row_outs), len(acc_outs)

    def kern(*refs):
        rv = [r[...] for r in refs[:n_r]]
        cv = [r[...] for r in refs[n_r:n_r + n_c]]
        ro_refs = refs[n_r + n_c:n_r + n_c + n_ro]
        ao_refs = refs[n_r + n_c + n_ro:]
        ro, ao = fn(rv, cv)
        for ref, v in zip(ro_refs, ro):
            ref[...] = v.astype(ref.dtype)
        if n_ao:
            @pl.when(pl.program_id(0) == 0)
            def _():
                for ref in ao_refs:
                    ref[...] = jnp.zeros(ref.shape, ref.dtype)

            for ref, v in zip(ao_refs, ao):
                ref[...] += v.astype(ref.dtype)

    in_specs = [pl.BlockSpec((tr, w), functools.partial(lambda i, cb: (i, cb), cb=cb)) for (_, cb, w) in rows]
    in_specs += [pl.BlockSpec(c.shape, lambda i: (0, 0)) for c in consts]
    out_specs = [pl.BlockSpec((tr, w), lambda i: (i, 0)) for (w, _) in row_outs]
    out_specs += [pl.BlockSpec(s, lambda i: (0, 0)) for (s, _) in acc_outs]
    out_shape = [jax.ShapeDtypeStruct((nrows, w), dt) for (w, dt) in row_outs]
    out_shape += [jax.ShapeDtypeStruct(s, dt) for (s, dt) in acc_outs]
    est = sum(tr * w * a.dtype.itemsize for (a, _, w) in rows) + sum(tr * w * jnp.dtype(dt).itemsize for (w, dt) in row_outs)
    est += sum(c.size * c.dtype.itemsize for c in consts)
    outs = pl.pallas_call(
        kern, grid=(nrows // tr,), in_specs=in_specs, out_specs=out_specs, out_shape=out_shape, name=name,
        compiler_params=pltpu.CompilerParams(dimension_semantics=("arbitrary",),
                                             vmem_limit_bytes=VMEM_CAP_BYTES),
    )(*[a for (a, _, _) in rows], *consts)
    return outs


def _mm(name, a, b, mode, out_dtype, tm, tn, tk, scale=1.0, res=None, a_lead=None, b_lead=None, norm_gain=None):
    ash = a.shape[-2:]
    bsh = b.shape[-2:]
    if mode == "nn":
        (M, K), (K2, N) = ash, bsh
    elif mode == "nt":
        (M, K), (N, K2) = ash, bsh
    else:
        (K, M), (K2, N) = ash, bsh
    assert K == K2, (name, a.shape, b.shape)
    tm, tn, tk = min(tm, M), min(tn, N), min(tk, K)
    assert M % tm == 0 and N % tn == 0 and K % tk == 0, (name, M, N, K, tm, tn, tk)
    nk = K // tk
    dims = {"nn": (1, 0), "nt": (1, 1), "tn": (0, 0)}[mode]

    def lead(spec_shape, index_fn, lead_idx):
        if lead_idx is None:
            return pl.BlockSpec(spec_shape, index_fn)
        return pl.BlockSpec((None,) + spec_shape, lambda i, j, k: (lead_idx,) + index_fn(i, j, k))

    if mode == "tn":
        a_spec = lead((tk, tm), lambda i, j, k: (k, i), a_lead)
    else:
        a_spec = lead((tm, tk), lambda i, j, k: (i, k), a_lead)
    if mode == "nt":
        b_spec = lead((tn, tk), lambda i, j, k: (j, k), b_lead)
    else:
        b_spec = lead((tk, tn), lambda i, j, k: (k, j), b_lead)
    o_spec = pl.BlockSpec((tm, tn), lambda i, j, k: (i, j))
    has_res = res is not None
    has_norm = norm_gain is not None
    assert not has_norm or tn == N

    def kern(*refs):
        a_ref, b_ref = refs[0], refs[1]
        pos = 2
        res_ref = gain_ref = h_ref = None
        if has_res:
            res_ref, pos = refs[pos], pos + 1
        if has_norm:
            gain_ref, pos = refs[pos], pos + 1
        o_ref, pos = refs[pos], pos + 1
        if has_norm:
            h_ref = refs[pos]
        acc_ref = refs[-1] if nk > 1 else None
        p = lax.dot_general(a_ref[...].astype(BF), b_ref[...].astype(BF), (((dims[0],), (dims[1],)), ((), ())),
                            preferred_element_type=F32)

        def finish(v):
            if scale != 1.0:
                v = v * scale
            if has_res:
                v = res_ref[...] + v
            o_ref[...] = v.astype(o_ref.dtype)
            if has_norm:
                h_ref[...] = _rmsnorm(v, gain_ref[...]).astype(h_ref.dtype)

        if nk == 1:
            finish(p)
        else:
            k = pl.program_id(2)

            @pl.when(k == 0)
            def _():
                acc_ref[...] = p

            @pl.when(k > 0)
            def _():
                acc_ref[...] += p

            @pl.when(k == nk - 1)
            def _():
                finish(acc_ref[...])

    ins = [a, b] + ([res] if has_res else []) + ([norm_gain] if has_norm else [])
    in_specs = [a_spec, b_spec] + ([o_spec] if has_res else [])
    in_specs += [pl.BlockSpec((1, N), lambda i, j, k: (0, 0))] if has_norm else []
    out_sd = jax.ShapeDtypeStruct((M, N), out_dtype)
    return pl.pallas_call(
        kern, grid=(M // tm, N // tn, nk), in_specs=in_specs,
        out_specs=[o_spec, o_spec] if has_norm else o_spec,
        out_shape=[out_sd, jax.ShapeDtypeStruct((M, N), BF)] if has_norm else out_sd,
        scratch_shapes=[pltpu.VMEM((tm, tn), F32)] if nk > 1 else [],
        name=name,
        compiler_params=pltpu.CompilerParams(dimension_semantics=("parallel", "parallel", "arbitrary"),
                                             vmem_limit_bytes=VMEM_CAP_BYTES),
    )(*ins)


def _ffn_in_swiglu(name, h, w3, tm, tn):
    T, D = h.shape
    dff = w3.shape[2] // 2
    tm = min(tm, T)
    assert T % tm == 0 and dff % tn == 0
    nj = dff // tn

    def kern(h_ref, wg_ref, wu_ref, zg_ref, zu_ref, a_ref):
        hb = h_ref[...]
        g = jnp.dot(hb, wg_ref[...], preferred_element_type=F32).astype(BF)
        u = jnp.dot(hb, wu_ref[...], preferred_element_type=F32).astype(BF)
        zg_ref[...] = g
        zu_ref[...] = u
        a_ref[...] = (_silu(g.astype(F32)) * u.astype(F32)).astype(BF)

    o_spec = pl.BlockSpec((tm, tn), lambda i, j: (i, j))
    return pl.pallas_call(
        kern, grid=(T // tm, nj),
        in_specs=[pl.BlockSpec((tm, D), lambda i, j: (i, 0)),
                  pl.BlockSpec((None, D, tn), lambda i, j: (0, 0, j)),
                  pl.BlockSpec((None, D, tn), lambda i, j: (0, 0, j + nj))],
        out_specs=[o_spec, o_spec, o_spec],
        out_shape=[jax.ShapeDtypeStruct((T, dff), BF)] * 3, name=name,
        compiler_params=pltpu.CompilerParams(dimension_semantics=("parallel", "arbitrary"),
                                             vmem_limit_bytes=VMEM_CAP_BYTES),
    )(h, w3, w3)


def _ffn_da_swiglu(name, dxo, w3, zg, zu, tm):
    T, D = dxo.shape
    dff = w3.shape[1]
    tm = min(tm, T)
    assert T % tm == 0 and dff % 2 == 0
    hc = dff // 2

    def kern(d_ref, w_ref, g_ref, u_ref, dz_ref):
        db = (d_ref[...] * 0.5).astype(BF)
        for s in range(2):
            cols = slice(s * hc, (s + 1) * hc)
            da = lax.dot_general(db, w_ref[cols, :], (((1,), (1,)), ((), ())), preferred_element_type=F32)
            g = g_ref[:, cols].astype(F32)
            sg = 1.0 / (1.0 + jnp.exp(-g))
            gs = g * sg
            dab = da.astype(BF)
            dz_ref[:, cols] = (dab * u_ref[:, cols]) * (sg + gs * (1.0 - sg)).astype(BF)
            dz_ref[:, dff + s * hc:dff + (s + 1) * hc] = dab * gs.astype(BF)

    row = lambda w: pl.BlockSpec((tm, w), lambda i: (i, 0))
    return pl.pallas_call(
        kern, grid=(T // tm,),
        in_specs=[row(D), pl.BlockSpec((None, dff, D), lambda i: (0, 0, 0), pipeline_mode=pl.Buffered(1)), row(dff), row(dff)],
        out_specs=row(2 * dff), out_shape=jax.ShapeDtypeStruct((T, 2 * dff), BF), name=name,
        compiler_params=pltpu.CompilerParams(dimension_semantics=("arbitrary",), vmem_limit_bytes=VMEM_CAP_BYTES),
    )(dxo, w3, zg, zu)


def _mm_dh_rms(name, dz, w3, xin, g, dres, tm):
    T, K = dz.shape
    D = w3.shape[1]
    tm = min(tm, T)
    assert T % tm == 0

    def kern(dz_ref, w_ref, x_ref, g_ref, r_ref, dx_ref, dg_ref):
        dh = lax.dot_general(dz_ref[...], w_ref[...], (((1,), (1,)), ((), ())), preferred_element_type=F32)
        _, vjp = jax.vjp(_rmsnorm, x_ref[...], g_ref[...])
        dx, dg = vjp(dh)
        dx_ref[...] = dx + r_ref[...]

        @pl.when(pl.program_id(0) == 0)
        def _():
            dg_ref[...] = jnp.zeros(dg_ref.shape, F32)

        dg_ref[...] += dg

    row = lambda w: pl.BlockSpec((tm, w), lambda i: (i, 0))
    one = pl.BlockSpec((1, D), lambda i: (0, 0))
    return pl.pallas_call(
        kern, grid=(T // tm,),
        in_specs=[row(K), pl.BlockSpec((None, D, K), lambda i: (0, 0, 0), pipeline_mode=pl.Buffered(1)), row(D), one, row(D)],
        out_specs=[row(D), one], out_shape=[jax.ShapeDtypeStruct((T, D), F32), jax.ShapeDtypeStruct((1, D), F32)], name=name,
        compiler_params=pltpu.CompilerParams(dimension_semantics=("arbitrary",), vmem_limit_bytes=VMEM_CAP_BYTES),
    )(dz, w3, xin, g, dres)


def _mm_tn_pair(name, a, b, kind, c_arr, tq, tk, scale=1.0):
    T, M = a.shape
    _, N = b.shape
    tk = min(tk, T)
    assert T % tk == 0
    nk = T // tk
    if kind == "col":
        hm = M // 2
        assert N % tq == 0
        nq = N // tq
        tile = (hm, tq)
        a_spec = pl.BlockSpec((tk, hm), lambda h, q, k, c: (k, jnp.bitwise_xor(h, 1 - c[0])))
        b_spec = pl.BlockSpec((tk, tq), lambda h, q, k, c: (k, q))
        o_spec = pl.BlockSpec(tile, lambda h, q, k, c: (0, q * h))
        out_sd = (hm, N)
    else:
        hn = N // 2
        assert M % tq == 0
        nq = M // tq
        tile = (tq, hn)
        a_spec = pl.BlockSpec((tk, tq), lambda h, q, k, c: (k, q))
        b_spec = pl.BlockSpec((tk, hn), lambda h, q, k, c: (k, jnp.bitwise_xor(h, 1 - c[0])))
        o_spec = pl.BlockSpec(tile, lambda h, q, k, c: (q * h, 0))
        out_sd = (M, hn)

    def kern(c_ref, a_ref, b_ref, o_ref, acc, stage, recv, ssem, rsem):
        h, q, k = pl.program_id(0), pl.program_id(1), pl.program_id(2)
        x, y, c, _ = _place()
        p = lax.dot_general(a_ref[...].astype(BF), b_ref[...].astype(BF), (((0,), (0,)), ((), ())), preferred_element_type=F32)

        @pl.when(k == 0)
        def _():
            acc[...] = p

        @pl.when(k > 0)
        def _():
            acc[...] += p

        def send(slot, qq):
            return pltpu.make_async_remote_copy(src_ref=stage.at[slot], dst_ref=recv.at[qq], send_sem=ssem.at[slot],
                                                recv_sem=rsem.at[qq], device_id=(x, y, 1 - c), device_id_type=MESH)

        last = k == nk - 1

        @pl.when(jnp.logical_and(last, h == 0))
        def _():
            slot = q % 2

            @pl.when(q >= 2)
            def _():
                send(slot, q).wait_send()

            stage[slot] = (acc[...] * scale).astype(BF)
            send(slot, q).start()

        @pl.when(jnp.logical_and(last, h == 1))
        def _():
            @pl.when(q == 0)
            def _():
                for s in range(min(nq, 2)):
                    send(s, 0).wait_send()

            send(0, q).wait_recv()
            o_ref[...] = (acc[...] * scale + recv[q].astype(F32)).astype(o_ref.dtype)

    tb = tile[0] * tile[1]
    est = tb * (4 + 2 * 2 + nq * 2 + 2 * 2) + 2 * tk * (a_spec.block_shape[1] + b_spec.block_shape[1]) * 2 * 2
    return pl.pallas_call(
        kern,
        grid_spec=pltpu.PrefetchScalarGridSpec(
            num_scalar_prefetch=1, grid=(2, nq, nk), in_specs=[a_spec, b_spec], out_specs=o_spec,
            scratch_shapes=[pltpu.VMEM(tile, F32), pltpu.VMEM((2,) + tile, BF), pltpu.VMEM((nq,) + tile, BF),
                            pltpu.SemaphoreType.DMA((2,)), pltpu.SemaphoreType.DMA((nq,))]),
        out_shape=jax.ShapeDtypeStruct(out_sd, BF), name=name,
        compiler_params=pltpu.CompilerParams(dimension_semantics=("arbitrary", "arbitrary", "arbitrary"),
                                             vmem_limit_bytes=VMEM_CAP_BYTES),
    )(c_arr, a, b)


def _hgrn_pieces(z_ref):
    W = HG_HEADS * HG_DIM
    zq = [z_ref[:, h * HG_DIM:(h + 1) * HG_DIM] for h in range(HG_HEADS)]
    zf = [z_ref[:, W + h * HG_DIM:W + (h + 1) * HG_DIM] for h in range(HG_HEADS)]
    zi = [z_ref[:, 2 * W + h * HG_DIM:2 * W + (h + 1) * HG_DIM] for h in range(HG_HEADS)]
    zg = [z_ref[:, 3 * W + h * HG_DIM:3 * W + (h + 1) * HG_DIM] for h in range(HG_HEADS)]
    zx = [z_ref[:, 4 * W + a * XA_DIM:4 * W + (a + 1) * XA_DIM] for a in range(XA_HEADS)]
    return zq, zf, zi, zg, zx


def _kv_pieces(kv_ref):
    W = XA_HEADS * XA_DIM
    mk = [kv_ref[:, a * XA_DIM:(a + 1) * XA_DIM] for a in range(XA_HEADS)]
    mv = [kv_ref[:, W + a * XA_DIM:W + (a + 1) * XA_DIM] for a in range(XA_HEADS)]
    return mk, mv


def _lb_pieces(lb_ref):
    return [[lb_ref[r:r + 1, h * HG_DIM:(h + 1) * HG_DIM] for h in range(HG_HEADS)] for r in range(3)]


def _hgrn_fwd(z, lb_logits, gnorm, kv, bl, nc):
    T, zw = z.shape
    mem_len = kv.shape[0] // bl
    cat_w = HG_HEADS * HG_DIM + XA_HEADS * XA_DIM

    def kern(z_ref, lb_ref, gn_ref, kv_ref, cat_ref, st_ref, s_scr):
        @pl.when(pl.program_id(1) == 0)
        def _():
            s_scr[...] = jnp.zeros(s_scr.shape, F32)

        st_ref[...] = s_scr[...]
        zq, zf, zi, zg, zx = _hgrn_pieces(z_ref)
        mk, mv = _kv_pieces(kv_ref)
        l0, l1, l2 = _lb_pieces(lb_ref)
        S = [s_scr[h] for h in range(HG_HEADS)]
        outs, s_new = _hgrn_block(zq, zf, zi, zg, zx, l0, l1, l2, gn_ref[...], mk, mv, S)
        for h in range(HG_HEADS):
            cat_ref[:, h * HG_DIM:(h + 1) * HG_DIM] = outs[h].astype(cat_ref.dtype)
            s_scr[h] = s_new[h]
        base = HG_HEADS * HG_DIM
        for a in range(XA_HEADS):
            cat_ref[:, base + a * XA_DIM:base + (a + 1) * XA_DIM] = outs[HG_HEADS + a].astype(cat_ref.dtype)

    return pl.pallas_call(
        kern, grid=(bl, nc),
        in_specs=[pl.BlockSpec((HG_CHUNK, zw), lambda b, n: (b * nc + n, 0)),
                  pl.BlockSpec(lb_logits.shape, lambda b, n: (0, 0)),
                  pl.BlockSpec(gnorm.shape, lambda b, n: (0, 0)),
                  pl.BlockSpec((mem_len, kv.shape[1]), lambda b, n: (b, 0))],
        out_specs=[pl.BlockSpec((HG_CHUNK, cat_w), lambda b, n: (b * nc + n, 0)),
                   pl.BlockSpec((None, HG_HEADS, HG_DIM, HG_DIM), lambda b, n: (b * nc + n, 0, 0, 0))],
        out_shape=[jax.ShapeDtypeStruct((T, cat_w), BF),
                   jax.ShapeDtypeStruct((bl * nc, HG_HEADS, HG_DIM, HG_DIM), F32)],
        scratch_shapes=[pltpu.VMEM((HG_HEADS, HG_DIM, HG_DIM), F32)],
        name="hgrn_fwd",
        compiler_params=pltpu.CompilerParams(dimension_semantics=("arbitrary", "arbitrary"), vmem_limit_bytes=VMEM_CAP_BYTES),
    )(z, lb_logits, gnorm, kv)


def _hgrn_bwd(z, dcat, stash, lb_logits, gnorm, kv, bl, nc):
    T, zw = z.shape
    mem_len = kv.shape[0] // bl
    cat_w = dcat.shape[1]

    def kern(z_ref, dc_ref, st_ref, lb_ref, gn_ref, kv_ref, dz_ref, dkv_ref, dlb_ref, dgn_ref, ds_scr):
        first = jnp.logical_and(pl.program_id(0) == 0, pl.program_id(1) == 0)

        @pl.when(pl.program_id(1) == 0)
        def _():
            ds_scr[...] = jnp.zeros(ds_scr.shape, F32)
            dkv_ref[...] = jnp.zeros(dkv_ref.shape, F32)

        @pl.when(first)
        def _():
            dlb_ref[...] = jnp.zeros(dlb_ref.shape, F32)
            dgn_ref[...] = jnp.zeros(dgn_ref.shape, F32)

        zq, zf, zi, zg, zx = _hgrn_pieces(z_ref)
        mk, mv = _kv_pieces(kv_ref)
        l0, l1, l2 = _lb_pieces(lb_ref)
        S = [st_ref[h] for h in range(HG_HEADS)]
        _, vjp = jax.vjp(_hgrn_block, zq, zf, zi, zg, zx, l0, l1, l2, gn_ref[...], mk, mv, S)
        d_outs = [dc_ref[:, h * HG_DIM:(h + 1) * HG_DIM] for h in range(HG_HEADS)]
        base = HG_HEADS * HG_DIM
        d_outs += [dc_ref[:, base + a * XA_DIM:base + (a + 1) * XA_DIM] for a in range(XA_HEADS)]
        d_s = [ds_scr[h] for h in range(HG_HEADS)]
        dzq, dzf, dzi, dzg, dzx, dl0, dl1, dl2, dgn, dmk, dmv, dS = vjp((d_outs, d_s))
        W = HG_HEADS * HG_DIM
        for h in range(HG_HEADS):
            sl = slice(h * HG_DIM, (h + 1) * HG_DIM)
            dz_ref[:, sl] = dzq[h].astype(dz_ref.dtype)
            dz_ref[:, W + h * HG_DIM:W + (h + 1) * HG_DIM] = dzf[h].astype(dz_ref.dtype)
            dz_ref[:, 2 * W + h * HG_DIM:2 * W + (h + 1) * HG_DIM] = dzi[h].astype(dz_ref.dtype)
            dz_ref[:, 3 * W + h * HG_DIM:3 * W + (h + 1) * HG_DIM] = dzg[h].astype(dz_ref.dtype)
            ds_scr[h] = dS[h]
            dlb_ref[0:1, sl] += dl0[h]
            dlb_ref[1:2, sl] += dl1[h]
            dlb_ref[2:3, sl] += dl2[h]
        dgn_ref[...] += dgn
        KW = XA_HEADS * XA_DIM
        for a in range(XA_HEADS):
            dz_ref[:, 4 * W + a * XA_DIM:4 * W + (a + 1) * XA_DIM] = dzx[a].astype(dz_ref.dtype)
            dkv_ref[:, a * XA_DIM:(a + 1) * XA_DIM] += dmk[a]
            dkv_ref[:, KW + a * XA_DIM:KW + (a + 1) * XA_DIM] += dmv[a]

    rev = lambda b, n: (b * nc + (nc - 1 - n), 0)
    return pl.pallas_call(
        kern, grid=(bl, nc),
        in_specs=[pl.BlockSpec((HG_CHUNK, zw), rev),
                  pl.BlockSpec((HG_CHUNK, cat_w), rev),
                  pl.BlockSpec((None, HG_HEADS, HG_DIM, HG_DIM), lambda b, n: (b * nc + (nc - 1 - n), 0, 0, 0)),
                  pl.BlockSpec(lb_logits.shape, lambda b, n: (0, 0)),
                  pl.BlockSpec(gnorm.shape, lambda b, n: (0, 0)),
                  pl.BlockSpec((mem_len, kv.shape[1]), lambda b, n: (b, 0))],
        out_specs=[pl.BlockSpec((HG_CHUNK, zw), rev),
                   pl.BlockSpec((mem_len, kv.shape[1]), lambda b, n: (b, 0)),
                   pl.BlockSpec(lb_logits.shape, lambda b, n: (0, 0)),
                   pl.BlockSpec(gnorm.shape, lambda b, n: (0, 0))],
        out_shape=[jax.ShapeDtypeStruct((T, zw), BF), jax.ShapeDtypeStruct(kv.shape, F32),
                   jax.ShapeDtypeStruct(lb_logits.shape, F32), jax.ShapeDtypeStruct(gnorm.shape, F32)],
        scratch_shapes=[pltpu.VMEM((HG_HEADS, HG_DIM, HG_DIM), F32)],
        name="hgrn_bwd",
        compiler_params=pltpu.CompilerParams(dimension_semantics=("arbitrary", "arbitrary"), vmem_limit_bytes=VMEM_CAP_BYTES),
    )(z, dcat, stash, lb_logits, gnorm, kv)


HG_SUB = 4


def _hgrn_rows(z_ref, dtype_cast=None):
    W = HG_HEADS * HG_DIM

    def piece(c, col, w):
        return z_ref[c * HG_CHUNK:(c + 1) * HG_CHUNK, col:col + w]

    zq = [[piece(c, h * HG_DIM, HG_DIM) for h in range(HG_HEADS)] for c in range(HG_SUB)]
    zf = [[piece(c, W + h * HG_DIM, HG_DIM) for h in range(HG_HEADS)] for c in range(HG_SUB)]
    zi = [[piece(c, 2 * W + h * HG_DIM, HG_DIM) for h in range(HG_HEADS)] for c in range(HG_SUB)]
    zg = [[piece(c, 3 * W + h * HG_DIM, HG_DIM) for h in range(HG_HEADS)] for c in range(HG_SUB)]
    zx = [z_ref[:, 4 * W + a * XA_DIM:4 * W + (a + 1) * XA_DIM] for a in range(XA_HEADS)]
    return zq, zf, zi, zg, zx


def _hgrn_steps(zq, zf, zi, zg, zx, l0, l1, l2, gn, mk, mv, S):
    mix = []
    for c in range(HG_SUB):
        row, s_next = [], []
        for h in range(HG_HEADS):
            o, sn = _hgrn_head(zq[c][h], zf[c][h], zi[c][h], zg[c][h], l0[h], l1[h], l2[h], gn, S[h])
            row.append(o)
            s_next.append(sn)
        mix.append(row)
        S = s_next
    att = [_attention(zx[a], mk[a], mv[a]) for a in range(XA_HEADS)]
    return mix, att, S


def _hgrn_fwd2(z, lb_logits, gnorm, kv, bl, seq):
    T, zw = z.shape
    mem_len = kv.shape[0] // bl
    cat_w = HG_HEADS * HG_DIM + XA_HEADS * XA_DIM
    R = HG_SUB * HG_CHUNK
    nb = seq // R

    def kern(z_ref, lb_ref, gn_ref, kv_ref, cat_ref, st_ref, s_scr):
        @pl.when(pl.program_id(1) == 0)
        def _():
            s_scr[...] = jnp.zeros(s_scr.shape, F32)

        st_ref[...] = s_scr[...]
        zq, zf, zi, zg, zx = _hgrn_rows(z_ref)
        mk, mv = _kv_pieces(kv_ref)
        l0, l1, l2 = _lb_pieces(lb_ref)
        S = [s_scr[h] for h in range(HG_HEADS)]
        mix, att, s_new = _hgrn_steps(zq, zf, zi, zg, zx, l0, l1, l2, gn_ref[...], mk, mv, S)
        for c in range(HG_SUB):
            for h in range(HG_HEADS):
                cat_ref[c * HG_CHUNK:(c + 1) * HG_CHUNK, h * HG_DIM:(h + 1) * HG_DIM] = mix[c][h].astype(cat_ref.dtype)
        for h in range(HG_HEADS):
            s_scr[h] = s_new[h]
        base = HG_HEADS * HG_DIM
        for a in range(XA_HEADS):
            cat_ref[:, base + a * XA_DIM:base + (a + 1) * XA_DIM] = att[a].astype(cat_ref.dtype)

    return pl.pallas_call(
        kern, grid=(bl, nb),
        in_specs=[pl.BlockSpec((R, zw), lambda b, n: (b * nb + n, 0)),
                  pl.BlockSpec(lb_logits.shape, lambda b, n: (0, 0)),
                  pl.BlockSpec(gnorm.shape, lambda b, n: (0, 0)),
                  pl.BlockSpec((mem_len, kv.shape[1]), lambda b, n: (b, 0))],
        out_specs=[pl.BlockSpec((R, cat_w), lambda b, n: (b * nb + n, 0)),
                   pl.BlockSpec((None, HG_HEADS, HG_DIM, HG_DIM), lambda b, n: (b * nb + n, 0, 0, 0))],
        out_shape=[jax.ShapeDtypeStruct((T, cat_w), BF),
                   jax.ShapeDtypeStruct((bl * nb, HG_HEADS, HG_DIM, HG_DIM), F32)],
        scratch_shapes=[pltpu.VMEM((HG_HEADS, HG_DIM, HG_DIM), F32)],
        name="hgrn_fwd",
        compiler_params=pltpu.CompilerParams(dimension_semantics=("arbitrary", "arbitrary"), vmem_limit_bytes=VMEM_CAP_BYTES),
    )(z, lb_logits, gnorm, kv)


def _hgrn_bwd2(z, dcat, stash, lb_logits, gnorm, kv, bl, seq):
    T, zw = z.shape
    mem_len = kv.shape[0] // bl
    cat_w = dcat.shape[1]
    R = HG_SUB * HG_CHUNK
    nb = seq // R

    def kern(z_ref, dc_ref, st_ref, lb_ref, gn_ref, kv_ref, dz_ref, dkv_ref, dlb_ref, dgn_ref, ds_scr):
        first = jnp.logical_and(pl.program_id(0) == 0, pl.program_id(1) == 0)

        @pl.when(pl.program_id(1) == 0)
        def _():
            ds_scr[...] = jnp.zeros(ds_scr.shape, F32)
            dkv_ref[...] = jnp.zeros(dkv_ref.shape, F32)

        @pl.when(first)
        def _():
            dlb_ref[...] = jnp.zeros(dlb_ref.shape, F32)
            dgn_ref[...] = jnp.zeros(dgn_ref.shape, F32)

        zq, zf, zi, zg, zx = _hgrn_rows(z_ref)
        mk, mv = _kv_pieces(kv_ref)
        l0, l1, l2 = _lb_pieces(lb_ref)
        S = [st_ref[h] for h in range(HG_HEADS)]
        _, vjp = jax.vjp(_hgrn_steps, zq, zf, zi, zg, zx, l0, l1, l2, gn_ref[...], mk, mv, S)
        d_mix = [[dc_ref[c * HG_CHUNK:(c + 1) * HG_CHUNK, h * HG_DIM:(h + 1) * HG_DIM] for h in range(HG_HEADS)]
                 for c in range(HG_SUB)]
        base = HG_HEADS * HG_DIM
        d_att = [dc_ref[:, base + a * XA_DIM:base + (a + 1) * XA_DIM] for a in range(XA_HEADS)]
        d_s = [ds_scr[h] for h in range(HG_HEADS)]
        dzq, dzf, dzi, dzg, dzx, dl0, dl1, dl2, dgn, dmk, dmv, dS = vjp((d_mix, d_att, d_s))
        W = HG_HEADS * HG_DIM
        for c in range(HG_SUB):
            rows = slice(c * HG_CHUNK, (c + 1) * HG_CHUNK)
            for h in range(HG_HEADS):
                for k, part in enumerate((dzq, dzf, dzi, dzg)):
                    dz_ref[rows, k * W + h * HG_DIM:k * W + (h + 1) * HG_DIM] = part[c][h].astype(dz_ref.dtype)
        for h in range(HG_HEADS):
            sl = slice(h * HG_DIM, (h + 1) * HG_DIM)
            ds_scr[h] = dS[h]
            dlb_ref[0:1, sl] += dl0[h]
            dlb_ref[1:2, sl] += dl1[h]
            dlb_ref[2:3, sl] += dl2[h]
        dgn_ref[...] += dgn
        KW = XA_HEADS * XA_DIM
        for a in range(XA_HEADS):
            dz_ref[:, 4 * W + a * XA_DIM:4 * W + (a + 1) * XA_DIM] = dzx[a].astype(dz_ref.dtype)
            dkv_ref[:, a * XA_DIM:(a + 1) * XA_DIM] += dmk[a]
            dkv_ref[:, KW + a * XA_DIM:KW + (a + 1) * XA_DIM] += dmv[a]

    rev = lambda b, n: (b * nb + (nb - 1 - n), 0)
    return pl.pallas_call(
        kern, grid=(bl, nb),
        in_specs=[pl.BlockSpec((R, zw), rev),
                  pl.BlockSpec((R, cat_w), rev),
                  pl.BlockSpec((None, HG_HEADS, HG_DIM, HG_DIM), lambda b, n: (b * nb + (nb - 1 - n), 0, 0, 0)),
                  pl.BlockSpec(lb_logits.shape, lambda b, n: (0, 0)),
                  pl.BlockSpec(gnorm.shape, lambda b, n: (0, 0)),
                  pl.BlockSpec((mem_len, kv.shape[1]), lambda b, n: (b, 0))],
        out_specs=[pl.BlockSpec((R, zw), rev),
                   pl.BlockSpec((mem_len, kv.shape[1]), lambda b, n: (b, 0)),
                   pl.BlockSpec(lb_logits.shape, lambda b, n: (0, 0)),
                   pl.BlockSpec(gnorm.shape, lambda b, n: (0, 0))],
        out_shape=[jax.ShapeDtypeStruct((T, zw), BF), jax.ShapeDtypeStruct(kv.shape, F32),
                   jax.ShapeDtypeStruct(lb_logits.shape, F32), jax.ShapeDtypeStruct(gnorm.shape, F32)],
        scratch_shapes=[pltpu.VMEM((HG_HEADS, HG_DIM, HG_DIM), F32)],
        name="hgrn_bwd",
        compiler_params=pltpu.CompilerParams(dimension_semantics=("arbitrary", "arbitrary"), vmem_limit_bytes=VMEM_CAP_BYTES),
    )(z, dcat, stash, lb_logits, gnorm, kv)


GM_SUB = 2


def _gmlp_pieces(z_ref):
    W = GM_GROUPS * GM_GROUP_DIM
    zu = [z_ref[:, g * GM_GROUP_DIM:(g + 1) * GM_GROUP_DIM] for g in range(GM_GROUPS)]
    zv = [z_ref[:, W + g * GM_GROUP_DIM:W + (g + 1) * GM_GROUP_DIM] for g in range(GM_GROUPS)]
    zx = [z_ref[:, 2 * W + a * XA_DIM:2 * W + (a + 1) * XA_DIM] for a in range(XA_HEADS)]
    return zu, zv, zx


def _gmlp_params(lng_ref, lnb_ref, ws_ref, bs_ref):
    lng = [lng_ref[:, g * GM_GROUP_DIM:(g + 1) * GM_GROUP_DIM] for g in range(GM_GROUPS)]
    lnb = [lnb_ref[:, g * GM_GROUP_DIM:(g + 1) * GM_GROUP_DIM] for g in range(GM_GROUPS)]
    ws = [ws_ref[g] for g in range(GM_GROUPS)]
    bs = [bs_ref[g:g + 1, :] for g in range(GM_GROUPS)]
    return lng, lnb, ws, bs


def _gmlp_fwd(z, ln_g, ln_b, w_s, b_s, kv, bl, nc):
    T, zw = z.shape
    mem_len = kv.shape[0] // bl
    cat_w = GM_GROUPS * GM_GROUP_DIM + XA_HEADS * XA_DIM

    assert nc % GM_SUB == 0
    nc = nc // GM_SUB
    R = GM_SUB * GM_CHUNK

    def kern(z_ref, lng_ref, lnb_ref, ws_ref, bs_ref, kv_ref, cat_ref):
        lng, lnb, ws, bs = _gmlp_params(lng_ref, lnb_ref, ws_ref, bs_ref)
        mk, mv = _kv_pieces(kv_ref)
        for c in range(GM_SUB):
            rows = pl.ds(c * GM_CHUNK, GM_CHUNK)
            zu, zv, zx = _gmlp_pieces(z_ref.at[rows])
            out = cat_ref.at[rows]
            outs = _gmlp_block(zu, zv, zx, lng, lnb, ws, bs, mk, mv)
            for g in range(GM_GROUPS):
                out[:, g * GM_GROUP_DIM:(g + 1) * GM_GROUP_DIM] = outs[g].astype(cat_ref.dtype)
            base = GM_GROUPS * GM_GROUP_DIM
            for a in range(XA_HEADS):
                out[:, base + a * XA_DIM:base + (a + 1) * XA_DIM] = outs[GM_GROUPS + a].astype(cat_ref.dtype)

    full2 = lambda b, n: (0, 0)
    return pl.pallas_call(
        kern, grid=(bl, nc),
        in_specs=[pl.BlockSpec((R, zw), lambda b, n: (b * nc + n, 0)),
                  pl.BlockSpec(ln_g.shape, full2), pl.BlockSpec(ln_b.shape, full2),
                  pl.BlockSpec(w_s.shape, lambda b, n: (0, 0, 0)), pl.BlockSpec(b_s.shape, full2),
                  pl.BlockSpec((mem_len, kv.shape[1]), lambda b, n: (b, 0))],
        out_specs=pl.BlockSpec((R, cat_w), lambda b, n: (b * nc + n, 0)),
        out_shape=jax.ShapeDtypeStruct((T, cat_w), BF),
        name="gmlp_fwd",
        compiler_params=pltpu.CompilerParams(dimension_semantics=("arbitrary", "arbitrary"), vmem_limit_bytes=VMEM_CAP_BYTES),
    )(z, ln_g, ln_b, w_s, b_s, kv)


def _gmlp_bwd(z, dcat, ln_g, ln_b, w_s, b_s, kv, bl, nc):
    T, zw = z.shape
    mem_len = kv.shape[0] // bl
    cat_w = dcat.shape[1]
    assert nc % GM_SUB == 0
    nc = nc // GM_SUB

    def kern(z_ref, dc_ref, lng_ref, lnb_ref, ws_ref, bs_ref, kv_ref,
             dz_ref, dkv_ref, dlng_ref, dlnb_ref, dws_ref, dbs_ref):
        first = jnp.logical_and(pl.program_id(0) == 0, pl.program_id(1) == 0)

        @pl.when(pl.program_id(1) == 0)
        def _():
            dkv_ref[...] = jnp.zeros(dkv_ref.shape, F32)

        @pl.when(first)
        def _():
            dlng_ref[...] = jnp.zeros(dlng_ref.shape, F32)
            dlnb_ref[...] = jnp.zeros(dlnb_ref.shape, F32)
            dws_ref[...] = jnp.zeros(dws_ref.shape, F32)
            dbs_ref[...] = jnp.zeros(dbs_ref.shape, F32)

        lng, lnb, ws, bs = _gmlp_params(lng_ref, lnb_ref, ws_ref, bs_ref)
        mk, mv = _kv_pieces(kv_ref)
        W = GM_GROUPS * GM_GROUP_DIM
        KW = XA_HEADS * XA_DIM
        for c in range(GM_SUB):
            rows = pl.ds(c * GM_CHUNK, GM_CHUNK)
            zu, zv, zx = _gmlp_pieces(z_ref.at[rows])
            dc, dz = dc_ref.at[rows], dz_ref.at[rows]
            _, vjp = jax.vjp(_gmlp_block, zu, zv, zx, lng, lnb, ws, bs, mk, mv)
            d_outs = [dc[:, g * GM_GROUP_DIM:(g + 1) * GM_GROUP_DIM] for g in range(GM_GROUPS)]
            d_outs += [dc[:, W + a * XA_DIM:W + (a + 1) * XA_DIM] for a in range(XA_HEADS)]
            dzu, dzv, dzx, dlng, dlnb, dws, dbs, dmk, dmv = vjp(d_outs)
            for g in range(GM_GROUPS):
                sl = slice(g * GM_GROUP_DIM, (g + 1) * GM_GROUP_DIM)
                dz[:, sl] = dzu[g].astype(dz_ref.dtype)
                dz[:, W + g * GM_GROUP_DIM:W + (g + 1) * GM_GROUP_DIM] = dzv[g].astype(dz_ref.dtype)
                dlng_ref[:, sl] += dlng[g]
                dlnb_ref[:, sl] += dlnb[g]
                dws_ref[g] += dws[g]
                dbs_ref[g:g + 1, :] += dbs[g]
            for a in range(XA_HEADS):
                dz[:, 2 * W + a * XA_DIM:2 * W + (a + 1) * XA_DIM] = dzx[a].astype(dz_ref.dtype)
                dkv_ref[:, a * XA_DIM:(a + 1) * XA_DIM] += dmk[a]
                dkv_ref[:, KW + a * XA_DIM:KW + (a + 1) * XA_DIM] += dmv[a]

    full2 = lambda b, n: (0, 0)
    full3 = lambda b, n: (0, 0, 0)
    blk = lambda b, n: (b * nc + n, 0)
    return pl.pallas_call(
        kern, grid=(bl, nc),
        in_specs=[pl.BlockSpec((GM_SUB * GM_CHUNK, zw), blk), pl.BlockSpec((GM_SUB * GM_CHUNK, cat_w), blk),
                  pl.BlockSpec(ln_g.shape, full2), pl.BlockSpec(ln_b.shape, full2),
                  pl.BlockSpec(w_s.shape, full3), pl.BlockSpec(b_s.shape, full2),
                  pl.BlockSpec((mem_len, kv.shape[1]), lambda b, n: (b, 0))],
        out_specs=[pl.BlockSpec((GM_SUB * GM_CHUNK, zw), blk),
                   pl.BlockSpec((mem_len, kv.shape[1]), lambda b, n: (b, 0)),
                   pl.BlockSpec(ln_g.shape, full2), pl.BlockSpec(ln_b.shape, full2),
                   pl.BlockSpec(w_s.shape, full3), pl.BlockSpec(b_s.shape, full2)],
        out_shape=[jax.ShapeDtypeStruct((T, zw), BF), jax.ShapeDtypeStruct(kv.shape, F32),
                   jax.ShapeDtypeStruct(ln_g.shape, F32), jax.ShapeDtypeStruct(ln_b.shape, F32),
                   jax.ShapeDtypeStruct(w_s.shape, F32), jax.ShapeDtypeStruct(b_s.shape, F32)],
        name="gmlp_bwd",
        compiler_params=pltpu.CompilerParams(dimension_semantics=("arbitrary", "arbitrary"), vmem_limit_bytes=VMEM_CAP_BYTES),
    )(z, dcat, ln_g, ln_b, w_s, b_s, kv)


def _place():
    x, y, c = lax.axis_index("x"), lax.axis_index("y"), lax.axis_index("c")
    chips = [(1 - x, y), (x, 1 - y), (1 - x, 1 - y)]
    return x, y, c, chips


def _half(ref, kind, e):
    if kind == "col":
        n = ref.shape[1] // 2
        return ref.at[:, pl.ds(pl.multiple_of(e * n, n), n), :]
    n = ref.shape[2] // 2
    return ref.at[:, :, pl.ds(pl.multiple_of(e * n, n), n)]


def _slot(ref, kind, j, n):
    if kind == "col":
        return ref.at[:, :, pl.ds(pl.multiple_of(j * n, n), n)]
    return ref.at[:, pl.ds(pl.multiple_of(j * n, n), n), :]


AG_CHUNKS = 4


def _allgather_seq(name, items, cid):
    nt = len(items)
    kinds = [k for (_, k, _) in items]
    slot_kind = ["row" if k == "row" else "col" for k in kinds]
    out_type = []
    for s, k, l in items:
        L, r, c = s.shape
        lo = L if l is None else 1
        out_type.append(jax.ShapeDtypeStruct((lo, 4 * r, c) if k == "row" else (lo, r, 4 * c), s.dtype))

    def part(ref, t, e, q):
        if kinds[t] == "vec":
            return ref
        half = _half(ref, kinds[t], e)
        n = half.shape[1] // AG_CHUNKS
        return half.at[:, pl.ds(q * n, n), :]

    def chunks(t):
        return 1 if kinds[t] == "vec" else AG_CHUNKS

    def body(*refs):
        sh = [refs[t] if items[t][2] is None else refs[t].at[pl.ds(items[t][2], 1)] for t in range(nt)]
        full = refs[nt:2 * nt]
        loc, s_ici, r_ici, s_d2d, r_d2d = refs[2 * nt:]
        x, y, c, chips = _place()
        own = 2 * x + y
        sibling = (x, y, 1 - c)
        barrier = pltpu.get_barrier_semaphore()
        for peer in [(px, py, c) for (px, py) in chips] + [sibling]:
            pl.semaphore_signal(barrier, inc=1, device_id=peer, device_id_type=MESH)
        pl.semaphore_wait(barrier, 4)
        width = [sh[t].shape[1] if kinds[t] == "row" else sh[t].shape[2] for t in range(nt)]
        started = []
        for t in range(nt):
            mine = pltpu.make_async_copy(sh[t], _slot(full[t], slot_kind[t], own, width[t]), loc.at[t])
            mine.start()
            started.append(mine)
        sent = []
        for q in range(AG_CHUNKS):
            for t in range(nt):
                if q >= chunks(t):
                    continue
                for p, (px, py) in enumerate(chips):
                    cp = pltpu.make_async_remote_copy(
                        src_ref=part(sh[t], t, c, q), dst_ref=part(_slot(full[t], slot_kind[t], own, width[t]), t, c, q),
                        send_sem=s_ici.at[t, p, q], recv_sem=r_ici.at[t, p, q], device_id=(px, py, c), device_id_type=MESH)
                    cp.start()
                    sent.append(cp)
        for q in range(AG_CHUNKS):
            for t in range(nt):
                if q >= chunks(t):
                    continue
                for p, (px, py) in enumerate(chips):
                    landed = part(_slot(full[t], slot_kind[t], 2 * px + py, width[t]), t, c, q)
                    pltpu.make_async_remote_copy(
                        src_ref=landed, dst_ref=landed, send_sem=s_ici.at[t, p, q], recv_sem=r_ici.at[t, p, q],
                        device_id=(px, py, c), device_id_type=MESH).wait_recv()
                    if kinds[t] == "vec":
                        continue
                    fw = pltpu.make_async_remote_copy(
                        src_ref=landed, dst_ref=landed, send_sem=s_d2d.at[t, p, q], recv_sem=r_d2d.at[t, p, q],
                        device_id=sibling, device_id_type=MESH)
                    fw.start()
                    sent.append(fw)
        for t in range(nt):
            if kinds[t] == "vec":
                continue
            for p, (px, py) in enumerate(chips):
                for q in range(AG_CHUNKS):
                    other = part(_slot(full[t], kinds[t], 2 * px + py, width[t]), t, 1 - c, q)
                    pltpu.make_async_remote_copy(
                        src_ref=other, dst_ref=other, send_sem=s_d2d.at[t, p, q], recv_sem=r_d2d.at[t, p, q],
                        device_id=sibling, device_id_type=MESH).wait_recv()
        for cp in sent:
            cp.wait_send()
        for cp in started:
            cp.wait()

    return pl.kernel(
        body, out_type=out_type, mesh=plsc.ScalarSubcoreMesh(axis_name="seq", num_cores=1),
        scratch_types=[pltpu.SemaphoreType.DMA((nt,))] + [pltpu.SemaphoreType.DMA((nt, 3, AG_CHUNKS))] * 4,
        compiler_params=pltpu.CompilerParams(collective_id=cid), name=name,
    )(*[s for (s, _, _) in items])


def _slot2(ref, kind, j, n):
    if kind == "col":
        return ref.at[:, pl.ds(pl.multiple_of(j * n, n), n)]
    return ref.at[pl.ds(pl.multiple_of(j * n, n), n), :]


def _rs_chips_seq(name, parts, kinds, cid):
    nm = len(parts)
    out_type = []
    for g, k in zip(parts, kinds):
        r, c = g.shape
        ps = (r, c // 4) if k == "col" else (r // 4, c)
        out_type += [jax.ShapeDtypeStruct(ps, BF), jax.ShapeDtypeStruct((3,) + ps, BF)]

    def body(*refs):
        g = refs[:nm]
        outs = refs[nm:3 * nm]
        loc, ssem, rsem = refs[3 * nm:]
        x, y, c, chips = _place()
        own = 2 * x + y
        barrier = pltpu.get_barrier_semaphore()
        for (px, py) in chips:
            pl.semaphore_signal(barrier, inc=1, device_id=(px, py, c), device_id_type=MESH)
        pl.semaphore_wait(barrier, 3)
        cps = []
        for m in range(nm):
            k = kinds[m]
            own_o, got_o = outs[2 * m], outs[2 * m + 1]
            n = g[m].shape[1] // 4 if k == "col" else g[m].shape[0] // 4
            lc = pltpu.make_async_copy(_slot2(g[m], k, own, n), own_o, loc.at[m])
            lc.start()
            cps.append(lc)
            for p, (px, py) in enumerate(chips):
                cp = pltpu.make_async_remote_copy(
                    src_ref=_slot2(g[m], k, 2 * px + py, n), dst_ref=got_o.at[p],
                    send_sem=ssem.at[m, p], recv_sem=rsem.at[m, p], device_id=(px, py, c), device_id_type=MESH)
                cp.start()
                cps.append(cp)
        for cp in cps:
            cp.wait()

    return pl.kernel(
        body, out_type=out_type, mesh=plsc.ScalarSubcoreMesh(axis_name="seq", num_cores=1),
        scratch_types=[pltpu.SemaphoreType.DMA((nm,)), pltpu.SemaphoreType.DMA((nm, 3)), pltpu.SemaphoreType.DMA((nm, 3))],
        compiler_params=pltpu.CompilerParams(collective_id=cid), name=name,
    )(*parts)


def _finish_share(name, own, got, kind, c_arr):
    L, r, c = own.shape
    tr = _pick(r, 128 if kind == "col" else 256)
    nb = r // tr
    nq = L * nb
    own2 = own.reshape(L * r, c)
    got2 = got.reshape(3 * L * r, c)
    pick = lambda h, q: q * (1 - h) + (nq - 1) * h
    in_specs = [pl.BlockSpec((tr, c), lambda h, q, cc: (pick(h, q), 0))]
    in_specs += [pl.BlockSpec((tr, c), functools.partial(lambda h, q, cc, p: (p * nq + pick(h, q), 0), p=p)) for p in range(3)]
    if kind == "col":
        out_sd = (L, 2, r, c)
        o_spec = pl.BlockSpec((None, 2, tr, c), lambda h, q, cc: ((q * h) // nb, 0, (q * h) % nb, 0))
    else:
        out_sd = (L * r, 2 * c)
        o_spec = pl.BlockSpec((tr, 2 * c), lambda h, q, cc: (q * h, 0))

    def kern(c_ref, o_ref, g0, g1, g2, out_ref, mine, recv, ssem, rsem):
        h, q = pl.program_id(0), pl.program_id(1)
        x, y, cc, _ = _place()

        def swap(qq):
            return pltpu.make_async_remote_copy(src_ref=mine.at[qq], dst_ref=recv.at[qq], send_sem=ssem.at[qq],
                                                recv_sem=rsem.at[qq], device_id=(x, y, 1 - cc), device_id_type=MESH)

        @pl.when(h == 0)
        def _():
            mine[q] = ((o_ref[...].astype(F32) + g0[...].astype(F32)) + g1[...].astype(F32)) + g2[...].astype(F32)
            swap(q).start()

        @pl.when(h == 1)
        def _():
            swap(q).wait()
            a, b = mine[q], recv[q]
            first = c_ref[0] == 0
            lo, hi = jnp.where(first, a, b), jnp.where(first, b, a)
            if kind == "col":
                out_ref[0] = lo
                out_ref[1] = hi
            else:
                out_ref[:, :c] = lo
                out_ref[:, c:] = hi

    est = 2 * nq * tr * c * 4 + 6 * tr * c * 4 + 8 * tr * c * 2
    full = pl.pallas_call(
        kern,
        grid_spec=pltpu.PrefetchScalarGridSpec(
            num_scalar_prefetch=1, grid=(2, nq), in_specs=in_specs, out_specs=o_spec,
            scratch_shapes=[pltpu.VMEM((nq, tr, c), F32), pltpu.VMEM((nq, tr, c), F32),
                            pltpu.SemaphoreType.DMA((nq,)), pltpu.SemaphoreType.DMA((nq,))]),
        out_shape=jax.ShapeDtypeStruct(out_sd, F32), name=name,
        compiler_params=pltpu.CompilerParams(dimension_semantics=("arbitrary", "arbitrary"),
                                             vmem_limit_bytes=VMEM_CAP_BYTES),
    )(c_arr, own2, got2, got2, got2)
    return full.reshape(L, 2 * r, c) if kind == "col" else full.reshape(L, r, 2 * c)


def _small_allreduce(buf, name):
    R = buf.shape[0]
    assert R % 16 == 0
    h = R // 2

    def body(x_ref, o_ref, sib, csum, got, s_a, r_a, s_b, r_b, s_c, r_c):
        x, y, c, chips = _place()
        sibling = (x, y, 1 - c)
        own = 2 * x + y
        swap = pltpu.make_async_remote_copy(src_ref=x_ref, dst_ref=sib, send_sem=s_a, recv_sem=r_a,
                                            device_id=sibling, device_id_type=MESH)
        swap.start()
        swap.wait()
        a, b = x_ref[...], sib[...]
        south = c == 0
        csum[...] = jnp.where(south, a, b) + jnp.where(south, b, a)
        lo = pl.multiple_of(c * h, 8)
        mine = csum.at[pl.ds(lo, h)]
        got[own] = csum[pl.ds(lo, h)]
        sends = []
        for p, (px, py) in enumerate(chips):
            cp = pltpu.make_async_remote_copy(src_ref=mine, dst_ref=got.at[own], send_sem=s_b.at[p], recv_sem=r_b.at[p],
                                              device_id=(px, py, c), device_id_type=MESH)
            cp.start()
            sends.append(cp)
        for cp in sends:
            cp.wait()
        o_ref[pl.ds(lo, h)] = ((got[0] + got[1]) + got[2]) + got[3]
        done = o_ref.at[pl.ds(lo, h)]
        back = pltpu.make_async_remote_copy(src_ref=done, dst_ref=done, send_sem=s_c, recv_sem=r_c,
                                            device_id=sibling, device_id_type=MESH)
        back.start()
        back.wait_send()
        other = o_ref.at[pl.ds(pl.multiple_of((1 - c) * h, 8), h)]
        pltpu.make_async_remote_copy(src_ref=other, dst_ref=other, send_sem=s_c, recv_sem=r_c,
                                     device_id=sibling, device_id_type=MESH).wait_recv()

    vm = pl.BlockSpec(memory_space=pltpu.VMEM)
    return pl.pallas_call(
        body, out_shape=jax.ShapeDtypeStruct(buf.shape, F32), in_specs=[vm], out_specs=vm,
        scratch_shapes=[pltpu.VMEM((R, LANES), F32), pltpu.VMEM((R, LANES), F32), pltpu.VMEM((4, h, LANES), F32),
                        pltpu.SemaphoreType.DMA, pltpu.SemaphoreType.DMA, pltpu.SemaphoreType.DMA((3,)),
                        pltpu.SemaphoreType.DMA((3,)), pltpu.SemaphoreType.DMA, pltpu.SemaphoreType.DMA],
        name=name,
        compiler_params=pltpu.CompilerParams(vmem_limit_bytes=VMEM_CAP_BYTES),
    )(buf)


PACK_TILE_ROWS = 8


def _item_rows(shape):
    n = 1
    for d in shape:
        n *= d
    return -(-n // (PACK_TILE_ROWS * LANES)) * PACK_TILE_ROWS


def _pack(arrs, rows_total):
    buf = jnp.zeros((rows_total, LANES), F32)
    r = 0
    for a in arrs:
        f = a.reshape(-1).astype(F32)
        nr = _item_rows(a.shape)
        block = jnp.pad(f, (0, nr * LANES - f.shape[0])).reshape(nr, LANES)
        buf = lax.dynamic_update_slice(buf, block, (r, 0))
        r += nr
    return buf


def _unpack(buf, shapes):
    out, r = [], 0
    for s in shapes:
        n = 1
        for d in s:
            n *= d
        nr = _item_rows(s)
        out.append(buf[r:r + nr].reshape(-1)[:n].reshape(s))
        r += nr
    return out


def _rows_needed(shapes):
    return -(-sum(_item_rows(s) for s in shapes) // (2 * PACK_TILE_ROWS)) * (2 * PACK_TILE_ROWS)


def _two_rows(a, b):
    out = jnp.zeros((2, a.shape[1]), a.dtype)
    return lax.dynamic_update_slice(lax.dynamic_update_slice(out, a, (0, 0)), b, (1, 0))


def _adam(w, g, m, v):
    m = ADAM_B1 * m + (1.0 - ADAM_B1) * g
    v = ADAM_B2 * v + (1.0 - ADAM_B2) * jnp.square(g)
    m_hat = m / (1.0 - ADAM_B1 ** ADAM_STEP)
    v_hat = v / (1.0 - ADAM_B2 ** ADAM_STEP)
    delta = -ADAM_LR * (m_hat / (jnp.sqrt(v_hat) + ADAM_EPS) + ADAM_WD * w)
    return delta, m, v


def _adam_call(name, w2, g2, m2, v2, tr):
    def fn(rv, cv):
        return list(_adam(*rv)), []

    width = w2.shape[1]
    return _rowcall(name, fn, [(w2, 0, width), (g2, 0, width), (m2, 0, width), (v2, 0, width)], [],
                    [(width, F32)] * 3, [], tr)


def kernel(x, mem, mem_norm, lb_logits, ffn1_norm, ffn1_w_in, ffn1_w_out, mix_norm, mem_w_kv, hgrn_w_in, hgrn_gnorm, hgrn_w_out, gmlp_w_in, gmlp_ln_g, gmlp_ln_b, gmlp_w_s, gmlp_b_s, gmlp_w_out, ffn2_norm, ffn2_w_in, ffn2_w_out, final_norm, loss_target, m_mem_norm, m_lb_logits, m_ffn1_norm, m_ffn1_w_in, m_ffn1_w_out, m_mix_norm, m_mem_w_kv, m_hgrn_w_in, m_hgrn_gnorm, m_hgrn_w_out, m_gmlp_w_in, m_gmlp_ln_g, m_gmlp_ln_b, m_gmlp_w_s, m_gmlp_b_s, m_gmlp_w_out, m_ffn2_norm, m_ffn2_w_in, m_ffn2_w_out, m_final_norm, v_mem_norm, v_lb_logits, v_ffn1_norm, v_ffn1_w_in, v_ffn1_w_out, v_mix_norm, v_mem_w_kv, v_hgrn_w_in, v_hgrn_gnorm, v_hgrn_w_out, v_gmlp_w_in, v_gmlp_ln_g, v_gmlp_ln_b, v_gmlp_w_s, v_gmlp_b_s, v_gmlp_w_out, v_ffn2_norm, v_ffn2_w_in, v_ffn2_w_out, v_final_norm):
    bl, seq, D = x.shape
    T = bl * seq
    mem_len = mem.shape[1]
    chip = 2 * lax.axis_index("x") + lax.axis_index("y")
    c_arr = lax.axis_index("c").astype(jnp.int32).reshape(1)
    TR = 1024

    big = [("ffn1_w_in", ffn1_w_in, "col"), ("ffn1_w_out", ffn1_w_out, "row"), ("mem_w_kv", mem_w_kv, "col"),
           ("hgrn_w_in", hgrn_w_in, "col"), ("hgrn_w_out", hgrn_w_out, "row"), ("gmlp_w_in", gmlp_w_in, "col"),
           ("gmlp_w_out", gmlp_w_out, "row"), ("ffn2_w_in", ffn2_w_in, "col"), ("ffn2_w_out", ffn2_w_out, "row")]
    kinds = [k for (_, _, k) in big]
    shards_bf = []
    for nm, w, _ in big:
        L, r, c = w.shape
        (wb,) = _rowcall("cast_" + nm, lambda rv, cv: ([rv[0]], []), [(w.reshape(L * r, c), 0, c)], [], [(c, BF)], [], 512)
        shards_bf.append(wb.reshape(L, r, c))
    sb = dict(zip([nm for (nm, _, _) in big], shards_bf))
    groups = [[("ffn1_w_in", 0)], [("ffn1_w_out", 0)], [("hgrn_w_in", None)], [("mem_w_kv", None)], [("hgrn_w_out", None)],
              [("ffn2_w_in", 0), ("ffn2_w_out", 0), ("gmlp_ln_g", None), ("gmlp_ln_b", None)],
              [("ffn1_w_in", 1), ("ffn1_w_out", 1)],
              [("gmlp_w_in", None), ("gmlp_w_out", None)],
              [("ffn2_w_in", 1), ("ffn2_w_out", 1)]]
    kind_of = {nm: k for (nm, _, k) in big}
    for nm, vec in (("gmlp_ln_g", gmlp_ln_g), ("gmlp_ln_b", gmlp_ln_b)):
        sb[nm] = vec.reshape(1, 1, -1)
        kind_of[nm] = "vec"
    gathered = {nm: [None, None] for nm in ("ffn1_w_in", "ffn1_w_out", "ffn2_w_in", "ffn2_w_out")}
    for gi, grp in enumerate(groups):
        outs = _allgather_seq("gather_%d" % gi, [(sb[nm], kind_of[nm], l) for (nm, l) in grp], gi)
        for (nm, l), o in zip(grp, outs):
            if l is None:
                gathered[nm] = o
            else:
                gathered[nm][l] = o

    ln_w = GM_GROUPS * GM_GROUP_DIM
    ln_g_full, ln_b_full = gathered["gmlp_ln_g"].reshape(1, ln_w), gathered["gmlp_ln_b"].reshape(1, ln_w)

    def rms_fwd(name, xin, g):
        (h,) = _rowcall(name, lambda rv, cv: ([_rmsnorm(rv[0], cv[0])], []), [(xin, 0, D)], [g.reshape(1, D)], [(D, BF)], [], TR)
        return h

    def ffn_fwd(tag, xin, h, w_in, w_out, layer, next_gain):
        dff = w_out[layer].shape[1]
        zg, zu, a = _ffn_in_swiglu("ffn_in_" + tag, h, w_in[layer], 1024, dff // 2)
        out = _mm("ffn_out_" + tag, a, w_out[layer], "nn", F32, 1024, 1024, dff, scale=0.5, res=xin, b_lead=0,
                  norm_gain=None if next_gain is None else next_gain.reshape(1, D))
        xo, h_next = (out, None) if next_gain is None else out
        return xo, h_next, (xin, h, zg, zu, a)

    def ffn_bwd(tag, dxo, saved, g, w_in, w_out, layer):
        xin, h, zg, zu, a = saved
        dff = w_out[layer].shape[1]
        dw_out = _mm_tn_pair("ffn_dwo_" + tag, a, dxo, "row", c_arr, dff // 2, T, scale=0.5)
        dz = _ffn_da_swiglu("ffn_da_" + tag, dxo, w_out[layer], zg, zu, 512)
        dw_in = _mm_tn_pair("ffn_dwi_" + tag, h, dz, "col", c_arr, 512, T)
        dx, dg = _mm_dh_rms("ffn_dh_" + tag, dz, w_in[layer], xin, g.reshape(1, D), dxo, 512)
        return dx, dg, dw_in, dw_out

    def rms_bwd(name, xin, g, dh, dres):
        def fn(rv, cv):
            _, vjp = jax.vjp(_rmsnorm, rv[0], cv[0])
            dx, dg = vjp(rv[1])
            if dres is not None:
                dx = dx + rv[2]
            return [dx], [dg]

        rows = [(xin, 0, D), (dh, 0, D)] + ([(dres, 0, D)] if dres is not None else [])
        dx, dg = _rowcall(name, fn, rows, [g.reshape(1, D)], [(D, F32)], [((1, D), F32)], TR)
        return dx, dg

    x0 = x.reshape(T, D)
    tgt = loss_target.reshape(T, D)
    mem2 = mem.reshape(bl * mem_len, D)
    memn = rms_fwd("rms_mem", mem2, mem_norm)

    h_f10 = rms_fwd("rms_f1l0", x0, ffn1_norm[0])
    x1, h_m0, sv_f10 = ffn_fwd("f1l0", x0, h_f10, gathered["ffn1_w_in"], gathered["ffn1_w_out"], 0, mix_norm[0])
    z_m0 = _mm("mix_in_0", h_m0, gathered["hgrn_w_in"], "nn", F32, 2048, 512, D, b_lead=0)
    kv = [_mm("kv_%d" % i, memn, gathered["mem_w_kv"], "nn", F32, 512, 512, D, b_lead=i) for i in range(2)]
    cat0, stash0 = _hgrn_fwd2(z_m0, lb_logits, hgrn_gnorm, kv[0], bl, seq)
    x2, h_f20 = _mm("mix_out_0", cat0, gathered["hgrn_w_out"], "nn", F32, 1024, 1024, cat0.shape[1], res=x1, b_lead=0,
                    norm_gain=ffn2_norm[0].reshape(1, D))
    x3, h_f11, sv_f20 = ffn_fwd("f2l0", x2, h_f20, gathered["ffn2_w_in"], gathered["ffn2_w_out"], 0, ffn1_norm[1])
    x4, h_m1, sv_f11 = ffn_fwd("f1l1", x3, h_f11, gathered["ffn1_w_in"], gathered["ffn1_w_out"], 1, mix_norm[1])
    z_m1 = _mm("mix_in_1", h_m1, gathered["gmlp_w_in"], "nn", F32, 2048, 512, D, b_lead=0)
    nc1 = seq // GM_CHUNK
    w_s, b_s = gmlp_w_s[0], gmlp_b_s[0]
    cat1 = _gmlp_fwd(z_m1, ln_g_full, ln_b_full, w_s, b_s, kv[1], bl, nc1)
    x5, h_f21 = _mm("mix_out_1", cat1, gathered["gmlp_w_out"], "nn", F32, 1024, 1024, cat1.shape[1], res=x4, b_lead=0,
                    norm_gain=ffn2_norm[1].reshape(1, D))
    x6, _, sv_f21 = ffn_fwd("f2l1", x5, h_f21, gathered["ffn2_w_in"], gathered["ffn2_w_out"], 1, None)

    def head(rv, cv):
        def f(xx, gg):
            err = _rmsnorm(xx, gg) - rv[1]
            return 0.5 * jnp.sum(jnp.mean(err * err, axis=-1, keepdims=True), axis=0, keepdims=True)

        ls, vjp = jax.vjp(f, rv[0], cv[0])
        dx, dg = vjp(jnp.ones((1, 1), F32))
        return [dx], [dg, jnp.broadcast_to(ls, (1, 128))]

    dx6, d_final, loss_part = _rowcall("loss_head", head, [(x6, 0, D), (tgt, 0, D)], [final_norm.reshape(1, D)],
                                       [(D, F32)], [((1, D), F32), ((1, 128), F32)], TR)

    rs_out = {}
    n_gather = len(groups)

    def rs(gi, items):
        outs = _rs_chips_seq("reduce_%d" % gi, [p for (_, p, _) in items], [k for (_, _, k) in items], n_gather + gi)
        for i, (key, _, _) in enumerate(items):
            rs_out[key] = (outs[2 * i], outs[2 * i + 1])

    dx5, dg_f21, dwi_f21, dwo_f21 = ffn_bwd("f2l1", dx6, sv_f21, ffn2_norm[1], gathered["ffn2_w_in"], gathered["ffn2_w_out"], 1)
    rs(0, [(("ffn2_w_out", 1), dwo_f21, "row"), (("ffn2_w_in", 1), dwi_f21, "col")])
    dcat1 = _mm("mix_dcat_1", dx5, gathered["gmlp_w_out"], "nt", F32, 2048, 1024, D, b_lead=0)
    dwo_m1 = _mm_tn_pair("mix_dwo_1", cat1, dx5, "row", c_arr, 1024, T)
    dz_m1, dkv1, d_lng, d_lnb, d_ws, d_bs = _gmlp_bwd(z_m1, dcat1, ln_g_full, ln_b_full, w_s, b_s, kv[1], bl, nc1)
    dx4, dg_m1 = _mm_dh_rms("mix_dh_1", dz_m1, gathered["gmlp_w_in"], x4, mix_norm[1].reshape(1, D), dx5, 512)
    dwi_m1 = _mm_tn_pair("mix_dwi_1", h_m1, dz_m1, "col", c_arr, 1024, T)
    rs(1, [(("gmlp_w_out", 0), dwo_m1, "row"), (("gmlp_w_in", 0), dwi_m1, "col")])
    dx3, dg_f11, dwi_f11, dwo_f11 = ffn_bwd("f1l1", dx4, sv_f11, ffn1_norm[1], gathered["ffn1_w_in"], gathered["ffn1_w_out"], 1)
    rs(2, [(("ffn1_w_out", 1), dwo_f11, "row"), (("ffn1_w_in", 1), dwi_f11, "col")])

    dx2, dg_f20, dwi_f20, dwo_f20 = ffn_bwd("f2l0", dx3, sv_f20, ffn2_norm[0], gathered["ffn2_w_in"], gathered["ffn2_w_out"], 0)
    rs(3, [(("ffn2_w_out", 0), dwo_f20, "row"), (("ffn2_w_in", 0), dwi_f20, "col")])
    dcat0 = _mm("mix_dcat_0", dx2, gathered["hgrn_w_out"], "nt", F32, 2048, 1024, D, b_lead=0)
    dwo_m0 = _mm_tn_pair("mix_dwo_0", cat0, dx2, "row", c_arr, 1024, T)
    dz_m0, dkv0, d_lb, d_gn = _hgrn_bwd2(z_m0, dcat0, stash0, lb_logits, hgrn_gnorm, kv[0], bl, seq)
    dx1, dg_m0 = _mm_dh_rms("mix_dh_0", dz_m0, gathered["hgrn_w_in"], x1, mix_norm[0].reshape(1, D), dx2, 512)
    dwi_m0 = _mm_tn_pair("mix_dwi_0", h_m0, dz_m0, "col", c_arr, 1024, T)
    rs(4, [(("hgrn_w_out", 0), dwo_m0, "row"), (("hgrn_w_in", 0), dwi_m0, "col")])

    dwkv = [_mm_tn_pair("kv_dw_%d" % i, memn, dkv, "col", c_arr, 1024, 512) for i, dkv in enumerate([dkv0, dkv1])]
    rs(5, [(("mem_w_kv", 0), dwkv[0], "col"), (("mem_w_kv", 1), dwkv[1], "col")])
    dmemn = _mm("kv_dx_0", dkv0, gathered["mem_w_kv"], "nt", F32, 512, 512, 1024, b_lead=0)
    dmemn = _mm("kv_dx_1", dkv1, gathered["mem_w_kv"], "nt", F32, 512, 512, 1024, res=dmemn, b_lead=1)
    _, d_memnorm = rms_bwd("rms_bwd_mem", mem2, mem_norm, dmemn, None)

    dx0, dg_f10, dwi_f10, dwo_f10 = ffn_bwd("f1l0", dx1, sv_f10, ffn1_norm[0], gathered["ffn1_w_in"], gathered["ffn1_w_out"], 0)
    rs(6, [(("ffn1_w_out", 0), dwo_f10, "row")])
    rs(7, [(("ffn1_w_in", 0), dwi_f10, "col")])

    shard_grads = []
    for (nm, w, k) in big:
        per_layer = []
        for l in range(w.shape[0]):
            own, got = rs_out[(nm, l)]
            per_layer.append(_finish_share("finish_%s_%d" % (nm, l), own[None], got[:, None], k, c_arr))
        shard_grads.append(per_layer[0] if len(per_layer) == 1 else jnp.concatenate(per_layer, axis=0))

    big_w = [w for (_, w, _) in big]
    big_m = [m_ffn1_w_in, m_ffn1_w_out, m_mem_w_kv, m_hgrn_w_in, m_hgrn_w_out, m_gmlp_w_in, m_gmlp_w_out, m_ffn2_w_in, m_ffn2_w_out]
    big_v = [v_ffn1_w_in, v_ffn1_w_out, v_mem_w_kv, v_hgrn_w_in, v_hgrn_w_out, v_gmlp_w_in, v_gmlp_w_out, v_ffn2_w_in, v_ffn2_w_out]
    big_out = {}
    for (nm, w, _), g, m, v in zip(big, shard_grads, big_m, big_v):
        L, r, c = w.shape
        d2, m2, v2 = _adam_call("adam_" + nm, w.reshape(L * r, c), g.reshape(L * r, c), m.reshape(L * r, c),
                                v.reshape(L * r, c), 256)
        big_out[nm] = (g, d2.reshape(w.shape), m2.reshape(w.shape), v2.reshape(w.shape))

    d_ffn1n = _two_rows(dg_f10, dg_f11)
    d_mixn = _two_rows(dg_m0, dg_m1)
    d_ffn2n = _two_rows(dg_f20, dg_f21)
    small_parts = [loss_part[:, :1], d_memnorm, d_lb, d_ffn1n, d_mixn, d_gn, d_lng, d_lnb, d_ws, d_bs, d_ffn2n, d_final]
    red_shapes = [(1,), mem_norm.shape, lb_logits.shape, ffn1_norm.shape, mix_norm.shape, hgrn_gnorm.shape, (1, ln_w), (1, ln_w),
                  gmlp_w_s.shape, gmlp_b_s.shape, ffn2_norm.shape, final_norm.shape]
    red = _small_allreduce(_pack(small_parts, _rows_needed(red_shapes)), "reduce_small")
    (loss_v, g_memn, g_lb, g_f1n, g_mixn, g_gn, g_lng_full, g_lnb_full, g_ws, g_bs, g_f2n, g_fin) = _unpack(red, red_shapes)
    lsh = gmlp_ln_g.shape[1]
    g_lng = lax.dynamic_slice(g_lng_full, (0, chip * lsh), (1, lsh))
    g_lnb = lax.dynamic_slice(g_lnb_full, (0, chip * lsh), (1, lsh))
    small_w = [mem_norm, lb_logits, ffn1_norm, mix_norm, hgrn_gnorm, gmlp_ln_g, gmlp_ln_b, gmlp_w_s, gmlp_b_s, ffn2_norm, final_norm]
    small_g = [g_memn, g_lb, g_f1n, g_mixn, g_gn, g_lng, g_lnb, g_ws, g_bs, g_f2n, g_fin]
    small_m = [m_mem_norm, m_lb_logits, m_ffn1_norm, m_mix_norm, m_hgrn_gnorm, m_gmlp_ln_g, m_gmlp_ln_b, m_gmlp_w_s, m_gmlp_b_s, m_ffn2_norm, m_final_norm]
    small_v = [v_mem_norm, v_lb_logits, v_ffn1_norm, v_mix_norm, v_hgrn_gnorm, v_gmlp_ln_g, v_gmlp_ln_b, v_gmlp_w_s, v_gmlp_b_s, v_ffn2_norm, v_final_norm]
    sshapes = [w.shape for w in small_w]
    nrow = _rows_needed(sshapes)
    d_p, m_p, v_p = _adam_call("adam_small", _pack(small_w, nrow), _pack(small_g, nrow), _pack(small_m, nrow), _pack(small_v, nrow), nrow)
    s_delta, s_m, s_v = _unpack(d_p, sshapes), _unpack(m_p, sshapes), _unpack(v_p, sshapes)
    small_names = ["mem_norm", "lb_logits", "ffn1_norm", "mix_norm", "hgrn_gnorm", "gmlp_ln_g", "gmlp_ln_b", "gmlp_w_s", "gmlp_b_s", "ffn2_norm", "final_norm"]
    small_out = {nm: (g.reshape(w.shape), d, m, v) for nm, w, g, d, m, v in zip(small_names, small_w, small_g, s_delta, s_m, s_v)}

    order = ["mem_norm", "lb_logits", "ffn1_norm", "ffn1_w_in", "ffn1_w_out", "mix_norm", "mem_w_kv", "hgrn_w_in", "hgrn_gnorm",
             "hgrn_w_out", "gmlp_w_in", "gmlp_ln_g", "gmlp_ln_b", "gmlp_w_s", "gmlp_b_s", "gmlp_w_out", "ffn2_norm", "ffn2_w_in",
             "ffn2_w_out", "final_norm"]
    allo = {**big_out, **small_out}
    grad_x = dx0.reshape(x.shape)
    return (loss_v.reshape(()), grad_x, *[allo[n][0] for n in order], *[allo[n][1] for n in order],
            *[allo[n][2] for n in order], *[allo[n][3] for n in order])
```

```python
import functools

import jax
import jax.numpy as jnp
from jax import lax
from jax.experimental import pallas as pl
from jax.experimental.pallas import tpu as pltpu
from jax.experimental.pallas import tpu_sc as plsc

BF = jnp.bfloat16
F32 = jnp.float32
MESH = pl.DeviceIdType.MESH

EPS = 1e-6
D_MODEL = 1024
HG_HEADS = 8
HG_DIM = 128
HG_CHUNK = 64
GM_CHUNK = 128
GM_GROUPS = 8
GM_GROUP_DIM = 256
XA_HEADS = 4
XA_DIM = 256
ADAM_LR = 0.001
ADAM_B1 = 0.9
ADAM_B2 = 0.999
ADAM_EPS = 1e-08
ADAM_WD = 0.01
ADAM_STEP = 10

VMEM_CAP_BYTES = 60 * 1024 * 1024
LANES = 1024


def _pick(n, cap, mult=16):
    if n <= cap:
        return n
    for d in range(cap - cap % mult, 0, -mult):
        if n % d == 0:
            return d
    raise ValueError((n, cap, mult))


def _dg(a, b, ca, cb):
    return lax.dot_general(a.astype(BF), b.astype(BF), (((ca,), (cb,)), ((), ())), preferred_element_type=F32)


@jax.custom_vjp
def dot_nn(a, b):
    return _dg(a, b, 1, 0)


def _nn_fwd(a, b):
    return _dg(a, b, 1, 0), (a, b)


def _nn_bwd(r, g):
    a, b = r
    return _dg(g, b, 1, 1), _dg(a, g, 0, 0)


dot_nn.defvjp(_nn_fwd, _nn_bwd)


@jax.custom_vjp
def dot_nt(a, b):
    return _dg(a, b, 1, 1)


def _nt_fwd(a, b):
    return _dg(a, b, 1, 1), (a, b)


def _nt_bwd(r, g):
    a, b = r
    return _dg(g, b, 1, 0), _dg(g, a, 0, 0)


dot_nt.defvjp(_nt_fwd, _nt_bwd)


@jax.custom_vjp
def dot_tn(a, b):
    return _dg(a, b, 0, 0)


def _tn_fwd(a, b):
    return _dg(a, b, 0, 0), (a, b)


def _tn_bwd(r, g):
    a, b = r
    return _dg(b, g, 1, 1), _dg(a, g, 1, 0)


dot_tn.defvjp(_tn_fwd, _tn_bwd)


def _rmsnorm(x, g):
    return x * lax.rsqrt(jnp.mean(x * x, axis=-1, keepdims=True) + EPS) * g


def _silu(x):
    return x * jax.nn.sigmoid(x)


@jax.custom_vjp
def _gelu(x):
    return 0.5 * x * (1.0 + lax.erf(x * (0.5 ** 0.5)))


def _gelu_fwd(x):
    return _gelu(x), x


def _gelu_bwd(x, g):
    t = x * (0.5 ** 0.5)
    cdf = 0.5 * (1.0 + lax.erf(t))
    return (g * (cdf + x * (jnp.exp(-(t * t)) * (0.5 / 3.141592653589793) ** 0.5)),)


_gelu.defvjp(_gelu_fwd, _gelu_bwd)


def _softmax_last(s):
    m = lax.stop_gradient(jnp.max(s, axis=-1, keepdims=True))
    e = jnp.exp(s - m)
    return e / jnp.sum(e, axis=-1, keepdims=True)


def _tril(n):
    r = lax.broadcasted_iota(jnp.int32, (n, n), 0)
    c = lax.broadcasted_iota(jnp.int32, (n, n), 1)
    return r >= c


def _cumsum_rows(l):
    n = l.shape[0]
    return lax.dot_general(_tril(n).astype(F32), l, (((1,), (0,)), ((), ())),
                           precision=lax.Precision.HIGHEST, preferred_element_type=F32)


def _attention(zx, mk, mv):
    s = dot_nt(zx, mk) * (XA_DIM ** -0.5)
    return dot_nn(_softmax_last(s), mv)


def _hgrn_head(zq, zf, zi, zg, l0, l1, l2, gn, S):
    m = lax.stop_gradient(jnp.maximum(jnp.maximum(l0, l1), l2))
    e0 = jnp.exp(l0 - m)
    lb = e0 / (e0 + jnp.exp(l1 - m) + jnp.exp(l2 - m))
    q = _silu(zq)
    f = lb + (1.0 - lb) * jax.nn.sigmoid(zf)
    k = 1.0 - f
    b = _cumsum_rows(jnp.log(f))
    b_last = b[HG_CHUNK - 1:HG_CHUNK, :]
    q_dec = q * jnp.exp(b)
    k_inv = k * jnp.exp(-b)
    a = jnp.where(_tril(HG_CHUNK), dot_nt(q_dec, k_inv), 0.0)
    o = dot_nn(a, zi) + dot_nn(q_dec, S)
    S_new = jnp.exp(b_last).reshape(HG_DIM, 1) * S + dot_tn(k * jnp.exp(b_last - b), zi)
    o = _rmsnorm(o, gn) * _silu(zg)
    return o, S_new


def _hgrn_block(zq, zf, zi, zg, zx, l0, l1, l2, gn, mk, mv, S):
    outs, s_new = [], []
    for h in range(HG_HEADS):
        o, sn = _hgrn_head(zq[h], zf[h], zi[h], zg[h], l0[h], l1[h], l2[h], gn, S[h])
        outs.append(o)
        s_new.append(sn)
    for a in range(XA_HEADS):
        outs.append(_attention(zx[a], mk[a], mv[a]))
    return outs, s_new


def _gmlp_block(zu, zv, zx, lng, lnb, ws, bs, mk, mv):
    gv = [_gelu(v) for v in zv]
    width = GM_GROUPS * GM_GROUP_DIM
    mu = sum(jnp.sum(g, axis=-1, keepdims=True) for g in gv) / width
    xc = [g - mu for g in gv]
    var = sum(jnp.sum(c * c, axis=-1, keepdims=True) for c in xc) / width
    r = lax.rsqrt(var + EPS)
    outs = []
    for g in range(GM_GROUPS):
        v = xc[g] * r * lng[g] + lnb[g]
        w = jnp.where(_tril(GM_CHUNK), ws[g], 0.0)
        mixed = dot_nn(w, v) + bs[g].reshape(GM_CHUNK, 1)
        outs.append(_gelu(zu[g]) * mixed)
    for a in range(XA_HEADS):
        outs.append(_attention(zx[a], mk[a], mv[a]))
    return outs


def _rowcall(name, fn, rows, consts, row_outs, acc_outs, tr):
    nrows = rows[0][0].shape[0]
    tr = _pick(nrows, tr)
    n_r, n_c, n_ro, n_ao = len(rows), len(consts), len(row_outs), len(acc_outs)

    def kern(*refs):
        rv = [r[...] for r in refs[:n_r]]
        cv = [r[...] for r in refs[n_r:n_r + n_c]]
        ro_refs = refs[n_r + n_c:n_r + n_c + n_ro]
        ao_refs = refs[n_r + n_c + n_ro:]
        ro, ao = fn(rv, cv)
        for ref, v in zip(ro_refs, ro):
            ref[...] = v.astype(ref.dtype)
        if n_ao:
            @pl.when(pl.program_id(0) == 0)
            def _():
                for ref in ao_refs:
                    ref[...] = jnp.zeros(ref.shape, ref.dtype)

            for ref, v in zip(ao_refs, ao):
                ref[...] += v.astype(ref.dtype)

    in_specs = [pl.BlockSpec((tr, w), functools.partial(lambda i, cb: (i, cb), cb=cb)) for (_, cb, w) in rows]
    in_specs += [pl.BlockSpec(c.shape, lambda i: (0, 0)) for c in consts]
    out_specs = [pl.BlockSpec((tr, w), lambda i: (i, 0)) for (w, _) in row_outs]
    out_specs += [pl.BlockSpec(s, lambda i: (0, 0)) for (s, _) in acc_outs]
    out_shape = [jax.ShapeDtypeStruct((nrows, w), dt) for (w, dt) in row_outs]
    out_shape += [jax.ShapeDtypeStruct(s, dt) for (s, dt) in acc_outs]
    est = sum(tr * w * a.dtype.itemsize for (a, _, w) in rows) + sum(tr * w * jnp.dtype(dt).itemsize for (w, dt) in row_outs)
    est += sum(c.size * c.dtype.itemsize for c in consts)
    outs = pl.pallas_call(
        kern, grid=(nrows // tr,), in_specs=in_specs, out_specs=out_specs, out_shape=out_shape, name=name,
        compiler_params=pltpu.CompilerParams(dimension_semantics=("arbitrary",),
                                             vmem_limit_bytes=VMEM_CAP_BYTES),
    )(*[a for (a, _, _) in rows], *consts)
    return outs


def _mm(name, a, b, mode, out_dtype, tm, tn, tk, scale=1.0, res=None, a_lead=None, b_lead=None, norm_gain=None):
    ash = a.shape[-2:]
    bsh = b.shape[-2:]
    if mode == "nn":
        (M, K), (K2, N) = ash, bsh
    elif mode == "nt":
        (M, K), (N, K2) = ash, bsh
    else:
        (K, M), (K2, N) = ash, bsh
    assert K == K2, (name, a.shape, b.shape)
    tm, tn, tk = min(tm, M), min(tn, N), min(tk, K)
    assert M % tm == 0 and N % tn == 0 and K % tk == 0, (name, M, N, K, tm, tn, tk)
    nk = K // tk
    dims = {"nn": (1, 0), "nt": (1, 1), "tn": (0, 0)}[mode]

    def lead(spec_shape, index_fn, lead_idx):
        if lead_idx is None:
            return pl.BlockSpec(spec_shape, index_fn)
        return pl.BlockSpec((None,) + spec_shape, lambda i, j, k: (lead_idx,) + index_fn(i, j, k))

    if mode == "tn":
        a_spec = lead((tk, tm), lambda i, j, k: (k, i), a_lead)
    else:
        a_spec = lead((tm, tk), lambda i, j, k: (i, k), a_lead)
    if mode == "nt":
        b_spec = lead((tn, tk), lambda i, j, k: (j, k), b_lead)
    else:
        b_spec = lead((tk, tn), lambda i, j, k: (k, j), b_lead)
    o_spec = pl.BlockSpec((tm, tn), lambda i, j, k: (i, j))
    has_res = res is not None
    has_norm = norm_gain is not None
    assert not has_norm or tn == N

    def kern(*refs):
        a_ref, b_ref = refs[0], refs[1]
        pos = 2
        res_ref = gain_ref = h_ref = None
        if has_res:
            res_ref, pos = refs[pos], pos + 1
        if has_norm:
            gain_ref, pos = refs[pos], pos + 1
        o_ref, pos = refs[pos], pos + 1
        if has_norm:
            h_ref = refs[pos]
        acc_ref = refs[-1] if nk > 1 else None
        p = lax.dot_general(a_ref[...].astype(BF), b_ref[...].astype(BF), (((dims[0],), (dims[1],)), ((), ())),
                            preferred_element_type=F32)

        def finish(v):
            if scale != 1.0:
                v = v * scale
            if has_res:
                v = res_ref[...] + v
            o_ref[...] = v.astype(o_ref.dtype)
            if has_norm:
                h_ref[...] = _rmsnorm(v, gain_ref[...]).astype(h_ref.dtype)

        if nk == 1:
            finish(p)
        else:
            k = pl.program_id(2)

            @pl.when(k == 0)
            def _():
                acc_ref[...] = p

            @pl.when(k > 0)
            def _():
                acc_ref[...] += p

            @pl.when(k == nk - 1)
            def _():
                finish(acc_ref[...])

    ins = [a, b] + ([res] if has_res else []) + ([norm_gain] if has_norm else [])
    in_specs = [a_spec, b_spec] + ([o_spec] if has_res else [])
    in_specs += [pl.BlockSpec((1, N), lambda i, j, k: (0, 0))] if has_norm else []
    out_sd = jax.ShapeDtypeStruct((M, N), out_dtype)
    return pl.pallas_call(
        kern, grid=(M // tm, N // tn, nk), in_specs=in_specs,
        out_specs=[o_spec, o_spec] if has_norm else o_spec,
        out_shape=[out_sd, jax.ShapeDtypeStruct((M, N), BF)] if has_norm else out_sd,
        scratch_shapes=[pltpu.VMEM((tm, tn), F32)] if nk > 1 else [],
        name=name,
        compiler_params=pltpu.CompilerParams(dimension_semantics=("parallel", "parallel", "arbitrary"),
                                             vmem_limit_bytes=VMEM_CAP_BYTES),
    )(*ins)


def _ffn_in_swiglu(name, h, w3, tm, tn):
    T, D = h.shape
    dff = w3.shape[2] // 2
    tm = min(tm, T)
    assert T % tm == 0 and dff % tn == 0
    nj = dff // tn

    def kern(h_ref, wg_ref, wu_ref, zg_ref, zu_ref, a_ref):
        hb = h_ref[...]
        g = jnp.dot(hb, wg_ref[...], preferred_element_type=F32).astype(BF)
        u = jnp.dot(hb, wu_ref[...], preferred_element_type=F32).astype(BF)
        zg_ref[...] = g
        zu_ref[...] = u
        a_ref[...] = (_silu(g.astype(F32)) * u.astype(F32)).astype(BF)

    o_spec = pl.BlockSpec((tm, tn), lambda i, j: (i, j))
    return pl.pallas_call(
        kern, grid=(T // tm, nj),
        in_specs=[pl.BlockSpec((tm, D), lambda i, j: (i, 0)),
                  pl.BlockSpec((None, D, tn), lambda i, j: (0, 0, j)),
                  pl.BlockSpec((None, D, tn), lambda i, j: (0, 0, j + nj))],
        out_specs=[o_spec, o_spec, o_spec],
        out_shape=[jax.ShapeDtypeStruct((T, dff), BF)] * 3, name=name,
        compiler_params=pltpu.CompilerParams(dimension_semantics=("parallel", "arbitrary"),
                                             vmem_limit_bytes=VMEM_CAP_BYTES),
    )(h, w3, w3)


def _ffn_da_swiglu(name, dxo, w3, zg, zu, tm):
    T, D = dxo.shape
    dff = w3.shape[1]
    tm = min(tm, T)
    assert T % tm == 0 and dff % 2 == 0
    hc = dff // 2

    def kern(d_ref, w_ref, g_ref, u_ref, dz_ref):
        db = (d_ref[...] * 0.5).astype(BF)
        for s in range(2):
            cols = slice(s * hc, (s + 1) * hc)
            da = lax.dot_general(db, w_ref[cols, :], (((1,), (1,)), ((), ())), preferred_element_type=F32)
            g = g_ref[:, cols].astype(F32)
            sg = 1.0 / (1.0 + jnp.exp(-g))
            gs = g * sg
            dab = da.astype(BF)
            dz_ref[:, cols] = (dab * u_ref[:, cols]) * (sg + gs * (1.0 - sg)).astype(BF)
            dz_ref[:, dff + s * hc:dff + (s + 1) * hc] = dab * gs.astype(BF)

    row = lambda w: pl.BlockSpec((tm, w), lambda i: (i, 0))
    return pl.pallas_call(
        kern, grid=(T // tm,),
        in_specs=[row(D), pl.BlockSpec((None, dff, D), lambda i: (0, 0, 0), pipeline_mode=pl.Buffered(1)), row(dff), row(dff)],
        out_specs=row(2 * dff), out_shape=jax.ShapeDtypeStruct((T, 2 * dff), BF), name=name,
        compiler_params=pltpu.CompilerParams(dimension_semantics=("arbitrary",), vmem_limit_bytes=VMEM_CAP_BYTES),
    )(dxo, w3, zg, zu)


def _mm_dh_rms(name, dz, w3, xin, g, dres, tm):
    T, K = dz.shape
    D = w3.shape[1]
    tm = min(tm, T)
    assert T % tm == 0

    def kern(dz_ref, w_ref, x_ref, g_ref, r_ref, dx_ref, dg_ref):
        dh = lax.dot_general(dz_ref[...], w_ref[...], (((1,), (1,)), ((), ())), preferred_element_type=F32)
        _, vjp = jax.vjp(_rmsnorm, x_ref[...], g_ref[...])
        dx, dg = vjp(dh)
        dx_ref[...] = dx + r_ref[...]

        @pl.when(pl.program_id(0) == 0)
        def _():
            dg_ref[...] = jnp.zeros(dg_ref.shape, F32)

        dg_ref[...] += dg

    row = lambda w: pl.BlockSpec((tm, w), lambda i: (i, 0))
    one = pl.BlockSpec((1, D), lambda i: (0, 0))
    return pl.pallas_call(
        kern, grid=(T // tm,),
        in_specs=[row(K), pl.BlockSpec((None, D, K), lambda i: (0, 0, 0), pipeline_mode=pl.Buffered(1)), row(D), one, row(D)],
        out_specs=[row(D), one], out_shape=[jax.ShapeDtypeStruct((T, D), F32), jax.ShapeDtypeStruct((1, D), F32)], name=name,
        compiler_params=pltpu.CompilerParams(dimension_semantics=("arbitrary",), vmem_limit_bytes=VMEM_CAP_BYTES),
    )(dz, w3, xin, g, dres)


def _mm_tn_pair(name, a, b, kind, c_arr, tq, tk, scale=1.0):
    T, M = a.shape
    _, N = b.shape
    tk = min(tk, T)
    assert T % tk == 0
    nk = T // tk
    if kind == "col":
        hm = M // 2
        assert N % tq == 0
        nq = N // tq
        tile = (hm, tq)
        a_spec = pl.BlockSpec((tk, hm), lambda h, q, k, c: (k, jnp.bitwise_xor(h, 1 - c[0])))
        b_spec = pl.BlockSpec((tk, tq), lambda h, q, k, c: (k, q))
        o_spec = pl.BlockSpec(tile, lambda h, q, k, c: (0, q * h))
        out_sd = (hm, N)
    else:
        hn = N // 2
        assert M % tq == 0
        nq = M // tq
        tile = (tq, hn)
        a_spec = pl.BlockSpec((tk, tq), lambda h, q, k, c: (k, q))
        b_spec = pl.BlockSpec((tk, hn), lambda h, q, k, c: (k, jnp.bitwise_xor(h, 1 - c[0])))
        o_spec = pl.BlockSpec(tile, lambda h, q, k, c: (q * h, 0))
        out_sd = (M, hn)

    def kern(c_ref, a_ref, b_ref, o_ref, acc, stage, recv, ssem, rsem):
        h, q, k = pl.program_id(0), pl.program_id(1), pl.program_id(2)
        x, y, c, _ = _place()
        p = lax.dot_general(a_ref[...].astype(BF), b_ref[...].astype(BF), (((0,), (0,)), ((), ())), preferred_element_type=F32)

        @pl.when(k == 0)
        def _():
            acc[...] = p

        @pl.when(k > 0)
        def _():
            acc[...] += p

        def send(slot, qq):
            return pltpu.make_async_remote_copy(src_ref=stage.at[slot], dst_ref=recv.at[qq], send_sem=ssem.at[slot],
                                                recv_sem=rsem.at[qq], device_id=(x, y, 1 - c), device_id_type=MESH)

        last = k == nk - 1

        @pl.when(jnp.logical_and(last, h == 0))
        def _():
            slot = q % 2

            @pl.when(q >= 2)
            def _():
                send(slot, q).wait_send()

            stage[slot] = (acc[...] * scale).astype(BF)
            send(slot, q).start()

        @pl.when(jnp.logical_and(last, h == 1))
        def _():
            @pl.when(q == 0)
            def _():
                for s in range(min(nq, 2)):
                    send(s, 0).wait_send()

            send(0, q).wait_recv()
            o_ref[...] = (acc[...] * scale + recv[q].astype(F32)).astype(o_ref.dtype)

    tb = tile[0] * tile[1]
    est = tb * (4 + 2 * 2 + nq * 2 + 2 * 2) + 2 * tk * (a_spec.block_shape[1] + b_spec.block_shape[1]) * 2 * 2
    return pl.pallas_call(
        kern,
        grid_spec=pltpu.PrefetchScalarGridSpec(
            num_scalar_prefetch=1, grid=(2, nq, nk), in_specs=[a_spec, b_spec], out_specs=o_spec,
            scratch_shapes=[pltpu.VMEM(tile, F32), pltpu.VMEM((2,) + tile, BF), pltpu.VMEM((nq,) + tile, BF),
                            pltpu.SemaphoreType.DMA((2,)), pltpu.SemaphoreType.DMA((nq,))]),
        out_shape=jax.ShapeDtypeStruct(out_sd, BF), name=name,
        compiler_params=pltpu.CompilerParams(dimension_semantics=("arbitrary", "arbitrary", "arbitrary"),
                                             vmem_limit_bytes=VMEM_CAP_BYTES),
    )(c_arr, a, b)


def _hgrn_pieces(z_ref):
    W = HG_HEADS * HG_DIM
    zq = [z_ref[:, h * HG_DIM:(h + 1) * HG_DIM] for h in range(HG_HEADS)]
    zf = [z_ref[:, W + h * HG_DIM:W + (h + 1) * HG_DIM] for h in range(HG_HEADS)]
    zi = [z_ref[:, 2 * W + h * HG_DIM:2 * W + (h + 1) * HG_DIM] for h in range(HG_HEADS)]
    zg = [z_ref[:, 3 * W + h * HG_DIM:3 * W + (h + 1) * HG_DIM] for h in range(HG_HEADS)]
    zx = [z_ref[:, 4 * W + a * XA_DIM:4 * W + (a + 1) * XA_DIM] for a in range(XA_HEADS)]
    return zq, zf, zi, zg, zx


def _kv_pieces(kv_ref):
    W = XA_HEADS * XA_DIM
    mk = [kv_ref[:, a * XA_DIM:(a + 1) * XA_DIM] for a in range(XA_HEADS)]
    mv = [kv_ref[:, W + a * XA_DIM:W + (a + 1) * XA_DIM] for a in range(XA_HEADS)]
    return mk, mv


def _lb_pieces(lb_ref):
    return [[lb_ref[r:r + 1, h * HG_DIM:(h + 1) * HG_DIM] for h in range(HG_HEADS)] for r in range(3)]


def _hgrn_fwd(z, lb_logits, gnorm, kv, bl, nc):
    T, zw = z.shape
    mem_len = kv.shape[0] // bl
    cat_w = HG_HEADS * HG_DIM + XA_HEADS * XA_DIM

    def kern(z_ref, lb_ref, gn_ref, kv_ref, cat_ref, st_ref, s_scr):
        @pl.when(pl.program_id(1) == 0)
        def _():
            s_scr[...] = jnp.zeros(s_scr.shape, F32)

        st_ref[...] = s_scr[...]
        zq, zf, zi, zg, zx = _hgrn_pieces(z_ref)
        mk, mv = _kv_pieces(kv_ref)
        l0, l1, l2 = _lb_pieces(lb_ref)
        S = [s_scr[h] for h in range(HG_HEADS)]
        outs, s_new = _hgrn_block(zq, zf, zi, zg, zx, l0, l1, l2, gn_ref[...], mk, mv, S)
        for h in range(HG_HEADS):
            cat_ref[:, h * HG_DIM:(h + 1) * HG_DIM] = outs[h].astype(cat_ref.dtype)
            s_scr[h] = s_new[h]
        base = HG_HEADS * HG_DIM
        for a in range(XA_HEADS):
            cat_ref[:, base + a * XA_DIM:base + (a + 1) * XA_DIM] = outs[HG_HEADS + a].astype(cat_ref.dtype)

    return pl.pallas_call(
        kern, grid=(bl, nc),
        in_specs=[pl.BlockSpec((HG_CHUNK, zw), lambda b, n: (b * nc + n, 0)),
                  pl.BlockSpec(lb_logits.shape, lambda b, n: (0, 0)),
                  pl.BlockSpec(gnorm.shape, lambda b, n: (0, 0)),
                  pl.BlockSpec((mem_len, kv.shape[1]), lambda b, n: (b, 0))],
        out_specs=[pl.BlockSpec((HG_CHUNK, cat_w), lambda b, n: (b * nc + n, 0)),
                   pl.BlockSpec((None, HG_HEADS, HG_DIM, HG_DIM), lambda b, n: (b * nc + n, 0, 0, 0))],
        out_shape=[jax.ShapeDtypeStruct((T, cat_w), BF),
                   jax.ShapeDtypeStruct((bl * nc, HG_HEADS, HG_DIM, HG_DIM), F32)],
        scratch_shapes=[pltpu.VMEM((HG_HEADS, HG_DIM, HG_DIM), F32)],
        name="hgrn_fwd",
        compiler_params=pltpu.CompilerParams(dimension_semantics=("arbitrary", "arbitrary"), vmem_limit_bytes=VMEM_CAP_BYTES),
    )(z, lb_logits, gnorm, kv)


def _hgrn_bwd(z, dcat, stash, lb_logits, gnorm, kv, bl, nc):
    T, zw = z.shape
    mem_len = kv.shape[0] // bl
    cat_w = dcat.shape[1]

    def kern(z_ref, dc_ref, st_ref, lb_ref, gn_ref, kv_ref, dz_ref, dkv_ref, dlb_ref, dgn_ref, ds_scr):
        first = jnp.logical_and(pl.program_id(0) == 0, pl.program_id(1) == 0)

        @pl.when(pl.program_id(1) == 0)
        def _():
            ds_scr[...] = jnp.zeros(ds_scr.shape, F32)
            dkv_ref[...] = jnp.zeros(dkv_ref.shape, F32)

        @pl.when(first)
        def _():
            dlb_ref[...] = jnp.zeros(dlb_ref.shape, F32)
            dgn_ref[...] = jnp.zeros(dgn_ref.shape, F32)

        zq, zf, zi, zg, zx = _hgrn_pieces(z_ref)
        mk, mv = _kv_pieces(kv_ref)
        l0, l1, l2 = _lb_pieces(lb_ref)
        S = [st_ref[h] for h in range(HG_HEADS)]
        _, vjp = jax.vjp(_hgrn_block, zq, zf, zi, zg, zx, l0, l1, l2, gn_ref[...], mk, mv, S)
        d_outs = [dc_ref[:, h * HG_DIM:(h + 1) * HG_DIM] for h in range(HG_HEADS)]
        base = HG_HEADS * HG_DIM
        d_outs += [dc_ref[:, base + a * XA_DIM:base + (a + 1) * XA_DIM] for a in range(XA_HEADS)]
        d_s = [ds_scr[h] for h in range(HG_HEADS)]
        dzq, dzf, dzi, dzg, dzx, dl0, dl1, dl2, dgn, dmk, dmv, dS = vjp((d_outs, d_s))
        W = HG_HEADS * HG_DIM
        for h in range(HG_HEADS):
            sl = slice(h * HG_DIM, (h + 1) * HG_DIM)
            dz_ref[:, sl] = dzq[h].astype(dz_ref.dtype)
            dz_ref[:, W + h * HG_DIM:W + (h + 1) * HG_DIM] = dzf[h].astype(dz_ref.dtype)
            dz_ref[:, 2 * W + h * HG_DIM:2 * W + (h + 1) * HG_DIM] = dzi[h].astype(dz_ref.dtype)
            dz_ref[:, 3 * W + h * HG_DIM:3 * W + (h + 1) * HG_DIM] = dzg[h].astype(dz_ref.dtype)
            ds_scr[h] = dS[h]
            dlb_ref[0:1, sl] += dl0[h]
            dlb_ref[1:2, sl] += dl1[h]
            dlb_ref[2:3, sl] += dl2[h]
        dgn_ref[...] += dgn
        KW = XA_HEADS * XA_DIM
        for a in range(XA_HEADS):
            dz_ref[:, 4 * W + a * XA_DIM:4 * W + (a + 1) * XA_DIM] = dzx[a].astype(dz_ref.dtype)
            dkv_ref[:, a * XA_DIM:(a + 1) * XA_DIM] += dmk[a]
            dkv_ref[:, KW + a * XA_DIM:KW + (a + 1) * XA_DIM] += dmv[a]

    rev = lambda b, n: (b * nc + (nc - 1 - n), 0)
    return pl.pallas_call(
        kern, grid=(bl, nc),
        in_specs=[pl.BlockSpec((HG_CHUNK, zw), rev),
                  pl.BlockSpec((HG_CHUNK, cat_w), rev),
                  pl.BlockSpec((None, HG_HEADS, HG_DIM, HG_DIM), lambda b, n: (b * nc + (nc - 1 - n), 0, 0, 0)),
                  pl.BlockSpec(lb_logits.shape, lambda b, n: (0, 0)),
                  pl.BlockSpec(gnorm.shape, lambda b, n: (0, 0)),
                  pl.BlockSpec((mem_len, kv.shape[1]), lambda b, n: (b, 0))],
        out_specs=[pl.BlockSpec((HG_CHUNK, zw), rev),
                   pl.BlockSpec((mem_len, kv.shape[1]), lambda b, n: (b, 0)),
                   pl.BlockSpec(lb_logits.shape, lambda b, n: (0, 0)),
                   pl.BlockSpec(gnorm.shape, lambda b, n: (0, 0))],
        out_shape=[jax.ShapeDtypeStruct((T, zw), BF), jax.ShapeDtypeStruct(kv.shape, F32),
                   jax.ShapeDtypeStruct(lb_logits.shape, F32), jax.ShapeDtypeStruct(gnorm.shape, F32)],
        scratch_shapes=[pltpu.VMEM((HG_HEADS, HG_DIM, HG_DIM), F32)],
        name="hgrn_bwd",
        compiler_params=pltpu.CompilerParams(dimension_semantics=("arbitrary", "arbitrary"), vmem_limit_bytes=VMEM_CAP_BYTES),
    )(z, dcat, stash, lb_logits, gnorm, kv)


HG_SUB = 4


def _hgrn_rows(z_ref, dtype_cast=None):
    W = HG_HEADS * HG_DIM

    def piece(c, col, w):
        return z_ref[c * HG_CHUNK:(c + 1) * HG_CHUNK, col:col + w]

    zq = [[piece(c, h * HG_DIM, HG_DIM) for h in range(HG_HEADS)] for c in range(HG_SUB)]
    zf = [[piece(c, W + h * HG_DIM, HG_DIM) for h in range(HG_HEADS)] for c in range(HG_SUB)]
    zi = [[piece(c, 2 * W + h * HG_DIM, HG_DIM) for h in range(HG_HEADS)] for c in range(HG_SUB)]
    zg = [[piece(c, 3 * W + h * HG_DIM, HG_DIM) for h in range(HG_HEADS)] for c in range(HG_SUB)]
    zx = [z_ref[:, 4 * W + a * XA_DIM:4 * W + (a + 1) * XA_DIM] for a in range(XA_HEADS)]
    return zq, zf, zi, zg, zx


def _hgrn_steps(zq, zf, zi, zg, zx, l0, l1, l2, gn, mk, mv, S):
    mix = []
    for c in range(HG_SUB):
        row, s_next = [], []
        for h in range(HG_HEADS):
            o, sn = _hgrn_head(zq[c][h], zf[c][h], zi[c][h], zg[c][h], l0[h], l1[h], l2[h], gn, S[h])
            row.append(o)
            s_next.append(sn)
        mix.append(row)
        S = s_next
    att = [_attention(zx[a], mk[a], mv[a]) for a in range(XA_HEADS)]
    return mix, att, S


def _hgrn_fwd2(z, lb_logits, gnorm, kv, bl, seq):
    T, zw = z.shape
    mem_len = kv.shape[0] // bl
    cat_w = HG_HEADS * HG_DIM + XA_HEADS * XA_DIM
    R = HG_SUB * HG_CHUNK
    nb = seq // R

    def kern(z_ref, lb_ref, gn_ref, kv_ref, cat_ref, st_ref, s_scr):
        @pl.when(pl.program_id(1) == 0)
        def _():
            s_scr[...] = jnp.zeros(s_scr.shape, F32)

        st_ref[...] = s_scr[...]
        zq, zf, zi, zg, zx = _hgrn_rows(z_ref)
        mk, mv = _kv_pieces(kv_ref)
        l0, l1, l2 = _lb_pieces(lb_ref)
        S = [s_scr[h] for h in range(HG_HEADS)]
        mix, att, s_new = _hgrn_steps(zq, zf, zi, zg, zx, l0, l1, l2, gn_ref[...], mk, mv, S)
        for c in range(HG_SUB):
            for h in range(HG_HEADS):
                cat_ref[c * HG_CHUNK:(c + 1) * HG_CHUNK, h * HG_DIM:(h + 1) * HG_DIM] = mix[c][h].astype(cat_ref.dtype)
        for h in range(HG_HEADS):
            s_scr[h] = s_new[h]
        base = HG_HEADS * HG_DIM
        for a in range(XA_HEADS):
            cat_ref[:, base + a * XA_DIM:base + (a + 1) * XA_DIM] = att[a].astype(cat_ref.dtype)

    return pl.pallas_call(
        kern, grid=(bl, nb),
        in_specs=[pl.BlockSpec((R, zw), lambda b, n: (b * nb + n, 0)),
                  pl.BlockSpec(lb_logits.shape, lambda b, n: (0, 0)),
                  pl.BlockSpec(gnorm.shape, lambda b, n: (0, 0)),
                  pl.BlockSpec((mem_len, kv.shape[1]), lambda b, n: (b, 0))],
        out_specs=[pl.BlockSpec((R, cat_w), lambda b, n: (b * nb + n, 0)),
                   pl.BlockSpec((None, HG_HEADS, HG_DIM, HG_DIM), lambda b, n: (b * nb + n, 0, 0, 0))],
        out_shape=[jax.ShapeDtypeStruct((T, cat_w), BF),
                   jax.ShapeDtypeStruct((bl * nb, HG_HEADS, HG_DIM, HG_DIM), F32)],
        scratch_shapes=[pltpu.VMEM((HG_HEADS, HG_DIM, HG_DIM), F32)],
        name="hgrn_fwd",
        compiler_params=pltpu.CompilerParams(dimension_semantics=("arbitrary", "arbitrary"), vmem_limit_bytes=VMEM_CAP_BYTES),
    )(z, lb_logits, gnorm, kv)


def _hgrn_bwd2(z, dcat, stash, lb_logits, gnorm, kv, bl, seq):
    T, zw = z.shape
    mem_len = kv.shape[0] // bl
    cat_w = dcat.shape[1]
    R = HG_SUB * HG_CHUNK
    nb = seq // R

    def kern(z_ref, dc_ref, st_ref, lb_ref, gn_ref, kv_ref, dz_ref, dkv_ref, dlb_ref, dgn_ref, ds_scr):
        first = jnp.logical_and(pl.program_id(0) == 0, pl.program_id(1) == 0)

        @pl.when(pl.program_id(1) == 0)
        def _():
            ds_scr[...] = jnp.zeros(ds_scr.shape, F32)
            dkv_ref[...] = jnp.zeros(dkv_ref.shape, F32)

        @pl.when(first)
        def _():
            dlb_ref[...] = jnp.zeros(dlb_ref.shape, F32)
            dgn_ref[...] = jnp.zeros(dgn_ref.shape, F32)

        zq, zf, zi, zg, zx = _hgrn_rows(z_ref)
        mk, mv = _kv_pieces(kv_ref)
        l0, l1, l2 = _lb_pieces(lb_ref)
        S = [st_ref[h] for h in range(HG_HEADS)]
        _, vjp = jax.vjp(_hgrn_steps, zq, zf, zi, zg, zx, l0, l1, l2, gn_ref[...], mk, mv, S)
        d_mix = [[dc_ref[c * HG_CHUNK:(c + 1) * HG_CHUNK, h * HG_DIM:(h + 1) * HG_DIM] for h in range(HG_HEADS)]
                 for c in range(HG_SUB)]
        base = HG_HEADS * HG_DIM
        d_att = [dc_ref[:, base + a * XA_DIM:base + (a + 1) * XA_DIM] for a in range(XA_HEADS)]
        d_s = [ds_scr[h] for h in range(HG_HEADS)]
        dzq, dzf, dzi, dzg, dzx, dl0, dl1, dl2, dgn, dmk, dmv, dS = vjp((d_mix, d_att, d_s))
        W = HG_HEADS * HG_DIM
        for c in range(HG_SUB):
            rows = slice(c * HG_CHUNK, (c + 1) * HG_CHUNK)
            for h in range(HG_HEADS):
                for k, part in enumerate((dzq, dzf, dzi, dzg)):
                    dz_ref[rows, k * W + h * HG_DIM:k * W + (h + 1) * HG_DIM] = part[c][h].astype(dz_ref.dtype)
        for h in range(HG_HEADS):
            sl = slice(h * HG_DIM, (h + 1) * HG_DIM)
            ds_scr[h] = dS[h]
            dlb_ref[0:1, sl] += dl0[h]
            dlb_ref[1:2, sl] += dl1[h]
            dlb_ref[2:3, sl] += dl2[h]
        dgn_ref[...] += dgn
        KW = XA_HEADS * XA_DIM
        for a in range(XA_HEADS):
            dz_ref[:, 4 * W + a * XA_DIM:4 * W + (a + 1) * XA_DIM] = dzx[a].astype(dz_ref.dtype)
            dkv_ref[:, a * XA_DIM:(a + 1) * XA_DIM] += dmk[a]
            dkv_ref[:, KW + a * XA_DIM:KW + (a + 1) * XA_DIM] += dmv[a]

    rev = lambda b, n: (b * nb + (nb - 1 - n), 0)
    return pl.pallas_call(
        kern, grid=(bl, nb),
        in_specs=[pl.BlockSpec((R, zw), rev),
                  pl.BlockSpec((R, cat_w), rev),
                  pl.BlockSpec((None, HG_HEADS, HG_DIM, HG_DIM), lambda b, n: (b * nb + (nb - 1 - n), 0, 0, 0)),
                  pl.BlockSpec(lb_logits.shape, lambda b, n: (0, 0)),
                  pl.BlockSpec(gnorm.shape, lambda b, n: (0, 0)),
                  pl.BlockSpec((mem_len, kv.shape[1]), lambda b, n: (b, 0))],
        out_specs=[pl.BlockSpec((R, zw), rev),
                   pl.BlockSpec((mem_len, kv.shape[1]), lambda b, n: (b, 0)),
                   pl.BlockSpec(lb_logits.shape, lambda b, n: (0, 0)),
                   pl.BlockSpec(gnorm.shape, lambda b, n: (0, 0))],
        out_shape=[jax.ShapeDtypeStruct((T, zw), BF), jax.ShapeDtypeStruct(kv.shape, F32),
                   jax.ShapeDtypeStruct(lb_logits.shape, F32), jax.ShapeDtypeStruct(gnorm.shape, F32)],
        scratch_shapes=[pltpu.VMEM((HG_HEADS, HG_DIM, HG_DIM), F32)],
        name="hgrn_bwd",
        compiler_params=pltpu.CompilerParams(dimension_semantics=("arbitrary", "arbitrary"), vmem_limit_bytes=VMEM_CAP_BYTES),
    )(z, dcat, stash, lb_logits, gnorm, kv)


GM_SUB = 2


def _gmlp_pieces(z_ref):
    W = GM_GROUPS * GM_GROUP_DIM
    zu = [z_ref[:, g * GM_GROUP_DIM:(g + 1) * GM_GROUP_DIM] for g in range(GM_GROUPS)]
    zv = [z_ref[:, W + g * GM_GROUP_DIM:W + (g + 1) * GM_GROUP_DIM] for g in range(GM_GROUPS)]
    zx = [z_ref[:, 2 * W + a * XA_DIM:2 * W + (a + 1) * XA_DIM] for a in range(XA_HEADS)]
    return zu, zv, zx


def _gmlp_params(lng_ref, lnb_ref, ws_ref, bs_ref):
    lng = [lng_ref[:, g * GM_GROUP_DIM:(g + 1) * GM_GROUP_DIM] for g in range(GM_GROUPS)]
    lnb = [lnb_ref[:, g * GM_GROUP_DIM:(g + 1) * GM_GROUP_DIM] for g in range(GM_GROUPS)]
    ws = [ws_ref[g] for g in range(GM_GROUPS)]
    bs = [bs_ref[g:g + 1, :] for g in range(GM_GROUPS)]
    return lng, lnb, ws, bs


def _gmlp_fwd(z, ln_g, ln_b, w_s, b_s, kv, bl, nc):
    T, zw = z.shape
    mem_len = kv.shape[0] // bl
    cat_w = GM_GROUPS * GM_GROUP_DIM + XA_HEADS * XA_DIM

    assert nc % GM_SUB == 0
    nc = nc // GM_SUB
    R = GM_SUB * GM_CHUNK

    def kern(z_ref, lng_ref, lnb_ref, ws_ref, bs_ref, kv_ref, cat_ref):
        lng, lnb, ws, bs = _gmlp_params(lng_ref, lnb_ref, ws_ref, bs_ref)
        mk, mv = _kv_pieces(kv_ref)
        for c in range(GM_SUB):
            rows = pl.ds(c * GM_CHUNK, GM_CHUNK)
            zu, zv, zx = _gmlp_pieces(z_ref.at[rows])
            out = cat_ref.at[rows]
            outs = _gmlp_block(zu, zv, zx, lng, lnb, ws, bs, mk, mv)
            for g in range(GM_GROUPS):
                out[:, g * GM_GROUP_DIM:(g + 1) * GM_GROUP_DIM] = outs[g].astype(cat_ref.dtype)
            base = GM_GROUPS * GM_GROUP_DIM
            for a in range(XA_HEADS):
                out[:, base + a * XA_DIM:base + (a + 1) * XA_DIM] = outs[GM_GROUPS + a].astype(cat_ref.dtype)

    full2 = lambda b, n: (0, 0)
    return pl.pallas_call(
        kern, grid=(bl, nc),
        in_specs=[pl.BlockSpec((R, zw), lambda b, n: (b * nc + n, 0)),
                  pl.BlockSpec(ln_g.shape, full2), pl.BlockSpec(ln_b.shape, full2),
                  pl.BlockSpec(w_s.shape, lambda b, n: (0, 0, 0)), pl.BlockSpec(b_s.shape, full2),
                  pl.BlockSpec((mem_len, kv.shape[1]), lambda b, n: (b, 0))],
        out_specs=pl.BlockSpec((R, cat_w), lambda b, n: (b * nc + n, 0)),
        out_shape=jax.ShapeDtypeStruct((T, cat_w), BF),
        name="gmlp_fwd",
        compiler_params=pltpu.CompilerParams(dimension_semantics=("arbitrary", "arbitrary"), vmem_limit_bytes=VMEM_CAP_BYTES),
    )(z, ln_g, ln_b, w_s, b_s, kv)


def _gmlp_bwd(z, dcat, ln_g, ln_b, w_s, b_s, kv, bl, nc):
    T, zw = z.shape
    mem_len = kv.shape[0] // bl
    cat_w = dcat.shape[1]
    assert nc % GM_SUB == 0
    nc = nc // GM_SUB

    def kern(z_ref, dc_ref, lng_ref, lnb_ref, ws_ref, bs_ref, kv_ref,
             dz_ref, dkv_ref, dlng_ref, dlnb_ref, dws_ref, dbs_ref):
        first = jnp.logical_and(pl.program_id(0) == 0, pl.program_id(1) == 0)

        @pl.when(pl.program_id(1) == 0)
        def _():
            dkv_ref[...] = jnp.zeros(dkv_ref.shape, F32)

        @pl.when(first)
        def _():
            dlng_ref[...] = jnp.zeros(dlng_ref.shape, F32)
            dlnb_ref[...] = jnp.zeros(dlnb_ref.shape, F32)
            dws_ref[...] = jnp.zeros(dws_ref.shape, F32)
            dbs_ref[...] = jnp.zeros(dbs_ref.shape, F32)

        lng, lnb, ws, bs = _gmlp_params(lng_ref, lnb_ref, ws_ref, bs_ref)
        mk, mv = _kv_pieces(kv_ref)
        W = GM_GROUPS * GM_GROUP_DIM
        KW = XA_HEADS * XA_DIM
        for c in range(GM_SUB):
            rows = pl.ds(c * GM_CHUNK, GM_CHUNK)
            zu, zv, zx = _gmlp_pieces(z_ref.at[rows])
            dc, dz = dc_ref.at[rows], dz_ref.at[rows]
            _, vjp = jax.vjp(_gmlp_block, zu, zv, zx, lng, lnb, ws, bs, mk, mv)
            d_outs = [dc[:, g * GM_GROUP_DIM:(g + 1) * GM_GROUP_DIM] for g in range(GM_GROUPS)]
            d_outs += [dc[:, W + a * XA_DIM:W + (a + 1) * XA_DIM] for a in range(XA_HEADS)]
            dzu, dzv, dzx, dlng, dlnb, dws, dbs, dmk, dmv = vjp(d_outs)
            for g in range(GM_GROUPS):
                sl = slice(g * GM_GROUP_DIM, (g + 1) * GM_GROUP_DIM)
                dz[:, sl] = dzu[g].astype(dz_ref.dtype)
                dz[:, W + g * GM_GROUP_DIM:W + (g + 1) * GM_GROUP_DIM] = dzv[g].astype(dz_ref.dtype)
                dlng_ref[:, sl] += dlng[g]
                dlnb_ref[:, sl] += dlnb[g]
                dws_ref[g] += dws[g]
                dbs_ref[g:g + 1, :] += dbs[g]
            for a in range(XA_HEADS):
                dz[:, 2 * W + a * XA_DIM:2 * W + (a + 1) * XA_DIM] = dzx[a].astype(dz_ref.dtype)
                dkv_ref[:, a * XA_DIM:(a + 1) * XA_DIM] += dmk[a]
                dkv_ref[:, KW + a * XA_DIM:KW + (a + 1) * XA_DIM] += dmv[a]

    full2 = lambda b, n: (0, 0)
    full3 = lambda b, n: (0, 0, 0)
    blk = lambda b, n: (b * nc + n, 0)
    return pl.pallas_call(
        kern, grid=(bl, nc),
        in_specs=[pl.BlockSpec((GM_SUB * GM_CHUNK, zw), blk), pl.BlockSpec((GM_SUB * GM_CHUNK, cat_w), blk),
                  pl.BlockSpec(ln_g.shape, full2), pl.BlockSpec(ln_b.shape, full2),
                  pl.BlockSpec(w_s.shape, full3), pl.BlockSpec(b_s.shape, full2),
                  pl.BlockSpec((mem_len, kv.shape[1]), lambda b, n: (b, 0))],
        out_specs=[pl.BlockSpec((GM_SUB * GM_CHUNK, zw), blk),
                   pl.BlockSpec((mem_len, kv.shape[1]), lambda b, n: (b, 0)),
                   pl.BlockSpec(ln_g.shape, full2), pl.BlockSpec(ln_b.shape, full2),
                   pl.BlockSpec(w_s.shape, full3), pl.BlockSpec(b_s.shape, full2)],
        out_shape=[jax.ShapeDtypeStruct((T, zw), BF), jax.ShapeDtypeStruct(kv.shape, F32),
                   jax.ShapeDtypeStruct(ln_g.shape, F32), jax.ShapeDtypeStruct(ln_b.shape, F32),
                   jax.ShapeDtypeStruct(w_s.shape, F32), jax.ShapeDtypeStruct(b_s.shape, F32)],
        name="gmlp_bwd",
        compiler_params=pltpu.CompilerParams(dimension_semantics=("arbitrary", "arbitrary"), vmem_limit_bytes=VMEM_CAP_BYTES),
    )(z, dcat, ln_g, ln_b, w_s, b_s, kv)


def _place():
    x, y, c = lax.axis_index("x"), lax.axis_index("y"), lax.axis_index("c")
    chips = [(1 - x, y), (x, 1 - y), (1 - x, 1 - y)]
    return x, y, c, chips


def _half(ref, kind, e):
    if kind == "col":
        n = ref.shape[1] // 2
        return ref.at[:, pl.ds(pl.multiple_of(e * n, n), n), :]
    n = ref.shape[2] // 2
    return ref.at[:, :, pl.ds(pl.multiple_of(e * n, n), n)]


def _slot(ref, kind, j, n):
    if kind == "col":
        return ref.at[:, :, pl.ds(pl.multiple_of(j * n, n), n)]
    return ref.at[:, pl.ds(pl.multiple_of(j * n, n), n), :]


AG_CHUNKS = 4


def _allgather_seq(name, items, cid):
    nt = len(items)
    kinds = [k for (_, k, _) in items]
    slot_kind = ["row" if k == "row" else "col" for k in kinds]
    out_type = []
    for s, k, l in items:
        L, r, c = s.shape
        lo = L if l is None else 1
        out_type.append(jax.ShapeDtypeStruct((lo, 4 * r, c) if k == "row" else (lo, r, 4 * c), s.dtype))

    def part(ref, t, e, q):
        if kinds[t] == "vec":
            return ref
        half = _half(ref, kinds[t], e)
        n = half.shape[1] // AG_CHUNKS
        return half.at[:, pl.ds(q * n, n), :]

    def chunks(t):
        return 1 if kinds[t] == "vec" else AG_CHUNKS

    def body(*refs):
        sh = [refs[t] if items[t][2] is None else refs[t].at[pl.ds(items[t][2], 1)] for t in range(nt)]
        full = refs[nt:2 * nt]
        loc, s_ici, r_ici, s_rel, r_rel, s_d2d, r_d2d = refs[2 * nt:]
        x, y, c, chips = _place()
        own = 2 * x + y
        sibling = (x, y, 1 - c)
        nbr = [(1 - x, y, c), (x, 1 - y, c)]
        nbr_chip = [2 * (1 - x) + y, 2 * x + (1 - y)]
        diag, diag_chip = (1 - x, 1 - y, c), 2 * (1 - x) + (1 - y)
        barrier = pltpu.get_barrier_semaphore()
        for peer in nbr + [diag, sibling]:
            pl.semaphore_signal(barrier, inc=1, device_id=peer, device_id_type=MESH)
        pl.semaphore_wait(barrier, 4)
        width = [sh[t].shape[1] if kinds[t] == "row" else sh[t].shape[2] for t in range(nt)]
        half_q = AG_CHUNKS // 2

        def region(t, chip_idx, e, q):
            return part(_slot(full[t], slot_kind[t], chip_idx, width[t]), t, e, q)

        def remote(src, dst, ssem, rsem, to):
            return pltpu.make_async_remote_copy(src_ref=src, dst_ref=dst, send_sem=ssem, recv_sem=rsem,
                                                device_id=to, device_id_type=MESH)

        started = []
        for t in range(nt):
            mine = pltpu.make_async_copy(sh[t], _slot(full[t], slot_kind[t], own, width[t]), loc.at[t])
            mine.start()
            started.append(mine)
        sent = []
        for q in range(AG_CHUNKS):
            for t in range(nt):
                if q >= chunks(t):
                    continue
                for d in range(2):
                    cp = remote(part(sh[t], t, c, q), region(t, own, c, q), s_ici.at[t, d, q], r_ici.at[t, d, q], nbr[d])
                    cp.start()
                    sent.append(cp)
                if kinds[t] == "vec":
                    cp = remote(sh[t], region(t, own, c, 0), s_rel.at[t, 0, 0], r_rel.at[t, 0, 0], diag)
                    cp.start()
                    sent.append(cp)
        for q in range(AG_CHUNKS):
            for t in range(nt):
                if q >= chunks(t):
                    continue
                for d in range(2):
                    landed = region(t, nbr_chip[d], c, q)
                    remote(landed, landed, s_ici.at[t, d, q], r_ici.at[t, d, q], nbr[d]).wait_recv()
                    if kinds[t] == "vec":
                        continue
                    fw = remote(landed, landed, s_d2d.at[t, d, q], r_d2d.at[t, d, q], sibling)
                    fw.start()
                    sent.append(fw)
                    if (q < half_q) == (d == 0):
                        relay = remote(landed, landed, s_rel.at[t, d, q], r_rel.at[t, d, q], nbr[1 - d])
                        relay.start()
                        sent.append(relay)
        for t in range(nt):
            for q in range(chunks(t)):
                d = 0 if q < half_q else 1
                landed = region(t, diag_chip, c, q)
                if kinds[t] == "vec":
                    remote(landed, landed, s_rel.at[t, 0, 0], r_rel.at[t, 0, 0], diag).wait_recv()
                    continue
                remote(landed, landed, s_rel.at[t, d, q], r_rel.at[t, d, q], nbr[1 - d]).wait_recv()
                fw = remote(landed, landed, s_d2d.at[t, 2, q], r_d2d.at[t, 2, q], sibling)
                fw.start()
                sent.append(fw)
        for t in range(nt):
            if kinds[t] == "vec":
                continue
            for p, chip_idx in enumerate(nbr_chip + [diag_chip]):
                for q in range(AG_CHUNKS):
                    other = region(t, chip_idx, 1 - c, q)
                    remote(other, other, s_d2d.at[t, p, q], r_d2d.at[t, p, q], sibling).wait_recv()
        for cp in sent:
            cp.wait_send()
        for cp in started:
            cp.wait()

    sems = pltpu.SemaphoreType.DMA
    return pl.kernel(
        body, out_type=out_type, mesh=plsc.ScalarSubcoreMesh(axis_name="seq", num_cores=1),
        scratch_types=[sems((nt,)), sems((nt, 2, AG_CHUNKS)), sems((nt, 2, AG_CHUNKS)), sems((nt, 2, AG_CHUNKS)),
                       sems((nt, 2, AG_CHUNKS)), sems((nt, 3, AG_CHUNKS)), sems((nt, 3, AG_CHUNKS))],
        compiler_params=pltpu.CompilerParams(collective_id=cid), name=name,
    )(*[s for (s, _, _) in items])


def _slot2(ref, kind, j, n):
    if kind == "col":
        return ref.at[:, pl.ds(pl.multiple_of(j * n, n), n)]
    return ref.at[pl.ds(pl.multiple_of(j * n, n), n), :]


def _rs_chips_seq(name, parts, kinds, cid):
    nm = len(parts)
    out_type = []
    for g, k in zip(parts, kinds):
        r, c = g.shape
        ps = (r, c // 4) if k == "col" else (r // 4, c)
        out_type += [jax.ShapeDtypeStruct(ps, BF), jax.ShapeDtypeStruct((3,) + ps, BF)]

    def body(*refs):
        g = refs[:nm]
        outs = refs[nm:3 * nm]
        loc, ssem, rsem = refs[3 * nm:]
        x, y, c, chips = _place()
        own = 2 * x + y
        barrier = pltpu.get_barrier_semaphore()
        for (px, py) in chips:
            pl.semaphore_signal(barrier, inc=1, device_id=(px, py, c), device_id_type=MESH)
        pl.semaphore_wait(barrier, 3)
        cps = []
        for m in range(nm):
            k = kinds[m]
            own_o, got_o = outs[2 * m], outs[2 * m + 1]
            n = g[m].shape[1] // 4 if k == "col" else g[m].shape[0] // 4
            lc = pltpu.make_async_copy(_slot2(g[m], k, own, n), own_o, loc.at[m])
            lc.start()
            cps.append(lc)
            for p, (px, py) in enumerate(chips):
                cp = pltpu.make_async_remote_copy(
                    src_ref=_slot2(g[m], k, 2 * px + py, n), dst_ref=got_o.at[p],
                    send_sem=ssem.at[m, p], recv_sem=rsem.at[m, p], device_id=(px, py, c), device_id_type=MESH)
                cp.start()
                cps.append(cp)
        for cp in cps:
            cp.wait()

    return pl.kernel(
        body, out_type=out_type, mesh=plsc.ScalarSubcoreMesh(axis_name="seq", num_cores=1),
        scratch_types=[pltpu.SemaphoreType.DMA((nm,)), pltpu.SemaphoreType.DMA((nm, 3)), pltpu.SemaphoreType.DMA((nm, 3))],
        compiler_params=pltpu.CompilerParams(collective_id=cid), name=name,
    )(*parts)


def _finish_share(name, own, got, kind, c_arr):
    L, r, c = own.shape
    tr = _pick(r, 128 if kind == "col" else 256)
    nb = r // tr
    nq = L * nb
    own2 = own.reshape(L * r, c)
    got2 = got.reshape(3 * L * r, c)
    pick = lambda h, q: q * (1 - h) + (nq - 1) * h
    in_specs = [pl.BlockSpec((tr, c), lambda h, q, cc: (pick(h, q), 0))]
    in_specs += [pl.BlockSpec((tr, c), functools.partial(lambda h, q, cc, p: (p * nq + pick(h, q), 0), p=p)) for p in range(3)]
    if kind == "col":
        out_sd = (L, 2, r, c)
        o_spec = pl.BlockSpec((None, 2, tr, c), lambda h, q, cc: ((q * h) // nb, 0, (q * h) % nb, 0))
    else:
        out_sd = (L * r, 2 * c)
        o_spec = pl.BlockSpec((tr, 2 * c), lambda h, q, cc: (q * h, 0))

    def kern(c_ref, o_ref, g0, g1, g2, out_ref, mine, recv, ssem, rsem):
        h, q = pl.program_id(0), pl.program_id(1)
        x, y, cc, _ = _place()

        def swap(qq):
            return pltpu.make_async_remote_copy(src_ref=mine.at[qq], dst_ref=recv.at[qq], send_sem=ssem.at[qq],
                                                recv_sem=rsem.at[qq], device_id=(x, y, 1 - cc), device_id_type=MESH)

        @pl.when(h == 0)
        def _():
            mine[q] = ((o_ref[...].astype(F32) + g0[...].astype(F32)) + g1[...].astype(F32)) + g2[...].astype(F32)
            swap(q).start()

        @pl.when(h == 1)
        def _():
            swap(q).wait()
            a, b = mine[q], recv[q]
            first = c_ref[0] == 0
            lo, hi = jnp.where(first, a, b), jnp.where(first, b, a)
            if kind == "col":
                out_ref[0] = lo
                out_ref[1] = hi
            else:
                out_ref[:, :c] = lo
                out_ref[:, c:] = hi

    est = 2 * nq * tr * c * 4 + 6 * tr * c * 4 + 8 * tr * c * 2
    full = pl.pallas_call(
        kern,
        grid_spec=pltpu.PrefetchScalarGridSpec(
            num_scalar_prefetch=1, grid=(2, nq), in_specs=in_specs, out_specs=o_spec,
            scratch_shapes=[pltpu.VMEM((nq, tr, c), F32), pltpu.VMEM((nq, tr, c), F32),
                            pltpu.SemaphoreType.DMA((nq,)), pltpu.SemaphoreType.DMA((nq,))]),
        out_shape=jax.ShapeDtypeStruct(out_sd, F32), name=name,
        compiler_params=pltpu.CompilerParams(dimension_semantics=("arbitrary", "arbitrary"),
                                             vmem_limit_bytes=VMEM_CAP_BYTES),
    )(c_arr, own2, got2, got2, got2)
    return full.reshape(L, 2 * r, c) if kind == "col" else full.reshape(L, r, 2 * c)


def _small_allreduce(buf, name):
    R = buf.shape[0]
    assert R % 16 == 0
    h = R // 2

    def body(x_ref, o_ref, sib, csum, got, s_a, r_a, s_b, r_b, s_c, r_c):
        x, y, c, chips = _place()
        sibling = (x, y, 1 - c)
        own = 2 * x + y
        swap = pltpu.make_async_remote_copy(src_ref=x_ref, dst_ref=sib, send_sem=s_a, recv_sem=r_a,
                                            device_id=sibling, device_id_type=MESH)
        swap.start()
        swap.wait()
        a, b = x_ref[...], sib[...]
        south = c == 0
        csum[...] = jnp.where(south, a, b) + jnp.where(south, b, a)
        lo = pl.multiple_of(c * h, 8)
        mine = csum.at[pl.ds(lo, h)]
        got[own] = csum[pl.ds(lo, h)]
        sends = []
        for p, (px, py) in enumerate(chips):
            cp = pltpu.make_async_remote_copy(src_ref=mine, dst_ref=got.at[own], send_sem=s_b.at[p], recv_sem=r_b.at[p],
                                              device_id=(px, py, c), device_id_type=MESH)
            cp.start()
            sends.append(cp)
        for cp in sends:
            cp.wait()
        o_ref[pl.ds(lo, h)] = ((got[0] + got[1]) + got[2]) + got[3]
        done = o_ref.at[pl.ds(lo, h)]
        back = pltpu.make_async_remote_copy(src_ref=done, dst_ref=done, send_sem=s_c, recv_sem=r_c,
                                            device_id=sibling, device_id_type=MESH)
        back.start()
        back.wait_send()
        other = o_ref.at[pl.ds(pl.multiple_of((1 - c) * h, 8), h)]
        pltpu.make_async_remote_copy(src_ref=other, dst_ref=other, send_sem=s_c, recv_sem=r_c,
                                     device_id=sibling, device_id_type=MESH).wait_recv()

    vm = pl.BlockSpec(memory_space=pltpu.VMEM)
    return pl.pallas_call(
        body, out_shape=jax.ShapeDtypeStruct(buf.shape, F32), in_specs=[vm], out_specs=vm,
        scratch_shapes=[pltpu.VMEM((R, LANES), F32), pltpu.VMEM((R, LANES), F32), pltpu.VMEM((4, h, LANES), F32),
                        pltpu.SemaphoreType.DMA, pltpu.SemaphoreType.DMA, pltpu.SemaphoreType.DMA((3,)),
                        pltpu.SemaphoreType.DMA((3,)), pltpu.SemaphoreType.DMA, pltpu.SemaphoreType.DMA],
        name=name,
        compiler_params=pltpu.CompilerParams(vmem_limit_bytes=VMEM_CAP_BYTES),
    )(buf)


PACK_TILE_ROWS = 8


def _item_rows(shape):
    n = 1
    for d in shape:
        n *= d
    return -(-n // (PACK_TILE_ROWS * LANES)) * PACK_TILE_ROWS


def _pack(arrs, rows_total):
    buf = jnp.zeros((rows_total, LANES), F32)
    r = 0
    for a in arrs:
        f = a.reshape(-1).astype(F32)
        nr = _item_rows(a.shape)
        block = jnp.pad(f, (0, nr * LANES - f.shape[0])).reshape(nr, LANES)
        buf = lax.dynamic_update_slice(buf, block, (r, 0))
        r += nr
    return buf


def _unpack(buf, shapes):
    out, r = [], 0
    for s in shapes:
        n = 1
        for d in s:
            n *= d
        nr = _item_rows(s)
        out.append(buf[r:r + nr].reshape(-1)[:n].reshape(s))
        r += nr
    return out


def _rows_needed(shapes):
    return -(-sum(_item_rows(s) for s in shapes) // (2 * PACK_TILE_ROWS)) * (2 * PACK_TILE_ROWS)


def _two_rows(a, b):
    out = jnp.zeros((2, a.shape[1]), a.dtype)
    return lax.dynamic_update_slice(lax.dynamic_update_slice(out, a, (0, 0)), b, (1, 0))


def _adam(w, g, m, v):
    m = ADAM_B1 * m + (1.0 - ADAM_B1) * g
    v = ADAM_B2 * v + (1.0 - ADAM_B2) * jnp.square(g)
    m_hat = m / (1.0 - ADAM_B1 ** ADAM_STEP)
    v_hat = v / (1.0 - ADAM_B2 ** ADAM_STEP)
    delta = -ADAM_LR * (m_hat / (jnp.sqrt(v_hat) + ADAM_EPS) + ADAM_WD * w)
    return delta, m, v


def _adam_call(name, w2, g2, m2, v2, tr):
    def fn(rv, cv):
        return list(_adam(*rv)), []

    width = w2.shape[1]
    return _rowcall(name, fn, [(w2, 0, width), (g2, 0, width), (m2, 0, width), (v2, 0, width)], [],
                    [(width, F32)] * 3, [], tr)


def kernel(x, mem, mem_norm, lb_logits, ffn1_norm, ffn1_w_in, ffn1_w_out, mix_norm, mem_w_kv, hgrn_w_in, hgrn_gnorm, hgrn_w_out, gmlp_w_in, gmlp_ln_g, gmlp_ln_b, gmlp_w_s, gmlp_b_s, gmlp_w_out, ffn2_norm, ffn2_w_in, ffn2_w_out, final_norm, loss_target, m_mem_norm, m_lb_logits, m_ffn1_norm, m_ffn1_w_in, m_ffn1_w_out, m_mix_norm, m_mem_w_kv, m_hgrn_w_in, m_hgrn_gnorm, m_hgrn_w_out, m_gmlp_w_in, m_gmlp_ln_g, m_gmlp_ln_b, m_gmlp_w_s, m_gmlp_b_s, m_gmlp_w_out, m_ffn2_norm, m_ffn2_w_in, m_ffn2_w_out, m_final_norm, v_mem_norm, v_lb_logits, v_ffn1_norm, v_ffn1_w_in, v_ffn1_w_out, v_mix_norm, v_mem_w_kv, v_hgrn_w_in, v_hgrn_gnorm, v_hgrn_w_out, v_gmlp_w_in, v_gmlp_ln_g, v_gmlp_ln_b, v_gmlp_w_s, v_gmlp_b_s, v_gmlp_w_out, v_ffn2_norm, v_ffn2_w_in, v_ffn2_w_out, v_final_norm):
    bl, seq, D = x.shape
    T = bl * seq
    mem_len = mem.shape[1]
    chip = 2 * lax.axis_index("x") + lax.axis_index("y")
    c_arr = lax.axis_index("c").astype(jnp.int32).reshape(1)
    TR = 1024

    big = [("ffn1_w_in", ffn1_w_in, "col"), ("ffn1_w_out", ffn1_w_out, "row"), ("mem_w_kv", mem_w_kv, "col"),
           ("hgrn_w_in", hgrn_w_in, "col"), ("hgrn_w_out", hgrn_w_out, "row"), ("gmlp_w_in", gmlp_w_in, "col"),
           ("gmlp_w_out", gmlp_w_out, "row"), ("ffn2_w_in", ffn2_w_in, "col"), ("ffn2_w_out", ffn2_w_out, "row")]
    kinds = [k for (_, _, k) in big]
    shards_bf = []
    for nm, w, _ in big:
        L, r, c = w.shape
        (wb,) = _rowcall("cast_" + nm, lambda rv, cv: ([rv[0]], []), [(w.reshape(L * r, c), 0, c)], [], [(c, BF)], [], 512)
        shards_bf.append(wb.reshape(L, r, c))
    sb = dict(zip([nm for (nm, _, _) in big], shards_bf))
    groups = [[("ffn1_w_in", 0)], [("ffn1_w_out", 0)], [("hgrn_w_in", None)], [("mem_w_kv", None)], [("hgrn_w_out", None)],
              [("ffn2_w_in", 0), ("ffn2_w_out", 0), ("gmlp_ln_g", None), ("gmlp_ln_b", None)],
              [("ffn1_w_in", 1), ("ffn1_w_out", 1)],
              [("gmlp_w_in", None), ("gmlp_w_out", None)],
              [("ffn2_w_in", 1), ("ffn2_w_out", 1)]]
    kind_of = {nm: k for (nm, _, k) in big}
    for nm, vec in (("gmlp_ln_g", gmlp_ln_g), ("gmlp_ln_b", gmlp_ln_b)):
        sb[nm] = vec.reshape(1, 1, -1)
        kind_of[nm] = "vec"
    gathered = {nm: [None, None] for nm in ("ffn1_w_in", "ffn1_w_out", "ffn2_w_in", "ffn2_w_out")}
    for gi, grp in enumerate(groups):
        outs = _allgather_seq("gather_%d" % gi, [(sb[nm], kind_of[nm], l) for (nm, l) in grp], gi)
        for (nm, l), o in zip(grp, outs):
            if l is None:
                gathered[nm] = o
            else:
                gathered[nm][l] = o

    ln_w = GM_GROUPS * GM_GROUP_DIM
    ln_g_full, ln_b_full = gathered["gmlp_ln_g"].reshape(1, ln_w), gathered["gmlp_ln_b"].reshape(1, ln_w)

    def rms_fwd(name, xin, g):
        (h,) = _rowcall(name, lambda rv, cv: ([_rmsnorm(rv[0], cv[0])], []), [(xin, 0, D)], [g.reshape(1, D)], [(D, BF)], [], TR)
        return h

    def ffn_fwd(tag, xin, h, w_in, w_out, layer, next_gain):
        dff = w_out[layer].shape[1]
        zg, zu, a = _ffn_in_swiglu("ffn_in_" + tag, h, w_in[layer], 1024, dff // 2)
        out = _mm("ffn_out_" + tag, a, w_out[layer], "nn", F32, 1024, 1024, dff, scale=0.5, res=xin, b_lead=0,
                  norm_gain=None if next_gain is None else next_gain.reshape(1, D))
        xo, h_next = (out, None) if next_gain is None else out
        return xo, h_next, (xin, h, zg, zu, a)

    def ffn_bwd(tag, dxo, saved, g, w_in, w_out, layer):
        xin, h, zg, zu, a = saved
        dff = w_out[layer].shape[1]
        dw_out = _mm_tn_pair("ffn_dwo_" + tag, a, dxo, "row", c_arr, dff // 2, T, scale=0.5)
        dz = _ffn_da_swiglu("ffn_da_" + tag, dxo, w_out[layer], zg, zu, 512)
        dw_in = _mm_tn_pair("ffn_dwi_" + tag, h, dz, "col", c_arr, 512, T)
        dx, dg = _mm_dh_rms("ffn_dh_" + tag, dz, w_in[layer], xin, g.reshape(1, D), dxo, 512)
        return dx, dg, dw_in, dw_out

    def rms_bwd(name, xin, g, dh, dres):
        def fn(rv, cv):
            _, vjp = jax.vjp(_rmsnorm, rv[0], cv[0])
            dx, dg = vjp(rv[1])
            if dres is not None:
                dx = dx + rv[2]
            return [dx], [dg]

        rows = [(xin, 0, D), (dh, 0, D)] + ([(dres, 0, D)] if dres is not None else [])
        dx, dg = _rowcall(name, fn, rows, [g.reshape(1, D)], [(D, F32)], [((1, D), F32)], TR)
        return dx, dg

    x0 = x.reshape(T, D)
    tgt = loss_target.reshape(T, D)
    mem2 = mem.reshape(bl * mem_len, D)
    memn = rms_fwd("rms_mem", mem2, mem_norm)

    h_f10 = rms_fwd("rms_f1l0", x0, ffn1_norm[0])
    x1, h_m0, sv_f10 = ffn_fwd("f1l0", x0, h_f10, gathered["ffn1_w_in"], gathered["ffn1_w_out"], 0, mix_norm[0])
    z_m0 = _mm("mix_in_0", h_m0, gathered["hgrn_w_in"], "nn", F32, 2048, 512, D, b_lead=0)
    kv = [_mm("kv_%d" % i, memn, gathered["mem_w_kv"], "nn", F32, 512, 512, D, b_lead=i) for i in range(2)]
    cat0, stash0 = _hgrn_fwd2(z_m0, lb_logits, hgrn_gnorm, kv[0], bl, seq)
    x2, h_f20 = _mm("mix_out_0", cat0, gathered["hgrn_w_out"], "nn", F32, 1024, 1024, cat0.shape[1], res=x1, b_lead=0,
                    norm_gain=ffn2_norm[0].reshape(1, D))
    x3, h_f11, sv_f20 = ffn_fwd("f2l0", x2, h_f20, gathered["ffn2_w_in"], gathered["ffn2_w_out"], 0, ffn1_norm[1])
    x4, h_m1, sv_f11 = ffn_fwd("f1l1", x3, h_f11, gathered["ffn1_w_in"], gathered["ffn1_w_out"], 1, mix_norm[1])
    z_m1 = _mm("mix_in_1", h_m1, gathered["gmlp_w_in"], "nn", F32, 2048, 512, D, b_lead=0)
    nc1 = seq // GM_CHUNK
    w_s, b_s = gmlp_w_s[0], gmlp_b_s[0]
    cat1 = _gmlp_fwd(z_m1, ln_g_full, ln_b_full, w_s, b_s, kv[1], bl, nc1)
    x5, h_f21 = _mm("mix_out_1", cat1, gathered["gmlp_w_out"], "nn", F32, 1024, 1024, cat1.shape[1], res=x4, b_lead=0,
                    norm_gain=ffn2_norm[1].reshape(1, D))
    x6, _, sv_f21 = ffn_fwd("f2l1", x5, h_f21, gathered["ffn2_w_in"], gathered["ffn2_w_out"], 1, None)

    def head(rv, cv):
        def f(xx, gg):
            err = _rmsnorm(xx, gg) - rv[1]
            return 0.5 * jnp.sum(jnp.mean(err * err, axis=-1, keepdims=True), axis=0, keepdims=True)

        ls, vjp = jax.vjp(f, rv[0], cv[0])
        dx, dg = vjp(jnp.ones((1, 1), F32))
        return [dx], [dg, jnp.broadcast_to(ls, (1, 128))]

    dx6, d_final, loss_part = _rowcall("loss_head", head, [(x6, 0, D), (tgt, 0, D)], [final_norm.reshape(1, D)],
                                       [(D, F32)], [((1, D), F32), ((1, 128), F32)], TR)

    rs_out = {}
    n_gather = len(groups)

    def rs(gi, items):
        outs = _rs_chips_seq("reduce_%d" % gi, [p for (_, p, _) in items], [k for (_, _, k) in items], n_gather + gi)
        for i, (key, _, _) in enumerate(items):
            rs_out[key] = (outs[2 * i], outs[2 * i + 1])

    dx5, dg_f21, dwi_f21, dwo_f21 = ffn_bwd("f2l1", dx6, sv_f21, ffn2_norm[1], gathered["ffn2_w_in"], gathered["ffn2_w_out"], 1)
    rs(0, [(("ffn2_w_out", 1), dwo_f21, "row"), (("ffn2_w_in", 1), dwi_f21, "col")])
    dcat1 = _mm("mix_dcat_1", dx5, gathered["gmlp_w_out"], "nt", F32, 2048, 1024, D, b_lead=0)
    dwo_m1 = _mm_tn_pair("mix_dwo_1", cat1, dx5, "row", c_arr, 1024, T)
    dz_m1, dkv1, d_lng, d_lnb, d_ws, d_bs = _gmlp_bwd(z_m1, dcat1, ln_g_full, ln_b_full, w_s, b_s, kv[1], bl, nc1)
    dx4, dg_m1 = _mm_dh_rms("mix_dh_1", dz_m1, gathered["gmlp_w_in"], x4, mix_norm[1].reshape(1, D), dx5, 512)
    dwi_m1 = _mm_tn_pair("mix_dwi_1", h_m1, dz_m1, "col", c_arr, 1024, T)
    rs(1, [(("gmlp_w_out", 0), dwo_m1, "row"), (("gmlp_w_in", 0), dwi_m1, "col")])
    dx3, dg_f11, dwi_f11, dwo_f11 = ffn_bwd("f1l1", dx4, sv_f11, ffn1_norm[1], gathered["ffn1_w_in"], gathered["ffn1_w_out"], 1)
    rs(2, [(("ffn1_w_out", 1), dwo_f11, "row"), (("ffn1_w_in", 1), dwi_f11, "col")])

    dx2, dg_f20, dwi_f20, dwo_f20 = ffn_bwd("f2l0", dx3, sv_f20, ffn2_norm[0], gathered["ffn2_w_in"], gathered["ffn2_w_out"], 0)
    rs(3, [(("ffn2_w_out", 0), dwo_f20, "row"), (("ffn2_w_in", 0), dwi_f20, "col")])
    dcat0 = _mm("mix_dcat_0", dx2, gathered["hgrn_w_out"], "nt", F32, 2048, 1024, D, b_lead=0)
    dwo_m0 = _mm_tn_pair("mix_dwo_0", cat0, dx2, "row", c_arr, 1024, T)
    dz_m0, dkv0, d_lb, d_gn = _hgrn_bwd2(z_m0, dcat0, stash0, lb_logits, hgrn_gnorm, kv[0], bl, seq)
    dx1, dg_m0 = _mm_dh_rms("mix_dh_0", dz_m0, gathered["hgrn_w_in"], x1, mix_norm[0].reshape(1, D), dx2, 512)
    dwi_m0 = _mm_tn_pair("mix_dwi_0", h_m0, dz_m0, "col", c_arr, 1024, T)
    rs(4, [(("hgrn_w_out", 0), dwo_m0, "row"), (("hgrn_w_in", 0), dwi_m0, "col")])

    dwkv = [_mm_tn_pair("kv_dw_%d" % i, memn, dkv, "col", c_arr, 1024, 512) for i, dkv in enumerate([dkv0, dkv1])]
    rs(5, [(("mem_w_kv", 0), dwkv[0], "col"), (("mem_w_kv", 1), dwkv[1], "col")])
    dmemn = _mm("kv_dx_0", dkv0, gathered["mem_w_kv"], "nt", F32, 512, 512, 1024, b_lead=0)
    dmemn = _mm("kv_dx_1", dkv1, gathered["mem_w_kv"], "nt", F32, 512, 512, 1024, res=dmemn, b_lead=1)
    _, d_memnorm = rms_bwd("rms_bwd_mem", mem2, mem_norm, dmemn, None)

    dx0, dg_f10, dwi_f10, dwo_f10 = ffn_bwd("f1l0", dx1, sv_f10, ffn1_norm[0], gathered["ffn1_w_in"], gathered["ffn1_w_out"], 0)
    rs(6, [(("ffn1_w_out", 0), dwo_f10, "row")])
    rs(7, [(("ffn1_w_in", 0), dwi_f10, "col")])

    shard_grads = []
    for (nm, w, k) in big:
        per_layer = []
        for l in range(w.shape[0]):
            own, got = rs_out[(nm, l)]
            per_layer.append(_finish_share("finish_%s_%d" % (nm, l), own[None], got[:, None], k, c_arr))
        shard_grads.append(per_layer[0] if len(per_layer) == 1 else jnp.concatenate(per_layer, axis=0))

    big_w = [w for (_, w, _) in big]
    big_m = [m_ffn1_w_in, m_ffn1_w_out, m_mem_w_kv, m_hgrn_w_in, m_hgrn_w_out, m_gmlp_w_in, m_gmlp_w_out, m_ffn2_w_in, m_ffn2_w_out]
    big_v = [v_ffn1_w_in, v_ffn1_w_out, v_mem_w_kv, v_hgrn_w_in, v_hgrn_w_out, v_gmlp_w_in, v_gmlp_w_out, v_ffn2_w_in, v_ffn2_w_out]
    big_out = {}
    for (nm, w, _), g, m, v in zip(big, shard_grads, big_m, big_v):
        L, r, c = w.shape
        d2, m2, v2 = _adam_call("adam_" + nm, w.reshape(L * r, c), g.reshape(L * r, c), m.reshape(L * r, c),
                                v.reshape(L * r, c), 256)
        big_out[nm] = (g, d2.reshape(w.shape), m2.reshape(w.shape), v2.reshape(w.shape))

    d_ffn1n = _two_rows(dg_f10, dg_f11)
    d_mixn = _two_rows(dg_m0, dg_m1)
    d_ffn2n = _two_rows(dg_f20, dg_f21)
    small_parts = [loss_part[:, :1], d_memnorm, d_lb, d_ffn1n, d_mixn, d_gn, d_lng, d_lnb, d_ws, d_bs, d_ffn2n, d_final]
    red_shapes = [(1,), mem_norm.shape, lb_logits.shape, ffn1_norm.shape, mix_norm.shape, hgrn_gnorm.shape, (1, ln_w), (1, ln_w),
                  gmlp_w_s.shape, gmlp_b_s.shape, ffn2_norm.shape, final_norm.shape]
    red = _small_allreduce(_pack(small_parts, _rows_needed(red_shapes)), "reduce_small")
    (loss_v, g_memn, g_lb, g_f1n, g_mixn, g_gn, g_lng_full, g_lnb_full, g_ws, g_bs, g_f2n, g_fin) = _unpack(red, red_shapes)
    lsh = gmlp_ln_g.shape[1]
    g_lng = lax.dynamic_slice(g_lng_full, (0, chip * lsh), (1, lsh))
    g_lnb = lax.dynamic_slice(g_lnb_full, (0, chip * lsh), (1, lsh))
    small_w = [mem_norm, lb_logits, ffn1_norm, mix_norm, hgrn_gnorm, gmlp_ln_g, gmlp_ln_b, gmlp_w_s, gmlp_b_s, ffn2_norm, final_norm]
    small_g = [g_memn, g_lb, g_f1n, g_mixn, g_gn, g_lng, g_lnb, g_ws, g_bs, g_f2n, g_fin]
    small_m = [m_mem_norm, m_lb_logits, m_ffn1_norm, m_mix_norm, m_hgrn_gnorm, m_gmlp_ln_g, m_gmlp_ln_b, m_gmlp_w_s, m_gmlp_b_s, m_ffn2_norm, m_final_norm]
    small_v = [v_mem_norm, v_lb_logits, v_ffn1_norm, v_mix_norm, v_hgrn_gnorm, v_gmlp_ln_g, v_gmlp_ln_b, v_gmlp_w_s, v_gmlp_b_s, v_ffn2_norm, v_final_norm]
    sshapes = [w.shape for w in small_w]
    nrow = _rows_needed(sshapes)
    d_p, m_p, v_p = _adam_call("adam_small", _pack(small_w, nrow), _pack(small_g, nrow), _pack(small_m, nrow), _pack(small_v, nrow), nrow)
    s_delta, s_m, s_v = _unpack(d_p, sshapes), _unpack(m_p, sshapes), _unpack(v_p, sshapes)
    small_names = ["mem_norm", "lb_logits", "ffn1_norm", "mix_norm", "hgrn_gnorm", "gmlp_ln_g", "gmlp_ln_b", "gmlp_w_s", "gmlp_b_s", "ffn2_norm", "final_norm"]
    small_out = {nm: (g.reshape(w.shape), d, m, v) for nm, w, g, d, m, v in zip(small_names, small_w, small_g, s_delta, s_m, s_v)}

    order = ["mem_norm", "lb_logits", "ffn1_norm", "ffn1_w_in", "ffn1_w_out", "mix_norm", "mem_w_kv", "hgrn_w_in", "hgrn_gnorm",
             "hgrn_w_out", "gmlp_w_in", "gmlp_ln_g", "gmlp_ln_b", "gmlp_w_s", "gmlp_b_s", "gmlp_w_out", "ffn2_norm", "ffn2_w_in",
             "ffn2_w_out", "final_norm"]
    allo = {**big_out, **small_out}
    grad_x = dx0.reshape(x.shape)
    return (loss_v.reshape(()), grad_x, *[allo[n][0] for n in order], *[allo[n][1] for n in order],
            *[allo[n][2] for n in order], *[allo[n][3] for n in order])
```

```python
import functools

import jax
import jax.numpy as jnp
from jax import lax
from jax.experimental import pallas as pl
from jax.experimental.pallas import tpu as pltpu
from jax.experimental.pallas import tpu_sc as plsc

BF = jnp.bfloat16
F32 = jnp.float32
MESH = pl.DeviceIdType.MESH

EPS = 1e-6
D_MODEL = 1024
HG_HEADS = 8
HG_DIM = 128
HG_CHUNK = 64
GM_CHUNK = 128
GM_GROUPS = 8
GM_GROUP_DIM = 256
XA_HEADS = 4
XA_DIM = 256
ADAM_LR = 0.001
ADAM_B1 = 0.9
ADAM_B2 = 0.999
ADAM_EPS = 1e-08
ADAM_WD = 0.01
ADAM_STEP = 10

VMEM_CAP_BYTES = 60 * 1024 * 1024
LANES = 1024


def _pick(n, cap, mult=16):
    if n <= cap:
        return n
    for d in range(cap - cap % mult, 0, -mult):
        if n % d == 0:
            return d
    raise ValueError((n, cap, mult))


def _dg(a, b, ca, cb):
    return lax.dot_general(a.astype(BF), b.astype(BF), (((ca,), (cb,)), ((), ())), preferred_element_type=F32)


@jax.custom_vjp
def dot_nn(a, b):
    return _dg(a, b, 1, 0)


def _nn_fwd(a, b):
    return _dg(a, b, 1, 0), (a, b)


def _nn_bwd(r, g):
    a, b = r
    return _dg(g, b, 1, 1), _dg(a, g, 0, 0)


dot_nn.defvjp(_nn_fwd, _nn_bwd)


@jax.custom_vjp
def dot_nt(a, b):
    return _dg(a, b, 1, 1)


def _nt_fwd(a, b):
    return _dg(a, b, 1, 1), (a, b)


def _nt_bwd(r, g):
    a, b = r
    return _dg(g, b, 1, 0), _dg(g, a, 0, 0)


dot_nt.defvjp(_nt_fwd, _nt_bwd)


@jax.custom_vjp
def dot_tn(a, b):
    return _dg(a, b, 0, 0)


def _tn_fwd(a, b):
    return _dg(a, b, 0, 0), (a, b)


def _tn_bwd(r, g):
    a, b = r
    return _dg(b, g, 1, 1), _dg(a, g, 1, 0)


dot_tn.defvjp(_tn_fwd, _tn_bwd)


def _rmsnorm(x, g):
    return x * lax.rsqrt(jnp.mean(x * x, axis=-1, keepdims=True) + EPS) * g


def _silu(x):
    return x * jax.nn.sigmoid(x)


@jax.custom_vjp
def _gelu(x):
    return 0.5 * x * (1.0 + lax.erf(x * (0.5 ** 0.5)))


def _gelu_fwd(x):
    return _gelu(x), x


def _gelu_bwd(x, g):
    t = x * (0.5 ** 0.5)
    cdf = 0.5 * (1.0 + lax.erf(t))
    return (g * (cdf + x * (jnp.exp(-(t * t)) * (0.5 / 3.141592653589793) ** 0.5)),)


_gelu.defvjp(_gelu_fwd, _gelu_bwd)


def _softmax_last(s):
    m = lax.stop_gradient(jnp.max(s, axis=-1, keepdims=True))
    e = jnp.exp(s - m)
    return e / jnp.sum(e, axis=-1, keepdims=True)


def _tril(n):
    r = lax.broadcasted_iota(jnp.int32, (n, n), 0)
    c = lax.broadcasted_iota(jnp.int32, (n, n), 1)
    return r >= c


def _cumsum_rows(l):
    n = l.shape[0]
    return lax.dot_general(_tril(n).astype(F32), l, (((1,), (0,)), ((), ())),
                           precision=lax.Precision.HIGHEST, preferred_element_type=F32)


def _attention(zx, mk, mv):
    s = dot_nt(zx, mk) * (XA_DIM ** -0.5)
    return dot_nn(_softmax_last(s), mv)


def _hgrn_head(zq, zf, zi, zg, l0, l1, l2, gn, S):
    m = lax.stop_gradient(jnp.maximum(jnp.maximum(l0, l1), l2))
    e0 = jnp.exp(l0 - m)
    lb = e0 / (e0 + jnp.exp(l1 - m) + jnp.exp(l2 - m))
    q = _silu(zq)
    f = lb + (1.0 - lb) * jax.nn.sigmoid(zf)
    k = 1.0 - f
    b = _cumsum_rows(jnp.log(f))
    b_last = b[HG_CHUNK - 1:HG_CHUNK, :]
    q_dec = q * jnp.exp(b)
    k_inv = k * jnp.exp(-b)
    a = jnp.where(_tril(HG_CHUNK), dot_nt(q_dec, k_inv), 0.0)
    o = dot_nn(a, zi) + dot_nn(q_dec, S)
    S_new = jnp.exp(b_last).reshape(HG_DIM, 1) * S + dot_tn(k * jnp.exp(b_last - b), zi)
    o = _rmsnorm(o, gn) * _silu(zg)
    return o, S_new


def _hgrn_block(zq, zf, zi, zg, zx, l0, l1, l2, gn, mk, mv, S):
    outs, s_new = [], []
    for h in range(HG_HEADS):
        o, sn = _hgrn_head(zq[h], zf[h], zi[h], zg[h], l0[h], l1[h], l2[h], gn, S[h])
        outs.append(o)
        s_new.append(sn)
    for a in range(XA_HEADS):
        outs.append(_attention(zx[a], mk[a], mv[a]))
    return outs, s_new


def _gmlp_block(zu, zv, zx, lng, lnb, ws, bs, mk, mv):
    gv = [_gelu(v) for v in zv]
    width = GM_GROUPS * GM_GROUP_DIM
    mu = sum(jnp.sum(g, axis=-1, keepdims=True) for g in gv) / width
    xc = [g - mu for g in gv]
    var = sum(jnp.sum(c * c, axis=-1, keepdims=True) for c in xc) / width
    r = lax.rsqrt(var + EPS)
    outs = []
    for g in range(GM_GROUPS):
        v = xc[g] * r * lng[g] + lnb[g]
        w = jnp.where(_tril(GM_CHUNK), ws[g], 0.0)
        mixed = dot_nn(w, v) + bs[g].reshape(GM_CHUNK, 1)
        outs.append(_gelu(zu[g]) * mixed)
    for a in range(XA_HEADS):
        outs.append(_attention(zx[a], mk[a], mv[a]))
    return outs


def _rowcall(name, fn, rows, consts, row_outs, acc_outs, tr):
    nrows = rows[0][0].shape[0]
    tr = _pick(nrows, tr)
    n_r, n_c, n_ro, n_ao = len(rows), len(consts), len(row_outs), len(acc_outs)

    def kern(*refs):
        rv = [r[...] for r in refs[:n_r]]
        cv = [r[...] for r in refs[n_r:n_r + n_c]]
        ro_refs = refs[n_r + n_c:n_r + n_c + n_ro]
        ao_refs = refs[n_r + n_c + n_ro:]
        ro, ao = fn(rv, cv)
        for ref, v in zip(ro_refs, ro):
            ref[...] = v.astype(ref.dtype)
        if n_ao:
            @pl.when(pl.program_id(0) == 0)
            def _():
                for ref in ao_refs:
                    ref[...] = jnp.zeros(ref.shape, ref.dtype)

            for ref, v in zip(ao_refs, ao):
                ref[...] += v.astype(ref.dtype)

    in_specs = [pl.BlockSpec((tr, w), functools.partial(lambda i, cb: (i, cb), cb=cb)) for (_, cb, w) in rows]
    in_specs += [pl.BlockSpec(c.shape, lambda i: (0, 0)) for c in consts]
    out_specs = [pl.BlockSpec((tr, w), lambda i: (i, 0)) for (w, _) in row_outs]
    out_specs += [pl.BlockSpec(s, lambda i: (0, 0)) for (s, _) in acc_outs]
    out_shape = [jax.ShapeDtypeStruct((nrows, w), dt) for (w, dt) in row_outs]
    out_shape += [jax.ShapeDtypeStruct(s, dt) for (s, dt) in acc_outs]
    est = sum(tr * w * a.dtype.itemsize for (a, _, w) in rows) + sum(tr * w * jnp.dtype(dt).itemsize for (w, dt) in row_outs)
    est += sum(c.size * c.dtype.itemsize for c in consts)
    outs = pl.pallas_call(
        kern, grid=(nrows // tr,), in_specs=in_specs, out_specs=out_specs, out_shape=out_shape, name=name,
        compiler_params=pltpu.CompilerParams(dimension_semantics=("arbitrary",),
                                             vmem_limit_bytes=VMEM_CAP_BYTES),
    )(*[a for (a, _, _) in rows], *consts)
    return outs


def _mm(name, a, b, mode, out_dtype, tm, tn, tk, scale=1.0, res=None, a_lead=None, b_lead=None, norm_gain=None):
    ash = a.shape[-2:]
    bsh = b.shape[-2:]
    if mode == "nn":
        (M, K), (K2, N) = ash, bsh
    elif mode == "nt":
        (M, K), (N, K2) = ash, bsh
    else:
        (K, M), (K2, N) = ash, bsh
    assert K == K2, (name, a.shape, b.shape)
    tm, tn, tk = min(tm, M), min(tn, N), min(tk, K)
    assert M % tm == 0 and N % tn == 0 and K % tk == 0, (name, M, N, K, tm, tn, tk)
    nk = K // tk
    dims = {"nn": (1, 0), "nt": (1, 1), "tn": (0, 0)}[mode]

    def lead(spec_shape, index_fn, lead_idx):
        if lead_idx is None:
            return pl.BlockSpec(spec_shape, index_fn)
        return pl.BlockSpec((None,) + spec_shape, lambda i, j, k: (lead_idx,) + index_fn(i, j, k))

    if mode == "tn":
        a_spec = lead((tk, tm), lambda i, j, k: (k, i), a_lead)
    else:
        a_spec = lead((tm, tk), lambda i, j, k: (i, k), a_lead)
    if mode == "nt":
        b_spec = lead((tn, tk), lambda i, j, k: (j, k), b_lead)
    else:
        b_spec = lead((tk, tn), lambda i, j, k: (k, j), b_lead)
    o_spec = pl.BlockSpec((tm, tn), lambda i, j, k: (i, j))
    has_res = res is not None
    has_norm = norm_gain is not None
    assert not has_norm or tn == N

    def kern(*refs):
        a_ref, b_ref = refs[0], refs[1]
        pos = 2
        res_ref = gain_ref = h_ref = None
        if has_res:
            res_ref, pos = refs[pos], pos + 1
        if has_norm:
            gain_ref, pos = refs[pos], pos + 1
        o_ref, pos = refs[pos], pos + 1
        if has_norm:
            h_ref = refs[pos]
        acc_ref = refs[-1] if nk > 1 else None
        p = lax.dot_general(a_ref[...].astype(BF), b_ref[...].astype(BF), (((dims[0],), (dims[1],)), ((), ())),
                            preferred_element_type=F32)

        def finish(v):
            if scale != 1.0:
                v = v * scale
            if has_res:
                v = res_ref[...] + v
            o_ref[...] = v.astype(o_ref.dtype)
            if has_norm:
                h_ref[...] = _rmsnorm(v, gain_ref[...]).astype(h_ref.dtype)

        if nk == 1:
            finish(p)
        else:
            k = pl.program_id(2)

            @pl.when(k == 0)
            def _():
                acc_ref[...] = p

            @pl.when(k > 0)
            def _():
                acc_ref[...] += p

            @pl.when(k == nk - 1)
            def _():
                finish(acc_ref[...])

    ins = [a, b] + ([res] if has_res else []) + ([norm_gain] if has_norm else [])
    in_specs = [a_spec, b_spec] + ([o_spec] if has_res else [])
    in_specs += [pl.BlockSpec((1, N), lambda i, j, k: (0, 0))] if has_norm else []
    out_sd = jax.ShapeDtypeStruct((M, N), out_dtype)
    return pl.pallas_call(
        kern, grid=(M // tm, N // tn, nk), in_specs=in_specs,
        out_specs=[o_spec, o_spec] if has_norm else o_spec,
        out_shape=[out_sd, jax.ShapeDtypeStruct((M, N), BF)] if has_norm else out_sd,
        scratch_shapes=[pltpu.VMEM((tm, tn), F32)] if nk > 1 else [],
        name=name,
        compiler_params=pltpu.CompilerParams(dimension_semantics=("parallel", "parallel", "arbitrary"),
                                             vmem_limit_bytes=VMEM_CAP_BYTES),
    )(*ins)


def _ffn_in_swiglu(name, h, w3, tm, tn):
    T, D = h.shape
    dff = w3.shape[2] // 2
    tm = min(tm, T)
    assert T % tm == 0 and dff % tn == 0
    nj = dff // tn

    def kern(h_ref, wg_ref, wu_ref, zg_ref, zu_ref, a_ref):
        hb = h_ref[...]
        g = jnp.dot(hb, wg_ref[...], preferred_element_type=F32).astype(BF)
        u = jnp.dot(hb, wu_ref[...], preferred_element_type=F32).astype(BF)
        zg_ref[...] = g
        zu_ref[...] = u
        a_ref[...] = (_silu(g.astype(F32)) * u.astype(F32)).astype(BF)

    o_spec = pl.BlockSpec((tm, tn), lambda i, j: (i, j))
    return pl.pallas_call(
        kern, grid=(T // tm, nj),
        in_specs=[pl.BlockSpec((tm, D), lambda i, j: (i, 0)),
                  pl.BlockSpec((None, D, tn), lambda i, j: (0, 0, j)),
                  pl.BlockSpec((None, D, tn), lambda i, j: (0, 0, j + nj))],
        out_specs=[o_spec, o_spec, o_spec],
        out_shape=[jax.ShapeDtypeStruct((T, dff), BF)] * 3, name=name,
        compiler_params=pltpu.CompilerParams(dimension_semantics=("parallel", "arbitrary"),
                                             vmem_limit_bytes=VMEM_CAP_BYTES),
    )(h, w3, w3)


def _ffn_da_swiglu(name, dxo, w3, zg, zu, tm):
    T, D = dxo.shape
    dff = w3.shape[1]
    tm = min(tm, T)
    assert T % tm == 0 and dff % 2 == 0
    hc = dff // 2

    def kern(d_ref, w_ref, g_ref, u_ref, dz_ref):
        db = (d_ref[...] * 0.5).astype(BF)
        for s in range(2):
            cols = slice(s * hc, (s + 1) * hc)
            da = lax.dot_general(db, w_ref[cols, :], (((1,), (1,)), ((), ())), preferred_element_type=F32)
            g = g_ref[:, cols].astype(F32)
            sg = 1.0 / (1.0 + jnp.exp(-g))
            gs = g * sg
            dab = da.astype(BF)
            dz_ref[:, cols] = (dab * u_ref[:, cols]) * (sg + gs * (1.0 - sg)).astype(BF)
            dz_ref[:, dff + s * hc:dff + (s + 1) * hc] = dab * gs.astype(BF)

    row = lambda w: pl.BlockSpec((tm, w), lambda i: (i, 0))
    return pl.pallas_call(
        kern, grid=(T // tm,),
        in_specs=[row(D), pl.BlockSpec((None, dff, D), lambda i: (0, 0, 0), pipeline_mode=pl.Buffered(1)), row(dff), row(dff)],
        out_specs=row(2 * dff), out_shape=jax.ShapeDtypeStruct((T, 2 * dff), BF), name=name,
        compiler_params=pltpu.CompilerParams(dimension_semantics=("arbitrary",), vmem_limit_bytes=VMEM_CAP_BYTES),
    )(dxo, w3, zg, zu)


def _mm_dh_rms(name, dz, w3, xin, g, dres, tm):
    T, K = dz.shape
    D = w3.shape[1]
    tm = min(tm, T)
    assert T % tm == 0

    def kern(dz_ref, w_ref, x_ref, g_ref, r_ref, dx_ref, dg_ref):
        dh = lax.dot_general(dz_ref[...], w_ref[...], (((1,), (1,)), ((), ())), preferred_element_type=F32)
        _, vjp = jax.vjp(_rmsnorm, x_ref[...], g_ref[...])
        dx, dg = vjp(dh)
        dx_ref[...] = dx + r_ref[...]

        @pl.when(pl.program_id(0) == 0)
        def _():
            dg_ref[...] = jnp.zeros(dg_ref.shape, F32)

        dg_ref[...] += dg

    row = lambda w: pl.BlockSpec((tm, w), lambda i: (i, 0))
    one = pl.BlockSpec((1, D), lambda i: (0, 0))
    return pl.pallas_call(
        kern, grid=(T // tm,),
        in_specs=[row(K), pl.BlockSpec((None, D, K), lambda i: (0, 0, 0), pipeline_mode=pl.Buffered(1)), row(D), one, row(D)],
        out_specs=[row(D), one], out_shape=[jax.ShapeDtypeStruct((T, D), F32), jax.ShapeDtypeStruct((1, D), F32)], name=name,
        compiler_params=pltpu.CompilerParams(dimension_semantics=("arbitrary",), vmem_limit_bytes=VMEM_CAP_BYTES),
    )(dz, w3, xin, g, dres)


def _mm_tn_pair(name, a, b, kind, c_arr, tq, tk, scale=1.0):
    T, M = a.shape
    _, N = b.shape
    tk = min(tk, T)
    assert T % tk == 0
    nk = T // tk
    if kind == "col":
        hm = M // 2
        assert N % tq == 0
        nq = N // tq
        tile = (hm, tq)
        a_spec = pl.BlockSpec((tk, hm), lambda h, q, k, c: (k, jnp.bitwise_xor(h, 1 - c[0])))
        b_spec = pl.BlockSpec((tk, tq), lambda h, q, k, c: (k, q))
        o_spec = pl.BlockSpec(tile, lambda h, q, k, c: (0, q * h))
        out_sd = (hm, N)
    else:
        hn = N // 2
        assert M % tq == 0
        nq = M // tq
        tile = (tq, hn)
        a_spec = pl.BlockSpec((tk, tq), lambda h, q, k, c: (k, q))
        b_spec = pl.BlockSpec((tk, hn), lambda h, q, k, c: (k, jnp.bitwise_xor(h, 1 - c[0])))
        o_spec = pl.BlockSpec(tile, lambda h, q, k, c: (q * h, 0))
        out_sd = (M, hn)

    def kern(c_ref, a_ref, b_ref, o_ref, acc, stage, recv, ssem, rsem):
        h, q, k = pl.program_id(0), pl.program_id(1), pl.program_id(2)
        x, y, c, _ = _place()
        p = lax.dot_general(a_ref[...].astype(BF), b_ref[...].astype(BF), (((0,), (0,)), ((), ())), preferred_element_type=F32)

        @pl.when(k == 0)
        def _():
            acc[...] = p

        @pl.when(k > 0)
        def _():
            acc[...] += p

        def send(slot, qq):
            return pltpu.make_async_remote_copy(src_ref=stage.at[slot], dst_ref=recv.at[qq], send_sem=ssem.at[slot],
                                                recv_sem=rsem.at[qq], device_id=(x, y, 1 - c), device_id_type=MESH)

        last = k == nk - 1

        @pl.when(jnp.logical_and(last, h == 0))
        def _():
            slot = q % 2

            @pl.when(q >= 2)
            def _():
                send(slot, q).wait_send()

            stage[slot] = (acc[...] * scale).astype(BF)
            send(slot, q).start()

        @pl.when(jnp.logical_and(last, h == 1))
        def _():
            @pl.when(q == 0)
            def _():
                for s in range(min(nq, 2)):
                    send(s, 0).wait_send()

            send(0, q).wait_recv()
            o_ref[...] = (acc[...] * scale + recv[q].astype(F32)).astype(o_ref.dtype)

    tb = tile[0] * tile[1]
    est = tb * (4 + 2 * 2 + nq * 2 + 2 * 2) + 2 * tk * (a_spec.block_shape[1] + b_spec.block_shape[1]) * 2 * 2
    return pl.pallas_call(
        kern,
        grid_spec=pltpu.PrefetchScalarGridSpec(
            num_scalar_prefetch=1, grid=(2, nq, nk), in_specs=[a_spec, b_spec], out_specs=o_spec,
            scratch_shapes=[pltpu.VMEM(tile, F32), pltpu.VMEM((2,) + tile, BF), pltpu.VMEM((nq,) + tile, BF),
                            pltpu.SemaphoreType.DMA((2,)), pltpu.SemaphoreType.DMA((nq,))]),
        out_shape=jax.ShapeDtypeStruct(out_sd, BF), name=name,
        compiler_params=pltpu.CompilerParams(dimension_semantics=("arbitrary", "arbitrary", "arbitrary"),
                                             vmem_limit_bytes=VMEM_CAP_BYTES),
    )(c_arr, a, b)


def _hgrn_pieces(z_ref):
    W = HG_HEADS * HG_DIM
    zq = [z_ref[:, h * HG_DIM:(h + 1) * HG_DIM] for h in range(HG_HEADS)]
    zf = [z_ref[:, W + h * HG_DIM:W + (h + 1) * HG_DIM] for h in range(HG_HEADS)]
    zi = [z_ref[:, 2 * W + h * HG_DIM:2 * W + (h + 1) * HG_DIM] for h in range(HG_HEADS)]
    zg = [z_ref[:, 3 * W + h * HG_DIM:3 * W + (h + 1) * HG_DIM] for h in range(HG_HEADS)]
    zx = [z_ref[:, 4 * W + a * XA_DIM:4 * W + (a + 1) * XA_DIM] for a in range(XA_HEADS)]
    return zq, zf, zi, zg, zx


def _kv_pieces(kv_ref):
    W = XA_HEADS * XA_DIM
    mk = [kv_ref[:, a * XA_DIM:(a + 1) * XA_DIM] for a in range(XA_HEADS)]
    mv = [kv_ref[:, W + a * XA_DIM:W + (a + 1) * XA_DIM] for a in range(XA_HEADS)]
    return mk, mv


def _lb_pieces(lb_ref):
    return [[lb_ref[r:r + 1, h * HG_DIM:(h + 1) * HG_DIM] for h in range(HG_HEADS)] for r in range(3)]


def _hgrn_fwd(z, lb_logits, gnorm, kv, bl, nc):
    T, zw = z.shape
    mem_len = kv.shape[0] // bl
    cat_w = HG_HEADS * HG_DIM + XA_HEADS * XA_DIM

    def kern(z_ref, lb_ref, gn_ref, kv_ref, cat_ref, st_ref, s_scr):
        @pl.when(pl.program_id(1) == 0)
        def _():
            s_scr[...] = jnp.zeros(s_scr.shape, F32)

        st_ref[...] = s_scr[...]
        zq, zf, zi, zg, zx = _hgrn_pieces(z_ref)
        mk, mv = _kv_pieces(kv_ref)
        l0, l1, l2 = _lb_pieces(lb_ref)
        S = [s_scr[h] for h in range(HG_HEADS)]
        outs, s_new = _hgrn_block(zq, zf, zi, zg, zx, l0, l1, l2, gn_ref[...], mk, mv, S)
        for h in range(HG_HEADS):
            cat_ref[:, h * HG_DIM:(h + 1) * HG_DIM] = outs[h].astype(cat_ref.dtype)
            s_scr[h] = s_new[h]
        base = HG_HEADS * HG_DIM
        for a in range(XA_HEADS):
            cat_ref[:, base + a * XA_DIM:base + (a + 1) * XA_DIM] = outs[HG_HEADS + a].astype(cat_ref.dtype)

    return pl.pallas_call(
        kern, grid=(bl, nc),
        in_specs=[pl.BlockSpec((HG_CHUNK, zw), lambda b, n: (b * nc + n, 0)),
                  pl.BlockSpec(lb_logits.shape, lambda b, n: (0, 0)),
                  pl.BlockSpec(gnorm.shape, lambda b, n: (0, 0)),
                  pl.BlockSpec((mem_len, kv.shape[1]), lambda b, n: (b, 0))],
        out_specs=[pl.BlockSpec((HG_CHUNK, cat_w), lambda b, n: (b * nc + n, 0)),
                   pl.BlockSpec((None, HG_HEADS, HG_DIM, HG_DIM), lambda b, n: (b * nc + n, 0, 0, 0))],
        out_shape=[jax.ShapeDtypeStruct((T, cat_w), BF),
                   jax.ShapeDtypeStruct((bl * nc, HG_HEADS, HG_DIM, HG_DIM), F32)],
        scratch_shapes=[pltpu.VMEM((HG_HEADS, HG_DIM, HG_DIM), F32)],
        name="hgrn_fwd",
        compiler_params=pltpu.CompilerParams(dimension_semantics=("arbitrary", "arbitrary"), vmem_limit_bytes=VMEM_CAP_BYTES),
    )(z, lb_logits, gnorm, kv)


def _hgrn_bwd(z, dcat, stash, lb_logits, gnorm, kv, bl, nc):
    T, zw = z.shape
    mem_len = kv.shape[0] // bl
    cat_w = dcat.shape[1]

    def kern(z_ref, dc_ref, st_ref, lb_ref, gn_ref, kv_ref, dz_ref, dkv_ref, dlb_ref, dgn_ref, ds_scr):
        first = jnp.logical_and(pl.program_id(0) == 0, pl.program_id(1) == 0)

        @pl.when(pl.program_id(1) == 0)
        def _():
            ds_scr[...] = jnp.zeros(ds_scr.shape, F32)
            dkv_ref[...] = jnp.zeros(dkv_ref.shape, F32)

        @pl.when(first)
        def _():
            dlb_ref[...] = jnp.zeros(dlb_ref.shape, F32)
            dgn_ref[...] = jnp.zeros(dgn_ref.shape, F32)

        zq, zf, zi, zg, zx = _hgrn_pieces(z_ref)
        mk, mv = _kv_pieces(kv_ref)
        l0, l1, l2 = _lb_pieces(lb_ref)
        S = [st_ref[h] for h in range(HG_HEADS)]
        _, vjp = jax.vjp(_hgrn_block, zq, zf, zi, zg, zx, l0, l1, l2, gn_ref[...], mk, mv, S)
        d_outs = [dc_ref[:, h * HG_DIM:(h + 1) * HG_DIM] for h in range(HG_HEADS)]
        base = HG_HEADS * HG_DIM
        d_outs += [dc_ref[:, base + a * XA_DIM:base + (a + 1) * XA_DIM] for a in range(XA_HEADS)]
        d_s = [ds_scr[h] for h in range(HG_HEADS)]
        dzq, dzf, dzi, dzg, dzx, dl0, dl1, dl2, dgn, dmk, dmv, dS = vjp((d_outs, d_s))
        W = HG_HEADS * HG_DIM
        for h in range(HG_HEADS):
            sl = slice(h * HG_DIM, (h + 1) * HG_DIM)
            dz_ref[:, sl] = dzq[h].astype(dz_ref.dtype)
            dz_ref[:, W + h * HG_DIM:W + (h + 1) * HG_DIM] = dzf[h].astype(dz_ref.dtype)
            dz_ref[:, 2 * W + h * HG_DIM:2 * W + (h + 1) * HG_DIM] = dzi[h].astype(dz_ref.dtype)
            dz_ref[:, 3 * W + h * HG_DIM:3 * W + (h + 1) * HG_DIM] = dzg[h].astype(dz_ref.dtype)
            ds_scr[h] = dS[h]
            dlb_ref[0:1, sl] += dl0[h]
            dlb_ref[1:2, sl] += dl1[h]
            dlb_ref[2:3, sl] += dl2[h]
        dgn_ref[...] += dgn
        KW = XA_HEADS * XA_DIM
        for a in range(XA_HEADS):
            dz_ref[:, 4 * W + a * XA_DIM:4 * W + (a + 1) * XA_DIM] = dzx[a].astype(dz_ref.dtype)
            dkv_ref[:, a * XA_DIM:(a + 1) * XA_DIM] += dmk[a]
            dkv_ref[:, KW + a * XA_DIM:KW + (a + 1) * XA_DIM] += dmv[a]

    rev = lambda b, n: (b * nc + (nc - 1 - n), 0)
    return pl.pallas_call(
        kern, grid=(bl, nc),
        in_specs=[pl.BlockSpec((HG_CHUNK, zw), rev),
                  pl.BlockSpec((HG_CHUNK, cat_w), rev),
                  pl.BlockSpec((None, HG_HEADS, HG_DIM, HG_DIM), lambda b, n: (b * nc + (nc - 1 - n), 0, 0, 0)),
                  pl.BlockSpec(lb_logits.shape, lambda b, n: (0, 0)),
                  pl.BlockSpec(gnorm.shape, lambda b, n: (0, 0)),
                  pl.BlockSpec((mem_len, kv.shape[1]), lambda b, n: (b, 0))],
        out_specs=[pl.BlockSpec((HG_CHUNK, zw), rev),
                   pl.BlockSpec((mem_len, kv.shape[1]), lambda b, n: (b, 0)),
                   pl.BlockSpec(lb_logits.shape, lambda b, n: (0, 0)),
                   pl.BlockSpec(gnorm.shape, lambda b, n: (0, 0))],
        out_shape=[jax.ShapeDtypeStruct((T, zw), BF), jax.ShapeDtypeStruct(kv.shape, F32),
                   jax.ShapeDtypeStruct(lb_logits.shape, F32), jax.ShapeDtypeStruct(gnorm.shape, F32)],
        scratch_shapes=[pltpu.VMEM((HG_HEADS, HG_DIM, HG_DIM), F32)],
        name="hgrn_bwd",
        compiler_params=pltpu.CompilerParams(dimension_semantics=("arbitrary", "arbitrary"), vmem_limit_bytes=VMEM_CAP_BYTES),
    )(z, dcat, stash, lb_logits, gnorm, kv)


HG_SUB = 4


def _hgrn_rows(z_ref, dtype_cast=None):
    W = HG_HEADS * HG_DIM

    def piece(c, col, w):
        return z_ref[c * HG_CHUNK:(c + 1) * HG_CHUNK, col:col + w]

    zq = [[piece(c, h * HG_DIM, HG_DIM) for h in range(HG_HEADS)] for c in range(HG_SUB)]
    zf = [[piece(c, W + h * HG_DIM, HG_DIM) for h in range(HG_HEADS)] for c in range(HG_SUB)]
    zi = [[piece(c, 2 * W + h * HG_DIM, HG_DIM) for h in range(HG_HEADS)] for c in range(HG_SUB)]
    zg = [[piece(c, 3 * W + h * HG_DIM, HG_DIM) for h in range(HG_HEADS)] for c in range(HG_SUB)]
    zx = [z_ref[:, 4 * W + a * XA_DIM:4 * W + (a + 1) * XA_DIM] for a in range(XA_HEADS)]
    return zq, zf, zi, zg, zx


def _hgrn_steps(zq, zf, zi, zg, zx, l0, l1, l2, gn, mk, mv, S):
    mix = []
    for c in range(HG_SUB):
        row, s_next = [], []
        for h in range(HG_HEADS):
            o, sn = _hgrn_head(zq[c][h], zf[c][h], zi[c][h], zg[c][h], l0[h], l1[h], l2[h], gn, S[h])
            row.append(o)
            s_next.append(sn)
        mix.append(row)
        S = s_next
    att = [_attention(zx[a], mk[a], mv[a]) for a in range(XA_HEADS)]
    return mix, att, S


def _hgrn_fwd2(z, lb_logits, gnorm, kv, bl, seq):
    T, zw = z.shape
    mem_len = kv.shape[0] // bl
    cat_w = HG_HEADS * HG_DIM + XA_HEADS * XA_DIM
    R = HG_SUB * HG_CHUNK
    nb = seq // R

    def kern(z_ref, lb_ref, gn_ref, kv_ref, cat_ref, st_ref, s_scr):
        @pl.when(pl.program_id(1) == 0)
        def _():
            s_scr[...] = jnp.zeros(s_scr.shape, F32)

        st_ref[...] = s_scr[...]
        zq, zf, zi, zg, zx = _hgrn_rows(z_ref)
        mk, mv = _kv_pieces(kv_ref)
        l0, l1, l2 = _lb_pieces(lb_ref)
        S = [s_scr[h] for h in range(HG_HEADS)]
        mix, att, s_new = _hgrn_steps(zq, zf, zi, zg, zx, l0, l1, l2, gn_ref[...], mk, mv, S)
        for c in range(HG_SUB):
            for h in range(HG_HEADS):
                cat_ref[c * HG_CHUNK:(c + 1) * HG_CHUNK, h * HG_DIM:(h + 1) * HG_DIM] = mix[c][h].astype(cat_ref.dtype)
        for h in range(HG_HEADS):
            s_scr[h] = s_new[h]
        base = HG_HEADS * HG_DIM
        for a in range(XA_HEADS):
            cat_ref[:, base + a * XA_DIM:base + (a + 1) * XA_DIM] = att[a].astype(cat_ref.dtype)

    return pl.pallas_call(
        kern, grid=(bl, nb),
        in_specs=[pl.BlockSpec((R, zw), lambda b, n: (b * nb + n, 0)),
                  pl.BlockSpec(lb_logits.shape, lambda b, n: (0, 0)),
                  pl.BlockSpec(gnorm.shape, lambda b, n: (0, 0)),
                  pl.BlockSpec((mem_len, kv.shape[1]), lambda b, n: (b, 0))],
        out_specs=[pl.BlockSpec((R, cat_w), lambda b, n: (b * nb + n, 0)),
                   pl.BlockSpec((None, HG_HEADS, HG_DIM, HG_DIM), lambda b, n: (b * nb + n, 0, 0, 0))],
        out_shape=[jax.ShapeDtypeStruct((T, cat_w), BF),
                   jax.ShapeDtypeStruct((bl * nb, HG_HEADS, HG_DIM, HG_DIM), F32)],
        scratch_shapes=[pltpu.VMEM((HG_HEADS, HG_DIM, HG_DIM), F32)],
        name="hgrn_fwd",
        compiler_params=pltpu.CompilerParams(dimension_semantics=("arbitrary", "arbitrary"), vmem_limit_bytes=VMEM_CAP_BYTES),
    )(z, lb_logits, gnorm, kv)


def _hgrn_bwd2(z, dcat, stash, lb_logits, gnorm, kv, bl, seq):
    T, zw = z.shape
    mem_len = kv.shape[0] // bl
    cat_w = dcat.shape[1]
    R = HG_SUB * HG_CHUNK
    nb = seq // R

    def kern(z_ref, dc_ref, st_ref, lb_ref, gn_ref, kv_ref, dz_ref, dkv_ref, dlb_ref, dgn_ref, ds_scr):
        first = jnp.logical_and(pl.program_id(0) == 0, pl.program_id(1) == 0)

        @pl.when(pl.program_id(1) == 0)
        def _():
            ds_scr[...] = jnp.zeros(ds_scr.shape, F32)
            dkv_ref[...] = jnp.zeros(dkv_ref.shape, F32)

        @pl.when(first)
        def _():
            dlb_ref[...] = jnp.zeros(dlb_ref.shape, F32)
            dgn_ref[...] = jnp.zeros(dgn_ref.shape, F32)

        zq, zf, zi, zg, zx = _hgrn_rows(z_ref)
        mk, mv = _kv_pieces(kv_ref)
        l0, l1, l2 = _lb_pieces(lb_ref)
        S = [st_ref[h] for h in range(HG_HEADS)]
        _, vjp = jax.vjp(_hgrn_steps, zq, zf, zi, zg, zx, l0, l1, l2, gn_ref[...], mk, mv, S)
        d_mix = [[dc_ref[c * HG_CHUNK:(c + 1) * HG_CHUNK, h * HG_DIM:(h + 1) * HG_DIM] for h in range(HG_HEADS)]
                 for c in range(HG_SUB)]
        base = HG_HEADS * HG_DIM
        d_att = [dc_ref[:, base + a * XA_DIM:base + (a + 1) * XA_DIM] for a in range(XA_HEADS)]
        d_s = [ds_scr[h] for h in range(HG_HEADS)]
        dzq, dzf, dzi, dzg, dzx, dl0, dl1, dl2, dgn, dmk, dmv, dS = vjp((d_mix, d_att, d_s))
        W = HG_HEADS * HG_DIM
        for c in range(HG_SUB):
            rows = slice(c * HG_CHUNK, (c + 1) * HG_CHUNK)
            for h in range(HG_HEADS):
                for k, part in enumerate((dzq, dzf, dzi, dzg)):
                    dz_ref[rows, k * W + h * HG_DIM:k * W + (h + 1) * HG_DIM] = part[c][h].astype(dz_ref.dtype)
        for h in range(HG_HEADS):
            sl = slice(h * HG_DIM, (h + 1) * HG_DIM)
            ds_scr[h] = dS[h]
            dlb_ref[0:1, sl] += dl0[h]
            dlb_ref[1:2, sl] += dl1[h]
            dlb_ref[2:3, sl] += dl2[h]
        dgn_ref[...] += dgn
        KW = XA_HEADS * XA_DIM
        for a in range(XA_HEADS):
            dz_ref[:, 4 * W + a * XA_DIM:4 * W + (a + 1) * XA_DIM] = dzx[a].astype(dz_ref.dtype)
            dkv_ref[:, a * XA_DIM:(a + 1) * XA_DIM] += dmk[a]
            dkv_ref[:, KW + a * XA_DIM:KW + (a + 1) * XA_DIM] += dmv[a]

    rev = lambda b, n: (b * nb + (nb - 1 - n), 0)
    return pl.pallas_call(
        kern, grid=(bl, nb),
        in_specs=[pl.BlockSpec((R, zw), rev),
                  pl.BlockSpec((R, cat_w), rev),
                  pl.BlockSpec((None, HG_HEADS, HG_DIM, HG_DIM), lambda b, n: (b * nb + (nb - 1 - n), 0, 0, 0)),
                  pl.BlockSpec(lb_logits.shape, lambda b, n: (0, 0)),
                  pl.BlockSpec(gnorm.shape, lambda b, n: (0, 0)),
                  pl.BlockSpec((mem_len, kv.shape[1]), lambda b, n: (b, 0))],
        out_specs=[pl.BlockSpec((R, zw), rev),
                   pl.BlockSpec((mem_len, kv.shape[1]), lambda b, n: (b, 0)),
                   pl.BlockSpec(lb_logits.shape, lambda b, n: (0, 0)),
                   pl.BlockSpec(gnorm.shape, lambda b, n: (0, 0))],
        out_shape=[jax.ShapeDtypeStruct((T, zw), BF), jax.ShapeDtypeStruct(kv.shape, F32),
                   jax.ShapeDtypeStruct(lb_logits.shape, F32), jax.ShapeDtypeStruct(gnorm.shape, F32)],
        scratch_shapes=[pltpu.VMEM((HG_HEADS, HG_DIM, HG_DIM), F32)],
        name="hgrn_bwd",
        compiler_params=pltpu.CompilerParams(dimension_semantics=("arbitrary", "arbitrary"), vmem_limit_bytes=VMEM_CAP_BYTES),
    )(z, dcat, stash, lb_logits, gnorm, kv)


GM_SUB = 2


def _gmlp_pieces(z_ref):
    W = GM_GROUPS * GM_GROUP_DIM
    zu = [z_ref[:, g * GM_GROUP_DIM:(g + 1) * GM_GROUP_DIM] for g in range(GM_GROUPS)]
    zv = [z_ref[:, W + g * GM_GROUP_DIM:W + (g + 1) * GM_GROUP_DIM] for g in range(GM_GROUPS)]
    zx = [z_ref[:, 2 * W + a * XA_DIM:2 * W + (a + 1) * XA_DIM] for a in range(XA_HEADS)]
    return zu, zv, zx


def _gmlp_params(lng_ref, lnb_ref, ws_ref, bs_ref):
    lng = [lng_ref[:, g * GM_GROUP_DIM:(g + 1) * GM_GROUP_DIM] for g in range(GM_GROUPS)]
    lnb = [lnb_ref[:, g * GM_GROUP_DIM:(g + 1) * GM_GROUP_DIM] for g in range(GM_GROUPS)]
    ws = [ws_ref[g] for g in range(GM_GROUPS)]
    bs = [bs_ref[g:g + 1, :] for g in range(GM_GROUPS)]
    return lng, lnb, ws, bs


def _gmlp_fwd(z, ln_g, ln_b, w_s, b_s, kv, bl, nc):
    T, zw = z.shape
    mem_len = kv.shape[0] // bl
    cat_w = GM_GROUPS * GM_GROUP_DIM + XA_HEADS * XA_DIM

    assert nc % GM_SUB == 0
    nc = nc // GM_SUB
    R = GM_SUB * GM_CHUNK

    def kern(z_ref, lng_ref, lnb_ref, ws_ref, bs_ref, kv_ref, cat_ref):
        lng, lnb, ws, bs = _gmlp_params(lng_ref, lnb_ref, ws_ref, bs_ref)
        mk, mv = _kv_pieces(kv_ref)
        for c in range(GM_SUB):
            rows = pl.ds(c * GM_CHUNK, GM_CHUNK)
            zu, zv, zx = _gmlp_pieces(z_ref.at[rows])
            out = cat_ref.at[rows]
            outs = _gmlp_block(zu, zv, zx, lng, lnb, ws, bs, mk, mv)
            for g in range(GM_GROUPS):
                out[:, g * GM_GROUP_DIM:(g + 1) * GM_GROUP_DIM] = outs[g].astype(cat_ref.dtype)
            base = GM_GROUPS * GM_GROUP_DIM
            for a in range(XA_HEADS):
                out[:, base + a * XA_DIM:base + (a + 1) * XA_DIM] = outs[GM_GROUPS + a].astype(cat_ref.dtype)

    full2 = lambda b, n: (0, 0)
    return pl.pallas_call(
        kern, grid=(bl, nc),
        in_specs=[pl.BlockSpec((R, zw), lambda b, n: (b * nc + n, 0)),
                  pl.BlockSpec(ln_g.shape, full2), pl.BlockSpec(ln_b.shape, full2),
                  pl.BlockSpec(w_s.shape, lambda b, n: (0, 0, 0)), pl.BlockSpec(b_s.shape, full2),
                  pl.BlockSpec((mem_len, kv.shape[1]), lambda b, n: (b, 0))],
        out_specs=pl.BlockSpec((R, cat_w), lambda b, n: (b * nc + n, 0)),
        out_shape=jax.ShapeDtypeStruct((T, cat_w), BF),
        name="gmlp_fwd",
        compiler_params=pltpu.CompilerParams(dimension_semantics=("arbitrary", "arbitrary"), vmem_limit_bytes=VMEM_CAP_BYTES),
    )(z, ln_g, ln_b, w_s, b_s, kv)


def _gmlp_bwd(z, dcat, ln_g, ln_b, w_s, b_s, kv, bl, nc):
    T, zw = z.shape
    mem_len = kv.shape[0] // bl
    cat_w = dcat.shape[1]
    assert nc % GM_SUB == 0
    nc = nc // GM_SUB

    def kern(z_ref, dc_ref, lng_ref, lnb_ref, ws_ref, bs_ref, kv_ref,
             dz_ref, dkv_ref, dlng_ref, dlnb_ref, dws_ref, dbs_ref):
        first = jnp.logical_and(pl.program_id(0) == 0, pl.program_id(1) == 0)

        @pl.when(pl.program_id(1) == 0)
        def _():
            dkv_ref[...] = jnp.zeros(dkv_ref.shape, F32)

        @pl.when(first)
        def _():
            dlng_ref[...] = jnp.zeros(dlng_ref.shape, F32)
            dlnb_ref[...] = jnp.zeros(dlnb_ref.shape, F32)
            dws_ref[...] = jnp.zeros(dws_ref.shape, F32)
            dbs_ref[...] = jnp.zeros(dbs_ref.shape, F32)

        lng, lnb, ws, bs = _gmlp_params(lng_ref, lnb_ref, ws_ref, bs_ref)
        mk, mv = _kv_pieces(kv_ref)
        W = GM_GROUPS * GM_GROUP_DIM
        KW = XA_HEADS * XA_DIM
        for c in range(GM_SUB):
            rows = pl.ds(c * GM_CHUNK, GM_CHUNK)
            zu, zv, zx = _gmlp_pieces(z_ref.at[rows])
            dc, dz = dc_ref.at[rows], dz_ref.at[rows]
            _, vjp = jax.vjp(_gmlp_block, zu, zv, zx, lng, lnb, ws, bs, mk, mv)
            d_outs = [dc[:, g * GM_GROUP_DIM:(g + 1) * GM_GROUP_DIM] for g in range(GM_GROUPS)]
            d_outs += [dc[:, W + a * XA_DIM:W + (a + 1) * XA_DIM] for a in range(XA_HEADS)]
            dzu, dzv, dzx, dlng, dlnb, dws, dbs, dmk, dmv = vjp(d_outs)
            for g in range(GM_GROUPS):
                sl = slice(g * GM_GROUP_DIM, (g + 1) * GM_GROUP_DIM)
                dz[:, sl] = dzu[g].astype(dz_ref.dtype)
                dz[:, W + g * GM_GROUP_DIM:W + (g + 1) * GM_GROUP_DIM] = dzv[g].astype(dz_ref.dtype)
                dlng_ref[:, sl] += dlng[g]
                dlnb_ref[:, sl] += dlnb[g]
                dws_ref[g] += dws[g]
                dbs_ref[g:g + 1, :] += dbs[g]
            for a in range(XA_HEADS):
                dz[:, 2 * W + a * XA_DIM:2 * W + (a + 1) * XA_DIM] = dzx[a].astype(dz_ref.dtype)
                dkv_ref[:, a * XA_DIM:(a + 1) * XA_DIM] += dmk[a]
                dkv_ref[:, KW + a * XA_DIM:KW + (a + 1) * XA_DIM] += dmv[a]

    full2 = lambda b, n: (0, 0)
    full3 = lambda b, n: (0, 0, 0)
    blk = lambda b, n: (b * nc + n, 0)
    return pl.pallas_call(
        kern, grid=(bl, nc),
        in_specs=[pl.BlockSpec((GM_SUB * GM_CHUNK, zw), blk), pl.BlockSpec((GM_SUB * GM_CHUNK, cat_w), blk),
                  pl.BlockSpec(ln_g.shape, full2), pl.BlockSpec(ln_b.shape, full2),
                  pl.BlockSpec(w_s.shape, full3), pl.BlockSpec(b_s.shape, full2),
                  pl.BlockSpec((mem_len, kv.shape[1]), lambda b, n: (b, 0))],
        out_specs=[pl.BlockSpec((GM_SUB * GM_CHUNK, zw), blk),
                   pl.BlockSpec((mem_len, kv.shape[1]), lambda b, n: (b, 0)),
                   pl.BlockSpec(ln_g.shape, full2), pl.BlockSpec(ln_b.shape, full2),
                   pl.BlockSpec(w_s.shape, full3), pl.BlockSpec(b_s.shape, full2)],
        out_shape=[jax.ShapeDtypeStruct((T, zw), BF), jax.ShapeDtypeStruct(kv.shape, F32),
                   jax.ShapeDtypeStruct(ln_g.shape, F32), jax.ShapeDtypeStruct(ln_b.shape, F32),
                   jax.ShapeDtypeStruct(w_s.shape, F32), jax.ShapeDtypeStruct(b_s.shape, F32)],
        name="gmlp_bwd",
        compiler_params=pltpu.CompilerParams(dimension_semantics=("arbitrary", "arbitrary"), vmem_limit_bytes=VMEM_CAP_BYTES),
    )(z, dcat, ln_g, ln_b, w_s, b_s, kv)


def _place():
    x, y, c = lax.axis_index("x"), lax.axis_index("y"), lax.axis_index("c")
    chips = [(1 - x, y), (x, 1 - y), (1 - x, 1 - y)]
    return x, y, c, chips


def _half(ref, kind, e):
    if kind == "col":
        n = ref.shape[1] // 2
        return ref.at[:, pl.ds(pl.multiple_of(e * n, n), n), :]
    n = ref.shape[2] // 2
    return ref.at[:, :, pl.ds(pl.multiple_of(e * n, n), n)]


def _slot(ref, kind, j, n):
    if kind == "col":
        return ref.at[:, :, pl.ds(pl.multiple_of(j * n, n), n)]
    return ref.at[:, pl.ds(pl.multiple_of(j * n, n), n), :]


AG_CHUNKS = 4


def _allgather_seq(name, items, cid):
    nt = len(items)
    kinds = [k for (_, k, _) in items]
    slot_kind = ["row" if k == "row" else "col" for k in kinds]
    out_type = []
    for s, k, l in items:
        L, r, c = s.shape
        lo = L if l is None else 1
        out_type.append(jax.ShapeDtypeStruct((lo, 4 * r, c) if k == "row" else (lo, r, 4 * c), s.dtype))

    def part(ref, t, e, q):
        if kinds[t] == "vec":
            return ref
        half = _half(ref, kinds[t], e)
        n = half.shape[1] // AG_CHUNKS
        return half.at[:, pl.ds(q * n, n), :]

    def chunks(t):
        return 1 if kinds[t] == "vec" else AG_CHUNKS

    def body(*refs):
        sh = [refs[t] if items[t][2] is None else refs[t].at[pl.ds(items[t][2], 1)] for t in range(nt)]
        full = refs[nt:2 * nt]
        loc, s_ici, r_ici, s_rel, r_rel, s_d2d, r_d2d = refs[2 * nt:]
        x, y, c, chips = _place()
        own = 2 * x + y
        sibling = (x, y, 1 - c)
        nbr = [(1 - x, y, c), (x, 1 - y, c)]
        nbr_chip = [2 * (1 - x) + y, 2 * x + (1 - y)]
        diag, diag_chip = (1 - x, 1 - y, c), 2 * (1 - x) + (1 - y)
        barrier = pltpu.get_barrier_semaphore()
        for peer in nbr + [diag, sibling]:
            pl.semaphore_signal(barrier, inc=1, device_id=peer, device_id_type=MESH)
        pl.semaphore_wait(barrier, 4)
        width = [sh[t].shape[1] if kinds[t] == "row" else sh[t].shape[2] for t in range(nt)]
        half_q = AG_CHUNKS // 2

        def region(t, chip_idx, e, q):
            return part(_slot(full[t], slot_kind[t], chip_idx, width[t]), t, e, q)

        def remote(src, dst, ssem, rsem, to):
            return pltpu.make_async_remote_copy(src_ref=src, dst_ref=dst, send_sem=ssem, recv_sem=rsem,
                                                device_id=to, device_id_type=MESH)

        sent = []
        for q in range(AG_CHUNKS):
            for t in range(nt):
                if q >= chunks(t):
                    continue
                for d in range(2):
                    cp = remote(part(sh[t], t, c, q), region(t, own, c, q), s_ici.at[t, d, q], r_ici.at[t, d, q], nbr[d])
                    cp.start()
                    sent.append(cp)
                if kinds[t] == "vec":
                    cp = remote(sh[t], region(t, own, c, 0), s_rel.at[t, 0, 0], r_rel.at[t, 0, 0], diag)
                    cp.start()
                    sent.append(cp)
        started = []
        for t in range(nt):
            dst = _slot(full[t], slot_kind[t], own, width[t])
            n = sh[t].shape[1] // chunks(t)
            for q in range(chunks(t)):
                rows = pl.ds(q * n, n)
                mine = pltpu.make_async_copy(sh[t].at[:, rows, :], dst.at[:, rows, :], loc.at[t, q])
                mine.start()
                started.append(mine)
        for q in range(AG_CHUNKS):
            for t in range(nt):
                if q >= chunks(t):
                    continue
                for d in range(2):
                    landed = region(t, nbr_chip[d], c, q)
                    remote(landed, landed, s_ici.at[t, d, q], r_ici.at[t, d, q], nbr[d]).wait_recv()
                    if kinds[t] == "vec":
                        continue
                    fw = remote(landed, landed, s_d2d.at[t, d, q], r_d2d.at[t, d, q], sibling)
                    fw.start()
                    sent.append(fw)
                    if (q < half_q) == (d == 0):
                        relay = remote(landed, landed, s_rel.at[t, d, q], r_rel.at[t, d, q], nbr[1 - d])
                        relay.start()
                        sent.append(relay)
        for t in range(nt):
            for q in range(chunks(t)):
                d = 0 if q < half_q else 1
                landed = region(t, diag_chip, c, q)
                if kinds[t] == "vec":
                    remote(landed, landed, s_rel.at[t, 0, 0], r_rel.at[t, 0, 0], diag).wait_recv()
                    continue
                remote(landed, landed, s_rel.at[t, d, q], r_rel.at[t, d, q], nbr[1 - d]).wait_recv()
                fw = remote(landed, landed, s_d2d.at[t, 2, q], r_d2d.at[t, 2, q], sibling)
                fw.start()
                sent.append(fw)
        for t in range(nt):
            if kinds[t] == "vec":
                continue
            for p, chip_idx in enumerate(nbr_chip + [diag_chip]):
                for q in range(AG_CHUNKS):
                    other = region(t, chip_idx, 1 - c, q)
                    remote(other, other, s_d2d.at[t, p, q], r_d2d.at[t, p, q], sibling).wait_recv()
        for cp in sent:
            cp.wait_send()
        for cp in started:
            cp.wait()

    sems = pltpu.SemaphoreType.DMA
    return pl.kernel(
        body, out_type=out_type, mesh=plsc.ScalarSubcoreMesh(axis_name="seq", num_cores=1),
        scratch_types=[sems((nt, AG_CHUNKS)), sems((nt, 2, AG_CHUNKS)), sems((nt, 2, AG_CHUNKS)), sems((nt, 2, AG_CHUNKS)),
                       sems((nt, 2, AG_CHUNKS)), sems((nt, 3, AG_CHUNKS)), sems((nt, 3, AG_CHUNKS))],
        compiler_params=pltpu.CompilerParams(collective_id=cid), name=name,
    )(*[s for (s, _, _) in items])


def _slot2(ref, kind, j, n):
    if kind == "col":
        return ref.at[:, pl.ds(pl.multiple_of(j * n, n), n)]
    return ref.at[pl.ds(pl.multiple_of(j * n, n), n), :]


def _rs_chips_seq(name, parts, kinds, cid):
    nm = len(parts)
    out_type = []
    for g, k in zip(parts, kinds):
        r, c = g.shape
        ps = (r, c // 4) if k == "col" else (r // 4, c)
        out_type += [jax.ShapeDtypeStruct(ps, BF), jax.ShapeDtypeStruct((3,) + ps, BF)]

    def body(*refs):
        g = refs[:nm]
        outs = refs[nm:3 * nm]
        loc, ssem, rsem = refs[3 * nm:]
        x, y, c, chips = _place()
        own = 2 * x + y
        barrier = pltpu.get_barrier_semaphore()
        for (px, py) in chips:
            pl.semaphore_signal(barrier, inc=1, device_id=(px, py, c), device_id_type=MESH)
        pl.semaphore_wait(barrier, 3)
        cps = []
        for m in range(nm):
            k = kinds[m]
            own_o, got_o = outs[2 * m], outs[2 * m + 1]
            n = g[m].shape[1] // 4 if k == "col" else g[m].shape[0] // 4
            lc = pltpu.make_async_copy(_slot2(g[m], k, own, n), own_o, loc.at[m])
            lc.start()
            cps.append(lc)
            for p, (px, py) in enumerate(chips):
                cp = pltpu.make_async_remote_copy(
                    src_ref=_slot2(g[m], k, 2 * px + py, n), dst_ref=got_o.at[p],
                    send_sem=ssem.at[m, p], recv_sem=rsem.at[m, p], device_id=(px, py, c), device_id_type=MESH)
                cp.start()
                cps.append(cp)
        for cp in cps:
            cp.wait()

    return pl.kernel(
        body, out_type=out_type, mesh=plsc.ScalarSubcoreMesh(axis_name="seq", num_cores=1),
        scratch_types=[pltpu.SemaphoreType.DMA((nm,)), pltpu.SemaphoreType.DMA((nm, 3)), pltpu.SemaphoreType.DMA((nm, 3))],
        compiler_params=pltpu.CompilerParams(collective_id=cid), name=name,
    )(*parts)


def _finish_share(name, own, got, kind, c_arr):
    L, r, c = own.shape
    tr = _pick(r, 128 if kind == "col" else 256)
    nb = r // tr
    nq = L * nb
    own2 = own.reshape(L * r, c)
    got2 = got.reshape(3 * L * r, c)
    pick = lambda h, q: q * (1 - h) + (nq - 1) * h
    in_specs = [pl.BlockSpec((tr, c), lambda h, q, cc: (pick(h, q), 0))]
    in_specs += [pl.BlockSpec((tr, c), functools.partial(lambda h, q, cc, p: (p * nq + pick(h, q), 0), p=p)) for p in range(3)]
    if kind == "col":
        out_sd = (L, 2, r, c)
        o_spec = pl.BlockSpec((None, 2, tr, c), lambda h, q, cc: ((q * h) // nb, 0, (q * h) % nb, 0))
    else:
        out_sd = (L * r, 2 * c)
        o_spec = pl.BlockSpec((tr, 2 * c), lambda h, q, cc: (q * h, 0))

    def kern(c_ref, o_ref, g0, g1, g2, out_ref, mine, recv, ssem, rsem):
        h, q = pl.program_id(0), pl.program_id(1)
        x, y, cc, _ = _place()

        def swap(qq):
            return pltpu.make_async_remote_copy(src_ref=mine.at[qq], dst_ref=recv.at[qq], send_sem=ssem.at[qq],
                                                recv_sem=rsem.at[qq], device_id=(x, y, 1 - cc), device_id_type=MESH)

        @pl.when(h == 0)
        def _():
            mine[q] = ((o_ref[...].astype(F32) + g0[...].astype(F32)) + g1[...].astype(F32)) + g2[...].astype(F32)
            swap(q).start()

        @pl.when(h == 1)
        def _():
            swap(q).wait()
            a, b = mine[q], recv[q]
            first = c_ref[0] == 0
            lo, hi = jnp.where(first, a, b), jnp.where(first, b, a)
            if kind == "col":
                out_ref[0] = lo
                out_ref[1] = hi
            else:
                out_ref[:, :c] = lo
                out_ref[:, c:] = hi

    est = 2 * nq * tr * c * 4 + 6 * tr * c * 4 + 8 * tr * c * 2
    full = pl.pallas_call(
        kern,
        grid_spec=pltpu.PrefetchScalarGridSpec(
            num_scalar_prefetch=1, grid=(2, nq), in_specs=in_specs, out_specs=o_spec,
            scratch_shapes=[pltpu.VMEM((nq, tr, c), F32), pltpu.VMEM((nq, tr, c), F32),
                            pltpu.SemaphoreType.DMA((nq,)), pltpu.SemaphoreType.DMA((nq,))]),
        out_shape=jax.ShapeDtypeStruct(out_sd, F32), name=name,
        compiler_params=pltpu.CompilerParams(dimension_semantics=("arbitrary", "arbitrary"),
                                             vmem_limit_bytes=VMEM_CAP_BYTES),
    )(c_arr, own2, got2, got2, got2)
    return full.reshape(L, 2 * r, c) if kind == "col" else full.reshape(L, r, 2 * c)


def _small_allreduce(buf, name):
    R = buf.shape[0]
    assert R % 16 == 0
    h = R // 2

    def body(x_ref, o_ref, sib, csum, got, s_a, r_a, s_b, r_b, s_c, r_c):
        x, y, c, chips = _place()
        sibling = (x, y, 1 - c)
        own = 2 * x + y
        swap = pltpu.make_async_remote_copy(src_ref=x_ref, dst_ref=sib, send_sem=s_a, recv_sem=r_a,
                                            device_id=sibling, device_id_type=MESH)
        swap.start()
        swap.wait()
        a, b = x_ref[...], sib[...]
        south = c == 0
        csum[...] = jnp.where(south, a, b) + jnp.where(south, b, a)
        lo = pl.multiple_of(c * h, 8)
        mine = csum.at[pl.ds(lo, h)]
        got[own] = csum[pl.ds(lo, h)]
        sends = []
        for p, (px, py) in enumerate(chips):
            cp = pltpu.make_async_remote_copy(src_ref=mine, dst_ref=got.at[own], send_sem=s_b.at[p], recv_sem=r_b.at[p],
                                              device_id=(px, py, c), device_id_type=MESH)
            cp.start()
            sends.append(cp)
        for cp in sends:
            cp.wait()
        o_ref[pl.ds(lo, h)] = ((got[0] + got[1]) + got[2]) + got[3]
        done = o_ref.at[pl.ds(lo, h)]
        back = pltpu.make_async_remote_copy(src_ref=done, dst_ref=done, send_sem=s_c, recv_sem=r_c,
                                            device_id=sibling, device_id_type=MESH)
        back.start()
        back.wait_send()
        other = o_ref.at[pl.ds(pl.multiple_of((1 - c) * h, 8), h)]
        pltpu.make_async_remote_copy(src_ref=other, dst_ref=other, send_sem=s_c, recv_sem=r_c,
                                     device_id=sibling, device_id_type=MESH).wait_recv()

    vm = pl.BlockSpec(memory_space=pltpu.VMEM)
    return pl.pallas_call(
        body, out_shape=jax.ShapeDtypeStruct(buf.shape, F32), in_specs=[vm], out_specs=vm,
        scratch_shapes=[pltpu.VMEM((R, LANES), F32), pltpu.VMEM((R, LANES), F32), pltpu.VMEM((4, h, LANES), F32),
                        pltpu.SemaphoreType.DMA, pltpu.SemaphoreType.DMA, pltpu.SemaphoreType.DMA((3,)),
                        pltpu.SemaphoreType.DMA((3,)), pltpu.SemaphoreType.DMA, pltpu.SemaphoreType.DMA],
        name=name,
        compiler_params=pltpu.CompilerParams(vmem_limit_bytes=VMEM_CAP_BYTES),
    )(buf)


PACK_TILE_ROWS = 8


def _item_rows(shape):
    n = 1
    for d in shape:
        n *= d
    return -(-n // (PACK_TILE_ROWS * LANES)) * PACK_TILE_ROWS


def _pack(arrs, rows_total):
    buf = jnp.zeros((rows_total, LANES), F32)
    r = 0
    for a in arrs:
        f = a.reshape(-1).astype(F32)
        nr = _item_rows(a.shape)
        block = jnp.pad(f, (0, nr * LANES - f.shape[0])).reshape(nr, LANES)
        buf = lax.dynamic_update_slice(buf, block, (r, 0))
        r += nr
    return buf


def _unpack(buf, shapes):
    out, r = [], 0
    for s in shapes:
        n = 1
        for d in s:
            n *= d
        nr = _item_rows(s)
        out.append(buf[r:r + nr].reshape(-1)[:n].reshape(s))
        r += nr
    return out


def _rows_needed(shapes):
    return -(-sum(_item_rows(s) for s in shapes) // (2 * PACK_TILE_ROWS)) * (2 * PACK_TILE_ROWS)


def _two_rows(a, b):
    out = jnp.zeros((2, a.shape[1]), a.dtype)
    return lax.dynamic_update_slice(lax.dynamic_update_slice(out, a, (0, 0)), b, (1, 0))


def _adam(w, g, m, v):
    m = ADAM_B1 * m + (1.0 - ADAM_B1) * g
    v = ADAM_B2 * v + (1.0 - ADAM_B2) * jnp.square(g)
    m_hat = m / (1.0 - ADAM_B1 ** ADAM_STEP)
    v_hat = v / (1.0 - ADAM_B2 ** ADAM_STEP)
    delta = -ADAM_LR * (m_hat / (jnp.sqrt(v_hat) + ADAM_EPS) + ADAM_WD * w)
    return delta, m, v


def _adam_call(name, w2, g2, m2, v2, tr):
    def fn(rv, cv):
        return list(_adam(*rv)), []

    width = w2.shape[1]
    return _rowcall(name, fn, [(w2, 0, width), (g2, 0, width), (m2, 0, width), (v2, 0, width)], [],
                    [(width, F32)] * 3, [], tr)


def kernel(x, mem, mem_norm, lb_logits, ffn1_norm, ffn1_w_in, ffn1_w_out, mix_norm, mem_w_kv, hgrn_w_in, hgrn_gnorm, hgrn_w_out, gmlp_w_in, gmlp_ln_g, gmlp_ln_b, gmlp_w_s, gmlp_b_s, gmlp_w_out, ffn2_norm, ffn2_w_in, ffn2_w_out, final_norm, loss_target, m_mem_norm, m_lb_logits, m_ffn1_norm, m_ffn1_w_in, m_ffn1_w_out, m_mix_norm, m_mem_w_kv, m_hgrn_w_in, m_hgrn_gnorm, m_hgrn_w_out, m_gmlp_w_in, m_gmlp_ln_g, m_gmlp_ln_b, m_gmlp_w_s, m_gmlp_b_s, m_gmlp_w_out, m_ffn2_norm, m_ffn2_w_in, m_ffn2_w_out, m_final_norm, v_mem_norm, v_lb_logits, v_ffn1_norm, v_ffn1_w_in, v_ffn1_w_out, v_mix_norm, v_mem_w_kv, v_hgrn_w_in, v_hgrn_gnorm, v_hgrn_w_out, v_gmlp_w_in, v_gmlp_ln_g, v_gmlp_ln_b, v_gmlp_w_s, v_gmlp_b_s, v_gmlp_w_out, v_ffn2_norm, v_ffn2_w_in, v_ffn2_w_out, v_final_norm):
    bl, seq, D = x.shape
    T = bl * seq
    mem_len = mem.shape[1]
    chip = 2 * lax.axis_index("x") + lax.axis_index("y")
    c_arr = lax.axis_index("c").astype(jnp.int32).reshape(1)
    TR = 1024

    big = [("ffn1_w_in", ffn1_w_in, "col"), ("ffn1_w_out", ffn1_w_out, "row"), ("mem_w_kv", mem_w_kv, "col"),
           ("hgrn_w_in", hgrn_w_in, "col"), ("hgrn_w_out", hgrn_w_out, "row"), ("gmlp_w_in", gmlp_w_in, "col"),
           ("gmlp_w_out", gmlp_w_out, "row"), ("ffn2_w_in", ffn2_w_in, "col"), ("ffn2_w_out", ffn2_w_out, "row")]
    kinds = [k for (_, _, k) in big]
    shards_bf = []
    for nm, w, _ in big:
        L, r, c = w.shape
        (wb,) = _rowcall("cast_" + nm, lambda rv, cv: ([rv[0]], []), [(w.reshape(L * r, c), 0, c)], [], [(c, BF)], [], 512)
        shards_bf.append(wb.reshape(L, r, c))
    sb = dict(zip([nm for (nm, _, _) in big], shards_bf))
    groups = [[("ffn1_w_in", 0)], [("ffn1_w_out", 0)], [("hgrn_w_in", None)], [("mem_w_kv", None)], [("hgrn_w_out", None)],
              [("ffn2_w_in", 0), ("ffn2_w_out", 0), ("gmlp_ln_g", None), ("gmlp_ln_b", None)],
              [("ffn1_w_in", 1), ("ffn1_w_out", 1)],
              [("gmlp_w_in", None), ("gmlp_w_out", None)],
              [("ffn2_w_in", 1), ("ffn2_w_out", 1)]]
    kind_of = {nm: k for (nm, _, k) in big}
    for nm, vec in (("gmlp_ln_g", gmlp_ln_g), ("gmlp_ln_b", gmlp_ln_b)):
        sb[nm] = vec.reshape(1, 1, -1)
        kind_of[nm] = "vec"
    gathered = {nm: [None, None] for nm in ("ffn1_w_in", "ffn1_w_out", "ffn2_w_in", "ffn2_w_out")}
    for gi, grp in enumerate(groups):
        outs = _allgather_seq("gather_%d" % gi, [(sb[nm], kind_of[nm], l) for (nm, l) in grp], gi)
        for (nm, l), o in zip(grp, outs):
            if l is None:
                gathered[nm] = o
            else:
                gathered[nm][l] = o

    ln_w = GM_GROUPS * GM_GROUP_DIM
    ln_g_full, ln_b_full = gathered["gmlp_ln_g"].reshape(1, ln_w), gathered["gmlp_ln_b"].reshape(1, ln_w)

    def rms_fwd(name, xin, g):
        (h,) = _rowcall(name, lambda rv, cv: ([_rmsnorm(rv[0], cv[0])], []), [(xin, 0, D)], [g.reshape(1, D)], [(D, BF)], [], TR)
        return h

    def ffn_fwd(tag, xin, h, w_in, w_out, layer, next_gain):
        dff = w_out[layer].shape[1]
        zg, zu, a = _ffn_in_swiglu("ffn_in_" + tag, h, w_in[layer], 1024, dff // 2)
        out = _mm("ffn_out_" + tag, a, w_out[layer], "nn", F32, 1024, 1024, dff, scale=0.5, res=xin, b_lead=0,
                  norm_gain=None if next_gain is None else next_gain.reshape(1, D))
        xo, h_next = (out, None) if next_gain is None else out
        return xo, h_next, (xin, h, zg, zu, a)

    def ffn_bwd(tag, dxo, saved, g, w_in, w_out, layer):
        xin, h, zg, zu, a = saved
        dff = w_out[layer].shape[1]
        dw_out = _mm_tn_pair("ffn_dwo_" + tag, a, dxo, "row", c_arr, dff // 2, T, scale=0.5)
        dz = _ffn_da_swiglu("ffn_da_" + tag, dxo, w_out[layer], zg, zu, 512)
        dw_in = _mm_tn_pair("ffn_dwi_" + tag, h, dz, "col", c_arr, 512, T)
        dx, dg = _mm_dh_rms("ffn_dh_" + tag, dz, w_in[layer], xin, g.reshape(1, D), dxo, 512)
        return dx, dg, dw_in, dw_out

    def rms_bwd(name, xin, g, dh, dres):
        def fn(rv, cv):
            _, vjp = jax.vjp(_rmsnorm, rv[0], cv[0])
            dx, dg = vjp(rv[1])
            if dres is not None:
                dx = dx + rv[2]
            return [dx], [dg]

        rows = [(xin, 0, D), (dh, 0, D)] + ([(dres, 0, D)] if dres is not None else [])
        dx, dg = _rowcall(name, fn, rows, [g.reshape(1, D)], [(D, F32)], [((1, D), F32)], TR)
        return dx, dg

    x0 = x.reshape(T, D)
    tgt = loss_target.reshape(T, D)
    mem2 = mem.reshape(bl * mem_len, D)
    memn = rms_fwd("rms_mem", mem2, mem_norm)

    h_f10 = rms_fwd("rms_f1l0", x0, ffn1_norm[0])
    x1, h_m0, sv_f10 = ffn_fwd("f1l0", x0, h_f10, gathered["ffn1_w_in"], gathered["ffn1_w_out"], 0, mix_norm[0])
    z_m0 = _mm("mix_in_0", h_m0, gathered["hgrn_w_in"], "nn", F32, 2048, 512, D, b_lead=0)
    kv = [_mm("kv_%d" % i, memn, gathered["mem_w_kv"], "nn", F32, 512, 512, D, b_lead=i) for i in range(2)]
    cat0, stash0 = _hgrn_fwd2(z_m0, lb_logits, hgrn_gnorm, kv[0], bl, seq)
    x2, h_f20 = _mm("mix_out_0", cat0, gathered["hgrn_w_out"], "nn", F32, 1024, 1024, cat0.shape[1], res=x1, b_lead=0,
                    norm_gain=ffn2_norm[0].reshape(1, D))
    x3, h_f11, sv_f20 = ffn_fwd("f2l0", x2, h_f20, gathered["ffn2_w_in"], gathered["ffn2_w_out"], 0, ffn1_norm[1])
    x4, h_m1, sv_f11 = ffn_fwd("f1l1", x3, h_f11, gathered["ffn1_w_in"], gathered["ffn1_w_out"], 1, mix_norm[1])
    z_m1 = _mm("mix_in_1", h_m1, gathered["gmlp_w_in"], "nn", F32, 2048, 512, D, b_lead=0)
    nc1 = seq // GM_CHUNK
    w_s, b_s = gmlp_w_s[0], gmlp_b_s[0]
    cat1 = _gmlp_fwd(z_m1, ln_g_full, ln_b_full, w_s, b_s, kv[1], bl, nc1)
    x5, h_f21 = _mm("mix_out_1", cat1, gathered["gmlp_w_out"], "nn", F32, 1024, 1024, cat1.shape[1], res=x4, b_lead=0,
                    norm_gain=ffn2_norm[1].reshape(1, D))
    x6, _, sv_f21 = ffn_fwd("f2l1", x5, h_f21, gathered["ffn2_w_in"], gathered["ffn2_w_out"], 1, None)

    def head(rv, cv):
        def f(xx, gg):
            err = _rmsnorm(xx, gg) - rv[1]
            return 0.5 * jnp.sum(jnp.mean(err * err, axis=-1, keepdims=True), axis=0, keepdims=True)

        ls, vjp = jax.vjp(f, rv[0], cv[0])
        dx, dg = vjp(jnp.ones((1, 1), F32))
        return [dx], [dg, jnp.broadcast_to(ls, (1, 128))]

    dx6, d_final, loss_part = _rowcall("loss_head", head, [(x6, 0, D), (tgt, 0, D)], [final_norm.reshape(1, D)],
                                       [(D, F32)], [((1, D), F32), ((1, 128), F32)], TR)

    rs_out = {}
    n_gather = len(groups)

    def rs(gi, items):
        outs = _rs_chips_seq("reduce_%d" % gi, [p for (_, p, _) in items], [k for (_, _, k) in items], n_gather + gi)
        for i, (key, _, _) in enumerate(items):
            rs_out[key] = (outs[2 * i], outs[2 * i + 1])

    dx5, dg_f21, dwi_f21, dwo_f21 = ffn_bwd("f2l1", dx6, sv_f21, ffn2_norm[1], gathered["ffn2_w_in"], gathered["ffn2_w_out"], 1)
    rs(0, [(("ffn2_w_out", 1), dwo_f21, "row"), (("ffn2_w_in", 1), dwi_f21, "col")])
    dcat1 = _mm("mix_dcat_1", dx5, gathered["gmlp_w_out"], "nt", F32, 2048, 1024, D, b_lead=0)
    dwo_m1 = _mm_tn_pair("mix_dwo_1", cat1, dx5, "row", c_arr, 1024, T)
    dz_m1, dkv1, d_lng, d_lnb, d_ws, d_bs = _gmlp_bwd(z_m1, dcat1, ln_g_full, ln_b_full, w_s, b_s, kv[1], bl, nc1)
    dx4, dg_m1 = _mm_dh_rms("mix_dh_1", dz_m1, gathered["gmlp_w_in"], x4, mix_norm[1].reshape(1, D), dx5, 512)
    dwi_m1 = _mm_tn_pair("mix_dwi_1", h_m1, dz_m1, "col", c_arr, 1024, T)
    rs(1, [(("gmlp_w_out", 0), dwo_m1, "row"), (("gmlp_w_in", 0), dwi_m1, "col")])
    dx3, dg_f11, dwi_f11, dwo_f11 = ffn_bwd("f1l1", dx4, sv_f11, ffn1_norm[1], gathered["ffn1_w_in"], gathered["ffn1_w_out"], 1)
    rs(2, [(("ffn1_w_out", 1), dwo_f11, "row"), (("ffn1_w_in", 1), dwi_f11, "col")])

    dx2, dg_f20, dwi_f20, dwo_f20 = ffn_bwd("f2l0", dx3, sv_f20, ffn2_norm[0], gathered["ffn2_w_in"], gathered["ffn2_w_out"], 0)
    rs(3, [(("ffn2_w_out", 0), dwo_f20, "row"), (("ffn2_w_in", 0), dwi_f20, "col")])
    dcat0 = _mm("mix_dcat_0", dx2, gathered["hgrn_w_out"], "nt", F32, 2048, 1024, D, b_lead=0)
    dwo_m0 = _mm_tn_pair("mix_dwo_0", cat0, dx2, "row", c_arr, 1024, T)
    dz_m0, dkv0, d_lb, d_gn = _hgrn_bwd2(z_m0, dcat0, stash0, lb_logits, hgrn_gnorm, kv[0], bl, seq)
    dx1, dg_m0 = _mm_dh_rms("mix_dh_0", dz_m0, gathered["hgrn_w_in"], x1, mix_norm[0].reshape(1, D), dx2, 512)
    dwi_m0 = _mm_tn_pair("mix_dwi_0", h_m0, dz_m0, "col", c_arr, 1024, T)
    rs(4, [(("hgrn_w_out", 0), dwo_m0, "row"), (("hgrn_w_in", 0), dwi_m0, "col")])

    dwkv = [_mm_tn_pair("kv_dw_%d" % i, memn, dkv, "col", c_arr, 1024, 512) for i, dkv in enumerate([dkv0, dkv1])]
    rs(5, [(("mem_w_kv", 0), dwkv[0], "col"), (("mem_w_kv", 1), dwkv[1], "col")])
    dmemn = _mm("kv_dx_0", dkv0, gathered["mem_w_kv"], "nt", F32, 512, 512, 1024, b_lead=0)
    dmemn = _mm("kv_dx_1", dkv1, gathered["mem_w_kv"], "nt", F32, 512, 512, 1024, res=dmemn, b_lead=1)
    _, d_memnorm = rms_bwd("rms_bwd_mem", mem2, mem_norm, dmemn, None)

    dx0, dg_f10, dwi_f10, dwo_f10 = ffn_bwd("f1l0", dx1, sv_f10, ffn1_norm[0], gathered["ffn1_w_in"], gathered["ffn1_w_out"], 0)
    rs(6, [(("ffn1_w_out", 0), dwo_f10, "row")])
    rs(7, [(("ffn1_w_in", 0), dwi_f10, "col")])

    shard_grads = []
    for (nm, w, k) in big:
        per_layer = []
        for l in range(w.shape[0]):
            own, got = rs_out[(nm, l)]
            per_layer.append(_finish_share("finish_%s_%d" % (nm, l), own[None], got[:, None], k, c_arr))
        shard_grads.append(per_layer[0] if len(per_layer) == 1 else jnp.concatenate(per_layer, axis=0))

    big_w = [w for (_, w, _) in big]
    big_m = [m_ffn1_w_in, m_ffn1_w_out, m_mem_w_kv, m_hgrn_w_in, m_hgrn_w_out, m_gmlp_w_in, m_gmlp_w_out, m_ffn2_w_in, m_ffn2_w_out]
    big_v = [v_ffn1_w_in, v_ffn1_w_out, v_mem_w_kv, v_hgrn_w_in, v_hgrn_w_out, v_gmlp_w_in, v_gmlp_w_out, v_ffn2_w_in, v_ffn2_w_out]
    big_out = {}
    for (nm, w, _), g, m, v in zip(big, shard_grads, big_m, big_v):
        L, r, c = w.shape
        d2, m2, v2 = _adam_call("adam_" + nm, w.reshape(L * r, c), g.reshape(L * r, c), m.reshape(L * r, c),
                                v.reshape(L * r, c), 256)
        big_out[nm] = (g, d2.reshape(w.shape), m2.reshape(w.shape), v2.reshape(w.shape))

    d_ffn1n = _two_rows(dg_f10, dg_f11)
    d_mixn = _two_rows(dg_m0, dg_m1)
    d_ffn2n = _two_rows(dg_f20, dg_f21)
    small_parts = [loss_part[:, :1], d_memnorm, d_lb, d_ffn1n, d_mixn, d_gn, d_lng, d_lnb, d_ws, d_bs, d_ffn2n, d_final]
    red_shapes = [(1,), mem_norm.shape, lb_logits.shape, ffn1_norm.shape, mix_norm.shape, hgrn_gnorm.shape, (1, ln_w), (1, ln_w),
                  gmlp_w_s.shape, gmlp_b_s.shape, ffn2_norm.shape, final_norm.shape]
    red = _small_allreduce(_pack(small_parts, _rows_needed(red_shapes)), "reduce_small")
    (loss_v, g_memn, g_lb, g_f1n, g_mixn, g_gn, g_lng_full, g_lnb_full, g_ws, g_bs, g_f2n, g_fin) = _unpack(red, red_shapes)
    lsh = gmlp_ln_g.shape[1]
    g_lng = lax.dynamic_slice(g_lng_full, (0, chip * lsh), (1, lsh))
    g_lnb = lax.dynamic_slice(g_lnb_full, (0, chip * lsh), (1, lsh))
    small_w = [mem_norm, lb_logits, ffn1_norm, mix_norm, hgrn_gnorm, gmlp_ln_g, gmlp_ln_b, gmlp_w_s, gmlp_b_s, ffn2_norm, final_norm]
    small_g = [g_memn, g_lb, g_f1n, g_mixn, g_gn, g_lng, g_lnb, g_ws, g_bs, g_f2n, g_fin]
    small_m = [m_mem_norm, m_lb_logits, m_ffn1_norm, m_mix_norm, m_hgrn_gnorm, m_gmlp_ln_g, m_gmlp_ln_b, m_gmlp_w_s, m_gmlp_b_s, m_ffn2_norm, m_final_norm]
    small_v = [v_mem_norm, v_lb_logits, v_ffn1_norm, v_mix_norm, v_hgrn_gnorm, v_gmlp_ln_g, v_gmlp_ln_b, v_gmlp_w_s, v_gmlp_b_s, v_ffn2_norm, v_final_norm]
    sshapes = [w.shape for w in small_w]
    nrow = _rows_needed(sshapes)
    d_p, m_p, v_p = _adam_call("adam_small", _pack(small_w, nrow), _pack(small_g, nrow), _pack(small_m, nrow), _pack(small_v, nrow), nrow)
    s_delta, s_m, s_v = _unpack(d_p, sshapes), _unpack(m_p, sshapes), _unpack(v_p, sshapes)
    small_names = ["mem_norm", "lb_logits", "ffn1_norm", "mix_norm", "hgrn_gnorm", "gmlp_ln_g", "gmlp_ln_b", "gmlp_w_s", "gmlp_b_s", "ffn2_norm", "final_norm"]
    small_out = {nm: (g.reshape(w.shape), d, m, v) for nm, w, g, d, m, v in zip(small_names, small_w, small_g, s_delta, s_m, s_v)}

    order = ["mem_norm", "lb_logits", "ffn1_norm", "ffn1_w_in", "ffn1_w_out", "mix_norm", "mem_w_kv", "hgrn_w_in", "hgrn_gnorm",
             "hgrn_w_out", "gmlp_w_in", "gmlp_ln_g", "gmlp_ln_b", "gmlp_w_s", "gmlp_b_s", "gmlp_w_out", "ffn2_norm", "ffn2_w_in",
             "ffn2_w_out", "final_norm"]
    allo = {**big_out, **small_out}
    grad_x = dx0.reshape(x.shape)
    return (loss_v.reshape(()), grad_x, *[allo[n][0] for n in order], *[allo[n][1] for n in order],
            *[allo[n][2] for n in order], *[allo[n][3] for n in order])
```

```python
import functools

import jax
import jax.numpy as jnp
from jax import lax
from jax.experimental import pallas as pl
from jax.experimental.pallas import tpu as pltpu
from jax.experimental.pallas import tpu_sc as plsc

BF = jnp.bfloat16
F32 = jnp.float32
MESH = pl.DeviceIdType.MESH

EPS = 1e-6
D_MODEL = 1024
HG_HEADS = 8
HG_DIM = 128
HG_CHUNK = 64
GM_CHUNK = 128
GM_GROUPS = 8
GM_GROUP_DIM = 256
XA_HEADS = 4
XA_DIM = 256
ADAM_LR = 0.001
ADAM_B1 = 0.9
ADAM_B2 = 0.999
ADAM_EPS = 1e-08
ADAM_WD = 0.01
ADAM_STEP = 10

VMEM_CAP_BYTES = 60 * 1024 * 1024
LANES = 1024


def _pick(n, cap, mult=16):
    if n <= cap:
        return n
    for d in range(cap - cap % mult, 0, -mult):
        if n % d == 0:
            return d
    raise ValueError((n, cap, mult))


def _dg(a, b, ca, cb):
    return lax.dot_general(a.astype(BF), b.astype(BF), (((ca,), (cb,)), ((), ())), preferred_element_type=F32)


@jax.custom_vjp
def dot_nn(a, b):
    return _dg(a, b, 1, 0)


def _nn_fwd(a, b):
    return _dg(a, b, 1, 0), (a, b)


def _nn_bwd(r, g):
    a, b = r
    return _dg(g, b, 1, 1), _dg(a, g, 0, 0)


dot_nn.defvjp(_nn_fwd, _nn_bwd)


@jax.custom_vjp
def dot_nt(a, b):
    return _dg(a, b, 1, 1)


def _nt_fwd(a, b):
    return _dg(a, b, 1, 1), (a, b)


def _nt_bwd(r, g):
    a, b = r
    return _dg(g, b, 1, 0), _dg(g, a, 0, 0)


dot_nt.defvjp(_nt_fwd, _nt_bwd)


@jax.custom_vjp
def dot_tn(a, b):
    return _dg(a, b, 0, 0)


def _tn_fwd(a, b):
    return _dg(a, b, 0, 0), (a, b)


def _tn_bwd(r, g):
    a, b = r
    return _dg(b, g, 1, 1), _dg(a, g, 1, 0)


dot_tn.defvjp(_tn_fwd, _tn_bwd)


def _rmsnorm(x, g):
    return x * lax.rsqrt(jnp.mean(x * x, axis=-1, keepdims=True) + EPS) * g


def _silu(x):
    return x * jax.nn.sigmoid(x)


@jax.custom_vjp
def _gelu(x):
    return 0.5 * x * (1.0 + lax.erf(x * (0.5 ** 0.5)))


def _gelu_fwd(x):
    return _gelu(x), x


def _gelu_bwd(x, g):
    t = x * (0.5 ** 0.5)
    cdf = 0.5 * (1.0 + lax.erf(t))
    return (g * (cdf + x * (jnp.exp(-(t * t)) * (0.5 / 3.141592653589793) ** 0.5)),)


_gelu.defvjp(_gelu_fwd, _gelu_bwd)


def _softmax_last(s):
    m = lax.stop_gradient(jnp.max(s, axis=-1, keepdims=True))
    e = jnp.exp(s - m)
    return e / jnp.sum(e, axis=-1, keepdims=True)


def _tril(n):
    r = lax.broadcasted_iota(jnp.int32, (n, n), 0)
    c = lax.broadcasted_iota(jnp.int32, (n, n), 1)
    return r >= c


def _cumsum_rows(l):
    n = l.shape[0]
    return lax.dot_general(_tril(n).astype(F32), l, (((1,), (0,)), ((), ())),
                           precision=lax.Precision.HIGHEST, preferred_element_type=F32)


def _attention(zx, mk, mv):
    s = dot_nt(zx, mk) * (XA_DIM ** -0.5)
    return dot_nn(_softmax_last(s), mv)


def _hgrn_head(zq, zf, zi, zg, l0, l1, l2, gn, S):
    m = lax.stop_gradient(jnp.maximum(jnp.maximum(l0, l1), l2))
    e0 = jnp.exp(l0 - m)
    lb = e0 / (e0 + jnp.exp(l1 - m) + jnp.exp(l2 - m))
    q = _silu(zq)
    f = lb + (1.0 - lb) * jax.nn.sigmoid(zf)
    k = 1.0 - f
    b = _cumsum_rows(jnp.log(f))
    b_last = b[HG_CHUNK - 1:HG_CHUNK, :]
    q_dec = q * jnp.exp(b)
    k_inv = k * jnp.exp(-b)
    a = jnp.where(_tril(HG_CHUNK), dot_nt(q_dec, k_inv), 0.0)
    o = dot_nn(a, zi) + dot_nn(q_dec, S)
    S_new = jnp.exp(b_last).reshape(HG_DIM, 1) * S + dot_tn(k * jnp.exp(b_last - b), zi)
    o = _rmsnorm(o, gn) * _silu(zg)
    return o, S_new


def _gmlp_block(zu, zv, zx, lng, lnb, ws, bs, mk, mv):
    gv = [_gelu(v) for v in zv]
    width = GM_GROUPS * GM_GROUP_DIM
    mu = sum(jnp.sum(g, axis=-1, keepdims=True) for g in gv) / width
    xc = [g - mu for g in gv]
    var = sum(jnp.sum(c * c, axis=-1, keepdims=True) for c in xc) / width
    r = lax.rsqrt(var + EPS)
    outs = []
    for g in range(GM_GROUPS):
        v = xc[g] * r * lng[g] + lnb[g]
        w = jnp.where(_tril(GM_CHUNK), ws[g], 0.0)
        mixed = dot_nn(w, v) + bs[g].reshape(GM_CHUNK, 1)
        outs.append(_gelu(zu[g]) * mixed)
    for a in range(XA_HEADS):
        outs.append(_attention(zx[a], mk[a], mv[a]))
    return outs


def _rowcall(name, fn, rows, consts, row_outs, acc_outs, tr):
    nrows = rows[0][0].shape[0]
    tr = _pick(nrows, tr)
    n_r, n_c, n_ro, n_ao = len(rows), len(consts), len(row_outs), len(acc_outs)

    def kern(*refs):
        rv = [r[...] for r in refs[:n_r]]
        cv = [r[...] for r in refs[n_r:n_r + n_c]]
        ro_refs = refs[n_r + n_c:n_r + n_c + n_ro]
        ao_refs = refs[n_r + n_c + n_ro:]
        ro, ao = fn(rv, cv)
        for ref, v in zip(ro_refs, ro):
            ref[...] = v.astype(ref.dtype)
        if n_ao:
            @pl.when(pl.program_id(0) == 0)
            def _():
                for ref in ao_refs:
                    ref[...] = jnp.zeros(ref.shape, ref.dtype)

            for ref, v in zip(ao_refs, ao):
                ref[...] += v.astype(ref.dtype)

    in_specs = [pl.BlockSpec((tr, w), functools.partial(lambda i, cb: (i, cb), cb=cb)) for (_, cb, w) in rows]
    in_specs += [pl.BlockSpec(c.shape, lambda i: (0, 0)) for c in consts]
    out_specs = [pl.BlockSpec((tr, w), lambda i: (i, 0)) for (w, _) in row_outs]
    out_specs += [pl.BlockSpec(s, lambda i: (0, 0)) for (s, _) in acc_outs]
    out_shape = [jax.ShapeDtypeStruct((nrows, w), dt) for (w, dt) in row_outs]
    out_shape += [jax.ShapeDtypeStruct(s, dt) for (s, dt) in acc_outs]
    outs = pl.pallas_call(
        kern, grid=(nrows // tr,), in_specs=in_specs, out_specs=out_specs, out_shape=out_shape, name=name,
        compiler_params=pltpu.CompilerParams(dimension_semantics=("arbitrary",),
                                             vmem_limit_bytes=VMEM_CAP_BYTES),
    )(*[a for (a, _, _) in rows], *consts)
    return outs


def _mm(name, a, b, mode, out_dtype, tm, tn, tk, scale=1.0, res=None, a_lead=None, b_lead=None, norm_gain=None):
    ash = a.shape[-2:]
    bsh = b.shape[-2:]
    if mode == "nn":
        (M, K), (K2, N) = ash, bsh
    elif mode == "nt":
        (M, K), (N, K2) = ash, bsh
    else:
        (K, M), (K2, N) = ash, bsh
    assert K == K2, (name, a.shape, b.shape)
    tm, tn, tk = min(tm, M), min(tn, N), min(tk, K)
    assert M % tm == 0 and N % tn == 0 and K % tk == 0, (name, M, N, K, tm, tn, tk)
    nk = K // tk
    dims = {"nn": (1, 0), "nt": (1, 1), "tn": (0, 0)}[mode]

    def lead(spec_shape, index_fn, lead_idx):
        if lead_idx is None:
            return pl.BlockSpec(spec_shape, index_fn)
        return pl.BlockSpec((None,) + spec_shape, lambda i, j, k: (lead_idx,) + index_fn(i, j, k))

    if mode == "tn":
        a_spec = lead((tk, tm), lambda i, j, k: (k, i), a_lead)
    else:
        a_spec = lead((tm, tk), lambda i, j, k: (i, k), a_lead)
    if mode == "nt":
        b_spec = lead((tn, tk), lambda i, j, k: (j, k), b_lead)
    else:
        b_spec = lead((tk, tn), lambda i, j, k: (k, j), b_lead)
    o_spec = pl.BlockSpec((tm, tn), lambda i, j, k: (i, j))
    has_res = res is not None
    has_norm = norm_gain is not None
    assert not has_norm or tn == N

    def kern(*refs):
        a_ref, b_ref = refs[0], refs[1]
        pos = 2
        res_ref = gain_ref = h_ref = None
        if has_res:
            res_ref, pos = refs[pos], pos + 1
        if has_norm:
            gain_ref, pos = refs[pos], pos + 1
        o_ref, pos = refs[pos], pos + 1
        if has_norm:
            h_ref = refs[pos]
        acc_ref = refs[-1] if nk > 1 else None
        p = lax.dot_general(a_ref[...].astype(BF), b_ref[...].astype(BF), (((dims[0],), (dims[1],)), ((), ())),
                            preferred_element_type=F32)

        def finish(v):
            if scale != 1.0:
                v = v * scale
            if has_res:
                v = res_ref[...] + v
            o_ref[...] = v.astype(o_ref.dtype)
            if has_norm:
                h_ref[...] = _rmsnorm(v, gain_ref[...]).astype(h_ref.dtype)

        if nk == 1:
            finish(p)
        else:
            k = pl.program_id(2)

            @pl.when(k == 0)
            def _():
                acc_ref[...] = p

            @pl.when(k > 0)
            def _():
                acc_ref[...] += p

            @pl.when(k == nk - 1)
            def _():
                finish(acc_ref[...])

    ins = [a, b] + ([res] if has_res else []) + ([norm_gain] if has_norm else [])
    in_specs = [a_spec, b_spec] + ([o_spec] if has_res else [])
    in_specs += [pl.BlockSpec((1, N), lambda i, j, k: (0, 0))] if has_norm else []
    out_sd = jax.ShapeDtypeStruct((M, N), out_dtype)
    return pl.pallas_call(
        kern, grid=(M // tm, N // tn, nk), in_specs=in_specs,
        out_specs=[o_spec, o_spec] if has_norm else o_spec,
        out_shape=[out_sd, jax.ShapeDtypeStruct((M, N), BF)] if has_norm else out_sd,
        scratch_shapes=[pltpu.VMEM((tm, tn), F32)] if nk > 1 else [],
        name=name,
        compiler_params=pltpu.CompilerParams(dimension_semantics=("parallel", "parallel", "arbitrary"),
                                             vmem_limit_bytes=VMEM_CAP_BYTES),
    )(*ins)


def _ffn_in_swiglu(name, h, w3, tm, tn):
    T, D = h.shape
    dff = w3.shape[2] // 2
    tm = min(tm, T)
    assert T % tm == 0 and dff % tn == 0
    nj = dff // tn

    def kern(h_ref, wg_ref, wu_ref, zg_ref, zu_ref, a_ref):
        hb = h_ref[...]
        g = jnp.dot(hb, wg_ref[...], preferred_element_type=F32).astype(BF)
        u = jnp.dot(hb, wu_ref[...], preferred_element_type=F32).astype(BF)
        zg_ref[...] = g
        zu_ref[...] = u
        a_ref[...] = (_silu(g.astype(F32)) * u.astype(F32)).astype(BF)

    o_spec = pl.BlockSpec((tm, tn), lambda i, j: (i, j))
    return pl.pallas_call(
        kern, grid=(T // tm, nj),
        in_specs=[pl.BlockSpec((tm, D), lambda i, j: (i, 0)),
                  pl.BlockSpec((None, D, tn), lambda i, j: (0, 0, j)),
                  pl.BlockSpec((None, D, tn), lambda i, j: (0, 0, j + nj))],
        out_specs=[o_spec, o_spec, o_spec],
        out_shape=[jax.ShapeDtypeStruct((T, dff), BF)] * 3, name=name,
        compiler_params=pltpu.CompilerParams(dimension_semantics=("parallel", "arbitrary"),
                                             vmem_limit_bytes=VMEM_CAP_BYTES),
    )(h, w3, w3)


def _ffn_da_swiglu(name, dxo, w3, zg, zu, tm):
    T, D = dxo.shape
    dff = w3.shape[1]
    tm = min(tm, T)
    assert T % tm == 0 and dff % 2 == 0
    hc = dff // 2

    def kern(d_ref, w_ref, g_ref, u_ref, dz_ref):
        db = (d_ref[...] * 0.5).astype(BF)
        for s in range(2):
            cols = slice(s * hc, (s + 1) * hc)
            da = lax.dot_general(db, w_ref[cols, :], (((1,), (1,)), ((), ())), preferred_element_type=F32)
            g = g_ref[:, cols].astype(F32)
            sg = 1.0 / (1.0 + jnp.exp(-g))
            gs = g * sg
            dab = da.astype(BF)
            dz_ref[:, cols] = (dab * u_ref[:, cols]) * (sg + gs * (1.0 - sg)).astype(BF)
            dz_ref[:, dff + s * hc:dff + (s + 1) * hc] = dab * gs.astype(BF)

    row = lambda w: pl.BlockSpec((tm, w), lambda i: (i, 0))
    return pl.pallas_call(
        kern, grid=(T // tm,),
        in_specs=[row(D), pl.BlockSpec((None, dff, D), lambda i: (0, 0, 0), pipeline_mode=pl.Buffered(1)), row(dff), row(dff)],
        out_specs=row(2 * dff), out_shape=jax.ShapeDtypeStruct((T, 2 * dff), BF), name=name,
        compiler_params=pltpu.CompilerParams(dimension_semantics=("arbitrary",), vmem_limit_bytes=VMEM_CAP_BYTES),
    )(dxo, w3, zg, zu)


def _mm_dh_rms(name, dz, w3, xin, g, dres, tm):
    T, K = dz.shape
    D = w3.shape[1]
    tm = min(tm, T)
    assert T % tm == 0

    def kern(dz_ref, w_ref, x_ref, g_ref, r_ref, dx_ref, dg_ref):
        dh = lax.dot_general(dz_ref[...], w_ref[...], (((1,), (1,)), ((), ())), preferred_element_type=F32)
        _, vjp = jax.vjp(_rmsnorm, x_ref[...], g_ref[...])
        dx, dg = vjp(dh)
        dx_ref[...] = dx + r_ref[...]

        @pl.when(pl.program_id(0) == 0)
        def _():
            dg_ref[...] = jnp.zeros(dg_ref.shape, F32)

        dg_ref[...] += dg

    row = lambda w: pl.BlockSpec((tm, w), lambda i: (i, 0))
    one = pl.BlockSpec((1, D), lambda i: (0, 0))
    return pl.pallas_call(
        kern, grid=(T // tm,),
        in_specs=[row(K), pl.BlockSpec((None, D, K), lambda i: (0, 0, 0), pipeline_mode=pl.Buffered(1)), row(D), one, row(D)],
        out_specs=[row(D), one], out_shape=[jax.ShapeDtypeStruct((T, D), F32), jax.ShapeDtypeStruct((1, D), F32)], name=name,
        compiler_params=pltpu.CompilerParams(dimension_semantics=("arbitrary",), vmem_limit_bytes=VMEM_CAP_BYTES),
    )(dz, w3, xin, g, dres)


def _mm_tn_pair(name, a, b, kind, c_arr, tq, tk, scale=1.0):
    T, M = a.shape
    _, N = b.shape
    tk = min(tk, T)
    assert T % tk == 0
    nk = T // tk
    if kind == "col":
        hm = M // 2
        assert N % tq == 0
        nq = N // tq
        tile = (hm, tq)
        a_spec = pl.BlockSpec((tk, hm), lambda h, q, k, c: (k, jnp.bitwise_xor(h, 1 - c[0])))
        b_spec = pl.BlockSpec((tk, tq), lambda h, q, k, c: (k, q))
        o_spec = pl.BlockSpec(tile, lambda h, q, k, c: (0, q * h))
        out_sd = (hm, N)
    else:
        hn = N // 2
        assert M % tq == 0
        nq = M // tq
        tile = (tq, hn)
        a_spec = pl.BlockSpec((tk, tq), lambda h, q, k, c: (k, q))
        b_spec = pl.BlockSpec((tk, hn), lambda h, q, k, c: (k, jnp.bitwise_xor(h, 1 - c[0])))
        o_spec = pl.BlockSpec(tile, lambda h, q, k, c: (q * h, 0))
        out_sd = (M, hn)

    def kern(c_ref, a_ref, b_ref, o_ref, acc, stage, recv, ssem, rsem):
        h, q, k = pl.program_id(0), pl.program_id(1), pl.program_id(2)
        x, y, c, _ = _place()
        p = lax.dot_general(a_ref[...].astype(BF), b_ref[...].astype(BF), (((0,), (0,)), ((), ())), preferred_element_type=F32)

        @pl.when(k == 0)
        def _():
            acc[...] = p

        @pl.when(k > 0)
        def _():
            acc[...] += p

        def send(slot, qq):
            return pltpu.make_async_remote_copy(src_ref=stage.at[slot], dst_ref=recv.at[qq], send_sem=ssem.at[slot],
                                                recv_sem=rsem.at[qq], device_id=(x, y, 1 - c), device_id_type=MESH)

        last = k == nk - 1

        @pl.when(jnp.logical_and(last, h == 0))
        def _():
            slot = q % 2

            @pl.when(q >= 2)
            def _():
                send(slot, q).wait_send()

            stage[slot] = (acc[...] * scale).astype(BF)
            send(slot, q).start()

        @pl.when(jnp.logical_and(last, h == 1))
        def _():
            @pl.when(q == 0)
            def _():
                for s in range(min(nq, 2)):
                    send(s, 0).wait_send()

            send(0, q).wait_recv()
            o_ref[...] = (acc[...] * scale + recv[q].astype(F32)).astype(o_ref.dtype)

    return pl.pallas_call(
        kern,
        grid_spec=pltpu.PrefetchScalarGridSpec(
            num_scalar_prefetch=1, grid=(2, nq, nk), in_specs=[a_spec, b_spec], out_specs=o_spec,
            scratch_shapes=[pltpu.VMEM(tile, F32), pltpu.VMEM((2,) + tile, BF), pltpu.VMEM((nq,) + tile, BF),
                            pltpu.SemaphoreType.DMA((2,)), pltpu.SemaphoreType.DMA((nq,))]),
        out_shape=jax.ShapeDtypeStruct(out_sd, BF), name=name,
        compiler_params=pltpu.CompilerParams(dimension_semantics=("arbitrary", "arbitrary", "arbitrary"),
                                             vmem_limit_bytes=VMEM_CAP_BYTES),
    )(c_arr, a, b)


def _kv_pieces(kv_ref):
    W = XA_HEADS * XA_DIM
    mk = [kv_ref[:, a * XA_DIM:(a + 1) * XA_DIM] for a in range(XA_HEADS)]
    mv = [kv_ref[:, W + a * XA_DIM:W + (a + 1) * XA_DIM] for a in range(XA_HEADS)]
    return mk, mv


def _lb_pieces(lb_ref):
    return [[lb_ref[r:r + 1, h * HG_DIM:(h + 1) * HG_DIM] for h in range(HG_HEADS)] for r in range(3)]


HG_SUB = 4


def _hgrn_rows(z_ref):
    W = HG_HEADS * HG_DIM

    def piece(c, col, w):
        return z_ref[c * HG_CHUNK:(c + 1) * HG_CHUNK, col:col + w]

    zq = [[piece(c, h * HG_DIM, HG_DIM) for h in range(HG_HEADS)] for c in range(HG_SUB)]
    zf = [[piece(c, W + h * HG_DIM, HG_DIM) for h in range(HG_HEADS)] for c in range(HG_SUB)]
    zi = [[piece(c, 2 * W + h * HG_DIM, HG_DIM) for h in range(HG_HEADS)] for c in range(HG_SUB)]
    zg = [[piece(c, 3 * W + h * HG_DIM, HG_DIM) for h in range(HG_HEADS)] for c in range(HG_SUB)]
    zx = [z_ref[:, 4 * W + a * XA_DIM:4 * W + (a + 1) * XA_DIM] for a in range(XA_HEADS)]
    return zq, zf, zi, zg, zx


def _hgrn_steps(zq, zf, zi, zg, zx, l0, l1, l2, gn, mk, mv, S):
    mix = []
    for c in range(HG_SUB):
        row, s_next = [], []
        for h in range(HG_HEADS):
            o, sn = _hgrn_head(zq[c][h], zf[c][h], zi[c][h], zg[c][h], l0[h], l1[h], l2[h], gn, S[h])
            row.append(o)
            s_next.append(sn)
        mix.append(row)
        S = s_next
    att = [_attention(zx[a], mk[a], mv[a]) for a in range(XA_HEADS)]
    return mix, att, S


def _hgrn_fwd2(z, lb_logits, gnorm, kv, bl, seq):
    T, zw = z.shape
    mem_len = kv.shape[0] // bl
    cat_w = HG_HEADS * HG_DIM + XA_HEADS * XA_DIM
    R = HG_SUB * HG_CHUNK
    nb = seq // R

    def kern(z_ref, lb_ref, gn_ref, kv_ref, cat_ref, st_ref, s_scr):
        @pl.when(pl.program_id(1) == 0)
        def _():
            s_scr[...] = jnp.zeros(s_scr.shape, F32)

        st_ref[...] = s_scr[...]
        zq, zf, zi, zg, zx = _hgrn_rows(z_ref)
        mk, mv = _kv_pieces(kv_ref)
        l0, l1, l2 = _lb_pieces(lb_ref)
        S = [s_scr[h] for h in range(HG_HEADS)]
        mix, att, s_new = _hgrn_steps(zq, zf, zi, zg, zx, l0, l1, l2, gn_ref[...], mk, mv, S)
        for c in range(HG_SUB):
            for h in range(HG_HEADS):
                cat_ref[c * HG_CHUNK:(c + 1) * HG_CHUNK, h * HG_DIM:(h + 1) * HG_DIM] = mix[c][h].astype(cat_ref.dtype)
        for h in range(HG_HEADS):
            s_scr[h] = s_new[h]
        base = HG_HEADS * HG_DIM
        for a in range(XA_HEADS):
            cat_ref[:, base + a * XA_DIM:base + (a + 1) * XA_DIM] = att[a].astype(cat_ref.dtype)

    return pl.pallas_call(
        kern, grid=(bl, nb),
        in_specs=[pl.BlockSpec((R, zw), lambda b, n: (b * nb + n, 0)),
                  pl.BlockSpec(lb_logits.shape, lambda b, n: (0, 0)),
                  pl.BlockSpec(gnorm.shape, lambda b, n: (0, 0)),
                  pl.BlockSpec((mem_len, kv.shape[1]), lambda b, n: (b, 0))],
        out_specs=[pl.BlockSpec((R, cat_w), lambda b, n: (b * nb + n, 0)),
                   pl.BlockSpec((None, HG_HEADS, HG_DIM, HG_DIM), lambda b, n: (b * nb + n, 0, 0, 0))],
        out_shape=[jax.ShapeDtypeStruct((T, cat_w), BF),
                   jax.ShapeDtypeStruct((bl * nb, HG_HEADS, HG_DIM, HG_DIM), F32)],
        scratch_shapes=[pltpu.VMEM((HG_HEADS, HG_DIM, HG_DIM), F32)],
        name="hgrn_fwd",
        compiler_params=pltpu.CompilerParams(dimension_semantics=("arbitrary", "arbitrary"), vmem_limit_bytes=VMEM_CAP_BYTES),
    )(z, lb_logits, gnorm, kv)


def _hgrn_bwd2(z, dcat, stash, lb_logits, gnorm, kv, bl, seq):
    T, zw = z.shape
    mem_len = kv.shape[0] // bl
    cat_w = dcat.shape[1]
    R = HG_SUB * HG_CHUNK
    nb = seq // R

    def kern(z_ref, dc_ref, st_ref, lb_ref, gn_ref, kv_ref, dz_ref, dkv_ref, dlb_ref, dgn_ref, ds_scr):
        first = jnp.logical_and(pl.program_id(0) == 0, pl.program_id(1) == 0)

        @pl.when(pl.program_id(1) == 0)
        def _():
            ds_scr[...] = jnp.zeros(ds_scr.shape, F32)
            dkv_ref[...] = jnp.zeros(dkv_ref.shape, F32)

        @pl.when(first)
        def _():
            dlb_ref[...] = jnp.zeros(dlb_ref.shape, F32)
            dgn_ref[...] = jnp.zeros(dgn_ref.shape, F32)

        zq, zf, zi, zg, zx = _hgrn_rows(z_ref)
        mk, mv = _kv_pieces(kv_ref)
        l0, l1, l2 = _lb_pieces(lb_ref)
        S = [st_ref[h] for h in range(HG_HEADS)]
        _, vjp = jax.vjp(_hgrn_steps, zq, zf, zi, zg, zx, l0, l1, l2, gn_ref[...], mk, mv, S)
        d_mix = [[dc_ref[c * HG_CHUNK:(c + 1) * HG_CHUNK, h * HG_DIM:(h + 1) * HG_DIM] for h in range(HG_HEADS)]
                 for c in range(HG_SUB)]
        base = HG_HEADS * HG_DIM
        d_att = [dc_ref[:, base + a * XA_DIM:base + (a + 1) * XA_DIM] for a in range(XA_HEADS)]
        d_s = [ds_scr[h] for h in range(HG_HEADS)]
        dzq, dzf, dzi, dzg, dzx, dl0, dl1, dl2, dgn, dmk, dmv, dS = vjp((d_mix, d_att, d_s))
        W = HG_HEADS * HG_DIM
        for c in range(HG_SUB):
            rows = slice(c * HG_CHUNK, (c + 1) * HG_CHUNK)
            for h in range(HG_HEADS):
                for k, part in enumerate((dzq, dzf, dzi, dzg)):
                    dz_ref[rows, k * W + h * HG_DIM:k * W + (h + 1) * HG_DIM] = part[c][h].astype(dz_ref.dtype)
        for h in range(HG_HEADS):
            sl = slice(h * HG_DIM, (h + 1) * HG_DIM)
            ds_scr[h] = dS[h]
            dlb_ref[0:1, sl] += dl0[h]
            dlb_ref[1:2, sl] += dl1[h]
            dlb_ref[2:3, sl] += dl2[h]
        dgn_ref[...] += dgn
        KW = XA_HEADS * XA_DIM
        for a in range(XA_HEADS):
            dz_ref[:, 4 * W + a * XA_DIM:4 * W + (a + 1) * XA_DIM] = dzx[a].astype(dz_ref.dtype)
            dkv_ref[:, a * XA_DIM:(a + 1) * XA_DIM] += dmk[a]
            dkv_ref[:, KW + a * XA_DIM:KW + (a + 1) * XA_DIM] += dmv[a]

    rev = lambda b, n: (b * nb + (nb - 1 - n), 0)
    return pl.pallas_call(
        kern, grid=(bl, nb),
        in_specs=[pl.BlockSpec((R, zw), rev),
                  pl.BlockSpec((R, cat_w), rev),
                  pl.BlockSpec((None, HG_HEADS, HG_DIM, HG_DIM), lambda b, n: (b * nb + (nb - 1 - n), 0, 0, 0)),
                  pl.BlockSpec(lb_logits.shape, lambda b, n: (0, 0)),
                  pl.BlockSpec(gnorm.shape, lambda b, n: (0, 0)),
                  pl.BlockSpec((mem_len, kv.shape[1]), lambda b, n: (b, 0))],
        out_specs=[pl.BlockSpec((R, zw), rev),
                   pl.BlockSpec((mem_len, kv.shape[1]), lambda b, n: (b, 0)),
                   pl.BlockSpec(lb_logits.shape, lambda b, n: (0, 0)),
                   pl.BlockSpec(gnorm.shape, lambda b, n: (0, 0))],
        out_shape=[jax.ShapeDtypeStruct((T, zw), BF), jax.ShapeDtypeStruct(kv.shape, F32),
                   jax.ShapeDtypeStruct(lb_logits.shape, F32), jax.ShapeDtypeStruct(gnorm.shape, F32)],
        scratch_shapes=[pltpu.VMEM((HG_HEADS, HG_DIM, HG_DIM), F32)],
        name="hgrn_bwd",
        compiler_params=pltpu.CompilerParams(dimension_semantics=("arbitrary", "arbitrary"), vmem_limit_bytes=VMEM_CAP_BYTES),
    )(z, dcat, stash, lb_logits, gnorm, kv)


GM_SUB = 2


def _gmlp_pieces(z_ref):
    W = GM_GROUPS * GM_GROUP_DIM
    zu = [z_ref[:, g * GM_GROUP_DIM:(g + 1) * GM_GROUP_DIM] for g in range(GM_GROUPS)]
    zv = [z_ref[:, W + g * GM_GROUP_DIM:W + (g + 1) * GM_GROUP_DIM] for g in range(GM_GROUPS)]
    zx = [z_ref[:, 2 * W + a * XA_DIM:2 * W + (a + 1) * XA_DIM] for a in range(XA_HEADS)]
    return zu, zv, zx


def _gmlp_params(lng_ref, lnb_ref, ws_ref, bs_ref):
    lng = [lng_ref[:, g * GM_GROUP_DIM:(g + 1) * GM_GROUP_DIM] for g in range(GM_GROUPS)]
    lnb = [lnb_ref[:, g * GM_GROUP_DIM:(g + 1) * GM_GROUP_DIM] for g in range(GM_GROUPS)]
    ws = [ws_ref[g] for g in range(GM_GROUPS)]
    bs = [bs_ref[g:g + 1, :] for g in range(GM_GROUPS)]
    return lng, lnb, ws, bs


def _gmlp_fwd(z, ln_g, ln_b, w_s, b_s, kv, bl, nc):
    T, zw = z.shape
    mem_len = kv.shape[0] // bl
    cat_w = GM_GROUPS * GM_GROUP_DIM + XA_HEADS * XA_DIM

    assert nc % GM_SUB == 0
    nc = nc // GM_SUB
    R = GM_SUB * GM_CHUNK

    def kern(z_ref, lng_ref, lnb_ref, ws_ref, bs_ref, kv_ref, cat_ref):
        lng, lnb, ws, bs = _gmlp_params(lng_ref, lnb_ref, ws_ref, bs_ref)
        mk, mv = _kv_pieces(kv_ref)
        for c in range(GM_SUB):
            rows = pl.ds(c * GM_CHUNK, GM_CHUNK)
            zu, zv, zx = _gmlp_pieces(z_ref.at[rows])
            out = cat_ref.at[rows]
            outs = _gmlp_block(zu, zv, zx, lng, lnb, ws, bs, mk, mv)
            for g in range(GM_GROUPS):
                out[:, g * GM_GROUP_DIM:(g + 1) * GM_GROUP_DIM] = outs[g].astype(cat_ref.dtype)
            base = GM_GROUPS * GM_GROUP_DIM
            for a in range(XA_HEADS):
                out[:, base + a * XA_DIM:base + (a + 1) * XA_DIM] = outs[GM_GROUPS + a].astype(cat_ref.dtype)

    full2 = lambda b, n: (0, 0)
    return pl.pallas_call(
        kern, grid=(bl, nc),
        in_specs=[pl.BlockSpec((R, zw), lambda b, n: (b * nc + n, 0)),
                  pl.BlockSpec(ln_g.shape, full2), pl.BlockSpec(ln_b.shape, full2),
                  pl.BlockSpec(w_s.shape, lambda b, n: (0, 0, 0)), pl.BlockSpec(b_s.shape, full2),
                  pl.BlockSpec((mem_len, kv.shape[1]), lambda b, n: (b, 0))],
        out_specs=pl.BlockSpec((R, cat_w), lambda b, n: (b * nc + n, 0)),
        out_shape=jax.ShapeDtypeStruct((T, cat_w), BF),
        name="gmlp_fwd",
        compiler_params=pltpu.CompilerParams(dimension_semantics=("arbitrary", "arbitrary"), vmem_limit_bytes=VMEM_CAP_BYTES),
    )(z, ln_g, ln_b, w_s, b_s, kv)


def _gmlp_bwd(z, dcat, ln_g, ln_b, w_s, b_s, kv, bl, nc):
    T, zw = z.shape
    mem_len = kv.shape[0] // bl
    cat_w = dcat.shape[1]
    assert nc % GM_SUB == 0
    nc = nc // GM_SUB

    def kern(z_ref, dc_ref, lng_ref, lnb_ref, ws_ref, bs_ref, kv_ref,
             dz_ref, dkv_ref, dlng_ref, dlnb_ref, dws_ref, dbs_ref):
        first = jnp.logical_and(pl.program_id(0) == 0, pl.program_id(1) == 0)

        @pl.when(pl.program_id(1) == 0)
        def _():
            dkv_ref[...] = jnp.zeros(dkv_ref.shape, F32)

        @pl.when(first)
        def _():
            dlng_ref[...] = jnp.zeros(dlng_ref.shape, F32)
            dlnb_ref[...] = jnp.zeros(dlnb_ref.shape, F32)
            dws_ref[...] = jnp.zeros(dws_ref.shape, F32)
            dbs_ref[...] = jnp.zeros(dbs_ref.shape, F32)

        lng, lnb, ws, bs = _gmlp_params(lng_ref, lnb_ref, ws_ref, bs_ref)
        mk, mv = _kv_pieces(kv_ref)
        W = GM_GROUPS * GM_GROUP_DIM
        KW = XA_HEADS * XA_DIM
        for c in range(GM_SUB):
            rows = pl.ds(c * GM_CHUNK, GM_CHUNK)
            zu, zv, zx = _gmlp_pieces(z_ref.at[rows])
            dc, dz = dc_ref.at[rows], dz_ref.at[rows]
            _, vjp = jax.vjp(_gmlp_block, zu, zv, zx, lng, lnb, ws, bs, mk, mv)
            d_outs = [dc[:, g * GM_GROUP_DIM:(g + 1) * GM_GROUP_DIM] for g in range(GM_GROUPS)]
            d_outs += [dc[:, W + a * XA_DIM:W + (a + 1) * XA_DIM] for a in range(XA_HEADS)]
            dzu, dzv, dzx, dlng, dlnb, dws, dbs, dmk, dmv = vjp(d_outs)
            for g in range(GM_GROUPS):
                sl = slice(g * GM_GROUP_DIM, (g + 1) * GM_GROUP_DIM)
                dz[:, sl] = dzu[g].astype(dz_ref.dtype)
                dz[:, W + g * GM_GROUP_DIM:W + (g + 1) * GM_GROUP_DIM] = dzv[g].astype(dz_ref.dtype)
                dlng_ref[:, sl] += dlng[g]
                dlnb_ref[:, sl] += dlnb[g]
                dws_ref[g] += dws[g]
                dbs_ref[g:g + 1, :] += dbs[g]
            for a in range(XA_HEADS):
                dz[:, 2 * W + a * XA_DIM:2 * W + (a + 1) * XA_DIM] = dzx[a].astype(dz_ref.dtype)
                dkv_ref[:, a * XA_DIM:(a + 1) * XA_DIM] += dmk[a]
                dkv_ref[:, KW + a * XA_DIM:KW + (a + 1) * XA_DIM] += dmv[a]

    full2 = lambda b, n: (0, 0)
    full3 = lambda b, n: (0, 0, 0)
    blk = lambda b, n: (b * nc + n, 0)
    return pl.pallas_call(
        kern, grid=(bl, nc),
        in_specs=[pl.BlockSpec((GM_SUB * GM_CHUNK, zw), blk), pl.BlockSpec((GM_SUB * GM_CHUNK, cat_w), blk),
                  pl.BlockSpec(ln_g.shape, full2), pl.BlockSpec(ln_b.shape, full2),
                  pl.BlockSpec(w_s.shape, full3), pl.BlockSpec(b_s.shape, full2),
                  pl.BlockSpec((mem_len, kv.shape[1]), lambda b, n: (b, 0))],
        out_specs=[pl.BlockSpec((GM_SUB * GM_CHUNK, zw), blk),
                   pl.BlockSpec((mem_len, kv.shape[1]), lambda b, n: (b, 0)),
                   pl.BlockSpec(ln_g.shape, full2), pl.BlockSpec(ln_b.shape, full2),
                   pl.BlockSpec(w_s.shape, full3), pl.BlockSpec(b_s.shape, full2)],
        out_shape=[jax.ShapeDtypeStruct((T, zw), BF), jax.ShapeDtypeStruct(kv.shape, F32),
                   jax.ShapeDtypeStruct(ln_g.shape, F32), jax.ShapeDtypeStruct(ln_b.shape, F32),
                   jax.ShapeDtypeStruct(w_s.shape, F32), jax.ShapeDtypeStruct(b_s.shape, F32)],
        name="gmlp_bwd",
        compiler_params=pltpu.CompilerParams(dimension_semantics=("arbitrary", "arbitrary"), vmem_limit_bytes=VMEM_CAP_BYTES),
    )(z, dcat, ln_g, ln_b, w_s, b_s, kv)


def _place():
    x, y, c = lax.axis_index("x"), lax.axis_index("y"), lax.axis_index("c")
    chips = [(1 - x, y), (x, 1 - y), (1 - x, 1 - y)]
    return x, y, c, chips


def _half(ref, kind, e):
    if kind == "col":
        n = ref.shape[1] // 2
        return ref.at[:, pl.ds(pl.multiple_of(e * n, n), n), :]
    n = ref.shape[2] // 2
    return ref.at[:, :, pl.ds(pl.multiple_of(e * n, n), n)]


def _slot(ref, kind, j, n):
    if kind == "col":
        return ref.at[:, :, pl.ds(pl.multiple_of(j * n, n), n)]
    return ref.at[:, pl.ds(pl.multiple_of(j * n, n), n), :]


def _allgather_seq(name, items, cid):
    nt = len(items)
    kinds = [k for (_, k, _) in items]
    slot_kind = ["row" if k == "row" else "col" for k in kinds]
    out_type = []
    for s, k, l in items:
        L, r, c = s.shape
        lo = L if l is None else 1
        out_type.append(jax.ShapeDtypeStruct((lo, 4 * r, c) if k == "row" else (lo, r, 4 * c), s.dtype))

    def part(ref, t, e):
        return ref if kinds[t] == "vec" else _half(ref, kinds[t], e)

    def body(*refs):
        sh = [refs[t] if items[t][2] is None else refs[t].at[pl.ds(items[t][2], 1)] for t in range(nt)]
        full = refs[nt:2 * nt]
        loc, s_ici, r_ici, s_d2d, r_d2d = refs[2 * nt:]
        x, y, c, chips = _place()
        own = 2 * x + y
        sibling = (x, y, 1 - c)
        barrier = pltpu.get_barrier_semaphore()
        for peer in [(px, py, c) for (px, py) in chips] + [sibling]:
            pl.semaphore_signal(barrier, inc=1, device_id=peer, device_id_type=MESH)
        pl.semaphore_wait(barrier, 4)
        width = [sh[t].shape[1] if kinds[t] == "row" else sh[t].shape[2] for t in range(nt)]
        started = []
        for t in range(nt):
            mine = pltpu.make_async_copy(sh[t], _slot(full[t], slot_kind[t], own, width[t]), loc.at[t])
            mine.start()
            started.append(mine)
        sent = []
        for t in range(nt):
            for p, (px, py) in enumerate(chips):
                cp = pltpu.make_async_remote_copy(
                    src_ref=part(sh[t], t, c), dst_ref=part(_slot(full[t], slot_kind[t], own, width[t]), t, c),
                    send_sem=s_ici.at[t, p], recv_sem=r_ici.at[t, p], device_id=(px, py, c), device_id_type=MESH)
                cp.start()
                sent.append(cp)
        for t in range(nt):
            for p, (px, py) in enumerate(chips):
                landed = part(_slot(full[t], slot_kind[t], 2 * px + py, width[t]), t, c)
                pltpu.make_async_remote_copy(
                    src_ref=landed, dst_ref=landed, send_sem=s_ici.at[t, p], recv_sem=r_ici.at[t, p],
                    device_id=(px, py, c), device_id_type=MESH).wait_recv()
                if kinds[t] == "vec":
                    continue
                fw = pltpu.make_async_remote_copy(
                    src_ref=landed, dst_ref=landed, send_sem=s_d2d.at[t, p], recv_sem=r_d2d.at[t, p],
                    device_id=sibling, device_id_type=MESH)
                fw.start()
                sent.append(fw)
        for t in range(nt):
            if kinds[t] == "vec":
                continue
            for p, (px, py) in enumerate(chips):
                other = _half(_slot(full[t], kinds[t], 2 * px + py, width[t]), kinds[t], 1 - c)
                pltpu.make_async_remote_copy(
                    src_ref=other, dst_ref=other, send_sem=s_d2d.at[t, p], recv_sem=r_d2d.at[t, p],
                    device_id=sibling, device_id_type=MESH).wait_recv()
        for cp in sent:
            cp.wait_send()
        for cp in started:
            cp.wait()

    return pl.kernel(
        body, out_type=out_type, mesh=plsc.ScalarSubcoreMesh(axis_name="seq", num_cores=1),
        scratch_types=[pltpu.SemaphoreType.DMA((nt,)), pltpu.SemaphoreType.DMA((nt, 3)), pltpu.SemaphoreType.DMA((nt, 3)),
                       pltpu.SemaphoreType.DMA((nt, 3)), pltpu.SemaphoreType.DMA((nt, 3))],
        compiler_params=pltpu.CompilerParams(collective_id=cid), name=name,
    )(*[s for (s, _, _) in items])


def _slot2(ref, kind, j, n):
    if kind == "col":
        return ref.at[:, pl.ds(pl.multiple_of(j * n, n), n)]
    return ref.at[pl.ds(pl.multiple_of(j * n, n), n), :]


def _rs_chips_seq(name, parts, kinds, cid):
    nm = len(parts)
    out_type = []
    for g, k in zip(parts, kinds):
        r, c = g.shape
        ps = (r, c // 4) if k == "col" else (r // 4, c)
        out_type += [jax.ShapeDtypeStruct(ps, BF), jax.ShapeDtypeStruct((3,) + ps, BF)]

    def body(*refs):
        g = refs[:nm]
        outs = refs[nm:3 * nm]
        loc, ssem, rsem = refs[3 * nm:]
        x, y, c, chips = _place()
        own = 2 * x + y
        barrier = pltpu.get_barrier_semaphore()
        for (px, py) in chips:
            pl.semaphore_signal(barrier, inc=1, device_id=(px, py, c), device_id_type=MESH)
        pl.semaphore_wait(barrier, 3)
        cps = []
        for m in range(nm):
            k = kinds[m]
            own_o, got_o = outs[2 * m], outs[2 * m + 1]
            n = g[m].shape[1] // 4 if k == "col" else g[m].shape[0] // 4
            lc = pltpu.make_async_copy(_slot2(g[m], k, own, n), own_o, loc.at[m])
            lc.start()
            cps.append(lc)
            for p, (px, py) in enumerate(chips):
                cp = pltpu.make_async_remote_copy(
                    src_ref=_slot2(g[m], k, 2 * px + py, n), dst_ref=got_o.at[p],
                    send_sem=ssem.at[m, p], recv_sem=rsem.at[m, p], device_id=(px, py, c), device_id_type=MESH)
                cp.start()
                cps.append(cp)
        for cp in cps:
            cp.wait()

    return pl.kernel(
        body, out_type=out_type, mesh=plsc.ScalarSubcoreMesh(axis_name="seq", num_cores=1),
        scratch_types=[pltpu.SemaphoreType.DMA((nm,)), pltpu.SemaphoreType.DMA((nm, 3)), pltpu.SemaphoreType.DMA((nm, 3))],
        compiler_params=pltpu.CompilerParams(collective_id=cid), name=name,
    )(*parts)


def _finish_share(name, owns, gots, kind, c_arr):
    L = len(owns)
    r, c = owns[0].shape
    tr = _pick(r, 128 if kind == "col" else 256)
    nb = r // tr
    nq = L * nb

    def chunk_of(l):
        return lambda h, q: jnp.clip(q * (1 - h) + (nq - 1) * h - l * nb, 0, nb - 1)

    ins, in_specs = [], []
    for l in range(L):
        at = chunk_of(l)
        ins += [owns[l], gots[l].reshape(3 * r, c), gots[l].reshape(3 * r, c), gots[l].reshape(3 * r, c)]
        in_specs.append(pl.BlockSpec((tr, c), functools.partial(lambda h, q, cc, at: (at(h, q), 0), at=at)))
        in_specs += [pl.BlockSpec((tr, c), functools.partial(lambda h, q, cc, at, p: (p * nb + at(h, q), 0), at=at, p=p))
                     for p in range(3)]
    if kind == "col":
        out_sd = (L, 2, r, c)
        o_spec = pl.BlockSpec((None, 2, tr, c), lambda h, q, cc: ((q * h) // nb, 0, (q * h) % nb, 0))
    else:
        out_sd = (L * r, 2 * c)
        o_spec = pl.BlockSpec((tr, 2 * c), lambda h, q, cc: (q * h, 0))

    def kern(c_ref, *refs):
        in_refs = refs[:4 * L]
        out_ref, mine, recv, ssem, rsem = refs[4 * L:]
        h, q = pl.program_id(0), pl.program_id(1)
        x, y, cc, _ = _place()

        def swap(qq):
            return pltpu.make_async_remote_copy(src_ref=mine.at[qq], dst_ref=recv.at[qq], send_sem=ssem.at[qq],
                                                recv_sem=rsem.at[qq], device_id=(x, y, 1 - cc), device_id_type=MESH)

        for l in range(L):
            @pl.when(jnp.logical_and(h == 0, q // nb == l))
            def _(l=l):
                o_ref, g0, g1, g2 = in_refs[4 * l:4 * l + 4]
                mine[q] = ((o_ref[...].astype(F32) + g0[...].astype(F32)) + g1[...].astype(F32)) + g2[...].astype(F32)
                swap(q).start()

        @pl.when(h == 1)
        def _():
            swap(q).wait()
            a, b = mine[q], recv[q]
            first = c_ref[0] == 0
            lo, hi = jnp.where(first, a, b), jnp.where(first, b, a)
            if kind == "col":
                out_ref[0] = lo
                out_ref[1] = hi
            else:
                out_ref[:, :c] = lo
                out_ref[:, c:] = hi

    full = pl.pallas_call(
        kern,
        grid_spec=pltpu.PrefetchScalarGridSpec(
            num_scalar_prefetch=1, grid=(2, nq), in_specs=in_specs, out_specs=o_spec,
            scratch_shapes=[pltpu.VMEM((nq, tr, c), F32), pltpu.VMEM((nq, tr, c), F32),
                            pltpu.SemaphoreType.DMA((nq,)), pltpu.SemaphoreType.DMA((nq,))]),
        out_shape=jax.ShapeDtypeStruct(out_sd, F32), name=name,
        compiler_params=pltpu.CompilerParams(dimension_semantics=("arbitrary", "arbitrary"),
                                             vmem_limit_bytes=VMEM_CAP_BYTES),
    )(c_arr, *ins)
    return full.reshape(L, 2 * r, c) if kind == "col" else full.reshape(L, r, 2 * c)


def _small_allreduce(buf, name):
    R = buf.shape[0]
    assert R % 16 == 0
    h = R // 2

    def body(x_ref, o_ref, sib, csum, got, s_a, r_a, s_b, r_b, s_c, r_c):
        x, y, c, chips = _place()
        sibling = (x, y, 1 - c)
        own = 2 * x + y
        swap = pltpu.make_async_remote_copy(src_ref=x_ref, dst_ref=sib, send_sem=s_a, recv_sem=r_a,
                                            device_id=sibling, device_id_type=MESH)
        swap.start()
        swap.wait()
        a, b = x_ref[...], sib[...]
        south = c == 0
        csum[...] = jnp.where(south, a, b) + jnp.where(south, b, a)
        lo = pl.multiple_of(c * h, 8)
        mine = csum.at[pl.ds(lo, h)]
        got[own] = csum[pl.ds(lo, h)]
        sends = []
        for p, (px, py) in enumerate(chips):
            cp = pltpu.make_async_remote_copy(src_ref=mine, dst_ref=got.at[own], send_sem=s_b.at[p], recv_sem=r_b.at[p],
                                              device_id=(px, py, c), device_id_type=MESH)
            cp.start()
            sends.append(cp)
        for cp in sends:
            cp.wait()
        o_ref[pl.ds(lo, h)] = ((got[0] + got[1]) + got[2]) + got[3]
        done = o_ref.at[pl.ds(lo, h)]
        back = pltpu.make_async_remote_copy(src_ref=done, dst_ref=done, send_sem=s_c, recv_sem=r_c,
                                            device_id=sibling, device_id_type=MESH)
        back.start()
        back.wait_send()
        other = o_ref.at[pl.ds(pl.multiple_of((1 - c) * h, 8), h)]
        pltpu.make_async_remote_copy(src_ref=other, dst_ref=other, send_sem=s_c, recv_sem=r_c,
                                     device_id=sibling, device_id_type=MESH).wait_recv()

    vm = pl.BlockSpec(memory_space=pltpu.VMEM)
    return pl.pallas_call(
        body, out_shape=jax.ShapeDtypeStruct(buf.shape, F32), in_specs=[vm], out_specs=vm,
        scratch_shapes=[pltpu.VMEM((R, LANES), F32), pltpu.VMEM((R, LANES), F32), pltpu.VMEM((4, h, LANES), F32),
                        pltpu.SemaphoreType.DMA, pltpu.SemaphoreType.DMA, pltpu.SemaphoreType.DMA((3,)),
                        pltpu.SemaphoreType.DMA((3,)), pltpu.SemaphoreType.DMA, pltpu.SemaphoreType.DMA],
        name=name,
        compiler_params=pltpu.CompilerParams(vmem_limit_bytes=VMEM_CAP_BYTES),
    )(buf)


PACK_TILE_ROWS = 8


def _item_rows(shape):
    n = 1
    for d in shape:
        n *= d
    return -(-n // (PACK_TILE_ROWS * LANES)) * PACK_TILE_ROWS


def _pack(arrs, rows_total):
    buf = jnp.zeros((rows_total, LANES), F32)
    r = 0
    for a in arrs:
        f = a.reshape(-1).astype(F32)
        nr = _item_rows(a.shape)
        block = jnp.pad(f, (0, nr * LANES - f.shape[0])).reshape(nr, LANES)
        buf = lax.dynamic_update_slice(buf, block, (r, 0))
        r += nr
    return buf


def _unpack(buf, shapes):
    out, r = [], 0
    for s in shapes:
        n = 1
        for d in s:
            n *= d
        nr = _item_rows(s)
        out.append(buf[r:r + nr].reshape(-1)[:n].reshape(s))
        r += nr
    return out


def _rows_needed(shapes):
    return -(-sum(_item_rows(s) for s in shapes) // (2 * PACK_TILE_ROWS)) * (2 * PACK_TILE_ROWS)


def _two_rows(a, b):
    out = jnp.zeros((2, a.shape[1]), a.dtype)
    return lax.dynamic_update_slice(lax.dynamic_update_slice(out, a, (0, 0)), b, (1, 0))


def _adam(w, g, m, v):
    m = ADAM_B1 * m + (1.0 - ADAM_B1) * g
    v = ADAM_B2 * v + (1.0 - ADAM_B2) * jnp.square(g)
    m_hat = m / (1.0 - ADAM_B1 ** ADAM_STEP)
    v_hat = v / (1.0 - ADAM_B2 ** ADAM_STEP)
    delta = -ADAM_LR * (m_hat / (jnp.sqrt(v_hat) + ADAM_EPS) + ADAM_WD * w)
    return delta, m, v


def _adam_call(name, w2, g2, m2, v2, tr):
    def fn(rv, cv):
        return list(_adam(*rv)), []

    width = w2.shape[1]
    return _rowcall(name, fn, [(w2, 0, width), (g2, 0, width), (m2, 0, width), (v2, 0, width)], [],
                    [(width, F32)] * 3, [], tr)


def kernel(x, mem, mem_norm, lb_logits, ffn1_norm, ffn1_w_in, ffn1_w_out, mix_norm, mem_w_kv, hgrn_w_in, hgrn_gnorm, hgrn_w_out, gmlp_w_in, gmlp_ln_g, gmlp_ln_b, gmlp_w_s, gmlp_b_s, gmlp_w_out, ffn2_norm, ffn2_w_in, ffn2_w_out, final_norm, loss_target, m_mem_norm, m_lb_logits, m_ffn1_norm, m_ffn1_w_in, m_ffn1_w_out, m_mix_norm, m_mem_w_kv, m_hgrn_w_in, m_hgrn_gnorm, m_hgrn_w_out, m_gmlp_w_in, m_gmlp_ln_g, m_gmlp_ln_b, m_gmlp_w_s, m_gmlp_b_s, m_gmlp_w_out, m_ffn2_norm, m_ffn2_w_in, m_ffn2_w_out, m_final_norm, v_mem_norm, v_lb_logits, v_ffn1_norm, v_ffn1_w_in, v_ffn1_w_out, v_mix_norm, v_mem_w_kv, v_hgrn_w_in, v_hgrn_gnorm, v_hgrn_w_out, v_gmlp_w_in, v_gmlp_ln_g, v_gmlp_ln_b, v_gmlp_w_s, v_gmlp_b_s, v_gmlp_w_out, v_ffn2_norm, v_ffn2_w_in, v_ffn2_w_out, v_final_norm):
    bl, seq, D = x.shape
    T = bl * seq
    mem_len = mem.shape[1]
    chip = 2 * lax.axis_index("x") + lax.axis_index("y")
    c_arr = lax.axis_index("c").astype(jnp.int32).reshape(1)
    TR = 1024

    big = [("ffn1_w_in", ffn1_w_in, "col"), ("ffn1_w_out", ffn1_w_out, "row"), ("mem_w_kv", mem_w_kv, "col"),
           ("hgrn_w_in", hgrn_w_in, "col"), ("hgrn_w_out", hgrn_w_out, "row"), ("gmlp_w_in", gmlp_w_in, "col"),
           ("gmlp_w_out", gmlp_w_out, "row"), ("ffn2_w_in", ffn2_w_in, "col"), ("ffn2_w_out", ffn2_w_out, "row")]
    kinds = [k for (_, _, k) in big]
    shards_bf = []
    for nm, w, _ in big:
        L, r, c = w.shape
        (wb,) = _rowcall("cast_" + nm, lambda rv, cv: ([rv[0]], []), [(w.reshape(L * r, c), 0, c)], [], [(c, BF)], [], 512)
        shards_bf.append(wb.reshape(L, r, c))
    sb = dict(zip([nm for (nm, _, _) in big], shards_bf))
    groups = [[("ffn1_w_in", 0)], [("ffn1_w_out", 0)], [("hgrn_w_in", None)], [("mem_w_kv", None)], [("hgrn_w_out", None)],
              [("ffn2_w_in", 0), ("ffn2_w_out", 0), ("gmlp_ln_g", None), ("gmlp_ln_b", None)],
              [("ffn1_w_in", 1), ("ffn1_w_out", 1)],
              [("gmlp_w_in", None), ("gmlp_w_out", None)],
              [("ffn2_w_in", 1), ("ffn2_w_out", 1)]]
    kind_of = {nm: k for (nm, _, k) in big}
    for nm, vec in (("gmlp_ln_g", gmlp_ln_g), ("gmlp_ln_b", gmlp_ln_b)):
        sb[nm] = vec.reshape(1, 1, -1)
        kind_of[nm] = "vec"
    gathered = {nm: [None, None] for nm in ("ffn1_w_in", "ffn1_w_out", "ffn2_w_in", "ffn2_w_out")}
    for gi, grp in enumerate(groups):
        outs = _allgather_seq("gather_%d" % gi, [(sb[nm], kind_of[nm], l) for (nm, l) in grp], gi)
        for (nm, l), o in zip(grp, outs):
            if l is None:
                gathered[nm] = o
            else:
                gathered[nm][l] = o

    ln_w = GM_GROUPS * GM_GROUP_DIM
    ln_g_full, ln_b_full = gathered["gmlp_ln_g"].reshape(1, ln_w), gathered["gmlp_ln_b"].reshape(1, ln_w)

    def rms_fwd(name, xin, g):
        (h,) = _rowcall(name, lambda rv, cv: ([_rmsnorm(rv[0], cv[0])], []), [(xin, 0, D)], [g.reshape(1, D)], [(D, BF)], [], TR)
        return h

    def ffn_fwd(tag, xin, h, w_in, w_out, layer, next_gain):
        dff = w_out[layer].shape[1]
        zg, zu, a = _ffn_in_swiglu("ffn_in_" + tag, h, w_in[layer], 1024, dff // 2)
        out = _mm("ffn_out_" + tag, a, w_out[layer], "nn", F32, 1024, 1024, dff, scale=0.5, res=xin, b_lead=0,
                  norm_gain=None if next_gain is None else next_gain.reshape(1, D))
        xo, h_next = (out, None) if next_gain is None else out
        return xo, h_next, (xin, h, zg, zu, a)

    def ffn_bwd(tag, dxo, saved, g, w_in, w_out, layer):
        xin, h, zg, zu, a = saved
        dff = w_out[layer].shape[1]
        dw_out = _mm_tn_pair("ffn_dwo_" + tag, a, dxo, "row", c_arr, dff // 2, T, scale=0.5)
        dz = _ffn_da_swiglu("ffn_da_" + tag, dxo, w_out[layer], zg, zu, 512)
        dw_in = _mm_tn_pair("ffn_dwi_" + tag, h, dz, "col", c_arr, 512, T)
        dx, dg = _mm_dh_rms("ffn_dh_" + tag, dz, w_in[layer], xin, g.reshape(1, D), dxo, 512)
        return dx, dg, dw_in, dw_out

    def rms_bwd(name, xin, g, dh, dres):
        def fn(rv, cv):
            _, vjp = jax.vjp(_rmsnorm, rv[0], cv[0])
            dx, dg = vjp(rv[1])
            if dres is not None:
                dx = dx + rv[2]
            return [dx], [dg]

        rows = [(xin, 0, D), (dh, 0, D)] + ([(dres, 0, D)] if dres is not None else [])
        dx, dg = _rowcall(name, fn, rows, [g.reshape(1, D)], [(D, F32)], [((1, D), F32)], TR)
        return dx, dg

    x0 = x.reshape(T, D)
    tgt = loss_target.reshape(T, D)
    mem2 = mem.reshape(bl * mem_len, D)
    memn = rms_fwd("rms_mem", mem2, mem_norm)

    h_f10 = rms_fwd("rms_f1l0", x0, ffn1_norm[0])
    x1, h_m0, sv_f10 = ffn_fwd("f1l0", x0, h_f10, gathered["ffn1_w_in"], gathered["ffn1_w_out"], 0, mix_norm[0])
    z_m0 = _mm("mix_in_0", h_m0, gathered["hgrn_w_in"], "nn", F32, 2048, 512, D, b_lead=0)
    kv = [_mm("kv_%d" % i, memn, gathered["mem_w_kv"], "nn", F32, 512, 512, D, b_lead=i) for i in range(2)]
    cat0, stash0 = _hgrn_fwd2(z_m0, lb_logits, hgrn_gnorm, kv[0], bl, seq)
    x2, h_f20 = _mm("mix_out_0", cat0, gathered["hgrn_w_out"], "nn", F32, 1024, 1024, cat0.shape[1], res=x1, b_lead=0,
                    norm_gain=ffn2_norm[0].reshape(1, D))
    x3, h_f11, sv_f20 = ffn_fwd("f2l0", x2, h_f20, gathered["ffn2_w_in"], gathered["ffn2_w_out"], 0, ffn1_norm[1])
    x4, h_m1, sv_f11 = ffn_fwd("f1l1", x3, h_f11, gathered["ffn1_w_in"], gathered["ffn1_w_out"], 1, mix_norm[1])
    z_m1 = _mm("mix_in_1", h_m1, gathered["gmlp_w_in"], "nn", F32, 2048, 512, D, b_lead=0)
    nc1 = seq // GM_CHUNK
    w_s, b_s = gmlp_w_s[0], gmlp_b_s[0]
    cat1 = _gmlp_fwd(z_m1, ln_g_full, ln_b_full, w_s, b_s, kv[1], bl, nc1)
    x5, h_f21 = _mm("mix_out_1", cat1, gathered["gmlp_w_out"], "nn", F32, 1024, 1024, cat1.shape[1], res=x4, b_lead=0,
                    norm_gain=ffn2_norm[1].reshape(1, D))
    x6, _, sv_f21 = ffn_fwd("f2l1", x5, h_f21, gathered["ffn2_w_in"], gathered["ffn2_w_out"], 1, None)

    def head(rv, cv):
        def f(xx, gg):
            err = _rmsnorm(xx, gg) - rv[1]
            return 0.5 * jnp.sum(jnp.mean(err * err, axis=-1, keepdims=True), axis=0, keepdims=True)

        ls, vjp = jax.vjp(f, rv[0], cv[0])
        dx, dg = vjp(jnp.ones((1, 1), F32))
        return [dx], [dg, jnp.broadcast_to(ls, (1, 128))]

    dx6, d_final, loss_part = _rowcall("loss_head", head, [(x6, 0, D), (tgt, 0, D)], [final_norm.reshape(1, D)],
                                       [(D, F32)], [((1, D), F32), ((1, 128), F32)], TR)

    rs_out = {}
    n_gather = len(groups)

    def rs(gi, items):
        outs = _rs_chips_seq("reduce_%d" % gi, [p for (_, p, _) in items], [k for (_, _, k) in items], n_gather + gi)
        for i, (key, _, _) in enumerate(items):
            rs_out[key] = (outs[2 * i], outs[2 * i + 1])

    dx5, dg_f21, dwi_f21, dwo_f21 = ffn_bwd("f2l1", dx6, sv_f21, ffn2_norm[1], gathered["ffn2_w_in"], gathered["ffn2_w_out"], 1)
    rs(0, [(("ffn2_w_out", 1), dwo_f21, "row"), (("ffn2_w_in", 1), dwi_f21, "col")])
    dcat1 = _mm("mix_dcat_1", dx5, gathered["gmlp_w_out"], "nt", F32, 2048, 1024, D, b_lead=0)
    dwo_m1 = _mm_tn_pair("mix_dwo_1", cat1, dx5, "row", c_arr, 1024, T)
    dz_m1, dkv1, d_lng, d_lnb, d_ws, d_bs = _gmlp_bwd(z_m1, dcat1, ln_g_full, ln_b_full, w_s, b_s, kv[1], bl, nc1)
    dx4, dg_m1 = _mm_dh_rms("mix_dh_1", dz_m1, gathered["gmlp_w_in"], x4, mix_norm[1].reshape(1, D), dx5, 512)
    dwi_m1 = _mm_tn_pair("mix_dwi_1", h_m1, dz_m1, "col", c_arr, 1024, T)
    rs(1, [(("gmlp_w_out", 0), dwo_m1, "row"), (("gmlp_w_in", 0), dwi_m1, "col")])
    dx3, dg_f11, dwi_f11, dwo_f11 = ffn_bwd("f1l1", dx4, sv_f11, ffn1_norm[1], gathered["ffn1_w_in"], gathered["ffn1_w_out"], 1)
    rs(2, [(("ffn1_w_out", 1), dwo_f11, "row"), (("ffn1_w_in", 1), dwi_f11, "col")])

    dx2, dg_f20, dwi_f20, dwo_f20 = ffn_bwd("f2l0", dx3, sv_f20, ffn2_norm[0], gathered["ffn2_w_in"], gathered["ffn2_w_out"], 0)
    rs(3, [(("ffn2_w_out", 0), dwo_f20, "row"), (("ffn2_w_in", 0), dwi_f20, "col")])
    dcat0 = _mm("mix_dcat_0", dx2, gathered["hgrn_w_out"], "nt", F32, 2048, 1024, D, b_lead=0)
    dwo_m0 = _mm_tn_pair("mix_dwo_0", cat0, dx2, "row", c_arr, 1024, T)
    dz_m0, dkv0, d_lb, d_gn = _hgrn_bwd2(z_m0, dcat0, stash0, lb_logits, hgrn_gnorm, kv[0], bl, seq)
    dx1, dg_m0 = _mm_dh_rms("mix_dh_0", dz_m0, gathered["hgrn_w_in"], x1, mix_norm[0].reshape(1, D), dx2, 512)
    dwi_m0 = _mm_tn_pair("mix_dwi_0", h_m0, dz_m0, "col", c_arr, 1024, T)
    rs(4, [(("hgrn_w_out", 0), dwo_m0, "row"), (("hgrn_w_in", 0), dwi_m0, "col")])

    dwkv = [_mm_tn_pair("kv_dw_%d" % i, memn, dkv, "col", c_arr, 1024, 512) for i, dkv in enumerate([dkv0, dkv1])]
    rs(5, [(("mem_w_kv", 0), dwkv[0], "col"), (("mem_w_kv", 1), dwkv[1], "col")])
    dmemn = _mm("kv_dx_0", dkv0, gathered["mem_w_kv"], "nt", F32, 512, 512, 1024, b_lead=0)
    dmemn = _mm("kv_dx_1", dkv1, gathered["mem_w_kv"], "nt", F32, 512, 512, 1024, res=dmemn, b_lead=1)
    _, d_memnorm = rms_bwd("rms_bwd_mem", mem2, mem_norm, dmemn, None)

    dx0, dg_f10, dwi_f10, dwo_f10 = ffn_bwd("f1l0", dx1, sv_f10, ffn1_norm[0], gathered["ffn1_w_in"], gathered["ffn1_w_out"], 0)
    rs(6, [(("ffn1_w_out", 0), dwo_f10, "row")])
    rs(7, [(("ffn1_w_in", 0), dwi_f10, "col")])

    shard_grads = [_finish_share("finish_" + nm, [rs_out[(nm, l)][0] for l in range(w.shape[0])],
                                 [rs_out[(nm, l)][1] for l in range(w.shape[0])], k, c_arr) for (nm, w, k) in big]

    big_w = [w for (_, w, _) in big]
    big_m = [m_ffn1_w_in, m_ffn1_w_out, m_mem_w_kv, m_hgrn_w_in, m_hgrn_w_out, m_gmlp_w_in, m_gmlp_w_out, m_ffn2_w_in, m_ffn2_w_out]
    big_v = [v_ffn1_w_in, v_ffn1_w_out, v_mem_w_kv, v_hgrn_w_in, v_hgrn_w_out, v_gmlp_w_in, v_gmlp_w_out, v_ffn2_w_in, v_ffn2_w_out]
    big_out = {}
    for (nm, w, _), g, m, v in zip(big, shard_grads, big_m, big_v):
        L, r, c = w.shape
        d2, m2, v2 = _adam_call("adam_" + nm, w.reshape(L * r, c), g.reshape(L * r, c), m.reshape(L * r, c),
                                v.reshape(L * r, c), 256)
        big_out[nm] = (g, d2.reshape(w.shape), m2.reshape(w.shape), v2.reshape(w.shape))

    d_ffn1n = _two_rows(dg_f10, dg_f11)
    d_mixn = _two_rows(dg_m0, dg_m1)
    d_ffn2n = _two_rows(dg_f20, dg_f21)
    small_parts = [loss_part[:, :1], d_memnorm, d_lb, d_ffn1n, d_mixn, d_gn, d_lng, d_lnb, d_ws, d_bs, d_ffn2n, d_final]
    red_shapes = [(1,), mem_norm.shape, lb_logits.shape, ffn1_norm.shape, mix_norm.shape, hgrn_gnorm.shape, (1, ln_w), (1, ln_w),
                  gmlp_w_s.shape, gmlp_b_s.shape, ffn2_norm.shape, final_norm.shape]
    red = _small_allreduce(_pack(small_parts, _rows_needed(red_shapes)), "reduce_small")
    (loss_v, g_memn, g_lb, g_f1n, g_mixn, g_gn, g_lng_full, g_lnb_full, g_ws, g_bs, g_f2n, g_fin) = _unpack(red, red_shapes)
    lsh = gmlp_ln_g.shape[1]
    g_lng = lax.dynamic_slice(g_lng_full, (0, chip * lsh), (1, lsh))
    g_lnb = lax.dynamic_slice(g_lnb_full, (0, chip * lsh), (1, lsh))
    small_w = [mem_norm, lb_logits, ffn1_norm, mix_norm, hgrn_gnorm, gmlp_ln_g, gmlp_ln_b, gmlp_w_s, gmlp_b_s, ffn2_norm, final_norm]
    small_g = [g_memn, g_lb, g_f1n, g_mixn, g_gn, g_lng, g_lnb, g_ws, g_bs, g_f2n, g_fin]
    small_m = [m_mem_norm, m_lb_logits, m_ffn1_norm, m_mix_norm, m_hgrn_gnorm, m_gmlp_ln_g, m_gmlp_ln_b, m_gmlp_w_s, m_gmlp_b_s, m_ffn2_norm, m_final_norm]
    small_v = [v_mem_norm, v_lb_logits, v_ffn1_norm, v_mix_norm, v_hgrn_gnorm, v_gmlp_ln_g, v_gmlp_ln_b, v_gmlp_w_s, v_gmlp_b_s, v_ffn2_norm, v_final_norm]
    sshapes = [w.shape for w in small_w]
    nrow = _rows_needed(sshapes)
    d_p, m_p, v_p = _adam_call("adam_small", _pack(small_w, nrow), _pack(small_g, nrow), _pack(small_m, nrow), _pack(small_v, nrow), nrow)
    s_delta, s_m, s_v = _unpack(d_p, sshapes), _unpack(m_p, sshapes), _unpack(v_p, sshapes)
    small_names = ["mem_norm", "lb_logits", "ffn1_norm", "mix_norm", "hgrn_gnorm", "gmlp_ln_g", "gmlp_ln_b", "gmlp_w_s", "gmlp_b_s", "ffn2_norm", "final_norm"]
    small_out = {nm: (g.reshape(w.shape), d, m, v) for nm, w, g, d, m, v in zip(small_names, small_w, small_g, s_delta, s_m, s_v)}

    order = ["mem_norm", "lb_logits", "ffn1_norm", "ffn1_w_in", "ffn1_w_out", "mix_norm", "mem_w_kv", "hgrn_w_in", "hgrn_gnorm",
             "hgrn_w_out", "gmlp_w_in", "gmlp_ln_g", "gmlp_ln_b", "gmlp_w_s", "gmlp_b_s", "gmlp_w_out", "ffn2_norm", "ffn2_w_in",
             "ffn2_w_out", "final_norm"]
    allo = {**big_out, **small_out}
    grad_x = dx0.reshape(x.shape)
    return (loss_v.reshape(()), grad_x, *[allo[n][0] for n in order], *[allo[n][1] for n in order],
            *[allo[n][2] for n in order], *[allo[n][3] for n in order])
```

```python
import functools

import jax
import jax.numpy as jnp
from jax import lax
from jax.experimental import pallas as pl
from jax.experimental.pallas import tpu as pltpu
from jax.experimental.pallas import tpu_sc as plsc

BF = jnp.bfloat16
F32 = jnp.float32
MESH = pl.DeviceIdType.MESH

EPS = 1e-6
D_MODEL = 1024
HG_HEADS = 8
HG_DIM = 128
HG_CHUNK = 64
GM_CHUNK = 128
GM_GROUPS = 8
GM_GROUP_DIM = 256
XA_HEADS = 4
XA_DIM = 256
ADAM_LR = 0.001
ADAM_B1 = 0.9
ADAM_B2 = 0.999
ADAM_EPS = 1e-08
ADAM_WD = 0.01
ADAM_STEP = 10

VMEM_CAP_BYTES = 60 * 1024 * 1024
LANES = 1024


def _pick(n, cap, mult=16):
    if n <= cap:
        return n
    for d in range(cap - cap % mult, 0, -mult):
        if n % d == 0:
            return d
    raise ValueError((n, cap, mult))


def _dg(a, b, ca, cb):
    return lax.dot_general(a.astype(BF), b.astype(BF), (((ca,), (cb,)), ((), ())), preferred_element_type=F32)


@jax.custom_vjp
def dot_nn(a, b):
    return _dg(a, b, 1, 0)


def _nn_fwd(a, b):
    return _dg(a, b, 1, 0), (a, b)


def _nn_bwd(r, g):
    a, b = r
    return _dg(g, b, 1, 1), _dg(a, g, 0, 0)


dot_nn.defvjp(_nn_fwd, _nn_bwd)


@jax.custom_vjp
def dot_nt(a, b):
    return _dg(a, b, 1, 1)


def _nt_fwd(a, b):
    return _dg(a, b, 1, 1), (a, b)


def _nt_bwd(r, g):
    a, b = r
    return _dg(g, b, 1, 0), _dg(g, a, 0, 0)


dot_nt.defvjp(_nt_fwd, _nt_bwd)


@jax.custom_vjp
def dot_tn(a, b):
    return _dg(a, b, 0, 0)


def _tn_fwd(a, b):
    return _dg(a, b, 0, 0), (a, b)


def _tn_bwd(r, g):
    a, b = r
    return _dg(b, g, 1, 1), _dg(a, g, 1, 0)


dot_tn.defvjp(_tn_fwd, _tn_bwd)


def _rmsnorm(x, g):
    return x * lax.rsqrt(jnp.mean(x * x, axis=-1, keepdims=True) + EPS) * g


def _silu(x):
    return x * jax.nn.sigmoid(x)


@jax.custom_vjp
def _gelu(x):
    return 0.5 * x * (1.0 + lax.erf(x * (0.5 ** 0.5)))


def _gelu_fwd(x):
    return _gelu(x), x


def _gelu_bwd(x, g):
    t = x * (0.5 ** 0.5)
    cdf = 0.5 * (1.0 + lax.erf(t))
    return (g * (cdf + x * (jnp.exp(-(t * t)) * (0.5 / 3.141592653589793) ** 0.5)),)


_gelu.defvjp(_gelu_fwd, _gelu_bwd)


def _softmax_last(s):
    m = lax.stop_gradient(jnp.max(s, axis=-1, keepdims=True))
    e = jnp.exp(s - m)
    return e / jnp.sum(e, axis=-1, keepdims=True)


def _tril(n):
    r = lax.broadcasted_iota(jnp.int32, (n, n), 0)
    c = lax.broadcasted_iota(jnp.int32, (n, n), 1)
    return r >= c


def _cumsum_rows(l):
    n = l.shape[0]
    return lax.dot_general(_tril(n).astype(F32), l, (((1,), (0,)), ((), ())),
                           precision=lax.Precision.HIGHEST, preferred_element_type=F32)


def _attention(zx, mk, mv):
    s = dot_nt(zx, mk) * (XA_DIM ** -0.5)
    return dot_nn(_softmax_last(s), mv)


def _hgrn_head(zq, zf, zi, zg, l0, l1, l2, gn, S):
    m = lax.stop_gradient(jnp.maximum(jnp.maximum(l0, l1), l2))
    e0 = jnp.exp(l0 - m)
    lb = e0 / (e0 + jnp.exp(l1 - m) + jnp.exp(l2 - m))
    q = _silu(zq)
    f = lb + (1.0 - lb) * jax.nn.sigmoid(zf)
    k = 1.0 - f
    b = _cumsum_rows(jnp.log(f))
    b_last = b[HG_CHUNK - 1:HG_CHUNK, :]
    q_dec = q * jnp.exp(b)
    k_inv = k * jnp.exp(-b)
    a = jnp.where(_tril(HG_CHUNK), dot_nt(q_dec, k_inv), 0.0)
    o = dot_nn(a, zi) + dot_nn(q_dec, S)
    S_new = jnp.exp(b_last).reshape(HG_DIM, 1) * S + dot_tn(k * jnp.exp(b_last - b), zi)
    o = _rmsnorm(o, gn) * _silu(zg)
    return o, S_new


def _gmlp_block(zu, zv, zx, lng, lnb, ws, bs, mk, mv):
    gv = [_gelu(v) for v in zv]
    width = GM_GROUPS * GM_GROUP_DIM
    mu = sum(jnp.sum(g, axis=-1, keepdims=True) for g in gv) / width
    xc = [g - mu for g in gv]
    var = sum(jnp.sum(c * c, axis=-1, keepdims=True) for c in xc) / width
    r = lax.rsqrt(var + EPS)
    outs = []
    for g in range(GM_GROUPS):
        v = xc[g] * r * lng[g] + lnb[g]
        w = jnp.where(_tril(GM_CHUNK), ws[g], 0.0)
        mixed = dot_nn(w, v) + bs[g].reshape(GM_CHUNK, 1)
        outs.append(_gelu(zu[g]) * mixed)
    for a in range(XA_HEADS):
        outs.append(_attention(zx[a], mk[a], mv[a]))
    return outs


def _rowcall(name, fn, rows, consts, row_outs, acc_outs, tr):
    nrows = rows[0][0].shape[0]
    tr = _pick(nrows, tr)
    n_r, n_c, n_ro, n_ao = len(rows), len(consts), len(row_outs), len(acc_outs)

    def kern(*refs):
        rv = [r[...] for r in refs[:n_r]]
        cv = [r[...] for r in refs[n_r:n_r + n_c]]
        ro_refs = refs[n_r + n_c:n_r + n_c + n_ro]
        ao_refs = refs[n_r + n_c + n_ro:]
        ro, ao = fn(rv, cv)
        for ref, v in zip(ro_refs, ro):
            ref[...] = v.astype(ref.dtype)
        if n_ao:
            @pl.when(pl.program_id(0) == 0)
            def _():
                for ref in ao_refs:
                    ref[...] = jnp.zeros(ref.shape, ref.dtype)

            for ref, v in zip(ao_refs, ao):
                ref[...] += v.astype(ref.dtype)

    in_specs = [pl.BlockSpec((tr, w), functools.partial(lambda i, cb: (i, cb), cb=cb)) for (_, cb, w) in rows]
    in_specs += [pl.BlockSpec(c.shape, lambda i: (0, 0)) for c in consts]
    out_specs = [pl.BlockSpec((tr, w), lambda i: (i, 0)) for (w, _) in row_outs]
    out_specs += [pl.BlockSpec(s, lambda i: (0, 0)) for (s, _) in acc_outs]
    out_shape = [jax.ShapeDtypeStruct((nrows, w), dt) for (w, dt) in row_outs]
    out_shape += [jax.ShapeDtypeStruct(s, dt) for (s, dt) in acc_outs]
    outs = pl.pallas_call(
        kern, grid=(nrows // tr,), in_specs=in_specs, out_specs=out_specs, out_shape=out_shape, name=name,
        compiler_params=pltpu.CompilerParams(dimension_semantics=("arbitrary",),
                                             vmem_limit_bytes=VMEM_CAP_BYTES),
    )(*[a for (a, _, _) in rows], *consts)
    return outs


def _mm(name, a, b, mode, out_dtype, tm, tn, tk, scale=1.0, res=None, a_lead=None, b_lead=None, norm_gain=None):
    ash = a.shape[-2:]
    bsh = b.shape[-2:]
    if mode == "nn":
        (M, K), (K2, N) = ash, bsh
    elif mode == "nt":
        (M, K), (N, K2) = ash, bsh
    else:
        (K, M), (K2, N) = ash, bsh
    assert K == K2, (name, a.shape, b.shape)
    tm, tn, tk = min(tm, M), min(tn, N), min(tk, K)
    assert M % tm == 0 and N % tn == 0 and K % tk == 0, (name, M, N, K, tm, tn, tk)
    nk = K // tk
    dims = {"nn": (1, 0), "nt": (1, 1), "tn": (0, 0)}[mode]

    def lead(spec_shape, index_fn, lead_idx):
        if lead_idx is None:
            return pl.BlockSpec(spec_shape, index_fn)
        return pl.BlockSpec((None,) + spec_shape, lambda i, j, k: (lead_idx,) + index_fn(i, j, k))

    if mode == "tn":
        a_spec = lead((tk, tm), lambda i, j, k: (k, i), a_lead)
    else:
        a_spec = lead((tm, tk), lambda i, j, k: (i, k), a_lead)
    if mode == "nt":
        b_spec = lead((tn, tk), lambda i, j, k: (j, k), b_lead)
    else:
        b_spec = lead((tk, tn), lambda i, j, k: (k, j), b_lead)
    o_spec = pl.BlockSpec((tm, tn), lambda i, j, k: (i, j))
    has_res = res is not None
    has_norm = norm_gain is not None
    assert not has_norm or tn == N

    def kern(*refs):
        a_ref, b_ref = refs[0], refs[1]
        pos = 2
        res_ref = gain_ref = h_ref = None
        if has_res:
            res_ref, pos = refs[pos], pos + 1
        if has_norm:
            gain_ref, pos = refs[pos], pos + 1
        o_ref, pos = refs[pos], pos + 1
        if has_norm:
            h_ref = refs[pos]
        acc_ref = refs[-1] if nk > 1 else None
        p = lax.dot_general(a_ref[...].astype(BF), b_ref[...].astype(BF), (((dims[0],), (dims[1],)), ((), ())),
                            preferred_element_type=F32)

        def finish(v):
            if scale != 1.0:
                v = v * scale
            if has_res:
                v = res_ref[...] + v
            o_ref[...] = v.astype(o_ref.dtype)
            if has_norm:
                h_ref[...] = _rmsnorm(v, gain_ref[...]).astype(h_ref.dtype)

        if nk == 1:
            finish(p)
        else:
            k = pl.program_id(2)

            @pl.when(k == 0)
            def _():
                acc_ref[...] = p

            @pl.when(k > 0)
            def _():
                acc_ref[...] += p

            @pl.when(k == nk - 1)
            def _():
                finish(acc_ref[...])

    ins = [a, b] + ([res] if has_res else []) + ([norm_gain] if has_norm else [])
    in_specs = [a_spec, b_spec] + ([o_spec] if has_res else [])
    in_specs += [pl.BlockSpec((1, N), lambda i, j, k: (0, 0))] if has_norm else []
    out_sd = jax.ShapeDtypeStruct((M, N), out_dtype)
    return pl.pallas_call(
        kern, grid=(M // tm, N // tn, nk), in_specs=in_specs,
        out_specs=[o_spec, o_spec] if has_norm else o_spec,
        out_shape=[out_sd, jax.ShapeDtypeStruct((M, N), BF)] if has_norm else out_sd,
        scratch_shapes=[pltpu.VMEM((tm, tn), F32)] if nk > 1 else [],
        name=name,
        compiler_params=pltpu.CompilerParams(dimension_semantics=("parallel", "parallel", "arbitrary"),
                                             vmem_limit_bytes=VMEM_CAP_BYTES),
    )(*ins)


def _ffn_in_swiglu(name, h, w3, tm, tn):
    T, D = h.shape
    dff = w3.shape[2] // 2
    tm = min(tm, T)
    assert T % tm == 0 and dff % tn == 0
    nj = dff // tn

    def kern(h_ref, wg_ref, wu_ref, zg_ref, zu_ref, a_ref):
        hb = h_ref[...]
        g = jnp.dot(hb, wg_ref[...], preferred_element_type=F32).astype(BF)
        u = jnp.dot(hb, wu_ref[...], preferred_element_type=F32).astype(BF)
        zg_ref[...] = g
        zu_ref[...] = u
        a_ref[...] = (_silu(g.astype(F32)) * u.astype(F32)).astype(BF)

    o_spec = pl.BlockSpec((tm, tn), lambda i, j: (i, j))
    return pl.pallas_call(
        kern, grid=(T // tm, nj),
        in_specs=[pl.BlockSpec((tm, D), lambda i, j: (i, 0)),
                  pl.BlockSpec((None, D, tn), lambda i, j: (0, 0, j)),
                  pl.BlockSpec((None, D, tn), lambda i, j: (0, 0, j + nj))],
        out_specs=[o_spec, o_spec, o_spec],
        out_shape=[jax.ShapeDtypeStruct((T, dff), BF)] * 3, name=name,
        compiler_params=pltpu.CompilerParams(dimension_semantics=("parallel", "arbitrary"),
                                             vmem_limit_bytes=VMEM_CAP_BYTES),
    )(h, w3, w3)


def _ffn_da_swiglu(name, dxo, w3, zg, zu, tm):
    T, D = dxo.shape
    dff = w3.shape[1]
    tm = min(tm, T)
    assert T % tm == 0 and dff % 2 == 0
    hc = dff // 2

    def kern(d_ref, w_ref, g_ref, u_ref, dz_ref):
        db = (d_ref[...] * 0.5).astype(BF)
        for s in range(2):
            cols = slice(s * hc, (s + 1) * hc)
            da = lax.dot_general(db, w_ref[cols, :], (((1,), (1,)), ((), ())), preferred_element_type=F32)
            g = g_ref[:, cols].astype(F32)
            sg = 1.0 / (1.0 + jnp.exp(-g))
            gs = g * sg
            dab = da.astype(BF)
            dz_ref[:, cols] = (dab * u_ref[:, cols]) * (sg + gs * (1.0 - sg)).astype(BF)
            dz_ref[:, dff + s * hc:dff + (s + 1) * hc] = dab * gs.astype(BF)

    row = lambda w: pl.BlockSpec((tm, w), lambda i: (i, 0))
    return pl.pallas_call(
        kern, grid=(T // tm,),
        in_specs=[row(D), pl.BlockSpec((None, dff, D), lambda i: (0, 0, 0), pipeline_mode=pl.Buffered(1)), row(dff), row(dff)],
        out_specs=row(2 * dff), out_shape=jax.ShapeDtypeStruct((T, 2 * dff), BF), name=name,
        compiler_params=pltpu.CompilerParams(dimension_semantics=("arbitrary",), vmem_limit_bytes=VMEM_CAP_BYTES),
    )(dxo, w3, zg, zu)


def _mm_dh_rms(name, dz, w3, xin, g, dres, tm):
    T, K = dz.shape
    D = w3.shape[1]
    tm = min(tm, T)
    assert T % tm == 0

    def kern(dz_ref, w_ref, x_ref, g_ref, r_ref, dx_ref, dg_ref):
        dh = lax.dot_general(dz_ref[...], w_ref[...], (((1,), (1,)), ((), ())), preferred_element_type=F32)
        _, vjp = jax.vjp(_rmsnorm, x_ref[...], g_ref[...])
        dx, dg = vjp(dh)
        dx_ref[...] = dx + r_ref[...]

        @pl.when(pl.program_id(0) == 0)
        def _():
            dg_ref[...] = jnp.zeros(dg_ref.shape, F32)

        dg_ref[...] += dg

    row = lambda w: pl.BlockSpec((tm, w), lambda i: (i, 0))
    one = pl.BlockSpec((1, D), lambda i: (0, 0))
    return pl.pallas_call(
        kern, grid=(T // tm,),
        in_specs=[row(K), pl.BlockSpec((None, D, K), lambda i: (0, 0, 0), pipeline_mode=pl.Buffered(1)), row(D), one, row(D)],
        out_specs=[row(D), one], out_shape=[jax.ShapeDtypeStruct((T, D), F32), jax.ShapeDtypeStruct((1, D), F32)], name=name,
        compiler_params=pltpu.CompilerParams(dimension_semantics=("arbitrary",), vmem_limit_bytes=VMEM_CAP_BYTES),
    )(dz, w3, xin, g, dres)


def _mm_tn_pair(name, a, b, kind, c_arr, tq, tk, scale=1.0):
    T, M = a.shape
    _, N = b.shape
    tk = min(tk, T)
    assert T % tk == 0
    nk = T // tk
    if kind == "col":
        hm = M // 2
        assert N % tq == 0
        nq = N // tq
        tile = (hm, tq)
        a_spec = pl.BlockSpec((tk, hm), lambda h, q, k, c: (k, jnp.bitwise_xor(h, 1 - c[0])))
        b_spec = pl.BlockSpec((tk, tq), lambda h, q, k, c: (k, q))
        o_spec = pl.BlockSpec(tile, lambda h, q, k, c: (0, q * h))
        out_sd = (hm, N)
    else:
        hn = N // 2
        assert M % tq == 0
        nq = M // tq
        tile = (tq, hn)
        a_spec = pl.BlockSpec((tk, tq), lambda h, q, k, c: (k, q))
        b_spec = pl.BlockSpec((tk, hn), lambda h, q, k, c: (k, jnp.bitwise_xor(h, 1 - c[0])))
        o_spec = pl.BlockSpec(tile, lambda h, q, k, c: (q * h, 0))
        out_sd = (M, hn)

    def kern(c_ref, a_ref, b_ref, o_ref, acc, stage, recv, ssem, rsem):
        h, q, k = pl.program_id(0), pl.program_id(1), pl.program_id(2)
        x, y, c, _ = _place()
        p = lax.dot_general(a_ref[...].astype(BF), b_ref[...].astype(BF), (((0,), (0,)), ((), ())), preferred_element_type=F32)

        @pl.when(k == 0)
        def _():
            acc[...] = p

        @pl.when(k > 0)
        def _():
            acc[...] += p

        def send(slot, qq):
            return pltpu.make_async_remote_copy(src_ref=stage.at[slot], dst_ref=recv.at[qq], send_sem=ssem.at[slot],
                                                recv_sem=rsem.at[qq], device_id=(x, y, 1 - c), device_id_type=MESH)

        last = k == nk - 1

        @pl.when(jnp.logical_and(last, h == 0))
        def _():
            slot = q % 2

            @pl.when(q >= 2)
            def _():
                send(slot, q).wait_send()

            stage[slot] = (acc[...] * scale).astype(BF)
            send(slot, q).start()

        @pl.when(jnp.logical_and(last, h == 1))
        def _():
            @pl.when(q == 0)
            def _():
                for s in range(min(nq, 2)):
                    send(s, 0).wait_send()

            send(0, q).wait_recv()
            o_ref[...] = (acc[...] * scale + recv[q].astype(F32)).astype(o_ref.dtype)

    return pl.pallas_call(
        kern,
        grid_spec=pltpu.PrefetchScalarGridSpec(
            num_scalar_prefetch=1, grid=(2, nq, nk), in_specs=[a_spec, b_spec], out_specs=o_spec,
            scratch_shapes=[pltpu.VMEM(tile, F32), pltpu.VMEM((2,) + tile, BF), pltpu.VMEM((nq,) + tile, BF),
                            pltpu.SemaphoreType.DMA((2,)), pltpu.SemaphoreType.DMA((nq,))]),
        out_shape=jax.ShapeDtypeStruct(out_sd, BF), name=name,
        compiler_params=pltpu.CompilerParams(dimension_semantics=("arbitrary", "arbitrary", "arbitrary"),
                                             vmem_limit_bytes=VMEM_CAP_BYTES),
    )(c_arr, a, b)


def _kv_pieces(kv_ref):
    W = XA_HEADS * XA_DIM
    mk = [kv_ref[:, a * XA_DIM:(a + 1) * XA_DIM] for a in range(XA_HEADS)]
    mv = [kv_ref[:, W + a * XA_DIM:W + (a + 1) * XA_DIM] for a in range(XA_HEADS)]
    return mk, mv


def _lb_pieces(lb_ref):
    return [[lb_ref[r:r + 1, h * HG_DIM:(h + 1) * HG_DIM] for h in range(HG_HEADS)] for r in range(3)]


HG_SUB = 4


def _hgrn_rows(z_ref):
    W = HG_HEADS * HG_DIM

    def piece(c, col, w):
        return z_ref[c * HG_CHUNK:(c + 1) * HG_CHUNK, col:col + w]

    zq = [[piece(c, h * HG_DIM, HG_DIM) for h in range(HG_HEADS)] for c in range(HG_SUB)]
    zf = [[piece(c, W + h * HG_DIM, HG_DIM) for h in range(HG_HEADS)] for c in range(HG_SUB)]
    zi = [[piece(c, 2 * W + h * HG_DIM, HG_DIM) for h in range(HG_HEADS)] for c in range(HG_SUB)]
    zg = [[piece(c, 3 * W + h * HG_DIM, HG_DIM) for h in range(HG_HEADS)] for c in range(HG_SUB)]
    zx = [z_ref[:, 4 * W + a * XA_DIM:4 * W + (a + 1) * XA_DIM] for a in range(XA_HEADS)]
    return zq, zf, zi, zg, zx


def _hgrn_steps(zq, zf, zi, zg, zx, l0, l1, l2, gn, mk, mv, S):
    mix = []
    for c in range(HG_SUB):
        row, s_next = [], []
        for h in range(HG_HEADS):
            o, sn = _hgrn_head(zq[c][h], zf[c][h], zi[c][h], zg[c][h], l0[h], l1[h], l2[h], gn, S[h])
            row.append(o)
            s_next.append(sn)
        mix.append(row)
        S = s_next
    att = [_attention(zx[a], mk[a], mv[a]) for a in range(XA_HEADS)]
    return mix, att, S


def _hgrn_fwd2(z, lb_logits, gnorm, kv, bl, seq):
    T, zw = z.shape
    mem_len = kv.shape[0] // bl
    cat_w = HG_HEADS * HG_DIM + XA_HEADS * XA_DIM
    R = HG_SUB * HG_CHUNK
    nb = seq // R

    def kern(z_ref, lb_ref, gn_ref, kv_ref, cat_ref, st_ref, s_scr):
        @pl.when(pl.program_id(1) == 0)
        def _():
            s_scr[...] = jnp.zeros(s_scr.shape, F32)

        st_ref[...] = s_scr[...]
        zq, zf, zi, zg, zx = _hgrn_rows(z_ref)
        mk, mv = _kv_pieces(kv_ref)
        l0, l1, l2 = _lb_pieces(lb_ref)
        S = [s_scr[h] for h in range(HG_HEADS)]
        mix, att, s_new = _hgrn_steps(zq, zf, zi, zg, zx, l0, l1, l2, gn_ref[...], mk, mv, S)
        for c in range(HG_SUB):
            for h in range(HG_HEADS):
                cat_ref[c * HG_CHUNK:(c + 1) * HG_CHUNK, h * HG_DIM:(h + 1) * HG_DIM] = mix[c][h].astype(cat_ref.dtype)
        for h in range(HG_HEADS):
            s_scr[h] = s_new[h]
        base = HG_HEADS * HG_DIM
        for a in range(XA_HEADS):
            cat_ref[:, base + a * XA_DIM:base + (a + 1) * XA_DIM] = att[a].astype(cat_ref.dtype)

    return pl.pallas_call(
        kern, grid=(bl, nb),
        in_specs=[pl.BlockSpec((R, zw), lambda b, n: (b * nb + n, 0)),
                  pl.BlockSpec(lb_logits.shape, lambda b, n: (0, 0)),
                  pl.BlockSpec(gnorm.shape, lambda b, n: (0, 0)),
                  pl.BlockSpec((mem_len, kv.shape[1]), lambda b, n: (b, 0))],
        out_specs=[pl.BlockSpec((R, cat_w), lambda b, n: (b * nb + n, 0)),
                   pl.BlockSpec((None, HG_HEADS, HG_DIM, HG_DIM), lambda b, n: (b * nb + n, 0, 0, 0))],
        out_shape=[jax.ShapeDtypeStruct((T, cat_w), BF),
                   jax.ShapeDtypeStruct((bl * nb, HG_HEADS, HG_DIM, HG_DIM), F32)],
        scratch_shapes=[pltpu.VMEM((HG_HEADS, HG_DIM, HG_DIM), F32)],
        name="hgrn_fwd",
        compiler_params=pltpu.CompilerParams(dimension_semantics=("arbitrary", "arbitrary"), vmem_limit_bytes=VMEM_CAP_BYTES),
    )(z, lb_logits, gnorm, kv)


def _hgrn_bwd2(z, dcat, stash, lb_logits, gnorm, kv, bl, seq):
    T, zw = z.shape
    mem_len = kv.shape[0] // bl
    cat_w = dcat.shape[1]
    R = HG_SUB * HG_CHUNK
    nb = seq // R

    def kern(z_ref, dc_ref, st_ref, lb_ref, gn_ref, kv_ref, dz_ref, dkv_ref, dlb_ref, dgn_ref, ds_scr):
        first = jnp.logical_and(pl.program_id(0) == 0, pl.program_id(1) == 0)

        @pl.when(pl.program_id(1) == 0)
        def _():
            ds_scr[...] = jnp.zeros(ds_scr.shape, F32)
            dkv_ref[...] = jnp.zeros(dkv_ref.shape, F32)

        @pl.when(first)
        def _():
            dlb_ref[...] = jnp.zeros(dlb_ref.shape, F32)
            dgn_ref[...] = jnp.zeros(dgn_ref.shape, F32)

        zq, zf, zi, zg, zx = _hgrn_rows(z_ref)
        mk, mv = _kv_pieces(kv_ref)
        l0, l1, l2 = _lb_pieces(lb_ref)
        S = [st_ref[h] for h in range(HG_HEADS)]
        _, vjp = jax.vjp(_hgrn_steps, zq, zf, zi, zg, zx, l0, l1, l2, gn_ref[...], mk, mv, S)
        d_mix = [[dc_ref[c * HG_CHUNK:(c + 1) * HG_CHUNK, h * HG_DIM:(h + 1) * HG_DIM] for h in range(HG_HEADS)]
                 for c in range(HG_SUB)]
        base = HG_HEADS * HG_DIM
        d_att = [dc_ref[:, base + a * XA_DIM:base + (a + 1) * XA_DIM] for a in range(XA_HEADS)]
        d_s = [ds_scr[h] for h in range(HG_HEADS)]
        dzq, dzf, dzi, dzg, dzx, dl0, dl1, dl2, dgn, dmk, dmv, dS = vjp((d_mix, d_att, d_s))
        W = HG_HEADS * HG_DIM
        for c in range(HG_SUB):
            rows = slice(c * HG_CHUNK, (c + 1) * HG_CHUNK)
            for h in range(HG_HEADS):
                for k, part in enumerate((dzq, dzf, dzi, dzg)):
                    dz_ref[rows, k * W + h * HG_DIM:k * W + (h + 1) * HG_DIM] = part[c][h].astype(dz_ref.dtype)
        for h in range(HG_HEADS):
            sl = slice(h * HG_DIM, (h + 1) * HG_DIM)
            ds_scr[h] = dS[h]
            dlb_ref[0:1, sl] += dl0[h]
            dlb_ref[1:2, sl] += dl1[h]
            dlb_ref[2:3, sl] += dl2[h]
        dgn_ref[...] += dgn
        KW = XA_HEADS * XA_DIM
        for a in range(XA_HEADS):
            dz_ref[:, 4 * W + a * XA_DIM:4 * W + (a + 1) * XA_DIM] = dzx[a].astype(dz_ref.dtype)
            dkv_ref[:, a * XA_DIM:(a + 1) * XA_DIM] += dmk[a]
            dkv_ref[:, KW + a * XA_DIM:KW + (a + 1) * XA_DIM] += dmv[a]

    rev = lambda b, n: (b * nb + (nb - 1 - n), 0)
    return pl.pallas_call(
        kern, grid=(bl, nb),
        in_specs=[pl.BlockSpec((R, zw), rev),
                  pl.BlockSpec((R, cat_w), rev),
                  pl.BlockSpec((None, HG_HEADS, HG_DIM, HG_DIM), lambda b, n: (b * nb + (nb - 1 - n), 0, 0, 0)),
                  pl.BlockSpec(lb_logits.shape, lambda b, n: (0, 0)),
                  pl.BlockSpec(gnorm.shape, lambda b, n: (0, 0)),
                  pl.BlockSpec((mem_len, kv.shape[1]), lambda b, n: (b, 0))],
        out_specs=[pl.BlockSpec((R, zw), rev),
                   pl.BlockSpec((mem_len, kv.shape[1]), lambda b, n: (b, 0)),
                   pl.BlockSpec(lb_logits.shape, lambda b, n: (0, 0)),
                   pl.BlockSpec(gnorm.shape, lambda b, n: (0, 0))],
        out_shape=[jax.ShapeDtypeStruct((T, zw), BF), jax.ShapeDtypeStruct(kv.shape, F32),
                   jax.ShapeDtypeStruct(lb_logits.shape, F32), jax.ShapeDtypeStruct(gnorm.shape, F32)],
        scratch_shapes=[pltpu.VMEM((HG_HEADS, HG_DIM, HG_DIM), F32)],
        name="hgrn_bwd",
        compiler_params=pltpu.CompilerParams(dimension_semantics=("arbitrary", "arbitrary"), vmem_limit_bytes=VMEM_CAP_BYTES),
    )(z, dcat, stash, lb_logits, gnorm, kv)


GM_SUB = 2


def _gmlp_pieces(z_ref):
    W = GM_GROUPS * GM_GROUP_DIM
    zu = [z_ref[:, g * GM_GROUP_DIM:(g + 1) * GM_GROUP_DIM] for g in range(GM_GROUPS)]
    zv = [z_ref[:, W + g * GM_GROUP_DIM:W + (g + 1) * GM_GROUP_DIM] for g in range(GM_GROUPS)]
    zx = [z_ref[:, 2 * W + a * XA_DIM:2 * W + (a + 1) * XA_DIM] for a in range(XA_HEADS)]
    return zu, zv, zx


def _gmlp_params(lng_ref, lnb_ref, ws_ref, bs_ref):
    lng = [lng_ref[:, g * GM_GROUP_DIM:(g + 1) * GM_GROUP_DIM] for g in range(GM_GROUPS)]
    lnb = [lnb_ref[:, g * GM_GROUP_DIM:(g + 1) * GM_GROUP_DIM] for g in range(GM_GROUPS)]
    ws = [ws_ref[g] for g in range(GM_GROUPS)]
    bs = [bs_ref[g:g + 1, :] for g in range(GM_GROUPS)]
    return lng, lnb, ws, bs


def _gmlp_fwd(z, ln_g, ln_b, w_s, b_s, kv, bl, nc):
    T, zw = z.shape
    mem_len = kv.shape[0] // bl
    cat_w = GM_GROUPS * GM_GROUP_DIM + XA_HEADS * XA_DIM

    assert nc % GM_SUB == 0
    nc = nc // GM_SUB
    R = GM_SUB * GM_CHUNK

    def kern(z_ref, lng_ref, lnb_ref, ws_ref, bs_ref, kv_ref, cat_ref):
        lng, lnb, ws, bs = _gmlp_params(lng_ref, lnb_ref, ws_ref, bs_ref)
        mk, mv = _kv_pieces(kv_ref)
        for c in range(GM_SUB):
            rows = pl.ds(c * GM_CHUNK, GM_CHUNK)
            zu, zv, zx = _gmlp_pieces(z_ref.at[rows])
            out = cat_ref.at[rows]
            outs = _gmlp_block(zu, zv, zx, lng, lnb, ws, bs, mk, mv)
            for g in range(GM_GROUPS):
                out[:, g * GM_GROUP_DIM:(g + 1) * GM_GROUP_DIM] = outs[g].astype(cat_ref.dtype)
            base = GM_GROUPS * GM_GROUP_DIM
            for a in range(XA_HEADS):
                out[:, base + a * XA_DIM:base + (a + 1) * XA_DIM] = outs[GM_GROUPS + a].astype(cat_ref.dtype)

    full2 = lambda b, n: (0, 0)
    return pl.pallas_call(
        kern, grid=(bl, nc),
        in_specs=[pl.BlockSpec((R, zw), lambda b, n: (b * nc + n, 0)),
                  pl.BlockSpec(ln_g.shape, full2), pl.BlockSpec(ln_b.shape, full2),
                  pl.BlockSpec(w_s.shape, lambda b, n: (0, 0, 0)), pl.BlockSpec(b_s.shape, full2),
                  pl.BlockSpec((mem_len, kv.shape[1]), lambda b, n: (b, 0))],
        out_specs=pl.BlockSpec((R, cat_w), lambda b, n: (b * nc + n, 0)),
        out_shape=jax.ShapeDtypeStruct((T, cat_w), BF),
        name="gmlp_fwd",
        compiler_params=pltpu.CompilerParams(dimension_semantics=("arbitrary", "arbitrary"), vmem_limit_bytes=VMEM_CAP_BYTES),
    )(z, ln_g, ln_b, w_s, b_s, kv)


def _gmlp_bwd(z, dcat, ln_g, ln_b, w_s, b_s, kv, bl, nc):
    T, zw = z.shape
    mem_len = kv.shape[0] // bl
    cat_w = dcat.shape[1]
    assert nc % GM_SUB == 0
    nc = nc // GM_SUB

    def kern(z_ref, dc_ref, lng_ref, lnb_ref, ws_ref, bs_ref, kv_ref,
             dz_ref, dkv_ref, dlng_ref, dlnb_ref, dws_ref, dbs_ref):
        first = jnp.logical_and(pl.program_id(0) == 0, pl.program_id(1) == 0)

        @pl.when(pl.program_id(1) == 0)
        def _():
            dkv_ref[...] = jnp.zeros(dkv_ref.shape, F32)

        @pl.when(first)
        def _():
            dlng_ref[...] = jnp.zeros(dlng_ref.shape, F32)
            dlnb_ref[...] = jnp.zeros(dlnb_ref.shape, F32)
            dws_ref[...] = jnp.zeros(dws_ref.shape, F32)
            dbs_ref[...] = jnp.zeros(dbs_ref.shape, F32)

        lng, lnb, ws, bs = _gmlp_params(lng_ref, lnb_ref, ws_ref, bs_ref)
        mk, mv = _kv_pieces(kv_ref)
        W = GM_GROUPS * GM_GROUP_DIM
        KW = XA_HEADS * XA_DIM
        for c in range(GM_SUB):
            rows = pl.ds(c * GM_CHUNK, GM_CHUNK)
            zu, zv, zx = _gmlp_pieces(z_ref.at[rows])
            dc, dz = dc_ref.at[rows], dz_ref.at[rows]
            _, vjp = jax.vjp(_gmlp_block, zu, zv, zx, lng, lnb, ws, bs, mk, mv)
            d_outs = [dc[:, g * GM_GROUP_DIM:(g + 1) * GM_GROUP_DIM] for g in range(GM_GROUPS)]
            d_outs += [dc[:, W + a * XA_DIM:W + (a + 1) * XA_DIM] for a in range(XA_HEADS)]
            dzu, dzv, dzx, dlng, dlnb, dws, dbs, dmk, dmv = vjp(d_outs)
            for g in range(GM_GROUPS):
                sl = slice(g * GM_GROUP_DIM, (g + 1) * GM_GROUP_DIM)
                dz[:, sl] = dzu[g].astype(dz_ref.dtype)
                dz[:, W + g * GM_GROUP_DIM:W + (g + 1) * GM_GROUP_DIM] = dzv[g].astype(dz_ref.dtype)
                dlng_ref[:, sl] += dlng[g]
                dlnb_ref[:, sl] += dlnb[g]
                dws_ref[g] += dws[g]
                dbs_ref[g:g + 1, :] += dbs[g]
            for a in range(XA_HEADS):
                dz[:, 2 * W + a * XA_DIM:2 * W + (a + 1) * XA_DIM] = dzx[a].astype(dz_ref.dtype)
                dkv_ref[:, a * XA_DIM:(a + 1) * XA_DIM] += dmk[a]
                dkv_ref[:, KW + a * XA_DIM:KW + (a + 1) * XA_DIM] += dmv[a]

    full2 = lambda b, n: (0, 0)
    full3 = lambda b, n: (0, 0, 0)
    blk = lambda b, n: (b * nc + n, 0)
    return pl.pallas_call(
        kern, grid=(bl, nc),
        in_specs=[pl.BlockSpec((GM_SUB * GM_CHUNK, zw), blk), pl.BlockSpec((GM_SUB * GM_CHUNK, cat_w), blk),
                  pl.BlockSpec(ln_g.shape, full2), pl.BlockSpec(ln_b.shape, full2),
                  pl.BlockSpec(w_s.shape, full3), pl.BlockSpec(b_s.shape, full2),
                  pl.BlockSpec((mem_len, kv.shape[1]), lambda b, n: (b, 0))],
        out_specs=[pl.BlockSpec((GM_SUB * GM_CHUNK, zw), blk),
                   pl.BlockSpec((mem_len, kv.shape[1]), lambda b, n: (b, 0)),
                   pl.BlockSpec(ln_g.shape, full2), pl.BlockSpec(ln_b.shape, full2),
                   pl.BlockSpec(w_s.shape, full3), pl.BlockSpec(b_s.shape, full2)],
        out_shape=[jax.ShapeDtypeStruct((T, zw), BF), jax.ShapeDtypeStruct(kv.shape, F32),
                   jax.ShapeDtypeStruct(ln_g.shape, F32), jax.ShapeDtypeStruct(ln_b.shape, F32),
                   jax.ShapeDtypeStruct(w_s.shape, F32), jax.ShapeDtypeStruct(b_s.shape, F32)],
        name="gmlp_bwd",
        compiler_params=pltpu.CompilerParams(dimension_semantics=("arbitrary", "arbitrary"), vmem_limit_bytes=VMEM_CAP_BYTES),
    )(z, dcat, ln_g, ln_b, w_s, b_s, kv)


def _place():
    x, y, c = lax.axis_index("x"), lax.axis_index("y"), lax.axis_index("c")
    chips = [(1 - x, y), (x, 1 - y), (1 - x, 1 - y)]
    return x, y, c, chips


def _half(ref, kind, e):
    if kind == "col":
        n = ref.shape[1] // 2
        return ref.at[:, pl.ds(pl.multiple_of(e * n, n), n), :]
    n = ref.shape[2] // 2
    return ref.at[:, :, pl.ds(pl.multiple_of(e * n, n), n)]


def _slot(ref, kind, j, n):
    if kind == "col":
        return ref.at[:, :, pl.ds(pl.multiple_of(j * n, n), n)]
    return ref.at[:, pl.ds(pl.multiple_of(j * n, n), n), :]


def _allgather_seq(name, items, cid):
    nt = len(items)
    kinds = [k for (_, k, _) in items]
    slot_kind = ["row" if k == "row" else "col" for k in kinds]
    out_type = []
    for s, k, l in items:
        L, r, c = s.shape
        lo = L if l is None else 1
        out_type.append(jax.ShapeDtypeStruct((lo, 4 * r, c) if k == "row" else (lo, r, 4 * c), s.dtype))

    def part(ref, t, e):
        return ref if kinds[t] == "vec" else _half(ref, kinds[t], e)

    def body(*refs):
        sh = [refs[t] if items[t][2] is None else refs[t].at[pl.ds(items[t][2], 1)] for t in range(nt)]
        full = refs[nt:2 * nt]
        loc, s_ici, r_ici, s_d2d, r_d2d = refs[2 * nt:]
        x, y, c, chips = _place()
        own = 2 * x + y
        sibling = (x, y, 1 - c)
        barrier = pltpu.get_barrier_semaphore()
        for peer in [(px, py, c) for (px, py) in chips] + [sibling]:
            pl.semaphore_signal(barrier, inc=1, device_id=peer, device_id_type=MESH)
        pl.semaphore_wait(barrier, 4)
        width = [sh[t].shape[1] if kinds[t] == "row" else sh[t].shape[2] for t in range(nt)]
        started = []
        for t in range(nt):
            mine = pltpu.make_async_copy(sh[t], _slot(full[t], slot_kind[t], own, width[t]), loc.at[t])
            mine.start()
            started.append(mine)
        sent = []
        for t in range(nt):
            for p, (px, py) in enumerate(chips):
                cp = pltpu.make_async_remote_copy(
                    src_ref=part(sh[t], t, c), dst_ref=part(_slot(full[t], slot_kind[t], own, width[t]), t, c),
                    send_sem=s_ici.at[t, p], recv_sem=r_ici.at[t, p], device_id=(px, py, c), device_id_type=MESH)
                cp.start()
                sent.append(cp)
        for t in range(nt):
            for p, (px, py) in enumerate(chips):
                landed = part(_slot(full[t], slot_kind[t], 2 * px + py, width[t]), t, c)
                pltpu.make_async_remote_copy(
                    src_ref=landed, dst_ref=landed, send_sem=s_ici.at[t, p], recv_sem=r_ici.at[t, p],
                    device_id=(px, py, c), device_id_type=MESH).wait_recv()
                if kinds[t] == "vec":
                    continue
                fw = pltpu.make_async_remote_copy(
                    src_ref=landed, dst_ref=landed, send_sem=s_d2d.at[t, p], recv_sem=r_d2d.at[t, p],
                    device_id=sibling, device_id_type=MESH)
                fw.start()
                sent.append(fw)
        for t in range(nt):
            if kinds[t] == "vec":
                continue
            for p, (px, py) in enumerate(chips):
                other = _half(_slot(full[t], kinds[t], 2 * px + py, width[t]), kinds[t], 1 - c)
                pltpu.make_async_remote_copy(
                    src_ref=other, dst_ref=other, send_sem=s_d2d.at[t, p], recv_sem=r_d2d.at[t, p],
                    device_id=sibling, device_id_type=MESH).wait_recv()
        for cp in sent:
            cp.wait_send()
        for cp in started:
            cp.wait()

    return pl.kernel(
        body, out_type=out_type, mesh=plsc.ScalarSubcoreMesh(axis_name="seq", num_cores=1),
        scratch_types=[pltpu.SemaphoreType.DMA((nt,)), pltpu.SemaphoreType.DMA((nt, 3)), pltpu.SemaphoreType.DMA((nt, 3)),
                       pltpu.SemaphoreType.DMA((nt, 3)), pltpu.SemaphoreType.DMA((nt, 3))],
        compiler_params=pltpu.CompilerParams(collective_id=cid), name=name,
    )(*[s for (s, _, _) in items])


def _slot2(ref, kind, j, n):
    if kind == "col":
        return ref.at[:, pl.ds(pl.multiple_of(j * n, n), n)]
    return ref.at[pl.ds(pl.multiple_of(j * n, n), n), :]


def _rs_chips_seq(name, parts, kinds, cid):
    nm = len(parts)
    out_type = []
    for g, k in zip(parts, kinds):
        r, c = g.shape
        ps = (r, c // 4) if k == "col" else (r // 4, c)
        out_type += [jax.ShapeDtypeStruct(ps, BF), jax.ShapeDtypeStruct((3,) + ps, BF)]

    def body(*refs):
        g = refs[:nm]
        outs = refs[nm:3 * nm]
        loc, ssem, rsem = refs[3 * nm:]
        x, y, c, chips = _place()
        own = 2 * x + y
        barrier = pltpu.get_barrier_semaphore()
        for (px, py) in chips:
            pl.semaphore_signal(barrier, inc=1, device_id=(px, py, c), device_id_type=MESH)
        pl.semaphore_wait(barrier, 3)
        cps = []
        for m in range(nm):
            k = kinds[m]
            own_o, got_o = outs[2 * m], outs[2 * m + 1]
            n = g[m].shape[1] // 4 if k == "col" else g[m].shape[0] // 4
            lc = pltpu.make_async_copy(_slot2(g[m], k, own, n), own_o, loc.at[m])
            lc.start()
            cps.append(lc)
            for p, (px, py) in enumerate(chips):
                cp = pltpu.make_async_remote_copy(
                    src_ref=_slot2(g[m], k, 2 * px + py, n), dst_ref=got_o.at[p],
                    send_sem=ssem.at[m, p], recv_sem=rsem.at[m, p], device_id=(px, py, c), device_id_type=MESH)
                cp.start()
                cps.append(cp)
        for cp in cps:
            cp.wait()

    return pl.kernel(
        body, out_type=out_type, mesh=plsc.ScalarSubcoreMesh(axis_name="seq", num_cores=1),
        scratch_types=[pltpu.SemaphoreType.DMA((nm,)), pltpu.SemaphoreType.DMA((nm, 3)), pltpu.SemaphoreType.DMA((nm, 3))],
        compiler_params=pltpu.CompilerParams(collective_id=cid), name=name,
    )(*parts)


def _finish_share(name, owns, gots, kind, c_arr):
    L = len(owns)
    r, c = owns[0].shape
    tr = _pick(r, 128 if kind == "col" else 256)
    nb = r // tr
    nq = L * nb

    def chunk_of(l):
        return lambda h, q: jnp.clip(q * (1 - h) + (nq - 1) * h - l * nb, 0, nb - 1)

    ins, in_specs = [], []
    for l in range(L):
        at = chunk_of(l)
        ins += [owns[l], gots[l].reshape(3 * r, c), gots[l].reshape(3 * r, c), gots[l].reshape(3 * r, c)]
        in_specs.append(pl.BlockSpec((tr, c), functools.partial(lambda h, q, cc, at: (at(h, q), 0), at=at)))
        in_specs += [pl.BlockSpec((tr, c), functools.partial(lambda h, q, cc, at, p: (p * nb + at(h, q), 0), at=at, p=p))
                     for p in range(3)]
    if kind == "col":
        out_sd = (L, 2, r, c)
        o_spec = pl.BlockSpec((None, 2, tr, c), lambda h, q, cc: ((q * h) // nb, 0, (q * h) % nb, 0))
    else:
        out_sd = (L * r, 2 * c)
        o_spec = pl.BlockSpec((tr, 2 * c), lambda h, q, cc: (q * h, 0))

    def kern(c_ref, *refs):
        in_refs = refs[:4 * L]
        out_ref, mine, recv, ssem, rsem = refs[4 * L:]
        h, q = pl.program_id(0), pl.program_id(1)
        x, y, cc, _ = _place()

        def swap(qq):
            return pltpu.make_async_remote_copy(src_ref=mine.at[qq], dst_ref=recv.at[qq], send_sem=ssem.at[qq],
                                                recv_sem=rsem.at[qq], device_id=(x, y, 1 - cc), device_id_type=MESH)

        for l in range(L):
            @pl.when(jnp.logical_and(h == 0, q // nb == l))
            def _(l=l):
                o_ref, g0, g1, g2 = in_refs[4 * l:4 * l + 4]
                mine[q] = ((o_ref[...].astype(F32) + g0[...].astype(F32)) + g1[...].astype(F32)) + g2[...].astype(F32)
                swap(q).start()

        @pl.when(h == 1)
        def _():
            swap(q).wait()
            a, b = mine[q], recv[q]
            first = c_ref[0] == 0
            lo, hi = jnp.where(first, a, b), jnp.where(first, b, a)
            if kind == "col":
                out_ref[0] = lo
                out_ref[1] = hi
            else:
                out_ref[:, :c] = lo
                out_ref[:, c:] = hi

    full = pl.pallas_call(
        kern,
        grid_spec=pltpu.PrefetchScalarGridSpec(
            num_scalar_prefetch=1, grid=(2, nq), in_specs=in_specs, out_specs=o_spec,
            scratch_shapes=[pltpu.VMEM((nq, tr, c), F32), pltpu.VMEM((nq, tr, c), F32),
                            pltpu.SemaphoreType.DMA((nq,)), pltpu.SemaphoreType.DMA((nq,))]),
        out_shape=jax.ShapeDtypeStruct(out_sd, F32), name=name,
        compiler_params=pltpu.CompilerParams(dimension_semantics=("arbitrary", "arbitrary"),
                                             vmem_limit_bytes=VMEM_CAP_BYTES),
    )(c_arr, *ins)
    return full.reshape(L, 2 * r, c) if kind == "col" else full.reshape(L, r, 2 * c)


def _small_allreduce(buf, name):
    R = buf.shape[0]
    assert R % 16 == 0
    h = R // 2

    def body(x_ref, o_ref, sib, csum, got, s_a, r_a, s_b, r_b, s_c, r_c):
        x, y, c, chips = _place()
        sibling = (x, y, 1 - c)
        own = 2 * x + y
        swap = pltpu.make_async_remote_copy(src_ref=x_ref, dst_ref=sib, send_sem=s_a, recv_sem=r_a,
                                            device_id=sibling, device_id_type=MESH)
        swap.start()
        swap.wait()
        a, b = x_ref[...], sib[...]
        south = c == 0
        csum[...] = jnp.where(south, a, b) + jnp.where(south, b, a)
        lo = pl.multiple_of(c * h, 8)
        mine = csum.at[pl.ds(lo, h)]
        got[own] = csum[pl.ds(lo, h)]
        sends = []
        for p, (px, py) in enumerate(chips):
            cp = pltpu.make_async_remote_copy(src_ref=mine, dst_ref=got.at[own], send_sem=s_b.at[p], recv_sem=r_b.at[p],
                                              device_id=(px, py, c), device_id_type=MESH)
            cp.start()
            sends.append(cp)
        for cp in sends:
            cp.wait()
        o_ref[pl.ds(lo, h)] = ((got[0] + got[1]) + got[2]) + got[3]
        done = o_ref.at[pl.ds(lo, h)]
        back = pltpu.make_async_remote_copy(src_ref=done, dst_ref=done, send_sem=s_c, recv_sem=r_c,
                                            device_id=sibling, device_id_type=MESH)
        back.start()
        back.wait_send()
        other = o_ref.at[pl.ds(pl.multiple_of((1 - c) * h, 8), h)]
        pltpu.make_async_remote_copy(src_ref=other, dst_ref=other, send_sem=s_c, recv_sem=r_c,
                                     device_id=sibling, device_id_type=MESH).wait_recv()

    vm = pl.BlockSpec(memory_space=pltpu.VMEM)
    return pl.pallas_call(
        body, out_shape=jax.ShapeDtypeStruct(buf.shape, F32), in_specs=[vm], out_specs=vm,
        scratch_shapes=[pltpu.VMEM((R, LANES), F32), pltpu.VMEM((R, LANES), F32), pltpu.VMEM((4, h, LANES), F32),
                        pltpu.SemaphoreType.DMA, pltpu.SemaphoreType.DMA, pltpu.SemaphoreType.DMA((3,)),
                        pltpu.SemaphoreType.DMA((3,)), pltpu.SemaphoreType.DMA, pltpu.SemaphoreType.DMA],
        name=name,
        compiler_params=pltpu.CompilerParams(vmem_limit_bytes=VMEM_CAP_BYTES),
    )(buf)


PACK_TILE_ROWS = 8


def _item_rows(shape):
    n = 1
    for d in shape:
        n *= d
    return -(-n // (PACK_TILE_ROWS * LANES)) * PACK_TILE_ROWS


def _pack(arrs, rows_total):
    buf = jnp.zeros((rows_total, LANES), F32)
    r = 0
    for a in arrs:
        f = a.reshape(-1).astype(F32)
        nr = _item_rows(a.shape)
        block = jnp.pad(f, (0, nr * LANES - f.shape[0])).reshape(nr, LANES)
        buf = lax.dynamic_update_slice(buf, block, (r, 0))
        r += nr
    return buf


def _unpack(buf, shapes):
    out, r = [], 0
    for s in shapes:
        n = 1
        for d in s:
            n *= d
        nr = _item_rows(s)
        out.append(buf[r:r + nr].reshape(-1)[:n].reshape(s))
        r += nr
    return out


def _rows_needed(shapes):
    return -(-sum(_item_rows(s) for s in shapes) // (2 * PACK_TILE_ROWS)) * (2 * PACK_TILE_ROWS)


def _two_rows(a, b):
    out = jnp.zeros((2, a.shape[1]), a.dtype)
    return lax.dynamic_update_slice(lax.dynamic_update_slice(out, a, (0, 0)), b, (1, 0))


def _adam(w, g, m, v):
    m = ADAM_B1 * m + (1.0 - ADAM_B1) * g
    v = ADAM_B2 * v + (1.0 - ADAM_B2) * jnp.square(g)
    m_hat = m / (1.0 - ADAM_B1 ** ADAM_STEP)
    v_hat = v / (1.0 - ADAM_B2 ** ADAM_STEP)
    delta = -ADAM_LR * (m_hat / (jnp.sqrt(v_hat) + ADAM_EPS) + ADAM_WD * w)
    return delta, m, v


def _adam_call(name, w2, g2, m2, v2, tr):
    def fn(rv, cv):
        return list(_adam(*rv)), []

    width = w2.shape[1]
    return _rowcall(name, fn, [(w2, 0, width), (g2, 0, width), (m2, 0, width), (v2, 0, width)], [],
                    [(width, F32)] * 3, [], tr)


def kernel(x, mem, mem_norm, lb_logits, ffn1_norm, ffn1_w_in, ffn1_w_out, mix_norm, mem_w_kv, hgrn_w_in, hgrn_gnorm, hgrn_w_out, gmlp_w_in, gmlp_ln_g, gmlp_ln_b, gmlp_w_s, gmlp_b_s, gmlp_w_out, ffn2_norm, ffn2_w_in, ffn2_w_out, final_norm, loss_target, m_mem_norm, m_lb_logits, m_ffn1_norm, m_ffn1_w_in, m_ffn1_w_out, m_mix_norm, m_mem_w_kv, m_hgrn_w_in, m_hgrn_gnorm, m_hgrn_w_out, m_gmlp_w_in, m_gmlp_ln_g, m_gmlp_ln_b, m_gmlp_w_s, m_gmlp_b_s, m_gmlp_w_out, m_ffn2_norm, m_ffn2_w_in, m_ffn2_w_out, m_final_norm, v_mem_norm, v_lb_logits, v_ffn1_norm, v_ffn1_w_in, v_ffn1_w_out, v_mix_norm, v_mem_w_kv, v_hgrn_w_in, v_hgrn_gnorm, v_hgrn_w_out, v_gmlp_w_in, v_gmlp_ln_g, v_gmlp_ln_b, v_gmlp_w_s, v_gmlp_b_s, v_gmlp_w_out, v_ffn2_norm, v_ffn2_w_in, v_ffn2_w_out, v_final_norm):
    bl, seq, D = x.shape
    T = bl * seq
    mem_len = mem.shape[1]
    chip = 2 * lax.axis_index("x") + lax.axis_index("y")
    c_arr = lax.axis_index("c").astype(jnp.int32).reshape(1)
    TR = 1024

    big = [("ffn1_w_in", ffn1_w_in, "col"), ("ffn1_w_out", ffn1_w_out, "row"), ("mem_w_kv", mem_w_kv, "col"),
           ("hgrn_w_in", hgrn_w_in, "col"), ("hgrn_w_out", hgrn_w_out, "row"), ("gmlp_w_in", gmlp_w_in, "col"),
           ("gmlp_w_out", gmlp_w_out, "row"), ("ffn2_w_in", ffn2_w_in, "col"), ("ffn2_w_out", ffn2_w_out, "row")]
    kinds = [k for (_, _, k) in big]
    shards_bf = []
    for nm, w, _ in big:
        L, r, c = w.shape
        (wb,) = _rowcall("cast_" + nm, lambda rv, cv: ([rv[0]], []), [(w.reshape(L * r, c), 0, c)], [], [(c, BF)], [], 512)
        shards_bf.append(wb.reshape(L, r, c))
    sb = dict(zip([nm for (nm, _, _) in big], shards_bf))
    groups = [[("ffn1_w_in", 0)], [("ffn1_w_out", 0)], [("hgrn_w_in", None)], [("mem_w_kv", None)], [("hgrn_w_out", None)],
              [("ffn2_w_in", 0), ("ffn2_w_out", 0), ("gmlp_ln_g", None), ("gmlp_ln_b", None)],
              [("ffn1_w_in", 1), ("ffn1_w_out", 1)],
              [("gmlp_w_in", None), ("gmlp_w_out", None)],
              [("ffn2_w_in", 1), ("ffn2_w_out", 1)]]
    kind_of = {nm: k for (nm, _, k) in big}
    for nm, vec in (("gmlp_ln_g", gmlp_ln_g), ("gmlp_ln_b", gmlp_ln_b)):
        sb[nm] = vec.reshape(1, 1, -1)
        kind_of[nm] = "vec"
    gathered = {nm: [None, None] for nm in ("ffn1_w_in", "ffn1_w_out", "ffn2_w_in", "ffn2_w_out")}
    for gi, grp in enumerate(groups):
        outs = _allgather_seq("gather_%d" % gi, [(sb[nm], kind_of[nm], l) for (nm, l) in grp], gi)
        for (nm, l), o in zip(grp, outs):
            if l is None:
                gathered[nm] = o
            else:
                gathered[nm][l] = o

    ln_w = GM_GROUPS * GM_GROUP_DIM
    ln_g_full, ln_b_full = gathered["gmlp_ln_g"].reshape(1, ln_w), gathered["gmlp_ln_b"].reshape(1, ln_w)

    def rms_fwd(name, xin, g):
        (h,) = _rowcall(name, lambda rv, cv: ([_rmsnorm(rv[0], cv[0])], []), [(xin, 0, D)], [g.reshape(1, D)], [(D, BF)], [], TR)
        return h

    def ffn_fwd(tag, xin, h, w_in, w_out, layer, next_gain):
        dff = w_out[layer].shape[1]
        zg, zu, a = _ffn_in_swiglu("ffn_in_" + tag, h, w_in[layer], 1024, dff // 2)
        out = _mm("ffn_out_" + tag, a, w_out[layer], "nn", F32, 1024, 1024, dff, scale=0.5, res=xin, b_lead=0,
                  norm_gain=None if next_gain is None else next_gain.reshape(1, D))
        xo, h_next = (out, None) if next_gain is None else out
        return xo, h_next, (xin, h, zg, zu, a)

    def ffn_bwd(tag, dxo, saved, g, w_in, w_out, layer):
        xin, h, zg, zu, a = saved
        dff = w_out[layer].shape[1]
        dw_out = _mm_tn_pair("ffn_dwo_" + tag, a, dxo, "row", c_arr, dff // 2, T, scale=0.5)
        dz = _ffn_da_swiglu("ffn_da_" + tag, dxo, w_out[layer], zg, zu, 256)
        dw_in = _mm_tn_pair("ffn_dwi_" + tag, h, dz, "col", c_arr, 512, T)
        dx, dg = _mm_dh_rms("ffn_dh_" + tag, dz, w_in[layer], xin, g.reshape(1, D), dxo, 256)
        return dx, dg, dw_in, dw_out

    def rms_bwd(name, xin, g, dh, dres):
        def fn(rv, cv):
            _, vjp = jax.vjp(_rmsnorm, rv[0], cv[0])
            dx, dg = vjp(rv[1])
            if dres is not None:
                dx = dx + rv[2]
            return [dx], [dg]

        rows = [(xin, 0, D), (dh, 0, D)] + ([(dres, 0, D)] if dres is not None else [])
        dx, dg = _rowcall(name, fn, rows, [g.reshape(1, D)], [(D, F32)], [((1, D), F32)], TR)
        return dx, dg

    x0 = x.reshape(T, D)
    tgt = loss_target.reshape(T, D)
    mem2 = mem.reshape(bl * mem_len, D)
    memn = rms_fwd("rms_mem", mem2, mem_norm)

    h_f10 = rms_fwd("rms_f1l0", x0, ffn1_norm[0])
    x1, h_m0, sv_f10 = ffn_fwd("f1l0", x0, h_f10, gathered["ffn1_w_in"], gathered["ffn1_w_out"], 0, mix_norm[0])
    z_m0 = _mm("mix_in_0", h_m0, gathered["hgrn_w_in"], "nn", F32, 2048, 512, D, b_lead=0)
    kv = [_mm("kv_%d" % i, memn, gathered["mem_w_kv"], "nn", F32, 512, 512, D, b_lead=i) for i in range(2)]
    cat0, stash0 = _hgrn_fwd2(z_m0, lb_logits, hgrn_gnorm, kv[0], bl, seq)
    x2, h_f20 = _mm("mix_out_0", cat0, gathered["hgrn_w_out"], "nn", F32, 1024, 1024, cat0.shape[1], res=x1, b_lead=0,
                    norm_gain=ffn2_norm[0].reshape(1, D))
    x3, h_f11, sv_f20 = ffn_fwd("f2l0", x2, h_f20, gathered["ffn2_w_in"], gathered["ffn2_w_out"], 0, ffn1_norm[1])
    x4, h_m1, sv_f11 = ffn_fwd("f1l1", x3, h_f11, gathered["ffn1_w_in"], gathered["ffn1_w_out"], 1, mix_norm[1])
    z_m1 = _mm("mix_in_1", h_m1, gathered["gmlp_w_in"], "nn", F32, 2048, 512, D, b_lead=0)
    nc1 = seq // GM_CHUNK
    w_s, b_s = gmlp_w_s[0], gmlp_b_s[0]
    cat1 = _gmlp_fwd(z_m1, ln_g_full, ln_b_full, w_s, b_s, kv[1], bl, nc1)
    x5, h_f21 = _mm("mix_out_1", cat1, gathered["gmlp_w_out"], "nn", F32, 1024, 1024, cat1.shape[1], res=x4, b_lead=0,
                    norm_gain=ffn2_norm[1].reshape(1, D))
    x6, _, sv_f21 = ffn_fwd("f2l1", x5, h_f21, gathered["ffn2_w_in"], gathered["ffn2_w_out"], 1, None)

    def head(rv, cv):
        def f(xx, gg):
            err = _rmsnorm(xx, gg) - rv[1]
            return 0.5 * jnp.sum(jnp.mean(err * err, axis=-1, keepdims=True), axis=0, keepdims=True)

        ls, vjp = jax.vjp(f, rv[0], cv[0])
        dx, dg = vjp(jnp.ones((1, 1), F32))
        return [dx], [dg, jnp.broadcast_to(ls, (1, 128))]

    dx6, d_final, loss_part = _rowcall("loss_head", head, [(x6, 0, D), (tgt, 0, D)], [final_norm.reshape(1, D)],
                                       [(D, F32)], [((1, D), F32), ((1, 128), F32)], TR)

    rs_out = {}
    n_gather = len(groups)

    def rs(gi, items):
        outs = _rs_chips_seq("reduce_%d" % gi, [p for (_, p, _) in items], [k for (_, _, k) in items], n_gather + gi)
        for i, (key, _, _) in enumerate(items):
            rs_out[key] = (outs[2 * i], outs[2 * i + 1])

    dx5, dg_f21, dwi_f21, dwo_f21 = ffn_bwd("f2l1", dx6, sv_f21, ffn2_norm[1], gathered["ffn2_w_in"], gathered["ffn2_w_out"], 1)
    rs(0, [(("ffn2_w_out", 1), dwo_f21, "row"), (("ffn2_w_in", 1), dwi_f21, "col")])
    dcat1 = _mm("mix_dcat_1", dx5, gathered["gmlp_w_out"], "nt", F32, 1024, 1024, D, b_lead=0)
    dwo_m1 = _mm_tn_pair("mix_dwo_1", cat1, dx5, "row", c_arr, 1024, T)
    dz_m1, dkv1, d_lng, d_lnb, d_ws, d_bs = _gmlp_bwd(z_m1, dcat1, ln_g_full, ln_b_full, w_s, b_s, kv[1], bl, nc1)
    dx4, dg_m1 = _mm_dh_rms("mix_dh_1", dz_m1, gathered["gmlp_w_in"], x4, mix_norm[1].reshape(1, D), dx5, 256)
    dwi_m1 = _mm_tn_pair("mix_dwi_1", h_m1, dz_m1, "col", c_arr, 1024, T)
    rs(1, [(("gmlp_w_out", 0), dwo_m1, "row"), (("gmlp_w_in", 0), dwi_m1, "col")])
    dx3, dg_f11, dwi_f11, dwo_f11 = ffn_bwd("f1l1", dx4, sv_f11, ffn1_norm[1], gathered["ffn1_w_in"], gathered["ffn1_w_out"], 1)
    rs(2, [(("ffn1_w_out", 1), dwo_f11, "row"), (("ffn1_w_in", 1), dwi_f11, "col")])

    dx2, dg_f20, dwi_f20, dwo_f20 = ffn_bwd("f2l0", dx3, sv_f20, ffn2_norm[0], gathered["ffn2_w_in"], gathered["ffn2_w_out"], 0)
    rs(3, [(("ffn2_w_out", 0), dwo_f20, "row"), (("ffn2_w_in", 0), dwi_f20, "col")])
    dcat0 = _mm("mix_dcat_0", dx2, gathered["hgrn_w_out"], "nt", F32, 1024, 1024, D, b_lead=0)
    dwo_m0 = _mm_tn_pair("mix_dwo_0", cat0, dx2, "row", c_arr, 1024, T)
    dz_m0, dkv0, d_lb, d_gn = _hgrn_bwd2(z_m0, dcat0, stash0, lb_logits, hgrn_gnorm, kv[0], bl, seq)
    dx1, dg_m0 = _mm_dh_rms("mix_dh_0", dz_m0, gathered["hgrn_w_in"], x1, mix_norm[0].reshape(1, D), dx2, 256)
    dwi_m0 = _mm_tn_pair("mix_dwi_0", h_m0, dz_m0, "col", c_arr, 1024, T)
    rs(4, [(("hgrn_w_out", 0), dwo_m0, "row"), (("hgrn_w_in", 0), dwi_m0, "col")])

    dwkv = [_mm_tn_pair("kv_dw_%d" % i, memn, dkv, "col", c_arr, 1024, 512) for i, dkv in enumerate([dkv0, dkv1])]
    rs(5, [(("mem_w_kv", 0), dwkv[0], "col"), (("mem_w_kv", 1), dwkv[1], "col")])
    dmemn = _mm("kv_dx_0", dkv0, gathered["mem_w_kv"], "nt", F32, 512, 512, 1024, b_lead=0)
    dmemn = _mm("kv_dx_1", dkv1, gathered["mem_w_kv"], "nt", F32, 512, 512, 1024, res=dmemn, b_lead=1)
    _, d_memnorm = rms_bwd("rms_bwd_mem", mem2, mem_norm, dmemn, None)

    dx0, dg_f10, dwi_f10, dwo_f10 = ffn_bwd("f1l0", dx1, sv_f10, ffn1_norm[0], gathered["ffn1_w_in"], gathered["ffn1_w_out"], 0)
    rs(6, [(("ffn1_w_out", 0), dwo_f10, "row")])
    rs(7, [(("ffn1_w_in", 0), dwi_f10, "col")])

    shard_grads = [_finish_share("finish_" + nm, [rs_out[(nm, l)][0] for l in range(w.shape[0])],
                                 [rs_out[(nm, l)][1] for l in range(w.shape[0])], k, c_arr) for (nm, w, k) in big]

    big_w = [w for (_, w, _) in big]
    big_m = [m_ffn1_w_in, m_ffn1_w_out, m_mem_w_kv, m_hgrn_w_in, m_hgrn_w_out, m_gmlp_w_in, m_gmlp_w_out, m_ffn2_w_in, m_ffn2_w_out]
    big_v = [v_ffn1_w_in, v_ffn1_w_out, v_mem_w_kv, v_hgrn_w_in, v_hgrn_w_out, v_gmlp_w_in, v_gmlp_w_out, v_ffn2_w_in, v_ffn2_w_out]
    big_out = {}
    for (nm, w, _), g, m, v in zip(big, shard_grads, big_m, big_v):
        L, r, c = w.shape
        d2, m2, v2 = _adam_call("adam_" + nm, w.reshape(L * r, c), g.reshape(L * r, c), m.reshape(L * r, c),
                                v.reshape(L * r, c), 256)
        big_out[nm] = (g, d2.reshape(w.shape), m2.reshape(w.shape), v2.reshape(w.shape))

    d_ffn1n = _two_rows(dg_f10, dg_f11)
    d_mixn = _two_rows(dg_m0, dg_m1)
    d_ffn2n = _two_rows(dg_f20, dg_f21)
    small_parts = [loss_part[:, :1], d_memnorm, d_lb, d_ffn1n, d_mixn, d_gn, d_lng, d_lnb, d_ws, d_bs, d_ffn2n, d_final]
    red_shapes = [(1,), mem_norm.shape, lb_logits.shape, ffn1_norm.shape, mix_norm.shape, hgrn_gnorm.shape, (1, ln_w), (1, ln_w),
                  gmlp_w_s.shape, gmlp_b_s.shape, ffn2_norm.shape, final_norm.shape]
    red = _small_allreduce(_pack(small_parts, _rows_needed(red_shapes)), "reduce_small")
    (loss_v, g_memn, g_lb, g_f1n, g_mixn, g_gn, g_lng_full, g_lnb_full, g_ws, g_bs, g_f2n, g_fin) = _unpack(red, red_shapes)
    lsh = gmlp_ln_g.shape[1]
    g_lng = lax.dynamic_slice(g_lng_full, (0, chip * lsh), (1, lsh))
    g_lnb = lax.dynamic_slice(g_lnb_full, (0, chip * lsh), (1, lsh))
    small_w = [mem_norm, lb_logits, ffn1_norm, mix_norm, hgrn_gnorm, gmlp_ln_g, gmlp_ln_b, gmlp_w_s, gmlp_b_s, ffn2_norm, final_norm]
    small_g = [g_memn, g_lb, g_f1n, g_mixn, g_gn, g_lng, g_lnb, g_ws, g_bs, g_f2n, g_fin]
    small_m = [m_mem_norm, m_lb_logits, m_ffn1_norm, m_mix_norm, m_hgrn_gnorm, m_gmlp_ln_g, m_gmlp_ln_b, m_gmlp_w_s, m_gmlp_b_s, m_ffn2_norm, m_final_norm]
    small_v = [v_mem_norm, v_lb_logits, v_ffn1_norm, v_mix_norm, v_hgrn_gnorm, v_gmlp_ln_g, v_gmlp_ln_b, v_gmlp_w_s, v_gmlp_b_s, v_ffn2_norm, v_final_norm]
    sshapes = [w.shape for w in small_w]
    nrow = _rows_needed(sshapes)
    d_p, m_p, v_p = _adam_call("adam_small", _pack(small_w, nrow), _pack(small_g, nrow), _pack(small_m, nrow), _pack(small_v, nrow), nrow)
    s_delta, s_m, s_v = _unpack(d_p, sshapes), _unpack(m_p, sshapes), _unpack(v_p, sshapes)
    small_names = ["mem_norm", "lb_logits", "ffn1_norm", "mix_norm", "hgrn_gnorm", "gmlp_ln_g", "gmlp_ln_b", "gmlp_w_s", "gmlp_b_s", "ffn2_norm", "final_norm"]
    small_out = {nm: (g.reshape(w.shape), d, m, v) for nm, w, g, d, m, v in zip(small_names, small_w, small_g, s_delta, s_m, s_v)}

    order = ["mem_norm", "lb_logits", "ffn1_norm", "ffn1_w_in", "ffn1_w_out", "mix_norm", "mem_w_kv", "hgrn_w_in", "hgrn_gnorm",
             "hgrn_w_out", "gmlp_w_in", "gmlp_ln_g", "gmlp_ln_b", "gmlp_w_s", "gmlp_b_s", "gmlp_w_out", "ffn2_norm", "ffn2_w_in",
             "ffn2_w_out", "final_norm"]
    allo = {**big_out, **small_out}
    grad_x = dx0.reshape(x.shape)
    return (loss_v.reshape(()), grad_x, *[allo[n][0] for n in order], *[allo[n][1] for n in order],
            *[allo[n][2] for n in order], *[allo[n][3] for n in order])
```

```python
import functools

import jax
import jax.numpy as jnp
from jax import lax
from jax.experimental import pallas as pl
from jax.experimental.pallas import tpu as pltpu
from jax.experimental.pallas import tpu_sc as plsc

BF = jnp.bfloat16
F32 = jnp.float32
MESH = pl.DeviceIdType.MESH

EPS = 1e-6
D_MODEL = 1024
HG_HEADS = 8
HG_DIM = 128
HG_CHUNK = 64
GM_CHUNK = 128
GM_GROUPS = 8
GM_GROUP_DIM = 256
XA_HEADS = 4
XA_DIM = 256
ADAM_LR = 0.001
ADAM_B1 = 0.9
ADAM_B2 = 0.999
ADAM_EPS = 1e-08
ADAM_WD = 0.01
ADAM_STEP = 10

VMEM_CAP_BYTES = 60 * 1024 * 1024
LANES = 1024


def _pick(n, cap, mult=16):
    if n <= cap:
        return n
    for d in range(cap - cap % mult, 0, -mult):
        if n % d == 0:
            return d
    raise ValueError((n, cap, mult))


def _dg(a, b, ca, cb):
    return lax.dot_general(a.astype(BF), b.astype(BF), (((ca,), (cb,)), ((), ())), preferred_element_type=F32)


@jax.custom_vjp
def dot_nn(a, b):
    return _dg(a, b, 1, 0)


def _nn_fwd(a, b):
    return _dg(a, b, 1, 0), (a, b)


def _nn_bwd(r, g):
    a, b = r
    return _dg(g, b, 1, 1), _dg(a, g, 0, 0)


dot_nn.defvjp(_nn_fwd, _nn_bwd)


@jax.custom_vjp
def dot_nt(a, b):
    return _dg(a, b, 1, 1)


def _nt_fwd(a, b):
    return _dg(a, b, 1, 1), (a, b)


def _nt_bwd(r, g):
    a, b = r
    return _dg(g, b, 1, 0), _dg(g, a, 0, 0)


dot_nt.defvjp(_nt_fwd, _nt_bwd)


@jax.custom_vjp
def dot_tn(a, b):
    return _dg(a, b, 0, 0)


def _tn_fwd(a, b):
    return _dg(a, b, 0, 0), (a, b)


def _tn_bwd(r, g):
    a, b = r
    return _dg(b, g, 1, 1), _dg(a, g, 1, 0)


dot_tn.defvjp(_tn_fwd, _tn_bwd)


def _rmsnorm(x, g):
    return x * lax.rsqrt(jnp.mean(x * x, axis=-1, keepdims=True) + EPS) * g


def _silu(x):
    return x * jax.nn.sigmoid(x)


@jax.custom_vjp
def _gelu(x):
    return 0.5 * x * (1.0 + lax.erf(x * (0.5 ** 0.5)))


def _gelu_fwd(x):
    return _gelu(x), x


def _gelu_bwd(x, g):
    t = x * (0.5 ** 0.5)
    cdf = 0.5 * (1.0 + lax.erf(t))
    return (g * (cdf + x * (jnp.exp(-(t * t)) * (0.5 / 3.141592653589793) ** 0.5)),)


_gelu.defvjp(_gelu_fwd, _gelu_bwd)


def _softmax_last(s):
    m = lax.stop_gradient(jnp.max(s, axis=-1, keepdims=True))
    e = jnp.exp(s - m)
    return e / jnp.sum(e, axis=-1, keepdims=True)


def _tril(n):
    r = lax.broadcasted_iota(jnp.int32, (n, n), 0)
    c = lax.broadcasted_iota(jnp.int32, (n, n), 1)
    return r >= c


def _cumsum_rows(l):
    n = l.shape[0]
    return lax.dot_general(_tril(n).astype(F32), l, (((1,), (0,)), ((), ())),
                           precision=lax.Precision.HIGHEST, preferred_element_type=F32)


def _attention(zx, mk, mv):
    s = dot_nt(zx, mk) * (XA_DIM ** -0.5)
    return dot_nn(_softmax_last(s), mv)


def _hgrn_head(zq, zf, zi, zg, l0, l1, l2, gn, S):
    m = lax.stop_gradient(jnp.maximum(jnp.maximum(l0, l1), l2))
    e0 = jnp.exp(l0 - m)
    lb = e0 / (e0 + jnp.exp(l1 - m) + jnp.exp(l2 - m))
    q = _silu(zq)
    f = lb + (1.0 - lb) * jax.nn.sigmoid(zf)
    k = 1.0 - f
    b = _cumsum_rows(jnp.log(f))
    b_last = b[HG_CHUNK - 1:HG_CHUNK, :]
    q_dec = q * jnp.exp(b)
    k_inv = k * jnp.exp(-b)
    a = jnp.where(_tril(HG_CHUNK), dot_nt(q_dec, k_inv), 0.0)
    o = dot_nn(a, zi) + dot_nn(q_dec, S)
    S_new = jnp.exp(b_last).reshape(HG_DIM, 1) * S + dot_tn(k * jnp.exp(b_last - b), zi)
    o = _rmsnorm(o, gn) * _silu(zg)
    return o, S_new


def _gmlp_block(zu, zv, zx, lng, lnb, ws, bs, mk, mv):
    gv = [_gelu(v) for v in zv]
    width = GM_GROUPS * GM_GROUP_DIM
    mu = sum(jnp.sum(g, axis=-1, keepdims=True) for g in gv) / width
    xc = [g - mu for g in gv]
    var = sum(jnp.sum(c * c, axis=-1, keepdims=True) for c in xc) / width
    r = lax.rsqrt(var + EPS)
    outs = []
    for g in range(GM_GROUPS):
        v = xc[g] * r * lng[g] + lnb[g]
        w = jnp.where(_tril(GM_CHUNK), ws[g], 0.0)
        mixed = dot_nn(w, v) + bs[g].reshape(GM_CHUNK, 1)
        outs.append(_gelu(zu[g]) * mixed)
    for a in range(XA_HEADS):
        outs.append(_attention(zx[a], mk[a], mv[a]))
    return outs


def _rowcall(name, fn, rows, consts, row_outs, acc_outs, tr):
    nrows = rows[0][0].shape[0]
    tr = _pick(nrows, tr)
    n_r, n_c, n_ro, n_ao = len(rows), len(consts), len(row_outs), len(acc_outs)

    def kern(*refs):
        rv = [r[...] for r in refs[:n_r]]
        cv = [r[...] for r in refs[n_r:n_r + n_c]]
        ro_refs = refs[n_r + n_c:n_r + n_c + n_ro]
        ao_refs = refs[n_r + n_c + n_ro:]
        ro, ao = fn(rv, cv)
        for ref, v in zip(ro_refs, ro):
            ref[...] = v.astype(ref.dtype)
        if n_ao:
            @pl.when(pl.program_id(0) == 0)
            def _():
                for ref in ao_refs:
                    ref[...] = jnp.zeros(ref.shape, ref.dtype)

            for ref, v in zip(ao_refs, ao):
                ref[...] += v.astype(ref.dtype)

    in_specs = [pl.BlockSpec((tr, w), functools.partial(lambda i, cb: (i, cb), cb=cb)) for (_, cb, w) in rows]
    in_specs += [pl.BlockSpec(c.shape, lambda i: (0, 0)) for c in consts]
    out_specs = [pl.BlockSpec((tr, w), lambda i: (i, 0)) for (w, _) in row_outs]
    out_specs += [pl.BlockSpec(s, lambda i: (0, 0)) for (s, _) in acc_outs]
    out_shape = [jax.ShapeDtypeStruct((nrows, w), dt) for (w, dt) in row_outs]
    out_shape += [jax.ShapeDtypeStruct(s, dt) for (s, dt) in acc_outs]
    outs = pl.pallas_call(
        kern, grid=(nrows // tr,), in_specs=in_specs, out_specs=out_specs, out_shape=out_shape, name=name,
        compiler_params=pltpu.CompilerParams(dimension_semantics=("arbitrary",),
                                             vmem_limit_bytes=VMEM_CAP_BYTES),
    )(*[a for (a, _, _) in rows], *consts)
    return outs


def _mm(name, a, b, mode, out_dtype, tm, tn, tk, scale=1.0, res=None, a_lead=None, b_lead=None, norm_gain=None):
    ash = a.shape[-2:]
    bsh = b.shape[-2:]
    if mode == "nn":
        (M, K), (K2, N) = ash, bsh
    elif mode == "nt":
        (M, K), (N, K2) = ash, bsh
    else:
        (K, M), (K2, N) = ash, bsh
    assert K == K2, (name, a.shape, b.shape)
    tm, tn, tk = min(tm, M), min(tn, N), min(tk, K)
    assert M % tm == 0 and N % tn == 0 and K % tk == 0, (name, M, N, K, tm, tn, tk)
    nk = K // tk
    dims = {"nn": (1, 0), "nt": (1, 1), "tn": (0, 0)}[mode]

    def lead(spec_shape, index_fn, lead_idx):
        if lead_idx is None:
            return pl.BlockSpec(spec_shape, index_fn)
        return pl.BlockSpec((None,) + spec_shape, lambda i, j, k: (lead_idx,) + index_fn(i, j, k))

    if mode == "tn":
        a_spec = lead((tk, tm), lambda i, j, k: (k, i), a_lead)
    else:
        a_spec = lead((tm, tk), lambda i, j, k: (i, k), a_lead)
    if mode == "nt":
        b_spec = lead((tn, tk), lambda i, j, k: (j, k), b_lead)
    else:
        b_spec = lead((tk, tn), lambda i, j, k: (k, j), b_lead)
    o_spec = pl.BlockSpec((tm, tn), lambda i, j, k: (i, j))
    has_res = res is not None
    has_norm = norm_gain is not None
    assert not has_norm or tn == N

    def kern(*refs):
        a_ref, b_ref = refs[0], refs[1]
        pos = 2
        res_ref = gain_ref = h_ref = None
        if has_res:
            res_ref, pos = refs[pos], pos + 1
        if has_norm:
            gain_ref, pos = refs[pos], pos + 1
        o_ref, pos = refs[pos], pos + 1
        if has_norm:
            h_ref = refs[pos]
        acc_ref = refs[-1] if nk > 1 else None
        p = lax.dot_general(a_ref[...].astype(BF), b_ref[...].astype(BF), (((dims[0],), (dims[1],)), ((), ())),
                            preferred_element_type=F32)

        def finish(v):
            if scale != 1.0:
                v = v * scale
            if has_res:
                v = res_ref[...] + v
            o_ref[...] = v.astype(o_ref.dtype)
            if has_norm:
                h_ref[...] = _rmsnorm(v, gain_ref[...]).astype(h_ref.dtype)

        if nk == 1:
            finish(p)
        else:
            k = pl.program_id(2)

            @pl.when(k == 0)
            def _():
                acc_ref[...] = p

            @pl.when(k > 0)
            def _():
                acc_ref[...] += p

            @pl.when(k == nk - 1)
            def _():
                finish(acc_ref[...])

    ins = [a, b] + ([res] if has_res else []) + ([norm_gain] if has_norm else [])
    in_specs = [a_spec, b_spec] + ([o_spec] if has_res else [])
    in_specs += [pl.BlockSpec((1, N), lambda i, j, k: (0, 0))] if has_norm else []
    out_sd = jax.ShapeDtypeStruct((M, N), out_dtype)
    return pl.pallas_call(
        kern, grid=(M // tm, N // tn, nk), in_specs=in_specs,
        out_specs=[o_spec, o_spec] if has_norm else o_spec,
        out_shape=[out_sd, jax.ShapeDtypeStruct((M, N), BF)] if has_norm else out_sd,
        scratch_shapes=[pltpu.VMEM((tm, tn), F32)] if nk > 1 else [],
        name=name,
        compiler_params=pltpu.CompilerParams(dimension_semantics=("parallel", "parallel", "arbitrary"),
                                             vmem_limit_bytes=VMEM_CAP_BYTES),
    )(*ins)


def _ffn_in_swiglu(name, h, w3, tm, tn):
    T, D = h.shape
    dff = w3.shape[2] // 2
    tm = min(tm, T)
    assert T % tm == 0 and dff % tn == 0
    nj = dff // tn

    def kern(h_ref, wg_ref, wu_ref, zg_ref, zu_ref, a_ref):
        hb = h_ref[...]
        g = jnp.dot(hb, wg_ref[...], preferred_element_type=F32).astype(BF)
        u = jnp.dot(hb, wu_ref[...], preferred_element_type=F32).astype(BF)
        zg_ref[...] = g
        zu_ref[...] = u
        a_ref[...] = (_silu(g.astype(F32)) * u.astype(F32)).astype(BF)

    o_spec = pl.BlockSpec((tm, tn), lambda i, j: (i, j))
    return pl.pallas_call(
        kern, grid=(T // tm, nj),
        in_specs=[pl.BlockSpec((tm, D), lambda i, j: (i, 0)),
                  pl.BlockSpec((None, D, tn), lambda i, j: (0, 0, j)),
                  pl.BlockSpec((None, D, tn), lambda i, j: (0, 0, j + nj))],
        out_specs=[o_spec, o_spec, o_spec],
        out_shape=[jax.ShapeDtypeStruct((T, dff), BF)] * 3, name=name,
        compiler_params=pltpu.CompilerParams(dimension_semantics=("parallel", "arbitrary"),
                                             vmem_limit_bytes=VMEM_CAP_BYTES),
    )(h, w3, w3)


def _ffn_da_swiglu(name, dxo, w3, zg, zu, tm):
    T, D = dxo.shape
    dff = w3.shape[1]
    tm = min(tm, T)
    assert T % tm == 0 and dff % 2 == 0
    hc = dff // 2

    def kern(d_ref, w_ref, g_ref, u_ref, dz_ref):
        db = (d_ref[...] * 0.5).astype(BF)
        for s in range(2):
            cols = slice(s * hc, (s + 1) * hc)
            da = lax.dot_general(db, w_ref[cols, :], (((1,), (1,)), ((), ())), preferred_element_type=F32)
            g = g_ref[:, cols].astype(F32)
            sg = 1.0 / (1.0 + jnp.exp(-g))
            gs = g * sg
            dab = da.astype(BF)
            dz_ref[:, cols] = (dab * u_ref[:, cols]) * (sg + gs * (1.0 - sg)).astype(BF)
            dz_ref[:, dff + s * hc:dff + (s + 1) * hc] = dab * gs.astype(BF)

    row = lambda w: pl.BlockSpec((tm, w), lambda i: (i, 0))
    return pl.pallas_call(
        kern, grid=(T // tm,),
        in_specs=[row(D), pl.BlockSpec((None, dff, D), lambda i: (0, 0, 0), pipeline_mode=pl.Buffered(1)), row(dff), row(dff)],
        out_specs=row(2 * dff), out_shape=jax.ShapeDtypeStruct((T, 2 * dff), BF), name=name,
        compiler_params=pltpu.CompilerParams(dimension_semantics=("arbitrary",), vmem_limit_bytes=VMEM_CAP_BYTES),
    )(dxo, w3, zg, zu)


def _mm_dh_rms(name, dz, w3, xin, g, dres, tm):
    T, K = dz.shape
    D = w3.shape[1]
    tm = min(tm, T)
    assert T % tm == 0

    def kern(dz_ref, w_ref, x_ref, g_ref, r_ref, dx_ref, dg_ref):
        dh = lax.dot_general(dz_ref[...], w_ref[...], (((1,), (1,)), ((), ())), preferred_element_type=F32)
        _, vjp = jax.vjp(_rmsnorm, x_ref[...], g_ref[...])
        dx, dg = vjp(dh)
        dx_ref[...] = dx + r_ref[...]

        @pl.when(pl.program_id(0) == 0)
        def _():
            dg_ref[...] = jnp.zeros(dg_ref.shape, F32)

        dg_ref[...] += dg

    row = lambda w: pl.BlockSpec((tm, w), lambda i: (i, 0))
    one = pl.BlockSpec((1, D), lambda i: (0, 0))
    return pl.pallas_call(
        kern, grid=(T // tm,),
        in_specs=[row(K), pl.BlockSpec((None, D, K), lambda i: (0, 0, 0), pipeline_mode=pl.Buffered(1)), row(D), one, row(D)],
        out_specs=[row(D), one], out_shape=[jax.ShapeDtypeStruct((T, D), F32), jax.ShapeDtypeStruct((1, D), F32)], name=name,
        compiler_params=pltpu.CompilerParams(dimension_semantics=("arbitrary",), vmem_limit_bytes=VMEM_CAP_BYTES),
    )(dz, w3, xin, g, dres)


def _mm_tn_pair(name, a, b, kind, c_arr, tq, tk, scale=1.0):
    T, M = a.shape
    _, N = b.shape
    tk = min(tk, T)
    assert T % tk == 0
    nk = T // tk
    if kind == "col":
        hm = M // 2
        assert N % tq == 0
        nq = N // tq
        tile = (hm, tq)
        a_spec = pl.BlockSpec((tk, hm), lambda h, q, k, c: (k, jnp.bitwise_xor(h, 1 - c[0])))
        b_spec = pl.BlockSpec((tk, tq), lambda h, q, k, c: (k, q))
        o_spec = pl.BlockSpec(tile, lambda h, q, k, c: (0, q * h))
        out_sd = (hm, N)
    else:
        hn = N // 2
        assert M % tq == 0
        nq = M // tq
        tile = (tq, hn)
        a_spec = pl.BlockSpec((tk, tq), lambda h, q, k, c: (k, q))
        b_spec = pl.BlockSpec((tk, hn), lambda h, q, k, c: (k, jnp.bitwise_xor(h, 1 - c[0])))
        o_spec = pl.BlockSpec(tile, lambda h, q, k, c: (q * h, 0))
        out_sd = (M, hn)

    def kern(c_ref, a_ref, b_ref, o_ref, acc, stage, recv, ssem, rsem):
        h, q, k = pl.program_id(0), pl.program_id(1), pl.program_id(2)
        x, y, c, _ = _place()
        p = lax.dot_general(a_ref[...].astype(BF), b_ref[...].astype(BF), (((0,), (0,)), ((), ())), preferred_element_type=F32)

        @pl.when(k == 0)
        def _():
            acc[...] = p

        @pl.when(k > 0)
        def _():
            acc[...] += p

        def send(slot, qq):
            return pltpu.make_async_remote_copy(src_ref=stage.at[slot], dst_ref=recv.at[qq], send_sem=ssem.at[slot],
                                                recv_sem=rsem.at[qq], device_id=(x, y, 1 - c), device_id_type=MESH)

        last = k == nk - 1

        @pl.when(jnp.logical_and(last, h == 0))
        def _():
            slot = q % 2

            @pl.when(q >= 2)
            def _():
                send(slot, q).wait_send()

            stage[slot] = (acc[...] * scale).astype(BF)
            send(slot, q).start()

        @pl.when(jnp.logical_and(last, h == 1))
        def _():
            @pl.when(q == 0)
            def _():
                for s in range(min(nq, 2)):
                    send(s, 0).wait_send()

            send(0, q).wait_recv()
            o_ref[...] = (acc[...] * scale + recv[q].astype(F32)).astype(o_ref.dtype)

    return pl.pallas_call(
        kern,
        grid_spec=pltpu.PrefetchScalarGridSpec(
            num_scalar_prefetch=1, grid=(2, nq, nk), in_specs=[a_spec, b_spec], out_specs=o_spec,
            scratch_shapes=[pltpu.VMEM(tile, F32), pltpu.VMEM((2,) + tile, BF), pltpu.VMEM((nq,) + tile, BF),
                            pltpu.SemaphoreType.DMA((2,)), pltpu.SemaphoreType.DMA((nq,))]),
        out_shape=jax.ShapeDtypeStruct(out_sd, BF), name=name,
        compiler_params=pltpu.CompilerParams(dimension_semantics=("arbitrary", "arbitrary", "arbitrary"),
                                             vmem_limit_bytes=VMEM_CAP_BYTES),
    )(c_arr, a, b)


def _kv_pieces(kv_ref):
    W = XA_HEADS * XA_DIM
    mk = [kv_ref[:, a * XA_DIM:(a + 1) * XA_DIM] for a in range(XA_HEADS)]
    mv = [kv_ref[:, W + a * XA_DIM:W + (a + 1) * XA_DIM] for a in range(XA_HEADS)]
    return mk, mv


def _lb_pieces(lb_ref):
    return [[lb_ref[r:r + 1, h * HG_DIM:(h + 1) * HG_DIM] for h in range(HG_HEADS)] for r in range(3)]


HG_SUB = 4


def _hgrn_rows(z_ref):
    W = HG_HEADS * HG_DIM

    def piece(c, col, w):
        return z_ref[c * HG_CHUNK:(c + 1) * HG_CHUNK, col:col + w]

    zq = [[piece(c, h * HG_DIM, HG_DIM) for h in range(HG_HEADS)] for c in range(HG_SUB)]
    zf = [[piece(c, W + h * HG_DIM, HG_DIM) for h in range(HG_HEADS)] for c in range(HG_SUB)]
    zi = [[piece(c, 2 * W + h * HG_DIM, HG_DIM) for h in range(HG_HEADS)] for c in range(HG_SUB)]
    zg = [[piece(c, 3 * W + h * HG_DIM, HG_DIM) for h in range(HG_HEADS)] for c in range(HG_SUB)]
    zx = [z_ref[:, 4 * W + a * XA_DIM:4 * W + (a + 1) * XA_DIM] for a in range(XA_HEADS)]
    return zq, zf, zi, zg, zx


def _hgrn_steps(zq, zf, zi, zg, zx, l0, l1, l2, gn, mk, mv, S):
    mix = []
    for c in range(HG_SUB):
        row, s_next = [], []
        for h in range(HG_HEADS):
            o, sn = _hgrn_head(zq[c][h], zf[c][h], zi[c][h], zg[c][h], l0[h], l1[h], l2[h], gn, S[h])
            row.append(o)
            s_next.append(sn)
        mix.append(row)
        S = s_next
    att = [_attention(zx[a], mk[a], mv[a]) for a in range(XA_HEADS)]
    return mix, att, S


def _hgrn_fwd2(z, lb_logits, gnorm, kv, bl, seq):
    T, zw = z.shape
    mem_len = kv.shape[0] // bl
    cat_w = HG_HEADS * HG_DIM + XA_HEADS * XA_DIM
    R = HG_SUB * HG_CHUNK
    nb = seq // R

    def kern(z_ref, lb_ref, gn_ref, kv_ref, cat_ref, st_ref, s_scr):
        @pl.when(pl.program_id(1) == 0)
        def _():
            s_scr[...] = jnp.zeros(s_scr.shape, F32)

        st_ref[...] = s_scr[...]
        zq, zf, zi, zg, zx = _hgrn_rows(z_ref)
        mk, mv = _kv_pieces(kv_ref)
        l0, l1, l2 = _lb_pieces(lb_ref)
        S = [s_scr[h] for h in range(HG_HEADS)]
        mix, att, s_new = _hgrn_steps(zq, zf, zi, zg, zx, l0, l1, l2, gn_ref[...], mk, mv, S)
        for c in range(HG_SUB):
            for h in range(HG_HEADS):
                cat_ref[c * HG_CHUNK:(c + 1) * HG_CHUNK, h * HG_DIM:(h + 1) * HG_DIM] = mix[c][h].astype(cat_ref.dtype)
        for h in range(HG_HEADS):
            s_scr[h] = s_new[h]
        base = HG_HEADS * HG_DIM
        for a in range(XA_HEADS):
            cat_ref[:, base + a * XA_DIM:base + (a + 1) * XA_DIM] = att[a].astype(cat_ref.dtype)

    return pl.pallas_call(
        kern, grid=(bl, nb),
        in_specs=[pl.BlockSpec((R, zw), lambda b, n: (b * nb + n, 0)),
                  pl.BlockSpec(lb_logits.shape, lambda b, n: (0, 0)),
                  pl.BlockSpec(gnorm.shape, lambda b, n: (0, 0)),
                  pl.BlockSpec((mem_len, kv.shape[1]), lambda b, n: (b, 0))],
        out_specs=[pl.BlockSpec((R, cat_w), lambda b, n: (b * nb + n, 0)),
                   pl.BlockSpec((None, HG_HEADS, HG_DIM, HG_DIM), lambda b, n: (b * nb + n, 0, 0, 0))],
        out_shape=[jax.ShapeDtypeStruct((T, cat_w), BF),
                   jax.ShapeDtypeStruct((bl * nb, HG_HEADS, HG_DIM, HG_DIM), F32)],
        scratch_shapes=[pltpu.VMEM((HG_HEADS, HG_DIM, HG_DIM), F32)],
        name="hgrn_fwd",
        compiler_params=pltpu.CompilerParams(dimension_semantics=("arbitrary", "arbitrary"), vmem_limit_bytes=VMEM_CAP_BYTES),
    )(z, lb_logits, gnorm, kv)


def _hgrn_bwd2(z, dcat, stash, lb_logits, gnorm, kv, bl, seq):
    T, zw = z.shape
    mem_len = kv.shape[0] // bl
    cat_w = dcat.shape[1]
    R = HG_SUB * HG_CHUNK
    nb = seq // R

    def kern(z_ref, dc_ref, st_ref, lb_ref, gn_ref, kv_ref, dz_ref, dkv_ref, dlb_ref, dgn_ref, ds_scr):
        first = jnp.logical_and(pl.program_id(0) == 0, pl.program_id(1) == 0)

        @pl.when(pl.program_id(1) == 0)
        def _():
            ds_scr[...] = jnp.zeros(ds_scr.shape, F32)
            dkv_ref[...] = jnp.zeros(dkv_ref.shape, F32)

        @pl.when(first)
        def _():
            dlb_ref[...] = jnp.zeros(dlb_ref.shape, F32)
            dgn_ref[...] = jnp.zeros(dgn_ref.shape, F32)

        zq, zf, zi, zg, zx = _hgrn_rows(z_ref)
        mk, mv = _kv_pieces(kv_ref)
        l0, l1, l2 = _lb_pieces(lb_ref)
        S = [st_ref[h] for h in range(HG_HEADS)]
        _, vjp = jax.vjp(_hgrn_steps, zq, zf, zi, zg, zx, l0, l1, l2, gn_ref[...], mk, mv, S)
        d_mix = [[dc_ref[c * HG_CHUNK:(c + 1) * HG_CHUNK, h * HG_DIM:(h + 1) * HG_DIM] for h in range(HG_HEADS)]
                 for c in range(HG_SUB)]
        base = HG_HEADS * HG_DIM
        d_att = [dc_ref[:, base + a * XA_DIM:base + (a + 1) * XA_DIM] for a in range(XA_HEADS)]
        d_s = [ds_scr[h] for h in range(HG_HEADS)]
        dzq, dzf, dzi, dzg, dzx, dl0, dl1, dl2, dgn, dmk, dmv, dS = vjp((d_mix, d_att, d_s))
        W = HG_HEADS * HG_DIM
        for c in range(HG_SUB):
            rows = slice(c * HG_CHUNK, (c + 1) * HG_CHUNK)
            for h in range(HG_HEADS):
                for k, part in enumerate((dzq, dzf, dzi, dzg)):
                    dz_ref[rows, k * W + h * HG_DIM:k * W + (h + 1) * HG_DIM] = part[c][h].astype(dz_ref.dtype)
        for h in range(HG_HEADS):
            sl = slice(h * HG_DIM, (h + 1) * HG_DIM)
            ds_scr[h] = dS[h]
            dlb_ref[0:1, sl] += dl0[h]
            dlb_ref[1:2, sl] += dl1[h]
            dlb_ref[2:3, sl] += dl2[h]
        dgn_ref[...] += dgn
        KW = XA_HEADS * XA_DIM
        for a in range(XA_HEADS):
            dz_ref[:, 4 * W + a * XA_DIM:4 * W + (a + 1) * XA_DIM] = dzx[a].astype(dz_ref.dtype)
            dkv_ref[:, a * XA_DIM:(a + 1) * XA_DIM] += dmk[a]
            dkv_ref[:, KW + a * XA_DIM:KW + (a + 1) * XA_DIM] += dmv[a]

    rev = lambda b, n: (b * nb + (nb - 1 - n), 0)
    return pl.pallas_call(
        kern, grid=(bl, nb),
        in_specs=[pl.BlockSpec((R, zw), rev),
                  pl.BlockSpec((R, cat_w), rev),
                  pl.BlockSpec((None, HG_HEADS, HG_DIM, HG_DIM), lambda b, n: (b * nb + (nb - 1 - n), 0, 0, 0)),
                  pl.BlockSpec(lb_logits.shape, lambda b, n: (0, 0)),
                  pl.BlockSpec(gnorm.shape, lambda b, n: (0, 0)),
                  pl.BlockSpec((mem_len, kv.shape[1]), lambda b, n: (b, 0))],
        out_specs=[pl.BlockSpec((R, zw), rev),
                   pl.BlockSpec((mem_len, kv.shape[1]), lambda b, n: (b, 0)),
                   pl.BlockSpec(lb_logits.shape, lambda b, n: (0, 0)),
                   pl.BlockSpec(gnorm.shape, lambda b, n: (0, 0))],
        out_shape=[jax.ShapeDtypeStruct((T, zw), BF), jax.ShapeDtypeStruct(kv.shape, F32),
                   jax.ShapeDtypeStruct(lb_logits.shape, F32), jax.ShapeDtypeStruct(gnorm.shape, F32)],
        scratch_shapes=[pltpu.VMEM((HG_HEADS, HG_DIM, HG_DIM), F32)],
        name="hgrn_bwd",
        compiler_params=pltpu.CompilerParams(dimension_semantics=("arbitrary", "arbitrary"), vmem_limit_bytes=VMEM_CAP_BYTES),
    )(z, dcat, stash, lb_logits, gnorm, kv)


GM_SUB = 2


def _gmlp_pieces(z_ref):
    W = GM_GROUPS * GM_GROUP_DIM
    zu = [z_ref[:, g * GM_GROUP_DIM:(g + 1) * GM_GROUP_DIM] for g in range(GM_GROUPS)]
    zv = [z_ref[:, W + g * GM_GROUP_DIM:W + (g + 1) * GM_GROUP_DIM] for g in range(GM_GROUPS)]
    zx = [z_ref[:, 2 * W + a * XA_DIM:2 * W + (a + 1) * XA_DIM] for a in range(XA_HEADS)]
    return zu, zv, zx


def _gmlp_params(lng_ref, lnb_ref, ws_ref, bs_ref):
    lng = [lng_ref[:, g * GM_GROUP_DIM:(g + 1) * GM_GROUP_DIM] for g in range(GM_GROUPS)]
    lnb = [lnb_ref[:, g * GM_GROUP_DIM:(g + 1) * GM_GROUP_DIM] for g in range(GM_GROUPS)]
    ws = [ws_ref[g] for g in range(GM_GROUPS)]
    bs = [bs_ref[g:g + 1, :] for g in range(GM_GROUPS)]
    return lng, lnb, ws, bs


def _gmlp_fwd(z, ln_g, ln_b, w_s, b_s, kv, bl, nc):
    T, zw = z.shape
    mem_len = kv.shape[0] // bl
    cat_w = GM_GROUPS * GM_GROUP_DIM + XA_HEADS * XA_DIM

    assert nc % GM_SUB == 0
    nc = nc // GM_SUB
    R = GM_SUB * GM_CHUNK

    def kern(z_ref, lng_ref, lnb_ref, ws_ref, bs_ref, kv_ref, cat_ref):
        lng, lnb, ws, bs = _gmlp_params(lng_ref, lnb_ref, ws_ref, bs_ref)
        mk, mv = _kv_pieces(kv_ref)
        for c in range(GM_SUB):
            rows = pl.ds(c * GM_CHUNK, GM_CHUNK)
            zu, zv, zx = _gmlp_pieces(z_ref.at[rows])
            out = cat_ref.at[rows]
            outs = _gmlp_block(zu, zv, zx, lng, lnb, ws, bs, mk, mv)
            for g in range(GM_GROUPS):
                out[:, g * GM_GROUP_DIM:(g + 1) * GM_GROUP_DIM] = outs[g].astype(cat_ref.dtype)
            base = GM_GROUPS * GM_GROUP_DIM
            for a in range(XA_HEADS):
                out[:, base + a * XA_DIM:base + (a + 1) * XA_DIM] = outs[GM_GROUPS + a].astype(cat_ref.dtype)

    full2 = lambda b, n: (0, 0)
    return pl.pallas_call(
        kern, grid=(bl, nc),
        in_specs=[pl.BlockSpec((R, zw), lambda b, n: (b * nc + n, 0)),
                  pl.BlockSpec(ln_g.shape, full2), pl.BlockSpec(ln_b.shape, full2),
                  pl.BlockSpec(w_s.shape, lambda b, n: (0, 0, 0)), pl.BlockSpec(b_s.shape, full2),
                  pl.BlockSpec((mem_len, kv.shape[1]), lambda b, n: (b, 0))],
        out_specs=pl.BlockSpec((R, cat_w), lambda b, n: (b * nc + n, 0)),
        out_shape=jax.ShapeDtypeStruct((T, cat_w), BF),
        name="gmlp_fwd",
        compiler_params=pltpu.CompilerParams(dimension_semantics=("arbitrary", "arbitrary"), vmem_limit_bytes=VMEM_CAP_BYTES),
    )(z, ln_g, ln_b, w_s, b_s, kv)


def _gmlp_bwd(z, dcat, ln_g, ln_b, w_s, b_s, kv, bl, nc):
    T, zw = z.shape
    mem_len = kv.shape[0] // bl
    cat_w = dcat.shape[1]
    assert nc % GM_SUB == 0
    nc = nc // GM_SUB

    def kern(z_ref, dc_ref, lng_ref, lnb_ref, ws_ref, bs_ref, kv_ref,
             dz_ref, dkv_ref, dlng_ref, dlnb_ref, dws_ref, dbs_ref):
        first = jnp.logical_and(pl.program_id(0) == 0, pl.program_id(1) == 0)

        @pl.when(pl.program_id(1) == 0)
        def _():
            dkv_ref[...] = jnp.zeros(dkv_ref.shape, F32)

        @pl.when(first)
        def _():
            dlng_ref[...] = jnp.zeros(dlng_ref.shape, F32)
            dlnb_ref[...] = jnp.zeros(dlnb_ref.shape, F32)
            dws_ref[...] = jnp.zeros(dws_ref.shape, F32)
            dbs_ref[...] = jnp.zeros(dbs_ref.shape, F32)

        lng, lnb, ws, bs = _gmlp_params(lng_ref, lnb_ref, ws_ref, bs_ref)
        mk, mv = _kv_pieces(kv_ref)
        W = GM_GROUPS * GM_GROUP_DIM
        KW = XA_HEADS * XA_DIM
        for c in range(GM_SUB):
            rows = pl.ds(c * GM_CHUNK, GM_CHUNK)
            zu, zv, zx = _gmlp_pieces(z_ref.at[rows])
            dc, dz = dc_ref.at[rows], dz_ref.at[rows]
            _, vjp = jax.vjp(_gmlp_block, zu, zv, zx, lng, lnb, ws, bs, mk, mv)
            d_outs = [dc[:, g * GM_GROUP_DIM:(g + 1) * GM_GROUP_DIM] for g in range(GM_GROUPS)]
            d_outs += [dc[:, W + a * XA_DIM:W + (a + 1) * XA_DIM] for a in range(XA_HEADS)]
            dzu, dzv, dzx, dlng, dlnb, dws, dbs, dmk, dmv = vjp(d_outs)
            for g in range(GM_GROUPS):
                sl = slice(g * GM_GROUP_DIM, (g + 1) * GM_GROUP_DIM)
                dz[:, sl] = dzu[g].astype(dz_ref.dtype)
                dz[:, W + g * GM_GROUP_DIM:W + (g + 1) * GM_GROUP_DIM] = dzv[g].astype(dz_ref.dtype)
                dlng_ref[:, sl] += dlng[g]
                dlnb_ref[:, sl] += dlnb[g]
                dws_ref[g] += dws[g]
                dbs_ref[g:g + 1, :] += dbs[g]
            for a in range(XA_HEADS):
                dz[:, 2 * W + a * XA_DIM:2 * W + (a + 1) * XA_DIM] = dzx[a].astype(dz_ref.dtype)
                dkv_ref[:, a * XA_DIM:(a + 1) * XA_DIM] += dmk[a]
                dkv_ref[:, KW + a * XA_DIM:KW + (a + 1) * XA_DIM] += dmv[a]

    full2 = lambda b, n: (0, 0)
    full3 = lambda b, n: (0, 0, 0)
    blk = lambda b, n: (b * nc + n, 0)
    return pl.pallas_call(
        kern, grid=(bl, nc),
        in_specs=[pl.BlockSpec((GM_SUB * GM_CHUNK, zw), blk), pl.BlockSpec((GM_SUB * GM_CHUNK, cat_w), blk),
                  pl.BlockSpec(ln_g.shape, full2), pl.BlockSpec(ln_b.shape, full2),
                  pl.BlockSpec(w_s.shape, full3), pl.BlockSpec(b_s.shape, full2),
                  pl.BlockSpec((mem_len, kv.shape[1]), lambda b, n: (b, 0))],
        out_specs=[pl.BlockSpec((GM_SUB * GM_CHUNK, zw), blk),
                   pl.BlockSpec((mem_len, kv.shape[1]), lambda b, n: (b, 0)),
                   pl.BlockSpec(ln_g.shape, full2), pl.BlockSpec(ln_b.shape, full2),
                   pl.BlockSpec(w_s.shape, full3), pl.BlockSpec(b_s.shape, full2)],
        out_shape=[jax.ShapeDtypeStruct((T, zw), BF), jax.ShapeDtypeStruct(kv.shape, F32),
                   jax.ShapeDtypeStruct(ln_g.shape, F32), jax.ShapeDtypeStruct(ln_b.shape, F32),
                   jax.ShapeDtypeStruct(w_s.shape, F32), jax.ShapeDtypeStruct(b_s.shape, F32)],
        name="gmlp_bwd",
        compiler_params=pltpu.CompilerParams(dimension_semantics=("arbitrary", "arbitrary"), vmem_limit_bytes=VMEM_CAP_BYTES),
    )(z, dcat, ln_g, ln_b, w_s, b_s, kv)


def _place():
    x, y, c = lax.axis_index("x"), lax.axis_index("y"), lax.axis_index("c")
    chips = [(1 - x, y), (x, 1 - y), (1 - x, 1 - y)]
    return x, y, c, chips


def _half(ref, kind, e):
    if kind == "col":
        n = ref.shape[1] // 2
        return ref.at[:, pl.ds(pl.multiple_of(e * n, n), n), :]
    n = ref.shape[2] // 2
    return ref.at[:, :, pl.ds(pl.multiple_of(e * n, n), n)]


def _slot(ref, kind, j, n):
    if kind == "col":
        return ref.at[:, :, pl.ds(pl.multiple_of(j * n, n), n)]
    return ref.at[:, pl.ds(pl.multiple_of(j * n, n), n), :]


def _allgather_seq(name, items, cid):
    nt = len(items)
    kinds = [k for (_, k, _) in items]
    slot_kind = ["row" if k == "row" else "col" for k in kinds]
    out_type = []
    for s, k, l in items:
        L, r, c = s.shape
        lo = L if l is None else 1
        out_type.append(jax.ShapeDtypeStruct((lo, 4 * r, c) if k == "row" else (lo, r, 4 * c), s.dtype))

    def part(ref, t, e):
        return ref if kinds[t] == "vec" else _half(ref, kinds[t], e)

    def body(*refs):
        sh = [refs[t] if items[t][2] is None else refs[t].at[pl.ds(items[t][2], 1)] for t in range(nt)]
        full = refs[nt:2 * nt]
        loc, s_ici, r_ici, s_d2d, r_d2d = refs[2 * nt:]
        x, y, c, chips = _place()
        own = 2 * x + y
        sibling = (x, y, 1 - c)
        barrier = pltpu.get_barrier_semaphore()
        for peer in [(px, py, c) for (px, py) in chips] + [sibling]:
            pl.semaphore_signal(barrier, inc=1, device_id=peer, device_id_type=MESH)
        pl.semaphore_wait(barrier, 4)
        width = [sh[t].shape[1] if kinds[t] == "row" else sh[t].shape[2] for t in range(nt)]
        started = []
        for t in range(nt):
            mine = pltpu.make_async_copy(sh[t], _slot(full[t], slot_kind[t], own, width[t]), loc.at[t])
            mine.start()
            started.append(mine)
        sent = []
        for t in range(nt):
            for p, (px, py) in enumerate(chips):
                cp = pltpu.make_async_remote_copy(
                    src_ref=part(sh[t], t, c), dst_ref=part(_slot(full[t], slot_kind[t], own, width[t]), t, c),
                    send_sem=s_ici.at[t, p], recv_sem=r_ici.at[t, p], device_id=(px, py, c), device_id_type=MESH)
                cp.start()
                sent.append(cp)
        for t in range(nt):
            for p, (px, py) in enumerate(chips):
                landed = part(_slot(full[t], slot_kind[t], 2 * px + py, width[t]), t, c)
                pltpu.make_async_remote_copy(
                    src_ref=landed, dst_ref=landed, send_sem=s_ici.at[t, p], recv_sem=r_ici.at[t, p],
                    device_id=(px, py, c), device_id_type=MESH).wait_recv()
                if kinds[t] == "vec":
                    continue
                fw = pltpu.make_async_remote_copy(
                    src_ref=landed, dst_ref=landed, send_sem=s_d2d.at[t, p], recv_sem=r_d2d.at[t, p],
                    device_id=sibling, device_id_type=MESH)
                fw.start()
                sent.append(fw)
        for t in range(nt):
            if kinds[t] == "vec":
                continue
            for p, (px, py) in enumerate(chips):
                other = _half(_slot(full[t], kinds[t], 2 * px + py, width[t]), kinds[t], 1 - c)
                pltpu.make_async_remote_copy(
                    src_ref=other, dst_ref=other, send_sem=s_d2d.at[t, p], recv_sem=r_d2d.at[t, p],
                    device_id=sibling, device_id_type=MESH).wait_recv()
        for cp in sent:
            cp.wait_send()
        for cp in started:
            cp.wait()

    return pl.kernel(
        body, out_type=out_type, mesh=plsc.ScalarSubcoreMesh(axis_name="seq", num_cores=1),
        scratch_types=[pltpu.SemaphoreType.DMA((nt,)), pltpu.SemaphoreType.DMA((nt, 3)), pltpu.SemaphoreType.DMA((nt, 3)),
                       pltpu.SemaphoreType.DMA((nt, 3)), pltpu.SemaphoreType.DMA((nt, 3))],
        compiler_params=pltpu.CompilerParams(collective_id=cid), name=name,
    )(*[s for (s, _, _) in items])


def _slot2(ref, kind, j, n):
    if kind == "col":
        return ref.at[:, pl.ds(pl.multiple_of(j * n, n), n)]
    return ref.at[pl.ds(pl.multiple_of(j * n, n), n), :]


def _rs_chips_seq(name, parts, kinds, cid):
    nm = len(parts)
    out_type = []
    for g, k in zip(parts, kinds):
        r, c = g.shape
        ps = (r, c // 4) if k == "col" else (r // 4, c)
        out_type += [jax.ShapeDtypeStruct(ps, BF), jax.ShapeDtypeStruct((3,) + ps, BF)]

    def body(*refs):
        g = refs[:nm]
        outs = refs[nm:3 * nm]
        loc, ssem, rsem = refs[3 * nm:]
        x, y, c, chips = _place()
        own = 2 * x + y
        barrier = pltpu.get_barrier_semaphore()
        for (px, py) in chips:
            pl.semaphore_signal(barrier, inc=1, device_id=(px, py, c), device_id_type=MESH)
        pl.semaphore_wait(barrier, 3)
        cps = []
        for m in range(nm):
            k = kinds[m]
            own_o, got_o = outs[2 * m], outs[2 * m + 1]
            n = g[m].shape[1] // 4 if k == "col" else g[m].shape[0] // 4
            lc = pltpu.make_async_copy(_slot2(g[m], k, own, n), own_o, loc.at[m])
            lc.start()
            cps.append(lc)
            for p, (px, py) in enumerate(chips):
                cp = pltpu.make_async_remote_copy(
                    src_ref=_slot2(g[m], k, 2 * px + py, n), dst_ref=got_o.at[p],
                    send_sem=ssem.at[m, p], recv_sem=rsem.at[m, p], device_id=(px, py, c), device_id_type=MESH)
                cp.start()
                cps.append(cp)
        for cp in cps:
            cp.wait()

    return pl.kernel(
        body, out_type=out_type, mesh=plsc.ScalarSubcoreMesh(axis_name="seq", num_cores=1),
        scratch_types=[pltpu.SemaphoreType.DMA((nm,)), pltpu.SemaphoreType.DMA((nm, 3)), pltpu.SemaphoreType.DMA((nm, 3))],
        compiler_params=pltpu.CompilerParams(collective_id=cid), name=name,
    )(*parts)


def _finish_share(name, owns, gots, kind, c_arr):
    L = len(owns)
    r, c = owns[0].shape
    tr = _pick(r, 128 if kind == "col" else 256)
    nb = r // tr
    nq = L * nb

    def chunk_of(l):
        return lambda h, q: jnp.clip(q * (1 - h) + (nq - 1) * h - l * nb, 0, nb - 1)

    ins, in_specs = [], []
    for l in range(L):
        at = chunk_of(l)
        ins += [owns[l], gots[l].reshape(3 * r, c), gots[l].reshape(3 * r, c), gots[l].reshape(3 * r, c)]
        in_specs.append(pl.BlockSpec((tr, c), functools.partial(lambda h, q, cc, at: (at(h, q), 0), at=at)))
        in_specs += [pl.BlockSpec((tr, c), functools.partial(lambda h, q, cc, at, p: (p * nb + at(h, q), 0), at=at, p=p))
                     for p in range(3)]
    if kind == "col":
        out_sd = (L, 2, r, c)
        o_spec = pl.BlockSpec((None, 2, tr, c), lambda h, q, cc: ((q * h) // nb, 0, (q * h) % nb, 0))
    else:
        out_sd = (L * r, 2 * c)
        o_spec = pl.BlockSpec((tr, 2 * c), lambda h, q, cc: (q * h, 0))

    def kern(c_ref, *refs):
        in_refs = refs[:4 * L]
        out_ref, mine, recv, ssem, rsem = refs[4 * L:]
        h, q = pl.program_id(0), pl.program_id(1)
        x, y, cc, _ = _place()

        def swap(qq):
            return pltpu.make_async_remote_copy(src_ref=mine.at[qq], dst_ref=recv.at[qq], send_sem=ssem.at[qq],
                                                recv_sem=rsem.at[qq], device_id=(x, y, 1 - cc), device_id_type=MESH)

        for l in range(L):
            @pl.when(jnp.logical_and(h == 0, q // nb == l))
            def _(l=l):
                o_ref, g0, g1, g2 = in_refs[4 * l:4 * l + 4]
                mine[q] = ((o_ref[...].astype(F32) + g0[...].astype(F32)) + g1[...].astype(F32)) + g2[...].astype(F32)
                swap(q).start()

        @pl.when(h == 1)
        def _():
            swap(q).wait()
            a, b = mine[q], recv[q]
            first = c_ref[0] == 0
            lo, hi = jnp.where(first, a, b), jnp.where(first, b, a)
            if kind == "col":
                out_ref[0] = lo
                out_ref[1] = hi
            else:
                out_ref[:, :c] = lo
                out_ref[:, c:] = hi

    full = pl.pallas_call(
        kern,
        grid_spec=pltpu.PrefetchScalarGridSpec(
            num_scalar_prefetch=1, grid=(2, nq), in_specs=in_specs, out_specs=o_spec,
            scratch_shapes=[pltpu.VMEM((nq, tr, c), F32), pltpu.VMEM((nq, tr, c), F32),
                            pltpu.SemaphoreType.DMA((nq,)), pltpu.SemaphoreType.DMA((nq,))]),
        out_shape=jax.ShapeDtypeStruct(out_sd, F32), name=name,
        compiler_params=pltpu.CompilerParams(dimension_semantics=("arbitrary", "arbitrary"),
                                             vmem_limit_bytes=VMEM_CAP_BYTES),
    )(c_arr, *ins)
    return full.reshape(L, 2 * r, c) if kind == "col" else full.reshape(L, r, 2 * c)


def _small_allreduce(buf, name):
    R = buf.shape[0]
    assert R % 16 == 0
    h = R // 2

    def body(x_ref, o_ref, sib, csum, got, s_a, r_a, s_b, r_b, s_c, r_c):
        x, y, c, chips = _place()
        sibling = (x, y, 1 - c)
        own = 2 * x + y
        swap = pltpu.make_async_remote_copy(src_ref=x_ref, dst_ref=sib, send_sem=s_a, recv_sem=r_a,
                                            device_id=sibling, device_id_type=MESH)
        swap.start()
        swap.wait()
        a, b = x_ref[...], sib[...]
        south = c == 0
        csum[...] = jnp.where(south, a, b) + jnp.where(south, b, a)
        lo = pl.multiple_of(c * h, 8)
        mine = csum.at[pl.ds(lo, h)]
        got[own] = csum[pl.ds(lo, h)]
        sends = []
        for p, (px, py) in enumerate(chips):
            cp = pltpu.make_async_remote_copy(src_ref=mine, dst_ref=got.at[own], send_sem=s_b.at[p], recv_sem=r_b.at[p],
                                              device_id=(px, py, c), device_id_type=MESH)
            cp.start()
            sends.append(cp)
        for cp in sends:
            cp.wait()
        o_ref[pl.ds(lo, h)] = ((got[0] + got[1]) + got[2]) + got[3]
        done = o_ref.at[pl.ds(lo, h)]
        back = pltpu.make_async_remote_copy(src_ref=done, dst_ref=done, send_sem=s_c, recv_sem=r_c,
                                            device_id=sibling, device_id_type=MESH)
        back.start()
        back.wait_send()
        other = o_ref.at[pl.ds(pl.multiple_of((1 - c) * h, 8), h)]
        pltpu.make_async_remote_copy(src_ref=other, dst_ref=other, send_sem=s_c, recv_sem=r_c,
                                     device_id=sibling, device_id_type=MESH).wait_recv()

    vm = pl.BlockSpec(memory_space=pltpu.VMEM)
    return pl.pallas_call(
        body, out_shape=jax.ShapeDtypeStruct(buf.shape, F32), in_specs=[vm], out_specs=vm,
        scratch_shapes=[pltpu.VMEM((R, LANES), F32), pltpu.VMEM((R, LANES), F32), pltpu.VMEM((4, h, LANES), F32),
                        pltpu.SemaphoreType.DMA, pltpu.SemaphoreType.DMA, pltpu.SemaphoreType.DMA((3,)),
                        pltpu.SemaphoreType.DMA((3,)), pltpu.SemaphoreType.DMA, pltpu.SemaphoreType.DMA],
        name=name,
        compiler_params=pltpu.CompilerParams(vmem_limit_bytes=VMEM_CAP_BYTES),
    )(buf)


PACK_TILE_ROWS = 8


def _item_rows(shape):
    n = 1
    for d in shape:
        n *= d
    return -(-n // (PACK_TILE_ROWS * LANES)) * PACK_TILE_ROWS


def _pack(arrs, rows_total):
    buf = jnp.zeros((rows_total, LANES), F32)
    r = 0
    for a in arrs:
        f = a.reshape(-1).astype(F32)
        nr = _item_rows(a.shape)
        block = jnp.pad(f, (0, nr * LANES - f.shape[0])).reshape(nr, LANES)
        buf = lax.dynamic_update_slice(buf, block, (r, 0))
        r += nr
    return buf


def _unpack(buf, shapes):
    out, r = [], 0
    for s in shapes:
        n = 1
        for d in s:
            n *= d
        nr = _item_rows(s)
        out.append(buf[r:r + nr].reshape(-1)[:n].reshape(s))
        r += nr
    return out


def _rows_needed(shapes):
    return -(-sum(_item_rows(s) for s in shapes) // (2 * PACK_TILE_ROWS)) * (2 * PACK_TILE_ROWS)


def _two_rows(a, b):
    out = jnp.zeros((2, a.shape[1]), a.dtype)
    return lax.dynamic_update_slice(lax.dynamic_update_slice(out, a, (0, 0)), b, (1, 0))


def _adam(w, g, m, v):
    m = ADAM_B1 * m + (1.0 - ADAM_B1) * g
    v = ADAM_B2 * v + (1.0 - ADAM_B2) * jnp.square(g)
    m_hat = m / (1.0 - ADAM_B1 ** ADAM_STEP)
    v_hat = v / (1.0 - ADAM_B2 ** ADAM_STEP)
    delta = -ADAM_LR * (m_hat / (jnp.sqrt(v_hat) + ADAM_EPS) + ADAM_WD * w)
    return delta, m, v


def _adam_call(name, w2, g2, m2, v2, tr, pass_grad=False):
    def fn(rv, cv):
        outs = list(_adam(*rv))
        return ([rv[1]] + outs if pass_grad else outs), []

    width = w2.shape[1]
    return _rowcall(name, fn, [(w2, 0, width), (g2, 0, width), (m2, 0, width), (v2, 0, width)], [],
                    [(width, F32)] * (4 if pass_grad else 3), [], tr)


def kernel(x, mem, mem_norm, lb_logits, ffn1_norm, ffn1_w_in, ffn1_w_out, mix_norm, mem_w_kv, hgrn_w_in, hgrn_gnorm, hgrn_w_out, gmlp_w_in, gmlp_ln_g, gmlp_ln_b, gmlp_w_s, gmlp_b_s, gmlp_w_out, ffn2_norm, ffn2_w_in, ffn2_w_out, final_norm, loss_target, m_mem_norm, m_lb_logits, m_ffn1_norm, m_ffn1_w_in, m_ffn1_w_out, m_mix_norm, m_mem_w_kv, m_hgrn_w_in, m_hgrn_gnorm, m_hgrn_w_out, m_gmlp_w_in, m_gmlp_ln_g, m_gmlp_ln_b, m_gmlp_w_s, m_gmlp_b_s, m_gmlp_w_out, m_ffn2_norm, m_ffn2_w_in, m_ffn2_w_out, m_final_norm, v_mem_norm, v_lb_logits, v_ffn1_norm, v_ffn1_w_in, v_ffn1_w_out, v_mix_norm, v_mem_w_kv, v_hgrn_w_in, v_hgrn_gnorm, v_hgrn_w_out, v_gmlp_w_in, v_gmlp_ln_g, v_gmlp_ln_b, v_gmlp_w_s, v_gmlp_b_s, v_gmlp_w_out, v_ffn2_norm, v_ffn2_w_in, v_ffn2_w_out, v_final_norm):
    bl, seq, D = x.shape
    T = bl * seq
    mem_len = mem.shape[1]
    chip = 2 * lax.axis_index("x") + lax.axis_index("y")
    c_arr = lax.axis_index("c").astype(jnp.int32).reshape(1)
    TR = 1024

    big = [("ffn1_w_in", ffn1_w_in, "col"), ("ffn1_w_out", ffn1_w_out, "row"), ("mem_w_kv", mem_w_kv, "col"),
           ("hgrn_w_in", hgrn_w_in, "col"), ("hgrn_w_out", hgrn_w_out, "row"), ("gmlp_w_in", gmlp_w_in, "col"),
           ("gmlp_w_out", gmlp_w_out, "row"), ("ffn2_w_in", ffn2_w_in, "col"), ("ffn2_w_out", ffn2_w_out, "row")]
    kinds = [k for (_, _, k) in big]
    shards_bf = []
    for nm, w, _ in big:
        L, r, c = w.shape
        (wb,) = _rowcall("cast_" + nm, lambda rv, cv: ([rv[0]], []), [(w.reshape(L * r, c), 0, c)], [], [(c, BF)], [], 512)
        shards_bf.append(wb.reshape(L, r, c))
    sb = dict(zip([nm for (nm, _, _) in big], shards_bf))
    groups = [[("ffn1_w_in", 0)], [("ffn1_w_out", 0)], [("hgrn_w_in", None)], [("mem_w_kv", None)], [("hgrn_w_out", None)],
              [("ffn2_w_in", 0), ("ffn2_w_out", 0), ("gmlp_ln_g", None), ("gmlp_ln_b", None)],
              [("ffn1_w_in", 1), ("ffn1_w_out", 1)],
              [("gmlp_w_in", None), ("gmlp_w_out", None)],
              [("ffn2_w_in", 1), ("ffn2_w_out", 1)]]
    kind_of = {nm: k for (nm, _, k) in big}
    for nm, vec in (("gmlp_ln_g", gmlp_ln_g), ("gmlp_ln_b", gmlp_ln_b)):
        sb[nm] = vec.reshape(1, 1, -1)
        kind_of[nm] = "vec"
    gathered = {nm: [None, None] for nm in ("ffn1_w_in", "ffn1_w_out", "ffn2_w_in", "ffn2_w_out")}
    for gi, grp in enumerate(groups):
        outs = _allgather_seq("gather_%d" % gi, [(sb[nm], kind_of[nm], l) for (nm, l) in grp], gi)
        for (nm, l), o in zip(grp, outs):
            if l is None:
                gathered[nm] = o
            else:
                gathered[nm][l] = o

    ln_w = GM_GROUPS * GM_GROUP_DIM
    ln_g_full, ln_b_full = gathered["gmlp_ln_g"].reshape(1, ln_w), gathered["gmlp_ln_b"].reshape(1, ln_w)

    def rms_fwd(name, xin, g):
        (h,) = _rowcall(name, lambda rv, cv: ([_rmsnorm(rv[0], cv[0])], []), [(xin, 0, D)], [g.reshape(1, D)], [(D, BF)], [], TR)
        return h

    def ffn_fwd(tag, xin, h, w_in, w_out, layer, next_gain):
        dff = w_out[layer].shape[1]
        zg, zu, a = _ffn_in_swiglu("ffn_in_" + tag, h, w_in[layer], 1024, dff // 2)
        out = _mm("ffn_out_" + tag, a, w_out[layer], "nn", F32, 1024, 1024, dff, scale=0.5, res=xin, b_lead=0,
                  norm_gain=None if next_gain is None else next_gain.reshape(1, D))
        xo, h_next = (out, None) if next_gain is None else out
        return xo, h_next, (xin, h, zg, zu, a)

    def ffn_bwd(tag, dxo, saved, g, w_in, w_out, layer):
        xin, h, zg, zu, a = saved
        dff = w_out[layer].shape[1]
        dw_out = _mm_tn_pair("ffn_dwo_" + tag, a, dxo, "row", c_arr, dff // 2, T, scale=0.5)
        dz = _ffn_da_swiglu("ffn_da_" + tag, dxo, w_out[layer], zg, zu, 512)
        dw_in = _mm_tn_pair("ffn_dwi_" + tag, h, dz, "col", c_arr, 512, T)
        dx, dg = _mm_dh_rms("ffn_dh_" + tag, dz, w_in[layer], xin, g.reshape(1, D), dxo, 512)
        return dx, dg, dw_in, dw_out

    def rms_bwd(name, xin, g, dh, dres):
        def fn(rv, cv):
            _, vjp = jax.vjp(_rmsnorm, rv[0], cv[0])
            dx, dg = vjp(rv[1])
            if dres is not None:
                dx = dx + rv[2]
            return [dx], [dg]

        rows = [(xin, 0, D), (dh, 0, D)] + ([(dres, 0, D)] if dres is not None else [])
        dx, dg = _rowcall(name, fn, rows, [g.reshape(1, D)], [(D, F32)], [((1, D), F32)], TR)
        return dx, dg

    x0 = x.reshape(T, D)
    tgt = loss_target.reshape(T, D)
    mem2 = mem.reshape(bl * mem_len, D)
    memn = rms_fwd("rms_mem", mem2, mem_norm)

    h_f10 = rms_fwd("rms_f1l0", x0, ffn1_norm[0])
    x1, h_m0, sv_f10 = ffn_fwd("f1l0", x0, h_f10, gathered["ffn1_w_in"], gathered["ffn1_w_out"], 0, mix_norm[0])
    z_m0 = _mm("mix_in_0", h_m0, gathered["hgrn_w_in"], "nn", F32, 2048, 512, D, b_lead=0)
    kv = [_mm("kv_%d" % i, memn, gathered["mem_w_kv"], "nn", F32, 512, 512, D, b_lead=i) for i in range(2)]
    cat0, stash0 = _hgrn_fwd2(z_m0, lb_logits, hgrn_gnorm, kv[0], bl, seq)
    x2, h_f20 = _mm("mix_out_0", cat0, gathered["hgrn_w_out"], "nn", F32, 1024, 1024, cat0.shape[1], res=x1, b_lead=0,
                    norm_gain=ffn2_norm[0].reshape(1, D))
    x3, h_f11, sv_f20 = ffn_fwd("f2l0", x2, h_f20, gathered["ffn2_w_in"], gathered["ffn2_w_out"], 0, ffn1_norm[1])
    x4, h_m1, sv_f11 = ffn_fwd("f1l1", x3, h_f11, gathered["ffn1_w_in"], gathered["ffn1_w_out"], 1, mix_norm[1])
    z_m1 = _mm("mix_in_1", h_m1, gathered["gmlp_w_in"], "nn", F32, 2048, 512, D, b_lead=0)
    nc1 = seq // GM_CHUNK
    w_s, b_s = gmlp_w_s[0], gmlp_b_s[0]
    cat1 = _gmlp_fwd(z_m1, ln_g_full, ln_b_full, w_s, b_s, kv[1], bl, nc1)
    x5, h_f21 = _mm("mix_out_1", cat1, gathered["gmlp_w_out"], "nn", F32, 1024, 1024, cat1.shape[1], res=x4, b_lead=0,
                    norm_gain=ffn2_norm[1].reshape(1, D))
    x6, _, sv_f21 = ffn_fwd("f2l1", x5, h_f21, gathered["ffn2_w_in"], gathered["ffn2_w_out"], 1, None)

    def head(rv, cv):
        def f(xx, gg):
            err = _rmsnorm(xx, gg) - rv[1]
            return 0.5 * jnp.sum(jnp.mean(err * err, axis=-1, keepdims=True), axis=0, keepdims=True)

        ls, vjp = jax.vjp(f, rv[0], cv[0])
        dx, dg = vjp(jnp.ones((1, 1), F32))
        return [dx], [dg, jnp.broadcast_to(ls, (1, 128))]

    dx6, d_final, loss_part = _rowcall("loss_head", head, [(x6, 0, D), (tgt, 0, D)], [final_norm.reshape(1, D)],
                                       [(D, F32)], [((1, D), F32), ((1, 128), F32)], TR)

    rs_out = {}
    n_gather = len(groups)

    def rs(gi, items):
        outs = _rs_chips_seq("reduce_%d" % gi, [p for (_, p, _) in items], [k for (_, _, k) in items], n_gather + gi)
        for i, (key, _, _) in enumerate(items):
            rs_out[key] = (outs[2 * i], outs[2 * i + 1])

    dx5, dg_f21, dwi_f21, dwo_f21 = ffn_bwd("f2l1", dx6, sv_f21, ffn2_norm[1], gathered["ffn2_w_in"], gathered["ffn2_w_out"], 1)
    rs(0, [(("ffn2_w_out", 1), dwo_f21, "row"), (("ffn2_w_in", 1), dwi_f21, "col")])
    dcat1 = _mm("mix_dcat_1", dx5, gathered["gmlp_w_out"], "nt", F32, 2048, 1024, D, b_lead=0)
    dwo_m1 = _mm_tn_pair("mix_dwo_1", cat1, dx5, "row", c_arr, 1024, T)
    dz_m1, dkv1, d_lng, d_lnb, d_ws, d_bs = _gmlp_bwd(z_m1, dcat1, ln_g_full, ln_b_full, w_s, b_s, kv[1], bl, nc1)
    dx4, dg_m1 = _mm_dh_rms("mix_dh_1", dz_m1, gathered["gmlp_w_in"], x4, mix_norm[1].reshape(1, D), dx5, 512)
    dwi_m1 = _mm_tn_pair("mix_dwi_1", h_m1, dz_m1, "col", c_arr, 1024, T)
    rs(1, [(("gmlp_w_out", 0), dwo_m1, "row"), (("gmlp_w_in", 0), dwi_m1, "col")])
    dx3, dg_f11, dwi_f11, dwo_f11 = ffn_bwd("f1l1", dx4, sv_f11, ffn1_norm[1], gathered["ffn1_w_in"], gathered["ffn1_w_out"], 1)
    rs(2, [(("ffn1_w_out", 1), dwo_f11, "row"), (("ffn1_w_in", 1), dwi_f11, "col")])

    dx2, dg_f20, dwi_f20, dwo_f20 = ffn_bwd("f2l0", dx3, sv_f20, ffn2_norm[0], gathered["ffn2_w_in"], gathered["ffn2_w_out"], 0)
    rs(3, [(("ffn2_w_out", 0), dwo_f20, "row"), (("ffn2_w_in", 0), dwi_f20, "col")])
    dcat0 = _mm("mix_dcat_0", dx2, gathered["hgrn_w_out"], "nt", F32, 2048, 1024, D, b_lead=0)
    dwo_m0 = _mm_tn_pair("mix_dwo_0", cat0, dx2, "row", c_arr, 1024, T)
    dz_m0, dkv0, d_lb, d_gn = _hgrn_bwd2(z_m0, dcat0, stash0, lb_logits, hgrn_gnorm, kv[0], bl, seq)
    dx1, dg_m0 = _mm_dh_rms("mix_dh_0", dz_m0, gathered["hgrn_w_in"], x1, mix_norm[0].reshape(1, D), dx2, 512)
    dwi_m0 = _mm_tn_pair("mix_dwi_0", h_m0, dz_m0, "col", c_arr, 1024, T)
    rs(4, [(("hgrn_w_out", 0), dwo_m0, "row"), (("hgrn_w_in", 0), dwi_m0, "col")])

    dwkv = [_mm_tn_pair("kv_dw_%d" % i, memn, dkv, "col", c_arr, 1024, 512) for i, dkv in enumerate([dkv0, dkv1])]
    rs(5, [(("mem_w_kv", 0), dwkv[0], "col"), (("mem_w_kv", 1), dwkv[1], "col")])
    dmemn = _mm("kv_dx_0", dkv0, gathered["mem_w_kv"], "nt", F32, 512, 512, 1024, b_lead=0)
    dmemn = _mm("kv_dx_1", dkv1, gathered["mem_w_kv"], "nt", F32, 512, 512, 1024, res=dmemn, b_lead=1)
    _, d_memnorm = rms_bwd("rms_bwd_mem", mem2, mem_norm, dmemn, None)

    dx0, dg_f10, dwi_f10, dwo_f10 = ffn_bwd("f1l0", dx1, sv_f10, ffn1_norm[0], gathered["ffn1_w_in"], gathered["ffn1_w_out"], 0)
    rs(6, [(("ffn1_w_out", 0), dwo_f10, "row")])
    rs(7, [(("ffn1_w_in", 0), dwi_f10, "col")])

    shard_grads = [_finish_share("finish_" + nm, [rs_out[(nm, l)][0] for l in range(w.shape[0])],
                                 [rs_out[(nm, l)][1] for l in range(w.shape[0])], k, c_arr) for (nm, w, k) in big]

    big_w = [w for (_, w, _) in big]
    big_m = [m_ffn1_w_in, m_ffn1_w_out, m_mem_w_kv, m_hgrn_w_in, m_hgrn_w_out, m_gmlp_w_in, m_gmlp_w_out, m_ffn2_w_in, m_ffn2_w_out]
    big_v = [v_ffn1_w_in, v_ffn1_w_out, v_mem_w_kv, v_hgrn_w_in, v_hgrn_w_out, v_gmlp_w_in, v_gmlp_w_out, v_ffn2_w_in, v_ffn2_w_out]
    big_out = {}
    for (nm, w, _), g, m, v in zip(big, shard_grads, big_m, big_v):
        L, r, c = w.shape
        g2, d2, m2, v2 = _adam_call("adam_" + nm, w.reshape(L * r, c), g.reshape(L * r, c), m.reshape(L * r, c),
                                    v.reshape(L * r, c), 256, pass_grad=True)
        big_out[nm] = (g2.reshape(w.shape), d2.reshape(w.shape), m2.reshape(w.shape), v2.reshape(w.shape))

    d_ffn1n = _two_rows(dg_f10, dg_f11)
    d_mixn = _two_rows(dg_m0, dg_m1)
    d_ffn2n = _two_rows(dg_f20, dg_f21)
    small_parts = [loss_part[:, :1], d_memnorm, d_lb, d_ffn1n, d_mixn, d_gn, d_lng, d_lnb, d_ws, d_bs, d_ffn2n, d_final]
    red_shapes = [(1,), mem_norm.shape, lb_logits.shape, ffn1_norm.shape, mix_norm.shape, hgrn_gnorm.shape, (1, ln_w), (1, ln_w),
                  gmlp_w_s.shape, gmlp_b_s.shape, ffn2_norm.shape, final_norm.shape]
    red = _small_allreduce(_pack(small_parts, _rows_needed(red_shapes)), "reduce_small")
    (loss_v, g_memn, g_lb, g_f1n, g_mixn, g_gn, g_lng_full, g_lnb_full, g_ws, g_bs, g_f2n, g_fin) = _unpack(red, red_shapes)
    lsh = gmlp_ln_g.shape[1]
    g_lng = lax.dynamic_slice(g_lng_full, (0, chip * lsh), (1, lsh))
    g_lnb = lax.dynamic_slice(g_lnb_full, (0, chip * lsh), (1, lsh))
    small_w = [mem_norm, lb_logits, ffn1_norm, mix_norm, hgrn_gnorm, gmlp_ln_g, gmlp_ln_b, gmlp_w_s, gmlp_b_s, ffn2_norm, final_norm]
    small_g = [g_memn, g_lb, g_f1n, g_mixn, g_gn, g_lng, g_lnb, g_ws, g_bs, g_f2n, g_fin]
    small_m = [m_mem_norm, m_lb_logits, m_ffn1_norm, m_mix_norm, m_hgrn_gnorm, m_gmlp_ln_g, m_gmlp_ln_b, m_gmlp_w_s, m_gmlp_b_s, m_ffn2_norm, m_final_norm]
    small_v = [v_mem_norm, v_lb_logits, v_ffn1_norm, v_mix_norm, v_hgrn_gnorm, v_gmlp_ln_g, v_gmlp_ln_b, v_gmlp_w_s, v_gmlp_b_s, v_ffn2_norm, v_final_norm]
    sshapes = [w.shape for w in small_w]
    nrow = _rows_needed(sshapes)
    d_p, m_p, v_p = _adam_call("adam_small", _pack(small_w, nrow), _pack(small_g, nrow), _pack(small_m, nrow), _pack(small_v, nrow), nrow)
    s_delta, s_m, s_v = _unpack(d_p, sshapes), _unpack(m_p, sshapes), _unpack(v_p, sshapes)
    small_names = ["mem_norm", "lb_logits", "ffn1_norm", "mix_norm", "hgrn_gnorm", "gmlp_ln_g", "gmlp_ln_b", "gmlp_w_s", "gmlp_b_s", "ffn2_norm", "final_norm"]
    small_out = {nm: (g.reshape(w.shape), d, m, v) for nm, w, g, d, m, v in zip(small_names, small_w, small_g, s_delta, s_m, s_v)}

    order = ["mem_norm", "lb_logits", "ffn1_norm", "ffn1_w_in", "ffn1_w_out", "mix_norm", "mem_w_kv", "hgrn_w_in", "hgrn_gnorm",
             "hgrn_w_out", "gmlp_w_in", "gmlp_ln_g", "gmlp_ln_b", "gmlp_w_s", "gmlp_b_s", "gmlp_w_out", "ffn2_norm", "ffn2_w_in",
             "ffn2_w_out", "final_norm"]
    allo = {**big_out, **small_out}
    grad_x = dx0.reshape(x.shape)
    return (loss_v.reshape(()), grad_x, *[allo[n][0] for n in order], *[allo[n][1] for n in order],
            *[allo[n][2] for n in order], *[allo[n][3] for n in order])
```

```python
import functools

import jax
import jax.numpy as jnp
from jax import lax
from jax.experimental import pallas as pl
from jax.experimental.pallas import tpu as pltpu
from jax.experimental.pallas import tpu_sc as plsc

BF = jnp.bfloat16
F32 = jnp.float32
MESH = pl.DeviceIdType.MESH

EPS = 1e-6
D_MODEL = 1024
HG_HEADS = 8
HG_DIM = 128
HG_CHUNK = 64
GM_CHUNK = 128
GM_GROUPS = 8
GM_GROUP_DIM = 256
XA_HEADS = 4
XA_DIM = 256
ADAM_LR = 0.001
ADAM_B1 = 0.9
ADAM_B2 = 0.999
ADAM_EPS = 1e-08
ADAM_WD = 0.01
ADAM_STEP = 10

VMEM_CAP_BYTES = 60 * 1024 * 1024
LANES = 1024


def _pick(n, cap, mult=16):
    if n <= cap:
        return n
    for d in range(cap - cap % mult, 0, -mult):
        if n % d == 0:
            return d
    raise ValueError((n, cap, mult))


def _dg(a, b, ca, cb):
    return lax.dot_general(a.astype(BF), b.astype(BF), (((ca,), (cb,)), ((), ())), preferred_element_type=F32)


@jax.custom_vjp
def dot_nn(a, b):
    return _dg(a, b, 1, 0)


def _nn_fwd(a, b):
    return _dg(a, b, 1, 0), (a, b)


def _nn_bwd(r, g):
    a, b = r
    return _dg(g, b, 1, 1), _dg(a, g, 0, 0)


dot_nn.defvjp(_nn_fwd, _nn_bwd)


@jax.custom_vjp
def dot_nt(a, b):
    return _dg(a, b, 1, 1)


def _nt_fwd(a, b):
    return _dg(a, b, 1, 1), (a, b)


def _nt_bwd(r, g):
    a, b = r
    return _dg(g, b, 1, 0), _dg(g, a, 0, 0)


dot_nt.defvjp(_nt_fwd, _nt_bwd)


@jax.custom_vjp
def dot_tn(a, b):
    return _dg(a, b, 0, 0)


def _tn_fwd(a, b):
    return _dg(a, b, 0, 0), (a, b)


def _tn_bwd(r, g):
    a, b = r
    return _dg(b, g, 1, 1), _dg(a, g, 1, 0)


dot_tn.defvjp(_tn_fwd, _tn_bwd)


def _rmsnorm(x, g):
    return x * lax.rsqrt(jnp.mean(x * x, axis=-1, keepdims=True) + EPS) * g


def _silu(x):
    return x * jax.nn.sigmoid(x)


@jax.custom_vjp
def _gelu(x):
    return 0.5 * x * (1.0 + lax.erf(x * (0.5 ** 0.5)))


def _gelu_fwd(x):
    return _gelu(x), x


def _gelu_bwd(x, g):
    t = x * (0.5 ** 0.5)
    cdf = 0.5 * (1.0 + lax.erf(t))
    return (g * (cdf + x * (jnp.exp(-(t * t)) * (0.5 / 3.141592653589793) ** 0.5)),)


_gelu.defvjp(_gelu_fwd, _gelu_bwd)


def _softmax_last(s):
    m = lax.stop_gradient(jnp.max(s, axis=-1, keepdims=True))
    e = jnp.exp(s - m)
    return e / jnp.sum(e, axis=-1, keepdims=True)


def _tril(n):
    r = lax.broadcasted_iota(jnp.int32, (n, n), 0)
    c = lax.broadcasted_iota(jnp.int32, (n, n), 1)
    return r >= c


def _cumsum_rows(l):
    n = l.shape[0]
    return lax.dot_general(_tril(n).astype(F32), l, (((1,), (0,)), ((), ())),
                           precision=lax.Precision.HIGHEST, preferred_element_type=F32)


def _attention(zx, mk, mv):
    s = dot_nt(zx, mk) * (XA_DIM ** -0.5)
    return dot_nn(_softmax_last(s), mv)


def _hgrn_head(zq, zf, zi, zg, l0, l1, l2, gn, S):
    m = lax.stop_gradient(jnp.maximum(jnp.maximum(l0, l1), l2))
    e0 = jnp.exp(l0 - m)
    lb = e0 / (e0 + jnp.exp(l1 - m) + jnp.exp(l2 - m))
    q = _silu(zq)
    f = lb + (1.0 - lb) * jax.nn.sigmoid(zf)
    k = 1.0 - f
    b = _cumsum_rows(jnp.log(f))
    b_last = b[HG_CHUNK - 1:HG_CHUNK, :]
    q_dec = q * jnp.exp(b)
    k_inv = k * jnp.exp(-b)
    a = jnp.where(_tril(HG_CHUNK), dot_nt(q_dec, k_inv), 0.0)
    o = dot_nn(a, zi) + dot_nn(q_dec, S)
    S_new = jnp.exp(b_last).reshape(HG_DIM, 1) * S + dot_tn(k * jnp.exp(b_last - b), zi)
    o = _rmsnorm(o, gn) * _silu(zg)
    return o, S_new


def _gmlp_block(zu, zv, zx, lng, lnb, ws, bs, mk, mv):
    gv = [_gelu(v) for v in zv]
    width = GM_GROUPS * GM_GROUP_DIM
    mu = sum(jnp.sum(g, axis=-1, keepdims=True) for g in gv) / width
    xc = [g - mu for g in gv]
    var = sum(jnp.sum(c * c, axis=-1, keepdims=True) for c in xc) / width
    r = lax.rsqrt(var + EPS)
    outs = []
    for g in range(GM_GROUPS):
        v = xc[g] * r * lng[g] + lnb[g]
        w = jnp.where(_tril(GM_CHUNK), ws[g], 0.0)
        mixed = dot_nn(w, v) + bs[g].reshape(GM_CHUNK, 1)
        outs.append(_gelu(zu[g]) * mixed)
    for a in range(XA_HEADS):
        outs.append(_attention(zx[a], mk[a], mv[a]))
    return outs


def _rowcall(name, fn, rows, consts, row_outs, acc_outs, tr):
    nrows = rows[0][0].shape[0]
    tr = _pick(nrows, tr)
    n_r, n_c, n_ro, n_ao = len(rows), len(consts), len(row_outs), len(acc_outs)

    def kern(*refs):
        rv = [r[...] for r in refs[:n_r]]
        cv = [r[...] for r in refs[n_r:n_r + n_c]]
        ro_refs = refs[n_r + n_c:n_r + n_c + n_ro]
        ao_refs = refs[n_r + n_c + n_ro:]
        ro, ao = fn(rv, cv)
        for ref, v in zip(ro_refs, ro):
            ref[...] = v.astype(ref.dtype)
        if n_ao:
            @pl.when(pl.program_id(0) == 0)
            def _():
                for ref in ao_refs:
                    ref[...] = jnp.zeros(ref.shape, ref.dtype)

            for ref, v in zip(ao_refs, ao):
                ref[...] += v.astype(ref.dtype)

    in_specs = [pl.BlockSpec((tr, w), functools.partial(lambda i, cb: (i, cb), cb=cb)) for (_, cb, w) in rows]
    in_specs += [pl.BlockSpec(c.shape, lambda i: (0, 0)) for c in consts]
    out_specs = [pl.BlockSpec((tr, w), lambda i: (i, 0)) for (w, _) in row_outs]
    out_specs += [pl.BlockSpec(s, lambda i: (0, 0)) for (s, _) in acc_outs]
    out_shape = [jax.ShapeDtypeStruct((nrows, w), dt) for (w, dt) in row_outs]
    out_shape += [jax.ShapeDtypeStruct(s, dt) for (s, dt) in acc_outs]
    outs = pl.pallas_call(
        kern, grid=(nrows // tr,), in_specs=in_specs, out_specs=out_specs, out_shape=out_shape, name=name,
        compiler_params=pltpu.CompilerParams(dimension_semantics=("arbitrary",),
                                             vmem_limit_bytes=VMEM_CAP_BYTES),
    )(*[a for (a, _, _) in rows], *consts)
    return outs


def _mm(name, a, b, mode, out_dtype, tm, tn, tk, scale=1.0, res=None, a_lead=None, b_lead=None, norm_gain=None):
    ash = a.shape[-2:]
    bsh = b.shape[-2:]
    if mode == "nn":
        (M, K), (K2, N) = ash, bsh
    elif mode == "nt":
        (M, K), (N, K2) = ash, bsh
    else:
        (K, M), (K2, N) = ash, bsh
    assert K == K2, (name, a.shape, b.shape)
    tm, tn, tk = min(tm, M), min(tn, N), min(tk, K)
    assert M % tm == 0 and N % tn == 0 and K % tk == 0, (name, M, N, K, tm, tn, tk)
    nk = K // tk
    dims = {"nn": (1, 0), "nt": (1, 1), "tn": (0, 0)}[mode]

    def lead(spec_shape, index_fn, lead_idx):
        if lead_idx is None:
            return pl.BlockSpec(spec_shape, index_fn)
        return pl.BlockSpec((None,) + spec_shape, lambda i, j, k: (lead_idx,) + index_fn(i, j, k))

    if mode == "tn":
        a_spec = lead((tk, tm), lambda i, j, k: (k, i), a_lead)
    else:
        a_spec = lead((tm, tk), lambda i, j, k: (i, k), a_lead)
    if mode == "nt":
        b_spec = lead((tn, tk), lambda i, j, k: (j, k), b_lead)
    else:
        b_spec = lead((tk, tn), lambda i, j, k: (k, j), b_lead)
    o_spec = pl.BlockSpec((tm, tn), lambda i, j, k: (i, j))
    has_res = res is not None
    has_norm = norm_gain is not None
    assert not has_norm or tn == N

    def kern(*refs):
        a_ref, b_ref = refs[0], refs[1]
        pos = 2
        res_ref = gain_ref = h_ref = None
        if has_res:
            res_ref, pos = refs[pos], pos + 1
        if has_norm:
            gain_ref, pos = refs[pos], pos + 1
        o_ref, pos = refs[pos], pos + 1
        if has_norm:
            h_ref = refs[pos]
        acc_ref = refs[-1] if nk > 1 else None
        p = lax.dot_general(a_ref[...].astype(BF), b_ref[...].astype(BF), (((dims[0],), (dims[1],)), ((), ())),
                            preferred_element_type=F32)

        def finish(v):
            if scale != 1.0:
                v = v * scale
            if has_res:
                v = res_ref[...] + v
            o_ref[...] = v.astype(o_ref.dtype)
            if has_norm:
                h_ref[...] = _rmsnorm(v, gain_ref[...]).astype(h_ref.dtype)

        if nk == 1:
            finish(p)
        else:
            k = pl.program_id(2)

            @pl.when(k == 0)
            def _():
                acc_ref[...] = p

            @pl.when(k > 0)
            def _():
                acc_ref[...] += p

            @pl.when(k == nk - 1)
            def _():
                finish(acc_ref[...])

    ins = [a, b] + ([res] if has_res else []) + ([norm_gain] if has_norm else [])
    in_specs = [a_spec, b_spec] + ([o_spec] if has_res else [])
    in_specs += [pl.BlockSpec((1, N), lambda i, j, k: (0, 0))] if has_norm else []
    out_sd = jax.ShapeDtypeStruct((M, N), out_dtype)
    return pl.pallas_call(
        kern, grid=(M // tm, N // tn, nk), in_specs=in_specs,
        out_specs=[o_spec, o_spec] if has_norm else o_spec,
        out_shape=[out_sd, jax.ShapeDtypeStruct((M, N), BF)] if has_norm else out_sd,
        scratch_shapes=[pltpu.VMEM((tm, tn), F32)] if nk > 1 else [],
        name=name,
        compiler_params=pltpu.CompilerParams(dimension_semantics=("parallel", "parallel", "arbitrary"),
                                             vmem_limit_bytes=VMEM_CAP_BYTES),
    )(*ins)


def _ffn_in_swiglu(name, h, w3, tm, tn):
    T, D = h.shape
    dff = w3.shape[2] // 2
    tm = min(tm, T)
    assert T % tm == 0 and dff % tn == 0
    nj = dff // tn

    def kern(h_ref, wg_ref, wu_ref, zg_ref, zu_ref, a_ref):
        hb = h_ref[...]
        g = jnp.dot(hb, wg_ref[...], preferred_element_type=F32).astype(BF)
        u = jnp.dot(hb, wu_ref[...], preferred_element_type=F32).astype(BF)
        zg_ref[...] = g
        zu_ref[...] = u
        a_ref[...] = (_silu(g.astype(F32)) * u.astype(F32)).astype(BF)

    o_spec = pl.BlockSpec((tm, tn), lambda i, j: (i, j))
    return pl.pallas_call(
        kern, grid=(T // tm, nj),
        in_specs=[pl.BlockSpec((tm, D), lambda i, j: (i, 0)),
                  pl.BlockSpec((None, D, tn), lambda i, j: (0, 0, j)),
                  pl.BlockSpec((None, D, tn), lambda i, j: (0, 0, j + nj))],
        out_specs=[o_spec, o_spec, o_spec],
        out_shape=[jax.ShapeDtypeStruct((T, dff), BF)] * 3, name=name,
        compiler_params=pltpu.CompilerParams(dimension_semantics=("parallel", "arbitrary"),
                                             vmem_limit_bytes=VMEM_CAP_BYTES),
    )(h, w3, w3)


def _ffn_da_swiglu(name, dxo, w3, zg, zu, tm):
    T, D = dxo.shape
    dff = w3.shape[1]
    tm = min(tm, T)
    assert T % tm == 0 and dff % 2 == 0
    hc = dff // 2

    def kern(d_ref, w_ref, g_ref, u_ref, dz_ref):
        db = (d_ref[...] * 0.5).astype(BF)
        for s in range(2):
            cols = slice(s * hc, (s + 1) * hc)
            da = lax.dot_general(db, w_ref[cols, :], (((1,), (1,)), ((), ())), preferred_element_type=F32)
            g = g_ref[:, cols].astype(F32)
            sg = 1.0 / (1.0 + jnp.exp(-g))
            gs = g * sg
            dab = da.astype(BF)
            dz_ref[:, cols] = (dab * u_ref[:, cols]) * (sg + gs * (1.0 - sg)).astype(BF)
            dz_ref[:, dff + s * hc:dff + (s + 1) * hc] = dab * gs.astype(BF)

    row = lambda w: pl.BlockSpec((tm, w), lambda i: (i, 0))
    return pl.pallas_call(
        kern, grid=(T // tm,),
        in_specs=[row(D), pl.BlockSpec((None, dff, D), lambda i: (0, 0, 0), pipeline_mode=pl.Buffered(1)), row(dff), row(dff)],
        out_specs=row(2 * dff), out_shape=jax.ShapeDtypeStruct((T, 2 * dff), BF), name=name,
        compiler_params=pltpu.CompilerParams(dimension_semantics=("arbitrary",), vmem_limit_bytes=VMEM_CAP_BYTES),
    )(dxo, w3, zg, zu)


def _mm_dh_rms(name, dz, w3, xin, g, dres, tm):
    T, K = dz.shape
    D = w3.shape[1]
    tm = min(tm, T)
    assert T % tm == 0

    def kern(dz_ref, w_ref, x_ref, g_ref, r_ref, dx_ref, dg_ref):
        dh = lax.dot_general(dz_ref[...], w_ref[...], (((1,), (1,)), ((), ())), preferred_element_type=F32)
        _, vjp = jax.vjp(_rmsnorm, x_ref[...], g_ref[...])
        dx, dg = vjp(dh)
        dx_ref[...] = dx + r_ref[...]

        @pl.when(pl.program_id(0) == 0)
        def _():
            dg_ref[...] = jnp.zeros(dg_ref.shape, F32)

        dg_ref[...] += dg

    row = lambda w: pl.BlockSpec((tm, w), lambda i: (i, 0))
    one = pl.BlockSpec((1, D), lambda i: (0, 0))
    return pl.pallas_call(
        kern, grid=(T // tm,),
        in_specs=[row(K), pl.BlockSpec((None, D, K), lambda i: (0, 0, 0), pipeline_mode=pl.Buffered(1)), row(D), one, row(D)],
        out_specs=[row(D), one], out_shape=[jax.ShapeDtypeStruct((T, D), F32), jax.ShapeDtypeStruct((1, D), F32)], name=name,
        compiler_params=pltpu.CompilerParams(dimension_semantics=("arbitrary",), vmem_limit_bytes=VMEM_CAP_BYTES),
    )(dz, w3, xin, g, dres)


def _mm_tn_pair(name, a, b, kind, c_arr, tq, tk, scale=1.0):
    T, M = a.shape
    _, N = b.shape
    tk = min(tk, T)
    assert T % tk == 0
    nk = T // tk
    if kind == "col":
        hm = M // 2
        assert N % tq == 0
        nq = N // tq
        tile = (hm, tq)
        a_spec = pl.BlockSpec((tk, hm), lambda h, q, k, c: (k, jnp.bitwise_xor(h, 1 - c[0])))
        b_spec = pl.BlockSpec((tk, tq), lambda h, q, k, c: (k, q))
        o_spec = pl.BlockSpec(tile, lambda h, q, k, c: (0, q * h))
        out_sd = (hm, N)
    else:
        hn = N // 2
        assert M % tq == 0
        nq = M // tq
        tile = (tq, hn)
        a_spec = pl.BlockSpec((tk, tq), lambda h, q, k, c: (k, q))
        b_spec = pl.BlockSpec((tk, hn), lambda h, q, k, c: (k, jnp.bitwise_xor(h, 1 - c[0])))
        o_spec = pl.BlockSpec(tile, lambda h, q, k, c: (q * h, 0))
        out_sd = (M, hn)

    def kern(c_ref, a_ref, b_ref, o_ref, acc, stage, recv, ssem, rsem):
        h, q, k = pl.program_id(0), pl.program_id(1), pl.program_id(2)
        x, y, c, _ = _place()
        p = lax.dot_general(a_ref[...].astype(BF), b_ref[...].astype(BF), (((0,), (0,)), ((), ())), preferred_element_type=F32)

        @pl.when(k == 0)
        def _():
            acc[...] = p

        @pl.when(k > 0)
        def _():
            acc[...] += p

        def send(slot, qq):
            return pltpu.make_async_remote_copy(src_ref=stage.at[slot], dst_ref=recv.at[qq], send_sem=ssem.at[slot],
                                                recv_sem=rsem.at[qq], device_id=(x, y, 1 - c), device_id_type=MESH)

        last = k == nk - 1

        @pl.when(jnp.logical_and(last, h == 0))
        def _():
            slot = q % 2

            @pl.when(q >= 2)
            def _():
                send(slot, q).wait_send()

            stage[slot] = (acc[...] * scale).astype(BF)
            send(slot, q).start()

        @pl.when(jnp.logical_and(last, h == 1))
        def _():
            @pl.when(q == 0)
            def _():
                for s in range(min(nq, 2)):
                    send(s, 0).wait_send()

            send(0, q).wait_recv()
            o_ref[...] = (acc[...] * scale + recv[q].astype(F32)).astype(o_ref.dtype)

    return pl.pallas_call(
        kern,
        grid_spec=pltpu.PrefetchScalarGridSpec(
            num_scalar_prefetch=1, grid=(2, nq, nk), in_specs=[a_spec, b_spec], out_specs=o_spec,
            scratch_shapes=[pltpu.VMEM(tile, F32), pltpu.VMEM((2,) + tile, BF), pltpu.VMEM((nq,) + tile, BF),
                            pltpu.SemaphoreType.DMA((2,)), pltpu.SemaphoreType.DMA((nq,))]),
        out_shape=jax.ShapeDtypeStruct(out_sd, BF), name=name,
        compiler_params=pltpu.CompilerParams(dimension_semantics=("arbitrary", "arbitrary", "arbitrary"),
                                             vmem_limit_bytes=VMEM_CAP_BYTES),
    )(c_arr, a, b)


def _kv_pieces(kv_ref):
    W = XA_HEADS * XA_DIM
    mk = [kv_ref[:, a * XA_DIM:(a + 1) * XA_DIM] for a in range(XA_HEADS)]
    mv = [kv_ref[:, W + a * XA_DIM:W + (a + 1) * XA_DIM] for a in range(XA_HEADS)]
    return mk, mv


def _lb_pieces(lb_ref):
    return [[lb_ref[r:r + 1, h * HG_DIM:(h + 1) * HG_DIM] for h in range(HG_HEADS)] for r in range(3)]


HG_SUB = 4


def _hgrn_rows(z_ref):
    W = HG_HEADS * HG_DIM

    def piece(c, col, w):
        return z_ref[c * HG_CHUNK:(c + 1) * HG_CHUNK, col:col + w]

    zq = [[piece(c, h * HG_DIM, HG_DIM) for h in range(HG_HEADS)] for c in range(HG_SUB)]
    zf = [[piece(c, W + h * HG_DIM, HG_DIM) for h in range(HG_HEADS)] for c in range(HG_SUB)]
    zi = [[piece(c, 2 * W + h * HG_DIM, HG_DIM) for h in range(HG_HEADS)] for c in range(HG_SUB)]
    zg = [[piece(c, 3 * W + h * HG_DIM, HG_DIM) for h in range(HG_HEADS)] for c in range(HG_SUB)]
    zx = [z_ref[:, 4 * W + a * XA_DIM:4 * W + (a + 1) * XA_DIM] for a in range(XA_HEADS)]
    return zq, zf, zi, zg, zx


def _hgrn_steps(zq, zf, zi, zg, zx, l0, l1, l2, gn, mk, mv, S):
    mix = []
    for c in range(HG_SUB):
        row, s_next = [], []
        for h in range(HG_HEADS):
            o, sn = _hgrn_head(zq[c][h], zf[c][h], zi[c][h], zg[c][h], l0[h], l1[h], l2[h], gn, S[h])
            row.append(o)
            s_next.append(sn)
        mix.append(row)
        S = s_next
    att = [_attention(zx[a], mk[a], mv[a]) for a in range(XA_HEADS)]
    return mix, att, S


def _hgrn_fwd2(z, lb_logits, gnorm, kv, bl, seq):
    T, zw = z.shape
    mem_len = kv.shape[0] // bl
    cat_w = HG_HEADS * HG_DIM + XA_HEADS * XA_DIM
    R = HG_SUB * HG_CHUNK
    nb = seq // R

    def kern(z_ref, lb_ref, gn_ref, kv_ref, cat_ref, st_ref, s_scr):
        @pl.when(pl.program_id(1) == 0)
        def _():
            s_scr[...] = jnp.zeros(s_scr.shape, F32)

        st_ref[...] = s_scr[...]
        zq, zf, zi, zg, zx = _hgrn_rows(z_ref)
        mk, mv = _kv_pieces(kv_ref)
        l0, l1, l2 = _lb_pieces(lb_ref)
        S = [s_scr[h] for h in range(HG_HEADS)]
        mix, att, s_new = _hgrn_steps(zq, zf, zi, zg, zx, l0, l1, l2, gn_ref[...], mk, mv, S)
        for c in range(HG_SUB):
            for h in range(HG_HEADS):
                cat_ref[c * HG_CHUNK:(c + 1) * HG_CHUNK, h * HG_DIM:(h + 1) * HG_DIM] = mix[c][h].astype(cat_ref.dtype)
        for h in range(HG_HEADS):
            s_scr[h] = s_new[h]
        base = HG_HEADS * HG_DIM
        for a in range(XA_HEADS):
            cat_ref[:, base + a * XA_DIM:base + (a + 1) * XA_DIM] = att[a].astype(cat_ref.dtype)

    return pl.pallas_call(
        kern, grid=(bl, nb),
        in_specs=[pl.BlockSpec((R, zw), lambda b, n: (b * nb + n, 0)),
                  pl.BlockSpec(lb_logits.shape, lambda b, n: (0, 0)),
                  pl.BlockSpec(gnorm.shape, lambda b, n: (0, 0)),
                  pl.BlockSpec((mem_len, kv.shape[1]), lambda b, n: (b, 0))],
        out_specs=[pl.BlockSpec((R, cat_w), lambda b, n: (b * nb + n, 0)),
                   pl.BlockSpec((None, HG_HEADS, HG_DIM, HG_DIM), lambda b, n: (b * nb + n, 0, 0, 0))],
        out_shape=[jax.ShapeDtypeStruct((T, cat_w), BF),
                   jax.ShapeDtypeStruct((bl * nb, HG_HEADS, HG_DIM, HG_DIM), F32)],
        scratch_shapes=[pltpu.VMEM((HG_HEADS, HG_DIM, HG_DIM), F32)],
        name="hgrn_fwd",
        compiler_params=pltpu.CompilerParams(dimension_semantics=("arbitrary", "arbitrary"), vmem_limit_bytes=VMEM_CAP_BYTES),
    )(z, lb_logits, gnorm, kv)


def _hgrn_bwd2(z, dx, w_out, stash, lb_logits, gnorm, kv, bl, seq):
    T, zw = z.shape
    mem_len = kv.shape[0] // bl
    D = dx.shape[1]
    R = HG_SUB * HG_CHUNK
    nb = seq // R

    def kern(z_ref, dx_ref, w_ref, st_ref, lb_ref, gn_ref, kv_ref, dz_ref, dkv_ref, dlb_ref, dgn_ref, ds_scr):
        dc_ref = lax.dot_general(dx_ref[...].astype(BF), w_ref[...], (((1,), (1,)), ((), ())), preferred_element_type=F32)
        first = jnp.logical_and(pl.program_id(0) == 0, pl.program_id(1) == 0)

        @pl.when(pl.program_id(1) == 0)
        def _():
            ds_scr[...] = jnp.zeros(ds_scr.shape, F32)
            dkv_ref[...] = jnp.zeros(dkv_ref.shape, F32)

        @pl.when(first)
        def _():
            dlb_ref[...] = jnp.zeros(dlb_ref.shape, F32)
            dgn_ref[...] = jnp.zeros(dgn_ref.shape, F32)

        zq, zf, zi, zg, zx = _hgrn_rows(z_ref)
        mk, mv = _kv_pieces(kv_ref)
        l0, l1, l2 = _lb_pieces(lb_ref)
        S = [st_ref[h] for h in range(HG_HEADS)]
        _, vjp = jax.vjp(_hgrn_steps, zq, zf, zi, zg, zx, l0, l1, l2, gn_ref[...], mk, mv, S)
        d_mix = [[dc_ref[c * HG_CHUNK:(c + 1) * HG_CHUNK, h * HG_DIM:(h + 1) * HG_DIM] for h in range(HG_HEADS)]
                 for c in range(HG_SUB)]
        base = HG_HEADS * HG_DIM
        d_att = [dc_ref[:, base + a * XA_DIM:base + (a + 1) * XA_DIM] for a in range(XA_HEADS)]
        d_s = [ds_scr[h] for h in range(HG_HEADS)]
        dzq, dzf, dzi, dzg, dzx, dl0, dl1, dl2, dgn, dmk, dmv, dS = vjp((d_mix, d_att, d_s))
        W = HG_HEADS * HG_DIM
        for c in range(HG_SUB):
            rows = slice(c * HG_CHUNK, (c + 1) * HG_CHUNK)
            for h in range(HG_HEADS):
                for k, part in enumerate((dzq, dzf, dzi, dzg)):
                    dz_ref[rows, k * W + h * HG_DIM:k * W + (h + 1) * HG_DIM] = part[c][h].astype(dz_ref.dtype)
        for h in range(HG_HEADS):
            sl = slice(h * HG_DIM, (h + 1) * HG_DIM)
            ds_scr[h] = dS[h]
            dlb_ref[0:1, sl] += dl0[h]
            dlb_ref[1:2, sl] += dl1[h]
            dlb_ref[2:3, sl] += dl2[h]
        dgn_ref[...] += dgn
        KW = XA_HEADS * XA_DIM
        for a in range(XA_HEADS):
            dz_ref[:, 4 * W + a * XA_DIM:4 * W + (a + 1) * XA_DIM] = dzx[a].astype(dz_ref.dtype)
            dkv_ref[:, a * XA_DIM:(a + 1) * XA_DIM] += dmk[a]
            dkv_ref[:, KW + a * XA_DIM:KW + (a + 1) * XA_DIM] += dmv[a]

    rev = lambda b, n: (b * nb + (nb - 1 - n), 0)
    return pl.pallas_call(
        kern, grid=(bl, nb),
        in_specs=[pl.BlockSpec((R, zw), rev),
                  pl.BlockSpec((R, D), rev),
                  pl.BlockSpec((None,) + w_out.shape[1:], lambda b, n: (0, 0, 0), pipeline_mode=pl.Buffered(1)),
                  pl.BlockSpec((None, HG_HEADS, HG_DIM, HG_DIM), lambda b, n: (b * nb + (nb - 1 - n), 0, 0, 0)),
                  pl.BlockSpec(lb_logits.shape, lambda b, n: (0, 0)),
                  pl.BlockSpec(gnorm.shape, lambda b, n: (0, 0)),
                  pl.BlockSpec((mem_len, kv.shape[1]), lambda b, n: (b, 0))],
        out_specs=[pl.BlockSpec((R, zw), rev),
                   pl.BlockSpec((mem_len, kv.shape[1]), lambda b, n: (b, 0)),
                   pl.BlockSpec(lb_logits.shape, lambda b, n: (0, 0)),
                   pl.BlockSpec(gnorm.shape, lambda b, n: (0, 0))],
        out_shape=[jax.ShapeDtypeStruct((T, zw), BF), jax.ShapeDtypeStruct(kv.shape, F32),
                   jax.ShapeDtypeStruct(lb_logits.shape, F32), jax.ShapeDtypeStruct(gnorm.shape, F32)],
        scratch_shapes=[pltpu.VMEM((HG_HEADS, HG_DIM, HG_DIM), F32)],
        name="hgrn_bwd",
        compiler_params=pltpu.CompilerParams(dimension_semantics=("arbitrary", "arbitrary"), vmem_limit_bytes=VMEM_CAP_BYTES),
    )(z, dx, w_out, stash, lb_logits, gnorm, kv)


GM_SUB = 2


def _gmlp_pieces(z_ref):
    W = GM_GROUPS * GM_GROUP_DIM
    zu = [z_ref[:, g * GM_GROUP_DIM:(g + 1) * GM_GROUP_DIM] for g in range(GM_GROUPS)]
    zv = [z_ref[:, W + g * GM_GROUP_DIM:W + (g + 1) * GM_GROUP_DIM] for g in range(GM_GROUPS)]
    zx = [z_ref[:, 2 * W + a * XA_DIM:2 * W + (a + 1) * XA_DIM] for a in range(XA_HEADS)]
    return zu, zv, zx


def _gmlp_params(lng_ref, lnb_ref, ws_ref, bs_ref):
    lng = [lng_ref[:, g * GM_GROUP_DIM:(g + 1) * GM_GROUP_DIM] for g in range(GM_GROUPS)]
    lnb = [lnb_ref[:, g * GM_GROUP_DIM:(g + 1) * GM_GROUP_DIM] for g in range(GM_GROUPS)]
    ws = [ws_ref[g] for g in range(GM_GROUPS)]
    bs = [bs_ref[g:g + 1, :] for g in range(GM_GROUPS)]
    return lng, lnb, ws, bs


def _gmlp_fwd(z, ln_g, ln_b, w_s, b_s, kv, bl, nc):
    T, zw = z.shape
    mem_len = kv.shape[0] // bl
    cat_w = GM_GROUPS * GM_GROUP_DIM + XA_HEADS * XA_DIM

    assert nc % GM_SUB == 0
    nc = nc // GM_SUB
    R = GM_SUB * GM_CHUNK

    def kern(z_ref, lng_ref, lnb_ref, ws_ref, bs_ref, kv_ref, cat_ref):
        lng, lnb, ws, bs = _gmlp_params(lng_ref, lnb_ref, ws_ref, bs_ref)
        mk, mv = _kv_pieces(kv_ref)
        for c in range(GM_SUB):
            rows = pl.ds(c * GM_CHUNK, GM_CHUNK)
            zu, zv, zx = _gmlp_pieces(z_ref.at[rows])
            out = cat_ref.at[rows]
            outs = _gmlp_block(zu, zv, zx, lng, lnb, ws, bs, mk, mv)
            for g in range(GM_GROUPS):
                out[:, g * GM_GROUP_DIM:(g + 1) * GM_GROUP_DIM] = outs[g].astype(cat_ref.dtype)
            base = GM_GROUPS * GM_GROUP_DIM
            for a in range(XA_HEADS):
                out[:, base + a * XA_DIM:base + (a + 1) * XA_DIM] = outs[GM_GROUPS + a].astype(cat_ref.dtype)

    full2 = lambda b, n: (0, 0)
    return pl.pallas_call(
        kern, grid=(bl, nc),
        in_specs=[pl.BlockSpec((R, zw), lambda b, n: (b * nc + n, 0)),
                  pl.BlockSpec(ln_g.shape, full2), pl.BlockSpec(ln_b.shape, full2),
                  pl.BlockSpec(w_s.shape, lambda b, n: (0, 0, 0)), pl.BlockSpec(b_s.shape, full2),
                  pl.BlockSpec((mem_len, kv.shape[1]), lambda b, n: (b, 0))],
        out_specs=pl.BlockSpec((R, cat_w), lambda b, n: (b * nc + n, 0)),
        out_shape=jax.ShapeDtypeStruct((T, cat_w), BF),
        name="gmlp_fwd",
        compiler_params=pltpu.CompilerParams(dimension_semantics=("arbitrary", "arbitrary"), vmem_limit_bytes=VMEM_CAP_BYTES),
    )(z, ln_g, ln_b, w_s, b_s, kv)


def _gmlp_bwd(z, dx, w_out, ln_g, ln_b, w_s, b_s, kv, bl, nc):
    T, zw = z.shape
    mem_len = kv.shape[0] // bl
    D = dx.shape[1]
    assert nc % GM_SUB == 0
    nc = nc // GM_SUB

    def kern(z_ref, dx_ref, w_ref, lng_ref, lnb_ref, ws_ref, bs_ref, kv_ref,
             dz_ref, dkv_ref, dlng_ref, dlnb_ref, dws_ref, dbs_ref):
        first = jnp.logical_and(pl.program_id(0) == 0, pl.program_id(1) == 0)

        @pl.when(pl.program_id(1) == 0)
        def _():
            dkv_ref[...] = jnp.zeros(dkv_ref.shape, F32)

        @pl.when(first)
        def _():
            dlng_ref[...] = jnp.zeros(dlng_ref.shape, F32)
            dlnb_ref[...] = jnp.zeros(dlnb_ref.shape, F32)
            dws_ref[...] = jnp.zeros(dws_ref.shape, F32)
            dbs_ref[...] = jnp.zeros(dbs_ref.shape, F32)

        lng, lnb, ws, bs = _gmlp_params(lng_ref, lnb_ref, ws_ref, bs_ref)
        mk, mv = _kv_pieces(kv_ref)
        W = GM_GROUPS * GM_GROUP_DIM
        KW = XA_HEADS * XA_DIM
        for c in range(GM_SUB):
            rows = pl.ds(c * GM_CHUNK, GM_CHUNK)
            zu, zv, zx = _gmlp_pieces(z_ref.at[rows])
            dz = dz_ref.at[rows]
            dc = lax.dot_general(dx_ref[rows, :].astype(BF), w_ref[...], (((1,), (1,)), ((), ())), preferred_element_type=F32)
            _, vjp = jax.vjp(_gmlp_block, zu, zv, zx, lng, lnb, ws, bs, mk, mv)
            d_outs = [dc[:, g * GM_GROUP_DIM:(g + 1) * GM_GROUP_DIM] for g in range(GM_GROUPS)]
            d_outs += [dc[:, W + a * XA_DIM:W + (a + 1) * XA_DIM] for a in range(XA_HEADS)]
            dzu, dzv, dzx, dlng, dlnb, dws, dbs, dmk, dmv = vjp(d_outs)
            for g in range(GM_GROUPS):
                sl = slice(g * GM_GROUP_DIM, (g + 1) * GM_GROUP_DIM)
                dz[:, sl] = dzu[g].astype(dz_ref.dtype)
                dz[:, W + g * GM_GROUP_DIM:W + (g + 1) * GM_GROUP_DIM] = dzv[g].astype(dz_ref.dtype)
                dlng_ref[:, sl] += dlng[g]
                dlnb_ref[:, sl] += dlnb[g]
                dws_ref[g] += dws[g]
                dbs_ref[g:g + 1, :] += dbs[g]
            for a in range(XA_HEADS):
                dz[:, 2 * W + a * XA_DIM:2 * W + (a + 1) * XA_DIM] = dzx[a].astype(dz_ref.dtype)
                dkv_ref[:, a * XA_DIM:(a + 1) * XA_DIM] += dmk[a]
                dkv_ref[:, KW + a * XA_DIM:KW + (a + 1) * XA_DIM] += dmv[a]

    full2 = lambda b, n: (0, 0)
    full3 = lambda b, n: (0, 0, 0)
    blk = lambda b, n: (b * nc + n, 0)
    return pl.pallas_call(
        kern, grid=(bl, nc),
        in_specs=[pl.BlockSpec((GM_SUB * GM_CHUNK, zw), blk), pl.BlockSpec((GM_SUB * GM_CHUNK, D), blk),
                  pl.BlockSpec((None,) + w_out.shape[1:], full3, pipeline_mode=pl.Buffered(1)),
                  pl.BlockSpec(ln_g.shape, full2), pl.BlockSpec(ln_b.shape, full2),
                  pl.BlockSpec(w_s.shape, full3), pl.BlockSpec(b_s.shape, full2),
                  pl.BlockSpec((mem_len, kv.shape[1]), lambda b, n: (b, 0))],
        out_specs=[pl.BlockSpec((GM_SUB * GM_CHUNK, zw), blk),
                   pl.BlockSpec((mem_len, kv.shape[1]), lambda b, n: (b, 0)),
                   pl.BlockSpec(ln_g.shape, full2), pl.BlockSpec(ln_b.shape, full2),
                   pl.BlockSpec(w_s.shape, full3), pl.BlockSpec(b_s.shape, full2)],
        out_shape=[jax.ShapeDtypeStruct((T, zw), BF), jax.ShapeDtypeStruct(kv.shape, F32),
                   jax.ShapeDtypeStruct(ln_g.shape, F32), jax.ShapeDtypeStruct(ln_b.shape, F32),
                   jax.ShapeDtypeStruct(w_s.shape, F32), jax.ShapeDtypeStruct(b_s.shape, F32)],
        name="gmlp_bwd",
        compiler_params=pltpu.CompilerParams(dimension_semantics=("arbitrary", "arbitrary"), vmem_limit_bytes=VMEM_CAP_BYTES),
    )(z, dx, w_out, ln_g, ln_b, w_s, b_s, kv)


def _place():
    x, y, c = lax.axis_index("x"), lax.axis_index("y"), lax.axis_index("c")
    chips = [(1 - x, y), (x, 1 - y), (1 - x, 1 - y)]
    return x, y, c, chips


def _half(ref, kind, e):
    if kind == "col":
        n = ref.shape[1] // 2
        return ref.at[:, pl.ds(pl.multiple_of(e * n, n), n), :]
    n = ref.shape[2] // 2
    return ref.at[:, :, pl.ds(pl.multiple_of(e * n, n), n)]


def _slot(ref, kind, j, n):
    if kind == "col":
        return ref.at[:, :, pl.ds(pl.multiple_of(j * n, n), n)]
    return ref.at[:, pl.ds(pl.multiple_of(j * n, n), n), :]


def _allgather_seq(name, items, cid):
    nt = len(items)
    kinds = [k for (_, k, _) in items]
    slot_kind = ["row" if k == "row" else "col" for k in kinds]
    out_type = []
    for s, k, l in items:
        L, r, c = s.shape
        lo = L if l is None else 1
        out_type.append(jax.ShapeDtypeStruct((lo, 4 * r, c) if k == "row" else (lo, r, 4 * c), s.dtype))

    def part(ref, t, e):
        return ref if kinds[t] == "vec" else _half(ref, kinds[t], e)

    def body(*refs):
        sh = [refs[t] if items[t][2] is None else refs[t].at[pl.ds(items[t][2], 1)] for t in range(nt)]
        full = refs[nt:2 * nt]
        loc, s_ici, r_ici, s_d2d, r_d2d = refs[2 * nt:]
        x, y, c, chips = _place()
        own = 2 * x + y
        sibling = (x, y, 1 - c)
        barrier = pltpu.get_barrier_semaphore()
        for peer in [(px, py, c) for (px, py) in chips] + [sibling]:
            pl.semaphore_signal(barrier, inc=1, device_id=peer, device_id_type=MESH)
        pl.semaphore_wait(barrier, 4)
        width = [sh[t].shape[1] if kinds[t] == "row" else sh[t].shape[2] for t in range(nt)]
        started = []
        for t in range(nt):
            mine = pltpu.make_async_copy(sh[t], _slot(full[t], slot_kind[t], own, width[t]), loc.at[t])
            mine.start()
            started.append(mine)
        sent = []
        for t in range(nt):
            for p, (px, py) in enumerate(chips):
                cp = pltpu.make_async_remote_copy(
                    src_ref=part(sh[t], t, c), dst_ref=part(_slot(full[t], slot_kind[t], own, width[t]), t, c),
                    send_sem=s_ici.at[t, p], recv_sem=r_ici.at[t, p], device_id=(px, py, c), device_id_type=MESH)
                cp.start()
                sent.append(cp)
        for t in range(nt):
            for p, (px, py) in enumerate(chips):
                landed = part(_slot(full[t], slot_kind[t], 2 * px + py, width[t]), t, c)
                pltpu.make_async_remote_copy(
                    src_ref=landed, dst_ref=landed, send_sem=s_ici.at[t, p], recv_sem=r_ici.at[t, p],
                    device_id=(px, py, c), device_id_type=MESH).wait_recv()
                if kinds[t] == "vec":
                    continue
                fw = pltpu.make_async_remote_copy(
                    src_ref=landed, dst_ref=landed, send_sem=s_d2d.at[t, p], recv_sem=r_d2d.at[t, p],
                    device_id=sibling, device_id_type=MESH)
                fw.start()
                sent.append(fw)
        for t in range(nt):
            if kinds[t] == "vec":
                continue
            for p, (px, py) in enumerate(chips):
                other = _half(_slot(full[t], kinds[t], 2 * px + py, width[t]), kinds[t], 1 - c)
                pltpu.make_async_remote_copy(
                    src_ref=other, dst_ref=other, send_sem=s_d2d.at[t, p], recv_sem=r_d2d.at[t, p],
                    device_id=sibling, device_id_type=MESH).wait_recv()
        for cp in sent:
            cp.wait_send()
        for cp in started:
            cp.wait()

    return pl.kernel(
        body, out_type=out_type, mesh=plsc.ScalarSubcoreMesh(axis_name="seq", num_cores=1),
        scratch_types=[pltpu.SemaphoreType.DMA((nt,)), pltpu.SemaphoreType.DMA((nt, 3)), pltpu.SemaphoreType.DMA((nt, 3)),
                       pltpu.SemaphoreType.DMA((nt, 3)), pltpu.SemaphoreType.DMA((nt, 3))],
        compiler_params=pltpu.CompilerParams(collective_id=cid), name=name,
    )(*[s for (s, _, _) in items])


def _slot2(ref, kind, j, n):
    if kind == "col":
        return ref.at[:, pl.ds(pl.multiple_of(j * n, n), n)]
    return ref.at[pl.ds(pl.multiple_of(j * n, n), n), :]


def _rs_chips_seq(name, parts, kinds, cid):
    nm = len(parts)
    out_type = []
    for g, k in zip(parts, kinds):
        r, c = g.shape
        ps = (r, c // 4) if k == "col" else (r // 4, c)
        out_type += [jax.ShapeDtypeStruct(ps, BF), jax.ShapeDtypeStruct((3,) + ps, BF)]

    def body(*refs):
        g = refs[:nm]
        outs = refs[nm:3 * nm]
        loc, ssem, rsem = refs[3 * nm:]
        x, y, c, chips = _place()
        own = 2 * x + y
        barrier = pltpu.get_barrier_semaphore()
        for (px, py) in chips:
            pl.semaphore_signal(barrier, inc=1, device_id=(px, py, c), device_id_type=MESH)
        pl.semaphore_wait(barrier, 3)
        cps = []
        for m in range(nm):
            k = kinds[m]
            own_o, got_o = outs[2 * m], outs[2 * m + 1]
            n = g[m].shape[1] // 4 if k == "col" else g[m].shape[0] // 4
            lc = pltpu.make_async_copy(_slot2(g[m], k, own, n), own_o, loc.at[m])
            lc.start()
            cps.append(lc)
            for p, (px, py) in enumerate(chips):
                cp = pltpu.make_async_remote_copy(
                    src_ref=_slot2(g[m], k, 2 * px + py, n), dst_ref=got_o.at[p],
                    send_sem=ssem.at[m, p], recv_sem=rsem.at[m, p], device_id=(px, py, c), device_id_type=MESH)
                cp.start()
                cps.append(cp)
        for cp in cps:
            cp.wait()

    return pl.kernel(
        body, out_type=out_type, mesh=plsc.ScalarSubcoreMesh(axis_name="seq", num_cores=1),
        scratch_types=[pltpu.SemaphoreType.DMA((nm,)), pltpu.SemaphoreType.DMA((nm, 3)), pltpu.SemaphoreType.DMA((nm, 3))],
        compiler_params=pltpu.CompilerParams(collective_id=cid), name=name,
    )(*parts)


def _finish_share(name, owns, gots, kind, c_arr):
    L = len(owns)
    r, c = owns[0].shape
    tr = _pick(r, 128 if kind == "col" else 256)
    nb = r // tr
    nq = L * nb

    def chunk_of(l):
        return lambda h, q: jnp.clip(q * (1 - h) + (nq - 1) * h - l * nb, 0, nb - 1)

    ins, in_specs = [], []
    for l in range(L):
        at = chunk_of(l)
        ins += [owns[l], gots[l].reshape(3 * r, c), gots[l].reshape(3 * r, c), gots[l].reshape(3 * r, c)]
        in_specs.append(pl.BlockSpec((tr, c), functools.partial(lambda h, q, cc, at: (at(h, q), 0), at=at)))
        in_specs += [pl.BlockSpec((tr, c), functools.partial(lambda h, q, cc, at, p: (p * nb + at(h, q), 0), at=at, p=p))
                     for p in range(3)]
    if kind == "col":
        out_sd = (L, 2, r, c)
        o_spec = pl.BlockSpec((None, 2, tr, c), lambda h, q, cc: ((q * h) // nb, 0, (q * h) % nb, 0))
    else:
        out_sd = (L * r, 2 * c)
        o_spec = pl.BlockSpec((tr, 2 * c), lambda h, q, cc: (q * h, 0))

    def kern(c_ref, *refs):
        in_refs = refs[:4 * L]
        out_ref, mine, recv, ssem, rsem = refs[4 * L:]
        h, q = pl.program_id(0), pl.program_id(1)
        x, y, cc, _ = _place()

        def swap(qq):
            return pltpu.make_async_remote_copy(src_ref=mine.at[qq], dst_ref=recv.at[qq], send_sem=ssem.at[qq],
                                                recv_sem=rsem.at[qq], device_id=(x, y, 1 - cc), device_id_type=MESH)

        for l in range(L):
            @pl.when(jnp.logical_and(h == 0, q // nb == l))
            def _(l=l):
                o_ref, g0, g1, g2 = in_refs[4 * l:4 * l + 4]
                mine[q] = ((o_ref[...].astype(F32) + g0[...].astype(F32)) + g1[...].astype(F32)) + g2[...].astype(F32)
                swap(q).start()

        @pl.when(h == 1)
        def _():
            swap(q).wait()
            a, b = mine[q], recv[q]
            first = c_ref[0] == 0
            lo, hi = jnp.where(first, a, b), jnp.where(first, b, a)
            if kind == "col":
                out_ref[0] = lo
                out_ref[1] = hi
            else:
                out_ref[:, :c] = lo
                out_ref[:, c:] = hi

    full = pl.pallas_call(
        kern,
        grid_spec=pltpu.PrefetchScalarGridSpec(
            num_scalar_prefetch=1, grid=(2, nq), in_specs=in_specs, out_specs=o_spec,
            scratch_shapes=[pltpu.VMEM((nq, tr, c), F32), pltpu.VMEM((nq, tr, c), F32),
                            pltpu.SemaphoreType.DMA((nq,)), pltpu.SemaphoreType.DMA((nq,))]),
        out_shape=jax.ShapeDtypeStruct(out_sd, F32), name=name,
        compiler_params=pltpu.CompilerParams(dimension_semantics=("arbitrary", "arbitrary"),
                                             vmem_limit_bytes=VMEM_CAP_BYTES),
    )(c_arr, *ins)
    return full.reshape(L, 2 * r, c) if kind == "col" else full.reshape(L, r, 2 * c)


def _small_allreduce(buf, name):
    R = buf.shape[0]
    assert R % 16 == 0
    h = R // 2

    def body(x_ref, o_ref, sib, csum, got, s_a, r_a, s_b, r_b, s_c, r_c):
        x, y, c, chips = _place()
        sibling = (x, y, 1 - c)
        own = 2 * x + y
        swap = pltpu.make_async_remote_copy(src_ref=x_ref, dst_ref=sib, send_sem=s_a, recv_sem=r_a,
                                            device_id=sibling, device_id_type=MESH)
        swap.start()
        swap.wait()
        a, b = x_ref[...], sib[...]
        south = c == 0
        csum[...] = jnp.where(south, a, b) + jnp.where(south, b, a)
        lo = pl.multiple_of(c * h, 8)
        mine = csum.at[pl.ds(lo, h)]
        got[own] = csum[pl.ds(lo, h)]
        sends = []
        for p, (px, py) in enumerate(chips):
            cp = pltpu.make_async_remote_copy(src_ref=mine, dst_ref=got.at[own], send_sem=s_b.at[p], recv_sem=r_b.at[p],
                                              device_id=(px, py, c), device_id_type=MESH)
            cp.start()
            sends.append(cp)
        for cp in sends:
            cp.wait()
        o_ref[pl.ds(lo, h)] = ((got[0] + got[1]) + got[2]) + got[3]
        done = o_ref.at[pl.ds(lo, h)]
        back = pltpu.make_async_remote_copy(src_ref=done, dst_ref=done, send_sem=s_c, recv_sem=r_c,
                                            device_id=sibling, device_id_type=MESH)
        back.start()
        back.wait_send()
        other = o_ref.at[pl.ds(pl.multiple_of((1 - c) * h, 8), h)]
        pltpu.make_async_remote_copy(src_ref=other, dst_ref=other, send_sem=s_c, recv_sem=r_c,
                                     device_id=sibling, device_id_type=MESH).wait_recv()

    vm = pl.BlockSpec(memory_space=pltpu.VMEM)
    return pl.pallas_call(
        body, out_shape=jax.ShapeDtypeStruct(buf.shape, F32), in_specs=[vm], out_specs=vm,
        scratch_shapes=[pltpu.VMEM((R, LANES), F32), pltpu.VMEM((R, LANES), F32), pltpu.VMEM((4, h, LANES), F32),
                        pltpu.SemaphoreType.DMA, pltpu.SemaphoreType.DMA, pltpu.SemaphoreType.DMA((3,)),
                        pltpu.SemaphoreType.DMA((3,)), pltpu.SemaphoreType.DMA, pltpu.SemaphoreType.DMA],
        name=name,
        compiler_params=pltpu.CompilerParams(vmem_limit_bytes=VMEM_CAP_BYTES),
    )(buf)


PACK_TILE_ROWS = 8


def _item_rows(shape):
    n = 1
    for d in shape:
        n *= d
    return -(-n // (PACK_TILE_ROWS * LANES)) * PACK_TILE_ROWS


def _pack(arrs, rows_total):
    buf = jnp.zeros((rows_total, LANES), F32)
    r = 0
    for a in arrs:
        f = a.reshape(-1).astype(F32)
        nr = _item_rows(a.shape)
        block = jnp.pad(f, (0, nr * LANES - f.shape[0])).reshape(nr, LANES)
        buf = lax.dynamic_update_slice(buf, block, (r, 0))
        r += nr
    return buf


def _unpack(buf, shapes):
    out, r = [], 0
    for s in shapes:
        n = 1
        for d in s:
            n *= d
        nr = _item_rows(s)
        out.append(buf[r:r + nr].reshape(-1)[:n].reshape(s))
        r += nr
    return out


def _rows_needed(shapes):
    return -(-sum(_item_rows(s) for s in shapes) // (2 * PACK_TILE_ROWS)) * (2 * PACK_TILE_ROWS)


def _two_rows(a, b):
    out = jnp.zeros((2, a.shape[1]), a.dtype)
    return lax.dynamic_update_slice(lax.dynamic_update_slice(out, a, (0, 0)), b, (1, 0))


def _adam(w, g, m, v):
    m = ADAM_B1 * m + (1.0 - ADAM_B1) * g
    v = ADAM_B2 * v + (1.0 - ADAM_B2) * jnp.square(g)
    m_hat = m / (1.0 - ADAM_B1 ** ADAM_STEP)
    v_hat = v / (1.0 - ADAM_B2 ** ADAM_STEP)
    delta = -ADAM_LR * (m_hat / (jnp.sqrt(v_hat) + ADAM_EPS) + ADAM_WD * w)
    return delta, m, v


def _adam_call(name, w2, g2, m2, v2, tr, pass_grad=False):
    def fn(rv, cv):
        outs = list(_adam(*rv))
        return ([rv[1]] + outs if pass_grad else outs), []

    width = w2.shape[1]
    return _rowcall(name, fn, [(w2, 0, width), (g2, 0, width), (m2, 0, width), (v2, 0, width)], [],
                    [(width, F32)] * (4 if pass_grad else 3), [], tr)


def kernel(x, mem, mem_norm, lb_logits, ffn1_norm, ffn1_w_in, ffn1_w_out, mix_norm, mem_w_kv, hgrn_w_in, hgrn_gnorm, hgrn_w_out, gmlp_w_in, gmlp_ln_g, gmlp_ln_b, gmlp_w_s, gmlp_b_s, gmlp_w_out, ffn2_norm, ffn2_w_in, ffn2_w_out, final_norm, loss_target, m_mem_norm, m_lb_logits, m_ffn1_norm, m_ffn1_w_in, m_ffn1_w_out, m_mix_norm, m_mem_w_kv, m_hgrn_w_in, m_hgrn_gnorm, m_hgrn_w_out, m_gmlp_w_in, m_gmlp_ln_g, m_gmlp_ln_b, m_gmlp_w_s, m_gmlp_b_s, m_gmlp_w_out, m_ffn2_norm, m_ffn2_w_in, m_ffn2_w_out, m_final_norm, v_mem_norm, v_lb_logits, v_ffn1_norm, v_ffn1_w_in, v_ffn1_w_out, v_mix_norm, v_mem_w_kv, v_hgrn_w_in, v_hgrn_gnorm, v_hgrn_w_out, v_gmlp_w_in, v_gmlp_ln_g, v_gmlp_ln_b, v_gmlp_w_s, v_gmlp_b_s, v_gmlp_w_out, v_ffn2_norm, v_ffn2_w_in, v_ffn2_w_out, v_final_norm):
    bl, seq, D = x.shape
    T = bl * seq
    mem_len = mem.shape[1]
    chip = 2 * lax.axis_index("x") + lax.axis_index("y")
    c_arr = lax.axis_index("c").astype(jnp.int32).reshape(1)
    TR = 1024

    big = [("ffn1_w_in", ffn1_w_in, "col"), ("ffn1_w_out", ffn1_w_out, "row"), ("mem_w_kv", mem_w_kv, "col"),
           ("hgrn_w_in", hgrn_w_in, "col"), ("hgrn_w_out", hgrn_w_out, "row"), ("gmlp_w_in", gmlp_w_in, "col"),
           ("gmlp_w_out", gmlp_w_out, "row"), ("ffn2_w_in", ffn2_w_in, "col"), ("ffn2_w_out", ffn2_w_out, "row")]
    kinds = [k for (_, _, k) in big]
    shards_bf = []
    for nm, w, _ in big:
        L, r, c = w.shape
        (wb,) = _rowcall("cast_" + nm, lambda rv, cv: ([rv[0]], []), [(w.reshape(L * r, c), 0, c)], [], [(c, BF)], [], 512)
        shards_bf.append(wb.reshape(L, r, c))
    sb = dict(zip([nm for (nm, _, _) in big], shards_bf))
    groups = [[("ffn1_w_in", 0)], [("ffn1_w_out", 0)], [("hgrn_w_in", None)], [("mem_w_kv", None)], [("hgrn_w_out", None)],
              [("ffn2_w_in", 0), ("ffn2_w_out", 0), ("gmlp_ln_g", None), ("gmlp_ln_b", None)],
              [("ffn1_w_in", 1), ("ffn1_w_out", 1)],
              [("gmlp_w_in", None), ("gmlp_w_out", None)],
              [("ffn2_w_in", 1), ("ffn2_w_out", 1)]]
    kind_of = {nm: k for (nm, _, k) in big}
    for nm, vec in (("gmlp_ln_g", gmlp_ln_g), ("gmlp_ln_b", gmlp_ln_b)):
        sb[nm] = vec.reshape(1, 1, -1)
        kind_of[nm] = "vec"
    gathered = {nm: [None, None] for nm in ("ffn1_w_in", "ffn1_w_out", "ffn2_w_in", "ffn2_w_out")}
    for gi, grp in enumerate(groups):
        outs = _allgather_seq("gather_%d" % gi, [(sb[nm], kind_of[nm], l) for (nm, l) in grp], gi)
        for (nm, l), o in zip(grp, outs):
            if l is None:
                gathered[nm] = o
            else:
                gathered[nm][l] = o

    ln_w = GM_GROUPS * GM_GROUP_DIM
    ln_g_full, ln_b_full = gathered["gmlp_ln_g"].reshape(1, ln_w), gathered["gmlp_ln_b"].reshape(1, ln_w)

    def rms_fwd(name, xin, g):
        (h,) = _rowcall(name, lambda rv, cv: ([_rmsnorm(rv[0], cv[0])], []), [(xin, 0, D)], [g.reshape(1, D)], [(D, BF)], [], TR)
        return h

    def ffn_fwd(tag, xin, h, w_in, w_out, layer, next_gain):
        dff = w_out[layer].shape[1]
        zg, zu, a = _ffn_in_swiglu("ffn_in_" + tag, h, w_in[layer], 1024, dff // 2)
        out = _mm("ffn_out_" + tag, a, w_out[layer], "nn", F32, 1024, 1024, dff, scale=0.5, res=xin, b_lead=0,
                  norm_gain=None if next_gain is None else next_gain.reshape(1, D))
        xo, h_next = (out, None) if next_gain is None else out
        return xo, h_next, (xin, h, zg, zu, a)

    def ffn_bwd(tag, dxo, saved, g, w_in, w_out, layer):
        xin, h, zg, zu, a = saved
        dff = w_out[layer].shape[1]
        dw_out = _mm_tn_pair("ffn_dwo_" + tag, a, dxo, "row", c_arr, dff // 2, T, scale=0.5)
        dz = _ffn_da_swiglu("ffn_da_" + tag, dxo, w_out[layer], zg, zu, 512)
        dw_in = _mm_tn_pair("ffn_dwi_" + tag, h, dz, "col", c_arr, 512, T)
        dx, dg = _mm_dh_rms("ffn_dh_" + tag, dz, w_in[layer], xin, g.reshape(1, D), dxo, 512)
        return dx, dg, dw_in, dw_out

    def rms_bwd(name, xin, g, dh, dres):
        def fn(rv, cv):
            _, vjp = jax.vjp(_rmsnorm, rv[0], cv[0])
            dx, dg = vjp(rv[1])
            if dres is not None:
                dx = dx + rv[2]
            return [dx], [dg]

        rows = [(xin, 0, D), (dh, 0, D)] + ([(dres, 0, D)] if dres is not None else [])
        dx, dg = _rowcall(name, fn, rows, [g.reshape(1, D)], [(D, F32)], [((1, D), F32)], TR)
        return dx, dg

    x0 = x.reshape(T, D)
    tgt = loss_target.reshape(T, D)
    mem2 = mem.reshape(bl * mem_len, D)
    memn = rms_fwd("rms_mem", mem2, mem_norm)

    h_f10 = rms_fwd("rms_f1l0", x0, ffn1_norm[0])
    x1, h_m0, sv_f10 = ffn_fwd("f1l0", x0, h_f10, gathered["ffn1_w_in"], gathered["ffn1_w_out"], 0, mix_norm[0])
    z_m0 = _mm("mix_in_0", h_m0, gathered["hgrn_w_in"], "nn", F32, 2048, 512, D, b_lead=0)
    kv = [_mm("kv_%d" % i, memn, gathered["mem_w_kv"], "nn", F32, 512, 512, D, b_lead=i) for i in range(2)]
    cat0, stash0 = _hgrn_fwd2(z_m0, lb_logits, hgrn_gnorm, kv[0], bl, seq)
    x2, h_f20 = _mm("mix_out_0", cat0, gathered["hgrn_w_out"], "nn", F32, 1024, 1024, cat0.shape[1], res=x1, b_lead=0,
                    norm_gain=ffn2_norm[0].reshape(1, D))
    x3, h_f11, sv_f20 = ffn_fwd("f2l0", x2, h_f20, gathered["ffn2_w_in"], gathered["ffn2_w_out"], 0, ffn1_norm[1])
    x4, h_m1, sv_f11 = ffn_fwd("f1l1", x3, h_f11, gathered["ffn1_w_in"], gathered["ffn1_w_out"], 1, mix_norm[1])
    z_m1 = _mm("mix_in_1", h_m1, gathered["gmlp_w_in"], "nn", F32, 2048, 512, D, b_lead=0)
    nc1 = seq // GM_CHUNK
    w_s, b_s = gmlp_w_s[0], gmlp_b_s[0]
    cat1 = _gmlp_fwd(z_m1, ln_g_full, ln_b_full, w_s, b_s, kv[1], bl, nc1)
    x5, h_f21 = _mm("mix_out_1", cat1, gathered["gmlp_w_out"], "nn", F32, 1024, 1024, cat1.shape[1], res=x4, b_lead=0,
                    norm_gain=ffn2_norm[1].reshape(1, D))
    x6, _, sv_f21 = ffn_fwd("f2l1", x5, h_f21, gathered["ffn2_w_in"], gathered["ffn2_w_out"], 1, None)

    def head(rv, cv):
        def f(xx, gg):
            err = _rmsnorm(xx, gg) - rv[1]
            return 0.5 * jnp.sum(jnp.mean(err * err, axis=-1, keepdims=True), axis=0, keepdims=True)

        ls, vjp = jax.vjp(f, rv[0], cv[0])
        dx, dg = vjp(jnp.ones((1, 1), F32))
        return [dx], [dg, jnp.broadcast_to(ls, (1, 128))]

    dx6, d_final, loss_part = _rowcall("loss_head", head, [(x6, 0, D), (tgt, 0, D)], [final_norm.reshape(1, D)],
                                       [(D, F32)], [((1, D), F32), ((1, 128), F32)], TR)

    rs_out = {}
    n_gather = len(groups)

    def rs(gi, items):
        outs = _rs_chips_seq("reduce_%d" % gi, [p for (_, p, _) in items], [k for (_, _, k) in items], n_gather + gi)
        for i, (key, _, _) in enumerate(items):
            rs_out[key] = (outs[2 * i], outs[2 * i + 1])

    dx5, dg_f21, dwi_f21, dwo_f21 = ffn_bwd("f2l1", dx6, sv_f21, ffn2_norm[1], gathered["ffn2_w_in"], gathered["ffn2_w_out"], 1)
    rs(0, [(("ffn2_w_out", 1), dwo_f21, "row"), (("ffn2_w_in", 1), dwi_f21, "col")])
    dwo_m1 = _mm_tn_pair("mix_dwo_1", cat1, dx5, "row", c_arr, 1024, T)
    dz_m1, dkv1, d_lng, d_lnb, d_ws, d_bs = _gmlp_bwd(z_m1, dx5, gathered["gmlp_w_out"], ln_g_full, ln_b_full, w_s, b_s, kv[1], bl,
                                                      nc1)
    dx4, dg_m1 = _mm_dh_rms("mix_dh_1", dz_m1, gathered["gmlp_w_in"], x4, mix_norm[1].reshape(1, D), dx5, 512)
    dwi_m1 = _mm_tn_pair("mix_dwi_1", h_m1, dz_m1, "col", c_arr, 1024, T)
    rs(1, [(("gmlp_w_out", 0), dwo_m1, "row"), (("gmlp_w_in", 0), dwi_m1, "col")])
    dx3, dg_f11, dwi_f11, dwo_f11 = ffn_bwd("f1l1", dx4, sv_f11, ffn1_norm[1], gathered["ffn1_w_in"], gathered["ffn1_w_out"], 1)
    rs(2, [(("ffn1_w_out", 1), dwo_f11, "row"), (("ffn1_w_in", 1), dwi_f11, "col")])

    dx2, dg_f20, dwi_f20, dwo_f20 = ffn_bwd("f2l0", dx3, sv_f20, ffn2_norm[0], gathered["ffn2_w_in"], gathered["ffn2_w_out"], 0)
    rs(3, [(("ffn2_w_out", 0), dwo_f20, "row"), (("ffn2_w_in", 0), dwi_f20, "col")])
    dwo_m0 = _mm_tn_pair("mix_dwo_0", cat0, dx2, "row", c_arr, 1024, T)
    dz_m0, dkv0, d_lb, d_gn = _hgrn_bwd2(z_m0, dx2, gathered["hgrn_w_out"], stash0, lb_logits, hgrn_gnorm, kv[0], bl, seq)
    dx1, dg_m0 = _mm_dh_rms("mix_dh_0", dz_m0, gathered["hgrn_w_in"], x1, mix_norm[0].reshape(1, D), dx2, 512)
    dwi_m0 = _mm_tn_pair("mix_dwi_0", h_m0, dz_m0, "col", c_arr, 1024, T)
    rs(4, [(("hgrn_w_out", 0), dwo_m0, "row"), (("hgrn_w_in", 0), dwi_m0, "col")])

    dwkv = [_mm_tn_pair("kv_dw_%d" % i, memn, dkv, "col", c_arr, 1024, 512) for i, dkv in enumerate([dkv0, dkv1])]
    rs(5, [(("mem_w_kv", 0), dwkv[0], "col"), (("mem_w_kv", 1), dwkv[1], "col")])
    dmemn = _mm("kv_dx_0", dkv0, gathered["mem_w_kv"], "nt", F32, 512, 512, 1024, b_lead=0)
    dmemn = _mm("kv_dx_1", dkv1, gathered["mem_w_kv"], "nt", F32, 512, 512, 1024, res=dmemn, b_lead=1)
    _, d_memnorm = rms_bwd("rms_bwd_mem", mem2, mem_norm, dmemn, None)

    dx0, dg_f10, dwi_f10, dwo_f10 = ffn_bwd("f1l0", dx1, sv_f10, ffn1_norm[0], gathered["ffn1_w_in"], gathered["ffn1_w_out"], 0)
    rs(6, [(("ffn1_w_out", 0), dwo_f10, "row")])
    rs(7, [(("ffn1_w_in", 0), dwi_f10, "col")])

    shard_grads = [_finish_share("finish_" + nm, [rs_out[(nm, l)][0] for l in range(w.shape[0])],
                                 [rs_out[(nm, l)][1] for l in range(w.shape[0])], k, c_arr) for (nm, w, k) in big]

    big_w = [w for (_, w, _) in big]
    big_m = [m_ffn1_w_in, m_ffn1_w_out, m_mem_w_kv, m_hgrn_w_in, m_hgrn_w_out, m_gmlp_w_in, m_gmlp_w_out, m_ffn2_w_in, m_ffn2_w_out]
    big_v = [v_ffn1_w_in, v_ffn1_w_out, v_mem_w_kv, v_hgrn_w_in, v_hgrn_w_out, v_gmlp_w_in, v_gmlp_w_out, v_ffn2_w_in, v_ffn2_w_out]
    big_out = {}
    for (nm, w, _), g, m, v in zip(big, shard_grads, big_m, big_v):
        L, r, c = w.shape
        g2, d2, m2, v2 = _adam_call("adam_" + nm, w.reshape(L * r, c), g.reshape(L * r, c), m.reshape(L * r, c),
                                    v.reshape(L * r, c), 256, pass_grad=True)
        big_out[nm] = (g2.reshape(w.shape), d2.reshape(w.shape), m2.reshape(w.shape), v2.reshape(w.shape))

    d_ffn1n = _two_rows(dg_f10, dg_f11)
    d_mixn = _two_rows(dg_m0, dg_m1)
    d_ffn2n = _two_rows(dg_f20, dg_f21)
    small_parts = [loss_part[:, :1], d_memnorm, d_lb, d_ffn1n, d_mixn, d_gn, d_lng, d_lnb, d_ws, d_bs, d_ffn2n, d_final]
    red_shapes = [(1,), mem_norm.shape, lb_logits.shape, ffn1_norm.shape, mix_norm.shape, hgrn_gnorm.shape, (1, ln_w), (1, ln_w),
                  gmlp_w_s.shape, gmlp_b_s.shape, ffn2_norm.shape, final_norm.shape]
    red = _small_allreduce(_pack(small_parts, _rows_needed(red_shapes)), "reduce_small")
    (loss_v, g_memn, g_lb, g_f1n, g_mixn, g_gn, g_lng_full, g_lnb_full, g_ws, g_bs, g_f2n, g_fin) = _unpack(red, red_shapes)
    lsh = gmlp_ln_g.shape[1]
    g_lng = lax.dynamic_slice(g_lng_full, (0, chip * lsh), (1, lsh))
    g_lnb = lax.dynamic_slice(g_lnb_full, (0, chip * lsh), (1, lsh))
    small_w = [mem_norm, lb_logits, ffn1_norm, mix_norm, hgrn_gnorm, gmlp_ln_g, gmlp_ln_b, gmlp_w_s, gmlp_b_s, ffn2_norm, final_norm]
    small_g = [g_memn, g_lb, g_f1n, g_mixn, g_gn, g_lng, g_lnb, g_ws, g_bs, g_f2n, g_fin]
    small_m = [m_mem_norm, m_lb_logits, m_ffn1_norm, m_mix_norm, m_hgrn_gnorm, m_gmlp_ln_g, m_gmlp_ln_b, m_gmlp_w_s, m_gmlp_b_s, m_ffn2_norm, m_final_norm]
    small_v = [v_mem_norm, v_lb_logits, v_ffn1_norm, v_mix_norm, v_hgrn_gnorm, v_gmlp_ln_g, v_gmlp_ln_b, v_gmlp_w_s, v_gmlp_b_s, v_ffn2_norm, v_final_norm]
    sshapes = [w.shape for w in small_w]
    nrow = _rows_needed(sshapes)
    d_p, m_p, v_p = _adam_call("adam_small", _pack(small_w, nrow), _pack(small_g, nrow), _pack(small_m, nrow), _pack(small_v, nrow), nrow)
    s_delta, s_m, s_v = _unpack(d_p, sshapes), _unpack(m_p, sshapes), _unpack(v_p, sshapes)
    small_names = ["mem_norm", "lb_logits", "ffn1_norm", "mix_norm", "hgrn_gnorm", "gmlp_ln_g", "gmlp_ln_b", "gmlp_w_s", "gmlp_b_s", "ffn2_norm", "final_norm"]
    small_out = {nm: (g.reshape(w.shape), d, m, v) for nm, w, g, d, m, v in zip(small_names, small_w, small_g, s_delta, s_m, s_v)}

    order = ["mem_norm", "lb_logits", "ffn1_norm", "ffn1_w_in", "ffn1_w_out", "mix_norm", "mem_w_kv", "hgrn_w_in", "hgrn_gnorm",
             "hgrn_w_out", "gmlp_w_in", "gmlp_ln_g", "gmlp_ln_b", "gmlp_w_s", "gmlp_b_s", "gmlp_w_out", "ffn2_norm", "ffn2_w_in",
             "ffn2_w_out", "final_norm"]
    allo = {**big_out, **small_out}
    grad_x = dx0.reshape(x.shape)
    return (loss_v.reshape(()), grad_x, *[allo[n][0] for n in order], *[allo[n][1] for n in order],
            *[allo[n][2] for n in order], *[allo[n][3] for n in order])
```

```python
import functools

import jax
import jax.numpy as jnp
from jax import lax
from jax.experimental import pallas as pl
from jax.experimental.pallas import tpu as pltpu
from jax.experimental.pallas import tpu_sc as plsc

BF = jnp.bfloat16
F32 = jnp.float32
MESH = pl.DeviceIdType.MESH

EPS = 1e-6
D_MODEL = 1024
HG_HEADS = 8
HG_DIM = 128
HG_CHUNK = 64
GM_CHUNK = 128
GM_GROUPS = 8
GM_GROUP_DIM = 256
XA_HEADS = 4
XA_DIM = 256
ADAM_LR = 0.001
ADAM_B1 = 0.9
ADAM_B2 = 0.999
ADAM_EPS = 1e-08
ADAM_WD = 0.01
ADAM_STEP = 10

VMEM_CAP_BYTES = 60 * 1024 * 1024
LANES = 1024


def _pick(n, cap, mult=16):
    if n <= cap:
        return n
    for d in range(cap - cap % mult, 0, -mult):
        if n % d == 0:
            return d
    raise ValueError((n, cap, mult))


def _dg(a, b, ca, cb):
    return lax.dot_general(a.astype(BF), b.astype(BF), (((ca,), (cb,)), ((), ())), preferred_element_type=F32)


@jax.custom_vjp
def dot_nn(a, b):
    return _dg(a, b, 1, 0)


def _nn_fwd(a, b):
    return _dg(a, b, 1, 0), (a, b)


def _nn_bwd(r, g):
    a, b = r
    return _dg(g, b, 1, 1), _dg(a, g, 0, 0)


dot_nn.defvjp(_nn_fwd, _nn_bwd)


@jax.custom_vjp
def dot_nt(a, b):
    return _dg(a, b, 1, 1)


def _nt_fwd(a, b):
    return _dg(a, b, 1, 1), (a, b)


def _nt_bwd(r, g):
    a, b = r
    return _dg(g, b, 1, 0), _dg(g, a, 0, 0)


dot_nt.defvjp(_nt_fwd, _nt_bwd)


@jax.custom_vjp
def dot_tn(a, b):
    return _dg(a, b, 0, 0)


def _tn_fwd(a, b):
    return _dg(a, b, 0, 0), (a, b)


def _tn_bwd(r, g):
    a, b = r
    return _dg(b, g, 1, 1), _dg(a, g, 1, 0)


dot_tn.defvjp(_tn_fwd, _tn_bwd)


def _rmsnorm(x, g):
    return x * lax.rsqrt(jnp.mean(x * x, axis=-1, keepdims=True) + EPS) * g


def _silu(x):
    return x * jax.nn.sigmoid(x)


@jax.custom_vjp
def _gelu(x):
    return 0.5 * x * (1.0 + lax.erf(x * (0.5 ** 0.5)))


def _gelu_fwd(x):
    return _gelu(x), x


def _gelu_bwd(x, g):
    t = x * (0.5 ** 0.5)
    cdf = 0.5 * (1.0 + lax.erf(t))
    return (g * (cdf + x * (jnp.exp(-(t * t)) * (0.5 / 3.141592653589793) ** 0.5)),)


_gelu.defvjp(_gelu_fwd, _gelu_bwd)


def _softmax_last(s):
    m = lax.stop_gradient(jnp.max(s, axis=-1, keepdims=True))
    e = jnp.exp(s - m)
    return e / jnp.sum(e, axis=-1, keepdims=True)


def _tril(n):
    r = lax.broadcasted_iota(jnp.int32, (n, n), 0)
    c = lax.broadcasted_iota(jnp.int32, (n, n), 1)
    return r >= c


def _cumsum_rows(l):
    n = l.shape[0]
    return lax.dot_general(_tril(n).astype(F32), l, (((1,), (0,)), ((), ())),
                           precision=lax.Precision.HIGHEST, preferred_element_type=F32)


def _attention(zx, mk, mv):
    s = dot_nt(zx, mk) * (XA_DIM ** -0.5)
    return dot_nn(_softmax_last(s), mv)


def _hgrn_head(zq, zf, zi, zg, l0, l1, l2, gn, S):
    m = lax.stop_gradient(jnp.maximum(jnp.maximum(l0, l1), l2))
    e0 = jnp.exp(l0 - m)
    lb = e0 / (e0 + jnp.exp(l1 - m) + jnp.exp(l2 - m))
    q = _silu(zq)
    f = lb + (1.0 - lb) * jax.nn.sigmoid(zf)
    k = 1.0 - f
    b = _cumsum_rows(jnp.log(f))
    b_last = b[HG_CHUNK - 1:HG_CHUNK, :]
    q_dec = q * jnp.exp(b)
    k_inv = k * jnp.exp(-b)
    a = jnp.where(_tril(HG_CHUNK), dot_nt(q_dec, k_inv), 0.0)
    o = dot_nn(a, zi) + dot_nn(q_dec, S)
    S_new = jnp.exp(b_last).reshape(HG_DIM, 1) * S + dot_tn(k * jnp.exp(b_last - b), zi)
    o = _rmsnorm(o, gn) * _silu(zg)
    return o, S_new


def _gmlp_block(zu, zv, zx, lng, lnb, ws, bs, mk, mv):
    gv = [_gelu(v) for v in zv]
    width = GM_GROUPS * GM_GROUP_DIM
    mu = sum(jnp.sum(g, axis=-1, keepdims=True) for g in gv) / width
    xc = [g - mu for g in gv]
    var = sum(jnp.sum(c * c, axis=-1, keepdims=True) for c in xc) / width
    r = lax.rsqrt(var + EPS)
    outs = []
    for g in range(GM_GROUPS):
        v = xc[g] * r * lng[g] + lnb[g]
        w = jnp.where(_tril(GM_CHUNK), ws[g], 0.0)
        mixed = dot_nn(w, v) + bs[g].reshape(GM_CHUNK, 1)
        outs.append(_gelu(zu[g]) * mixed)
    for a in range(XA_HEADS):
        outs.append(_attention(zx[a], mk[a], mv[a]))
    return outs


def _rowcall(name, fn, rows, consts, row_outs, acc_outs, tr):
    nrows = rows[0][0].shape[0]
    tr = _pick(nrows, tr)
    n_r, n_c, n_ro, n_ao = len(rows), len(consts), len(row_outs), len(acc_outs)

    def kern(*refs):
        rv = [r[...] for r in refs[:n_r]]
        cv = [r[...] for r in refs[n_r:n_r + n_c]]
        ro_refs = refs[n_r + n_c:n_r + n_c + n_ro]
        ao_refs = refs[n_r + n_c + n_ro:]
        ro, ao = fn(rv, cv)
        for ref, v in zip(ro_refs, ro):
            ref[...] = v.astype(ref.dtype)
        if n_ao:
            @pl.when(pl.program_id(0) == 0)
            def _():
                for ref in ao_refs:
                    ref[...] = jnp.zeros(ref.shape, ref.dtype)

            for ref, v in zip(ao_refs, ao):
                ref[...] += v.astype(ref.dtype)

    in_specs = [pl.BlockSpec((tr, w), functools.partial(lambda i, cb: (i, cb), cb=cb)) for (_, cb, w) in rows]
    in_specs += [pl.BlockSpec(c.shape, lambda i: (0, 0)) for c in consts]
    out_specs = [pl.BlockSpec((tr, w), lambda i: (i, 0)) for (w, _) in row_outs]
    out_specs += [pl.BlockSpec(s, lambda i: (0, 0)) for (s, _) in acc_outs]
    out_shape = [jax.ShapeDtypeStruct((nrows, w), dt) for (w, dt) in row_outs]
    out_shape += [jax.ShapeDtypeStruct(s, dt) for (s, dt) in acc_outs]
    outs = pl.pallas_call(
        kern, grid=(nrows // tr,), in_specs=in_specs, out_specs=out_specs, out_shape=out_shape, name=name,
        compiler_params=pltpu.CompilerParams(dimension_semantics=("arbitrary",),
                                             vmem_limit_bytes=VMEM_CAP_BYTES),
    )(*[a for (a, _, _) in rows], *consts)
    return outs


def _mm(name, a, b, mode, out_dtype, tm, tn, tk, scale=1.0, res=None, a_lead=None, b_lead=None, norm_gain=None):
    ash = a.shape[-2:]
    bsh = b.shape[-2:]
    if mode == "nn":
        (M, K), (K2, N) = ash, bsh
    elif mode == "nt":
        (M, K), (N, K2) = ash, bsh
    else:
        (K, M), (K2, N) = ash, bsh
    assert K == K2, (name, a.shape, b.shape)
    tm, tn, tk = min(tm, M), min(tn, N), min(tk, K)
    assert M % tm == 0 and N % tn == 0 and K % tk == 0, (name, M, N, K, tm, tn, tk)
    nk = K // tk
    dims = {"nn": (1, 0), "nt": (1, 1), "tn": (0, 0)}[mode]

    def lead(spec_shape, index_fn, lead_idx):
        if lead_idx is None:
            return pl.BlockSpec(spec_shape, index_fn)
        return pl.BlockSpec((None,) + spec_shape, lambda i, j, k: (lead_idx,) + index_fn(i, j, k))

    if mode == "tn":
        a_spec = lead((tk, tm), lambda i, j, k: (k, i), a_lead)
    else:
        a_spec = lead((tm, tk), lambda i, j, k: (i, k), a_lead)
    if mode == "nt":
        b_spec = lead((tn, tk), lambda i, j, k: (j, k), b_lead)
    else:
        b_spec = lead((tk, tn), lambda i, j, k: (k, j), b_lead)
    o_spec = pl.BlockSpec((tm, tn), lambda i, j, k: (i, j))
    has_res = res is not None
    has_norm = norm_gain is not None
    assert not has_norm or tn == N

    def kern(*refs):
        a_ref, b_ref = refs[0], refs[1]
        pos = 2
        res_ref = gain_ref = h_ref = None
        if has_res:
            res_ref, pos = refs[pos], pos + 1
        if has_norm:
            gain_ref, pos = refs[pos], pos + 1
        o_ref, pos = refs[pos], pos + 1
        if has_norm:
            h_ref = refs[pos]
        acc_ref = refs[-1] if nk > 1 else None
        p = lax.dot_general(a_ref[...].astype(BF), b_ref[...].astype(BF), (((dims[0],), (dims[1],)), ((), ())),
                            preferred_element_type=F32)

        def finish(v):
            if scale != 1.0:
                v = v * scale
            if has_res:
                v = res_ref[...] + v
            o_ref[...] = v.astype(o_ref.dtype)
            if has_norm:
                h_ref[...] = _rmsnorm(v, gain_ref[...]).astype(h_ref.dtype)

        if nk == 1:
            finish(p)
        else:
            k = pl.program_id(2)

            @pl.when(k == 0)
            def _():
                acc_ref[...] = p

            @pl.when(k > 0)
            def _():
                acc_ref[...] += p

            @pl.when(k == nk - 1)
            def _():
                finish(acc_ref[...])

    ins = [a, b] + ([res] if has_res else []) + ([norm_gain] if has_norm else [])
    in_specs = [a_spec, b_spec] + ([o_spec] if has_res else [])
    in_specs += [pl.BlockSpec((1, N), lambda i, j, k: (0, 0))] if has_norm else []
    out_sd = jax.ShapeDtypeStruct((M, N), out_dtype)
    return pl.pallas_call(
        kern, grid=(M // tm, N // tn, nk), in_specs=in_specs,
        out_specs=[o_spec, o_spec] if has_norm else o_spec,
        out_shape=[out_sd, jax.ShapeDtypeStruct((M, N), BF)] if has_norm else out_sd,
        scratch_shapes=[pltpu.VMEM((tm, tn), F32)] if nk > 1 else [],
        name=name,
        compiler_params=pltpu.CompilerParams(dimension_semantics=("parallel", "parallel", "arbitrary"),
                                             vmem_limit_bytes=VMEM_CAP_BYTES),
    )(*ins)


def _ffn_in_swiglu(name, h, w3, tm, tn):
    T, D = h.shape
    dff = w3.shape[2] // 2
    tm = min(tm, T)
    assert T % tm == 0 and dff % tn == 0
    nj = dff // tn

    def kern(h_ref, wg_ref, wu_ref, zg_ref, zu_ref, a_ref):
        hb = h_ref[...]
        g = jnp.dot(hb, wg_ref[...], preferred_element_type=F32).astype(BF)
        u = jnp.dot(hb, wu_ref[...], preferred_element_type=F32).astype(BF)
        zg_ref[...] = g
        zu_ref[...] = u
        a_ref[...] = (_silu(g.astype(F32)) * u.astype(F32)).astype(BF)

    o_spec = pl.BlockSpec((tm, tn), lambda i, j: (i, j))
    return pl.pallas_call(
        kern, grid=(T // tm, nj),
        in_specs=[pl.BlockSpec((tm, D), lambda i, j: (i, 0)),
                  pl.BlockSpec((None, D, tn), lambda i, j: (0, 0, j)),
                  pl.BlockSpec((None, D, tn), lambda i, j: (0, 0, j + nj))],
        out_specs=[o_spec, o_spec, o_spec],
        out_shape=[jax.ShapeDtypeStruct((T, dff), BF)] * 3, name=name,
        compiler_params=pltpu.CompilerParams(dimension_semantics=("parallel", "arbitrary"),
                                             vmem_limit_bytes=VMEM_CAP_BYTES),
    )(h, w3, w3)


def _ffn_da_swiglu(name, dxo, w3, zg, zu, tm):
    T, D = dxo.shape
    dff = w3.shape[1]
    tm = min(tm, T)
    assert T % tm == 0 and dff % 2 == 0
    hc = dff // 2

    def kern(d_ref, w_ref, g_ref, u_ref, dz_ref):
        db = (d_ref[...] * 0.5).astype(BF)
        for s in range(2):
            cols = slice(s * hc, (s + 1) * hc)
            da = lax.dot_general(db, w_ref[cols, :], (((1,), (1,)), ((), ())), preferred_element_type=F32)
            g = g_ref[:, cols].astype(F32)
            sg = 1.0 / (1.0 + jnp.exp(-g))
            gs = g * sg
            dab = da.astype(BF)
            dz_ref[:, cols] = (dab * u_ref[:, cols]) * (sg + gs * (1.0 - sg)).astype(BF)
            dz_ref[:, dff + s * hc:dff + (s + 1) * hc] = dab * gs.astype(BF)

    row = lambda w: pl.BlockSpec((tm, w), lambda i: (i, 0))
    return pl.pallas_call(
        kern, grid=(T // tm,),
        in_specs=[row(D), pl.BlockSpec((None, dff, D), lambda i: (0, 0, 0), pipeline_mode=pl.Buffered(1)), row(dff), row(dff)],
        out_specs=row(2 * dff), out_shape=jax.ShapeDtypeStruct((T, 2 * dff), BF), name=name,
        compiler_params=pltpu.CompilerParams(dimension_semantics=("arbitrary",), vmem_limit_bytes=VMEM_CAP_BYTES),
    )(dxo, w3, zg, zu)


def _mm_dh_rms(name, dz, w3, xin, g, dres, tm):
    T, K = dz.shape
    D = w3.shape[1]
    tm = min(tm, T)
    assert T % tm == 0

    def kern(dz_ref, w_ref, x_ref, g_ref, r_ref, dx_ref, dg_ref):
        dh = lax.dot_general(dz_ref[...], w_ref[...], (((1,), (1,)), ((), ())), preferred_element_type=F32)
        _, vjp = jax.vjp(_rmsnorm, x_ref[...], g_ref[...])
        dx, dg = vjp(dh)
        dx_ref[...] = dx + r_ref[...]

        @pl.when(pl.program_id(0) == 0)
        def _():
            dg_ref[...] = jnp.zeros(dg_ref.shape, F32)

        dg_ref[...] += dg

    row = lambda w: pl.BlockSpec((tm, w), lambda i: (i, 0))
    one = pl.BlockSpec((1, D), lambda i: (0, 0))
    return pl.pallas_call(
        kern, grid=(T // tm,),
        in_specs=[row(K), pl.BlockSpec((None, D, K), lambda i: (0, 0, 0), pipeline_mode=pl.Buffered(1)), row(D), one, row(D)],
        out_specs=[row(D), one], out_shape=[jax.ShapeDtypeStruct((T, D), F32), jax.ShapeDtypeStruct((1, D), F32)], name=name,
        compiler_params=pltpu.CompilerParams(dimension_semantics=("arbitrary",), vmem_limit_bytes=VMEM_CAP_BYTES),
    )(dz, w3, xin, g, dres)


def _mm_tn_pair(name, a, b, kind, c_arr, tq, tk, scale=1.0):
    T, M = a.shape
    _, N = b.shape
    tk = min(tk, T)
    assert T % tk == 0
    nk = T // tk
    if kind == "col":
        hm = M // 2
        assert N % tq == 0
        nq = N // tq
        tile = (hm, tq)
        a_spec = pl.BlockSpec((tk, hm), lambda h, q, k, c: (k, jnp.bitwise_xor(h, 1 - c[0])))
        b_spec = pl.BlockSpec((tk, tq), lambda h, q, k, c: (k, q))
        o_spec = pl.BlockSpec(tile, lambda h, q, k, c: (0, q * h))
        out_sd = (hm, N)
    else:
        hn = N // 2
        assert M % tq == 0
        nq = M // tq
        tile = (tq, hn)
        a_spec = pl.BlockSpec((tk, tq), lambda h, q, k, c: (k, q))
        b_spec = pl.BlockSpec((tk, hn), lambda h, q, k, c: (k, jnp.bitwise_xor(h, 1 - c[0])))
        o_spec = pl.BlockSpec(tile, lambda h, q, k, c: (q * h, 0))
        out_sd = (M, hn)

    def kern(c_ref, a_ref, b_ref, o_ref, acc, stage, recv, ssem, rsem):
        h, q, k = pl.program_id(0), pl.program_id(1), pl.program_id(2)
        x, y, c, _ = _place()
        p = lax.dot_general(a_ref[...].astype(BF), b_ref[...].astype(BF), (((0,), (0,)), ((), ())), preferred_element_type=F32)

        @pl.when(k == 0)
        def _():
            acc[...] = p

        @pl.when(k > 0)
        def _():
            acc[...] += p

        def send(slot, qq):
            return pltpu.make_async_remote_copy(src_ref=stage.at[slot], dst_ref=recv.at[qq], send_sem=ssem.at[slot],
                                                recv_sem=rsem.at[qq], device_id=(x, y, 1 - c), device_id_type=MESH)

        last = k == nk - 1

        @pl.when(jnp.logical_and(last, h == 0))
        def _():
            slot = q % 2

            @pl.when(q >= 2)
            def _():
                send(slot, q).wait_send()

            stage[slot] = (acc[...] * scale).astype(BF)
            send(slot, q).start()

        @pl.when(jnp.logical_and(last, h == 1))
        def _():
            @pl.when(q == 0)
            def _():
                for s in range(min(nq, 2)):
                    send(s, 0).wait_send()

            send(0, q).wait_recv()
            o_ref[...] = (acc[...] * scale + recv[q].astype(F32)).astype(o_ref.dtype)

    return pl.pallas_call(
        kern,
        grid_spec=pltpu.PrefetchScalarGridSpec(
            num_scalar_prefetch=1, grid=(2, nq, nk), in_specs=[a_spec, b_spec], out_specs=o_spec,
            scratch_shapes=[pltpu.VMEM(tile, F32), pltpu.VMEM((2,) + tile, BF), pltpu.VMEM((nq,) + tile, BF),
                            pltpu.SemaphoreType.DMA((2,)), pltpu.SemaphoreType.DMA((nq,))]),
        out_shape=jax.ShapeDtypeStruct(out_sd, BF), name=name,
        compiler_params=pltpu.CompilerParams(dimension_semantics=("arbitrary", "arbitrary", "arbitrary"),
                                             vmem_limit_bytes=VMEM_CAP_BYTES),
    )(c_arr, a, b)


def _kv_pieces(kv_ref):
    W = XA_HEADS * XA_DIM
    mk = [kv_ref[:, a * XA_DIM:(a + 1) * XA_DIM] for a in range(XA_HEADS)]
    mv = [kv_ref[:, W + a * XA_DIM:W + (a + 1) * XA_DIM] for a in range(XA_HEADS)]
    return mk, mv


def _lb_pieces(lb_ref):
    return [[lb_ref[r:r + 1, h * HG_DIM:(h + 1) * HG_DIM] for h in range(HG_HEADS)] for r in range(3)]


HG_SUB = 4


def _hgrn_rows(z_ref):
    W = HG_HEADS * HG_DIM

    def piece(c, col, w):
        return z_ref[c * HG_CHUNK:(c + 1) * HG_CHUNK, col:col + w]

    zq = [[piece(c, h * HG_DIM, HG_DIM) for h in range(HG_HEADS)] for c in range(HG_SUB)]
    zf = [[piece(c, W + h * HG_DIM, HG_DIM) for h in range(HG_HEADS)] for c in range(HG_SUB)]
    zi = [[piece(c, 2 * W + h * HG_DIM, HG_DIM) for h in range(HG_HEADS)] for c in range(HG_SUB)]
    zg = [[piece(c, 3 * W + h * HG_DIM, HG_DIM) for h in range(HG_HEADS)] for c in range(HG_SUB)]
    zx = [z_ref[:, 4 * W + a * XA_DIM:4 * W + (a + 1) * XA_DIM] for a in range(XA_HEADS)]
    return zq, zf, zi, zg, zx


def _hgrn_steps(zq, zf, zi, zg, zx, l0, l1, l2, gn, mk, mv, S):
    mix = []
    for c in range(HG_SUB):
        row, s_next = [], []
        for h in range(HG_HEADS):
            o, sn = _hgrn_head(zq[c][h], zf[c][h], zi[c][h], zg[c][h], l0[h], l1[h], l2[h], gn, S[h])
            row.append(o)
            s_next.append(sn)
        mix.append(row)
        S = s_next
    att = [_attention(zx[a], mk[a], mv[a]) for a in range(XA_HEADS)]
    return mix, att, S


def _hgrn_fwd2(z, lb_logits, gnorm, kv, bl, seq):
    T, zw = z.shape
    mem_len = kv.shape[0] // bl
    cat_w = HG_HEADS * HG_DIM + XA_HEADS * XA_DIM
    R = HG_SUB * HG_CHUNK
    nb = seq // R

    def kern(z_ref, lb_ref, gn_ref, kv_ref, cat_ref, st_ref, s_scr):
        @pl.when(pl.program_id(1) == 0)
        def _():
            s_scr[...] = jnp.zeros(s_scr.shape, F32)

        st_ref[...] = s_scr[...]
        zq, zf, zi, zg, zx = _hgrn_rows(z_ref)
        mk, mv = _kv_pieces(kv_ref)
        l0, l1, l2 = _lb_pieces(lb_ref)
        S = [s_scr[h] for h in range(HG_HEADS)]
        mix, att, s_new = _hgrn_steps(zq, zf, zi, zg, zx, l0, l1, l2, gn_ref[...], mk, mv, S)
        for c in range(HG_SUB):
            for h in range(HG_HEADS):
                cat_ref[c * HG_CHUNK:(c + 1) * HG_CHUNK, h * HG_DIM:(h + 1) * HG_DIM] = mix[c][h].astype(cat_ref.dtype)
        for h in range(HG_HEADS):
            s_scr[h] = s_new[h]
        base = HG_HEADS * HG_DIM
        for a in range(XA_HEADS):
            cat_ref[:, base + a * XA_DIM:base + (a + 1) * XA_DIM] = att[a].astype(cat_ref.dtype)

    return pl.pallas_call(
        kern, grid=(bl, nb),
        in_specs=[pl.BlockSpec((R, zw), lambda b, n: (b * nb + n, 0)),
                  pl.BlockSpec(lb_logits.shape, lambda b, n: (0, 0)),
                  pl.BlockSpec(gnorm.shape, lambda b, n: (0, 0)),
                  pl.BlockSpec((mem_len, kv.shape[1]), lambda b, n: (b, 0))],
        out_specs=[pl.BlockSpec((R, cat_w), lambda b, n: (b * nb + n, 0)),
                   pl.BlockSpec((None, HG_HEADS, HG_DIM, HG_DIM), lambda b, n: (b * nb + n, 0, 0, 0))],
        out_shape=[jax.ShapeDtypeStruct((T, cat_w), BF),
                   jax.ShapeDtypeStruct((bl * nb, HG_HEADS, HG_DIM, HG_DIM), F32)],
        scratch_shapes=[pltpu.VMEM((HG_HEADS, HG_DIM, HG_DIM), F32)],
        name="hgrn_fwd",
        compiler_params=pltpu.CompilerParams(dimension_semantics=("arbitrary", "arbitrary"), vmem_limit_bytes=VMEM_CAP_BYTES),
    )(z, lb_logits, gnorm, kv)


def _hgrn_bwd2(z, dcat, stash, lb_logits, gnorm, kv, bl, seq):
    T, zw = z.shape
    mem_len = kv.shape[0] // bl
    cat_w = dcat.shape[1]
    R = HG_SUB * HG_CHUNK
    nb = seq // R

    def kern(z_ref, dc_ref, st_ref, lb_ref, gn_ref, kv_ref, dz_ref, dkv_ref, dlb_ref, dgn_ref, ds_scr):
        first = jnp.logical_and(pl.program_id(0) == 0, pl.program_id(1) == 0)

        @pl.when(pl.program_id(1) == 0)
        def _():
            ds_scr[...] = jnp.zeros(ds_scr.shape, F32)
            dkv_ref[...] = jnp.zeros(dkv_ref.shape, F32)

        @pl.when(first)
        def _():
            dlb_ref[...] = jnp.zeros(dlb_ref.shape, F32)
            dgn_ref[...] = jnp.zeros(dgn_ref.shape, F32)

        zq, zf, zi, zg, zx = _hgrn_rows(z_ref)
        mk, mv = _kv_pieces(kv_ref)
        l0, l1, l2 = _lb_pieces(lb_ref)
        S = [st_ref[h] for h in range(HG_HEADS)]
        _, vjp = jax.vjp(_hgrn_steps, zq, zf, zi, zg, zx, l0, l1, l2, gn_ref[...], mk, mv, S)
        d_mix = [[dc_ref[c * HG_CHUNK:(c + 1) * HG_CHUNK, h * HG_DIM:(h + 1) * HG_DIM] for h in range(HG_HEADS)]
                 for c in range(HG_SUB)]
        base = HG_HEADS * HG_DIM
        d_att = [dc_ref[:, base + a * XA_DIM:base + (a + 1) * XA_DIM] for a in range(XA_HEADS)]
        d_s = [ds_scr[h] for h in range(HG_HEADS)]
        dzq, dzf, dzi, dzg, dzx, dl0, dl1, dl2, dgn, dmk, dmv, dS = vjp((d_mix, d_att, d_s))
        W = HG_HEADS * HG_DIM
        for c in range(HG_SUB):
            rows = slice(c * HG_CHUNK, (c + 1) * HG_CHUNK)
            for h in range(HG_HEADS):
                for k, part in enumerate((dzq, dzf, dzi, dzg)):
                    dz_ref[rows, k * W + h * HG_DIM:k * W + (h + 1) * HG_DIM] = part[c][h].astype(dz_ref.dtype)
        for h in range(HG_HEADS):
            sl = slice(h * HG_DIM, (h + 1) * HG_DIM)
            ds_scr[h] = dS[h]
            dlb_ref[0:1, sl] += dl0[h]
            dlb_ref[1:2, sl] += dl1[h]
            dlb_ref[2:3, sl] += dl2[h]
        dgn_ref[...] += dgn
        KW = XA_HEADS * XA_DIM
        for a in range(XA_HEADS):
            dz_ref[:, 4 * W + a * XA_DIM:4 * W + (a + 1) * XA_DIM] = dzx[a].astype(dz_ref.dtype)
            dkv_ref[:, a * XA_DIM:(a + 1) * XA_DIM] += dmk[a]
            dkv_ref[:, KW + a * XA_DIM:KW + (a + 1) * XA_DIM] += dmv[a]

    rev = lambda b, n: (b * nb + (nb - 1 - n), 0)
    return pl.pallas_call(
        kern, grid=(bl, nb),
        in_specs=[pl.BlockSpec((R, zw), rev),
                  pl.BlockSpec((R, cat_w), rev),
                  pl.BlockSpec((None, HG_HEADS, HG_DIM, HG_DIM), lambda b, n: (b * nb + (nb - 1 - n), 0, 0, 0)),
                  pl.BlockSpec(lb_logits.shape, lambda b, n: (0, 0)),
                  pl.BlockSpec(gnorm.shape, lambda b, n: (0, 0)),
                  pl.BlockSpec((mem_len, kv.shape[1]), lambda b, n: (b, 0))],
        out_specs=[pl.BlockSpec((R, zw), rev),
                   pl.BlockSpec((mem_len, kv.shape[1]), lambda b, n: (b, 0)),
                   pl.BlockSpec(lb_logits.shape, lambda b, n: (0, 0)),
                   pl.BlockSpec(gnorm.shape, lambda b, n: (0, 0))],
        out_shape=[jax.ShapeDtypeStruct((T, zw), BF), jax.ShapeDtypeStruct(kv.shape, F32),
                   jax.ShapeDtypeStruct(lb_logits.shape, F32), jax.ShapeDtypeStruct(gnorm.shape, F32)],
        scratch_shapes=[pltpu.VMEM((HG_HEADS, HG_DIM, HG_DIM), F32)],
        name="hgrn_bwd",
        compiler_params=pltpu.CompilerParams(dimension_semantics=("arbitrary", "arbitrary"), vmem_limit_bytes=VMEM_CAP_BYTES),
    )(z, dcat, stash, lb_logits, gnorm, kv)


GM_SUB = 2


def _gmlp_pieces(z_ref):
    W = GM_GROUPS * GM_GROUP_DIM
    zu = [z_ref[:, g * GM_GROUP_DIM:(g + 1) * GM_GROUP_DIM] for g in range(GM_GROUPS)]
    zv = [z_ref[:, W + g * GM_GROUP_DIM:W + (g + 1) * GM_GROUP_DIM] for g in range(GM_GROUPS)]
    zx = [z_ref[:, 2 * W + a * XA_DIM:2 * W + (a + 1) * XA_DIM] for a in range(XA_HEADS)]
    return zu, zv, zx


def _gmlp_params(lng_ref, lnb_ref, ws_ref, bs_ref):
    lng = [lng_ref[:, g * GM_GROUP_DIM:(g + 1) * GM_GROUP_DIM] for g in range(GM_GROUPS)]
    lnb = [lnb_ref[:, g * GM_GROUP_DIM:(g + 1) * GM_GROUP_DIM] for g in range(GM_GROUPS)]
    ws = [ws_ref[g] for g in range(GM_GROUPS)]
    bs = [bs_ref[g:g + 1, :] for g in range(GM_GROUPS)]
    return lng, lnb, ws, bs


def _gmlp_fwd(z, ln_g, ln_b, w_s, b_s, kv, bl, nc):
    T, zw = z.shape
    mem_len = kv.shape[0] // bl
    cat_w = GM_GROUPS * GM_GROUP_DIM + XA_HEADS * XA_DIM

    assert nc % GM_SUB == 0
    nc = nc // GM_SUB
    R = GM_SUB * GM_CHUNK

    def kern(z_ref, lng_ref, lnb_ref, ws_ref, bs_ref, kv_ref, cat_ref):
        lng, lnb, ws, bs = _gmlp_params(lng_ref, lnb_ref, ws_ref, bs_ref)
        mk, mv = _kv_pieces(kv_ref)
        for c in range(GM_SUB):
            rows = pl.ds(c * GM_CHUNK, GM_CHUNK)
            zu, zv, zx = _gmlp_pieces(z_ref.at[rows])
            out = cat_ref.at[rows]
            outs = _gmlp_block(zu, zv, zx, lng, lnb, ws, bs, mk, mv)
            for g in range(GM_GROUPS):
                out[:, g * GM_GROUP_DIM:(g + 1) * GM_GROUP_DIM] = outs[g].astype(cat_ref.dtype)
            base = GM_GROUPS * GM_GROUP_DIM
            for a in range(XA_HEADS):
                out[:, base + a * XA_DIM:base + (a + 1) * XA_DIM] = outs[GM_GROUPS + a].astype(cat_ref.dtype)

    full2 = lambda b, n: (0, 0)
    return pl.pallas_call(
        kern, grid=(bl, nc),
        in_specs=[pl.BlockSpec((R, zw), lambda b, n: (b * nc + n, 0)),
                  pl.BlockSpec(ln_g.shape, full2), pl.BlockSpec(ln_b.shape, full2),
                  pl.BlockSpec(w_s.shape, lambda b, n: (0, 0, 0)), pl.BlockSpec(b_s.shape, full2),
                  pl.BlockSpec((mem_len, kv.shape[1]), lambda b, n: (b, 0))],
        out_specs=pl.BlockSpec((R, cat_w), lambda b, n: (b * nc + n, 0)),
        out_shape=jax.ShapeDtypeStruct((T, cat_w), BF),
        name="gmlp_fwd",
        compiler_params=pltpu.CompilerParams(dimension_semantics=("arbitrary", "arbitrary"), vmem_limit_bytes=VMEM_CAP_BYTES),
    )(z, ln_g, ln_b, w_s, b_s, kv)


def _gmlp_bwd(z, dcat, ln_g, ln_b, w_s, b_s, kv, bl, nc):
    T, zw = z.shape
    mem_len = kv.shape[0] // bl
    cat_w = dcat.shape[1]
    assert nc % GM_SUB == 0
    nc = nc // GM_SUB

    def kern(z_ref, dc_ref, lng_ref, lnb_ref, ws_ref, bs_ref, kv_ref,
             dz_ref, dkv_ref, dlng_ref, dlnb_ref, dws_ref, dbs_ref):
        first = jnp.logical_and(pl.program_id(0) == 0, pl.program_id(1) == 0)

        @pl.when(pl.program_id(1) == 0)
        def _():
            dkv_ref[...] = jnp.zeros(dkv_ref.shape, F32)

        @pl.when(first)
        def _():
            dlng_ref[...] = jnp.zeros(dlng_ref.shape, F32)
            dlnb_ref[...] = jnp.zeros(dlnb_ref.shape, F32)
            dws_ref[...] = jnp.zeros(dws_ref.shape, F32)
            dbs_ref[...] = jnp.zeros(dbs_ref.shape, F32)

        lng, lnb, ws, bs = _gmlp_params(lng_ref, lnb_ref, ws_ref, bs_ref)
        mk, mv = _kv_pieces(kv_ref)
        W = GM_GROUPS * GM_GROUP_DIM
        KW = XA_HEADS * XA_DIM
        for c in range(GM_SUB):
            rows = pl.ds(c * GM_CHUNK, GM_CHUNK)
            zu, zv, zx = _gmlp_pieces(z_ref.at[rows])
            dc, dz = dc_ref.at[rows], dz_ref.at[rows]
            _, vjp = jax.vjp(_gmlp_block, zu, zv, zx, lng, lnb, ws, bs, mk, mv)
            d_outs = [dc[:, g * GM_GROUP_DIM:(g + 1) * GM_GROUP_DIM] for g in range(GM_GROUPS)]
            d_outs += [dc[:, W + a * XA_DIM:W + (a + 1) * XA_DIM] for a in range(XA_HEADS)]
            dzu, dzv, dzx, dlng, dlnb, dws, dbs, dmk, dmv = vjp(d_outs)
            for g in range(GM_GROUPS):
                sl = slice(g * GM_GROUP_DIM, (g + 1) * GM_GROUP_DIM)
                dz[:, sl] = dzu[g].astype(dz_ref.dtype)
                dz[:, W + g * GM_GROUP_DIM:W + (g + 1) * GM_GROUP_DIM] = dzv[g].astype(dz_ref.dtype)
                dlng_ref[:, sl] += dlng[g]
                dlnb_ref[:, sl] += dlnb[g]
                dws_ref[g] += dws[g]
                dbs_ref[g:g + 1, :] += dbs[g]
            for a in range(XA_HEADS):
                dz[:, 2 * W + a * XA_DIM:2 * W + (a + 1) * XA_DIM] = dzx[a].astype(dz_ref.dtype)
                dkv_ref[:, a * XA_DIM:(a + 1) * XA_DIM] += dmk[a]
                dkv_ref[:, KW + a * XA_DIM:KW + (a + 1) * XA_DIM] += dmv[a]

    full2 = lambda b, n: (0, 0)
    full3 = lambda b, n: (0, 0, 0)
    blk = lambda b, n: (b * nc + n, 0)
    return pl.pallas_call(
        kern, grid=(bl, nc),
        in_specs=[pl.BlockSpec((GM_SUB * GM_CHUNK, zw), blk), pl.BlockSpec((GM_SUB * GM_CHUNK, cat_w), blk),
                  pl.BlockSpec(ln_g.shape, full2), pl.BlockSpec(ln_b.shape, full2),
                  pl.BlockSpec(w_s.shape, full3), pl.BlockSpec(b_s.shape, full2),
                  pl.BlockSpec((mem_len, kv.shape[1]), lambda b, n: (b, 0))],
        out_specs=[pl.BlockSpec((GM_SUB * GM_CHUNK, zw), blk),
                   pl.BlockSpec((mem_len, kv.shape[1]), lambda b, n: (b, 0)),
                   pl.BlockSpec(ln_g.shape, full2), pl.BlockSpec(ln_b.shape, full2),
                   pl.BlockSpec(w_s.shape, full3), pl.BlockSpec(b_s.shape, full2)],
        out_shape=[jax.ShapeDtypeStruct((T, zw), BF), jax.ShapeDtypeStruct(kv.shape, F32),
                   jax.ShapeDtypeStruct(ln_g.shape, F32), jax.ShapeDtypeStruct(ln_b.shape, F32),
                   jax.ShapeDtypeStruct(w_s.shape, F32), jax.ShapeDtypeStruct(b_s.shape, F32)],
        name="gmlp_bwd",
        compiler_params=pltpu.CompilerParams(dimension_semantics=("arbitrary", "arbitrary"), vmem_limit_bytes=VMEM_CAP_BYTES),
    )(z, dcat, ln_g, ln_b, w_s, b_s, kv)


def _place():
    x, y, c = lax.axis_index("x"), lax.axis_index("y"), lax.axis_index("c")
    chips = [(1 - x, y), (x, 1 - y), (1 - x, 1 - y)]
    return x, y, c, chips


def _half(ref, kind, e):
    if kind == "col":
        n = ref.shape[1] // 2
        return ref.at[:, pl.ds(pl.multiple_of(e * n, n), n), :]
    n = ref.shape[2] // 2
    return ref.at[:, :, pl.ds(pl.multiple_of(e * n, n), n)]


def _slot(ref, kind, j, n):
    if kind == "col":
        return ref.at[:, :, pl.ds(pl.multiple_of(j * n, n), n)]
    return ref.at[:, pl.ds(pl.multiple_of(j * n, n), n), :]


BF16_TILE_ROWS = 16
AG_DIRECT_SIXTEENTHS = 3


def _allgather_seq(name, items, cid):
    nt = len(items)
    kinds = [k for (_, k, _) in items]
    slot_kind = ["row" if k == "row" else "col" for k in kinds]
    out_type = []
    for s, k, l in items:
        L, r, c = s.shape
        lo = L if l is None else 1
        out_type.append(jax.ShapeDtypeStruct((lo, 4 * r, c) if k == "row" else (lo, r, 4 * c), s.dtype))

    def part(ref, t, e):
        return ref if kinds[t] == "vec" else _half(ref, kinds[t], e)

    def split(half):
        rows = half.shape[1]
        direct = rows * AG_DIRECT_SIXTEENTHS // 16 // BF16_TILE_ROWS * BF16_TILE_ROWS
        return half.at[:, pl.ds(0, rows - direct), :], half.at[:, pl.ds(rows - direct, direct), :]

    def body(*refs):
        sh = [refs[t] if items[t][2] is None else refs[t].at[pl.ds(items[t][2], 1)] for t in range(nt)]
        full = refs[nt:2 * nt]
        loc, s_ici, r_ici, s_far, r_far, s_d2d, r_d2d = refs[2 * nt:]
        x, y, c, chips = _place()
        own = 2 * x + y
        sibling = (x, y, 1 - c)
        barrier = pltpu.get_barrier_semaphore()
        for peer in [(px, py, pc) for (px, py) in chips for pc in (0, 1)] + [sibling]:
            pl.semaphore_signal(barrier, inc=1, device_id=peer, device_id_type=MESH)
        pl.semaphore_wait(barrier, 7)
        width = [sh[t].shape[1] if kinds[t] == "row" else sh[t].shape[2] for t in range(nt)]
        started = []
        for t in range(nt):
            mine = pltpu.make_async_copy(sh[t], _slot(full[t], slot_kind[t], own, width[t]), loc.at[t])
            mine.start()
            started.append(mine)
        sent = []
        for t in range(nt):
            for p, (px, py) in enumerate(chips):
                src, dst = part(sh[t], t, c), part(_slot(full[t], slot_kind[t], own, width[t]), t, c)
                cp = pltpu.make_async_remote_copy(
                    src_ref=src, dst_ref=dst, send_sem=s_ici.at[t, p], recv_sem=r_ici.at[t, p], device_id=(px, py, c),
                    device_id_type=MESH)
                cp.start()
                sent.append(cp)
                if kinds[t] == "vec":
                    continue
                far = pltpu.make_async_remote_copy(
                    src_ref=split(src)[1], dst_ref=split(dst)[1], send_sem=s_far.at[t, p], recv_sem=r_far.at[t, p],
                    device_id=(px, py, 1 - c), device_id_type=MESH)
                far.start()
                sent.append(far)
        for t in range(nt):
            for p, (px, py) in enumerate(chips):
                landed = part(_slot(full[t], slot_kind[t], 2 * px + py, width[t]), t, c)
                pltpu.make_async_remote_copy(
                    src_ref=landed, dst_ref=landed, send_sem=s_ici.at[t, p], recv_sem=r_ici.at[t, p],
                    device_id=(px, py, c), device_id_type=MESH).wait_recv()
                if kinds[t] == "vec":
                    continue
                fw = pltpu.make_async_remote_copy(
                    src_ref=split(landed)[0], dst_ref=split(landed)[0], send_sem=s_d2d.at[t, p], recv_sem=r_d2d.at[t, p],
                    device_id=sibling, device_id_type=MESH)
                fw.start()
                sent.append(fw)
        for t in range(nt):
            if kinds[t] == "vec":
                continue
            for p, (px, py) in enumerate(chips):
                forwarded, direct = split(_half(_slot(full[t], kinds[t], 2 * px + py, width[t]), kinds[t], 1 - c))
                pltpu.make_async_remote_copy(
                    src_ref=forwarded, dst_ref=forwarded, send_sem=s_d2d.at[t, p], recv_sem=r_d2d.at[t, p],
                    device_id=sibling, device_id_type=MESH).wait_recv()
                pltpu.make_async_remote_copy(
                    src_ref=direct, dst_ref=direct, send_sem=s_far.at[t, p], recv_sem=r_far.at[t, p],
                    device_id=(px, py, 1 - c), device_id_type=MESH).wait_recv()
        for cp in sent:
            cp.wait_send()
        for cp in started:
            cp.wait()

    sems = pltpu.SemaphoreType.DMA
    return pl.kernel(
        body, out_type=out_type, mesh=plsc.ScalarSubcoreMesh(axis_name="seq", num_cores=1),
        scratch_types=[sems((nt,)), sems((nt, 3)), sems((nt, 3)), sems((nt, 3)), sems((nt, 3)), sems((nt, 3)), sems((nt, 3))],
        compiler_params=pltpu.CompilerParams(collective_id=cid), name=name,
    )(*[s for (s, _, _) in items])


def _slot2(ref, kind, j, n):
    if kind == "col":
        return ref.at[:, pl.ds(pl.multiple_of(j * n, n), n)]
    return ref.at[pl.ds(pl.multiple_of(j * n, n), n), :]


def _rs_chips_seq(name, parts, kinds, cid):
    nm = len(parts)
    out_type = []
    for g, k in zip(parts, kinds):
        r, c = g.shape
        ps = (r, c // 4) if k == "col" else (r // 4, c)
        out_type += [jax.ShapeDtypeStruct(ps, BF), jax.ShapeDtypeStruct((3,) + ps, BF)]

    def body(*refs):
        g = refs[:nm]
        outs = refs[nm:3 * nm]
        loc, ssem, rsem = refs[3 * nm:]
        x, y, c, chips = _place()
        own = 2 * x + y
        barrier = pltpu.get_barrier_semaphore()
        for (px, py) in chips:
            pl.semaphore_signal(barrier, inc=1, device_id=(px, py, c), device_id_type=MESH)
        pl.semaphore_wait(barrier, 3)
        cps = []
        for m in range(nm):
            k = kinds[m]
            own_o, got_o = outs[2 * m], outs[2 * m + 1]
            n = g[m].shape[1] // 4 if k == "col" else g[m].shape[0] // 4
            lc = pltpu.make_async_copy(_slot2(g[m], k, own, n), own_o, loc.at[m])
            lc.start()
            cps.append(lc)
            for p, (px, py) in enumerate(chips):
                cp = pltpu.make_async_remote_copy(
                    src_ref=_slot2(g[m], k, 2 * px + py, n), dst_ref=got_o.at[p],
                    send_sem=ssem.at[m, p], recv_sem=rsem.at[m, p], device_id=(px, py, c), device_id_type=MESH)
                cp.start()
                cps.append(cp)
        for cp in cps:
            cp.wait()

    return pl.kernel(
        body, out_type=out_type, mesh=plsc.ScalarSubcoreMesh(axis_name="seq", num_cores=1),
        scratch_types=[pltpu.SemaphoreType.DMA((nm,)), pltpu.SemaphoreType.DMA((nm, 3)), pltpu.SemaphoreType.DMA((nm, 3))],
        compiler_params=pltpu.CompilerParams(collective_id=cid), name=name,
    )(*parts)


def _finish_share(name, owns, gots, kind, c_arr):
    L = len(owns)
    r, c = owns[0].shape
    tr = _pick(r, 128 if kind == "col" else 256)
    nb = r // tr
    nq = L * nb

    def chunk_of(l):
        return lambda h, q: jnp.clip(q * (1 - h) + (nq - 1) * h - l * nb, 0, nb - 1)

    ins, in_specs = [], []
    for l in range(L):
        at = chunk_of(l)
        ins += [owns[l], gots[l].reshape(3 * r, c), gots[l].reshape(3 * r, c), gots[l].reshape(3 * r, c)]
        in_specs.append(pl.BlockSpec((tr, c), functools.partial(lambda h, q, cc, at: (at(h, q), 0), at=at)))
        in_specs += [pl.BlockSpec((tr, c), functools.partial(lambda h, q, cc, at, p: (p * nb + at(h, q), 0), at=at, p=p))
                     for p in range(3)]
    if kind == "col":
        out_sd = (L, 2, r, c)
        o_spec = pl.BlockSpec((None, 2, tr, c), lambda h, q, cc: ((q * h) // nb, 0, (q * h) % nb, 0))
    else:
        out_sd = (L * r, 2 * c)
        o_spec = pl.BlockSpec((tr, 2 * c), lambda h, q, cc: (q * h, 0))

    def kern(c_ref, *refs):
        in_refs = refs[:4 * L]
        out_ref, mine, recv, ssem, rsem = refs[4 * L:]
        h, q = pl.program_id(0), pl.program_id(1)
        x, y, cc, _ = _place()

        def swap(qq):
            return pltpu.make_async_remote_copy(src_ref=mine.at[qq], dst_ref=recv.at[qq], send_sem=ssem.at[qq],
                                                recv_sem=rsem.at[qq], device_id=(x, y, 1 - cc), device_id_type=MESH)

        for l in range(L):
            @pl.when(jnp.logical_and(h == 0, q // nb == l))
            def _(l=l):
                o_ref, g0, g1, g2 = in_refs[4 * l:4 * l + 4]
                mine[q] = ((o_ref[...].astype(F32) + g0[...].astype(F32)) + g1[...].astype(F32)) + g2[...].astype(F32)
                swap(q).start()

        @pl.when(h == 1)
        def _():
            swap(q).wait()
            a, b = mine[q], recv[q]
            first = c_ref[0] == 0
            lo, hi = jnp.where(first, a, b), jnp.where(first, b, a)
            if kind == "col":
                out_ref[0] = lo
                out_ref[1] = hi
            else:
                out_ref[:, :c] = lo
                out_ref[:, c:] = hi

    full = pl.pallas_call(
        kern,
        grid_spec=pltpu.PrefetchScalarGridSpec(
            num_scalar_prefetch=1, grid=(2, nq), in_specs=in_specs, out_specs=o_spec,
            scratch_shapes=[pltpu.VMEM((nq, tr, c), F32), pltpu.VMEM((nq, tr, c), F32),
                            pltpu.SemaphoreType.DMA((nq,)), pltpu.SemaphoreType.DMA((nq,))]),
        out_shape=jax.ShapeDtypeStruct(out_sd, F32), name=name,
        compiler_params=pltpu.CompilerParams(dimension_semantics=("arbitrary", "arbitrary"),
                                             vmem_limit_bytes=VMEM_CAP_BYTES),
    )(c_arr, *ins)
    return full.reshape(L, 2 * r, c) if kind == "col" else full.reshape(L, r, 2 * c)


def _small_allreduce(buf, name):
    R = buf.shape[0]
    assert R % 16 == 0
    h = R // 2

    def body(x_ref, o_ref, sib, csum, got, s_a, r_a, s_b, r_b, s_c, r_c):
        x, y, c, chips = _place()
        sibling = (x, y, 1 - c)
        own = 2 * x + y
        swap = pltpu.make_async_remote_copy(src_ref=x_ref, dst_ref=sib, send_sem=s_a, recv_sem=r_a,
                                            device_id=sibling, device_id_type=MESH)
        swap.start()
        swap.wait()
        a, b = x_ref[...], sib[...]
        south = c == 0
        csum[...] = jnp.where(south, a, b) + jnp.where(south, b, a)
        lo = pl.multiple_of(c * h, 8)
        mine = csum.at[pl.ds(lo, h)]
        got[own] = csum[pl.ds(lo, h)]
        sends = []
        for p, (px, py) in enumerate(chips):
            cp = pltpu.make_async_remote_copy(src_ref=mine, dst_ref=got.at[own], send_sem=s_b.at[p], recv_sem=r_b.at[p],
                                              device_id=(px, py, c), device_id_type=MESH)
            cp.start()
            sends.append(cp)
        for cp in sends:
            cp.wait()
        o_ref[pl.ds(lo, h)] = ((got[0] + got[1]) + got[2]) + got[3]
        done = o_ref.at[pl.ds(lo, h)]
        back = pltpu.make_async_remote_copy(src_ref=done, dst_ref=done, send_sem=s_c, recv_sem=r_c,
                                            device_id=sibling, device_id_type=MESH)
        back.start()
        back.wait_send()
        other = o_ref.at[pl.ds(pl.multiple_of((1 - c) * h, 8), h)]
        pltpu.make_async_remote_copy(src_ref=other, dst_ref=other, send_sem=s_c, recv_sem=r_c,
                                     device_id=sibling, device_id_type=MESH).wait_recv()

    vm = pl.BlockSpec(memory_space=pltpu.VMEM)
    return pl.pallas_call(
        body, out_shape=jax.ShapeDtypeStruct(buf.shape, F32), in_specs=[vm], out_specs=vm,
        scratch_shapes=[pltpu.VMEM((R, LANES), F32), pltpu.VMEM((R, LANES), F32), pltpu.VMEM((4, h, LANES), F32),
                        pltpu.SemaphoreType.DMA, pltpu.SemaphoreType.DMA, pltpu.SemaphoreType.DMA((3,)),
                        pltpu.SemaphoreType.DMA((3,)), pltpu.SemaphoreType.DMA, pltpu.SemaphoreType.DMA],
        name=name,
        compiler_params=pltpu.CompilerParams(vmem_limit_bytes=VMEM_CAP_BYTES),
    )(buf)


PACK_TILE_ROWS = 8


def _item_rows(shape):
    n = 1
    for d in shape:
        n *= d
    return -(-n // (PACK_TILE_ROWS * LANES)) * PACK_TILE_ROWS


def _pack(arrs, rows_total):
    buf = jnp.zeros((rows_total, LANES), F32)
    r = 0
    for a in arrs:
        f = a.reshape(-1).astype(F32)
        nr = _item_rows(a.shape)
        block = jnp.pad(f, (0, nr * LANES - f.shape[0])).reshape(nr, LANES)
        buf = lax.dynamic_update_slice(buf, block, (r, 0))
        r += nr
    return buf


def _unpack(buf, shapes):
    out, r = [], 0
    for s in shapes:
        n = 1
        for d in s:
            n *= d
        nr = _item_rows(s)
        out.append(buf[r:r + nr].reshape(-1)[:n].reshape(s))
        r += nr
    return out


def _rows_needed(shapes):
    return -(-sum(_item_rows(s) for s in shapes) // (2 * PACK_TILE_ROWS)) * (2 * PACK_TILE_ROWS)


def _two_rows(a, b):
    out = jnp.zeros((2, a.shape[1]), a.dtype)
    return lax.dynamic_update_slice(lax.dynamic_update_slice(out, a, (0, 0)), b, (1, 0))


def _adam(w, g, m, v):
    m = ADAM_B1 * m + (1.0 - ADAM_B1) * g
    v = ADAM_B2 * v + (1.0 - ADAM_B2) * jnp.square(g)
    m_hat = m / (1.0 - ADAM_B1 ** ADAM_STEP)
    v_hat = v / (1.0 - ADAM_B2 ** ADAM_STEP)
    delta = -ADAM_LR * (m_hat / (jnp.sqrt(v_hat) + ADAM_EPS) + ADAM_WD * w)
    return delta, m, v


def _adam_call(name, w2, g2, m2, v2, tr, pass_grad=False):
    def fn(rv, cv):
        outs = list(_adam(*rv))
        return ([rv[1]] + outs if pass_grad else outs), []

    width = w2.shape[1]
    return _rowcall(name, fn, [(w2, 0, width), (g2, 0, width), (m2, 0, width), (v2, 0, width)], [],
                    [(width, F32)] * (4 if pass_grad else 3), [], tr)


def kernel(x, mem, mem_norm, lb_logits, ffn1_norm, ffn1_w_in, ffn1_w_out, mix_norm, mem_w_kv, hgrn_w_in, hgrn_gnorm, hgrn_w_out, gmlp_w_in, gmlp_ln_g, gmlp_ln_b, gmlp_w_s, gmlp_b_s, gmlp_w_out, ffn2_norm, ffn2_w_in, ffn2_w_out, final_norm, loss_target, m_mem_norm, m_lb_logits, m_ffn1_norm, m_ffn1_w_in, m_ffn1_w_out, m_mix_norm, m_mem_w_kv, m_hgrn_w_in, m_hgrn_gnorm, m_hgrn_w_out, m_gmlp_w_in, m_gmlp_ln_g, m_gmlp_ln_b, m_gmlp_w_s, m_gmlp_b_s, m_gmlp_w_out, m_ffn2_norm, m_ffn2_w_in, m_ffn2_w_out, m_final_norm, v_mem_norm, v_lb_logits, v_ffn1_norm, v_ffn1_w_in, v_ffn1_w_out, v_mix_norm, v_mem_w_kv, v_hgrn_w_in, v_hgrn_gnorm, v_hgrn_w_out, v_gmlp_w_in, v_gmlp_ln_g, v_gmlp_ln_b, v_gmlp_w_s, v_gmlp_b_s, v_gmlp_w_out, v_ffn2_norm, v_ffn2_w_in, v_ffn2_w_out, v_final_norm):
    bl, seq, D = x.shape
    T = bl * seq
    mem_len = mem.shape[1]
    chip = 2 * lax.axis_index("x") + lax.axis_index("y")
    c_arr = lax.axis_index("c").astype(jnp.int32).reshape(1)
    TR = 1024

    big = [("ffn1_w_in", ffn1_w_in, "col"), ("ffn1_w_out", ffn1_w_out, "row"), ("mem_w_kv", mem_w_kv, "col"),
           ("hgrn_w_in", hgrn_w_in, "col"), ("hgrn_w_out", hgrn_w_out, "row"), ("gmlp_w_in", gmlp_w_in, "col"),
           ("gmlp_w_out", gmlp_w_out, "row"), ("ffn2_w_in", ffn2_w_in, "col"), ("ffn2_w_out", ffn2_w_out, "row")]
    kinds = [k for (_, _, k) in big]
    shards_bf = []
    for nm, w, _ in big:
        L, r, c = w.shape
        (wb,) = _rowcall("cast_" + nm, lambda rv, cv: ([rv[0]], []), [(w.reshape(L * r, c), 0, c)], [], [(c, BF)], [], 512)
        shards_bf.append(wb.reshape(L, r, c))
    sb = dict(zip([nm for (nm, _, _) in big], shards_bf))
    groups = [[("ffn1_w_in", 0)], [("ffn1_w_out", 0)], [("hgrn_w_in", None)], [("mem_w_kv", None)], [("hgrn_w_out", None)],
              [("ffn2_w_in", 0), ("ffn2_w_out", 0), ("gmlp_ln_g", None), ("gmlp_ln_b", None)],
              [("ffn1_w_in", 1), ("ffn1_w_out", 1)],
              [("gmlp_w_in", None), ("gmlp_w_out", None)],
              [("ffn2_w_in", 1), ("ffn2_w_out", 1)]]
    kind_of = {nm: k for (nm, _, k) in big}
    for nm, vec in (("gmlp_ln_g", gmlp_ln_g), ("gmlp_ln_b", gmlp_ln_b)):
        sb[nm] = vec.reshape(1, 1, -1)
        kind_of[nm] = "vec"
    gathered = {nm: [None, None] for nm in ("ffn1_w_in", "ffn1_w_out", "ffn2_w_in", "ffn2_w_out")}
    for gi, grp in enumerate(groups):
        outs = _allgather_seq("gather_%d" % gi, [(sb[nm], kind_of[nm], l) for (nm, l) in grp], gi)
        for (nm, l), o in zip(grp, outs):
            if l is None:
                gathered[nm] = o
            else:
                gathered[nm][l] = o

    ln_w = GM_GROUPS * GM_GROUP_DIM
    ln_g_full, ln_b_full = gathered["gmlp_ln_g"].reshape(1, ln_w), gathered["gmlp_ln_b"].reshape(1, ln_w)

    def rms_fwd(name, xin, g):
        (h,) = _rowcall(name, lambda rv, cv: ([_rmsnorm(rv[0], cv[0])], []), [(xin, 0, D)], [g.reshape(1, D)], [(D, BF)], [], TR)
        return h

    def ffn_fwd(tag, xin, h, w_in, w_out, layer, next_gain):
        dff = w_out[layer].shape[1]
        zg, zu, a = _ffn_in_swiglu("ffn_in_" + tag, h, w_in[layer], 1024, dff // 2)
        out = _mm("ffn_out_" + tag, a, w_out[layer], "nn", F32, 1024, 1024, dff, scale=0.5, res=xin, b_lead=0,
                  norm_gain=None if next_gain is None else next_gain.reshape(1, D))
        xo, h_next = (out, None) if next_gain is None else out
        return xo, h_next, (xin, h, zg, zu, a)

    def ffn_bwd(tag, dxo, saved, g, w_in, w_out, layer):
        xin, h, zg, zu, a = saved
        dff = w_out[layer].shape[1]
        dw_out = _mm_tn_pair("ffn_dwo_" + tag, a, dxo, "row", c_arr, dff // 2, T, scale=0.5)
        dz = _ffn_da_swiglu("ffn_da_" + tag, dxo, w_out[layer], zg, zu, 512)
        dw_in = _mm_tn_pair("ffn_dwi_" + tag, h, dz, "col", c_arr, 512, T)
        dx, dg = _mm_dh_rms("ffn_dh_" + tag, dz, w_in[layer], xin, g.reshape(1, D), dxo, 512)
        return dx, dg, dw_in, dw_out

    def rms_bwd(name, xin, g, dh, dres):
        def fn(rv, cv):
            _, vjp = jax.vjp(_rmsnorm, rv[0], cv[0])
            dx, dg = vjp(rv[1])
            if dres is not None:
                dx = dx + rv[2]
            return [dx], [dg]

        rows = [(xin, 0, D), (dh, 0, D)] + ([(dres, 0, D)] if dres is not None else [])
        dx, dg = _rowcall(name, fn, rows, [g.reshape(1, D)], [(D, F32)], [((1, D), F32)], TR)
        return dx, dg

    x0 = x.reshape(T, D)
    tgt = loss_target.reshape(T, D)
    mem2 = mem.reshape(bl * mem_len, D)
    memn = rms_fwd("rms_mem", mem2, mem_norm)

    h_f10 = rms_fwd("rms_f1l0", x0, ffn1_norm[0])
    x1, h_m0, sv_f10 = ffn_fwd("f1l0", x0, h_f10, gathered["ffn1_w_in"], gathered["ffn1_w_out"], 0, mix_norm[0])
    z_m0 = _mm("mix_in_0", h_m0, gathered["hgrn_w_in"], "nn", F32, 2048, 512, D, b_lead=0)
    kv = [_mm("kv_%d" % i, memn, gathered["mem_w_kv"], "nn", F32, 512, 512, D, b_lead=i) for i in range(2)]
    cat0, stash0 = _hgrn_fwd2(z_m0, lb_logits, hgrn_gnorm, kv[0], bl, seq)
    x2, h_f20 = _mm("mix_out_0", cat0, gathered["hgrn_w_out"], "nn", F32, 1024, 1024, cat0.shape[1], res=x1, b_lead=0,
                    norm_gain=ffn2_norm[0].reshape(1, D))
    x3, h_f11, sv_f20 = ffn_fwd("f2l0", x2, h_f20, gathered["ffn2_w_in"], gathered["ffn2_w_out"], 0, ffn1_norm[1])
    x4, h_m1, sv_f11 = ffn_fwd("f1l1", x3, h_f11, gathered["ffn1_w_in"], gathered["ffn1_w_out"], 1, mix_norm[1])
    z_m1 = _mm("mix_in_1", h_m1, gathered["gmlp_w_in"], "nn", F32, 2048, 512, D, b_lead=0)
    nc1 = seq // GM_CHUNK
    w_s, b_s = gmlp_w_s[0], gmlp_b_s[0]
    cat1 = _gmlp_fwd(z_m1, ln_g_full, ln_b_full, w_s, b_s, kv[1], bl, nc1)
    x5, h_f21 = _mm("mix_out_1", cat1, gathered["gmlp_w_out"], "nn", F32, 1024, 1024, cat1.shape[1], res=x4, b_lead=0,
                    norm_gain=ffn2_norm[1].reshape(1, D))
    x6, _, sv_f21 = ffn_fwd("f2l1", x5, h_f21, gathered["ffn2_w_in"], gathered["ffn2_w_out"], 1, None)

    def head(rv, cv):
        def f(xx, gg):
            err = _rmsnorm(xx, gg) - rv[1]
            return 0.5 * jnp.sum(jnp.mean(err * err, axis=-1, keepdims=True), axis=0, keepdims=True)

        ls, vjp = jax.vjp(f, rv[0], cv[0])
        dx, dg = vjp(jnp.ones((1, 1), F32))
        return [dx], [dg, jnp.broadcast_to(ls, (1, 128))]

    dx6, d_final, loss_part = _rowcall("loss_head", head, [(x6, 0, D), (tgt, 0, D)], [final_norm.reshape(1, D)],
                                       [(D, F32)], [((1, D), F32), ((1, 128), F32)], TR)

    rs_out = {}
    n_gather = len(groups)

    def rs(gi, items):
        outs = _rs_chips_seq("reduce_%d" % gi, [p for (_, p, _) in items], [k for (_, _, k) in items], n_gather + gi)
        for i, (key, _, _) in enumerate(items):
            rs_out[key] = (outs[2 * i], outs[2 * i + 1])

    dx5, dg_f21, dwi_f21, dwo_f21 = ffn_bwd("f2l1", dx6, sv_f21, ffn2_norm[1], gathered["ffn2_w_in"], gathered["ffn2_w_out"], 1)
    rs(0, [(("ffn2_w_out", 1), dwo_f21, "row"), (("ffn2_w_in", 1), dwi_f21, "col")])
    dcat1 = _mm("mix_dcat_1", dx5, gathered["gmlp_w_out"], "nt", F32, 2048, 1024, D, b_lead=0)
    dwo_m1 = _mm_tn_pair("mix_dwo_1", cat1, dx5, "row", c_arr, 1024, T)
    dz_m1, dkv1, d_lng, d_lnb, d_ws, d_bs = _gmlp_bwd(z_m1, dcat1, ln_g_full, ln_b_full, w_s, b_s, kv[1], bl, nc1)
    dx4, dg_m1 = _mm_dh_rms("mix_dh_1", dz_m1, gathered["gmlp_w_in"], x4, mix_norm[1].reshape(1, D), dx5, 512)
    dwi_m1 = _mm_tn_pair("mix_dwi_1", h_m1, dz_m1, "col", c_arr, 1024, T)
    rs(1, [(("gmlp_w_out", 0), dwo_m1, "row"), (("gmlp_w_in", 0), dwi_m1, "col")])
    dx3, dg_f11, dwi_f11, dwo_f11 = ffn_bwd("f1l1", dx4, sv_f11, ffn1_norm[1], gathered["ffn1_w_in"], gathered["ffn1_w_out"], 1)
    rs(2, [(("ffn1_w_out", 1), dwo_f11, "row"), (("ffn1_w_in", 1), dwi_f11, "col")])

    dx2, dg_f20, dwi_f20, dwo_f20 = ffn_bwd("f2l0", dx3, sv_f20, ffn2_norm[0], gathered["ffn2_w_in"], gathered["ffn2_w_out"], 0)
    rs(3, [(("ffn2_w_out", 0), dwo_f20, "row"), (("ffn2_w_in", 0), dwi_f20, "col")])
    dcat0 = _mm("mix_dcat_0", dx2, gathered["hgrn_w_out"], "nt", F32, 2048, 1024, D, b_lead=0)
    dwo_m0 = _mm_tn_pair("mix_dwo_0", cat0, dx2, "row", c_arr, 1024, T)
    dz_m0, dkv0, d_lb, d_gn = _hgrn_bwd2(z_m0, dcat0, stash0, lb_logits, hgrn_gnorm, kv[0], bl, seq)
    dx1, dg_m0 = _mm_dh_rms("mix_dh_0", dz_m0, gathered["hgrn_w_in"], x1, mix_norm[0].reshape(1, D), dx2, 512)
    dwi_m0 = _mm_tn_pair("mix_dwi_0", h_m0, dz_m0, "col", c_arr, 1024, T)
    rs(4, [(("hgrn_w_out", 0), dwo_m0, "row"), (("hgrn_w_in", 0), dwi_m0, "col")])

    dwkv = [_mm_tn_pair("kv_dw_%d" % i, memn, dkv, "col", c_arr, 1024, 512) for i, dkv in enumerate([dkv0, dkv1])]
    rs(5, [(("mem_w_kv", 0), dwkv[0], "col"), (("mem_w_kv", 1), dwkv[1], "col")])
    dmemn = _mm("kv_dx_0", dkv0, gathered["mem_w_kv"], "nt", F32, 512, 512, 1024, b_lead=0)
    dmemn = _mm("kv_dx_1", dkv1, gathered["mem_w_kv"], "nt", F32, 512, 512, 1024, res=dmemn, b_lead=1)
    _, d_memnorm = rms_bwd("rms_bwd_mem", mem2, mem_norm, dmemn, None)

    dx0, dg_f10, dwi_f10, dwo_f10 = ffn_bwd("f1l0", dx1, sv_f10, ffn1_norm[0], gathered["ffn1_w_in"], gathered["ffn1_w_out"], 0)
    rs(6, [(("ffn1_w_out", 0), dwo_f10, "row")])
    rs(7, [(("ffn1_w_in", 0), dwi_f10, "col")])

    shard_grads = [_finish_share("finish_" + nm, [rs_out[(nm, l)][0] for l in range(w.shape[0])],
                                 [rs_out[(nm, l)][1] for l in range(w.shape[0])], k, c_arr) for (nm, w, k) in big]

    big_w = [w for (_, w, _) in big]
    big_m = [m_ffn1_w_in, m_ffn1_w_out, m_mem_w_kv, m_hgrn_w_in, m_hgrn_w_out, m_gmlp_w_in, m_gmlp_w_out, m_ffn2_w_in, m_ffn2_w_out]
    big_v = [v_ffn1_w_in, v_ffn1_w_out, v_mem_w_kv, v_hgrn_w_in, v_hgrn_w_out, v_gmlp_w_in, v_gmlp_w_out, v_ffn2_w_in, v_ffn2_w_out]
    big_out = {}
    for (nm, w, _), g, m, v in zip(big, shard_grads, big_m, big_v):
        L, r, c = w.shape
        g2, d2, m2, v2 = _adam_call("adam_" + nm, w.reshape(L * r, c), g.reshape(L * r, c), m.reshape(L * r, c),
                                    v.reshape(L * r, c), 256, pass_grad=True)
        big_out[nm] = (g2.reshape(w.shape), d2.reshape(w.shape), m2.reshape(w.shape), v2.reshape(w.shape))

    d_ffn1n = _two_rows(dg_f10, dg_f11)
    d_mixn = _two_rows(dg_m0, dg_m1)
    d_ffn2n = _two_rows(dg_f20, dg_f21)
    small_parts = [loss_part[:, :1], d_memnorm, d_lb, d_ffn1n, d_mixn, d_gn, d_lng, d_lnb, d_ws, d_bs, d_ffn2n, d_final]
    red_shapes = [(1,), mem_norm.shape, lb_logits.shape, ffn1_norm.shape, mix_norm.shape, hgrn_gnorm.shape, (1, ln_w), (1, ln_w),
                  gmlp_w_s.shape, gmlp_b_s.shape, ffn2_norm.shape, final_norm.shape]
    red = _small_allreduce(_pack(small_parts, _rows_needed(red_shapes)), "reduce_small")
    (loss_v, g_memn, g_lb, g_f1n, g_mixn, g_gn, g_lng_full, g_lnb_full, g_ws, g_bs, g_f2n, g_fin) = _unpack(red, red_shapes)
    lsh = gmlp_ln_g.shape[1]
    g_lng = lax.dynamic_slice(g_lng_full, (0, chip * lsh), (1, lsh))
    g_lnb = lax.dynamic_slice(g_lnb_full, (0, chip * lsh), (1, lsh))
    small_w = [mem_norm, lb_logits, ffn1_norm, mix_norm, hgrn_gnorm, gmlp_ln_g, gmlp_ln_b, gmlp_w_s, gmlp_b_s, ffn2_norm, final_norm]
    small_g = [g_memn, g_lb, g_f1n, g_mixn, g_gn, g_lng, g_lnb, g_ws, g_bs, g_f2n, g_fin]
    small_m = [m_mem_norm, m_lb_logits, m_ffn1_norm, m_mix_norm, m_hgrn_gnorm, m_gmlp_ln_g, m_gmlp_ln_b, m_gmlp_w_s, m_gmlp_b_s, m_ffn2_norm, m_final_norm]
    small_v = [v_mem_norm, v_lb_logits, v_ffn1_norm, v_mix_norm, v_hgrn_gnorm, v_gmlp_ln_g, v_gmlp_ln_b, v_gmlp_w_s, v_gmlp_b_s, v_ffn2_norm, v_final_norm]
    sshapes = [w.shape for w in small_w]
    nrow = _rows_needed(sshapes)
    d_p, m_p, v_p = _adam_call("adam_small", _pack(small_w, nrow), _pack(small_g, nrow), _pack(small_m, nrow), _pack(small_v, nrow), nrow)
    s_delta, s_m, s_v = _unpack(d_p, sshapes), _unpack(m_p, sshapes), _unpack(v_p, sshapes)
    small_names = ["mem_norm", "lb_logits", "ffn1_norm", "mix_norm", "hgrn_gnorm", "gmlp_ln_g", "gmlp_ln_b", "gmlp_w_s", "gmlp_b_s", "ffn2_norm", "final_norm"]
    small_out = {nm: (g.reshape(w.shape), d, m, v) for nm, w, g, d, m, v in zip(small_names, small_w, small_g, s_delta, s_m, s_v)}

    order = ["mem_norm", "lb_logits", "ffn1_norm", "ffn1_w_in", "ffn1_w_out", "mix_norm", "mem_w_kv", "hgrn_w_in", "hgrn_gnorm",
             "hgrn_w_out", "gmlp_w_in", "gmlp_ln_g", "gmlp_ln_b", "gmlp_w_s", "gmlp_b_s", "gmlp_w_out", "ffn2_norm", "ffn2_w_in",
             "ffn2_w_out", "final_norm"]
    allo = {**big_out, **small_out}
    grad_x = dx0.reshape(x.shape)
    return (loss_v.reshape(()), grad_x, *[allo[n][0] for n in order], *[allo[n][1] for n in order],
            *[allo[n][2] for n in order], *[allo[n][3] for n in order])
```

```python
import functools

import jax
import jax.numpy as jnp
from jax import lax
from jax.experimental import pallas as pl
from jax.experimental.pallas import tpu as pltpu
from jax.experimental.pallas import tpu_sc as plsc

BF = jnp.bfloat16
F32 = jnp.float32
MESH = pl.DeviceIdType.MESH

EPS = 1e-6
D_MODEL = 1024
HG_HEADS = 8
HG_DIM = 128
HG_CHUNK = 64
GM_CHUNK = 128
GM_GROUPS = 8
GM_GROUP_DIM = 256
XA_HEADS = 4
XA_DIM = 256
ADAM_LR = 0.001
ADAM_B1 = 0.9
ADAM_B2 = 0.999
ADAM_EPS = 1e-08
ADAM_WD = 0.01
ADAM_STEP = 10

VMEM_CAP_BYTES = 60 * 1024 * 1024
LANES = 1024


def _pick(n, cap, mult=16):
    if n <= cap:
        return n
    for d in range(cap - cap % mult, 0, -mult):
        if n % d == 0:
            return d
    raise ValueError((n, cap, mult))


def _dg(a, b, ca, cb):
    return lax.dot_general(a.astype(BF), b.astype(BF), (((ca,), (cb,)), ((), ())), preferred_element_type=F32)


@jax.custom_vjp
def dot_nn(a, b):
    return _dg(a, b, 1, 0)


def _nn_fwd(a, b):
    return _dg(a, b, 1, 0), (a, b)


def _nn_bwd(r, g):
    a, b = r
    return _dg(g, b, 1, 1), _dg(a, g, 0, 0)


dot_nn.defvjp(_nn_fwd, _nn_bwd)


@jax.custom_vjp
def dot_nt(a, b):
    return _dg(a, b, 1, 1)


def _nt_fwd(a, b):
    return _dg(a, b, 1, 1), (a, b)


def _nt_bwd(r, g):
    a, b = r
    return _dg(g, b, 1, 0), _dg(g, a, 0, 0)


dot_nt.defvjp(_nt_fwd, _nt_bwd)


@jax.custom_vjp
def dot_tn(a, b):
    return _dg(a, b, 0, 0)


def _tn_fwd(a, b):
    return _dg(a, b, 0, 0), (a, b)


def _tn_bwd(r, g):
    a, b = r
    return _dg(b, g, 1, 1), _dg(a, g, 1, 0)


dot_tn.defvjp(_tn_fwd, _tn_bwd)


def _rmsnorm(x, g):
    return x * lax.rsqrt(jnp.mean(x * x, axis=-1, keepdims=True) + EPS) * g


def _silu(x):
    return x * jax.nn.sigmoid(x)


@jax.custom_vjp
def _gelu(x):
    return 0.5 * x * (1.0 + lax.erf(x * (0.5 ** 0.5)))


def _gelu_fwd(x):
    return _gelu(x), x


def _gelu_bwd(x, g):
    t = x * (0.5 ** 0.5)
    cdf = 0.5 * (1.0 + lax.erf(t))
    return (g * (cdf + x * (jnp.exp(-(t * t)) * (0.5 / 3.141592653589793) ** 0.5)),)


_gelu.defvjp(_gelu_fwd, _gelu_bwd)


def _softmax_last(s):
    m = lax.stop_gradient(jnp.max(s, axis=-1, keepdims=True))
    e = jnp.exp(s - m)
    return e / jnp.sum(e, axis=-1, keepdims=True)


def _tril(n):
    r = lax.broadcasted_iota(jnp.int32, (n, n), 0)
    c = lax.broadcasted_iota(jnp.int32, (n, n), 1)
    return r >= c


def _cumsum_rows(l):
    n = l.shape[0]
    return lax.dot_general(_tril(n).astype(F32), l, (((1,), (0,)), ((), ())),
                           precision=lax.Precision.HIGHEST, preferred_element_type=F32)


def _attention(zx, mk, mv):
    s = dot_nt(zx, mk) * (XA_DIM ** -0.5)
    return dot_nn(_softmax_last(s), mv)


def _hgrn_head(zq, zf, zi, zg, l0, l1, l2, gn, S):
    m = lax.stop_gradient(jnp.maximum(jnp.maximum(l0, l1), l2))
    e0 = jnp.exp(l0 - m)
    lb = e0 / (e0 + jnp.exp(l1 - m) + jnp.exp(l2 - m))
    q = _silu(zq)
    f = lb + (1.0 - lb) * jax.nn.sigmoid(zf)
    k = 1.0 - f
    b = _cumsum_rows(jnp.log(f))
    b_last = b[HG_CHUNK - 1:HG_CHUNK, :]
    q_dec = q * jnp.exp(b)
    k_inv = k * jnp.exp(-b)
    a = jnp.where(_tril(HG_CHUNK), dot_nt(q_dec, k_inv), 0.0)
    o = dot_nn(a, zi) + dot_nn(q_dec, S)
    S_new = jnp.exp(b_last).reshape(HG_DIM, 1) * S + dot_tn(k * jnp.exp(b_last - b), zi)
    o = _rmsnorm(o, gn) * _silu(zg)
    return o, S_new


def _gmlp_block(zu, zv, zx, lng, lnb, ws, bs, mk, mv):
    gv = [_gelu(v) for v in zv]
    width = GM_GROUPS * GM_GROUP_DIM
    mu = sum(jnp.sum(g, axis=-1, keepdims=True) for g in gv) / width
    xc = [g - mu for g in gv]
    var = sum(jnp.sum(c * c, axis=-1, keepdims=True) for c in xc) / width
    r = lax.rsqrt(var + EPS)
    outs = []
    for g in range(GM_GROUPS):
        v = xc[g] * r * lng[g] + lnb[g]
        w = jnp.where(_tril(GM_CHUNK), ws[g], 0.0)
        mixed = dot_nn(w, v) + bs[g].reshape(GM_CHUNK, 1)
        outs.append(_gelu(zu[g]) * mixed)
    for a in range(XA_HEADS):
        outs.append(_attention(zx[a], mk[a], mv[a]))
    return outs


def _rowcall(name, fn, rows, consts, row_outs, acc_outs, tr):
    nrows = rows[0][0].shape[0]
    tr = _pick(nrows, tr)
    n_r, n_c, n_ro, n_ao = len(rows), len(consts), len(row_outs), len(acc_outs)

    def kern(*refs):
        rv = [r[...] for r in refs[:n_r]]
        cv = [r[...] for r in refs[n_r:n_r + n_c]]
        ro_refs = refs[n_r + n_c:n_r + n_c + n_ro]
        ao_refs = refs[n_r + n_c + n_ro:]
        ro, ao = fn(rv, cv)
        for ref, v in zip(ro_refs, ro):
            ref[...] = v.astype(ref.dtype)
        if n_ao:
            @pl.when(pl.program_id(0) == 0)
            def _():
                for ref in ao_refs:
                    ref[...] = jnp.zeros(ref.shape, ref.dtype)

            for ref, v in zip(ao_refs, ao):
                ref[...] += v.astype(ref.dtype)

    in_specs = [pl.BlockSpec((tr, w), functools.partial(lambda i, cb: (i, cb), cb=cb)) for (_, cb, w) in rows]
    in_specs += [pl.BlockSpec(c.shape, lambda i: (0, 0)) for c in consts]
    out_specs = [pl.BlockSpec((tr, w), lambda i: (i, 0)) for (w, _) in row_outs]
    out_specs += [pl.BlockSpec(s, lambda i: (0, 0)) for (s, _) in acc_outs]
    out_shape = [jax.ShapeDtypeStruct((nrows, w), dt) for (w, dt) in row_outs]
    out_shape += [jax.ShapeDtypeStruct(s, dt) for (s, dt) in acc_outs]
    outs = pl.pallas_call(
        kern, grid=(nrows // tr,), in_specs=in_specs, out_specs=out_specs, out_shape=out_shape, name=name,
        compiler_params=pltpu.CompilerParams(dimension_semantics=("arbitrary",),
                                             vmem_limit_bytes=VMEM_CAP_BYTES),
    )(*[a for (a, _, _) in rows], *consts)
    return outs


def _mm(name, a, b, mode, out_dtype, tm, tn, tk, scale=1.0, res=None, a_lead=None, b_lead=None, norm_gain=None):
    ash = a.shape[-2:]
    bsh = b.shape[-2:]
    if mode == "nn":
        (M, K), (K2, N) = ash, bsh
    elif mode == "nt":
        (M, K), (N, K2) = ash, bsh
    else:
        (K, M), (K2, N) = ash, bsh
    assert K == K2, (name, a.shape, b.shape)
    tm, tn, tk = min(tm, M), min(tn, N), min(tk, K)
    assert M % tm == 0 and N % tn == 0 and K % tk == 0, (name, M, N, K, tm, tn, tk)
    nk = K // tk
    dims = {"nn": (1, 0), "nt": (1, 1), "tn": (0, 0)}[mode]

    def lead(spec_shape, index_fn, lead_idx):
        if lead_idx is None:
            return pl.BlockSpec(spec_shape, index_fn)
        return pl.BlockSpec((None,) + spec_shape, lambda i, j, k: (lead_idx,) + index_fn(i, j, k))

    if mode == "tn":
        a_spec = lead((tk, tm), lambda i, j, k: (k, i), a_lead)
    else:
        a_spec = lead((tm, tk), lambda i, j, k: (i, k), a_lead)
    if mode == "nt":
        b_spec = lead((tn, tk), lambda i, j, k: (j, k), b_lead)
    else:
        b_spec = lead((tk, tn), lambda i, j, k: (k, j), b_lead)
    o_spec = pl.BlockSpec((tm, tn), lambda i, j, k: (i, j))
    has_res = res is not None
    has_norm = norm_gain is not None
    assert not has_norm or tn == N

    def kern(*refs):
        a_ref, b_ref = refs[0], refs[1]
        pos = 2
        res_ref = gain_ref = h_ref = None
        if has_res:
            res_ref, pos = refs[pos], pos + 1
        if has_norm:
            gain_ref, pos = refs[pos], pos + 1
        o_ref, pos = refs[pos], pos + 1
        if has_norm:
            h_ref = refs[pos]
        acc_ref = refs[-1] if nk > 1 else None
        p = lax.dot_general(a_ref[...].astype(BF), b_ref[...].astype(BF), (((dims[0],), (dims[1],)), ((), ())),
                            preferred_element_type=F32)

        def finish(v):
            if scale != 1.0:
                v = v * scale
            if has_res:
                v = res_ref[...] + v
            o_ref[...] = v.astype(o_ref.dtype)
            if has_norm:
                h_ref[...] = _rmsnorm(v, gain_ref[...]).astype(h_ref.dtype)

        if nk == 1:
            finish(p)
        else:
            k = pl.program_id(2)

            @pl.when(k == 0)
            def _():
                acc_ref[...] = p

            @pl.when(k > 0)
            def _():
                acc_ref[...] += p

            @pl.when(k == nk - 1)
            def _():
                finish(acc_ref[...])

    ins = [a, b] + ([res] if has_res else []) + ([norm_gain] if has_norm else [])
    in_specs = [a_spec, b_spec] + ([o_spec] if has_res else [])
    in_specs += [pl.BlockSpec((1, N), lambda i, j, k: (0, 0))] if has_norm else []
    out_sd = jax.ShapeDtypeStruct((M, N), out_dtype)
    return pl.pallas_call(
        kern, grid=(M // tm, N // tn, nk), in_specs=in_specs,
        out_specs=[o_spec, o_spec] if has_norm else o_spec,
        out_shape=[out_sd, jax.ShapeDtypeStruct((M, N), BF)] if has_norm else out_sd,
        scratch_shapes=[pltpu.VMEM((tm, tn), F32)] if nk > 1 else [],
        name=name,
        compiler_params=pltpu.CompilerParams(dimension_semantics=("parallel", "parallel", "arbitrary"),
                                             vmem_limit_bytes=VMEM_CAP_BYTES),
    )(*ins)


def _ffn_in_swiglu(name, h, w3, tm, tn):
    T, D = h.shape
    dff = w3.shape[2] // 2
    tm = min(tm, T)
    assert T % tm == 0 and dff % tn == 0
    nj = dff // tn

    def kern(h_ref, wg_ref, wu_ref, zg_ref, zu_ref, a_ref):
        hb = h_ref[...]
        g = jnp.dot(hb, wg_ref[...], preferred_element_type=F32).astype(BF)
        u = jnp.dot(hb, wu_ref[...], preferred_element_type=F32).astype(BF)
        zg_ref[...] = g
        zu_ref[...] = u
        a_ref[...] = (_silu(g.astype(F32)) * u.astype(F32)).astype(BF)

    o_spec = pl.BlockSpec((tm, tn), lambda i, j: (i, j))
    return pl.pallas_call(
        kern, grid=(T // tm, nj),
        in_specs=[pl.BlockSpec((tm, D), lambda i, j: (i, 0)),
                  pl.BlockSpec((None, D, tn), lambda i, j: (0, 0, j)),
                  pl.BlockSpec((None, D, tn), lambda i, j: (0, 0, j + nj))],
        out_specs=[o_spec, o_spec, o_spec],
        out_shape=[jax.ShapeDtypeStruct((T, dff), BF)] * 3, name=name,
        compiler_params=pltpu.CompilerParams(dimension_semantics=("parallel", "arbitrary"),
                                             vmem_limit_bytes=VMEM_CAP_BYTES),
    )(h, w3, w3)


def _ffn_da_swiglu(name, dxo, w3, zg, zu, tm):
    T, D = dxo.shape
    dff = w3.shape[1]
    tm = min(tm, T)
    assert T % tm == 0 and dff % 2 == 0
    hc = dff // 2

    def kern(d_ref, w_ref, g_ref, u_ref, dz_ref):
        db = (d_ref[...] * 0.5).astype(BF)
        for s in range(2):
            cols = slice(s * hc, (s + 1) * hc)
            da = lax.dot_general(db, w_ref[cols, :], (((1,), (1,)), ((), ())), preferred_element_type=F32)
            g = g_ref[:, cols].astype(F32)
            sg = 1.0 / (1.0 + jnp.exp(-g))
            gs = g * sg
            dab = da.astype(BF)
            dz_ref[:, cols] = (dab * u_ref[:, cols]) * (sg + gs * (1.0 - sg)).astype(BF)
            dz_ref[:, dff + s * hc:dff + (s + 1) * hc] = dab * gs.astype(BF)

    row = lambda w: pl.BlockSpec((tm, w), lambda i: (i, 0))
    return pl.pallas_call(
        kern, grid=(T // tm,),
        in_specs=[row(D), pl.BlockSpec((None, dff, D), lambda i: (0, 0, 0), pipeline_mode=pl.Buffered(1)), row(dff), row(dff)],
        out_specs=row(2 * dff), out_shape=jax.ShapeDtypeStruct((T, 2 * dff), BF), name=name,
        compiler_params=pltpu.CompilerParams(dimension_semantics=("arbitrary",), vmem_limit_bytes=VMEM_CAP_BYTES),
    )(dxo, w3, zg, zu)


def _mm_dh_rms(name, dz, w3, xin, g, dres, tm):
    T, K = dz.shape
    D = w3.shape[1]
    tm = min(tm, T)
    assert T % tm == 0

    def kern(dz_ref, w_ref, x_ref, g_ref, r_ref, dx_ref, dg_ref):
        dh = lax.dot_general(dz_ref[...], w_ref[...], (((1,), (1,)), ((), ())), preferred_element_type=F32)
        _, vjp = jax.vjp(_rmsnorm, x_ref[...], g_ref[...])
        dx, dg = vjp(dh)
        dx_ref[...] = dx + r_ref[...]

        @pl.when(pl.program_id(0) == 0)
        def _():
            dg_ref[...] = jnp.zeros(dg_ref.shape, F32)

        dg_ref[...] += dg

    row = lambda w: pl.BlockSpec((tm, w), lambda i: (i, 0))
    one = pl.BlockSpec((1, D), lambda i: (0, 0))
    return pl.pallas_call(
        kern, grid=(T // tm,),
        in_specs=[row(K), pl.BlockSpec((None, D, K), lambda i: (0, 0, 0), pipeline_mode=pl.Buffered(1)), row(D), one, row(D)],
        out_specs=[row(D), one], out_shape=[jax.ShapeDtypeStruct((T, D), F32), jax.ShapeDtypeStruct((1, D), F32)], name=name,
        compiler_params=pltpu.CompilerParams(dimension_semantics=("arbitrary",), vmem_limit_bytes=VMEM_CAP_BYTES),
    )(dz, w3, xin, g, dres)


def _mm_tn_pair(name, a, b, kind, c_arr, tq, tk, scale=1.0):
    T, M = a.shape
    _, N = b.shape
    tk = min(tk, T)
    assert T % tk == 0
    nk = T // tk
    if kind == "col":
        hm = M // 2
        assert N % tq == 0
        nq = N // tq
        tile = (hm, tq)
        a_spec = pl.BlockSpec((tk, hm), lambda h, q, k, c: (k, jnp.bitwise_xor(h, 1 - c[0])))
        b_spec = pl.BlockSpec((tk, tq), lambda h, q, k, c: (k, q))
        o_spec = pl.BlockSpec(tile, lambda h, q, k, c: (0, q * h))
        out_sd = (hm, N)
    else:
        hn = N // 2
        assert M % tq == 0
        nq = M // tq
        tile = (tq, hn)
        a_spec = pl.BlockSpec((tk, tq), lambda h, q, k, c: (k, q))
        b_spec = pl.BlockSpec((tk, hn), lambda h, q, k, c: (k, jnp.bitwise_xor(h, 1 - c[0])))
        o_spec = pl.BlockSpec(tile, lambda h, q, k, c: (q * h, 0))
        out_sd = (M, hn)

    def kern(c_ref, a_ref, b_ref, o_ref, acc, stage, recv, ssem, rsem):
        h, q, k = pl.program_id(0), pl.program_id(1), pl.program_id(2)
        x, y, c, _ = _place()
        p = lax.dot_general(a_ref[...].astype(BF), b_ref[...].astype(BF), (((0,), (0,)), ((), ())), preferred_element_type=F32)

        @pl.when(k == 0)
        def _():
            acc[...] = p

        @pl.when(k > 0)
        def _():
            acc[...] += p

        def send(slot, qq):
            return pltpu.make_async_remote_copy(src_ref=stage.at[slot], dst_ref=recv.at[qq], send_sem=ssem.at[slot],
                                                recv_sem=rsem.at[qq], device_id=(x, y, 1 - c), device_id_type=MESH)

        last = k == nk - 1

        @pl.when(jnp.logical_and(last, h == 0))
        def _():
            slot = q % 2

            @pl.when(q >= 2)
            def _():
                send(slot, q).wait_send()

            stage[slot] = (acc[...] * scale).astype(BF)
            send(slot, q).start()

        @pl.when(jnp.logical_and(last, h == 1))
        def _():
            @pl.when(q == 0)
            def _():
                for s in range(min(nq, 2)):
                    send(s, 0).wait_send()

            send(0, q).wait_recv()
            o_ref[...] = (acc[...] * scale + recv[q].astype(F32)).astype(o_ref.dtype)

    return pl.pallas_call(
        kern,
        grid_spec=pltpu.PrefetchScalarGridSpec(
            num_scalar_prefetch=1, grid=(2, nq, nk), in_specs=[a_spec, b_spec], out_specs=o_spec,
            scratch_shapes=[pltpu.VMEM(tile, F32), pltpu.VMEM((2,) + tile, BF), pltpu.VMEM((nq,) + tile, BF),
                            pltpu.SemaphoreType.DMA((2,)), pltpu.SemaphoreType.DMA((nq,))]),
        out_shape=jax.ShapeDtypeStruct(out_sd, BF), name=name,
        compiler_params=pltpu.CompilerParams(dimension_semantics=("arbitrary", "arbitrary", "arbitrary"),
                                             vmem_limit_bytes=VMEM_CAP_BYTES),
    )(c_arr, a, b)


def _kv_pieces(kv_ref):
    W = XA_HEADS * XA_DIM
    mk = [kv_ref[:, a * XA_DIM:(a + 1) * XA_DIM] for a in range(XA_HEADS)]
    mv = [kv_ref[:, W + a * XA_DIM:W + (a + 1) * XA_DIM] for a in range(XA_HEADS)]
    return mk, mv


def _lb_pieces(lb_ref):
    return [[lb_ref[r:r + 1, h * HG_DIM:(h + 1) * HG_DIM] for h in range(HG_HEADS)] for r in range(3)]


HG_SUB = 4


def _hgrn_rows(z_ref):
    W = HG_HEADS * HG_DIM

    def piece(c, col, w):
        return z_ref[c * HG_CHUNK:(c + 1) * HG_CHUNK, col:col + w]

    zq = [[piece(c, h * HG_DIM, HG_DIM) for h in range(HG_HEADS)] for c in range(HG_SUB)]
    zf = [[piece(c, W + h * HG_DIM, HG_DIM) for h in range(HG_HEADS)] for c in range(HG_SUB)]
    zi = [[piece(c, 2 * W + h * HG_DIM, HG_DIM) for h in range(HG_HEADS)] for c in range(HG_SUB)]
    zg = [[piece(c, 3 * W + h * HG_DIM, HG_DIM) for h in range(HG_HEADS)] for c in range(HG_SUB)]
    zx = [z_ref[:, 4 * W + a * XA_DIM:4 * W + (a + 1) * XA_DIM] for a in range(XA_HEADS)]
    return zq, zf, zi, zg, zx


def _hgrn_steps(zq, zf, zi, zg, zx, l0, l1, l2, gn, mk, mv, S):
    mix = []
    for c in range(HG_SUB):
        row, s_next = [], []
        for h in range(HG_HEADS):
            o, sn = _hgrn_head(zq[c][h], zf[c][h], zi[c][h], zg[c][h], l0[h], l1[h], l2[h], gn, S[h])
            row.append(o)
            s_next.append(sn)
        mix.append(row)
        S = s_next
    att = [_attention(zx[a], mk[a], mv[a]) for a in range(XA_HEADS)]
    return mix, att, S


def _hgrn_fwd2(z, lb_logits, gnorm, kv, bl, seq):
    T, zw = z.shape
    mem_len = kv.shape[0] // bl
    cat_w = HG_HEADS * HG_DIM + XA_HEADS * XA_DIM
    R = HG_SUB * HG_CHUNK
    nb = seq // R

    def kern(z_ref, lb_ref, gn_ref, kv_ref, cat_ref, st_ref, s_scr):
        @pl.when(pl.program_id(1) == 0)
        def _():
            s_scr[...] = jnp.zeros(s_scr.shape, F32)

        st_ref[...] = s_scr[...]
        zq, zf, zi, zg, zx = _hgrn_rows(z_ref)
        mk, mv = _kv_pieces(kv_ref)
        l0, l1, l2 = _lb_pieces(lb_ref)
        S = [s_scr[h] for h in range(HG_HEADS)]
        mix, att, s_new = _hgrn_steps(zq, zf, zi, zg, zx, l0, l1, l2, gn_ref[...], mk, mv, S)
        for c in range(HG_SUB):
            for h in range(HG_HEADS):
                cat_ref[c * HG_CHUNK:(c + 1) * HG_CHUNK, h * HG_DIM:(h + 1) * HG_DIM] = mix[c][h].astype(cat_ref.dtype)
        for h in range(HG_HEADS):
            s_scr[h] = s_new[h]
        base = HG_HEADS * HG_DIM
        for a in range(XA_HEADS):
            cat_ref[:, base + a * XA_DIM:base + (a + 1) * XA_DIM] = att[a].astype(cat_ref.dtype)

    return pl.pallas_call(
        kern, grid=(bl, nb),
        in_specs=[pl.BlockSpec((R, zw), lambda b, n: (b * nb + n, 0)),
                  pl.BlockSpec(lb_logits.shape, lambda b, n: (0, 0)),
                  pl.BlockSpec(gnorm.shape, lambda b, n: (0, 0)),
                  pl.BlockSpec((mem_len, kv.shape[1]), lambda b, n: (b, 0))],
        out_specs=[pl.BlockSpec((R, cat_w), lambda b, n: (b * nb + n, 0)),
                   pl.BlockSpec((None, HG_HEADS, HG_DIM, HG_DIM), lambda b, n: (b * nb + n, 0, 0, 0))],
        out_shape=[jax.ShapeDtypeStruct((T, cat_w), BF),
                   jax.ShapeDtypeStruct((bl * nb, HG_HEADS, HG_DIM, HG_DIM), F32)],
        scratch_shapes=[pltpu.VMEM((HG_HEADS, HG_DIM, HG_DIM), F32)],
        name="hgrn_fwd",
        compiler_params=pltpu.CompilerParams(dimension_semantics=("arbitrary", "arbitrary"), vmem_limit_bytes=VMEM_CAP_BYTES),
    )(z, lb_logits, gnorm, kv)


def _hgrn_bwd2(z, dcat, stash, lb_logits, gnorm, kv, bl, seq):
    T, zw = z.shape
    mem_len = kv.shape[0] // bl
    cat_w = dcat.shape[1]
    R = HG_SUB * HG_CHUNK
    nb = seq // R

    def kern(z_ref, dc_ref, st_ref, lb_ref, gn_ref, kv_ref, dz_ref, dkv_ref, dlb_ref, dgn_ref, ds_scr):
        first = jnp.logical_and(pl.program_id(0) == 0, pl.program_id(1) == 0)

        @pl.when(pl.program_id(1) == 0)
        def _():
            ds_scr[...] = jnp.zeros(ds_scr.shape, F32)
            dkv_ref[...] = jnp.zeros(dkv_ref.shape, F32)

        @pl.when(first)
        def _():
            dlb_ref[...] = jnp.zeros(dlb_ref.shape, F32)
            dgn_ref[...] = jnp.zeros(dgn_ref.shape, F32)

        zq, zf, zi, zg, zx = _hgrn_rows(z_ref)
        mk, mv = _kv_pieces(kv_ref)
        l0, l1, l2 = _lb_pieces(lb_ref)
        S = [st_ref[h] for h in range(HG_HEADS)]
        _, vjp = jax.vjp(_hgrn_steps, zq, zf, zi, zg, zx, l0, l1, l2, gn_ref[...], mk, mv, S)
        d_mix = [[dc_ref[c * HG_CHUNK:(c + 1) * HG_CHUNK, h * HG_DIM:(h + 1) * HG_DIM] for h in range(HG_HEADS)]
                 for c in range(HG_SUB)]
        base = HG_HEADS * HG_DIM
        d_att = [dc_ref[:, base + a * XA_DIM:base + (a + 1) * XA_DIM] for a in range(XA_HEADS)]
        d_s = [ds_scr[h] for h in range(HG_HEADS)]
        dzq, dzf, dzi, dzg, dzx, dl0, dl1, dl2, dgn, dmk, dmv, dS = vjp((d_mix, d_att, d_s))
        W = HG_HEADS * HG_DIM
        for c in range(HG_SUB):
            rows = slice(c * HG_CHUNK, (c + 1) * HG_CHUNK)
            for h in range(HG_HEADS):
                for k, part in enumerate((dzq, dzf, dzi, dzg)):
                    dz_ref[rows, k * W + h * HG_DIM:k * W + (h + 1) * HG_DIM] = part[c][h].astype(dz_ref.dtype)
        for h in range(HG_HEADS):
            sl = slice(h * HG_DIM, (h + 1) * HG_DIM)
            ds_scr[h] = dS[h]
            dlb_ref[0:1, sl] += dl0[h]
            dlb_ref[1:2, sl] += dl1[h]
            dlb_ref[2:3, sl] += dl2[h]
        dgn_ref[...] += dgn
        KW = XA_HEADS * XA_DIM
        for a in range(XA_HEADS):
            dz_ref[:, 4 * W + a * XA_DIM:4 * W + (a + 1) * XA_DIM] = dzx[a].astype(dz_ref.dtype)
            dkv_ref[:, a * XA_DIM:(a + 1) * XA_DIM] += dmk[a]
            dkv_ref[:, KW + a * XA_DIM:KW + (a + 1) * XA_DIM] += dmv[a]

    rev = lambda b, n: (b * nb + (nb - 1 - n), 0)
    return pl.pallas_call(
        kern, grid=(bl, nb),
        in_specs=[pl.BlockSpec((R, zw), rev),
                  pl.BlockSpec((R, cat_w), rev),
                  pl.BlockSpec((None, HG_HEADS, HG_DIM, HG_DIM), lambda b, n: (b * nb + (nb - 1 - n), 0, 0, 0)),
                  pl.BlockSpec(lb_logits.shape, lambda b, n: (0, 0)),
                  pl.BlockSpec(gnorm.shape, lambda b, n: (0, 0)),
                  pl.BlockSpec((mem_len, kv.shape[1]), lambda b, n: (b, 0))],
        out_specs=[pl.BlockSpec((R, zw), rev),
                   pl.BlockSpec((mem_len, kv.shape[1]), lambda b, n: (b, 0)),
                   pl.BlockSpec(lb_logits.shape, lambda b, n: (0, 0)),
                   pl.BlockSpec(gnorm.shape, lambda b, n: (0, 0))],
        out_shape=[jax.ShapeDtypeStruct((T, zw), BF), jax.ShapeDtypeStruct(kv.shape, F32),
                   jax.ShapeDtypeStruct(lb_logits.shape, F32), jax.ShapeDtypeStruct(gnorm.shape, F32)],
        scratch_shapes=[pltpu.VMEM((HG_HEADS, HG_DIM, HG_DIM), F32)],
        name="hgrn_bwd",
        compiler_params=pltpu.CompilerParams(dimension_semantics=("arbitrary", "arbitrary"), vmem_limit_bytes=VMEM_CAP_BYTES),
    )(z, dcat, stash, lb_logits, gnorm, kv)


GM_SUB = 2


def _gmlp_pieces(z_ref):
    W = GM_GROUPS * GM_GROUP_DIM
    zu = [z_ref[:, g * GM_GROUP_DIM:(g + 1) * GM_GROUP_DIM] for g in range(GM_GROUPS)]
    zv = [z_ref[:, W + g * GM_GROUP_DIM:W + (g + 1) * GM_GROUP_DIM] for g in range(GM_GROUPS)]
    zx = [z_ref[:, 2 * W + a * XA_DIM:2 * W + (a + 1) * XA_DIM] for a in range(XA_HEADS)]
    return zu, zv, zx


def _gmlp_params(lng_ref, lnb_ref, ws_ref, bs_ref):
    lng = [lng_ref[:, g * GM_GROUP_DIM:(g + 1) * GM_GROUP_DIM] for g in range(GM_GROUPS)]
    lnb = [lnb_ref[:, g * GM_GROUP_DIM:(g + 1) * GM_GROUP_DIM] for g in range(GM_GROUPS)]
    ws = [ws_ref[g] for g in range(GM_GROUPS)]
    bs = [bs_ref[g:g + 1, :] for g in range(GM_GROUPS)]
    return lng, lnb, ws, bs


def _gmlp_fwd(z, ln_g, ln_b, w_s, b_s, kv, bl, nc):
    T, zw = z.shape
    mem_len = kv.shape[0] // bl
    cat_w = GM_GROUPS * GM_GROUP_DIM + XA_HEADS * XA_DIM

    assert nc % GM_SUB == 0
    nc = nc // GM_SUB
    R = GM_SUB * GM_CHUNK

    def kern(z_ref, lng_ref, lnb_ref, ws_ref, bs_ref, kv_ref, cat_ref):
        lng, lnb, ws, bs = _gmlp_params(lng_ref, lnb_ref, ws_ref, bs_ref)
        mk, mv = _kv_pieces(kv_ref)
        for c in range(GM_SUB):
            rows = pl.ds(c * GM_CHUNK, GM_CHUNK)
            zu, zv, zx = _gmlp_pieces(z_ref.at[rows])
            out = cat_ref.at[rows]
            outs = _gmlp_block(zu, zv, zx, lng, lnb, ws, bs, mk, mv)
            for g in range(GM_GROUPS):
                out[:, g * GM_GROUP_DIM:(g + 1) * GM_GROUP_DIM] = outs[g].astype(cat_ref.dtype)
            base = GM_GROUPS * GM_GROUP_DIM
            for a in range(XA_HEADS):
                out[:, base + a * XA_DIM:base + (a + 1) * XA_DIM] = outs[GM_GROUPS + a].astype(cat_ref.dtype)

    full2 = lambda b, n: (0, 0)
    return pl.pallas_call(
        kern, grid=(bl, nc),
        in_specs=[pl.BlockSpec((R, zw), lambda b, n: (b * nc + n, 0)),
                  pl.BlockSpec(ln_g.shape, full2), pl.BlockSpec(ln_b.shape, full2),
                  pl.BlockSpec(w_s.shape, lambda b, n: (0, 0, 0)), pl.BlockSpec(b_s.shape, full2),
                  pl.BlockSpec((mem_len, kv.shape[1]), lambda b, n: (b, 0))],
        out_specs=pl.BlockSpec((R, cat_w), lambda b, n: (b * nc + n, 0)),
        out_shape=jax.ShapeDtypeStruct((T, cat_w), BF),
        name="gmlp_fwd",
        compiler_params=pltpu.CompilerParams(dimension_semantics=("arbitrary", "arbitrary"), vmem_limit_bytes=VMEM_CAP_BYTES),
    )(z, ln_g, ln_b, w_s, b_s, kv)


def _gmlp_bwd(z, dcat, ln_g, ln_b, w_s, b_s, kv, bl, nc):
    T, zw = z.shape
    mem_len = kv.shape[0] // bl
    cat_w = dcat.shape[1]
    assert nc % GM_SUB == 0
    nc = nc // GM_SUB

    def kern(z_ref, dc_ref, lng_ref, lnb_ref, ws_ref, bs_ref, kv_ref,
             dz_ref, dkv_ref, dlng_ref, dlnb_ref, dws_ref, dbs_ref):
        first = jnp.logical_and(pl.program_id(0) == 0, pl.program_id(1) == 0)

        @pl.when(pl.program_id(1) == 0)
        def _():
            dkv_ref[...] = jnp.zeros(dkv_ref.shape, F32)

        @pl.when(first)
        def _():
            dlng_ref[...] = jnp.zeros(dlng_ref.shape, F32)
            dlnb_ref[...] = jnp.zeros(dlnb_ref.shape, F32)
            dws_ref[...] = jnp.zeros(dws_ref.shape, F32)
            dbs_ref[...] = jnp.zeros(dbs_ref.shape, F32)

        lng, lnb, ws, bs = _gmlp_params(lng_ref, lnb_ref, ws_ref, bs_ref)
        mk, mv = _kv_pieces(kv_ref)
        W = GM_GROUPS * GM_GROUP_DIM
        KW = XA_HEADS * XA_DIM
        for c in range(GM_SUB):
            rows = pl.ds(c * GM_CHUNK, GM_CHUNK)
            zu, zv, zx = _gmlp_pieces(z_ref.at[rows])
            dc, dz = dc_ref.at[rows], dz_ref.at[rows]
            _, vjp = jax.vjp(_gmlp_block, zu, zv, zx, lng, lnb, ws, bs, mk, mv)
            d_outs = [dc[:, g * GM_GROUP_DIM:(g + 1) * GM_GROUP_DIM] for g in range(GM_GROUPS)]
            d_outs += [dc[:, W + a * XA_DIM:W + (a + 1) * XA_DIM] for a in range(XA_HEADS)]
            dzu, dzv, dzx, dlng, dlnb, dws, dbs, dmk, dmv = vjp(d_outs)
            for g in range(GM_GROUPS):
                sl = slice(g * GM_GROUP_DIM, (g + 1) * GM_GROUP_DIM)
                dz[:, sl] = dzu[g].astype(dz_ref.dtype)
                dz[:, W + g * GM_GROUP_DIM:W + (g + 1) * GM_GROUP_DIM] = dzv[g].astype(dz_ref.dtype)
                dlng_ref[:, sl] += dlng[g]
                dlnb_ref[:, sl] += dlnb[g]
                dws_ref[g] += dws[g]
                dbs_ref[g:g + 1, :] += dbs[g]
            for a in range(XA_HEADS):
                dz[:, 2 * W + a * XA_DIM:2 * W + (a + 1) * XA_DIM] = dzx[a].astype(dz_ref.dtype)
                dkv_ref[:, a * XA_DIM:(a + 1) * XA_DIM] += dmk[a]
                dkv_ref[:, KW + a * XA_DIM:KW + (a + 1) * XA_DIM] += dmv[a]

    full2 = lambda b, n: (0, 0)
    full3 = lambda b, n: (0, 0, 0)
    blk = lambda b, n: (b * nc + n, 0)
    return pl.pallas_call(
        kern, grid=(bl, nc),
        in_specs=[pl.BlockSpec((GM_SUB * GM_CHUNK, zw), blk), pl.BlockSpec((GM_SUB * GM_CHUNK, cat_w), blk),
                  pl.BlockSpec(ln_g.shape, full2), pl.BlockSpec(ln_b.shape, full2),
                  pl.BlockSpec(w_s.shape, full3), pl.BlockSpec(b_s.shape, full2),
                  pl.BlockSpec((mem_len, kv.shape[1]), lambda b, n: (b, 0))],
        out_specs=[pl.BlockSpec((GM_SUB * GM_CHUNK, zw), blk),
                   pl.BlockSpec((mem_len, kv.shape[1]), lambda b, n: (b, 0)),
                   pl.BlockSpec(ln_g.shape, full2), pl.BlockSpec(ln_b.shape, full2),
                   pl.BlockSpec(w_s.shape, full3), pl.BlockSpec(b_s.shape, full2)],
        out_shape=[jax.ShapeDtypeStruct((T, zw), BF), jax.ShapeDtypeStruct(kv.shape, F32),
                   jax.ShapeDtypeStruct(ln_g.shape, F32), jax.ShapeDtypeStruct(ln_b.shape, F32),
                   jax.ShapeDtypeStruct(w_s.shape, F32), jax.ShapeDtypeStruct(b_s.shape, F32)],
        name="gmlp_bwd",
        compiler_params=pltpu.CompilerParams(dimension_semantics=("arbitrary", "arbitrary"), vmem_limit_bytes=VMEM_CAP_BYTES),
    )(z, dcat, ln_g, ln_b, w_s, b_s, kv)


def _place():
    x, y, c = lax.axis_index("x"), lax.axis_index("y"), lax.axis_index("c")
    chips = [(1 - x, y), (x, 1 - y), (1 - x, 1 - y)]
    return x, y, c, chips


def _half(ref, kind, e):
    if kind == "col":
        n = ref.shape[1] // 2
        return ref.at[:, pl.ds(pl.multiple_of(e * n, n), n), :]
    n = ref.shape[2] // 2
    return ref.at[:, :, pl.ds(pl.multiple_of(e * n, n), n)]


def _slot(ref, kind, j, n):
    if kind == "col":
        return ref.at[:, :, pl.ds(pl.multiple_of(j * n, n), n)]
    return ref.at[:, pl.ds(pl.multiple_of(j * n, n), n), :]


BF16_TILE_ROWS = 16
AG_DIRECT_SIXTEENTHS = 3


def _allgather_seq(name, items, cid, chip_arr):
    nt = len(items)
    kinds = [k for (_, k, _) in items]
    slot_kind = ["row" if k == "row" else "col" for k in kinds]
    out_type = []
    for s, k, l in items:
        L, r, c = s.shape
        lo = L if l is None else 1
        out_type.append(jax.ShapeDtypeStruct((lo, 4 * r, c) if k == "row" else (lo, r, 4 * c), s.dtype))

    def part(ref, t, e):
        return ref if kinds[t] == "vec" else _half(ref, kinds[t], e)

    def split(half):
        rows = half.shape[1]
        direct = rows * AG_DIRECT_SIXTEENTHS // 16 // BF16_TILE_ROWS * BF16_TILE_ROWS
        return half.at[:, pl.ds(0, rows - direct), :], half.at[:, pl.ds(rows - direct, direct), :]

    def body(*refs):
        sh = [refs[t] if items[t][2] is None else refs[t].at[pl.ds(items[t][2], 1)] for t in range(nt)]
        full = refs[nt:2 * nt]
        s_ici, r_ici, s_far, r_far, s_d2d, r_d2d = refs[2 * nt:]
        x, y, c, chips = _place()
        own = 2 * x + y
        sibling = (x, y, 1 - c)
        barrier = pltpu.get_barrier_semaphore()
        for peer in [(px, py, pc) for (px, py) in chips for pc in (0, 1)] + [sibling]:
            pl.semaphore_signal(barrier, inc=1, device_id=peer, device_id_type=MESH)
        pl.semaphore_wait(barrier, 7)
        width = [sh[t].shape[1] if kinds[t] == "row" else sh[t].shape[2] for t in range(nt)]
        sent = []
        for t in range(nt):
            for p, (px, py) in enumerate(chips):
                src, dst = part(sh[t], t, c), part(_slot(full[t], slot_kind[t], own, width[t]), t, c)
                cp = pltpu.make_async_remote_copy(
                    src_ref=src, dst_ref=dst, send_sem=s_ici.at[t, p], recv_sem=r_ici.at[t, p], device_id=(px, py, c),
                    device_id_type=MESH)
                cp.start()
                sent.append(cp)
                if kinds[t] == "vec":
                    continue
                far = pltpu.make_async_remote_copy(
                    src_ref=split(src)[1], dst_ref=split(dst)[1], send_sem=s_far.at[t, p], recv_sem=r_far.at[t, p],
                    device_id=(px, py, 1 - c), device_id_type=MESH)
                far.start()
                sent.append(far)
        for t in range(nt):
            for p, (px, py) in enumerate(chips):
                landed = part(_slot(full[t], slot_kind[t], 2 * px + py, width[t]), t, c)
                pltpu.make_async_remote_copy(
                    src_ref=landed, dst_ref=landed, send_sem=s_ici.at[t, p], recv_sem=r_ici.at[t, p],
                    device_id=(px, py, c), device_id_type=MESH).wait_recv()
                if kinds[t] == "vec":
                    continue
                fw = pltpu.make_async_remote_copy(
                    src_ref=split(landed)[0], dst_ref=split(landed)[0], send_sem=s_d2d.at[t, p], recv_sem=r_d2d.at[t, p],
                    device_id=sibling, device_id_type=MESH)
                fw.start()
                sent.append(fw)
        for t in range(nt):
            if kinds[t] == "vec":
                continue
            for p, (px, py) in enumerate(chips):
                forwarded, direct = split(_half(_slot(full[t], kinds[t], 2 * px + py, width[t]), kinds[t], 1 - c))
                pltpu.make_async_remote_copy(
                    src_ref=forwarded, dst_ref=forwarded, send_sem=s_d2d.at[t, p], recv_sem=r_d2d.at[t, p],
                    device_id=sibling, device_id_type=MESH).wait_recv()
                pltpu.make_async_remote_copy(
                    src_ref=direct, dst_ref=direct, send_sem=s_far.at[t, p], recv_sem=r_far.at[t, p],
                    device_id=(px, py, 1 - c), device_id_type=MESH).wait_recv()
        for cp in sent:
            cp.wait_send()

    sems = pltpu.SemaphoreType.DMA
    fulls = pl.kernel(
        body, out_type=out_type, mesh=plsc.ScalarSubcoreMesh(axis_name="seq", num_cores=1),
        scratch_types=[sems((nt, 3)), sems((nt, 3)), sems((nt, 3)), sems((nt, 3)), sems((nt, 3)), sems((nt, 3))],
        compiler_params=pltpu.CompilerParams(collective_id=cid), name=name,
    )(*[s for (s, _, _) in items])
    return [_place_own("%s_own_%d" % (name, t), fulls[t], items[t][0], slot_kind[t], items[t][2], chip_arr) for t in range(nt)]


def _place_own(name, full, shard, kind, layer, chip_arr):
    lo, r, c = (shard.shape[0] if layer is None else 1,) + shard.shape[1:]
    first = 0 if layer is None else layer
    tr = _pick(r, 512)
    nr = r // tr

    def body(chip_ref, s_ref, f_ref, o_ref):
        o_ref[...] = s_ref[...]

    if kind == "row":
        out_map = lambda i, j, chip: (i, chip[0] * nr + j, 0)
    else:
        out_map = lambda i, j, chip: (i, j, chip[0])
    return pl.pallas_call(
        body, out_shape=jax.ShapeDtypeStruct(full.shape, full.dtype),
        grid_spec=pltpu.PrefetchScalarGridSpec(
            num_scalar_prefetch=1, grid=(lo, nr),
            in_specs=[pl.BlockSpec((1, tr, c), lambda i, j, chip: (i + first, j, 0)), pl.BlockSpec(memory_space=pl.ANY)],
            out_specs=pl.BlockSpec((1, tr, c), out_map)),
        input_output_aliases={2: 0},
        compiler_params=pltpu.CompilerParams(dimension_semantics=("parallel", "parallel"), vmem_limit_bytes=VMEM_CAP_BYTES),
        name=name,
    )(chip_arr, shard, full)


def _slot2(ref, kind, j, n):
    if kind == "col":
        return ref.at[:, pl.ds(pl.multiple_of(j * n, n), n)]
    return ref.at[pl.ds(pl.multiple_of(j * n, n), n), :]


def _rs_chips_seq(name, parts, kinds, cid):
    nm = len(parts)
    out_type = []
    for g, k in zip(parts, kinds):
        r, c = g.shape
        ps = (r, c // 4) if k == "col" else (r // 4, c)
        out_type += [jax.ShapeDtypeStruct(ps, BF), jax.ShapeDtypeStruct((3,) + ps, BF)]

    def body(*refs):
        g = refs[:nm]
        outs = refs[nm:3 * nm]
        loc, ssem, rsem = refs[3 * nm:]
        x, y, c, chips = _place()
        own = 2 * x + y
        barrier = pltpu.get_barrier_semaphore()
        for (px, py) in chips:
            pl.semaphore_signal(barrier, inc=1, device_id=(px, py, c), device_id_type=MESH)
        pl.semaphore_wait(barrier, 3)
        cps = []
        for m in range(nm):
            k = kinds[m]
            own_o, got_o = outs[2 * m], outs[2 * m + 1]
            n = g[m].shape[1] // 4 if k == "col" else g[m].shape[0] // 4
            lc = pltpu.make_async_copy(_slot2(g[m], k, own, n), own_o, loc.at[m])
            lc.start()
            cps.append(lc)
            for p, (px, py) in enumerate(chips):
                cp = pltpu.make_async_remote_copy(
                    src_ref=_slot2(g[m], k, 2 * px + py, n), dst_ref=got_o.at[p],
                    send_sem=ssem.at[m, p], recv_sem=rsem.at[m, p], device_id=(px, py, c), device_id_type=MESH)
                cp.start()
                cps.append(cp)
        for cp in cps:
            cp.wait()

    return pl.kernel(
        body, out_type=out_type, mesh=plsc.ScalarSubcoreMesh(axis_name="seq", num_cores=1),
        scratch_types=[pltpu.SemaphoreType.DMA((nm,)), pltpu.SemaphoreType.DMA((nm, 3)), pltpu.SemaphoreType.DMA((nm, 3))],
        compiler_params=pltpu.CompilerParams(collective_id=cid), name=name,
    )(*parts)


def _finish_share(name, owns, gots, kind, c_arr):
    L = len(owns)
    r, c = owns[0].shape
    tr = _pick(r, 128 if kind == "col" else 256)
    nb = r // tr
    nq = L * nb

    def chunk_of(l):
        return lambda h, q: jnp.clip(q * (1 - h) + (nq - 1) * h - l * nb, 0, nb - 1)

    ins, in_specs = [], []
    for l in range(L):
        at = chunk_of(l)
        ins += [owns[l], gots[l].reshape(3 * r, c), gots[l].reshape(3 * r, c), gots[l].reshape(3 * r, c)]
        in_specs.append(pl.BlockSpec((tr, c), functools.partial(lambda h, q, cc, at: (at(h, q), 0), at=at)))
        in_specs += [pl.BlockSpec((tr, c), functools.partial(lambda h, q, cc, at, p: (p * nb + at(h, q), 0), at=at, p=p))
                     for p in range(3)]
    if kind == "col":
        out_sd = (L, 2, r, c)
        o_spec = pl.BlockSpec((None, 2, tr, c), lambda h, q, cc: ((q * h) // nb, 0, (q * h) % nb, 0))
    else:
        out_sd = (L * r, 2 * c)
        o_spec = pl.BlockSpec((tr, 2 * c), lambda h, q, cc: (q * h, 0))

    def kern(c_ref, *refs):
        in_refs = refs[:4 * L]
        out_ref, mine, recv, ssem, rsem = refs[4 * L:]
        h, q = pl.program_id(0), pl.program_id(1)
        x, y, cc, _ = _place()

        def swap(qq):
            return pltpu.make_async_remote_copy(src_ref=mine.at[qq], dst_ref=recv.at[qq], send_sem=ssem.at[qq],
                                                recv_sem=rsem.at[qq], device_id=(x, y, 1 - cc), device_id_type=MESH)

        for l in range(L):
            @pl.when(jnp.logical_and(h == 0, q // nb == l))
            def _(l=l):
                o_ref, g0, g1, g2 = in_refs[4 * l:4 * l + 4]
                mine[q] = ((o_ref[...].astype(F32) + g0[...].astype(F32)) + g1[...].astype(F32)) + g2[...].astype(F32)
                swap(q).start()

        @pl.when(h == 1)
        def _():
            swap(q).wait()
            a, b = mine[q], recv[q]
            first = c_ref[0] == 0
            lo, hi = jnp.where(first, a, b), jnp.where(first, b, a)
            if kind == "col":
                out_ref[0] = lo
                out_ref[1] = hi
            else:
                out_ref[:, :c] = lo
                out_ref[:, c:] = hi

    full = pl.pallas_call(
        kern,
        grid_spec=pltpu.PrefetchScalarGridSpec(
            num_scalar_prefetch=1, grid=(2, nq), in_specs=in_specs, out_specs=o_spec,
            scratch_shapes=[pltpu.VMEM((nq, tr, c), F32), pltpu.VMEM((nq, tr, c), F32),
                            pltpu.SemaphoreType.DMA((nq,)), pltpu.SemaphoreType.DMA((nq,))]),
        out_shape=jax.ShapeDtypeStruct(out_sd, F32), name=name,
        compiler_params=pltpu.CompilerParams(dimension_semantics=("arbitrary", "arbitrary"),
                                             vmem_limit_bytes=VMEM_CAP_BYTES),
    )(c_arr, *ins)
    return full.reshape(L, 2 * r, c) if kind == "col" else full.reshape(L, r, 2 * c)


def _small_allreduce(buf, name):
    R = buf.shape[0]
    assert R % 16 == 0
    h = R // 2

    def body(x_ref, o_ref, sib, csum, got, s_a, r_a, s_b, r_b, s_c, r_c):
        x, y, c, chips = _place()
        sibling = (x, y, 1 - c)
        own = 2 * x + y
        swap = pltpu.make_async_remote_copy(src_ref=x_ref, dst_ref=sib, send_sem=s_a, recv_sem=r_a,
                                            device_id=sibling, device_id_type=MESH)
        swap.start()
        swap.wait()
        a, b = x_ref[...], sib[...]
        south = c == 0
        csum[...] = jnp.where(south, a, b) + jnp.where(south, b, a)
        lo = pl.multiple_of(c * h, 8)
        mine = csum.at[pl.ds(lo, h)]
        got[own] = csum[pl.ds(lo, h)]
        sends = []
        for p, (px, py) in enumerate(chips):
            cp = pltpu.make_async_remote_copy(src_ref=mine, dst_ref=got.at[own], send_sem=s_b.at[p], recv_sem=r_b.at[p],
                                              device_id=(px, py, c), device_id_type=MESH)
            cp.start()
            sends.append(cp)
        for cp in sends:
            cp.wait()
        o_ref[pl.ds(lo, h)] = ((got[0] + got[1]) + got[2]) + got[3]
        done = o_ref.at[pl.ds(lo, h)]
        back = pltpu.make_async_remote_copy(src_ref=done, dst_ref=done, send_sem=s_c, recv_sem=r_c,
                                            device_id=sibling, device_id_type=MESH)
        back.start()
        back.wait_send()
        other = o_ref.at[pl.ds(pl.multiple_of((1 - c) * h, 8), h)]
        pltpu.make_async_remote_copy(src_ref=other, dst_ref=other, send_sem=s_c, recv_sem=r_c,
                                     device_id=sibling, device_id_type=MESH).wait_recv()

    vm = pl.BlockSpec(memory_space=pltpu.VMEM)
    return pl.pallas_call(
        body, out_shape=jax.ShapeDtypeStruct(buf.shape, F32), in_specs=[vm], out_specs=vm,
        scratch_shapes=[pltpu.VMEM((R, LANES), F32), pltpu.VMEM((R, LANES), F32), pltpu.VMEM((4, h, LANES), F32),
                        pltpu.SemaphoreType.DMA, pltpu.SemaphoreType.DMA, pltpu.SemaphoreType.DMA((3,)),
                        pltpu.SemaphoreType.DMA((3,)), pltpu.SemaphoreType.DMA, pltpu.SemaphoreType.DMA],
        name=name,
        compiler_params=pltpu.CompilerParams(vmem_limit_bytes=VMEM_CAP_BYTES),
    )(buf)


PACK_TILE_ROWS = 8


def _item_rows(shape):
    n = 1
    for d in shape:
        n *= d
    return -(-n // (PACK_TILE_ROWS * LANES)) * PACK_TILE_ROWS


def _pack(arrs, rows_total):
    buf = jnp.zeros((rows_total, LANES), F32)
    r = 0
    for a in arrs:
        f = a.reshape(-1).astype(F32)
        nr = _item_rows(a.shape)
        block = jnp.pad(f, (0, nr * LANES - f.shape[0])).reshape(nr, LANES)
        buf = lax.dynamic_update_slice(buf, block, (r, 0))
        r += nr
    return buf


def _unpack(buf, shapes):
    out, r = [], 0
    for s in shapes:
        n = 1
        for d in s:
            n *= d
        nr = _item_rows(s)
        out.append(buf[r:r + nr].reshape(-1)[:n].reshape(s))
        r += nr
    return out


def _rows_needed(shapes):
    return -(-sum(_item_rows(s) for s in shapes) // (2 * PACK_TILE_ROWS)) * (2 * PACK_TILE_ROWS)


def _two_rows(a, b):
    out = jnp.zeros((2, a.shape[1]), a.dtype)
    return lax.dynamic_update_slice(lax.dynamic_update_slice(out, a, (0, 0)), b, (1, 0))


def _adam(w, g, m, v):
    m = ADAM_B1 * m + (1.0 - ADAM_B1) * g
    v = ADAM_B2 * v + (1.0 - ADAM_B2) * jnp.square(g)
    m_hat = m / (1.0 - ADAM_B1 ** ADAM_STEP)
    v_hat = v / (1.0 - ADAM_B2 ** ADAM_STEP)
    delta = -ADAM_LR * (m_hat / (jnp.sqrt(v_hat) + ADAM_EPS) + ADAM_WD * w)
    return delta, m, v


def _adam_call(name, w2, g2, m2, v2, tr, pass_grad=False):
    def fn(rv, cv):
        outs = list(_adam(*rv))
        return ([rv[1]] + outs if pass_grad else outs), []

    width = w2.shape[1]
    return _rowcall(name, fn, [(w2, 0, width), (g2, 0, width), (m2, 0, width), (v2, 0, width)], [],
                    [(width, F32)] * (4 if pass_grad else 3), [], tr)


def kernel(x, mem, mem_norm, lb_logits, ffn1_norm, ffn1_w_in, ffn1_w_out, mix_norm, mem_w_kv, hgrn_w_in, hgrn_gnorm, hgrn_w_out, gmlp_w_in, gmlp_ln_g, gmlp_ln_b, gmlp_w_s, gmlp_b_s, gmlp_w_out, ffn2_norm, ffn2_w_in, ffn2_w_out, final_norm, loss_target, m_mem_norm, m_lb_logits, m_ffn1_norm, m_ffn1_w_in, m_ffn1_w_out, m_mix_norm, m_mem_w_kv, m_hgrn_w_in, m_hgrn_gnorm, m_hgrn_w_out, m_gmlp_w_in, m_gmlp_ln_g, m_gmlp_ln_b, m_gmlp_w_s, m_gmlp_b_s, m_gmlp_w_out, m_ffn2_norm, m_ffn2_w_in, m_ffn2_w_out, m_final_norm, v_mem_norm, v_lb_logits, v_ffn1_norm, v_ffn1_w_in, v_ffn1_w_out, v_mix_norm, v_mem_w_kv, v_hgrn_w_in, v_hgrn_gnorm, v_hgrn_w_out, v_gmlp_w_in, v_gmlp_ln_g, v_gmlp_ln_b, v_gmlp_w_s, v_gmlp_b_s, v_gmlp_w_out, v_ffn2_norm, v_ffn2_w_in, v_ffn2_w_out, v_final_norm):
    bl, seq, D = x.shape
    T = bl * seq
    mem_len = mem.shape[1]
    chip = 2 * lax.axis_index("x") + lax.axis_index("y")
    c_arr = lax.axis_index("c").astype(jnp.int32).reshape(1)
    chip_arr = chip.astype(jnp.int32).reshape(1)
    TR = 1024

    big = [("ffn1_w_in", ffn1_w_in, "col"), ("ffn1_w_out", ffn1_w_out, "row"), ("mem_w_kv", mem_w_kv, "col"),
           ("hgrn_w_in", hgrn_w_in, "col"), ("hgrn_w_out", hgrn_w_out, "row"), ("gmlp_w_in", gmlp_w_in, "col"),
           ("gmlp_w_out", gmlp_w_out, "row"), ("ffn2_w_in", ffn2_w_in, "col"), ("ffn2_w_out", ffn2_w_out, "row")]
    kinds = [k for (_, _, k) in big]
    shards_bf = []
    for nm, w, _ in big:
        L, r, c = w.shape
        (wb,) = _rowcall("cast_" + nm, lambda rv, cv: ([rv[0]], []), [(w.reshape(L * r, c), 0, c)], [], [(c, BF)], [], 512)
        shards_bf.append(wb.reshape(L, r, c))
    sb = dict(zip([nm for (nm, _, _) in big], shards_bf))
    groups = [[("ffn1_w_in", 0)], [("ffn1_w_out", 0)], [("hgrn_w_in", None)], [("mem_w_kv", None)], [("hgrn_w_out", None)],
              [("ffn2_w_in", 0), ("ffn2_w_out", 0), ("gmlp_ln_g", None), ("gmlp_ln_b", None)],
              [("ffn1_w_in", 1), ("ffn1_w_out", 1)],
              [("gmlp_w_in", None), ("gmlp_w_out", None)],
              [("ffn2_w_in", 1), ("ffn2_w_out", 1)]]
    kind_of = {nm: k for (nm, _, k) in big}
    for nm, vec in (("gmlp_ln_g", gmlp_ln_g), ("gmlp_ln_b", gmlp_ln_b)):
        sb[nm] = vec.reshape(1, 1, -1)
        kind_of[nm] = "vec"
    gathered = {nm: [None, None] for nm in ("ffn1_w_in", "ffn1_w_out", "ffn2_w_in", "ffn2_w_out")}
    for gi, grp in enumerate(groups):
        outs = _allgather_seq("gather_%d" % gi, [(sb[nm], kind_of[nm], l) for (nm, l) in grp], gi, chip_arr)
        for (nm, l), o in zip(grp, outs):
            if l is None:
                gathered[nm] = o
            else:
                gathered[nm][l] = o

    ln_w = GM_GROUPS * GM_GROUP_DIM
    ln_g_full, ln_b_full = gathered["gmlp_ln_g"].reshape(1, ln_w), gathered["gmlp_ln_b"].reshape(1, ln_w)

    def rms_fwd(name, xin, g):
        (h,) = _rowcall(name, lambda rv, cv: ([_rmsnorm(rv[0], cv[0])], []), [(xin, 0, D)], [g.reshape(1, D)], [(D, BF)], [], TR)
        return h

    def ffn_fwd(tag, xin, h, w_in, w_out, layer, next_gain):
        dff = w_out[layer].shape[1]
        zg, zu, a = _ffn_in_swiglu("ffn_in_" + tag, h, w_in[layer], 1024, dff // 2)
        out = _mm("ffn_out_" + tag, a, w_out[layer], "nn", F32, 1024, 1024, dff, scale=0.5, res=xin, b_lead=0,
                  norm_gain=None if next_gain is None else next_gain.reshape(1, D))
        xo, h_next = (out, None) if next_gain is None else out
        return xo, h_next, (xin, h, zg, zu, a)

    def ffn_bwd(tag, dxo, saved, g, w_in, w_out, layer):
        xin, h, zg, zu, a = saved
        dff = w_out[layer].shape[1]
        dw_out = _mm_tn_pair("ffn_dwo_" + tag, a, dxo, "row", c_arr, dff // 2, T, scale=0.5)
        dz = _ffn_da_swiglu("ffn_da_" + tag, dxo, w_out[layer], zg, zu, 512)
        dw_in = _mm_tn_pair("ffn_dwi_" + tag, h, dz, "col", c_arr, 512, T)
        dx, dg = _mm_dh_rms("ffn_dh_" + tag, dz, w_in[layer], xin, g.reshape(1, D), dxo, 512)
        return dx, dg, dw_in, dw_out

    def rms_bwd(name, xin, g, dh, dres):
        def fn(rv, cv):
            _, vjp = jax.vjp(_rmsnorm, rv[0], cv[0])
            dx, dg = vjp(rv[1])
            if dres is not None:
                dx = dx + rv[2]
            return [dx], [dg]

        rows = [(xin, 0, D), (dh, 0, D)] + ([(dres, 0, D)] if dres is not None else [])
        dx, dg = _rowcall(name, fn, rows, [g.reshape(1, D)], [(D, F32)], [((1, D), F32)], TR)
        return dx, dg

    x0 = x.reshape(T, D)
    tgt = loss_target.reshape(T, D)
    mem2 = mem.reshape(bl * mem_len, D)
    memn = rms_fwd("rms_mem", mem2, mem_norm)

    h_f10 = rms_fwd("rms_f1l0", x0, ffn1_norm[0])
    x1, h_m0, sv_f10 = ffn_fwd("f1l0", x0, h_f10, gathered["ffn1_w_in"], gathered["ffn1_w_out"], 0, mix_norm[0])
    z_m0 = _mm("mix_in_0", h_m0, gathered["hgrn_w_in"], "nn", F32, 2048, 512, D, b_lead=0)
    kv = [_mm("kv_%d" % i, memn, gathered["mem_w_kv"], "nn", F32, 512, 512, D, b_lead=i) for i in range(2)]
    cat0, stash0 = _hgrn_fwd2(z_m0, lb_logits, hgrn_gnorm, kv[0], bl, seq)
    x2, h_f20 = _mm("mix_out_0", cat0, gathered["hgrn_w_out"], "nn", F32, 1024, 1024, cat0.shape[1], res=x1, b_lead=0,
                    norm_gain=ffn2_norm[0].reshape(1, D))
    x3, h_f11, sv_f20 = ffn_fwd("f2l0", x2, h_f20, gathered["ffn2_w_in"], gathered["ffn2_w_out"], 0, ffn1_norm[1])
    x4, h_m1, sv_f11 = ffn_fwd("f1l1", x3, h_f11, gathered["ffn1_w_in"], gathered["ffn1_w_out"], 1, mix_norm[1])
    z_m1 = _mm("mix_in_1", h_m1, gathered["gmlp_w_in"], "nn", F32, 2048, 512, D, b_lead=0)
    nc1 = seq // GM_CHUNK
    w_s, b_s = gmlp_w_s[0], gmlp_b_s[0]
    cat1 = _gmlp_fwd(z_m1, ln_g_full, ln_b_full, w_s, b_s, kv[1], bl, nc1)
    x5, h_f21 = _mm("mix_out_1", cat1, gathered["gmlp_w_out"], "nn", F32, 1024, 1024, cat1.shape[1], res=x4, b_lead=0,
                    norm_gain=ffn2_norm[1].reshape(1, D))
    x6, _, sv_f21 = ffn_fwd("f2l1", x5, h_f21, gathered["ffn2_w_in"], gathered["ffn2_w_out"], 1, None)

    def head(rv, cv):
        def f(xx, gg):
            err = _rmsnorm(xx, gg) - rv[1]
            return 0.5 * jnp.sum(jnp.mean(err * err, axis=-1, keepdims=True), axis=0, keepdims=True)

        ls, vjp = jax.vjp(f, rv[0], cv[0])
        dx, dg = vjp(jnp.ones((1, 1), F32))
        return [dx], [dg, jnp.broadcast_to(ls, (1, 128))]

    dx6, d_final, loss_part = _rowcall("loss_head", head, [(x6, 0, D), (tgt, 0, D)], [final_norm.reshape(1, D)],
                                       [(D, F32)], [((1, D), F32), ((1, 128), F32)], TR)

    rs_out = {}
    n_gather = len(groups)

    def rs(gi, items):
        outs = _rs_chips_seq("reduce_%d" % gi, [p for (_, p, _) in items], [k for (_, _, k) in items], n_gather + gi)
        for i, (key, _, _) in enumerate(items):
            rs_out[key] = (outs[2 * i], outs[2 * i + 1])

    dx5, dg_f21, dwi_f21, dwo_f21 = ffn_bwd("f2l1", dx6, sv_f21, ffn2_norm[1], gathered["ffn2_w_in"], gathered["ffn2_w_out"], 1)
    rs(0, [(("ffn2_w_out", 1), dwo_f21, "row"), (("ffn2_w_in", 1), dwi_f21, "col")])
    dcat1 = _mm("mix_dcat_1", dx5, gathered["gmlp_w_out"], "nt", F32, 2048, 1024, D, b_lead=0)
    dwo_m1 = _mm_tn_pair("mix_dwo_1", cat1, dx5, "row", c_arr, 1024, T)
    dz_m1, dkv1, d_lng, d_lnb, d_ws, d_bs = _gmlp_bwd(z_m1, dcat1, ln_g_full, ln_b_full, w_s, b_s, kv[1], bl, nc1)
    dx4, dg_m1 = _mm_dh_rms("mix_dh_1", dz_m1, gathered["gmlp_w_in"], x4, mix_norm[1].reshape(1, D), dx5, 512)
    dwi_m1 = _mm_tn_pair("mix_dwi_1", h_m1, dz_m1, "col", c_arr, 1024, T)
    rs(1, [(("gmlp_w_out", 0), dwo_m1, "row"), (("gmlp_w_in", 0), dwi_m1, "col")])
    dx3, dg_f11, dwi_f11, dwo_f11 = ffn_bwd("f1l1", dx4, sv_f11, ffn1_norm[1], gathered["ffn1_w_in"], gathered["ffn1_w_out"], 1)
    rs(2, [(("ffn1_w_out", 1), dwo_f11, "row"), (("ffn1_w_in", 1), dwi_f11, "col")])

    dx2, dg_f20, dwi_f20, dwo_f20 = ffn_bwd("f2l0", dx3, sv_f20, ffn2_norm[0], gathered["ffn2_w_in"], gathered["ffn2_w_out"], 0)
    rs(3, [(("ffn2_w_out", 0), dwo_f20, "row"), (("ffn2_w_in", 0), dwi_f20, "col")])
    dcat0 = _mm("mix_dcat_0", dx2, gathered["hgrn_w_out"], "nt", F32, 2048, 1024, D, b_lead=0)
    dwo_m0 = _mm_tn_pair("mix_dwo_0", cat0, dx2, "row", c_arr, 1024, T)
    dz_m0, dkv0, d_lb, d_gn = _hgrn_bwd2(z_m0, dcat0, stash0, lb_logits, hgrn_gnorm, kv[0], bl, seq)
    dx1, dg_m0 = _mm_dh_rms("mix_dh_0", dz_m0, gathered["hgrn_w_in"], x1, mix_norm[0].reshape(1, D), dx2, 512)
    dwi_m0 = _mm_tn_pair("mix_dwi_0", h_m0, dz_m0, "col", c_arr, 1024, T)
    rs(4, [(("hgrn_w_out", 0), dwo_m0, "row"), (("hgrn_w_in", 0), dwi_m0, "col")])

    dwkv = [_mm_tn_pair("kv_dw_%d" % i, memn, dkv, "col", c_arr, 1024, 512) for i, dkv in enumerate([dkv0, dkv1])]
    rs(5, [(("mem_w_kv", 0), dwkv[0], "col"), (("mem_w_kv", 1), dwkv[1], "col")])
    dmemn = _mm("kv_dx_0", dkv0, gathered["mem_w_kv"], "nt", F32, 512, 512, 1024, b_lead=0)
    dmemn = _mm("kv_dx_1", dkv1, gathered["mem_w_kv"], "nt", F32, 512, 512, 1024, res=dmemn, b_lead=1)
    _, d_memnorm = rms_bwd("rms_bwd_mem", mem2, mem_norm, dmemn, None)

    dx0, dg_f10, dwi_f10, dwo_f10 = ffn_bwd("f1l0", dx1, sv_f10, ffn1_norm[0], gathered["ffn1_w_in"], gathered["ffn1_w_out"], 0)
    rs(6, [(("ffn1_w_out", 0), dwo_f10, "row")])
    rs(7, [(("ffn1_w_in", 0), dwi_f10, "col")])

    shard_grads = [_finish_share("finish_" + nm, [rs_out[(nm, l)][0] for l in range(w.shape[0])],
                                 [rs_out[(nm, l)][1] for l in range(w.shape[0])], k, c_arr) for (nm, w, k) in big]

    big_w = [w for (_, w, _) in big]
    big_m = [m_ffn1_w_in, m_ffn1_w_out, m_mem_w_kv, m_hgrn_w_in, m_hgrn_w_out, m_gmlp_w_in, m_gmlp_w_out, m_ffn2_w_in, m_ffn2_w_out]
    big_v = [v_ffn1_w_in, v_ffn1_w_out, v_mem_w_kv, v_hgrn_w_in, v_hgrn_w_out, v_gmlp_w_in, v_gmlp_w_out, v_ffn2_w_in, v_ffn2_w_out]
    big_out = {}
    for (nm, w, _), g, m, v in zip(big, shard_grads, big_m, big_v):
        L, r, c = w.shape
        g2, d2, m2, v2 = _adam_call("adam_" + nm, w.reshape(L * r, c), g.reshape(L * r, c), m.reshape(L * r, c),
                                    v.reshape(L * r, c), 256, pass_grad=True)
        big_out[nm] = (g2.reshape(w.shape), d2.reshape(w.shape), m2.reshape(w.shape), v2.reshape(w.shape))

    d_ffn1n = _two_rows(dg_f10, dg_f11)
    d_mixn = _two_rows(dg_m0, dg_m1)
    d_ffn2n = _two_rows(dg_f20, dg_f21)
    small_parts = [loss_part[:, :1], d_memnorm, d_lb, d_ffn1n, d_mixn, d_gn, d_lng, d_lnb, d_ws, d_bs, d_ffn2n, d_final]
    red_shapes = [(1,), mem_norm.shape, lb_logits.shape, ffn1_norm.shape, mix_norm.shape, hgrn_gnorm.shape, (1, ln_w), (1, ln_w),
                  gmlp_w_s.shape, gmlp_b_s.shape, ffn2_norm.shape, final_norm.shape]
    red = _small_allreduce(_pack(small_parts, _rows_needed(red_shapes)), "reduce_small")
    (loss_v, g_memn, g_lb, g_f1n, g_mixn, g_gn, g_lng_full, g_lnb_full, g_ws, g_bs, g_f2n, g_fin) = _unpack(red, red_shapes)
    lsh = gmlp_ln_g.shape[1]
    g_lng = lax.dynamic_slice(g_lng_full, (0, chip * lsh), (1, lsh))
    g_lnb = lax.dynamic_slice(g_lnb_full, (0, chip * lsh), (1, lsh))
    small_w = [mem_norm, lb_logits, ffn1_norm, mix_norm, hgrn_gnorm, gmlp_ln_g, gmlp_ln_b, gmlp_w_s, gmlp_b_s, ffn2_norm, final_norm]
    small_g = [g_memn, g_lb, g_f1n, g_mixn, g_gn, g_lng, g_lnb, g_ws, g_bs, g_f2n, g_fin]
    small_m = [m_mem_norm, m_lb_logits, m_ffn1_norm, m_mix_norm, m_hgrn_gnorm, m_gmlp_ln_g, m_gmlp_ln_b, m_gmlp_w_s, m_gmlp_b_s, m_ffn2_norm, m_final_norm]
    small_v = [v_mem_norm, v_lb_logits, v_ffn1_norm, v_mix_norm, v_hgrn_gnorm, v_gmlp_ln_g, v_gmlp_ln_b, v_gmlp_w_s, v_gmlp_b_s, v_ffn2_norm, v_final_norm]
    sshapes = [w.shape for w in small_w]
    nrow = _rows_needed(sshapes)
    d_p, m_p, v_p = _adam_call("adam_small", _pack(small_w, nrow), _pack(small_g, nrow), _pack(small_m, nrow), _pack(small_v, nrow), nrow)
    s_delta, s_m, s_v = _unpack(d_p, sshapes), _unpack(m_p, sshapes), _unpack(v_p, sshapes)
    small_names = ["mem_norm", "lb_logits", "ffn1_norm", "mix_norm", "hgrn_gnorm", "gmlp_ln_g", "gmlp_ln_b", "gmlp_w_s", "gmlp_b_s", "ffn2_norm", "final_norm"]
    small_out = {nm: (g.reshape(w.shape), d, m, v) for nm, w, g, d, m, v in zip(small_names, small_w, small_g, s_delta, s_m, s_v)}

    order = ["mem_norm", "lb_logits", "ffn1_norm", "ffn1_w_in", "ffn1_w_out", "mix_norm", "mem_w_kv", "hgrn_w_in", "hgrn_gnorm",
             "hgrn_w_out", "gmlp_w_in", "gmlp_ln_g", "gmlp_ln_b", "gmlp_w_s", "gmlp_b_s", "gmlp_w_out", "ffn2_norm", "ffn2_w_in",
             "ffn2_w_out", "final_norm"]
    allo = {**big_out, **small_out}
    grad_x = dx0.reshape(x.shape)
    return (loss_v.reshape(()), grad_x, *[allo[n][0] for n in order], *[allo[n][1] for n in order],
            *[allo[n][2] for n in order], *[allo[n][3] for n in order])
```

```python
import functools

import jax
import jax.numpy as jnp
from jax import lax
from jax.experimental import pallas as pl
from jax.experimental.pallas import tpu as pltpu
from jax.experimental.pallas import tpu_sc as plsc

BF = jnp.bfloat16
F32 = jnp.float32
MESH = pl.DeviceIdType.MESH

EPS = 1e-6
D_MODEL = 1024
HG_HEADS = 8
HG_DIM = 128
HG_CHUNK = 64
GM_CHUNK = 128
GM_GROUPS = 8
GM_GROUP_DIM = 256
XA_HEADS = 4
XA_DIM = 256
ADAM_LR = 0.001
ADAM_B1 = 0.9
ADAM_B2 = 0.999
ADAM_EPS = 1e-08
ADAM_WD = 0.01
ADAM_STEP = 10

VMEM_CAP_BYTES = 60 * 1024 * 1024
LANES = 1024


def _pick(n, cap, mult=16):
    if n <= cap:
        return n
    for d in range(cap - cap % mult, 0, -mult):
        if n % d == 0:
            return d
    raise ValueError((n, cap, mult))


def _dg(a, b, ca, cb):
    return lax.dot_general(a.astype(BF), b.astype(BF), (((ca,), (cb,)), ((), ())), preferred_element_type=F32)


@jax.custom_vjp
def dot_nn(a, b):
    return _dg(a, b, 1, 0)


def _nn_fwd(a, b):
    return _dg(a, b, 1, 0), (a, b)


def _nn_bwd(r, g):
    a, b = r
    return _dg(g, b, 1, 1), _dg(a, g, 0, 0)


dot_nn.defvjp(_nn_fwd, _nn_bwd)


@jax.custom_vjp
def dot_nt(a, b):
    return _dg(a, b, 1, 1)


def _nt_fwd(a, b):
    return _dg(a, b, 1, 1), (a, b)


def _nt_bwd(r, g):
    a, b = r
    return _dg(g, b, 1, 0), _dg(g, a, 0, 0)


dot_nt.defvjp(_nt_fwd, _nt_bwd)


@jax.custom_vjp
def dot_tn(a, b):
    return _dg(a, b, 0, 0)


def _tn_fwd(a, b):
    return _dg(a, b, 0, 0), (a, b)


def _tn_bwd(r, g):
    a, b = r
    return _dg(b, g, 1, 1), _dg(a, g, 1, 0)


dot_tn.defvjp(_tn_fwd, _tn_bwd)


def _rmsnorm(x, g):
    return x * lax.rsqrt(jnp.mean(x * x, axis=-1, keepdims=True) + EPS) * g


def _silu(x):
    return x * jax.nn.sigmoid(x)


@jax.custom_vjp
def _gelu(x):
    return 0.5 * x * (1.0 + lax.erf(x * (0.5 ** 0.5)))


def _gelu_fwd(x):
    return _gelu(x), x


def _gelu_bwd(x, g):
    t = x * (0.5 ** 0.5)
    cdf = 0.5 * (1.0 + lax.erf(t))
    return (g * (cdf + x * (jnp.exp(-(t * t)) * (0.5 / 3.141592653589793) ** 0.5)),)


_gelu.defvjp(_gelu_fwd, _gelu_bwd)


def _softmax_last(s):
    m = lax.stop_gradient(jnp.max(s, axis=-1, keepdims=True))
    e = jnp.exp(s - m)
    return e / jnp.sum(e, axis=-1, keepdims=True)


def _tril(n):
    r = lax.broadcasted_iota(jnp.int32, (n, n), 0)
    c = lax.broadcasted_iota(jnp.int32, (n, n), 1)
    return r >= c


def _cumsum_rows(l):
    n = l.shape[0]
    return lax.dot_general(_tril(n).astype(F32), l, (((1,), (0,)), ((), ())),
                           precision=lax.Precision.HIGHEST, preferred_element_type=F32)


def _attention(zx, mk, mv):
    s = dot_nt(zx, mk) * (XA_DIM ** -0.5)
    return dot_nn(_softmax_last(s), mv)


def _hgrn_head(zq, zf, zi, zg, l0, l1, l2, gn, S):
    m = lax.stop_gradient(jnp.maximum(jnp.maximum(l0, l1), l2))
    e0 = jnp.exp(l0 - m)
    lb = e0 / (e0 + jnp.exp(l1 - m) + jnp.exp(l2 - m))
    q = _silu(zq)
    f = lb + (1.0 - lb) * jax.nn.sigmoid(zf)
    k = 1.0 - f
    b = _cumsum_rows(jnp.log(f))
    b_last = b[HG_CHUNK - 1:HG_CHUNK, :]
    q_dec = q * jnp.exp(b)
    k_inv = k * jnp.exp(-b)
    a = jnp.where(_tril(HG_CHUNK), dot_nt(q_dec, k_inv), 0.0)
    o = dot_nn(a, zi) + dot_nn(q_dec, S)
    S_new = jnp.exp(b_last).reshape(HG_DIM, 1) * S + dot_tn(k * jnp.exp(b_last - b), zi)
    o = _rmsnorm(o, gn) * _silu(zg)
    return o, S_new


def _gmlp_block(zu, zv, zx, lng, lnb, ws, bs, mk, mv):
    gv = [_gelu(v) for v in zv]
    width = GM_GROUPS * GM_GROUP_DIM
    mu = sum(jnp.sum(g, axis=-1, keepdims=True) for g in gv) / width
    xc = [g - mu for g in gv]
    var = sum(jnp.sum(c * c, axis=-1, keepdims=True) for c in xc) / width
    r = lax.rsqrt(var + EPS)
    outs = []
    for g in range(GM_GROUPS):
        v = xc[g] * r * lng[g] + lnb[g]
        w = jnp.where(_tril(GM_CHUNK), ws[g], 0.0)
        mixed = dot_nn(w, v) + bs[g].reshape(GM_CHUNK, 1)
        outs.append(_gelu(zu[g]) * mixed)
    for a in range(XA_HEADS):
        outs.append(_attention(zx[a], mk[a], mv[a]))
    return outs


def _rowcall(name, fn, rows, consts, row_outs, acc_outs, tr):
    nrows = rows[0][0].shape[0]
    tr = _pick(nrows, tr)
    n_r, n_c, n_ro, n_ao = len(rows), len(consts), len(row_outs), len(acc_outs)

    def kern(*refs):
        rv = [r[...] for r in refs[:n_r]]
        cv = [r[...] for r in refs[n_r:n_r + n_c]]
        ro_refs = refs[n_r + n_c:n_r + n_c + n_ro]
        ao_refs = refs[n_r + n_c + n_ro:]
        ro, ao = fn(rv, cv)
        for ref, v in zip(ro_refs, ro):
            ref[...] = v.astype(ref.dtype)
        if n_ao:
            @pl.when(pl.program_id(0) == 0)
            def _():
                for ref in ao_refs:
                    ref[...] = jnp.zeros(ref.shape, ref.dtype)

            for ref, v in zip(ao_refs, ao):
                ref[...] += v.astype(ref.dtype)

    in_specs = [pl.BlockSpec((tr, w), functools.partial(lambda i, cb: (i, cb), cb=cb)) for (_, cb, w) in rows]
    in_specs += [pl.BlockSpec(c.shape, lambda i: (0, 0)) for c in consts]
    out_specs = [pl.BlockSpec((tr, w), lambda i: (i, 0)) for (w, _) in row_outs]
    out_specs += [pl.BlockSpec(s, lambda i: (0, 0)) for (s, _) in acc_outs]
    out_shape = [jax.ShapeDtypeStruct((nrows, w), dt) for (w, dt) in row_outs]
    out_shape += [jax.ShapeDtypeStruct(s, dt) for (s, dt) in acc_outs]
    outs = pl.pallas_call(
        kern, grid=(nrows // tr,), in_specs=in_specs, out_specs=out_specs, out_shape=out_shape, name=name,
        compiler_params=pltpu.CompilerParams(dimension_semantics=("arbitrary",),
                                             vmem_limit_bytes=VMEM_CAP_BYTES),
    )(*[a for (a, _, _) in rows], *consts)
    return outs


def _mm(name, a, b, mode, out_dtype, tm, tn, tk, scale=1.0, res=None, a_lead=None, b_lead=None, norm_gain=None):
    ash = a.shape[-2:]
    bsh = b.shape[-2:]
    if mode == "nn":
        (M, K), (K2, N) = ash, bsh
    elif mode == "nt":
        (M, K), (N, K2) = ash, bsh
    else:
        (K, M), (K2, N) = ash, bsh
    assert K == K2, (name, a.shape, b.shape)
    tm, tn, tk = min(tm, M), min(tn, N), min(tk, K)
    assert M % tm == 0 and N % tn == 0 and K % tk == 0, (name, M, N, K, tm, tn, tk)
    nk = K // tk
    dims = {"nn": (1, 0), "nt": (1, 1), "tn": (0, 0)}[mode]

    def lead(spec_shape, index_fn, lead_idx):
        if lead_idx is None:
            return pl.BlockSpec(spec_shape, index_fn)
        return pl.BlockSpec((None,) + spec_shape, lambda i, j, k: (lead_idx,) + index_fn(i, j, k))

    if mode == "tn":
        a_spec = lead((tk, tm), lambda i, j, k: (k, i), a_lead)
    else:
        a_spec = lead((tm, tk), lambda i, j, k: (i, k), a_lead)
    if mode == "nt":
        b_spec = lead((tn, tk), lambda i, j, k: (j, k), b_lead)
    else:
        b_spec = lead((tk, tn), lambda i, j, k: (k, j), b_lead)
    o_spec = pl.BlockSpec((tm, tn), lambda i, j, k: (i, j))
    has_res = res is not None
    has_norm = norm_gain is not None
    assert not has_norm or tn == N

    def kern(*refs):
        a_ref, b_ref = refs[0], refs[1]
        pos = 2
        res_ref = gain_ref = h_ref = None
        if has_res:
            res_ref, pos = refs[pos], pos + 1
        if has_norm:
            gain_ref, pos = refs[pos], pos + 1
        o_ref, pos = refs[pos], pos + 1
        if has_norm:
            h_ref = refs[pos]
        acc_ref = refs[-1] if nk > 1 else None
        p = lax.dot_general(a_ref[...].astype(BF), b_ref[...].astype(BF), (((dims[0],), (dims[1],)), ((), ())),
                            preferred_element_type=F32)

        def finish(v):
            if scale != 1.0:
                v = v * scale
            if has_res:
                v = res_ref[...] + v
            o_ref[...] = v.astype(o_ref.dtype)
            if has_norm:
                h_ref[...] = _rmsnorm(v, gain_ref[...]).astype(h_ref.dtype)

        if nk == 1:
            finish(p)
        else:
            k = pl.program_id(2)

            @pl.when(k == 0)
            def _():
                acc_ref[...] = p

            @pl.when(k > 0)
            def _():
                acc_ref[...] += p

            @pl.when(k == nk - 1)
            def _():
                finish(acc_ref[...])

    ins = [a, b] + ([res] if has_res else []) + ([norm_gain] if has_norm else [])
    in_specs = [a_spec, b_spec] + ([o_spec] if has_res else [])
    in_specs += [pl.BlockSpec((1, N), lambda i, j, k: (0, 0))] if has_norm else []
    out_sd = jax.ShapeDtypeStruct((M, N), out_dtype)
    return pl.pallas_call(
        kern, grid=(M // tm, N // tn, nk), in_specs=in_specs,
        out_specs=[o_spec, o_spec] if has_norm else o_spec,
        out_shape=[out_sd, jax.ShapeDtypeStruct((M, N), BF)] if has_norm else out_sd,
        scratch_shapes=[pltpu.VMEM((tm, tn), F32)] if nk > 1 else [],
        name=name,
        compiler_params=pltpu.CompilerParams(dimension_semantics=("parallel", "parallel", "arbitrary"),
                                             vmem_limit_bytes=VMEM_CAP_BYTES),
    )(*ins)


def _ffn_in_swiglu(name, h, w3, tm, tn):
    T, D = h.shape
    dff = w3.shape[2] // 2
    tm = min(tm, T)
    assert T % tm == 0 and dff % tn == 0
    nj = dff // tn

    def kern(h_ref, wg_ref, wu_ref, zg_ref, zu_ref, a_ref):
        hb = h_ref[...]
        g = jnp.dot(hb, wg_ref[...], preferred_element_type=F32).astype(BF)
        u = jnp.dot(hb, wu_ref[...], preferred_element_type=F32).astype(BF)
        zg_ref[...] = g
        zu_ref[...] = u
        a_ref[...] = (_silu(g.astype(F32)) * u.astype(F32)).astype(BF)

    o_spec = pl.BlockSpec((tm, tn), lambda i, j: (i, j))
    return pl.pallas_call(
        kern, grid=(T // tm, nj),
        in_specs=[pl.BlockSpec((tm, D), lambda i, j: (i, 0)),
                  pl.BlockSpec((None, D, tn), lambda i, j: (0, 0, j)),
                  pl.BlockSpec((None, D, tn), lambda i, j: (0, 0, j + nj))],
        out_specs=[o_spec, o_spec, o_spec],
        out_shape=[jax.ShapeDtypeStruct((T, dff), BF)] * 3, name=name,
        compiler_params=pltpu.CompilerParams(dimension_semantics=("parallel", "arbitrary"),
                                             vmem_limit_bytes=VMEM_CAP_BYTES),
    )(h, w3, w3)


def _ffn_da_swiglu(name, dxo, w3, zg, zu, tm):
    T, D = dxo.shape
    dff = w3.shape[1]
    tm = min(tm, T)
    assert T % tm == 0 and dff % 2 == 0
    hc = dff // 2

    def kern(d_ref, w_ref, g_ref, u_ref, dz_ref):
        db = (d_ref[...] * 0.5).astype(BF)
        for s in range(2):
            cols = slice(s * hc, (s + 1) * hc)
            da = lax.dot_general(db, w_ref[cols, :], (((1,), (1,)), ((), ())), preferred_element_type=F32)
            g = g_ref[:, cols].astype(F32)
            sg = 1.0 / (1.0 + jnp.exp(-g))
            gs = g * sg
            dab = da.astype(BF)
            dz_ref[:, cols] = (dab * u_ref[:, cols]) * (sg + gs * (1.0 - sg)).astype(BF)
            dz_ref[:, dff + s * hc:dff + (s + 1) * hc] = dab * gs.astype(BF)

    row = lambda w: pl.BlockSpec((tm, w), lambda i: (i, 0))
    return pl.pallas_call(
        kern, grid=(T // tm,),
        in_specs=[row(D), pl.BlockSpec((None, dff, D), lambda i: (0, 0, 0), pipeline_mode=pl.Buffered(1)), row(dff), row(dff)],
        out_specs=row(2 * dff), out_shape=jax.ShapeDtypeStruct((T, 2 * dff), BF), name=name,
        compiler_params=pltpu.CompilerParams(dimension_semantics=("arbitrary",), vmem_limit_bytes=VMEM_CAP_BYTES),
    )(dxo, w3, zg, zu)


def _mm_dh_rms(name, dz, w3, xin, g, dres, tm):
    T, K = dz.shape
    D = w3.shape[1]
    tm = min(tm, T)
    assert T % tm == 0

    def kern(dz_ref, w_ref, x_ref, g_ref, r_ref, dx_ref, dg_ref):
        dh = lax.dot_general(dz_ref[...], w_ref[...], (((1,), (1,)), ((), ())), preferred_element_type=F32)
        _, vjp = jax.vjp(_rmsnorm, x_ref[...], g_ref[...])
        dx, dg = vjp(dh)
        dx_ref[...] = dx + r_ref[...]

        @pl.when(pl.program_id(0) == 0)
        def _():
            dg_ref[...] = jnp.zeros(dg_ref.shape, F32)

        dg_ref[...] += dg

    row = lambda w: pl.BlockSpec((tm, w), lambda i: (i, 0))
    one = pl.BlockSpec((1, D), lambda i: (0, 0))
    return pl.pallas_call(
        kern, grid=(T // tm,),
        in_specs=[row(K), pl.BlockSpec((None, D, K), lambda i: (0, 0, 0), pipeline_mode=pl.Buffered(1)), row(D), one, row(D)],
        out_specs=[row(D), one], out_shape=[jax.ShapeDtypeStruct((T, D), F32), jax.ShapeDtypeStruct((1, D), F32)], name=name,
        compiler_params=pltpu.CompilerParams(dimension_semantics=("arbitrary",), vmem_limit_bytes=VMEM_CAP_BYTES),
    )(dz, w3, xin, g, dres)


def _mm_tn_pair(name, a, b, kind, c_arr, tq, tk, scale=1.0):
    T, M = a.shape
    _, N = b.shape
    tk = min(tk, T)
    assert T % tk == 0
    nk = T // tk
    if kind == "col":
        hm = M // 2
        assert N % tq == 0
        nq = N // tq
        tile = (hm, tq)
        a_spec = pl.BlockSpec((tk, hm), lambda h, q, k, c: (k, jnp.bitwise_xor(h, 1 - c[0])))
        b_spec = pl.BlockSpec((tk, tq), lambda h, q, k, c: (k, q))
        o_spec = pl.BlockSpec(tile, lambda h, q, k, c: (0, q * h))
        out_sd = (hm, N)
    else:
        hn = N // 2
        assert M % tq == 0
        nq = M // tq
        tile = (tq, hn)
        a_spec = pl.BlockSpec((tk, tq), lambda h, q, k, c: (k, q))
        b_spec = pl.BlockSpec((tk, hn), lambda h, q, k, c: (k, jnp.bitwise_xor(h, 1 - c[0])))
        o_spec = pl.BlockSpec(tile, lambda h, q, k, c: (q * h, 0))
        out_sd = (M, hn)

    def kern(c_ref, a_ref, b_ref, o_ref, acc, stage, recv, ssem, rsem):
        h, q, k = pl.program_id(0), pl.program_id(1), pl.program_id(2)
        x, y, c, _ = _place()
        p = lax.dot_general(a_ref[...].astype(BF), b_ref[...].astype(BF), (((0,), (0,)), ((), ())), preferred_element_type=F32)

        @pl.when(k == 0)
        def _():
            acc[...] = p

        @pl.when(k > 0)
        def _():
            acc[...] += p

        def send(slot, qq):
            return pltpu.make_async_remote_copy(src_ref=stage.at[slot], dst_ref=recv.at[qq], send_sem=ssem.at[slot],
                                                recv_sem=rsem.at[qq], device_id=(x, y, 1 - c), device_id_type=MESH)

        last = k == nk - 1

        @pl.when(jnp.logical_and(last, h == 0))
        def _():
            slot = q % 2

            @pl.when(q >= 2)
            def _():
                send(slot, q).wait_send()

            stage[slot] = (acc[...] * scale).astype(BF)
            send(slot, q).start()

        @pl.when(jnp.logical_and(last, h == 1))
        def _():
            @pl.when(q == 0)
            def _():
                for s in range(min(nq, 2)):
                    send(s, 0).wait_send()

            send(0, q).wait_recv()
            o_ref[...] = (acc[...] * scale + recv[q].astype(F32)).astype(o_ref.dtype)

    return pl.pallas_call(
        kern,
        grid_spec=pltpu.PrefetchScalarGridSpec(
            num_scalar_prefetch=1, grid=(2, nq, nk), in_specs=[a_spec, b_spec], out_specs=o_spec,
            scratch_shapes=[pltpu.VMEM(tile, F32), pltpu.VMEM((2,) + tile, BF), pltpu.VMEM((nq,) + tile, BF),
                            pltpu.SemaphoreType.DMA((2,)), pltpu.SemaphoreType.DMA((nq,))]),
        out_shape=jax.ShapeDtypeStruct(out_sd, BF), name=name,
        compiler_params=pltpu.CompilerParams(dimension_semantics=("arbitrary", "arbitrary", "arbitrary"),
                                             vmem_limit_bytes=VMEM_CAP_BYTES),
    )(c_arr, a, b)


def _kv_pieces(kv_ref):
    W = XA_HEADS * XA_DIM
    mk = [kv_ref[:, a * XA_DIM:(a + 1) * XA_DIM] for a in range(XA_HEADS)]
    mv = [kv_ref[:, W + a * XA_DIM:W + (a + 1) * XA_DIM] for a in range(XA_HEADS)]
    return mk, mv


def _lb_pieces(lb_ref):
    return [[lb_ref[r:r + 1, h * HG_DIM:(h + 1) * HG_DIM] for h in range(HG_HEADS)] for r in range(3)]


HG_SUB = 4


def _hgrn_rows(z_ref):
    W = HG_HEADS * HG_DIM

    def piece(c, col, w):
        return z_ref[c * HG_CHUNK:(c + 1) * HG_CHUNK, col:col + w]

    zq = [[piece(c, h * HG_DIM, HG_DIM) for h in range(HG_HEADS)] for c in range(HG_SUB)]
    zf = [[piece(c, W + h * HG_DIM, HG_DIM) for h in range(HG_HEADS)] for c in range(HG_SUB)]
    zi = [[piece(c, 2 * W + h * HG_DIM, HG_DIM) for h in range(HG_HEADS)] for c in range(HG_SUB)]
    zg = [[piece(c, 3 * W + h * HG_DIM, HG_DIM) for h in range(HG_HEADS)] for c in range(HG_SUB)]
    zx = [z_ref[:, 4 * W + a * XA_DIM:4 * W + (a + 1) * XA_DIM] for a in range(XA_HEADS)]
    return zq, zf, zi, zg, zx


def _hgrn_steps(zq, zf, zi, zg, zx, l0, l1, l2, gn, mk, mv, S):
    mix = []
    for c in range(HG_SUB):
        row, s_next = [], []
        for h in range(HG_HEADS):
            o, sn = _hgrn_head(zq[c][h], zf[c][h], zi[c][h], zg[c][h], l0[h], l1[h], l2[h], gn, S[h])
            row.append(o)
            s_next.append(sn)
        mix.append(row)
        S = s_next
    att = [_attention(zx[a], mk[a], mv[a]) for a in range(XA_HEADS)]
    return mix, att, S


def _hgrn_fwd2(z, lb_logits, gnorm, kv, bl, seq):
    T, zw = z.shape
    mem_len = kv.shape[0] // bl
    cat_w = HG_HEADS * HG_DIM + XA_HEADS * XA_DIM
    R = HG_SUB * HG_CHUNK
    nb = seq // R

    def kern(z_ref, lb_ref, gn_ref, kv_ref, cat_ref, st_ref, s_scr):
        @pl.when(pl.program_id(1) == 0)
        def _():
            s_scr[...] = jnp.zeros(s_scr.shape, F32)

        st_ref[...] = s_scr[...]
        zq, zf, zi, zg, zx = _hgrn_rows(z_ref)
        mk, mv = _kv_pieces(kv_ref)
        l0, l1, l2 = _lb_pieces(lb_ref)
        S = [s_scr[h] for h in range(HG_HEADS)]
        mix, att, s_new = _hgrn_steps(zq, zf, zi, zg, zx, l0, l1, l2, gn_ref[...], mk, mv, S)
        for c in range(HG_SUB):
            for h in range(HG_HEADS):
                cat_ref[c * HG_CHUNK:(c + 1) * HG_CHUNK, h * HG_DIM:(h + 1) * HG_DIM] = mix[c][h].astype(cat_ref.dtype)
        for h in range(HG_HEADS):
            s_scr[h] = s_new[h]
        base = HG_HEADS * HG_DIM
        for a in range(XA_HEADS):
            cat_ref[:, base + a * XA_DIM:base + (a + 1) * XA_DIM] = att[a].astype(cat_ref.dtype)

    return pl.pallas_call(
        kern, grid=(bl, nb),
        in_specs=[pl.BlockSpec((R, zw), lambda b, n: (b * nb + n, 0)),
                  pl.BlockSpec(lb_logits.shape, lambda b, n: (0, 0)),
                  pl.BlockSpec(gnorm.shape, lambda b, n: (0, 0)),
                  pl.BlockSpec((mem_len, kv.shape[1]), lambda b, n: (b, 0))],
        out_specs=[pl.BlockSpec((R, cat_w), lambda b, n: (b * nb + n, 0)),
                   pl.BlockSpec((None, HG_HEADS, HG_DIM, HG_DIM), lambda b, n: (b * nb + n, 0, 0, 0))],
        out_shape=[jax.ShapeDtypeStruct((T, cat_w), BF),
                   jax.ShapeDtypeStruct((bl * nb, HG_HEADS, HG_DIM, HG_DIM), F32)],
        scratch_shapes=[pltpu.VMEM((HG_HEADS, HG_DIM, HG_DIM), F32)],
        name="hgrn_fwd",
        compiler_params=pltpu.CompilerParams(dimension_semantics=("arbitrary", "arbitrary"), vmem_limit_bytes=VMEM_CAP_BYTES),
    )(z, lb_logits, gnorm, kv)


def _hgrn_bwd2(z, dcat, stash, lb_logits, gnorm, kv, bl, seq):
    T, zw = z.shape
    mem_len = kv.shape[0] // bl
    cat_w = dcat.shape[1]
    R = HG_SUB * HG_CHUNK
    nb = seq // R

    def kern(z_ref, dc_ref, st_ref, lb_ref, gn_ref, kv_ref, dz_ref, dkv_ref, dlb_ref, dgn_ref, ds_scr):
        first = jnp.logical_and(pl.program_id(0) == 0, pl.program_id(1) == 0)

        @pl.when(pl.program_id(1) == 0)
        def _():
            ds_scr[...] = jnp.zeros(ds_scr.shape, F32)
            dkv_ref[...] = jnp.zeros(dkv_ref.shape, F32)

        @pl.when(first)
        def _():
            dlb_ref[...] = jnp.zeros(dlb_ref.shape, F32)
            dgn_ref[...] = jnp.zeros(dgn_ref.shape, F32)

        zq, zf, zi, zg, zx = _hgrn_rows(z_ref)
        mk, mv = _kv_pieces(kv_ref)
        l0, l1, l2 = _lb_pieces(lb_ref)
        S = [st_ref[h] for h in range(HG_HEADS)]
        _, vjp = jax.vjp(_hgrn_steps, zq, zf, zi, zg, zx, l0, l1, l2, gn_ref[...], mk, mv, S)
        d_mix = [[dc_ref[c * HG_CHUNK:(c + 1) * HG_CHUNK, h * HG_DIM:(h + 1) * HG_DIM] for h in range(HG_HEADS)]
                 for c in range(HG_SUB)]
        base = HG_HEADS * HG_DIM
        d_att = [dc_ref[:, base + a * XA_DIM:base + (a + 1) * XA_DIM] for a in range(XA_HEADS)]
        d_s = [ds_scr[h] for h in range(HG_HEADS)]
        dzq, dzf, dzi, dzg, dzx, dl0, dl1, dl2, dgn, dmk, dmv, dS = vjp((d_mix, d_att, d_s))
        W = HG_HEADS * HG_DIM
        for c in range(HG_SUB):
            rows = slice(c * HG_CHUNK, (c + 1) * HG_CHUNK)
            for h in range(HG_HEADS):
                for k, part in enumerate((dzq, dzf, dzi, dzg)):
                    dz_ref[rows, k * W + h * HG_DIM:k * W + (h + 1) * HG_DIM] = part[c][h].astype(dz_ref.dtype)
        for h in range(HG_HEADS):
            sl = slice(h * HG_DIM, (h + 1) * HG_DIM)
            ds_scr[h] = dS[h]
            dlb_ref[0:1, sl] += dl0[h]
            dlb_ref[1:2, sl] += dl1[h]
            dlb_ref[2:3, sl] += dl2[h]
        dgn_ref[...] += dgn
        KW = XA_HEADS * XA_DIM
        for a in range(XA_HEADS):
            dz_ref[:, 4 * W + a * XA_DIM:4 * W + (a + 1) * XA_DIM] = dzx[a].astype(dz_ref.dtype)
            dkv_ref[:, a * XA_DIM:(a + 1) * XA_DIM] += dmk[a]
            dkv_ref[:, KW + a * XA_DIM:KW + (a + 1) * XA_DIM] += dmv[a]

    rev = lambda b, n: (b * nb + (nb - 1 - n), 0)
    return pl.pallas_call(
        kern, grid=(bl, nb),
        in_specs=[pl.BlockSpec((R, zw), rev),
                  pl.BlockSpec((R, cat_w), rev),
                  pl.BlockSpec((None, HG_HEADS, HG_DIM, HG_DIM), lambda b, n: (b * nb + (nb - 1 - n), 0, 0, 0)),
                  pl.BlockSpec(lb_logits.shape, lambda b, n: (0, 0)),
                  pl.BlockSpec(gnorm.shape, lambda b, n: (0, 0)),
                  pl.BlockSpec((mem_len, kv.shape[1]), lambda b, n: (b, 0))],
        out_specs=[pl.BlockSpec((R, zw), rev),
                   pl.BlockSpec((mem_len, kv.shape[1]), lambda b, n: (b, 0)),
                   pl.BlockSpec(lb_logits.shape, lambda b, n: (0, 0)),
                   pl.BlockSpec(gnorm.shape, lambda b, n: (0, 0))],
        out_shape=[jax.ShapeDtypeStruct((T, zw), BF), jax.ShapeDtypeStruct(kv.shape, F32),
                   jax.ShapeDtypeStruct(lb_logits.shape, F32), jax.ShapeDtypeStruct(gnorm.shape, F32)],
        scratch_shapes=[pltpu.VMEM((HG_HEADS, HG_DIM, HG_DIM), F32)],
        name="hgrn_bwd",
        compiler_params=pltpu.CompilerParams(dimension_semantics=("arbitrary", "arbitrary"), vmem_limit_bytes=VMEM_CAP_BYTES),
    )(z, dcat, stash, lb_logits, gnorm, kv)


GM_SUB = 2


def _gmlp_pieces(z_ref):
    W = GM_GROUPS * GM_GROUP_DIM
    zu = [z_ref[:, g * GM_GROUP_DIM:(g + 1) * GM_GROUP_DIM] for g in range(GM_GROUPS)]
    zv = [z_ref[:, W + g * GM_GROUP_DIM:W + (g + 1) * GM_GROUP_DIM] for g in range(GM_GROUPS)]
    zx = [z_ref[:, 2 * W + a * XA_DIM:2 * W + (a + 1) * XA_DIM] for a in range(XA_HEADS)]
    return zu, zv, zx


def _gmlp_params(lng_ref, lnb_ref, ws_ref, bs_ref):
    lng = [lng_ref[:, g * GM_GROUP_DIM:(g + 1) * GM_GROUP_DIM] for g in range(GM_GROUPS)]
    lnb = [lnb_ref[:, g * GM_GROUP_DIM:(g + 1) * GM_GROUP_DIM] for g in range(GM_GROUPS)]
    ws = [ws_ref[g] for g in range(GM_GROUPS)]
    bs = [bs_ref[g:g + 1, :] for g in range(GM_GROUPS)]
    return lng, lnb, ws, bs


def _gmlp_fwd(z, ln_g, ln_b, w_s, b_s, kv, bl, nc):
    T, zw = z.shape
    mem_len = kv.shape[0] // bl
    cat_w = GM_GROUPS * GM_GROUP_DIM + XA_HEADS * XA_DIM

    assert nc % GM_SUB == 0
    nc = nc // GM_SUB
    R = GM_SUB * GM_CHUNK

    def kern(z_ref, lng_ref, lnb_ref, ws_ref, bs_ref, kv_ref, cat_ref):
        lng, lnb, ws, bs = _gmlp_params(lng_ref, lnb_ref, ws_ref, bs_ref)
        mk, mv = _kv_pieces(kv_ref)
        for c in range(GM_SUB):
            rows = pl.ds(c * GM_CHUNK, GM_CHUNK)
            zu, zv, zx = _gmlp_pieces(z_ref.at[rows])
            out = cat_ref.at[rows]
            outs = _gmlp_block(zu, zv, zx, lng, lnb, ws, bs, mk, mv)
            for g in range(GM_GROUPS):
                out[:, g * GM_GROUP_DIM:(g + 1) * GM_GROUP_DIM] = outs[g].astype(cat_ref.dtype)
            base = GM_GROUPS * GM_GROUP_DIM
            for a in range(XA_HEADS):
                out[:, base + a * XA_DIM:base + (a + 1) * XA_DIM] = outs[GM_GROUPS + a].astype(cat_ref.dtype)

    full2 = lambda b, n: (0, 0)
    return pl.pallas_call(
        kern, grid=(bl, nc),
        in_specs=[pl.BlockSpec((R, zw), lambda b, n: (b * nc + n, 0)),
                  pl.BlockSpec(ln_g.shape, full2), pl.BlockSpec(ln_b.shape, full2),
                  pl.BlockSpec(w_s.shape, lambda b, n: (0, 0, 0)), pl.BlockSpec(b_s.shape, full2),
                  pl.BlockSpec((mem_len, kv.shape[1]), lambda b, n: (b, 0))],
        out_specs=pl.BlockSpec((R, cat_w), lambda b, n: (b * nc + n, 0)),
        out_shape=jax.ShapeDtypeStruct((T, cat_w), BF),
        name="gmlp_fwd",
        compiler_params=pltpu.CompilerParams(dimension_semantics=("arbitrary", "arbitrary"), vmem_limit_bytes=VMEM_CAP_BYTES),
    )(z, ln_g, ln_b, w_s, b_s, kv)


def _gmlp_bwd(z, dcat, ln_g, ln_b, w_s, b_s, kv, bl, nc):
    T, zw = z.shape
    mem_len = kv.shape[0] // bl
    cat_w = dcat.shape[1]
    assert nc % GM_SUB == 0
    nc = nc // GM_SUB

    def kern(z_ref, dc_ref, lng_ref, lnb_ref, ws_ref, bs_ref, kv_ref,
             dz_ref, dkv_ref, dlng_ref, dlnb_ref, dws_ref, dbs_ref):
        first = jnp.logical_and(pl.program_id(0) == 0, pl.program_id(1) == 0)

        @pl.when(pl.program_id(1) == 0)
        def _():
            dkv_ref[...] = jnp.zeros(dkv_ref.shape, F32)

        @pl.when(first)
        def _():
            dlng_ref[...] = jnp.zeros(dlng_ref.shape, F32)
            dlnb_ref[...] = jnp.zeros(dlnb_ref.shape, F32)
            dws_ref[...] = jnp.zeros(dws_ref.shape, F32)
            dbs_ref[...] = jnp.zeros(dbs_ref.shape, F32)

        lng, lnb, ws, bs = _gmlp_params(lng_ref, lnb_ref, ws_ref, bs_ref)
        mk, mv = _kv_pieces(kv_ref)
        W = GM_GROUPS * GM_GROUP_DIM
        KW = XA_HEADS * XA_DIM
        for c in range(GM_SUB):
            rows = pl.ds(c * GM_CHUNK, GM_CHUNK)
            zu, zv, zx = _gmlp_pieces(z_ref.at[rows])
            dc, dz = dc_ref.at[rows], dz_ref.at[rows]
            _, vjp = jax.vjp(_gmlp_block, zu, zv, zx, lng, lnb, ws, bs, mk, mv)
            d_outs = [dc[:, g * GM_GROUP_DIM:(g + 1) * GM_GROUP_DIM] for g in range(GM_GROUPS)]
            d_outs += [dc[:, W + a * XA_DIM:W + (a + 1) * XA_DIM] for a in range(XA_HEADS)]
            dzu, dzv, dzx, dlng, dlnb, dws, dbs, dmk, dmv = vjp(d_outs)
            for g in range(GM_GROUPS):
                sl = slice(g * GM_GROUP_DIM, (g + 1) * GM_GROUP_DIM)
                dz[:, sl] = dzu[g].astype(dz_ref.dtype)
                dz[:, W + g * GM_GROUP_DIM:W + (g + 1) * GM_GROUP_DIM] = dzv[g].astype(dz_ref.dtype)
                dlng_ref[:, sl] += dlng[g]
                dlnb_ref[:, sl] += dlnb[g]
                dws_ref[g] += dws[g]
                dbs_ref[g:g + 1, :] += dbs[g]
            for a in range(XA_HEADS):
                dz[:, 2 * W + a * XA_DIM:2 * W + (a + 1) * XA_DIM] = dzx[a].astype(dz_ref.dtype)
                dkv_ref[:, a * XA_DIM:(a + 1) * XA_DIM] += dmk[a]
                dkv_ref[:, KW + a * XA_DIM:KW + (a + 1) * XA_DIM] += dmv[a]

    full2 = lambda b, n: (0, 0)
    full3 = lambda b, n: (0, 0, 0)
    blk = lambda b, n: (b * nc + n, 0)
    return pl.pallas_call(
        kern, grid=(bl, nc),
        in_specs=[pl.BlockSpec((GM_SUB * GM_CHUNK, zw), blk), pl.BlockSpec((GM_SUB * GM_CHUNK, cat_w), blk),
                  pl.BlockSpec(ln_g.shape, full2), pl.BlockSpec(ln_b.shape, full2),
                  pl.BlockSpec(w_s.shape, full3), pl.BlockSpec(b_s.shape, full2),
                  pl.BlockSpec((mem_len, kv.shape[1]), lambda b, n: (b, 0))],
        out_specs=[pl.BlockSpec((GM_SUB * GM_CHUNK, zw), blk),
                   pl.BlockSpec((mem_len, kv.shape[1]), lambda b, n: (b, 0)),
                   pl.BlockSpec(ln_g.shape, full2), pl.BlockSpec(ln_b.shape, full2),
                   pl.BlockSpec(w_s.shape, full3), pl.BlockSpec(b_s.shape, full2)],
        out_shape=[jax.ShapeDtypeStruct((T, zw), BF), jax.ShapeDtypeStruct(kv.shape, F32),
                   jax.ShapeDtypeStruct(ln_g.shape, F32), jax.ShapeDtypeStruct(ln_b.shape, F32),
                   jax.ShapeDtypeStruct(w_s.shape, F32), jax.ShapeDtypeStruct(b_s.shape, F32)],
        name="gmlp_bwd",
        compiler_params=pltpu.CompilerParams(dimension_semantics=("arbitrary", "arbitrary"), vmem_limit_bytes=VMEM_CAP_BYTES),
    )(z, dcat, ln_g, ln_b, w_s, b_s, kv)


def _place():
    x, y, c = lax.axis_index("x"), lax.axis_index("y"), lax.axis_index("c")
    chips = [(1 - x, y), (x, 1 - y), (1 - x, 1 - y)]
    return x, y, c, chips


def _half(ref, kind, e):
    if kind == "col":
        n = ref.shape[1] // 2
        return ref.at[:, pl.ds(pl.multiple_of(e * n, n), n), :]
    n = ref.shape[2] // 2
    return ref.at[:, :, pl.ds(pl.multiple_of(e * n, n), n)]


def _slot(ref, kind, j, n):
    if kind == "col":
        return ref.at[:, :, pl.ds(pl.multiple_of(j * n, n), n)]
    return ref.at[:, pl.ds(pl.multiple_of(j * n, n), n), :]


BF16_TILE_ROWS = 16
AG_DIRECT_SIXTEENTHS = 3


def _allgather_seq(name, items, cid):
    nt = len(items)
    kinds = [k for (_, k, _) in items]
    slot_kind = ["row" if k == "row" else "col" for k in kinds]
    out_type = []
    for s, k, l in items:
        L, r, c = s.shape
        lo = L if l is None else 1
        out_type.append(jax.ShapeDtypeStruct((lo, 4 * r, c) if k == "row" else (lo, r, 4 * c), s.dtype))

    def part(ref, t, e):
        return ref if kinds[t] == "vec" else _half(ref, kinds[t], e)

    def split(half):
        rows = half.shape[1]
        direct = rows * AG_DIRECT_SIXTEENTHS // 16 // BF16_TILE_ROWS * BF16_TILE_ROWS
        return half.at[:, pl.ds(0, rows - direct), :], half.at[:, pl.ds(rows - direct, direct), :]

    def body(*refs):
        sh = [refs[t] if items[t][2] is None else refs[t].at[pl.ds(items[t][2], 1)] for t in range(nt)]
        full = refs[nt:2 * nt]
        s_ici, r_ici, s_far, r_far, s_d2d, r_d2d = refs[2 * nt:]
        x, y, c, chips = _place()
        own = 2 * x + y
        sibling = (x, y, 1 - c)
        barrier = pltpu.get_barrier_semaphore()
        for peer in [(px, py, pc) for (px, py) in chips for pc in (0, 1)] + [sibling]:
            pl.semaphore_signal(barrier, inc=1, device_id=peer, device_id_type=MESH)
        pl.semaphore_wait(barrier, 7)
        width = [sh[t].shape[1] if kinds[t] == "row" else sh[t].shape[2] for t in range(nt)]
        sent = []
        for t in range(nt):
            for p, (px, py) in enumerate(chips):
                src, dst = part(sh[t], t, c), part(_slot(full[t], slot_kind[t], own, width[t]), t, c)
                cp = pltpu.make_async_remote_copy(
                    src_ref=src, dst_ref=dst, send_sem=s_ici.at[t, p], recv_sem=r_ici.at[t, p], device_id=(px, py, c),
                    device_id_type=MESH)
                cp.start()
                sent.append(cp)
                if kinds[t] == "vec":
                    continue
                far = pltpu.make_async_remote_copy(
                    src_ref=split(src)[1], dst_ref=split(dst)[1], send_sem=s_far.at[t, p], recv_sem=r_far.at[t, p],
                    device_id=(px, py, 1 - c), device_id_type=MESH)
                far.start()
                sent.append(far)
        for t in range(nt):
            for p, (px, py) in enumerate(chips):
                landed = part(_slot(full[t], slot_kind[t], 2 * px + py, width[t]), t, c)
                pltpu.make_async_remote_copy(
                    src_ref=landed, dst_ref=landed, send_sem=s_ici.at[t, p], recv_sem=r_ici.at[t, p],
                    device_id=(px, py, c), device_id_type=MESH).wait_recv()
                if kinds[t] == "vec":
                    continue
                fw = pltpu.make_async_remote_copy(
                    src_ref=split(landed)[0], dst_ref=split(landed)[0], send_sem=s_d2d.at[t, p], recv_sem=r_d2d.at[t, p],
                    device_id=sibling, device_id_type=MESH)
                fw.start()
                sent.append(fw)
        for t in range(nt):
            if kinds[t] == "vec":
                continue
            for p, (px, py) in enumerate(chips):
                forwarded, direct = split(_half(_slot(full[t], kinds[t], 2 * px + py, width[t]), kinds[t], 1 - c))
                pltpu.make_async_remote_copy(
                    src_ref=forwarded, dst_ref=forwarded, send_sem=s_d2d.at[t, p], recv_sem=r_d2d.at[t, p],
                    device_id=sibling, device_id_type=MESH).wait_recv()
                pltpu.make_async_remote_copy(
                    src_ref=direct, dst_ref=direct, send_sem=s_far.at[t, p], recv_sem=r_far.at[t, p],
                    device_id=(px, py, 1 - c), device_id_type=MESH).wait_recv()
        for cp in sent:
            cp.wait_send()

    sems = pltpu.SemaphoreType.DMA
    return pl.kernel(
        body, out_type=out_type, mesh=plsc.ScalarSubcoreMesh(axis_name="seq", num_cores=1),
        scratch_types=[sems((nt, 3)), sems((nt, 3)), sems((nt, 3)), sems((nt, 3)), sems((nt, 3)), sems((nt, 3))],
        compiler_params=pltpu.CompilerParams(collective_id=cid), name=name,
    )(*[s for (s, _, _) in items])


def _place_own(name, full, shard, kind, layer, chip_arr, after):
    lo, r, c = (shard.shape[0] if layer is None else 1,) + shard.shape[1:]
    first = 0 if layer is None else layer
    tr = _pick(r, 512)
    nr = r // tr

    def body(chip_ref, s_ref, f_ref, after_ref, o_ref):
        o_ref[...] = s_ref[...]

    if kind == "row":
        out_map = lambda i, j, chip: (i, chip[0] * nr + j, 0)
    else:
        out_map = lambda i, j, chip: (i, j, chip[0])
    return pl.pallas_call(
        body, out_shape=jax.ShapeDtypeStruct(full.shape, full.dtype),
        grid_spec=pltpu.PrefetchScalarGridSpec(
            num_scalar_prefetch=1, grid=(lo, nr),
            in_specs=[pl.BlockSpec((1, tr, c), lambda i, j, chip: (i + first, j, 0)), pl.BlockSpec(memory_space=pl.ANY),
                      pl.BlockSpec(memory_space=pl.ANY)],
            out_specs=pl.BlockSpec((1, tr, c), out_map)),
        input_output_aliases={2: 0},
        compiler_params=pltpu.CompilerParams(dimension_semantics=("parallel", "parallel"), vmem_limit_bytes=VMEM_CAP_BYTES),
        name=name,
    )(chip_arr, shard, full, after)


def _slot2(ref, kind, j, n):
    if kind == "col":
        return ref.at[:, pl.ds(pl.multiple_of(j * n, n), n)]
    return ref.at[pl.ds(pl.multiple_of(j * n, n), n), :]


def _rs_chips_seq(name, parts, kinds, cid):
    nm = len(parts)
    out_type = []
    for g, k in zip(parts, kinds):
        r, c = g.shape
        ps = (r, c // 4) if k == "col" else (r // 4, c)
        out_type += [jax.ShapeDtypeStruct(ps, BF), jax.ShapeDtypeStruct((3,) + ps, BF)]

    def body(*refs):
        g = refs[:nm]
        outs = refs[nm:3 * nm]
        loc, ssem, rsem = refs[3 * nm:]
        x, y, c, chips = _place()
        own = 2 * x + y
        barrier = pltpu.get_barrier_semaphore()
        for (px, py) in chips:
            pl.semaphore_signal(barrier, inc=1, device_id=(px, py, c), device_id_type=MESH)
        pl.semaphore_wait(barrier, 3)
        cps = []
        for m in range(nm):
            k = kinds[m]
            own_o, got_o = outs[2 * m], outs[2 * m + 1]
            n = g[m].shape[1] // 4 if k == "col" else g[m].shape[0] // 4
            lc = pltpu.make_async_copy(_slot2(g[m], k, own, n), own_o, loc.at[m])
            lc.start()
            cps.append(lc)
            for p, (px, py) in enumerate(chips):
                cp = pltpu.make_async_remote_copy(
                    src_ref=_slot2(g[m], k, 2 * px + py, n), dst_ref=got_o.at[p],
                    send_sem=ssem.at[m, p], recv_sem=rsem.at[m, p], device_id=(px, py, c), device_id_type=MESH)
                cp.start()
                cps.append(cp)
        for cp in cps:
            cp.wait()

    return pl.kernel(
        body, out_type=out_type, mesh=plsc.ScalarSubcoreMesh(axis_name="seq", num_cores=1),
        scratch_types=[pltpu.SemaphoreType.DMA((nm,)), pltpu.SemaphoreType.DMA((nm, 3)), pltpu.SemaphoreType.DMA((nm, 3))],
        compiler_params=pltpu.CompilerParams(collective_id=cid), name=name,
    )(*parts)


def _finish_share(name, owns, gots, kind, c_arr):
    L = len(owns)
    r, c = owns[0].shape
    tr = _pick(r, 128 if kind == "col" else 256)
    nb = r // tr
    nq = L * nb

    def chunk_of(l):
        return lambda h, q: jnp.clip(q * (1 - h) + (nq - 1) * h - l * nb, 0, nb - 1)

    ins, in_specs = [], []
    for l in range(L):
        at = chunk_of(l)
        ins += [owns[l], gots[l].reshape(3 * r, c), gots[l].reshape(3 * r, c), gots[l].reshape(3 * r, c)]
        in_specs.append(pl.BlockSpec((tr, c), functools.partial(lambda h, q, cc, at: (at(h, q), 0), at=at)))
        in_specs += [pl.BlockSpec((tr, c), functools.partial(lambda h, q, cc, at, p: (p * nb + at(h, q), 0), at=at, p=p))
                     for p in range(3)]
    if kind == "col":
        out_sd = (L, 2, r, c)
        o_spec = pl.BlockSpec((None, 2, tr, c), lambda h, q, cc: ((q * h) // nb, 0, (q * h) % nb, 0))
    else:
        out_sd = (L * r, 2 * c)
        o_spec = pl.BlockSpec((tr, 2 * c), lambda h, q, cc: (q * h, 0))

    def kern(c_ref, *refs):
        in_refs = refs[:4 * L]
        out_ref, mine, recv, ssem, rsem = refs[4 * L:]
        h, q = pl.program_id(0), pl.program_id(1)
        x, y, cc, _ = _place()

        def swap(qq):
            return pltpu.make_async_remote_copy(src_ref=mine.at[qq], dst_ref=recv.at[qq], send_sem=ssem.at[qq],
                                                recv_sem=rsem.at[qq], device_id=(x, y, 1 - cc), device_id_type=MESH)

        for l in range(L):
            @pl.when(jnp.logical_and(h == 0, q // nb == l))
            def _(l=l):
                o_ref, g0, g1, g2 = in_refs[4 * l:4 * l + 4]
                mine[q] = ((o_ref[...].astype(F32) + g0[...].astype(F32)) + g1[...].astype(F32)) + g2[...].astype(F32)
                swap(q).start()

        @pl.when(h == 1)
        def _():
            swap(q).wait()
            a, b = mine[q], recv[q]
            first = c_ref[0] == 0
            lo, hi = jnp.where(first, a, b), jnp.where(first, b, a)
            if kind == "col":
                out_ref[0] = lo
                out_ref[1] = hi
            else:
                out_ref[:, :c] = lo
                out_ref[:, c:] = hi

    full = pl.pallas_call(
        kern,
        grid_spec=pltpu.PrefetchScalarGridSpec(
            num_scalar_prefetch=1, grid=(2, nq), in_specs=in_specs, out_specs=o_spec,
            scratch_shapes=[pltpu.VMEM((nq, tr, c), F32), pltpu.VMEM((nq, tr, c), F32),
                            pltpu.SemaphoreType.DMA((nq,)), pltpu.SemaphoreType.DMA((nq,))]),
        out_shape=jax.ShapeDtypeStruct(out_sd, F32), name=name,
        compiler_params=pltpu.CompilerParams(dimension_semantics=("arbitrary", "arbitrary"),
                                             vmem_limit_bytes=VMEM_CAP_BYTES),
    )(c_arr, *ins)
    return full.reshape(L, 2 * r, c) if kind == "col" else full.reshape(L, r, 2 * c)


def _small_allreduce(buf, name):
    R = buf.shape[0]
    assert R % 16 == 0
    h = R // 2

    def body(x_ref, o_ref, sib, csum, got, s_a, r_a, s_b, r_b, s_c, r_c):
        x, y, c, chips = _place()
        sibling = (x, y, 1 - c)
        own = 2 * x + y
        swap = pltpu.make_async_remote_copy(src_ref=x_ref, dst_ref=sib, send_sem=s_a, recv_sem=r_a,
                                            device_id=sibling, device_id_type=MESH)
        swap.start()
        swap.wait()
        a, b = x_ref[...], sib[...]
        south = c == 0
        csum[...] = jnp.where(south, a, b) + jnp.where(south, b, a)
        lo = pl.multiple_of(c * h, 8)
        mine = csum.at[pl.ds(lo, h)]
        got[own] = csum[pl.ds(lo, h)]
        sends = []
        for p, (px, py) in enumerate(chips):
            cp = pltpu.make_async_remote_copy(src_ref=mine, dst_ref=got.at[own], send_sem=s_b.at[p], recv_sem=r_b.at[p],
                                              device_id=(px, py, c), device_id_type=MESH)
            cp.start()
            sends.append(cp)
        for cp in sends:
            cp.wait()
        o_ref[pl.ds(lo, h)] = ((got[0] + got[1]) + got[2]) + got[3]
        done = o_ref.at[pl.ds(lo, h)]
        back = pltpu.make_async_remote_copy(src_ref=done, dst_ref=done, send_sem=s_c, recv_sem=r_c,
                                            device_id=sibling, device_id_type=MESH)
        back.start()
        back.wait_send()
        other = o_ref.at[pl.ds(pl.multiple_of((1 - c) * h, 8), h)]
        pltpu.make_async_remote_copy(src_ref=other, dst_ref=other, send_sem=s_c, recv_sem=r_c,
                                     device_id=sibling, device_id_type=MESH).wait_recv()

    vm = pl.BlockSpec(memory_space=pltpu.VMEM)
    return pl.pallas_call(
        body, out_shape=jax.ShapeDtypeStruct(buf.shape, F32), in_specs=[vm], out_specs=vm,
        scratch_shapes=[pltpu.VMEM((R, LANES), F32), pltpu.VMEM((R, LANES), F32), pltpu.VMEM((4, h, LANES), F32),
                        pltpu.SemaphoreType.DMA, pltpu.SemaphoreType.DMA, pltpu.SemaphoreType.DMA((3,)),
                        pltpu.SemaphoreType.DMA((3,)), pltpu.SemaphoreType.DMA, pltpu.SemaphoreType.DMA],
        name=name,
        compiler_params=pltpu.CompilerParams(vmem_limit_bytes=VMEM_CAP_BYTES),
    )(buf)


PACK_TILE_ROWS = 8


def _item_rows(shape):
    n = 1
    for d in shape:
        n *= d
    return -(-n // (PACK_TILE_ROWS * LANES)) * PACK_TILE_ROWS


def _pack(arrs, rows_total):
    buf = jnp.zeros((rows_total, LANES), F32)
    r = 0
    for a in arrs:
        f = a.reshape(-1).astype(F32)
        nr = _item_rows(a.shape)
        block = jnp.pad(f, (0, nr * LANES - f.shape[0])).reshape(nr, LANES)
        buf = lax.dynamic_update_slice(buf, block, (r, 0))
        r += nr
    return buf


def _unpack(buf, shapes):
    out, r = [], 0
    for s in shapes:
        n = 1
        for d in s:
            n *= d
        nr = _item_rows(s)
        out.append(buf[r:r + nr].reshape(-1)[:n].reshape(s))
        r += nr
    return out


def _rows_needed(shapes):
    return -(-sum(_item_rows(s) for s in shapes) // (2 * PACK_TILE_ROWS)) * (2 * PACK_TILE_ROWS)


def _two_rows(a, b):
    out = jnp.zeros((2, a.shape[1]), a.dtype)
    return lax.dynamic_update_slice(lax.dynamic_update_slice(out, a, (0, 0)), b, (1, 0))


def _adam(w, g, m, v):
    m = ADAM_B1 * m + (1.0 - ADAM_B1) * g
    v = ADAM_B2 * v + (1.0 - ADAM_B2) * jnp.square(g)
    m_hat = m / (1.0 - ADAM_B1 ** ADAM_STEP)
    v_hat = v / (1.0 - ADAM_B2 ** ADAM_STEP)
    delta = -ADAM_LR * (m_hat / (jnp.sqrt(v_hat) + ADAM_EPS) + ADAM_WD * w)
    return delta, m, v


def _adam_call(name, w2, g2, m2, v2, tr, pass_grad=False):
    def fn(rv, cv):
        outs = list(_adam(*rv))
        return ([rv[1]] + outs if pass_grad else outs), []

    width = w2.shape[1]
    return _rowcall(name, fn, [(w2, 0, width), (g2, 0, width), (m2, 0, width), (v2, 0, width)], [],
                    [(width, F32)] * (4 if pass_grad else 3), [], tr)


def kernel(x, mem, mem_norm, lb_logits, ffn1_norm, ffn1_w_in, ffn1_w_out, mix_norm, mem_w_kv, hgrn_w_in, hgrn_gnorm, hgrn_w_out, gmlp_w_in, gmlp_ln_g, gmlp_ln_b, gmlp_w_s, gmlp_b_s, gmlp_w_out, ffn2_norm, ffn2_w_in, ffn2_w_out, final_norm, loss_target, m_mem_norm, m_lb_logits, m_ffn1_norm, m_ffn1_w_in, m_ffn1_w_out, m_mix_norm, m_mem_w_kv, m_hgrn_w_in, m_hgrn_gnorm, m_hgrn_w_out, m_gmlp_w_in, m_gmlp_ln_g, m_gmlp_ln_b, m_gmlp_w_s, m_gmlp_b_s, m_gmlp_w_out, m_ffn2_norm, m_ffn2_w_in, m_ffn2_w_out, m_final_norm, v_mem_norm, v_lb_logits, v_ffn1_norm, v_ffn1_w_in, v_ffn1_w_out, v_mix_norm, v_mem_w_kv, v_hgrn_w_in, v_hgrn_gnorm, v_hgrn_w_out, v_gmlp_w_in, v_gmlp_ln_g, v_gmlp_ln_b, v_gmlp_w_s, v_gmlp_b_s, v_gmlp_w_out, v_ffn2_norm, v_ffn2_w_in, v_ffn2_w_out, v_final_norm):
    bl, seq, D = x.shape
    T = bl * seq
    mem_len = mem.shape[1]
    chip = 2 * lax.axis_index("x") + lax.axis_index("y")
    c_arr = lax.axis_index("c").astype(jnp.int32).reshape(1)
    chip_arr = chip.astype(jnp.int32).reshape(1)
    TR = 1024

    big = [("ffn1_w_in", ffn1_w_in, "col"), ("ffn1_w_out", ffn1_w_out, "row"), ("mem_w_kv", mem_w_kv, "col"),
           ("hgrn_w_in", hgrn_w_in, "col"), ("hgrn_w_out", hgrn_w_out, "row"), ("gmlp_w_in", gmlp_w_in, "col"),
           ("gmlp_w_out", gmlp_w_out, "row"), ("ffn2_w_in", ffn2_w_in, "col"), ("ffn2_w_out", ffn2_w_out, "row")]
    kinds = [k for (_, _, k) in big]
    shards_bf = []
    for nm, w, _ in big:
        L, r, c = w.shape
        (wb,) = _rowcall("cast_" + nm, lambda rv, cv: ([rv[0]], []), [(w.reshape(L * r, c), 0, c)], [], [(c, BF)], [], 512)
        shards_bf.append(wb.reshape(L, r, c))
    sb = dict(zip([nm for (nm, _, _) in big], shards_bf))
    groups = [[("ffn1_w_in", 0)], [("ffn1_w_out", 0)], [("hgrn_w_in", None)], [("mem_w_kv", None)], [("hgrn_w_out", None)],
              [("ffn2_w_in", 0), ("ffn2_w_out", 0), ("gmlp_ln_g", None), ("gmlp_ln_b", None)],
              [("ffn1_w_in", 1), ("ffn1_w_out", 1)],
              [("gmlp_w_in", None), ("gmlp_w_out", None)],
              [("ffn2_w_in", 1), ("ffn2_w_out", 1)]]
    kind_of = {nm: k for (nm, _, k) in big}
    for nm, vec in (("gmlp_ln_g", gmlp_ln_g), ("gmlp_ln_b", gmlp_ln_b)):
        sb[nm] = vec.reshape(1, 1, -1)
        kind_of[nm] = "vec"
    gathered = {nm: [None, None] for nm in ("ffn1_w_in", "ffn1_w_out", "ffn2_w_in", "ffn2_w_out")}
    others = {}
    for gi, grp in enumerate(groups):
        outs = _allgather_seq("gather_%d" % gi, [(sb[nm], kind_of[nm], l) for (nm, l) in grp], gi)
        for (nm, l), o in zip(grp, outs):
            others[(nm, l)] = o

    def whole(nm, l, after):
        full = _place_own("own_%s_%d" % (nm, l or 0), others[(nm, l)], sb[nm], "row" if kind_of[nm] == "row" else "col", l,
                          chip_arr, after)
        if l is None:
            gathered[nm] = full
        else:
            gathered[nm][l] = full
        return full

    def rms_fwd(name, xin, g):
        (h,) = _rowcall(name, lambda rv, cv: ([_rmsnorm(rv[0], cv[0])], []), [(xin, 0, D)], [g.reshape(1, D)], [(D, BF)], [], TR)
        return h

    def ffn_fwd(tag, xin, h, nm_in, nm_out, layer, next_gain):
        w_in = whole(nm_in, layer, h)
        dff = w_in.shape[2] // 2
        zg, zu, a = _ffn_in_swiglu("ffn_in_" + tag, h, w_in, 1024, dff // 2)
        out = _mm("ffn_out_" + tag, a, whole(nm_out, layer, a), "nn", F32, 1024, 1024, dff, scale=0.5, res=xin, b_lead=0,
                  norm_gain=None if next_gain is None else next_gain.reshape(1, D))
        xo, h_next = (out, None) if next_gain is None else out
        return xo, h_next, (xin, h, zg, zu, a)

    def ffn_bwd(tag, dxo, saved, g, w_in, w_out, layer):
        xin, h, zg, zu, a = saved
        dff = w_out[layer].shape[1]
        dw_out = _mm_tn_pair("ffn_dwo_" + tag, a, dxo, "row", c_arr, dff // 2, T, scale=0.5)
        dz = _ffn_da_swiglu("ffn_da_" + tag, dxo, w_out[layer], zg, zu, 512)
        dw_in = _mm_tn_pair("ffn_dwi_" + tag, h, dz, "col", c_arr, 512, T)
        dx, dg = _mm_dh_rms("ffn_dh_" + tag, dz, w_in[layer], xin, g.reshape(1, D), dxo, 512)
        return dx, dg, dw_in, dw_out

    def rms_bwd(name, xin, g, dh, dres):
        def fn(rv, cv):
            _, vjp = jax.vjp(_rmsnorm, rv[0], cv[0])
            dx, dg = vjp(rv[1])
            if dres is not None:
                dx = dx + rv[2]
            return [dx], [dg]

        rows = [(xin, 0, D), (dh, 0, D)] + ([(dres, 0, D)] if dres is not None else [])
        dx, dg = _rowcall(name, fn, rows, [g.reshape(1, D)], [(D, F32)], [((1, D), F32)], TR)
        return dx, dg

    x0 = x.reshape(T, D)
    tgt = loss_target.reshape(T, D)
    mem2 = mem.reshape(bl * mem_len, D)
    memn = rms_fwd("rms_mem", mem2, mem_norm)

    h_f10 = rms_fwd("rms_f1l0", x0, ffn1_norm[0])
    x1, h_m0, sv_f10 = ffn_fwd("f1l0", x0, h_f10, "ffn1_w_in", "ffn1_w_out", 0, mix_norm[0])
    z_m0 = _mm("mix_in_0", h_m0, whole("hgrn_w_in", None, h_m0), "nn", F32, 2048, 512, D, b_lead=0)
    w_kv = whole("mem_w_kv", None, z_m0)
    kv = [_mm("kv_%d" % i, memn, w_kv, "nn", F32, 512, 512, D, b_lead=i) for i in range(2)]
    cat0, stash0 = _hgrn_fwd2(z_m0, lb_logits, hgrn_gnorm, kv[0], bl, seq)
    x2, h_f20 = _mm("mix_out_0", cat0, whole("hgrn_w_out", None, cat0), "nn", F32, 1024, 1024, cat0.shape[1], res=x1, b_lead=0,
                    norm_gain=ffn2_norm[0].reshape(1, D))
    x3, h_f11, sv_f20 = ffn_fwd("f2l0", x2, h_f20, "ffn2_w_in", "ffn2_w_out", 0, ffn1_norm[1])
    x4, h_m1, sv_f11 = ffn_fwd("f1l1", x3, h_f11, "ffn1_w_in", "ffn1_w_out", 1, mix_norm[1])
    z_m1 = _mm("mix_in_1", h_m1, whole("gmlp_w_in", None, h_m1), "nn", F32, 2048, 512, D, b_lead=0)
    nc1 = seq // GM_CHUNK
    w_s, b_s = gmlp_w_s[0], gmlp_b_s[0]
    ln_w = GM_GROUPS * GM_GROUP_DIM
    ln_g_full, ln_b_full = [whole(nm, None, z_m1).reshape(1, ln_w) for nm in ("gmlp_ln_g", "gmlp_ln_b")]
    cat1 = _gmlp_fwd(z_m1, ln_g_full, ln_b_full, w_s, b_s, kv[1], bl, nc1)
    x5, h_f21 = _mm("mix_out_1", cat1, whole("gmlp_w_out", None, cat1), "nn", F32, 1024, 1024, cat1.shape[1], res=x4, b_lead=0,
                    norm_gain=ffn2_norm[1].reshape(1, D))
    x6, _, sv_f21 = ffn_fwd("f2l1", x5, h_f21, "ffn2_w_in", "ffn2_w_out", 1, None)

    def head(rv, cv):
        def f(xx, gg):
            err = _rmsnorm(xx, gg) - rv[1]
            return 0.5 * jnp.sum(jnp.mean(err * err, axis=-1, keepdims=True), axis=0, keepdims=True)

        ls, vjp = jax.vjp(f, rv[0], cv[0])
        dx, dg = vjp(jnp.ones((1, 1), F32))
        return [dx], [dg, jnp.broadcast_to(ls, (1, 128))]

    dx6, d_final, loss_part = _rowcall("loss_head", head, [(x6, 0, D), (tgt, 0, D)], [final_norm.reshape(1, D)],
                                       [(D, F32)], [((1, D), F32), ((1, 128), F32)], TR)

    rs_out = {}
    n_gather = len(groups)

    def rs(gi, items):
        outs = _rs_chips_seq("reduce_%d" % gi, [p for (_, p, _) in items], [k for (_, _, k) in items], n_gather + gi)
        for i, (key, _, _) in enumerate(items):
            rs_out[key] = (outs[2 * i], outs[2 * i + 1])

    dx5, dg_f21, dwi_f21, dwo_f21 = ffn_bwd("f2l1", dx6, sv_f21, ffn2_norm[1], gathered["ffn2_w_in"], gathered["ffn2_w_out"], 1)
    rs(0, [(("ffn2_w_out", 1), dwo_f21, "row"), (("ffn2_w_in", 1), dwi_f21, "col")])
    dcat1 = _mm("mix_dcat_1", dx5, gathered["gmlp_w_out"], "nt", F32, 2048, 1024, D, b_lead=0)
    dwo_m1 = _mm_tn_pair("mix_dwo_1", cat1, dx5, "row", c_arr, 1024, T)
    dz_m1, dkv1, d_lng, d_lnb, d_ws, d_bs = _gmlp_bwd(z_m1, dcat1, ln_g_full, ln_b_full, w_s, b_s, kv[1], bl, nc1)
    dx4, dg_m1 = _mm_dh_rms("mix_dh_1", dz_m1, gathered["gmlp_w_in"], x4, mix_norm[1].reshape(1, D), dx5, 512)
    dwi_m1 = _mm_tn_pair("mix_dwi_1", h_m1, dz_m1, "col", c_arr, 1024, T)
    rs(1, [(("gmlp_w_out", 0), dwo_m1, "row"), (("gmlp_w_in", 0), dwi_m1, "col")])
    dx3, dg_f11, dwi_f11, dwo_f11 = ffn_bwd("f1l1", dx4, sv_f11, ffn1_norm[1], gathered["ffn1_w_in"], gathered["ffn1_w_out"], 1)
    rs(2, [(("ffn1_w_out", 1), dwo_f11, "row"), (("ffn1_w_in", 1), dwi_f11, "col")])

    dx2, dg_f20, dwi_f20, dwo_f20 = ffn_bwd("f2l0", dx3, sv_f20, ffn2_norm[0], gathered["ffn2_w_in"], gathered["ffn2_w_out"], 0)
    rs(3, [(("ffn2_w_out", 0), dwo_f20, "row"), (("ffn2_w_in", 0), dwi_f20, "col")])
    dcat0 = _mm("mix_dcat_0", dx2, gathered["hgrn_w_out"], "nt", F32, 2048, 1024, D, b_lead=0)
    dwo_m0 = _mm_tn_pair("mix_dwo_0", cat0, dx2, "row", c_arr, 1024, T)
    dz_m0, dkv0, d_lb, d_gn = _hgrn_bwd2(z_m0, dcat0, stash0, lb_logits, hgrn_gnorm, kv[0], bl, seq)
    dx1, dg_m0 = _mm_dh_rms("mix_dh_0", dz_m0, gathered["hgrn_w_in"], x1, mix_norm[0].reshape(1, D), dx2, 512)
    dwi_m0 = _mm_tn_pair("mix_dwi_0", h_m0, dz_m0, "col", c_arr, 1024, T)
    rs(4, [(("hgrn_w_out", 0), dwo_m0, "row"), (("hgrn_w_in", 0), dwi_m0, "col")])

    dwkv = [_mm_tn_pair("kv_dw_%d" % i, memn, dkv, "col", c_arr, 1024, 512) for i, dkv in enumerate([dkv0, dkv1])]
    rs(5, [(("mem_w_kv", 0), dwkv[0], "col"), (("mem_w_kv", 1), dwkv[1], "col")])
    dmemn = _mm("kv_dx_0", dkv0, gathered["mem_w_kv"], "nt", F32, 512, 512, 1024, b_lead=0)
    dmemn = _mm("kv_dx_1", dkv1, gathered["mem_w_kv"], "nt", F32, 512, 512, 1024, res=dmemn, b_lead=1)
    _, d_memnorm = rms_bwd("rms_bwd_mem", mem2, mem_norm, dmemn, None)

    dx0, dg_f10, dwi_f10, dwo_f10 = ffn_bwd("f1l0", dx1, sv_f10, ffn1_norm[0], gathered["ffn1_w_in"], gathered["ffn1_w_out"], 0)
    rs(6, [(("ffn1_w_out", 0), dwo_f10, "row")])
    rs(7, [(("ffn1_w_in", 0), dwi_f10, "col")])

    shard_grads = [_finish_share("finish_" + nm, [rs_out[(nm, l)][0] for l in range(w.shape[0])],
                                 [rs_out[(nm, l)][1] for l in range(w.shape[0])], k, c_arr) for (nm, w, k) in big]

    big_w = [w for (_, w, _) in big]
    big_m = [m_ffn1_w_in, m_ffn1_w_out, m_mem_w_kv, m_hgrn_w_in, m_hgrn_w_out, m_gmlp_w_in, m_gmlp_w_out, m_ffn2_w_in, m_ffn2_w_out]
    big_v = [v_ffn1_w_in, v_ffn1_w_out, v_mem_w_kv, v_hgrn_w_in, v_hgrn_w_out, v_gmlp_w_in, v_gmlp_w_out, v_ffn2_w_in, v_ffn2_w_out]
    big_out = {}
    for (nm, w, _), g, m, v in zip(big, shard_grads, big_m, big_v):
        L, r, c = w.shape
        g2, d2, m2, v2 = _adam_call("adam_" + nm, w.reshape(L * r, c), g.reshape(L * r, c), m.reshape(L * r, c),
                                    v.reshape(L * r, c), 256, pass_grad=True)
        big_out[nm] = (g2.reshape(w.shape), d2.reshape(w.shape), m2.reshape(w.shape), v2.reshape(w.shape))

    d_ffn1n = _two_rows(dg_f10, dg_f11)
    d_mixn = _two_rows(dg_m0, dg_m1)
    d_ffn2n = _two_rows(dg_f20, dg_f21)
    small_parts = [loss_part[:, :1], d_memnorm, d_lb, d_ffn1n, d_mixn, d_gn, d_lng, d_lnb, d_ws, d_bs, d_ffn2n, d_final]
    red_shapes = [(1,), mem_norm.shape, lb_logits.shape, ffn1_norm.shape, mix_norm.shape, hgrn_gnorm.shape, (1, ln_w), (1, ln_w),
                  gmlp_w_s.shape, gmlp_b_s.shape, ffn2_norm.shape, final_norm.shape]
    red = _small_allreduce(_pack(small_parts, _rows_needed(red_shapes)), "reduce_small")
    (loss_v, g_memn, g_lb, g_f1n, g_mixn, g_gn, g_lng_full, g_lnb_full, g_ws, g_bs, g_f2n, g_fin) = _unpack(red, red_shapes)
    lsh = gmlp_ln_g.shape[1]
    g_lng = lax.dynamic_slice(g_lng_full, (0, chip * lsh), (1, lsh))
    g_lnb = lax.dynamic_slice(g_lnb_full, (0, chip * lsh), (1, lsh))
    small_w = [mem_norm, lb_logits, ffn1_norm, mix_norm, hgrn_gnorm, gmlp_ln_g, gmlp_ln_b, gmlp_w_s, gmlp_b_s, ffn2_norm, final_norm]
    small_g = [g_memn, g_lb, g_f1n, g_mixn, g_gn, g_lng, g_lnb, g_ws, g_bs, g_f2n, g_fin]
    small_m = [m_mem_norm, m_lb_logits, m_ffn1_norm, m_mix_norm, m_hgrn_gnorm, m_gmlp_ln_g, m_gmlp_ln_b, m_gmlp_w_s, m_gmlp_b_s, m_ffn2_norm, m_final_norm]
    small_v = [v_mem_norm, v_lb_logits, v_ffn1_norm, v_mix_norm, v_hgrn_gnorm, v_gmlp_ln_g, v_gmlp_ln_b, v_gmlp_w_s, v_gmlp_b_s, v_ffn2_norm, v_final_norm]
    sshapes = [w.shape for w in small_w]
    nrow = _rows_needed(sshapes)
    d_p, m_p, v_p = _adam_call("adam_small", _pack(small_w, nrow), _pack(small_g, nrow), _pack(small_m, nrow), _pack(small_v, nrow), nrow)
    s_delta, s_m, s_v = _unpack(d_p, sshapes), _unpack(m_p, sshapes), _unpack(v_p, sshapes)
    small_names = ["mem_norm", "lb_logits", "ffn1_norm", "mix_norm", "hgrn_gnorm", "gmlp_ln_g", "gmlp_ln_b", "gmlp_w_s", "gmlp_b_s", "ffn2_norm", "final_norm"]
    small_out = {nm: (g.reshape(w.shape), d, m, v) for nm, w, g, d, m, v in zip(small_names, small_w, small_g, s_delta, s_m, s_v)}

    order = ["mem_norm", "lb_logits", "ffn1_norm", "ffn1_w_in", "ffn1_w_out", "mix_norm", "mem_w_kv", "hgrn_w_in", "hgrn_gnorm",
             "hgrn_w_out", "gmlp_w_in", "gmlp_ln_g", "gmlp_ln_b", "gmlp_w_s", "gmlp_b_s", "gmlp_w_out", "ffn2_norm", "ffn2_w_in",
             "ffn2_w_out", "final_norm"]
    allo = {**big_out, **small_out}
    grad_x = dx0.reshape(x.shape)
    return (loss_v.reshape(()), grad_x, *[allo[n][0] for n in order], *[allo[n][1] for n in order],
            *[allo[n][2] for n in order], *[allo[n][3] for n in order])
```

```python
import functools

import jax
import jax.numpy as jnp
from jax import lax
from jax.experimental import pallas as pl
from jax.experimental.pallas import tpu as pltpu
from jax.experimental.pallas import tpu_sc as plsc

BF = jnp.bfloat16
F32 = jnp.float32
MESH = pl.DeviceIdType.MESH

EPS = 1e-6
D_MODEL = 1024
HG_HEADS = 8
HG_DIM = 128
HG_CHUNK = 64
GM_CHUNK = 128
GM_GROUPS = 8
GM_GROUP_DIM = 256
XA_HEADS = 4
XA_DIM = 256
ADAM_LR = 0.001
ADAM_B1 = 0.9
ADAM_B2 = 0.999
ADAM_EPS = 1e-08
ADAM_WD = 0.01
ADAM_STEP = 10

VMEM_CAP_BYTES = 60 * 1024 * 1024
LANES = 1024


def _pick(n, cap, mult=16):
    if n <= cap:
        return n
    for d in range(cap - cap % mult, 0, -mult):
        if n % d == 0:
            return d
    raise ValueError((n, cap, mult))


def _dg(a, b, ca, cb):
    return lax.dot_general(a.astype(BF), b.astype(BF), (((ca,), (cb,)), ((), ())), preferred_element_type=F32)


@jax.custom_vjp
def dot_nn(a, b):
    return _dg(a, b, 1, 0)


def _nn_fwd(a, b):
    return _dg(a, b, 1, 0), (a, b)


def _nn_bwd(r, g):
    a, b = r
    return _dg(g, b, 1, 1), _dg(a, g, 0, 0)


dot_nn.defvjp(_nn_fwd, _nn_bwd)


@jax.custom_vjp
def dot_nt(a, b):
    return _dg(a, b, 1, 1)


def _nt_fwd(a, b):
    return _dg(a, b, 1, 1), (a, b)


def _nt_bwd(r, g):
    a, b = r
    return _dg(g, b, 1, 0), _dg(g, a, 0, 0)


dot_nt.defvjp(_nt_fwd, _nt_bwd)


@jax.custom_vjp
def dot_tn(a, b):
    return _dg(a, b, 0, 0)


def _tn_fwd(a, b):
    return _dg(a, b, 0, 0), (a, b)


def _tn_bwd(r, g):
    a, b = r
    return _dg(b, g, 1, 1), _dg(a, g, 1, 0)


dot_tn.defvjp(_tn_fwd, _tn_bwd)


def _rmsnorm(x, g):
    return x * lax.rsqrt(jnp.mean(x * x, axis=-1, keepdims=True) + EPS) * g


def _silu(x):
    return x * jax.nn.sigmoid(x)


@jax.custom_vjp
def _gelu(x):
    return 0.5 * x * (1.0 + lax.erf(x * (0.5 ** 0.5)))


def _gelu_fwd(x):
    return _gelu(x), x


def _gelu_bwd(x, g):
    t = x * (0.5 ** 0.5)
    cdf = 0.5 * (1.0 + lax.erf(t))
    return (g * (cdf + x * (jnp.exp(-(t * t)) * (0.5 / 3.141592653589793) ** 0.5)),)


_gelu.defvjp(_gelu_fwd, _gelu_bwd)


def _softmax_last(s):
    m = lax.stop_gradient(jnp.max(s, axis=-1, keepdims=True))
    e = jnp.exp(s - m)
    return e / jnp.sum(e, axis=-1, keepdims=True)


def _tril(n):
    r = lax.broadcasted_iota(jnp.int32, (n, n), 0)
    c = lax.broadcasted_iota(jnp.int32, (n, n), 1)
    return r >= c


def _attention(zx, mk, mv):
    s = dot_nt(zx, mk) * (XA_DIM ** -0.5)
    return dot_nn(_softmax_last(s), mv)


def _hgrn_decays(zf, lb3, sub):
    l0, l1, l2 = lb3[0:1], lb3[1:2], lb3[2:3]
    m = lax.stop_gradient(jnp.maximum(jnp.maximum(l0, l1), l2))
    e0 = jnp.exp(l0 - m)
    lb = e0 / (e0 + jnp.exp(l1 - m) + jnp.exp(l2 - m))
    f = lb + (1.0 - lb) * jax.nn.sigmoid(zf)
    n = sub * HG_CHUNK
    r = lax.broadcasted_iota(jnp.int32, (n, n), 0)
    c = lax.broadcasted_iota(jnp.int32, (n, n), 1)
    tri = jnp.logical_and(r >= c, r // HG_CHUNK == c // HG_CHUNK).astype(F32)
    b = lax.dot_general(tri, jnp.log(f), (((1,), (0,)), ((), ())), precision=lax.Precision.HIGHEST, preferred_element_type=F32)
    return f, b


def _hgrn_head(zq, f, b, zi, zg, gn, S):
    q = _silu(zq)
    k = 1.0 - f
    b_last = b[HG_CHUNK - 1:HG_CHUNK, :]
    q_dec = q * jnp.exp(b)
    k_inv = k * jnp.exp(-b)
    a = jnp.where(_tril(HG_CHUNK), dot_nt(q_dec, k_inv), 0.0)
    o = dot_nn(a, zi) + dot_nn(q_dec, S)
    S_new = jnp.exp(b_last).reshape(HG_DIM, 1) * S + dot_tn(k * jnp.exp(b_last - b), zi)
    o = _rmsnorm(o, gn) * _silu(zg)
    return o, S_new


def _gmlp_block(zu, zv, zx, lng, lnb, ws, bs, mk, mv):
    gv = [_gelu(v) for v in zv]
    width = GM_GROUPS * GM_GROUP_DIM
    mu = sum(jnp.sum(g, axis=-1, keepdims=True) for g in gv) / width
    xc = [g - mu for g in gv]
    var = sum(jnp.sum(c * c, axis=-1, keepdims=True) for c in xc) / width
    r = lax.rsqrt(var + EPS)
    outs = []
    for g in range(GM_GROUPS):
        v = xc[g] * r * lng[g] + lnb[g]
        w = jnp.where(_tril(GM_CHUNK), ws[g], 0.0)
        mixed = dot_nn(w, v) + bs[g].reshape(GM_CHUNK, 1)
        outs.append(_gelu(zu[g]) * mixed)
    for a in range(XA_HEADS):
        outs.append(_attention(zx[a], mk[a], mv[a]))
    return outs


def _rowcall(name, fn, rows, consts, row_outs, acc_outs, tr):
    nrows = rows[0][0].shape[0]
    tr = _pick(nrows, tr)
    n_r, n_c, n_ro, n_ao = len(rows), len(consts), len(row_outs), len(acc_outs)

    def kern(*refs):
        rv = [r[...] for r in refs[:n_r]]
        cv = [r[...] for r in refs[n_r:n_r + n_c]]
        ro_refs = refs[n_r + n_c:n_r + n_c + n_ro]
        ao_refs = refs[n_r + n_c + n_ro:]
        ro, ao = fn(rv, cv)
        for ref, v in zip(ro_refs, ro):
            ref[...] = v.astype(ref.dtype)
        if n_ao:
            @pl.when(pl.program_id(0) == 0)
            def _():
                for ref in ao_refs:
                    ref[...] = jnp.zeros(ref.shape, ref.dtype)

            for ref, v in zip(ao_refs, ao):
                ref[...] += v.astype(ref.dtype)

    in_specs = [pl.BlockSpec((tr, w), functools.partial(lambda i, cb: (i, cb), cb=cb)) for (_, cb, w) in rows]
    in_specs += [pl.BlockSpec(c.shape, lambda i: (0, 0)) for c in consts]
    out_specs = [pl.BlockSpec((tr, w), lambda i: (i, 0)) for (w, _) in row_outs]
    out_specs += [pl.BlockSpec(s, lambda i: (0, 0)) for (s, _) in acc_outs]
    out_shape = [jax.ShapeDtypeStruct((nrows, w), dt) for (w, dt) in row_outs]
    out_shape += [jax.ShapeDtypeStruct(s, dt) for (s, dt) in acc_outs]
    outs = pl.pallas_call(
        kern, grid=(nrows // tr,), in_specs=in_specs, out_specs=out_specs, out_shape=out_shape, name=name,
        compiler_params=pltpu.CompilerParams(dimension_semantics=("arbitrary",),
                                             vmem_limit_bytes=VMEM_CAP_BYTES),
    )(*[a for (a, _, _) in rows], *consts)
    return outs


def _mm(name, a, b, mode, out_dtype, tm, tn, tk, scale=1.0, res=None, a_lead=None, b_lead=None, norm_gain=None):
    ash = a.shape[-2:]
    bsh = b.shape[-2:]
    if mode == "nn":
        (M, K), (K2, N) = ash, bsh
    elif mode == "nt":
        (M, K), (N, K2) = ash, bsh
    else:
        (K, M), (K2, N) = ash, bsh
    assert K == K2, (name, a.shape, b.shape)
    tm, tn, tk = min(tm, M), min(tn, N), min(tk, K)
    assert M % tm == 0 and N % tn == 0 and K % tk == 0, (name, M, N, K, tm, tn, tk)
    nk = K // tk
    dims = {"nn": (1, 0), "nt": (1, 1), "tn": (0, 0)}[mode]

    def lead(spec_shape, index_fn, lead_idx):
        if lead_idx is None:
            return pl.BlockSpec(spec_shape, index_fn)
        return pl.BlockSpec((None,) + spec_shape, lambda i, j, k: (lead_idx,) + index_fn(i, j, k))

    if mode == "tn":
        a_spec = lead((tk, tm), lambda i, j, k: (k, i), a_lead)
    else:
        a_spec = lead((tm, tk), lambda i, j, k: (i, k), a_lead)
    if mode == "nt":
        b_spec = lead((tn, tk), lambda i, j, k: (j, k), b_lead)
    else:
        b_spec = lead((tk, tn), lambda i, j, k: (k, j), b_lead)
    o_spec = pl.BlockSpec((tm, tn), lambda i, j, k: (i, j))
    has_res = res is not None
    has_norm = norm_gain is not None
    assert not has_norm or tn == N

    def kern(*refs):
        a_ref, b_ref = refs[0], refs[1]
        pos = 2
        res_ref = gain_ref = h_ref = None
        if has_res:
            res_ref, pos = refs[pos], pos + 1
        if has_norm:
            gain_ref, pos = refs[pos], pos + 1
        o_ref, pos = refs[pos], pos + 1
        if has_norm:
            h_ref = refs[pos]
        acc_ref = refs[-1] if nk > 1 else None
        p = lax.dot_general(a_ref[...].astype(BF), b_ref[...].astype(BF), (((dims[0],), (dims[1],)), ((), ())),
                            preferred_element_type=F32)

        def finish(v):
            if scale != 1.0:
                v = v * scale
            if has_res:
                v = res_ref[...] + v
            o_ref[...] = v.astype(o_ref.dtype)
            if has_norm:
                h_ref[...] = _rmsnorm(v, gain_ref[...]).astype(h_ref.dtype)

        if nk == 1:
            finish(p)
        else:
            k = pl.program_id(2)

            @pl.when(k == 0)
            def _():
                acc_ref[...] = p

            @pl.when(k > 0)
            def _():
                acc_ref[...] += p

            @pl.when(k == nk - 1)
            def _():
                finish(acc_ref[...])

    ins = [a, b] + ([res] if has_res else []) + ([norm_gain] if has_norm else [])
    in_specs = [a_spec, b_spec] + ([o_spec] if has_res else [])
    in_specs += [pl.BlockSpec((1, N), lambda i, j, k: (0, 0))] if has_norm else []
    out_sd = jax.ShapeDtypeStruct((M, N), out_dtype)
    return pl.pallas_call(
        kern, grid=(M // tm, N // tn, nk), in_specs=in_specs,
        out_specs=[o_spec, o_spec] if has_norm else o_spec,
        out_shape=[out_sd, jax.ShapeDtypeStruct((M, N), BF)] if has_norm else out_sd,
        scratch_shapes=[pltpu.VMEM((tm, tn), F32)] if nk > 1 else [],
        name=name,
        compiler_params=pltpu.CompilerParams(dimension_semantics=("parallel", "parallel", "arbitrary"),
                                             vmem_limit_bytes=VMEM_CAP_BYTES),
    )(*ins)


def _ffn_in_swiglu(name, h, w3, tm, tn):
    T, D = h.shape
    dff = w3.shape[2] // 2
    tm = min(tm, T)
    assert T % tm == 0 and dff % tn == 0
    nj = dff // tn

    def kern(h_ref, wg_ref, wu_ref, zg_ref, zu_ref, a_ref):
        hb = h_ref[...]
        g = jnp.dot(hb, wg_ref[...], preferred_element_type=F32).astype(BF)
        u = jnp.dot(hb, wu_ref[...], preferred_element_type=F32).astype(BF)
        zg_ref[...] = g
        zu_ref[...] = u
        a_ref[...] = (_silu(g.astype(F32)) * u.astype(F32)).astype(BF)

    o_spec = pl.BlockSpec((tm, tn), lambda i, j: (i, j))
    return pl.pallas_call(
        kern, grid=(T // tm, nj),
        in_specs=[pl.BlockSpec((tm, D), lambda i, j: (i, 0)),
                  pl.BlockSpec((None, D, tn), lambda i, j: (0, 0, j)),
                  pl.BlockSpec((None, D, tn), lambda i, j: (0, 0, j + nj))],
        out_specs=[o_spec, o_spec, o_spec],
        out_shape=[jax.ShapeDtypeStruct((T, dff), BF)] * 3, name=name,
        compiler_params=pltpu.CompilerParams(dimension_semantics=("parallel", "arbitrary"),
                                             vmem_limit_bytes=VMEM_CAP_BYTES),
    )(h, w3, w3)


def _ffn_da_swiglu(name, dxo, w3, zg, zu, tm):
    T, D = dxo.shape
    dff = w3.shape[1]
    tm = min(tm, T)
    assert T % tm == 0 and dff % 2 == 0
    hc = dff // 2

    def kern(d_ref, w_ref, g_ref, u_ref, dz_ref):
        db = (d_ref[...] * 0.5).astype(BF)
        for s in range(2):
            cols = slice(s * hc, (s + 1) * hc)
            da = lax.dot_general(db, w_ref[cols, :], (((1,), (1,)), ((), ())), preferred_element_type=F32)
            g = g_ref[:, cols].astype(F32)
            sg = 1.0 / (1.0 + jnp.exp(-g))
            gs = g * sg
            dab = da.astype(BF)
            dz_ref[:, cols] = (dab * u_ref[:, cols]) * (sg + gs * (1.0 - sg)).astype(BF)
            dz_ref[:, dff + s * hc:dff + (s + 1) * hc] = dab * gs.astype(BF)

    row = lambda w: pl.BlockSpec((tm, w), lambda i: (i, 0))
    return pl.pallas_call(
        kern, grid=(T // tm,),
        in_specs=[row(D), pl.BlockSpec((None, dff, D), lambda i: (0, 0, 0), pipeline_mode=pl.Buffered(1)), row(dff), row(dff)],
        out_specs=row(2 * dff), out_shape=jax.ShapeDtypeStruct((T, 2 * dff), BF), name=name,
        compiler_params=pltpu.CompilerParams(dimension_semantics=("arbitrary",), vmem_limit_bytes=VMEM_CAP_BYTES),
    )(dxo, w3, zg, zu)


def _mm_dh_rms(name, dz, w3, xin, g, dres, tm):
    T, K = dz.shape
    D = w3.shape[1]
    tm = min(tm, T)
    assert T % tm == 0

    def kern(dz_ref, w_ref, x_ref, g_ref, r_ref, dx_ref, dg_ref):
        dh = lax.dot_general(dz_ref[...], w_ref[...], (((1,), (1,)), ((), ())), preferred_element_type=F32)
        _, vjp = jax.vjp(_rmsnorm, x_ref[...], g_ref[...])
        dx, dg = vjp(dh)
        dx_ref[...] = dx + r_ref[...]

        @pl.when(pl.program_id(0) == 0)
        def _():
            dg_ref[...] = jnp.zeros(dg_ref.shape, F32)

        dg_ref[...] += dg

    row = lambda w: pl.BlockSpec((tm, w), lambda i: (i, 0))
    one = pl.BlockSpec((1, D), lambda i: (0, 0))
    return pl.pallas_call(
        kern, grid=(T // tm,),
        in_specs=[row(K), pl.BlockSpec((None, D, K), lambda i: (0, 0, 0), pipeline_mode=pl.Buffered(1)), row(D), one, row(D)],
        out_specs=[row(D), one], out_shape=[jax.ShapeDtypeStruct((T, D), F32), jax.ShapeDtypeStruct((1, D), F32)], name=name,
        compiler_params=pltpu.CompilerParams(dimension_semantics=("arbitrary",), vmem_limit_bytes=VMEM_CAP_BYTES),
    )(dz, w3, xin, g, dres)


def _mm_tn_pair(name, a, b, kind, c_arr, tq, tk, scale=1.0):
    T, M = a.shape
    _, N = b.shape
    tk = min(tk, T)
    assert T % tk == 0
    nk = T // tk
    if kind == "col":
        hm = M // 2
        assert N % tq == 0
        nq = N // tq
        tile = (hm, tq)
        a_spec = pl.BlockSpec((tk, hm), lambda h, q, k, c: (k, jnp.bitwise_xor(h, 1 - c[0])))
        b_spec = pl.BlockSpec((tk, tq), lambda h, q, k, c: (k, q))
        o_spec = pl.BlockSpec(tile, lambda h, q, k, c: (0, q * h))
        out_sd = (hm, N)
    else:
        hn = N // 2
        assert M % tq == 0
        nq = M // tq
        tile = (tq, hn)
        a_spec = pl.BlockSpec((tk, tq), lambda h, q, k, c: (k, q))
        b_spec = pl.BlockSpec((tk, hn), lambda h, q, k, c: (k, jnp.bitwise_xor(h, 1 - c[0])))
        o_spec = pl.BlockSpec(tile, lambda h, q, k, c: (q * h, 0))
        out_sd = (M, hn)

    def kern(c_ref, a_ref, b_ref, o_ref, acc, stage, recv, ssem, rsem):
        h, q, k = pl.program_id(0), pl.program_id(1), pl.program_id(2)
        x, y, c, _ = _place()
        p = lax.dot_general(a_ref[...].astype(BF), b_ref[...].astype(BF), (((0,), (0,)), ((), ())), preferred_element_type=F32)

        @pl.when(k == 0)
        def _():
            acc[...] = p

        @pl.when(k > 0)
        def _():
            acc[...] += p

        def send(slot, qq):
            return pltpu.make_async_remote_copy(src_ref=stage.at[slot], dst_ref=recv.at[qq], send_sem=ssem.at[slot],
                                                recv_sem=rsem.at[qq], device_id=(x, y, 1 - c), device_id_type=MESH)

        last = k == nk - 1

        @pl.when(jnp.logical_and(last, h == 0))
        def _():
            slot = q % 2

            @pl.when(q >= 2)
            def _():
                send(slot, q).wait_send()

            stage[slot] = (acc[...] * scale).astype(BF)
            send(slot, q).start()

        @pl.when(jnp.logical_and(last, h == 1))
        def _():
            @pl.when(q == 0)
            def _():
                for s in range(min(nq, 2)):
                    send(s, 0).wait_send()

            send(0, q).wait_recv()
            o_ref[...] = (acc[...] * scale + recv[q].astype(F32)).astype(o_ref.dtype)

    return pl.pallas_call(
        kern,
        grid_spec=pltpu.PrefetchScalarGridSpec(
            num_scalar_prefetch=1, grid=(2, nq, nk), in_specs=[a_spec, b_spec], out_specs=o_spec,
            scratch_shapes=[pltpu.VMEM(tile, F32), pltpu.VMEM((2,) + tile, BF), pltpu.VMEM((nq,) + tile, BF),
                            pltpu.SemaphoreType.DMA((2,)), pltpu.SemaphoreType.DMA((nq,))]),
        out_shape=jax.ShapeDtypeStruct(out_sd, BF), name=name,
        compiler_params=pltpu.CompilerParams(dimension_semantics=("arbitrary", "arbitrary", "arbitrary"),
                                             vmem_limit_bytes=VMEM_CAP_BYTES),
    )(c_arr, a, b)


def _kv_pieces(kv_ref):
    W = XA_HEADS * XA_DIM
    mk = [kv_ref[:, a * XA_DIM:(a + 1) * XA_DIM] for a in range(XA_HEADS)]
    mv = [kv_ref[:, W + a * XA_DIM:W + (a + 1) * XA_DIM] for a in range(XA_HEADS)]
    return mk, mv


HG_SUB = 4


def _hgrn_rows(z_ref):
    W = HG_HEADS * HG_DIM

    def piece(c, col, w):
        return z_ref[c * HG_CHUNK:(c + 1) * HG_CHUNK, col:col + w]

    zq = [[piece(c, h * HG_DIM, HG_DIM) for h in range(HG_HEADS)] for c in range(HG_SUB)]
    zf = z_ref[:, W:2 * W]
    zi =[[piece(c, 2 * W + h * HG_DIM, HG_DIM) for h in range(HG_HEADS)] for c in range(HG_SUB)]
    zg = [[piece(c, 3 * W + h * HG_DIM, HG_DIM) for h in range(HG_HEADS)] for c in range(HG_SUB)]
    zx = [z_ref[:, 4 * W + a * XA_DIM:4 * W + (a + 1) * XA_DIM] for a in range(XA_HEADS)]
    return zq, zf, zi, zg, zx


def _hgrn_steps(zq, zf, zi, zg, zx, lb3, gn, mk, mv, S):
    f, b = _hgrn_decays(zf, lb3, HG_SUB)
    mix = []
    for c in range(HG_SUB):
        row, s_next = [], []
        rows = slice(c * HG_CHUNK, (c + 1) * HG_CHUNK)
        for h in range(HG_HEADS):
            cols = slice(h * HG_DIM, (h + 1) * HG_DIM)
            o, sn = _hgrn_head(zq[c][h], f[rows, cols], b[rows, cols], zi[c][h], zg[c][h], gn, S[h])
            row.append(o)
            s_next.append(sn)
        mix.append(row)
        S = s_next
    att = [_attention(zx[a], mk[a], mv[a]) for a in range(XA_HEADS)]
    return mix, att, S


def _hgrn_fwd2(z, lb_logits, gnorm, kv, bl, seq):
    T, zw = z.shape
    mem_len = kv.shape[0] // bl
    cat_w = HG_HEADS * HG_DIM + XA_HEADS * XA_DIM
    R = HG_SUB * HG_CHUNK
    nb = seq // R

    def kern(z_ref, lb_ref, gn_ref, kv_ref, cat_ref, st_ref, s_scr):
        @pl.when(pl.program_id(1) == 0)
        def _():
            s_scr[...] = jnp.zeros(s_scr.shape, F32)

        st_ref[...] = s_scr[...]
        zq, zf, zi, zg, zx = _hgrn_rows(z_ref)
        mk, mv = _kv_pieces(kv_ref)
        S = [s_scr[h] for h in range(HG_HEADS)]
        mix, att, s_new = _hgrn_steps(zq, zf, zi, zg, zx, lb_ref[...], gn_ref[...], mk, mv, S)
        for c in range(HG_SUB):
            for h in range(HG_HEADS):
                cat_ref[c * HG_CHUNK:(c + 1) * HG_CHUNK, h * HG_DIM:(h + 1) * HG_DIM] = mix[c][h].astype(cat_ref.dtype)
        for h in range(HG_HEADS):
            s_scr[h] = s_new[h]
        base = HG_HEADS * HG_DIM
        for a in range(XA_HEADS):
            cat_ref[:, base + a * XA_DIM:base + (a + 1) * XA_DIM] = att[a].astype(cat_ref.dtype)

    return pl.pallas_call(
        kern, grid=(bl, nb),
        in_specs=[pl.BlockSpec((R, zw), lambda b, n: (b * nb + n, 0)),
                  pl.BlockSpec(lb_logits.shape, lambda b, n: (0, 0)),
                  pl.BlockSpec(gnorm.shape, lambda b, n: (0, 0)),
                  pl.BlockSpec((mem_len, kv.shape[1]), lambda b, n: (b, 0))],
        out_specs=[pl.BlockSpec((R, cat_w), lambda b, n: (b * nb + n, 0)),
                   pl.BlockSpec((None, HG_HEADS, HG_DIM, HG_DIM), lambda b, n: (b * nb + n, 0, 0, 0))],
        out_shape=[jax.ShapeDtypeStruct((T, cat_w), BF),
                   jax.ShapeDtypeStruct((bl * nb, HG_HEADS, HG_DIM, HG_DIM), F32)],
        scratch_shapes=[pltpu.VMEM((HG_HEADS, HG_DIM, HG_DIM), F32)],
        name="hgrn_fwd",
        compiler_params=pltpu.CompilerParams(dimension_semantics=("arbitrary", "arbitrary"), vmem_limit_bytes=VMEM_CAP_BYTES),
    )(z, lb_logits, gnorm, kv)


def _hgrn_bwd2(z, dcat, stash, lb_logits, gnorm, kv, bl, seq):
    T, zw = z.shape
    mem_len = kv.shape[0] // bl
    cat_w = dcat.shape[1]
    R = HG_SUB * HG_CHUNK
    nb = seq // R

    def kern(z_ref, dc_ref, st_ref, lb_ref, gn_ref, kv_ref, dz_ref, dkv_ref, dlb_ref, dgn_ref, ds_scr):
        first = jnp.logical_and(pl.program_id(0) == 0, pl.program_id(1) == 0)

        @pl.when(pl.program_id(1) == 0)
        def _():
            ds_scr[...] = jnp.zeros(ds_scr.shape, F32)
            dkv_ref[...] = jnp.zeros(dkv_ref.shape, F32)

        @pl.when(first)
        def _():
            dlb_ref[...] = jnp.zeros(dlb_ref.shape, F32)
            dgn_ref[...] = jnp.zeros(dgn_ref.shape, F32)

        zq, zf, zi, zg, zx = _hgrn_rows(z_ref)
        mk, mv = _kv_pieces(kv_ref)
        S = [st_ref[h] for h in range(HG_HEADS)]
        _, vjp = jax.vjp(_hgrn_steps, zq, zf, zi, zg, zx, lb_ref[...], gn_ref[...], mk, mv, S)
        d_mix = [[dc_ref[c * HG_CHUNK:(c + 1) * HG_CHUNK, h * HG_DIM:(h + 1) * HG_DIM] for h in range(HG_HEADS)]
                 for c in range(HG_SUB)]
        base = HG_HEADS * HG_DIM
        d_att = [dc_ref[:, base + a * XA_DIM:base + (a + 1) * XA_DIM] for a in range(XA_HEADS)]
        d_s = [ds_scr[h] for h in range(HG_HEADS)]
        dzq, dzf, dzi, dzg, dzx, dlb3, dgn, dmk, dmv, dS = vjp((d_mix, d_att, d_s))
        W = HG_HEADS * HG_DIM
        dz_ref[:, W:2 * W] = dzf.astype(dz_ref.dtype)
        for c in range(HG_SUB):
            rows = slice(c * HG_CHUNK, (c + 1) * HG_CHUNK)
            for h in range(HG_HEADS):
                for k, part in ((0, dzq), (2, dzi), (3, dzg)):
                    dz_ref[rows, k * W + h * HG_DIM:k * W + (h + 1) * HG_DIM] = part[c][h].astype(dz_ref.dtype)
        for h in range(HG_HEADS):
            ds_scr[h] = dS[h]
        dlb_ref[...] += dlb3
        dgn_ref[...] += dgn
        KW = XA_HEADS * XA_DIM
        for a in range(XA_HEADS):
            dz_ref[:, 4 * W + a * XA_DIM:4 * W + (a + 1) * XA_DIM] = dzx[a].astype(dz_ref.dtype)
            dkv_ref[:, a * XA_DIM:(a + 1) * XA_DIM] += dmk[a]
            dkv_ref[:, KW + a * XA_DIM:KW + (a + 1) * XA_DIM] += dmv[a]

    rev = lambda b, n: (b * nb + (nb - 1 - n), 0)
    return pl.pallas_call(
        kern, grid=(bl, nb),
        in_specs=[pl.BlockSpec((R, zw), rev),
                  pl.BlockSpec((R, cat_w), rev),
                  pl.BlockSpec((None, HG_HEADS, HG_DIM, HG_DIM), lambda b, n: (b * nb + (nb - 1 - n), 0, 0, 0)),
                  pl.BlockSpec(lb_logits.shape, lambda b, n: (0, 0)),
                  pl.BlockSpec(gnorm.shape, lambda b, n: (0, 0)),
                  pl.BlockSpec((mem_len, kv.shape[1]), lambda b, n: (b, 0))],
        out_specs=[pl.BlockSpec((R, zw), rev),
                   pl.BlockSpec((mem_len, kv.shape[1]), lambda b, n: (b, 0)),
                   pl.BlockSpec(lb_logits.shape, lambda b, n: (0, 0)),
                   pl.BlockSpec(gnorm.shape, lambda b, n: (0, 0))],
        out_shape=[jax.ShapeDtypeStruct((T, zw), BF), jax.ShapeDtypeStruct(kv.shape, F32),
                   jax.ShapeDtypeStruct(lb_logits.shape, F32), jax.ShapeDtypeStruct(gnorm.shape, F32)],
        scratch_shapes=[pltpu.VMEM((HG_HEADS, HG_DIM, HG_DIM), F32)],
        name="hgrn_bwd",
        compiler_params=pltpu.CompilerParams(dimension_semantics=("arbitrary", "arbitrary"), vmem_limit_bytes=VMEM_CAP_BYTES),
    )(z, dcat, stash, lb_logits, gnorm, kv)


GM_SUB = 2


def _gmlp_pieces(z_ref):
    W = GM_GROUPS * GM_GROUP_DIM
    zu = [z_ref[:, g * GM_GROUP_DIM:(g + 1) * GM_GROUP_DIM] for g in range(GM_GROUPS)]
    zv = [z_ref[:, W + g * GM_GROUP_DIM:W + (g + 1) * GM_GROUP_DIM] for g in range(GM_GROUPS)]
    zx = [z_ref[:, 2 * W + a * XA_DIM:2 * W + (a + 1) * XA_DIM] for a in range(XA_HEADS)]
    return zu, zv, zx


def _gmlp_params(lng_ref, lnb_ref, ws_ref, bs_ref):
    lng = [lng_ref[:, g * GM_GROUP_DIM:(g + 1) * GM_GROUP_DIM] for g in range(GM_GROUPS)]
    lnb = [lnb_ref[:, g * GM_GROUP_DIM:(g + 1) * GM_GROUP_DIM] for g in range(GM_GROUPS)]
    ws = [ws_ref[g] for g in range(GM_GROUPS)]
    bs = [bs_ref[g:g + 1, :] for g in range(GM_GROUPS)]
    return lng, lnb, ws, bs


def _gmlp_fwd(z, ln_g, ln_b, w_s, b_s, kv, bl, nc):
    T, zw = z.shape
    mem_len = kv.shape[0] // bl
    cat_w = GM_GROUPS * GM_GROUP_DIM + XA_HEADS * XA_DIM

    assert nc % GM_SUB == 0
    nc = nc // GM_SUB
    R = GM_SUB * GM_CHUNK

    def kern(z_ref, lng_ref, lnb_ref, ws_ref, bs_ref, kv_ref, cat_ref):
        lng, lnb, ws, bs = _gmlp_params(lng_ref, lnb_ref, ws_ref, bs_ref)
        mk, mv = _kv_pieces(kv_ref)
        for c in range(GM_SUB):
            rows = pl.ds(c * GM_CHUNK, GM_CHUNK)
            zu, zv, zx = _gmlp_pieces(z_ref.at[rows])
            out = cat_ref.at[rows]
            outs = _gmlp_block(zu, zv, zx, lng, lnb, ws, bs, mk, mv)
            for g in range(GM_GROUPS):
                out[:, g * GM_GROUP_DIM:(g + 1) * GM_GROUP_DIM] = outs[g].astype(cat_ref.dtype)
            base = GM_GROUPS * GM_GROUP_DIM
            for a in range(XA_HEADS):
                out[:, base + a * XA_DIM:base + (a + 1) * XA_DIM] = outs[GM_GROUPS + a].astype(cat_ref.dtype)

    full2 = lambda b, n: (0, 0)
    return pl.pallas_call(
        kern, grid=(bl, nc),
        in_specs=[pl.BlockSpec((R, zw), lambda b, n: (b * nc + n, 0)),
                  pl.BlockSpec(ln_g.shape, full2), pl.BlockSpec(ln_b.shape, full2),
                  pl.BlockSpec(w_s.shape, lambda b, n: (0, 0, 0)), pl.BlockSpec(b_s.shape, full2),
                  pl.BlockSpec((mem_len, kv.shape[1]), lambda b, n: (b, 0))],
        out_specs=pl.BlockSpec((R, cat_w), lambda b, n: (b * nc + n, 0)),
        out_shape=jax.ShapeDtypeStruct((T, cat_w), BF),
        name="gmlp_fwd",
        compiler_params=pltpu.CompilerParams(dimension_semantics=("arbitrary", "arbitrary"), vmem_limit_bytes=VMEM_CAP_BYTES),
    )(z, ln_g, ln_b, w_s, b_s, kv)


def _gmlp_bwd(z, dcat, ln_g, ln_b, w_s, b_s, kv, bl, nc):
    T, zw = z.shape
    mem_len = kv.shape[0] // bl
    cat_w = dcat.shape[1]
    assert nc % GM_SUB == 0
    nc = nc // GM_SUB

    def kern(z_ref, dc_ref, lng_ref, lnb_ref, ws_ref, bs_ref, kv_ref,
             dz_ref, dkv_ref, dlng_ref, dlnb_ref, dws_ref, dbs_ref):
        first = jnp.logical_and(pl.program_id(0) == 0, pl.program_id(1) == 0)

        @pl.when(pl.program_id(1) == 0)
        def _():
            dkv_ref[...] = jnp.zeros(dkv_ref.shape, F32)

        @pl.when(first)
        def _():
            dlng_ref[...] = jnp.zeros(dlng_ref.shape, F32)
            dlnb_ref[...] = jnp.zeros(dlnb_ref.shape, F32)
            dws_ref[...] = jnp.zeros(dws_ref.shape, F32)
            dbs_ref[...] = jnp.zeros(dbs_ref.shape, F32)

        lng, lnb, ws, bs = _gmlp_params(lng_ref, lnb_ref, ws_ref, bs_ref)
        mk, mv = _kv_pieces(kv_ref)
        W = GM_GROUPS * GM_GROUP_DIM
        KW = XA_HEADS * XA_DIM
        for c in range(GM_SUB):
            rows = pl.ds(c * GM_CHUNK, GM_CHUNK)
            zu, zv, zx = _gmlp_pieces(z_ref.at[rows])
            dc, dz = dc_ref.at[rows], dz_ref.at[rows]
            _, vjp = jax.vjp(_gmlp_block, zu, zv, zx, lng, lnb, ws, bs, mk, mv)
            d_outs = [dc[:, g * GM_GROUP_DIM:(g + 1) * GM_GROUP_DIM] for g in range(GM_GROUPS)]
            d_outs += [dc[:, W + a * XA_DIM:W + (a + 1) * XA_DIM] for a in range(XA_HEADS)]
            dzu, dzv, dzx, dlng, dlnb, dws, dbs, dmk, dmv = vjp(d_outs)
            for g in range(GM_GROUPS):
                sl = slice(g * GM_GROUP_DIM, (g + 1) * GM_GROUP_DIM)
                dz[:, sl] = dzu[g].astype(dz_ref.dtype)
                dz[:, W + g * GM_GROUP_DIM:W + (g + 1) * GM_GROUP_DIM] = dzv[g].astype(dz_ref.dtype)
                dlng_ref[:, sl] += dlng[g]
                dlnb_ref[:, sl] += dlnb[g]
                dws_ref[g] += dws[g]
                dbs_ref[g:g + 1, :] += dbs[g]
            for a in range(XA_HEADS):
                dz[:, 2 * W + a * XA_DIM:2 * W + (a + 1) * XA_DIM] = dzx[a].astype(dz_ref.dtype)
                dkv_ref[:, a * XA_DIM:(a + 1) * XA_DIM] += dmk[a]
                dkv_ref[:, KW + a * XA_DIM:KW + (a + 1) * XA_DIM] += dmv[a]

    full2 = lambda b, n: (0, 0)
    full3 = lambda b, n: (0, 0, 0)
    blk = lambda b, n: (b * nc + n, 0)
    return pl.pallas_call(
        kern, grid=(bl, nc),
        in_specs=[pl.BlockSpec((GM_SUB * GM_CHUNK, zw), blk), pl.BlockSpec((GM_SUB * GM_CHUNK, cat_w), blk),
                  pl.BlockSpec(ln_g.shape, full2), pl.BlockSpec(ln_b.shape, full2),
                  pl.BlockSpec(w_s.shape, full3), pl.BlockSpec(b_s.shape, full2),
                  pl.BlockSpec((mem_len, kv.shape[1]), lambda b, n: (b, 0))],
        out_specs=[pl.BlockSpec((GM_SUB * GM_CHUNK, zw), blk),
                   pl.BlockSpec((mem_len, kv.shape[1]), lambda b, n: (b, 0)),
                   pl.BlockSpec(ln_g.shape, full2), pl.BlockSpec(ln_b.shape, full2),
                   pl.BlockSpec(w_s.shape, full3), pl.BlockSpec(b_s.shape, full2)],
        out_shape=[jax.ShapeDtypeStruct((T, zw), BF), jax.ShapeDtypeStruct(kv.shape, F32),
                   jax.ShapeDtypeStruct(ln_g.shape, F32), jax.ShapeDtypeStruct(ln_b.shape, F32),
                   jax.ShapeDtypeStruct(w_s.shape, F32), jax.ShapeDtypeStruct(b_s.shape, F32)],
        name="gmlp_bwd",
        compiler_params=pltpu.CompilerParams(dimension_semantics=("arbitrary", "arbitrary"), vmem_limit_bytes=VMEM_CAP_BYTES),
    )(z, dcat, ln_g, ln_b, w_s, b_s, kv)


def _place():
    x, y, c = lax.axis_index("x"), lax.axis_index("y"), lax.axis_index("c")
    chips = [(1 - x, y), (x, 1 - y), (1 - x, 1 - y)]
    return x, y, c, chips


def _half(ref, kind, e):
    if kind == "col":
        n = ref.shape[1] // 2
        return ref.at[:, pl.ds(pl.multiple_of(e * n, n), n), :]
    n = ref.shape[2] // 2
    return ref.at[:, :, pl.ds(pl.multiple_of(e * n, n), n)]


def _slot(ref, kind, j, n):
    if kind == "col":
        return ref.at[:, :, pl.ds(pl.multiple_of(j * n, n), n)]
    return ref.at[:, pl.ds(pl.multiple_of(j * n, n), n), :]


BF16_TILE_ROWS = 16
AG_DIRECT_SIXTEENTHS = 3


def _allgather_seq(name, items, cid):
    nt = len(items)
    kinds = [k for (_, k, _) in items]
    slot_kind = ["row" if k == "row" else "col" for k in kinds]
    out_type = []
    for s, k, l in items:
        L, r, c = s.shape
        lo = L if l is None else 1
        out_type.append(jax.ShapeDtypeStruct((lo, 4 * r, c) if k == "row" else (lo, r, 4 * c), s.dtype))

    def part(ref, t, e):
        return ref if kinds[t] == "vec" else _half(ref, kinds[t], e)

    def split(half):
        rows = half.shape[1]
        direct = rows * AG_DIRECT_SIXTEENTHS // 16 // BF16_TILE_ROWS * BF16_TILE_ROWS
        return half.at[:, pl.ds(0, rows - direct), :], half.at[:, pl.ds(rows - direct, direct), :]

    def body(*refs):
        sh = [refs[t] if items[t][2] is None else refs[t].at[pl.ds(items[t][2], 1)] for t in range(nt)]
        full = refs[nt:2 * nt]
        s_ici, r_ici, s_far, r_far, s_d2d, r_d2d = refs[2 * nt:]
        x, y, c, chips = _place()
        own = 2 * x + y
        sibling = (x, y, 1 - c)
        barrier = pltpu.get_barrier_semaphore()
        for peer in [(px, py, pc) for (px, py) in chips for pc in (0, 1)] + [sibling]:
            pl.semaphore_signal(barrier, inc=1, device_id=peer, device_id_type=MESH)
        pl.semaphore_wait(barrier, 7)
        width = [sh[t].shape[1] if kinds[t] == "row" else sh[t].shape[2] for t in range(nt)]
        sent = []
        for t in range(nt):
            for p, (px, py) in enumerate(chips):
                src, dst = part(sh[t], t, c), part(_slot(full[t], slot_kind[t], own, width[t]), t, c)
                cp = pltpu.make_async_remote_copy(
                    src_ref=src, dst_ref=dst, send_sem=s_ici.at[t, p], recv_sem=r_ici.at[t, p], device_id=(px, py, c),
                    device_id_type=MESH)
                cp.start()
                sent.append(cp)
                if kinds[t] == "vec":
                    continue
                far = pltpu.make_async_remote_copy(
                    src_ref=split(src)[1], dst_ref=split(dst)[1], send_sem=s_far.at[t, p], recv_sem=r_far.at[t, p],
                    device_id=(px, py, 1 - c), device_id_type=MESH)
                far.start()
                sent.append(far)
        for t in range(nt):
            for p, (px, py) in enumerate(chips):
                landed = part(_slot(full[t], slot_kind[t], 2 * px + py, width[t]), t, c)
                pltpu.make_async_remote_copy(
                    src_ref=landed, dst_ref=landed, send_sem=s_ici.at[t, p], recv_sem=r_ici.at[t, p],
                    device_id=(px, py, c), device_id_type=MESH).wait_recv()
                if kinds[t] == "vec":
                    continue
                fw = pltpu.make_async_remote_copy(
                    src_ref=split(landed)[0], dst_ref=split(landed)[0], send_sem=s_d2d.at[t, p], recv_sem=r_d2d.at[t, p],
                    device_id=sibling, device_id_type=MESH)
                fw.start()
                sent.append(fw)
        for t in range(nt):
            if kinds[t] == "vec":
                continue
            for p, (px, py) in enumerate(chips):
                forwarded, direct = split(_half(_slot(full[t], kinds[t], 2 * px + py, width[t]), kinds[t], 1 - c))
                pltpu.make_async_remote_copy(
                    src_ref=forwarded, dst_ref=forwarded, send_sem=s_d2d.at[t, p], recv_sem=r_d2d.at[t, p],
                    device_id=sibling, device_id_type=MESH).wait_recv()
                pltpu.make_async_remote_copy(
                    src_ref=direct, dst_ref=direct, send_sem=s_far.at[t, p], recv_sem=r_far.at[t, p],
                    device_id=(px, py, 1 - c), device_id_type=MESH).wait_recv()
        for cp in sent:
            cp.wait_send()

    sems = pltpu.SemaphoreType.DMA
    return pl.kernel(
        body, out_type=out_type, mesh=plsc.ScalarSubcoreMesh(axis_name="seq", num_cores=1),
        scratch_types=[sems((nt, 3)), sems((nt, 3)), sems((nt, 3)), sems((nt, 3)), sems((nt, 3)), sems((nt, 3))],
        compiler_params=pltpu.CompilerParams(collective_id=cid), name=name,
    )(*[s for (s, _, _) in items])


def _place_own(name, full, shard, kind, layer, chip_arr, after):
    lo, r, c = (shard.shape[0] if layer is None else 1,) + shard.shape[1:]
    first = 0 if layer is None else layer
    tr = _pick(r, 512)
    nr = r // tr

    def body(chip_ref, s_ref, f_ref, after_ref, o_ref):
        o_ref[...] = s_ref[...]

    if kind == "row":
        out_map = lambda i, j, chip: (i, chip[0] * nr + j, 0)
    else:
        out_map = lambda i, j, chip: (i, j, chip[0])
    return pl.pallas_call(
        body, out_shape=jax.ShapeDtypeStruct(full.shape, full.dtype),
        grid_spec=pltpu.PrefetchScalarGridSpec(
            num_scalar_prefetch=1, grid=(lo, nr),
            in_specs=[pl.BlockSpec((1, tr, c), lambda i, j, chip: (i + first, j, 0)), pl.BlockSpec(memory_space=pl.ANY),
                      pl.BlockSpec(memory_space=pl.ANY)],
            out_specs=pl.BlockSpec((1, tr, c), out_map)),
        input_output_aliases={2: 0},
        compiler_params=pltpu.CompilerParams(dimension_semantics=("parallel", "parallel"), vmem_limit_bytes=VMEM_CAP_BYTES),
        name=name,
    )(chip_arr, shard, full, after)


def _slot2(ref, kind, j, n):
    if kind == "col":
        return ref.at[:, pl.ds(pl.multiple_of(j * n, n), n)]
    return ref.at[pl.ds(pl.multiple_of(j * n, n), n), :]


def _rs_chips_seq(name, parts, kinds, cid):
    nm = len(parts)
    out_type = []
    for g, k in zip(parts, kinds):
        r, c = g.shape
        ps = (r, c // 4) if k == "col" else (r // 4, c)
        out_type += [jax.ShapeDtypeStruct(ps, BF), jax.ShapeDtypeStruct((3,) + ps, BF)]

    def body(*refs):
        g = refs[:nm]
        outs = refs[nm:3 * nm]
        loc, ssem, rsem = refs[3 * nm:]
        x, y, c, chips = _place()
        own = 2 * x + y
        barrier = pltpu.get_barrier_semaphore()
        for (px, py) in chips:
            pl.semaphore_signal(barrier, inc=1, device_id=(px, py, c), device_id_type=MESH)
        pl.semaphore_wait(barrier, 3)
        cps = []
        for m in range(nm):
            k = kinds[m]
            own_o, got_o = outs[2 * m], outs[2 * m + 1]
            n = g[m].shape[1] // 4 if k == "col" else g[m].shape[0] // 4
            lc = pltpu.make_async_copy(_slot2(g[m], k, own, n), own_o, loc.at[m])
            lc.start()
            cps.append(lc)
            for p, (px, py) in enumerate(chips):
                cp = pltpu.make_async_remote_copy(
                    src_ref=_slot2(g[m], k, 2 * px + py, n), dst_ref=got_o.at[p],
                    send_sem=ssem.at[m, p], recv_sem=rsem.at[m, p], device_id=(px, py, c), device_id_type=MESH)
                cp.start()
                cps.append(cp)
        for cp in cps:
            cp.wait()

    return pl.kernel(
        body, out_type=out_type, mesh=plsc.ScalarSubcoreMesh(axis_name="seq", num_cores=1),
        scratch_types=[pltpu.SemaphoreType.DMA((nm,)), pltpu.SemaphoreType.DMA((nm, 3)), pltpu.SemaphoreType.DMA((nm, 3))],
        compiler_params=pltpu.CompilerParams(collective_id=cid), name=name,
    )(*parts)


def _finish_share(name, owns, gots, kind, c_arr):
    L = len(owns)
    r, c = owns[0].shape
    tr = _pick(r, 128 if kind == "col" else 256)
    nb = r // tr
    nq = L * nb

    def chunk_of(l):
        return lambda h, q: jnp.clip(q * (1 - h) + (nq - 1) * h - l * nb, 0, nb - 1)

    ins, in_specs = [], []
    for l in range(L):
        at = chunk_of(l)
        ins += [owns[l], gots[l].reshape(3 * r, c), gots[l].reshape(3 * r, c), gots[l].reshape(3 * r, c)]
        in_specs.append(pl.BlockSpec((tr, c), functools.partial(lambda h, q, cc, at: (at(h, q), 0), at=at)))
        in_specs += [pl.BlockSpec((tr, c), functools.partial(lambda h, q, cc, at, p: (p * nb + at(h, q), 0), at=at, p=p))
                     for p in range(3)]
    if kind == "col":
        out_sd = (L, 2, r, c)
        o_spec = pl.BlockSpec((None, 2, tr, c), lambda h, q, cc: ((q * h) // nb, 0, (q * h) % nb, 0))
    else:
        out_sd = (L * r, 2 * c)
        o_spec = pl.BlockSpec((tr, 2 * c), lambda h, q, cc: (q * h, 0))

    def kern(c_ref, *refs):
        in_refs = refs[:4 * L]
        out_ref, mine, recv, ssem, rsem = refs[4 * L:]
        h, q = pl.program_id(0), pl.program_id(1)
        x, y, cc, _ = _place()

        def swap(qq):
            return pltpu.make_async_remote_copy(src_ref=mine.at[qq], dst_ref=recv.at[qq], send_sem=ssem.at[qq],
                                                recv_sem=rsem.at[qq], device_id=(x, y, 1 - cc), device_id_type=MESH)

        for l in range(L):
            @pl.when(jnp.logical_and(h == 0, q // nb == l))
            def _(l=l):
                o_ref, g0, g1, g2 = in_refs[4 * l:4 * l + 4]
                mine[q] = ((o_ref[...].astype(F32) + g0[...].astype(F32)) + g1[...].astype(F32)) + g2[...].astype(F32)
                swap(q).start()

        @pl.when(h == 1)
        def _():
            swap(q).wait()
            a, b = mine[q], recv[q]
            first = c_ref[0] == 0
            lo, hi = jnp.where(first, a, b), jnp.where(first, b, a)
            if kind == "col":
                out_ref[0] = lo
                out_ref[1] = hi
            else:
                out_ref[:, :c] = lo
                out_ref[:, c:] = hi

    full = pl.pallas_call(
        kern,
        grid_spec=pltpu.PrefetchScalarGridSpec(
            num_scalar_prefetch=1, grid=(2, nq), in_specs=in_specs, out_specs=o_spec,
            scratch_shapes=[pltpu.VMEM((nq, tr, c), F32), pltpu.VMEM((nq, tr, c), F32),
                            pltpu.SemaphoreType.DMA((nq,)), pltpu.SemaphoreType.DMA((nq,))]),
        out_shape=jax.ShapeDtypeStruct(out_sd, F32), name=name,
        compiler_params=pltpu.CompilerParams(dimension_semantics=("arbitrary", "arbitrary"),
                                             vmem_limit_bytes=VMEM_CAP_BYTES),
    )(c_arr, *ins)
    return full.reshape(L, 2 * r, c) if kind == "col" else full.reshape(L, r, 2 * c)


def _small_allreduce(buf, name):
    R = buf.shape[0]
    assert R % 16 == 0
    h = R // 2

    def body(x_ref, o_ref, sib, csum, got, s_a, r_a, s_b, r_b, s_c, r_c):
        x, y, c, chips = _place()
        sibling = (x, y, 1 - c)
        own = 2 * x + y
        swap = pltpu.make_async_remote_copy(src_ref=x_ref, dst_ref=sib, send_sem=s_a, recv_sem=r_a,
                                            device_id=sibling, device_id_type=MESH)
        swap.start()
        swap.wait()
        a, b = x_ref[...], sib[...]
        south = c == 0
        csum[...] = jnp.where(south, a, b) + jnp.where(south, b, a)
        lo = pl.multiple_of(c * h, 8)
        mine = csum.at[pl.ds(lo, h)]
        got[own] = csum[pl.ds(lo, h)]
        sends = []
        for p, (px, py) in enumerate(chips):
            cp = pltpu.make_async_remote_copy(src_ref=mine, dst_ref=got.at[own], send_sem=s_b.at[p], recv_sem=r_b.at[p],
                                              device_id=(px, py, c), device_id_type=MESH)
            cp.start()
            sends.append(cp)
        for cp in sends:
            cp.wait()
        o_ref[pl.ds(lo, h)] = ((got[0] + got[1]) + got[2]) + got[3]
        done = o_ref.at[pl.ds(lo, h)]
        back = pltpu.make_async_remote_copy(src_ref=done, dst_ref=done, send_sem=s_c, recv_sem=r_c,
                                            device_id=sibling, device_id_type=MESH)
        back.start()
        back.wait_send()
        other = o_ref.at[pl.ds(pl.multiple_of((1 - c) * h, 8), h)]
        pltpu.make_async_remote_copy(src_ref=other, dst_ref=other, send_sem=s_c, recv_sem=r_c,
                                     device_id=sibling, device_id_type=MESH).wait_recv()

    vm = pl.BlockSpec(memory_space=pltpu.VMEM)
    return pl.pallas_call(
        body, out_shape=jax.ShapeDtypeStruct(buf.shape, F32), in_specs=[vm], out_specs=vm,
        scratch_shapes=[pltpu.VMEM((R, LANES), F32), pltpu.VMEM((R, LANES), F32), pltpu.VMEM((4, h, LANES), F32),
                        pltpu.SemaphoreType.DMA, pltpu.SemaphoreType.DMA, pltpu.SemaphoreType.DMA((3,)),
                        pltpu.SemaphoreType.DMA((3,)), pltpu.SemaphoreType.DMA, pltpu.SemaphoreType.DMA],
        name=name,
        compiler_params=pltpu.CompilerParams(vmem_limit_bytes=VMEM_CAP_BYTES),
    )(buf)


PACK_TILE_ROWS = 8


def _item_rows(shape):
    n = 1
    for d in shape:
        n *= d
    return -(-n // (PACK_TILE_ROWS * LANES)) * PACK_TILE_ROWS


def _pack(arrs, rows_total):
    buf = jnp.zeros((rows_total, LANES), F32)
    r = 0
    for a in arrs:
        f = a.reshape(-1).astype(F32)
        nr = _item_rows(a.shape)
        block = jnp.pad(f, (0, nr * LANES - f.shape[0])).reshape(nr, LANES)
        buf = lax.dynamic_update_slice(buf, block, (r, 0))
        r += nr
    return buf


def _unpack(buf, shapes):
    out, r = [], 0
    for s in shapes:
        n = 1
        for d in s:
            n *= d
        nr = _item_rows(s)
        out.append(buf[r:r + nr].reshape(-1)[:n].reshape(s))
        r += nr
    return out


def _rows_needed(shapes):
    return -(-sum(_item_rows(s) for s in shapes) // (2 * PACK_TILE_ROWS)) * (2 * PACK_TILE_ROWS)


def _two_rows(a, b):
    out = jnp.zeros((2, a.shape[1]), a.dtype)
    return lax.dynamic_update_slice(lax.dynamic_update_slice(out, a, (0, 0)), b, (1, 0))


def _adam(w, g, m, v):
    m = ADAM_B1 * m + (1.0 - ADAM_B1) * g
    v = ADAM_B2 * v + (1.0 - ADAM_B2) * jnp.square(g)
    m_hat = m / (1.0 - ADAM_B1 ** ADAM_STEP)
    v_hat = v / (1.0 - ADAM_B2 ** ADAM_STEP)
    delta = -ADAM_LR * (m_hat / (jnp.sqrt(v_hat) + ADAM_EPS) + ADAM_WD * w)
    return delta, m, v


def _adam_call(name, w2, g2, m2, v2, tr, pass_grad=False):
    def fn(rv, cv):
        outs = list(_adam(*rv))
        return ([rv[1]] + outs if pass_grad else outs), []

    width = w2.shape[1]
    return _rowcall(name, fn, [(w2, 0, width), (g2, 0, width), (m2, 0, width), (v2, 0, width)], [],
                    [(width, F32)] * (4 if pass_grad else 3), [], tr)


def kernel(x, mem, mem_norm, lb_logits, ffn1_norm, ffn1_w_in, ffn1_w_out, mix_norm, mem_w_kv, hgrn_w_in, hgrn_gnorm, hgrn_w_out, gmlp_w_in, gmlp_ln_g, gmlp_ln_b, gmlp_w_s, gmlp_b_s, gmlp_w_out, ffn2_norm, ffn2_w_in, ffn2_w_out, final_norm, loss_target, m_mem_norm, m_lb_logits, m_ffn1_norm, m_ffn1_w_in, m_ffn1_w_out, m_mix_norm, m_mem_w_kv, m_hgrn_w_in, m_hgrn_gnorm, m_hgrn_w_out, m_gmlp_w_in, m_gmlp_ln_g, m_gmlp_ln_b, m_gmlp_w_s, m_gmlp_b_s, m_gmlp_w_out, m_ffn2_norm, m_ffn2_w_in, m_ffn2_w_out, m_final_norm, v_mem_norm, v_lb_logits, v_ffn1_norm, v_ffn1_w_in, v_ffn1_w_out, v_mix_norm, v_mem_w_kv, v_hgrn_w_in, v_hgrn_gnorm, v_hgrn_w_out, v_gmlp_w_in, v_gmlp_ln_g, v_gmlp_ln_b, v_gmlp_w_s, v_gmlp_b_s, v_gmlp_w_out, v_ffn2_norm, v_ffn2_w_in, v_ffn2_w_out, v_final_norm):
    bl, seq, D = x.shape
    T = bl * seq
    mem_len = mem.shape[1]
    chip = 2 * lax.axis_index("x") + lax.axis_index("y")
    c_arr = lax.axis_index("c").astype(jnp.int32).reshape(1)
    chip_arr = chip.astype(jnp.int32).reshape(1)
    TR = 1024

    big = [("ffn1_w_in", ffn1_w_in, "col"), ("ffn1_w_out", ffn1_w_out, "row"), ("mem_w_kv", mem_w_kv, "col"),
           ("hgrn_w_in", hgrn_w_in, "col"), ("hgrn_w_out", hgrn_w_out, "row"), ("gmlp_w_in", gmlp_w_in, "col"),
           ("gmlp_w_out", gmlp_w_out, "row"), ("ffn2_w_in", ffn2_w_in, "col"), ("ffn2_w_out", ffn2_w_out, "row")]
    kinds = [k for (_, _, k) in big]
    shards_bf = []
    for nm, w, _ in big:
        L, r, c = w.shape
        (wb,) = _rowcall("cast_" + nm, lambda rv, cv: ([rv[0]], []), [(w.reshape(L * r, c), 0, c)], [], [(c, BF)], [], 512)
        shards_bf.append(wb.reshape(L, r, c))
    sb = dict(zip([nm for (nm, _, _) in big], shards_bf))
    groups = [[("ffn1_w_in", 0)], [("ffn1_w_out", 0)], [("hgrn_w_in", None)], [("mem_w_kv", None)], [("hgrn_w_out", None)],
              [("ffn2_w_in", 0), ("ffn2_w_out", 0), ("gmlp_ln_g", None), ("gmlp_ln_b", None)],
              [("ffn1_w_in", 1), ("ffn1_w_out", 1)],
              [("gmlp_w_in", None), ("gmlp_w_out", None)],
              [("ffn2_w_in", 1), ("ffn2_w_out", 1)]]
    kind_of = {nm: k for (nm, _, k) in big}
    for nm, vec in (("gmlp_ln_g", gmlp_ln_g), ("gmlp_ln_b", gmlp_ln_b)):
        sb[nm] = vec.reshape(1, 1, -1)
        kind_of[nm] = "vec"
    gathered = {nm: [None, None] for nm in ("ffn1_w_in", "ffn1_w_out", "ffn2_w_in", "ffn2_w_out")}
    others = {}
    for gi, grp in enumerate(groups):
        outs = _allgather_seq("gather_%d" % gi, [(sb[nm], kind_of[nm], l) for (nm, l) in grp], gi)
        for (nm, l), o in zip(grp, outs):
            others[(nm, l)] = o

    def whole(nm, l, after):
        full = _place_own("own_%s_%d" % (nm, l or 0), others[(nm, l)], sb[nm], "row" if kind_of[nm] == "row" else "col", l,
                          chip_arr, after)
        if l is None:
            gathered[nm] = full
        else:
            gathered[nm][l] = full
        return full

    def rms_fwd(name, xin, g):
        (h,) = _rowcall(name, lambda rv, cv: ([_rmsnorm(rv[0], cv[0])], []), [(xin, 0, D)], [g.reshape(1, D)], [(D, BF)], [], TR)
        return h

    def ffn_fwd(tag, xin, h, nm_in, nm_out, layer, next_gain):
        w_in = whole(nm_in, layer, h)
        dff = w_in.shape[2] // 2
        zg, zu, a = _ffn_in_swiglu("ffn_in_" + tag, h, w_in, 1024, dff // 2)
        out = _mm("ffn_out_" + tag, a, whole(nm_out, layer, a), "nn", F32, 1024, 1024, dff, scale=0.5, res=xin, b_lead=0,
                  norm_gain=None if next_gain is None else next_gain.reshape(1, D))
        xo, h_next = (out, None) if next_gain is None else out
        return xo, h_next, (xin, h, zg, zu, a)

    def ffn_bwd(tag, dxo, saved, g, w_in, w_out, layer):
        xin, h, zg, zu, a = saved
        dff = w_out[layer].shape[1]
        dw_out = _mm_tn_pair("ffn_dwo_" + tag, a, dxo, "row", c_arr, dff // 2, T, scale=0.5)
        dz = _ffn_da_swiglu("ffn_da_" + tag, dxo, w_out[layer], zg, zu, 512)
        dw_in = _mm_tn_pair("ffn_dwi_" + tag, h, dz, "col", c_arr, 512, T)
        dx, dg = _mm_dh_rms("ffn_dh_" + tag, dz, w_in[layer], xin, g.reshape(1, D), dxo, 512)
        return dx, dg, dw_in, dw_out

    def rms_bwd(name, xin, g, dh, dres):
        def fn(rv, cv):
            _, vjp = jax.vjp(_rmsnorm, rv[0], cv[0])
            dx, dg = vjp(rv[1])
            if dres is not None:
                dx = dx + rv[2]
            return [dx], [dg]

        rows = [(xin, 0, D), (dh, 0, D)] + ([(dres, 0, D)] if dres is not None else [])
        dx, dg = _rowcall(name, fn, rows, [g.reshape(1, D)], [(D, F32)], [((1, D), F32)], TR)
        return dx, dg

    x0 = x.reshape(T, D)
    tgt = loss_target.reshape(T, D)
    mem2 = mem.reshape(bl * mem_len, D)
    memn = rms_fwd("rms_mem", mem2, mem_norm)

    h_f10 = rms_fwd("rms_f1l0", x0, ffn1_norm[0])
    x1, h_m0, sv_f10 = ffn_fwd("f1l0", x0, h_f10, "ffn1_w_in", "ffn1_w_out", 0, mix_norm[0])
    z_m0 = _mm("mix_in_0", h_m0, whole("hgrn_w_in", None, h_m0), "nn", F32, 2048, 512, D, b_lead=0)
    w_kv = whole("mem_w_kv", None, z_m0)
    kv = [_mm("kv_%d" % i, memn, w_kv, "nn", F32, 512, 512, D, b_lead=i) for i in range(2)]
    cat0, stash0 = _hgrn_fwd2(z_m0, lb_logits, hgrn_gnorm, kv[0], bl, seq)
    x2, h_f20 = _mm("mix_out_0", cat0, whole("hgrn_w_out", None, cat0), "nn", F32, 1024, 1024, cat0.shape[1], res=x1, b_lead=0,
                    norm_gain=ffn2_norm[0].reshape(1, D))
    x3, h_f11, sv_f20 = ffn_fwd("f2l0", x2, h_f20, "ffn2_w_in", "ffn2_w_out", 0, ffn1_norm[1])
    x4, h_m1, sv_f11 = ffn_fwd("f1l1", x3, h_f11, "ffn1_w_in", "ffn1_w_out", 1, mix_norm[1])
    z_m1 = _mm("mix_in_1", h_m1, whole("gmlp_w_in", None, h_m1), "nn", F32, 2048, 512, D, b_lead=0)
    nc1 = seq // GM_CHUNK
    w_s, b_s = gmlp_w_s[0], gmlp_b_s[0]
    ln_w = GM_GROUPS * GM_GROUP_DIM
    ln_g_full, ln_b_full = [whole(nm, None, z_m1).reshape(1, ln_w) for nm in ("gmlp_ln_g", "gmlp_ln_b")]
    cat1 = _gmlp_fwd(z_m1, ln_g_full, ln_b_full, w_s, b_s, kv[1], bl, nc1)
    x5, h_f21 = _mm("mix_out_1", cat1, whole("gmlp_w_out", None, cat1), "nn", F32, 1024, 1024, cat1.shape[1], res=x4, b_lead=0,
                    norm_gain=ffn2_norm[1].reshape(1, D))
    x6, _, sv_f21 = ffn_fwd("f2l1", x5, h_f21, "ffn2_w_in", "ffn2_w_out", 1, None)

    def head(rv, cv):
        def f(xx, gg):
            err = _rmsnorm(xx, gg) - rv[1]
            return 0.5 * jnp.sum(jnp.mean(err * err, axis=-1, keepdims=True), axis=0, keepdims=True)

        ls, vjp = jax.vjp(f, rv[0], cv[0])
        dx, dg = vjp(jnp.ones((1, 1), F32))
        return [dx], [dg, jnp.broadcast_to(ls, (1, 128))]

    dx6, d_final, loss_part = _rowcall("loss_head", head, [(x6, 0, D), (tgt, 0, D)], [final_norm.reshape(1, D)],
                                       [(D, F32)], [((1, D), F32), ((1, 128), F32)], TR)

    rs_out = {}
    n_gather = len(groups)

    def rs(gi, items):
        outs = _rs_chips_seq("reduce_%d" % gi, [p for (_, p, _) in items], [k for (_, _, k) in items], n_gather + gi)
        for i, (key, _, _) in enumerate(items):
            rs_out[key] = (outs[2 * i], outs[2 * i + 1])

    dx5, dg_f21, dwi_f21, dwo_f21 = ffn_bwd("f2l1", dx6, sv_f21, ffn2_norm[1], gathered["ffn2_w_in"], gathered["ffn2_w_out"], 1)
    rs(0, [(("ffn2_w_out", 1), dwo_f21, "row"), (("ffn2_w_in", 1), dwi_f21, "col")])
    dcat1 = _mm("mix_dcat_1", dx5, gathered["gmlp_w_out"], "nt", F32, 2048, 1024, D, b_lead=0)
    dwo_m1 = _mm_tn_pair("mix_dwo_1", cat1, dx5, "row", c_arr, 1024, T)
    dz_m1, dkv1, d_lng, d_lnb, d_ws, d_bs = _gmlp_bwd(z_m1, dcat1, ln_g_full, ln_b_full, w_s, b_s, kv[1], bl, nc1)
    dx4, dg_m1 = _mm_dh_rms("mix_dh_1", dz_m1, gathered["gmlp_w_in"], x4, mix_norm[1].reshape(1, D), dx5, 512)
    dwi_m1 = _mm_tn_pair("mix_dwi_1", h_m1, dz_m1, "col", c_arr, 1024, T)
    rs(1, [(("gmlp_w_out", 0), dwo_m1, "row"), (("gmlp_w_in", 0), dwi_m1, "col")])
    dx3, dg_f11, dwi_f11, dwo_f11 = ffn_bwd("f1l1", dx4, sv_f11, ffn1_norm[1], gathered["ffn1_w_in"], gathered["ffn1_w_out"], 1)
    rs(2, [(("ffn1_w_out", 1), dwo_f11, "row"), (("ffn1_w_in", 1), dwi_f11, "col")])

    dx2, dg_f20, dwi_f20, dwo_f20 = ffn_bwd("f2l0", dx3, sv_f20, ffn2_norm[0], gathered["ffn2_w_in"], gathered["ffn2_w_out"], 0)
    rs(3, [(("ffn2_w_out", 0), dwo_f20, "row"), (("ffn2_w_in", 0), dwi_f20, "col")])
    dcat0 = _mm("mix_dcat_0", dx2, gathered["hgrn_w_out"], "nt", F32, 2048, 1024, D, b_lead=0)
    dwo_m0 = _mm_tn_pair("mix_dwo_0", cat0, dx2, "row", c_arr, 1024, T)
    dz_m0, dkv0, d_lb, d_gn = _hgrn_bwd2(z_m0, dcat0, stash0, lb_logits, hgrn_gnorm, kv[0], bl, seq)
    dx1, dg_m0 = _mm_dh_rms("mix_dh_0", dz_m0, gathered["hgrn_w_in"], x1, mix_norm[0].reshape(1, D), dx2, 512)
    dwi_m0 = _mm_tn_pair("mix_dwi_0", h_m0, dz_m0, "col", c_arr, 1024, T)
    rs(4, [(("hgrn_w_out", 0), dwo_m0, "row"), (("hgrn_w_in", 0), dwi_m0, "col")])

    dwkv = [_mm_tn_pair("kv_dw_%d" % i, memn, dkv, "col", c_arr, 1024, 512) for i, dkv in enumerate([dkv0, dkv1])]
    rs(5, [(("mem_w_kv", 0), dwkv[0], "col"), (("mem_w_kv", 1), dwkv[1], "col")])
    dmemn = _mm("kv_dx_0", dkv0, gathered["mem_w_kv"], "nt", F32, 512, 512, 1024, b_lead=0)
    dmemn = _mm("kv_dx_1", dkv1, gathered["mem_w_kv"], "nt", F32, 512, 512, 1024, res=dmemn, b_lead=1)
    _, d_memnorm = rms_bwd("rms_bwd_mem", mem2, mem_norm, dmemn, None)

    dx0, dg_f10, dwi_f10, dwo_f10 = ffn_bwd("f1l0", dx1, sv_f10, ffn1_norm[0], gathered["ffn1_w_in"], gathered["ffn1_w_out"], 0)
    rs(6, [(("ffn1_w_out", 0), dwo_f10, "row")])
    rs(7, [(("ffn1_w_in", 0), dwi_f10, "col")])

    shard_grads = [_finish_share("finish_" + nm, [rs_out[(nm, l)][0] for l in range(w.shape[0])],
                                 [rs_out[(nm, l)][1] for l in range(w.shape[0])], k, c_arr) for (nm, w, k) in big]

    big_w = [w for (_, w, _) in big]
    big_m = [m_ffn1_w_in, m_ffn1_w_out, m_mem_w_kv, m_hgrn_w_in, m_hgrn_w_out, m_gmlp_w_in, m_gmlp_w_out, m_ffn2_w_in, m_ffn2_w_out]
    big_v = [v_ffn1_w_in, v_ffn1_w_out, v_mem_w_kv, v_hgrn_w_in, v_hgrn_w_out, v_gmlp_w_in, v_gmlp_w_out, v_ffn2_w_in, v_ffn2_w_out]
    big_out = {}
    for (nm, w, _), g, m, v in zip(big, shard_grads, big_m, big_v):
        L, r, c = w.shape
        g2, d2, m2, v2 = _adam_call("adam_" + nm, w.reshape(L * r, c), g.reshape(L * r, c), m.reshape(L * r, c),
                                    v.reshape(L * r, c), 256, pass_grad=True)
        big_out[nm] = (g2.reshape(w.shape), d2.reshape(w.shape), m2.reshape(w.shape), v2.reshape(w.shape))

    d_ffn1n = _two_rows(dg_f10, dg_f11)
    d_mixn = _two_rows(dg_m0, dg_m1)
    d_ffn2n = _two_rows(dg_f20, dg_f21)
    small_parts = [loss_part[:, :1], d_memnorm, d_lb, d_ffn1n, d_mixn, d_gn, d_lng, d_lnb, d_ws, d_bs, d_ffn2n, d_final]
    red_shapes = [(1,), mem_norm.shape, lb_logits.shape, ffn1_norm.shape, mix_norm.shape, hgrn_gnorm.shape, (1, ln_w), (1, ln_w),
                  gmlp_w_s.shape, gmlp_b_s.shape, ffn2_norm.shape, final_norm.shape]
    red = _small_allreduce(_pack(small_parts, _rows_needed(red_shapes)), "reduce_small")
    (loss_v, g_memn, g_lb, g_f1n, g_mixn, g_gn, g_lng_full, g_lnb_full, g_ws, g_bs, g_f2n, g_fin) = _unpack(red, red_shapes)
    lsh = gmlp_ln_g.shape[1]
    g_lng = lax.dynamic_slice(g_lng_full, (0, chip * lsh), (1, lsh))
    g_lnb = lax.dynamic_slice(g_lnb_full, (0, chip * lsh), (1, lsh))
    small_w = [mem_norm, lb_logits, ffn1_norm, mix_norm, hgrn_gnorm, gmlp_ln_g, gmlp_ln_b, gmlp_w_s, gmlp_b_s, ffn2_norm, final_norm]
    small_g = [g_memn, g_lb, g_f1n, g_mixn, g_gn, g_lng, g_lnb, g_ws, g_bs, g_f2n, g_fin]
    small_m = [m_mem_norm, m_lb_logits, m_ffn1_norm, m_mix_norm, m_hgrn_gnorm, m_gmlp_ln_g, m_gmlp_ln_b, m_gmlp_w_s, m_gmlp_b_s, m_ffn2_norm, m_final_norm]
    small_v = [v_mem_norm, v_lb_logits, v_ffn1_norm, v_mix_norm, v_hgrn_gnorm, v_gmlp_ln_g, v_gmlp_ln_b, v_gmlp_w_s, v_gmlp_b_s, v_ffn2_norm, v_final_norm]
    sshapes = [w.shape for w in small_w]
    nrow = _rows_needed(sshapes)
    d_p, m_p, v_p = _adam_call("adam_small", _pack(small_w, nrow), _pack(small_g, nrow), _pack(small_m, nrow), _pack(small_v, nrow), nrow)
    s_delta, s_m, s_v = _unpack(d_p, sshapes), _unpack(m_p, sshapes), _unpack(v_p, sshapes)
    small_names = ["mem_norm", "lb_logits", "ffn1_norm", "mix_norm", "hgrn_gnorm", "gmlp_ln_g", "gmlp_ln_b", "gmlp_w_s", "gmlp_b_s", "ffn2_norm", "final_norm"]
    small_out = {nm: (g.reshape(w.shape), d, m, v) for nm, w, g, d, m, v in zip(small_names, small_w, small_g, s_delta, s_m, s_v)}

    order = ["mem_norm", "lb_logits", "ffn1_norm", "ffn1_w_in", "ffn1_w_out", "mix_norm", "mem_w_kv", "hgrn_w_in", "hgrn_gnorm",
             "hgrn_w_out", "gmlp_w_in", "gmlp_ln_g", "gmlp_ln_b", "gmlp_w_s", "gmlp_b_s", "gmlp_w_out", "ffn2_norm", "ffn2_w_in",
             "ffn2_w_out", "final_norm"]
    allo = {**big_out, **small_out}
    grad_x = dx0.reshape(x.shape)
    return (loss_v.reshape(()), grad_x, *[allo[n][0] for n in order], *[allo[n][1] for n in order],
            *[allo[n][2] for n in order], *[allo[n][3] for n in order])
```

```python
import functools

import jax
import jax.numpy as jnp
from jax import lax
from jax.experimental import pallas as pl
from jax.experimental.pallas import tpu as pltpu
from jax.experimental.pallas import tpu_sc as plsc

BF = jnp.bfloat16
F32 = jnp.float32
MESH = pl.DeviceIdType.MESH

EPS = 1e-6
D_MODEL = 1024
HG_HEADS = 8
HG_DIM = 128
HG_CHUNK = 64
GM_CHUNK = 128
GM_GROUPS = 8
GM_GROUP_DIM = 256
XA_HEADS = 4
XA_DIM = 256
ADAM_LR = 0.001
ADAM_B1 = 0.9
ADAM_B2 = 0.999
ADAM_EPS = 1e-08
ADAM_WD = 0.01
ADAM_STEP = 10

VMEM_CAP_BYTES = 60 * 1024 * 1024
LANES = 1024


def _pick(n, cap, mult=16):
    if n <= cap:
        return n
    for d in range(cap - cap % mult, 0, -mult):
        if n % d == 0:
            return d
    raise ValueError((n, cap, mult))


def _dg(a, b, ca, cb):
    return lax.dot_general(a.astype(BF), b.astype(BF), (((ca,), (cb,)), ((), ())), preferred_element_type=F32)


@jax.custom_vjp
def dot_nn(a, b):
    return _dg(a, b, 1, 0)


def _nn_fwd(a, b):
    return _dg(a, b, 1, 0), (a, b)


def _nn_bwd(r, g):
    a, b = r
    return _dg(g, b, 1, 1), _dg(a, g, 0, 0)


dot_nn.defvjp(_nn_fwd, _nn_bwd)


@jax.custom_vjp
def dot_nt(a, b):
    return _dg(a, b, 1, 1)


def _nt_fwd(a, b):
    return _dg(a, b, 1, 1), (a, b)


def _nt_bwd(r, g):
    a, b = r
    return _dg(g, b, 1, 0), _dg(g, a, 0, 0)


dot_nt.defvjp(_nt_fwd, _nt_bwd)


@jax.custom_vjp
def dot_tn(a, b):
    return _dg(a, b, 0, 0)


def _tn_fwd(a, b):
    return _dg(a, b, 0, 0), (a, b)


def _tn_bwd(r, g):
    a, b = r
    return _dg(b, g, 1, 1), _dg(a, g, 1, 0)


dot_tn.defvjp(_tn_fwd, _tn_bwd)


def _rmsnorm(x, g):
    return x * lax.rsqrt(jnp.mean(x * x, axis=-1, keepdims=True) + EPS) * g


def _silu(x):
    return x * jax.nn.sigmoid(x)


@jax.custom_vjp
def _gelu(x):
    return 0.5 * x * (1.0 + lax.erf(x * (0.5 ** 0.5)))


def _gelu_fwd(x):
    return _gelu(x), x


def _gelu_bwd(x, g):
    t = x * (0.5 ** 0.5)
    cdf = 0.5 * (1.0 + lax.erf(t))
    return (g * (cdf + x * (jnp.exp(-(t * t)) * (0.5 / 3.141592653589793) ** 0.5)),)


_gelu.defvjp(_gelu_fwd, _gelu_bwd)


def _softmax_last(s):
    m = lax.stop_gradient(jnp.max(s, axis=-1, keepdims=True))
    e = jnp.exp(s - m)
    return e / jnp.sum(e, axis=-1, keepdims=True)


def _tril(n):
    r = lax.broadcasted_iota(jnp.int32, (n, n), 0)
    c = lax.broadcasted_iota(jnp.int32, (n, n), 1)
    return r >= c


def _attention(zx, mk, mv):
    s = dot_nt(zx, mk) * (XA_DIM ** -0.5)
    return dot_nn(_softmax_last(s), mv)


def _chunk_sums(x, suffix):
    n = x.shape[0]
    r = lax.broadcasted_iota(jnp.int32, (n, n), 0)
    c = lax.broadcasted_iota(jnp.int32, (n, n), 1)
    tri = jnp.logical_and(r <= c if suffix else r >= c, r // HG_CHUNK == c // HG_CHUNK).astype(BF)
    hi = x.astype(BF)
    rest = x - hi.astype(F32)
    mid = rest.astype(BF)
    lo = (rest - mid.astype(F32)).astype(BF)
    return (_dg(tri, hi, 1, 0) + _dg(tri, mid, 1, 0)) + _dg(tri, lo, 1, 0)


@jax.custom_vjp
def _running_sums(x):
    return _chunk_sums(x, False)


_running_sums.defvjp(lambda x: (_chunk_sums(x, False), None), lambda _, g: (_chunk_sums(g, True),))


def _hgrn_decays(zf, lb3):
    l0, l1, l2 = lb3[0:1], lb3[1:2], lb3[2:3]
    m = lax.stop_gradient(jnp.maximum(jnp.maximum(l0, l1), l2))
    e0 = jnp.exp(l0 - m)
    lb = e0 / (e0 + jnp.exp(l1 - m) + jnp.exp(l2 - m))
    f = lb + (1.0 - lb) * jax.nn.sigmoid(zf)
    return f, _running_sums(jnp.log(f))


def _hgrn_head(zq, f, b, zi, zg, gn, S):
    q = _silu(zq)
    k = 1.0 - f
    b_last = b[HG_CHUNK - 1:HG_CHUNK, :]
    q_dec = q * jnp.exp(b)
    k_inv = k * jnp.exp(-b)
    a = jnp.where(_tril(HG_CHUNK), dot_nt(q_dec, k_inv), 0.0)
    o = dot_nn(a, zi) + dot_nn(q_dec, S)
    S_new = jnp.exp(b_last).reshape(HG_DIM, 1) * S + dot_tn(k * jnp.exp(b_last - b), zi)
    o = _rmsnorm(o, gn) * _silu(zg)
    return o, S_new


def _gmlp_block(zu, zv, zx, lng, lnb, ws, bs, mk, mv):
    gv = [_gelu(v) for v in zv]
    width = GM_GROUPS * GM_GROUP_DIM
    mu = sum(jnp.sum(g, axis=-1, keepdims=True) for g in gv) / width
    xc = [g - mu for g in gv]
    var = sum(jnp.sum(c * c, axis=-1, keepdims=True) for c in xc) / width
    r = lax.rsqrt(var + EPS)
    outs = []
    for g in range(GM_GROUPS):
        v = xc[g] * r * lng[g] + lnb[g]
        w = jnp.where(_tril(GM_CHUNK), ws[g], 0.0)
        mixed = dot_nn(w, v) + bs[g].reshape(GM_CHUNK, 1)
        outs.append(_gelu(zu[g]) * mixed)
    for a in range(XA_HEADS):
        outs.append(_attention(zx[a], mk[a], mv[a]))
    return outs


def _rowcall(name, fn, rows, consts, row_outs, acc_outs, tr):
    nrows = rows[0][0].shape[0]
    tr = _pick(nrows, tr)
    n_r, n_c, n_ro, n_ao = len(rows), len(consts), len(row_outs), len(acc_outs)

    def kern(*refs):
        rv = [r[...] for r in refs[:n_r]]
        cv = [r[...] for r in refs[n_r:n_r + n_c]]
        ro_refs = refs[n_r + n_c:n_r + n_c + n_ro]
        ao_refs = refs[n_r + n_c + n_ro:]
        ro, ao = fn(rv, cv)
        for ref, v in zip(ro_refs, ro):
            ref[...] = v.astype(ref.dtype)
        if n_ao:
            @pl.when(pl.program_id(0) == 0)
            def _():
                for ref in ao_refs:
                    ref[...] = jnp.zeros(ref.shape, ref.dtype)

            for ref, v in zip(ao_refs, ao):
                ref[...] += v.astype(ref.dtype)

    in_specs = [pl.BlockSpec((tr, w), functools.partial(lambda i, cb: (i, cb), cb=cb)) for (_, cb, w) in rows]
    in_specs += [pl.BlockSpec(c.shape, lambda i: (0, 0)) for c in consts]
    out_specs = [pl.BlockSpec((tr, w), lambda i: (i, 0)) for (w, _) in row_outs]
    out_specs += [pl.BlockSpec(s, lambda i: (0, 0)) for (s, _) in acc_outs]
    out_shape = [jax.ShapeDtypeStruct((nrows, w), dt) for (w, dt) in row_outs]
    out_shape += [jax.ShapeDtypeStruct(s, dt) for (s, dt) in acc_outs]
    outs = pl.pallas_call(
        kern, grid=(nrows // tr,), in_specs=in_specs, out_specs=out_specs, out_shape=out_shape, name=name,
        compiler_params=pltpu.CompilerParams(dimension_semantics=("arbitrary",),
                                             vmem_limit_bytes=VMEM_CAP_BYTES),
    )(*[a for (a, _, _) in rows], *consts)
    return outs


def _mm(name, a, b, mode, out_dtype, tm, tn, tk, scale=1.0, res=None, a_lead=None, b_lead=None, norm_gain=None):
    ash = a.shape[-2:]
    bsh = b.shape[-2:]
    if mode == "nn":
        (M, K), (K2, N) = ash, bsh
    elif mode == "nt":
        (M, K), (N, K2) = ash, bsh
    else:
        (K, M), (K2, N) = ash, bsh
    assert K == K2, (name, a.shape, b.shape)
    tm, tn, tk = min(tm, M), min(tn, N), min(tk, K)
    assert M % tm == 0 and N % tn == 0 and K % tk == 0, (name, M, N, K, tm, tn, tk)
    nk = K // tk
    dims = {"nn": (1, 0), "nt": (1, 1), "tn": (0, 0)}[mode]

    def lead(spec_shape, index_fn, lead_idx):
        if lead_idx is None:
            return pl.BlockSpec(spec_shape, index_fn)
        return pl.BlockSpec((None,) + spec_shape, lambda i, j, k: (lead_idx,) + index_fn(i, j, k))

    if mode == "tn":
        a_spec = lead((tk, tm), lambda i, j, k: (k, i), a_lead)
    else:
        a_spec = lead((tm, tk), lambda i, j, k: (i, k), a_lead)
    if mode == "nt":
        b_spec = lead((tn, tk), lambda i, j, k: (j, k), b_lead)
    else:
        b_spec = lead((tk, tn), lambda i, j, k: (k, j), b_lead)
    o_spec = pl.BlockSpec((tm, tn), lambda i, j, k: (i, j))
    has_res = res is not None
    has_norm = norm_gain is not None
    assert not has_norm or tn == N

    def kern(*refs):
        a_ref, b_ref = refs[0], refs[1]
        pos = 2
        res_ref = gain_ref = h_ref = None
        if has_res:
            res_ref, pos = refs[pos], pos + 1
        if has_norm:
            gain_ref, pos = refs[pos], pos + 1
        o_ref, pos = refs[pos], pos + 1
        if has_norm:
            h_ref = refs[pos]
        acc_ref = refs[-1] if nk > 1 else None
        p = lax.dot_general(a_ref[...].astype(BF), b_ref[...].astype(BF), (((dims[0],), (dims[1],)), ((), ())),
                            preferred_element_type=F32)

        def finish(v):
            if scale != 1.0:
                v = v * scale
            if has_res:
                v = res_ref[...] + v
            o_ref[...] = v.astype(o_ref.dtype)
            if has_norm:
                h_ref[...] = _rmsnorm(v, gain_ref[...]).astype(h_ref.dtype)

        if nk == 1:
            finish(p)
        else:
            k = pl.program_id(2)

            @pl.when(k == 0)
            def _():
                acc_ref[...] = p

            @pl.when(k > 0)
            def _():
                acc_ref[...] += p

            @pl.when(k == nk - 1)
            def _():
                finish(acc_ref[...])

    ins = [a, b] + ([res] if has_res else []) + ([norm_gain] if has_norm else [])
    in_specs = [a_spec, b_spec] + ([o_spec] if has_res else [])
    in_specs += [pl.BlockSpec((1, N), lambda i, j, k: (0, 0))] if has_norm else []
    out_sd = jax.ShapeDtypeStruct((M, N), out_dtype)
    return pl.pallas_call(
        kern, grid=(M // tm, N // tn, nk), in_specs=in_specs,
        out_specs=[o_spec, o_spec] if has_norm else o_spec,
        out_shape=[out_sd, jax.ShapeDtypeStruct((M, N), BF)] if has_norm else out_sd,
        scratch_shapes=[pltpu.VMEM((tm, tn), F32)] if nk > 1 else [],
        name=name,
        compiler_params=pltpu.CompilerParams(dimension_semantics=("parallel", "parallel", "arbitrary"),
                                             vmem_limit_bytes=VMEM_CAP_BYTES),
    )(*ins)


def _ffn_in_swiglu(name, h, w3, tm, tn):
    T, D = h.shape
    dff = w3.shape[2] // 2
    tm = min(tm, T)
    assert T % tm == 0 and dff % tn == 0
    nj = dff // tn

    def kern(h_ref, wg_ref, wu_ref, zg_ref, zu_ref, a_ref):
        hb = h_ref[...]
        g = jnp.dot(hb, wg_ref[...], preferred_element_type=F32).astype(BF)
        u = jnp.dot(hb, wu_ref[...], preferred_element_type=F32).astype(BF)
        zg_ref[...] = g
        zu_ref[...] = u
        a_ref[...] = (_silu(g.astype(F32)) * u.astype(F32)).astype(BF)

    o_spec = pl.BlockSpec((tm, tn), lambda i, j: (i, j))
    return pl.pallas_call(
        kern, grid=(T // tm, nj),
        in_specs=[pl.BlockSpec((tm, D), lambda i, j: (i, 0)),
                  pl.BlockSpec((None, D, tn), lambda i, j: (0, 0, j)),
                  pl.BlockSpec((None, D, tn), lambda i, j: (0, 0, j + nj))],
        out_specs=[o_spec, o_spec, o_spec],
        out_shape=[jax.ShapeDtypeStruct((T, dff), BF)] * 3, name=name,
        compiler_params=pltpu.CompilerParams(dimension_semantics=("parallel", "arbitrary"),
                                             vmem_limit_bytes=VMEM_CAP_BYTES),
    )(h, w3, w3)


def _ffn_da_swiglu(name, dxo, w3, zg, zu, tm):
    T, D = dxo.shape
    dff = w3.shape[1]
    tm = min(tm, T)
    assert T % tm == 0 and dff % 2 == 0
    hc = dff // 2

    def kern(d_ref, w_ref, g_ref, u_ref, dz_ref):
        db = (d_ref[...] * 0.5).astype(BF)
        for s in range(2):
            cols = slice(s * hc, (s + 1) * hc)
            da = lax.dot_general(db, w_ref[cols, :], (((1,), (1,)), ((), ())), preferred_element_type=F32)
            g = g_ref[:, cols].astype(F32)
            sg = 1.0 / (1.0 + jnp.exp(-g))
            gs = g * sg
            dab = da.astype(BF)
            dz_ref[:, cols] = (dab * u_ref[:, cols]) * (sg + gs * (1.0 - sg)).astype(BF)
            dz_ref[:, dff + s * hc:dff + (s + 1) * hc] = dab * gs.astype(BF)

    row = lambda w: pl.BlockSpec((tm, w), lambda i: (i, 0))
    return pl.pallas_call(
        kern, grid=(T // tm,),
        in_specs=[row(D), pl.BlockSpec((None, dff, D), lambda i: (0, 0, 0), pipeline_mode=pl.Buffered(1)), row(dff), row(dff)],
        out_specs=row(2 * dff), out_shape=jax.ShapeDtypeStruct((T, 2 * dff), BF), name=name,
        compiler_params=pltpu.CompilerParams(dimension_semantics=("arbitrary",), vmem_limit_bytes=VMEM_CAP_BYTES),
    )(dxo, w3, zg, zu)


def _mm_dh_rms(name, dz, w3, xin, g, dres, tm):
    T, K = dz.shape
    D = w3.shape[1]
    tm = min(tm, T)
    assert T % tm == 0

    def kern(dz_ref, w_ref, x_ref, g_ref, r_ref, dx_ref, dg_ref):
        dh = lax.dot_general(dz_ref[...], w_ref[...], (((1,), (1,)), ((), ())), preferred_element_type=F32)
        _, vjp = jax.vjp(_rmsnorm, x_ref[...], g_ref[...])
        dx, dg = vjp(dh)
        dx_ref[...] = dx + r_ref[...]

        @pl.when(pl.program_id(0) == 0)
        def _():
            dg_ref[...] = jnp.zeros(dg_ref.shape, F32)

        dg_ref[...] += dg

    row = lambda w: pl.BlockSpec((tm, w), lambda i: (i, 0))
    one = pl.BlockSpec((1, D), lambda i: (0, 0))
    return pl.pallas_call(
        kern, grid=(T // tm,),
        in_specs=[row(K), pl.BlockSpec((None, D, K), lambda i: (0, 0, 0), pipeline_mode=pl.Buffered(1)), row(D), one, row(D)],
        out_specs=[row(D), one], out_shape=[jax.ShapeDtypeStruct((T, D), F32), jax.ShapeDtypeStruct((1, D), F32)], name=name,
        compiler_params=pltpu.CompilerParams(dimension_semantics=("arbitrary",), vmem_limit_bytes=VMEM_CAP_BYTES),
    )(dz, w3, xin, g, dres)


def _mm_tn_pair(name, a, b, kind, c_arr, tq, tk, scale=1.0):
    T, M = a.shape
    _, N = b.shape
    tk = min(tk, T)
    assert T % tk == 0
    nk = T // tk
    if kind == "col":
        hm = M // 2
        assert N % tq == 0
        nq = N // tq
        tile = (hm, tq)
        a_spec = pl.BlockSpec((tk, hm), lambda h, q, k, c: (k, jnp.bitwise_xor(h, 1 - c[0])))
        b_spec = pl.BlockSpec((tk, tq), lambda h, q, k, c: (k, q))
        o_spec = pl.BlockSpec(tile, lambda h, q, k, c: (0, q * h))
        out_sd = (hm, N)
    else:
        hn = N // 2
        assert M % tq == 0
        nq = M // tq
        tile = (tq, hn)
        a_spec = pl.BlockSpec((tk, tq), lambda h, q, k, c: (k, q))
        b_spec = pl.BlockSpec((tk, hn), lambda h, q, k, c: (k, jnp.bitwise_xor(h, 1 - c[0])))
        o_spec = pl.BlockSpec(tile, lambda h, q, k, c: (q * h, 0))
        out_sd = (M, hn)

    def kern(c_ref, a_ref, b_ref, o_ref, acc, stage, recv, ssem, rsem):
        h, q, k = pl.program_id(0), pl.program_id(1), pl.program_id(2)
        x, y, c, _ = _place()
        p = lax.dot_general(a_ref[...].astype(BF), b_ref[...].astype(BF), (((0,), (0,)), ((), ())), preferred_element_type=F32)

        @pl.when(k == 0)
        def _():
            acc[...] = p

        @pl.when(k > 0)
        def _():
            acc[...] += p

        def send(slot, qq):
            return pltpu.make_async_remote_copy(src_ref=stage.at[slot], dst_ref=recv.at[qq], send_sem=ssem.at[slot],
                                                recv_sem=rsem.at[qq], device_id=(x, y, 1 - c), device_id_type=MESH)

        last = k == nk - 1

        @pl.when(jnp.logical_and(last, h == 0))
        def _():
            slot = q % 2

            @pl.when(q >= 2)
            def _():
                send(slot, q).wait_send()

            stage[slot] = (acc[...] * scale).astype(BF)
            send(slot, q).start()

        @pl.when(jnp.logical_and(last, h == 1))
        def _():
            @pl.when(q == 0)
            def _():
                for s in range(min(nq, 2)):
                    send(s, 0).wait_send()

            send(0, q).wait_recv()
            o_ref[...] = (acc[...] * scale + recv[q].astype(F32)).astype(o_ref.dtype)

    return pl.pallas_call(
        kern,
        grid_spec=pltpu.PrefetchScalarGridSpec(
            num_scalar_prefetch=1, grid=(2, nq, nk), in_specs=[a_spec, b_spec], out_specs=o_spec,
            scratch_shapes=[pltpu.VMEM(tile, F32), pltpu.VMEM((2,) + tile, BF), pltpu.VMEM((nq,) + tile, BF),
                            pltpu.SemaphoreType.DMA((2,)), pltpu.SemaphoreType.DMA((nq,))]),
        out_shape=jax.ShapeDtypeStruct(out_sd, BF), name=name,
        compiler_params=pltpu.CompilerParams(dimension_semantics=("arbitrary", "arbitrary", "arbitrary"),
                                             vmem_limit_bytes=VMEM_CAP_BYTES),
    )(c_arr, a, b)


def _kv_pieces(kv_ref):
    W = XA_HEADS * XA_DIM
    mk = [kv_ref[:, a * XA_DIM:(a + 1) * XA_DIM] for a in range(XA_HEADS)]
    mv = [kv_ref[:, W + a * XA_DIM:W + (a + 1) * XA_DIM] for a in range(XA_HEADS)]
    return mk, mv


HG_SUB = 4


def _hgrn_rows(z_ref):
    W = HG_HEADS * HG_DIM

    def piece(c, col, w):
        return z_ref[c * HG_CHUNK:(c + 1) * HG_CHUNK, col:col + w]

    zq = [[piece(c, h * HG_DIM, HG_DIM) for h in range(HG_HEADS)] for c in range(HG_SUB)]
    zf = z_ref[:, W:2 * W]
    zi =[[piece(c, 2 * W + h * HG_DIM, HG_DIM) for h in range(HG_HEADS)] for c in range(HG_SUB)]
    zg = [[piece(c, 3 * W + h * HG_DIM, HG_DIM) for h in range(HG_HEADS)] for c in range(HG_SUB)]
    zx = [z_ref[:, 4 * W + a * XA_DIM:4 * W + (a + 1) * XA_DIM] for a in range(XA_HEADS)]
    return zq, zf, zi, zg, zx


def _hgrn_steps(zq, zf, zi, zg, zx, lb3, gn, mk, mv, S):
    f, b = _hgrn_decays(zf, lb3)
    mix = []
    for c in range(HG_SUB):
        row, s_next = [], []
        rows = slice(c * HG_CHUNK, (c + 1) * HG_CHUNK)
        for h in range(HG_HEADS):
            cols = slice(h * HG_DIM, (h + 1) * HG_DIM)
            o, sn = _hgrn_head(zq[c][h], f[rows, cols], b[rows, cols], zi[c][h], zg[c][h], gn, S[h])
            row.append(o)
            s_next.append(sn)
        mix.append(row)
        S = s_next
    att = [_attention(zx[a], mk[a], mv[a]) for a in range(XA_HEADS)]
    return mix, att, S


def _hgrn_fwd2(z, lb_logits, gnorm, kv, bl, seq):
    T, zw = z.shape
    mem_len = kv.shape[0] // bl
    cat_w = HG_HEADS * HG_DIM + XA_HEADS * XA_DIM
    R = HG_SUB * HG_CHUNK
    nb = seq // R

    def kern(z_ref, lb_ref, gn_ref, kv_ref, cat_ref, st_ref, s_scr):
        @pl.when(pl.program_id(1) == 0)
        def _():
            s_scr[...] = jnp.zeros(s_scr.shape, F32)

        st_ref[...] = s_scr[...]
        zq, zf, zi, zg, zx = _hgrn_rows(z_ref)
        mk, mv = _kv_pieces(kv_ref)
        S = [s_scr[h] for h in range(HG_HEADS)]
        mix, att, s_new = _hgrn_steps(zq, zf, zi, zg, zx, lb_ref[...], gn_ref[...], mk, mv, S)
        for c in range(HG_SUB):
            for h in range(HG_HEADS):
                cat_ref[c * HG_CHUNK:(c + 1) * HG_CHUNK, h * HG_DIM:(h + 1) * HG_DIM] = mix[c][h].astype(cat_ref.dtype)
        for h in range(HG_HEADS):
            s_scr[h] = s_new[h]
        base = HG_HEADS * HG_DIM
        for a in range(XA_HEADS):
            cat_ref[:, base + a * XA_DIM:base + (a + 1) * XA_DIM] = att[a].astype(cat_ref.dtype)

    return pl.pallas_call(
        kern, grid=(bl, nb),
        in_specs=[pl.BlockSpec((R, zw), lambda b, n: (b * nb + n, 0)),
                  pl.BlockSpec(lb_logits.shape, lambda b, n: (0, 0)),
                  pl.BlockSpec(gnorm.shape, lambda b, n: (0, 0)),
                  pl.BlockSpec((mem_len, kv.shape[1]), lambda b, n: (b, 0))],
        out_specs=[pl.BlockSpec((R, cat_w), lambda b, n: (b * nb + n, 0)),
                   pl.BlockSpec((None, HG_HEADS, HG_DIM, HG_DIM), lambda b, n: (b * nb + n, 0, 0, 0))],
        out_shape=[jax.ShapeDtypeStruct((T, cat_w), BF),
                   jax.ShapeDtypeStruct((bl * nb, HG_HEADS, HG_DIM, HG_DIM), F32)],
        scratch_shapes=[pltpu.VMEM((HG_HEADS, HG_DIM, HG_DIM), F32)],
        name="hgrn_fwd",
        compiler_params=pltpu.CompilerParams(dimension_semantics=("arbitrary", "arbitrary"), vmem_limit_bytes=VMEM_CAP_BYTES),
    )(z, lb_logits, gnorm, kv)


def _hgrn_bwd2(z, dcat, stash, lb_logits, gnorm, kv, bl, seq):
    T, zw = z.shape
    mem_len = kv.shape[0] // bl
    cat_w = dcat.shape[1]
    R = HG_SUB * HG_CHUNK
    nb = seq // R

    def kern(z_ref, dc_ref, st_ref, lb_ref, gn_ref, kv_ref, dz_ref, dkv_ref, dlb_ref, dgn_ref, ds_scr):
        first = jnp.logical_and(pl.program_id(0) == 0, pl.program_id(1) == 0)

        @pl.when(pl.program_id(1) == 0)
        def _():
            ds_scr[...] = jnp.zeros(ds_scr.shape, F32)
            dkv_ref[...] = jnp.zeros(dkv_ref.shape, F32)

        @pl.when(first)
        def _():
            dlb_ref[...] = jnp.zeros(dlb_ref.shape, F32)
            dgn_ref[...] = jnp.zeros(dgn_ref.shape, F32)

        zq, zf, zi, zg, zx = _hgrn_rows(z_ref)
        mk, mv = _kv_pieces(kv_ref)
        S = [st_ref[h] for h in range(HG_HEADS)]
        _, vjp = jax.vjp(_hgrn_steps, zq, zf, zi, zg, zx, lb_ref[...], gn_ref[...], mk, mv, S)
        d_mix = [[dc_ref[c * HG_CHUNK:(c + 1) * HG_CHUNK, h * HG_DIM:(h + 1) * HG_DIM] for h in range(HG_HEADS)]
                 for c in range(HG_SUB)]
        base = HG_HEADS * HG_DIM
        d_att = [dc_ref[:, base + a * XA_DIM:base + (a + 1) * XA_DIM] for a in range(XA_HEADS)]
        d_s = [ds_scr[h] for h in range(HG_HEADS)]
        dzq, dzf, dzi, dzg, dzx, dlb3, dgn, dmk, dmv, dS = vjp((d_mix, d_att, d_s))
        W = HG_HEADS * HG_DIM
        dz_ref[:, W:2 * W] = dzf.astype(dz_ref.dtype)
        for c in range(HG_SUB):
            rows = slice(c * HG_CHUNK, (c + 1) * HG_CHUNK)
            for h in range(HG_HEADS):
                for k, part in ((0, dzq), (2, dzi), (3, dzg)):
                    dz_ref[rows, k * W + h * HG_DIM:k * W + (h + 1) * HG_DIM] = part[c][h].astype(dz_ref.dtype)
        for h in range(HG_HEADS):
            ds_scr[h] = dS[h]
        dlb_ref[...] += dlb3
        dgn_ref[...] += dgn
        KW = XA_HEADS * XA_DIM
        for a in range(XA_HEADS):
            dz_ref[:, 4 * W + a * XA_DIM:4 * W + (a + 1) * XA_DIM] = dzx[a].astype(dz_ref.dtype)
            dkv_ref[:, a * XA_DIM:(a + 1) * XA_DIM] += dmk[a]
            dkv_ref[:, KW + a * XA_DIM:KW + (a + 1) * XA_DIM] += dmv[a]

    rev = lambda b, n: (b * nb + (nb - 1 - n), 0)
    return pl.pallas_call(
        kern, grid=(bl, nb),
        in_specs=[pl.BlockSpec((R, zw), rev),
                  pl.BlockSpec((R, cat_w), rev),
                  pl.BlockSpec((None, HG_HEADS, HG_DIM, HG_DIM), lambda b, n: (b * nb + (nb - 1 - n), 0, 0, 0)),
                  pl.BlockSpec(lb_logits.shape, lambda b, n: (0, 0)),
                  pl.BlockSpec(gnorm.shape, lambda b, n: (0, 0)),
                  pl.BlockSpec((mem_len, kv.shape[1]), lambda b, n: (b, 0))],
        out_specs=[pl.BlockSpec((R, zw), rev),
                   pl.BlockSpec((mem_len, kv.shape[1]), lambda b, n: (b, 0)),
                   pl.BlockSpec(lb_logits.shape, lambda b, n: (0, 0)),
                   pl.BlockSpec(gnorm.shape, lambda b, n: (0, 0))],
        out_shape=[jax.ShapeDtypeStruct((T, zw), BF), jax.ShapeDtypeStruct(kv.shape, F32),
                   jax.ShapeDtypeStruct(lb_logits.shape, F32), jax.ShapeDtypeStruct(gnorm.shape, F32)],
        scratch_shapes=[pltpu.VMEM((HG_HEADS, HG_DIM, HG_DIM), F32)],
        name="hgrn_bwd",
        compiler_params=pltpu.CompilerParams(dimension_semantics=("arbitrary", "arbitrary"), vmem_limit_bytes=VMEM_CAP_BYTES),
    )(z, dcat, stash, lb_logits, gnorm, kv)


GM_SUB = 2


def _gmlp_pieces(z_ref):
    W = GM_GROUPS * GM_GROUP_DIM
    zu = [z_ref[:, g * GM_GROUP_DIM:(g + 1) * GM_GROUP_DIM] for g in range(GM_GROUPS)]
    zv = [z_ref[:, W + g * GM_GROUP_DIM:W + (g + 1) * GM_GROUP_DIM] for g in range(GM_GROUPS)]
    zx = [z_ref[:, 2 * W + a * XA_DIM:2 * W + (a + 1) * XA_DIM] for a in range(XA_HEADS)]
    return zu, zv, zx


def _gmlp_params(lng_ref, lnb_ref, ws_ref, bs_ref):
    lng = [lng_ref[:, g * GM_GROUP_DIM:(g + 1) * GM_GROUP_DIM] for g in range(GM_GROUPS)]
    lnb = [lnb_ref[:, g * GM_GROUP_DIM:(g + 1) * GM_GROUP_DIM] for g in range(GM_GROUPS)]
    ws = [ws_ref[g] for g in range(GM_GROUPS)]
    bs = [bs_ref[g:g + 1, :] for g in range(GM_GROUPS)]
    return lng, lnb, ws, bs


def _gmlp_fwd(z, ln_g, ln_b, w_s, b_s, kv, bl, nc):
    T, zw = z.shape
    mem_len = kv.shape[0] // bl
    cat_w = GM_GROUPS * GM_GROUP_DIM + XA_HEADS * XA_DIM

    assert nc % GM_SUB == 0
    nc = nc // GM_SUB
    R = GM_SUB * GM_CHUNK

    def kern(z_ref, lng_ref, lnb_ref, ws_ref, bs_ref, kv_ref, cat_ref):
        lng, lnb, ws, bs = _gmlp_params(lng_ref, lnb_ref, ws_ref, bs_ref)
        mk, mv = _kv_pieces(kv_ref)
        for c in range(GM_SUB):
            rows = pl.ds(c * GM_CHUNK, GM_CHUNK)
            zu, zv, zx = _gmlp_pieces(z_ref.at[rows])
            out = cat_ref.at[rows]
            outs = _gmlp_block(zu, zv, zx, lng, lnb, ws, bs, mk, mv)
            for g in range(GM_GROUPS):
                out[:, g * GM_GROUP_DIM:(g + 1) * GM_GROUP_DIM] = outs[g].astype(cat_ref.dtype)
            base = GM_GROUPS * GM_GROUP_DIM
            for a in range(XA_HEADS):
                out[:, base + a * XA_DIM:base + (a + 1) * XA_DIM] = outs[GM_GROUPS + a].astype(cat_ref.dtype)

    full2 = lambda b, n: (0, 0)
    return pl.pallas_call(
        kern, grid=(bl, nc),
        in_specs=[pl.BlockSpec((R, zw), lambda b, n: (b * nc + n, 0)),
                  pl.BlockSpec(ln_g.shape, full2), pl.BlockSpec(ln_b.shape, full2),
                  pl.BlockSpec(w_s.shape, lambda b, n: (0, 0, 0)), pl.BlockSpec(b_s.shape, full2),
                  pl.BlockSpec((mem_len, kv.shape[1]), lambda b, n: (b, 0))],
        out_specs=pl.BlockSpec((R, cat_w), lambda b, n: (b * nc + n, 0)),
        out_shape=jax.ShapeDtypeStruct((T, cat_w), BF),
        name="gmlp_fwd",
        compiler_params=pltpu.CompilerParams(dimension_semantics=("arbitrary", "arbitrary"), vmem_limit_bytes=VMEM_CAP_BYTES),
    )(z, ln_g, ln_b, w_s, b_s, kv)


def _gmlp_bwd(z, dcat, ln_g, ln_b, w_s, b_s, kv, bl, nc):
    T, zw = z.shape
    mem_len = kv.shape[0] // bl
    cat_w = dcat.shape[1]
    assert nc % GM_SUB == 0
    nc = nc // GM_SUB

    def kern(z_ref, dc_ref, lng_ref, lnb_ref, ws_ref, bs_ref, kv_ref,
             dz_ref, dkv_ref, dlng_ref, dlnb_ref, dws_ref, dbs_ref):
        first = jnp.logical_and(pl.program_id(0) == 0, pl.program_id(1) == 0)

        @pl.when(pl.program_id(1) == 0)
        def _():
            dkv_ref[...] = jnp.zeros(dkv_ref.shape, F32)

        @pl.when(first)
        def _():
            dlng_ref[...] = jnp.zeros(dlng_ref.shape, F32)
            dlnb_ref[...] = jnp.zeros(dlnb_ref.shape, F32)
            dws_ref[...] = jnp.zeros(dws_ref.shape, F32)
            dbs_ref[...] = jnp.zeros(dbs_ref.shape, F32)

        lng, lnb, ws, bs = _gmlp_params(lng_ref, lnb_ref, ws_ref, bs_ref)
        mk, mv = _kv_pieces(kv_ref)
        W = GM_GROUPS * GM_GROUP_DIM
        KW = XA_HEADS * XA_DIM
        for c in range(GM_SUB):
            rows = pl.ds(c * GM_CHUNK, GM_CHUNK)
            zu, zv, zx = _gmlp_pieces(z_ref.at[rows])
            dc, dz = dc_ref.at[rows], dz_ref.at[rows]
            _, vjp = jax.vjp(_gmlp_block, zu, zv, zx, lng, lnb, ws, bs, mk, mv)
            d_outs = [dc[:, g * GM_GROUP_DIM:(g + 1) * GM_GROUP_DIM] for g in range(GM_GROUPS)]
            d_outs += [dc[:, W + a * XA_DIM:W + (a + 1) * XA_DIM] for a in range(XA_HEADS)]
            dzu, dzv, dzx, dlng, dlnb, dws, dbs, dmk, dmv = vjp(d_outs)
            for g in range(GM_GROUPS):
                sl = slice(g * GM_GROUP_DIM, (g + 1) * GM_GROUP_DIM)
                dz[:, sl] = dzu[g].astype(dz_ref.dtype)
                dz[:, W + g * GM_GROUP_DIM:W + (g + 1) * GM_GROUP_DIM] = dzv[g].astype(dz_ref.dtype)
                dlng_ref[:, sl] += dlng[g]
                dlnb_ref[:, sl] += dlnb[g]
                dws_ref[g] += dws[g]
                dbs_ref[g:g + 1, :] += dbs[g]
            for a in range(XA_HEADS):
                dz[:, 2 * W + a * XA_DIM:2 * W + (a + 1) * XA_DIM] = dzx[a].astype(dz_ref.dtype)
                dkv_ref[:, a * XA_DIM:(a + 1) * XA_DIM] += dmk[a]
                dkv_ref[:, KW + a * XA_DIM:KW + (a + 1) * XA_DIM] += dmv[a]

    full2 = lambda b, n: (0, 0)
    full3 = lambda b, n: (0, 0, 0)
    blk = lambda b, n: (b * nc + n, 0)
    return pl.pallas_call(
        kern, grid=(bl, nc),
        in_specs=[pl.BlockSpec((GM_SUB * GM_CHUNK, zw), blk), pl.BlockSpec((GM_SUB * GM_CHUNK, cat_w), blk),
                  pl.BlockSpec(ln_g.shape, full2), pl.BlockSpec(ln_b.shape, full2),
                  pl.BlockSpec(w_s.shape, full3), pl.BlockSpec(b_s.shape, full2),
                  pl.BlockSpec((mem_len, kv.shape[1]), lambda b, n: (b, 0))],
        out_specs=[pl.BlockSpec((GM_SUB * GM_CHUNK, zw), blk),
                   pl.BlockSpec((mem_len, kv.shape[1]), lambda b, n: (b, 0)),
                   pl.BlockSpec(ln_g.shape, full2), pl.BlockSpec(ln_b.shape, full2),
                   pl.BlockSpec(w_s.shape, full3), pl.BlockSpec(b_s.shape, full2)],
        out_shape=[jax.ShapeDtypeStruct((T, zw), BF), jax.ShapeDtypeStruct(kv.shape, F32),
                   jax.ShapeDtypeStruct(ln_g.shape, F32), jax.ShapeDtypeStruct(ln_b.shape, F32),
                   jax.ShapeDtypeStruct(w_s.shape, F32), jax.ShapeDtypeStruct(b_s.shape, F32)],
        name="gmlp_bwd",
        compiler_params=pltpu.CompilerParams(dimension_semantics=("arbitrary", "arbitrary"), vmem_limit_bytes=VMEM_CAP_BYTES),
    )(z, dcat, ln_g, ln_b, w_s, b_s, kv)


def _place():
    x, y, c = lax.axis_index("x"), lax.axis_index("y"), lax.axis_index("c")
    chips = [(1 - x, y), (x, 1 - y), (1 - x, 1 - y)]
    return x, y, c, chips


def _half(ref, kind, e):
    if kind == "col":
        n = ref.shape[1] // 2
        return ref.at[:, pl.ds(pl.multiple_of(e * n, n), n), :]
    n = ref.shape[2] // 2
    return ref.at[:, :, pl.ds(pl.multiple_of(e * n, n), n)]


def _slot(ref, kind, j, n):
    if kind == "col":
        return ref.at[:, :, pl.ds(pl.multiple_of(j * n, n), n)]
    return ref.at[:, pl.ds(pl.multiple_of(j * n, n), n), :]


BF16_TILE_ROWS = 16
AG_DIRECT_SIXTEENTHS = 3


def _allgather_seq(name, items, cid):
    nt = len(items)
    kinds = [k for (_, k, _) in items]
    slot_kind = ["row" if k == "row" else "col" for k in kinds]
    out_type = []
    for s, k, l in items:
        L, r, c = s.shape
        lo = L if l is None else 1
        out_type.append(jax.ShapeDtypeStruct((lo, 4 * r, c) if k == "row" else (lo, r, 4 * c), s.dtype))

    def part(ref, t, e):
        return ref if kinds[t] == "vec" else _half(ref, kinds[t], e)

    def split(half):
        rows = half.shape[1]
        direct = rows * AG_DIRECT_SIXTEENTHS // 16 // BF16_TILE_ROWS * BF16_TILE_ROWS
        return half.at[:, pl.ds(0, rows - direct), :], half.at[:, pl.ds(rows - direct, direct), :]

    def body(*refs):
        sh = [refs[t] if items[t][2] is None else refs[t].at[pl.ds(items[t][2], 1)] for t in range(nt)]
        full = refs[nt:2 * nt]
        s_ici, r_ici, s_far, r_far, s_d2d, r_d2d = refs[2 * nt:]
        x, y, c, chips = _place()
        own = 2 * x + y
        sibling = (x, y, 1 - c)
        barrier = pltpu.get_barrier_semaphore()
        for peer in [(px, py, pc) for (px, py) in chips for pc in (0, 1)] + [sibling]:
            pl.semaphore_signal(barrier, inc=1, device_id=peer, device_id_type=MESH)
        pl.semaphore_wait(barrier, 7)
        width = [sh[t].shape[1] if kinds[t] == "row" else sh[t].shape[2] for t in range(nt)]
        sent = []
        for t in range(nt):
            for p, (px, py) in enumerate(chips):
                src, dst = part(sh[t], t, c), part(_slot(full[t], slot_kind[t], own, width[t]), t, c)
                cp = pltpu.make_async_remote_copy(
                    src_ref=src, dst_ref=dst, send_sem=s_ici.at[t, p], recv_sem=r_ici.at[t, p], device_id=(px, py, c),
                    device_id_type=MESH)
                cp.start()
                sent.append(cp)
                if kinds[t] == "vec":
                    continue
                far = pltpu.make_async_remote_copy(
                    src_ref=split(src)[1], dst_ref=split(dst)[1], send_sem=s_far.at[t, p], recv_sem=r_far.at[t, p],
                    device_id=(px, py, 1 - c), device_id_type=MESH)
                far.start()
                sent.append(far)
        for t in range(nt):
            for p, (px, py) in enumerate(chips):
                landed = part(_slot(full[t], slot_kind[t], 2 * px + py, width[t]), t, c)
                pltpu.make_async_remote_copy(
                    src_ref=landed, dst_ref=landed, send_sem=s_ici.at[t, p], recv_sem=r_ici.at[t, p],
                    device_id=(px, py, c), device_id_type=MESH).wait_recv()
                if kinds[t] == "vec":
                    continue
                fw = pltpu.make_async_remote_copy(
                    src_ref=split(landed)[0], dst_ref=split(landed)[0], send_sem=s_d2d.at[t, p], recv_sem=r_d2d.at[t, p],
                    device_id=sibling, device_id_type=MESH)
                fw.start()
                sent.append(fw)
        for t in range(nt):
            if kinds[t] == "vec":
                continue
            for p, (px, py) in enumerate(chips):
                forwarded, direct = split(_half(_slot(full[t], kinds[t], 2 * px + py, width[t]), kinds[t], 1 - c))
                pltpu.make_async_remote_copy(
                    src_ref=forwarded, dst_ref=forwarded, send_sem=s_d2d.at[t, p], recv_sem=r_d2d.at[t, p],
                    device_id=sibling, device_id_type=MESH).wait_recv()
                pltpu.make_async_remote_copy(
                    src_ref=direct, dst_ref=direct, send_sem=s_far.at[t, p], recv_sem=r_far.at[t, p],
                    device_id=(px, py, 1 - c), device_id_type=MESH).wait_recv()
        for cp in sent:
            cp.wait_send()

    sems = pltpu.SemaphoreType.DMA
    return pl.kernel(
        body, out_type=out_type, mesh=plsc.ScalarSubcoreMesh(axis_name="seq", num_cores=1),
        scratch_types=[sems((nt, 3)), sems((nt, 3)), sems((nt, 3)), sems((nt, 3)), sems((nt, 3)), sems((nt, 3))],
        compiler_params=pltpu.CompilerParams(collective_id=cid), name=name,
    )(*[s for (s, _, _) in items])


def _place_own(name, full, shard, kind, layer, chip_arr, after):
    lo, r, c = (shard.shape[0] if layer is None else 1,) + shard.shape[1:]
    first = 0 if layer is None else layer
    tr = _pick(r, 512)
    nr = r // tr

    def body(chip_ref, s_ref, f_ref, after_ref, o_ref):
        o_ref[...] = s_ref[...]

    if kind == "row":
        out_map = lambda i, j, chip: (i, chip[0] * nr + j, 0)
    else:
        out_map = lambda i, j, chip: (i, j, chip[0])
    return pl.pallas_call(
        body, out_shape=jax.ShapeDtypeStruct(full.shape, full.dtype),
        grid_spec=pltpu.PrefetchScalarGridSpec(
            num_scalar_prefetch=1, grid=(lo, nr),
            in_specs=[pl.BlockSpec((1, tr, c), lambda i, j, chip: (i + first, j, 0)), pl.BlockSpec(memory_space=pl.ANY),
                      pl.BlockSpec(memory_space=pl.ANY)],
            out_specs=pl.BlockSpec((1, tr, c), out_map)),
        input_output_aliases={2: 0},
        compiler_params=pltpu.CompilerParams(dimension_semantics=("parallel", "parallel"), vmem_limit_bytes=VMEM_CAP_BYTES),
        name=name,
    )(chip_arr, shard, full, after)


def _slot2(ref, kind, j, n):
    if kind == "col":
        return ref.at[:, pl.ds(pl.multiple_of(j * n, n), n)]
    return ref.at[pl.ds(pl.multiple_of(j * n, n), n), :]


def _rs_chips_seq(name, parts, kinds, cid):
    nm = len(parts)
    out_type = []
    for g, k in zip(parts, kinds):
        r, c = g.shape
        ps = (r, c // 4) if k == "col" else (r // 4, c)
        out_type += [jax.ShapeDtypeStruct(ps, BF), jax.ShapeDtypeStruct((3,) + ps, BF)]

    def body(*refs):
        g = refs[:nm]
        outs = refs[nm:3 * nm]
        loc, ssem, rsem = refs[3 * nm:]
        x, y, c, chips = _place()
        own = 2 * x + y
        barrier = pltpu.get_barrier_semaphore()
        for (px, py) in chips:
            pl.semaphore_signal(barrier, inc=1, device_id=(px, py, c), device_id_type=MESH)
        pl.semaphore_wait(barrier, 3)
        cps = []
        for m in range(nm):
            k = kinds[m]
            own_o, got_o = outs[2 * m], outs[2 * m + 1]
            n = g[m].shape[1] // 4 if k == "col" else g[m].shape[0] // 4
            lc = pltpu.make_async_copy(_slot2(g[m], k, own, n), own_o, loc.at[m])
            lc.start()
            cps.append(lc)
            for p, (px, py) in enumerate(chips):
                cp = pltpu.make_async_remote_copy(
                    src_ref=_slot2(g[m], k, 2 * px + py, n), dst_ref=got_o.at[p],
                    send_sem=ssem.at[m, p], recv_sem=rsem.at[m, p], device_id=(px, py, c), device_id_type=MESH)
                cp.start()
                cps.append(cp)
        for cp in cps:
            cp.wait()

    return pl.kernel(
        body, out_type=out_type, mesh=plsc.ScalarSubcoreMesh(axis_name="seq", num_cores=1),
        scratch_types=[pltpu.SemaphoreType.DMA((nm,)), pltpu.SemaphoreType.DMA((nm, 3)), pltpu.SemaphoreType.DMA((nm, 3))],
        compiler_params=pltpu.CompilerParams(collective_id=cid), name=name,
    )(*parts)


def _finish_share(name, owns, gots, kind, c_arr):
    L = len(owns)
    r, c = owns[0].shape
    tr = _pick(r, 128 if kind == "col" else 256)
    nb = r // tr
    nq = L * nb

    def chunk_of(l):
        return lambda h, q: jnp.clip(q * (1 - h) + (nq - 1) * h - l * nb, 0, nb - 1)

    ins, in_specs = [], []
    for l in range(L):
        at = chunk_of(l)
        ins += [owns[l], gots[l].reshape(3 * r, c), gots[l].reshape(3 * r, c), gots[l].reshape(3 * r, c)]
        in_specs.append(pl.BlockSpec((tr, c), functools.partial(lambda h, q, cc, at: (at(h, q), 0), at=at)))
        in_specs += [pl.BlockSpec((tr, c), functools.partial(lambda h, q, cc, at, p: (p * nb + at(h, q), 0), at=at, p=p))
                     for p in range(3)]
    if kind == "col":
        out_sd = (L, 2, r, c)
        o_spec = pl.BlockSpec((None, 2, tr, c), lambda h, q, cc: ((q * h) // nb, 0, (q * h) % nb, 0))
    else:
        out_sd = (L * r, 2 * c)
        o_spec = pl.BlockSpec((tr, 2 * c), lambda h, q, cc: (q * h, 0))

    def kern(c_ref, *refs):
        in_refs = refs[:4 * L]
        out_ref, mine, recv, ssem, rsem = refs[4 * L:]
        h, q = pl.program_id(0), pl.program_id(1)
        x, y, cc, _ = _place()

        def swap(qq):
            return pltpu.make_async_remote_copy(src_ref=mine.at[qq], dst_ref=recv.at[qq], send_sem=ssem.at[qq],
                                                recv_sem=rsem.at[qq], device_id=(x, y, 1 - cc), device_id_type=MESH)

        for l in range(L):
            @pl.when(jnp.logical_and(h == 0, q // nb == l))
            def _(l=l):
                o_ref, g0, g1, g2 = in_refs[4 * l:4 * l + 4]
                mine[q] = ((o_ref[...].astype(F32) + g0[...].astype(F32)) + g1[...].astype(F32)) + g2[...].astype(F32)
                swap(q).start()

        @pl.when(h == 1)
        def _():
            swap(q).wait()
            a, b = mine[q], recv[q]
            first = c_ref[0] == 0
            lo, hi = jnp.where(first, a, b), jnp.where(first, b, a)
            if kind == "col":
                out_ref[0] = lo
                out_ref[1] = hi
            else:
                out_ref[:, :c] = lo
                out_ref[:, c:] = hi

    full = pl.pallas_call(
        kern,
        grid_spec=pltpu.PrefetchScalarGridSpec(
            num_scalar_prefetch=1, grid=(2, nq), in_specs=in_specs, out_specs=o_spec,
            scratch_shapes=[pltpu.VMEM((nq, tr, c), F32), pltpu.VMEM((nq, tr, c), F32),
                            pltpu.SemaphoreType.DMA((nq,)), pltpu.SemaphoreType.DMA((nq,))]),
        out_shape=jax.ShapeDtypeStruct(out_sd, F32), name=name,
        compiler_params=pltpu.CompilerParams(dimension_semantics=("arbitrary", "arbitrary"),
                                             vmem_limit_bytes=VMEM_CAP_BYTES),
    )(c_arr, *ins)
    return full.reshape(L, 2 * r, c) if kind == "col" else full.reshape(L, r, 2 * c)


def _small_allreduce(buf, name):
    R = buf.shape[0]
    assert R % 16 == 0
    h = R // 2

    def body(x_ref, o_ref, sib, csum, got, s_a, r_a, s_b, r_b, s_c, r_c):
        x, y, c, chips = _place()
        sibling = (x, y, 1 - c)
        own = 2 * x + y
        swap = pltpu.make_async_remote_copy(src_ref=x_ref, dst_ref=sib, send_sem=s_a, recv_sem=r_a,
                                            device_id=sibling, device_id_type=MESH)
        swap.start()
        swap.wait()
        a, b = x_ref[...], sib[...]
        south = c == 0
        csum[...] = jnp.where(south, a, b) + jnp.where(south, b, a)
        lo = pl.multiple_of(c * h, 8)
        mine = csum.at[pl.ds(lo, h)]
        got[own] = csum[pl.ds(lo, h)]
        sends = []
        for p, (px, py) in enumerate(chips):
            cp = pltpu.make_async_remote_copy(src_ref=mine, dst_ref=got.at[own], send_sem=s_b.at[p], recv_sem=r_b.at[p],
                                              device_id=(px, py, c), device_id_type=MESH)
            cp.start()
            sends.append(cp)
        for cp in sends:
            cp.wait()
        o_ref[pl.ds(lo, h)] = ((got[0] + got[1]) + got[2]) + got[3]
        done = o_ref.at[pl.ds(lo, h)]
        back = pltpu.make_async_remote_copy(src_ref=done, dst_ref=done, send_sem=s_c, recv_sem=r_c,
                                            device_id=sibling, device_id_type=MESH)
        back.start()
        back.wait_send()
        other = o_ref.at[pl.ds(pl.multiple_of((1 - c) * h, 8), h)]
        pltpu.make_async_remote_copy(src_ref=other, dst_ref=other, send_sem=s_c, recv_sem=r_c,
                                     device_id=sibling, device_id_type=MESH).wait_recv()

    vm = pl.BlockSpec(memory_space=pltpu.VMEM)
    return pl.pallas_call(
        body, out_shape=jax.ShapeDtypeStruct(buf.shape, F32), in_specs=[vm], out_specs=vm,
        scratch_shapes=[pltpu.VMEM((R, LANES), F32), pltpu.VMEM((R, LANES), F32), pltpu.VMEM((4, h, LANES), F32),
                        pltpu.SemaphoreType.DMA, pltpu.SemaphoreType.DMA, pltpu.SemaphoreType.DMA((3,)),
                        pltpu.SemaphoreType.DMA((3,)), pltpu.SemaphoreType.DMA, pltpu.SemaphoreType.DMA],
        name=name,
        compiler_params=pltpu.CompilerParams(vmem_limit_bytes=VMEM_CAP_BYTES),
    )(buf)


PACK_TILE_ROWS = 8


def _item_rows(shape):
    n = 1
    for d in shape:
        n *= d
    return -(-n // (PACK_TILE_ROWS * LANES)) * PACK_TILE_ROWS


def _pack(arrs, rows_total):
    buf = jnp.zeros((rows_total, LANES), F32)
    r = 0
    for a in arrs:
        f = a.reshape(-1).astype(F32)
        nr = _item_rows(a.shape)
        block = jnp.pad(f, (0, nr * LANES - f.shape[0])).reshape(nr, LANES)
        buf = lax.dynamic_update_slice(buf, block, (r, 0))
        r += nr
    return buf


def _unpack(buf, shapes):
    out, r = [], 0
    for s in shapes:
        n = 1
        for d in s:
            n *= d
        nr = _item_rows(s)
        out.append(buf[r:r + nr].reshape(-1)[:n].reshape(s))
        r += nr
    return out


def _rows_needed(shapes):
    return -(-sum(_item_rows(s) for s in shapes) // (2 * PACK_TILE_ROWS)) * (2 * PACK_TILE_ROWS)


def _two_rows(a, b):
    out = jnp.zeros((2, a.shape[1]), a.dtype)
    return lax.dynamic_update_slice(lax.dynamic_update_slice(out, a, (0, 0)), b, (1, 0))


def _adam(w, g, m, v):
    m = ADAM_B1 * m + (1.0 - ADAM_B1) * g
    v = ADAM_B2 * v + (1.0 - ADAM_B2) * jnp.square(g)
    m_hat = m / (1.0 - ADAM_B1 ** ADAM_STEP)
    v_hat = v / (1.0 - ADAM_B2 ** ADAM_STEP)
    delta = -ADAM_LR * (m_hat / (jnp.sqrt(v_hat) + ADAM_EPS) + ADAM_WD * w)
    return delta, m, v


def _adam_call(name, w2, g2, m2, v2, tr, pass_grad=False):
    def fn(rv, cv):
        outs = list(_adam(*rv))
        return ([rv[1]] + outs if pass_grad else outs), []

    width = w2.shape[1]
    return _rowcall(name, fn, [(w2, 0, width), (g2, 0, width), (m2, 0, width), (v2, 0, width)], [],
                    [(width, F32)] * (4 if pass_grad else 3), [], tr)


def kernel(x, mem, mem_norm, lb_logits, ffn1_norm, ffn1_w_in, ffn1_w_out, mix_norm, mem_w_kv, hgrn_w_in, hgrn_gnorm, hgrn_w_out, gmlp_w_in, gmlp_ln_g, gmlp_ln_b, gmlp_w_s, gmlp_b_s, gmlp_w_out, ffn2_norm, ffn2_w_in, ffn2_w_out, final_norm, loss_target, m_mem_norm, m_lb_logits, m_ffn1_norm, m_ffn1_w_in, m_ffn1_w_out, m_mix_norm, m_mem_w_kv, m_hgrn_w_in, m_hgrn_gnorm, m_hgrn_w_out, m_gmlp_w_in, m_gmlp_ln_g, m_gmlp_ln_b, m_gmlp_w_s, m_gmlp_b_s, m_gmlp_w_out, m_ffn2_norm, m_ffn2_w_in, m_ffn2_w_out, m_final_norm, v_mem_norm, v_lb_logits, v_ffn1_norm, v_ffn1_w_in, v_ffn1_w_out, v_mix_norm, v_mem_w_kv, v_hgrn_w_in, v_hgrn_gnorm, v_hgrn_w_out, v_gmlp_w_in, v_gmlp_ln_g, v_gmlp_ln_b, v_gmlp_w_s, v_gmlp_b_s, v_gmlp_w_out, v_ffn2_norm, v_ffn2_w_in, v_ffn2_w_out, v_final_norm):
    bl, seq, D = x.shape
    T = bl * seq
    mem_len = mem.shape[1]
    chip = 2 * lax.axis_index("x") + lax.axis_index("y")
    c_arr = lax.axis_index("c").astype(jnp.int32).reshape(1)
    chip_arr = chip.astype(jnp.int32).reshape(1)
    TR = 1024

    big = [("ffn1_w_in", ffn1_w_in, "col"), ("ffn1_w_out", ffn1_w_out, "row"), ("mem_w_kv", mem_w_kv, "col"),
           ("hgrn_w_in", hgrn_w_in, "col"), ("hgrn_w_out", hgrn_w_out, "row"), ("gmlp_w_in", gmlp_w_in, "col"),
           ("gmlp_w_out", gmlp_w_out, "row"), ("ffn2_w_in", ffn2_w_in, "col"), ("ffn2_w_out", ffn2_w_out, "row")]
    kinds = [k for (_, _, k) in big]
    shards_bf = []
    for nm, w, _ in big:
        L, r, c = w.shape
        (wb,) = _rowcall("cast_" + nm, lambda rv, cv: ([rv[0]], []), [(w.reshape(L * r, c), 0, c)], [], [(c, BF)], [], 512)
        shards_bf.append(wb.reshape(L, r, c))
    sb = dict(zip([nm for (nm, _, _) in big], shards_bf))
    groups = [[("ffn1_w_in", 0)], [("ffn1_w_out", 0)], [("hgrn_w_in", None)], [("mem_w_kv", None)], [("hgrn_w_out", None)],
              [("ffn2_w_in", 0), ("ffn2_w_out", 0), ("gmlp_ln_g", None), ("gmlp_ln_b", None)],
              [("ffn1_w_in", 1), ("ffn1_w_out", 1)],
              [("gmlp_w_in", None), ("gmlp_w_out", None)],
              [("ffn2_w_in", 1), ("ffn2_w_out", 1)]]
    kind_of = {nm: k for (nm, _, k) in big}
    for nm, vec in (("gmlp_ln_g", gmlp_ln_g), ("gmlp_ln_b", gmlp_ln_b)):
        sb[nm] = vec.reshape(1, 1, -1)
        kind_of[nm] = "vec"
    gathered = {nm: [None, None] for nm in ("ffn1_w_in", "ffn1_w_out", "ffn2_w_in", "ffn2_w_out")}
    others = {}
    for gi, grp in enumerate(groups):
        outs = _allgather_seq("gather_%d" % gi, [(sb[nm], kind_of[nm], l) for (nm, l) in grp], gi)
        for (nm, l), o in zip(grp, outs):
            others[(nm, l)] = o

    def whole(nm, l, after):
        full = _place_own("own_%s_%d" % (nm, l or 0), others[(nm, l)], sb[nm], "row" if kind_of[nm] == "row" else "col", l,
                          chip_arr, after)
        if l is None:
            gathered[nm] = full
        else:
            gathered[nm][l] = full
        return full

    def rms_fwd(name, xin, g):
        (h,) = _rowcall(name, lambda rv, cv: ([_rmsnorm(rv[0], cv[0])], []), [(xin, 0, D)], [g.reshape(1, D)], [(D, BF)], [], TR)
        return h

    def ffn_fwd(tag, xin, h, nm_in, nm_out, layer, next_gain):
        w_in = whole(nm_in, layer, h)
        dff = w_in.shape[2] // 2
        zg, zu, a = _ffn_in_swiglu("ffn_in_" + tag, h, w_in, 1024, dff // 2)
        out = _mm("ffn_out_" + tag, a, whole(nm_out, layer, a), "nn", F32, 1024, 1024, dff, scale=0.5, res=xin, b_lead=0,
                  norm_gain=None if next_gain is None else next_gain.reshape(1, D))
        xo, h_next = (out, None) if next_gain is None else out
        return xo, h_next, (xin, h, zg, zu, a)

    def ffn_bwd(tag, dxo, saved, g, w_in, w_out, layer):
        xin, h, zg, zu, a = saved
        dff = w_out[layer].shape[1]
        dw_out = _mm_tn_pair("ffn_dwo_" + tag, a, dxo, "row", c_arr, dff // 2, T, scale=0.5)
        dz = _ffn_da_swiglu("ffn_da_" + tag, dxo, w_out[layer], zg, zu, 512)
        dw_in = _mm_tn_pair("ffn_dwi_" + tag, h, dz, "col", c_arr, 512, T)
        dx, dg = _mm_dh_rms("ffn_dh_" + tag, dz, w_in[layer], xin, g.reshape(1, D), dxo, 512)
        return dx, dg, dw_in, dw_out

    def rms_bwd(name, xin, g, dh, dres):
        def fn(rv, cv):
            _, vjp = jax.vjp(_rmsnorm, rv[0], cv[0])
            dx, dg = vjp(rv[1])
            if dres is not None:
                dx = dx + rv[2]
            return [dx], [dg]

        rows = [(xin, 0, D), (dh, 0, D)] + ([(dres, 0, D)] if dres is not None else [])
        dx, dg = _rowcall(name, fn, rows, [g.reshape(1, D)], [(D, F32)], [((1, D), F32)], TR)
        return dx, dg

    x0 = x.reshape(T, D)
    tgt = loss_target.reshape(T, D)
    mem2 = mem.reshape(bl * mem_len, D)
    memn = rms_fwd("rms_mem", mem2, mem_norm)

    h_f10 = rms_fwd("rms_f1l0", x0, ffn1_norm[0])
    x1, h_m0, sv_f10 = ffn_fwd("f1l0", x0, h_f10, "ffn1_w_in", "ffn1_w_out", 0, mix_norm[0])
    z_m0 = _mm("mix_in_0", h_m0, whole("hgrn_w_in", None, h_m0), "nn", F32, 2048, 512, D, b_lead=0)
    w_kv = whole("mem_w_kv", None, z_m0)
    kv = [_mm("kv_%d" % i, memn, w_kv, "nn", F32, 512, 512, D, b_lead=i) for i in range(2)]
    cat0, stash0 = _hgrn_fwd2(z_m0, lb_logits, hgrn_gnorm, kv[0], bl, seq)
    x2, h_f20 = _mm("mix_out_0", cat0, whole("hgrn_w_out", None, cat0), "nn", F32, 1024, 1024, cat0.shape[1], res=x1, b_lead=0,
                    norm_gain=ffn2_norm[0].reshape(1, D))
    x3, h_f11, sv_f20 = ffn_fwd("f2l0", x2, h_f20, "ffn2_w_in", "ffn2_w_out", 0, ffn1_norm[1])
    x4, h_m1, sv_f11 = ffn_fwd("f1l1", x3, h_f11, "ffn1_w_in", "ffn1_w_out", 1, mix_norm[1])
    z_m1 = _mm("mix_in_1", h_m1, whole("gmlp_w_in", None, h_m1), "nn", F32, 2048, 512, D, b_lead=0)
    nc1 = seq // GM_CHUNK
    w_s, b_s = gmlp_w_s[0], gmlp_b_s[0]
    ln_w = GM_GROUPS * GM_GROUP_DIM
    ln_g_full, ln_b_full = [whole(nm, None, z_m1).reshape(1, ln_w) for nm in ("gmlp_ln_g", "gmlp_ln_b")]
    cat1 = _gmlp_fwd(z_m1, ln_g_full, ln_b_full, w_s, b_s, kv[1], bl, nc1)
    x5, h_f21 = _mm("mix_out_1", cat1, whole("gmlp_w_out", None, cat1), "nn", F32, 1024, 1024, cat1.shape[1], res=x4, b_lead=0,
                    norm_gain=ffn2_norm[1].reshape(1, D))
    x6, _, sv_f21 = ffn_fwd("f2l1", x5, h_f21, "ffn2_w_in", "ffn2_w_out", 1, None)

    def head(rv, cv):
        def f(xx, gg):
            err = _rmsnorm(xx, gg) - rv[1]
            return 0.5 * jnp.sum(jnp.mean(err * err, axis=-1, keepdims=True), axis=0, keepdims=True)

        ls, vjp = jax.vjp(f, rv[0], cv[0])
        dx, dg = vjp(jnp.ones((1, 1), F32))
        return [dx], [dg, jnp.broadcast_to(ls, (1, 128))]

    dx6, d_final, loss_part = _rowcall("loss_head", head, [(x6, 0, D), (tgt, 0, D)], [final_norm.reshape(1, D)],
                                       [(D, F32)], [((1, D), F32), ((1, 128), F32)], TR)

    rs_out = {}
    n_gather = len(groups)

    def rs(gi, items):
        outs = _rs_chips_seq("reduce_%d" % gi, [p for (_, p, _) in items], [k for (_, _, k) in items], n_gather + gi)
        for i, (key, _, _) in enumerate(items):
            rs_out[key] = (outs[2 * i], outs[2 * i + 1])

    dx5, dg_f21, dwi_f21, dwo_f21 = ffn_bwd("f2l1", dx6, sv_f21, ffn2_norm[1], gathered["ffn2_w_in"], gathered["ffn2_w_out"], 1)
    rs(0, [(("ffn2_w_out", 1), dwo_f21, "row"), (("ffn2_w_in", 1), dwi_f21, "col")])
    dcat1 = _mm("mix_dcat_1", dx5, gathered["gmlp_w_out"], "nt", F32, 2048, 1024, D, b_lead=0)
    dwo_m1 = _mm_tn_pair("mix_dwo_1", cat1, dx5, "row", c_arr, 1024, T)
    dz_m1, dkv1, d_lng, d_lnb, d_ws, d_bs = _gmlp_bwd(z_m1, dcat1, ln_g_full, ln_b_full, w_s, b_s, kv[1], bl, nc1)
    dx4, dg_m1 = _mm_dh_rms("mix_dh_1", dz_m1, gathered["gmlp_w_in"], x4, mix_norm[1].reshape(1, D), dx5, 512)
    dwi_m1 = _mm_tn_pair("mix_dwi_1", h_m1, dz_m1, "col", c_arr, 1024, T)
    rs(1, [(("gmlp_w_out", 0), dwo_m1, "row"), (("gmlp_w_in", 0), dwi_m1, "col")])
    dx3, dg_f11, dwi_f11, dwo_f11 = ffn_bwd("f1l1", dx4, sv_f11, ffn1_norm[1], gathered["ffn1_w_in"], gathered["ffn1_w_out"], 1)
    rs(2, [(("ffn1_w_out", 1), dwo_f11, "row"), (("ffn1_w_in", 1), dwi_f11, "col")])

    dx2, dg_f20, dwi_f20, dwo_f20 = ffn_bwd("f2l0", dx3, sv_f20, ffn2_norm[0], gathered["ffn2_w_in"], gathered["ffn2_w_out"], 0)
    rs(3, [(("ffn2_w_out", 0), dwo_f20, "row"), (("ffn2_w_in", 0), dwi_f20, "col")])
    dcat0 = _mm("mix_dcat_0", dx2, gathered["hgrn_w_out"], "nt", F32, 2048, 1024, D, b_lead=0)
    dwo_m0 = _mm_tn_pair("mix_dwo_0", cat0, dx2, "row", c_arr, 1024, T)
    dz_m0, dkv0, d_lb, d_gn = _hgrn_bwd2(z_m0, dcat0, stash0, lb_logits, hgrn_gnorm, kv[0], bl, seq)
    dx1, dg_m0 = _mm_dh_rms("mix_dh_0", dz_m0, gathered["hgrn_w_in"], x1, mix_norm[0].reshape(1, D), dx2, 512)
    dwi_m0 = _mm_tn_pair("mix_dwi_0", h_m0, dz_m0, "col", c_arr, 1024, T)
    rs(4, [(("hgrn_w_out", 0), dwo_m0, "row"), (("hgrn_w_in", 0), dwi_m0, "col")])

    dwkv = [_mm_tn_pair("kv_dw_%d" % i, memn, dkv, "col", c_arr, 1024, 512) for i, dkv in enumerate([dkv0, dkv1])]
    rs(5, [(("mem_w_kv", 0), dwkv[0], "col"), (("mem_w_kv", 1), dwkv[1], "col")])
    dmemn = _mm("kv_dx_0", dkv0, gathered["mem_w_kv"], "nt", F32, 512, 512, 1024, b_lead=0)
    dmemn = _mm("kv_dx_1", dkv1, gathered["mem_w_kv"], "nt", F32, 512, 512, 1024, res=dmemn, b_lead=1)
    _, d_memnorm = rms_bwd("rms_bwd_mem", mem2, mem_norm, dmemn, None)

    dx0, dg_f10, dwi_f10, dwo_f10 = ffn_bwd("f1l0", dx1, sv_f10, ffn1_norm[0], gathered["ffn1_w_in"], gathered["ffn1_w_out"], 0)
    rs(6, [(("ffn1_w_out", 0), dwo_f10, "row")])
    rs(7, [(("ffn1_w_in", 0), dwi_f10, "col")])

    shard_grads = [_finish_share("finish_" + nm, [rs_out[(nm, l)][0] for l in range(w.shape[0])],
                                 [rs_out[(nm, l)][1] for l in range(w.shape[0])], k, c_arr) for (nm, w, k) in big]

    big_w = [w for (_, w, _) in big]
    big_m = [m_ffn1_w_in, m_ffn1_w_out, m_mem_w_kv, m_hgrn_w_in, m_hgrn_w_out, m_gmlp_w_in, m_gmlp_w_out, m_ffn2_w_in, m_ffn2_w_out]
    big_v = [v_ffn1_w_in, v_ffn1_w_out, v_mem_w_kv, v_hgrn_w_in, v_hgrn_w_out, v_gmlp_w_in, v_gmlp_w_out, v_ffn2_w_in, v_ffn2_w_out]
    big_out = {}
    for (nm, w, _), g, m, v in zip(big, shard_grads, big_m, big_v):
        L, r, c = w.shape
        g2, d2, m2, v2 = _adam_call("adam_" + nm, w.reshape(L * r, c), g.reshape(L * r, c), m.reshape(L * r, c),
                                    v.reshape(L * r, c), 256, pass_grad=True)
        big_out[nm] = (g2.reshape(w.shape), d2.reshape(w.shape), m2.reshape(w.shape), v2.reshape(w.shape))

    d_ffn1n = _two_rows(dg_f10, dg_f11)
    d_mixn = _two_rows(dg_m0, dg_m1)
    d_ffn2n = _two_rows(dg_f20, dg_f21)
    small_parts = [loss_part[:, :1], d_memnorm, d_lb, d_ffn1n, d_mixn, d_gn, d_lng, d_lnb, d_ws, d_bs, d_ffn2n, d_final]
    red_shapes = [(1,), mem_norm.shape, lb_logits.shape, ffn1_norm.shape, mix_norm.shape, hgrn_gnorm.shape, (1, ln_w), (1, ln_w),
                  gmlp_w_s.shape, gmlp_b_s.shape, ffn2_norm.shape, final_norm.shape]
    red = _small_allreduce(_pack(small_parts, _rows_needed(red_shapes)), "reduce_small")
    (loss_v, g_memn, g_lb, g_f1n, g_mixn, g_gn, g_lng_full, g_lnb_full, g_ws, g_bs, g_f2n, g_fin) = _unpack(red, red_shapes)
    lsh = gmlp_ln_g.shape[1]
    g_lng = lax.dynamic_slice(g_lng_full, (0, chip * lsh), (1, lsh))
    g_lnb = lax.dynamic_slice(g_lnb_full, (0, chip * lsh), (1, lsh))
    small_w = [mem_norm, lb_logits, ffn1_norm, mix_norm, hgrn_gnorm, gmlp_ln_g, gmlp_ln_b, gmlp_w_s, gmlp_b_s, ffn2_norm, final_norm]
    small_g = [g_memn, g_lb, g_f1n, g_mixn, g_gn, g_lng, g_lnb, g_ws, g_bs, g_f2n, g_fin]
    small_m = [m_mem_norm, m_lb_logits, m_ffn1_norm, m_mix_norm, m_hgrn_gnorm, m_gmlp_ln_g, m_gmlp_ln_b, m_gmlp_w_s, m_gmlp_b_s, m_ffn2_norm, m_final_norm]
    small_v = [v_mem_norm, v_lb_logits, v_ffn1_norm, v_mix_norm, v_hgrn_gnorm, v_gmlp_ln_g, v_gmlp_ln_b, v_gmlp_w_s, v_gmlp_b_s, v_ffn2_norm, v_final_norm]
    sshapes = [w.shape for w in small_w]
    nrow = _rows_needed(sshapes)
    d_p, m_p, v_p = _adam_call("adam_small", _pack(small_w, nrow), _pack(small_g, nrow), _pack(small_m, nrow), _pack(small_v, nrow), nrow)
    s_delta, s_m, s_v = _unpack(d_p, sshapes), _unpack(m_p, sshapes), _unpack(v_p, sshapes)
    small_names = ["mem_norm", "lb_logits", "ffn1_norm", "mix_norm", "hgrn_gnorm", "gmlp_ln_g", "gmlp_ln_b", "gmlp_w_s", "gmlp_b_s", "ffn2_norm", "final_norm"]
    small_out = {nm: (g.reshape(w.shape), d, m, v) for nm, w, g, d, m, v in zip(small_names, small_w, small_g, s_delta, s_m, s_v)}

    order = ["mem_norm", "lb_logits", "ffn1_norm", "ffn1_w_in", "ffn1_w_out", "mix_norm", "mem_w_kv", "hgrn_w_in", "hgrn_gnorm",
             "hgrn_w_out", "gmlp_w_in", "gmlp_ln_g", "gmlp_ln_b", "gmlp_w_s", "gmlp_b_s", "gmlp_w_out", "ffn2_norm", "ffn2_w_in",
             "ffn2_w_out", "final_norm"]
    allo = {**big_out, **small_out}
    grad_x = dx0.reshape(x.shape)
    return (loss_v.reshape(()), grad_x, *[allo[n][0] for n in order], *[allo[n][1] for n in order],
            *[allo[n][2] for n in order], *[allo[n][3] for n in order])
```

```python
import functools

import jax
import jax.numpy as jnp
from jax import lax
from jax.experimental import pallas as pl
from jax.experimental.pallas import tpu as pltpu
from jax.experimental.pallas import tpu_sc as plsc

BF = jnp.bfloat16
F32 = jnp.float32
MESH = pl.DeviceIdType.MESH

EPS = 1e-6
D_MODEL = 1024
HG_HEADS = 8
HG_DIM = 128
HG_CHUNK = 64
GM_CHUNK = 128
GM_GROUPS = 8
GM_GROUP_DIM = 256
XA_HEADS = 4
XA_DIM = 256
ADAM_LR = 0.001
ADAM_B1 = 0.9
ADAM_B2 = 0.999
ADAM_EPS = 1e-08
ADAM_WD = 0.01
ADAM_STEP = 10

VMEM_CAP_BYTES = 60 * 1024 * 1024
LANES = 1024


def _pick(n, cap, mult=16):
    if n <= cap:
        return n
    for d in range(cap - cap % mult, 0, -mult):
        if n % d == 0:
            return d
    raise ValueError((n, cap, mult))


def _dg(a, b, ca, cb):
    return lax.dot_general(a.astype(BF), b.astype(BF), (((ca,), (cb,)), ((), ())), preferred_element_type=F32)


@jax.custom_vjp
def dot_nn(a, b):
    return _dg(a, b, 1, 0)


def _nn_fwd(a, b):
    return _dg(a, b, 1, 0), (a, b)


def _nn_bwd(r, g):
    a, b = r
    return _dg(g, b, 1, 1), _dg(a, g, 0, 0)


dot_nn.defvjp(_nn_fwd, _nn_bwd)


@jax.custom_vjp
def dot_nt(a, b):
    return _dg(a, b, 1, 1)


def _nt_fwd(a, b):
    return _dg(a, b, 1, 1), (a, b)


def _nt_bwd(r, g):
    a, b = r
    return _dg(g, b, 1, 0), _dg(g, a, 0, 0)


dot_nt.defvjp(_nt_fwd, _nt_bwd)


@jax.custom_vjp
def dot_tn(a, b):
    return _dg(a, b, 0, 0)


def _tn_fwd(a, b):
    return _dg(a, b, 0, 0), (a, b)


def _tn_bwd(r, g):
    a, b = r
    return _dg(b, g, 1, 1), _dg(a, g, 1, 0)


dot_tn.defvjp(_tn_fwd, _tn_bwd)


def _rmsnorm(x, g):
    return x * lax.rsqrt(jnp.mean(x * x, axis=-1, keepdims=True) + EPS) * g


def _silu(x):
    return x * jax.nn.sigmoid(x)


@jax.custom_vjp
def _gelu(x):
    return 0.5 * x * (1.0 + lax.erf(x * (0.5 ** 0.5)))


def _gelu_fwd(x):
    return _gelu(x), x


def _gelu_bwd(x, g):
    t = x * (0.5 ** 0.5)
    cdf = 0.5 * (1.0 + lax.erf(t))
    return (g * (cdf + x * (jnp.exp(-(t * t)) * (0.5 / 3.141592653589793) ** 0.5)),)


_gelu.defvjp(_gelu_fwd, _gelu_bwd)


def _softmax_last(s):
    m = lax.stop_gradient(jnp.max(s, axis=-1, keepdims=True))
    e = jnp.exp(s - m)
    return e / jnp.sum(e, axis=-1, keepdims=True)


def _tril(n):
    r = lax.broadcasted_iota(jnp.int32, (n, n), 0)
    c = lax.broadcasted_iota(jnp.int32, (n, n), 1)
    return r >= c


def _attention(zx, mk, mv):
    s = dot_nt(zx, mk) * (XA_DIM ** -0.5)
    return dot_nn(_softmax_last(s), mv)


def _chunk_sums(x, suffix):
    n = x.shape[0]
    r = lax.broadcasted_iota(jnp.int32, (n, n), 0)
    c = lax.broadcasted_iota(jnp.int32, (n, n), 1)
    tri = jnp.logical_and(r <= c if suffix else r >= c, r // HG_CHUNK == c // HG_CHUNK).astype(BF)
    hi = x.astype(BF)
    rest = x - hi.astype(F32)
    mid = rest.astype(BF)
    lo = (rest - mid.astype(F32)).astype(BF)
    return (_dg(tri, hi, 1, 0) + _dg(tri, mid, 1, 0)) + _dg(tri, lo, 1, 0)


@jax.custom_vjp
def _running_sums(x):
    return _chunk_sums(x, False)


_running_sums.defvjp(lambda x: (_chunk_sums(x, False), None), lambda _, g: (_chunk_sums(g, True),))


def _hgrn_decays(zf, lb3):
    l0, l1, l2 = lb3[0:1], lb3[1:2], lb3[2:3]
    m = lax.stop_gradient(jnp.maximum(jnp.maximum(l0, l1), l2))
    e0 = jnp.exp(l0 - m)
    lb = e0 / (e0 + jnp.exp(l1 - m) + jnp.exp(l2 - m))
    f = lb + (1.0 - lb) * jax.nn.sigmoid(zf)
    return f, _running_sums(jnp.log(f))


def _hgrn_head(zq, f, b, zi, zg, gn, S):
    q = _silu(zq)
    k = 1.0 - f
    b_last = b[HG_CHUNK - 1:HG_CHUNK, :]
    q_dec = q * jnp.exp(b)
    k_inv = k * jnp.exp(-b)
    a = jnp.where(_tril(HG_CHUNK), dot_nt(q_dec, k_inv), 0.0)
    o = dot_nn(a, zi) + dot_nn(q_dec, S)
    S_new = jnp.exp(b_last).reshape(HG_DIM, 1) * S + dot_tn(k * jnp.exp(b_last - b), zi)
    o = _rmsnorm(o, gn) * _silu(zg)
    return o, S_new


def _gmlp_block(zu, zv, zx, lng, lnb, ws, bs, mk, mv):
    gv = [_gelu(v) for v in zv]
    width = GM_GROUPS * GM_GROUP_DIM
    mu = sum(jnp.sum(g, axis=-1, keepdims=True) for g in gv) / width
    xc = [g - mu for g in gv]
    var = sum(jnp.sum(c * c, axis=-1, keepdims=True) for c in xc) / width
    r = lax.rsqrt(var + EPS)
    outs = []
    for g in range(GM_GROUPS):
        v = xc[g] * r * lng[g] + lnb[g]
        w = jnp.where(_tril(GM_CHUNK), ws[g], 0.0)
        mixed = dot_nn(w, v) + bs[g].reshape(GM_CHUNK, 1)
        outs.append(_gelu(zu[g]) * mixed)
    for a in range(XA_HEADS):
        outs.append(_attention(zx[a], mk[a], mv[a]))
    return outs


def _rowcall(name, fn, rows, consts, row_outs, acc_outs, tr):
    nrows = rows[0][0].shape[0]
    tr = _pick(nrows, tr)
    n_r, n_c, n_ro, n_ao = len(rows), len(consts), len(row_outs), len(acc_outs)

    def kern(*refs):
        rv = [r[...] for r in refs[:n_r]]
        cv = [r[...] for r in refs[n_r:n_r + n_c]]
        ro_refs = refs[n_r + n_c:n_r + n_c + n_ro]
        ao_refs = refs[n_r + n_c + n_ro:]
        ro, ao = fn(rv, cv)
        for ref, v in zip(ro_refs, ro):
            ref[...] = v.astype(ref.dtype)
        if n_ao:
            @pl.when(pl.program_id(0) == 0)
            def _():
                for ref in ao_refs:
                    ref[...] = jnp.zeros(ref.shape, ref.dtype)

            for ref, v in zip(ao_refs, ao):
                ref[...] += v.astype(ref.dtype)

    in_specs = [pl.BlockSpec((tr, w), functools.partial(lambda i, cb: (i, cb), cb=cb)) for (_, cb, w) in rows]
    in_specs += [pl.BlockSpec(c.shape, lambda i: (0, 0)) for c in consts]
    out_specs = [pl.BlockSpec((tr, w), lambda i: (i, 0)) for (w, _) in row_outs]
    out_specs += [pl.BlockSpec(s, lambda i: (0, 0)) for (s, _) in acc_outs]
    out_shape = [jax.ShapeDtypeStruct((nrows, w), dt) for (w, dt) in row_outs]
    out_shape += [jax.ShapeDtypeStruct(s, dt) for (s, dt) in acc_outs]
    outs = pl.pallas_call(
        kern, grid=(nrows // tr,), in_specs=in_specs, out_specs=out_specs, out_shape=out_shape, name=name,
        compiler_params=pltpu.CompilerParams(dimension_semantics=("arbitrary",),
                                             vmem_limit_bytes=VMEM_CAP_BYTES),
    )(*[a for (a, _, _) in rows], *consts)
    return outs


def _mm(name, a, b, mode, out_dtype, tm, tn, tk, scale=1.0, res=None, a_lead=None, b_lead=None, norm_gain=None):
    ash = a.shape[-2:]
    bsh = b.shape[-2:]
    if mode == "nn":
        (M, K), (K2, N) = ash, bsh
    elif mode == "nt":
        (M, K), (N, K2) = ash, bsh
    else:
        (K, M), (K2, N) = ash, bsh
    assert K == K2, (name, a.shape, b.shape)
    tm, tn, tk = min(tm, M), min(tn, N), min(tk, K)
    assert M % tm == 0 and N % tn == 0 and K % tk == 0, (name, M, N, K, tm, tn, tk)
    nk = K // tk
    dims = {"nn": (1, 0), "nt": (1, 1), "tn": (0, 0)}[mode]

    def lead(spec_shape, index_fn, lead_idx):
        if lead_idx is None:
            return pl.BlockSpec(spec_shape, index_fn)
        return pl.BlockSpec((None,) + spec_shape, lambda i, j, k: (lead_idx,) + index_fn(i, j, k))

    if mode == "tn":
        a_spec = lead((tk, tm), lambda i, j, k: (k, i), a_lead)
    else:
        a_spec = lead((tm, tk), lambda i, j, k: (i, k), a_lead)
    if mode == "nt":
        b_spec = lead((tn, tk), lambda i, j, k: (j, k), b_lead)
    else:
        b_spec = lead((tk, tn), lambda i, j, k: (k, j), b_lead)
    o_spec = pl.BlockSpec((tm, tn), lambda i, j, k: (i, j))
    has_res = res is not None
    has_norm = norm_gain is not None
    assert not has_norm or tn == N

    def kern(*refs):
        a_ref, b_ref = refs[0], refs[1]
        pos = 2
        res_ref = gain_ref = h_ref = None
        if has_res:
            res_ref, pos = refs[pos], pos + 1
        if has_norm:
            gain_ref, pos = refs[pos], pos + 1
        o_ref, pos = refs[pos], pos + 1
        if has_norm:
            h_ref = refs[pos]
        acc_ref = refs[-1] if nk > 1 else None
        p = lax.dot_general(a_ref[...].astype(BF), b_ref[...].astype(BF), (((dims[0],), (dims[1],)), ((), ())),
                            preferred_element_type=F32)

        def finish(v):
            if scale != 1.0:
                v = v * scale
            if has_res:
                v = res_ref[...] + v
            o_ref[...] = v.astype(o_ref.dtype)
            if has_norm:
                h_ref[...] = _rmsnorm(v, gain_ref[...]).astype(h_ref.dtype)

        if nk == 1:
            finish(p)
        else:
            k = pl.program_id(2)

            @pl.when(k == 0)
            def _():
                acc_ref[...] = p

            @pl.when(k > 0)
            def _():
                acc_ref[...] += p

            @pl.when(k == nk - 1)
            def _():
                finish(acc_ref[...])

    ins = [a, b] + ([res] if has_res else []) + ([norm_gain] if has_norm else [])
    in_specs = [a_spec, b_spec] + ([o_spec] if has_res else [])
    in_specs += [pl.BlockSpec((1, N), lambda i, j, k: (0, 0))] if has_norm else []
    out_sd = jax.ShapeDtypeStruct((M, N), out_dtype)
    return pl.pallas_call(
        kern, grid=(M // tm, N // tn, nk), in_specs=in_specs,
        out_specs=[o_spec, o_spec] if has_norm else o_spec,
        out_shape=[out_sd, jax.ShapeDtypeStruct((M, N), BF)] if has_norm else out_sd,
        scratch_shapes=[pltpu.VMEM((tm, tn), F32)] if nk > 1 else [],
        name=name,
        compiler_params=pltpu.CompilerParams(dimension_semantics=("parallel", "parallel", "arbitrary"),
                                             vmem_limit_bytes=VMEM_CAP_BYTES),
    )(*ins)


def _ffn_in_swiglu(name, h, w3, tm, tn):
    T, D = h.shape
    dff = w3.shape[2] // 2
    tm = min(tm, T)
    assert T % tm == 0 and dff % tn == 0
    nj = dff // tn

    def kern(h_ref, wg_ref, wu_ref, zg_ref, zu_ref, a_ref):
        hb = h_ref[...]
        g = jnp.dot(hb, wg_ref[...], preferred_element_type=F32).astype(BF)
        u = jnp.dot(hb, wu_ref[...], preferred_element_type=F32).astype(BF)
        zg_ref[...] = g
        zu_ref[...] = u
        a_ref[...] = (_silu(g.astype(F32)) * u.astype(F32)).astype(BF)

    o_spec = pl.BlockSpec((tm, tn), lambda i, j: (i, j))
    return pl.pallas_call(
        kern, grid=(T // tm, nj),
        in_specs=[pl.BlockSpec((tm, D), lambda i, j: (i, 0)),
                  pl.BlockSpec((None, D, tn), lambda i, j: (0, 0, j)),
                  pl.BlockSpec((None, D, tn), lambda i, j: (0, 0, j + nj))],
        out_specs=[o_spec, o_spec, o_spec],
        out_shape=[jax.ShapeDtypeStruct((T, dff), BF)] * 3, name=name,
        compiler_params=pltpu.CompilerParams(dimension_semantics=("parallel", "arbitrary"),
                                             vmem_limit_bytes=VMEM_CAP_BYTES),
    )(h, w3, w3)


def _ffn_da_swiglu(name, dxo, w3, zg, zu, tm):
    T, D = dxo.shape
    dff = w3.shape[1]
    tm = min(tm, T)
    assert T % tm == 0 and dff % 2 == 0
    hc = dff // 2

    def kern(d_ref, w_ref, g_ref, u_ref, dz_ref):
        db = (d_ref[...] * 0.5).astype(BF)
        for s in range(2):
            cols = slice(s * hc, (s + 1) * hc)
            da = lax.dot_general(db, w_ref[cols, :], (((1,), (1,)), ((), ())), preferred_element_type=F32)
            g = g_ref[:, cols].astype(F32)
            sg = 1.0 / (1.0 + jnp.exp(-g))
            gs = g * sg
            dab = da.astype(BF)
            dz_ref[:, cols] = (dab * u_ref[:, cols]) * (sg + gs * (1.0 - sg)).astype(BF)
            dz_ref[:, dff + s * hc:dff + (s + 1) * hc] = dab * gs.astype(BF)

    row = lambda w: pl.BlockSpec((tm, w), lambda i: (i, 0))
    return pl.pallas_call(
        kern, grid=(T // tm,),
        in_specs=[row(D), pl.BlockSpec((None, dff, D), lambda i: (0, 0, 0), pipeline_mode=pl.Buffered(1)), row(dff), row(dff)],
        out_specs=row(2 * dff), out_shape=jax.ShapeDtypeStruct((T, 2 * dff), BF), name=name,
        compiler_params=pltpu.CompilerParams(dimension_semantics=("arbitrary",), vmem_limit_bytes=VMEM_CAP_BYTES),
    )(dxo, w3, zg, zu)


def _mm_dh_rms(name, dz, w3, xin, g, dres, tm):
    T, K = dz.shape
    D = w3.shape[1]
    tm = min(tm, T)
    assert T % tm == 0

    def kern(dz_ref, w_ref, x_ref, g_ref, r_ref, dx_ref, dg_ref):
        dh = lax.dot_general(dz_ref[...], w_ref[...], (((1,), (1,)), ((), ())), preferred_element_type=F32)
        _, vjp = jax.vjp(_rmsnorm, x_ref[...], g_ref[...])
        dx, dg = vjp(dh)
        dx_ref[...] = dx + r_ref[...]

        @pl.when(pl.program_id(0) == 0)
        def _():
            dg_ref[...] = jnp.zeros(dg_ref.shape, F32)

        dg_ref[...] += dg

    row = lambda w: pl.BlockSpec((tm, w), lambda i: (i, 0))
    one = pl.BlockSpec((1, D), lambda i: (0, 0))
    return pl.pallas_call(
        kern, grid=(T // tm,),
        in_specs=[row(K), pl.BlockSpec((None, D, K), lambda i: (0, 0, 0), pipeline_mode=pl.Buffered(1)), row(D), one, row(D)],
        out_specs=[row(D), one], out_shape=[jax.ShapeDtypeStruct((T, D), F32), jax.ShapeDtypeStruct((1, D), F32)], name=name,
        compiler_params=pltpu.CompilerParams(dimension_semantics=("arbitrary",), vmem_limit_bytes=VMEM_CAP_BYTES),
    )(dz, w3, xin, g, dres)


def _mm_tn_pair(name, a, b, kind, c_arr, tq, tk, scale=1.0):
    T, M = a.shape
    _, N = b.shape
    tk = min(tk, T)
    assert T % tk == 0
    nk = T // tk
    if kind == "col":
        hm = M // 2
        assert N % tq == 0
        nq = N // tq
        tile = (hm, tq)
        a_spec = pl.BlockSpec((tk, hm), lambda h, q, k, c: (k, jnp.bitwise_xor(h, 1 - c[0])))
        b_spec = pl.BlockSpec((tk, tq), lambda h, q, k, c: (k, q))
        o_spec = pl.BlockSpec(tile, lambda h, q, k, c: (0, q * h))
        out_sd = (hm, N)
    else:
        hn = N // 2
        assert M % tq == 0
        nq = M // tq
        tile = (tq, hn)
        a_spec = pl.BlockSpec((tk, tq), lambda h, q, k, c: (k, q))
        b_spec = pl.BlockSpec((tk, hn), lambda h, q, k, c: (k, jnp.bitwise_xor(h, 1 - c[0])))
        o_spec = pl.BlockSpec(tile, lambda h, q, k, c: (q * h, 0))
        out_sd = (M, hn)

    def kern(c_ref, a_ref, b_ref, o_ref, acc, stage, recv, ssem, rsem):
        h, q, k = pl.program_id(0), pl.program_id(1), pl.program_id(2)
        x, y, c, _ = _place()
        p = lax.dot_general(a_ref[...].astype(BF), b_ref[...].astype(BF), (((0,), (0,)), ((), ())), preferred_element_type=F32)

        @pl.when(k == 0)
        def _():
            acc[...] = p

        @pl.when(k > 0)
        def _():
            acc[...] += p

        def send(slot, qq):
            return pltpu.make_async_remote_copy(src_ref=stage.at[slot], dst_ref=recv.at[qq], send_sem=ssem.at[slot],
                                                recv_sem=rsem.at[qq], device_id=(x, y, 1 - c), device_id_type=MESH)

        last = k == nk - 1

        @pl.when(jnp.logical_and(last, h == 0))
        def _():
            slot = q % 2

            @pl.when(q >= 2)
            def _():
                send(slot, q).wait_send()

            stage[slot] = (acc[...] * scale).astype(BF)
            send(slot, q).start()

        @pl.when(jnp.logical_and(last, h == 1))
        def _():
            @pl.when(q == 0)
            def _():
                for s in range(min(nq, 2)):
                    send(s, 0).wait_send()

            send(0, q).wait_recv()
            o_ref[...] = (acc[...] * scale + recv[q].astype(F32)).astype(o_ref.dtype)

    return pl.pallas_call(
        kern,
        grid_spec=pltpu.PrefetchScalarGridSpec(
            num_scalar_prefetch=1, grid=(2, nq, nk), in_specs=[a_spec, b_spec], out_specs=o_spec,
            scratch_shapes=[pltpu.VMEM(tile, F32), pltpu.VMEM((2,) + tile, BF), pltpu.VMEM((nq,) + tile, BF),
                            pltpu.SemaphoreType.DMA((2,)), pltpu.SemaphoreType.DMA((nq,))]),
        out_shape=jax.ShapeDtypeStruct(out_sd, BF), name=name,
        compiler_params=pltpu.CompilerParams(dimension_semantics=("arbitrary", "arbitrary", "arbitrary"),
                                             vmem_limit_bytes=VMEM_CAP_BYTES),
    )(c_arr, a, b)


def _kv_pieces(kv_ref):
    W = XA_HEADS * XA_DIM
    mk = [kv_ref[:, a * XA_DIM:(a + 1) * XA_DIM] for a in range(XA_HEADS)]
    mv = [kv_ref[:, W + a * XA_DIM:W + (a + 1) * XA_DIM] for a in range(XA_HEADS)]
    return mk, mv


HG_SUB = 4


def _hgrn_rows(z_ref):
    W = HG_HEADS * HG_DIM

    def piece(c, col, w):
        return z_ref[c * HG_CHUNK:(c + 1) * HG_CHUNK, col:col + w]

    zq = [[piece(c, h * HG_DIM, HG_DIM) for h in range(HG_HEADS)] for c in range(HG_SUB)]
    zf = z_ref[:, W:2 * W]
    zi =[[piece(c, 2 * W + h * HG_DIM, HG_DIM) for h in range(HG_HEADS)] for c in range(HG_SUB)]
    zg = [[piece(c, 3 * W + h * HG_DIM, HG_DIM) for h in range(HG_HEADS)] for c in range(HG_SUB)]
    zx = [z_ref[:, 4 * W + a * XA_DIM:4 * W + (a + 1) * XA_DIM] for a in range(XA_HEADS)]
    return zq, zf, zi, zg, zx


def _hgrn_steps(zq, zf, zi, zg, zx, lb3, gn, mk, mv, S):
    f, b = _hgrn_decays(zf, lb3)
    mix = []
    for c in range(HG_SUB):
        row, s_next = [], []
        rows = slice(c * HG_CHUNK, (c + 1) * HG_CHUNK)
        for h in range(HG_HEADS):
            cols = slice(h * HG_DIM, (h + 1) * HG_DIM)
            o, sn = _hgrn_head(zq[c][h], f[rows, cols], b[rows, cols], zi[c][h], zg[c][h], gn, S[h])
            row.append(o)
            s_next.append(sn)
        mix.append(row)
        S = s_next
    att = [_attention(zx[a], mk[a], mv[a]) for a in range(XA_HEADS)]
    return mix, att, S


def _hgrn_fwd2(z, lb_logits, gnorm, kv, bl, seq):
    T, zw = z.shape
    mem_len = kv.shape[0] // bl
    cat_w = HG_HEADS * HG_DIM + XA_HEADS * XA_DIM
    R = HG_SUB * HG_CHUNK
    nb = seq // R

    def kern(z_ref, lb_ref, gn_ref, kv_ref, cat_ref, st_ref, s_scr):
        @pl.when(pl.program_id(1) == 0)
        def _():
            s_scr[...] = jnp.zeros(s_scr.shape, F32)

        st_ref[...] = s_scr[...]
        zq, zf, zi, zg, zx = _hgrn_rows(z_ref)
        mk, mv = _kv_pieces(kv_ref)
        S = [s_scr[h] for h in range(HG_HEADS)]
        mix, att, s_new = _hgrn_steps(zq, zf, zi, zg, zx, lb_ref[...], gn_ref[...], mk, mv, S)
        for c in range(HG_SUB):
            for h in range(HG_HEADS):
                cat_ref[c * HG_CHUNK:(c + 1) * HG_CHUNK, h * HG_DIM:(h + 1) * HG_DIM] = mix[c][h].astype(cat_ref.dtype)
        for h in range(HG_HEADS):
            s_scr[h] = s_new[h]
        base = HG_HEADS * HG_DIM
        for a in range(XA_HEADS):
            cat_ref[:, base + a * XA_DIM:base + (a + 1) * XA_DIM] = att[a].astype(cat_ref.dtype)

    return pl.pallas_call(
        kern, grid=(bl, nb),
        in_specs=[pl.BlockSpec((R, zw), lambda b, n: (b * nb + n, 0)),
                  pl.BlockSpec(lb_logits.shape, lambda b, n: (0, 0)),
                  pl.BlockSpec(gnorm.shape, lambda b, n: (0, 0)),
                  pl.BlockSpec((mem_len, kv.shape[1]), lambda b, n: (b, 0))],
        out_specs=[pl.BlockSpec((R, cat_w), lambda b, n: (b * nb + n, 0)),
                   pl.BlockSpec((None, HG_HEADS, HG_DIM, HG_DIM), lambda b, n: (b * nb + n, 0, 0, 0))],
        out_shape=[jax.ShapeDtypeStruct((T, cat_w), BF),
                   jax.ShapeDtypeStruct((bl * nb, HG_HEADS, HG_DIM, HG_DIM), F32)],
        scratch_shapes=[pltpu.VMEM((HG_HEADS, HG_DIM, HG_DIM), F32)],
        name="hgrn_fwd",
        compiler_params=pltpu.CompilerParams(dimension_semantics=("arbitrary", "arbitrary"), vmem_limit_bytes=VMEM_CAP_BYTES),
    )(z, lb_logits, gnorm, kv)


def _hgrn_bwd2(z, dcat, stash, lb_logits, gnorm, kv, bl, seq):
    T, zw = z.shape
    mem_len = kv.shape[0] // bl
    cat_w = dcat.shape[1]
    R = HG_SUB * HG_CHUNK
    nb = seq // R

    def kern(z_ref, dc_ref, st_ref, lb_ref, gn_ref, kv_ref, dz_ref, dkv_ref, dlb_ref, dgn_ref, ds_scr):
        first = jnp.logical_and(pl.program_id(0) == 0, pl.program_id(1) == 0)

        @pl.when(pl.program_id(1) == 0)
        def _():
            ds_scr[...] = jnp.zeros(ds_scr.shape, F32)
            dkv_ref[...] = jnp.zeros(dkv_ref.shape, F32)

        @pl.when(first)
        def _():
            dlb_ref[...] = jnp.zeros(dlb_ref.shape, F32)
            dgn_ref[...] = jnp.zeros(dgn_ref.shape, F32)

        zq, zf, zi, zg, zx = _hgrn_rows(z_ref)
        mk, mv = _kv_pieces(kv_ref)
        S = [st_ref[h] for h in range(HG_HEADS)]
        _, vjp = jax.vjp(_hgrn_steps, zq, zf, zi, zg, zx, lb_ref[...], gn_ref[...], mk, mv, S)
        d_mix = [[dc_ref[c * HG_CHUNK:(c + 1) * HG_CHUNK, h * HG_DIM:(h + 1) * HG_DIM] for h in range(HG_HEADS)]
                 for c in range(HG_SUB)]
        base = HG_HEADS * HG_DIM
        d_att = [dc_ref[:, base + a * XA_DIM:base + (a + 1) * XA_DIM] for a in range(XA_HEADS)]
        d_s = [ds_scr[h] for h in range(HG_HEADS)]
        dzq, dzf, dzi, dzg, dzx, dlb3, dgn, dmk, dmv, dS = vjp((d_mix, d_att, d_s))
        W = HG_HEADS * HG_DIM
        dz_ref[:, W:2 * W] = dzf.astype(dz_ref.dtype)
        for c in range(HG_SUB):
            rows = slice(c * HG_CHUNK, (c + 1) * HG_CHUNK)
            for h in range(HG_HEADS):
                for k, part in ((0, dzq), (2, dzi), (3, dzg)):
                    dz_ref[rows, k * W + h * HG_DIM:k * W + (h + 1) * HG_DIM] = part[c][h].astype(dz_ref.dtype)
        for h in range(HG_HEADS):
            ds_scr[h] = dS[h]
        dlb_ref[...] += dlb3
        dgn_ref[...] += dgn
        KW = XA_HEADS * XA_DIM
        for a in range(XA_HEADS):
            dz_ref[:, 4 * W + a * XA_DIM:4 * W + (a + 1) * XA_DIM] = dzx[a].astype(dz_ref.dtype)
            dkv_ref[:, a * XA_DIM:(a + 1) * XA_DIM] += dmk[a]
            dkv_ref[:, KW + a * XA_DIM:KW + (a + 1) * XA_DIM] += dmv[a]

    rev = lambda b, n: (b * nb + (nb - 1 - n), 0)
    return pl.pallas_call(
        kern, grid=(bl, nb),
        in_specs=[pl.BlockSpec((R, zw), rev),
                  pl.BlockSpec((R, cat_w), rev),
                  pl.BlockSpec((None, HG_HEADS, HG_DIM, HG_DIM), lambda b, n: (b * nb + (nb - 1 - n), 0, 0, 0)),
                  pl.BlockSpec(lb_logits.shape, lambda b, n: (0, 0)),
                  pl.BlockSpec(gnorm.shape, lambda b, n: (0, 0)),
                  pl.BlockSpec((mem_len, kv.shape[1]), lambda b, n: (b, 0))],
        out_specs=[pl.BlockSpec((R, zw), rev),
                   pl.BlockSpec((mem_len, kv.shape[1]), lambda b, n: (b, 0)),
                   pl.BlockSpec(lb_logits.shape, lambda b, n: (0, 0)),
                   pl.BlockSpec(gnorm.shape, lambda b, n: (0, 0))],
        out_shape=[jax.ShapeDtypeStruct((T, zw), BF), jax.ShapeDtypeStruct(kv.shape, F32),
                   jax.ShapeDtypeStruct(lb_logits.shape, F32), jax.ShapeDtypeStruct(gnorm.shape, F32)],
        scratch_shapes=[pltpu.VMEM((HG_HEADS, HG_DIM, HG_DIM), F32)],
        name="hgrn_bwd",
        compiler_params=pltpu.CompilerParams(dimension_semantics=("arbitrary", "arbitrary"), vmem_limit_bytes=VMEM_CAP_BYTES),
    )(z, dcat, stash, lb_logits, gnorm, kv)


GM_SUB = 2


def _gmlp_pieces(z_ref):
    W = GM_GROUPS * GM_GROUP_DIM
    zu = [z_ref[:, g * GM_GROUP_DIM:(g + 1) * GM_GROUP_DIM] for g in range(GM_GROUPS)]
    zv = [z_ref[:, W + g * GM_GROUP_DIM:W + (g + 1) * GM_GROUP_DIM] for g in range(GM_GROUPS)]
    zx = [z_ref[:, 2 * W + a * XA_DIM:2 * W + (a + 1) * XA_DIM] for a in range(XA_HEADS)]
    return zu, zv, zx


def _gmlp_params(lng_ref, lnb_ref, ws_ref, bs_ref):
    lng = [lng_ref[:, g * GM_GROUP_DIM:(g + 1) * GM_GROUP_DIM] for g in range(GM_GROUPS)]
    lnb = [lnb_ref[:, g * GM_GROUP_DIM:(g + 1) * GM_GROUP_DIM] for g in range(GM_GROUPS)]
    ws = [ws_ref[g] for g in range(GM_GROUPS)]
    bs = [bs_ref[g:g + 1, :] for g in range(GM_GROUPS)]
    return lng, lnb, ws, bs


def _gmlp_fwd(z, ln_g, ln_b, w_s, b_s, kv, bl, nc):
    T, zw = z.shape
    mem_len = kv.shape[0] // bl
    cat_w = GM_GROUPS * GM_GROUP_DIM + XA_HEADS * XA_DIM

    assert nc % GM_SUB == 0
    nc = nc // GM_SUB
    R = GM_SUB * GM_CHUNK

    def kern(z_ref, lng_ref, lnb_ref, ws_ref, bs_ref, kv_ref, cat_ref):
        lng, lnb, ws, bs = _gmlp_params(lng_ref, lnb_ref, ws_ref, bs_ref)
        mk, mv = _kv_pieces(kv_ref)
        for c in range(GM_SUB):
            rows = pl.ds(c * GM_CHUNK, GM_CHUNK)
            zu, zv, zx = _gmlp_pieces(z_ref.at[rows])
            out = cat_ref.at[rows]
            outs = _gmlp_block(zu, zv, zx, lng, lnb, ws, bs, mk, mv)
            for g in range(GM_GROUPS):
                out[:, g * GM_GROUP_DIM:(g + 1) * GM_GROUP_DIM] = outs[g].astype(cat_ref.dtype)
            base = GM_GROUPS * GM_GROUP_DIM
            for a in range(XA_HEADS):
                out[:, base + a * XA_DIM:base + (a + 1) * XA_DIM] = outs[GM_GROUPS + a].astype(cat_ref.dtype)

    full2 = lambda b, n: (0, 0)
    return pl.pallas_call(
        kern, grid=(bl, nc),
        in_specs=[pl.BlockSpec((R, zw), lambda b, n: (b * nc + n, 0)),
                  pl.BlockSpec(ln_g.shape, full2), pl.BlockSpec(ln_b.shape, full2),
                  pl.BlockSpec(w_s.shape, lambda b, n: (0, 0, 0)), pl.BlockSpec(b_s.shape, full2),
                  pl.BlockSpec((mem_len, kv.shape[1]), lambda b, n: (b, 0))],
        out_specs=pl.BlockSpec((R, cat_w), lambda b, n: (b * nc + n, 0)),
        out_shape=jax.ShapeDtypeStruct((T, cat_w), BF),
        name="gmlp_fwd",
        compiler_params=pltpu.CompilerParams(dimension_semantics=("arbitrary", "arbitrary"), vmem_limit_bytes=VMEM_CAP_BYTES),
    )(z, ln_g, ln_b, w_s, b_s, kv)


def _gmlp_bwd(z, dcat, ln_g, ln_b, w_s, b_s, kv, bl, nc):
    T, zw = z.shape
    mem_len = kv.shape[0] // bl
    cat_w = dcat.shape[1]
    assert nc % GM_SUB == 0
    nc = nc // GM_SUB

    def kern(z_ref, dc_ref, lng_ref, lnb_ref, ws_ref, bs_ref, kv_ref,
             dz_ref, dkv_ref, dlng_ref, dlnb_ref, dws_ref, dbs_ref):
        first = jnp.logical_and(pl.program_id(0) == 0, pl.program_id(1) == 0)

        @pl.when(pl.program_id(1) == 0)
        def _():
            dkv_ref[...] = jnp.zeros(dkv_ref.shape, F32)

        @pl.when(first)
        def _():
            dlng_ref[...] = jnp.zeros(dlng_ref.shape, F32)
            dlnb_ref[...] = jnp.zeros(dlnb_ref.shape, F32)
            dws_ref[...] = jnp.zeros(dws_ref.shape, F32)
            dbs_ref[...] = jnp.zeros(dbs_ref.shape, F32)

        lng, lnb, ws, bs = _gmlp_params(lng_ref, lnb_ref, ws_ref, bs_ref)
        mk, mv = _kv_pieces(kv_ref)
        W = GM_GROUPS * GM_GROUP_DIM
        KW = XA_HEADS * XA_DIM
        for c in range(GM_SUB):
            rows = pl.ds(c * GM_CHUNK, GM_CHUNK)
            zu, zv, zx = _gmlp_pieces(z_ref.at[rows])
            dc, dz = dc_ref.at[rows], dz_ref.at[rows]
            _, vjp = jax.vjp(_gmlp_block, zu, zv, zx, lng, lnb, ws, bs, mk, mv)
            d_outs = [dc[:, g * GM_GROUP_DIM:(g + 1) * GM_GROUP_DIM] for g in range(GM_GROUPS)]
            d_outs += [dc[:, W + a * XA_DIM:W + (a + 1) * XA_DIM] for a in range(XA_HEADS)]
            dzu, dzv, dzx, dlng, dlnb, dws, dbs, dmk, dmv = vjp(d_outs)
            for g in range(GM_GROUPS):
                sl = slice(g * GM_GROUP_DIM, (g + 1) * GM_GROUP_DIM)
                dz[:, sl] = dzu[g].astype(dz_ref.dtype)
                dz[:, W + g * GM_GROUP_DIM:W + (g + 1) * GM_GROUP_DIM] = dzv[g].astype(dz_ref.dtype)
                dlng_ref[:, sl] += dlng[g]
                dlnb_ref[:, sl] += dlnb[g]
                dws_ref[g] += dws[g]
                dbs_ref[g:g + 1, :] += dbs[g]
            for a in range(XA_HEADS):
                dz[:, 2 * W + a * XA_DIM:2 * W + (a + 1) * XA_DIM] = dzx[a].astype(dz_ref.dtype)
                dkv_ref[:, a * XA_DIM:(a + 1) * XA_DIM] += dmk[a]
                dkv_ref[:, KW + a * XA_DIM:KW + (a + 1) * XA_DIM] += dmv[a]

    full2 = lambda b, n: (0, 0)
    full3 = lambda b, n: (0, 0, 0)
    blk = lambda b, n: (b * nc + n, 0)
    return pl.pallas_call(
        kern, grid=(bl, nc),
        in_specs=[pl.BlockSpec((GM_SUB * GM_CHUNK, zw), blk), pl.BlockSpec((GM_SUB * GM_CHUNK, cat_w), blk),
                  pl.BlockSpec(ln_g.shape, full2), pl.BlockSpec(ln_b.shape, full2),
                  pl.BlockSpec(w_s.shape, full3), pl.BlockSpec(b_s.shape, full2),
                  pl.BlockSpec((mem_len, kv.shape[1]), lambda b, n: (b, 0))],
        out_specs=[pl.BlockSpec((GM_SUB * GM_CHUNK, zw), blk),
                   pl.BlockSpec((mem_len, kv.shape[1]), lambda b, n: (b, 0)),
                   pl.BlockSpec(ln_g.shape, full2), pl.BlockSpec(ln_b.shape, full2),
                   pl.BlockSpec(w_s.shape, full3), pl.BlockSpec(b_s.shape, full2)],
        out_shape=[jax.ShapeDtypeStruct((T, zw), BF), jax.ShapeDtypeStruct(kv.shape, F32),
                   jax.ShapeDtypeStruct(ln_g.shape, F32), jax.ShapeDtypeStruct(ln_b.shape, F32),
                   jax.ShapeDtypeStruct(w_s.shape, F32), jax.ShapeDtypeStruct(b_s.shape, F32)],
        name="gmlp_bwd",
        compiler_params=pltpu.CompilerParams(dimension_semantics=("arbitrary", "arbitrary"), vmem_limit_bytes=VMEM_CAP_BYTES),
    )(z, dcat, ln_g, ln_b, w_s, b_s, kv)


def _place():
    x, y, c = lax.axis_index("x"), lax.axis_index("y"), lax.axis_index("c")
    chips = [(1 - x, y), (x, 1 - y), (1 - x, 1 - y)]
    return x, y, c, chips


def _half(ref, kind, e):
    if kind == "col":
        n = ref.shape[1] // 2
        return ref.at[:, pl.ds(pl.multiple_of(e * n, n), n), :]
    n = ref.shape[2] // 2
    return ref.at[:, :, pl.ds(pl.multiple_of(e * n, n), n)]


def _slot(ref, kind, j, n):
    if kind == "col":
        return ref.at[:, :, pl.ds(pl.multiple_of(j * n, n), n)]
    return ref.at[:, pl.ds(pl.multiple_of(j * n, n), n), :]


BF16_TILE_ROWS = 16
AG_DIRECT_SIXTEENTHS = 3


def _allgather_seq(name, items, cid):
    nt = len(items)
    kinds = [k for (_, k, _) in items]
    slot_kind = ["row" if k == "row" else "col" for k in kinds]
    out_type = []
    for s, k, l in items:
        L, r, c = s.shape
        lo = L if l is None else 1
        out_type.append(jax.ShapeDtypeStruct((lo, 4 * r, c) if k == "row" else (lo, r, 4 * c), s.dtype))

    def part(ref, t, e):
        return ref if kinds[t] == "vec" else _half(ref, kinds[t], e)

    def split(half):
        rows = half.shape[1]
        direct = rows * AG_DIRECT_SIXTEENTHS // 16 // BF16_TILE_ROWS * BF16_TILE_ROWS
        return half.at[:, pl.ds(0, rows - direct), :], half.at[:, pl.ds(rows - direct, direct), :]

    def body(*refs):
        sh = [refs[t] if items[t][2] is None else refs[t].at[pl.ds(items[t][2], 1)] for t in range(nt)]
        full = refs[nt:2 * nt]
        s_ici, r_ici, s_far, r_far, s_d2d, r_d2d = refs[2 * nt:]
        x, y, c, chips = _place()
        own = 2 * x + y
        sibling = (x, y, 1 - c)
        barrier = pltpu.get_barrier_semaphore()
        for peer in [(px, py, pc) for (px, py) in chips for pc in (0, 1)] + [sibling]:
            pl.semaphore_signal(barrier, inc=1, device_id=peer, device_id_type=MESH)
        pl.semaphore_wait(barrier, 7)
        width = [sh[t].shape[1] if kinds[t] == "row" else sh[t].shape[2] for t in range(nt)]
        sent = []
        for t in range(nt):
            for p, (px, py) in enumerate(chips):
                src, dst = part(sh[t], t, c), part(_slot(full[t], slot_kind[t], own, width[t]), t, c)
                cp = pltpu.make_async_remote_copy(
                    src_ref=src, dst_ref=dst, send_sem=s_ici.at[t, p], recv_sem=r_ici.at[t, p], device_id=(px, py, c),
                    device_id_type=MESH)
                cp.start()
                sent.append(cp)
                if kinds[t] == "vec":
                    continue
                far = pltpu.make_async_remote_copy(
                    src_ref=split(src)[1], dst_ref=split(dst)[1], send_sem=s_far.at[t, p], recv_sem=r_far.at[t, p],
                    device_id=(px, py, 1 - c), device_id_type=MESH)
                far.start()
                sent.append(far)
        for t in range(nt):
            for p, (px, py) in enumerate(chips):
                landed = part(_slot(full[t], slot_kind[t], 2 * px + py, width[t]), t, c)
                pltpu.make_async_remote_copy(
                    src_ref=landed, dst_ref=landed, send_sem=s_ici.at[t, p], recv_sem=r_ici.at[t, p],
                    device_id=(px, py, c), device_id_type=MESH).wait_recv()
                if kinds[t] == "vec":
                    continue
                fw = pltpu.make_async_remote_copy(
                    src_ref=split(landed)[0], dst_ref=split(landed)[0], send_sem=s_d2d.at[t, p], recv_sem=r_d2d.at[t, p],
                    device_id=sibling, device_id_type=MESH)
                fw.start()
                sent.append(fw)
        for t in range(nt):
            if kinds[t] == "vec":
                continue
            for p, (px, py) in enumerate(chips):
                forwarded, direct = split(_half(_slot(full[t], kinds[t], 2 * px + py, width[t]), kinds[t], 1 - c))
                pltpu.make_async_remote_copy(
                    src_ref=forwarded, dst_ref=forwarded, send_sem=s_d2d.at[t, p], recv_sem=r_d2d.at[t, p],
                    device_id=sibling, device_id_type=MESH).wait_recv()
                pltpu.make_async_remote_copy(
                    src_ref=direct, dst_ref=direct, send_sem=s_far.at[t, p], recv_sem=r_far.at[t, p],
                    device_id=(px, py, 1 - c), device_id_type=MESH).wait_recv()
        for cp in sent:
            cp.wait_send()

    sems = pltpu.SemaphoreType.DMA
    return pl.kernel(
        body, out_type=out_type, mesh=plsc.ScalarSubcoreMesh(axis_name="seq", num_cores=1),
        scratch_types=[sems((nt, 3)), sems((nt, 3)), sems((nt, 3)), sems((nt, 3)), sems((nt, 3)), sems((nt, 3))],
        compiler_params=pltpu.CompilerParams(collective_id=cid), name=name,
    )(*[s for (s, _, _) in items])


def _place_own(name, full, shard, kind, layer, chip_arr, after):
    lo, r, c = (shard.shape[0] if layer is None else 1,) + shard.shape[1:]
    first = 0 if layer is None else layer
    tr = _pick(r, 512)
    nr = r // tr

    def body(chip_ref, s_ref, f_ref, after_ref, o_ref):
        o_ref[...] = s_ref[...]

    if kind == "row":
        out_map = lambda i, j, chip: (i, chip[0] * nr + j, 0)
    else:
        out_map = lambda i, j, chip: (i, j, chip[0])
    return pl.pallas_call(
        body, out_shape=jax.ShapeDtypeStruct(full.shape, full.dtype),
        grid_spec=pltpu.PrefetchScalarGridSpec(
            num_scalar_prefetch=1, grid=(lo, nr),
            in_specs=[pl.BlockSpec((1, tr, c), lambda i, j, chip: (i + first, j, 0)), pl.BlockSpec(memory_space=pl.ANY),
                      pl.BlockSpec(memory_space=pl.ANY)],
            out_specs=pl.BlockSpec((1, tr, c), out_map)),
        input_output_aliases={2: 0},
        compiler_params=pltpu.CompilerParams(dimension_semantics=("parallel", "parallel"), vmem_limit_bytes=VMEM_CAP_BYTES),
        name=name,
    )(chip_arr, shard, full, after)


def _slot2(ref, kind, j, n):
    if kind == "col":
        return ref.at[:, pl.ds(pl.multiple_of(j * n, n), n)]
    return ref.at[pl.ds(pl.multiple_of(j * n, n), n), :]


def _rs_chips_seq(name, parts, kinds, cid):
    nm = len(parts)
    out_type = []
    for g, k in zip(parts, kinds):
        r, c = g.shape
        ps = (r, c // 4) if k == "col" else (r // 4, c)
        out_type += [jax.ShapeDtypeStruct(ps, BF), jax.ShapeDtypeStruct((3,) + ps, BF)]

    def body(*refs):
        g = refs[:nm]
        outs = refs[nm:3 * nm]
        loc, ssem, rsem = refs[3 * nm:]
        x, y, c, chips = _place()
        own = 2 * x + y
        barrier = pltpu.get_barrier_semaphore()
        for (px, py) in chips:
            pl.semaphore_signal(barrier, inc=1, device_id=(px, py, c), device_id_type=MESH)
        pl.semaphore_wait(barrier, 3)
        cps = []
        for m in range(nm):
            k = kinds[m]
            own_o, got_o = outs[2 * m], outs[2 * m + 1]
            n = g[m].shape[1] // 4 if k == "col" else g[m].shape[0] // 4
            lc = pltpu.make_async_copy(_slot2(g[m], k, own, n), own_o, loc.at[m])
            lc.start()
            cps.append(lc)
            for p, (px, py) in enumerate(chips):
                cp = pltpu.make_async_remote_copy(
                    src_ref=_slot2(g[m], k, 2 * px + py, n), dst_ref=got_o.at[p],
                    send_sem=ssem.at[m, p], recv_sem=rsem.at[m, p], device_id=(px, py, c), device_id_type=MESH)
                cp.start()
                cps.append(cp)
        for cp in cps:
            cp.wait()

    return pl.kernel(
        body, out_type=out_type, mesh=plsc.ScalarSubcoreMesh(axis_name="seq", num_cores=1),
        scratch_types=[pltpu.SemaphoreType.DMA((nm,)), pltpu.SemaphoreType.DMA((nm, 3)), pltpu.SemaphoreType.DMA((nm, 3))],
        compiler_params=pltpu.CompilerParams(collective_id=cid), name=name,
    )(*parts)


def _finish_share(name, owns, gots, kind, c_arr):
    L = len(owns)
    r, c = owns[0].shape
    tr = _pick(r, 128 if kind == "col" else 256)
    nb = r // tr
    nq = L * nb

    def chunk_of(l):
        return lambda h, q: jnp.clip(q * (1 - h) + (nq - 1) * h - l * nb, 0, nb - 1)

    ins, in_specs = [], []
    for l in range(L):
        at = chunk_of(l)
        ins += [owns[l], gots[l].reshape(3 * r, c), gots[l].reshape(3 * r, c), gots[l].reshape(3 * r, c)]
        in_specs.append(pl.BlockSpec((tr, c), functools.partial(lambda h, q, cc, at: (at(h, q), 0), at=at)))
        in_specs += [pl.BlockSpec((tr, c), functools.partial(lambda h, q, cc, at, p: (p * nb + at(h, q), 0), at=at, p=p))
                     for p in range(3)]
    if kind == "col":
        out_sd = (L, 2, r, c)
        o_spec = pl.BlockSpec((None, 2, tr, c), lambda h, q, cc: ((q * h) // nb, 0, (q * h) % nb, 0))
    else:
        out_sd = (L * r, 2 * c)
        o_spec = pl.BlockSpec((tr, 2 * c), lambda h, q, cc: (q * h, 0))

    def kern(c_ref, *refs):
        in_refs = refs[:4 * L]
        out_ref, mine, recv, ssem, rsem = refs[4 * L:]
        h, q = pl.program_id(0), pl.program_id(1)
        x, y, cc, _ = _place()

        def swap(qq):
            return pltpu.make_async_remote_copy(src_ref=mine.at[qq], dst_ref=recv.at[qq], send_sem=ssem.at[qq],
                                                recv_sem=rsem.at[qq], device_id=(x, y, 1 - cc), device_id_type=MESH)

        for l in range(L):
            @pl.when(jnp.logical_and(h == 0, q // nb == l))
            def _(l=l):
                o_ref, g0, g1, g2 = in_refs[4 * l:4 * l + 4]
                mine[q] = ((o_ref[...].astype(F32) + g0[...].astype(F32)) + g1[...].astype(F32)) + g2[...].astype(F32)
                swap(q).start()

        @pl.when(h == 1)
        def _():
            swap(q).wait()
            a, b = mine[q], recv[q]
            first = c_ref[0] == 0
            lo, hi = jnp.where(first, a, b), jnp.where(first, b, a)
            if kind == "col":
                out_ref[0] = lo
                out_ref[1] = hi
            else:
                out_ref[:, :c] = lo
                out_ref[:, c:] = hi

    full = pl.pallas_call(
        kern,
        grid_spec=pltpu.PrefetchScalarGridSpec(
            num_scalar_prefetch=1, grid=(2, nq), in_specs=in_specs, out_specs=o_spec,
            scratch_shapes=[pltpu.VMEM((nq, tr, c), F32), pltpu.VMEM((nq, tr, c), F32),
                            pltpu.SemaphoreType.DMA((nq,)), pltpu.SemaphoreType.DMA((nq,))]),
        out_shape=jax.ShapeDtypeStruct(out_sd, F32), name=name,
        compiler_params=pltpu.CompilerParams(dimension_semantics=("arbitrary", "arbitrary"),
                                             vmem_limit_bytes=VMEM_CAP_BYTES),
    )(c_arr, *ins)
    return full.reshape(L, 2 * r, c) if kind == "col" else full.reshape(L, r, 2 * c)


def _small_allreduce(buf, name):
    R = buf.shape[0]
    assert R % 16 == 0
    h = R // 2

    def body(x_ref, o_ref, sib, csum, got, s_a, r_a, s_b, r_b, s_c, r_c):
        x, y, c, chips = _place()
        sibling = (x, y, 1 - c)
        own = 2 * x + y
        swap = pltpu.make_async_remote_copy(src_ref=x_ref, dst_ref=sib, send_sem=s_a, recv_sem=r_a,
                                            device_id=sibling, device_id_type=MESH)
        swap.start()
        swap.wait()
        a, b = x_ref[...], sib[...]
        south = c == 0
        csum[...] = jnp.where(south, a, b) + jnp.where(south, b, a)
        lo = pl.multiple_of(c * h, 8)
        mine = csum.at[pl.ds(lo, h)]
        got[own] = csum[pl.ds(lo, h)]
        sends = []
        for p, (px, py) in enumerate(chips):
            cp = pltpu.make_async_remote_copy(src_ref=mine, dst_ref=got.at[own], send_sem=s_b.at[p], recv_sem=r_b.at[p],
                                              device_id=(px, py, c), device_id_type=MESH)
            cp.start()
            sends.append(cp)
        for cp in sends:
            cp.wait()
        o_ref[pl.ds(lo, h)] = ((got[0] + got[1]) + got[2]) + got[3]
        done = o_ref.at[pl.ds(lo, h)]
        back = pltpu.make_async_remote_copy(src_ref=done, dst_ref=done, send_sem=s_c, recv_sem=r_c,
                                            device_id=sibling, device_id_type=MESH)
        back.start()
        back.wait_send()
        other = o_ref.at[pl.ds(pl.multiple_of((1 - c) * h, 8), h)]
        pltpu.make_async_remote_copy(src_ref=other, dst_ref=other, send_sem=s_c, recv_sem=r_c,
                                     device_id=sibling, device_id_type=MESH).wait_recv()

    vm = pl.BlockSpec(memory_space=pltpu.VMEM)
    return pl.pallas_call(
        body, out_shape=jax.ShapeDtypeStruct(buf.shape, F32), in_specs=[vm], out_specs=vm,
        scratch_shapes=[pltpu.VMEM((R, LANES), F32), pltpu.VMEM((R, LANES), F32), pltpu.VMEM((4, h, LANES), F32),
                        pltpu.SemaphoreType.DMA, pltpu.SemaphoreType.DMA, pltpu.SemaphoreType.DMA((3,)),
                        pltpu.SemaphoreType.DMA((3,)), pltpu.SemaphoreType.DMA, pltpu.SemaphoreType.DMA],
        name=name,
        compiler_params=pltpu.CompilerParams(vmem_limit_bytes=VMEM_CAP_BYTES),
    )(buf)


PACK_TILE_ROWS = 8


def _item_rows(shape):
    n = 1
    for d in shape:
        n *= d
    return -(-n // (PACK_TILE_ROWS * LANES)) * PACK_TILE_ROWS


def _pack(arrs, rows_total):
    buf = jnp.zeros((rows_total, LANES), F32)
    r = 0
    for a in arrs:
        f = a.reshape(-1).astype(F32)
        nr = _item_rows(a.shape)
        block = jnp.pad(f, (0, nr * LANES - f.shape[0])).reshape(nr, LANES)
        buf = lax.dynamic_update_slice(buf, block, (r, 0))
        r += nr
    return buf


def _unpack(buf, shapes):
    out, r = [], 0
    for s in shapes:
        n = 1
        for d in s:
            n *= d
        nr = _item_rows(s)
        out.append(buf[r:r + nr].reshape(-1)[:n].reshape(s))
        r += nr
    return out


def _rows_needed(shapes):
    return -(-sum(_item_rows(s) for s in shapes) // (2 * PACK_TILE_ROWS)) * (2 * PACK_TILE_ROWS)


def _two_rows(a, b):
    out = jnp.zeros((2, a.shape[1]), a.dtype)
    return lax.dynamic_update_slice(lax.dynamic_update_slice(out, a, (0, 0)), b, (1, 0))


def _adam(w, g, m, v):
    m = ADAM_B1 * m + (1.0 - ADAM_B1) * g
    v = ADAM_B2 * v + (1.0 - ADAM_B2) * jnp.square(g)
    m_hat = m / (1.0 - ADAM_B1 ** ADAM_STEP)
    v_hat = v / (1.0 - ADAM_B2 ** ADAM_STEP)
    delta = -ADAM_LR * (m_hat / (jnp.sqrt(v_hat) + ADAM_EPS) + ADAM_WD * w)
    return delta, m, v


def _adam_call(name, w2, g2, m2, v2, tr):
    width = w2.shape[1]
    return _rowcall(name, lambda rv, cv: (list(_adam(*rv)), []), [(w2, 0, width), (g2, 0, width), (m2, 0, width), (v2, 0, width)],
                    [], [(width, F32)] * 3, [], tr)


def _adam_pass(name, w2, g2, m2, v2, tr):
    rows, width = w2.shape
    tr = _pick(rows, tr)

    def kern(w_ref, g_ref, m_ref, v_ref, g_whole, g_same, d_ref, mo_ref, vo_ref):
        d_ref[...], mo_ref[...], vo_ref[...] = _adam(w_ref[...], g_ref[...], m_ref[...], v_ref[...])

    blk = pl.BlockSpec((tr, width), lambda i: (i, 0))
    whole = pl.BlockSpec(memory_space=pl.ANY)
    return pl.pallas_call(
        kern, grid=(rows // tr,), in_specs=[blk, blk, blk, blk, whole], out_specs=[whole, blk, blk, blk],
        out_shape=[jax.ShapeDtypeStruct((rows, width), F32)] * 4, input_output_aliases={4: 0}, name=name,
        compiler_params=pltpu.CompilerParams(dimension_semantics=("arbitrary",), vmem_limit_bytes=VMEM_CAP_BYTES),
    )(w2, g2, m2, v2, g2)


def kernel(x, mem, mem_norm, lb_logits, ffn1_norm, ffn1_w_in, ffn1_w_out, mix_norm, mem_w_kv, hgrn_w_in, hgrn_gnorm, hgrn_w_out, gmlp_w_in, gmlp_ln_g, gmlp_ln_b, gmlp_w_s, gmlp_b_s, gmlp_w_out, ffn2_norm, ffn2_w_in, ffn2_w_out, final_norm, loss_target, m_mem_norm, m_lb_logits, m_ffn1_norm, m_ffn1_w_in, m_ffn1_w_out, m_mix_norm, m_mem_w_kv, m_hgrn_w_in, m_hgrn_gnorm, m_hgrn_w_out, m_gmlp_w_in, m_gmlp_ln_g, m_gmlp_ln_b, m_gmlp_w_s, m_gmlp_b_s, m_gmlp_w_out, m_ffn2_norm, m_ffn2_w_in, m_ffn2_w_out, m_final_norm, v_mem_norm, v_lb_logits, v_ffn1_norm, v_ffn1_w_in, v_ffn1_w_out, v_mix_norm, v_mem_w_kv, v_hgrn_w_in, v_hgrn_gnorm, v_hgrn_w_out, v_gmlp_w_in, v_gmlp_ln_g, v_gmlp_ln_b, v_gmlp_w_s, v_gmlp_b_s, v_gmlp_w_out, v_ffn2_norm, v_ffn2_w_in, v_ffn2_w_out, v_final_norm):
    bl, seq, D = x.shape
    T = bl * seq
    mem_len = mem.shape[1]
    chip = 2 * lax.axis_index("x") + lax.axis_index("y")
    c_arr = lax.axis_index("c").astype(jnp.int32).reshape(1)
    chip_arr = chip.astype(jnp.int32).reshape(1)
    TR = 1024

    big = [("ffn1_w_in", ffn1_w_in, "col"), ("ffn1_w_out", ffn1_w_out, "row"), ("mem_w_kv", mem_w_kv, "col"),
           ("hgrn_w_in", hgrn_w_in, "col"), ("hgrn_w_out", hgrn_w_out, "row"), ("gmlp_w_in", gmlp_w_in, "col"),
           ("gmlp_w_out", gmlp_w_out, "row"), ("ffn2_w_in", ffn2_w_in, "col"), ("ffn2_w_out", ffn2_w_out, "row")]
    kinds = [k for (_, _, k) in big]
    shards_bf = []
    for nm, w, _ in big:
        L, r, c = w.shape
        (wb,) = _rowcall("cast_" + nm, lambda rv, cv: ([rv[0]], []), [(w.reshape(L * r, c), 0, c)], [], [(c, BF)], [], 512)
        shards_bf.append(wb.reshape(L, r, c))
    sb = dict(zip([nm for (nm, _, _) in big], shards_bf))
    groups = [[("ffn1_w_in", 0)], [("ffn1_w_out", 0)], [("hgrn_w_in", None)], [("mem_w_kv", None)], [("hgrn_w_out", None)],
              [("ffn2_w_in", 0), ("ffn2_w_out", 0), ("gmlp_ln_g", None), ("gmlp_ln_b", None)],
              [("ffn1_w_in", 1), ("ffn1_w_out", 1)],
              [("gmlp_w_in", None), ("gmlp_w_out", None)],
              [("ffn2_w_in", 1), ("ffn2_w_out", 1)]]
    kind_of = {nm: k for (nm, _, k) in big}
    for nm, vec in (("gmlp_ln_g", gmlp_ln_g), ("gmlp_ln_b", gmlp_ln_b)):
        sb[nm] = vec.reshape(1, 1, -1)
        kind_of[nm] = "vec"
    gathered = {nm: [None, None] for nm in ("ffn1_w_in", "ffn1_w_out", "ffn2_w_in", "ffn2_w_out")}
    others = {}
    for gi, grp in enumerate(groups):
        outs = _allgather_seq("gather_%d" % gi, [(sb[nm], kind_of[nm], l) for (nm, l) in grp], gi)
        for (nm, l), o in zip(grp, outs):
            others[(nm, l)] = o

    def whole(nm, l, after):
        full = _place_own("own_%s_%d" % (nm, l or 0), others[(nm, l)], sb[nm], "row" if kind_of[nm] == "row" else "col", l,
                          chip_arr, after)
        if l is None:
            gathered[nm] = full
        else:
            gathered[nm][l] = full
        return full

    def rms_fwd(name, xin, g):
        (h,) = _rowcall(name, lambda rv, cv: ([_rmsnorm(rv[0], cv[0])], []), [(xin, 0, D)], [g.reshape(1, D)], [(D, BF)], [], TR)
        return h

    def ffn_fwd(tag, xin, h, nm_in, nm_out, layer, next_gain):
        w_in = whole(nm_in, layer, h)
        dff = w_in.shape[2] // 2
        zg, zu, a = _ffn_in_swiglu("ffn_in_" + tag, h, w_in, 1024, dff // 2)
        out = _mm("ffn_out_" + tag, a, whole(nm_out, layer, a), "nn", F32, 1024, 1024, dff, scale=0.5, res=xin, b_lead=0,
                  norm_gain=None if next_gain is None else next_gain.reshape(1, D))
        xo, h_next = (out, None) if next_gain is None else out
        return xo, h_next, (xin, h, zg, zu, a)

    def ffn_bwd(tag, dxo, saved, g, w_in, w_out, layer):
        xin, h, zg, zu, a = saved
        dff = w_out[layer].shape[1]
        dw_out = _mm_tn_pair("ffn_dwo_" + tag, a, dxo, "row", c_arr, dff // 2, T, scale=0.5)
        dz = _ffn_da_swiglu("ffn_da_" + tag, dxo, w_out[layer], zg, zu, 512)
        dw_in = _mm_tn_pair("ffn_dwi_" + tag, h, dz, "col", c_arr, 512, T)
        dx, dg = _mm_dh_rms("ffn_dh_" + tag, dz, w_in[layer], xin, g.reshape(1, D), dxo, 512)
        return dx, dg, dw_in, dw_out

    def rms_bwd(name, xin, g, dh, dres):
        def fn(rv, cv):
            _, vjp = jax.vjp(_rmsnorm, rv[0], cv[0])
            dx, dg = vjp(rv[1])
            if dres is not None:
                dx = dx + rv[2]
            return [dx], [dg]

        rows = [(xin, 0, D), (dh, 0, D)] + ([(dres, 0, D)] if dres is not None else [])
        dx, dg = _rowcall(name, fn, rows, [g.reshape(1, D)], [(D, F32)], [((1, D), F32)], TR)
        return dx, dg

    x0 = x.reshape(T, D)
    tgt = loss_target.reshape(T, D)
    mem2 = mem.reshape(bl * mem_len, D)
    memn = rms_fwd("rms_mem", mem2, mem_norm)

    h_f10 = rms_fwd("rms_f1l0", x0, ffn1_norm[0])
    x1, h_m0, sv_f10 = ffn_fwd("f1l0", x0, h_f10, "ffn1_w_in", "ffn1_w_out", 0, mix_norm[0])
    z_m0 = _mm("mix_in_0", h_m0, whole("hgrn_w_in", None, h_m0), "nn", F32, 2048, 512, D, b_lead=0)
    w_kv = whole("mem_w_kv", None, z_m0)
    kv = [_mm("kv_%d" % i, memn, w_kv, "nn", F32, 512, 512, D, b_lead=i) for i in range(2)]
    cat0, stash0 = _hgrn_fwd2(z_m0, lb_logits, hgrn_gnorm, kv[0], bl, seq)
    x2, h_f20 = _mm("mix_out_0", cat0, whole("hgrn_w_out", None, cat0), "nn", F32, 1024, 1024, cat0.shape[1], res=x1, b_lead=0,
                    norm_gain=ffn2_norm[0].reshape(1, D))
    x3, h_f11, sv_f20 = ffn_fwd("f2l0", x2, h_f20, "ffn2_w_in", "ffn2_w_out", 0, ffn1_norm[1])
    x4, h_m1, sv_f11 = ffn_fwd("f1l1", x3, h_f11, "ffn1_w_in", "ffn1_w_out", 1, mix_norm[1])
    z_m1 = _mm("mix_in_1", h_m1, whole("gmlp_w_in", None, h_m1), "nn", F32, 2048, 512, D, b_lead=0)
    nc1 = seq // GM_CHUNK
    w_s, b_s = gmlp_w_s[0], gmlp_b_s[0]
    ln_w = GM_GROUPS * GM_GROUP_DIM
    ln_g_full, ln_b_full = [whole(nm, None, z_m1).reshape(1, ln_w) for nm in ("gmlp_ln_g", "gmlp_ln_b")]
    cat1 = _gmlp_fwd(z_m1, ln_g_full, ln_b_full, w_s, b_s, kv[1], bl, nc1)
    x5, h_f21 = _mm("mix_out_1", cat1, whole("gmlp_w_out", None, cat1), "nn", F32, 1024, 1024, cat1.shape[1], res=x4, b_lead=0,
                    norm_gain=ffn2_norm[1].reshape(1, D))
    x6, _, sv_f21 = ffn_fwd("f2l1", x5, h_f21, "ffn2_w_in", "ffn2_w_out", 1, None)

    def head(rv, cv):
        def f(xx, gg):
            err = _rmsnorm(xx, gg) - rv[1]
            return 0.5 * jnp.sum(jnp.mean(err * err, axis=-1, keepdims=True), axis=0, keepdims=True)

        ls, vjp = jax.vjp(f, rv[0], cv[0])
        dx, dg = vjp(jnp.ones((1, 1), F32))
        return [dx], [dg, jnp.broadcast_to(ls, (1, 128))]

    dx6, d_final, loss_part = _rowcall("loss_head", head, [(x6, 0, D), (tgt, 0, D)], [final_norm.reshape(1, D)],
                                       [(D, F32)], [((1, D), F32), ((1, 128), F32)], TR)

    rs_out = {}
    n_gather = len(groups)

    def rs(gi, items):
        outs = _rs_chips_seq("reduce_%d" % gi, [p for (_, p, _) in items], [k for (_, _, k) in items], n_gather + gi)
        for i, (key, _, _) in enumerate(items):
            rs_out[key] = (outs[2 * i], outs[2 * i + 1])

    dx5, dg_f21, dwi_f21, dwo_f21 = ffn_bwd("f2l1", dx6, sv_f21, ffn2_norm[1], gathered["ffn2_w_in"], gathered["ffn2_w_out"], 1)
    rs(0, [(("ffn2_w_out", 1), dwo_f21, "row"), (("ffn2_w_in", 1), dwi_f21, "col")])
    dcat1 = _mm("mix_dcat_1", dx5, gathered["gmlp_w_out"], "nt", F32, 2048, 1024, D, b_lead=0)
    dwo_m1 = _mm_tn_pair("mix_dwo_1", cat1, dx5, "row", c_arr, 1024, T)
    dz_m1, dkv1, d_lng, d_lnb, d_ws, d_bs = _gmlp_bwd(z_m1, dcat1, ln_g_full, ln_b_full, w_s, b_s, kv[1], bl, nc1)
    dx4, dg_m1 = _mm_dh_rms("mix_dh_1", dz_m1, gathered["gmlp_w_in"], x4, mix_norm[1].reshape(1, D), dx5, 512)
    dwi_m1 = _mm_tn_pair("mix_dwi_1", h_m1, dz_m1, "col", c_arr, 1024, T)
    rs(1, [(("gmlp_w_out", 0), dwo_m1, "row"), (("gmlp_w_in", 0), dwi_m1, "col")])
    dx3, dg_f11, dwi_f11, dwo_f11 = ffn_bwd("f1l1", dx4, sv_f11, ffn1_norm[1], gathered["ffn1_w_in"], gathered["ffn1_w_out"], 1)
    rs(2, [(("ffn1_w_out", 1), dwo_f11, "row"), (("ffn1_w_in", 1), dwi_f11, "col")])

    dx2, dg_f20, dwi_f20, dwo_f20 = ffn_bwd("f2l0", dx3, sv_f20, ffn2_norm[0], gathered["ffn2_w_in"], gathered["ffn2_w_out"], 0)
    rs(3, [(("ffn2_w_out", 0), dwo_f20, "row"), (("ffn2_w_in", 0), dwi_f20, "col")])
    dcat0 = _mm("mix_dcat_0", dx2, gathered["hgrn_w_out"], "nt", F32, 2048, 1024, D, b_lead=0)
    dwo_m0 = _mm_tn_pair("mix_dwo_0", cat0, dx2, "row", c_arr, 1024, T)
    dz_m0, dkv0, d_lb, d_gn = _hgrn_bwd2(z_m0, dcat0, stash0, lb_logits, hgrn_gnorm, kv[0], bl, seq)
    dx1, dg_m0 = _mm_dh_rms("mix_dh_0", dz_m0, gathered["hgrn_w_in"], x1, mix_norm[0].reshape(1, D), dx2, 512)
    dwi_m0 = _mm_tn_pair("mix_dwi_0", h_m0, dz_m0, "col", c_arr, 1024, T)
    rs(4, [(("hgrn_w_out", 0), dwo_m0, "row"), (("hgrn_w_in", 0), dwi_m0, "col")])

    dwkv = [_mm_tn_pair("kv_dw_%d" % i, memn, dkv, "col", c_arr, 1024, 512) for i, dkv in enumerate([dkv0, dkv1])]
    rs(5, [(("mem_w_kv", 0), dwkv[0], "col"), (("mem_w_kv", 1), dwkv[1], "col")])
    dmemn = _mm("kv_dx_0", dkv0, gathered["mem_w_kv"], "nt", F32, 512, 512, 1024, b_lead=0)
    dmemn = _mm("kv_dx_1", dkv1, gathered["mem_w_kv"], "nt", F32, 512, 512, 1024, res=dmemn, b_lead=1)
    _, d_memnorm = rms_bwd("rms_bwd_mem", mem2, mem_norm, dmemn, None)

    dx0, dg_f10, dwi_f10, dwo_f10 = ffn_bwd("f1l0", dx1, sv_f10, ffn1_norm[0], gathered["ffn1_w_in"], gathered["ffn1_w_out"], 0)
    rs(6, [(("ffn1_w_out", 0), dwo_f10, "row")])
    rs(7, [(("ffn1_w_in", 0), dwi_f10, "col")])

    shard_grads = [_finish_share("finish_" + nm, [rs_out[(nm, l)][0] for l in range(w.shape[0])],
                                 [rs_out[(nm, l)][1] for l in range(w.shape[0])], k, c_arr) for (nm, w, k) in big]

    big_w = [w for (_, w, _) in big]
    big_m = [m_ffn1_w_in, m_ffn1_w_out, m_mem_w_kv, m_hgrn_w_in, m_hgrn_w_out, m_gmlp_w_in, m_gmlp_w_out, m_ffn2_w_in, m_ffn2_w_out]
    big_v = [v_ffn1_w_in, v_ffn1_w_out, v_mem_w_kv, v_hgrn_w_in, v_hgrn_w_out, v_gmlp_w_in, v_gmlp_w_out, v_ffn2_w_in, v_ffn2_w_out]
    big_out = {}
    for (nm, w, _), g, m, v in zip(big, shard_grads, big_m, big_v):
        L, r, c = w.shape
        g2, d2, m2, v2 = _adam_pass("adam_" + nm, w.reshape(L * r, c), g.reshape(L * r, c), m.reshape(L * r, c),
                                    v.reshape(L * r, c), 256)
        big_out[nm] = (g2.reshape(w.shape), d2.reshape(w.shape), m2.reshape(w.shape), v2.reshape(w.shape))

    d_ffn1n = _two_rows(dg_f10, dg_f11)
    d_mixn = _two_rows(dg_m0, dg_m1)
    d_ffn2n = _two_rows(dg_f20, dg_f21)
    small_parts = [loss_part[:, :1], d_memnorm, d_lb, d_ffn1n, d_mixn, d_gn, d_lng, d_lnb, d_ws, d_bs, d_ffn2n, d_final]
    red_shapes = [(1,), mem_norm.shape, lb_logits.shape, ffn1_norm.shape, mix_norm.shape, hgrn_gnorm.shape, (1, ln_w), (1, ln_w),
                  gmlp_w_s.shape, gmlp_b_s.shape, ffn2_norm.shape, final_norm.shape]
    red = _small_allreduce(_pack(small_parts, _rows_needed(red_shapes)), "reduce_small")
    (loss_v, g_memn, g_lb, g_f1n, g_mixn, g_gn, g_lng_full, g_lnb_full, g_ws, g_bs, g_f2n, g_fin) = _unpack(red, red_shapes)
    lsh = gmlp_ln_g.shape[1]
    g_lng = lax.dynamic_slice(g_lng_full, (0, chip * lsh), (1, lsh))
    g_lnb = lax.dynamic_slice(g_lnb_full, (0, chip * lsh), (1, lsh))
    small_w = [mem_norm, lb_logits, ffn1_norm, mix_norm, hgrn_gnorm, gmlp_ln_g, gmlp_ln_b, gmlp_w_s, gmlp_b_s, ffn2_norm, final_norm]
    small_g = [g_memn, g_lb, g_f1n, g_mixn, g_gn, g_lng, g_lnb, g_ws, g_bs, g_f2n, g_fin]
    small_m = [m_mem_norm, m_lb_logits, m_ffn1_norm, m_mix_norm, m_hgrn_gnorm, m_gmlp_ln_g, m_gmlp_ln_b, m_gmlp_w_s, m_gmlp_b_s, m_ffn2_norm, m_final_norm]
    small_v = [v_mem_norm, v_lb_logits, v_ffn1_norm, v_mix_norm, v_hgrn_gnorm, v_gmlp_ln_g, v_gmlp_ln_b, v_gmlp_w_s, v_gmlp_b_s, v_ffn2_norm, v_final_norm]
    sshapes = [w.shape for w in small_w]
    nrow = _rows_needed(sshapes)
    d_p, m_p, v_p = _adam_call("adam_small", _pack(small_w, nrow), _pack(small_g, nrow), _pack(small_m, nrow), _pack(small_v, nrow), nrow)
    s_delta, s_m, s_v = _unpack(d_p, sshapes), _unpack(m_p, sshapes), _unpack(v_p, sshapes)
    small_names = ["mem_norm", "lb_logits", "ffn1_norm", "mix_norm", "hgrn_gnorm", "gmlp_ln_g", "gmlp_ln_b", "gmlp_w_s", "gmlp_b_s", "ffn2_norm", "final_norm"]
    small_out = {nm: (g.reshape(w.shape), d, m, v) for nm, w, g, d, m, v in zip(small_names, small_w, small_g, s_delta, s_m, s_v)}

    order = ["mem_norm", "lb_logits", "ffn1_norm", "ffn1_w_in", "ffn1_w_out", "mix_norm", "mem_w_kv", "hgrn_w_in", "hgrn_gnorm",
             "hgrn_w_out", "gmlp_w_in", "gmlp_ln_g", "gmlp_ln_b", "gmlp_w_s", "gmlp_b_s", "gmlp_w_out", "ffn2_norm", "ffn2_w_in",
             "ffn2_w_out", "final_norm"]
    allo = {**big_out, **small_out}
    grad_x = dx0.reshape(x.shape)
    return (loss_v.reshape(()), grad_x, *[allo[n][0] for n in order], *[allo[n][1] for n in order],
            *[allo[n][2] for n in order], *[allo[n][3] for n in order])
```

```python
import functools

import jax
import jax.numpy as jnp
from jax import lax
from jax.experimental import pallas as pl
from jax.experimental.pallas import tpu as pltpu
from jax.experimental.pallas import tpu_sc as plsc

BF = jnp.bfloat16
F32 = jnp.float32
MESH = pl.DeviceIdType.MESH

EPS = 1e-6
D_MODEL = 1024
HG_HEADS = 8
HG_DIM = 128
HG_CHUNK = 64
GM_CHUNK = 128
GM_GROUPS = 8
GM_GROUP_DIM = 256
XA_HEADS = 4
XA_DIM = 256
ADAM_LR = 0.001
ADAM_B1 = 0.9
ADAM_B2 = 0.999
ADAM_EPS = 1e-08
ADAM_WD = 0.01
ADAM_STEP = 10

VMEM_CAP_BYTES = 60 * 1024 * 1024
LANES = 1024


def _pick(n, cap, mult=16):
    if n <= cap:
        return n
    for d in range(cap - cap % mult, 0, -mult):
        if n % d == 0:
            return d
    raise ValueError((n, cap, mult))


def _dg(a, b, ca, cb):
    return lax.dot_general(a.astype(BF), b.astype(BF), (((ca,), (cb,)), ((), ())), preferred_element_type=F32)


@jax.custom_vjp
def dot_nn(a, b):
    return _dg(a, b, 1, 0)


def _nn_fwd(a, b):
    return _dg(a, b, 1, 0), (a, b)


def _nn_bwd(r, g):
    a, b = r
    return _dg(g, b, 1, 1), _dg(a, g, 0, 0)


dot_nn.defvjp(_nn_fwd, _nn_bwd)


@jax.custom_vjp
def dot_nt(a, b):
    return _dg(a, b, 1, 1)


def _nt_fwd(a, b):
    return _dg(a, b, 1, 1), (a, b)


def _nt_bwd(r, g):
    a, b = r
    return _dg(g, b, 1, 0), _dg(g, a, 0, 0)


dot_nt.defvjp(_nt_fwd, _nt_bwd)


@jax.custom_vjp
def dot_tn(a, b):
    return _dg(a, b, 0, 0)


def _tn_fwd(a, b):
    return _dg(a, b, 0, 0), (a, b)


def _tn_bwd(r, g):
    a, b = r
    return _dg(b, g, 1, 1), _dg(a, g, 1, 0)


dot_tn.defvjp(_tn_fwd, _tn_bwd)


def _rmsnorm(x, g):
    return x * lax.rsqrt(jnp.mean(x * x, axis=-1, keepdims=True) + EPS) * g


def _silu(x):
    return x * jax.nn.sigmoid(x)


@jax.custom_vjp
def _gelu(x):
    return 0.5 * x * (1.0 + lax.erf(x * (0.5 ** 0.5)))


def _gelu_fwd(x):
    return _gelu(x), x


def _gelu_bwd(x, g):
    t = x * (0.5 ** 0.5)
    cdf = 0.5 * (1.0 + lax.erf(t))
    return (g * (cdf + x * (jnp.exp(-(t * t)) * (0.5 / 3.141592653589793) ** 0.5)),)


_gelu.defvjp(_gelu_fwd, _gelu_bwd)


def _softmax_last(s):
    m = lax.stop_gradient(jnp.max(s, axis=-1, keepdims=True))
    e = jnp.exp(s - m)
    return e / jnp.sum(e, axis=-1, keepdims=True)


def _tril(n):
    r = lax.broadcasted_iota(jnp.int32, (n, n), 0)
    c = lax.broadcasted_iota(jnp.int32, (n, n), 1)
    return r >= c


def _attention(zx, mk, mv):
    s = dot_nt(zx, mk) * (XA_DIM ** -0.5)
    return dot_nn(_softmax_last(s), mv)


def _chunk_sums(x, suffix):
    n = x.shape[0]
    r = lax.broadcasted_iota(jnp.int32, (n, n), 0)
    c = lax.broadcasted_iota(jnp.int32, (n, n), 1)
    tri = jnp.logical_and(r <= c if suffix else r >= c, r // HG_CHUNK == c // HG_CHUNK).astype(BF)
    hi = x.astype(BF)
    rest = x - hi.astype(F32)
    mid = rest.astype(BF)
    lo = (rest - mid.astype(F32)).astype(BF)
    return (_dg(tri, hi, 1, 0) + _dg(tri, mid, 1, 0)) + _dg(tri, lo, 1, 0)


@jax.custom_vjp
def _running_sums(x):
    return _chunk_sums(x, False)


_running_sums.defvjp(lambda x: (_chunk_sums(x, False), None), lambda _, g: (_chunk_sums(g, True),))


def _hgrn_decays(zf, lb3):
    l0, l1, l2 = lb3[0:1], lb3[1:2], lb3[2:3]
    m = lax.stop_gradient(jnp.maximum(jnp.maximum(l0, l1), l2))
    e0 = jnp.exp(l0 - m)
    lb = e0 / (e0 + jnp.exp(l1 - m) + jnp.exp(l2 - m))
    f = lb + (1.0 - lb) * jax.nn.sigmoid(zf)
    return f, _running_sums(jnp.log(f))


def _hgrn_head(zq, f, b, zi, zg, gn, S):
    q = _silu(zq)
    k = 1.0 - f
    b_last = b[HG_CHUNK - 1:HG_CHUNK, :]
    q_dec = q * jnp.exp(b)
    k_inv = k * jnp.exp(-b)
    a = jnp.where(_tril(HG_CHUNK), dot_nt(q_dec, k_inv), 0.0)
    o = dot_nn(a, zi) + dot_nn(q_dec, S)
    S_new = jnp.exp(b_last).reshape(HG_DIM, 1) * S + dot_tn(k * jnp.exp(b_last - b), zi)
    o = _rmsnorm(o, gn) * _silu(zg)
    return o, S_new


def _gmlp_block(zu, zv, zx, lng, lnb, ws, bs, mk, mv):
    gv = [_gelu(v) for v in zv]
    width = GM_GROUPS * GM_GROUP_DIM
    mu = sum(jnp.sum(g, axis=-1, keepdims=True) for g in gv) / width
    xc = [g - mu for g in gv]
    var = sum(jnp.sum(c * c, axis=-1, keepdims=True) for c in xc) / width
    r = lax.rsqrt(var + EPS)
    outs = []
    for g in range(GM_GROUPS):
        v = xc[g] * r * lng[g] + lnb[g]
        w = jnp.where(_tril(GM_CHUNK), ws[g], 0.0)
        mixed = dot_nn(w, v) + bs[g].reshape(GM_CHUNK, 1)
        outs.append(_gelu(zu[g]) * mixed)
    for a in range(XA_HEADS):
        outs.append(_attention(zx[a], mk[a], mv[a]))
    return outs


def _rowcall(name, fn, rows, consts, row_outs, acc_outs, tr):
    nrows = rows[0][0].shape[0]
    tr = _pick(nrows, tr)
    n_r, n_c, n_ro, n_ao = len(rows), len(consts), len(row_outs), len(acc_outs)

    def kern(*refs):
        rv = [r[...] for r in refs[:n_r]]
        cv = [r[...] for r in refs[n_r:n_r + n_c]]
        ro_refs = refs[n_r + n_c:n_r + n_c + n_ro]
        ao_refs = refs[n_r + n_c + n_ro:]
        ro, ao = fn(rv, cv)
        for ref, v in zip(ro_refs, ro):
            ref[...] = v.astype(ref.dtype)
        if n_ao:
            @pl.when(pl.program_id(0) == 0)
            def _():
                for ref in ao_refs:
                    ref[...] = jnp.zeros(ref.shape, ref.dtype)

            for ref, v in zip(ao_refs, ao):
                ref[...] += v.astype(ref.dtype)

    in_specs = [pl.BlockSpec((tr, w), functools.partial(lambda i, cb: (i, cb), cb=cb)) for (_, cb, w) in rows]
    in_specs += [pl.BlockSpec(c.shape, lambda i: (0, 0)) for c in consts]
    out_specs = [pl.BlockSpec((tr, w), lambda i: (i, 0)) for (w, _) in row_outs]
    out_specs += [pl.BlockSpec(s, lambda i: (0, 0)) for (s, _) in acc_outs]
    out_shape = [jax.ShapeDtypeStruct((nrows, w), dt) for (w, dt) in row_outs]
    out_shape += [jax.ShapeDtypeStruct(s, dt) for (s, dt) in acc_outs]
    outs = pl.pallas_call(
        kern, grid=(nrows // tr,), in_specs=in_specs, out_specs=out_specs, out_shape=out_shape, name=name,
        compiler_params=pltpu.CompilerParams(dimension_semantics=("arbitrary",),
                                             vmem_limit_bytes=VMEM_CAP_BYTES),
    )(*[a for (a, _, _) in rows], *consts)
    return outs


def _mm(name, a, b, mode, out_dtype, tm, tn, tk, scale=1.0, res=None, a_lead=None, b_lead=None, norm_gain=None):
    ash = a.shape[-2:]
    bsh = b.shape[-2:]
    if mode == "nn":
        (M, K), (K2, N) = ash, bsh
    elif mode == "nt":
        (M, K), (N, K2) = ash, bsh
    else:
        (K, M), (K2, N) = ash, bsh
    assert K == K2, (name, a.shape, b.shape)
    tm, tn, tk = min(tm, M), min(tn, N), min(tk, K)
    assert M % tm == 0 and N % tn == 0 and K % tk == 0, (name, M, N, K, tm, tn, tk)
    nk = K // tk
    dims = {"nn": (1, 0), "nt": (1, 1), "tn": (0, 0)}[mode]

    def lead(spec_shape, index_fn, lead_idx):
        if lead_idx is None:
            return pl.BlockSpec(spec_shape, index_fn)
        return pl.BlockSpec((None,) + spec_shape, lambda i, j, k: (lead_idx,) + index_fn(i, j, k))

    if mode == "tn":
        a_spec = lead((tk, tm), lambda i, j, k: (k, i), a_lead)
    else:
        a_spec = lead((tm, tk), lambda i, j, k: (i, k), a_lead)
    if mode == "nt":
        b_spec = lead((tn, tk), lambda i, j, k: (j, k), b_lead)
    else:
        b_spec = lead((tk, tn), lambda i, j, k: (k, j), b_lead)
    o_spec = pl.BlockSpec((tm, tn), lambda i, j, k: (i, j))
    has_res = res is not None
    has_norm = norm_gain is not None
    assert not has_norm or tn == N

    def kern(*refs):
        a_ref, b_ref = refs[0], refs[1]
        pos = 2
        res_ref = gain_ref = h_ref = None
        if has_res:
            res_ref, pos = refs[pos], pos + 1
        if has_norm:
            gain_ref, pos = refs[pos], pos + 1
        o_ref, pos = refs[pos], pos + 1
        if has_norm:
            h_ref = refs[pos]
        acc_ref = refs[-1] if nk > 1 else None
        p = lax.dot_general(a_ref[...].astype(BF), b_ref[...].astype(BF), (((dims[0],), (dims[1],)), ((), ())),
                            preferred_element_type=F32)

        def finish(v):
            if scale != 1.0:
                v = v * scale
            if has_res:
                v = res_ref[...] + v
            o_ref[...] = v.astype(o_ref.dtype)
            if has_norm:
                h_ref[...] = _rmsnorm(v, gain_ref[...]).astype(h_ref.dtype)

        if nk == 1:
            finish(p)
        else:
            k = pl.program_id(2)

            @pl.when(k == 0)
            def _():
                acc_ref[...] = p

            @pl.when(k > 0)
            def _():
                acc_ref[...] += p

            @pl.when(k == nk - 1)
            def _():
                finish(acc_ref[...])

    ins = [a, b] + ([res] if has_res else []) + ([norm_gain] if has_norm else [])
    in_specs = [a_spec, b_spec] + ([o_spec] if has_res else [])
    in_specs += [pl.BlockSpec((1, N), lambda i, j, k: (0, 0))] if has_norm else []
    out_sd = jax.ShapeDtypeStruct((M, N), out_dtype)
    return pl.pallas_call(
        kern, grid=(M // tm, N // tn, nk), in_specs=in_specs,
        out_specs=[o_spec, o_spec] if has_norm else o_spec,
        out_shape=[out_sd, jax.ShapeDtypeStruct((M, N), BF)] if has_norm else out_sd,
        scratch_shapes=[pltpu.VMEM((tm, tn), F32)] if nk > 1 else [],
        name=name,
        compiler_params=pltpu.CompilerParams(dimension_semantics=("parallel", "parallel", "arbitrary"),
                                             vmem_limit_bytes=VMEM_CAP_BYTES),
    )(*ins)


def _ffn_in_swiglu(name, h, w3, tm, tn):
    T, D = h.shape
    dff = w3.shape[2] // 2
    tm = min(tm, T)
    assert T % tm == 0 and dff % tn == 0
    nj = dff // tn

    def kern(h_ref, wg_ref, wu_ref, zg_ref, zu_ref, a_ref):
        hb = h_ref[...]
        g = jnp.dot(hb, wg_ref[...], preferred_element_type=F32).astype(BF)
        u = jnp.dot(hb, wu_ref[...], preferred_element_type=F32).astype(BF)
        zg_ref[...] = g
        zu_ref[...] = u
        a_ref[...] = (_silu(g.astype(F32)) * u.astype(F32)).astype(BF)

    o_spec = pl.BlockSpec((tm, tn), lambda j, i: (i, j))
    return pl.pallas_call(
        kern, grid=(nj, T // tm),
        in_specs=[pl.BlockSpec((tm, D), lambda j, i: (i, 0)),
                  pl.BlockSpec((None, D, tn), lambda j, i: (0, 0, j)),
                  pl.BlockSpec((None, D, tn), lambda j, i: (0, 0, j + nj))],
        out_specs=[o_spec, o_spec, o_spec],
        out_shape=[jax.ShapeDtypeStruct((T, dff), BF)] * 3, name=name,
        compiler_params=pltpu.CompilerParams(dimension_semantics=("arbitrary", "parallel"),
                                             vmem_limit_bytes=VMEM_CAP_BYTES),
    )(h, w3, w3)


def _ffn_da_swiglu(name, dxo, w3, zg, zu, tm):
    T, D = dxo.shape
    dff = w3.shape[1]
    tm = min(tm, T)
    assert T % tm == 0 and dff % 2 == 0
    hc = dff // 2

    def kern(d_ref, w_ref, g_ref, u_ref, dz_ref):
        db = (d_ref[...] * 0.5).astype(BF)
        for s in range(2):
            cols = slice(s * hc, (s + 1) * hc)
            da = lax.dot_general(db, w_ref[cols, :], (((1,), (1,)), ((), ())), preferred_element_type=F32)
            g = g_ref[:, cols].astype(F32)
            sg = 1.0 / (1.0 + jnp.exp(-g))
            gs = g * sg
            dab = da.astype(BF)
            dz_ref[:, cols] = (dab * u_ref[:, cols]) * (sg + gs * (1.0 - sg)).astype(BF)
            dz_ref[:, dff + s * hc:dff + (s + 1) * hc] = dab * gs.astype(BF)

    row = lambda w: pl.BlockSpec((tm, w), lambda i: (i, 0))
    return pl.pallas_call(
        kern, grid=(T // tm,),
        in_specs=[row(D), pl.BlockSpec((None, dff, D), lambda i: (0, 0, 0), pipeline_mode=pl.Buffered(1)), row(dff), row(dff)],
        out_specs=row(2 * dff), out_shape=jax.ShapeDtypeStruct((T, 2 * dff), BF), name=name,
        compiler_params=pltpu.CompilerParams(dimension_semantics=("arbitrary",), vmem_limit_bytes=VMEM_CAP_BYTES),
    )(dxo, w3, zg, zu)


def _mm_dh_rms(name, dz, w3, xin, g, dres, tm):
    T, K = dz.shape
    D = w3.shape[1]
    tm = min(tm, T)
    assert T % tm == 0

    def kern(dz_ref, w_ref, x_ref, g_ref, r_ref, dx_ref, dg_ref):
        dh = lax.dot_general(dz_ref[...], w_ref[...], (((1,), (1,)), ((), ())), preferred_element_type=F32)
        _, vjp = jax.vjp(_rmsnorm, x_ref[...], g_ref[...])
        dx, dg = vjp(dh)
        dx_ref[...] = dx + r_ref[...]

        @pl.when(pl.program_id(0) == 0)
        def _():
            dg_ref[...] = jnp.zeros(dg_ref.shape, F32)

        dg_ref[...] += dg

    row = lambda w: pl.BlockSpec((tm, w), lambda i: (i, 0))
    one = pl.BlockSpec((1, D), lambda i: (0, 0))
    return pl.pallas_call(
        kern, grid=(T // tm,),
        in_specs=[row(K), pl.BlockSpec((None, D, K), lambda i: (0, 0, 0), pipeline_mode=pl.Buffered(1)), row(D), one, row(D)],
        out_specs=[row(D), one], out_shape=[jax.ShapeDtypeStruct((T, D), F32), jax.ShapeDtypeStruct((1, D), F32)], name=name,
        compiler_params=pltpu.CompilerParams(dimension_semantics=("arbitrary",), vmem_limit_bytes=VMEM_CAP_BYTES),
    )(dz, w3, xin, g, dres)


def _mm_tn_pair(name, a, b, kind, c_arr, tq, tk, scale=1.0):
    T, M = a.shape
    _, N = b.shape
    tk = min(tk, T)
    assert T % tk == 0
    nk = T // tk
    if kind == "col":
        hm = M // 2
        assert N % tq == 0
        nq = N // tq
        tile = (hm, tq)
        a_spec = pl.BlockSpec((tk, hm), lambda h, q, k, c: (k, jnp.bitwise_xor(h, 1 - c[0])))
        b_spec = pl.BlockSpec((tk, tq), lambda h, q, k, c: (k, q))
        o_spec = pl.BlockSpec(tile, lambda h, q, k, c: (0, q * h))
        out_sd = (hm, N)
    else:
        hn = N // 2
        assert M % tq == 0
        nq = M // tq
        tile = (tq, hn)
        a_spec = pl.BlockSpec((tk, tq), lambda h, q, k, c: (k, q))
        b_spec = pl.BlockSpec((tk, hn), lambda h, q, k, c: (k, jnp.bitwise_xor(h, 1 - c[0])))
        o_spec = pl.BlockSpec(tile, lambda h, q, k, c: (q * h, 0))
        out_sd = (M, hn)

    def kern(c_ref, a_ref, b_ref, o_ref, acc, stage, recv, ssem, rsem):
        h, q, k = pl.program_id(0), pl.program_id(1), pl.program_id(2)
        x, y, c, _ = _place()
        p = lax.dot_general(a_ref[...].astype(BF), b_ref[...].astype(BF), (((0,), (0,)), ((), ())), preferred_element_type=F32)

        @pl.when(k == 0)
        def _():
            acc[...] = p

        @pl.when(k > 0)
        def _():
            acc[...] += p

        def send(slot, qq):
            return pltpu.make_async_remote_copy(src_ref=stage.at[slot], dst_ref=recv.at[qq], send_sem=ssem.at[slot],
                                                recv_sem=rsem.at[qq], device_id=(x, y, 1 - c), device_id_type=MESH)

        last = k == nk - 1

        @pl.when(jnp.logical_and(last, h == 0))
        def _():
            slot = q % 2

            @pl.when(q >= 2)
            def _():
                send(slot, q).wait_send()

            stage[slot] = (acc[...] * scale).astype(BF)
            send(slot, q).start()

        @pl.when(jnp.logical_and(last, h == 1))
        def _():
            @pl.when(q == 0)
            def _():
                for s in range(min(nq, 2)):
                    send(s, 0).wait_send()

            send(0, q).wait_recv()
            o_ref[...] = (acc[...] * scale + recv[q].astype(F32)).astype(o_ref.dtype)

    return pl.pallas_call(
        kern,
        grid_spec=pltpu.PrefetchScalarGridSpec(
            num_scalar_prefetch=1, grid=(2, nq, nk), in_specs=[a_spec, b_spec], out_specs=o_spec,
            scratch_shapes=[pltpu.VMEM(tile, F32), pltpu.VMEM((2,) + tile, BF), pltpu.VMEM((nq,) + tile, BF),
                            pltpu.SemaphoreType.DMA((2,)), pltpu.SemaphoreType.DMA((nq,))]),
        out_shape=jax.ShapeDtypeStruct(out_sd, BF), name=name,
        compiler_params=pltpu.CompilerParams(dimension_semantics=("arbitrary", "arbitrary", "arbitrary"),
                                             vmem_limit_bytes=VMEM_CAP_BYTES),
    )(c_arr, a, b)


def _kv_pieces(kv_ref):
    W = XA_HEADS * XA_DIM
    mk = [kv_ref[:, a * XA_DIM:(a + 1) * XA_DIM] for a in range(XA_HEADS)]
    mv = [kv_ref[:, W + a * XA_DIM:W + (a + 1) * XA_DIM] for a in range(XA_HEADS)]
    return mk, mv


HG_SUB = 4


def _hgrn_rows(z_ref):
    W = HG_HEADS * HG_DIM

    def piece(c, col, w):
        return z_ref[c * HG_CHUNK:(c + 1) * HG_CHUNK, col:col + w]

    zq = [[piece(c, h * HG_DIM, HG_DIM) for h in range(HG_HEADS)] for c in range(HG_SUB)]
    zf = z_ref[:, W:2 * W]
    zi =[[piece(c, 2 * W + h * HG_DIM, HG_DIM) for h in range(HG_HEADS)] for c in range(HG_SUB)]
    zg = [[piece(c, 3 * W + h * HG_DIM, HG_DIM) for h in range(HG_HEADS)] for c in range(HG_SUB)]
    zx = [z_ref[:, 4 * W + a * XA_DIM:4 * W + (a + 1) * XA_DIM] for a in range(XA_HEADS)]
    return zq, zf, zi, zg, zx


def _hgrn_steps(zq, zf, zi, zg, zx, lb3, gn, mk, mv, S):
    f, b = _hgrn_decays(zf, lb3)
    mix = []
    for c in range(HG_SUB):
        row, s_next = [], []
        rows = slice(c * HG_CHUNK, (c + 1) * HG_CHUNK)
        for h in range(HG_HEADS):
            cols = slice(h * HG_DIM, (h + 1) * HG_DIM)
            o, sn = _hgrn_head(zq[c][h], f[rows, cols], b[rows, cols], zi[c][h], zg[c][h], gn, S[h])
            row.append(o)
            s_next.append(sn)
        mix.append(row)
        S = s_next
    att = [_attention(zx[a], mk[a], mv[a]) for a in range(XA_HEADS)]
    return mix, att, S


def _hgrn_fwd2(z, lb_logits, gnorm, kv, bl, seq):
    T, zw = z.shape
    mem_len = kv.shape[0] // bl
    cat_w = HG_HEADS * HG_DIM + XA_HEADS * XA_DIM
    R = HG_SUB * HG_CHUNK
    nb = seq // R

    def kern(z_ref, lb_ref, gn_ref, kv_ref, cat_ref, st_ref, s_scr):
        @pl.when(pl.program_id(1) == 0)
        def _():
            s_scr[...] = jnp.zeros(s_scr.shape, F32)

        st_ref[...] = s_scr[...]
        zq, zf, zi, zg, zx = _hgrn_rows(z_ref)
        mk, mv = _kv_pieces(kv_ref)
        S = [s_scr[h] for h in range(HG_HEADS)]
        mix, att, s_new = _hgrn_steps(zq, zf, zi, zg, zx, lb_ref[...], gn_ref[...], mk, mv, S)
        for c in range(HG_SUB):
            for h in range(HG_HEADS):
                cat_ref[c * HG_CHUNK:(c + 1) * HG_CHUNK, h * HG_DIM:(h + 1) * HG_DIM] = mix[c][h].astype(cat_ref.dtype)
        for h in range(HG_HEADS):
            s_scr[h] = s_new[h]
        base = HG_HEADS * HG_DIM
        for a in range(XA_HEADS):
            cat_ref[:, base + a * XA_DIM:base + (a + 1) * XA_DIM] = att[a].astype(cat_ref.dtype)

    return pl.pallas_call(
        kern, grid=(bl, nb),
        in_specs=[pl.BlockSpec((R, zw), lambda b, n: (b * nb + n, 0)),
                  pl.BlockSpec(lb_logits.shape, lambda b, n: (0, 0)),
                  pl.BlockSpec(gnorm.shape, lambda b, n: (0, 0)),
                  pl.BlockSpec((mem_len, kv.shape[1]), lambda b, n: (b, 0))],
        out_specs=[pl.BlockSpec((R, cat_w), lambda b, n: (b * nb + n, 0)),
                   pl.BlockSpec((None, HG_HEADS, HG_DIM, HG_DIM), lambda b, n: (b * nb + n, 0, 0, 0))],
        out_shape=[jax.ShapeDtypeStruct((T, cat_w), BF),
                   jax.ShapeDtypeStruct((bl * nb, HG_HEADS, HG_DIM, HG_DIM), F32)],
        scratch_shapes=[pltpu.VMEM((HG_HEADS, HG_DIM, HG_DIM), F32)],
        name="hgrn_fwd",
        compiler_params=pltpu.CompilerParams(dimension_semantics=("arbitrary", "arbitrary"), vmem_limit_bytes=VMEM_CAP_BYTES),
    )(z, lb_logits, gnorm, kv)


def _hgrn_bwd2(z, dcat, stash, lb_logits, gnorm, kv, bl, seq):
    T, zw = z.shape
    mem_len = kv.shape[0] // bl
    cat_w = dcat.shape[1]
    R = HG_SUB * HG_CHUNK
    nb = seq // R

    def kern(z_ref, dc_ref, st_ref, lb_ref, gn_ref, kv_ref, dz_ref, dkv_ref, dlb_ref, dgn_ref, ds_scr):
        first = jnp.logical_and(pl.program_id(0) == 0, pl.program_id(1) == 0)

        @pl.when(pl.program_id(1) == 0)
        def _():
            ds_scr[...] = jnp.zeros(ds_scr.shape, F32)
            dkv_ref[...] = jnp.zeros(dkv_ref.shape, F32)

        @pl.when(first)
        def _():
            dlb_ref[...] = jnp.zeros(dlb_ref.shape, F32)
            dgn_ref[...] = jnp.zeros(dgn_ref.shape, F32)

        zq, zf, zi, zg, zx = _hgrn_rows(z_ref)
        mk, mv = _kv_pieces(kv_ref)
        S = [st_ref[h] for h in range(HG_HEADS)]
        _, vjp = jax.vjp(_hgrn_steps, zq, zf, zi, zg, zx, lb_ref[...], gn_ref[...], mk, mv, S)
        d_mix = [[dc_ref[c * HG_CHUNK:(c + 1) * HG_CHUNK, h * HG_DIM:(h + 1) * HG_DIM] for h in range(HG_HEADS)]
                 for c in range(HG_SUB)]
        base = HG_HEADS * HG_DIM
        d_att = [dc_ref[:, base + a * XA_DIM:base + (a + 1) * XA_DIM] for a in range(XA_HEADS)]
        d_s = [ds_scr[h] for h in range(HG_HEADS)]
        dzq, dzf, dzi, dzg, dzx, dlb3, dgn, dmk, dmv, dS = vjp((d_mix, d_att, d_s))
        W = HG_HEADS * HG_DIM
        dz_ref[:, W:2 * W] = dzf.astype(dz_ref.dtype)
        for c in range(HG_SUB):
            rows = slice(c * HG_CHUNK, (c + 1) * HG_CHUNK)
            for h in range(HG_HEADS):
                for k, part in ((0, dzq), (2, dzi), (3, dzg)):
                    dz_ref[rows, k * W + h * HG_DIM:k * W + (h + 1) * HG_DIM] = part[c][h].astype(dz_ref.dtype)
        for h in range(HG_HEADS):
            ds_scr[h] = dS[h]
        dlb_ref[...] += dlb3
        dgn_ref[...] += dgn
        KW = XA_HEADS * XA_DIM
        for a in range(XA_HEADS):
            dz_ref[:, 4 * W + a * XA_DIM:4 * W + (a + 1) * XA_DIM] = dzx[a].astype(dz_ref.dtype)
            dkv_ref[:, a * XA_DIM:(a + 1) * XA_DIM] += dmk[a]
            dkv_ref[:, KW + a * XA_DIM:KW + (a + 1) * XA_DIM] += dmv[a]

    rev = lambda b, n: (b * nb + (nb - 1 - n), 0)
    return pl.pallas_call(
        kern, grid=(bl, nb),
        in_specs=[pl.BlockSpec((R, zw), rev),
                  pl.BlockSpec((R, cat_w), rev),
                  pl.BlockSpec((None, HG_HEADS, HG_DIM, HG_DIM), lambda b, n: (b * nb + (nb - 1 - n), 0, 0, 0)),
                  pl.BlockSpec(lb_logits.shape, lambda b, n: (0, 0)),
                  pl.BlockSpec(gnorm.shape, lambda b, n: (0, 0)),
                  pl.BlockSpec((mem_len, kv.shape[1]), lambda b, n: (b, 0))],
        out_specs=[pl.BlockSpec((R, zw), rev),
                   pl.BlockSpec((mem_len, kv.shape[1]), lambda b, n: (b, 0)),
                   pl.BlockSpec(lb_logits.shape, lambda b, n: (0, 0)),
                   pl.BlockSpec(gnorm.shape, lambda b, n: (0, 0))],
        out_shape=[jax.ShapeDtypeStruct((T, zw), BF), jax.ShapeDtypeStruct(kv.shape, F32),
                   jax.ShapeDtypeStruct(lb_logits.shape, F32), jax.ShapeDtypeStruct(gnorm.shape, F32)],
        scratch_shapes=[pltpu.VMEM((HG_HEADS, HG_DIM, HG_DIM), F32)],
        name="hgrn_bwd",
        compiler_params=pltpu.CompilerParams(dimension_semantics=("arbitrary", "arbitrary"), vmem_limit_bytes=VMEM_CAP_BYTES),
    )(z, dcat, stash, lb_logits, gnorm, kv)


GM_SUB = 2


def _gmlp_pieces(z_ref):
    W = GM_GROUPS * GM_GROUP_DIM
    zu = [z_ref[:, g * GM_GROUP_DIM:(g + 1) * GM_GROUP_DIM] for g in range(GM_GROUPS)]
    zv = [z_ref[:, W + g * GM_GROUP_DIM:W + (g + 1) * GM_GROUP_DIM] for g in range(GM_GROUPS)]
    zx = [z_ref[:, 2 * W + a * XA_DIM:2 * W + (a + 1) * XA_DIM] for a in range(XA_HEADS)]
    return zu, zv, zx


def _gmlp_params(lng_ref, lnb_ref, ws_ref, bs_ref):
    lng = [lng_ref[:, g * GM_GROUP_DIM:(g + 1) * GM_GROUP_DIM] for g in range(GM_GROUPS)]
    lnb = [lnb_ref[:, g * GM_GROUP_DIM:(g + 1) * GM_GROUP_DIM] for g in range(GM_GROUPS)]
    ws = [ws_ref[g] for g in range(GM_GROUPS)]
    bs = [bs_ref[g:g + 1, :] for g in range(GM_GROUPS)]
    return lng, lnb, ws, bs


def _gmlp_fwd(z, ln_g, ln_b, w_s, b_s, kv, bl, nc):
    T, zw = z.shape
    mem_len = kv.shape[0] // bl
    cat_w = GM_GROUPS * GM_GROUP_DIM + XA_HEADS * XA_DIM

    assert nc % GM_SUB == 0
    nc = nc // GM_SUB
    R = GM_SUB * GM_CHUNK

    def kern(z_ref, lng_ref, lnb_ref, ws_ref, bs_ref, kv_ref, cat_ref):
        lng, lnb, ws, bs = _gmlp_params(lng_ref, lnb_ref, ws_ref, bs_ref)
        mk, mv = _kv_pieces(kv_ref)
        for c in range(GM_SUB):
            rows = pl.ds(c * GM_CHUNK, GM_CHUNK)
            zu, zv, zx = _gmlp_pieces(z_ref.at[rows])
            out = cat_ref.at[rows]
            outs = _gmlp_block(zu, zv, zx, lng, lnb, ws, bs, mk, mv)
            for g in range(GM_GROUPS):
                out[:, g * GM_GROUP_DIM:(g + 1) * GM_GROUP_DIM] = outs[g].astype(cat_ref.dtype)
            base = GM_GROUPS * GM_GROUP_DIM
            for a in range(XA_HEADS):
                out[:, base + a * XA_DIM:base + (a + 1) * XA_DIM] = outs[GM_GROUPS + a].astype(cat_ref.dtype)

    full2 = lambda b, n: (0, 0)
    return pl.pallas_call(
        kern, grid=(bl, nc),
        in_specs=[pl.BlockSpec((R, zw), lambda b, n: (b * nc + n, 0)),
                  pl.BlockSpec(ln_g.shape, full2), pl.BlockSpec(ln_b.shape, full2),
                  pl.BlockSpec(w_s.shape, lambda b, n: (0, 0, 0)), pl.BlockSpec(b_s.shape, full2),
                  pl.BlockSpec((mem_len, kv.shape[1]), lambda b, n: (b, 0))],
        out_specs=pl.BlockSpec((R, cat_w), lambda b, n: (b * nc + n, 0)),
        out_shape=jax.ShapeDtypeStruct((T, cat_w), BF),
        name="gmlp_fwd",
        compiler_params=pltpu.CompilerParams(dimension_semantics=("arbitrary", "arbitrary"), vmem_limit_bytes=VMEM_CAP_BYTES),
    )(z, ln_g, ln_b, w_s, b_s, kv)


def _gmlp_bwd(z, dcat, ln_g, ln_b, w_s, b_s, kv, bl, nc):
    T, zw = z.shape
    mem_len = kv.shape[0] // bl
    cat_w = dcat.shape[1]
    assert nc % GM_SUB == 0
    nc = nc // GM_SUB

    def kern(z_ref, dc_ref, lng_ref, lnb_ref, ws_ref, bs_ref, kv_ref,
             dz_ref, dkv_ref, dlng_ref, dlnb_ref, dws_ref, dbs_ref):
        first = jnp.logical_and(pl.program_id(0) == 0, pl.program_id(1) == 0)

        @pl.when(pl.program_id(1) == 0)
        def _():
            dkv_ref[...] = jnp.zeros(dkv_ref.shape, F32)

        @pl.when(first)
        def _():
            dlng_ref[...] = jnp.zeros(dlng_ref.shape, F32)
            dlnb_ref[...] = jnp.zeros(dlnb_ref.shape, F32)
            dws_ref[...] = jnp.zeros(dws_ref.shape, F32)
            dbs_ref[...] = jnp.zeros(dbs_ref.shape, F32)

        lng, lnb, ws, bs = _gmlp_params(lng_ref, lnb_ref, ws_ref, bs_ref)
        mk, mv = _kv_pieces(kv_ref)
        W = GM_GROUPS * GM_GROUP_DIM
        KW = XA_HEADS * XA_DIM
        for c in range(GM_SUB):
            rows = pl.ds(c * GM_CHUNK, GM_CHUNK)
            zu, zv, zx = _gmlp_pieces(z_ref.at[rows])
            dc, dz = dc_ref.at[rows], dz_ref.at[rows]
            _, vjp = jax.vjp(_gmlp_block, zu, zv, zx, lng, lnb, ws, bs, mk, mv)
            d_outs = [dc[:, g * GM_GROUP_DIM:(g + 1) * GM_GROUP_DIM] for g in range(GM_GROUPS)]
            d_outs += [dc[:, W + a * XA_DIM:W + (a + 1) * XA_DIM] for a in range(XA_HEADS)]
            dzu, dzv, dzx, dlng, dlnb, dws, dbs, dmk, dmv = vjp(d_outs)
            for g in range(GM_GROUPS):
                sl = slice(g * GM_GROUP_DIM, (g + 1) * GM_GROUP_DIM)
                dz[:, sl] = dzu[g].astype(dz_ref.dtype)
                dz[:, W + g * GM_GROUP_DIM:W + (g + 1) * GM_GROUP_DIM] = dzv[g].astype(dz_ref.dtype)
                dlng_ref[:, sl] += dlng[g]
                dlnb_ref[:, sl] += dlnb[g]
                dws_ref[g] += dws[g]
                dbs_ref[g:g + 1, :] += dbs[g]
            for a in range(XA_HEADS):
                dz[:, 2 * W + a * XA_DIM:2 * W + (a + 1) * XA_DIM] = dzx[a].astype(dz_ref.dtype)
                dkv_ref[:, a * XA_DIM:(a + 1) * XA_DIM] += dmk[a]
                dkv_ref[:, KW + a * XA_DIM:KW + (a + 1) * XA_DIM] += dmv[a]

    full2 = lambda b, n: (0, 0)
    full3 = lambda b, n: (0, 0, 0)
    blk = lambda b, n: (b * nc + n, 0)
    return pl.pallas_call(
        kern, grid=(bl, nc),
        in_specs=[pl.BlockSpec((GM_SUB * GM_CHUNK, zw), blk), pl.BlockSpec((GM_SUB * GM_CHUNK, cat_w), blk),
                  pl.BlockSpec(ln_g.shape, full2), pl.BlockSpec(ln_b.shape, full2),
                  pl.BlockSpec(w_s.shape, full3), pl.BlockSpec(b_s.shape, full2),
                  pl.BlockSpec((mem_len, kv.shape[1]), lambda b, n: (b, 0))],
        out_specs=[pl.BlockSpec((GM_SUB * GM_CHUNK, zw), blk),
                   pl.BlockSpec((mem_len, kv.shape[1]), lambda b, n: (b, 0)),
                   pl.BlockSpec(ln_g.shape, full2), pl.BlockSpec(ln_b.shape, full2),
                   pl.BlockSpec(w_s.shape, full3), pl.BlockSpec(b_s.shape, full2)],
        out_shape=[jax.ShapeDtypeStruct((T, zw), BF), jax.ShapeDtypeStruct(kv.shape, F32),
                   jax.ShapeDtypeStruct(ln_g.shape, F32), jax.ShapeDtypeStruct(ln_b.shape, F32),
                   jax.ShapeDtypeStruct(w_s.shape, F32), jax.ShapeDtypeStruct(b_s.shape, F32)],
        name="gmlp_bwd",
        compiler_params=pltpu.CompilerParams(dimension_semantics=("arbitrary", "arbitrary"), vmem_limit_bytes=VMEM_CAP_BYTES),
    )(z, dcat, ln_g, ln_b, w_s, b_s, kv)


def _place():
    x, y, c = lax.axis_index("x"), lax.axis_index("y"), lax.axis_index("c")
    chips = [(1 - x, y), (x, 1 - y), (1 - x, 1 - y)]
    return x, y, c, chips


def _half(ref, kind, e):
    if kind == "col":
        n = ref.shape[1] // 2
        return ref.at[:, pl.ds(pl.multiple_of(e * n, n), n), :]
    n = ref.shape[2] // 2
    return ref.at[:, :, pl.ds(pl.multiple_of(e * n, n), n)]


def _slot(ref, kind, j, n):
    if kind == "col":
        return ref.at[:, :, pl.ds(pl.multiple_of(j * n, n), n)]
    return ref.at[:, pl.ds(pl.multiple_of(j * n, n), n), :]


BF16_TILE_ROWS = 16
AG_DIRECT_SIXTEENTHS = 3


def _allgather_seq(name, items, cid):
    nt = len(items)
    kinds = [k for (_, k, _) in items]
    slot_kind = ["row" if k == "row" else "col" for k in kinds]
    out_type = []
    for s, k, l in items:
        L, r, c = s.shape
        lo = L if l is None else 1
        out_type.append(jax.ShapeDtypeStruct((lo, 4 * r, c) if k == "row" else (lo, r, 4 * c), s.dtype))

    def part(ref, t, e):
        return ref if kinds[t] == "vec" else _half(ref, kinds[t], e)

    def split(half):
        rows = half.shape[1]
        direct = rows * AG_DIRECT_SIXTEENTHS // 16 // BF16_TILE_ROWS * BF16_TILE_ROWS
        return half.at[:, pl.ds(0, rows - direct), :], half.at[:, pl.ds(rows - direct, direct), :]

    def body(*refs):
        sh = [refs[t] if items[t][2] is None else refs[t].at[pl.ds(items[t][2], 1)] for t in range(nt)]
        full = refs[nt:2 * nt]
        s_ici, r_ici, s_far, r_far, s_d2d, r_d2d = refs[2 * nt:]
        x, y, c, chips = _place()
        own = 2 * x + y
        sibling = (x, y, 1 - c)
        barrier = pltpu.get_barrier_semaphore()
        for peer in [(px, py, pc) for (px, py) in chips for pc in (0, 1)] + [sibling]:
            pl.semaphore_signal(barrier, inc=1, device_id=peer, device_id_type=MESH)
        pl.semaphore_wait(barrier, 7)
        width = [sh[t].shape[1] if kinds[t] == "row" else sh[t].shape[2] for t in range(nt)]
        sent = []
        for t in range(nt):
            for p, (px, py) in enumerate(chips):
                src, dst = part(sh[t], t, c), part(_slot(full[t], slot_kind[t], own, width[t]), t, c)
                cp = pltpu.make_async_remote_copy(
                    src_ref=src, dst_ref=dst, send_sem=s_ici.at[t, p], recv_sem=r_ici.at[t, p], device_id=(px, py, c),
                    device_id_type=MESH)
                cp.start()
                sent.append(cp)
                if kinds[t] == "vec":
                    continue
                far = pltpu.make_async_remote_copy(
                    src_ref=split(src)[1], dst_ref=split(dst)[1], send_sem=s_far.at[t, p], recv_sem=r_far.at[t, p],
                    device_id=(px, py, 1 - c), device_id_type=MESH)
                far.start()
                sent.append(far)
        for t in range(nt):
            for p, (px, py) in enumerate(chips):
                landed = part(_slot(full[t], slot_kind[t], 2 * px + py, width[t]), t, c)
                pltpu.make_async_remote_copy(
                    src_ref=landed, dst_ref=landed, send_sem=s_ici.at[t, p], recv_sem=r_ici.at[t, p],
                    device_id=(px, py, c), device_id_type=MESH).wait_recv()
                if kinds[t] == "vec":
                    continue
                fw = pltpu.make_async_remote_copy(
                    src_ref=split(landed)[0], dst_ref=split(landed)[0], send_sem=s_d2d.at[t, p], recv_sem=r_d2d.at[t, p],
                    device_id=sibling, device_id_type=MESH)
                fw.start()
                sent.append(fw)
        for t in range(nt):
            if kinds[t] == "vec":
                continue
            for p, (px, py) in enumerate(chips):
                forwarded, direct = split(_half(_slot(full[t], kinds[t], 2 * px + py, width[t]), kinds[t], 1 - c))
                pltpu.make_async_remote_copy(
                    src_ref=forwarded, dst_ref=forwarded, send_sem=s_d2d.at[t, p], recv_sem=r_d2d.at[t, p],
                    device_id=sibling, device_id_type=MESH).wait_recv()
                pltpu.make_async_remote_copy(
                    src_ref=direct, dst_ref=direct, send_sem=s_far.at[t, p], recv_sem=r_far.at[t, p],
                    device_id=(px, py, 1 - c), device_id_type=MESH).wait_recv()
        for cp in sent:
            cp.wait_send()

    sems = pltpu.SemaphoreType.DMA
    return pl.kernel(
        body, out_type=out_type, mesh=plsc.ScalarSubcoreMesh(axis_name="seq", num_cores=1),
        scratch_types=[sems((nt, 3)), sems((nt, 3)), sems((nt, 3)), sems((nt, 3)), sems((nt, 3)), sems((nt, 3))],
        compiler_params=pltpu.CompilerParams(collective_id=cid), name=name,
    )(*[s for (s, _, _) in items])


def _place_own(name, full, shard, kind, layer, chip_arr, after):
    lo, r, c = (shard.shape[0] if layer is None else 1,) + shard.shape[1:]
    first = 0 if layer is None else layer
    tr = _pick(r, 512)
    nr = r // tr

    def body(chip_ref, s_ref, f_ref, after_ref, o_ref):
        o_ref[...] = s_ref[...]

    if kind == "row":
        out_map = lambda i, j, chip: (i, chip[0] * nr + j, 0)
    else:
        out_map = lambda i, j, chip: (i, j, chip[0])
    return pl.pallas_call(
        body, out_shape=jax.ShapeDtypeStruct(full.shape, full.dtype),
        grid_spec=pltpu.PrefetchScalarGridSpec(
            num_scalar_prefetch=1, grid=(lo, nr),
            in_specs=[pl.BlockSpec((1, tr, c), lambda i, j, chip: (i + first, j, 0)), pl.BlockSpec(memory_space=pl.ANY),
                      pl.BlockSpec(memory_space=pl.ANY)],
            out_specs=pl.BlockSpec((1, tr, c), out_map)),
        input_output_aliases={2: 0},
        compiler_params=pltpu.CompilerParams(dimension_semantics=("parallel", "parallel"), vmem_limit_bytes=VMEM_CAP_BYTES),
        name=name,
    )(chip_arr, shard, full, after)


def _slot2(ref, kind, j, n):
    if kind == "col":
        return ref.at[:, pl.ds(pl.multiple_of(j * n, n), n)]
    return ref.at[pl.ds(pl.multiple_of(j * n, n), n), :]


def _rs_chips_seq(name, parts, kinds, cid):
    nm = len(parts)
    out_type = []
    for g, k in zip(parts, kinds):
        r, c = g.shape
        ps = (r, c // 4) if k == "col" else (r // 4, c)
        out_type += [jax.ShapeDtypeStruct(ps, BF), jax.ShapeDtypeStruct((3,) + ps, BF)]

    def body(*refs):
        g = refs[:nm]
        outs = refs[nm:3 * nm]
        loc, ssem, rsem = refs[3 * nm:]
        x, y, c, chips = _place()
        own = 2 * x + y
        barrier = pltpu.get_barrier_semaphore()
        for (px, py) in chips:
            pl.semaphore_signal(barrier, inc=1, device_id=(px, py, c), device_id_type=MESH)
        pl.semaphore_wait(barrier, 3)
        cps = []
        for m in range(nm):
            k = kinds[m]
            own_o, got_o = outs[2 * m], outs[2 * m + 1]
            n = g[m].shape[1] // 4 if k == "col" else g[m].shape[0] // 4
            lc = pltpu.make_async_copy(_slot2(g[m], k, own, n), own_o, loc.at[m])
            lc.start()
            cps.append(lc)
            for p, (px, py) in enumerate(chips):
                cp = pltpu.make_async_remote_copy(
                    src_ref=_slot2(g[m], k, 2 * px + py, n), dst_ref=got_o.at[p],
                    send_sem=ssem.at[m, p], recv_sem=rsem.at[m, p], device_id=(px, py, c), device_id_type=MESH)
                cp.start()
                cps.append(cp)
        for cp in cps:
            cp.wait()

    return pl.kernel(
        body, out_type=out_type, mesh=plsc.ScalarSubcoreMesh(axis_name="seq", num_cores=1),
        scratch_types=[pltpu.SemaphoreType.DMA((nm,)), pltpu.SemaphoreType.DMA((nm, 3)), pltpu.SemaphoreType.DMA((nm, 3))],
        compiler_params=pltpu.CompilerParams(collective_id=cid), name=name,
    )(*parts)


def _finish_share(name, owns, gots, kind, c_arr):
    L = len(owns)
    r, c = owns[0].shape
    tr = _pick(r, 128 if kind == "col" else 256)
    nb = r // tr
    nq = L * nb

    def chunk_of(l):
        return lambda h, q: jnp.clip(q * (1 - h) + (nq - 1) * h - l * nb, 0, nb - 1)

    ins, in_specs = [], []
    for l in range(L):
        at = chunk_of(l)
        ins += [owns[l], gots[l].reshape(3 * r, c), gots[l].reshape(3 * r, c), gots[l].reshape(3 * r, c)]
        in_specs.append(pl.BlockSpec((tr, c), functools.partial(lambda h, q, cc, at: (at(h, q), 0), at=at)))
        in_specs += [pl.BlockSpec((tr, c), functools.partial(lambda h, q, cc, at, p: (p * nb + at(h, q), 0), at=at, p=p))
                     for p in range(3)]
    if kind == "col":
        out_sd = (L, 2, r, c)
        o_spec = pl.BlockSpec((None, 2, tr, c), lambda h, q, cc: ((q * h) // nb, 0, (q * h) % nb, 0))
    else:
        out_sd = (L * r, 2 * c)
        o_spec = pl.BlockSpec((tr, 2 * c), lambda h, q, cc: (q * h, 0))

    def kern(c_ref, *refs):
        in_refs = refs[:4 * L]
        out_ref, mine, recv, ssem, rsem = refs[4 * L:]
        h, q = pl.program_id(0), pl.program_id(1)
        x, y, cc, _ = _place()

        def swap(qq):
            return pltpu.make_async_remote_copy(src_ref=mine.at[qq], dst_ref=recv.at[qq], send_sem=ssem.at[qq],
                                                recv_sem=rsem.at[qq], device_id=(x, y, 1 - cc), device_id_type=MESH)

        for l in range(L):
            @pl.when(jnp.logical_and(h == 0, q // nb == l))
            def _(l=l):
                o_ref, g0, g1, g2 = in_refs[4 * l:4 * l + 4]
                mine[q] = ((o_ref[...].astype(F32) + g0[...].astype(F32)) + g1[...].astype(F32)) + g2[...].astype(F32)
                swap(q).start()

        @pl.when(h == 1)
        def _():
            swap(q).wait()
            a, b = mine[q], recv[q]
            first = c_ref[0] == 0
            lo, hi = jnp.where(first, a, b), jnp.where(first, b, a)
            if kind == "col":
                out_ref[0] = lo
                out_ref[1] = hi
            else:
                out_ref[:, :c] = lo
                out_ref[:, c:] = hi

    full = pl.pallas_call(
        kern,
        grid_spec=pltpu.PrefetchScalarGridSpec(
            num_scalar_prefetch=1, grid=(2, nq), in_specs=in_specs, out_specs=o_spec,
            scratch_shapes=[pltpu.VMEM((nq, tr, c), F32), pltpu.VMEM((nq, tr, c), F32),
                            pltpu.SemaphoreType.DMA((nq,)), pltpu.SemaphoreType.DMA((nq,))]),
        out_shape=jax.ShapeDtypeStruct(out_sd, F32), name=name,
        compiler_params=pltpu.CompilerParams(dimension_semantics=("arbitrary", "arbitrary"),
                                             vmem_limit_bytes=VMEM_CAP_BYTES),
    )(c_arr, *ins)
    return full.reshape(L, 2 * r, c) if kind == "col" else full.reshape(L, r, 2 * c)


def _small_allreduce(buf, name):
    R = buf.shape[0]
    assert R % 16 == 0
    h = R // 2

    def body(x_ref, o_ref, sib, csum, got, s_a, r_a, s_b, r_b, s_c, r_c):
        x, y, c, chips = _place()
        sibling = (x, y, 1 - c)
        own = 2 * x + y
        swap = pltpu.make_async_remote_copy(src_ref=x_ref, dst_ref=sib, send_sem=s_a, recv_sem=r_a,
                                            device_id=sibling, device_id_type=MESH)
        swap.start()
        swap.wait()
        a, b = x_ref[...], sib[...]
        south = c == 0
        csum[...] = jnp.where(south, a, b) + jnp.where(south, b, a)
        lo = pl.multiple_of(c * h, 8)
        mine = csum.at[pl.ds(lo, h)]
        got[own] = csum[pl.ds(lo, h)]
        sends = []
        for p, (px, py) in enumerate(chips):
            cp = pltpu.make_async_remote_copy(src_ref=mine, dst_ref=got.at[own], send_sem=s_b.at[p], recv_sem=r_b.at[p],
                                              device_id=(px, py, c), device_id_type=MESH)
            cp.start()
            sends.append(cp)
        for cp in sends:
            cp.wait()
        o_ref[pl.ds(lo, h)] = ((got[0] + got[1]) + got[2]) + got[3]
        done = o_ref.at[pl.ds(lo, h)]
        back = pltpu.make_async_remote_copy(src_ref=done, dst_ref=done, send_sem=s_c, recv_sem=r_c,
                                            device_id=sibling, device_id_type=MESH)
        back.start()
        back.wait_send()
        other = o_ref.at[pl.ds(pl.multiple_of((1 - c) * h, 8), h)]
        pltpu.make_async_remote_copy(src_ref=other, dst_ref=other, send_sem=s_c, recv_sem=r_c,
                                     device_id=sibling, device_id_type=MESH).wait_recv()

    vm = pl.BlockSpec(memory_space=pltpu.VMEM)
    return pl.pallas_call(
        body, out_shape=jax.ShapeDtypeStruct(buf.shape, F32), in_specs=[vm], out_specs=vm,
        scratch_shapes=[pltpu.VMEM((R, LANES), F32), pltpu.VMEM((R, LANES), F32), pltpu.VMEM((4, h, LANES), F32),
                        pltpu.SemaphoreType.DMA, pltpu.SemaphoreType.DMA, pltpu.SemaphoreType.DMA((3,)),
                        pltpu.SemaphoreType.DMA((3,)), pltpu.SemaphoreType.DMA, pltpu.SemaphoreType.DMA],
        name=name,
        compiler_params=pltpu.CompilerParams(vmem_limit_bytes=VMEM_CAP_BYTES),
    )(buf)


PACK_TILE_ROWS = 8


def _item_rows(shape):
    n = 1
    for d in shape:
        n *= d
    return -(-n // (PACK_TILE_ROWS * LANES)) * PACK_TILE_ROWS


def _pack(arrs, rows_total):
    buf = jnp.zeros((rows_total, LANES), F32)
    r = 0
    for a in arrs:
        f = a.reshape(-1).astype(F32)
        nr = _item_rows(a.shape)
        block = jnp.pad(f, (0, nr * LANES - f.shape[0])).reshape(nr, LANES)
        buf = lax.dynamic_update_slice(buf, block, (r, 0))
        r += nr
    return buf


def _unpack(buf, shapes):
    out, r = [], 0
    for s in shapes:
        n = 1
        for d in s:
            n *= d
        nr = _item_rows(s)
        out.append(buf[r:r + nr].reshape(-1)[:n].reshape(s))
        r += nr
    return out


def _rows_needed(shapes):
    return -(-sum(_item_rows(s) for s in shapes) // (2 * PACK_TILE_ROWS)) * (2 * PACK_TILE_ROWS)


def _two_rows(a, b):
    out = jnp.zeros((2, a.shape[1]), a.dtype)
    return lax.dynamic_update_slice(lax.dynamic_update_slice(out, a, (0, 0)), b, (1, 0))


def _adam(w, g, m, v):
    m = ADAM_B1 * m + (1.0 - ADAM_B1) * g
    v = ADAM_B2 * v + (1.0 - ADAM_B2) * jnp.square(g)
    m_hat = m / (1.0 - ADAM_B1 ** ADAM_STEP)
    v_hat = v / (1.0 - ADAM_B2 ** ADAM_STEP)
    delta = -ADAM_LR * (m_hat / (jnp.sqrt(v_hat) + ADAM_EPS) + ADAM_WD * w)
    return delta, m, v


def _adam_call(name, w2, g2, m2, v2, tr, pass_grad=False):
    def fn(rv, cv):
        outs = list(_adam(*rv))
        return ([rv[1]] + outs if pass_grad else outs), []

    width = w2.shape[1]
    return _rowcall(name, fn, [(w2, 0, width), (g2, 0, width), (m2, 0, width), (v2, 0, width)], [],
                    [(width, F32)] * (4 if pass_grad else 3), [], tr)


def kernel(x, mem, mem_norm, lb_logits, ffn1_norm, ffn1_w_in, ffn1_w_out, mix_norm, mem_w_kv, hgrn_w_in, hgrn_gnorm, hgrn_w_out, gmlp_w_in, gmlp_ln_g, gmlp_ln_b, gmlp_w_s, gmlp_b_s, gmlp_w_out, ffn2_norm, ffn2_w_in, ffn2_w_out, final_norm, loss_target, m_mem_norm, m_lb_logits, m_ffn1_norm, m_ffn1_w_in, m_ffn1_w_out, m_mix_norm, m_mem_w_kv, m_hgrn_w_in, m_hgrn_gnorm, m_hgrn_w_out, m_gmlp_w_in, m_gmlp_ln_g, m_gmlp_ln_b, m_gmlp_w_s, m_gmlp_b_s, m_gmlp_w_out, m_ffn2_norm, m_ffn2_w_in, m_ffn2_w_out, m_final_norm, v_mem_norm, v_lb_logits, v_ffn1_norm, v_ffn1_w_in, v_ffn1_w_out, v_mix_norm, v_mem_w_kv, v_hgrn_w_in, v_hgrn_gnorm, v_hgrn_w_out, v_gmlp_w_in, v_gmlp_ln_g, v_gmlp_ln_b, v_gmlp_w_s, v_gmlp_b_s, v_gmlp_w_out, v_ffn2_norm, v_ffn2_w_in, v_ffn2_w_out, v_final_norm):
    bl, seq, D = x.shape
    T = bl * seq
    mem_len = mem.shape[1]
    chip = 2 * lax.axis_index("x") + lax.axis_index("y")
    c_arr = lax.axis_index("c").astype(jnp.int32).reshape(1)
    chip_arr = chip.astype(jnp.int32).reshape(1)
    TR = 1024

    big = [("ffn1_w_in", ffn1_w_in, "col"), ("ffn1_w_out", ffn1_w_out, "row"), ("mem_w_kv", mem_w_kv, "col"),
           ("hgrn_w_in", hgrn_w_in, "col"), ("hgrn_w_out", hgrn_w_out, "row"), ("gmlp_w_in", gmlp_w_in, "col"),
           ("gmlp_w_out", gmlp_w_out, "row"), ("ffn2_w_in", ffn2_w_in, "col"), ("ffn2_w_out", ffn2_w_out, "row")]
    kinds = [k for (_, _, k) in big]
    shards_bf = []
    for nm, w, _ in big:
        L, r, c = w.shape
        (wb,) = _rowcall("cast_" + nm, lambda rv, cv: ([rv[0]], []), [(w.reshape(L * r, c), 0, c)], [], [(c, BF)], [], 512)
        shards_bf.append(wb.reshape(L, r, c))
    sb = dict(zip([nm for (nm, _, _) in big], shards_bf))
    groups = [[("ffn1_w_in", 0)], [("ffn1_w_out", 0)], [("hgrn_w_in", None)], [("mem_w_kv", None)], [("hgrn_w_out", None)],
              [("ffn2_w_in", 0), ("ffn2_w_out", 0), ("gmlp_ln_g", None), ("gmlp_ln_b", None)],
              [("ffn1_w_in", 1), ("ffn1_w_out", 1)],
              [("gmlp_w_in", None), ("gmlp_w_out", None)],
              [("ffn2_w_in", 1), ("ffn2_w_out", 1)]]
    kind_of = {nm: k for (nm, _, k) in big}
    for nm, vec in (("gmlp_ln_g", gmlp_ln_g), ("gmlp_ln_b", gmlp_ln_b)):
        sb[nm] = vec.reshape(1, 1, -1)
        kind_of[nm] = "vec"
    gathered = {nm: [None, None] for nm in ("ffn1_w_in", "ffn1_w_out", "ffn2_w_in", "ffn2_w_out")}
    others = {}
    for gi, grp in enumerate(groups):
        outs = _allgather_seq("gather_%d" % gi, [(sb[nm], kind_of[nm], l) for (nm, l) in grp], gi)
        for (nm, l), o in zip(grp, outs):
            others[(nm, l)] = o

    def whole(nm, l, after):
        full = _place_own("own_%s_%d" % (nm, l or 0), others[(nm, l)], sb[nm], "row" if kind_of[nm] == "row" else "col", l,
                          chip_arr, after)
        if l is None:
            gathered[nm] = full
        else:
            gathered[nm][l] = full
        return full

    def rms_fwd(name, xin, g):
        (h,) = _rowcall(name, lambda rv, cv: ([_rmsnorm(rv[0], cv[0])], []), [(xin, 0, D)], [g.reshape(1, D)], [(D, BF)], [], TR)
        return h

    def ffn_fwd(tag, xin, h, nm_in, nm_out, layer, next_gain):
        w_in = whole(nm_in, layer, h)
        dff = w_in.shape[2] // 2
        zg, zu, a = _ffn_in_swiglu("ffn_in_" + tag, h, w_in, 1024, dff // 2)
        out = _mm("ffn_out_" + tag, a, whole(nm_out, layer, a), "nn", F32, 512, 1024, dff, scale=0.5, res=xin, b_lead=0,
                  norm_gain=None if next_gain is None else next_gain.reshape(1, D))
        xo, h_next = (out, None) if next_gain is None else out
        return xo, h_next, (xin, h, zg, zu, a)

    def ffn_bwd(tag, dxo, saved, g, w_in, w_out, layer):
        xin, h, zg, zu, a = saved
        dff = w_out[layer].shape[1]
        dw_out = _mm_tn_pair("ffn_dwo_" + tag, a, dxo, "row", c_arr, dff // 2, T, scale=0.5)
        dz = _ffn_da_swiglu("ffn_da_" + tag, dxo, w_out[layer], zg, zu, 512)
        dw_in = _mm_tn_pair("ffn_dwi_" + tag, h, dz, "col", c_arr, 512, T)
        dx, dg = _mm_dh_rms("ffn_dh_" + tag, dz, w_in[layer], xin, g.reshape(1, D), dxo, 512)
        return dx, dg, dw_in, dw_out

    def rms_bwd(name, xin, g, dh, dres):
        def fn(rv, cv):
            _, vjp = jax.vjp(_rmsnorm, rv[0], cv[0])
            dx, dg = vjp(rv[1])
            if dres is not None:
                dx = dx + rv[2]
            return [dx], [dg]

        rows = [(xin, 0, D), (dh, 0, D)] + ([(dres, 0, D)] if dres is not None else [])
        dx, dg = _rowcall(name, fn, rows, [g.reshape(1, D)], [(D, F32)], [((1, D), F32)], TR)
        return dx, dg

    x0 = x.reshape(T, D)
    tgt = loss_target.reshape(T, D)
    mem2 = mem.reshape(bl * mem_len, D)
    memn = rms_fwd("rms_mem", mem2, mem_norm)

    h_f10 = rms_fwd("rms_f1l0", x0, ffn1_norm[0])
    x1, h_m0, sv_f10 = ffn_fwd("f1l0", x0, h_f10, "ffn1_w_in", "ffn1_w_out", 0, mix_norm[0])
    z_m0 = _mm("mix_in_0", h_m0, whole("hgrn_w_in", None, h_m0), "nn", F32, 2048, 512, D, b_lead=0)
    w_kv = whole("mem_w_kv", None, z_m0)
    kv = [_mm("kv_%d" % i, memn, w_kv, "nn", F32, 512, 512, D, b_lead=i) for i in range(2)]
    cat0, stash0 = _hgrn_fwd2(z_m0, lb_logits, hgrn_gnorm, kv[0], bl, seq)
    x2, h_f20 = _mm("mix_out_0", cat0, whole("hgrn_w_out", None, cat0), "nn", F32, 512, 1024, cat0.shape[1], res=x1, b_lead=0,
                    norm_gain=ffn2_norm[0].reshape(1, D))
    x3, h_f11, sv_f20 = ffn_fwd("f2l0", x2, h_f20, "ffn2_w_in", "ffn2_w_out", 0, ffn1_norm[1])
    x4, h_m1, sv_f11 = ffn_fwd("f1l1", x3, h_f11, "ffn1_w_in", "ffn1_w_out", 1, mix_norm[1])
    z_m1 = _mm("mix_in_1", h_m1, whole("gmlp_w_in", None, h_m1), "nn", F32, 2048, 512, D, b_lead=0)
    nc1 = seq // GM_CHUNK
    w_s, b_s = gmlp_w_s[0], gmlp_b_s[0]
    ln_w = GM_GROUPS * GM_GROUP_DIM
    ln_g_full, ln_b_full = [whole(nm, None, z_m1).reshape(1, ln_w) for nm in ("gmlp_ln_g", "gmlp_ln_b")]
    cat1 = _gmlp_fwd(z_m1, ln_g_full, ln_b_full, w_s, b_s, kv[1], bl, nc1)
    x5, h_f21 = _mm("mix_out_1", cat1, whole("gmlp_w_out", None, cat1), "nn", F32, 512, 1024, cat1.shape[1], res=x4, b_lead=0,
                    norm_gain=ffn2_norm[1].reshape(1, D))
    x6, _, sv_f21 = ffn_fwd("f2l1", x5, h_f21, "ffn2_w_in", "ffn2_w_out", 1, None)

    def head(rv, cv):
        def f(xx, gg):
            err = _rmsnorm(xx, gg) - rv[1]
            return 0.5 * jnp.sum(jnp.mean(err * err, axis=-1, keepdims=True), axis=0, keepdims=True)

        ls, vjp = jax.vjp(f, rv[0], cv[0])
        dx, dg = vjp(jnp.ones((1, 1), F32))
        return [dx], [dg, jnp.broadcast_to(ls, (1, 128))]

    dx6, d_final, loss_part = _rowcall("loss_head", head, [(x6, 0, D), (tgt, 0, D)], [final_norm.reshape(1, D)],
                                       [(D, F32)], [((1, D), F32), ((1, 128), F32)], TR)

    rs_out = {}
    n_gather = len(groups)

    def rs(gi, items):
        outs = _rs_chips_seq("reduce_%d" % gi, [p for (_, p, _) in items], [k for (_, _, k) in items], n_gather + gi)
        for i, (key, _, _) in enumerate(items):
            rs_out[key] = (outs[2 * i], outs[2 * i + 1])

    dx5, dg_f21, dwi_f21, dwo_f21 = ffn_bwd("f2l1", dx6, sv_f21, ffn2_norm[1], gathered["ffn2_w_in"], gathered["ffn2_w_out"], 1)
    rs(0, [(("ffn2_w_out", 1), dwo_f21, "row"), (("ffn2_w_in", 1), dwi_f21, "col")])
    dcat1 = _mm("mix_dcat_1", dx5, gathered["gmlp_w_out"], "nt", F32, 2048, 1024, D, b_lead=0)
    dwo_m1 = _mm_tn_pair("mix_dwo_1", cat1, dx5, "row", c_arr, 1024, T)
    dz_m1, dkv1, d_lng, d_lnb, d_ws, d_bs = _gmlp_bwd(z_m1, dcat1, ln_g_full, ln_b_full, w_s, b_s, kv[1], bl, nc1)
    dx4, dg_m1 = _mm_dh_rms("mix_dh_1", dz_m1, gathered["gmlp_w_in"], x4, mix_norm[1].reshape(1, D), dx5, 512)
    dwi_m1 = _mm_tn_pair("mix_dwi_1", h_m1, dz_m1, "col", c_arr, 1024, T)
    rs(1, [(("gmlp_w_out", 0), dwo_m1, "row"), (("gmlp_w_in", 0), dwi_m1, "col")])
    dx3, dg_f11, dwi_f11, dwo_f11 = ffn_bwd("f1l1", dx4, sv_f11, ffn1_norm[1], gathered["ffn1_w_in"], gathered["ffn1_w_out"], 1)
    rs(2, [(("ffn1_w_out", 1), dwo_f11, "row"), (("ffn1_w_in", 1), dwi_f11, "col")])

    dx2, dg_f20, dwi_f20, dwo_f20 = ffn_bwd("f2l0", dx3, sv_f20, ffn2_norm[0], gathered["ffn2_w_in"], gathered["ffn2_w_out"], 0)
    rs(3, [(("ffn2_w_out", 0), dwo_f20, "row"), (("ffn2_w_in", 0), dwi_f20, "col")])
    dcat0 = _mm("mix_dcat_0", dx2, gathered["hgrn_w_out"], "nt", F32, 2048, 1024, D, b_lead=0)
    dwo_m0 = _mm_tn_pair("mix_dwo_0", cat0, dx2, "row", c_arr, 1024, T)
    dz_m0, dkv0, d_lb, d_gn = _hgrn_bwd2(z_m0, dcat0, stash0, lb_logits, hgrn_gnorm, kv[0], bl, seq)
    dx1, dg_m0 = _mm_dh_rms("mix_dh_0", dz_m0, gathered["hgrn_w_in"], x1, mix_norm[0].reshape(1, D), dx2, 512)
    dwi_m0 = _mm_tn_pair("mix_dwi_0", h_m0, dz_m0, "col", c_arr, 1024, T)
    rs(4, [(("hgrn_w_out", 0), dwo_m0, "row"), (("hgrn_w_in", 0), dwi_m0, "col")])

    dwkv = [_mm_tn_pair("kv_dw_%d" % i, memn, dkv, "col", c_arr, 1024, 512) for i, dkv in enumerate([dkv0, dkv1])]
    rs(5, [(("mem_w_kv", 0), dwkv[0], "col"), (("mem_w_kv", 1), dwkv[1], "col")])
    dmemn = _mm("kv_dx_0", dkv0, gathered["mem_w_kv"], "nt", F32, 512, 512, 1024, b_lead=0)
    dmemn = _mm("kv_dx_1", dkv1, gathered["mem_w_kv"], "nt", F32, 512, 512, 1024, res=dmemn, b_lead=1)
    _, d_memnorm = rms_bwd("rms_bwd_mem", mem2, mem_norm, dmemn, None)

    dx0, dg_f10, dwi_f10, dwo_f10 = ffn_bwd("f1l0", dx1, sv_f10, ffn1_norm[0], gathered["ffn1_w_in"], gathered["ffn1_w_out"], 0)
    rs(6, [(("ffn1_w_out", 0), dwo_f10, "row")])
    rs(7, [(("ffn1_w_in", 0), dwi_f10, "col")])

    shard_grads = [_finish_share("finish_" + nm, [rs_out[(nm, l)][0] for l in range(w.shape[0])],
                                 [rs_out[(nm, l)][1] for l in range(w.shape[0])], k, c_arr) for (nm, w, k) in big]

    big_w = [w for (_, w, _) in big]
    big_m = [m_ffn1_w_in, m_ffn1_w_out, m_mem_w_kv, m_hgrn_w_in, m_hgrn_w_out, m_gmlp_w_in, m_gmlp_w_out, m_ffn2_w_in, m_ffn2_w_out]
    big_v = [v_ffn1_w_in, v_ffn1_w_out, v_mem_w_kv, v_hgrn_w_in, v_hgrn_w_out, v_gmlp_w_in, v_gmlp_w_out, v_ffn2_w_in, v_ffn2_w_out]
    big_out = {}
    for (nm, w, _), g, m, v in zip(big, shard_grads, big_m, big_v):
        L, r, c = w.shape
        g2, d2, m2, v2 = _adam_call("adam_" + nm, w.reshape(L * r, c), g.reshape(L * r, c), m.reshape(L * r, c),
                                    v.reshape(L * r, c), 256, pass_grad=True)
        big_out[nm] = (g2.reshape(w.shape), d2.reshape(w.shape), m2.reshape(w.shape), v2.reshape(w.shape))

    d_ffn1n = _two_rows(dg_f10, dg_f11)
    d_mixn = _two_rows(dg_m0, dg_m1)
    d_ffn2n = _two_rows(dg_f20, dg_f21)
    small_parts = [loss_part[:, :1], d_memnorm, d_lb, d_ffn1n, d_mixn, d_gn, d_lng, d_lnb, d_ws, d_bs, d_ffn2n, d_final]
    red_shapes = [(1,), mem_norm.shape, lb_logits.shape, ffn1_norm.shape, mix_norm.shape, hgrn_gnorm.shape, (1, ln_w), (1, ln_w),
                  gmlp_w_s.shape, gmlp_b_s.shape, ffn2_norm.shape, final_norm.shape]
    red = _small_allreduce(_pack(small_parts, _rows_needed(red_shapes)), "reduce_small")
    (loss_v, g_memn, g_lb, g_f1n, g_mixn, g_gn, g_lng_full, g_lnb_full, g_ws, g_bs, g_f2n, g_fin) = _unpack(red, red_shapes)
    lsh = gmlp_ln_g.shape[1]
    g_lng = lax.dynamic_slice(g_lng_full, (0, chip * lsh), (1, lsh))
    g_lnb = lax.dynamic_slice(g_lnb_full, (0, chip * lsh), (1, lsh))
    small_w = [mem_norm, lb_logits, ffn1_norm, mix_norm, hgrn_gnorm, gmlp_ln_g, gmlp_ln_b, gmlp_w_s, gmlp_b_s, ffn2_norm, final_norm]
    small_g = [g_memn, g_lb, g_f1n, g_mixn, g_gn, g_lng, g_lnb, g_ws, g_bs, g_f2n, g_fin]
    small_m = [m_mem_norm, m_lb_logits, m_ffn1_norm, m_mix_norm, m_hgrn_gnorm, m_gmlp_ln_g, m_gmlp_ln_b, m_gmlp_w_s, m_gmlp_b_s, m_ffn2_norm, m_final_norm]
    small_v = [v_mem_norm, v_lb_logits, v_ffn1_norm, v_mix_norm, v_hgrn_gnorm, v_gmlp_ln_g, v_gmlp_ln_b, v_gmlp_w_s, v_gmlp_b_s, v_ffn2_norm, v_final_norm]
    sshapes = [w.shape for w in small_w]
    nrow = _rows_needed(sshapes)
    d_p, m_p, v_p = _adam_call("adam_small", _pack(small_w, nrow), _pack(small_g, nrow), _pack(small_m, nrow), _pack(small_v, nrow), nrow)
    s_delta, s_m, s_v = _unpack(d_p, sshapes), _unpack(m_p, sshapes), _unpack(v_p, sshapes)
    small_names = ["mem_norm", "lb_logits", "ffn1_norm", "mix_norm", "hgrn_gnorm", "gmlp_ln_g", "gmlp_ln_b", "gmlp_w_s", "gmlp_b_s", "ffn2_norm", "final_norm"]
    small_out = {nm: (g.reshape(w.shape), d, m, v) for nm, w, g, d, m, v in zip(small_names, small_w, small_g, s_delta, s_m, s_v)}

    order = ["mem_norm", "lb_logits", "ffn1_norm", "ffn1_w_in", "ffn1_w_out", "mix_norm", "mem_w_kv", "hgrn_w_in", "hgrn_gnorm",
             "hgrn_w_out", "gmlp_w_in", "gmlp_ln_g", "gmlp_ln_b", "gmlp_w_s", "gmlp_b_s", "gmlp_w_out", "ffn2_norm", "ffn2_w_in",
             "ffn2_w_out", "final_norm"]
    allo = {**big_out, **small_out}
    grad_x = dx0.reshape(x.shape)
    return (loss_v.reshape(()), grad_x, *[allo[n][0] for n in order], *[allo[n][1] for n in order],
            *[allo[n][2] for n in order], *[allo[n][3] for n in order])
```

```python
import functools

import jax
import jax.numpy as jnp
from jax import lax
from jax.experimental import pallas as pl
from jax.experimental.pallas import tpu as pltpu
from jax.experimental.pallas import tpu_sc as plsc

BF = jnp.bfloat16
F32 = jnp.float32
MESH = pl.DeviceIdType.MESH

EPS = 1e-6
D_MODEL = 1024
HG_HEADS = 8
HG_DIM = 128
HG_CHUNK = 64
GM_CHUNK = 128
GM_GROUPS = 8
GM_GROUP_DIM = 256
XA_HEADS = 4
XA_DIM = 256
ADAM_LR = 0.001
ADAM_B1 = 0.9
ADAM_B2 = 0.999
ADAM_EPS = 1e-08
ADAM_WD = 0.01
ADAM_STEP = 10

VMEM_CAP_BYTES = 60 * 1024 * 1024
LANES = 1024


def _pick(n, cap, mult=16):
    if n <= cap:
        return n
    for d in range(cap - cap % mult, 0, -mult):
        if n % d == 0:
            return d
    raise ValueError((n, cap, mult))


def _dg(a, b, ca, cb):
    return lax.dot_general(a.astype(BF), b.astype(BF), (((ca,), (cb,)), ((), ())), preferred_element_type=F32)


@jax.custom_vjp
def dot_nn(a, b):
    return _dg(a, b, 1, 0)


def _nn_fwd(a, b):
    return _dg(a, b, 1, 0), (a, b)


def _nn_bwd(r, g):
    a, b = r
    return _dg(g, b, 1, 1), _dg(a, g, 0, 0)


dot_nn.defvjp(_nn_fwd, _nn_bwd)


@jax.custom_vjp
def dot_nt(a, b):
    return _dg(a, b, 1, 1)


def _nt_fwd(a, b):
    return _dg(a, b, 1, 1), (a, b)


def _nt_bwd(r, g):
    a, b = r
    return _dg(g, b, 1, 0), _dg(g, a, 0, 0)


dot_nt.defvjp(_nt_fwd, _nt_bwd)


@jax.custom_vjp
def dot_tn(a, b):
    return _dg(a, b, 0, 0)


def _tn_fwd(a, b):
    return _dg(a, b, 0, 0), (a, b)


def _tn_bwd(r, g):
    a, b = r
    return _dg(b, g, 1, 1), _dg(a, g, 1, 0)


dot_tn.defvjp(_tn_fwd, _tn_bwd)


def _rmsnorm(x, g):
    return x * lax.rsqrt(jnp.mean(x * x, axis=-1, keepdims=True) + EPS) * g


def _silu(x):
    return x * jax.nn.sigmoid(x)


@jax.custom_vjp
def _gelu(x):
    return 0.5 * x * (1.0 + lax.erf(x * (0.5 ** 0.5)))


def _gelu_fwd(x):
    return _gelu(x), x


def _gelu_bwd(x, g):
    t = x * (0.5 ** 0.5)
    cdf = 0.5 * (1.0 + lax.erf(t))
    return (g * (cdf + x * (jnp.exp(-(t * t)) * (0.5 / 3.141592653589793) ** 0.5)),)


_gelu.defvjp(_gelu_fwd, _gelu_bwd)


def _softmax_last(s):
    m = lax.stop_gradient(jnp.max(s, axis=-1, keepdims=True))
    e = jnp.exp(s - m)
    return e / jnp.sum(e, axis=-1, keepdims=True)


def _tril(n):
    r = lax.broadcasted_iota(jnp.int32, (n, n), 0)
    c = lax.broadcasted_iota(jnp.int32, (n, n), 1)
    return r >= c


def _attention(zx, mk, mv):
    s = dot_nt(zx, mk) * (XA_DIM ** -0.5)
    return dot_nn(_softmax_last(s), mv)


def _chunk_sums(x, suffix):
    n = x.shape[0]
    r = lax.broadcasted_iota(jnp.int32, (n, n), 0)
    c = lax.broadcasted_iota(jnp.int32, (n, n), 1)
    tri = jnp.logical_and(r <= c if suffix else r >= c, r // HG_CHUNK == c // HG_CHUNK).astype(BF)
    hi = x.astype(BF)
    rest = x - hi.astype(F32)
    mid = rest.astype(BF)
    lo = (rest - mid.astype(F32)).astype(BF)
    return (_dg(tri, hi, 1, 0) + _dg(tri, mid, 1, 0)) + _dg(tri, lo, 1, 0)


@jax.custom_vjp
def _running_sums(x):
    return _chunk_sums(x, False)


_running_sums.defvjp(lambda x: (_chunk_sums(x, False), None), lambda _, g: (_chunk_sums(g, True),))


def _hgrn_decays(zf, lb3):
    l0, l1, l2 = lb3[0:1], lb3[1:2], lb3[2:3]
    m = lax.stop_gradient(jnp.maximum(jnp.maximum(l0, l1), l2))
    e0 = jnp.exp(l0 - m)
    lb = e0 / (e0 + jnp.exp(l1 - m) + jnp.exp(l2 - m))
    f = lb + (1.0 - lb) * jax.nn.sigmoid(zf)
    return f, _running_sums(jnp.log(f))


def _hgrn_head(zq, f, b, zi, zg, gn, S):
    q = _silu(zq)
    k = 1.0 - f
    b_last = b[HG_CHUNK - 1:HG_CHUNK, :]
    q_dec = q * jnp.exp(b)
    k_inv = k * jnp.exp(-b)
    a = jnp.where(_tril(HG_CHUNK), dot_nt(q_dec, k_inv), 0.0)
    o = dot_nn(a, zi) + dot_nn(q_dec, S)
    S_new = jnp.exp(b_last).reshape(HG_DIM, 1) * S + dot_tn(k * jnp.exp(b_last - b), zi)
    o = _rmsnorm(o, gn) * _silu(zg)
    return o, S_new


def _gmlp_block(zu, zv, zx, lng, lnb, ws, bs, mk, mv):
    gv = [_gelu(v) for v in zv]
    width = GM_GROUPS * GM_GROUP_DIM
    mu = sum(jnp.sum(g, axis=-1, keepdims=True) for g in gv) / width
    xc = [g - mu for g in gv]
    var = sum(jnp.sum(c * c, axis=-1, keepdims=True) for c in xc) / width
    r = lax.rsqrt(var + EPS)
    outs = []
    for g in range(GM_GROUPS):
        v = xc[g] * r * lng[g] + lnb[g]
        w = jnp.where(_tril(GM_CHUNK), ws[g], 0.0)
        mixed = dot_nn(w, v) + bs[g].reshape(GM_CHUNK, 1)
        outs.append(_gelu(zu[g]) * mixed)
    for a in range(XA_HEADS):
        outs.append(_attention(zx[a], mk[a], mv[a]))
    return outs


def _rowcall(name, fn, rows, consts, row_outs, acc_outs, tr):
    nrows = rows[0][0].shape[0]
    tr = _pick(nrows, tr)
    n_r, n_c, n_ro, n_ao = len(rows), len(consts), len(row_outs), len(acc_outs)

    def kern(*refs):
        rv = [r[...] for r in refs[:n_r]]
        cv = [r[...] for r in refs[n_r:n_r + n_c]]
        ro_refs = refs[n_r + n_c:n_r + n_c + n_ro]
        ao_refs = refs[n_r + n_c + n_ro:]
        ro, ao = fn(rv, cv)
        for ref, v in zip(ro_refs, ro):
            ref[...] = v.astype(ref.dtype)
        if n_ao:
            @pl.when(pl.program_id(0) == 0)
            def _():
                for ref in ao_refs:
                    ref[...] = jnp.zeros(ref.shape, ref.dtype)

            for ref, v in zip(ao_refs, ao):
                ref[...] += v.astype(ref.dtype)

    in_specs = [pl.BlockSpec((tr, w), functools.partial(lambda i, cb: (i, cb), cb=cb)) for (_, cb, w) in rows]
    in_specs += [pl.BlockSpec(c.shape, lambda i: (0, 0)) for c in consts]
    out_specs = [pl.BlockSpec((tr, w), lambda i: (i, 0)) for (w, _) in row_outs]
    out_specs += [pl.BlockSpec(s, lambda i: (0, 0)) for (s, _) in acc_outs]
    out_shape = [jax.ShapeDtypeStruct((nrows, w), dt) for (w, dt) in row_outs]
    out_shape += [jax.ShapeDtypeStruct(s, dt) for (s, dt) in acc_outs]
    outs = pl.pallas_call(
        kern, grid=(nrows // tr,), in_specs=in_specs, out_specs=out_specs, out_shape=out_shape, name=name,
        compiler_params=pltpu.CompilerParams(dimension_semantics=("arbitrary",),
                                             vmem_limit_bytes=VMEM_CAP_BYTES),
    )(*[a for (a, _, _) in rows], *consts)
    return outs


def _mm(name, a, b, mode, out_dtype, tm, tn, tk, scale=1.0, res=None, a_lead=None, b_lead=None, norm_gain=None):
    ash = a.shape[-2:]
    bsh = b.shape[-2:]
    if mode == "nn":
        (M, K), (K2, N) = ash, bsh
    elif mode == "nt":
        (M, K), (N, K2) = ash, bsh
    else:
        (K, M), (K2, N) = ash, bsh
    assert K == K2, (name, a.shape, b.shape)
    tm, tn, tk = min(tm, M), min(tn, N), min(tk, K)
    assert M % tm == 0 and N % tn == 0 and K % tk == 0, (name, M, N, K, tm, tn, tk)
    nk = K // tk
    dims = {"nn": (1, 0), "nt": (1, 1), "tn": (0, 0)}[mode]

    def lead(spec_shape, index_fn, lead_idx):
        if lead_idx is None:
            return pl.BlockSpec(spec_shape, index_fn)
        return pl.BlockSpec((None,) + spec_shape, lambda i, j, k: (lead_idx,) + index_fn(i, j, k))

    if mode == "tn":
        a_spec = lead((tk, tm), lambda i, j, k: (k, i), a_lead)
    else:
        a_spec = lead((tm, tk), lambda i, j, k: (i, k), a_lead)
    if mode == "nt":
        b_spec = lead((tn, tk), lambda i, j, k: (j, k), b_lead)
    else:
        b_spec = lead((tk, tn), lambda i, j, k: (k, j), b_lead)
    o_spec = pl.BlockSpec((tm, tn), lambda i, j, k: (i, j))
    has_res = res is not None
    has_norm = norm_gain is not None
    assert not has_norm or tn == N

    def kern(*refs):
        a_ref, b_ref = refs[0], refs[1]
        pos = 2
        res_ref = gain_ref = h_ref = None
        if has_res:
            res_ref, pos = refs[pos], pos + 1
        if has_norm:
            gain_ref, pos = refs[pos], pos + 1
        o_ref, pos = refs[pos], pos + 1
        if has_norm:
            h_ref = refs[pos]
        acc_ref = refs[-1] if nk > 1 else None
        p = lax.dot_general(a_ref[...].astype(BF), b_ref[...].astype(BF), (((dims[0],), (dims[1],)), ((), ())),
                            preferred_element_type=F32)

        def finish(v):
            if scale != 1.0:
                v = v * scale
            if has_res:
                v = res_ref[...] + v
            o_ref[...] = v.astype(o_ref.dtype)
            if has_norm:
                h_ref[...] = _rmsnorm(v, gain_ref[...]).astype(h_ref.dtype)

        if nk == 1:
            finish(p)
        else:
            k = pl.program_id(2)

            @pl.when(k == 0)
            def _():
                acc_ref[...] = p

            @pl.when(k > 0)
            def _():
                acc_ref[...] += p

            @pl.when(k == nk - 1)
            def _():
                finish(acc_ref[...])

    ins = [a, b] + ([res] if has_res else []) + ([norm_gain] if has_norm else [])
    in_specs = [a_spec, b_spec] + ([o_spec] if has_res else [])
    in_specs += [pl.BlockSpec((1, N), lambda i, j, k: (0, 0))] if has_norm else []
    out_sd = jax.ShapeDtypeStruct((M, N), out_dtype)
    return pl.pallas_call(
        kern, grid=(M // tm, N // tn, nk), in_specs=in_specs,
        out_specs=[o_spec, o_spec] if has_norm else o_spec,
        out_shape=[out_sd, jax.ShapeDtypeStruct((M, N), BF)] if has_norm else out_sd,
        scratch_shapes=[pltpu.VMEM((tm, tn), F32)] if nk > 1 else [],
        name=name,
        compiler_params=pltpu.CompilerParams(dimension_semantics=("parallel", "parallel", "arbitrary"),
                                             vmem_limit_bytes=VMEM_CAP_BYTES),
    )(*ins)


def _ffn_in_swiglu(name, h, w3, tm, tn):
    T, D = h.shape
    dff = w3.shape[2] // 2
    tm = min(tm, T)
    assert T % tm == 0 and dff % tn == 0
    nj = dff // tn

    def kern(h_ref, wg_ref, wu_ref, zg_ref, zu_ref, a_ref):
        hb = h_ref[...]
        g = jnp.dot(hb, wg_ref[...], preferred_element_type=F32).astype(BF)
        u = jnp.dot(hb, wu_ref[...], preferred_element_type=F32).astype(BF)
        zg_ref[...] = g
        zu_ref[...] = u
        a_ref[...] = (_silu(g.astype(F32)) * u.astype(F32)).astype(BF)

    o_spec = pl.BlockSpec((tm, tn), lambda i, j: (i, j))
    return pl.pallas_call(
        kern, grid=(T // tm, nj),
        in_specs=[pl.BlockSpec((tm, D), lambda i, j: (i, 0)),
                  pl.BlockSpec((None, D, tn), lambda i, j: (0, 0, j)),
                  pl.BlockSpec((None, D, tn), lambda i, j: (0, 0, j + nj))],
        out_specs=[o_spec, o_spec, o_spec],
        out_shape=[jax.ShapeDtypeStruct((T, dff), BF)] * 3, name=name,
        compiler_params=pltpu.CompilerParams(dimension_semantics=("parallel", "arbitrary"),
                                             vmem_limit_bytes=VMEM_CAP_BYTES),
    )(h, w3, w3)


def _ffn_da_swiglu(name, dxo, w3, zg, zu, tm):
    T, D = dxo.shape
    dff = w3.shape[1]
    tm = min(tm, T)
    assert T % tm == 0 and dff % 2 == 0
    hc = dff // 2

    def kern(d_ref, w_ref, g_ref, u_ref, dz_ref):
        db = (d_ref[...] * 0.5).astype(BF)
        for s in range(2):
            cols = slice(s * hc, (s + 1) * hc)
            da = lax.dot_general(db, w_ref[cols, :], (((1,), (1,)), ((), ())), preferred_element_type=F32)
            g = g_ref[:, cols].astype(F32)
            sg = 1.0 / (1.0 + jnp.exp(-g))
            gs = g * sg
            dab = da.astype(BF)
            dz_ref[:, cols] = (dab * u_ref[:, cols]) * (sg + gs * (1.0 - sg)).astype(BF)
            dz_ref[:, dff + s * hc:dff + (s + 1) * hc] = dab * gs.astype(BF)

    row = lambda w: pl.BlockSpec((tm, w), lambda i: (i, 0))
    return pl.pallas_call(
        kern, grid=(T // tm,),
        in_specs=[row(D), pl.BlockSpec((None, dff, D), lambda i: (0, 0, 0), pipeline_mode=pl.Buffered(1)), row(dff), row(dff)],
        out_specs=row(2 * dff), out_shape=jax.ShapeDtypeStruct((T, 2 * dff), BF), name=name,
        compiler_params=pltpu.CompilerParams(dimension_semantics=("arbitrary",), vmem_limit_bytes=VMEM_CAP_BYTES),
    )(dxo, w3, zg, zu)


def _mm_dh_rms(name, dz, w3, xin, g, dres, tm):
    T, K = dz.shape
    D = w3.shape[1]
    tm = min(tm, T)
    assert T % tm == 0

    def kern(dz_ref, w_ref, x_ref, g_ref, r_ref, dx_ref, dg_ref):
        dh = lax.dot_general(dz_ref[...], w_ref[...], (((1,), (1,)), ((), ())), preferred_element_type=F32)
        _, vjp = jax.vjp(_rmsnorm, x_ref[...], g_ref[...])
        dx, dg = vjp(dh)
        dx_ref[...] = dx + r_ref[...]

        @pl.when(pl.program_id(0) == 0)
        def _():
            dg_ref[...] = jnp.zeros(dg_ref.shape, F32)

        dg_ref[...] += dg

    row = lambda w: pl.BlockSpec((tm, w), lambda i: (i, 0))
    one = pl.BlockSpec((1, D), lambda i: (0, 0))
    return pl.pallas_call(
        kern, grid=(T // tm,),
        in_specs=[row(K), pl.BlockSpec((None, D, K), lambda i: (0, 0, 0), pipeline_mode=pl.Buffered(1)), row(D), one, row(D)],
        out_specs=[row(D), one], out_shape=[jax.ShapeDtypeStruct((T, D), F32), jax.ShapeDtypeStruct((1, D), F32)], name=name,
        compiler_params=pltpu.CompilerParams(dimension_semantics=("arbitrary",), vmem_limit_bytes=VMEM_CAP_BYTES),
    )(dz, w3, xin, g, dres)


def _mm_tn_pair(name, a, b, kind, c_arr, tq, tk, scale=1.0):
    T, M = a.shape
    _, N = b.shape
    tk = min(tk, T)
    assert T % tk == 0
    nk = T // tk
    if kind == "col":
        hm = M // 2
        assert N % tq == 0
        nq = N // tq
        tile = (hm, tq)
        a_spec = pl.BlockSpec((tk, hm), lambda h, q, k, c: (k, jnp.bitwise_xor(h, 1 - c[0])))
        b_spec = pl.BlockSpec((tk, tq), lambda h, q, k, c: (k, q))
        o_spec = pl.BlockSpec(tile, lambda h, q, k, c: (0, q * h))
        out_sd = (hm, N)
    else:
        hn = N // 2
        assert M % tq == 0
        nq = M // tq
        tile = (tq, hn)
        a_spec = pl.BlockSpec((tk, tq), lambda h, q, k, c: (k, q))
        b_spec = pl.BlockSpec((tk, hn), lambda h, q, k, c: (k, jnp.bitwise_xor(h, 1 - c[0])))
        o_spec = pl.BlockSpec(tile, lambda h, q, k, c: (q * h, 0))
        out_sd = (M, hn)

    def kern(c_ref, a_ref, b_ref, o_ref, acc, stage, recv, ssem, rsem):
        h, q, k = pl.program_id(0), pl.program_id(1), pl.program_id(2)
        x, y, c, _ = _place()
        p = lax.dot_general(a_ref[...].astype(BF), b_ref[...].astype(BF), (((0,), (0,)), ((), ())), preferred_element_type=F32)

        @pl.when(k == 0)
        def _():
            acc[...] = p

        @pl.when(k > 0)
        def _():
            acc[...] += p

        def send(slot, qq):
            return pltpu.make_async_remote_copy(src_ref=stage.at[slot], dst_ref=recv.at[qq], send_sem=ssem.at[slot],
                                                recv_sem=rsem.at[qq], device_id=(x, y, 1 - c), device_id_type=MESH)

        last = k == nk - 1

        @pl.when(jnp.logical_and(last, h == 0))
        def _():
            slot = q % 2

            @pl.when(q >= 2)
            def _():
                send(slot, q).wait_send()

            stage[slot] = (acc[...] * scale).astype(BF)
            send(slot, q).start()

        @pl.when(jnp.logical_and(last, h == 1))
        def _():
            @pl.when(q == 0)
            def _():
                for s in range(min(nq, 2)):
                    send(s, 0).wait_send()

            send(0, q).wait_recv()
            o_ref[...] = (acc[...] * scale + recv[q].astype(F32)).astype(o_ref.dtype)

    return pl.pallas_call(
        kern,
        grid_spec=pltpu.PrefetchScalarGridSpec(
            num_scalar_prefetch=1, grid=(2, nq, nk), in_specs=[a_spec, b_spec], out_specs=o_spec,
            scratch_shapes=[pltpu.VMEM(tile, F32), pltpu.VMEM((2,) + tile, BF), pltpu.VMEM((nq,) + tile, BF),
                            pltpu.SemaphoreType.DMA((2,)), pltpu.SemaphoreType.DMA((nq,))]),
        out_shape=jax.ShapeDtypeStruct(out_sd, BF), name=name,
        compiler_params=pltpu.CompilerParams(dimension_semantics=("arbitrary", "arbitrary", "arbitrary"),
                                             vmem_limit_bytes=VMEM_CAP_BYTES),
    )(c_arr, a, b)


def _kv_pieces(kv_ref):
    W = XA_HEADS * XA_DIM
    mk = [kv_ref[:, a * XA_DIM:(a + 1) * XA_DIM] for a in range(XA_HEADS)]
    mv = [kv_ref[:, W + a * XA_DIM:W + (a + 1) * XA_DIM] for a in range(XA_HEADS)]
    return mk, mv


HG_SUB = 4


def _hgrn_rows(z_ref):
    W = HG_HEADS * HG_DIM

    def piece(c, col, w):
        return z_ref[c * HG_CHUNK:(c + 1) * HG_CHUNK, col:col + w]

    zq = [[piece(c, h * HG_DIM, HG_DIM) for h in range(HG_HEADS)] for c in range(HG_SUB)]
    zf = z_ref[:, W:2 * W]
    zi =[[piece(c, 2 * W + h * HG_DIM, HG_DIM) for h in range(HG_HEADS)] for c in range(HG_SUB)]
    zg = [[piece(c, 3 * W + h * HG_DIM, HG_DIM) for h in range(HG_HEADS)] for c in range(HG_SUB)]
    zx = [z_ref[:, 4 * W + a * XA_DIM:4 * W + (a + 1) * XA_DIM] for a in range(XA_HEADS)]
    return zq, zf, zi, zg, zx


def _hgrn_steps(zq, zf, zi, zg, zx, lb3, gn, mk, mv, S):
    f, b = _hgrn_decays(zf, lb3)
    mix = []
    for c in range(HG_SUB):
        row, s_next = [], []
        rows = slice(c * HG_CHUNK, (c + 1) * HG_CHUNK)
        for h in range(HG_HEADS):
            cols = slice(h * HG_DIM, (h + 1) * HG_DIM)
            o, sn = _hgrn_head(zq[c][h], f[rows, cols], b[rows, cols], zi[c][h], zg[c][h], gn, S[h])
            row.append(o)
            s_next.append(sn)
        mix.append(row)
        S = s_next
    att = [_attention(zx[a], mk[a], mv[a]) for a in range(XA_HEADS)]
    return mix, att, S


def _hgrn_fwd2(z, lb_logits, gnorm, kv, bl, seq):
    T, zw = z.shape
    mem_len = kv.shape[0] // bl
    cat_w = HG_HEADS * HG_DIM + XA_HEADS * XA_DIM
    R = HG_SUB * HG_CHUNK
    nb = seq // R

    def kern(z_ref, lb_ref, gn_ref, kv_ref, cat_ref, st_ref, s_scr):
        @pl.when(pl.program_id(1) == 0)
        def _():
            s_scr[...] = jnp.zeros(s_scr.shape, F32)

        st_ref[...] = s_scr[...]
        zq, zf, zi, zg, zx = _hgrn_rows(z_ref)
        mk, mv = _kv_pieces(kv_ref)
        S = [s_scr[h] for h in range(HG_HEADS)]
        mix, att, s_new = _hgrn_steps(zq, zf, zi, zg, zx, lb_ref[...], gn_ref[...], mk, mv, S)
        for c in range(HG_SUB):
            for h in range(HG_HEADS):
                cat_ref[c * HG_CHUNK:(c + 1) * HG_CHUNK, h * HG_DIM:(h + 1) * HG_DIM] = mix[c][h].astype(cat_ref.dtype)
        for h in range(HG_HEADS):
            s_scr[h] = s_new[h]
        base = HG_HEADS * HG_DIM
        for a in range(XA_HEADS):
            cat_ref[:, base + a * XA_DIM:base + (a + 1) * XA_DIM] = att[a].astype(cat_ref.dtype)

    return pl.pallas_call(
        kern, grid=(bl, nb),
        in_specs=[pl.BlockSpec((R, zw), lambda b, n: (b * nb + n, 0)),
                  pl.BlockSpec(lb_logits.shape, lambda b, n: (0, 0)),
                  pl.BlockSpec(gnorm.shape, lambda b, n: (0, 0)),
                  pl.BlockSpec((mem_len, kv.shape[1]), lambda b, n: (b, 0))],
        out_specs=[pl.BlockSpec((R, cat_w), lambda b, n: (b * nb + n, 0)),
                   pl.BlockSpec((None, HG_HEADS, HG_DIM, HG_DIM), lambda b, n: (b * nb + n, 0, 0, 0))],
        out_shape=[jax.ShapeDtypeStruct((T, cat_w), BF),
                   jax.ShapeDtypeStruct((bl * nb, HG_HEADS, HG_DIM, HG_DIM), F32)],
        scratch_shapes=[pltpu.VMEM((HG_HEADS, HG_DIM, HG_DIM), F32)],
        name="hgrn_fwd",
        compiler_params=pltpu.CompilerParams(dimension_semantics=("arbitrary", "arbitrary"), vmem_limit_bytes=VMEM_CAP_BYTES),
    )(z, lb_logits, gnorm, kv)


def _hgrn_bwd2(z, dcat, stash, lb_logits, gnorm, kv, bl, seq):
    T, zw = z.shape
    mem_len = kv.shape[0] // bl
    cat_w = dcat.shape[1]
    R = HG_SUB * HG_CHUNK
    nb = seq // R

    def kern(z_ref, dc_ref, st_ref, lb_ref, gn_ref, kv_ref, dz_ref, dkv_ref, dlb_ref, dgn_ref, ds_scr):
        first = jnp.logical_and(pl.program_id(0) == 0, pl.program_id(1) == 0)

        @pl.when(pl.program_id(1) == 0)
        def _():
            ds_scr[...] = jnp.zeros(ds_scr.shape, F32)
            dkv_ref[...] = jnp.zeros(dkv_ref.shape, F32)

        @pl.when(first)
        def _():
            dlb_ref[...] = jnp.zeros(dlb_ref.shape, F32)
            dgn_ref[...] = jnp.zeros(dgn_ref.shape, F32)

        zq, zf, zi, zg, zx = _hgrn_rows(z_ref)
        mk, mv = _kv_pieces(kv_ref)
        S = [st_ref[h] for h in range(HG_HEADS)]
        _, vjp = jax.vjp(_hgrn_steps, zq, zf, zi, zg, zx, lb_ref[...], gn_ref[...], mk, mv, S)
        d_mix = [[dc_ref[c * HG_CHUNK:(c + 1) * HG_CHUNK, h * HG_DIM:(h + 1) * HG_DIM] for h in range(HG_HEADS)]
                 for c in range(HG_SUB)]
        base = HG_HEADS * HG_DIM
        d_att = [dc_ref[:, base + a * XA_DIM:base + (a + 1) * XA_DIM] for a in range(XA_HEADS)]
        d_s = [ds_scr[h] for h in range(HG_HEADS)]
        dzq, dzf, dzi, dzg, dzx, dlb3, dgn, dmk, dmv, dS = vjp((d_mix, d_att, d_s))
        W = HG_HEADS * HG_DIM
        dz_ref[:, W:2 * W] = dzf.astype(dz_ref.dtype)
        for c in range(HG_SUB):
            rows = slice(c * HG_CHUNK, (c + 1) * HG_CHUNK)
            for h in range(HG_HEADS):
                for k, part in ((0, dzq), (2, dzi), (3, dzg)):
                    dz_ref[rows, k * W + h * HG_DIM:k * W + (h + 1) * HG_DIM] = part[c][h].astype(dz_ref.dtype)
        for h in range(HG_HEADS):
            ds_scr[h] = dS[h]
        dlb_ref[...] += dlb3
        dgn_ref[...] += dgn
        KW = XA_HEADS * XA_DIM
        for a in range(XA_HEADS):
            dz_ref[:, 4 * W + a * XA_DIM:4 * W + (a + 1) * XA_DIM] = dzx[a].astype(dz_ref.dtype)
            dkv_ref[:, a * XA_DIM:(a + 1) * XA_DIM] += dmk[a]
            dkv_ref[:, KW + a * XA_DIM:KW + (a + 1) * XA_DIM] += dmv[a]

    rev = lambda b, n: (b * nb + (nb - 1 - n), 0)
    return pl.pallas_call(
        kern, grid=(bl, nb),
        in_specs=[pl.BlockSpec((R, zw), rev),
                  pl.BlockSpec((R, cat_w), rev),
                  pl.BlockSpec((None, HG_HEADS, HG_DIM, HG_DIM), lambda b, n: (b * nb + (nb - 1 - n), 0, 0, 0)),
                  pl.BlockSpec(lb_logits.shape, lambda b, n: (0, 0)),
                  pl.BlockSpec(gnorm.shape, lambda b, n: (0, 0)),
                  pl.BlockSpec((mem_len, kv.shape[1]), lambda b, n: (b, 0))],
        out_specs=[pl.BlockSpec((R, zw), rev),
                   pl.BlockSpec((mem_len, kv.shape[1]), lambda b, n: (b, 0)),
                   pl.BlockSpec(lb_logits.shape, lambda b, n: (0, 0)),
                   pl.BlockSpec(gnorm.shape, lambda b, n: (0, 0))],
        out_shape=[jax.ShapeDtypeStruct((T, zw), BF), jax.ShapeDtypeStruct(kv.shape, F32),
                   jax.ShapeDtypeStruct(lb_logits.shape, F32), jax.ShapeDtypeStruct(gnorm.shape, F32)],
        scratch_shapes=[pltpu.VMEM((HG_HEADS, HG_DIM, HG_DIM), F32)],
        name="hgrn_bwd",
        compiler_params=pltpu.CompilerParams(dimension_semantics=("arbitrary", "arbitrary"), vmem_limit_bytes=VMEM_CAP_BYTES),
    )(z, dcat, stash, lb_logits, gnorm, kv)


GM_SUB = 2


def _gmlp_pieces(z_ref):
    W = GM_GROUPS * GM_GROUP_DIM
    zu = [z_ref[:, g * GM_GROUP_DIM:(g + 1) * GM_GROUP_DIM] for g in range(GM_GROUPS)]
    zv = [z_ref[:, W + g * GM_GROUP_DIM:W + (g + 1) * GM_GROUP_DIM] for g in range(GM_GROUPS)]
    zx = [z_ref[:, 2 * W + a * XA_DIM:2 * W + (a + 1) * XA_DIM] for a in range(XA_HEADS)]
    return zu, zv, zx


def _gmlp_params(lng_ref, lnb_ref, ws_ref, bs_ref):
    lng = [lng_ref[:, g * GM_GROUP_DIM:(g + 1) * GM_GROUP_DIM] for g in range(GM_GROUPS)]
    lnb = [lnb_ref[:, g * GM_GROUP_DIM:(g + 1) * GM_GROUP_DIM] for g in range(GM_GROUPS)]
    ws = [ws_ref[g] for g in range(GM_GROUPS)]
    bs = [bs_ref[g:g + 1, :] for g in range(GM_GROUPS)]
    return lng, lnb, ws, bs


def _gmlp_fwd(z, ln_g, ln_b, w_s, b_s, kv, bl, nc):
    T, zw = z.shape
    mem_len = kv.shape[0] // bl
    cat_w = GM_GROUPS * GM_GROUP_DIM + XA_HEADS * XA_DIM

    assert nc % GM_SUB == 0
    nc = nc // GM_SUB
    R = GM_SUB * GM_CHUNK

    def kern(z_ref, lng_ref, lnb_ref, ws_ref, bs_ref, kv_ref, cat_ref):
        lng, lnb, ws, bs = _gmlp_params(lng_ref, lnb_ref, ws_ref, bs_ref)
        mk, mv = _kv_pieces(kv_ref)
        for c in range(GM_SUB):
            rows = pl.ds(c * GM_CHUNK, GM_CHUNK)
            zu, zv, zx = _gmlp_pieces(z_ref.at[rows])
            out = cat_ref.at[rows]
            outs = _gmlp_block(zu, zv, zx, lng, lnb, ws, bs, mk, mv)
            for g in range(GM_GROUPS):
                out[:, g * GM_GROUP_DIM:(g + 1) * GM_GROUP_DIM] = outs[g].astype(cat_ref.dtype)
            base = GM_GROUPS * GM_GROUP_DIM
            for a in range(XA_HEADS):
                out[:, base + a * XA_DIM:base + (a + 1) * XA_DIM] = outs[GM_GROUPS + a].astype(cat_ref.dtype)

    full2 = lambda b, n: (0, 0)
    return pl.pallas_call(
        kern, grid=(bl, nc),
        in_specs=[pl.BlockSpec((R, zw), lambda b, n: (b * nc + n, 0)),
                  pl.BlockSpec(ln_g.shape, full2), pl.BlockSpec(ln_b.shape, full2),
                  pl.BlockSpec(w_s.shape, lambda b, n: (0, 0, 0)), pl.BlockSpec(b_s.shape, full2),
                  pl.BlockSpec((mem_len, kv.shape[1]), lambda b, n: (b, 0))],
        out_specs=pl.BlockSpec((R, cat_w), lambda b, n: (b * nc + n, 0)),
        out_shape=jax.ShapeDtypeStruct((T, cat_w), BF),
        name="gmlp_fwd",
        compiler_params=pltpu.CompilerParams(dimension_semantics=("arbitrary", "arbitrary"), vmem_limit_bytes=VMEM_CAP_BYTES),
    )(z, ln_g, ln_b, w_s, b_s, kv)


def _gmlp_bwd(z, dcat, ln_g, ln_b, w_s, b_s, kv, bl, nc):
    T, zw = z.shape
    mem_len = kv.shape[0] // bl
    cat_w = dcat.shape[1]
    assert nc % GM_SUB == 0
    nc = nc // GM_SUB

    def kern(z_ref, dc_ref, lng_ref, lnb_ref, ws_ref, bs_ref, kv_ref,
             dz_ref, dkv_ref, dlng_ref, dlnb_ref, dws_ref, dbs_ref):
        first = jnp.logical_and(pl.program_id(0) == 0, pl.program_id(1) == 0)

        @pl.when(pl.program_id(1) == 0)
        def _():
            dkv_ref[...] = jnp.zeros(dkv_ref.shape, F32)

        @pl.when(first)
        def _():
            dlng_ref[...] = jnp.zeros(dlng_ref.shape, F32)
            dlnb_ref[...] = jnp.zeros(dlnb_ref.shape, F32)
            dws_ref[...] = jnp.zeros(dws_ref.shape, F32)
            dbs_ref[...] = jnp.zeros(dbs_ref.shape, F32)

        lng, lnb, ws, bs = _gmlp_params(lng_ref, lnb_ref, ws_ref, bs_ref)
        mk, mv = _kv_pieces(kv_ref)
        W = GM_GROUPS * GM_GROUP_DIM
        KW = XA_HEADS * XA_DIM
        for c in range(GM_SUB):
            rows = pl.ds(c * GM_CHUNK, GM_CHUNK)
            zu, zv, zx = _gmlp_pieces(z_ref.at[rows])
            dc, dz = dc_ref.at[rows], dz_ref.at[rows]
            _, vjp = jax.vjp(_gmlp_block, zu, zv, zx, lng, lnb, ws, bs, mk, mv)
            d_outs = [dc[:, g * GM_GROUP_DIM:(g + 1) * GM_GROUP_DIM] for g in range(GM_GROUPS)]
            d_outs += [dc[:, W + a * XA_DIM:W + (a + 1) * XA_DIM] for a in range(XA_HEADS)]
            dzu, dzv, dzx, dlng, dlnb, dws, dbs, dmk, dmv = vjp(d_outs)
            for g in range(GM_GROUPS):
                sl = slice(g * GM_GROUP_DIM, (g + 1) * GM_GROUP_DIM)
                dz[:, sl] = dzu[g].astype(dz_ref.dtype)
                dz[:, W + g * GM_GROUP_DIM:W + (g + 1) * GM_GROUP_DIM] = dzv[g].astype(dz_ref.dtype)
                dlng_ref[:, sl] += dlng[g]
                dlnb_ref[:, sl] += dlnb[g]
                dws_ref[g] += dws[g]
                dbs_ref[g:g + 1, :] += dbs[g]
            for a in range(XA_HEADS):
                dz[:, 2 * W + a * XA_DIM:2 * W + (a + 1) * XA_DIM] = dzx[a].astype(dz_ref.dtype)
                dkv_ref[:, a * XA_DIM:(a + 1) * XA_DIM] += dmk[a]
                dkv_ref[:, KW + a * XA_DIM:KW + (a + 1) * XA_DIM] += dmv[a]

    full2 = lambda b, n: (0, 0)
    full3 = lambda b, n: (0, 0, 0)
    blk = lambda b, n: (b * nc + n, 0)
    return pl.pallas_call(
        kern, grid=(bl, nc),
        in_specs=[pl.BlockSpec((GM_SUB * GM_CHUNK, zw), blk), pl.BlockSpec((GM_SUB * GM_CHUNK, cat_w), blk),
                  pl.BlockSpec(ln_g.shape, full2), pl.BlockSpec(ln_b.shape, full2),
                  pl.BlockSpec(w_s.shape, full3), pl.BlockSpec(b_s.shape, full2),
                  pl.BlockSpec((mem_len, kv.shape[1]), lambda b, n: (b, 0))],
        out_specs=[pl.BlockSpec((GM_SUB * GM_CHUNK, zw), blk),
                   pl.BlockSpec((mem_len, kv.shape[1]), lambda b, n: (b, 0)),
                   pl.BlockSpec(ln_g.shape, full2), pl.BlockSpec(ln_b.shape, full2),
                   pl.BlockSpec(w_s.shape, full3), pl.BlockSpec(b_s.shape, full2)],
        out_shape=[jax.ShapeDtypeStruct((T, zw), BF), jax.ShapeDtypeStruct(kv.shape, F32),
                   jax.ShapeDtypeStruct(ln_g.shape, F32), jax.ShapeDtypeStruct(ln_b.shape, F32),
                   jax.ShapeDtypeStruct(w_s.shape, F32), jax.ShapeDtypeStruct(b_s.shape, F32)],
        name="gmlp_bwd",
        compiler_params=pltpu.CompilerParams(dimension_semantics=("arbitrary", "arbitrary"), vmem_limit_bytes=VMEM_CAP_BYTES),
    )(z, dcat, ln_g, ln_b, w_s, b_s, kv)


def _place():
    x, y, c = lax.axis_index("x"), lax.axis_index("y"), lax.axis_index("c")
    chips = [(1 - x, y), (x, 1 - y), (1 - x, 1 - y)]
    return x, y, c, chips


def _half(ref, kind, e):
    if kind == "col":
        n = ref.shape[1] // 2
        return ref.at[:, pl.ds(pl.multiple_of(e * n, n), n), :]
    n = ref.shape[2] // 2
    return ref.at[:, :, pl.ds(pl.multiple_of(e * n, n), n)]


def _slot(ref, kind, j, n):
    if kind == "col":
        return ref.at[:, :, pl.ds(pl.multiple_of(j * n, n), n)]
    return ref.at[:, pl.ds(pl.multiple_of(j * n, n), n), :]


def _allgather_seq(name, items, cid):
    nt = len(items)
    kinds = [k for (_, k, _) in items]
    slot_kind = ["row" if k == "row" else "col" for k in kinds]
    out_type = []
    for s, k, l in items:
        L, r, c = s.shape
        lo = L if l is None else 1
        out_type.append(jax.ShapeDtypeStruct((lo, 4 * r, c) if k == "row" else (lo, r, 4 * c), s.dtype))

    def part(ref, t, e):
        return ref if kinds[t] == "vec" else _half(ref, kinds[t], e)

    def body(*refs):
        sh = [refs[t] if items[t][2] is None else refs[t].at[pl.ds(items[t][2], 1)] for t in range(nt)]
        full = refs[nt:2 * nt]
        s_ici, r_ici, s_d2d, r_d2d = refs[2 * nt:]
        x, y, c, chips = _place()
        own = 2 * x + y
        sibling = (x, y, 1 - c)
        barrier = pltpu.get_barrier_semaphore()
        for peer in [(px, py, c) for (px, py) in chips] + [sibling]:
            pl.semaphore_signal(barrier, inc=1, device_id=peer, device_id_type=MESH)
        pl.semaphore_wait(barrier, 4)
        width = [sh[t].shape[1] if kinds[t] == "row" else sh[t].shape[2] for t in range(nt)]
        sent = []
        for t in range(nt):
            for p, (px, py) in enumerate(chips):
                cp = pltpu.make_async_remote_copy(
                    src_ref=part(sh[t], t, c), dst_ref=part(_slot(full[t], slot_kind[t], own, width[t]), t, c),
                    send_sem=s_ici.at[t, p], recv_sem=r_ici.at[t, p], device_id=(px, py, c), device_id_type=MESH)
                cp.start()
                sent.append(cp)
        for t in range(nt):
            for p, (px, py) in enumerate(chips):
                landed = part(_slot(full[t], slot_kind[t], 2 * px + py, width[t]), t, c)
                pltpu.make_async_remote_copy(
                    src_ref=landed, dst_ref=landed, send_sem=s_ici.at[t, p], recv_sem=r_ici.at[t, p],
                    device_id=(px, py, c), device_id_type=MESH).wait_recv()
                if kinds[t] == "vec":
                    continue
                fw = pltpu.make_async_remote_copy(
                    src_ref=landed, dst_ref=landed, send_sem=s_d2d.at[t, p], recv_sem=r_d2d.at[t, p],
                    device_id=sibling, device_id_type=MESH)
                fw.start()
                sent.append(fw)
        for t in range(nt):
            if kinds[t] == "vec":
                continue
            for p, (px, py) in enumerate(chips):
                other = _half(_slot(full[t], kinds[t], 2 * px + py, width[t]), kinds[t], 1 - c)
                pltpu.make_async_remote_copy(
                    src_ref=other, dst_ref=other, send_sem=s_d2d.at[t, p], recv_sem=r_d2d.at[t, p],
                    device_id=sibling, device_id_type=MESH).wait_recv()
        for cp in sent:
            cp.wait_send()

    sems = pltpu.SemaphoreType.DMA
    return pl.kernel(
        body, out_type=out_type, mesh=plsc.ScalarSubcoreMesh(axis_name="seq", num_cores=1),
        scratch_types=[sems((nt, 3)), sems((nt, 3)), sems((nt, 3)), sems((nt, 3))],
        compiler_params=pltpu.CompilerParams(collective_id=cid), name=name,
    )(*[s for (s, _, _) in items])


def _place_own(name, full, shard, kind, layer, chip_arr, after):
    lo, r, c = (shard.shape[0] if layer is None else 1,) + shard.shape[1:]
    first = 0 if layer is None else layer
    tr = _pick(r, 512)
    nr = r // tr

    def body(chip_ref, s_ref, f_ref, after_ref, o_ref):
        o_ref[...] = s_ref[...]

    if kind == "row":
        out_map = lambda i, j, chip: (i, chip[0] * nr + j, 0)
    else:
        out_map = lambda i, j, chip: (i, j, chip[0])
    return pl.pallas_call(
        body, out_shape=jax.ShapeDtypeStruct(full.shape, full.dtype),
        grid_spec=pltpu.PrefetchScalarGridSpec(
            num_scalar_prefetch=1, grid=(lo, nr),
            in_specs=[pl.BlockSpec((1, tr, c), lambda i, j, chip: (i + first, j, 0)), pl.BlockSpec(memory_space=pl.ANY),
                      pl.BlockSpec(memory_space=pl.ANY)],
            out_specs=pl.BlockSpec((1, tr, c), out_map)),
        input_output_aliases={2: 0},
        compiler_params=pltpu.CompilerParams(dimension_semantics=("parallel", "parallel"), vmem_limit_bytes=VMEM_CAP_BYTES),
        name=name,
    )(chip_arr, shard, full, after)


def _slot2(ref, kind, j, n):
    if kind == "col":
        return ref.at[:, pl.ds(pl.multiple_of(j * n, n), n)]
    return ref.at[pl.ds(pl.multiple_of(j * n, n), n), :]


def _rs_chips_seq(name, parts, kinds, cid):
    nm = len(parts)
    out_type = []
    for g, k in zip(parts, kinds):
        r, c = g.shape
        ps = (r, c // 4) if k == "col" else (r // 4, c)
        out_type += [jax.ShapeDtypeStruct(ps, BF), jax.ShapeDtypeStruct((3,) + ps, BF)]

    def body(*refs):
        g = refs[:nm]
        outs = refs[nm:3 * nm]
        loc, ssem, rsem = refs[3 * nm:]
        x, y, c, chips = _place()
        own = 2 * x + y
        barrier = pltpu.get_barrier_semaphore()
        for (px, py) in chips:
            pl.semaphore_signal(barrier, inc=1, device_id=(px, py, c), device_id_type=MESH)
        pl.semaphore_wait(barrier, 3)
        cps = []
        for m in range(nm):
            k = kinds[m]
            own_o, got_o = outs[2 * m], outs[2 * m + 1]
            n = g[m].shape[1] // 4 if k == "col" else g[m].shape[0] // 4
            lc = pltpu.make_async_copy(_slot2(g[m], k, own, n), own_o, loc.at[m])
            lc.start()
            cps.append(lc)
            for p, (px, py) in enumerate(chips):
                cp = pltpu.make_async_remote_copy(
                    src_ref=_slot2(g[m], k, 2 * px + py, n), dst_ref=got_o.at[p],
                    send_sem=ssem.at[m, p], recv_sem=rsem.at[m, p], device_id=(px, py, c), device_id_type=MESH)
                cp.start()
                cps.append(cp)
        for cp in cps:
            cp.wait()

    return pl.kernel(
        body, out_type=out_type, mesh=plsc.ScalarSubcoreMesh(axis_name="seq", num_cores=1),
        scratch_types=[pltpu.SemaphoreType.DMA((nm,)), pltpu.SemaphoreType.DMA((nm, 3)), pltpu.SemaphoreType.DMA((nm, 3))],
        compiler_params=pltpu.CompilerParams(collective_id=cid), name=name,
    )(*parts)


def _finish_share(name, owns, gots, kind, c_arr):
    L = len(owns)
    r, c = owns[0].shape
    tr = _pick(r, 128 if kind == "col" else 256)
    nb = r // tr
    nq = L * nb

    def chunk_of(l):
        return lambda h, q: jnp.clip(q * (1 - h) + (nq - 1) * h - l * nb, 0, nb - 1)

    ins, in_specs = [], []
    for l in range(L):
        at = chunk_of(l)
        ins += [owns[l], gots[l].reshape(3 * r, c), gots[l].reshape(3 * r, c), gots[l].reshape(3 * r, c)]
        in_specs.append(pl.BlockSpec((tr, c), functools.partial(lambda h, q, cc, at: (at(h, q), 0), at=at)))
        in_specs += [pl.BlockSpec((tr, c), functools.partial(lambda h, q, cc, at, p: (p * nb + at(h, q), 0), at=at, p=p))
                     for p in range(3)]
    if kind == "col":
        out_sd = (L, 2, r, c)
        o_spec = pl.BlockSpec((None, 2, tr, c), lambda h, q, cc: ((q * h) // nb, 0, (q * h) % nb, 0))
    else:
        out_sd = (L * r, 2 * c)
        o_spec = pl.BlockSpec((tr, 2 * c), lambda h, q, cc: (q * h, 0))

    def kern(c_ref, *refs):
        in_refs = refs[:4 * L]
        out_ref, mine, recv, ssem, rsem = refs[4 * L:]
        h, q = pl.program_id(0), pl.program_id(1)
        x, y, cc, _ = _place()

        def swap(qq):
            return pltpu.make_async_remote_copy(src_ref=mine.at[qq], dst_ref=recv.at[qq], send_sem=ssem.at[qq],
                                                recv_sem=rsem.at[qq], device_id=(x, y, 1 - cc), device_id_type=MESH)

        for l in range(L):
            @pl.when(jnp.logical_and(h == 0, q // nb == l))
            def _(l=l):
                o_ref, g0, g1, g2 = in_refs[4 * l:4 * l + 4]
                mine[q] = ((o_ref[...].astype(F32) + g0[...].astype(F32)) + g1[...].astype(F32)) + g2[...].astype(F32)
                swap(q).start()

        @pl.when(h == 1)
        def _():
            swap(q).wait()
            a, b = mine[q], recv[q]
            first = c_ref[0] == 0
            lo, hi = jnp.where(first, a, b), jnp.where(first, b, a)
            if kind == "col":
                out_ref[0] = lo
                out_ref[1] = hi
            else:
                out_ref[:, :c] = lo
                out_ref[:, c:] = hi

    full = pl.pallas_call(
        kern,
        grid_spec=pltpu.PrefetchScalarGridSpec(
            num_scalar_prefetch=1, grid=(2, nq), in_specs=in_specs, out_specs=o_spec,
            scratch_shapes=[pltpu.VMEM((nq, tr, c), F32), pltpu.VMEM((nq, tr, c), F32),
                            pltpu.SemaphoreType.DMA((nq,)), pltpu.SemaphoreType.DMA((nq,))]),
        out_shape=jax.ShapeDtypeStruct(out_sd, F32), name=name,
        compiler_params=pltpu.CompilerParams(dimension_semantics=("arbitrary", "arbitrary"),
                                             vmem_limit_bytes=VMEM_CAP_BYTES),
    )(c_arr, *ins)
    return full.reshape(L, 2 * r, c) if kind == "col" else full.reshape(L, r, 2 * c)


def _small_allreduce(buf, name):
    R = buf.shape[0]
    assert R % 16 == 0
    h = R // 2

    def body(x_ref, o_ref, sib, csum, got, s_a, r_a, s_b, r_b, s_c, r_c):
        x, y, c, chips = _place()
        sibling = (x, y, 1 - c)
        own = 2 * x + y
        swap = pltpu.make_async_remote_copy(src_ref=x_ref, dst_ref=sib, send_sem=s_a, recv_sem=r_a,
                                            device_id=sibling, device_id_type=MESH)
        swap.start()
        swap.wait()
        a, b = x_ref[...], sib[...]
        south = c == 0
        csum[...] = jnp.where(south, a, b) + jnp.where(south, b, a)
        lo = pl.multiple_of(c * h, 8)
        mine = csum.at[pl.ds(lo, h)]
        got[own] = csum[pl.ds(lo, h)]
        sends = []
        for p, (px, py) in enumerate(chips):
            cp = pltpu.make_async_remote_copy(src_ref=mine, dst_ref=got.at[own], send_sem=s_b.at[p], recv_sem=r_b.at[p],
                                              device_id=(px, py, c), device_id_type=MESH)
            cp.start()
            sends.append(cp)
        for cp in sends:
            cp.wait()
        o_ref[pl.ds(lo, h)] = ((got[0] + got[1]) + got[2]) + got[3]
        done = o_ref.at[pl.ds(lo, h)]
        back = pltpu.make_async_remote_copy(src_ref=done, dst_ref=done, send_sem=s_c, recv_sem=r_c,
                                            device_id=sibling, device_id_type=MESH)
        back.start()
        back.wait_send()
        other = o_ref.at[pl.ds(pl.multiple_of((1 - c) * h, 8), h)]
        pltpu.make_async_remote_copy(src_ref=other, dst_ref=other, send_sem=s_c, recv_sem=r_c,
                                     device_id=sibling, device_id_type=MESH).wait_recv()

    vm = pl.BlockSpec(memory_space=pltpu.VMEM)
    return pl.pallas_call(
        body, out_shape=jax.ShapeDtypeStruct(buf.shape, F32), in_specs=[vm], out_specs=vm,
        scratch_shapes=[pltpu.VMEM((R, LANES), F32), pltpu.VMEM((R, LANES), F32), pltpu.VMEM((4, h, LANES), F32),
                        pltpu.SemaphoreType.DMA, pltpu.SemaphoreType.DMA, pltpu.SemaphoreType.DMA((3,)),
                        pltpu.SemaphoreType.DMA((3,)), pltpu.SemaphoreType.DMA, pltpu.SemaphoreType.DMA],
        name=name,
        compiler_params=pltpu.CompilerParams(vmem_limit_bytes=VMEM_CAP_BYTES),
    )(buf)


PACK_TILE_ROWS = 8


def _item_rows(shape):
    n = 1
    for d in shape:
        n *= d
    return -(-n // (PACK_TILE_ROWS * LANES)) * PACK_TILE_ROWS


def _pack(arrs, rows_total):
    buf = jnp.zeros((rows_total, LANES), F32)
    r = 0
    for a in arrs:
        f = a.reshape(-1).astype(F32)
        nr = _item_rows(a.shape)
        block = jnp.pad(f, (0, nr * LANES - f.shape[0])).reshape(nr, LANES)
        buf = lax.dynamic_update_slice(buf, block, (r, 0))
        r += nr
    return buf


def _unpack(buf, shapes):
    out, r = [], 0
    for s in shapes:
        n = 1
        for d in s:
            n *= d
        nr = _item_rows(s)
        out.append(buf[r:r + nr].reshape(-1)[:n].reshape(s))
        r += nr
    return out


def _rows_needed(shapes):
    return -(-sum(_item_rows(s) for s in shapes) // (2 * PACK_TILE_ROWS)) * (2 * PACK_TILE_ROWS)


def _two_rows(a, b):
    out = jnp.zeros((2, a.shape[1]), a.dtype)
    return lax.dynamic_update_slice(lax.dynamic_update_slice(out, a, (0, 0)), b, (1, 0))


def _adam(w, g, m, v):
    m = ADAM_B1 * m + (1.0 - ADAM_B1) * g
    v = ADAM_B2 * v + (1.0 - ADAM_B2) * jnp.square(g)
    m_hat = m / (1.0 - ADAM_B1 ** ADAM_STEP)
    v_hat = v / (1.0 - ADAM_B2 ** ADAM_STEP)
    delta = -ADAM_LR * (m_hat / (jnp.sqrt(v_hat) + ADAM_EPS) + ADAM_WD * w)
    return delta, m, v


def _adam_call(name, w2, g2, m2, v2, tr, pass_grad=False):
    def fn(rv, cv):
        outs = list(_adam(*rv))
        return ([rv[1]] + outs if pass_grad else outs), []

    width = w2.shape[1]
    return _rowcall(name, fn, [(w2, 0, width), (g2, 0, width), (m2, 0, width), (v2, 0, width)], [],
                    [(width, F32)] * (4 if pass_grad else 3), [], tr)


def kernel(x, mem, mem_norm, lb_logits, ffn1_norm, ffn1_w_in, ffn1_w_out, mix_norm, mem_w_kv, hgrn_w_in, hgrn_gnorm, hgrn_w_out, gmlp_w_in, gmlp_ln_g, gmlp_ln_b, gmlp_w_s, gmlp_b_s, gmlp_w_out, ffn2_norm, ffn2_w_in, ffn2_w_out, final_norm, loss_target, m_mem_norm, m_lb_logits, m_ffn1_norm, m_ffn1_w_in, m_ffn1_w_out, m_mix_norm, m_mem_w_kv, m_hgrn_w_in, m_hgrn_gnorm, m_hgrn_w_out, m_gmlp_w_in, m_gmlp_ln_g, m_gmlp_ln_b, m_gmlp_w_s, m_gmlp_b_s, m_gmlp_w_out, m_ffn2_norm, m_ffn2_w_in, m_ffn2_w_out, m_final_norm, v_mem_norm, v_lb_logits, v_ffn1_norm, v_ffn1_w_in, v_ffn1_w_out, v_mix_norm, v_mem_w_kv, v_hgrn_w_in, v_hgrn_gnorm, v_hgrn_w_out, v_gmlp_w_in, v_gmlp_ln_g, v_gmlp_ln_b, v_gmlp_w_s, v_gmlp_b_s, v_gmlp_w_out, v_ffn2_norm, v_ffn2_w_in, v_ffn2_w_out, v_final_norm):
    bl, seq, D = x.shape
    T = bl * seq
    mem_len = mem.shape[1]
    chip = 2 * lax.axis_index("x") + lax.axis_index("y")
    c_arr = lax.axis_index("c").astype(jnp.int32).reshape(1)
    chip_arr = chip.astype(jnp.int32).reshape(1)
    TR = 1024

    big = [("ffn1_w_in", ffn1_w_in, "col"), ("ffn1_w_out", ffn1_w_out, "row"), ("mem_w_kv", mem_w_kv, "col"),
           ("hgrn_w_in", hgrn_w_in, "col"), ("hgrn_w_out", hgrn_w_out, "row"), ("gmlp_w_in", gmlp_w_in, "col"),
           ("gmlp_w_out", gmlp_w_out, "row"), ("ffn2_w_in", ffn2_w_in, "col"), ("ffn2_w_out", ffn2_w_out, "row")]
    kinds = [k for (_, _, k) in big]
    shards_bf = []
    for nm, w, _ in big:
        L, r, c = w.shape
        (wb,) = _rowcall("cast_" + nm, lambda rv, cv: ([rv[0]], []), [(w.reshape(L * r, c), 0, c)], [], [(c, BF)], [], 512)
        shards_bf.append(wb.reshape(L, r, c))
    sb = dict(zip([nm for (nm, _, _) in big], shards_bf))
    groups = [[("ffn1_w_in", 0)], [("ffn1_w_out", 0)], [("hgrn_w_in", None)], [("mem_w_kv", None)], [("hgrn_w_out", None)],
              [("ffn2_w_in", 0), ("ffn2_w_out", 0), ("gmlp_ln_g", None), ("gmlp_ln_b", None)],
              [("ffn1_w_in", 1), ("ffn1_w_out", 1)],
              [("gmlp_w_in", None), ("gmlp_w_out", None)],
              [("ffn2_w_in", 1), ("ffn2_w_out", 1)]]
    kind_of = {nm: k for (nm, _, k) in big}
    for nm, vec in (("gmlp_ln_g", gmlp_ln_g), ("gmlp_ln_b", gmlp_ln_b)):
        sb[nm] = vec.reshape(1, 1, -1)
        kind_of[nm] = "vec"
    gathered = {nm: [None, None] for nm in ("ffn1_w_in", "ffn1_w_out", "ffn2_w_in", "ffn2_w_out")}
    others = {}
    for gi, grp in enumerate(groups):
        outs = _allgather_seq("gather_%d" % gi, [(sb[nm], kind_of[nm], l) for (nm, l) in grp], gi)
        for (nm, l), o in zip(grp, outs):
            others[(nm, l)] = o

    def whole(nm, l, after):
        full = _place_own("own_%s_%d" % (nm, l or 0), others[(nm, l)], sb[nm], "row" if kind_of[nm] == "row" else "col", l,
                          chip_arr, after)
        if l is None:
            gathered[nm] = full
        else:
            gathered[nm][l] = full
        return full

    def rms_fwd(name, xin, g):
        (h,) = _rowcall(name, lambda rv, cv: ([_rmsnorm(rv[0], cv[0])], []), [(xin, 0, D)], [g.reshape(1, D)], [(D, BF)], [], TR)
        return h

    def ffn_fwd(tag, xin, h, nm_in, nm_out, layer, next_gain):
        w_in = whole(nm_in, layer, h)
        dff = w_in.shape[2] // 2
        zg, zu, a = _ffn_in_swiglu("ffn_in_" + tag, h, w_in, 1024, dff // 2)
        out = _mm("ffn_out_" + tag, a, whole(nm_out, layer, a), "nn", F32, 1024, 1024, dff, scale=0.5, res=xin, b_lead=0,
                  norm_gain=None if next_gain is None else next_gain.reshape(1, D))
        xo, h_next = (out, None) if next_gain is None else out
        return xo, h_next, (xin, h, zg, zu, a)

    def ffn_bwd(tag, dxo, saved, g, w_in, w_out, layer):
        xin, h, zg, zu, a = saved
        dff = w_out[layer].shape[1]
        dw_out = _mm_tn_pair("ffn_dwo_" + tag, a, dxo, "row", c_arr, dff // 2, T, scale=0.5)
        dz = _ffn_da_swiglu("ffn_da_" + tag, dxo, w_out[layer], zg, zu, 512)
        dw_in = _mm_tn_pair("ffn_dwi_" + tag, h, dz, "col", c_arr, 512, T)
        dx, dg = _mm_dh_rms("ffn_dh_" + tag, dz, w_in[layer], xin, g.reshape(1, D), dxo, 512)
        return dx, dg, dw_in, dw_out

    def rms_bwd(name, xin, g, dh, dres):
        def fn(rv, cv):
            _, vjp = jax.vjp(_rmsnorm, rv[0], cv[0])
            dx, dg = vjp(rv[1])
            if dres is not None:
                dx = dx + rv[2]
            return [dx], [dg]

        rows = [(xin, 0, D), (dh, 0, D)] + ([(dres, 0, D)] if dres is not None else [])
        dx, dg = _rowcall(name, fn, rows, [g.reshape(1, D)], [(D, F32)], [((1, D), F32)], TR)
        return dx, dg

    x0 = x.reshape(T, D)
    tgt = loss_target.reshape(T, D)
    mem2 = mem.reshape(bl * mem_len, D)
    memn = rms_fwd("rms_mem", mem2, mem_norm)

    h_f10 = rms_fwd("rms_f1l0", x0, ffn1_norm[0])
    x1, h_m0, sv_f10 = ffn_fwd("f1l0", x0, h_f10, "ffn1_w_in", "ffn1_w_out", 0, mix_norm[0])
    z_m0 = _mm("mix_in_0", h_m0, whole("hgrn_w_in", None, h_m0), "nn", F32, 2048, 512, D, b_lead=0)
    w_kv = whole("mem_w_kv", None, z_m0)
    kv = [_mm("kv_%d" % i, memn, w_kv, "nn", F32, 512, 512, D, b_lead=i) for i in range(2)]
    cat0, stash0 = _hgrn_fwd2(z_m0, lb_logits, hgrn_gnorm, kv[0], bl, seq)
    x2, h_f20 = _mm("mix_out_0", cat0, whole("hgrn_w_out", None, cat0), "nn", F32, 1024, 1024, cat0.shape[1], res=x1, b_lead=0,
                    norm_gain=ffn2_norm[0].reshape(1, D))
    x3, h_f11, sv_f20 = ffn_fwd("f2l0", x2, h_f20, "ffn2_w_in", "ffn2_w_out", 0, ffn1_norm[1])
    x4, h_m1, sv_f11 = ffn_fwd("f1l1", x3, h_f11, "ffn1_w_in", "ffn1_w_out", 1, mix_norm[1])
    z_m1 = _mm("mix_in_1", h_m1, whole("gmlp_w_in", None, h_m1), "nn", F32, 2048, 512, D, b_lead=0)
    nc1 = seq // GM_CHUNK
    w_s, b_s = gmlp_w_s[0], gmlp_b_s[0]
    ln_w = GM_GROUPS * GM_GROUP_DIM
    ln_g_full, ln_b_full = [whole(nm, None, z_m1).reshape(1, ln_w) for nm in ("gmlp_ln_g", "gmlp_ln_b")]
    cat1 = _gmlp_fwd(z_m1, ln_g_full, ln_b_full, w_s, b_s, kv[1], bl, nc1)
    x5, h_f21 = _mm("mix_out_1", cat1, whole("gmlp_w_out", None, cat1), "nn", F32, 1024, 1024, cat1.shape[1], res=x4, b_lead=0,
                    norm_gain=ffn2_norm[1].reshape(1, D))
    x6, _, sv_f21 = ffn_fwd("f2l1", x5, h_f21, "ffn2_w_in", "ffn2_w_out", 1, None)

    def head(rv, cv):
        def f(xx, gg):
            err = _rmsnorm(xx, gg) - rv[1]
            return 0.5 * jnp.sum(jnp.mean(err * err, axis=-1, keepdims=True), axis=0, keepdims=True)

        ls, vjp = jax.vjp(f, rv[0], cv[0])
        dx, dg = vjp(jnp.ones((1, 1), F32))
        return [dx], [dg, jnp.broadcast_to(ls, (1, 128))]

    dx6, d_final, loss_part = _rowcall("loss_head", head, [(x6, 0, D), (tgt, 0, D)], [final_norm.reshape(1, D)],
                                       [(D, F32)], [((1, D), F32), ((1, 128), F32)], TR)

    rs_out = {}
    n_gather = len(groups)

    def rs(gi, items):
        outs = _rs_chips_seq("reduce_%d" % gi, [p for (_, p, _) in items], [k for (_, _, k) in items], n_gather + gi)
        for i, (key, _, _) in enumerate(items):
            rs_out[key] = (outs[2 * i], outs[2 * i + 1])

    dx5, dg_f21, dwi_f21, dwo_f21 = ffn_bwd("f2l1", dx6, sv_f21, ffn2_norm[1], gathered["ffn2_w_in"], gathered["ffn2_w_out"], 1)
    rs(0, [(("ffn2_w_out", 1), dwo_f21, "row"), (("ffn2_w_in", 1), dwi_f21, "col")])
    dcat1 = _mm("mix_dcat_1", dx5, gathered["gmlp_w_out"], "nt", F32, 2048, 1024, D, b_lead=0)
    dwo_m1 = _mm_tn_pair("mix_dwo_1", cat1, dx5, "row", c_arr, 1024, T)
    dz_m1, dkv1, d_lng, d_lnb, d_ws, d_bs = _gmlp_bwd(z_m1, dcat1, ln_g_full, ln_b_full, w_s, b_s, kv[1], bl, nc1)
    dx4, dg_m1 = _mm_dh_rms("mix_dh_1", dz_m1, gathered["gmlp_w_in"], x4, mix_norm[1].reshape(1, D), dx5, 512)
    dwi_m1 = _mm_tn_pair("mix_dwi_1", h_m1, dz_m1, "col", c_arr, 1024, T)
    rs(1, [(("gmlp_w_out", 0), dwo_m1, "row"), (("gmlp_w_in", 0), dwi_m1, "col")])
    dx3, dg_f11, dwi_f11, dwo_f11 = ffn_bwd("f1l1", dx4, sv_f11, ffn1_norm[1], gathered["ffn1_w_in"], gathered["ffn1_w_out"], 1)
    rs(2, [(("ffn1_w_out", 1), dwo_f11, "row"), (("ffn1_w_in", 1), dwi_f11, "col")])

    dx2, dg_f20, dwi_f20, dwo_f20 = ffn_bwd("f2l0", dx3, sv_f20, ffn2_norm[0], gathered["ffn2_w_in"], gathered["ffn2_w_out"], 0)
    rs(3, [(("ffn2_w_out", 0), dwo_f20, "row"), (("ffn2_w_in", 0), dwi_f20, "col")])
    dcat0 = _mm("mix_dcat_0", dx2, gathered["hgrn_w_out"], "nt", F32, 2048, 1024, D, b_lead=0)
    dwo_m0 = _mm_tn_pair("mix_dwo_0", cat0, dx2, "row", c_arr, 1024, T)
    dz_m0, dkv0, d_lb, d_gn = _hgrn_bwd2(z_m0, dcat0, stash0, lb_logits, hgrn_gnorm, kv[0], bl, seq)
    dx1, dg_m0 = _mm_dh_rms("mix_dh_0", dz_m0, gathered["hgrn_w_in"], x1, mix_norm[0].reshape(1, D), dx2, 512)
    dwi_m0 = _mm_tn_pair("mix_dwi_0", h_m0, dz_m0, "col", c_arr, 1024, T)
    rs(4, [(("hgrn_w_out", 0), dwo_m0, "row"), (("hgrn_w_in", 0), dwi_m0, "col")])

    dwkv = [_mm_tn_pair("kv_dw_%d" % i, memn, dkv, "col", c_arr, 1024, 512) for i, dkv in enumerate([dkv0, dkv1])]
    rs(5, [(("mem_w_kv", 0), dwkv[0], "col"), (("mem_w_kv", 1), dwkv[1], "col")])
    dmemn = _mm("kv_dx_0", dkv0, gathered["mem_w_kv"], "nt", F32, 512, 512, 1024, b_lead=0)
    dmemn = _mm("kv_dx_1", dkv1, gathered["mem_w_kv"], "nt", F32, 512, 512, 1024, res=dmemn, b_lead=1)
    _, d_memnorm = rms_bwd("rms_bwd_mem", mem2, mem_norm, dmemn, None)

    dx0, dg_f10, dwi_f10, dwo_f10 = ffn_bwd("f1l0", dx1, sv_f10, ffn1_norm[0], gathered["ffn1_w_in"], gathered["ffn1_w_out"], 0)
    rs(6, [(("ffn1_w_out", 0), dwo_f10, "row")])
    rs(7, [(("ffn1_w_in", 0), dwi_f10, "col")])

    shard_grads = [_finish_share("finish_" + nm, [rs_out[(nm, l)][0] for l in range(w.shape[0])],
                                 [rs_out[(nm, l)][1] for l in range(w.shape[0])], k, c_arr) for (nm, w, k) in big]

    big_w = [w for (_, w, _) in big]
    big_m = [m_ffn1_w_in, m_ffn1_w_out, m_mem_w_kv, m_hgrn_w_in, m_hgrn_w_out, m_gmlp_w_in, m_gmlp_w_out, m_ffn2_w_in, m_ffn2_w_out]
    big_v = [v_ffn1_w_in, v_ffn1_w_out, v_mem_w_kv, v_hgrn_w_in, v_hgrn_w_out, v_gmlp_w_in, v_gmlp_w_out, v_ffn2_w_in, v_ffn2_w_out]
    big_out = {}
    for (nm, w, _), g, m, v in zip(big, shard_grads, big_m, big_v):
        L, r, c = w.shape
        g2, d2, m2, v2 = _adam_call("adam_" + nm, w.reshape(L * r, c), g.reshape(L * r, c), m.reshape(L * r, c),
                                    v.reshape(L * r, c), 256, pass_grad=True)
        big_out[nm] = (g2.reshape(w.shape), d2.reshape(w.shape), m2.reshape(w.shape), v2.reshape(w.shape))

    d_ffn1n = _two_rows(dg_f10, dg_f11)
    d_mixn = _two_rows(dg_m0, dg_m1)
    d_ffn2n = _two_rows(dg_f20, dg_f21)
    small_parts = [loss_part[:, :1], d_memnorm, d_lb, d_ffn1n, d_mixn, d_gn, d_lng, d_lnb, d_ws, d_bs, d_ffn2n, d_final]
    red_shapes = [(1,), mem_norm.shape, lb_logits.shape, ffn1_norm.shape, mix_norm.shape, hgrn_gnorm.shape, (1, ln_w), (1, ln_w),
                  gmlp_w_s.shape, gmlp_b_s.shape, ffn2_norm.shape, final_norm.shape]
    red = _small_allreduce(_pack(small_parts, _rows_needed(red_shapes)), "reduce_small")
    (loss_v, g_memn, g_lb, g_f1n, g_mixn, g_gn, g_lng_full, g_lnb_full, g_ws, g_bs, g_f2n, g_fin) = _unpack(red, red_shapes)
    lsh = gmlp_ln_g.shape[1]
    g_lng = lax.dynamic_slice(g_lng_full, (0, chip * lsh), (1, lsh))
    g_lnb = lax.dynamic_slice(g_lnb_full, (0, chip * lsh), (1, lsh))
    small_w = [mem_norm, lb_logits, ffn1_norm, mix_norm, hgrn_gnorm, gmlp_ln_g, gmlp_ln_b, gmlp_w_s, gmlp_b_s, ffn2_norm, final_norm]
    small_g = [g_memn, g_lb, g_f1n, g_mixn, g_gn, g_lng, g_lnb, g_ws, g_bs, g_f2n, g_fin]
    small_m = [m_mem_norm, m_lb_logits, m_ffn1_norm, m_mix_norm, m_hgrn_gnorm, m_gmlp_ln_g, m_gmlp_ln_b, m_gmlp_w_s, m_gmlp_b_s, m_ffn2_norm, m_final_norm]
    small_v = [v_mem_norm, v_lb_logits, v_ffn1_norm, v_mix_norm, v_hgrn_gnorm, v_gmlp_ln_g, v_gmlp_ln_b, v_gmlp_w_s, v_gmlp_b_s, v_ffn2_norm, v_final_norm]
    sshapes = [w.shape for w in small_w]
    nrow = _rows_needed(sshapes)
    d_p, m_p, v_p = _adam_call("adam_small", _pack(small_w, nrow), _pack(small_g, nrow), _pack(small_m, nrow), _pack(small_v, nrow), nrow)
    s_delta, s_m, s_v = _unpack(d_p, sshapes), _unpack(m_p, sshapes), _unpack(v_p, sshapes)
    small_names = ["mem_norm", "lb_logits", "ffn1_norm", "mix_norm", "hgrn_gnorm", "gmlp_ln_g", "gmlp_ln_b", "gmlp_w_s", "gmlp_b_s", "ffn2_norm", "final_norm"]
    small_out = {nm: (g.reshape(w.shape), d, m, v) for nm, w, g, d, m, v in zip(small_names, small_w, small_g, s_delta, s_m, s_v)}

    order = ["mem_norm", "lb_logits", "ffn1_norm", "ffn1_w_in", "ffn1_w_out", "mix_norm", "mem_w_kv", "hgrn_w_in", "hgrn_gnorm",
             "hgrn_w_out", "gmlp_w_in", "gmlp_ln_g", "gmlp_ln_b", "gmlp_w_s", "gmlp_b_s", "gmlp_w_out", "ffn2_norm", "ffn2_w_in",
             "ffn2_w_out", "final_norm"]
    allo = {**big_out, **small_out}
    grad_x = dx0.reshape(x.shape)
    return (loss_v.reshape(()), grad_x, *[allo[n][0] for n in order], *[allo[n][1] for n in order],
            *[allo[n][2] for n in order], *[allo[n][3] for n in order])
```

```python
import functools

import jax
import jax.numpy as jnp
from jax import lax
from jax.experimental import pallas as pl
from jax.experimental.pallas import tpu as pltpu
from jax.experimental.pallas import tpu_sc as plsc

BF = jnp.bfloat16
F32 = jnp.float32
MESH = pl.DeviceIdType.MESH

EPS = 1e-6
D_MODEL = 1024
HG_HEADS = 8
HG_DIM = 128
HG_CHUNK = 64
GM_CHUNK = 128
GM_GROUPS = 8
GM_GROUP_DIM = 256
XA_HEADS = 4
XA_DIM = 256
ADAM_LR = 0.001
ADAM_B1 = 0.9
ADAM_B2 = 0.999
ADAM_EPS = 1e-08
ADAM_WD = 0.01
ADAM_STEP = 10

VMEM_CAP_BYTES = 60 * 1024 * 1024
LANES = 1024


def _pick(n, cap, mult=16):
    if n <= cap:
        return n
    for d in range(cap - cap % mult, 0, -mult):
        if n % d == 0:
            return d
    raise ValueError((n, cap, mult))


def _dg(a, b, ca, cb):
    return lax.dot_general(a.astype(BF), b.astype(BF), (((ca,), (cb,)), ((), ())), preferred_element_type=F32)


@jax.custom_vjp
def dot_nn(a, b):
    return _dg(a, b, 1, 0)


def _nn_fwd(a, b):
    return _dg(a, b, 1, 0), (a, b)


def _nn_bwd(r, g):
    a, b = r
    return _dg(g, b, 1, 1), _dg(a, g, 0, 0)


dot_nn.defvjp(_nn_fwd, _nn_bwd)


@jax.custom_vjp
def dot_nt(a, b):
    return _dg(a, b, 1, 1)


def _nt_fwd(a, b):
    return _dg(a, b, 1, 1), (a, b)


def _nt_bwd(r, g):
    a, b = r
    return _dg(g, b, 1, 0), _dg(g, a, 0, 0)


dot_nt.defvjp(_nt_fwd, _nt_bwd)


@jax.custom_vjp
def dot_tn(a, b):
    return _dg(a, b, 0, 0)


def _tn_fwd(a, b):
    return _dg(a, b, 0, 0), (a, b)


def _tn_bwd(r, g):
    a, b = r
    return _dg(b, g, 1, 1), _dg(a, g, 1, 0)


dot_tn.defvjp(_tn_fwd, _tn_bwd)


def _rmsnorm(x, g):
    return x * lax.rsqrt(jnp.mean(x * x, axis=-1, keepdims=True) + EPS) * g


def _silu(x):
    return x * jax.nn.sigmoid(x)


@jax.custom_vjp
def _gelu(x):
    return 0.5 * x * (1.0 + lax.erf(x * (0.5 ** 0.5)))


def _gelu_fwd(x):
    return _gelu(x), x


def _gelu_bwd(x, g):
    t = x * (0.5 ** 0.5)
    cdf = 0.5 * (1.0 + lax.erf(t))
    return (g * (cdf + x * (jnp.exp(-(t * t)) * (0.5 / 3.141592653589793) ** 0.5)),)


_gelu.defvjp(_gelu_fwd, _gelu_bwd)


def _softmax_last(s):
    m = lax.stop_gradient(jnp.max(s, axis=-1, keepdims=True))
    e = jnp.exp(s - m)
    return e / jnp.sum(e, axis=-1, keepdims=True)


def _tril(n):
    r = lax.broadcasted_iota(jnp.int32, (n, n), 0)
    c = lax.broadcasted_iota(jnp.int32, (n, n), 1)
    return r >= c


def _attention(zx, mk, mv):
    s = dot_nt(zx, mk) * (XA_DIM ** -0.5)
    return dot_nn(_softmax_last(s), mv)


def _chunk_sums(x, suffix):
    n = x.shape[0]
    r = lax.broadcasted_iota(jnp.int32, (n, n), 0)
    c = lax.broadcasted_iota(jnp.int32, (n, n), 1)
    tri = jnp.logical_and(r <= c if suffix else r >= c, r // HG_CHUNK == c // HG_CHUNK).astype(BF)
    hi = x.astype(BF)
    rest = x - hi.astype(F32)
    mid = rest.astype(BF)
    lo = (rest - mid.astype(F32)).astype(BF)
    return (_dg(tri, hi, 1, 0) + _dg(tri, mid, 1, 0)) + _dg(tri, lo, 1, 0)


@jax.custom_vjp
def _running_sums(x):
    return _chunk_sums(x, False)


_running_sums.defvjp(lambda x: (_chunk_sums(x, False), None), lambda _, g: (_chunk_sums(g, True),))


def _hgrn_decays(zf, lb3):
    l0, l1, l2 = lb3[0:1], lb3[1:2], lb3[2:3]
    m = lax.stop_gradient(jnp.maximum(jnp.maximum(l0, l1), l2))
    e0 = jnp.exp(l0 - m)
    lb = e0 / (e0 + jnp.exp(l1 - m) + jnp.exp(l2 - m))
    f = lb + (1.0 - lb) * jax.nn.sigmoid(zf)
    return f, _running_sums(jnp.log(f))


def _hgrn_head(zq, f, b, zi, zg, gn, S):
    q = _silu(zq)
    k = 1.0 - f
    b_last = b[HG_CHUNK - 1:HG_CHUNK, :]
    q_dec = q * jnp.exp(b)
    k_inv = k * jnp.exp(-b)
    a = jnp.where(_tril(HG_CHUNK), dot_nt(q_dec, k_inv), 0.0)
    o = dot_nn(a, zi) + dot_nn(q_dec, S)
    S_new = jnp.exp(b_last).reshape(HG_DIM, 1) * S + dot_tn(k * jnp.exp(b_last - b), zi)
    o = _rmsnorm(o, gn) * _silu(zg)
    return o, S_new


def _gmlp_block(zu, zv, zx, lng, lnb, ws, bs, mk, mv):
    gv = [_gelu(v) for v in zv]
    width = GM_GROUPS * GM_GROUP_DIM
    mu = sum(jnp.sum(g, axis=-1, keepdims=True) for g in gv) / width
    xc = [g - mu for g in gv]
    var = sum(jnp.sum(c * c, axis=-1, keepdims=True) for c in xc) / width
    r = lax.rsqrt(var + EPS)
    outs = []
    for g in range(GM_GROUPS):
        v = xc[g] * r * lng[g] + lnb[g]
        w = jnp.where(_tril(GM_CHUNK), ws[g], 0.0)
        mixed = dot_nn(w, v) + bs[g].reshape(GM_CHUNK, 1)
        outs.append(_gelu(zu[g]) * mixed)
    for a in range(XA_HEADS):
        outs.append(_attention(zx[a], mk[a], mv[a]))
    return outs


def _rowcall(name, fn, rows, consts, row_outs, acc_outs, tr):
    nrows = rows[0][0].shape[0]
    tr = _pick(nrows, tr)
    n_r, n_c, n_ro, n_ao = len(rows), len(consts), len(row_outs), len(acc_outs)

    def kern(*refs):
        rv = [r[...] for r in refs[:n_r]]
        cv = [r[...] for r in refs[n_r:n_r + n_c]]
        ro_refs = refs[n_r + n_c:n_r + n_c + n_ro]
        ao_refs = refs[n_r + n_c + n_ro:]
        ro, ao = fn(rv, cv)
        for ref, v in zip(ro_refs, ro):
            ref[...] = v.astype(ref.dtype)
        if n_ao:
            @pl.when(pl.program_id(0) == 0)
            def _():
                for ref in ao_refs:
                    ref[...] = jnp.zeros(ref.shape, ref.dtype)

            for ref, v in zip(ao_refs, ao):
                ref[...] += v.astype(ref.dtype)

    in_specs = [pl.BlockSpec((tr, w), functools.partial(lambda i, cb: (i, cb), cb=cb)) for (_, cb, w) in rows]
    in_specs += [pl.BlockSpec(c.shape, lambda i: (0, 0)) for c in consts]
    out_specs = [pl.BlockSpec((tr, w), lambda i: (i, 0)) for (w, _) in row_outs]
    out_specs += [pl.BlockSpec(s, lambda i: (0, 0)) for (s, _) in acc_outs]
    out_shape = [jax.ShapeDtypeStruct((nrows, w), dt) for (w, dt) in row_outs]
    out_shape += [jax.ShapeDtypeStruct(s, dt) for (s, dt) in acc_outs]
    outs = pl.pallas_call(
        kern, grid=(nrows // tr,), in_specs=in_specs, out_specs=out_specs, out_shape=out_shape, name=name,
        compiler_params=pltpu.CompilerParams(dimension_semantics=("arbitrary",),
                                             vmem_limit_bytes=VMEM_CAP_BYTES),
    )(*[a for (a, _, _) in rows], *consts)
    return outs


def _mm(name, a, b, mode, out_dtype, tm, tn, tk, scale=1.0, res=None, a_lead=None, b_lead=None, norm_gain=None):
    ash = a.shape[-2:]
    bsh = b.shape[-2:]
    if mode == "nn":
        (M, K), (K2, N) = ash, bsh
    elif mode == "nt":
        (M, K), (N, K2) = ash, bsh
    else:
        (K, M), (K2, N) = ash, bsh
    assert K == K2, (name, a.shape, b.shape)
    tm, tn, tk = min(tm, M), min(tn, N), min(tk, K)
    assert M % tm == 0 and N % tn == 0 and K % tk == 0, (name, M, N, K, tm, tn, tk)
    nk = K // tk
    dims = {"nn": (1, 0), "nt": (1, 1), "tn": (0, 0)}[mode]

    def lead(spec_shape, index_fn, lead_idx):
        if lead_idx is None:
            return pl.BlockSpec(spec_shape, index_fn)
        return pl.BlockSpec((None,) + spec_shape, lambda i, j, k: (lead_idx,) + index_fn(i, j, k))

    if mode == "tn":
        a_spec = lead((tk, tm), lambda i, j, k: (k, i), a_lead)
    else:
        a_spec = lead((tm, tk), lambda i, j, k: (i, k), a_lead)
    if mode == "nt":
        b_spec = lead((tn, tk), lambda i, j, k: (j, k), b_lead)
    else:
        b_spec = lead((tk, tn), lambda i, j, k: (k, j), b_lead)
    o_spec = pl.BlockSpec((tm, tn), lambda i, j, k: (i, j))
    has_res = res is not None
    has_norm = norm_gain is not None
    assert not has_norm or tn == N

    def kern(*refs):
        a_ref, b_ref = refs[0], refs[1]
        pos = 2
        res_ref = gain_ref = h_ref = None
        if has_res:
            res_ref, pos = refs[pos], pos + 1
        if has_norm:
            gain_ref, pos = refs[pos], pos + 1
        o_ref, pos = refs[pos], pos + 1
        if has_norm:
            h_ref = refs[pos]
        acc_ref = refs[-1] if nk > 1 else None
        p = lax.dot_general(a_ref[...].astype(BF), b_ref[...].astype(BF), (((dims[0],), (dims[1],)), ((), ())),
                            preferred_element_type=F32)

        def finish(v):
            if scale != 1.0:
                v = v * scale
            if has_res:
                v = res_ref[...] + v
            o_ref[...] = v.astype(o_ref.dtype)
            if has_norm:
                h_ref[...] = _rmsnorm(v, gain_ref[...]).astype(h_ref.dtype)

        if nk == 1:
            finish(p)
        else:
            k = pl.program_id(2)

            @pl.when(k == 0)
            def _():
                acc_ref[...] = p

            @pl.when(k > 0)
            def _():
                acc_ref[...] += p

            @pl.when(k == nk - 1)
            def _():
                finish(acc_ref[...])

    ins = [a, b] + ([res] if has_res else []) + ([norm_gain] if has_norm else [])
    in_specs = [a_spec, b_spec] + ([o_spec] if has_res else [])
    in_specs += [pl.BlockSpec((1, N), lambda i, j, k: (0, 0))] if has_norm else []
    out_sd = jax.ShapeDtypeStruct((M, N), out_dtype)
    return pl.pallas_call(
        kern, grid=(M // tm, N // tn, nk), in_specs=in_specs,
        out_specs=[o_spec, o_spec] if has_norm else o_spec,
        out_shape=[out_sd, jax.ShapeDtypeStruct((M, N), BF)] if has_norm else out_sd,
        scratch_shapes=[pltpu.VMEM((tm, tn), F32)] if nk > 1 else [],
        name=name,
        compiler_params=pltpu.CompilerParams(dimension_semantics=("parallel", "parallel", "arbitrary"),
                                             vmem_limit_bytes=VMEM_CAP_BYTES),
    )(*ins)


def _ffn_in_swiglu(name, h, w3, tm, tn):
    T, D = h.shape
    dff = w3.shape[2] // 2
    tm = min(tm, T)
    assert T % tm == 0 and dff % tn == 0
    nj = dff // tn

    def kern(h_ref, wg_ref, wu_ref, zg_ref, zu_ref, a_ref):
        hb = h_ref[...]
        g = jnp.dot(hb, wg_ref[...], preferred_element_type=F32).astype(BF)
        u = jnp.dot(hb, wu_ref[...], preferred_element_type=F32).astype(BF)
        zg_ref[...] = g
        zu_ref[...] = u
        a_ref[...] = (_silu(g.astype(F32)) * u.astype(F32)).astype(BF)

    o_spec = pl.BlockSpec((tm, tn), lambda i, j: (i, j))
    return pl.pallas_call(
        kern, grid=(T // tm, nj),
        in_specs=[pl.BlockSpec((tm, D), lambda i, j: (i, 0)),
                  pl.BlockSpec((None, D, tn), lambda i, j: (0, 0, j)),
                  pl.BlockSpec((None, D, tn), lambda i, j: (0, 0, j + nj))],
        out_specs=[o_spec, o_spec, o_spec],
        out_shape=[jax.ShapeDtypeStruct((T, dff), BF)] * 3, name=name,
        compiler_params=pltpu.CompilerParams(dimension_semantics=("parallel", "arbitrary"),
                                             vmem_limit_bytes=VMEM_CAP_BYTES),
    )(h, w3, w3)


def _ffn_da_swiglu(name, dxo, w3, zg, zu, tm):
    T, D = dxo.shape
    dff = w3.shape[1]
    tm = min(tm, T)
    assert T % tm == 0 and dff % 2 == 0
    hc = dff // 2

    def kern(d_ref, w_ref, g_ref, u_ref, dz_ref):
        db = (d_ref[...] * 0.5).astype(BF)
        for s in range(2):
            cols = slice(s * hc, (s + 1) * hc)
            da = lax.dot_general(db, w_ref[cols, :], (((1,), (1,)), ((), ())), preferred_element_type=F32)
            g = g_ref[:, cols].astype(F32)
            sg = 1.0 / (1.0 + jnp.exp(-g))
            gs = g * sg
            dab = da.astype(BF)
            dz_ref[:, cols] = (dab * u_ref[:, cols]) * (sg + gs * (1.0 - sg)).astype(BF)
            dz_ref[:, dff + s * hc:dff + (s + 1) * hc] = dab * gs.astype(BF)

    row = lambda w: pl.BlockSpec((tm, w), lambda i: (i, 0))
    return pl.pallas_call(
        kern, grid=(T // tm,),
        in_specs=[row(D), pl.BlockSpec((None, dff, D), lambda i: (0, 0, 0), pipeline_mode=pl.Buffered(1)), row(dff), row(dff)],
        out_specs=row(2 * dff), out_shape=jax.ShapeDtypeStruct((T, 2 * dff), BF), name=name,
        compiler_params=pltpu.CompilerParams(dimension_semantics=("arbitrary",), vmem_limit_bytes=VMEM_CAP_BYTES),
    )(dxo, w3, zg, zu)


def _mm_dh_rms(name, dz, w3, xin, g, dres, tm):
    T, K = dz.shape
    D = w3.shape[1]
    tm = min(tm, T)
    assert T % tm == 0

    def kern(dz_ref, w_ref, x_ref, g_ref, r_ref, dx_ref, dg_ref):
        dh = lax.dot_general(dz_ref[...], w_ref[...], (((1,), (1,)), ((), ())), preferred_element_type=F32)
        _, vjp = jax.vjp(_rmsnorm, x_ref[...], g_ref[...])
        dx, dg = vjp(dh)
        dx_ref[...] = dx + r_ref[...]

        @pl.when(pl.program_id(0) == 0)
        def _():
            dg_ref[...] = jnp.zeros(dg_ref.shape, F32)

        dg_ref[...] += dg

    row = lambda w: pl.BlockSpec((tm, w), lambda i: (i, 0))
    one = pl.BlockSpec((1, D), lambda i: (0, 0))
    return pl.pallas_call(
        kern, grid=(T // tm,),
        in_specs=[row(K), pl.BlockSpec((None, D, K), lambda i: (0, 0, 0), pipeline_mode=pl.Buffered(1)), row(D), one, row(D)],
        out_specs=[row(D), one], out_shape=[jax.ShapeDtypeStruct((T, D), F32), jax.ShapeDtypeStruct((1, D), F32)], name=name,
        compiler_params=pltpu.CompilerParams(dimension_semantics=("arbitrary",), vmem_limit_bytes=VMEM_CAP_BYTES),
    )(dz, w3, xin, g, dres)


def _mm_tn_pair(name, a, b, kind, c_arr, tq, tk, scale=1.0):
    T, M = a.shape
    _, N = b.shape
    tk = min(tk, T)
    assert T % tk == 0
    nk = T // tk
    if kind == "col":
        hm = M // 2
        assert N % tq == 0
        nq = N // tq
        tile = (hm, tq)
        a_spec = pl.BlockSpec((tk, hm), lambda h, q, k, c: (k, jnp.bitwise_xor(h, 1 - c[0])))
        b_spec = pl.BlockSpec((tk, tq), lambda h, q, k, c: (k, q))
        o_spec = pl.BlockSpec(tile, lambda h, q, k, c: (0, q * h))
        out_sd = (hm, N)
    else:
        hn = N // 2
        assert M % tq == 0
        nq = M // tq
        tile = (tq, hn)
        a_spec = pl.BlockSpec((tk, tq), lambda h, q, k, c: (k, q))
        b_spec = pl.BlockSpec((tk, hn), lambda h, q, k, c: (k, jnp.bitwise_xor(h, 1 - c[0])))
        o_spec = pl.BlockSpec(tile, lambda h, q, k, c: (q * h, 0))
        out_sd = (M, hn)

    def kern(c_ref, a_ref, b_ref, o_ref, acc, stage, recv, ssem, rsem):
        h, q, k = pl.program_id(0), pl.program_id(1), pl.program_id(2)
        x, y, c, _ = _place()
        p = lax.dot_general(a_ref[...].astype(BF), b_ref[...].astype(BF), (((0,), (0,)), ((), ())), preferred_element_type=F32)

        @pl.when(k == 0)
        def _():
            acc[...] = p

        @pl.when(k > 0)
        def _():
            acc[...] += p

        def send(slot, qq):
            return pltpu.make_async_remote_copy(src_ref=stage.at[slot], dst_ref=recv.at[qq], send_sem=ssem.at[slot],
                                                recv_sem=rsem.at[qq], device_id=(x, y, 1 - c), device_id_type=MESH)

        last = k == nk - 1

        @pl.when(jnp.logical_and(last, h == 0))
        def _():
            slot = q % 2

            @pl.when(q >= 2)
            def _():
                send(slot, q).wait_send()

            stage[slot] = (acc[...] * scale).astype(BF)
            send(slot, q).start()

        @pl.when(jnp.logical_and(last, h == 1))
        def _():
            @pl.when(q == 0)
            def _():
                for s in range(min(nq, 2)):
                    send(s, 0).wait_send()

            send(0, q).wait_recv()
            o_ref[...] = (acc[...] * scale + recv[q].astype(F32)).astype(o_ref.dtype)

    return pl.pallas_call(
        kern,
        grid_spec=pltpu.PrefetchScalarGridSpec(
            num_scalar_prefetch=1, grid=(2, nq, nk), in_specs=[a_spec, b_spec], out_specs=o_spec,
            scratch_shapes=[pltpu.VMEM(tile, F32), pltpu.VMEM((2,) + tile, BF), pltpu.VMEM((nq,) + tile, BF),
                            pltpu.SemaphoreType.DMA((2,)), pltpu.SemaphoreType.DMA((nq,))]),
        out_shape=jax.ShapeDtypeStruct(out_sd, BF), name=name,
        compiler_params=pltpu.CompilerParams(dimension_semantics=("arbitrary", "arbitrary", "arbitrary"),
                                             vmem_limit_bytes=VMEM_CAP_BYTES),
    )(c_arr, a, b)


def _kv_pieces(kv_ref):
    W = XA_HEADS * XA_DIM
    mk = [kv_ref[:, a * XA_DIM:(a + 1) * XA_DIM] for a in range(XA_HEADS)]
    mv = [kv_ref[:, W + a * XA_DIM:W + (a + 1) * XA_DIM] for a in range(XA_HEADS)]
    return mk, mv


HG_SUB = 4


def _hgrn_rows(z_ref):
    W = HG_HEADS * HG_DIM

    def piece(c, col, w):
        return z_ref[c * HG_CHUNK:(c + 1) * HG_CHUNK, col:col + w]

    zq = [[piece(c, h * HG_DIM, HG_DIM) for h in range(HG_HEADS)] for c in range(HG_SUB)]
    zf = z_ref[:, W:2 * W]
    zi =[[piece(c, 2 * W + h * HG_DIM, HG_DIM) for h in range(HG_HEADS)] for c in range(HG_SUB)]
    zg = [[piece(c, 3 * W + h * HG_DIM, HG_DIM) for h in range(HG_HEADS)] for c in range(HG_SUB)]
    zx = [z_ref[:, 4 * W + a * XA_DIM:4 * W + (a + 1) * XA_DIM] for a in range(XA_HEADS)]
    return zq, zf, zi, zg, zx


def _hgrn_steps(zq, zf, zi, zg, zx, lb3, gn, mk, mv, S):
    f, b = _hgrn_decays(zf, lb3)
    mix = []
    for c in range(HG_SUB):
        row, s_next = [], []
        rows = slice(c * HG_CHUNK, (c + 1) * HG_CHUNK)
        for h in range(HG_HEADS):
            cols = slice(h * HG_DIM, (h + 1) * HG_DIM)
            o, sn = _hgrn_head(zq[c][h], f[rows, cols], b[rows, cols], zi[c][h], zg[c][h], gn, S[h])
            row.append(o)
            s_next.append(sn)
        mix.append(row)
        S = s_next
    att = [_attention(zx[a], mk[a], mv[a]) for a in range(XA_HEADS)]
    return mix, att, S


def _hgrn_fwd2(z, lb_logits, gnorm, kv, bl, seq):
    T, zw = z.shape
    mem_len = kv.shape[0] // bl
    cat_w = HG_HEADS * HG_DIM + XA_HEADS * XA_DIM
    R = HG_SUB * HG_CHUNK
    nb = seq // R

    def kern(z_ref, lb_ref, gn_ref, kv_ref, cat_ref, st_ref, s_scr):
        @pl.when(pl.program_id(1) == 0)
        def _():
            s_scr[...] = jnp.zeros(s_scr.shape, F32)

        st_ref[...] = s_scr[...]
        zq, zf, zi, zg, zx = _hgrn_rows(z_ref)
        mk, mv = _kv_pieces(kv_ref)
        S = [s_scr[h] for h in range(HG_HEADS)]
        mix, att, s_new = _hgrn_steps(zq, zf, zi, zg, zx, lb_ref[...], gn_ref[...], mk, mv, S)
        for c in range(HG_SUB):
            for h in range(HG_HEADS):
                cat_ref[c * HG_CHUNK:(c + 1) * HG_CHUNK, h * HG_DIM:(h + 1) * HG_DIM] = mix[c][h].astype(cat_ref.dtype)
        for h in range(HG_HEADS):
            s_scr[h] = s_new[h]
        base = HG_HEADS * HG_DIM
        for a in range(XA_HEADS):
            cat_ref[:, base + a * XA_DIM:base + (a + 1) * XA_DIM] = att[a].astype(cat_ref.dtype)

    return pl.pallas_call(
        kern, grid=(bl, nb),
        in_specs=[pl.BlockSpec((R, zw), lambda b, n: (b * nb + n, 0)),
                  pl.BlockSpec(lb_logits.shape, lambda b, n: (0, 0)),
                  pl.BlockSpec(gnorm.shape, lambda b, n: (0, 0)),
                  pl.BlockSpec((mem_len, kv.shape[1]), lambda b, n: (b, 0))],
        out_specs=[pl.BlockSpec((R, cat_w), lambda b, n: (b * nb + n, 0)),
                   pl.BlockSpec((None, HG_HEADS, HG_DIM, HG_DIM), lambda b, n: (b * nb + n, 0, 0, 0))],
        out_shape=[jax.ShapeDtypeStruct((T, cat_w), BF),
                   jax.ShapeDtypeStruct((bl * nb, HG_HEADS, HG_DIM, HG_DIM), F32)],
        scratch_shapes=[pltpu.VMEM((HG_HEADS, HG_DIM, HG_DIM), F32)],
        name="hgrn_fwd",
        compiler_params=pltpu.CompilerParams(dimension_semantics=("arbitrary", "arbitrary"), vmem_limit_bytes=VMEM_CAP_BYTES),
    )(z, lb_logits, gnorm, kv)


def _hgrn_bwd2(z, dcat, stash, lb_logits, gnorm, kv, bl, seq):
    T, zw = z.shape
    mem_len = kv.shape[0] // bl
    cat_w = dcat.shape[1]
    R = HG_SUB * HG_CHUNK
    nb = seq // R

    def kern(z_ref, dc_ref, st_ref, lb_ref, gn_ref, kv_ref, dz_ref, dkv_ref, dlb_ref, dgn_ref, ds_scr):
        first = jnp.logical_and(pl.program_id(0) == 0, pl.program_id(1) == 0)

        @pl.when(pl.program_id(1) == 0)
        def _():
            ds_scr[...] = jnp.zeros(ds_scr.shape, F32)
            dkv_ref[...] = jnp.zeros(dkv_ref.shape, F32)

        @pl.when(first)
        def _():
            dlb_ref[...] = jnp.zeros(dlb_ref.shape, F32)
            dgn_ref[...] = jnp.zeros(dgn_ref.shape, F32)

        zq, zf, zi, zg, zx = _hgrn_rows(z_ref)
        mk, mv = _kv_pieces(kv_ref)
        S = [st_ref[h] for h in range(HG_HEADS)]
        _, vjp = jax.vjp(_hgrn_steps, zq, zf, zi, zg, zx, lb_ref[...], gn_ref[...], mk, mv, S)
        d_mix = [[dc_ref[c * HG_CHUNK:(c + 1) * HG_CHUNK, h * HG_DIM:(h + 1) * HG_DIM] for h in range(HG_HEADS)]
                 for c in range(HG_SUB)]
        base = HG_HEADS * HG_DIM
        d_att = [dc_ref[:, base + a * XA_DIM:base + (a + 1) * XA_DIM] for a in range(XA_HEADS)]
        d_s = [ds_scr[h] for h in range(HG_HEADS)]
        dzq, dzf, dzi, dzg, dzx, dlb3, dgn, dmk, dmv, dS = vjp((d_mix, d_att, d_s))
        W = HG_HEADS * HG_DIM
        dz_ref[:, W:2 * W] = dzf.astype(dz_ref.dtype)
        for c in range(HG_SUB):
            rows = slice(c * HG_CHUNK, (c + 1) * HG_CHUNK)
            for h in range(HG_HEADS):
                for k, part in ((0, dzq), (2, dzi), (3, dzg)):
                    dz_ref[rows, k * W + h * HG_DIM:k * W + (h + 1) * HG_DIM] = part[c][h].astype(dz_ref.dtype)
        for h in range(HG_HEADS):
            ds_scr[h] = dS[h]
        dlb_ref[...] += dlb3
        dgn_ref[...] += dgn
        KW = XA_HEADS * XA_DIM
        for a in range(XA_HEADS):
            dz_ref[:, 4 * W + a * XA_DIM:4 * W + (a + 1) * XA_DIM] = dzx[a].astype(dz_ref.dtype)
            dkv_ref[:, a * XA_DIM:(a + 1) * XA_DIM] += dmk[a]
            dkv_ref[:, KW + a * XA_DIM:KW + (a + 1) * XA_DIM] += dmv[a]

    rev = lambda b, n: (b * nb + (nb - 1 - n), 0)
    return pl.pallas_call(
        kern, grid=(bl, nb),
        in_specs=[pl.BlockSpec((R, zw), rev),
                  pl.BlockSpec((R, cat_w), rev),
                  pl.BlockSpec((None, HG_HEADS, HG_DIM, HG_DIM), lambda b, n: (b * nb + (nb - 1 - n), 0, 0, 0)),
                  pl.BlockSpec(lb_logits.shape, lambda b, n: (0, 0)),
                  pl.BlockSpec(gnorm.shape, lambda b, n: (0, 0)),
                  pl.BlockSpec((mem_len, kv.shape[1]), lambda b, n: (b, 0))],
        out_specs=[pl.BlockSpec((R, zw), rev),
                   pl.BlockSpec((mem_len, kv.shape[1]), lambda b, n: (b, 0)),
                   pl.BlockSpec(lb_logits.shape, lambda b, n: (0, 0)),
                   pl.BlockSpec(gnorm.shape, lambda b, n: (0, 0))],
        out_shape=[jax.ShapeDtypeStruct((T, zw), BF), jax.ShapeDtypeStruct(kv.shape, F32),
                   jax.ShapeDtypeStruct(lb_logits.shape, F32), jax.ShapeDtypeStruct(gnorm.shape, F32)],
        scratch_shapes=[pltpu.VMEM((HG_HEADS, HG_DIM, HG_DIM), F32)],
        name="hgrn_bwd",
        compiler_params=pltpu.CompilerParams(dimension_semantics=("arbitrary", "arbitrary"), vmem_limit_bytes=VMEM_CAP_BYTES),
    )(z, dcat, stash, lb_logits, gnorm, kv)


GM_SUB = 2


def _gmlp_pieces(z_ref):
    W = GM_GROUPS * GM_GROUP_DIM
    zu = [z_ref[:, g * GM_GROUP_DIM:(g + 1) * GM_GROUP_DIM] for g in range(GM_GROUPS)]
    zv = [z_ref[:, W + g * GM_GROUP_DIM:W + (g + 1) * GM_GROUP_DIM] for g in range(GM_GROUPS)]
    zx = [z_ref[:, 2 * W + a * XA_DIM:2 * W + (a + 1) * XA_DIM] for a in range(XA_HEADS)]
    return zu, zv, zx


def _gmlp_params(lng_ref, lnb_ref, ws_ref, bs_ref):
    lng = [lng_ref[:, g * GM_GROUP_DIM:(g + 1) * GM_GROUP_DIM] for g in range(GM_GROUPS)]
    lnb = [lnb_ref[:, g * GM_GROUP_DIM:(g + 1) * GM_GROUP_DIM] for g in range(GM_GROUPS)]
    ws = [ws_ref[g] for g in range(GM_GROUPS)]
    bs = [bs_ref[g:g + 1, :] for g in range(GM_GROUPS)]
    return lng, lnb, ws, bs


def _gmlp_fwd(z, ln_g, ln_b, w_s, b_s, kv, bl, nc):
    T, zw = z.shape
    mem_len = kv.shape[0] // bl
    cat_w = GM_GROUPS * GM_GROUP_DIM + XA_HEADS * XA_DIM

    assert nc % GM_SUB == 0
    nc = nc // GM_SUB
    R = GM_SUB * GM_CHUNK

    def kern(z_ref, lng_ref, lnb_ref, ws_ref, bs_ref, kv_ref, cat_ref):
        lng, lnb, ws, bs = _gmlp_params(lng_ref, lnb_ref, ws_ref, bs_ref)
        mk, mv = _kv_pieces(kv_ref)
        for c in range(GM_SUB):
            rows = pl.ds(c * GM_CHUNK, GM_CHUNK)
            zu, zv, zx = _gmlp_pieces(z_ref.at[rows])
            out = cat_ref.at[rows]
            outs = _gmlp_block(zu, zv, zx, lng, lnb, ws, bs, mk, mv)
            for g in range(GM_GROUPS):
                out[:, g * GM_GROUP_DIM:(g + 1) * GM_GROUP_DIM] = outs[g].astype(cat_ref.dtype)
            base = GM_GROUPS * GM_GROUP_DIM
            for a in range(XA_HEADS):
                out[:, base + a * XA_DIM:base + (a + 1) * XA_DIM] = outs[GM_GROUPS + a].astype(cat_ref.dtype)

    full2 = lambda b, n: (0, 0)
    return pl.pallas_call(
        kern, grid=(bl, nc),
        in_specs=[pl.BlockSpec((R, zw), lambda b, n: (b * nc + n, 0)),
                  pl.BlockSpec(ln_g.shape, full2), pl.BlockSpec(ln_b.shape, full2),
                  pl.BlockSpec(w_s.shape, lambda b, n: (0, 0, 0)), pl.BlockSpec(b_s.shape, full2),
                  pl.BlockSpec((mem_len, kv.shape[1]), lambda b, n: (b, 0))],
        out_specs=pl.BlockSpec((R, cat_w), lambda b, n: (b * nc + n, 0)),
        out_shape=jax.ShapeDtypeStruct((T, cat_w), BF),
        name="gmlp_fwd",
        compiler_params=pltpu.CompilerParams(dimension_semantics=("arbitrary", "arbitrary"), vmem_limit_bytes=VMEM_CAP_BYTES),
    )(z, ln_g, ln_b, w_s, b_s, kv)


def _gmlp_bwd(z, dcat, ln_g, ln_b, w_s, b_s, kv, bl, nc):
    T, zw = z.shape
    mem_len = kv.shape[0] // bl
    cat_w = dcat.shape[1]
    assert nc % GM_SUB == 0
    nc = nc // GM_SUB

    def kern(z_ref, dc_ref, lng_ref, lnb_ref, ws_ref, bs_ref, kv_ref,
             dz_ref, dkv_ref, dlng_ref, dlnb_ref, dws_ref, dbs_ref):
        first = jnp.logical_and(pl.program_id(0) == 0, pl.program_id(1) == 0)

        @pl.when(pl.program_id(1) == 0)
        def _():
            dkv_ref[...] = jnp.zeros(dkv_ref.shape, F32)

        @pl.when(first)
        def _():
            dlng_ref[...] = jnp.zeros(dlng_ref.shape, F32)
            dlnb_ref[...] = jnp.zeros(dlnb_ref.shape, F32)
            dws_ref[...] = jnp.zeros(dws_ref.shape, F32)
            dbs_ref[...] = jnp.zeros(dbs_ref.shape, F32)

        lng, lnb, ws, bs = _gmlp_params(lng_ref, lnb_ref, ws_ref, bs_ref)
        mk, mv = _kv_pieces(kv_ref)
        W = GM_GROUPS * GM_GROUP_DIM
        KW = XA_HEADS * XA_DIM
        for c in range(GM_SUB):
            rows = pl.ds(c * GM_CHUNK, GM_CHUNK)
            zu, zv, zx = _gmlp_pieces(z_ref.at[rows])
            dc, dz = dc_ref.at[rows], dz_ref.at[rows]
            _, vjp = jax.vjp(_gmlp_block, zu, zv, zx, lng, lnb, ws, bs, mk, mv)
            d_outs = [dc[:, g * GM_GROUP_DIM:(g + 1) * GM_GROUP_DIM] for g in range(GM_GROUPS)]
            d_outs += [dc[:, W + a * XA_DIM:W + (a + 1) * XA_DIM] for a in range(XA_HEADS)]
            dzu, dzv, dzx, dlng, dlnb, dws, dbs, dmk, dmv = vjp(d_outs)
            for g in range(GM_GROUPS):
                sl = slice(g * GM_GROUP_DIM, (g + 1) * GM_GROUP_DIM)
                dz[:, sl] = dzu[g].astype(dz_ref.dtype)
                dz[:, W + g * GM_GROUP_DIM:W + (g + 1) * GM_GROUP_DIM] = dzv[g].astype(dz_ref.dtype)
                dlng_ref[:, sl] += dlng[g]
                dlnb_ref[:, sl] += dlnb[g]
                dws_ref[g] += dws[g]
                dbs_ref[g:g + 1, :] += dbs[g]
            for a in range(XA_HEADS):
                dz[:, 2 * W + a * XA_DIM:2 * W + (a + 1) * XA_DIM] = dzx[a].astype(dz_ref.dtype)
                dkv_ref[:, a * XA_DIM:(a + 1) * XA_DIM] += dmk[a]
                dkv_ref[:, KW + a * XA_DIM:KW + (a + 1) * XA_DIM] += dmv[a]

    full2 = lambda b, n: (0, 0)
    full3 = lambda b, n: (0, 0, 0)
    blk = lambda b, n: (b * nc + n, 0)
    return pl.pallas_call(
        kern, grid=(bl, nc),
        in_specs=[pl.BlockSpec((GM_SUB * GM_CHUNK, zw), blk), pl.BlockSpec((GM_SUB * GM_CHUNK, cat_w), blk),
                  pl.BlockSpec(ln_g.shape, full2), pl.BlockSpec(ln_b.shape, full2),
                  pl.BlockSpec(w_s.shape, full3), pl.BlockSpec(b_s.shape, full2),
                  pl.BlockSpec((mem_len, kv.shape[1]), lambda b, n: (b, 0))],
        out_specs=[pl.BlockSpec((GM_SUB * GM_CHUNK, zw), blk),
                   pl.BlockSpec((mem_len, kv.shape[1]), lambda b, n: (b, 0)),
                   pl.BlockSpec(ln_g.shape, full2), pl.BlockSpec(ln_b.shape, full2),
                   pl.BlockSpec(w_s.shape, full3), pl.BlockSpec(b_s.shape, full2)],
        out_shape=[jax.ShapeDtypeStruct((T, zw), BF), jax.ShapeDtypeStruct(kv.shape, F32),
                   jax.ShapeDtypeStruct(ln_g.shape, F32), jax.ShapeDtypeStruct(ln_b.shape, F32),
                   jax.ShapeDtypeStruct(w_s.shape, F32), jax.ShapeDtypeStruct(b_s.shape, F32)],
        name="gmlp_bwd",
        compiler_params=pltpu.CompilerParams(dimension_semantics=("arbitrary", "arbitrary"), vmem_limit_bytes=VMEM_CAP_BYTES),
    )(z, dcat, ln_g, ln_b, w_s, b_s, kv)


def _place():
    x, y, c = lax.axis_index("x"), lax.axis_index("y"), lax.axis_index("c")
    chips = [(1 - x, y), (x, 1 - y), (1 - x, 1 - y)]
    return x, y, c, chips


def _half(ref, kind, e):
    if kind == "col":
        n = ref.shape[1] // 2
        return ref.at[:, pl.ds(pl.multiple_of(e * n, n), n), :]
    n = ref.shape[2] // 2
    return ref.at[:, :, pl.ds(pl.multiple_of(e * n, n), n)]


def _slot(ref, kind, j, n):
    if kind == "col":
        return ref.at[:, :, pl.ds(pl.multiple_of(j * n, n), n)]
    return ref.at[:, pl.ds(pl.multiple_of(j * n, n), n), :]


def _allgather_seq(name, items, cid):
    nt = len(items)
    kinds = [k for (_, k, _) in items]
    slot_kind = ["row" if k == "row" else "col" for k in kinds]
    out_type = []
    for s, k, l in items:
        L, r, c = s.shape
        lo = L if l is None else 1
        out_type.append(jax.ShapeDtypeStruct((lo, 4 * r, c) if k == "row" else (lo, r, 4 * c), s.dtype))

    def part(ref, t, e):
        return ref if kinds[t] == "vec" else _half(ref, kinds[t], e)

    def body(*refs):
        sh = [refs[t] if items[t][2] is None else refs[t].at[pl.ds(items[t][2], 1)] for t in range(nt)]
        full = refs[nt:2 * nt]
        s_ici, r_ici, s_d2d, r_d2d = refs[2 * nt:]
        x, y, c, chips = _place()
        own = 2 * x + y
        sibling = (x, y, 1 - c)
        barrier = pltpu.get_barrier_semaphore()
        for peer in [(px, py, c) for (px, py) in chips] + [sibling]:
            pl.semaphore_signal(barrier, inc=1, device_id=peer, device_id_type=MESH)
        pl.semaphore_wait(barrier, 4)
        width = [sh[t].shape[1] if kinds[t] == "row" else sh[t].shape[2] for t in range(nt)]
        sent = []
        for t in range(nt):
            for p, (px, py) in enumerate(chips):
                cp = pltpu.make_async_remote_copy(
                    src_ref=part(sh[t], t, c), dst_ref=part(_slot(full[t], slot_kind[t], own, width[t]), t, c),
                    send_sem=s_ici.at[t, p], recv_sem=r_ici.at[t, p], device_id=(px, py, c), device_id_type=MESH)
                cp.start()
                sent.append(cp)
        for t in range(nt):
            for p, (px, py) in enumerate(chips):
                landed = part(_slot(full[t], slot_kind[t], 2 * px + py, width[t]), t, c)
                pltpu.make_async_remote_copy(
                    src_ref=landed, dst_ref=landed, send_sem=s_ici.at[t, p], recv_sem=r_ici.at[t, p],
                    device_id=(px, py, c), device_id_type=MESH).wait_recv()
                if kinds[t] == "vec":
                    continue
                fw = pltpu.make_async_remote_copy(
                    src_ref=landed, dst_ref=landed, send_sem=s_d2d.at[t, p], recv_sem=r_d2d.at[t, p],
                    device_id=sibling, device_id_type=MESH)
                fw.start()
                sent.append(fw)
        for t in range(nt):
            if kinds[t] == "vec":
                continue
            for p, (px, py) in enumerate(chips):
                other = _half(_slot(full[t], kinds[t], 2 * px + py, width[t]), kinds[t], 1 - c)
                pltpu.make_async_remote_copy(
                    src_ref=other, dst_ref=other, send_sem=s_d2d.at[t, p], recv_sem=r_d2d.at[t, p],
                    device_id=sibling, device_id_type=MESH).wait_recv()
        for cp in sent:
            cp.wait_send()

    sems = pltpu.SemaphoreType.DMA
    return pl.kernel(
        body, out_type=out_type, mesh=plsc.ScalarSubcoreMesh(axis_name="seq", num_cores=1),
        scratch_types=[sems((nt, 3)), sems((nt, 3)), sems((nt, 3)), sems((nt, 3))],
        compiler_params=pltpu.CompilerParams(collective_id=cid), name=name,
    )(*[s for (s, _, _) in items])


def _place_own(name, full, shard, kind, layer, chip_arr, after):
    lo, r, c = (shard.shape[0] if layer is None else 1,) + shard.shape[1:]
    first = 0 if layer is None else layer
    tr = _pick(r, 512)
    nr = r // tr

    def body(chip_ref, s_ref, f_ref, after_ref, o_ref):
        o_ref[...] = s_ref[...]

    if kind == "row":
        out_map = lambda i, j, chip: (i, chip[0] * nr + j, 0)
    else:
        out_map = lambda i, j, chip: (i, j, chip[0])
    return pl.pallas_call(
        body, out_shape=jax.ShapeDtypeStruct(full.shape, full.dtype),
        grid_spec=pltpu.PrefetchScalarGridSpec(
            num_scalar_prefetch=1, grid=(lo, nr),
            in_specs=[pl.BlockSpec((1, tr, c), lambda i, j, chip: (i + first, j, 0)), pl.BlockSpec(memory_space=pl.ANY),
                      pl.BlockSpec(memory_space=pl.ANY)],
            out_specs=pl.BlockSpec((1, tr, c), out_map)),
        input_output_aliases={2: 0},
        compiler_params=pltpu.CompilerParams(dimension_semantics=("parallel", "parallel"), vmem_limit_bytes=VMEM_CAP_BYTES),
        name=name,
    )(chip_arr, shard, full, after)


def _slot2(ref, kind, j, n):
    if kind == "col":
        return ref.at[:, pl.ds(pl.multiple_of(j * n, n), n)]
    return ref.at[pl.ds(pl.multiple_of(j * n, n), n), :]


def _rs_chips_seq(name, parts, kinds, cid):
    nm = len(parts)
    out_type = []
    for g, k in zip(parts, kinds):
        r, c = g.shape
        ps = (r, c // 4) if k == "col" else (r // 4, c)
        out_type += [jax.ShapeDtypeStruct(ps, BF), jax.ShapeDtypeStruct((3,) + ps, BF)]

    def body(*refs):
        g = refs[:nm]
        outs = refs[nm:3 * nm]
        loc, ssem, rsem = refs[3 * nm:]
        x, y, c, chips = _place()
        own = 2 * x + y
        barrier = pltpu.get_barrier_semaphore()
        for (px, py) in chips:
            pl.semaphore_signal(barrier, inc=1, device_id=(px, py, c), device_id_type=MESH)
        pl.semaphore_wait(barrier, 3)
        cps = []
        for m in range(nm):
            k = kinds[m]
            own_o, got_o = outs[2 * m], outs[2 * m + 1]
            n = g[m].shape[1] // 4 if k == "col" else g[m].shape[0] // 4
            lc = pltpu.make_async_copy(_slot2(g[m], k, own, n), own_o, loc.at[m])
            lc.start()
            cps.append(lc)
            for p, (px, py) in enumerate(chips):
                cp = pltpu.make_async_remote_copy(
                    src_ref=_slot2(g[m], k, 2 * px + py, n), dst_ref=got_o.at[p],
                    send_sem=ssem.at[m, p], recv_sem=rsem.at[m, p], device_id=(px, py, c), device_id_type=MESH)
                cp.start()
                cps.append(cp)
        for cp in cps:
            cp.wait()

    return pl.kernel(
        body, out_type=out_type, mesh=plsc.ScalarSubcoreMesh(axis_name="seq", num_cores=1),
        scratch_types=[pltpu.SemaphoreType.DMA((nm,)), pltpu.SemaphoreType.DMA((nm, 3)), pltpu.SemaphoreType.DMA((nm, 3))],
        compiler_params=pltpu.CompilerParams(collective_id=cid), name=name,
    )(*parts)


def _finish_share(name, owns, gots, kind, c_arr):
    L = len(owns)
    r, c = owns[0].shape
    tr = _pick(r, 128 if kind == "col" else 256)
    nb = r // tr
    nq = L * nb

    def chunk_of(l):
        return lambda h, q: jnp.clip(q * (1 - h) + (nq - 1) * h - l * nb, 0, nb - 1)

    ins, in_specs = [], []
    for l in range(L):
        at = chunk_of(l)
        ins += [owns[l], gots[l].reshape(3 * r, c), gots[l].reshape(3 * r, c), gots[l].reshape(3 * r, c)]
        in_specs.append(pl.BlockSpec((tr, c), functools.partial(lambda h, q, cc, at: (at(h, q), 0), at=at)))
        in_specs += [pl.BlockSpec((tr, c), functools.partial(lambda h, q, cc, at, p: (p * nb + at(h, q), 0), at=at, p=p))
                     for p in range(3)]
    if kind == "col":
        out_sd = (L, 2, r, c)
        o_spec = pl.BlockSpec((None, 2, tr, c), lambda h, q, cc: ((q * h) // nb, 0, (q * h) % nb, 0))
    else:
        out_sd = (L * r, 2 * c)
        o_spec = pl.BlockSpec((tr, 2 * c), lambda h, q, cc: (q * h, 0))

    def kern(c_ref, *refs):
        in_refs = refs[:4 * L]
        out_ref, mine, recv, ssem, rsem = refs[4 * L:]
        h, q = pl.program_id(0), pl.program_id(1)
        x, y, cc, _ = _place()

        def swap(qq):
            return pltpu.make_async_remote_copy(src_ref=mine.at[qq], dst_ref=recv.at[qq], send_sem=ssem.at[qq],
                                                recv_sem=rsem.at[qq], device_id=(x, y, 1 - cc), device_id_type=MESH)

        for l in range(L):
            @pl.when(jnp.logical_and(h == 0, q // nb == l))
            def _(l=l):
                o_ref, g0, g1, g2 = in_refs[4 * l:4 * l + 4]
                mine[q] = ((o_ref[...].astype(F32) + g0[...].astype(F32)) + g1[...].astype(F32)) + g2[...].astype(F32)
                swap(q).start()

        @pl.when(h == 1)
        def _():
            swap(q).wait()
            a, b = mine[q], recv[q]
            first = c_ref[0] == 0
            lo, hi = jnp.where(first, a, b), jnp.where(first, b, a)
            if kind == "col":
                out_ref[0] = lo
                out_ref[1] = hi
            else:
                out_ref[:, :c] = lo
                out_ref[:, c:] = hi

    full = pl.pallas_call(
        kern,
        grid_spec=pltpu.PrefetchScalarGridSpec(
            num_scalar_prefetch=1, grid=(2, nq), in_specs=in_specs, out_specs=o_spec,
            scratch_shapes=[pltpu.VMEM((nq, tr, c), F32), pltpu.VMEM((nq, tr, c), F32),
                            pltpu.SemaphoreType.DMA((nq,)), pltpu.SemaphoreType.DMA((nq,))]),
        out_shape=jax.ShapeDtypeStruct(out_sd, F32), name=name,
        compiler_params=pltpu.CompilerParams(dimension_semantics=("arbitrary", "arbitrary"),
                                             vmem_limit_bytes=VMEM_CAP_BYTES),
    )(c_arr, *ins)
    return full.reshape(L, 2 * r, c) if kind == "col" else full.reshape(L, r, 2 * c)


def _small_allreduce(buf, name):
    R = buf.shape[0]
    assert R % 16 == 0
    h = R // 2

    def body(x_ref, o_ref, sib, csum, got, s_a, r_a, s_b, r_b, s_c, r_c):
        x, y, c, chips = _place()
        sibling = (x, y, 1 - c)
        own = 2 * x + y
        swap = pltpu.make_async_remote_copy(src_ref=x_ref, dst_ref=sib, send_sem=s_a, recv_sem=r_a,
                                            device_id=sibling, device_id_type=MESH)
        swap.start()
        swap.wait()
        a, b = x_ref[...], sib[...]
        south = c == 0
        csum[...] = jnp.where(south, a, b) + jnp.where(south, b, a)
        lo = pl.multiple_of(c * h, 8)
        mine = csum.at[pl.ds(lo, h)]
        got[own] = csum[pl.ds(lo, h)]
        sends = []
        for p, (px, py) in enumerate(chips):
            cp = pltpu.make_async_remote_copy(src_ref=mine, dst_ref=got.at[own], send_sem=s_b.at[p], recv_sem=r_b.at[p],
                                              device_id=(px, py, c), device_id_type=MESH)
            cp.start()
            sends.append(cp)
        for cp in sends:
            cp.wait()
        o_ref[pl.ds(lo, h)] = ((got[0] + got[1]) + got[2]) + got[3]
        done = o_ref.at[pl.ds(lo, h)]
        back = pltpu.make_async_remote_copy(src_ref=done, dst_ref=done, send_sem=s_c, recv_sem=r_c,
                                            device_id=sibling, device_id_type=MESH)
        back.start()
        back.wait_send()
        other = o_ref.at[pl.ds(pl.multiple_of((1 - c) * h, 8), h)]
        pltpu.make_async_remote_copy(src_ref=other, dst_ref=other, send_sem=s_c, recv_sem=r_c,
                                     device_id=sibling, device_id_type=MESH).wait_recv()

    vm = pl.BlockSpec(memory_space=pltpu.VMEM)
    return pl.pallas_call(
        body, out_shape=jax.ShapeDtypeStruct(buf.shape, F32), in_specs=[vm], out_specs=vm,
        scratch_shapes=[pltpu.VMEM((R, LANES), F32), pltpu.VMEM((R, LANES), F32), pltpu.VMEM((4, h, LANES), F32),
                        pltpu.SemaphoreType.DMA, pltpu.SemaphoreType.DMA, pltpu.SemaphoreType.DMA((3,)),
                        pltpu.SemaphoreType.DMA((3,)), pltpu.SemaphoreType.DMA, pltpu.SemaphoreType.DMA],
        name=name,
        compiler_params=pltpu.CompilerParams(vmem_limit_bytes=VMEM_CAP_BYTES),
    )(buf)


PACK_TILE_ROWS = 8


def _item_rows(shape):
    n = 1
    for d in shape:
        n *= d
    return -(-n // (PACK_TILE_ROWS * LANES)) * PACK_TILE_ROWS


def _pack(arrs, rows_total):
    buf = jnp.zeros((rows_total, LANES), F32)
    r = 0
    for a in arrs:
        f = a.reshape(-1).astype(F32)
        nr = _item_rows(a.shape)
        block = jnp.pad(f, (0, nr * LANES - f.shape[0])).reshape(nr, LANES)
        buf = lax.dynamic_update_slice(buf, block, (r, 0))
        r += nr
    return buf


def _unpack(buf, shapes):
    out, r = [], 0
    for s in shapes:
        n = 1
        for d in s:
            n *= d
        nr = _item_rows(s)
        out.append(buf[r:r + nr].reshape(-1)[:n].reshape(s))
        r += nr
    return out


def _rows_needed(shapes):
    return -(-sum(_item_rows(s) for s in shapes) // (2 * PACK_TILE_ROWS)) * (2 * PACK_TILE_ROWS)


def _two_rows(a, b):
    out = jnp.zeros((2, a.shape[1]), a.dtype)
    return lax.dynamic_update_slice(lax.dynamic_update_slice(out, a, (0, 0)), b, (1, 0))


def _adam(w, g, m, v):
    m = ADAM_B1 * m + (1.0 - ADAM_B1) * g
    v = ADAM_B2 * v + (1.0 - ADAM_B2) * jnp.square(g)
    m_hat = m / (1.0 - ADAM_B1 ** ADAM_STEP)
    v_hat = v / (1.0 - ADAM_B2 ** ADAM_STEP)
    delta = -ADAM_LR * (m_hat / (jnp.sqrt(v_hat) + ADAM_EPS) + ADAM_WD * w)
    return delta, m, v


def _adam_call(name, w2, g2, m2, v2, tr, pass_grad=False):
    def fn(rv, cv):
        outs = list(_adam(*rv))
        return ([rv[1]] + outs if pass_grad else outs), []

    width = w2.shape[1]
    return _rowcall(name, fn, [(w2, 0, width), (g2, 0, width), (m2, 0, width), (v2, 0, width)], [],
                    [(width, F32)] * (4 if pass_grad else 3), [], tr)


def kernel(x, mem, mem_norm, lb_logits, ffn1_norm, ffn1_w_in, ffn1_w_out, mix_norm, mem_w_kv, hgrn_w_in, hgrn_gnorm, hgrn_w_out, gmlp_w_in, gmlp_ln_g, gmlp_ln_b, gmlp_w_s, gmlp_b_s, gmlp_w_out, ffn2_norm, ffn2_w_in, ffn2_w_out, final_norm, loss_target, m_mem_norm, m_lb_logits, m_ffn1_norm, m_ffn1_w_in, m_ffn1_w_out, m_mix_norm, m_mem_w_kv, m_hgrn_w_in, m_hgrn_gnorm, m_hgrn_w_out, m_gmlp_w_in, m_gmlp_ln_g, m_gmlp_ln_b, m_gmlp_w_s, m_gmlp_b_s, m_gmlp_w_out, m_ffn2_norm, m_ffn2_w_in, m_ffn2_w_out, m_final_norm, v_mem_norm, v_lb_logits, v_ffn1_norm, v_ffn1_w_in, v_ffn1_w_out, v_mix_norm, v_mem_w_kv, v_hgrn_w_in, v_hgrn_gnorm, v_hgrn_w_out, v_gmlp_w_in, v_gmlp_ln_g, v_gmlp_ln_b, v_gmlp_w_s, v_gmlp_b_s, v_gmlp_w_out, v_ffn2_norm, v_ffn2_w_in, v_ffn2_w_out, v_final_norm):
    bl, seq, D = x.shape
    T = bl * seq
    mem_len = mem.shape[1]
    chip = 2 * lax.axis_index("x") + lax.axis_index("y")
    c_arr = lax.axis_index("c").astype(jnp.int32).reshape(1)
    chip_arr = chip.astype(jnp.int32).reshape(1)
    TR = 1024

    big = [("ffn1_w_in", ffn1_w_in, "col"), ("ffn1_w_out", ffn1_w_out, "row"), ("mem_w_kv", mem_w_kv, "col"),
           ("hgrn_w_in", hgrn_w_in, "col"), ("hgrn_w_out", hgrn_w_out, "row"), ("gmlp_w_in", gmlp_w_in, "col"),
           ("gmlp_w_out", gmlp_w_out, "row"), ("ffn2_w_in", ffn2_w_in, "col"), ("ffn2_w_out", ffn2_w_out, "row")]
    kinds = [k for (_, _, k) in big]
    shards_bf = []
    for nm, w, _ in big:
        L, r, c = w.shape
        (wb,) = _rowcall("cast_" + nm, lambda rv, cv: ([rv[0]], []), [(w.reshape(L * r, c), 0, c)], [], [(c, BF)], [], 512)
        shards_bf.append(wb.reshape(L, r, c))
    sb = dict(zip([nm for (nm, _, _) in big], shards_bf))
    groups = [[("ffn1_w_in", 0)], [("ffn1_w_out", 0)], [("hgrn_w_in", None)], [("mem_w_kv", None)], [("hgrn_w_out", None)],
              [("ffn2_w_in", 0)], [("ffn2_w_out", 0), ("gmlp_ln_g", None), ("gmlp_ln_b", None)],
              [("ffn1_w_in", 1), ("ffn1_w_out", 1)],
              [("gmlp_w_in", None), ("gmlp_w_out", None)],
              [("ffn2_w_in", 1), ("ffn2_w_out", 1)]]
    kind_of = {nm: k for (nm, _, k) in big}
    for nm, vec in (("gmlp_ln_g", gmlp_ln_g), ("gmlp_ln_b", gmlp_ln_b)):
        sb[nm] = vec.reshape(1, 1, -1)
        kind_of[nm] = "vec"
    gathered = {nm: [None, None] for nm in ("ffn1_w_in", "ffn1_w_out", "ffn2_w_in", "ffn2_w_out")}
    others = {}
    for gi, grp in enumerate(groups):
        outs = _allgather_seq("gather_%d" % gi, [(sb[nm], kind_of[nm], l) for (nm, l) in grp], gi)
        for (nm, l), o in zip(grp, outs):
            others[(nm, l)] = o

    def whole(nm, l, after):
        full = _place_own("own_%s_%d" % (nm, l or 0), others[(nm, l)], sb[nm], "row" if kind_of[nm] == "row" else "col", l,
                          chip_arr, after)
        if l is None:
            gathered[nm] = full
        else:
            gathered[nm][l] = full
        return full

    def rms_fwd(name, xin, g):
        (h,) = _rowcall(name, lambda rv, cv: ([_rmsnorm(rv[0], cv[0])], []), [(xin, 0, D)], [g.reshape(1, D)], [(D, BF)], [], TR)
        return h

    def ffn_fwd(tag, xin, h, nm_in, nm_out, layer, next_gain):
        w_in = whole(nm_in, layer, h)
        dff = w_in.shape[2] // 2
        zg, zu, a = _ffn_in_swiglu("ffn_in_" + tag, h, w_in, 1024, dff // 2)
        out = _mm("ffn_out_" + tag, a, whole(nm_out, layer, a), "nn", F32, 1024, 1024, dff, scale=0.5, res=xin, b_lead=0,
                  norm_gain=None if next_gain is None else next_gain.reshape(1, D))
        xo, h_next = (out, None) if next_gain is None else out
        return xo, h_next, (xin, h, zg, zu, a)

    def ffn_bwd(tag, dxo, saved, g, w_in, w_out, layer):
        xin, h, zg, zu, a = saved
        dff = w_out[layer].shape[1]
        dw_out = _mm_tn_pair("ffn_dwo_" + tag, a, dxo, "row", c_arr, dff // 2, T, scale=0.5)
        dz = _ffn_da_swiglu("ffn_da_" + tag, dxo, w_out[layer], zg, zu, 512)
        dw_in = _mm_tn_pair("ffn_dwi_" + tag, h, dz, "col", c_arr, 512, T)
        dx, dg = _mm_dh_rms("ffn_dh_" + tag, dz, w_in[layer], xin, g.reshape(1, D), dxo, 512)
        return dx, dg, dw_in, dw_out

    def rms_bwd(name, xin, g, dh, dres):
        def fn(rv, cv):
            _, vjp = jax.vjp(_rmsnorm, rv[0], cv[0])
            dx, dg = vjp(rv[1])
            if dres is not None:
                dx = dx + rv[2]
            return [dx], [dg]

        rows = [(xin, 0, D), (dh, 0, D)] + ([(dres, 0, D)] if dres is not None else [])
        dx, dg = _rowcall(name, fn, rows, [g.reshape(1, D)], [(D, F32)], [((1, D), F32)], TR)
        return dx, dg

    x0 = x.reshape(T, D)
    tgt = loss_target.reshape(T, D)
    mem2 = mem.reshape(bl * mem_len, D)
    memn = rms_fwd("rms_mem", mem2, mem_norm)

    h_f10 = rms_fwd("rms_f1l0", x0, ffn1_norm[0])
    x1, h_m0, sv_f10 = ffn_fwd("f1l0", x0, h_f10, "ffn1_w_in", "ffn1_w_out", 0, mix_norm[0])
    z_m0 = _mm("mix_in_0", h_m0, whole("hgrn_w_in", None, h_m0), "nn", F32, 2048, 512, D, b_lead=0)
    w_kv = whole("mem_w_kv", None, z_m0)
    kv = [_mm("kv_%d" % i, memn, w_kv, "nn", F32, 512, 512, D, b_lead=i) for i in range(2)]
    cat0, stash0 = _hgrn_fwd2(z_m0, lb_logits, hgrn_gnorm, kv[0], bl, seq)
    x2, h_f20 = _mm("mix_out_0", cat0, whole("hgrn_w_out", None, cat0), "nn", F32, 1024, 1024, cat0.shape[1], res=x1, b_lead=0,
                    norm_gain=ffn2_norm[0].reshape(1, D))
    x3, h_f11, sv_f20 = ffn_fwd("f2l0", x2, h_f20, "ffn2_w_in", "ffn2_w_out", 0, ffn1_norm[1])
    x4, h_m1, sv_f11 = ffn_fwd("f1l1", x3, h_f11, "ffn1_w_in", "ffn1_w_out", 1, mix_norm[1])
    z_m1 = _mm("mix_in_1", h_m1, whole("gmlp_w_in", None, h_m1), "nn", F32, 2048, 512, D, b_lead=0)
    nc1 = seq // GM_CHUNK
    w_s, b_s = gmlp_w_s[0], gmlp_b_s[0]
    ln_w = GM_GROUPS * GM_GROUP_DIM
    ln_g_full, ln_b_full = [whole(nm, None, z_m1).reshape(1, ln_w) for nm in ("gmlp_ln_g", "gmlp_ln_b")]
    cat1 = _gmlp_fwd(z_m1, ln_g_full, ln_b_full, w_s, b_s, kv[1], bl, nc1)
    x5, h_f21 = _mm("mix_out_1", cat1, whole("gmlp_w_out", None, cat1), "nn", F32, 1024, 1024, cat1.shape[1], res=x4, b_lead=0,
                    norm_gain=ffn2_norm[1].reshape(1, D))
    x6, _, sv_f21 = ffn_fwd("f2l1", x5, h_f21, "ffn2_w_in", "ffn2_w_out", 1, None)

    def head(rv, cv):
        def f(xx, gg):
            err = _rmsnorm(xx, gg) - rv[1]
            return 0.5 * jnp.sum(jnp.mean(err * err, axis=-1, keepdims=True), axis=0, keepdims=True)

        ls, vjp = jax.vjp(f, rv[0], cv[0])
        dx, dg = vjp(jnp.ones((1, 1), F32))
        return [dx], [dg, jnp.broadcast_to(ls, (1, 128))]

    dx6, d_final, loss_part = _rowcall("loss_head", head, [(x6, 0, D), (tgt, 0, D)], [final_norm.reshape(1, D)],
                                       [(D, F32)], [((1, D), F32), ((1, 128), F32)], TR)

    rs_out = {}
    n_gather = len(groups)

    def rs(gi, items):
        outs = _rs_chips_seq("reduce_%d" % gi, [p for (_, p, _) in items], [k for (_, _, k) in items], n_gather + gi)
        for i, (key, _, _) in enumerate(items):
            rs_out[key] = (outs[2 * i], outs[2 * i + 1])

    dx5, dg_f21, dwi_f21, dwo_f21 = ffn_bwd("f2l1", dx6, sv_f21, ffn2_norm[1], gathered["ffn2_w_in"], gathered["ffn2_w_out"], 1)
    rs(0, [(("ffn2_w_out", 1), dwo_f21, "row"), (("ffn2_w_in", 1), dwi_f21, "col")])
    dcat1 = _mm("mix_dcat_1", dx5, gathered["gmlp_w_out"], "nt", F32, 2048, 1024, D, b_lead=0)
    dwo_m1 = _mm_tn_pair("mix_dwo_1", cat1, dx5, "row", c_arr, 1024, T)
    dz_m1, dkv1, d_lng, d_lnb, d_ws, d_bs = _gmlp_bwd(z_m1, dcat1, ln_g_full, ln_b_full, w_s, b_s, kv[1], bl, nc1)
    dx4, dg_m1 = _mm_dh_rms("mix_dh_1", dz_m1, gathered["gmlp_w_in"], x4, mix_norm[1].reshape(1, D), dx5, 512)
    dwi_m1 = _mm_tn_pair("mix_dwi_1", h_m1, dz_m1, "col", c_arr, 1024, T)
    rs(1, [(("gmlp_w_out", 0), dwo_m1, "row"), (("gmlp_w_in", 0), dwi_m1, "col")])
    dx3, dg_f11, dwi_f11, dwo_f11 = ffn_bwd("f1l1", dx4, sv_f11, ffn1_norm[1], gathered["ffn1_w_in"], gathered["ffn1_w_out"], 1)
    rs(2, [(("ffn1_w_out", 1), dwo_f11, "row"), (("ffn1_w_in", 1), dwi_f11, "col")])

    dx2, dg_f20, dwi_f20, dwo_f20 = ffn_bwd("f2l0", dx3, sv_f20, ffn2_norm[0], gathered["ffn2_w_in"], gathered["ffn2_w_out"], 0)
    rs(3, [(("ffn2_w_out", 0), dwo_f20, "row"), (("ffn2_w_in", 0), dwi_f20, "col")])
    dcat0 = _mm("mix_dcat_0", dx2, gathered["hgrn_w_out"], "nt", F32, 2048, 1024, D, b_lead=0)
    dwo_m0 = _mm_tn_pair("mix_dwo_0", cat0, dx2, "row", c_arr, 1024, T)
    dz_m0, dkv0, d_lb, d_gn = _hgrn_bwd2(z_m0, dcat0, stash0, lb_logits, hgrn_gnorm, kv[0], bl, seq)
    dx1, dg_m0 = _mm_dh_rms("mix_dh_0", dz_m0, gathered["hgrn_w_in"], x1, mix_norm[0].reshape(1, D), dx2, 512)
    dwi_m0 = _mm_tn_pair("mix_dwi_0", h_m0, dz_m0, "col", c_arr, 1024, T)
    rs(4, [(("hgrn_w_out", 0), dwo_m0, "row"), (("hgrn_w_in", 0), dwi_m0, "col")])

    dwkv = [_mm_tn_pair("kv_dw_%d" % i, memn, dkv, "col", c_arr, 1024, 512) for i, dkv in enumerate([dkv0, dkv1])]
    rs(5, [(("mem_w_kv", 0), dwkv[0], "col"), (("mem_w_kv", 1), dwkv[1], "col")])
    dmemn = _mm("kv_dx_0", dkv0, gathered["mem_w_kv"], "nt", F32, 512, 512, 1024, b_lead=0)
    dmemn = _mm("kv_dx_1", dkv1, gathered["mem_w_kv"], "nt", F32, 512, 512, 1024, res=dmemn, b_lead=1)
    _, d_memnorm = rms_bwd("rms_bwd_mem", mem2, mem_norm, dmemn, None)

    dx0, dg_f10, dwi_f10, dwo_f10 = ffn_bwd("f1l0", dx1, sv_f10, ffn1_norm[0], gathered["ffn1_w_in"], gathered["ffn1_w_out"], 0)
    rs(6, [(("ffn1_w_out", 0), dwo_f10, "row")])
    rs(7, [(("ffn1_w_in", 0), dwi_f10, "col")])

    shard_grads = [_finish_share("finish_" + nm, [rs_out[(nm, l)][0] for l in range(w.shape[0])],
                                 [rs_out[(nm, l)][1] for l in range(w.shape[0])], k, c_arr) for (nm, w, k) in big]

    big_w = [w for (_, w, _) in big]
    big_m = [m_ffn1_w_in, m_ffn1_w_out, m_mem_w_kv, m_hgrn_w_in, m_hgrn_w_out, m_gmlp_w_in, m_gmlp_w_out, m_ffn2_w_in, m_ffn2_w_out]
    big_v = [v_ffn1_w_in, v_ffn1_w_out, v_mem_w_kv, v_hgrn_w_in, v_hgrn_w_out, v_gmlp_w_in, v_gmlp_w_out, v_ffn2_w_in, v_ffn2_w_out]
    big_out = {}
    for (nm, w, _), g, m, v in zip(big, shard_grads, big_m, big_v):
        L, r, c = w.shape
        g2, d2, m2, v2 = _adam_call("adam_" + nm, w.reshape(L * r, c), g.reshape(L * r, c), m.reshape(L * r, c),
                                    v.reshape(L * r, c), 256, pass_grad=True)
        big_out[nm] = (g2.reshape(w.shape), d2.reshape(w.shape), m2.reshape(w.shape), v2.reshape(w.shape))

    d_ffn1n = _two_rows(dg_f10, dg_f11)
    d_mixn = _two_rows(dg_m0, dg_m1)
    d_ffn2n = _two_rows(dg_f20, dg_f21)
    small_parts = [loss_part[:, :1], d_memnorm, d_lb, d_ffn1n, d_mixn, d_gn, d_lng, d_lnb, d_ws, d_bs, d_ffn2n, d_final]
    red_shapes = [(1,), mem_norm.shape, lb_logits.shape, ffn1_norm.shape, mix_norm.shape, hgrn_gnorm.shape, (1, ln_w), (1, ln_w),
                  gmlp_w_s.shape, gmlp_b_s.shape, ffn2_norm.shape, final_norm.shape]
    red = _small_allreduce(_pack(small_parts, _rows_needed(red_shapes)), "reduce_small")
    (loss_v, g_memn, g_lb, g_f1n, g_mixn, g_gn, g_lng_full, g_lnb_full, g_ws, g_bs, g_f2n, g_fin) = _unpack(red, red_shapes)
    lsh = gmlp_ln_g.shape[1]
    g_lng = lax.dynamic_slice(g_lng_full, (0, chip * lsh), (1, lsh))
    g_lnb = lax.dynamic_slice(g_lnb_full, (0, chip * lsh), (1, lsh))
    small_w = [mem_norm, lb_logits, ffn1_norm, mix_norm, hgrn_gnorm, gmlp_ln_g, gmlp_ln_b, gmlp_w_s, gmlp_b_s, ffn2_norm, final_norm]
    small_g = [g_memn, g_lb, g_f1n, g_mixn, g_gn, g_lng, g_lnb, g_ws, g_bs, g_f2n, g_fin]
    small_m = [m_mem_norm, m_lb_logits, m_ffn1_norm, m_mix_norm, m_hgrn_gnorm, m_gmlp_ln_g, m_gmlp_ln_b, m_gmlp_w_s, m_gmlp_b_s, m_ffn2_norm, m_final_norm]
    small_v = [v_mem_norm, v_lb_logits, v_ffn1_norm, v_mix_norm, v_hgrn_gnorm, v_gmlp_ln_g, v_gmlp_ln_b, v_gmlp_w_s, v_gmlp_b_s, v_ffn2_norm, v_final_norm]
    sshapes = [w.shape for w in small_w]
    nrow = _rows_needed(sshapes)
    d_p, m_p, v_p = _adam_call("adam_small", _pack(small_w, nrow), _pack(small_g, nrow), _pack(small_m, nrow), _pack(small_v, nrow), nrow)
    s_delta, s_m, s_v = _unpack(d_p, sshapes), _unpack(m_p, sshapes), _unpack(v_p, sshapes)
    small_names = ["mem_norm", "lb_logits", "ffn1_norm", "mix_norm", "hgrn_gnorm", "gmlp_ln_g", "gmlp_ln_b", "gmlp_w_s", "gmlp_b_s", "ffn2_norm", "final_norm"]
    small_out = {nm: (g.reshape(w.shape), d, m, v) for nm, w, g, d, m, v in zip(small_names, small_w, small_g, s_delta, s_m, s_v)}

    order = ["mem_norm", "lb_logits", "ffn1_norm", "ffn1_w_in", "ffn1_w_out", "mix_norm", "mem_w_kv", "hgrn_w_in", "hgrn_gnorm",
             "hgrn_w_out", "gmlp_w_in", "gmlp_ln_g", "gmlp_ln_b", "gmlp_w_s", "gmlp_b_s", "gmlp_w_out", "ffn2_norm", "ffn2_w_in",
             "ffn2_w_out", "final_norm"]
    allo = {**big_out, **small_out}
    grad_x = dx0.reshape(x.shape)
    return (loss_v.reshape(()), grad_x, *[allo[n][0] for n in order], *[allo[n][1] for n in order],
            *[allo[n][2] for n in order], *[allo[n][3] for n in order])
```

```python
import functools

import jax
import jax.numpy as jnp
from jax import lax
from jax.experimental import pallas as pl
from jax.experimental.pallas import tpu as pltpu
from jax.experimental.pallas import tpu_sc as plsc

BF = jnp.bfloat16
F32 = jnp.float32
MESH = pl.DeviceIdType.MESH

EPS = 1e-6
D_MODEL = 1024
HG_HEADS = 8
HG_DIM = 128
HG_CHUNK = 64
GM_CHUNK = 128
GM_GROUPS = 8
GM_GROUP_DIM = 256
XA_HEADS = 4
XA_DIM = 256
ADAM_LR = 0.001
ADAM_B1 = 0.9
ADAM_B2 = 0.999
ADAM_EPS = 1e-08
ADAM_WD = 0.01
ADAM_STEP = 10

VMEM_CAP_BYTES = 60 * 1024 * 1024
LANES = 1024


def _pick(n, cap, mult=16):
    if n <= cap:
        return n
    for d in range(cap - cap % mult, 0, -mult):
        if n % d == 0:
            return d
    raise ValueError((n, cap, mult))


def _dg(a, b, ca, cb):
    return lax.dot_general(a.astype(BF), b.astype(BF), (((ca,), (cb,)), ((), ())), preferred_element_type=F32)


@jax.custom_vjp
def dot_nn(a, b):
    return _dg(a, b, 1, 0)


def _nn_fwd(a, b):
    return _dg(a, b, 1, 0), (a, b)


def _nn_bwd(r, g):
    a, b = r
    return _dg(g, b, 1, 1), _dg(a, g, 0, 0)


dot_nn.defvjp(_nn_fwd, _nn_bwd)


@jax.custom_vjp
def dot_nt(a, b):
    return _dg(a, b, 1, 1)


def _nt_fwd(a, b):
    return _dg(a, b, 1, 1), (a, b)


def _nt_bwd(r, g):
    a, b = r
    return _dg(g, b, 1, 0), _dg(g, a, 0, 0)


dot_nt.defvjp(_nt_fwd, _nt_bwd)


@jax.custom_vjp
def dot_tn(a, b):
    return _dg(a, b, 0, 0)


def _tn_fwd(a, b):
    return _dg(a, b, 0, 0), (a, b)


def _tn_bwd(r, g):
    a, b = r
    return _dg(b, g, 1, 1), _dg(a, g, 1, 0)


dot_tn.defvjp(_tn_fwd, _tn_bwd)


def _rmsnorm(x, g):
    return x * lax.rsqrt(jnp.mean(x * x, axis=-1, keepdims=True) + EPS) * g


def _silu(x):
    return x * jax.nn.sigmoid(x)


@jax.custom_vjp
def _gelu(x):
    return 0.5 * x * (1.0 + lax.erf(x * (0.5 ** 0.5)))


def _gelu_fwd(x):
    return _gelu(x), x


def _gelu_bwd(x, g):
    t = x * (0.5 ** 0.5)
    cdf = 0.5 * (1.0 + lax.erf(t))
    return (g * (cdf + x * (jnp.exp(-(t * t)) * (0.5 / 3.141592653589793) ** 0.5)),)


_gelu.defvjp(_gelu_fwd, _gelu_bwd)


def _softmax_last(s):
    m = lax.stop_gradient(jnp.max(s, axis=-1, keepdims=True))
    e = jnp.exp(s - m)
    return e / jnp.sum(e, axis=-1, keepdims=True)


def _tril(n):
    r = lax.broadcasted_iota(jnp.int32, (n, n), 0)
    c = lax.broadcasted_iota(jnp.int32, (n, n), 1)
    return r >= c


def _attention(zx, mk, mv):
    s = dot_nt(zx, mk) * (XA_DIM ** -0.5)
    return dot_nn(_softmax_last(s), mv)


def _chunk_sums(x, suffix):
    n = x.shape[0]
    r = lax.broadcasted_iota(jnp.int32, (n, n), 0)
    c = lax.broadcasted_iota(jnp.int32, (n, n), 1)
    tri = jnp.logical_and(r <= c if suffix else r >= c, r // HG_CHUNK == c // HG_CHUNK).astype(BF)
    hi = x.astype(BF)
    rest = x - hi.astype(F32)
    mid = rest.astype(BF)
    lo = (rest - mid.astype(F32)).astype(BF)
    return (_dg(tri, hi, 1, 0) + _dg(tri, mid, 1, 0)) + _dg(tri, lo, 1, 0)


@jax.custom_vjp
def _running_sums(x):
    return _chunk_sums(x, False)


_running_sums.defvjp(lambda x: (_chunk_sums(x, False), None), lambda _, g: (_chunk_sums(g, True),))


def _hgrn_decays(zf, lb3):
    l0, l1, l2 = lb3[0:1], lb3[1:2], lb3[2:3]
    m = lax.stop_gradient(jnp.maximum(jnp.maximum(l0, l1), l2))
    e0 = jnp.exp(l0 - m)
    lb = e0 / (e0 + jnp.exp(l1 - m) + jnp.exp(l2 - m))
    f = lb + (1.0 - lb) * jax.nn.sigmoid(zf)
    return f, _running_sums(jnp.log(f))


def _hgrn_head(zq, f, b, zi, zg, gn, S):
    q = _silu(zq)
    k = 1.0 - f
    b_last = b[HG_CHUNK - 1:HG_CHUNK, :]
    q_dec = q * jnp.exp(b)
    k_inv = k * jnp.exp(-b)
    a = jnp.where(_tril(HG_CHUNK), dot_nt(q_dec, k_inv), 0.0)
    o = dot_nn(a, zi) + dot_nn(q_dec, S)
    S_new = jnp.exp(b_last).reshape(HG_DIM, 1) * S + dot_tn(k * jnp.exp(b_last - b), zi)
    o = _rmsnorm(o, gn) * _silu(zg)
    return o, S_new


def _gmlp_block(zu, zv, zx, lng, lnb, ws, bs, mk, mv):
    gv = [_gelu(v) for v in zv]
    width = GM_GROUPS * GM_GROUP_DIM
    mu = sum(jnp.sum(g, axis=-1, keepdims=True) for g in gv) / width
    xc = [g - mu for g in gv]
    var = sum(jnp.sum(c * c, axis=-1, keepdims=True) for c in xc) / width
    r = lax.rsqrt(var + EPS)
    outs = []
    for g in range(GM_GROUPS):
        v = xc[g] * r * lng[g] + lnb[g]
        w = jnp.where(_tril(GM_CHUNK), ws[g], 0.0)
        mixed = dot_nn(w, v) + bs[g].reshape(GM_CHUNK, 1)
        outs.append(_gelu(zu[g]) * mixed)
    for a in range(XA_HEADS):
        outs.append(_attention(zx[a], mk[a], mv[a]))
    return outs


def _rowcall(name, fn, rows, consts, row_outs, acc_outs, tr):
    nrows = rows[0][0].shape[0]
    tr = _pick(nrows, tr)
    n_r, n_c, n_ro, n_ao = len(rows), len(consts), len(row_outs), len(acc_outs)

    def kern(*refs):
        rv = [r[...] for r in refs[:n_r]]
        cv = [r[...] for r in refs[n_r:n_r + n_c]]
        ro_refs = refs[n_r + n_c:n_r + n_c + n_ro]
        ao_refs = refs[n_r + n_c + n_ro:]
        ro, ao = fn(rv, cv)
        for ref, v in zip(ro_refs, ro):
            ref[...] = v.astype(ref.dtype)
        if n_ao:
            @pl.when(pl.program_id(0) == 0)
            def _():
                for ref in ao_refs:
                    ref[...] = jnp.zeros(ref.shape, ref.dtype)

            for ref, v in zip(ao_refs, ao):
                ref[...] += v.astype(ref.dtype)

    in_specs = [pl.BlockSpec((tr, w), functools.partial(lambda i, cb: (i, cb), cb=cb)) for (_, cb, w) in rows]
    in_specs += [pl.BlockSpec(c.shape, lambda i: (0, 0)) for c in consts]
    out_specs = [pl.BlockSpec((tr, w), lambda i: (i, 0)) for (w, _) in row_outs]
    out_specs += [pl.BlockSpec(s, lambda i: (0, 0)) for (s, _) in acc_outs]
    out_shape = [jax.ShapeDtypeStruct((nrows, w), dt) for (w, dt) in row_outs]
    out_shape += [jax.ShapeDtypeStruct(s, dt) for (s, dt) in acc_outs]
    outs = pl.pallas_call(
        kern, grid=(nrows // tr,), in_specs=in_specs, out_specs=out_specs, out_shape=out_shape, name=name,
        compiler_params=pltpu.CompilerParams(dimension_semantics=("arbitrary",),
                                             vmem_limit_bytes=VMEM_CAP_BYTES),
    )(*[a for (a, _, _) in rows], *consts)
    return outs


def _mm(name, a, b, mode, out_dtype, tm, tn, tk, scale=1.0, res=None, a_lead=None, b_lead=None, norm_gain=None):
    ash = a.shape[-2:]
    bsh = b.shape[-2:]
    if mode == "nn":
        (M, K), (K2, N) = ash, bsh
    elif mode == "nt":
        (M, K), (N, K2) = ash, bsh
    else:
        (K, M), (K2, N) = ash, bsh
    assert K == K2, (name, a.shape, b.shape)
    tm, tn, tk = min(tm, M), min(tn, N), min(tk, K)
    assert M % tm == 0 and N % tn == 0 and K % tk == 0, (name, M, N, K, tm, tn, tk)
    nk = K // tk
    dims = {"nn": (1, 0), "nt": (1, 1), "tn": (0, 0)}[mode]

    def lead(spec_shape, index_fn, lead_idx):
        if lead_idx is None:
            return pl.BlockSpec(spec_shape, index_fn)
        return pl.BlockSpec((None,) + spec_shape, lambda i, j, k: (lead_idx,) + index_fn(i, j, k))

    if mode == "tn":
        a_spec = lead((tk, tm), lambda i, j, k: (k, i), a_lead)
    else:
        a_spec = lead((tm, tk), lambda i, j, k: (i, k), a_lead)
    if mode == "nt":
        b_spec = lead((tn, tk), lambda i, j, k: (j, k), b_lead)
    else:
        b_spec = lead((tk, tn), lambda i, j, k: (k, j), b_lead)
    o_spec = pl.BlockSpec((tm, tn), lambda i, j, k: (i, j))
    has_res = res is not None
    has_norm = norm_gain is not None
    assert not has_norm or tn == N

    def kern(*refs):
        a_ref, b_ref = refs[0], refs[1]
        pos = 2
        res_ref = gain_ref = h_ref = None
        if has_res:
            res_ref, pos = refs[pos], pos + 1
        if has_norm:
            gain_ref, pos = refs[pos], pos + 1
        o_ref, pos = refs[pos], pos + 1
        if has_norm:
            h_ref = refs[pos]
        acc_ref = refs[-1] if nk > 1 else None
        p = lax.dot_general(a_ref[...].astype(BF), b_ref[...].astype(BF), (((dims[0],), (dims[1],)), ((), ())),
                            preferred_element_type=F32)

        def finish(v):
            if scale != 1.0:
                v = v * scale
            if has_res:
                v = res_ref[...] + v
            o_ref[...] = v.astype(o_ref.dtype)
            if has_norm:
                h_ref[...] = _rmsnorm(v, gain_ref[...]).astype(h_ref.dtype)

        if nk == 1:
            finish(p)
        else:
            k = pl.program_id(2)

            @pl.when(k == 0)
            def _():
                acc_ref[...] = p

            @pl.when(k > 0)
            def _():
                acc_ref[...] += p

            @pl.when(k == nk - 1)
            def _():
                finish(acc_ref[...])

    ins = [a, b] + ([res] if has_res else []) + ([norm_gain] if has_norm else [])
    in_specs = [a_spec, b_spec] + ([o_spec] if has_res else [])
    in_specs += [pl.BlockSpec((1, N), lambda i, j, k: (0, 0))] if has_norm else []
    out_sd = jax.ShapeDtypeStruct((M, N), out_dtype)
    return pl.pallas_call(
        kern, grid=(M // tm, N // tn, nk), in_specs=in_specs,
        out_specs=[o_spec, o_spec] if has_norm else o_spec,
        out_shape=[out_sd, jax.ShapeDtypeStruct((M, N), BF)] if has_norm else out_sd,
        scratch_shapes=[pltpu.VMEM((tm, tn), F32)] if nk > 1 else [],
        name=name,
        compiler_params=pltpu.CompilerParams(dimension_semantics=("parallel", "parallel", "arbitrary"),
                                             vmem_limit_bytes=VMEM_CAP_BYTES),
    )(*ins)


def _ffn_in_swiglu(name, h, w3, tm, tn):
    T, D = h.shape
    dff = w3.shape[2] // 2
    tm = min(tm, T)
    assert T % tm == 0 and dff % tn == 0
    nj = dff // tn

    def kern(h_ref, wg_ref, wu_ref, zg_ref, zu_ref, a_ref):
        hb = h_ref[...]
        g = jnp.dot(hb, wg_ref[...], preferred_element_type=F32).astype(BF)
        u = jnp.dot(hb, wu_ref[...], preferred_element_type=F32).astype(BF)
        zg_ref[...] = g
        zu_ref[...] = u
        a_ref[...] = (_silu(g.astype(F32)) * u.astype(F32)).astype(BF)

    o_spec = pl.BlockSpec((tm, tn), lambda i, j: (i, j))
    return pl.pallas_call(
        kern, grid=(T // tm, nj),
        in_specs=[pl.BlockSpec((tm, D), lambda i, j: (i, 0)),
                  pl.BlockSpec((None, D, tn), lambda i, j: (0, 0, j)),
                  pl.BlockSpec((None, D, tn), lambda i, j: (0, 0, j + nj))],
        out_specs=[o_spec, o_spec, o_spec],
        out_shape=[jax.ShapeDtypeStruct((T, dff), BF)] * 3, name=name,
        compiler_params=pltpu.CompilerParams(dimension_semantics=("parallel", "arbitrary"),
                                             vmem_limit_bytes=VMEM_CAP_BYTES),
    )(h, w3, w3)


def _ffn_da_swiglu(name, dxo, w3, zg, zu, tm):
    T, D = dxo.shape
    dff = w3.shape[1]
    tm = min(tm, T)
    assert T % tm == 0 and dff % 2 == 0
    hc = dff // 2

    nsteps = T // tm

    def kern(d_ref, w_ref, g_hbm, u_hbm, dz_ref, g_ring, u_ring, g_sem, u_sem):
        i = pl.program_id(0)

        def copies(step, slot):
            rows = pl.ds(pl.multiple_of(step * tm, tm), tm)
            return (pltpu.make_async_copy(g_hbm.at[rows, :], g_ring.at[slot], g_sem.at[slot]),
                    pltpu.make_async_copy(u_hbm.at[rows, :], u_ring.at[slot], u_sem.at[slot]))

        @pl.when(i == 0)
        def _():
            for first in range(min(2, nsteps)):
                for cp in copies(first, first):
                    cp.start()

        @pl.when(i + 2 < nsteps)
        def _():
            for cp in copies(i + 2, (i + 2) % 3):
                cp.start()

        slot = i % 3
        for cp in copies(i, slot):
            cp.wait()
        g_ref, u_ref = g_ring.at[slot], u_ring.at[slot]
        db = (d_ref[...] * 0.5).astype(BF)
        for s in range(2):
            cols = slice(s * hc, (s + 1) * hc)
            da = lax.dot_general(db, w_ref[cols, :], (((1,), (1,)), ((), ())), preferred_element_type=F32)
            g = g_ref[:, cols].astype(F32)
            sg = 1.0 / (1.0 + jnp.exp(-g))
            gs = g * sg
            dab = da.astype(BF)
            dz_ref[:, cols] = (dab * u_ref[:, cols]) * (sg + gs * (1.0 - sg)).astype(BF)
            dz_ref[:, dff + s * hc:dff + (s + 1) * hc] = dab * gs.astype(BF)

    row = lambda w: pl.BlockSpec((tm, w), lambda i: (i, 0))
    whole = pl.BlockSpec(memory_space=pl.ANY)
    return pl.pallas_call(
        kern, grid=(nsteps,),
        in_specs=[row(D), pl.BlockSpec((None, dff, D), lambda i: (0, 0, 0), pipeline_mode=pl.Buffered(1)), whole, whole],
        out_specs=row(2 * dff), out_shape=jax.ShapeDtypeStruct((T, 2 * dff), BF), name=name,
        scratch_shapes=[pltpu.VMEM((3, tm, dff), BF), pltpu.VMEM((3, tm, dff), BF),
                        pltpu.SemaphoreType.DMA((3,)), pltpu.SemaphoreType.DMA((3,))],
        compiler_params=pltpu.CompilerParams(dimension_semantics=("arbitrary",), vmem_limit_bytes=VMEM_CAP_BYTES),
    )(dxo, w3, zg, zu)


def _mm_dh_rms(name, dz, w3, xin, g, dres, tm):
    T, K = dz.shape
    D = w3.shape[1]
    tm = min(tm, T)
    assert T % tm == 0

    def kern(dz_ref, w_ref, x_ref, g_ref, r_ref, dx_ref, dg_ref):
        dh = lax.dot_general(dz_ref[...], w_ref[...], (((1,), (1,)), ((), ())), preferred_element_type=F32)
        _, vjp = jax.vjp(_rmsnorm, x_ref[...], g_ref[...])
        dx, dg = vjp(dh)
        dx_ref[...] = dx + r_ref[...]

        @pl.when(pl.program_id(0) == 0)
        def _():
            dg_ref[...] = jnp.zeros(dg_ref.shape, F32)

        dg_ref[...] += dg

    row = lambda w: pl.BlockSpec((tm, w), lambda i: (i, 0))
    one = pl.BlockSpec((1, D), lambda i: (0, 0))
    return pl.pallas_call(
        kern, grid=(T // tm,),
        in_specs=[row(K), pl.BlockSpec((None, D, K), lambda i: (0, 0, 0), pipeline_mode=pl.Buffered(1)), row(D), one, row(D)],
        out_specs=[row(D), one], out_shape=[jax.ShapeDtypeStruct((T, D), F32), jax.ShapeDtypeStruct((1, D), F32)], name=name,
        compiler_params=pltpu.CompilerParams(dimension_semantics=("arbitrary",), vmem_limit_bytes=VMEM_CAP_BYTES),
    )(dz, w3, xin, g, dres)


def _mm_tn_pair(name, a, b, kind, c_arr, tq, tk, scale=1.0):
    T, M = a.shape
    _, N = b.shape
    tk = min(tk, T)
    assert T % tk == 0
    nk = T // tk
    if kind == "col":
        hm = M // 2
        assert N % tq == 0
        nq = N // tq
        tile = (hm, tq)
        a_spec = pl.BlockSpec((tk, hm), lambda h, q, k, c: (k, jnp.bitwise_xor(h, 1 - c[0])))
        b_spec = pl.BlockSpec((tk, tq), lambda h, q, k, c: (k, q))
        o_spec = pl.BlockSpec(tile, lambda h, q, k, c: (0, q * h))
        out_sd = (hm, N)
    else:
        hn = N // 2
        assert M % tq == 0
        nq = M // tq
        tile = (tq, hn)
        a_spec = pl.BlockSpec((tk, tq), lambda h, q, k, c: (k, q))
        b_spec = pl.BlockSpec((tk, hn), lambda h, q, k, c: (k, jnp.bitwise_xor(h, 1 - c[0])))
        o_spec = pl.BlockSpec(tile, lambda h, q, k, c: (q * h, 0))
        out_sd = (M, hn)

    def kern(c_ref, a_ref, b_ref, o_ref, acc, stage, recv, ssem, rsem):
        h, q, k = pl.program_id(0), pl.program_id(1), pl.program_id(2)
        x, y, c, _ = _place()
        p = lax.dot_general(a_ref[...].astype(BF), b_ref[...].astype(BF), (((0,), (0,)), ((), ())), preferred_element_type=F32)

        @pl.when(k == 0)
        def _():
            acc[...] = p

        @pl.when(k > 0)
        def _():
            acc[...] += p

        def send(slot, qq):
            return pltpu.make_async_remote_copy(src_ref=stage.at[slot], dst_ref=recv.at[qq], send_sem=ssem.at[slot],
                                                recv_sem=rsem.at[qq], device_id=(x, y, 1 - c), device_id_type=MESH)

        last = k == nk - 1

        @pl.when(jnp.logical_and(last, h == 0))
        def _():
            slot = q % 2

            @pl.when(q >= 2)
            def _():
                send(slot, q).wait_send()

            stage[slot] = (acc[...] * scale).astype(BF)
            send(slot, q).start()

        @pl.when(jnp.logical_and(last, h == 1))
        def _():
            @pl.when(q == 0)
            def _():
                for s in range(min(nq, 2)):
                    send(s, 0).wait_send()

            send(0, q).wait_recv()
            o_ref[...] = (acc[...] * scale + recv[q].astype(F32)).astype(o_ref.dtype)

    return pl.pallas_call(
        kern,
        grid_spec=pltpu.PrefetchScalarGridSpec(
            num_scalar_prefetch=1, grid=(2, nq, nk), in_specs=[a_spec, b_spec], out_specs=o_spec,
            scratch_shapes=[pltpu.VMEM(tile, F32), pltpu.VMEM((2,) + tile, BF), pltpu.VMEM((nq,) + tile, BF),
                            pltpu.SemaphoreType.DMA((2,)), pltpu.SemaphoreType.DMA((nq,))]),
        out_shape=jax.ShapeDtypeStruct(out_sd, BF), name=name,
        compiler_params=pltpu.CompilerParams(dimension_semantics=("arbitrary", "arbitrary", "arbitrary"),
                                             vmem_limit_bytes=VMEM_CAP_BYTES),
    )(c_arr, a, b)


def _kv_pieces(kv_ref):
    W = XA_HEADS * XA_DIM
    mk = [kv_ref[:, a * XA_DIM:(a + 1) * XA_DIM] for a in range(XA_HEADS)]
    mv = [kv_ref[:, W + a * XA_DIM:W + (a + 1) * XA_DIM] for a in range(XA_HEADS)]
    return mk, mv


HG_SUB = 4


def _hgrn_rows(z_ref):
    W = HG_HEADS * HG_DIM

    def piece(c, col, w):
        return z_ref[c * HG_CHUNK:(c + 1) * HG_CHUNK, col:col + w]

    zq = [[piece(c, h * HG_DIM, HG_DIM) for h in range(HG_HEADS)] for c in range(HG_SUB)]
    zf = z_ref[:, W:2 * W]
    zi =[[piece(c, 2 * W + h * HG_DIM, HG_DIM) for h in range(HG_HEADS)] for c in range(HG_SUB)]
    zg = [[piece(c, 3 * W + h * HG_DIM, HG_DIM) for h in range(HG_HEADS)] for c in range(HG_SUB)]
    zx = [z_ref[:, 4 * W + a * XA_DIM:4 * W + (a + 1) * XA_DIM] for a in range(XA_HEADS)]
    return zq, zf, zi, zg, zx


def _hgrn_steps(zq, zf, zi, zg, zx, lb3, gn, mk, mv, S):
    f, b = _hgrn_decays(zf, lb3)
    mix = []
    for c in range(HG_SUB):
        row, s_next = [], []
        rows = slice(c * HG_CHUNK, (c + 1) * HG_CHUNK)
        for h in range(HG_HEADS):
            cols = slice(h * HG_DIM, (h + 1) * HG_DIM)
            o, sn = _hgrn_head(zq[c][h], f[rows, cols], b[rows, cols], zi[c][h], zg[c][h], gn, S[h])
            row.append(o)
            s_next.append(sn)
        mix.append(row)
        S = s_next
    att = [_attention(zx[a], mk[a], mv[a]) for a in range(XA_HEADS)]
    return mix, att, S


def _hgrn_fwd2(z, lb_logits, gnorm, kv, bl, seq):
    T, zw = z.shape
    mem_len = kv.shape[0] // bl
    cat_w = HG_HEADS * HG_DIM + XA_HEADS * XA_DIM
    R = HG_SUB * HG_CHUNK
    nb = seq // R

    def kern(z_ref, lb_ref, gn_ref, kv_ref, cat_ref, st_ref, s_scr):
        @pl.when(pl.program_id(1) == 0)
        def _():
            s_scr[...] = jnp.zeros(s_scr.shape, F32)

        st_ref[...] = s_scr[...]
        zq, zf, zi, zg, zx = _hgrn_rows(z_ref)
        mk, mv = _kv_pieces(kv_ref)
        S = [s_scr[h] for h in range(HG_HEADS)]
        mix, att, s_new = _hgrn_steps(zq, zf, zi, zg, zx, lb_ref[...], gn_ref[...], mk, mv, S)
        for c in range(HG_SUB):
            for h in range(HG_HEADS):
                cat_ref[c * HG_CHUNK:(c + 1) * HG_CHUNK, h * HG_DIM:(h + 1) * HG_DIM] = mix[c][h].astype(cat_ref.dtype)
        for h in range(HG_HEADS):
            s_scr[h] = s_new[h]
        base = HG_HEADS * HG_DIM
        for a in range(XA_HEADS):
            cat_ref[:, base + a * XA_DIM:base + (a + 1) * XA_DIM] = att[a].astype(cat_ref.dtype)

    return pl.pallas_call(
        kern, grid=(bl, nb),
        in_specs=[pl.BlockSpec((R, zw), lambda b, n: (b * nb + n, 0)),
                  pl.BlockSpec(lb_logits.shape, lambda b, n: (0, 0)),
                  pl.BlockSpec(gnorm.shape, lambda b, n: (0, 0)),
                  pl.BlockSpec((mem_len, kv.shape[1]), lambda b, n: (b, 0))],
        out_specs=[pl.BlockSpec((R, cat_w), lambda b, n: (b * nb + n, 0)),
                   pl.BlockSpec((None, HG_HEADS, HG_DIM, HG_DIM), lambda b, n: (b * nb + n, 0, 0, 0))],
        out_shape=[jax.ShapeDtypeStruct((T, cat_w), BF),
                   jax.ShapeDtypeStruct((bl * nb, HG_HEADS, HG_DIM, HG_DIM), F32)],
        scratch_shapes=[pltpu.VMEM((HG_HEADS, HG_DIM, HG_DIM), F32)],
        name="hgrn_fwd",
        compiler_params=pltpu.CompilerParams(dimension_semantics=("arbitrary", "arbitrary"), vmem_limit_bytes=VMEM_CAP_BYTES),
    )(z, lb_logits, gnorm, kv)


def _hgrn_bwd2(z, dcat, stash, lb_logits, gnorm, kv, bl, seq):
    T, zw = z.shape
    mem_len = kv.shape[0] // bl
    cat_w = dcat.shape[1]
    R = HG_SUB * HG_CHUNK
    nb = seq // R

    def kern(z_ref, dc_ref, st_ref, lb_ref, gn_ref, kv_ref, dz_ref, dkv_ref, dlb_ref, dgn_ref, ds_scr):
        first = jnp.logical_and(pl.program_id(0) == 0, pl.program_id(1) == 0)

        @pl.when(pl.program_id(1) == 0)
        def _():
            ds_scr[...] = jnp.zeros(ds_scr.shape, F32)
            dkv_ref[...] = jnp.zeros(dkv_ref.shape, F32)

        @pl.when(first)
        def _():
            dlb_ref[...] = jnp.zeros(dlb_ref.shape, F32)
            dgn_ref[...] = jnp.zeros(dgn_ref.shape, F32)

        zq, zf, zi, zg, zx = _hgrn_rows(z_ref)
        mk, mv = _kv_pieces(kv_ref)
        S = [st_ref[h] for h in range(HG_HEADS)]
        _, vjp = jax.vjp(_hgrn_steps, zq, zf, zi, zg, zx, lb_ref[...], gn_ref[...], mk, mv, S)
        d_mix = [[dc_ref[c * HG_CHUNK:(c + 1) * HG_CHUNK, h * HG_DIM:(h + 1) * HG_DIM] for h in range(HG_HEADS)]
                 for c in range(HG_SUB)]
        base = HG_HEADS * HG_DIM
        d_att = [dc_ref[:, base + a * XA_DIM:base + (a + 1) * XA_DIM] for a in range(XA_HEADS)]
        d_s = [ds_scr[h] for h in range(HG_HEADS)]
        dzq, dzf, dzi, dzg, dzx, dlb3, dgn, dmk, dmv, dS = vjp((d_mix, d_att, d_s))
        W = HG_HEADS * HG_DIM
        dz_ref[:, W:2 * W] = dzf.astype(dz_ref.dtype)
        for c in range(HG_SUB):
            rows = slice(c * HG_CHUNK, (c + 1) * HG_CHUNK)
            for h in range(HG_HEADS):
                for k, part in ((0, dzq), (2, dzi), (3, dzg)):
                    dz_ref[rows, k * W + h * HG_DIM:k * W + (h + 1) * HG_DIM] = part[c][h].astype(dz_ref.dtype)
        for h in range(HG_HEADS):
            ds_scr[h] = dS[h]
        dlb_ref[...] += dlb3
        dgn_ref[...] += dgn
        KW = XA_HEADS * XA_DIM
        for a in range(XA_HEADS):
            dz_ref[:, 4 * W + a * XA_DIM:4 * W + (a + 1) * XA_DIM] = dzx[a].astype(dz_ref.dtype)
            dkv_ref[:, a * XA_DIM:(a + 1) * XA_DIM] += dmk[a]
            dkv_ref[:, KW + a * XA_DIM:KW + (a + 1) * XA_DIM] += dmv[a]

    rev = lambda b, n: (b * nb + (nb - 1 - n), 0)
    return pl.pallas_call(
        kern, grid=(bl, nb),
        in_specs=[pl.BlockSpec((R, zw), rev),
                  pl.BlockSpec((R, cat_w), rev),
                  pl.BlockSpec((None, HG_HEADS, HG_DIM, HG_DIM), lambda b, n: (b * nb + (nb - 1 - n), 0, 0, 0)),
                  pl.BlockSpec(lb_logits.shape, lambda b, n: (0, 0)),
                  pl.BlockSpec(gnorm.shape, lambda b, n: (0, 0)),
                  pl.BlockSpec((mem_len, kv.shape[1]), lambda b, n: (b, 0))],
        out_specs=[pl.BlockSpec((R, zw), rev),
                   pl.BlockSpec((mem_len, kv.shape[1]), lambda b, n: (b, 0)),
                   pl.BlockSpec(lb_logits.shape, lambda b, n: (0, 0)),
                   pl.BlockSpec(gnorm.shape, lambda b, n: (0, 0))],
        out_shape=[jax.ShapeDtypeStruct((T, zw), BF), jax.ShapeDtypeStruct(kv.shape, F32),
                   jax.ShapeDtypeStruct(lb_logits.shape, F32), jax.ShapeDtypeStruct(gnorm.shape, F32)],
        scratch_shapes=[pltpu.VMEM((HG_HEADS, HG_DIM, HG_DIM), F32)],
        name="hgrn_bwd",
        compiler_params=pltpu.CompilerParams(dimension_semantics=("arbitrary", "arbitrary"), vmem_limit_bytes=VMEM_CAP_BYTES),
    )(z, dcat, stash, lb_logits, gnorm, kv)


GM_SUB = 2


def _gmlp_pieces(z_ref):
    W = GM_GROUPS * GM_GROUP_DIM
    zu = [z_ref[:, g * GM_GROUP_DIM:(g + 1) * GM_GROUP_DIM] for g in range(GM_GROUPS)]
    zv = [z_ref[:, W + g * GM_GROUP_DIM:W + (g + 1) * GM_GROUP_DIM] for g in range(GM_GROUPS)]
    zx = [z_ref[:, 2 * W + a * XA_DIM:2 * W + (a + 1) * XA_DIM] for a in range(XA_HEADS)]
    return zu, zv, zx


def _gmlp_params(lng_ref, lnb_ref, ws_ref, bs_ref):
    lng = [lng_ref[:, g * GM_GROUP_DIM:(g + 1) * GM_GROUP_DIM] for g in range(GM_GROUPS)]
    lnb = [lnb_ref[:, g * GM_GROUP_DIM:(g + 1) * GM_GROUP_DIM] for g in range(GM_GROUPS)]
    ws = [ws_ref[g] for g in range(GM_GROUPS)]
    bs = [bs_ref[g:g + 1, :] for g in range(GM_GROUPS)]
    return lng, lnb, ws, bs


def _gmlp_fwd(z, ln_g, ln_b, w_s, b_s, kv, bl, nc):
    T, zw = z.shape
    mem_len = kv.shape[0] // bl
    cat_w = GM_GROUPS * GM_GROUP_DIM + XA_HEADS * XA_DIM

    assert nc % GM_SUB == 0
    nc = nc // GM_SUB
    R = GM_SUB * GM_CHUNK

    def kern(z_ref, lng_ref, lnb_ref, ws_ref, bs_ref, kv_ref, cat_ref):
        lng, lnb, ws, bs = _gmlp_params(lng_ref, lnb_ref, ws_ref, bs_ref)
        mk, mv = _kv_pieces(kv_ref)
        for c in range(GM_SUB):
            rows = pl.ds(c * GM_CHUNK, GM_CHUNK)
            zu, zv, zx = _gmlp_pieces(z_ref.at[rows])
            out = cat_ref.at[rows]
            outs = _gmlp_block(zu, zv, zx, lng, lnb, ws, bs, mk, mv)
            for g in range(GM_GROUPS):
                out[:, g * GM_GROUP_DIM:(g + 1) * GM_GROUP_DIM] = outs[g].astype(cat_ref.dtype)
            base = GM_GROUPS * GM_GROUP_DIM
            for a in range(XA_HEADS):
                out[:, base + a * XA_DIM:base + (a + 1) * XA_DIM] = outs[GM_GROUPS + a].astype(cat_ref.dtype)

    full2 = lambda b, n: (0, 0)
    return pl.pallas_call(
        kern, grid=(bl, nc),
        in_specs=[pl.BlockSpec((R, zw), lambda b, n: (b * nc + n, 0)),
                  pl.BlockSpec(ln_g.shape, full2), pl.BlockSpec(ln_b.shape, full2),
                  pl.BlockSpec(w_s.shape, lambda b, n: (0, 0, 0)), pl.BlockSpec(b_s.shape, full2),
                  pl.BlockSpec((mem_len, kv.shape[1]), lambda b, n: (b, 0))],
        out_specs=pl.BlockSpec((R, cat_w), lambda b, n: (b * nc + n, 0)),
        out_shape=jax.ShapeDtypeStruct((T, cat_w), BF),
        name="gmlp_fwd",
        compiler_params=pltpu.CompilerParams(dimension_semantics=("arbitrary", "arbitrary"), vmem_limit_bytes=VMEM_CAP_BYTES),
    )(z, ln_g, ln_b, w_s, b_s, kv)


def _gmlp_bwd(z, dcat, ln_g, ln_b, w_s, b_s, kv, bl, nc):
    T, zw = z.shape
    mem_len = kv.shape[0] // bl
    cat_w = dcat.shape[1]
    assert nc % GM_SUB == 0
    nc = nc // GM_SUB

    def kern(z_ref, dc_ref, lng_ref, lnb_ref, ws_ref, bs_ref, kv_ref,
             dz_ref, dkv_ref, dlng_ref, dlnb_ref, dws_ref, dbs_ref):
        first = jnp.logical_and(pl.program_id(0) == 0, pl.program_id(1) == 0)

        @pl.when(pl.program_id(1) == 0)
        def _():
            dkv_ref[...] = jnp.zeros(dkv_ref.shape, F32)

        @pl.when(first)
        def _():
            dlng_ref[...] = jnp.zeros(dlng_ref.shape, F32)
            dlnb_ref[...] = jnp.zeros(dlnb_ref.shape, F32)
            dws_ref[...] = jnp.zeros(dws_ref.shape, F32)
            dbs_ref[...] = jnp.zeros(dbs_ref.shape, F32)

        lng, lnb, ws, bs = _gmlp_params(lng_ref, lnb_ref, ws_ref, bs_ref)
        mk, mv = _kv_pieces(kv_ref)
        W = GM_GROUPS * GM_GROUP_DIM
        KW = XA_HEADS * XA_DIM
        for c in range(GM_SUB):
            rows = pl.ds(c * GM_CHUNK, GM_CHUNK)
            zu, zv, zx = _gmlp_pieces(z_ref.at[rows])
            dc, dz = dc_ref.at[rows], dz_ref.at[rows]
            _, vjp = jax.vjp(_gmlp_block, zu, zv, zx, lng, lnb, ws, bs, mk, mv)
            d_outs = [dc[:, g * GM_GROUP_DIM:(g + 1) * GM_GROUP_DIM] for g in range(GM_GROUPS)]
            d_outs += [dc[:, W + a * XA_DIM:W + (a + 1) * XA_DIM] for a in range(XA_HEADS)]
            dzu, dzv, dzx, dlng, dlnb, dws, dbs, dmk, dmv = vjp(d_outs)
            for g in range(GM_GROUPS):
                sl = slice(g * GM_GROUP_DIM, (g + 1) * GM_GROUP_DIM)
                dz[:, sl] = dzu[g].astype(dz_ref.dtype)
                dz[:, W + g * GM_GROUP_DIM:W + (g + 1) * GM_GROUP_DIM] = dzv[g].astype(dz_ref.dtype)
                dlng_ref[:, sl] += dlng[g]
                dlnb_ref[:, sl] += dlnb[g]
                dws_ref[g] += dws[g]
                dbs_ref[g:g + 1, :] += dbs[g]
            for a in range(XA_HEADS):
                dz[:, 2 * W + a * XA_DIM:2 * W + (a + 1) * XA_DIM] = dzx[a].astype(dz_ref.dtype)
                dkv_ref[:, a * XA_DIM:(a + 1) * XA_DIM] += dmk[a]
                dkv_ref[:, KW + a * XA_DIM:KW + (a + 1) * XA_DIM] += dmv[a]

    full2 = lambda b, n: (0, 0)
    full3 = lambda b, n: (0, 0, 0)
    blk = lambda b, n: (b * nc + n, 0)
    return pl.pallas_call(
        kern, grid=(bl, nc),
        in_specs=[pl.BlockSpec((GM_SUB * GM_CHUNK, zw), blk), pl.BlockSpec((GM_SUB * GM_CHUNK, cat_w), blk),
                  pl.BlockSpec(ln_g.shape, full2), pl.BlockSpec(ln_b.shape, full2),
                  pl.BlockSpec(w_s.shape, full3), pl.BlockSpec(b_s.shape, full2),
                  pl.BlockSpec((mem_len, kv.shape[1]), lambda b, n: (b, 0))],
        out_specs=[pl.BlockSpec((GM_SUB * GM_CHUNK, zw), blk),
                   pl.BlockSpec((mem_len, kv.shape[1]), lambda b, n: (b, 0)),
                   pl.BlockSpec(ln_g.shape, full2), pl.BlockSpec(ln_b.shape, full2),
                   pl.BlockSpec(w_s.shape, full3), pl.BlockSpec(b_s.shape, full2)],
        out_shape=[jax.ShapeDtypeStruct((T, zw), BF), jax.ShapeDtypeStruct(kv.shape, F32),
                   jax.ShapeDtypeStruct(ln_g.shape, F32), jax.ShapeDtypeStruct(ln_b.shape, F32),
                   jax.ShapeDtypeStruct(w_s.shape, F32), jax.ShapeDtypeStruct(b_s.shape, F32)],
        name="gmlp_bwd",
        compiler_params=pltpu.CompilerParams(dimension_semantics=("arbitrary", "arbitrary"), vmem_limit_bytes=VMEM_CAP_BYTES),
    )(z, dcat, ln_g, ln_b, w_s, b_s, kv)


def _place():
    x, y, c = lax.axis_index("x"), lax.axis_index("y"), lax.axis_index("c")
    chips = [(1 - x, y), (x, 1 - y), (1 - x, 1 - y)]
    return x, y, c, chips


def _half(ref, kind, e):
    if kind == "col":
        n = ref.shape[1] // 2
        return ref.at[:, pl.ds(pl.multiple_of(e * n, n), n), :]
    n = ref.shape[2] // 2
    return ref.at[:, :, pl.ds(pl.multiple_of(e * n, n), n)]


def _slot(ref, kind, j, n):
    if kind == "col":
        return ref.at[:, :, pl.ds(pl.multiple_of(j * n, n), n)]
    return ref.at[:, pl.ds(pl.multiple_of(j * n, n), n), :]


def _allgather_seq(name, items, cid):
    nt = len(items)
    kinds = [k for (_, k, _) in items]
    slot_kind = ["row" if k == "row" else "col" for k in kinds]
    out_type = []
    for s, k, l in items:
        L, r, c = s.shape
        lo = L if l is None else 1
        out_type.append(jax.ShapeDtypeStruct((lo, 4 * r, c) if k == "row" else (lo, r, 4 * c), s.dtype))

    def part(ref, t, e):
        return ref if kinds[t] == "vec" else _half(ref, kinds[t], e)

    def body(*refs):
        sh = [refs[t] if items[t][2] is None else refs[t].at[pl.ds(items[t][2], 1)] for t in range(nt)]
        full = refs[nt:2 * nt]
        s_ici, r_ici, s_d2d, r_d2d = refs[2 * nt:]
        x, y, c, chips = _place()
        own = 2 * x + y
        sibling = (x, y, 1 - c)
        barrier = pltpu.get_barrier_semaphore()
        for peer in [(px, py, c) for (px, py) in chips] + [sibling]:
            pl.semaphore_signal(barrier, inc=1, device_id=peer, device_id_type=MESH)
        pl.semaphore_wait(barrier, 4)
        width = [sh[t].shape[1] if kinds[t] == "row" else sh[t].shape[2] for t in range(nt)]
        sent = []
        for t in range(nt):
            for p, (px, py) in enumerate(chips):
                cp = pltpu.make_async_remote_copy(
                    src_ref=part(sh[t], t, c), dst_ref=part(_slot(full[t], slot_kind[t], own, width[t]), t, c),
                    send_sem=s_ici.at[t, p], recv_sem=r_ici.at[t, p], device_id=(px, py, c), device_id_type=MESH)
                cp.start()
                sent.append(cp)
        for t in range(nt):
            for p, (px, py) in enumerate(chips):
                landed = part(_slot(full[t], slot_kind[t], 2 * px + py, width[t]), t, c)
                pltpu.make_async_remote_copy(
                    src_ref=landed, dst_ref=landed, send_sem=s_ici.at[t, p], recv_sem=r_ici.at[t, p],
                    device_id=(px, py, c), device_id_type=MESH).wait_recv()
                if kinds[t] == "vec":
                    continue
                fw = pltpu.make_async_remote_copy(
                    src_ref=landed, dst_ref=landed, send_sem=s_d2d.at[t, p], recv_sem=r_d2d.at[t, p],
                    device_id=sibling, device_id_type=MESH)
                fw.start()
                sent.append(fw)
        for t in range(nt):
            if kinds[t] == "vec":
                continue
            for p, (px, py) in enumerate(chips):
                other = _half(_slot(full[t], kinds[t], 2 * px + py, width[t]), kinds[t], 1 - c)
                pltpu.make_async_remote_copy(
                    src_ref=other, dst_ref=other, send_sem=s_d2d.at[t, p], recv_sem=r_d2d.at[t, p],
                    device_id=sibling, device_id_type=MESH).wait_recv()
        for cp in sent:
            cp.wait_send()

    sems = pltpu.SemaphoreType.DMA
    return pl.kernel(
        body, out_type=out_type, mesh=plsc.ScalarSubcoreMesh(axis_name="seq", num_cores=1),
        scratch_types=[sems((nt, 3)), sems((nt, 3)), sems((nt, 3)), sems((nt, 3))],
        compiler_params=pltpu.CompilerParams(collective_id=cid), name=name,
    )(*[s for (s, _, _) in items])


def _place_own(name, full, shard, kind, layer, chip_arr, after):
    lo, r, c = (shard.shape[0] if layer is None else 1,) + shard.shape[1:]
    first = 0 if layer is None else layer
    tr = _pick(r, 512)
    nr = r // tr

    def body(chip_ref, s_ref, f_ref, after_ref, o_ref):
        o_ref[...] = s_ref[...]

    if kind == "row":
        out_map = lambda i, j, chip: (i, chip[0] * nr + j, 0)
    else:
        out_map = lambda i, j, chip: (i, j, chip[0])
    return pl.pallas_call(
        body, out_shape=jax.ShapeDtypeStruct(full.shape, full.dtype),
        grid_spec=pltpu.PrefetchScalarGridSpec(
            num_scalar_prefetch=1, grid=(lo, nr),
            in_specs=[pl.BlockSpec((1, tr, c), lambda i, j, chip: (i + first, j, 0)), pl.BlockSpec(memory_space=pl.ANY),
                      pl.BlockSpec(memory_space=pl.ANY)],
            out_specs=pl.BlockSpec((1, tr, c), out_map)),
        input_output_aliases={2: 0},
        compiler_params=pltpu.CompilerParams(dimension_semantics=("parallel", "parallel"), vmem_limit_bytes=VMEM_CAP_BYTES),
        name=name,
    )(chip_arr, shard, full, after)


def _slot2(ref, kind, j, n):
    if kind == "col":
        return ref.at[:, pl.ds(pl.multiple_of(j * n, n), n)]
    return ref.at[pl.ds(pl.multiple_of(j * n, n), n), :]


def _rs_chips_seq(name, parts, kinds, cid):
    nm = len(parts)
    out_type = []
    for g, k in zip(parts, kinds):
        r, c = g.shape
        ps = (r, c // 4) if k == "col" else (r // 4, c)
        out_type += [jax.ShapeDtypeStruct(ps, BF), jax.ShapeDtypeStruct((3,) + ps, BF)]

    def body(*refs):
        g = refs[:nm]
        outs = refs[nm:3 * nm]
        loc, ssem, rsem = refs[3 * nm:]
        x, y, c, chips = _place()
        own = 2 * x + y
        barrier = pltpu.get_barrier_semaphore()
        for (px, py) in chips:
            pl.semaphore_signal(barrier, inc=1, device_id=(px, py, c), device_id_type=MESH)
        pl.semaphore_wait(barrier, 3)
        cps = []
        for m in range(nm):
            k = kinds[m]
            own_o, got_o = outs[2 * m], outs[2 * m + 1]
            n = g[m].shape[1] // 4 if k == "col" else g[m].shape[0] // 4
            lc = pltpu.make_async_copy(_slot2(g[m], k, own, n), own_o, loc.at[m])
            lc.start()
            cps.append(lc)
            for p, (px, py) in enumerate(chips):
                cp = pltpu.make_async_remote_copy(
                    src_ref=_slot2(g[m], k, 2 * px + py, n), dst_ref=got_o.at[p],
                    send_sem=ssem.at[m, p], recv_sem=rsem.at[m, p], device_id=(px, py, c), device_id_type=MESH)
                cp.start()
                cps.append(cp)
        for cp in cps:
            cp.wait()

    return pl.kernel(
        body, out_type=out_type, mesh=plsc.ScalarSubcoreMesh(axis_name="seq", num_cores=1),
        scratch_types=[pltpu.SemaphoreType.DMA((nm,)), pltpu.SemaphoreType.DMA((nm, 3)), pltpu.SemaphoreType.DMA((nm, 3))],
        compiler_params=pltpu.CompilerParams(collective_id=cid), name=name,
    )(*parts)


def _finish_share(name, owns, gots, kind, c_arr):
    L = len(owns)
    r, c = owns[0].shape
    tr = _pick(r, 128 if kind == "col" else 256)
    nb = r // tr
    nq = L * nb

    def chunk_of(l):
        return lambda h, q: jnp.clip(q * (1 - h) + (nq - 1) * h - l * nb, 0, nb - 1)

    ins, in_specs = [], []
    for l in range(L):
        at = chunk_of(l)
        ins += [owns[l], gots[l].reshape(3 * r, c), gots[l].reshape(3 * r, c), gots[l].reshape(3 * r, c)]
        in_specs.append(pl.BlockSpec((tr, c), functools.partial(lambda h, q, cc, at: (at(h, q), 0), at=at)))
        in_specs += [pl.BlockSpec((tr, c), functools.partial(lambda h, q, cc, at, p: (p * nb + at(h, q), 0), at=at, p=p))
                     for p in range(3)]
    if kind == "col":
        out_sd = (L, 2, r, c)
        o_spec = pl.BlockSpec((None, 2, tr, c), lambda h, q, cc: ((q * h) // nb, 0, (q * h) % nb, 0))
    else:
        out_sd = (L * r, 2 * c)
        o_spec = pl.BlockSpec((tr, 2 * c), lambda h, q, cc: (q * h, 0))

    def kern(c_ref, *refs):
        in_refs = refs[:4 * L]
        out_ref, mine, recv, ssem, rsem = refs[4 * L:]
        h, q = pl.program_id(0), pl.program_id(1)
        x, y, cc, _ = _place()

        def swap(qq):
            return pltpu.make_async_remote_copy(src_ref=mine.at[qq], dst_ref=recv.at[qq], send_sem=ssem.at[qq],
                                                recv_sem=rsem.at[qq], device_id=(x, y, 1 - cc), device_id_type=MESH)

        for l in range(L):
            @pl.when(jnp.logical_and(h == 0, q // nb == l))
            def _(l=l):
                o_ref, g0, g1, g2 = in_refs[4 * l:4 * l + 4]
                mine[q] = ((o_ref[...].astype(F32) + g0[...].astype(F32)) + g1[...].astype(F32)) + g2[...].astype(F32)
                swap(q).start()

        @pl.when(h == 1)
        def _():
            swap(q).wait()
            a, b = mine[q], recv[q]
            first = c_ref[0] == 0
            lo, hi = jnp.where(first, a, b), jnp.where(first, b, a)
            if kind == "col":
                out_ref[0] = lo
                out_ref[1] = hi
            else:
                out_ref[:, :c] = lo
                out_ref[:, c:] = hi

    full = pl.pallas_call(
        kern,
        grid_spec=pltpu.PrefetchScalarGridSpec(
            num_scalar_prefetch=1, grid=(2, nq), in_specs=in_specs, out_specs=o_spec,
            scratch_shapes=[pltpu.VMEM((nq, tr, c), F32), pltpu.VMEM((nq, tr, c), F32),
                            pltpu.SemaphoreType.DMA((nq,)), pltpu.SemaphoreType.DMA((nq,))]),
        out_shape=jax.ShapeDtypeStruct(out_sd, F32), name=name,
        compiler_params=pltpu.CompilerParams(dimension_semantics=("arbitrary", "arbitrary"),
                                             vmem_limit_bytes=VMEM_CAP_BYTES),
    )(c_arr, *ins)
    return full.reshape(L, 2 * r, c) if kind == "col" else full.reshape(L, r, 2 * c)


def _small_allreduce(buf, name):
    R = buf.shape[0]
    assert R % 16 == 0
    h = R // 2

    def body(x_ref, o_ref, sib, csum, got, s_a, r_a, s_b, r_b, s_c, r_c):
        x, y, c, chips = _place()
        sibling = (x, y, 1 - c)
        own = 2 * x + y
        swap = pltpu.make_async_remote_copy(src_ref=x_ref, dst_ref=sib, send_sem=s_a, recv_sem=r_a,
                                            device_id=sibling, device_id_type=MESH)
        swap.start()
        swap.wait()
        a, b = x_ref[...], sib[...]
        south = c == 0
        csum[...] = jnp.where(south, a, b) + jnp.where(south, b, a)
        lo = pl.multiple_of(c * h, 8)
        mine = csum.at[pl.ds(lo, h)]
        got[own] = csum[pl.ds(lo, h)]
        sends = []
        for p, (px, py) in enumerate(chips):
            cp = pltpu.make_async_remote_copy(src_ref=mine, dst_ref=got.at[own], send_sem=s_b.at[p], recv_sem=r_b.at[p],
                                              device_id=(px, py, c), device_id_type=MESH)
            cp.start()
            sends.append(cp)
        for cp in sends:
            cp.wait()
        o_ref[pl.ds(lo, h)] = ((got[0] + got[1]) + got[2]) + got[3]
        done = o_ref.at[pl.ds(lo, h)]
        back = pltpu.make_async_remote_copy(src_ref=done, dst_ref=done, send_sem=s_c, recv_sem=r_c,
                                            device_id=sibling, device_id_type=MESH)
        back.start()
        back.wait_send()
        other = o_ref.at[pl.ds(pl.multiple_of((1 - c) * h, 8), h)]
        pltpu.make_async_remote_copy(src_ref=other, dst_ref=other, send_sem=s_c, recv_sem=r_c,
                                     device_id=sibling, device_id_type=MESH).wait_recv()

    vm = pl.BlockSpec(memory_space=pltpu.VMEM)
    return pl.pallas_call(
        body, out_shape=jax.ShapeDtypeStruct(buf.shape, F32), in_specs=[vm], out_specs=vm,
        scratch_shapes=[pltpu.VMEM((R, LANES), F32), pltpu.VMEM((R, LANES), F32), pltpu.VMEM((4, h, LANES), F32),
                        pltpu.SemaphoreType.DMA, pltpu.SemaphoreType.DMA, pltpu.SemaphoreType.DMA((3,)),
                        pltpu.SemaphoreType.DMA((3,)), pltpu.SemaphoreType.DMA, pltpu.SemaphoreType.DMA],
        name=name,
        compiler_params=pltpu.CompilerParams(vmem_limit_bytes=VMEM_CAP_BYTES),
    )(buf)


PACK_TILE_ROWS = 8


def _item_rows(shape):
    n = 1
    for d in shape:
        n *= d
    return -(-n // (PACK_TILE_ROWS * LANES)) * PACK_TILE_ROWS


def _pack(arrs, rows_total):
    buf = jnp.zeros((rows_total, LANES), F32)
    r = 0
    for a in arrs:
        f = a.reshape(-1).astype(F32)
        nr = _item_rows(a.shape)
        block = jnp.pad(f, (0, nr * LANES - f.shape[0])).reshape(nr, LANES)
        buf = lax.dynamic_update_slice(buf, block, (r, 0))
        r += nr
    return buf


def _unpack(buf, shapes):
    out, r = [], 0
    for s in shapes:
        n = 1
        for d in s:
            n *= d
        nr = _item_rows(s)
        out.append(buf[r:r + nr].reshape(-1)[:n].reshape(s))
        r += nr
    return out


def _rows_needed(shapes):
    return -(-sum(_item_rows(s) for s in shapes) // (2 * PACK_TILE_ROWS)) * (2 * PACK_TILE_ROWS)


def _two_rows(a, b):
    out = jnp.zeros((2, a.shape[1]), a.dtype)
    return lax.dynamic_update_slice(lax.dynamic_update_slice(out, a, (0, 0)), b, (1, 0))


def _adam(w, g, m, v):
    m = ADAM_B1 * m + (1.0 - ADAM_B1) * g
    v = ADAM_B2 * v + (1.0 - ADAM_B2) * jnp.square(g)
    m_hat = m / (1.0 - ADAM_B1 ** ADAM_STEP)
    v_hat = v / (1.0 - ADAM_B2 ** ADAM_STEP)
    delta = -ADAM_LR * (m_hat / (jnp.sqrt(v_hat) + ADAM_EPS) + ADAM_WD * w)
    return delta, m, v


def _adam_call(name, w2, g2, m2, v2, tr, pass_grad=False):
    def fn(rv, cv):
        outs = list(_adam(*rv))
        return ([rv[1]] + outs if pass_grad else outs), []

    width = w2.shape[1]
    return _rowcall(name, fn, [(w2, 0, width), (g2, 0, width), (m2, 0, width), (v2, 0, width)], [],
                    [(width, F32)] * (4 if pass_grad else 3), [], tr)


def kernel(x, mem, mem_norm, lb_logits, ffn1_norm, ffn1_w_in, ffn1_w_out, mix_norm, mem_w_kv, hgrn_w_in, hgrn_gnorm, hgrn_w_out, gmlp_w_in, gmlp_ln_g, gmlp_ln_b, gmlp_w_s, gmlp_b_s, gmlp_w_out, ffn2_norm, ffn2_w_in, ffn2_w_out, final_norm, loss_target, m_mem_norm, m_lb_logits, m_ffn1_norm, m_ffn1_w_in, m_ffn1_w_out, m_mix_norm, m_mem_w_kv, m_hgrn_w_in, m_hgrn_gnorm, m_hgrn_w_out, m_gmlp_w_in, m_gmlp_ln_g, m_gmlp_ln_b, m_gmlp_w_s, m_gmlp_b_s, m_gmlp_w_out, m_ffn2_norm, m_ffn2_w_in, m_ffn2_w_out, m_final_norm, v_mem_norm, v_lb_logits, v_ffn1_norm, v_ffn1_w_in, v_ffn1_w_out, v_mix_norm, v_mem_w_kv, v_hgrn_w_in, v_hgrn_gnorm, v_hgrn_w_out, v_gmlp_w_in, v_gmlp_ln_g, v_gmlp_ln_b, v_gmlp_w_s, v_gmlp_b_s, v_gmlp_w_out, v_ffn2_norm, v_ffn2_w_in, v_ffn2_w_out, v_final_norm):
    bl, seq, D = x.shape
    T = bl * seq
    mem_len = mem.shape[1]
    chip = 2 * lax.axis_index("x") + lax.axis_index("y")
    c_arr = lax.axis_index("c").astype(jnp.int32).reshape(1)
    chip_arr = chip.astype(jnp.int32).reshape(1)
    TR = 1024

    big = [("ffn1_w_in", ffn1_w_in, "col"), ("ffn1_w_out", ffn1_w_out, "row"), ("mem_w_kv", mem_w_kv, "col"),
           ("hgrn_w_in", hgrn_w_in, "col"), ("hgrn_w_out", hgrn_w_out, "row"), ("gmlp_w_in", gmlp_w_in, "col"),
           ("gmlp_w_out", gmlp_w_out, "row"), ("ffn2_w_in", ffn2_w_in, "col"), ("ffn2_w_out", ffn2_w_out, "row")]
    kinds = [k for (_, _, k) in big]
    shards_bf = []
    for nm, w, _ in big:
        L, r, c = w.shape
        (wb,) = _rowcall("cast_" + nm, lambda rv, cv: ([rv[0]], []), [(w.reshape(L * r, c), 0, c)], [], [(c, BF)], [], 512)
        shards_bf.append(wb.reshape(L, r, c))
    sb = dict(zip([nm for (nm, _, _) in big], shards_bf))
    groups = [[("ffn1_w_in", 0)], [("ffn1_w_out", 0)], [("hgrn_w_in", None)], [("mem_w_kv", None)], [("hgrn_w_out", None)],
              [("ffn2_w_in", 0), ("ffn2_w_out", 0), ("gmlp_ln_g", None), ("gmlp_ln_b", None)],
              [("ffn1_w_in", 1), ("ffn1_w_out", 1)],
              [("gmlp_w_in", None), ("gmlp_w_out", None)],
              [("ffn2_w_in", 1), ("ffn2_w_out", 1)]]
    kind_of = {nm: k for (nm, _, k) in big}
    for nm, vec in (("gmlp_ln_g", gmlp_ln_g), ("gmlp_ln_b", gmlp_ln_b)):
        sb[nm] = vec.reshape(1, 1, -1)
        kind_of[nm] = "vec"
    gathered = {nm: [None, None] for nm in ("ffn1_w_in", "ffn1_w_out", "ffn2_w_in", "ffn2_w_out")}
    others = {}
    for gi, grp in enumerate(groups):
        outs = _allgather_seq("gather_%d" % gi, [(sb[nm], kind_of[nm], l) for (nm, l) in grp], gi)
        for (nm, l), o in zip(grp, outs):
            others[(nm, l)] = o

    def whole(nm, l, after):
        full = _place_own("own_%s_%d" % (nm, l or 0), others[(nm, l)], sb[nm], "row" if kind_of[nm] == "row" else "col", l,
                          chip_arr, after)
        if l is None:
            gathered[nm] = full
        else:
            gathered[nm][l] = full
        return full

    def rms_fwd(name, xin, g):
        (h,) = _rowcall(name, lambda rv, cv: ([_rmsnorm(rv[0], cv[0])], []), [(xin, 0, D)], [g.reshape(1, D)], [(D, BF)], [], TR)
        return h

    def ffn_fwd(tag, xin, h, nm_in, nm_out, layer, next_gain):
        w_in = whole(nm_in, layer, h)
        dff = w_in.shape[2] // 2
        zg, zu, a = _ffn_in_swiglu("ffn_in_" + tag, h, w_in, 1024, dff // 2)
        out = _mm("ffn_out_" + tag, a, whole(nm_out, layer, a), "nn", F32, 1024, 1024, dff, scale=0.5, res=xin, b_lead=0,
                  norm_gain=None if next_gain is None else next_gain.reshape(1, D))
        xo, h_next = (out, None) if next_gain is None else out
        return xo, h_next, (xin, h, zg, zu, a)

    def ffn_bwd(tag, dxo, saved, g, w_in, w_out, layer):
        xin, h, zg, zu, a = saved
        dff = w_out[layer].shape[1]
        dw_out = _mm_tn_pair("ffn_dwo_" + tag, a, dxo, "row", c_arr, dff // 2, T, scale=0.5)
        dz = _ffn_da_swiglu("ffn_da_" + tag, dxo, w_out[layer], zg, zu, 512)
        dw_in = _mm_tn_pair("ffn_dwi_" + tag, h, dz, "col", c_arr, 512, T)
        dx, dg = _mm_dh_rms("ffn_dh_" + tag, dz, w_in[layer], xin, g.reshape(1, D), dxo, 512)
        return dx, dg, dw_in, dw_out

    def rms_bwd(name, xin, g, dh, dres):
        def fn(rv, cv):
            _, vjp = jax.vjp(_rmsnorm, rv[0], cv[0])
            dx, dg = vjp(rv[1])
            if dres is not None:
                dx = dx + rv[2]
            return [dx], [dg]

        rows = [(xin, 0, D), (dh, 0, D)] + ([(dres, 0, D)] if dres is not None else [])
        dx, dg = _rowcall(name, fn, rows, [g.reshape(1, D)], [(D, F32)], [((1, D), F32)], TR)
        return dx, dg

    x0 = x.reshape(T, D)
    tgt = loss_target.reshape(T, D)
    mem2 = mem.reshape(bl * mem_len, D)
    memn = rms_fwd("rms_mem", mem2, mem_norm)

    h_f10 = rms_fwd("rms_f1l0", x0, ffn1_norm[0])
    x1, h_m0, sv_f10 = ffn_fwd("f1l0", x0, h_f10, "ffn1_w_in", "ffn1_w_out", 0, mix_norm[0])
    z_m0 = _mm("mix_in_0", h_m0, whole("hgrn_w_in", None, h_m0), "nn", F32, 2048, 512, D, b_lead=0)
    w_kv = whole("mem_w_kv", None, z_m0)
    kv = [_mm("kv_%d" % i, memn, w_kv, "nn", F32, 512, 512, D, b_lead=i) for i in range(2)]
    cat0, stash0 = _hgrn_fwd2(z_m0, lb_logits, hgrn_gnorm, kv[0], bl, seq)
    x2, h_f20 = _mm("mix_out_0", cat0, whole("hgrn_w_out", None, cat0), "nn", F32, 1024, 1024, cat0.shape[1], res=x1, b_lead=0,
                    norm_gain=ffn2_norm[0].reshape(1, D))
    x3, h_f11, sv_f20 = ffn_fwd("f2l0", x2, h_f20, "ffn2_w_in", "ffn2_w_out", 0, ffn1_norm[1])
    x4, h_m1, sv_f11 = ffn_fwd("f1l1", x3, h_f11, "ffn1_w_in", "ffn1_w_out", 1, mix_norm[1])
    z_m1 = _mm("mix_in_1", h_m1, whole("gmlp_w_in", None, h_m1), "nn", F32, 2048, 512, D, b_lead=0)
    nc1 = seq // GM_CHUNK
    w_s, b_s = gmlp_w_s[0], gmlp_b_s[0]
    ln_w = GM_GROUPS * GM_GROUP_DIM
    ln_g_full, ln_b_full = [whole(nm, None, z_m1).reshape(1, ln_w) for nm in ("gmlp_ln_g", "gmlp_ln_b")]
    cat1 = _gmlp_fwd(z_m1, ln_g_full, ln_b_full, w_s, b_s, kv[1], bl, nc1)
    x5, h_f21 = _mm("mix_out_1", cat1, whole("gmlp_w_out", None, cat1), "nn", F32, 1024, 1024, cat1.shape[1], res=x4, b_lead=0,
                    norm_gain=ffn2_norm[1].reshape(1, D))
    x6, _, sv_f21 = ffn_fwd("f2l1", x5, h_f21, "ffn2_w_in", "ffn2_w_out", 1, None)

    def head(rv, cv):
        def f(xx, gg):
            err = _rmsnorm(xx, gg) - rv[1]
            return 0.5 * jnp.sum(jnp.mean(err * err, axis=-1, keepdims=True), axis=0, keepdims=True)

        ls, vjp = jax.vjp(f, rv[0], cv[0])
        dx, dg = vjp(jnp.ones((1, 1), F32))
        return [dx], [dg, jnp.broadcast_to(ls, (1, 128))]

    dx6, d_final, loss_part = _rowcall("loss_head", head, [(x6, 0, D), (tgt, 0, D)], [final_norm.reshape(1, D)],
                                       [(D, F32)], [((1, D), F32), ((1, 128), F32)], TR)

    rs_out = {}
    n_gather = len(groups)

    def rs(gi, items):
        outs = _rs_chips_seq("reduce_%d" % gi, [p for (_, p, _) in items], [k for (_, _, k) in items], n_gather + gi)
        for i, (key, _, _) in enumerate(items):
            rs_out[key] = (outs[2 * i], outs[2 * i + 1])

    dx5, dg_f21, dwi_f21, dwo_f21 = ffn_bwd("f2l1", dx6, sv_f21, ffn2_norm[1], gathered["ffn2_w_in"], gathered["ffn2_w_out"], 1)
    rs(0, [(("ffn2_w_out", 1), dwo_f21, "row"), (("ffn2_w_in", 1), dwi_f21, "col")])
    dcat1 = _mm("mix_dcat_1", dx5, gathered["gmlp_w_out"], "nt", F32, 2048, 1024, D, b_lead=0)
    dwo_m1 = _mm_tn_pair("mix_dwo_1", cat1, dx5, "row", c_arr, 1024, T)
    dz_m1, dkv1, d_lng, d_lnb, d_ws, d_bs = _gmlp_bwd(z_m1, dcat1, ln_g_full, ln_b_full, w_s, b_s, kv[1], bl, nc1)
    dx4, dg_m1 = _mm_dh_rms("mix_dh_1", dz_m1, gathered["gmlp_w_in"], x4, mix_norm[1].reshape(1, D), dx5, 512)
    dwi_m1 = _mm_tn_pair("mix_dwi_1", h_m1, dz_m1, "col", c_arr, 1024, T)
    rs(1, [(("gmlp_w_out", 0), dwo_m1, "row"), (("gmlp_w_in", 0), dwi_m1, "col")])
    dx3, dg_f11, dwi_f11, dwo_f11 = ffn_bwd("f1l1", dx4, sv_f11, ffn1_norm[1], gathered["ffn1_w_in"], gathered["ffn1_w_out"], 1)
    rs(2, [(("ffn1_w_out", 1), dwo_f11, "row"), (("ffn1_w_in", 1), dwi_f11, "col")])

    dx2, dg_f20, dwi_f20, dwo_f20 = ffn_bwd("f2l0", dx3, sv_f20, ffn2_norm[0], gathered["ffn2_w_in"], gathered["ffn2_w_out"], 0)
    rs(3, [(("ffn2_w_out", 0), dwo_f20, "row"), (("ffn2_w_in", 0), dwi_f20, "col")])
    dcat0 = _mm("mix_dcat_0", dx2, gathered["hgrn_w_out"], "nt", F32, 2048, 1024, D, b_lead=0)
    dwo_m0 = _mm_tn_pair("mix_dwo_0", cat0, dx2, "row", c_arr, 1024, T)
    dz_m0, dkv0, d_lb, d_gn = _hgrn_bwd2(z_m0, dcat0, stash0, lb_logits, hgrn_gnorm, kv[0], bl, seq)
    dx1, dg_m0 = _mm_dh_rms("mix_dh_0", dz_m0, gathered["hgrn_w_in"], x1, mix_norm[0].reshape(1, D), dx2, 512)
    dwi_m0 = _mm_tn_pair("mix_dwi_0", h_m0, dz_m0, "col", c_arr, 1024, T)
    rs(4, [(("hgrn_w_out", 0), dwo_m0, "row"), (("hgrn_w_in", 0), dwi_m0, "col")])

    dwkv = [_mm_tn_pair("kv_dw_%d" % i, memn, dkv, "col", c_arr, 1024, 512) for i, dkv in enumerate([dkv0, dkv1])]
    rs(5, [(("mem_w_kv", 0), dwkv[0], "col"), (("mem_w_kv", 1), dwkv[1], "col")])
    dmemn = _mm("kv_dx_0", dkv0, gathered["mem_w_kv"], "nt", F32, 512, 512, 1024, b_lead=0)
    dmemn = _mm("kv_dx_1", dkv1, gathered["mem_w_kv"], "nt", F32, 512, 512, 1024, res=dmemn, b_lead=1)
    _, d_memnorm = rms_bwd("rms_bwd_mem", mem2, mem_norm, dmemn, None)

    dx0, dg_f10, dwi_f10, dwo_f10 = ffn_bwd("f1l0", dx1, sv_f10, ffn1_norm[0], gathered["ffn1_w_in"], gathered["ffn1_w_out"], 0)
    rs(6, [(("ffn1_w_out", 0), dwo_f10, "row")])
    rs(7, [(("ffn1_w_in", 0), dwi_f10, "col")])

    shard_grads = [_finish_share("finish_" + nm, [rs_out[(nm, l)][0] for l in range(w.shape[0])],
                                 [rs_out[(nm, l)][1] for l in range(w.shape[0])], k, c_arr) for (nm, w, k) in big]

    big_w = [w for (_, w, _) in big]
    big_m = [m_ffn1_w_in, m_ffn1_w_out, m_mem_w_kv, m_hgrn_w_in, m_hgrn_w_out, m_gmlp_w_in, m_gmlp_w_out, m_ffn2_w_in, m_ffn2_w_out]
    big_v = [v_ffn1_w_in, v_ffn1_w_out, v_mem_w_kv, v_hgrn_w_in, v_hgrn_w_out, v_gmlp_w_in, v_gmlp_w_out, v_ffn2_w_in, v_ffn2_w_out]
    big_out = {}
    for (nm, w, _), g, m, v in zip(big, shard_grads, big_m, big_v):
        L, r, c = w.shape
        g2, d2, m2, v2 = _adam_call("adam_" + nm, w.reshape(L * r, c), g.reshape(L * r, c), m.reshape(L * r, c),
                                    v.reshape(L * r, c), 256, pass_grad=True)
        big_out[nm] = (g2.reshape(w.shape), d2.reshape(w.shape), m2.reshape(w.shape), v2.reshape(w.shape))

    d_ffn1n = _two_rows(dg_f10, dg_f11)
    d_mixn = _two_rows(dg_m0, dg_m1)
    d_ffn2n = _two_rows(dg_f20, dg_f21)
    small_parts = [loss_part[:, :1], d_memnorm, d_lb, d_ffn1n, d_mixn, d_gn, d_lng, d_lnb, d_ws, d_bs, d_ffn2n, d_final]
    red_shapes = [(1,), mem_norm.shape, lb_logits.shape, ffn1_norm.shape, mix_norm.shape, hgrn_gnorm.shape, (1, ln_w), (1, ln_w),
                  gmlp_w_s.shape, gmlp_b_s.shape, ffn2_norm.shape, final_norm.shape]
    red = _small_allreduce(_pack(small_parts, _rows_needed(red_shapes)), "reduce_small")
    (loss_v, g_memn, g_lb, g_f1n, g_mixn, g_gn, g_lng_full, g_lnb_full, g_ws, g_bs, g_f2n, g_fin) = _unpack(red, red_shapes)
    lsh = gmlp_ln_g.shape[1]
    g_lng = lax.dynamic_slice(g_lng_full, (0, chip * lsh), (1, lsh))
    g_lnb = lax.dynamic_slice(g_lnb_full, (0, chip * lsh), (1, lsh))
    small_w = [mem_norm, lb_logits, ffn1_norm, mix_norm, hgrn_gnorm, gmlp_ln_g, gmlp_ln_b, gmlp_w_s, gmlp_b_s, ffn2_norm, final_norm]
    small_g = [g_memn, g_lb, g_f1n, g_mixn, g_gn, g_lng, g_lnb, g_ws, g_bs, g_f2n, g_fin]
    small_m = [m_mem_norm, m_lb_logits, m_ffn1_norm, m_mix_norm, m_hgrn_gnorm, m_gmlp_ln_g, m_gmlp_ln_b, m_gmlp_w_s, m_gmlp_b_s, m_ffn2_norm, m_final_norm]
    small_v = [v_mem_norm, v_lb_logits, v_ffn1_norm, v_mix_norm, v_hgrn_gnorm, v_gmlp_ln_g, v_gmlp_ln_b, v_gmlp_w_s, v_gmlp_b_s, v_ffn2_norm, v_final_norm]
    sshapes = [w.shape for w in small_w]
    nrow = _rows_needed(sshapes)
    d_p, m_p, v_p = _adam_call("adam_small", _pack(small_w, nrow), _pack(small_g, nrow), _pack(small_m, nrow), _pack(small_v, nrow), nrow)
    s_delta, s_m, s_v = _unpack(d_p, sshapes), _unpack(m_p, sshapes), _unpack(v_p, sshapes)
    small_names = ["mem_norm", "lb_logits", "ffn1_norm", "mix_norm", "hgrn_gnorm", "gmlp_ln_g", "gmlp_ln_b", "gmlp_w_s", "gmlp_b_s", "ffn2_norm", "final_norm"]
    small_out = {nm: (g.reshape(w.shape), d, m, v) for nm, w, g, d, m, v in zip(small_names, small_w, small_g, s_delta, s_m, s_v)}

    order = ["mem_norm", "lb_logits", "ffn1_norm", "ffn1_w_in", "ffn1_w_out", "mix_norm", "mem_w_kv", "hgrn_w_in", "hgrn_gnorm",
             "hgrn_w_out", "gmlp_w_in", "gmlp_ln_g", "gmlp_ln_b", "gmlp_w_s", "gmlp_b_s", "gmlp_w_out", "ffn2_norm", "ffn2_w_in",
             "ffn2_w_out", "final_norm"]
    allo = {**big_out, **small_out}
    grad_x = dx0.reshape(x.shape)
    return (loss_v.reshape(()), grad_x, *[allo[n][0] for n in order], *[allo[n][1] for n in order],
            *[allo[n][2] for n in order], *[allo[n][3] for n in order])
```

```python
import functools

import jax
import jax.numpy as jnp
from jax import lax
from jax.experimental import pallas as pl
from jax.experimental.pallas import tpu as pltpu
from jax.experimental.pallas import tpu_sc as plsc

BF = jnp.bfloat16
F32 = jnp.float32
MESH = pl.DeviceIdType.MESH

EPS = 1e-6
D_MODEL = 1024
HG_HEADS = 8
HG_DIM = 128
HG_CHUNK = 64
GM_CHUNK = 128
GM_GROUPS = 8
GM_GROUP_DIM = 256
XA_HEADS = 4
XA_DIM = 256
ADAM_LR = 0.001
ADAM_B1 = 0.9
ADAM_B2 = 0.999
ADAM_EPS = 1e-08
ADAM_WD = 0.01
ADAM_STEP = 10

VMEM_CAP_BYTES = 60 * 1024 * 1024
LANES = 1024


def _pick(n, cap, mult=16):
    if n <= cap:
        return n
    for d in range(cap - cap % mult, 0, -mult):
        if n % d == 0:
            return d
    raise ValueError((n, cap, mult))


def _dg(a, b, ca, cb):
    return lax.dot_general(a.astype(BF), b.astype(BF), (((ca,), (cb,)), ((), ())), preferred_element_type=F32)


@jax.custom_vjp
def dot_nn(a, b):
    return _dg(a, b, 1, 0)


def _nn_fwd(a, b):
    return _dg(a, b, 1, 0), (a, b)


def _nn_bwd(r, g):
    a, b = r
    return _dg(g, b, 1, 1), _dg(a, g, 0, 0)


dot_nn.defvjp(_nn_fwd, _nn_bwd)


@jax.custom_vjp
def dot_nt(a, b):
    return _dg(a, b, 1, 1)


def _nt_fwd(a, b):
    return _dg(a, b, 1, 1), (a, b)


def _nt_bwd(r, g):
    a, b = r
    return _dg(g, b, 1, 0), _dg(g, a, 0, 0)


dot_nt.defvjp(_nt_fwd, _nt_bwd)


@jax.custom_vjp
def dot_tn(a, b):
    return _dg(a, b, 0, 0)


def _tn_fwd(a, b):
    return _dg(a, b, 0, 0), (a, b)


def _tn_bwd(r, g):
    a, b = r
    return _dg(b, g, 1, 1), _dg(a, g, 1, 0)


dot_tn.defvjp(_tn_fwd, _tn_bwd)


def _rmsnorm(x, g):
    return x * lax.rsqrt(jnp.mean(x * x, axis=-1, keepdims=True) + EPS) * g


def _silu(x):
    return x * jax.nn.sigmoid(x)


@jax.custom_vjp
def _gelu(x):
    return 0.5 * x * (1.0 + lax.erf(x * (0.5 ** 0.5)))


def _gelu_fwd(x):
    return _gelu(x), x


def _gelu_bwd(x, g):
    t = x * (0.5 ** 0.5)
    cdf = 0.5 * (1.0 + lax.erf(t))
    return (g * (cdf + x * (jnp.exp(-(t * t)) * (0.5 / 3.141592653589793) ** 0.5)),)


_gelu.defvjp(_gelu_fwd, _gelu_bwd)


def _softmax_last(s):
    m = lax.stop_gradient(jnp.max(s, axis=-1, keepdims=True))
    e = jnp.exp(s - m)
    return e / jnp.sum(e, axis=-1, keepdims=True)


def _tril(n):
    r = lax.broadcasted_iota(jnp.int32, (n, n), 0)
    c = lax.broadcasted_iota(jnp.int32, (n, n), 1)
    return r >= c


def _attention(zx, mk, mv):
    s = dot_nt(zx, mk) * (XA_DIM ** -0.5)
    return dot_nn(_softmax_last(s), mv)


def _chunk_sums(x, suffix):
    n = x.shape[0]
    r = lax.broadcasted_iota(jnp.int32, (n, n), 0)
    c = lax.broadcasted_iota(jnp.int32, (n, n), 1)
    tri = jnp.logical_and(r <= c if suffix else r >= c, r // HG_CHUNK == c // HG_CHUNK).astype(BF)
    hi = x.astype(BF)
    rest = x - hi.astype(F32)
    mid = rest.astype(BF)
    lo = (rest - mid.astype(F32)).astype(BF)
    return (_dg(tri, hi, 1, 0) + _dg(tri, mid, 1, 0)) + _dg(tri, lo, 1, 0)


@jax.custom_vjp
def _running_sums(x):
    return _chunk_sums(x, False)


_running_sums.defvjp(lambda x: (_chunk_sums(x, False), None), lambda _, g: (_chunk_sums(g, True),))


def _hgrn_decays(zf, lb3):
    l0, l1, l2 = lb3[0:1], lb3[1:2], lb3[2:3]
    m = lax.stop_gradient(jnp.maximum(jnp.maximum(l0, l1), l2))
    e0 = jnp.exp(l0 - m)
    lb = e0 / (e0 + jnp.exp(l1 - m) + jnp.exp(l2 - m))
    f = lb + (1.0 - lb) * jax.nn.sigmoid(zf)
    return f, _running_sums(jnp.log(f))


def _hgrn_head(zq, f, b, zi, zg, gn, S):
    q = _silu(zq)
    k = 1.0 - f
    b_last = b[HG_CHUNK - 1:HG_CHUNK, :]
    q_dec = q * jnp.exp(b)
    k_inv = k * jnp.exp(-b)
    a = jnp.where(_tril(HG_CHUNK), dot_nt(q_dec, k_inv), 0.0)
    o = dot_nn(a, zi) + dot_nn(q_dec, S)
    S_new = jnp.exp(b_last).reshape(HG_DIM, 1) * S + dot_tn(k * jnp.exp(b_last - b), zi)
    o = _rmsnorm(o, gn) * _silu(zg)
    return o, S_new


def _gmlp_block(zu, zv, zx, lng, lnb, ws, bs, mk, mv):
    gv = [_gelu(v) for v in zv]
    width = GM_GROUPS * GM_GROUP_DIM
    mu = sum(jnp.sum(g, axis=-1, keepdims=True) for g in gv) / width
    xc = [g - mu for g in gv]
    var = sum(jnp.sum(c * c, axis=-1, keepdims=True) for c in xc) / width
    r = lax.rsqrt(var + EPS)
    outs = []
    for g in range(GM_GROUPS):
        v = xc[g] * r * lng[g] + lnb[g]
        w = jnp.where(_tril(GM_CHUNK), ws[g], 0.0)
        mixed = dot_nn(w, v) + bs[g].reshape(GM_CHUNK, 1)
        outs.append(_gelu(zu[g]) * mixed)
    for a in range(XA_HEADS):
        outs.append(_attention(zx[a], mk[a], mv[a]))
    return outs


def _rowcall(name, fn, rows, consts, row_outs, acc_outs, tr):
    nrows = rows[0][0].shape[0]
    tr = _pick(nrows, tr)
    n_r, n_c, n_ro, n_ao = len(rows), len(consts), len(row_outs), len(acc_outs)

    def kern(*refs):
        rv = [r[...] for r in refs[:n_r]]
        cv = [r[...] for r in refs[n_r:n_r + n_c]]
        ro_refs = refs[n_r + n_c:n_r + n_c + n_ro]
        ao_refs = refs[n_r + n_c + n_ro:]
        ro, ao = fn(rv, cv)
        for ref, v in zip(ro_refs, ro):
            ref[...] = v.astype(ref.dtype)
        if n_ao:
            @pl.when(pl.program_id(0) == 0)
            def _():
                for ref in ao_refs:
                    ref[...] = jnp.zeros(ref.shape, ref.dtype)

            for ref, v in zip(ao_refs, ao):
                ref[...] += v.astype(ref.dtype)

    in_specs = [pl.BlockSpec((tr, w), functools.partial(lambda i, cb: (i, cb), cb=cb)) for (_, cb, w) in rows]
    in_specs += [pl.BlockSpec(c.shape, lambda i: (0, 0)) for c in consts]
    out_specs = [pl.BlockSpec((tr, w), lambda i: (i, 0)) for (w, _) in row_outs]
    out_specs += [pl.BlockSpec(s, lambda i: (0, 0)) for (s, _) in acc_outs]
    out_shape = [jax.ShapeDtypeStruct((nrows, w), dt) for (w, dt) in row_outs]
    out_shape += [jax.ShapeDtypeStruct(s, dt) for (s, dt) in acc_outs]
    outs = pl.pallas_call(
        kern, grid=(nrows // tr,), in_specs=in_specs, out_specs=out_specs, out_shape=out_shape, name=name,
        compiler_params=pltpu.CompilerParams(dimension_semantics=("arbitrary",),
                                             vmem_limit_bytes=VMEM_CAP_BYTES),
    )(*[a for (a, _, _) in rows], *consts)
    return outs


def _mm(name, a, b, mode, out_dtype, tm, tn, tk, scale=1.0, res=None, a_lead=None, b_lead=None, norm_gain=None):
    ash = a.shape[-2:]
    bsh = b.shape[-2:]
    if mode == "nn":
        (M, K), (K2, N) = ash, bsh
    elif mode == "nt":
        (M, K), (N, K2) = ash, bsh
    else:
        (K, M), (K2, N) = ash, bsh
    assert K == K2, (name, a.shape, b.shape)
    tm, tn, tk = min(tm, M), min(tn, N), min(tk, K)
    assert M % tm == 0 and N % tn == 0 and K % tk == 0, (name, M, N, K, tm, tn, tk)
    nk = K // tk
    dims = {"nn": (1, 0), "nt": (1, 1), "tn": (0, 0)}[mode]

    def lead(spec_shape, index_fn, lead_idx):
        if lead_idx is None:
            return pl.BlockSpec(spec_shape, index_fn)
        return pl.BlockSpec((None,) + spec_shape, lambda i, j, k: (lead_idx,) + index_fn(i, j, k))

    if mode == "tn":
        a_spec = lead((tk, tm), lambda i, j, k: (k, i), a_lead)
    else:
        a_spec = lead((tm, tk), lambda i, j, k: (i, k), a_lead)
    if mode == "nt":
        b_spec = lead((tn, tk), lambda i, j, k: (j, k), b_lead)
    else:
        b_spec = lead((tk, tn), lambda i, j, k: (k, j), b_lead)
    o_spec = pl.BlockSpec((tm, tn), lambda i, j, k: (i, j))
    has_res = res is not None
    has_norm = norm_gain is not None
    assert not has_norm or tn == N

    def kern(*refs):
        a_ref, b_ref = refs[0], refs[1]
        pos = 2
        res_ref = gain_ref = h_ref = None
        if has_res:
            res_ref, pos = refs[pos], pos + 1
        if has_norm:
            gain_ref, pos = refs[pos], pos + 1
        o_ref, pos = refs[pos], pos + 1
        if has_norm:
            h_ref = refs[pos]
        acc_ref = refs[-1] if nk > 1 else None
        p = lax.dot_general(a_ref[...].astype(BF), b_ref[...].astype(BF), (((dims[0],), (dims[1],)), ((), ())),
                            preferred_element_type=F32)

        def finish(v):
            if scale != 1.0:
                v = v * scale
            if has_res:
                v = res_ref[...] + v
            o_ref[...] = v.astype(o_ref.dtype)
            if has_norm:
                h_ref[...] = _rmsnorm(v, gain_ref[...]).astype(h_ref.dtype)

        if nk == 1:
            finish(p)
        else:
            k = pl.program_id(2)

            @pl.when(k == 0)
            def _():
                acc_ref[...] = p

            @pl.when(k > 0)
            def _():
                acc_ref[...] += p

            @pl.when(k == nk - 1)
            def _():
                finish(acc_ref[...])

    ins = [a, b] + ([res] if has_res else []) + ([norm_gain] if has_norm else [])
    in_specs = [a_spec, b_spec] + ([o_spec] if has_res else [])
    in_specs += [pl.BlockSpec((1, N), lambda i, j, k: (0, 0))] if has_norm else []
    out_sd = jax.ShapeDtypeStruct((M, N), out_dtype)
    return pl.pallas_call(
        kern, grid=(M // tm, N // tn, nk), in_specs=in_specs,
        out_specs=[o_spec, o_spec] if has_norm else o_spec,
        out_shape=[out_sd, jax.ShapeDtypeStruct((M, N), BF)] if has_norm else out_sd,
        scratch_shapes=[pltpu.VMEM((tm, tn), F32)] if nk > 1 else [],
        name=name,
        compiler_params=pltpu.CompilerParams(dimension_semantics=("parallel", "parallel", "arbitrary"),
                                             vmem_limit_bytes=VMEM_CAP_BYTES),
    )(*ins)


def _ffn_in_swiglu(name, h, w3, tm, tn):
    T, D = h.shape
    dff = w3.shape[2] // 2
    tm = min(tm, T)
    assert T % tm == 0 and dff % tn == 0
    nj = dff // tn

    def kern(h_ref, wg_ref, wu_ref, zg_ref, zu_ref, a_ref):
        hb = h_ref[...]
        g = jnp.dot(hb, wg_ref[...], preferred_element_type=F32).astype(BF)
        u = jnp.dot(hb, wu_ref[...], preferred_element_type=F32).astype(BF)
        zg_ref[...] = g
        zu_ref[...] = u
        a_ref[...] = (_silu(g.astype(F32)) * u.astype(F32)).astype(BF)

    o_spec = pl.BlockSpec((tm, tn), lambda i, j: (i, j))
    return pl.pallas_call(
        kern, grid=(T // tm, nj),
        in_specs=[pl.BlockSpec((tm, D), lambda i, j: (i, 0)),
                  pl.BlockSpec((None, D, tn), lambda i, j: (0, 0, j)),
                  pl.BlockSpec((None, D, tn), lambda i, j: (0, 0, j + nj))],
        out_specs=[o_spec, o_spec, o_spec],
        out_shape=[jax.ShapeDtypeStruct((T, dff), BF)] * 3, name=name,
        compiler_params=pltpu.CompilerParams(dimension_semantics=("parallel", "arbitrary"),
                                             vmem_limit_bytes=VMEM_CAP_BYTES),
    )(h, w3, w3)


def _ffn_da_swiglu(name, dxo, w3, zg, zu, tm):
    T, D = dxo.shape
    dff = w3.shape[1]
    tm = min(tm, T)
    assert T % tm == 0 and dff % 2 == 0
    hc = dff // 2

    nsteps = T // tm

    def kern(d_ref, w_ref, g_hbm, u_hbm, dz_ref, g_ring, u_ring, g_sem, u_sem):
        i = pl.program_id(0)

        def copies(step, slot):
            rows = pl.ds(pl.multiple_of(step * tm, tm), tm)
            return (pltpu.make_async_copy(g_hbm.at[rows, :], g_ring.at[slot], g_sem.at[slot]),
                    pltpu.make_async_copy(u_hbm.at[rows, :], u_ring.at[slot], u_sem.at[slot]))

        @pl.when(i == 0)
        def _():
            for first in range(min(2, nsteps)):
                for cp in copies(first, first):
                    cp.start()

        @pl.when(i + 2 < nsteps)
        def _():
            for cp in copies(i + 2, (i + 2) % 3):
                cp.start()

        slot = i % 3
        for cp in copies(i, slot):
            cp.wait()
        g_ref, u_ref = g_ring.at[slot], u_ring.at[slot]
        db = (d_ref[...] * 0.5).astype(BF)
        for s in range(2):
            cols = slice(s * hc, (s + 1) * hc)
            da = lax.dot_general(db, w_ref[cols, :], (((1,), (1,)), ((), ())), preferred_element_type=F32)
            g = g_ref[:, cols].astype(F32)
            sg = 1.0 / (1.0 + jnp.exp(-g))
            gs = g * sg
            dab = da.astype(BF)
            dz_ref[:, cols] = (dab * u_ref[:, cols]) * (sg + gs * (1.0 - sg)).astype(BF)
            dz_ref[:, dff + s * hc:dff + (s + 1) * hc] = dab * gs.astype(BF)

    row = lambda w: pl.BlockSpec((tm, w), lambda i: (i, 0))
    whole = pl.BlockSpec(memory_space=pl.ANY)
    return pl.pallas_call(
        kern, grid=(nsteps,),
        in_specs=[row(D), pl.BlockSpec((None, dff, D), lambda i: (0, 0, 0), pipeline_mode=pl.Buffered(1)), whole, whole],
        out_specs=row(2 * dff), out_shape=jax.ShapeDtypeStruct((T, 2 * dff), BF), name=name,
        scratch_shapes=[pltpu.VMEM((3, tm, dff), BF), pltpu.VMEM((3, tm, dff), BF),
                        pltpu.SemaphoreType.DMA((3,)), pltpu.SemaphoreType.DMA((3,))],
        compiler_params=pltpu.CompilerParams(dimension_semantics=("arbitrary",), vmem_limit_bytes=VMEM_CAP_BYTES),
    )(dxo, w3, zg, zu)


def _mm_dh_rms(name, dz, w3, xin, g, dres, tm):
    T, K = dz.shape
    D = w3.shape[1]
    tm = min(tm, T)
    assert T % tm == 0

    nsteps = T // tm

    def kern(dz_hbm, w_ref, x_ref, g_ref, r_ref, dx_ref, dg_ref, ring, sem):
        i = pl.program_id(0)

        def copy(step, slot):
            return pltpu.make_async_copy(dz_hbm.at[pl.ds(pl.multiple_of(step * tm, tm), tm), :], ring.at[slot], sem.at[slot])

        @pl.when(i == 0)
        def _():
            for first in range(min(2, nsteps)):
                copy(first, first).start()

        @pl.when(i + 2 < nsteps)
        def _():
            copy(i + 2, (i + 2) % 3).start()

        slot = i % 3
        copy(i, slot).wait()
        dh = lax.dot_general(ring[slot], w_ref[...], (((1,), (1,)), ((), ())), preferred_element_type=F32)
        _, vjp = jax.vjp(_rmsnorm, x_ref[...], g_ref[...])
        dx, dg = vjp(dh)
        dx_ref[...] = dx + r_ref[...]

        @pl.when(pl.program_id(0) == 0)
        def _():
            dg_ref[...] = jnp.zeros(dg_ref.shape, F32)

        dg_ref[...] += dg

    row = lambda w: pl.BlockSpec((tm, w), lambda i: (i, 0))
    one = pl.BlockSpec((1, D), lambda i: (0, 0))
    return pl.pallas_call(
        kern, grid=(nsteps,),
        in_specs=[pl.BlockSpec(memory_space=pl.ANY),
                  pl.BlockSpec((None, D, K), lambda i: (0, 0, 0), pipeline_mode=pl.Buffered(1)), row(D), one, row(D)],
        out_specs=[row(D), one], out_shape=[jax.ShapeDtypeStruct((T, D), F32), jax.ShapeDtypeStruct((1, D), F32)], name=name,
        scratch_shapes=[pltpu.VMEM((3, tm, K), BF), pltpu.SemaphoreType.DMA((3,))],
        compiler_params=pltpu.CompilerParams(dimension_semantics=("arbitrary",), vmem_limit_bytes=VMEM_CAP_BYTES),
    )(dz, w3, xin, g, dres)


def _mm_tn_pair(name, a, b, kind, c_arr, tq, tk, scale=1.0):
    T, M = a.shape
    _, N = b.shape
    tk = min(tk, T)
    assert T % tk == 0
    nk = T // tk
    if kind == "col":
        hm = M // 2
        assert N % tq == 0
        nq = N // tq
        tile = (hm, tq)
        a_spec = pl.BlockSpec((tk, hm), lambda h, q, k, c: (k, jnp.bitwise_xor(h, 1 - c[0])))
        b_spec = pl.BlockSpec((tk, tq), lambda h, q, k, c: (k, q))
        o_spec = pl.BlockSpec(tile, lambda h, q, k, c: (0, q * h))
        out_sd = (hm, N)
    else:
        hn = N // 2
        assert M % tq == 0
        nq = M // tq
        tile = (tq, hn)
        a_spec = pl.BlockSpec((tk, tq), lambda h, q, k, c: (k, q))
        b_spec = pl.BlockSpec((tk, hn), lambda h, q, k, c: (k, jnp.bitwise_xor(h, 1 - c[0])))
        o_spec = pl.BlockSpec(tile, lambda h, q, k, c: (q * h, 0))
        out_sd = (M, hn)

    def kern(c_ref, a_ref, b_ref, o_ref, acc, stage, recv, ssem, rsem):
        h, q, k = pl.program_id(0), pl.program_id(1), pl.program_id(2)
        x, y, c, _ = _place()
        p = lax.dot_general(a_ref[...].astype(BF), b_ref[...].astype(BF), (((0,), (0,)), ((), ())), preferred_element_type=F32)

        @pl.when(k == 0)
        def _():
            acc[...] = p

        @pl.when(k > 0)
        def _():
            acc[...] += p

        def send(slot, qq):
            return pltpu.make_async_remote_copy(src_ref=stage.at[slot], dst_ref=recv.at[qq], send_sem=ssem.at[slot],
                                                recv_sem=rsem.at[qq], device_id=(x, y, 1 - c), device_id_type=MESH)

        last = k == nk - 1

        @pl.when(jnp.logical_and(last, h == 0))
        def _():
            slot = q % 2

            @pl.when(q >= 2)
            def _():
                send(slot, q).wait_send()

            stage[slot] = (acc[...] * scale).astype(BF)
            send(slot, q).start()

        @pl.when(jnp.logical_and(last, h == 1))
        def _():
            @pl.when(q == 0)
            def _():
                for s in range(min(nq, 2)):
                    send(s, 0).wait_send()

            send(0, q).wait_recv()
            o_ref[...] = (acc[...] * scale + recv[q].astype(F32)).astype(o_ref.dtype)

    return pl.pallas_call(
        kern,
        grid_spec=pltpu.PrefetchScalarGridSpec(
            num_scalar_prefetch=1, grid=(2, nq, nk), in_specs=[a_spec, b_spec], out_specs=o_spec,
            scratch_shapes=[pltpu.VMEM(tile, F32), pltpu.VMEM((2,) + tile, BF), pltpu.VMEM((nq,) + tile, BF),
                            pltpu.SemaphoreType.DMA((2,)), pltpu.SemaphoreType.DMA((nq,))]),
        out_shape=jax.ShapeDtypeStruct(out_sd, BF), name=name,
        compiler_params=pltpu.CompilerParams(dimension_semantics=("arbitrary", "arbitrary", "arbitrary"),
                                             vmem_limit_bytes=VMEM_CAP_BYTES),
    )(c_arr, a, b)


def _kv_pieces(kv_ref):
    W = XA_HEADS * XA_DIM
    mk = [kv_ref[:, a * XA_DIM:(a + 1) * XA_DIM] for a in range(XA_HEADS)]
    mv = [kv_ref[:, W + a * XA_DIM:W + (a + 1) * XA_DIM] for a in range(XA_HEADS)]
    return mk, mv


HG_SUB = 4


def _hgrn_rows(z_ref):
    W = HG_HEADS * HG_DIM

    def piece(c, col, w):
        return z_ref[c * HG_CHUNK:(c + 1) * HG_CHUNK, col:col + w]

    zq = [[piece(c, h * HG_DIM, HG_DIM) for h in range(HG_HEADS)] for c in range(HG_SUB)]
    zf = z_ref[:, W:2 * W]
    zi =[[piece(c, 2 * W + h * HG_DIM, HG_DIM) for h in range(HG_HEADS)] for c in range(HG_SUB)]
    zg = [[piece(c, 3 * W + h * HG_DIM, HG_DIM) for h in range(HG_HEADS)] for c in range(HG_SUB)]
    zx = [z_ref[:, 4 * W + a * XA_DIM:4 * W + (a + 1) * XA_DIM] for a in range(XA_HEADS)]
    return zq, zf, zi, zg, zx


def _hgrn_steps(zq, zf, zi, zg, zx, lb3, gn, mk, mv, S):
    f, b = _hgrn_decays(zf, lb3)
    mix = []
    for c in range(HG_SUB):
        row, s_next = [], []
        rows = slice(c * HG_CHUNK, (c + 1) * HG_CHUNK)
        for h in range(HG_HEADS):
            cols = slice(h * HG_DIM, (h + 1) * HG_DIM)
            o, sn = _hgrn_head(zq[c][h], f[rows, cols], b[rows, cols], zi[c][h], zg[c][h], gn, S[h])
            row.append(o)
            s_next.append(sn)
        mix.append(row)
        S = s_next
    att = [_attention(zx[a], mk[a], mv[a]) for a in range(XA_HEADS)]
    return mix, att, S


def _hgrn_fwd2(z, lb_logits, gnorm, kv, bl, seq):
    T, zw = z.shape
    mem_len = kv.shape[0] // bl
    cat_w = HG_HEADS * HG_DIM + XA_HEADS * XA_DIM
    R = HG_SUB * HG_CHUNK
    nb = seq // R

    def kern(z_ref, lb_ref, gn_ref, kv_ref, cat_ref, st_ref, s_scr):
        @pl.when(pl.program_id(1) == 0)
        def _():
            s_scr[...] = jnp.zeros(s_scr.shape, F32)

        st_ref[...] = s_scr[...]
        zq, zf, zi, zg, zx = _hgrn_rows(z_ref)
        mk, mv = _kv_pieces(kv_ref)
        S = [s_scr[h] for h in range(HG_HEADS)]
        mix, att, s_new = _hgrn_steps(zq, zf, zi, zg, zx, lb_ref[...], gn_ref[...], mk, mv, S)
        for c in range(HG_SUB):
            for h in range(HG_HEADS):
                cat_ref[c * HG_CHUNK:(c + 1) * HG_CHUNK, h * HG_DIM:(h + 1) * HG_DIM] = mix[c][h].astype(cat_ref.dtype)
        for h in range(HG_HEADS):
            s_scr[h] = s_new[h]
        base = HG_HEADS * HG_DIM
        for a in range(XA_HEADS):
            cat_ref[:, base + a * XA_DIM:base + (a + 1) * XA_DIM] = att[a].astype(cat_ref.dtype)

    return pl.pallas_call(
        kern, grid=(bl, nb),
        in_specs=[pl.BlockSpec((R, zw), lambda b, n: (b * nb + n, 0)),
                  pl.BlockSpec(lb_logits.shape, lambda b, n: (0, 0)),
                  pl.BlockSpec(gnorm.shape, lambda b, n: (0, 0)),
                  pl.BlockSpec((mem_len, kv.shape[1]), lambda b, n: (b, 0))],
        out_specs=[pl.BlockSpec((R, cat_w), lambda b, n: (b * nb + n, 0)),
                   pl.BlockSpec((None, HG_HEADS, HG_DIM, HG_DIM), lambda b, n: (b * nb + n, 0, 0, 0))],
        out_shape=[jax.ShapeDtypeStruct((T, cat_w), BF),
                   jax.ShapeDtypeStruct((bl * nb, HG_HEADS, HG_DIM, HG_DIM), F32)],
        scratch_shapes=[pltpu.VMEM((HG_HEADS, HG_DIM, HG_DIM), F32)],
        name="hgrn_fwd",
        compiler_params=pltpu.CompilerParams(dimension_semantics=("arbitrary", "arbitrary"), vmem_limit_bytes=VMEM_CAP_BYTES),
    )(z, lb_logits, gnorm, kv)


def _hgrn_bwd2(z, dcat, stash, lb_logits, gnorm, kv, bl, seq):
    T, zw = z.shape
    mem_len = kv.shape[0] // bl
    cat_w = dcat.shape[1]
    R = HG_SUB * HG_CHUNK
    nb = seq // R

    def kern(z_ref, dc_ref, st_ref, lb_ref, gn_ref, kv_ref, dz_ref, dkv_ref, dlb_ref, dgn_ref, ds_scr):
        first = jnp.logical_and(pl.program_id(0) == 0, pl.program_id(1) == 0)

        @pl.when(pl.program_id(1) == 0)
        def _():
            ds_scr[...] = jnp.zeros(ds_scr.shape, F32)
            dkv_ref[...] = jnp.zeros(dkv_ref.shape, F32)

        @pl.when(first)
        def _():
            dlb_ref[...] = jnp.zeros(dlb_ref.shape, F32)
            dgn_ref[...] = jnp.zeros(dgn_ref.shape, F32)

        zq, zf, zi, zg, zx = _hgrn_rows(z_ref)
        mk, mv = _kv_pieces(kv_ref)
        S = [st_ref[h] for h in range(HG_HEADS)]
        _, vjp = jax.vjp(_hgrn_steps, zq, zf, zi, zg, zx, lb_ref[...], gn_ref[...], mk, mv, S)
        d_mix = [[dc_ref[c * HG_CHUNK:(c + 1) * HG_CHUNK, h * HG_DIM:(h + 1) * HG_DIM] for h in range(HG_HEADS)]
                 for c in range(HG_SUB)]
        base = HG_HEADS * HG_DIM
        d_att = [dc_ref[:, base + a * XA_DIM:base + (a + 1) * XA_DIM] for a in range(XA_HEADS)]
        d_s = [ds_scr[h] for h in range(HG_HEADS)]
        dzq, dzf, dzi, dzg, dzx, dlb3, dgn, dmk, dmv, dS = vjp((d_mix, d_att, d_s))
        W = HG_HEADS * HG_DIM
        dz_ref[:, W:2 * W] = dzf.astype(dz_ref.dtype)
        for c in range(HG_SUB):
            rows = slice(c * HG_CHUNK, (c + 1) * HG_CHUNK)
            for h in range(HG_HEADS):
                for k, part in ((0, dzq), (2, dzi), (3, dzg)):
                    dz_ref[rows, k * W + h * HG_DIM:k * W + (h + 1) * HG_DIM] = part[c][h].astype(dz_ref.dtype)
        for h in range(HG_HEADS):
            ds_scr[h] = dS[h]
        dlb_ref[...] += dlb3
        dgn_ref[...] += dgn
        KW = XA_HEADS * XA_DIM
        for a in range(XA_HEADS):
            dz_ref[:, 4 * W + a * XA_DIM:4 * W + (a + 1) * XA_DIM] = dzx[a].astype(dz_ref.dtype)
            dkv_ref[:, a * XA_DIM:(a + 1) * XA_DIM] += dmk[a]
            dkv_ref[:, KW + a * XA_DIM:KW + (a + 1) * XA_DIM] += dmv[a]

    rev = lambda b, n: (b * nb + (nb - 1 - n), 0)
    return pl.pallas_call(
        kern, grid=(bl, nb),
        in_specs=[pl.BlockSpec((R, zw), rev),
                  pl.BlockSpec((R, cat_w), rev),
                  pl.BlockSpec((None, HG_HEADS, HG_DIM, HG_DIM), lambda b, n: (b * nb + (nb - 1 - n), 0, 0, 0)),
                  pl.BlockSpec(lb_logits.shape, lambda b, n: (0, 0)),
                  pl.BlockSpec(gnorm.shape, lambda b, n: (0, 0)),
                  pl.BlockSpec((mem_len, kv.shape[1]), lambda b, n: (b, 0))],
        out_specs=[pl.BlockSpec((R, zw), rev),
                   pl.BlockSpec((mem_len, kv.shape[1]), lambda b, n: (b, 0)),
                   pl.BlockSpec(lb_logits.shape, lambda b, n: (0, 0)),
                   pl.BlockSpec(gnorm.shape, lambda b, n: (0, 0))],
        out_shape=[jax.ShapeDtypeStruct((T, zw), BF), jax.ShapeDtypeStruct(kv.shape, F32),
                   jax.ShapeDtypeStruct(lb_logits.shape, F32), jax.ShapeDtypeStruct(gnorm.shape, F32)],
        scratch_shapes=[pltpu.VMEM((HG_HEADS, HG_DIM, HG_DIM), F32)],
        name="hgrn_bwd",
        compiler_params=pltpu.CompilerParams(dimension_semantics=("arbitrary", "arbitrary"), vmem_limit_bytes=VMEM_CAP_BYTES),
    )(z, dcat, stash, lb_logits, gnorm, kv)


GM_SUB = 2


def _gmlp_pieces(z_ref):
    W = GM_GROUPS * GM_GROUP_DIM
    zu = [z_ref[:, g * GM_GROUP_DIM:(g + 1) * GM_GROUP_DIM] for g in range(GM_GROUPS)]
    zv = [z_ref[:, W + g * GM_GROUP_DIM:W + (g + 1) * GM_GROUP_DIM] for g in range(GM_GROUPS)]
    zx = [z_ref[:, 2 * W + a * XA_DIM:2 * W + (a + 1) * XA_DIM] for a in range(XA_HEADS)]
    return zu, zv, zx


def _gmlp_params(lng_ref, lnb_ref, ws_ref, bs_ref):
    lng = [lng_ref[:, g * GM_GROUP_DIM:(g + 1) * GM_GROUP_DIM] for g in range(GM_GROUPS)]
    lnb = [lnb_ref[:, g * GM_GROUP_DIM:(g + 1) * GM_GROUP_DIM] for g in range(GM_GROUPS)]
    ws = [ws_ref[g] for g in range(GM_GROUPS)]
    bs = [bs_ref[g:g + 1, :] for g in range(GM_GROUPS)]
    return lng, lnb, ws, bs


def _gmlp_fwd(z, ln_g, ln_b, w_s, b_s, kv, bl, nc):
    T, zw = z.shape
    mem_len = kv.shape[0] // bl
    cat_w = GM_GROUPS * GM_GROUP_DIM + XA_HEADS * XA_DIM

    assert nc % GM_SUB == 0
    nc = nc // GM_SUB
    R = GM_SUB * GM_CHUNK

    def kern(z_ref, lng_ref, lnb_ref, ws_ref, bs_ref, kv_ref, cat_ref):
        lng, lnb, ws, bs = _gmlp_params(lng_ref, lnb_ref, ws_ref, bs_ref)
        mk, mv = _kv_pieces(kv_ref)
        for c in range(GM_SUB):
            rows = pl.ds(c * GM_CHUNK, GM_CHUNK)
            zu, zv, zx = _gmlp_pieces(z_ref.at[rows])
            out = cat_ref.at[rows]
            outs = _gmlp_block(zu, zv, zx, lng, lnb, ws, bs, mk, mv)
            for g in range(GM_GROUPS):
                out[:, g * GM_GROUP_DIM:(g + 1) * GM_GROUP_DIM] = outs[g].astype(cat_ref.dtype)
            base = GM_GROUPS * GM_GROUP_DIM
            for a in range(XA_HEADS):
                out[:, base + a * XA_DIM:base + (a + 1) * XA_DIM] = outs[GM_GROUPS + a].astype(cat_ref.dtype)

    full2 = lambda b, n: (0, 0)
    return pl.pallas_call(
        kern, grid=(bl, nc),
        in_specs=[pl.BlockSpec((R, zw), lambda b, n: (b * nc + n, 0)),
                  pl.BlockSpec(ln_g.shape, full2), pl.BlockSpec(ln_b.shape, full2),
                  pl.BlockSpec(w_s.shape, lambda b, n: (0, 0, 0)), pl.BlockSpec(b_s.shape, full2),
                  pl.BlockSpec((mem_len, kv.shape[1]), lambda b, n: (b, 0))],
        out_specs=pl.BlockSpec((R, cat_w), lambda b, n: (b * nc + n, 0)),
        out_shape=jax.ShapeDtypeStruct((T, cat_w), BF),
        name="gmlp_fwd",
        compiler_params=pltpu.CompilerParams(dimension_semantics=("arbitrary", "arbitrary"), vmem_limit_bytes=VMEM_CAP_BYTES),
    )(z, ln_g, ln_b, w_s, b_s, kv)


def _gmlp_bwd(z, dcat, ln_g, ln_b, w_s, b_s, kv, bl, nc):
    T, zw = z.shape
    mem_len = kv.shape[0] // bl
    cat_w = dcat.shape[1]
    assert nc % GM_SUB == 0
    nc = nc // GM_SUB

    def kern(z_ref, dc_ref, lng_ref, lnb_ref, ws_ref, bs_ref, kv_ref,
             dz_ref, dkv_ref, dlng_ref, dlnb_ref, dws_ref, dbs_ref):
        first = jnp.logical_and(pl.program_id(0) == 0, pl.program_id(1) == 0)

        @pl.when(pl.program_id(1) == 0)
        def _():
            dkv_ref[...] = jnp.zeros(dkv_ref.shape, F32)

        @pl.when(first)
        def _():
            dlng_ref[...] = jnp.zeros(dlng_ref.shape, F32)
            dlnb_ref[...] = jnp.zeros(dlnb_ref.shape, F32)
            dws_ref[...] = jnp.zeros(dws_ref.shape, F32)
            dbs_ref[...] = jnp.zeros(dbs_ref.shape, F32)

        lng, lnb, ws, bs = _gmlp_params(lng_ref, lnb_ref, ws_ref, bs_ref)
        mk, mv = _kv_pieces(kv_ref)
        W = GM_GROUPS * GM_GROUP_DIM
        KW = XA_HEADS * XA_DIM
        for c in range(GM_SUB):
            rows = pl.ds(c * GM_CHUNK, GM_CHUNK)
            zu, zv, zx = _gmlp_pieces(z_ref.at[rows])
            dc, dz = dc_ref.at[rows], dz_ref.at[rows]
            _, vjp = jax.vjp(_gmlp_block, zu, zv, zx, lng, lnb, ws, bs, mk, mv)
            d_outs = [dc[:, g * GM_GROUP_DIM:(g + 1) * GM_GROUP_DIM] for g in range(GM_GROUPS)]
            d_outs += [dc[:, W + a * XA_DIM:W + (a + 1) * XA_DIM] for a in range(XA_HEADS)]
            dzu, dzv, dzx, dlng, dlnb, dws, dbs, dmk, dmv = vjp(d_outs)
            for g in range(GM_GROUPS):
                sl = slice(g * GM_GROUP_DIM, (g + 1) * GM_GROUP_DIM)
                dz[:, sl] = dzu[g].astype(dz_ref.dtype)
                dz[:, W + g * GM_GROUP_DIM:W + (g + 1) * GM_GROUP_DIM] = dzv[g].astype(dz_ref.dtype)
                dlng_ref[:, sl] += dlng[g]
                dlnb_ref[:, sl] += dlnb[g]
                dws_ref[g] += dws[g]
                dbs_ref[g:g + 1, :] += dbs[g]
            for a in range(XA_HEADS):
                dz[:, 2 * W + a * XA_DIM:2 * W + (a + 1) * XA_DIM] = dzx[a].astype(dz_ref.dtype)
                dkv_ref[:, a * XA_DIM:(a + 1) * XA_DIM] += dmk[a]
                dkv_ref[:, KW + a * XA_DIM:KW + (a + 1) * XA_DIM] += dmv[a]

    full2 = lambda b, n: (0, 0)
    full3 = lambda b, n: (0, 0, 0)
    blk = lambda b, n: (b * nc + n, 0)
    return pl.pallas_call(
        kern, grid=(bl, nc),
        in_specs=[pl.BlockSpec((GM_SUB * GM_CHUNK, zw), blk), pl.BlockSpec((GM_SUB * GM_CHUNK, cat_w), blk),
                  pl.BlockSpec(ln_g.shape, full2), pl.BlockSpec(ln_b.shape, full2),
                  pl.BlockSpec(w_s.shape, full3), pl.BlockSpec(b_s.shape, full2),
                  pl.BlockSpec((mem_len, kv.shape[1]), lambda b, n: (b, 0))],
        out_specs=[pl.BlockSpec((GM_SUB * GM_CHUNK, zw), blk),
                   pl.BlockSpec((mem_len, kv.shape[1]), lambda b, n: (b, 0)),
                   pl.BlockSpec(ln_g.shape, full2), pl.BlockSpec(ln_b.shape, full2),
                   pl.BlockSpec(w_s.shape, full3), pl.BlockSpec(b_s.shape, full2)],
        out_shape=[jax.ShapeDtypeStruct((T, zw), BF), jax.ShapeDtypeStruct(kv.shape, F32),
                   jax.ShapeDtypeStruct(ln_g.shape, F32), jax.ShapeDtypeStruct(ln_b.shape, F32),
                   jax.ShapeDtypeStruct(w_s.shape, F32), jax.ShapeDtypeStruct(b_s.shape, F32)],
        name="gmlp_bwd",
        compiler_params=pltpu.CompilerParams(dimension_semantics=("arbitrary", "arbitrary"), vmem_limit_bytes=VMEM_CAP_BYTES),
    )(z, dcat, ln_g, ln_b, w_s, b_s, kv)


def _place():
    x, y, c = lax.axis_index("x"), lax.axis_index("y"), lax.axis_index("c")
    chips = [(1 - x, y), (x, 1 - y), (1 - x, 1 - y)]
    return x, y, c, chips


def _half(ref, kind, e):
    if kind == "col":
        n = ref.shape[1] // 2
        return ref.at[:, pl.ds(pl.multiple_of(e * n, n), n), :]
    n = ref.shape[2] // 2
    return ref.at[:, :, pl.ds(pl.multiple_of(e * n, n), n)]


def _slot(ref, kind, j, n):
    if kind == "col":
        return ref.at[:, :, pl.ds(pl.multiple_of(j * n, n), n)]
    return ref.at[:, pl.ds(pl.multiple_of(j * n, n), n), :]


def _allgather_seq(name, items, cid):
    nt = len(items)
    kinds = [k for (_, k, _) in items]
    slot_kind = ["row" if k == "row" else "col" for k in kinds]
    out_type = []
    for s, k, l in items:
        L, r, c = s.shape
        lo = L if l is None else 1
        out_type.append(jax.ShapeDtypeStruct((lo, 4 * r, c) if k == "row" else (lo, r, 4 * c), s.dtype))

    def part(ref, t, e):
        return ref if kinds[t] == "vec" else _half(ref, kinds[t], e)

    def body(*refs):
        sh = [refs[t] if items[t][2] is None else refs[t].at[pl.ds(items[t][2], 1)] for t in range(nt)]
        full = refs[nt:2 * nt]
        s_ici, r_ici, s_d2d, r_d2d = refs[2 * nt:]
        x, y, c, chips = _place()
        own = 2 * x + y
        sibling = (x, y, 1 - c)
        barrier = pltpu.get_barrier_semaphore()
        for peer in [(px, py, c) for (px, py) in chips] + [sibling]:
            pl.semaphore_signal(barrier, inc=1, device_id=peer, device_id_type=MESH)
        pl.semaphore_wait(barrier, 4)
        width = [sh[t].shape[1] if kinds[t] == "row" else sh[t].shape[2] for t in range(nt)]
        sent = []
        for t in range(nt):
            for p, (px, py) in enumerate(chips):
                cp = pltpu.make_async_remote_copy(
                    src_ref=part(sh[t], t, c), dst_ref=part(_slot(full[t], slot_kind[t], own, width[t]), t, c),
                    send_sem=s_ici.at[t, p], recv_sem=r_ici.at[t, p], device_id=(px, py, c), device_id_type=MESH)
                cp.start()
                sent.append(cp)
        for t in range(nt):
            for p, (px, py) in enumerate(chips):
                landed = part(_slot(full[t], slot_kind[t], 2 * px + py, width[t]), t, c)
                pltpu.make_async_remote_copy(
                    src_ref=landed, dst_ref=landed, send_sem=s_ici.at[t, p], recv_sem=r_ici.at[t, p],
                    device_id=(px, py, c), device_id_type=MESH).wait_recv()
                if kinds[t] == "vec":
                    continue
                fw = pltpu.make_async_remote_copy(
                    src_ref=landed, dst_ref=landed, send_sem=s_d2d.at[t, p], recv_sem=r_d2d.at[t, p],
                    device_id=sibling, device_id_type=MESH)
                fw.start()
                sent.append(fw)
        for t in range(nt):
            if kinds[t] == "vec":
                continue
            for p, (px, py) in enumerate(chips):
                other = _half(_slot(full[t], kinds[t], 2 * px + py, width[t]), kinds[t], 1 - c)
                pltpu.make_async_remote_copy(
                    src_ref=other, dst_ref=other, send_sem=s_d2d.at[t, p], recv_sem=r_d2d.at[t, p],
                    device_id=sibling, device_id_type=MESH).wait_recv()
        for cp in sent:
            cp.wait_send()

    sems = pltpu.SemaphoreType.DMA
    return pl.kernel(
        body, out_type=out_type, mesh=plsc.ScalarSubcoreMesh(axis_name="seq", num_cores=1),
        scratch_types=[sems((nt, 3)), sems((nt, 3)), sems((nt, 3)), sems((nt, 3))],
        compiler_params=pltpu.CompilerParams(collective_id=cid), name=name,
    )(*[s for (s, _, _) in items])


def _place_own(name, full, shard, kind, layer, chip_arr, after):
    lo, r, c = (shard.shape[0] if layer is None else 1,) + shard.shape[1:]
    first = 0 if layer is None else layer
    tr = _pick(r, 512)
    nr = r // tr

    def body(chip_ref, s_ref, f_ref, after_ref, o_ref):
        o_ref[...] = s_ref[...]

    if kind == "row":
        out_map = lambda i, j, chip: (i, chip[0] * nr + j, 0)
    else:
        out_map = lambda i, j, chip: (i, j, chip[0])
    return pl.pallas_call(
        body, out_shape=jax.ShapeDtypeStruct(full.shape, full.dtype),
        grid_spec=pltpu.PrefetchScalarGridSpec(
            num_scalar_prefetch=1, grid=(lo, nr),
            in_specs=[pl.BlockSpec((1, tr, c), lambda i, j, chip: (i + first, j, 0)), pl.BlockSpec(memory_space=pl.ANY),
                      pl.BlockSpec(memory_space=pl.ANY)],
            out_specs=pl.BlockSpec((1, tr, c), out_map)),
        input_output_aliases={2: 0},
        compiler_params=pltpu.CompilerParams(dimension_semantics=("parallel", "parallel"), vmem_limit_bytes=VMEM_CAP_BYTES),
        name=name,
    )(chip_arr, shard, full, after)


def _slot2(ref, kind, j, n):
    if kind == "col":
        return ref.at[:, pl.ds(pl.multiple_of(j * n, n), n)]
    return ref.at[pl.ds(pl.multiple_of(j * n, n), n), :]


def _rs_chips_seq(name, parts, kinds, cid):
    nm = len(parts)
    out_type = []
    for g, k in zip(parts, kinds):
        r, c = g.shape
        ps = (r, c // 4) if k == "col" else (r // 4, c)
        out_type += [jax.ShapeDtypeStruct(ps, BF), jax.ShapeDtypeStruct((3,) + ps, BF)]

    def body(*refs):
        g = refs[:nm]
        outs = refs[nm:3 * nm]
        loc, ssem, rsem = refs[3 * nm:]
        x, y, c, chips = _place()
        own = 2 * x + y
        barrier = pltpu.get_barrier_semaphore()
        for (px, py) in chips:
            pl.semaphore_signal(barrier, inc=1, device_id=(px, py, c), device_id_type=MESH)
        pl.semaphore_wait(barrier, 3)
        cps = []
        for m in range(nm):
            k = kinds[m]
            own_o, got_o = outs[2 * m], outs[2 * m + 1]
            n = g[m].shape[1] // 4 if k == "col" else g[m].shape[0] // 4
            lc = pltpu.make_async_copy(_slot2(g[m], k, own, n), own_o, loc.at[m])
            lc.start()
            cps.append(lc)
            for p, (px, py) in enumerate(chips):
                cp = pltpu.make_async_remote_copy(
                    src_ref=_slot2(g[m], k, 2 * px + py, n), dst_ref=got_o.at[p],
                    send_sem=ssem.at[m, p], recv_sem=rsem.at[m, p], device_id=(px, py, c), device_id_type=MESH)
                cp.start()
                cps.append(cp)
        for cp in cps:
            cp.wait()

    return pl.kernel(
        body, out_type=out_type, mesh=plsc.ScalarSubcoreMesh(axis_name="seq", num_cores=1),
        scratch_types=[pltpu.SemaphoreType.DMA((nm,)), pltpu.SemaphoreType.DMA((nm, 3)), pltpu.SemaphoreType.DMA((nm, 3))],
        compiler_params=pltpu.CompilerParams(collective_id=cid), name=name,
    )(*parts)


def _finish_share(name, owns, gots, kind, c_arr):
    L = len(owns)
    r, c = owns[0].shape
    tr = _pick(r, 128 if kind == "col" else 256)
    nb = r // tr
    nq = L * nb

    def chunk_of(l):
        return lambda h, q: jnp.clip(q * (1 - h) + (nq - 1) * h - l * nb, 0, nb - 1)

    ins, in_specs = [], []
    for l in range(L):
        at = chunk_of(l)
        ins += [owns[l], gots[l].reshape(3 * r, c), gots[l].reshape(3 * r, c), gots[l].reshape(3 * r, c)]
        in_specs.append(pl.BlockSpec((tr, c), functools.partial(lambda h, q, cc, at: (at(h, q), 0), at=at)))
        in_specs += [pl.BlockSpec((tr, c), functools.partial(lambda h, q, cc, at, p: (p * nb + at(h, q), 0), at=at, p=p))
                     for p in range(3)]
    if kind == "col":
        out_sd = (L, 2, r, c)
        o_spec = pl.BlockSpec((None, 2, tr, c), lambda h, q, cc: ((q * h) // nb, 0, (q * h) % nb, 0))
    else:
        out_sd = (L * r, 2 * c)
        o_spec = pl.BlockSpec((tr, 2 * c), lambda h, q, cc: (q * h, 0))

    def kern(c_ref, *refs):
        in_refs = refs[:4 * L]
        out_ref, mine, recv, ssem, rsem = refs[4 * L:]
        h, q = pl.program_id(0), pl.program_id(1)
        x, y, cc, _ = _place()

        def swap(qq):
            return pltpu.make_async_remote_copy(src_ref=mine.at[qq], dst_ref=recv.at[qq], send_sem=ssem.at[qq],
                                                recv_sem=rsem.at[qq], device_id=(x, y, 1 - cc), device_id_type=MESH)

        for l in range(L):
            @pl.when(jnp.logical_and(h == 0, q // nb == l))
            def _(l=l):
                o_ref, g0, g1, g2 = in_refs[4 * l:4 * l + 4]
                mine[q] = ((o_ref[...].astype(F32) + g0[...].astype(F32)) + g1[...].astype(F32)) + g2[...].astype(F32)
                swap(q).start()

        @pl.when(h == 1)
        def _():
            swap(q).wait()
            a, b = mine[q], recv[q]
            first = c_ref[0] == 0
            lo, hi = jnp.where(first, a, b), jnp.where(first, b, a)
            if kind == "col":
                out_ref[0] = lo
                out_ref[1] = hi
            else:
                out_ref[:, :c] = lo
                out_ref[:, c:] = hi

    full = pl.pallas_call(
        kern,
        grid_spec=pltpu.PrefetchScalarGridSpec(
            num_scalar_prefetch=1, grid=(2, nq), in_specs=in_specs, out_specs=o_spec,
            scratch_shapes=[pltpu.VMEM((nq, tr, c), F32), pltpu.VMEM((nq, tr, c), F32),
                            pltpu.SemaphoreType.DMA((nq,)), pltpu.SemaphoreType.DMA((nq,))]),
        out_shape=jax.ShapeDtypeStruct(out_sd, F32), name=name,
        compiler_params=pltpu.CompilerParams(dimension_semantics=("arbitrary", "arbitrary"),
                                             vmem_limit_bytes=VMEM_CAP_BYTES),
    )(c_arr, *ins)
    return full.reshape(L, 2 * r, c) if kind == "col" else full.reshape(L, r, 2 * c)


def _small_allreduce(buf, name):
    R = buf.shape[0]
    assert R % 16 == 0
    h = R // 2

    def body(x_ref, o_ref, sib, csum, got, s_a, r_a, s_b, r_b, s_c, r_c):
        x, y, c, chips = _place()
        sibling = (x, y, 1 - c)
        own = 2 * x + y
        swap = pltpu.make_async_remote_copy(src_ref=x_ref, dst_ref=sib, send_sem=s_a, recv_sem=r_a,
                                            device_id=sibling, device_id_type=MESH)
        swap.start()
        swap.wait()
        a, b = x_ref[...], sib[...]
        south = c == 0
        csum[...] = jnp.where(south, a, b) + jnp.where(south, b, a)
        lo = pl.multiple_of(c * h, 8)
        mine = csum.at[pl.ds(lo, h)]
        got[own] = csum[pl.ds(lo, h)]
        sends = []
        for p, (px, py) in enumerate(chips):
            cp = pltpu.make_async_remote_copy(src_ref=mine, dst_ref=got.at[own], send_sem=s_b.at[p], recv_sem=r_b.at[p],
                                              device_id=(px, py, c), device_id_type=MESH)
            cp.start()
            sends.append(cp)
        for cp in sends:
            cp.wait()
        o_ref[pl.ds(lo, h)] = ((got[0] + got[1]) + got[2]) + got[3]
        done = o_ref.at[pl.ds(lo, h)]
        back = pltpu.make_async_remote_copy(src_ref=done, dst_ref=done, send_sem=s_c, recv_sem=r_c,
                                            device_id=sibling, device_id_type=MESH)
        back.start()
        back.wait_send()
        other = o_ref.at[pl.ds(pl.multiple_of((1 - c) * h, 8), h)]
        pltpu.make_async_remote_copy(src_ref=other, dst_ref=other, send_sem=s_c, recv_sem=r_c,
                                     device_id=sibling, device_id_type=MESH).wait_recv()

    vm = pl.BlockSpec(memory_space=pltpu.VMEM)
    return pl.pallas_call(
        body, out_shape=jax.ShapeDtypeStruct(buf.shape, F32), in_specs=[vm], out_specs=vm,
        scratch_shapes=[pltpu.VMEM((R, LANES), F32), pltpu.VMEM((R, LANES), F32), pltpu.VMEM((4, h, LANES), F32),
                        pltpu.SemaphoreType.DMA, pltpu.SemaphoreType.DMA, pltpu.SemaphoreType.DMA((3,)),
                        pltpu.SemaphoreType.DMA((3,)), pltpu.SemaphoreType.DMA, pltpu.SemaphoreType.DMA],
        name=name,
        compiler_params=pltpu.CompilerParams(vmem_limit_bytes=VMEM_CAP_BYTES),
    )(buf)


PACK_TILE_ROWS = 8


def _item_rows(shape):
    n = 1
    for d in shape:
        n *= d
    return -(-n // (PACK_TILE_ROWS * LANES)) * PACK_TILE_ROWS


def _pack(arrs, rows_total):
    buf = jnp.zeros((rows_total, LANES), F32)
    r = 0
    for a in arrs:
        f = a.reshape(-1).astype(F32)
        nr = _item_rows(a.shape)
        block = jnp.pad(f, (0, nr * LANES - f.shape[0])).reshape(nr, LANES)
        buf = lax.dynamic_update_slice(buf, block, (r, 0))
        r += nr
    return buf


def _unpack(buf, shapes):
    out, r = [], 0
    for s in shapes:
        n = 1
        for d in s:
            n *= d
        nr = _item_rows(s)
        out.append(buf[r:r + nr].reshape(-1)[:n].reshape(s))
        r += nr
    return out


def _rows_needed(shapes):
    return -(-sum(_item_rows(s) for s in shapes) // (2 * PACK_TILE_ROWS)) * (2 * PACK_TILE_ROWS)


def _two_rows(a, b):
    out = jnp.zeros((2, a.shape[1]), a.dtype)
    return lax.dynamic_update_slice(lax.dynamic_update_slice(out, a, (0, 0)), b, (1, 0))


def _adam(w, g, m, v):
    m = ADAM_B1 * m + (1.0 - ADAM_B1) * g
    v = ADAM_B2 * v + (1.0 - ADAM_B2) * jnp.square(g)
    m_hat = m / (1.0 - ADAM_B1 ** ADAM_STEP)
    v_hat = v / (1.0 - ADAM_B2 ** ADAM_STEP)
    delta = -ADAM_LR * (m_hat / (jnp.sqrt(v_hat) + ADAM_EPS) + ADAM_WD * w)
    return delta, m, v


def _adam_call(name, w2, g2, m2, v2, tr, pass_grad=False):
    def fn(rv, cv):
        outs = list(_adam(*rv))
        return ([rv[1]] + outs if pass_grad else outs), []

    width = w2.shape[1]
    return _rowcall(name, fn, [(w2, 0, width), (g2, 0, width), (m2, 0, width), (v2, 0, width)], [],
                    [(width, F32)] * (4 if pass_grad else 3), [], tr)


def kernel(x, mem, mem_norm, lb_logits, ffn1_norm, ffn1_w_in, ffn1_w_out, mix_norm, mem_w_kv, hgrn_w_in, hgrn_gnorm, hgrn_w_out, gmlp_w_in, gmlp_ln_g, gmlp_ln_b, gmlp_w_s, gmlp_b_s, gmlp_w_out, ffn2_norm, ffn2_w_in, ffn2_w_out, final_norm, loss_target, m_mem_norm, m_lb_logits, m_ffn1_norm, m_ffn1_w_in, m_ffn1_w_out, m_mix_norm, m_mem_w_kv, m_hgrn_w_in, m_hgrn_gnorm, m_hgrn_w_out, m_gmlp_w_in, m_gmlp_ln_g, m_gmlp_ln_b, m_gmlp_w_s, m_gmlp_b_s, m_gmlp_w_out, m_ffn2_norm, m_ffn2_w_in, m_ffn2_w_out, m_final_norm, v_mem_norm, v_lb_logits, v_ffn1_norm, v_ffn1_w_in, v_ffn1_w_out, v_mix_norm, v_mem_w_kv, v_hgrn_w_in, v_hgrn_gnorm, v_hgrn_w_out, v_gmlp_w_in, v_gmlp_ln_g, v_gmlp_ln_b, v_gmlp_w_s, v_gmlp_b_s, v_gmlp_w_out, v_ffn2_norm, v_ffn2_w_in, v_ffn2_w_out, v_final_norm):
    bl, seq, D = x.shape
    T = bl * seq
    mem_len = mem.shape[1]
    chip = 2 * lax.axis_index("x") + lax.axis_index("y")
    c_arr = lax.axis_index("c").astype(jnp.int32).reshape(1)
    chip_arr = chip.astype(jnp.int32).reshape(1)
    TR = 1024

    big = [("ffn1_w_in", ffn1_w_in, "col"), ("ffn1_w_out", ffn1_w_out, "row"), ("mem_w_kv", mem_w_kv, "col"),
           ("hgrn_w_in", hgrn_w_in, "col"), ("hgrn_w_out", hgrn_w_out, "row"), ("gmlp_w_in", gmlp_w_in, "col"),
           ("gmlp_w_out", gmlp_w_out, "row"), ("ffn2_w_in", ffn2_w_in, "col"), ("ffn2_w_out", ffn2_w_out, "row")]
    kinds = [k for (_, _, k) in big]
    shards_bf = []
    for nm, w, _ in big:
        L, r, c = w.shape
        (wb,) = _rowcall("cast_" + nm, lambda rv, cv: ([rv[0]], []), [(w.reshape(L * r, c), 0, c)], [], [(c, BF)], [], 512)
        shards_bf.append(wb.reshape(L, r, c))
    sb = dict(zip([nm for (nm, _, _) in big], shards_bf))
    groups = [[("ffn1_w_in", 0)], [("ffn1_w_out", 0)], [("hgrn_w_in", None)], [("mem_w_kv", None)], [("hgrn_w_out", None)],
              [("ffn2_w_in", 0), ("ffn2_w_out", 0), ("gmlp_ln_g", None), ("gmlp_ln_b", None)],
              [("ffn1_w_in", 1), ("ffn1_w_out", 1)],
              [("gmlp_w_in", None), ("gmlp_w_out", None)],
              [("ffn2_w_in", 1), ("ffn2_w_out", 1)]]
    kind_of = {nm: k for (nm, _, k) in big}
    for nm, vec in (("gmlp_ln_g", gmlp_ln_g), ("gmlp_ln_b", gmlp_ln_b)):
        sb[nm] = vec.reshape(1, 1, -1)
        kind_of[nm] = "vec"
    gathered = {nm: [None, None] for nm in ("ffn1_w_in", "ffn1_w_out", "ffn2_w_in", "ffn2_w_out")}
    others = {}
    for gi, grp in enumerate(groups):
        outs = _allgather_seq("gather_%d" % gi, [(sb[nm], kind_of[nm], l) for (nm, l) in grp], gi)
        for (nm, l), o in zip(grp, outs):
            others[(nm, l)] = o

    def whole(nm, l, after):
        full = _place_own("own_%s_%d" % (nm, l or 0), others[(nm, l)], sb[nm], "row" if kind_of[nm] == "row" else "col", l,
                          chip_arr, after)
        if l is None:
            gathered[nm] = full
        else:
            gathered[nm][l] = full
        return full

    def rms_fwd(name, xin, g):
        (h,) = _rowcall(name, lambda rv, cv: ([_rmsnorm(rv[0], cv[0])], []), [(xin, 0, D)], [g.reshape(1, D)], [(D, BF)], [], TR)
        return h

    def ffn_fwd(tag, xin, h, nm_in, nm_out, layer, next_gain):
        w_in = whole(nm_in, layer, h)
        dff = w_in.shape[2] // 2
        zg, zu, a = _ffn_in_swiglu("ffn_in_" + tag, h, w_in, 1024, dff // 2)
        out = _mm("ffn_out_" + tag, a, whole(nm_out, layer, a), "nn", F32, 1024, 1024, dff, scale=0.5, res=xin, b_lead=0,
                  norm_gain=None if next_gain is None else next_gain.reshape(1, D))
        xo, h_next = (out, None) if next_gain is None else out
        return xo, h_next, (xin, h, zg, zu, a)

    def ffn_bwd(tag, dxo, saved, g, w_in, w_out, layer):
        xin, h, zg, zu, a = saved
        dff = w_out[layer].shape[1]
        dw_out = _mm_tn_pair("ffn_dwo_" + tag, a, dxo, "row", c_arr, dff // 2, T, scale=0.5)
        dz = _ffn_da_swiglu("ffn_da_" + tag, dxo, w_out[layer], zg, zu, 512)
        dw_in = _mm_tn_pair("ffn_dwi_" + tag, h, dz, "col", c_arr, 512, T)
        dx, dg = _mm_dh_rms("ffn_dh_" + tag, dz, w_in[layer], xin, g.reshape(1, D), dxo, 512)
        return dx, dg, dw_in, dw_out

    def rms_bwd(name, xin, g, dh, dres):
        def fn(rv, cv):
            _, vjp = jax.vjp(_rmsnorm, rv[0], cv[0])
            dx, dg = vjp(rv[1])
            if dres is not None:
                dx = dx + rv[2]
            return [dx], [dg]

        rows = [(xin, 0, D), (dh, 0, D)] + ([(dres, 0, D)] if dres is not None else [])
        dx, dg = _rowcall(name, fn, rows, [g.reshape(1, D)], [(D, F32)], [((1, D), F32)], TR)
        return dx, dg

    x0 = x.reshape(T, D)
    tgt = loss_target.reshape(T, D)
    mem2 = mem.reshape(bl * mem_len, D)
    memn = rms_fwd("rms_mem", mem2, mem_norm)

    h_f10 = rms_fwd("rms_f1l0", x0, ffn1_norm[0])
    x1, h_m0, sv_f10 = ffn_fwd("f1l0", x0, h_f10, "ffn1_w_in", "ffn1_w_out", 0, mix_norm[0])
    z_m0 = _mm("mix_in_0", h_m0, whole("hgrn_w_in", None, h_m0), "nn", F32, 2048, 512, D, b_lead=0)
    w_kv = whole("mem_w_kv", None, z_m0)
    kv = [_mm("kv_%d" % i, memn, w_kv, "nn", F32, 512, 512, D, b_lead=i) for i in range(2)]
    cat0, stash0 = _hgrn_fwd2(z_m0, lb_logits, hgrn_gnorm, kv[0], bl, seq)
    x2, h_f20 = _mm("mix_out_0", cat0, whole("hgrn_w_out", None, cat0), "nn", F32, 1024, 1024, cat0.shape[1], res=x1, b_lead=0,
                    norm_gain=ffn2_norm[0].reshape(1, D))
    x3, h_f11, sv_f20 = ffn_fwd("f2l0", x2, h_f20, "ffn2_w_in", "ffn2_w_out", 0, ffn1_norm[1])
    x4, h_m1, sv_f11 = ffn_fwd("f1l1", x3, h_f11, "ffn1_w_in", "ffn1_w_out", 1, mix_norm[1])
    z_m1 = _mm("mix_in_1", h_m1, whole("gmlp_w_in", None, h_m1), "nn", F32, 2048, 512, D, b_lead=0)
    nc1 = seq // GM_CHUNK
    w_s, b_s = gmlp_w_s[0], gmlp_b_s[0]
    ln_w = GM_GROUPS * GM_GROUP_DIM
    ln_g_full, ln_b_full = [whole(nm, None, z_m1).reshape(1, ln_w) for nm in ("gmlp_ln_g", "gmlp_ln_b")]
    cat1 = _gmlp_fwd(z_m1, ln_g_full, ln_b_full, w_s, b_s, kv[1], bl, nc1)
    x5, h_f21 = _mm("mix_out_1", cat1, whole("gmlp_w_out", None, cat1), "nn", F32, 1024, 1024, cat1.shape[1], res=x4, b_lead=0,
                    norm_gain=ffn2_norm[1].reshape(1, D))
    x6, _, sv_f21 = ffn_fwd("f2l1", x5, h_f21, "ffn2_w_in", "ffn2_w_out", 1, None)

    def head(rv, cv):
        def f(xx, gg):
            err = _rmsnorm(xx, gg) - rv[1]
            return 0.5 * jnp.sum(jnp.mean(err * err, axis=-1, keepdims=True), axis=0, keepdims=True)

        ls, vjp = jax.vjp(f, rv[0], cv[0])
        dx, dg = vjp(jnp.ones((1, 1), F32))
        return [dx], [dg, jnp.broadcast_to(ls, (1, 128))]

    dx6, d_final, loss_part = _rowcall("loss_head", head, [(x6, 0, D), (tgt, 0, D)], [final_norm.reshape(1, D)],
                                       [(D, F32)], [((1, D), F32), ((1, 128), F32)], TR)

    rs_out = {}
    n_gather = len(groups)

    def rs(gi, items):
        outs = _rs_chips_seq("reduce_%d" % gi, [p for (_, p, _) in items], [k for (_, _, k) in items], n_gather + gi)
        for i, (key, _, _) in enumerate(items):
            rs_out[key] = (outs[2 * i], outs[2 * i + 1])

    dx5, dg_f21, dwi_f21, dwo_f21 = ffn_bwd("f2l1", dx6, sv_f21, ffn2_norm[1], gathered["ffn2_w_in"], gathered["ffn2_w_out"], 1)
    rs(0, [(("ffn2_w_out", 1), dwo_f21, "row"), (("ffn2_w_in", 1), dwi_f21, "col")])
    dcat1 = _mm("mix_dcat_1", dx5, gathered["gmlp_w_out"], "nt", F32, 2048, 1024, D, b_lead=0)
    dwo_m1 = _mm_tn_pair("mix_dwo_1", cat1, dx5, "row", c_arr, 1024, T)
    dz_m1, dkv1, d_lng, d_lnb, d_ws, d_bs = _gmlp_bwd(z_m1, dcat1, ln_g_full, ln_b_full, w_s, b_s, kv[1], bl, nc1)
    dx4, dg_m1 = _mm_dh_rms("mix_dh_1", dz_m1, gathered["gmlp_w_in"], x4, mix_norm[1].reshape(1, D), dx5, 512)
    dwi_m1 = _mm_tn_pair("mix_dwi_1", h_m1, dz_m1, "col", c_arr, 1024, T)
    rs(1, [(("gmlp_w_out", 0), dwo_m1, "row"), (("gmlp_w_in", 0), dwi_m1, "col")])
    dx3, dg_f11, dwi_f11, dwo_f11 = ffn_bwd("f1l1", dx4, sv_f11, ffn1_norm[1], gathered["ffn1_w_in"], gathered["ffn1_w_out"], 1)
    rs(2, [(("ffn1_w_out", 1), dwo_f11, "row"), (("ffn1_w_in", 1), dwi_f11, "col")])

    dx2, dg_f20, dwi_f20, dwo_f20 = ffn_bwd("f2l0", dx3, sv_f20, ffn2_norm[0], gathered["ffn2_w_in"], gathered["ffn2_w_out"], 0)
    rs(3, [(("ffn2_w_out", 0), dwo_f20, "row"), (("ffn2_w_in", 0), dwi_f20, "col")])
    dcat0 = _mm("mix_dcat_0", dx2, gathered["hgrn_w_out"], "nt", F32, 2048, 1024, D, b_lead=0)
    dwo_m0 = _mm_tn_pair("mix_dwo_0", cat0, dx2, "row", c_arr, 1024, T)
    dz_m0, dkv0, d_lb, d_gn = _hgrn_bwd2(z_m0, dcat0, stash0, lb_logits, hgrn_gnorm, kv[0], bl, seq)
    dx1, dg_m0 = _mm_dh_rms("mix_dh_0", dz_m0, gathered["hgrn_w_in"], x1, mix_norm[0].reshape(1, D), dx2, 512)
    dwi_m0 = _mm_tn_pair("mix_dwi_0", h_m0, dz_m0, "col", c_arr, 1024, T)
    rs(4, [(("hgrn_w_out", 0), dwo_m0, "row"), (("hgrn_w_in", 0), dwi_m0, "col")])

    dwkv = [_mm_tn_pair("kv_dw_%d" % i, memn, dkv, "col", c_arr, 1024, 512) for i, dkv in enumerate([dkv0, dkv1])]
    rs(5, [(("mem_w_kv", 0), dwkv[0], "col"), (("mem_w_kv", 1), dwkv[1], "col")])
    dmemn = _mm("kv_dx_0", dkv0, gathered["mem_w_kv"], "nt", F32, 512, 512, 1024, b_lead=0)
    dmemn = _mm("kv_dx_1", dkv1, gathered["mem_w_kv"], "nt", F32, 512, 512, 1024, res=dmemn, b_lead=1)
    _, d_memnorm = rms_bwd("rms_bwd_mem", mem2, mem_norm, dmemn, None)

    dx0, dg_f10, dwi_f10, dwo_f10 = ffn_bwd("f1l0", dx1, sv_f10, ffn1_norm[0], gathered["ffn1_w_in"], gathered["ffn1_w_out"], 0)
    rs(6, [(("ffn1_w_out", 0), dwo_f10, "row")])
    rs(7, [(("ffn1_w_in", 0), dwi_f10, "col")])

    shard_grads = [_finish_share("finish_" + nm, [rs_out[(nm, l)][0] for l in range(w.shape[0])],
                                 [rs_out[(nm, l)][1] for l in range(w.shape[0])], k, c_arr) for (nm, w, k) in big]

    big_w = [w for (_, w, _) in big]
    big_m = [m_ffn1_w_in, m_ffn1_w_out, m_mem_w_kv, m_hgrn_w_in, m_hgrn_w_out, m_gmlp_w_in, m_gmlp_w_out, m_ffn2_w_in, m_ffn2_w_out]
    big_v = [v_ffn1_w_in, v_ffn1_w_out, v_mem_w_kv, v_hgrn_w_in, v_hgrn_w_out, v_gmlp_w_in, v_gmlp_w_out, v_ffn2_w_in, v_ffn2_w_out]
    big_out = {}
    for (nm, w, _), g, m, v in zip(big, shard_grads, big_m, big_v):
        L, r, c = w.shape
        g2, d2, m2, v2 = _adam_call("adam_" + nm, w.reshape(L * r, c), g.reshape(L * r, c), m.reshape(L * r, c),
                                    v.reshape(L * r, c), 256, pass_grad=True)
        big_out[nm] = (g2.reshape(w.shape), d2.reshape(w.shape), m2.reshape(w.shape), v2.reshape(w.shape))

    d_ffn1n = _two_rows(dg_f10, dg_f11)
    d_mixn = _two_rows(dg_m0, dg_m1)
    d_ffn2n = _two_rows(dg_f20, dg_f21)
    small_parts = [loss_part[:, :1], d_memnorm, d_lb, d_ffn1n, d_mixn, d_gn, d_lng, d_lnb, d_ws, d_bs, d_ffn2n, d_final]
    red_shapes = [(1,), mem_norm.shape, lb_logits.shape, ffn1_norm.shape, mix_norm.shape, hgrn_gnorm.shape, (1, ln_w), (1, ln_w),
                  gmlp_w_s.shape, gmlp_b_s.shape, ffn2_norm.shape, final_norm.shape]
    red = _small_allreduce(_pack(small_parts, _rows_needed(red_shapes)), "reduce_small")
    (loss_v, g_memn, g_lb, g_f1n, g_mixn, g_gn, g_lng_full, g_lnb_full, g_ws, g_bs, g_f2n, g_fin) = _unpack(red, red_shapes)
    lsh = gmlp_ln_g.shape[1]
    g_lng = lax.dynamic_slice(g_lng_full, (0, chip * lsh), (1, lsh))
    g_lnb = lax.dynamic_slice(g_lnb_full, (0, chip * lsh), (1, lsh))
    small_w = [mem_norm, lb_logits, ffn1_norm, mix_norm, hgrn_gnorm, gmlp_ln_g, gmlp_ln_b, gmlp_w_s, gmlp_b_s, ffn2_norm, final_norm]
    small_g = [g_memn, g_lb, g_f1n, g_mixn, g_gn, g_lng, g_lnb, g_ws, g_bs, g_f2n, g_fin]
    small_m = [m_mem_norm, m_lb_logits, m_ffn1_norm, m_mix_norm, m_hgrn_gnorm, m_gmlp_ln_g, m_gmlp_ln_b, m_gmlp_w_s, m_gmlp_b_s, m_ffn2_norm, m_final_norm]
    small_v = [v_mem_norm, v_lb_logits, v_ffn1_norm, v_mix_norm, v_hgrn_gnorm, v_gmlp_ln_g, v_gmlp_ln_b, v_gmlp_w_s, v_gmlp_b_s, v_ffn2_norm, v_final_norm]
    sshapes = [w.shape for w in small_w]
    nrow = _rows_needed(sshapes)
    d_p, m_p, v_p = _adam_call("adam_small", _pack(small_w, nrow), _pack(small_g, nrow), _pack(small_m, nrow), _pack(small_v, nrow), nrow)
    s_delta, s_m, s_v = _unpack(d_p, sshapes), _unpack(m_p, sshapes), _unpack(v_p, sshapes)
    small_names = ["mem_norm", "lb_logits", "ffn1_norm", "mix_norm", "hgrn_gnorm", "gmlp_ln_g", "gmlp_ln_b", "gmlp_w_s", "gmlp_b_s", "ffn2_norm", "final_norm"]
    small_out = {nm: (g.reshape(w.shape), d, m, v) for nm, w, g, d, m, v in zip(small_names, small_w, small_g, s_delta, s_m, s_v)}

    order = ["mem_norm", "lb_logits", "ffn1_norm", "ffn1_w_in", "ffn1_w_out", "mix_norm", "mem_w_kv", "hgrn_w_in", "hgrn_gnorm",
             "hgrn_w_out", "gmlp_w_in", "gmlp_ln_g", "gmlp_ln_b", "gmlp_w_s", "gmlp_b_s", "gmlp_w_out", "ffn2_norm", "ffn2_w_in",
             "ffn2_w_out", "final_norm"]
    allo = {**big_out, **small_out}
    grad_x = dx0.reshape(x.shape)
    return (loss_v.reshape(()), grad_x, *[allo[n][0] for n in order], *[allo[n][1] for n in order],
            *[allo[n][2] for n in order], *[allo[n][3] for n in order])
```
